```python
import math
import jax, jax.numpy as jnp
from jax import lax
import numpy as np

D_MODEL = 1024
BATCH = 16
SEQ = 2048
DEPTH = 1

HEAD_DIM = 64
SWA_Q_HEADS = 8
SWA_KV_HEADS = 2
SWA_WINDOW = 128
DIL_HEADS = 4
DIL_PAIRS = ((128, 1), (512, 4), (2048, 16))
MEM_HEADS = 4
MEM_LEN = 256

BLOCK = 128
ROPE_THETA = 10000.0
LN_EPS = 1e-5
RMS_EPS = 1e-6
DEEPNORM_ALPHA = (2 * DEPTH) ** 0.25
DEEPNORM_BETA = (8 * DEPTH) ** -0.25

W_A = SWA_Q_HEADS * HEAD_DIM
W_KV_A = SWA_KV_HEADS * HEAD_DIM
W_B = DIL_HEADS * HEAD_DIM
W_C = MEM_HEADS * HEAD_DIM
D_MIX = W_A + W_B + W_C
IN_SPLITS = (W_A, W_KV_A, W_KV_A, W_B, W_B, W_B, W_C, D_MIX)
D_IN = sum(IN_SPLITS)

kernel_name = "hymba_swa_sink_dilated_mem_deepnorm"


def rope_tables(seq_len):
    pos = jnp.arange(seq_len, dtype=jnp.float32)
    inv = ROPE_THETA ** (-jnp.arange(0, HEAD_DIM, 2, dtype=jnp.float32) / HEAD_DIM)
    ang = pos[:, None] * inv[None, :]
    ang = jnp.concatenate([ang, ang], axis=-1)
    return jnp.cos(ang), jnp.sin(ang)


def apply_rope(t, cos, sin):
    tf = t.astype(jnp.float32)
    half = HEAD_DIM // 2
    rot = jnp.concatenate([-tf[..., half:], tf[..., :half]], axis=-1)
    return (tf * cos[None, :, None, :] + rot * sin[None, :, None, :]).astype(t.dtype)


def banded_causal_attention(q, k, v, max_dist, sinks=None):
    N, L, H, D = q.shape
    Hkv = k.shape[2]
    G = H // Hkv
    n_blk = -(-L // BLOCK)
    pad = n_blk * BLOCK - L
    if pad:
        q, k, v = [jnp.pad(t, ((0, 0), (0, pad), (0, 0), (0, 0))) for t in (q, k, v)]
    qb = q.reshape(N, n_blk, BLOCK, Hkv, G, D)
    kb = k.reshape(N, n_blk, BLOCK, Hkv, D)
    vb = v.reshape(N, n_blk, BLOCK, Hkv, D)

    def with_prev(t):
        prev = jnp.pad(t, ((0, 0), (1, 0), (0, 0), (0, 0), (0, 0)))[:, :-1]
        return jnp.concatenate([prev, t], axis=2)

    kk, vv = with_prev(kb), with_prev(vb)
    s = jnp.einsum('nbqhgd,nbkhd->nbhgqk', qb, kk,
                   preferred_element_type=jnp.float32) * (D ** -0.5)
    qi = jnp.arange(BLOCK)[:, None]
    kj = jnp.arange(2 * BLOCK)[None, :]
    dist = qi + BLOCK - kj
    band = (dist >= 0) & (dist <= max_dist)
    blk = jnp.arange(n_blk)[:, None, None]
    mask = band[None] & ((blk > 0) | (kj >= BLOCK)[None])
    s = jnp.where(mask[None, :, None, None], s, -jnp.inf)
    m = s.max(axis=-1)
    if sinks is not None:
        sink = sinks.astype(jnp.float32).reshape(Hkv, G)[None, None, :, :, None]
        m = jnp.maximum(m, sink)
    p = jnp.exp(s - m[..., None])
    denom = p.sum(axis=-1)
    if sinks is not None:
        denom = denom + jnp.exp(sink - m)
    o = jnp.einsum('nbhgqk,nbkhd->nbqhgd', (p / denom[..., None]).astype(v.dtype), vv)
    lse = m + jnp.log(denom)
    o = o.reshape(N, n_blk * BLOCK, H, D)[:, :L]
    lse = lse.transpose(0, 1, 4, 2, 3).reshape(N, n_blk * BLOCK, H)[:, :L]
    return o, lse


def dilated_attention(q, k, v):
    B, S, H, D = q.shape
    outs, lses = [], []
    for window, dil in DIL_PAIRS:
        Ls = S // dil

        def to_stream(t):
            return t.reshape(B, Ls, dil, H, D).transpose(0, 2, 1, 3, 4).reshape(B * dil, Ls, H, D)

        o, lse = banded_causal_attention(to_stream(q), to_stream(k), to_stream(v), window // dil)
        outs.append(o.reshape(B, dil, Ls, H, D).transpose(0, 2, 1, 3, 4).reshape(B, S, H, D))
        lses.append(lse.reshape(B, dil, Ls, H).transpose(0, 2, 1, 3).reshape(B, S, H))
    w = jax.nn.softmax(jnp.stack(lses), axis=0)
    o = jnp.einsum('pbsh,pbshd->bshd', w, jnp.stack(outs).astype(jnp.float32))
    return o.astype(q.dtype)


def memory_attention(q, mk, mv):
    s = jnp.einsum('bshd,bmhd->bhsm', q, mk, preferred_element_type=jnp.float32) * (HEAD_DIM ** -0.5)
    p = jax.nn.softmax(s, axis=-1)
    return jnp.einsum('bhsm,bmhd->bshd', p.astype(mv.dtype), mv)


def rms_normalize(t):
    tf = t.astype(jnp.float32)
    return tf * lax.rsqrt(jnp.mean(tf * tf, axis=-1, keepdims=True) + RMS_EPS)


def layer_norm(t, gain, bias):
    tf = t.astype(jnp.float32)
    mu = jnp.mean(tf, axis=-1, keepdims=True)
    var = jnp.mean(jnp.square(tf - mu), axis=-1, keepdims=True)
    return (tf - mu) * lax.rsqrt(var + LN_EPS) * gain.astype(jnp.float32) + bias.astype(jnp.float32)


def _fwd_setup_inputs(seed: int = 0) -> dict:
    key = jax.random.key(seed)
    ks = jax.random.split(key, 10)
    x = jax.random.normal(ks[0], (BATCH, SEQ, D_MODEL), jnp.float32)
    mem = jax.random.normal(ks[1], (BATCH, MEM_LEN, D_MODEL), jnp.float32)
    w_in = jax.random.normal(ks[2], (DEPTH, D_MODEL, D_IN), jnp.float32) * D_MODEL ** -0.5
    b_in = 0.02 * jax.random.normal(ks[3], (DEPTH, D_IN), jnp.float32)
    w_mem = jax.random.normal(ks[4], (DEPTH, D_MODEL, 2 * W_C), jnp.float32) * D_MODEL ** -0.5
    attn_sinks = 0.5 * jax.random.normal(ks[5], (DEPTH, SWA_Q_HEADS), jnp.float32)
    g_branch = 1.0 + 0.05 * jax.random.normal(ks[6], (DEPTH, D_MIX), jnp.float32)
    w_out = (jax.random.normal(ks[7], (DEPTH, D_MIX, D_MODEL), jnp.float32)
             * D_MIX ** -0.5 * DEEPNORM_BETA)
    ln_gain = 1.0 + 0.05 * jax.random.normal(ks[8], (DEPTH, D_MODEL), jnp.float32)
    ln_bias = 0.02 * jax.random.normal(ks[9], (DEPTH, D_MODEL), jnp.float32)
    return {"x": x, "mem": mem, "w_in": w_in, "b_in": b_in, "w_mem": w_mem,
            "attn_sinks": attn_sinks, "g_branch": g_branch, "w_out": w_out,
            "ln_gain": ln_gain, "ln_bias": ln_bias}


def _fwd_reference(x, mem, w_in, b_in, w_mem, attn_sinks, g_branch, w_out, ln_gain, ln_bias):
    B, S, _ = x.shape
    M = mem.shape[1]
    cos, sin = rope_tables(S)
    split_idx = [int(i) for i in np.cumsum(IN_SPLITS)[:-1]]

    def heads(t, n):
        return t.reshape(t.shape[0], t.shape[1], n, HEAD_DIM)

    for l in range(DEPTH):
        h = jnp.einsum('bsd,de->bse', x, w_in[l]) + b_in[l]
        qa, ka, va, qb, kb, vb, qc, z = jnp.split(h, split_idx, axis=-1)

        oa, _ = banded_causal_attention(apply_rope(heads(qa, SWA_Q_HEADS), cos, sin),
                                        apply_rope(heads(ka, SWA_KV_HEADS), cos, sin),
                                        heads(va, SWA_KV_HEADS),
                                        SWA_WINDOW - 1, attn_sinks[l])
        ob = dilated_attention(apply_rope(heads(qb, DIL_HEADS), cos, sin),
                               apply_rope(heads(kb, DIL_HEADS), cos, sin),
                               heads(vb, DIL_HEADS))
        mkv = jnp.einsum('bmd,de->bme', mem, w_mem[l])
        mk, mv = jnp.split(mkv, 2, axis=-1)
        oc = memory_attention(heads(qc, MEM_HEADS), heads(mk, MEM_HEADS), heads(mv, MEM_HEADS))

        y = jnp.concatenate([rms_normalize(oa.reshape(B, S, W_A)),
                             rms_normalize(ob.reshape(B, S, W_B)),
                             rms_normalize(oc.reshape(B, S, W_C))], axis=-1)
        y = y * g_branch[l].astype(jnp.float32) * jax.nn.silu(z.astype(jnp.float32))
        y = jnp.einsum('bse,ed->bsd', y.astype(x.dtype), w_out[l])

        x = layer_norm(DEEPNORM_ALPHA * x + y, ln_gain[l], ln_bias[l]).astype(x.dtype)
    return x


import jax as _jax
import jax.numpy as _jnp

TWIN_FORMAT = 'train_step'
FWD_PARAMS = ['x', 'mem', 'w_in', 'b_in', 'w_mem', 'attn_sinks', 'g_branch', 'w_out', 'ln_gain', 'ln_bias']
TWIN_WEIGHTS = ['w_in', 'b_in', 'w_mem', 'attn_sinks', 'g_branch', 'w_out', 'ln_gain', 'ln_bias']
TWIN_DIFF_INPUT = 'x'
TWIN_INPUTS = ['x', 'mem', 'w_in', 'b_in', 'w_mem', 'attn_sinks', 'g_branch', 'w_out', 'ln_gain', 'ln_bias', 'loss_target', 'm_w_in', 'm_b_in', 'm_w_mem', 'm_attn_sinks', 'm_g_branch', 'm_w_out', 'm_ln_gain', 'm_ln_bias', 'v_w_in', 'v_b_in', 'v_w_mem', 'v_attn_sinks', 'v_g_branch', 'v_w_out', 'v_ln_gain', 'v_ln_bias']
TWIN_OUTPUTS = ['loss', 'grad_x', 'grad_w_in', 'grad_b_in', 'grad_w_mem', 'grad_attn_sinks', 'grad_g_branch', 'grad_w_out', 'grad_ln_gain', 'grad_ln_bias', 'delta_w_in', 'delta_b_in', 'delta_w_mem', 'delta_attn_sinks', 'delta_g_branch', 'delta_w_out', 'delta_ln_gain', 'delta_ln_bias', 'new_m_w_in', 'new_m_b_in', 'new_m_w_mem', 'new_m_attn_sinks', 'new_m_g_branch', 'new_m_w_out', 'new_m_ln_gain', 'new_m_ln_bias', 'new_v_w_in', 'new_v_b_in', 'new_v_w_mem', 'new_v_attn_sinks', 'new_v_g_branch', 'new_v_w_out', 'new_v_ln_gain', 'new_v_ln_bias']
TWIN_LEAF_KINDS = {'loss': 'loss', 'grad_x': 'grad_x', 'grad_w_in': 'grad_w', 'grad_b_in': 'grad_w', 'grad_w_mem': 'grad_w', 'grad_attn_sinks': 'grad_w', 'grad_g_branch': 'grad_w', 'grad_w_out': 'grad_w', 'grad_ln_gain': 'grad_w', 'grad_ln_bias': 'grad_w', 'delta_w_in': 'delta_w', 'delta_b_in': 'delta_w', 'delta_w_mem': 'delta_w', 'delta_attn_sinks': 'delta_w', 'delta_g_branch': 'delta_w', 'delta_w_out': 'delta_w', 'delta_ln_gain': 'delta_w', 'delta_ln_bias': 'delta_w', 'new_m_w_in': 'new_m', 'new_m_b_in': 'new_m', 'new_m_w_mem': 'new_m', 'new_m_attn_sinks': 'new_m', 'new_m_g_branch': 'new_m', 'new_m_w_out': 'new_m', 'new_m_ln_gain': 'new_m', 'new_m_ln_bias': 'new_m', 'new_v_w_in': 'new_v', 'new_v_b_in': 'new_v', 'new_v_w_mem': 'new_v', 'new_v_attn_sinks': 'new_v', 'new_v_g_branch': 'new_v', 'new_v_w_out': 'new_v', 'new_v_ln_gain': 'new_v', 'new_v_ln_bias': 'new_v'}


def _forward(args):
    return _fwd_reference(*[args[k] for k in FWD_PARAMS])


def _output_shape():
    out = _jax.eval_shape(lambda: _forward(_fwd_setup_inputs(0)))
    return out.shape, out.dtype

N_MICROBATCH = 1
ADAM_LR = 0.001
ADAM_B1 = 0.9
ADAM_B2 = 0.999
ADAM_EPS = 1e-08
ADAM_WD = 0.01
ADAM_STEP = 10
PER_EXAMPLE_BATCH_AXIS = {'x': 0, 'mem': 0, 'loss_target': 0}
SHARED_INPUTS = []
_WEIGHT_DTYPES = {'w_in': _jnp.float32, 'b_in': _jnp.float32, 'w_mem': _jnp.float32, 'attn_sinks': _jnp.float32, 'g_branch': _jnp.float32, 'w_out': _jnp.float32, 'ln_gain': _jnp.float32, 'ln_bias': _jnp.float32}
MOMENT_SCALE = {'w_in': 5.742092e-02, 'b_in': 3.556622e-01, 'w_mem': 5.025809e-02, 'attn_sinks': 1.886379e-02, 'g_branch': 5.412709e-02, 'w_out': 8.704725e-02, 'ln_gain': 3.205578e+01, 'ln_bias': 6.372856e-01}


def _to_microbatches(a, axis):
    t = _jnp.moveaxis(a, axis, 0)
    t = t.reshape((N_MICROBATCH, t.shape[0] // N_MICROBATCH) + t.shape[1:])
    return _jnp.moveaxis(t, 1, axis + 1)


def setup_inputs(seed: int = 0) -> dict:
    inp = _fwd_setup_inputs(seed)
    key = _jax.random.fold_in(_jax.random.key(seed), 7919)
    shape, _ = _output_shape()
    out = dict(inp)
    out["loss_target"] = _jax.random.normal(_jax.random.fold_in(key, 0), shape, _jnp.float32)
    for i, name in enumerate(TWIN_WEIGHTS):
        w = inp[name].astype(_jnp.float32)
        if MOMENT_SCALE is None:
            s = _jnp.sqrt(_jnp.mean(_jnp.square(w)) + 1e-30)
        else:
            s = MOMENT_SCALE[name]
        km, kv = _jax.random.split(_jax.random.fold_in(key, i + 1))
        out[name] = w
        out["m_" + name] = s * _jax.random.normal(km, w.shape, _jnp.float32)
        out["v_" + name] = (s * s) * _jax.random.uniform(kv, w.shape, _jnp.float32, 0.5, 1.5)
    if N_MICROBATCH > 1:
        for name, axis in PER_EXAMPLE_BATCH_AXIS.items():
            out[name] = _to_microbatches(out[name], axis)
    return {'x': out['x'], 'mem': out['mem'], 'w_in': out['w_in'], 'b_in': out['b_in'], 'w_mem': out['w_mem'], 'attn_sinks': out['attn_sinks'], 'g_branch': out['g_branch'], 'w_out': out['w_out'], 'ln_gain': out['ln_gain'], 'ln_bias': out['ln_bias'], 'loss_target': out['loss_target'], 'm_w_in': out['m_w_in'], 'm_b_in': out['m_b_in'], 'm_w_mem': out['m_w_mem'], 'm_attn_sinks': out['m_attn_sinks'], 'm_g_branch': out['m_g_branch'], 'm_w_out': out['m_w_out'], 'm_ln_gain': out['m_ln_gain'], 'm_ln_bias': out['m_ln_bias'], 'v_w_in': out['v_w_in'], 'v_b_in': out['v_b_in'], 'v_w_mem': out['v_w_mem'], 'v_attn_sinks': out['v_attn_sinks'], 'v_g_branch': out['v_g_branch'], 'v_w_out': out['v_w_out'], 'v_ln_gain': out['v_ln_gain'], 'v_ln_bias': out['v_ln_bias']}


def _loss(weights, diff, rest, loss_target):
    with _jax.named_scope("forward"):
        args = {**rest, TWIN_DIFF_INPUT: diff, **{k: w.astype(_WEIGHT_DTYPES[k]) for k, w in weights.items()}}
        y = _forward(args)
    with _jax.named_scope("loss_head"):
        err = _jnp.square(y.astype(_jnp.float32) - loss_target)
        return 0.5 * _jnp.sum(_jnp.mean(err, axis=-1)) if err.ndim else 0.5 * err


def _adamw(w, g, m, v):
    m = ADAM_B1 * m + (1.0 - ADAM_B1) * g
    v = ADAM_B2 * v + (1.0 - ADAM_B2) * _jnp.square(g)
    m_hat = m / (1.0 - ADAM_B1 ** ADAM_STEP)
    v_hat = v / (1.0 - ADAM_B2 ** ADAM_STEP)
    delta = -ADAM_LR * (m_hat / (_jnp.sqrt(v_hat) + ADAM_EPS) + ADAM_WD * w)
    return delta, m, v


def reference(x, mem, w_in, b_in, w_mem, attn_sinks, g_branch, w_out, ln_gain, ln_bias, loss_target, m_w_in, m_b_in, m_w_mem, m_attn_sinks, m_g_branch, m_w_out, m_ln_gain, m_ln_bias, v_w_in, v_b_in, v_w_mem, v_attn_sinks, v_g_branch, v_w_out, v_ln_gain, v_ln_bias):
    given = dict(x=x, mem=mem, w_in=w_in, b_in=b_in, w_mem=w_mem, attn_sinks=attn_sinks, g_branch=g_branch, w_out=w_out, ln_gain=ln_gain, ln_bias=ln_bias, loss_target=loss_target, m_w_in=m_w_in, m_b_in=m_b_in, m_w_mem=m_w_mem, m_attn_sinks=m_attn_sinks, m_g_branch=m_g_branch, m_w_out=m_w_out, m_ln_gain=m_ln_gain, m_ln_bias=m_ln_bias, v_w_in=v_w_in, v_b_in=v_b_in, v_w_mem=v_w_mem, v_attn_sinks=v_attn_sinks, v_g_branch=v_g_branch, v_w_out=v_w_out, v_ln_gain=v_ln_gain, v_ln_bias=v_ln_bias)
    weights = {n: given[n] for n in TWIN_WEIGHTS}
    shared = {n: given[n] for n in SHARED_INPUTS}
    per_example = {n: given[n] for n in ['x', 'mem']}
    grad_fn = _jax.value_and_grad(_loss, argnums=(0, 1))

    def one_microbatch(ex, loss_target):
        ex = dict(ex)
        diff = ex.pop(TWIN_DIFF_INPUT)
        return grad_fn(weights, diff, {**shared, **ex}, loss_target)

    if N_MICROBATCH == 1:
        loss, (grad_w, grad_x) = one_microbatch(per_example, given["loss_target"])
    else:
        def body(carry, xs):
            loss_sum, grad_sum = carry
            l_k, (gw_k, gx_k) = one_microbatch(xs[0], xs[1])
            with _jax.named_scope("update"):
                return (loss_sum + l_k, _jax.tree.map(_jnp.add, grad_sum, gw_k)), gx_k

        init = (_jnp.zeros((), _jnp.float32), _jax.tree.map(_jnp.zeros_like, weights))
        (loss, grad_w), grad_x = _jax.lax.scan(body, init, (per_example, given["loss_target"]))
    with _jax.named_scope("update"):
        delta_w, new_m, new_v = {}, {}, {}
        for n in TWIN_WEIGHTS:
            delta_w[n], new_m[n], new_v[n] = _adamw(weights[n], grad_w[n], given["m_" + n], given["v_" + n])
    return (loss, grad_x, *[grad_w[n] for n in TWIN_WEIGHTS], *[delta_w[n] for n in TWIN_WEIGHTS],
            *[new_m[n] for n in TWIN_WEIGHTS], *[new_v[n] for n in TWIN_WEIGHTS])
```

```python
import functools

import jax
import jax.numpy as jnp
from jax import lax
from jax.experimental import pallas as pl
from jax.experimental.pallas import tpu as pltpu

F32, BF16 = jnp.float32, jnp.bfloat16

D_MODEL = 1024
SEQ = 2048
B_LOC = 2
T = B_LOC * SEQ
BLK = 128
MEM_LEN = 256
W_A, W_KV_A, W_B, W_C, D_MIX = 512, 128, 256, 256, 1024
D_IN = 2816
O_QA, O_KA, O_VA, O_QB, O_KB, O_VB, O_QC, O_Z = 0, 512, 640, 768, 1024, 1280, 1536, 1792
ROPE_THETA = 10000.0
LN_EPS = 1e-5
RMS_EPS = 1e-6
ALPHA = 2.0 ** 0.25
QK_SCALE = 0.125
N_CHIP = 4
SH_IN, SH_OUT, SH_MEM = D_IN // N_CHIP, D_MIX // N_CHIP, D_MODEL // N_CHIP
NEG = -1e30
ADAM_LR, ADAM_B1, ADAM_B2, ADAM_EPS, ADAM_WD, ADAM_STEP = 0.001, 0.9, 0.999, 1e-08, 0.01, 10
SV_W = 3072
MESH = pl.DeviceIdType.MESH

NN = ((1,), (0,))
NT = ((1,), (1,))
TN = ((0,), (0,))


def _dot(a, b, dims):
    return lax.dot_general(a, b, (dims, ((), ())), preferred_element_type=F32)


def _cp(sem=None, vmem_mb=None):
    kw = {}
    if sem is not None:
        kw["dimension_semantics"] = sem
    if vmem_mb is not None:
        kw["vmem_limit_bytes"] = vmem_mb * 1024 * 1024
    return pltpu.CompilerParams(**kw)


def _sds(shape, dtype):
    return jax.ShapeDtypeStruct(shape, dtype)


def _full(shape):
    n = len(shape)
    return pl.BlockSpec(shape, lambda *_: (0,) * n)


def _gather_weights(winT_sh, wout_sh, wmem_sh):
    shard_rows = (SH_IN, SH_OUT, SH_MEM)

    def body(a_ref, b_ref, c_ref, oa_ref, ob_ref, oc_ref, ici_send, ici_recv, d2d_send, d2d_recv):
        x, y, c = lax.axis_index("x"), lax.axis_index("y"), lax.axis_index("c")
        sibling = (x, y, 1 - c)
        chips = [(1 - x, y), (x, 1 - y), (1 - x, 1 - y)]
        srcs = (a_ref, b_ref, c_ref)
        outs = (oa_ref, ob_ref, oc_ref)

        def rows(a, chip, half):
            n = shard_rows[a]
            start = pl.multiple_of((2 * chip[0] + chip[1]) * n + half * (n // 2), 16)
            return outs[a].at[pl.ds(start, n // 2), :]

        for a in range(3):
            n = shard_rows[a]
            start = pl.multiple_of((2 * x + y) * n, 16)
            outs[a][pl.ds(start, n), :] = srcs[a][...].astype(BF16)

        def ici(a, j, chip_of_block, to):
            blk = rows(a, chip_of_block, c)
            return pltpu.make_async_remote_copy(
                src_ref=blk, dst_ref=blk, send_sem=ici_send.at[a, j], recv_sem=ici_recv.at[a, j],
                device_id=to, device_id_type=MESH)

        def d2d(a, j, chip_of_block, half, to):
            blk = rows(a, chip_of_block, half)
            return pltpu.make_async_remote_copy(
                src_ref=blk, dst_ref=blk, send_sem=d2d_send.at[a, j], recv_sem=d2d_recv.at[a, j],
                device_id=to, device_id_type=MESH)

        first = [ici(a, j, (x, y), (*chip, c)) for a in range(3) for j, chip in enumerate(chips)]
        for cp in first:
            cp.start()
        passed = []
        for j, chip in enumerate(chips):
            for a in range(3):
                ici(a, j, chip, (x, y, c)).wait_recv()
                fw = d2d(a, j, chip, c, sibling)
                fw.start()
                passed.append(fw)
        for j, chip in enumerate(chips):
            for a in range(3):
                d2d(a, j, chip, 1 - c, (x, y, c)).wait_recv()
        for cp in first + passed:
            cp.wait_send()

    vm = pl.BlockSpec(memory_space=pltpu.VMEM)
    return pl.pallas_call(
        body, name="gather_weights",
        out_shape=(_sds((D_IN, D_MODEL), BF16), _sds((D_MIX, D_MODEL), BF16), _sds((D_MODEL, 2 * W_C), BF16)),
        in_specs=[vm, vm, vm], out_specs=(vm, vm, vm),
        scratch_shapes=[pltpu.SemaphoreType.DMA((3, 3))] * 4,
        compiler_params=_cp(vmem_mb=40),
    )(winT_sh, wout_sh, wmem_sh)


def _rope(t, cos, sa, sb, sign):
    w = t.shape[1]
    reps = w // 128
    c, a, b = (jnp.tile(v, (1, reps)) if reps > 1 else v for v in (cos, sa, sb))
    rot = pltpu.roll(t, w - 32, 1) * a + pltpu.roll(t, 32, 1) * b
    return t * c + rot if sign > 0 else t * c - rot


def _in_proj(x, winT, b_in, cos, sa, sb):
    tm = 256
    spt = SEQ // tm

    def body(x_ref, w_ref, b_ref, cos_ref, sa_ref, sb_ref,
             xb_ref, qa_ref, ka_ref, va_ref, bn_ref, b4_ref, b16_ref, qc_ref, z_ref, scr):
        xb = x_ref[...].astype(BF16)
        xb_ref[...] = xb
        cos_t, sa_t, sb_t = cos_ref[...], sa_ref[...], sb_ref[...]

        def proj(r0, n):
            return _dot(xb, w_ref[r0:r0 + n, :], NT) + b_ref[:, r0:r0 + n]

        def rope(t):
            return _rope(t, cos_t, sa_t, sb_t, +1)

        qa_ref[...] = (rope(proj(O_QA, W_A)) * QK_SCALE).astype(BF16)
        ka_ref[...] = rope(proj(O_KA, W_KV_A)).astype(BF16)
        va_ref[...] = proj(O_VA, W_KV_A).astype(BF16)
        qc_ref[...] = (proj(O_QC, W_C) * QK_SCALE).astype(BF16)
        z_ref[...] = proj(O_Z, D_MIX).astype(BF16)
        parts = (rope(proj(O_QB, W_B)) * QK_SCALE, rope(proj(O_KB, W_B)), proj(O_VB, W_B))
        for k, part in enumerate(parts):
            bn_ref[:, 256 * k:256 * (k + 1)] = part.astype(BF16)
            scr[2 * k] = part[:, :128]
            scr[2 * k + 1] = part[:, 128:]
        for j in range(6):
            for res in range(4):
                b4_ref[0, res, :, 128 * j:128 * (j + 1)] = scr[j, pl.ds(res, tm // 4, stride=4), :].astype(BF16)
            for res in range(16):
                b16_ref[0, res, :, 128 * j:128 * (j + 1)] = scr[j, pl.ds(res, tm // 16, stride=16), :].astype(BF16)

    tok = lambda w: pl.BlockSpec((tm, w), lambda i: (i, 0))
    tab = pl.BlockSpec((tm, 128), lambda i: (i % spt, 0))
    return pl.pallas_call(
        body, name="in_proj", grid=(T // tm,),
        in_specs=[tok(D_MODEL), _full((D_IN, D_MODEL)), _full((1, D_IN)), tab, tab, tab],
        out_specs=(tok(D_MODEL), tok(W_A), tok(W_KV_A), tok(W_KV_A), tok(768),
                   pl.BlockSpec((1, 4, tm // 4, 768), lambda i: (i // spt, 0, i % spt, 0)),
                   pl.BlockSpec((1, 16, tm // 16, 768), lambda i: (i // spt, 0, i % spt, 0)),
                   tok(W_C), tok(D_MIX)),
        out_shape=(_sds((T, D_MODEL), BF16), _sds((T, W_A), BF16), _sds((T, W_KV_A), BF16), _sds((T, W_KV_A), BF16),
                   _sds((T, 768), BF16), _sds((B_LOC, 4, SEQ // 4, 768), BF16), _sds((B_LOC, 16, SEQ // 16, 768), BF16),
                   _sds((T, W_C), BF16), _sds((T, D_MIX), BF16)),
        scratch_shapes=[pltpu.VMEM((6, tm, 128), F32)],
        compiler_params=_cp(("parallel",), vmem_mb=48),
    )(x, winT, b_in, cos, sa, sb)


def _mem_kv(mem, wmem):
    def body(m_ref, w_ref, mb_ref, kv_ref):
        mb = m_ref[...].astype(BF16)
        mb_ref[...] = mb
        kv_ref[...] = _dot(mb, w_ref[...], NN).astype(BF16)

    n = B_LOC * MEM_LEN
    return pl.pallas_call(
        body, name="mem_kv",
        out_shape=(_sds((n, D_MODEL), BF16), _sds((n, 2 * W_C), BF16)),
    )(mem, wmem)


def _lane_lo():
    return lax.broadcasted_iota(jnp.int32, (1, 128), 1) < 64


def _dup_head(k2, hk, lo):
    kf = k2.astype(F32)
    r = pltpu.roll(kf, 64, 1)
    return (jnp.where(lo, kf, r) if hk == 0 else jnp.where(lo, r, kf)).astype(BF16)


def _band_masks(max_dist):
    qi = lax.broadcasted_iota(jnp.int32, (BLK, BLK), 0)
    kj = lax.broadcasted_iota(jnp.int32, (BLK, BLK), 1)
    return qi, kj, BLK - max_dist


def _attn_specs(kind, nb, qcb, qw, kcb, vcb, kvw):
    q_spec = pl.BlockSpec((BLK, qw), lambda g: (g, qcb))
    if kind == "mem":
        per = SEQ // BLK
        kc = pl.BlockSpec((MEM_LEN, kvw), lambda g: (g // per, kcb))
        vc = pl.BlockSpec((MEM_LEN, kvw), lambda g: (g // per, vcb))
        return q_spec, [kc, vc]
    kc = pl.BlockSpec((BLK, kvw), lambda g: (g, kcb))
    vc = pl.BlockSpec((BLK, kvw), lambda g: (g, vcb))
    specs = [kc, vc]
    if nb > 1:
        specs += [pl.BlockSpec((BLK, kvw), lambda g: (jnp.maximum(g - 1, 0), kcb)),
                  pl.BlockSpec((BLK, kvw), lambda g: (jnp.maximum(g - 1, 0), vcb))]
    return q_spec, specs


def _attn_fwd(name, q, qcb, qw, k, kcb, v, vcb, kvw, *, kind, nb=1, max_dist=BLK, gqa=False, sink2=None):
    npairs = qw // 128
    two = kind == "band" and nb > 1

    def body(*refs):
        it = iter(refs)
        q_ref, kc_ref, vc_ref = next(it), next(it), next(it)
        kp_ref, vp_ref = (next(it), next(it)) if two else (None, None)
        sink_ref = next(it) if sink2 is not None else None
        o_ref, lse_ref = next(it), next(it)
        lo = _lane_lo()
        if kind == "band":
            qi, kj, reach = _band_masks(max_dist)
            m_cur = kj <= qi
            if two:
                has_prev = (pl.program_id(0) % nb) > 0
                m_prev = kj >= qi + jnp.where(has_prev, reach, 2 * BLK)
        dup = {}
        for p in range(npairs):
            q2 = q_ref[:, 128 * p:128 * (p + 1)]
            if gqa:
                hk = p // 2
                if hk not in dup:
                    dup[hk] = (_dup_head(kc_ref[...], hk, lo), _dup_head(vc_ref[...], hk, lo),
                               _dup_head(kp_ref[...], hk, lo), _dup_head(vp_ref[...], hk, lo))
                kc, vc, kp, vp = dup[hk]
            else:
                sl = slice(128 * p, 128 * (p + 1))
                kc, vc = kc_ref[:, sl], vc_ref[:, sl]
                kp, vp = (kp_ref[:, sl], vp_ref[:, sl]) if two else (None, None)
            outs, lses = [], []
            for e in (0, 1):
                sel = lo if e == 0 else jnp.logical_not(lo)
                qm = jnp.where(sel, q2, jnp.zeros_like(q2))
                sc = _dot(qm, kc, NT)
                if kind == "band":
                    sc = jnp.where(m_cur, sc, NEG)
                m = jnp.max(sc, axis=1, keepdims=True)
                if two:
                    sp = jnp.where(m_prev, _dot(qm, kp, NT), NEG)
                    m = jnp.maximum(m, jnp.max(sp, axis=1, keepdims=True))
                if sink_ref is not None:
                    sk = sink_ref[:, 128 * p + 64 * e:128 * p + 64 * e + 1]
                    m = jnp.maximum(m, sk)
                pc = jnp.exp(sc - m)
                l = jnp.sum(pc, axis=1, keepdims=True)
                acc = _dot(pc.astype(BF16), vc, NN)
                if two:
                    pp = jnp.exp(sp - m)
                    l = l + jnp.sum(pp, axis=1, keepdims=True)
                    acc = acc + _dot(pp.astype(BF16), vp, NN)
                if sink_ref is not None:
                    l = l + jnp.exp(sk - m)
                outs.append(acc / l)
                lses.append(m + jnp.log(l))
            o_ref[:, 128 * p:128 * (p + 1)] = jnp.where(lo, outs[0], outs[1]).astype(BF16)
            lse_ref[:, 128 * p:128 * (p + 1)] = jnp.where(lo, lses[0], lses[1])

    q_spec, kv_specs = _attn_specs(kind, nb, qcb, qw, kcb, vcb, kvw)
    args = [q, k, v] + ([k, v] if two else [])
    in_specs = [q_spec] + kv_specs
    if sink2 is not None:
        args.append(sink2)
        in_specs.append(_full(sink2.shape))
    out_spec = pl.BlockSpec((BLK, qw), lambda g: (g, 0))
    return pl.pallas_call(
        body, name=name, grid=(T // BLK,), in_specs=in_specs, out_specs=(out_spec, out_spec),
        out_shape=(_sds((T, qw), BF16), _sds((T, qw), F32)),
        compiler_params=_cp(("parallel",)),
    )(*args)


def _attn_bwd(name, q, qcb, qw, k, kcb, v, vcb, kvw, do, lse, dl, *, kind, nb=1, max_dist=BLK, gqa=False,
              sink2=None):
    npairs = qw // 128
    two = kind == "band" and nb > 1
    kb = MEM_LEN if kind == "mem" else BLK

    def body(*refs):
        it = iter(refs)
        q_ref, kc_ref, vc_ref = next(it), next(it), next(it)
        kp_ref, vp_ref = (next(it), next(it)) if two else (None, None)
        do_ref, lse_ref, dl_ref = next(it), next(it), next(it)
        sink_ref = next(it) if sink2 is not None else None
        dq_ref = next(it)
        if kind == "mem":
            dkv_ref = next(it)
        else:
            dk_ref, dv_ref = next(it), next(it)
        dsink_ref = next(it) if sink2 is not None else None
        dq_scr = next(it)
        g = pl.program_id(0)
        lo = _lane_lo()

        @pl.when(g == 0)
        def _():
            if kind == "mem":
                dkv_ref[...] = jnp.zeros_like(dkv_ref)
            else:
                dk_ref[...] = jnp.zeros_like(dk_ref)
                dv_ref[...] = jnp.zeros_like(dv_ref)
            if dsink_ref is not None:
                dsink_ref[...] = jnp.zeros_like(dsink_ref)

        if kind == "band":
            qi, kj, reach = _band_masks(max_dist)
            m_cur = kj <= qi
            m_prev = kj >= qi + reach

        def half(k_ref, v_ref, mask, row0, first):
            dk_acc, dv_acc = {}, {}
            for p in range(npairs):
                q2 = q_ref[:, 128 * p:128 * (p + 1)]
                do2 = do_ref[:, 128 * p:128 * (p + 1)]
                lse2 = lse_ref[:, 128 * p:128 * (p + 1)]
                dl2 = dl_ref[:, 128 * p:128 * (p + 1)]
                if gqa:
                    hk = p // 2
                    kk, vv = _dup_head(k_ref[...], hk, lo), _dup_head(v_ref[...], hk, lo)
                else:
                    hk = p
                    kk, vv = k_ref[:, 128 * p:128 * (p + 1)], v_ref[:, 128 * p:128 * (p + 1)]
                dq_e = []
                for e in (0, 1):
                    sel = lo if e == 0 else jnp.logical_not(lo)
                    qm = jnp.where(sel, q2, jnp.zeros_like(q2))
                    dom = jnp.where(sel, do2, jnp.zeros_like(do2))
                    lse_e = lse2[:, 64 * e:64 * e + 1]
                    dl_e = dl2[:, 64 * e:64 * e + 1]
                    s = _dot(qm, kk, NT)
                    if mask is not None:
                        s = jnp.where(mask, s, NEG)
                    pr = jnp.exp(s - lse_e)
                    dp = _dot(dom, vv, NT)
                    ds = (pr * (dp - dl_e)).astype(BF16)
                    dq_e.append(_dot(ds, kk, NN))
                    ck = _dot(ds, qm, TN)
                    cv = _dot(pr.astype(BF16), dom, TN)
                    dk_acc[hk] = ck if hk not in dk_acc else dk_acc[hk] + ck
                    dv_acc[hk] = cv if hk not in dv_acc else dv_acc[hk] + cv
                dq2 = jnp.where(lo, dq_e[0], dq_e[1])
                if first:
                    dq_scr[:, 128 * p:128 * (p + 1)] = dq2
                else:
                    dq_scr[:, 128 * p:128 * (p + 1)] += dq2
            rows = pl.ds(row0, kb)
            for hk in dk_acc:
                ck, cv = dk_acc[hk], dv_acc[hk]
                if gqa:
                    sel = lo if hk == 0 else jnp.logical_not(lo)
                    ck = jnp.where(sel, ck + pltpu.roll(ck, 64, 1), 0.0)
                    cv = jnp.where(sel, cv + pltpu.roll(cv, 64, 1), 0.0)
                    cols = slice(0, 128)
                else:
                    cols = slice(128 * hk, 128 * (hk + 1))
                if kind == "mem":
                    dkv_ref[rows, cols] += ck
                    dkv_ref[rows, slice(kvw + cols.start, kvw + cols.stop)] += cv
                else:
                    dk_ref[rows, cols] += ck
                    dv_ref[rows, cols] += cv

        if kind == "mem":
            half(kc_ref, vc_ref, None, pl.multiple_of((g // (SEQ // BLK)) * MEM_LEN, MEM_LEN), True)
        else:
            half(kc_ref, vc_ref, m_cur, pl.multiple_of(g * BLK, BLK), True)
            if two:
                @pl.when((g % nb) > 0)
                def _():
                    half(kp_ref, vp_ref, m_prev, pl.multiple_of((g - 1) * BLK, BLK), False)
        dq_ref[...] = dq_scr[...].astype(BF16)
        if dsink_ref is not None:
            ps = jnp.exp(sink_ref[...] - lse_ref[...]) * dl_ref[...]
            dsink_ref[...] += jnp.sum(ps, axis=0, keepdims=True)

    q_spec, kv_specs = _attn_specs(kind, nb, qcb, qw, kcb, vcb, kvw)
    row_spec = pl.BlockSpec((BLK, qw), lambda g: (g, 0))
    args = [q, k, v] + ([k, v] if two else []) + [do, lse, dl]
    in_specs = [q_spec] + kv_specs + [row_spec, row_spec, row_spec]
    if sink2 is not None:
        args.append(sink2)
        in_specs.append(_full(sink2.shape))
    out_shape = [_sds((T, qw), BF16)]
    out_specs = [row_spec]
    if kind == "mem":
        out_shape.append(_sds((B_LOC * MEM_LEN, 2 * kvw), F32))
        out_specs.append(_full((B_LOC * MEM_LEN, 2 * kvw)))
    else:
        out_shape += [_sds((T, kvw), F32)] * 2
        out_specs += [_full((T, kvw))] * 2
    if sink2 is not None:
        out_shape.append(_sds((1, qw), F32))
        out_specs.append(_full((1, qw)))
    return pl.pallas_call(
        body, name=name, grid=(T // BLK,), in_specs=in_specs, out_specs=tuple(out_specs),
        out_shape=tuple(out_shape), scratch_shapes=[pltpu.VMEM((BLK, qw), F32)],
        compiler_params=_cp(("arbitrary",), vmem_mb=40),
    )(*args)


def _seg64(v, e_ref):
    hi = v.astype(BF16)
    lo = (v - hi.astype(F32)).astype(BF16)
    cols = []
    for j in range(v.shape[1] // 128):
        sl = slice(128 * j, 128 * (j + 1))
        cols.append(_dot(hi[:, sl], e_ref[...], NN) + _dot(lo[:, sl], e_ref[...], NN))
    return cols[0] if len(cols) == 1 else jnp.concatenate(cols, axis=1)


def _middle(oa, o1, l1, o4, l4, o16, l16, oc, z, x, tgt, g_br, ln_g, ln_b, wout, ones_bd):
    tm = 256
    spt = SEQ // tm

    def body(oa_ref, o1_ref, l1_ref, o4_ref, l4_ref, o16_ref, l16_ref, oc_ref, z_ref, x_ref, t_ref,
             g_ref, lg_ref, lb_ref, w_ref, e_ref,
             y_ref, du_ref, dz_ref, doa_ref, dla_ref,
             dobn_ref, lsen_ref, dlbn_ref, dob4_ref, lse4_ref, dlb4_ref, dob16_ref, lse16_ref, dlb16_ref,
             doc_ref, dlc_ref, acc_ref, scr):
        i = pl.program_id(0)

        @pl.when(i == 0)
        def _():
            acc_ref[...] = jnp.zeros_like(acc_ref)

        for j in range(2):
            sl = slice(128 * j, 128 * (j + 1))
            for res in range(4):
                rows = pl.ds(res, tm // 4, stride=4)
                scr[j, rows, :] = o4_ref[0, res, :, sl].astype(F32)
                scr[2 + j, rows, :] = l4_ref[0, res, :, sl]
            for res in range(16):
                rows = pl.ds(res, tm // 16, stride=16)
                scr[4 + j, rows, :] = o16_ref[0, res, :, sl].astype(F32)
                scr[6 + j, rows, :] = l16_ref[0, res, :, sl]
        cat = lambda a: jnp.concatenate([scr[a], scr[a + 1]], axis=1)
        o1v, l1v = o1_ref[...].astype(F32), l1_ref[...]
        o4v, l4v, o16v, l16v = cat(0), cat(2), cat(4), cat(6)
        mx = jnp.maximum(jnp.maximum(l1v, l4v), l16v)
        e1, e4, e16 = jnp.exp(l1v - mx), jnp.exp(l4v - mx), jnp.exp(l16v - mx)
        ssum = e1 + e4 + e16
        lse_b = mx + jnp.log(ssum)
        ob = (e1 * o1v + e4 * o4v + e16 * o16v) / ssum
        oav, ocv = oa_ref[...].astype(F32), oc_ref[...].astype(F32)

        def rms(o):
            r = lax.rsqrt(jnp.sum(o * o, axis=1, keepdims=True) * (1.0 / o.shape[1]) + RMS_EPS)
            return o * r, r

        na, ra = rms(oav)
        nb_, rb = rms(ob)
        nc, rc = rms(ocv)
        n = jnp.concatenate([na, nb_, nc], axis=1)
        zf = z_ref[...].astype(F32)
        sig = 1.0 / (1.0 + jnp.exp(-zf))
        sz = zf * sig
        gb = g_ref[...]
        yb = (n * gb * sz).astype(BF16)
        y_ref[...] = yb
        u = ALPHA * x_ref[...] + _dot(yb, w_ref[...], NN)
        inv_d = 1.0 / D_MODEL
        mu = jnp.sum(u, axis=1, keepdims=True) * inv_d
        uc = u - mu
        rstd = lax.rsqrt(jnp.sum(uc * uc, axis=1, keepdims=True) * inv_d + LN_EPS)
        xh = uc * rstd
        lg = lg_ref[...]
        diff = xh * lg + lb_ref[...] - t_ref[...]
        acc_ref[0:1, :] += jnp.sum(diff * diff, axis=0, keepdims=True) * (0.5 * inv_d)
        dout = diff * inv_d
        acc_ref[2:3, :] += jnp.sum(dout * xh, axis=0, keepdims=True)
        acc_ref[3:4, :] += jnp.sum(dout, axis=0, keepdims=True)
        dxh = dout * lg
        du = rstd * (dxh - jnp.sum(dxh, axis=1, keepdims=True) * inv_d
                     - xh * (jnp.sum(dxh * xh, axis=1, keepdims=True) * inv_d))
        dub = du.astype(BF16)
        du_ref[...] = dub
        dy = _dot(dub, w_ref[...], NT)
        t1 = dy * sz
        acc_ref[1:2, :] += jnp.sum(t1 * n, axis=0, keepdims=True)
        dn = t1 * gb
        dz_ref[...] = (dy * n * gb * (sig * (1.0 + zf * (1.0 - sig)))).astype(BF16)

        def rms_bwd(dn_, n_, r):
            return r * (dn_ - n_ * (jnp.sum(dn_ * n_, axis=1, keepdims=True) * (1.0 / n_.shape[1])))

        doa = rms_bwd(dn[:, :W_A], na, ra)
        dob = rms_bwd(dn[:, W_A:W_A + W_B], nb_, rb)
        doc = rms_bwd(dn[:, W_A + W_B:], nc, rc)
        doa_ref[...] = doa.astype(BF16)
        dla_ref[...] = _seg64(doa * oav, e_ref)
        doc_ref[...] = doc.astype(BF16)
        dlc_ref[...] = _seg64(doc * ocv, e_ref)
        dlb = _seg64(dob * ob, e_ref)
        dobn_ref[...] = dob.astype(BF16)
        lsen_ref[...] = lse_b
        dlbn_ref[...] = dlb
        for k, val in enumerate((dob, lse_b, dlb)):
            scr[2 * k] = val[:, :128]
            scr[2 * k + 1] = val[:, 128:]
        for j in range(2):
            sl = slice(128 * j, 128 * (j + 1))
            for res in range(4):
                rows = pl.ds(res, tm // 4, stride=4)
                dob4_ref[0, res, :, sl] = scr[j, rows, :].astype(BF16)
                lse4_ref[0, res, :, sl] = scr[2 + j, rows, :]
                dlb4_ref[0, res, :, sl] = scr[4 + j, rows, :]
            for res in range(16):
                rows = pl.ds(res, tm // 16, stride=16)
                dob16_ref[0, res, :, sl] = scr[j, rows, :].astype(BF16)
                lse16_ref[0, res, :, sl] = scr[2 + j, rows, :]
                dlb16_ref[0, res, :, sl] = scr[4 + j, rows, :]

    tok = lambda w: pl.BlockSpec((tm, w), lambda i: (i, 0))
    p4 = pl.BlockSpec((1, 4, tm // 4, W_B), lambda i: (i // spt, 0, i % spt, 0))
    p16 = pl.BlockSpec((1, 16, tm // 16, W_B), lambda i: (i // spt, 0, i % spt, 0))
    s4 = lambda dt: _sds((B_LOC, 4, SEQ // 4, W_B), dt)
    s16 = lambda dt: _sds((B_LOC, 16, SEQ // 16, W_B), dt)
    row = _full((1, D_MODEL))
    return pl.pallas_call(
        body, name="middle", grid=(T // tm,),
        in_specs=[tok(W_A), tok(W_B), tok(W_B), p4, p4, p16, p16, tok(W_C), tok(D_MIX), tok(D_MODEL), tok(D_MODEL),
                  row, row, row, _full((D_MIX, D_MODEL)), _full((128, 128))],
        out_specs=(tok(D_MIX), tok(D_MODEL), tok(D_MIX), tok(W_A), tok(W_A),
                   tok(W_B), tok(W_B), tok(W_B), p4, p4, p4, p16, p16, p16,
                   tok(W_C), tok(W_C), _full((8, D_MODEL))),
        out_shape=(_sds((T, D_MIX), BF16), _sds((T, D_MODEL), BF16), _sds((T, D_MIX), BF16),
                   _sds((T, W_A), BF16), _sds((T, W_A), F32),
                   _sds((T, W_B), BF16), _sds((T, W_B), F32), _sds((T, W_B), F32),
                   s4(BF16), s4(F32), s4(F32), s16(BF16), s16(F32), s16(F32),
                   _sds((T, W_C), BF16), _sds((T, W_C), F32), _sds((8, D_MODEL), F32)),
        scratch_shapes=[pltpu.VMEM((8, tm, 128), F32)],
        compiler_params=_cp(("arbitrary",), vmem_mb=48),
    )(oa, o1, l1, o4, l4, o16, l16, oc, z, x, tgt, g_br, ln_g, ln_b, wout, ones_bd)


def _dh_dx(dqa, dka, dva, dqn, dkn, dvn, dq4, dk4, dv4, dq16, dk16, dv16, dqc, dz, du, cos, sa, sb, winT):
    tm = 256
    spt = SEQ // tm

    def body(dqa_ref, dka_ref, dva_ref, dqn_ref, dkn_ref, dvn_ref, dq4_ref, dk4_ref, dv4_ref,
             dq16_ref, dk16_ref, dv16_ref, dqc_ref, dz_ref, du_ref, cos_ref, sa_ref, sb_ref, w_ref,
             dh_ref, gx_ref, db_ref, scr):
        i = pl.program_id(0)

        @pl.when(i == 0)
        def _():
            db_ref[...] = jnp.zeros_like(db_ref)

        cos_t, sa_t, sb_t = cos_ref[...], sa_ref[...], sb_ref[...]

        def rope_t(t):
            return _rope(t, cos_t, sa_t, sb_t, -1)

        def put(r0, val):
            n = val.shape[1]
            dh_ref[:, r0:r0 + n] = val.astype(BF16)
            db_ref[:, r0:r0 + n] += jnp.sum(val, axis=0, keepdims=True)

        put(O_QA, rope_t(dqa_ref[...].astype(F32)) * QK_SCALE)
        put(O_KA, rope_t(dka_ref[...]))
        put(O_VA, dva_ref[...])
        put(O_QC, dqc_ref[...].astype(F32) * QK_SCALE)
        put(O_Z, dz_ref[...].astype(F32))
        for k, (n_ref, r4, r16) in enumerate(((dqn_ref, dq4_ref, dq16_ref), (dkn_ref, dk4_ref, dk16_ref),
                                               (dvn_ref, dv4_ref, dv16_ref))):
            for j in range(2):
                sl = slice(128 * j, 128 * (j + 1))
                scr[2 * k + j] = n_ref[:, sl].astype(F32)
                for res in range(4):
                    scr[2 * k + j, pl.ds(res, tm // 4, stride=4), :] += r4[0, res, :, sl].astype(F32)
                for res in range(16):
                    scr[2 * k + j, pl.ds(res, tm // 16, stride=16), :] += r16[0, res, :, sl].astype(F32)
        cat = lambda a: jnp.concatenate([scr[a], scr[a + 1]], axis=1)
        put(O_QB, rope_t(cat(0)) * QK_SCALE)
        put(O_KB, rope_t(cat(2)))
        put(O_VB, cat(4))
        gx_ref[...] = _dot(dh_ref[...], w_ref[...], NN) + ALPHA * du_ref[...].astype(F32)

    tok = lambda w: pl.BlockSpec((tm, w), lambda i: (i, 0))
    tab = pl.BlockSpec((tm, 128), lambda i: (i % spt, 0))
    p4 = pl.BlockSpec((1, 4, tm // 4, W_B), lambda i: (i // spt, 0, i % spt, 0))
    p16 = pl.BlockSpec((1, 16, tm // 16, W_B), lambda i: (i // spt, 0, i % spt, 0))
    return pl.pallas_call(
        body, name="dh_dx", grid=(T // tm,),
        in_specs=[tok(W_A), tok(W_KV_A), tok(W_KV_A), tok(W_B), tok(W_B), tok(W_B), p4, p4, p4, p16, p16, p16,
                  tok(W_C), tok(D_MIX), tok(D_MODEL), tab, tab, tab, _full((D_IN, D_MODEL))],
        out_specs=(tok(D_IN), tok(D_MODEL), _full((1, D_IN))),
        out_shape=(_sds((T, D_IN), BF16), _sds((T, D_MODEL), F32), _sds((1, D_IN), F32)),
        scratch_shapes=[pltpu.VMEM((6, tm, 128), F32)],
        compiler_params=_cp(("arbitrary",), vmem_mb=48),
    )(dqa, dka, dva, dqn, dkn, dvn, dq4, dk4, dv4, dq16, dk16, dv16, dqc, dz, du, cos, sa, sb, winT)


def _tn_matmul(name, a, b, bm, bt):
    n, m_all = a.shape
    n_cols = b.shape[1]

    def body(a_ref, b_ref, o_ref):
        @pl.when(pl.program_id(1) == 0)
        def _():
            o_ref[...] = jnp.zeros_like(o_ref)

        o_ref[...] += _dot(a_ref[...].astype(BF16), b_ref[...].astype(BF16), TN)

    return pl.pallas_call(
        body, name=name, grid=(m_all // bm, n // bt),
        in_specs=[pl.BlockSpec((bt, bm), lambda m, t: (t, m)), pl.BlockSpec((bt, n_cols), lambda m, t: (t, 0))],
        out_specs=pl.BlockSpec((bm, n_cols), lambda m, t: (m, 0)),
        out_shape=_sds((m_all, n_cols), F32),
        compiler_params=_cp(("parallel", "arbitrary"), vmem_mb=48),
    )(a, b)


def _reduce_grads(g_in, g_out, g_mem, acc, dbin, dsink):
    shard_rows = (SH_IN, SH_OUT, SH_MEM)
    widths = (D_MODEL, D_MODEL, 2 * W_C)

    def body(ga_ref, gb_ref, gc_ref, acc_ref, dbin_ref, dsink_ref,
             ra_ref, rb_ref, rc_ref, sv_ref,
             sib_a, sib_b, sib_c, stage_a, stage_b, stage_c, land_a, land_b, land_c, sv_mine, sv_all,
             s1_send, s1_recv, s2_send, s2_recv, s3_send, s3_recv, sv_send, sv_recv):
        x, y, c = lax.axis_index("x"), lax.axis_index("y"), lax.axis_index("c")
        me, sibling = (x, y, c), (x, y, 1 - c)
        my_chip = 2 * x + y
        chips = [(1 - x, y), (x, 1 - y), (1 - x, 1 - y)]
        grads = (ga_ref, gb_ref, gc_ref)
        sibs = (sib_a, sib_b, sib_c)
        stages = (stage_a, stage_b, stage_c)
        lands = (land_a, land_b, land_c)
        res = (ra_ref, rb_ref, rc_ref)

        def half_rows(a, chip_idx, half):
            n = shard_rows[a]
            return pl.ds(pl.multiple_of(chip_idx * n + half * (n // 2), 16), n // 2)

        sv_mine[...] = jnp.zeros_like(sv_mine)
        sv_mine[0:4, 0:D_MODEL] = acc_ref[0:4, :]
        sv_mine[4:5, 0:D_IN] = dbin_ref[...]
        sv_mine[5:6, 0:W_A] = dsink_ref[...]
        my_dev = 4 * x + 2 * y + c
        others = [(x, y, 1 - c)] + [(*chip, cc) for chip in chips for cc in (c, 1 - c)]

        def sv_copy(j, to):
            return pltpu.make_async_remote_copy(
                src_ref=sv_mine, dst_ref=sv_all.at[my_dev], send_sem=sv_send.at[j], recv_sem=sv_recv.at[j],
                device_id=to, device_id_type=MESH)

        sv_sends = [sv_copy(j, to) for j, to in enumerate(others)]
        for cp in sv_sends:
            cp.start()

        def s1(a, k):
            return pltpu.make_async_remote_copy(
                src_ref=grads[a].at[half_rows(a, k, 1 - c), :], dst_ref=sibs[a].at[k],
                send_sem=s1_send.at[a, k], recv_sem=s1_recv.at[a, k], device_id=sibling, device_id_type=MESH)

        s1s = [s1(a, k) for a in range(3) for k in range(4)]
        for cp in s1s:
            cp.start()

        def s2(a, j, to):
            return pltpu.make_async_remote_copy(
                src_ref=stages[a].at[j], dst_ref=lands[a].at[j], send_sem=s2_send.at[a, j], recv_sem=s2_recv.at[a, j],
                device_id=to, device_id_type=MESH)

        s2s = []
        for j, chip in enumerate(chips):
            k = 2 * chip[0] + chip[1]
            for a in range(3):
                pltpu.make_async_remote_copy(
                    src_ref=grads[a].at[half_rows(a, k, c), :], dst_ref=sibs[a].at[k],
                    send_sem=s1_send.at[a, k], recv_sem=s1_recv.at[a, k], device_id=sibling,
                    device_id_type=MESH).wait_recv()
                stages[a][j] = (grads[a][half_rows(a, k, c), :] + sibs[a][k]).astype(BF16)
                cp = s2(a, j, (*chip, c))
                cp.start()
                s2s.append(cp)

        for a in range(3):
            pltpu.make_async_remote_copy(
                src_ref=grads[a].at[half_rows(a, my_chip, c), :], dst_ref=sibs[a].at[my_chip],
                send_sem=s1_send.at[a, my_chip], recv_sem=s1_recv.at[a, my_chip], device_id=sibling,
                device_id_type=MESH).wait_recv()
        for a in range(3):
            n = shard_rows[a]
            tot = grads[a][half_rows(a, my_chip, c), :] + sibs[a][my_chip]
            for j in range(3):
                s2(a, j, me).wait_recv()
                tot = tot + lands[a][j].astype(F32)
            mine = pl.ds(pl.multiple_of(c * (n // 2), 16), n // 2)
            res[a][mine, :] = tot

        def s3(a, half, to):
            n = shard_rows[a]
            blk = res[a].at[pl.ds(pl.multiple_of(half * (n // 2), 16), n // 2), :]
            return pltpu.make_async_remote_copy(
                src_ref=blk, dst_ref=blk, send_sem=s3_send.at[a], recv_sem=s3_recv.at[a],
                device_id=to, device_id_type=MESH)

        s3s = [s3(a, c, sibling) for a in range(3)]
        for cp in s3s:
            cp.start()
        for a in range(3):
            s3(a, 1 - c, me).wait_recv()

        sv_all[my_dev] = sv_mine[...]
        for j in range(7):
            sv_copy(j, me).wait_recv()
        tot = sv_all[0]
        for d in range(1, 8):
            tot = tot + sv_all[d]
        sv_ref[...] = tot
        for cp in sv_sends + s1s + s2s + s3s:
            cp.wait_send()

    vm = pl.BlockSpec(memory_space=pltpu.VMEM)
    half = lambda a: (shard_rows[a] // 2, widths[a])
    scratch = ([pltpu.VMEM((4, *half(a)), F32) for a in range(3)]
               + [pltpu.VMEM((3, *half(a)), BF16) for a in range(3)]
               + [pltpu.VMEM((3, *half(a)), BF16) for a in range(3)]
               + [pltpu.VMEM((8, SV_W), F32), pltpu.VMEM((8, 8, SV_W), F32)]
               + [pltpu.SemaphoreType.DMA((3, 4))] * 2 + [pltpu.SemaphoreType.DMA((3, 3))] * 2
               + [pltpu.SemaphoreType.DMA((3,))] * 2 + [pltpu.SemaphoreType.DMA((7,))] * 2)
    return pl.pallas_call(
        body, name="reduce_grads",
        out_shape=(_sds((SH_IN, D_MODEL), F32), _sds((SH_OUT, D_MODEL), F32), _sds((SH_MEM, 2 * W_C), F32),
                   _sds((8, SV_W), F32)),
        in_specs=[vm] * 6, out_specs=(vm, vm, vm, vm), scratch_shapes=scratch,
        compiler_params=_cp(vmem_mb=56),
    )(g_in, g_out, g_mem, acc, dbin, dsink)


def _adamw(name, w, g, m, v, rows=None):
    shape = w.shape
    rows = shape[0] if rows is None else rows

    def body(w_ref, g_ref, m_ref, v_ref, d_ref, nm_ref, nv_ref):
        gv = g_ref[...]
        nm = ADAM_B1 * m_ref[...] + (1.0 - ADAM_B1) * gv
        nv = ADAM_B2 * v_ref[...] + (1.0 - ADAM_B2) * (gv * gv)
        m_hat = nm / (1.0 - ADAM_B1 ** ADAM_STEP)
        v_hat = nv / (1.0 - ADAM_B2 ** ADAM_STEP)
        d_ref[...] = -ADAM_LR * (m_hat / (jnp.sqrt(v_hat) + ADAM_EPS) + ADAM_WD * w_ref[...])
        nm_ref[...] = nm
        nv_ref[...] = nv

    spec = pl.BlockSpec((rows, shape[1]), lambda i: (i, 0))
    return pl.pallas_call(
        body, name=name, grid=(shape[0] // rows,), in_specs=[spec] * 4, out_specs=(spec,) * 3,
        out_shape=(_sds(shape, F32),) * 3, compiler_params=_cp(("parallel",)),
    )(w, g, m, v)


def _rope_tables():
    pos = jnp.arange(SEQ, dtype=F32)
    inv = ROPE_THETA ** (-jnp.arange(0, 64, 2, dtype=F32) / 64)
    ang = pos[:, None] * inv[None, :]
    ang = jnp.concatenate([ang, ang, ang, ang], axis=-1)
    low = (jnp.arange(128) % 64) < 32
    cos, sin = jnp.cos(ang), jnp.sin(ang)
    return cos, jnp.where(low, -sin, 0.0), jnp.where(low, 0.0, sin)


def _local_step(x2, mem2, tgt2, winT, wout, wmem, b_in, sinks, g_branch, ln_gain, ln_bias):
    cos, sa, sb = _rope_tables()
    sink2 = jnp.repeat(sinks, 64, axis=1)
    ones_bd = (jnp.arange(128)[:, None] // 64 == jnp.arange(128)[None, :] // 64).astype(BF16)

    xb, qa, ka, va, bn, b4, b16, qc, z = _in_proj(x2, winT, b_in, cos, sa, sb)
    memb, mkv = _mem_kv(mem2, wmem)
    b4f, b16f = b4.reshape(T, 768), b16.reshape(T, 768)

    swa = dict(kind="band", nb=SEQ // BLK, max_dist=BLK - 1, gqa=True, sink2=sink2)
    dil = (dict(kind="band", nb=SEQ // BLK), dict(kind="band", nb=SEQ // 4 // BLK), dict(kind="band", nb=1))
    oa, lse_a = _attn_fwd("swa_fwd", qa, 0, W_A, ka, 0, va, 0, W_KV_A, **swa)
    o1, l1 = _attn_fwd("dil1_fwd", bn, 0, W_B, bn, 1, bn, 2, W_B, **dil[0])
    o4, l4 = _attn_fwd("dil4_fwd", b4f, 0, W_B, b4f, 1, b4f, 2, W_B, **dil[1])
    o16, l16 = _attn_fwd("dil16_fwd", b16f, 0, W_B, b16f, 1, b16f, 2, W_B, **dil[2])
    oc, lse_c = _attn_fwd("mem_fwd", qc, 0, W_C, mkv, 0, mkv, 1, W_C, kind="mem")

    s4 = (B_LOC, 4, SEQ // 4, W_B)
    s16 = (B_LOC, 16, SEQ // 16, W_B)
    (y, du, dz, doa, dla, dobn, lsen, dlbn, dob4, lse4, dlb4, dob16, lse16, dlb16, doc, dlc, acc) = _middle(
        oa, o1, l1, o4.reshape(s4), l4.reshape(s4), o16.reshape(s16), l16.reshape(s16), oc, z, x2, tgt2,
        g_branch, ln_gain, ln_bias, wout, ones_bd)

    flat = lambda a: a.reshape(T, W_B)
    dqa, dka, dva, dsink = _attn_bwd("swa_bwd", qa, 0, W_A, ka, 0, va, 0, W_KV_A, doa, lse_a, dla, **swa)
    dqn, dkn, dvn = _attn_bwd("dil1_bwd", bn, 0, W_B, bn, 1, bn, 2, W_B, dobn, lsen, dlbn, **dil[0])
    dq4, dk4, dv4 = _attn_bwd("dil4_bwd", b4f, 0, W_B, b4f, 1, b4f, 2, W_B, flat(dob4), flat(lse4), flat(dlb4),
                              **dil[1])
    dq16, dk16, dv16 = _attn_bwd("dil16_bwd", b16f, 0, W_B, b16f, 1, b16f, 2, W_B, flat(dob16), flat(lse16),
                                 flat(dlb16), **dil[2])
    dqc, dmkv = _attn_bwd("mem_bwd", qc, 0, W_C, mkv, 0, mkv, 1, W_C, doc, lse_c, dlc, kind="mem")

    r4 = lambda a: a.reshape(s4)
    r16 = lambda a: a.reshape(s16)
    dh, gx, dbin = _dh_dx(dqa, dka, dva, dqn, dkn, dvn, r4(dq4), r4(dk4), r4(dv4), r16(dq16), r16(dk16), r16(dv16),
                          dqc, dz, du, cos, sa, sb, winT)
    g_in = _tn_matmul("dw_in", dh, xb, 1408, 512)
    g_out = _tn_matmul("dw_out", y, du, D_MIX, 512)
    g_mem = _tn_matmul("dw_mem", memb, dmkv, D_MODEL, B_LOC * MEM_LEN)
    return gx, g_in, g_out, g_mem, acc, dbin, dsink


def kernel(x, mem, w_in, b_in, w_mem, attn_sinks, g_branch, w_out, ln_gain, ln_bias, loss_target, m_w_in, m_b_in, m_w_mem, m_attn_sinks, m_g_branch, m_w_out, m_ln_gain, m_ln_bias, v_w_in, v_b_in, v_w_mem, v_attn_sinks, v_g_branch, v_w_out, v_ln_gain, v_ln_bias):
    winT, wout, wmem = _gather_weights(w_in[0].T, w_out[0], w_mem[0])
    gx, g_in, g_out, g_mem, acc, dbin, dsink = _local_step(
        x.reshape(T, D_MODEL), mem.reshape(B_LOC * MEM_LEN, D_MODEL), loss_target.reshape(T, D_MODEL),
        winT, wout, wmem, b_in, attn_sinks, g_branch, ln_gain, ln_bias)
    r_in, r_out, r_mem, sv = _reduce_grads(g_in, g_out, g_mem, acc, dbin, dsink)

    loss = jnp.sum(sv[0, :D_MODEL])
    grads = {
        "w_in": r_in.T[None], "b_in": sv[4:5, :D_IN], "w_mem": r_mem[None],
        "attn_sinks": -sv[5:6, 0:W_A:64], "g_branch": sv[1:2, :D_MODEL], "w_out": r_out[None],
        "ln_gain": sv[2:3, :D_MODEL], "ln_bias": sv[3:4, :D_MODEL],
    }
    weights = dict(w_in=w_in, b_in=b_in, w_mem=w_mem, attn_sinks=attn_sinks, g_branch=g_branch, w_out=w_out,
                   ln_gain=ln_gain, ln_bias=ln_bias)
    ms = dict(w_in=m_w_in, b_in=m_b_in, w_mem=m_w_mem, attn_sinks=m_attn_sinks, g_branch=m_g_branch, w_out=m_w_out,
              ln_gain=m_ln_gain, ln_bias=m_ln_bias)
    vs = dict(w_in=v_w_in, b_in=v_b_in, w_mem=v_w_mem, attn_sinks=v_attn_sinks, g_branch=v_g_branch, w_out=v_w_out,
              ln_gain=v_ln_gain, ln_bias=v_ln_bias)
    names = ["w_in", "b_in", "w_mem", "attn_sinks", "g_branch", "w_out", "ln_gain", "ln_bias"]
    deltas, new_m, new_v = [], [], []
    for n in names:
        shape = weights[n].shape
        two_d = lambda a: a.reshape(shape[-2], shape[-1])
        rows = 256 if shape[-2] > 256 else None
        d, nm, nv = _adamw("adamw_" + n, two_d(weights[n]), two_d(grads[n]), two_d(ms[n]), two_d(vs[n]), rows)
        deltas.append(d.reshape(shape))
        new_m.append(nm.reshape(shape))
        new_v.append(nv.reshape(shape))
    return (loss, gx.reshape(B_LOC, SEQ, D_MODEL), *[grads[n].reshape(weights[n].shape) for n in names],
            *deltas, *new_m, *new_v)
```

```python
import functools

import jax
import jax.numpy as jnp
from jax import lax
from jax.experimental import pallas as pl
from jax.experimental.pallas import tpu as pltpu

F32, BF16 = jnp.float32, jnp.bfloat16

D_MODEL = 1024
SEQ = 2048
B_LOC = 2
T = B_LOC * SEQ
BLK = 128
MEM_LEN = 256
W_A, W_KV_A, W_B, W_C, D_MIX = 512, 128, 256, 256, 1024
D_IN = 2816
O_QA, O_KA, O_VA, O_QB, O_KB, O_VB, O_QC, O_Z = 0, 512, 640, 768, 1024, 1280, 1536, 1792
ROPE_THETA = 10000.0
LN_EPS = 1e-5
RMS_EPS = 1e-6
ALPHA = 2.0 ** 0.25
QK_SCALE = 0.125
N_CHIP = 4
SH_IN, SH_OUT, SH_MEM = D_IN // N_CHIP, D_MIX // N_CHIP, D_MODEL // N_CHIP
NEG = -1e30
ADAM_LR, ADAM_B1, ADAM_B2, ADAM_EPS, ADAM_WD, ADAM_STEP = 0.001, 0.9, 0.999, 1e-08, 0.01, 10
SV_W = 3072
MESH = pl.DeviceIdType.MESH

NN = ((1,), (0,))
NT = ((1,), (1,))
TN = ((0,), (0,))


def _dot(a, b, dims):
    return lax.dot_general(a, b, (dims, ((), ())), preferred_element_type=F32)


def _cp(sem=None, vmem_mb=None):
    kw = {}
    if sem is not None:
        kw["dimension_semantics"] = sem
    if vmem_mb is not None:
        kw["vmem_limit_bytes"] = vmem_mb * 1024 * 1024
    return pltpu.CompilerParams(**kw)


def _sds(shape, dtype):
    return jax.ShapeDtypeStruct(shape, dtype)


def _full(shape):
    n = len(shape)
    return pl.BlockSpec(shape, lambda *_: (0,) * n)


def _gather_weights(winT_sh, wout_sh, wmem_sh):
    shard_rows = (SH_IN, SH_OUT, SH_MEM)

    def body(a_ref, b_ref, c_ref, oa_ref, ob_ref, oc_ref, ici_send, ici_recv, d2d_send, d2d_recv):
        x, y, c = lax.axis_index("x"), lax.axis_index("y"), lax.axis_index("c")
        sibling = (x, y, 1 - c)
        chips = [(1 - x, y), (x, 1 - y), (1 - x, 1 - y)]
        srcs = (a_ref, b_ref, c_ref)
        outs = (oa_ref, ob_ref, oc_ref)

        def rows(a, chip, half):
            n = shard_rows[a]
            start = pl.multiple_of((2 * chip[0] + chip[1]) * n + half * (n // 2), 16)
            return outs[a].at[pl.ds(start, n // 2), :]

        for a in range(3):
            n = shard_rows[a]
            start = pl.multiple_of((2 * x + y) * n, 16)
            outs[a][pl.ds(start, n), :] = srcs[a][...].astype(BF16)

        def ici(a, j, chip_of_block, to):
            blk = rows(a, chip_of_block, c)
            return pltpu.make_async_remote_copy(
                src_ref=blk, dst_ref=blk, send_sem=ici_send.at[a, j], recv_sem=ici_recv.at[a, j],
                device_id=to, device_id_type=MESH)

        def d2d(a, j, chip_of_block, half, to):
            blk = rows(a, chip_of_block, half)
            return pltpu.make_async_remote_copy(
                src_ref=blk, dst_ref=blk, send_sem=d2d_send.at[a, j], recv_sem=d2d_recv.at[a, j],
                device_id=to, device_id_type=MESH)

        first = [ici(a, j, (x, y), (*chip, c)) for a in range(3) for j, chip in enumerate(chips)]
        for cp in first:
            cp.start()
        passed = []
        for j, chip in enumerate(chips):
            for a in range(3):
                ici(a, j, chip, (x, y, c)).wait_recv()
                fw = d2d(a, j, chip, c, sibling)
                fw.start()
                passed.append(fw)
        for j, chip in enumerate(chips):
            for a in range(3):
                d2d(a, j, chip, 1 - c, (x, y, c)).wait_recv()
        for cp in first + passed:
            cp.wait_send()

    vm = pl.BlockSpec(memory_space=pltpu.VMEM)
    return pl.pallas_call(
        body, name="gather_weights",
        out_shape=(_sds((D_IN, D_MODEL), BF16), _sds((D_MIX, D_MODEL), BF16), _sds((D_MODEL, 2 * W_C), BF16)),
        in_specs=[vm, vm, vm], out_specs=(vm, vm, vm),
        scratch_shapes=[pltpu.SemaphoreType.DMA((3, 3))] * 4,
        compiler_params=_cp(vmem_mb=40),
    )(winT_sh, wout_sh, wmem_sh)


def _rope(t, cos, sa, sb, sign):
    w = t.shape[1]
    reps = w // 128
    c, a, b = (jnp.tile(v, (1, reps)) if reps > 1 else v for v in (cos, sa, sb))
    rot = pltpu.roll(t, w - 32, 1) * a + pltpu.roll(t, 32, 1) * b
    return t * c + rot if sign > 0 else t * c - rot


def _in_proj(x, winT, b_in, cos, sa, sb):
    tm = 256
    spt = SEQ // tm

    def body(x_ref, w_ref, b_ref, cos_ref, sa_ref, sb_ref,
             xb_ref, qa_ref, ka_ref, va_ref, bn_ref, b4_ref, b16_ref, qc_ref, z_ref, scr):
        xb = x_ref[...].astype(BF16)
        xb_ref[...] = xb
        cos_t, sa_t, sb_t = cos_ref[...], sa_ref[...], sb_ref[...]

        def proj(r0, n):
            return _dot(xb, w_ref[r0:r0 + n, :], NT) + b_ref[:, r0:r0 + n]

        def rope(t):
            return _rope(t, cos_t, sa_t, sb_t, +1)

        qa_ref[...] = (rope(proj(O_QA, W_A)) * QK_SCALE).astype(BF16)
        ka_ref[...] = rope(proj(O_KA, W_KV_A)).astype(BF16)
        va_ref[...] = proj(O_VA, W_KV_A).astype(BF16)
        qc_ref[...] = (proj(O_QC, W_C) * QK_SCALE).astype(BF16)
        z_ref[...] = proj(O_Z, D_MIX).astype(BF16)
        parts = (rope(proj(O_QB, W_B)) * QK_SCALE, rope(proj(O_KB, W_B)), proj(O_VB, W_B))
        for k, part in enumerate(parts):
            bn_ref[:, 256 * k:256 * (k + 1)] = part.astype(BF16)
            scr[2 * k] = part[:, :128]
            scr[2 * k + 1] = part[:, 128:]
        for j in range(6):
            for res in range(4):
                b4_ref[0, res, :, 128 * j:128 * (j + 1)] = scr[j, pl.ds(res, tm // 4, stride=4), :].astype(BF16)
            for res in range(16):
                b16_ref[0, res, :, 128 * j:128 * (j + 1)] = scr[j, pl.ds(res, tm // 16, stride=16), :].astype(BF16)

    tok = lambda w: pl.BlockSpec((tm, w), lambda i: (i, 0))
    tab = pl.BlockSpec((tm, 128), lambda i: (i % spt, 0))
    return pl.pallas_call(
        body, name="in_proj", grid=(T // tm,),
        in_specs=[tok(D_MODEL), _full((D_IN, D_MODEL)), _full((1, D_IN)), tab, tab, tab],
        out_specs=(tok(D_MODEL), tok(W_A), tok(W_KV_A), tok(W_KV_A), tok(768),
                   pl.BlockSpec((1, 4, tm // 4, 768), lambda i: (i // spt, 0, i % spt, 0)),
                   pl.BlockSpec((1, 16, tm // 16, 768), lambda i: (i // spt, 0, i % spt, 0)),
                   tok(W_C), tok(D_MIX)),
        out_shape=(_sds((T, D_MODEL), BF16), _sds((T, W_A), BF16), _sds((T, W_KV_A), BF16), _sds((T, W_KV_A), BF16),
                   _sds((T, 768), BF16), _sds((B_LOC, 4, SEQ // 4, 768), BF16), _sds((B_LOC, 16, SEQ // 16, 768), BF16),
                   _sds((T, W_C), BF16), _sds((T, D_MIX), BF16)),
        scratch_shapes=[pltpu.VMEM((6, tm, 128), F32)],
        compiler_params=_cp(("parallel",), vmem_mb=48),
    )(x, winT, b_in, cos, sa, sb)


def _mem_kv(mem, wmem):
    def body(m_ref, w_ref, mb_ref, kv_ref):
        mb = m_ref[...].astype(BF16)
        mb_ref[...] = mb
        kv_ref[...] = _dot(mb, w_ref[...], NN).astype(BF16)

    n = B_LOC * MEM_LEN
    return pl.pallas_call(
        body, name="mem_kv",
        out_shape=(_sds((n, D_MODEL), BF16), _sds((n, 2 * W_C), BF16)),
    )(mem, wmem)


QB = 4
QR = QB * BLK


def _lane_lo():
    return lax.broadcasted_iota(jnp.int32, (1, 128), 1) < 64


def _dup_head(k2, hk, lo):
    kf = k2.astype(F32)
    r = pltpu.roll(kf, 64, 1)
    return (jnp.where(lo, kf, r) if hk == 0 else jnp.where(lo, r, kf)).astype(BF16)


def _stack_heads(pairs, lo):
    parts = []
    for x2 in pairs:
        z = jnp.zeros_like(x2)
        parts += [jnp.where(lo, x2, z), jnp.where(lo, z, x2)]
    return jnp.concatenate(parts, axis=0)


def _prev_mode(kind, nb, j):
    if kind == "mem" or nb == 1:
        return "no"
    if nb <= QB:
        return "yes" if j % nb else "no"
    return "yes" if j else "dyn"


class _Attn:
    def __init__(self, kind, nb, max_dist, gqa, qw, kvw, qcb, kcb, vcb):
        self.kind, self.nb, self.gqa, self.qw, self.kvw = kind, nb, gqa, qw, kvw
        npairs = qw // 128
        self.groups = ([(hk, [2 * hk, 2 * hk + 1]) for hk in range(npairs // 2)] if gqa
                       else [(p, [p]) for p in range(npairs)])
        self.nh = 2 * len(self.groups[0][1])
        self.cols = 128 * self.nh
        self.reach = BLK - max_dist
        self.ext_prev = kind == "band" and nb > QB
        self.kb = MEM_LEN if kind == "mem" else BLK
        self.q_spec = pl.BlockSpec((QR, qw), lambda g: (g, qcb))
        self.row_spec = pl.BlockSpec((QR, qw), lambda g: (g, 0))
        self.stat_spec = pl.BlockSpec((QR, 128), lambda g: (g, 0))
        if kind == "mem":
            per = SEQ // QR
            self.kv_specs = [pl.BlockSpec((MEM_LEN, kvw), lambda g: (g // per, kcb)),
                             pl.BlockSpec((MEM_LEN, kvw), lambda g: (g // per, vcb))]
        else:
            self.kv_specs = [pl.BlockSpec((QR, kvw), lambda g: (g, kcb)), pl.BlockSpec((QR, kvw), lambda g: (g, vcb))]
            if self.ext_prev:
                self.kv_specs += [pl.BlockSpec((BLK, kvw), lambda g: (jnp.maximum(g * QB - 1, 0), kcb)),
                                  pl.BlockSpec((BLK, kvw), lambda g: (jnp.maximum(g * QB - 1, 0), vcb))]

    def masks(self):
        if self.kind == "mem":
            return None
        kj = lax.broadcasted_iota(jnp.int32, (BLK, self.cols), 0)
        qi = lax.broadcasted_iota(jnp.int32, (BLK, self.cols), 1) & (BLK - 1)
        return kj, qi

    def halves(self, j, gi, kc_ref, vc_ref, kp_ref, vp_ref, lo, kq, g):
        def kv(k_ref, v_ref, r):
            if self.gqa:
                return _dup_head(k_ref[r, :], gi, lo), _dup_head(v_ref[r, :], gi, lo)
            sl = slice(128 * gi, 128 * (gi + 1))
            return k_ref[r, sl], v_ref[r, sl]

        if self.kind == "mem":
            return [(*kv(kc_ref, vc_ref, slice(None)), None, pl.multiple_of((g // (SEQ // QR)) * MEM_LEN, MEM_LEN))]
        kj, qi = kq
        row0 = g * QR + BLK * j
        out = [(*kv(kc_ref, vc_ref, slice(BLK * j, BLK * (j + 1))), kj <= qi, pl.multiple_of(row0, BLK))]
        mode = _prev_mode(self.kind, self.nb, j)
        if mode == "yes":
            out.append((*kv(kc_ref, vc_ref, slice(BLK * (j - 1), BLK * j)), kj >= qi + self.reach,
                        pl.multiple_of(row0 - BLK, BLK)))
        elif mode == "dyn":
            has_prev = ((g * QB) % self.nb) > 0
            out.append((*kv(kp_ref, vp_ref, slice(None)), kj >= qi + jnp.where(has_prev, self.reach, 2 * BLK),
                        pl.multiple_of(jnp.maximum(row0 - BLK, 0), BLK)))
        return out


def _attn_fwd(name, q, qcb, qw, k, kcb, v, vcb, kvw, *, kind, nb=1, max_dist=BLK, gqa=False, sinks=None):
    a = _Attn(kind, nb, max_dist, gqa, qw, kvw, qcb, kcb, vcb)

    def body(*refs):
        it = iter(refs)
        q_ref, kc_ref, vc_ref = next(it), next(it), next(it)
        kp_ref, vp_ref = (next(it), next(it)) if a.ext_prev else (None, None)
        sink_ref = next(it) if sinks is not None else None
        o_ref, lse_ref = next(it), next(it)
        g = pl.program_id(0)
        lo = _lane_lo()
        top = lax.broadcasted_iota(jnp.int32, (128, 1), 0) < 64
        rid = lax.broadcasted_iota(jnp.int32, (8, 128), 0)
        kq = a.masks()
        for j in range(QB):
            rows = slice(BLK * j, BLK * (j + 1))
            stat = jnp.zeros((8, 128), F32)
            for gi, pairs in a.groups:
                qs = _stack_heads([q_ref[rows, 128 * p:128 * (p + 1)] for p in pairs], lo)
                hv = a.halves(j, gi, kc_ref, vc_ref, kp_ref, vp_ref, lo, kq, g)
                ss = []
                for kk, vv, mask, _ in hv:
                    s = _dot(kk, qs, NT)
                    ss.append(s if mask is None else jnp.where(mask, s, NEG))
                m = jnp.max(ss[0], axis=0, keepdims=True)
                for s in ss[1:]:
                    m = jnp.maximum(m, jnp.max(s, axis=0, keepdims=True))
                if sink_ref is not None:
                    sk = jnp.concatenate([jnp.full((1, 128), sink_ref[0, a.nh * gi + i], F32) for i in range(a.nh)],
                                         axis=1)
                    m = jnp.maximum(m, sk)
                l, ot = None, None
                for (kk, vv, mask, _), s in zip(hv, ss):
                    p = jnp.exp(s - m)
                    ps = jnp.sum(p, axis=0, keepdims=True)
                    c = _dot(vv, p.astype(BF16), TN)
                    l, ot = (ps, c) if l is None else (l + ps, ot + c)
                if sink_ref is not None:
                    l = l + jnp.exp(sk - m)
                ot = ot * pl.reciprocal(l, approx=True)
                lse = m + jnp.log(l)
                for i, p in enumerate(pairs):
                    o2t = jnp.where(top, ot[:, 256 * i:256 * i + 128], ot[:, 256 * i + 128:256 * i + 256])
                    o_ref[rows, 128 * p:128 * (p + 1)] = o2t.T.astype(BF16)
                for i in range(a.nh):
                    stat = jnp.where(rid == a.nh * gi + i, lse[:, 128 * i:128 * (i + 1)], stat)
            lse_ref[rows, :] = jnp.concatenate([stat, jnp.zeros((120, 128), F32)], axis=0).T

    args = [q, k, v] + ([k, v] if a.ext_prev else [])
    in_specs = [a.q_spec] + a.kv_specs
    if sinks is not None:
        args.append(sinks)
        in_specs.append(pl.BlockSpec(memory_space=pltpu.SMEM))
    return pl.pallas_call(
        body, name=name, grid=(T // QR,), in_specs=in_specs, out_specs=(a.row_spec, a.stat_spec),
        out_shape=(_sds((T, qw), BF16), _sds((T, 128), F32)),
        compiler_params=_cp(("parallel",), vmem_mb=40),
    )(*args)


def _attn_bwd(name, q, qcb, qw, k, kcb, v, vcb, kvw, do, lse, dl, *, kind, nb=1, max_dist=BLK, gqa=False,
              sinkv=None):
    a = _Attn(kind, nb, max_dist, gqa, qw, kvw, qcb, kcb, vcb)

    def body(*refs):
        it = iter(refs)
        q_ref, kc_ref, vc_ref = next(it), next(it), next(it)
        kp_ref, vp_ref = (next(it), next(it)) if a.ext_prev else (None, None)
        do_ref, lse_ref, dl_ref = next(it), next(it), next(it)
        sinkv_ref = next(it) if sinkv is not None else None
        dq_ref = next(it)
        if kind == "mem":
            dkv_ref = next(it)
        else:
            dk_ref, dv_ref = next(it), next(it)
        dsink_ref = next(it) if sinkv is not None else None
        g = pl.program_id(0)
        lo = _lane_lo()
        top = lax.broadcasted_iota(jnp.int32, (128, 1), 0) < 64

        @pl.when(g == 0)
        def _():
            if kind == "mem":
                dkv_ref[...] = jnp.zeros_like(dkv_ref)
            else:
                dk_ref[...] = jnp.zeros_like(dk_ref)
                dv_ref[...] = jnp.zeros_like(dv_ref)
            if dsink_ref is not None:
                dsink_ref[...] = jnp.zeros_like(dsink_ref)

        kq = a.masks()
        for j in range(QB):
            rows = slice(BLK * j, BLK * (j + 1))
            lse_t = lse_ref[rows, :].T
            dl_t = dl_ref[rows, :].T
            for gi, pairs in a.groups:
                heads = [a.nh * gi + i for i in range(a.nh)]
                qs = _stack_heads([q_ref[rows, 128 * p:128 * (p + 1)] for p in pairs], lo)
                dos = _stack_heads([do_ref[rows, 128 * p:128 * (p + 1)] for p in pairs], lo)
                lse_row = jnp.concatenate([lse_t[h:h + 1, :] for h in heads], axis=1)
                dl_row = jnp.concatenate([dl_t[h:h + 1, :] for h in heads], axis=1)
                dqt = None
                for kk, vv, mask, key0 in a.halves(j, gi, kc_ref, vc_ref, kp_ref, vp_ref, lo, kq, g):
                    s = _dot(kk, qs, NT)
                    if mask is not None:
                        s = jnp.where(mask, s, NEG)
                    p = jnp.exp(s - lse_row)
                    ds = (p * (_dot(vv, dos, NT) - dl_row)).astype(BF16)
                    c = _dot(kk, ds, TN)
                    dqt = c if dqt is None else dqt + c
                    ck = _dot(ds, qs, NN)
                    cv = _dot(p.astype(BF16), dos, NN)
                    krows = pl.ds(key0, a.kb)
                    if gqa:
                        sel = lo if gi == 0 else jnp.logical_not(lo)
                        ck = jnp.where(sel, ck + pltpu.roll(ck, 64, 1), 0.0)
                        cv = jnp.where(sel, cv + pltpu.roll(cv, 64, 1), 0.0)
                        kcols = slice(0, 128)
                    else:
                        kcols = slice(128 * gi, 128 * (gi + 1))
                    if kind == "mem":
                        dkv_ref[krows, kcols] += ck
                        dkv_ref[krows, slice(kvw + kcols.start, kvw + kcols.stop)] += cv
                    else:
                        dk_ref[krows, kcols] += ck
                        dv_ref[krows, kcols] += cv
                for i, p in enumerate(pairs):
                    dq2t = jnp.where(top, dqt[:, 256 * i:256 * i + 128], dqt[:, 256 * i + 128:256 * i + 256])
                    dq_ref[rows, 128 * p:128 * (p + 1)] = dq2t.T.astype(BF16)
        if dsink_ref is not None:
            ps = jnp.exp(sinkv_ref[...] - lse_ref[...]) * dl_ref[...]
            dsink_ref[...] += jnp.sum(ps, axis=0, keepdims=True)

    args = [q, k, v] + ([k, v] if a.ext_prev else []) + [do, lse, dl]
    in_specs = [a.q_spec] + a.kv_specs + [a.row_spec, a.stat_spec, a.stat_spec]
    if sinkv is not None:
        args.append(sinkv)
        in_specs.append(_full((1, 128)))
    out_shape = [_sds((T, qw), BF16)]
    out_specs = [a.row_spec]
    if kind == "mem":
        out_shape.append(_sds((B_LOC * MEM_LEN, 2 * kvw), F32))
        out_specs.append(_full((B_LOC * MEM_LEN, 2 * kvw)))
    else:
        out_shape += [_sds((T, kvw), F32)] * 2
        out_specs += [_full((T, kvw))] * 2
    if sinkv is not None:
        out_shape.append(_sds((1, 128), F32))
        out_specs.append(_full((1, 128)))
    return pl.pallas_call(
        body, name=name, grid=(T // QR,), in_specs=in_specs, out_specs=tuple(out_specs),
        out_shape=tuple(out_shape), compiler_params=_cp(("arbitrary",), vmem_mb=48),
    )(*args)


def _dot2(v, w_ref):
    hi = v.astype(BF16)
    lo = (v - hi.astype(F32)).astype(BF16)
    return _dot(hi, w_ref[...], NN) + _dot(lo, w_ref[...], NN)


def _middle(oa, o1, l1, o4, l4, o16, l16, oc, z, x, tgt, g_br, ln_g, ln_b, wout, spread4, gather4, gather8):
    tm = 256
    spt = SEQ // tm

    def body(oa_ref, o1_ref, l1_ref, o4_ref, l4_ref, o16_ref, l16_ref, oc_ref, z_ref, x_ref, t_ref,
             g_ref, lg_ref, lb_ref, w_ref, sp4_ref, ga4_ref, ga8_ref,
             y_ref, du_ref, dz_ref, doa_ref, dla_ref,
             dobn_ref, lsen_ref, dlbn_ref, dob4_ref, lse4_ref, dlb4_ref, dob16_ref, lse16_ref, dlb16_ref,
             doc_ref, dlc_ref, acc_ref, scr):
        i = pl.program_id(0)

        @pl.when(i == 0)
        def _():
            acc_ref[...] = jnp.zeros_like(acc_ref)

        for res in range(4):
            rows = pl.ds(res, tm // 4, stride=4)
            for j in range(2):
                scr[j, rows, :] = o4_ref[0, res, :, 128 * j:128 * (j + 1)].astype(F32)
            scr[2, rows, :] = l4_ref[0, res]
        for res in range(16):
            rows = pl.ds(res, tm // 16, stride=16)
            for j in range(2):
                scr[3 + j, rows, :] = o16_ref[0, res, :, 128 * j:128 * (j + 1)].astype(F32)
            scr[5, rows, :] = l16_ref[0, res]
        cat = lambda a: jnp.concatenate([scr[a], scr[a + 1]], axis=1)
        o1v, o4v, o16v = o1_ref[...].astype(F32), cat(0), cat(3)
        l1v, l4v, l16v = l1_ref[...], scr[2], scr[5]
        mx = jnp.maximum(jnp.maximum(l1v, l4v), l16v)
        e1, e4, e16 = jnp.exp(l1v - mx), jnp.exp(l4v - mx), jnp.exp(l16v - mx)
        ssum = e1 + e4 + e16
        lse_b = mx + jnp.log(ssum)
        inv = 1.0 / ssum
        ob = (_dot2(e1 * inv, sp4_ref) * o1v + _dot2(e4 * inv, sp4_ref) * o4v + _dot2(e16 * inv, sp4_ref) * o16v)
        oav, ocv = oa_ref[...].astype(F32), oc_ref[...].astype(F32)

        def rms(o):
            r = lax.rsqrt(jnp.sum(o * o, axis=1, keepdims=True) * (1.0 / o.shape[1]) + RMS_EPS)
            return o * r, r

        na, ra = rms(oav)
        nb_, rb = rms(ob)
        nc, rc = rms(ocv)
        n = jnp.concatenate([na, nb_, nc], axis=1)
        zf = z_ref[...].astype(F32)
        sig = 1.0 / (1.0 + jnp.exp(-zf))
        sz = zf * sig
        gb = g_ref[...]
        yb = (n * gb * sz).astype(BF16)
        y_ref[...] = yb
        u = ALPHA * x_ref[...] + _dot(yb, w_ref[...], NN)
        inv_d = 1.0 / D_MODEL
        mu = jnp.sum(u, axis=1, keepdims=True) * inv_d
        uc = u - mu
        rstd = lax.rsqrt(jnp.sum(uc * uc, axis=1, keepdims=True) * inv_d + LN_EPS)
        xh = uc * rstd
        lg = lg_ref[...]
        diff = xh * lg + lb_ref[...] - t_ref[...]
        acc_ref[0:1, :] += jnp.sum(diff * diff, axis=0, keepdims=True) * (0.5 * inv_d)
        dout = diff * inv_d
        acc_ref[2:3, :] += jnp.sum(dout * xh, axis=0, keepdims=True)
        acc_ref[3:4, :] += jnp.sum(dout, axis=0, keepdims=True)
        dxh = dout * lg
        du = rstd * (dxh - jnp.sum(dxh, axis=1, keepdims=True) * inv_d
                     - xh * (jnp.sum(dxh * xh, axis=1, keepdims=True) * inv_d))
        dub = du.astype(BF16)
        du_ref[...] = dub
        dy = _dot(dub, w_ref[...], NT)
        t1 = dy * sz
        acc_ref[1:2, :] += jnp.sum(t1 * n, axis=0, keepdims=True)
        dn = t1 * gb
        dz_ref[...] = (dy * n * gb * (sig * (1.0 + zf * (1.0 - sig)))).astype(BF16)

        def rms_bwd(dn_, n_, r):
            return r * (dn_ - n_ * (jnp.sum(dn_ * n_, axis=1, keepdims=True) * (1.0 / n_.shape[1])))

        doa = rms_bwd(dn[:, :W_A], na, ra)
        dob = rms_bwd(dn[:, W_A:W_A + W_B], nb_, rb)
        doc = rms_bwd(dn[:, W_A + W_B:], nc, rc)
        doa_ref[...] = doa.astype(BF16)
        dla_ref[...] = _dot2(doa * oav, ga8_ref)
        doc_ref[...] = doc.astype(BF16)
        dlc_ref[...] = _dot2(doc * ocv, ga4_ref)
        dlb = _dot2(dob * ob, ga4_ref)
        dobn_ref[...] = dob.astype(BF16)
        lsen_ref[...] = lse_b
        dlbn_ref[...] = dlb
        scr[0] = dob[:, :128]
        scr[1] = dob[:, 128:]
        scr[2] = lse_b
        scr[3] = dlb
        for res in range(4):
            rows = pl.ds(res, tm // 4, stride=4)
            for j in range(2):
                dob4_ref[0, res, :, 128 * j:128 * (j + 1)] = scr[j, rows, :].astype(BF16)
            lse4_ref[0, res] = scr[2, rows, :]
            dlb4_ref[0, res] = scr[3, rows, :]
        for res in range(16):
            rows = pl.ds(res, tm // 16, stride=16)
            for j in range(2):
                dob16_ref[0, res, :, 128 * j:128 * (j + 1)] = scr[j, rows, :].astype(BF16)
            lse16_ref[0, res] = scr[2, rows, :]
            dlb16_ref[0, res] = scr[3, rows, :]

    tok = lambda w: pl.BlockSpec((tm, w), lambda i: (i, 0))
    p4 = lambda w: pl.BlockSpec((1, 4, tm // 4, w), lambda i: (i // spt, 0, i % spt, 0))
    p16 = lambda w: pl.BlockSpec((1, 16, tm // 16, w), lambda i: (i // spt, 0, i % spt, 0))
    s4 = lambda w, dt: _sds((B_LOC, 4, SEQ // 4, w), dt)
    s16 = lambda w, dt: _sds((B_LOC, 16, SEQ // 16, w), dt)
    row = _full((1, D_MODEL))
    return pl.pallas_call(
        body, name="middle", grid=(T // tm,),
        in_specs=[tok(W_A), tok(W_B), tok(128), p4(W_B), p4(128), p16(W_B), p16(128), tok(W_C), tok(D_MIX),
                  tok(D_MODEL), tok(D_MODEL), row, row, row, _full((D_MIX, D_MODEL)),
                  _full((128, W_B)), _full((W_B, 128)), _full((W_A, 128))],
        out_specs=(tok(D_MIX), tok(D_MODEL), tok(D_MIX), tok(W_A), tok(128),
                   tok(W_B), tok(128), tok(128), p4(W_B), p4(128), p4(128), p16(W_B), p16(128), p16(128),
                   tok(W_C), tok(128), _full((8, D_MODEL))),
        out_shape=(_sds((T, D_MIX), BF16), _sds((T, D_MODEL), BF16), _sds((T, D_MIX), BF16),
                   _sds((T, W_A), BF16), _sds((T, 128), F32),
                   _sds((T, W_B), BF16), _sds((T, 128), F32), _sds((T, 128), F32),
                   s4(W_B, BF16), s4(128, F32), s4(128, F32), s16(W_B, BF16), s16(128, F32), s16(128, F32),
                   _sds((T, W_C), BF16), _sds((T, 128), F32), _sds((8, D_MODEL), F32)),
        scratch_shapes=[pltpu.VMEM((6, tm, 128), F32)],
        compiler_params=_cp(("arbitrary",), vmem_mb=48),
    )(oa, o1, l1, o4, l4, o16, l16, oc, z, x, tgt, g_br, ln_g, ln_b, wout, spread4, gather4, gather8)


def _dh_dx(dqa, dka, dva, dqn, dkn, dvn, dq4, dk4, dv4, dq16, dk16, dv16, dqc, dz, du, cos, sa, sb, winT):
    tm = 256
    spt = SEQ // tm

    def body(dqa_ref, dka_ref, dva_ref, dqn_ref, dkn_ref, dvn_ref, dq4_ref, dk4_ref, dv4_ref,
             dq16_ref, dk16_ref, dv16_ref, dqc_ref, dz_ref, du_ref, cos_ref, sa_ref, sb_ref, w_ref,
             dh_ref, gx_ref, db_ref, scr):
        i = pl.program_id(0)

        @pl.when(i == 0)
        def _():
            db_ref[...] = jnp.zeros_like(db_ref)

        cos_t, sa_t, sb_t = cos_ref[...], sa_ref[...], sb_ref[...]

        def rope_t(t):
            return _rope(t, cos_t, sa_t, sb_t, -1)

        def put(r0, val):
            n = val.shape[1]
            dh_ref[:, r0:r0 + n] = val.astype(BF16)
            db_ref[:, r0:r0 + n] += jnp.sum(val, axis=0, keepdims=True)

        put(O_QA, rope_t(dqa_ref[...].astype(F32)) * QK_SCALE)
        put(O_KA, rope_t(dka_ref[...]))
        put(O_VA, dva_ref[...])
        put(O_QC, dqc_ref[...].astype(F32) * QK_SCALE)
        put(O_Z, dz_ref[...].astype(F32))
        for k, (n_ref, r4, r16) in enumerate(((dqn_ref, dq4_ref, dq16_ref), (dkn_ref, dk4_ref, dk16_ref),
                                               (dvn_ref, dv4_ref, dv16_ref))):
            for j in range(2):
                sl = slice(128 * j, 128 * (j + 1))
                scr[2 * k + j] = n_ref[:, sl].astype(F32)
                for res in range(4):
                    scr[2 * k + j, pl.ds(res, tm // 4, stride=4), :] += r4[0, res, :, sl].astype(F32)
                for res in range(16):
                    scr[2 * k + j, pl.ds(res, tm // 16, stride=16), :] += r16[0, res, :, sl].astype(F32)
        cat = lambda a: jnp.concatenate([scr[a], scr[a + 1]], axis=1)
        put(O_QB, rope_t(cat(0)) * QK_SCALE)
        put(O_KB, rope_t(cat(2)))
        put(O_VB, cat(4))
        gx_ref[...] = _dot(dh_ref[...], w_ref[...], NN) + ALPHA * du_ref[...].astype(F32)

    tok = lambda w: pl.BlockSpec((tm, w), lambda i: (i, 0))
    tab = pl.BlockSpec((tm, 128), lambda i: (i % spt, 0))
    p4 = pl.BlockSpec((1, 4, tm // 4, W_B), lambda i: (i // spt, 0, i % spt, 0))
    p16 = pl.BlockSpec((1, 16, tm // 16, W_B), lambda i: (i // spt, 0, i % spt, 0))
    return pl.pallas_call(
        body, name="dh_dx", grid=(T // tm,),
        in_specs=[tok(W_A), tok(W_KV_A), tok(W_KV_A), tok(W_B), tok(W_B), tok(W_B), p4, p4, p4, p16, p16, p16,
                  tok(W_C), tok(D_MIX), tok(D_MODEL), tab, tab, tab, _full((D_IN, D_MODEL))],
        out_specs=(tok(D_IN), tok(D_MODEL), _full((1, D_IN))),
        out_shape=(_sds((T, D_IN), BF16), _sds((T, D_MODEL), F32), _sds((1, D_IN), F32)),
        scratch_shapes=[pltpu.VMEM((6, tm, 128), F32)],
        compiler_params=_cp(("arbitrary",), vmem_mb=48),
    )(dqa, dka, dva, dqn, dkn, dvn, dq4, dk4, dv4, dq16, dk16, dv16, dqc, dz, du, cos, sa, sb, winT)


def _tn_matmul(name, a, b, bm, bt):
    n, m_all = a.shape
    n_cols = b.shape[1]

    def body(a_ref, b_ref, o_ref):
        @pl.when(pl.program_id(1) == 0)
        def _():
            o_ref[...] = jnp.zeros_like(o_ref)

        o_ref[...] += _dot(a_ref[...].astype(BF16), b_ref[...].astype(BF16), TN)

    return pl.pallas_call(
        body, name=name, grid=(m_all // bm, n // bt),
        in_specs=[pl.BlockSpec((bt, bm), lambda m, t: (t, m)), pl.BlockSpec((bt, n_cols), lambda m, t: (t, 0))],
        out_specs=pl.BlockSpec((bm, n_cols), lambda m, t: (m, 0)),
        out_shape=_sds((m_all, n_cols), F32),
        compiler_params=_cp(("parallel", "arbitrary"), vmem_mb=48),
    )(a, b)


def _reduce_grads(g_in, g_out, g_mem, acc, dbin, dsink):
    shard_rows = (SH_IN, SH_OUT, SH_MEM)
    widths = (D_MODEL, D_MODEL, 2 * W_C)

    def body(ga_ref, gb_ref, gc_ref, acc_ref, dbin_ref, dsink_ref,
             ra_ref, rb_ref, rc_ref, sv_ref,
             sib_a, sib_b, sib_c, stage_a, stage_b, stage_c, land_a, land_b, land_c, sv_mine, sv_all,
             s1_send, s1_recv, s2_send, s2_recv, s3_send, s3_recv, sv_send, sv_recv):
        x, y, c = lax.axis_index("x"), lax.axis_index("y"), lax.axis_index("c")
        me, sibling = (x, y, c), (x, y, 1 - c)
        my_chip = 2 * x + y
        chips = [(1 - x, y), (x, 1 - y), (1 - x, 1 - y)]
        grads = (ga_ref, gb_ref, gc_ref)
        sibs = (sib_a, sib_b, sib_c)
        stages = (stage_a, stage_b, stage_c)
        lands = (land_a, land_b, land_c)
        res = (ra_ref, rb_ref, rc_ref)

        def half_rows(a, chip_idx, half):
            n = shard_rows[a]
            return pl.ds(pl.multiple_of(chip_idx * n + half * (n // 2), 16), n // 2)

        sv_mine[...] = jnp.zeros_like(sv_mine)
        sv_mine[0:4, 0:D_MODEL] = acc_ref[0:4, :]
        sv_mine[4:5, 0:D_IN] = dbin_ref[...]
        sv_mine[5:6, 0:128] = dsink_ref[...]
        my_dev = 4 * x + 2 * y + c
        others = [(x, y, 1 - c)] + [(*chip, cc) for chip in chips for cc in (c, 1 - c)]

        def sv_copy(j, to):
            return pltpu.make_async_remote_copy(
                src_ref=sv_mine, dst_ref=sv_all.at[my_dev], send_sem=sv_send.at[j], recv_sem=sv_recv.at[j],
                device_id=to, device_id_type=MESH)

        sv_sends = [sv_copy(j, to) for j, to in enumerate(others)]
        for cp in sv_sends:
            cp.start()

        def s1(a, k):
            return pltpu.make_async_remote_copy(
                src_ref=grads[a].at[half_rows(a, k, 1 - c), :], dst_ref=sibs[a].at[k],
                send_sem=s1_send.at[a, k], recv_sem=s1_recv.at[a, k], device_id=sibling, device_id_type=MESH)

        s1s = [s1(a, k) for a in range(3) for k in range(4)]
        for cp in s1s:
            cp.start()

        def s2(a, j, to):
            return pltpu.make_async_remote_copy(
                src_ref=stages[a].at[j], dst_ref=lands[a].at[j], send_sem=s2_send.at[a, j], recv_sem=s2_recv.at[a, j],
                device_id=to, device_id_type=MESH)

        s2s = []
        for j, chip in enumerate(chips):
            k = 2 * chip[0] + chip[1]
            for a in range(3):
                pltpu.make_async_remote_copy(
                    src_ref=grads[a].at[half_rows(a, k, c), :], dst_ref=sibs[a].at[k],
                    send_sem=s1_send.at[a, k], recv_sem=s1_recv.at[a, k], device_id=sibling,
                    device_id_type=MESH).wait_recv()
                stages[a][j] = (grads[a][half_rows(a, k, c), :] + sibs[a][k]).astype(BF16)
                cp = s2(a, j, (*chip, c))
                cp.start()
                s2s.append(cp)

        for a in range(3):
            pltpu.make_async_remote_copy(
                src_ref=grads[a].at[half_rows(a, my_chip, c), :], dst_ref=sibs[a].at[my_chip],
                send_sem=s1_send.at[a, my_chip], recv_sem=s1_recv.at[a, my_chip], device_id=sibling,
                device_id_type=MESH).wait_recv()
        for a in range(3):
            n = shard_rows[a]
            tot = grads[a][half_rows(a, my_chip, c), :] + sibs[a][my_chip]
            for j in range(3):
                s2(a, j, me).wait_recv()
                tot = tot + lands[a][j].astype(F32)
            mine = pl.ds(pl.multiple_of(c * (n // 2), 16), n // 2)
            res[a][mine, :] = tot

        def s3(a, half, to):
            n = shard_rows[a]
            blk = res[a].at[pl.ds(pl.multiple_of(half * (n // 2), 16), n // 2), :]
            return pltpu.make_async_remote_copy(
                src_ref=blk, dst_ref=blk, send_sem=s3_send.at[a], recv_sem=s3_recv.at[a],
                device_id=to, device_id_type=MESH)

        s3s = [s3(a, c, sibling) for a in range(3)]
        for cp in s3s:
            cp.start()
        for a in range(3):
            s3(a, 1 - c, me).wait_recv()

        sv_all[my_dev] = sv_mine[...]
        for j in range(7):
            sv_copy(j, me).wait_recv()
        tot = sv_all[0]
        for d in range(1, 8):
            tot = tot + sv_all[d]
        sv_ref[...] = tot
        for cp in sv_sends + s1s + s2s + s3s:
            cp.wait_send()

    vm = pl.BlockSpec(memory_space=pltpu.VMEM)
    half = lambda a: (shard_rows[a] // 2, widths[a])
    scratch = ([pltpu.VMEM((4, *half(a)), F32) for a in range(3)]
               + [pltpu.VMEM((3, *half(a)), BF16) for a in range(3)]
               + [pltpu.VMEM((3, *half(a)), BF16) for a in range(3)]
               + [pltpu.VMEM((8, SV_W), F32), pltpu.VMEM((8, 8, SV_W), F32)]
               + [pltpu.SemaphoreType.DMA((3, 4))] * 2 + [pltpu.SemaphoreType.DMA((3, 3))] * 2
               + [pltpu.SemaphoreType.DMA((3,))] * 2 + [pltpu.SemaphoreType.DMA((7,))] * 2)
    return pl.pallas_call(
        body, name="reduce_grads",
        out_shape=(_sds((SH_IN, D_MODEL), F32), _sds((SH_OUT, D_MODEL), F32), _sds((SH_MEM, 2 * W_C), F32),
                   _sds((8, SV_W), F32)),
        in_specs=[vm] * 6, out_specs=(vm, vm, vm, vm), scratch_shapes=scratch,
        compiler_params=_cp(vmem_mb=56),
    )(g_in, g_out, g_mem, acc, dbin, dsink)


def _adamw(name, w, g, m, v, rows=None):
    shape = w.shape
    rows = shape[0] if rows is None else rows

    def body(w_ref, g_ref, m_ref, v_ref, d_ref, nm_ref, nv_ref):
        gv = g_ref[...]
        nm = ADAM_B1 * m_ref[...] + (1.0 - ADAM_B1) * gv
        nv = ADAM_B2 * v_ref[...] + (1.0 - ADAM_B2) * (gv * gv)
        m_hat = nm / (1.0 - ADAM_B1 ** ADAM_STEP)
        v_hat = nv / (1.0 - ADAM_B2 ** ADAM_STEP)
        d_ref[...] = -ADAM_LR * (m_hat / (jnp.sqrt(v_hat) + ADAM_EPS) + ADAM_WD * w_ref[...])
        nm_ref[...] = nm
        nv_ref[...] = nv

    spec = pl.BlockSpec((rows, shape[1]), lambda i: (i, 0))
    return pl.pallas_call(
        body, name=name, grid=(shape[0] // rows,), in_specs=[spec] * 4, out_specs=(spec,) * 3,
        out_shape=(_sds(shape, F32),) * 3, compiler_params=_cp(("parallel",)),
    )(w, g, m, v)


def _rope_tables():
    pos = jnp.arange(SEQ, dtype=F32)
    inv = ROPE_THETA ** (-jnp.arange(0, 64, 2, dtype=F32) / 64)
    ang = pos[:, None] * inv[None, :]
    ang = jnp.concatenate([ang, ang, ang, ang], axis=-1)
    low = (jnp.arange(128) % 64) < 32
    cos, sin = jnp.cos(ang), jnp.sin(ang)
    return cos, jnp.where(low, -sin, 0.0), jnp.where(low, 0.0, sin)


def _local_step(x2, mem2, tgt2, winT, wout, wmem, b_in, sinks, g_branch, ln_gain, ln_bias):
    cos, sa, sb = _rope_tables()
    sinkv = jnp.pad(sinks, ((0, 0), (0, 120)))
    head_of_lane = jnp.arange(512)[None, :] // 64
    gather8 = (head_of_lane.T == jnp.arange(128)[None, :]).astype(BF16)
    gather4 = gather8[:W_B]
    spread4 = gather4.T

    xb, qa, ka, va, bn, b4, b16, qc, z = _in_proj(x2, winT, b_in, cos, sa, sb)
    memb, mkv = _mem_kv(mem2, wmem)
    b4f, b16f = b4.reshape(T, 768), b16.reshape(T, 768)

    swa = dict(kind="band", nb=SEQ // BLK, max_dist=BLK - 1, gqa=True)
    dil = (dict(kind="band", nb=SEQ // BLK), dict(kind="band", nb=SEQ // 4 // BLK), dict(kind="band", nb=1))
    oa, lse_a = _attn_fwd("swa_fwd", qa, 0, W_A, ka, 0, va, 0, W_KV_A, sinks=sinks, **swa)
    o1, l1 = _attn_fwd("dil1_fwd", bn, 0, W_B, bn, 1, bn, 2, W_B, **dil[0])
    o4, l4 = _attn_fwd("dil4_fwd", b4f, 0, W_B, b4f, 1, b4f, 2, W_B, **dil[1])
    o16, l16 = _attn_fwd("dil16_fwd", b16f, 0, W_B, b16f, 1, b16f, 2, W_B, **dil[2])
    oc, lse_c = _attn_fwd("mem_fwd", qc, 0, W_C, mkv, 0, mkv, 1, W_C, kind="mem")

    s4 = lambda w: (B_LOC, 4, SEQ // 4, w)
    s16 = lambda w: (B_LOC, 16, SEQ // 16, w)
    (y, du, dz, doa, dla, dobn, lsen, dlbn, dob4, lse4, dlb4, dob16, lse16, dlb16, doc, dlc, acc) = _middle(
        oa, o1, l1, o4.reshape(s4(W_B)), l4.reshape(s4(128)), o16.reshape(s16(W_B)), l16.reshape(s16(128)), oc, z,
        x2, tgt2, g_branch, ln_gain, ln_bias, wout, spread4, gather4, gather8)

    flat = lambda a: a.reshape(T, a.shape[-1])
    dqa, dka, dva, dsink = _attn_bwd("swa_bwd", qa, 0, W_A, ka, 0, va, 0, W_KV_A, doa, lse_a, dla, sinkv=sinkv,
                                     **swa)
    dqn, dkn, dvn = _attn_bwd("dil1_bwd", bn, 0, W_B, bn, 1, bn, 2, W_B, dobn, lsen, dlbn, **dil[0])
    dq4, dk4, dv4 = _attn_bwd("dil4_bwd", b4f, 0, W_B, b4f, 1, b4f, 2, W_B, flat(dob4), flat(lse4), flat(dlb4),
                              **dil[1])
    dq16, dk16, dv16 = _attn_bwd("dil16_bwd", b16f, 0, W_B, b16f, 1, b16f, 2, W_B, flat(dob16), flat(lse16),
                                 flat(dlb16), **dil[2])
    dqc, dmkv = _attn_bwd("mem_bwd", qc, 0, W_C, mkv, 0, mkv, 1, W_C, doc, lse_c, dlc, kind="mem")

    r4 = lambda a: a.reshape(s4(W_B))
    r16 = lambda a: a.reshape(s16(W_B))
    dh, gx, dbin = _dh_dx(dqa, dka, dva, dqn, dkn, dvn, r4(dq4), r4(dk4), r4(dv4), r16(dq16), r16(dk16), r16(dv16),
                          dqc, dz, du, cos, sa, sb, winT)
    g_in = _tn_matmul("dw_in", dh, xb, 1408, 512)
    g_out = _tn_matmul("dw_out", y, du, D_MIX, 512)
    g_mem = _tn_matmul("dw_mem", memb, dmkv, D_MODEL, B_LOC * MEM_LEN)
    return gx, g_in, g_out, g_mem, acc, dbin, dsink


def kernel(x, mem, w_in, b_in, w_mem, attn_sinks, g_branch, w_out, ln_gain, ln_bias, loss_target, m_w_in, m_b_in, m_w_mem, m_attn_sinks, m_g_branch, m_w_out, m_ln_gain, m_ln_bias, v_w_in, v_b_in, v_w_mem, v_attn_sinks, v_g_branch, v_w_out, v_ln_gain, v_ln_bias):
    winT, wout, wmem = _gather_weights(w_in[0].T, w_out[0], w_mem[0])
    gx, g_in, g_out, g_mem, acc, dbin, dsink = _local_step(
        x.reshape(T, D_MODEL), mem.reshape(B_LOC * MEM_LEN, D_MODEL), loss_target.reshape(T, D_MODEL),
        winT, wout, wmem, b_in, attn_sinks, g_branch, ln_gain, ln_bias)
    r_in, r_out, r_mem, sv = _reduce_grads(g_in, g_out, g_mem, acc, dbin, dsink)

    loss = jnp.sum(sv[0, :D_MODEL])
    grads = {
        "w_in": r_in.T[None], "b_in": sv[4:5, :D_IN], "w_mem": r_mem[None],
        "attn_sinks": -sv[5:6, 0:8], "g_branch": sv[1:2, :D_MODEL], "w_out": r_out[None],
        "ln_gain": sv[2:3, :D_MODEL], "ln_bias": sv[3:4, :D_MODEL],
    }
    weights = dict(w_in=w_in, b_in=b_in, w_mem=w_mem, attn_sinks=attn_sinks, g_branch=g_branch, w_out=w_out,
                   ln_gain=ln_gain, ln_bias=ln_bias)
    ms = dict(w_in=m_w_in, b_in=m_b_in, w_mem=m_w_mem, attn_sinks=m_attn_sinks, g_branch=m_g_branch, w_out=m_w_out,
              ln_gain=m_ln_gain, ln_bias=m_ln_bias)
    vs = dict(w_in=v_w_in, b_in=v_b_in, w_mem=v_w_mem, attn_sinks=v_attn_sinks, g_branch=v_g_branch, w_out=v_w_out,
              ln_gain=v_ln_gain, ln_bias=v_ln_bias)
    names = ["w_in", "b_in", "w_mem", "attn_sinks", "g_branch", "w_out", "ln_gain", "ln_bias"]
    deltas, new_m, new_v = [], [], []
    for n in names:
        shape = weights[n].shape
        two_d = lambda a: a.reshape(shape[-2], shape[-1])
        rows = 256 if shape[-2] > 256 else None
        d, nm, nv = _adamw("adamw_" + n, two_d(weights[n]), two_d(grads[n]), two_d(ms[n]), two_d(vs[n]), rows)
        deltas.append(d.reshape(shape))
        new_m.append(nm.reshape(shape))
        new_v.append(nv.reshape(shape))
    return (loss, gx.reshape(B_LOC, SEQ, D_MODEL), *[grads[n].reshape(weights[n].shape) for n in names],
            *deltas, *new_m, *new_v)
```

```python
import functools

import jax
import jax.numpy as jnp
from jax import lax
from jax.experimental import pallas as pl
from jax.experimental.pallas import tpu as pltpu

F32, BF16 = jnp.float32, jnp.bfloat16

D_MODEL = 1024
SEQ = 2048
B_LOC = 2
T = B_LOC * SEQ
BLK = 128
MEM_LEN = 256
W_A, W_KV_A, W_B, W_C, D_MIX = 512, 128, 256, 256, 1024
D_IN = 2816
O_QA, O_KA, O_VA, O_QB, O_KB, O_VB, O_QC, O_Z = 0, 512, 640, 768, 1024, 1280, 1536, 1792
ROPE_THETA = 10000.0
LN_EPS = 1e-5
RMS_EPS = 1e-6
ALPHA = 2.0 ** 0.25
QK_SCALE = 0.125
N_CHIP = 4
SH_IN, SH_OUT, SH_MEM = D_IN // N_CHIP, D_MIX // N_CHIP, D_MODEL // N_CHIP
NEG = -1e30
ADAM_LR, ADAM_B1, ADAM_B2, ADAM_EPS, ADAM_WD, ADAM_STEP = 0.001, 0.9, 0.999, 1e-08, 0.01, 10
SV_W = 3072
MESH = pl.DeviceIdType.MESH

NN = ((1,), (0,))
NT = ((1,), (1,))
TN = ((0,), (0,))


def _dot(a, b, dims):
    return lax.dot_general(a, b, (dims, ((), ())), preferred_element_type=F32)


def _cp(sem=None, vmem_mb=None):
    kw = {}
    if sem is not None:
        kw["dimension_semantics"] = sem
    if vmem_mb is not None:
        kw["vmem_limit_bytes"] = vmem_mb * 1024 * 1024
    return pltpu.CompilerParams(**kw)


def _sds(shape, dtype):
    return pltpu.HBM(shape, dtype)


def _vm_sds(shape, dtype):
    return jax.ShapeDtypeStruct(shape, dtype)


def _pin(*args):
    return [pltpu.with_memory_space_constraint(a, pltpu.HBM) for a in args]


def _full(shape):
    n = len(shape)
    return pl.BlockSpec(shape, lambda *_: (0,) * n)


def _gather_weights(win_sh, wout_sh, wmem_sh):
    shard_rows = (SH_IN, SH_OUT, SH_MEM)

    def body(a_ref, b_ref, c_ref, oa_ref, ob_ref, oc_ref, ici_send, ici_recv, d2d_send, d2d_recv):
        x, y, c = lax.axis_index("x"), lax.axis_index("y"), lax.axis_index("c")
        sibling = (x, y, 1 - c)
        chips = [(1 - x, y), (x, 1 - y), (1 - x, 1 - y)]
        srcs = (a_ref, b_ref, c_ref)
        outs = (oa_ref, ob_ref, oc_ref)

        def rows(a, chip, half):
            n = shard_rows[a]
            start = pl.multiple_of((2 * chip[0] + chip[1]) * n + half * (n // 2), 16)
            return outs[a].at[pl.ds(start, n // 2), :]

        for a in range(3):
            n = shard_rows[a]
            start = pl.multiple_of((2 * x + y) * n, 16)
            own = srcs[a][...]
            outs[a][pl.ds(start, n), :] = (own.T if a == 0 else own).astype(BF16)

        def ici(a, j, chip_of_block, to):
            blk = rows(a, chip_of_block, c)
            return pltpu.make_async_remote_copy(
                src_ref=blk, dst_ref=blk, send_sem=ici_send.at[a, j], recv_sem=ici_recv.at[a, j],
                device_id=to, device_id_type=MESH)

        def d2d(a, j, chip_of_block, half, to):
            blk = rows(a, chip_of_block, half)
            return pltpu.make_async_remote_copy(
                src_ref=blk, dst_ref=blk, send_sem=d2d_send.at[a, j], recv_sem=d2d_recv.at[a, j],
                device_id=to, device_id_type=MESH)

        first = [ici(a, j, (x, y), (*chip, c)) for a in range(3) for j, chip in enumerate(chips)]
        for cp in first:
            cp.start()
        passed = []
        for j, chip in enumerate(chips):
            for a in range(3):
                ici(a, j, chip, (x, y, c)).wait_recv()
                fw = d2d(a, j, chip, c, sibling)
                fw.start()
                passed.append(fw)
        for j, chip in enumerate(chips):
            for a in range(3):
                d2d(a, j, chip, 1 - c, (x, y, c)).wait_recv()
        for cp in first + passed:
            cp.wait_send()

    vm = pl.BlockSpec(memory_space=pltpu.VMEM)
    return pl.pallas_call(
        body, name="gather_weights",
        out_shape=(_vm_sds((D_IN, D_MODEL), BF16), _vm_sds((D_MIX, D_MODEL), BF16),
                   _vm_sds((D_MODEL, 2 * W_C), BF16)),
        in_specs=[vm, vm, vm], out_specs=(vm, vm, vm),
        scratch_shapes=[pltpu.SemaphoreType.DMA((3, 3))] * 4,
        compiler_params=_cp(vmem_mb=40),
    )(win_sh, wout_sh, wmem_sh)


def _rope(t, cos, sa, sb, sign):
    w = t.shape[1]
    reps = w // 128
    c, a, b = (jnp.tile(v, (1, reps)) if reps > 1 else v for v in (cos, sa, sb))
    rot = pltpu.roll(t, w - 32, 1) * a + pltpu.roll(t, 32, 1) * b
    return t * c + rot if sign > 0 else t * c - rot


def _in_proj(x, winT, b_in, cos, sa, sb):
    tm = 256
    spt = SEQ // tm

    def body(x_ref, w_ref, b_ref, cos_ref, sa_ref, sb_ref,
             xb_ref, qa_ref, ka_ref, va_ref, bn_ref, b4_ref, b16_ref, qc_ref, z_ref, scr):
        xb = x_ref[...].astype(BF16)
        xb_ref[...] = xb
        cos_t, sa_t, sb_t = cos_ref[...], sa_ref[...], sb_ref[...]

        def proj(r0, n):
            return _dot(xb, w_ref[r0:r0 + n, :], NT) + b_ref[:, r0:r0 + n]

        def rope(t):
            return _rope(t, cos_t, sa_t, sb_t, +1)

        qa_ref[...] = (rope(proj(O_QA, W_A)) * QK_SCALE).astype(BF16)
        ka_ref[...] = rope(proj(O_KA, W_KV_A)).astype(BF16)
        va_ref[...] = proj(O_VA, W_KV_A).astype(BF16)
        qc_ref[...] = (proj(O_QC, W_C) * QK_SCALE).astype(BF16)
        z_ref[...] = proj(O_Z, D_MIX).astype(BF16)
        parts = (rope(proj(O_QB, W_B)) * QK_SCALE, rope(proj(O_KB, W_B)), proj(O_VB, W_B))
        for k, part in enumerate(parts):
            bn_ref[:, 256 * k:256 * (k + 1)] = part.astype(BF16)
            scr[2 * k] = part[:, :128]
            scr[2 * k + 1] = part[:, 128:]
        for j in range(6):
            for res in range(4):
                b4_ref[0, res, :, 128 * j:128 * (j + 1)] = scr[j, pl.ds(res, tm // 4, stride=4), :].astype(BF16)
            for res in range(16):
                b16_ref[0, res, :, 128 * j:128 * (j + 1)] = scr[j, pl.ds(res, tm // 16, stride=16), :].astype(BF16)

    tok = lambda w: pl.BlockSpec((tm, w), lambda i: (i, 0))
    tab = pl.BlockSpec((tm, 128), lambda i: (i % spt, 0))
    return pl.pallas_call(
        body, name="in_proj", grid=(T // tm,),
        in_specs=[tok(D_MODEL), _full((D_IN, D_MODEL)), _full((1, D_IN)), tab, tab, tab],
        out_specs=(tok(D_MODEL), tok(W_A), tok(W_KV_A), tok(W_KV_A), tok(768),
                   pl.BlockSpec((1, 4, tm // 4, 768), lambda i: (i // spt, 0, i % spt, 0)),
                   pl.BlockSpec((1, 16, tm // 16, 768), lambda i: (i // spt, 0, i % spt, 0)),
                   tok(W_C), tok(D_MIX)),
        out_shape=(_sds((T, D_MODEL), BF16), _sds((T, W_A), BF16), _sds((T, W_KV_A), BF16), _sds((T, W_KV_A), BF16),
                   _sds((T, 768), BF16), _sds((B_LOC, 4, SEQ // 4, 768), BF16), _sds((B_LOC, 16, SEQ // 16, 768), BF16),
                   _sds((T, W_C), BF16), _sds((T, D_MIX), BF16)),
        scratch_shapes=[pltpu.VMEM((6, tm, 128), F32)],
        compiler_params=_cp(("parallel",), vmem_mb=48),
    )(*_pin(x, winT, b_in, cos, sa, sb))


def _mem_kv(mem, wmem):
    def body(m_ref, w_ref, mb_ref, kv_ref):
        mb = m_ref[...].astype(BF16)
        mb_ref[...] = mb
        kv_ref[...] = _dot(mb, w_ref[...], NN).astype(BF16)

    n = B_LOC * MEM_LEN
    return pl.pallas_call(
        body, name="mem_kv",
        out_shape=(_sds((n, D_MODEL), BF16), _sds((n, 2 * W_C), BF16)),
    )(*_pin(mem, wmem))


QB = 4
QR = QB * BLK


def _lane_lo():
    return lax.broadcasted_iota(jnp.int32, (1, 128), 1) < 64


def _dup_head(k2, hk, lo):
    kf = k2.astype(F32)
    r = pltpu.roll(kf, 64, 1)
    return (jnp.where(lo, kf, r) if hk == 0 else jnp.where(lo, r, kf)).astype(BF16)


def _stack_heads(pairs, lo):
    parts = []
    for x2 in pairs:
        z = jnp.zeros_like(x2)
        parts += [jnp.where(lo, x2, z), jnp.where(lo, z, x2)]
    return jnp.concatenate(parts, axis=0)


def _prev_mode(kind, nb, j):
    if kind == "mem" or nb == 1:
        return "no"
    if nb <= QB:
        return "yes" if j % nb else "no"
    return "yes" if j else "dyn"


class _Attn:
    def __init__(self, kind, nb, max_dist, gqa, qw, kvw, qcb, kcb, vcb):
        self.kind, self.nb, self.gqa, self.qw, self.kvw = kind, nb, gqa, qw, kvw
        npairs = qw // 128
        self.groups = ([(hk, [2 * hk, 2 * hk + 1]) for hk in range(npairs // 2)] if gqa
                       else [(p, [p]) for p in range(npairs)])
        self.nh = 2 * len(self.groups[0][1])
        self.cols = 128 * self.nh
        self.reach = BLK - max_dist
        self.ext_prev = kind == "band" and nb > QB
        self.kb = MEM_LEN if kind == "mem" else BLK
        self.q_spec = pl.BlockSpec((QR, qw), lambda g: (g, qcb))
        self.row_spec = pl.BlockSpec((QR, qw), lambda g: (g, 0))
        self.stat_spec = pl.BlockSpec((QR, 128), lambda g: (g, 0))
        if kind == "mem":
            per = SEQ // QR
            self.kv_specs = [pl.BlockSpec((MEM_LEN, kvw), lambda g: (g // per, kcb)),
                             pl.BlockSpec((MEM_LEN, kvw), lambda g: (g // per, vcb))]
        else:
            self.kv_specs = [pl.BlockSpec((QR, kvw), lambda g: (g, kcb)), pl.BlockSpec((QR, kvw), lambda g: (g, vcb))]
            if self.ext_prev:
                self.kv_specs += [pl.BlockSpec((BLK, kvw), lambda g: (jnp.maximum(g * QB - 1, 0), kcb)),
                                  pl.BlockSpec((BLK, kvw), lambda g: (jnp.maximum(g * QB - 1, 0), vcb))]

    def masks(self):
        if self.kind == "mem":
            return None
        kj = lax.broadcasted_iota(jnp.int32, (BLK, self.cols), 0)
        qi = lax.broadcasted_iota(jnp.int32, (BLK, self.cols), 1) & (BLK - 1)
        return kj, qi

    def halves(self, j, gi, kc_ref, vc_ref, kp_ref, vp_ref, lo, kq, g):
        def kv(k_ref, v_ref, r):
            if self.gqa:
                return _dup_head(k_ref[r, :], gi, lo), _dup_head(v_ref[r, :], gi, lo)
            sl = slice(128 * gi, 128 * (gi + 1))
            return k_ref[r, sl], v_ref[r, sl]

        if self.kind == "mem":
            return [(*kv(kc_ref, vc_ref, slice(None)), None, pl.multiple_of((g // (SEQ // QR)) * MEM_LEN, MEM_LEN))]
        kj, qi = kq
        row0 = g * QR + BLK * j
        out = [(*kv(kc_ref, vc_ref, slice(BLK * j, BLK * (j + 1))), kj <= qi, pl.multiple_of(row0, BLK))]
        mode = _prev_mode(self.kind, self.nb, j)
        if mode == "yes":
            out.append((*kv(kc_ref, vc_ref, slice(BLK * (j - 1), BLK * j)), kj >= qi + self.reach,
                        pl.multiple_of(row0 - BLK, BLK)))
        elif mode == "dyn":
            has_prev = ((g * QB) % self.nb) > 0
            out.append((*kv(kp_ref, vp_ref, slice(None)), kj >= qi + jnp.where(has_prev, self.reach, 2 * BLK),
                        pl.multiple_of(jnp.maximum(row0 - BLK, 0), BLK)))
        return out


def _attn_fwd(name, q, qcb, qw, k, kcb, v, vcb, kvw, *, kind, nb=1, max_dist=BLK, gqa=False, sinks=None):
    a = _Attn(kind, nb, max_dist, gqa, qw, kvw, qcb, kcb, vcb)

    def body(*refs):
        it = iter(refs)
        q_ref, kc_ref, vc_ref = next(it), next(it), next(it)
        kp_ref, vp_ref = (next(it), next(it)) if a.ext_prev else (None, None)
        sink_ref = next(it) if sinks is not None else None
        o_ref, lse_ref = next(it), next(it)
        g = pl.program_id(0)
        lo = _lane_lo()
        top = lax.broadcasted_iota(jnp.int32, (128, 1), 0) < 64
        rid = lax.broadcasted_iota(jnp.int32, (8, 128), 0)
        kq = a.masks()
        for j in range(QB):
            rows = slice(BLK * j, BLK * (j + 1))
            stat = jnp.zeros((8, 128), F32)
            for gi, pairs in a.groups:
                qs = _stack_heads([q_ref[rows, 128 * p:128 * (p + 1)] for p in pairs], lo)
                hv = a.halves(j, gi, kc_ref, vc_ref, kp_ref, vp_ref, lo, kq, g)
                ss = []
                for kk, vv, mask, _ in hv:
                    s = _dot(kk, qs, NT)
                    ss.append(s if mask is None else jnp.where(mask, s, NEG))
                m = jnp.max(ss[0], axis=0, keepdims=True)
                for s in ss[1:]:
                    m = jnp.maximum(m, jnp.max(s, axis=0, keepdims=True))
                if sink_ref is not None:
                    sk = jnp.concatenate([jnp.full((1, 128), sink_ref[0, a.nh * gi + i], F32) for i in range(a.nh)],
                                         axis=1)
                    m = jnp.maximum(m, sk)
                l, ot = None, None
                for (kk, vv, mask, _), s in zip(hv, ss):
                    p = jnp.exp(s - m)
                    ps = jnp.sum(p, axis=0, keepdims=True)
                    c = _dot(vv, p.astype(BF16), TN)
                    l, ot = (ps, c) if l is None else (l + ps, ot + c)
                if sink_ref is not None:
                    l = l + jnp.exp(sk - m)
                ot = ot * pl.reciprocal(l, approx=True)
                lse = m + jnp.log(l)
                for i, p in enumerate(pairs):
                    o2t = jnp.where(top, ot[:, 256 * i:256 * i + 128], ot[:, 256 * i + 128:256 * i + 256])
                    o_ref[rows, 128 * p:128 * (p + 1)] = o2t.T.astype(BF16)
                for i in range(a.nh):
                    stat = jnp.where(rid == a.nh * gi + i, lse[:, 128 * i:128 * (i + 1)], stat)
            lse_ref[rows, :] = jnp.concatenate([stat, jnp.zeros((120, 128), F32)], axis=0).T

    args = [q, k, v] + ([k, v] if a.ext_prev else [])
    in_specs = [a.q_spec] + a.kv_specs
    if sinks is not None:
        args.append(sinks)
        in_specs.append(pl.BlockSpec(memory_space=pltpu.SMEM))
    return pl.pallas_call(
        body, name=name, grid=(T // QR,), in_specs=in_specs, out_specs=(a.row_spec, a.stat_spec),
        out_shape=(_sds((T, qw), BF16), _sds((T, 128), F32)),
        compiler_params=_cp(("parallel",), vmem_mb=40),
    )(*_pin(*args))


def _attn_bwd(name, q, qcb, qw, k, kcb, v, vcb, kvw, do, lse, dl, *, kind, nb=1, max_dist=BLK, gqa=False,
              sinkv=None):
    a = _Attn(kind, nb, max_dist, gqa, qw, kvw, qcb, kcb, vcb)

    def body(*refs):
        it = iter(refs)
        q_ref, kc_ref, vc_ref = next(it), next(it), next(it)
        kp_ref, vp_ref = (next(it), next(it)) if a.ext_prev else (None, None)
        do_ref, lse_ref, dl_ref = next(it), next(it), next(it)
        sinkv_ref = next(it) if sinkv is not None else None
        dq_ref = next(it)
        if kind == "mem":
            dkv_ref = next(it)
        else:
            dk_ref, dv_ref = next(it), next(it)
        dsink_ref = next(it) if sinkv is not None else None
        g = pl.program_id(0)
        lo = _lane_lo()
        top = lax.broadcasted_iota(jnp.int32, (128, 1), 0) < 64

        @pl.when(g == 0)
        def _():
            if kind == "mem":
                dkv_ref[...] = jnp.zeros_like(dkv_ref)
            else:
                dk_ref[...] = jnp.zeros_like(dk_ref)
                dv_ref[...] = jnp.zeros_like(dv_ref)
            if dsink_ref is not None:
                dsink_ref[...] = jnp.zeros_like(dsink_ref)

        kq = a.masks()
        for j in range(QB):
            rows = slice(BLK * j, BLK * (j + 1))
            lse_t = lse_ref[rows, :].T
            dl_t = dl_ref[rows, :].T
            for gi, pairs in a.groups:
                heads = [a.nh * gi + i for i in range(a.nh)]
                qs = _stack_heads([q_ref[rows, 128 * p:128 * (p + 1)] for p in pairs], lo)
                dos = _stack_heads([do_ref[rows, 128 * p:128 * (p + 1)] for p in pairs], lo)
                lse_row = jnp.concatenate([lse_t[h:h + 1, :] for h in heads], axis=1)
                dl_row = jnp.concatenate([dl_t[h:h + 1, :] for h in heads], axis=1)
                dqt = None
                for kk, vv, mask, key0 in a.halves(j, gi, kc_ref, vc_ref, kp_ref, vp_ref, lo, kq, g):
                    s = _dot(kk, qs, NT)
                    if mask is not None:
                        s = jnp.where(mask, s, NEG)
                    p = jnp.exp(s - lse_row)
                    ds = (p * (_dot(vv, dos, NT) - dl_row)).astype(BF16)
                    c = _dot(kk, ds, TN)
                    dqt = c if dqt is None else dqt + c
                    ck = _dot(ds, qs, NN)
                    cv = _dot(p.astype(BF16), dos, NN)
                    krows = pl.ds(key0, a.kb)
                    if gqa:
                        sel = lo if gi == 0 else jnp.logical_not(lo)
                        ck = jnp.where(sel, ck + pltpu.roll(ck, 64, 1), 0.0)
                        cv = jnp.where(sel, cv + pltpu.roll(cv, 64, 1), 0.0)
                        kcols = slice(0, 128)
                    else:
                        kcols = slice(128 * gi, 128 * (gi + 1))
                    if kind == "mem":
                        dkv_ref[krows, kcols] += ck
                        dkv_ref[krows, slice(kvw + kcols.start, kvw + kcols.stop)] += cv
                    else:
                        dk_ref[krows, kcols] += ck
                        dv_ref[krows, kcols] += cv
                for i, p in enumerate(pairs):
                    dq2t = jnp.where(top, dqt[:, 256 * i:256 * i + 128], dqt[:, 256 * i + 128:256 * i + 256])
                    dq_ref[rows, 128 * p:128 * (p + 1)] = dq2t.T.astype(BF16)
        if dsink_ref is not None:
            ps = jnp.exp(sinkv_ref[...] - lse_ref[...]) * dl_ref[...]
            dsink_ref[...] += jnp.sum(ps, axis=0, keepdims=True)

    args = [q, k, v] + ([k, v] if a.ext_prev else []) + [do, lse, dl]
    in_specs = [a.q_spec] + a.kv_specs + [a.row_spec, a.stat_spec, a.stat_spec]
    if sinkv is not None:
        args.append(sinkv)
        in_specs.append(_full((1, 128)))
    out_shape = [_sds((T, qw), BF16)]
    out_specs = [a.row_spec]
    if kind == "mem":
        out_shape.append(_sds((B_LOC * MEM_LEN, 2 * kvw), F32))
        out_specs.append(_full((B_LOC * MEM_LEN, 2 * kvw)))
    else:
        out_shape += [_sds((T, kvw), F32)] * 2
        out_specs += [_full((T, kvw))] * 2
    if sinkv is not None:
        out_shape.append(_sds((1, 128), F32))
        out_specs.append(_full((1, 128)))
    return pl.pallas_call(
        body, name=name, grid=(T // QR,), in_specs=in_specs, out_specs=tuple(out_specs),
        out_shape=tuple(out_shape), compiler_params=_cp(("arbitrary",), vmem_mb=48),
    )(*_pin(*args))


def _dot2(v, w_ref):
    hi = v.astype(BF16)
    lo = (v - hi.astype(F32)).astype(BF16)
    return _dot(hi, w_ref[...], NN) + _dot(lo, w_ref[...], NN)


def _middle(oa, o1, l1, o4, l4, o16, l16, oc, z, x, tgt, g_br, ln_g, ln_b, wout, spread4, gather4, gather8):
    tm = 256
    spt = SEQ // tm

    def body(oa_ref, o1_ref, l1_ref, o4_ref, l4_ref, o16_ref, l16_ref, oc_ref, z_ref, x_ref, t_ref,
             g_ref, lg_ref, lb_ref, w_ref, sp4_ref, ga4_ref, ga8_ref,
             du_ref, dz_ref, doa_ref, dla_ref,
             dobn_ref, lsen_ref, dlbn_ref, dob4_ref, lse4_ref, dlb4_ref, dob16_ref, lse16_ref, dlb16_ref,
             doc_ref, dlc_ref, acc_ref, gout_ref, scr):
        i = pl.program_id(0)

        @pl.when(i == 0)
        def _():
            acc_ref[...] = jnp.zeros_like(acc_ref)
            gout_ref[...] = jnp.zeros_like(gout_ref)

        for res in range(4):
            rows = pl.ds(res, tm // 4, stride=4)
            for j in range(2):
                scr[j, rows, :] = o4_ref[0, res, :, 128 * j:128 * (j + 1)].astype(F32)
            scr[2, rows, :] = l4_ref[0, res]
        for res in range(16):
            rows = pl.ds(res, tm // 16, stride=16)
            for j in range(2):
                scr[3 + j, rows, :] = o16_ref[0, res, :, 128 * j:128 * (j + 1)].astype(F32)
            scr[5, rows, :] = l16_ref[0, res]
        cat = lambda a: jnp.concatenate([scr[a], scr[a + 1]], axis=1)
        o1v, o4v, o16v = o1_ref[...].astype(F32), cat(0), cat(3)
        l1v, l4v, l16v = l1_ref[...], scr[2], scr[5]
        mx = jnp.maximum(jnp.maximum(l1v, l4v), l16v)
        e1, e4, e16 = jnp.exp(l1v - mx), jnp.exp(l4v - mx), jnp.exp(l16v - mx)
        ssum = e1 + e4 + e16
        lse_b = mx + jnp.log(ssum)
        inv = 1.0 / ssum
        ob = (_dot2(e1 * inv, sp4_ref) * o1v + _dot2(e4 * inv, sp4_ref) * o4v + _dot2(e16 * inv, sp4_ref) * o16v)
        oav, ocv = oa_ref[...].astype(F32), oc_ref[...].astype(F32)

        def rms(o):
            r = lax.rsqrt(jnp.sum(o * o, axis=1, keepdims=True) * (1.0 / o.shape[1]) + RMS_EPS)
            return o * r, r

        na, ra = rms(oav)
        nb_, rb = rms(ob)
        nc, rc = rms(ocv)
        n = jnp.concatenate([na, nb_, nc], axis=1)
        zf = z_ref[...].astype(F32)
        sig = 1.0 / (1.0 + jnp.exp(-zf))
        sz = zf * sig
        gb = g_ref[...]
        yb = (n * gb * sz).astype(BF16)
        u = ALPHA * x_ref[...] + _dot(yb, w_ref[...], NN)
        inv_d = 1.0 / D_MODEL
        mu = jnp.sum(u, axis=1, keepdims=True) * inv_d
        uc = u - mu
        rstd = lax.rsqrt(jnp.sum(uc * uc, axis=1, keepdims=True) * inv_d + LN_EPS)
        xh = uc * rstd
        lg = lg_ref[...]
        diff = xh * lg + lb_ref[...] - t_ref[...]
        acc_ref[0:1, :] += jnp.sum(diff * diff, axis=0, keepdims=True) * (0.5 * inv_d)
        dout = diff * inv_d
        acc_ref[2:3, :] += jnp.sum(dout * xh, axis=0, keepdims=True)
        acc_ref[3:4, :] += jnp.sum(dout, axis=0, keepdims=True)
        dxh = dout * lg
        du = rstd * (dxh - jnp.sum(dxh, axis=1, keepdims=True) * inv_d
                     - xh * (jnp.sum(dxh * xh, axis=1, keepdims=True) * inv_d))
        dub = du.astype(BF16)
        du_ref[...] = dub
        gout_ref[...] += _dot(yb, dub, TN)
        dy = _dot(dub, w_ref[...], NT)
        t1 = dy * sz
        acc_ref[1:2, :] += jnp.sum(t1 * n, axis=0, keepdims=True)
        dn = t1 * gb
        dz_ref[...] = (dy * n * gb * (sig * (1.0 + zf * (1.0 - sig)))).astype(BF16)

        def rms_bwd(dn_, n_, r):
            return r * (dn_ - n_ * (jnp.sum(dn_ * n_, axis=1, keepdims=True) * (1.0 / n_.shape[1])))

        doa = rms_bwd(dn[:, :W_A], na, ra)
        dob = rms_bwd(dn[:, W_A:W_A + W_B], nb_, rb)
        doc = rms_bwd(dn[:, W_A + W_B:], nc, rc)
        doa_ref[...] = doa.astype(BF16)
        dla_ref[...] = _dot2(doa * oav, ga8_ref)
        doc_ref[...] = doc.astype(BF16)
        dlc_ref[...] = _dot2(doc * ocv, ga4_ref)
        dlb = _dot2(dob * ob, ga4_ref)
        dobn_ref[...] = dob.astype(BF16)
        lsen_ref[...] = lse_b
        dlbn_ref[...] = dlb
        scr[0] = dob[:, :128]
        scr[1] = dob[:, 128:]
        scr[2] = lse_b
        scr[3] = dlb
        for res in range(4):
            rows = pl.ds(res, tm // 4, stride=4)
            for j in range(2):
                dob4_ref[0, res, :, 128 * j:128 * (j + 1)] = scr[j, rows, :].astype(BF16)
            lse4_ref[0, res] = scr[2, rows, :]
            dlb4_ref[0, res] = scr[3, rows, :]
        for res in range(16):
            rows = pl.ds(res, tm // 16, stride=16)
            for j in range(2):
                dob16_ref[0, res, :, 128 * j:128 * (j + 1)] = scr[j, rows, :].astype(BF16)
            lse16_ref[0, res] = scr[2, rows, :]
            dlb16_ref[0, res] = scr[3, rows, :]

    tok = lambda w: pl.BlockSpec((tm, w), lambda i: (i, 0))
    p4 = lambda w: pl.BlockSpec((1, 4, tm // 4, w), lambda i: (i // spt, 0, i % spt, 0))
    p16 = lambda w: pl.BlockSpec((1, 16, tm // 16, w), lambda i: (i // spt, 0, i % spt, 0))
    s4 = lambda w, dt: _sds((B_LOC, 4, SEQ // 4, w), dt)
    s16 = lambda w, dt: _sds((B_LOC, 16, SEQ // 16, w), dt)
    row = _full((1, D_MODEL))
    return pl.pallas_call(
        body, name="middle", grid=(T // tm,),
        in_specs=[tok(W_A), tok(W_B), tok(128), p4(W_B), p4(128), p16(W_B), p16(128), tok(W_C), tok(D_MIX),
                  tok(D_MODEL), tok(D_MODEL), row, row, row, _full((D_MIX, D_MODEL)),
                  _full((128, W_B)), _full((W_B, 128)), _full((W_A, 128))],
        out_specs=(tok(D_MODEL), tok(D_MIX), tok(W_A), tok(128),
                   tok(W_B), tok(128), tok(128), p4(W_B), p4(128), p4(128), p16(W_B), p16(128), p16(128),
                   tok(W_C), tok(128), _full((8, D_MODEL)), _full((D_MIX, D_MODEL))),
        out_shape=(_sds((T, D_MODEL), BF16), _sds((T, D_MIX), BF16),
                   _sds((T, W_A), BF16), _sds((T, 128), F32),
                   _sds((T, W_B), BF16), _sds((T, 128), F32), _sds((T, 128), F32),
                   s4(W_B, BF16), s4(128, F32), s4(128, F32), s16(W_B, BF16), s16(128, F32), s16(128, F32),
                   _sds((T, W_C), BF16), _sds((T, 128), F32), _sds((8, D_MODEL), F32),
                   _sds((D_MIX, D_MODEL), F32)),
        scratch_shapes=[pltpu.VMEM((6, tm, 128), F32)],
        compiler_params=_cp(("arbitrary",), vmem_mb=56),
    )(*_pin(oa, o1, l1, o4, l4, o16, l16, oc, z, x, tgt, g_br, ln_g, ln_b, wout, spread4, gather4, gather8))


def _dh_dx(dqa, dka, dva, dqn, dkn, dvn, dq4, dk4, dv4, dq16, dk16, dv16, dqc, dz, du, xb, cos, sa, sb, winT):
    tm = 512
    spt = SEQ // tm

    def body(dqa_ref, dka_ref, dva_ref, dqn_ref, dkn_ref, dvn_ref, dq4_ref, dk4_ref, dv4_ref,
             dq16_ref, dk16_ref, dv16_ref, dqc_ref, dz_ref, du_ref, xb_ref, cos_ref, sa_ref, sb_ref, w_ref,
             gx_ref, db_ref, gin_ref, dh_ref, scr):
        i = pl.program_id(0)

        @pl.when(i == 0)
        def _():
            db_ref[...] = jnp.zeros_like(db_ref)
            gin_ref[...] = jnp.zeros_like(gin_ref)

        cos_t, sa_t, sb_t = cos_ref[...], sa_ref[...], sb_ref[...]

        def rope_t(t):
            return _rope(t, cos_t, sa_t, sb_t, -1)

        def put(r0, val):
            n = val.shape[1]
            dh_ref[:, r0:r0 + n] = val.astype(BF16)
            db_ref[:, r0:r0 + n] += jnp.sum(val, axis=0, keepdims=True)

        put(O_QA, rope_t(dqa_ref[...].astype(F32)) * QK_SCALE)
        put(O_KA, rope_t(dka_ref[...]))
        put(O_VA, dva_ref[...])
        put(O_QC, dqc_ref[...].astype(F32) * QK_SCALE)
        put(O_Z, dz_ref[...].astype(F32))
        for k, (n_ref, r4, r16) in enumerate(((dqn_ref, dq4_ref, dq16_ref), (dkn_ref, dk4_ref, dk16_ref),
                                               (dvn_ref, dv4_ref, dv16_ref))):
            for j in range(2):
                sl = slice(128 * j, 128 * (j + 1))
                scr[2 * k + j] = n_ref[:, sl].astype(F32)
                for res in range(4):
                    scr[2 * k + j, pl.ds(res, tm // 4, stride=4), :] += r4[0, res, :, sl].astype(F32)
                for res in range(16):
                    scr[2 * k + j, pl.ds(res, tm // 16, stride=16), :] += r16[0, res, :, sl].astype(F32)
        cat = lambda a: jnp.concatenate([scr[a], scr[a + 1]], axis=1)
        put(O_QB, rope_t(cat(0)) * QK_SCALE)
        put(O_KB, rope_t(cat(2)))
        put(O_VB, cat(4))
        gx_ref[...] = _dot(dh_ref[...], w_ref[...], NN) + ALPHA * du_ref[...].astype(F32)
        gin_ref[...] += _dot(dh_ref[...], xb_ref[...], TN)

    tok = lambda w: pl.BlockSpec((tm, w), lambda i: (i, 0))
    tab = pl.BlockSpec((tm, 128), lambda i: (i % spt, 0))
    p4 = pl.BlockSpec((1, 4, tm // 4, W_B), lambda i: (i // spt, 0, i % spt, 0))
    p16 = pl.BlockSpec((1, 16, tm // 16, W_B), lambda i: (i // spt, 0, i % spt, 0))
    once = lambda shape: pl.BlockSpec(shape, lambda i: (0, 0), pipeline_mode=pl.Buffered(1))
    return pl.pallas_call(
        body, name="dh_dx", grid=(T // tm,),
        in_specs=[tok(W_A), tok(W_KV_A), tok(W_KV_A), tok(W_B), tok(W_B), tok(W_B), p4, p4, p4, p16, p16, p16,
                  tok(W_C), tok(D_MIX), tok(D_MODEL), tok(D_MODEL), tab, tab, tab, once((D_IN, D_MODEL))],
        out_specs=(tok(D_MODEL), _full((1, D_IN)), once((D_IN, D_MODEL))),
        out_shape=(_sds((T, D_MODEL), F32), _sds((1, D_IN), F32), _sds((D_IN, D_MODEL), F32)),
        scratch_shapes=[pltpu.VMEM((tm, D_IN), BF16), pltpu.VMEM((6, tm, 128), F32)],
        compiler_params=_cp(("arbitrary",), vmem_mb=56),
    )(*_pin(dqa, dka, dva, dqn, dkn, dvn, dq4, dk4, dv4, dq16, dk16, dv16, dqc, dz, du, xb, cos, sa, sb, winT))


def _tn_matmul(name, a, b, bm, bt):
    n, m_all = a.shape
    n_cols = b.shape[1]

    def body(a_ref, b_ref, o_ref):
        @pl.when(pl.program_id(1) == 0)
        def _():
            o_ref[...] = jnp.zeros_like(o_ref)

        o_ref[...] += _dot(a_ref[...].astype(BF16), b_ref[...].astype(BF16), TN)

    return pl.pallas_call(
        body, name=name, grid=(m_all // bm, n // bt),
        in_specs=[pl.BlockSpec((bt, bm), lambda m, t: (t, m)), pl.BlockSpec((bt, n_cols), lambda m, t: (t, 0))],
        out_specs=pl.BlockSpec((bm, n_cols), lambda m, t: (m, 0)),
        out_shape=_sds((m_all, n_cols), F32),
        compiler_params=_cp(("parallel", "arbitrary"), vmem_mb=48),
    )(*_pin(a, b))


def _reduce_grads(g_in, g_out, g_mem, acc, dbin, dsink):
    shard_rows = (SH_IN, SH_OUT, SH_MEM)
    widths = (D_MODEL, D_MODEL, 2 * W_C)

    def body(ga_ref, gb_ref, gc_ref, acc_ref, dbin_ref, dsink_ref,
             ra_ref, rb_ref, rc_ref, sv_ref,
             sib_a, sib_b, sib_c, stage_a, stage_b, stage_c, land_a, land_b, land_c, sv_mine, sv_all,
             s1_send, s1_recv, s2_send, s2_recv, s3_send, s3_recv, sv_send, sv_recv):
        x, y, c = lax.axis_index("x"), lax.axis_index("y"), lax.axis_index("c")
        me, sibling = (x, y, c), (x, y, 1 - c)
        my_chip = 2 * x + y
        chips = [(1 - x, y), (x, 1 - y), (1 - x, 1 - y)]
        grads = (ga_ref, gb_ref, gc_ref)
        sibs = (sib_a, sib_b, sib_c)
        stages = (stage_a, stage_b, stage_c)
        lands = (land_a, land_b, land_c)
        res = (ra_ref, rb_ref, rc_ref)

        def half_rows(a, chip_idx, half):
            n = shard_rows[a]
            return pl.ds(pl.multiple_of(chip_idx * n + half * (n // 2), 16), n // 2)

        sv_mine[...] = jnp.zeros_like(sv_mine)
        sv_mine[0:4, 0:D_MODEL] = acc_ref[0:4, :]
        sv_mine[4:5, 0:D_IN] = dbin_ref[...]
        sv_mine[5:6, 0:128] = dsink_ref[...]
        my_dev = 4 * x + 2 * y + c
        others = [(x, y, 1 - c)] + [(*chip, cc) for chip in chips for cc in (c, 1 - c)]

        def sv_copy(j, to):
            return pltpu.make_async_remote_copy(
                src_ref=sv_mine, dst_ref=sv_all.at[my_dev], send_sem=sv_send.at[j], recv_sem=sv_recv.at[j],
                device_id=to, device_id_type=MESH)

        sv_sends = [sv_copy(j, to) for j, to in enumerate(others)]
        for cp in sv_sends:
            cp.start()

        def s1(a, k):
            return pltpu.make_async_remote_copy(
                src_ref=grads[a].at[half_rows(a, k, 1 - c), :], dst_ref=sibs[a].at[k],
                send_sem=s1_send.at[a, k], recv_sem=s1_recv.at[a, k], device_id=sibling, device_id_type=MESH)

        s1s = [s1(a, k) for a in range(3) for k in range(4)]
        for cp in s1s:
            cp.start()

        def s2(a, j, to):
            return pltpu.make_async_remote_copy(
                src_ref=stages[a].at[j], dst_ref=lands[a].at[j], send_sem=s2_send.at[a, j], recv_sem=s2_recv.at[a, j],
                device_id=to, device_id_type=MESH)

        s2s = []
        for j, chip in enumerate(chips):
            k = 2 * chip[0] + chip[1]
            for a in range(3):
                pltpu.make_async_remote_copy(
                    src_ref=grads[a].at[half_rows(a, k, c), :], dst_ref=sibs[a].at[k],
                    send_sem=s1_send.at[a, k], recv_sem=s1_recv.at[a, k], device_id=sibling,
                    device_id_type=MESH).wait_recv()
                stages[a][j] = (grads[a][half_rows(a, k, c), :] + sibs[a][k]).astype(BF16)
                cp = s2(a, j, (*chip, c))
                cp.start()
                s2s.append(cp)

        for a in range(3):
            pltpu.make_async_remote_copy(
                src_ref=grads[a].at[half_rows(a, my_chip, c), :], dst_ref=sibs[a].at[my_chip],
                send_sem=s1_send.at[a, my_chip], recv_sem=s1_recv.at[a, my_chip], device_id=sibling,
                device_id_type=MESH).wait_recv()
        for a in range(3):
            n = shard_rows[a]
            tot = grads[a][half_rows(a, my_chip, c), :] + sibs[a][my_chip]
            for j in range(3):
                s2(a, j, me).wait_recv()
                tot = tot + lands[a][j].astype(F32)
            mine = pl.ds(pl.multiple_of(c * (n // 2), 16), n // 2)
            res[a][mine, :] = tot

        def s3(a, half, to):
            n = shard_rows[a]
            blk = res[a].at[pl.ds(pl.multiple_of(half * (n // 2), 16), n // 2), :]
            return pltpu.make_async_remote_copy(
                src_ref=blk, dst_ref=blk, send_sem=s3_send.at[a], recv_sem=s3_recv.at[a],
                device_id=to, device_id_type=MESH)

        s3s = [s3(a, c, sibling) for a in range(3)]
        for cp in s3s:
            cp.start()
        for a in range(3):
            s3(a, 1 - c, me).wait_recv()

        sv_all[my_dev] = sv_mine[...]
        for j in range(7):
            sv_copy(j, me).wait_recv()
        tot = sv_all[0]
        for d in range(1, 8):
            tot = tot + sv_all[d]
        sv_ref[...] = tot
        for cp in sv_sends + s1s + s2s + s3s:
            cp.wait_send()

    vm = pl.BlockSpec(memory_space=pltpu.VMEM)
    half = lambda a: (shard_rows[a] // 2, widths[a])
    scratch = ([pltpu.VMEM((4, *half(a)), F32) for a in range(3)]
               + [pltpu.VMEM((3, *half(a)), BF16) for a in range(3)]
               + [pltpu.VMEM((3, *half(a)), BF16) for a in range(3)]
               + [pltpu.VMEM((8, SV_W), F32), pltpu.VMEM((8, 8, SV_W), F32)]
               + [pltpu.SemaphoreType.DMA((3, 4))] * 2 + [pltpu.SemaphoreType.DMA((3, 3))] * 2
               + [pltpu.SemaphoreType.DMA((3,))] * 2 + [pltpu.SemaphoreType.DMA((7,))] * 2)
    return pl.pallas_call(
        body, name="reduce_grads",
        out_shape=(_vm_sds((SH_IN, D_MODEL), F32), _vm_sds((SH_OUT, D_MODEL), F32),
                   _vm_sds((SH_MEM, 2 * W_C), F32), _vm_sds((8, SV_W), F32)),
        in_specs=[vm] * 6, out_specs=(vm, vm, vm, vm), scratch_shapes=scratch,
        compiler_params=_cp(vmem_mb=56),
    )(g_in, g_out, g_mem, acc, dbin, dsink)


def _adamw(name, w, g, m, v, rows=None):
    shape = w.shape
    rows = shape[0] if rows is None else rows

    def body(w_ref, g_ref, m_ref, v_ref, d_ref, nm_ref, nv_ref):
        gv = g_ref[...]
        nm = ADAM_B1 * m_ref[...] + (1.0 - ADAM_B1) * gv
        nv = ADAM_B2 * v_ref[...] + (1.0 - ADAM_B2) * (gv * gv)
        m_hat = nm / (1.0 - ADAM_B1 ** ADAM_STEP)
        v_hat = nv / (1.0 - ADAM_B2 ** ADAM_STEP)
        d_ref[...] = -ADAM_LR * (m_hat / (jnp.sqrt(v_hat) + ADAM_EPS) + ADAM_WD * w_ref[...])
        nm_ref[...] = nm
        nv_ref[...] = nv

    spec = pl.BlockSpec((rows, shape[1]), lambda i: (i, 0))
    return pl.pallas_call(
        body, name=name, grid=(shape[0] // rows,), in_specs=[spec] * 4, out_specs=(spec,) * 3,
        out_shape=(_sds(shape, F32),) * 3, compiler_params=_cp(("parallel",)),
    )(*_pin(w, g, m, v))


def _adamw_t(name, w, g_t, m, v, rows):
    shape = w.shape

    def body(w_ref, g_ref, m_ref, v_ref, go_ref, d_ref, nm_ref, nv_ref):
        gv = g_ref[...].T
        go_ref[...] = gv
        nm = ADAM_B1 * m_ref[...] + (1.0 - ADAM_B1) * gv
        nv = ADAM_B2 * v_ref[...] + (1.0 - ADAM_B2) * (gv * gv)
        m_hat = nm / (1.0 - ADAM_B1 ** ADAM_STEP)
        v_hat = nv / (1.0 - ADAM_B2 ** ADAM_STEP)
        d_ref[...] = -ADAM_LR * (m_hat / (jnp.sqrt(v_hat) + ADAM_EPS) + ADAM_WD * w_ref[...])
        nm_ref[...] = nm
        nv_ref[...] = nv

    spec = pl.BlockSpec((rows, shape[1]), lambda i: (i, 0))
    spec_t = pl.BlockSpec((shape[1], rows), lambda i: (0, i))
    return pl.pallas_call(
        body, name=name, grid=(shape[0] // rows,), in_specs=[spec, spec_t, spec, spec], out_specs=(spec,) * 4,
        out_shape=(_sds(shape, F32),) * 4, compiler_params=_cp(("parallel",)),
    )(*_pin(w, g_t, m, v))


def _rope_tables():
    pos = jnp.arange(SEQ, dtype=F32)
    inv = ROPE_THETA ** (-jnp.arange(0, 64, 2, dtype=F32) / 64)
    ang = pos[:, None] * inv[None, :]
    ang = jnp.concatenate([ang, ang, ang, ang], axis=-1)
    low = (jnp.arange(128) % 64) < 32
    cos, sin = jnp.cos(ang), jnp.sin(ang)
    return cos, jnp.where(low, -sin, 0.0), jnp.where(low, 0.0, sin)


def _local_step(x2, mem2, tgt2, winT, wout, wmem, b_in, sinks, g_branch, ln_gain, ln_bias):
    cos, sa, sb = _rope_tables()
    sinkv = jnp.pad(sinks, ((0, 0), (0, 120)))
    head_of_lane = jnp.arange(512)[None, :] // 64
    gather8 = (head_of_lane.T == jnp.arange(128)[None, :]).astype(BF16)
    gather4 = gather8[:W_B]
    spread4 = gather4.T

    xb, qa, ka, va, bn, b4, b16, qc, z = _in_proj(x2, winT, b_in, cos, sa, sb)
    memb, mkv = _mem_kv(mem2, wmem)
    b4f, b16f = b4.reshape(T, 768), b16.reshape(T, 768)

    swa = dict(kind="band", nb=SEQ // BLK, max_dist=BLK - 1, gqa=True)
    dil = (dict(kind="band", nb=SEQ // BLK), dict(kind="band", nb=SEQ // 4 // BLK), dict(kind="band", nb=1))
    oa, lse_a = _attn_fwd("swa_fwd", qa, 0, W_A, ka, 0, va, 0, W_KV_A, sinks=sinks, **swa)
    o1, l1 = _attn_fwd("dil1_fwd", bn, 0, W_B, bn, 1, bn, 2, W_B, **dil[0])
    o4, l4 = _attn_fwd("dil4_fwd", b4f, 0, W_B, b4f, 1, b4f, 2, W_B, **dil[1])
    o16, l16 = _attn_fwd("dil16_fwd", b16f, 0, W_B, b16f, 1, b16f, 2, W_B, **dil[2])
    oc, lse_c = _attn_fwd("mem_fwd", qc, 0, W_C, mkv, 0, mkv, 1, W_C, kind="mem")

    s4 = lambda w: (B_LOC, 4, SEQ // 4, w)
    s16 = lambda w: (B_LOC, 16, SEQ // 16, w)
    (du, dz, doa, dla, dobn, lsen, dlbn, dob4, lse4, dlb4, dob16, lse16, dlb16, doc, dlc, acc, g_out) = _middle(
        oa, o1, l1, o4.reshape(s4(W_B)), l4.reshape(s4(128)), o16.reshape(s16(W_B)), l16.reshape(s16(128)), oc, z,
        x2, tgt2, g_branch, ln_gain, ln_bias, wout, spread4, gather4, gather8)

    flat = lambda a: a.reshape(T, a.shape[-1])
    dqa, dka, dva, dsink = _attn_bwd("swa_bwd", qa, 0, W_A, ka, 0, va, 0, W_KV_A, doa, lse_a, dla, sinkv=sinkv,
                                     **swa)
    dqn, dkn, dvn = _attn_bwd("dil1_bwd", bn, 0, W_B, bn, 1, bn, 2, W_B, dobn, lsen, dlbn, **dil[0])
    dq4, dk4, dv4 = _attn_bwd("dil4_bwd", b4f, 0, W_B, b4f, 1, b4f, 2, W_B, flat(dob4), flat(lse4), flat(dlb4),
                              **dil[1])
    dq16, dk16, dv16 = _attn_bwd("dil16_bwd", b16f, 0, W_B, b16f, 1, b16f, 2, W_B, flat(dob16), flat(lse16),
                                 flat(dlb16), **dil[2])
    dqc, dmkv = _attn_bwd("mem_bwd", qc, 0, W_C, mkv, 0, mkv, 1, W_C, doc, lse_c, dlc, kind="mem")

    r4 = lambda a: a.reshape(s4(W_B))
    r16 = lambda a: a.reshape(s16(W_B))
    gx, dbin, g_in = _dh_dx(dqa, dka, dva, dqn, dkn, dvn, r4(dq4), r4(dk4), r4(dv4), r16(dq16), r16(dk16),
                            r16(dv16), dqc, dz, du, xb, cos, sa, sb, winT)
    g_mem = _tn_matmul("dw_mem", memb, dmkv, D_MODEL, B_LOC * MEM_LEN)
    return gx, g_in, g_out, g_mem, acc, dbin, dsink


def kernel(x, mem, w_in, b_in, w_mem, attn_sinks, g_branch, w_out, ln_gain, ln_bias, loss_target, m_w_in, m_b_in, m_w_mem, m_attn_sinks, m_g_branch, m_w_out, m_ln_gain, m_ln_bias, v_w_in, v_b_in, v_w_mem, v_attn_sinks, v_g_branch, v_w_out, v_ln_gain, v_ln_bias):
    winT, wout, wmem = _gather_weights(w_in[0], w_out[0], w_mem[0])
    gx, g_in, g_out, g_mem, acc, dbin, dsink = _local_step(
        x.reshape(T, D_MODEL), mem.reshape(B_LOC * MEM_LEN, D_MODEL), loss_target.reshape(T, D_MODEL),
        winT, wout, wmem, b_in, attn_sinks, g_branch, ln_gain, ln_bias)
    r_in, r_out, r_mem, sv = _reduce_grads(g_in, g_out, g_mem, acc, dbin, dsink)

    loss = jnp.sum(sv[0, :D_MODEL])
    grads = {
        "b_in": sv[4:5, :D_IN], "w_mem": r_mem[None],
        "attn_sinks": -sv[5:6, 0:8], "g_branch": sv[1:2, :D_MODEL], "w_out": r_out[None],
        "ln_gain": sv[2:3, :D_MODEL], "ln_bias": sv[3:4, :D_MODEL],
    }
    weights = dict(w_in=w_in, b_in=b_in, w_mem=w_mem, attn_sinks=attn_sinks, g_branch=g_branch, w_out=w_out,
                   ln_gain=ln_gain, ln_bias=ln_bias)
    ms = dict(w_in=m_w_in, b_in=m_b_in, w_mem=m_w_mem, attn_sinks=m_attn_sinks, g_branch=m_g_branch, w_out=m_w_out,
              ln_gain=m_ln_gain, ln_bias=m_ln_bias)
    vs = dict(w_in=v_w_in, b_in=v_b_in, w_mem=v_w_mem, attn_sinks=v_attn_sinks, g_branch=v_g_branch, w_out=v_w_out,
              ln_gain=v_ln_gain, ln_bias=v_ln_bias)
    names = ["w_in", "b_in", "w_mem", "attn_sinks", "g_branch", "w_out", "ln_gain", "ln_bias"]
    deltas, new_m, new_v = [], [], []
    for n in names:
        shape = weights[n].shape
        two_d = lambda a: a.reshape(shape[-2], shape[-1])
        if n == "w_in":
            gw, d, nm, nv = _adamw_t("adamw_w_in", two_d(w_in), r_in, two_d(m_w_in), two_d(v_w_in), 256)
            grads[n] = gw
        else:
            d, nm, nv = _adamw("adamw_" + n, two_d(weights[n]), two_d(grads[n]), two_d(ms[n]), two_d(vs[n]))
        deltas.append(d.reshape(shape))
        new_m.append(nm.reshape(shape))
        new_v.append(nv.reshape(shape))
    return (loss, gx.reshape(B_LOC, SEQ, D_MODEL), *[grads[n].reshape(weights[n].shape) for n in names],
            *deltas, *new_m, *new_v)
```

```python
import functools

import jax
import jax.numpy as jnp
from jax import lax
from jax.experimental import pallas as pl
from jax.experimental.pallas import tpu as pltpu

F32, BF16 = jnp.float32, jnp.bfloat16

D_MODEL = 1024
SEQ = 2048
B_LOC = 2
T = B_LOC * SEQ
BLK = 128
MEM_LEN = 256
W_A, W_KV_A, W_B, W_C, D_MIX = 512, 128, 256, 256, 1024
D_IN = 2816
O_QA, O_KA, O_VA, O_QB, O_KB, O_VB, O_QC, O_Z = 0, 512, 640, 768, 1024, 1280, 1536, 1792
ROPE_THETA = 10000.0
LN_EPS = 1e-5
RMS_EPS = 1e-6
ALPHA = 2.0 ** 0.25
QK_SCALE = 0.125
N_CHIP = 4
SH_IN, SH_OUT, SH_MEM = D_IN // N_CHIP, D_MIX // N_CHIP, D_MODEL // N_CHIP
NEG = -1e30
ADAM_LR, ADAM_B1, ADAM_B2, ADAM_EPS, ADAM_WD, ADAM_STEP = 0.001, 0.9, 0.999, 1e-08, 0.01, 10
SV_W = 3072
MESH = pl.DeviceIdType.MESH

NN = ((1,), (0,))
NT = ((1,), (1,))
TN = ((0,), (0,))


def _dot(a, b, dims):
    return lax.dot_general(a, b, (dims, ((), ())), preferred_element_type=F32)


def _cp(sem=None, vmem_mb=None):
    kw = {}
    if sem is not None:
        kw["dimension_semantics"] = sem
    if vmem_mb is not None:
        kw["vmem_limit_bytes"] = vmem_mb * 1024 * 1024
    return pltpu.CompilerParams(**kw)


def _sds(shape, dtype):
    return pltpu.HBM(shape, dtype)


def _vm_sds(shape, dtype):
    return jax.ShapeDtypeStruct(shape, dtype)


def _pin(*args):
    return [pltpu.with_memory_space_constraint(a, pltpu.HBM) for a in args]


def _full(shape):
    n = len(shape)
    return pl.BlockSpec(shape, lambda *_: (0,) * n)


def _gather_weights(win_sh, wout_sh, wmem_sh):
    shard_rows = (SH_IN, SH_OUT, SH_MEM)

    def body(a_ref, b_ref, c_ref, oa_ref, ob_ref, oc_ref, ici_send, ici_recv, d2d_send, d2d_recv):
        x, y, c = lax.axis_index("x"), lax.axis_index("y"), lax.axis_index("c")
        sibling = (x, y, 1 - c)
        chips = [(1 - x, y), (x, 1 - y), (1 - x, 1 - y)]
        srcs = (a_ref, b_ref, c_ref)
        outs = (oa_ref, ob_ref, oc_ref)

        def rows(a, chip, half):
            n = shard_rows[a]
            start = pl.multiple_of((2 * chip[0] + chip[1]) * n + half * (n // 2), 16)
            return outs[a].at[pl.ds(start, n // 2), :]

        for a in range(3):
            n = shard_rows[a]
            start = pl.multiple_of((2 * x + y) * n, 16)
            outs[a][pl.ds(start, n), :] = srcs[a][...].astype(BF16)

        def ici(a, j, chip_of_block, to):
            blk = rows(a, chip_of_block, c)
            return pltpu.make_async_remote_copy(
                src_ref=blk, dst_ref=blk, send_sem=ici_send.at[a, j], recv_sem=ici_recv.at[a, j],
                device_id=to, device_id_type=MESH)

        def d2d(a, j, chip_of_block, half, to):
            blk = rows(a, chip_of_block, half)
            return pltpu.make_async_remote_copy(
                src_ref=blk, dst_ref=blk, send_sem=d2d_send.at[a, j], recv_sem=d2d_recv.at[a, j],
                device_id=to, device_id_type=MESH)

        first = [ici(a, j, (x, y), (*chip, c)) for a in range(3) for j, chip in enumerate(chips)]
        for cp in first:
            cp.start()
        passed = []
        for j, chip in enumerate(chips):
            for a in range(3):
                ici(a, j, chip, (x, y, c)).wait_recv()
                fw = d2d(a, j, chip, c, sibling)
                fw.start()
                passed.append(fw)
        for j, chip in enumerate(chips):
            for a in range(3):
                d2d(a, j, chip, 1 - c, (x, y, c)).wait_recv()
        for cp in first + passed:
            cp.wait_send()

    vm = pl.BlockSpec(memory_space=pltpu.VMEM)
    return pl.pallas_call(
        body, name="gather_weights",
        out_shape=(_vm_sds((D_IN, D_MODEL), BF16), _vm_sds((D_MIX, D_MODEL), BF16),
                   _vm_sds((D_MODEL, 2 * W_C), BF16)),
        in_specs=[vm, vm, vm], out_specs=(vm, vm, vm),
        scratch_shapes=[pltpu.SemaphoreType.DMA((3, 3))] * 4,
        compiler_params=_cp(vmem_mb=40),
    )(win_sh, wout_sh, wmem_sh)


def _rope(t, cos, sa, sb, sign):
    w = t.shape[1]
    reps = w // 128
    c, a, b = (jnp.tile(v, (1, reps)) if reps > 1 else v for v in (cos, sa, sb))
    rot = pltpu.roll(t, w - 32, 1) * a + pltpu.roll(t, 32, 1) * b
    return t * c + rot if sign > 0 else t * c - rot


def _in_proj(x, winT, b_in, cos, sa, sb):
    tm = 256
    spt = SEQ // tm

    def body(x_ref, w_ref, b_ref, cos_ref, sa_ref, sb_ref,
             xb_ref, qa_ref, ka_ref, va_ref, bn_ref, b4_ref, b16_ref, qc_ref, z_ref, scr):
        xb = x_ref[...].astype(BF16)
        xb_ref[...] = xb
        cos_t, sa_t, sb_t = cos_ref[...], sa_ref[...], sb_ref[...]

        def proj(r0, n):
            return _dot(xb, w_ref[r0:r0 + n, :], NT) + b_ref[:, r0:r0 + n]

        def rope(t):
            return _rope(t, cos_t, sa_t, sb_t, +1)

        qa_ref[...] = (rope(proj(O_QA, W_A)) * QK_SCALE).astype(BF16)
        ka_ref[...] = rope(proj(O_KA, W_KV_A)).astype(BF16)
        va_ref[...] = proj(O_VA, W_KV_A).astype(BF16)
        qc_ref[...] = (proj(O_QC, W_C) * QK_SCALE).astype(BF16)
        z_ref[...] = proj(O_Z, D_MIX).astype(BF16)
        parts = (rope(proj(O_QB, W_B)) * QK_SCALE, rope(proj(O_KB, W_B)), proj(O_VB, W_B))
        for k, part in enumerate(parts):
            bn_ref[:, 256 * k:256 * (k + 1)] = part.astype(BF16)
            scr[2 * k] = part[:, :128]
            scr[2 * k + 1] = part[:, 128:]
        for j in range(6):
            for res in range(4):
                b4_ref[0, res, :, 128 * j:128 * (j + 1)] = scr[j, pl.ds(res, tm // 4, stride=4), :].astype(BF16)
            for res in range(16):
                b16_ref[0, res, :, 128 * j:128 * (j + 1)] = scr[j, pl.ds(res, tm // 16, stride=16), :].astype(BF16)

    tok = lambda w: pl.BlockSpec((tm, w), lambda i: (i, 0))
    tab = pl.BlockSpec((tm, 128), lambda i: (i % spt, 0))
    return pl.pallas_call(
        body, name="in_proj", grid=(T // tm,),
        in_specs=[tok(D_MODEL), _full((D_IN, D_MODEL)), _full((1, D_IN)), tab, tab, tab],
        out_specs=(tok(D_MODEL), tok(W_A), tok(W_KV_A), tok(W_KV_A), tok(768),
                   pl.BlockSpec((1, 4, tm // 4, 768), lambda i: (i // spt, 0, i % spt, 0)),
                   pl.BlockSpec((1, 16, tm // 16, 768), lambda i: (i // spt, 0, i % spt, 0)),
                   tok(W_C), tok(D_MIX)),
        out_shape=(_sds((T, D_MODEL), BF16), _sds((T, W_A), BF16), _sds((T, W_KV_A), BF16), _sds((T, W_KV_A), BF16),
                   _sds((T, 768), BF16), _sds((B_LOC, 4, SEQ // 4, 768), BF16), _sds((B_LOC, 16, SEQ // 16, 768), BF16),
                   _sds((T, W_C), BF16), _sds((T, D_MIX), BF16)),
        scratch_shapes=[pltpu.VMEM((6, tm, 128), F32)],
        compiler_params=_cp(("parallel",), vmem_mb=48),
    )(*_pin(x, winT, b_in, cos, sa, sb))


def _mem_kv(mem, wmem):
    def body(m_ref, w_ref, mb_ref, kv_ref):
        mb = m_ref[...].astype(BF16)
        mb_ref[...] = mb
        kv_ref[...] = _dot(mb, w_ref[...], NN).astype(BF16)

    n = B_LOC * MEM_LEN
    return pl.pallas_call(
        body, name="mem_kv",
        out_shape=(_sds((n, D_MODEL), BF16), _sds((n, 2 * W_C), BF16)),
    )(*_pin(mem, wmem))


QB = 4
QR = QB * BLK


def _lane_lo():
    return lax.broadcasted_iota(jnp.int32, (1, 128), 1) < 64


def _dup_head(k2, hk, lo):
    kf = k2.astype(F32)
    r = pltpu.roll(kf, 64, 1)
    return (jnp.where(lo, kf, r) if hk == 0 else jnp.where(lo, r, kf)).astype(BF16)


def _stack_heads(pairs, lo):
    parts = []
    for x2 in pairs:
        z = jnp.zeros_like(x2)
        parts += [jnp.where(lo, x2, z), jnp.where(lo, z, x2)]
    return jnp.concatenate(parts, axis=0)


def _prev_mode(kind, nb, j):
    if kind == "mem" or nb == 1:
        return "no"
    if nb <= QB:
        return "yes" if j % nb else "no"
    return "yes" if j else "dyn"


class _Attn:
    def __init__(self, kind, nb, max_dist, gqa, qw, kvw, qcb, kcb, vcb):
        self.kind, self.nb, self.gqa, self.qw, self.kvw = kind, nb, gqa, qw, kvw
        npairs = qw // 128
        self.groups = ([(hk, [2 * hk, 2 * hk + 1]) for hk in range(npairs // 2)] if gqa
                       else [(p, [p]) for p in range(npairs)])
        self.nh = 2 * len(self.groups[0][1])
        self.cols = 128 * self.nh
        self.reach = BLK - max_dist
        self.ext_prev = kind == "band" and nb > QB
        self.q_spec = pl.BlockSpec((QR, qw), lambda g: (g, qcb))
        self.row_spec = pl.BlockSpec((QR, qw), lambda g: (g, 0))
        self.stat_spec = pl.BlockSpec((QR, 128), lambda g: (g, 0))
        if kind == "mem":
            per = SEQ // QR
            self.kv_specs = [pl.BlockSpec((MEM_LEN, kvw), lambda g: (g // per, kcb)),
                             pl.BlockSpec((MEM_LEN, kvw), lambda g: (g // per, vcb))]
        else:
            self.kv_specs = [pl.BlockSpec((QR, kvw), lambda g: (g, kcb)), pl.BlockSpec((QR, kvw), lambda g: (g, vcb))]
            if self.ext_prev:
                self.kv_specs += [pl.BlockSpec((BLK, kvw), lambda g: (jnp.maximum(g * QB - 1, 0), kcb)),
                                  pl.BlockSpec((BLK, kvw), lambda g: (jnp.maximum(g * QB - 1, 0), vcb))]

    def masks(self):
        if self.kind == "mem":
            return None
        kj = lax.broadcasted_iota(jnp.int32, (2 * BLK, self.cols), 0)
        qi = lax.broadcasted_iota(jnp.int32, (2 * BLK, self.cols), 1) & (BLK - 1)
        kj1 = lax.broadcasted_iota(jnp.int32, (BLK, self.cols), 0)
        qi1 = lax.broadcasted_iota(jnp.int32, (BLK, self.cols), 1) & (BLK - 1)
        return kj, qi, kj1 <= qi1

    def keys(self, j, gi, kc_ref, vc_ref, kp_ref, vp_ref, lo, kq, g):
        def kv(k_ref, v_ref, r):
            if self.gqa:
                return _dup_head(k_ref[r, :], gi, lo), _dup_head(v_ref[r, :], gi, lo)
            sl = slice(128 * gi, 128 * (gi + 1))
            return k_ref[r, sl], v_ref[r, sl]

        if self.kind == "mem":
            key0 = pl.multiple_of((g // (SEQ // QR)) * MEM_LEN, MEM_LEN)
            return (*kv(kc_ref, vc_ref, slice(None)), None, [(0, MEM_LEN, key0)])
        kj, qi, cur = kq
        row0 = g * QR + BLK * j
        mode = _prev_mode(self.kind, self.nb, j)
        if mode == "no":
            return (*kv(kc_ref, vc_ref, slice(BLK * j, BLK * (j + 1))), cur, [(0, BLK, pl.multiple_of(row0, BLK))])
        if mode == "yes":
            mask = jnp.logical_and(kj >= qi + self.reach, kj <= qi + BLK)
            return (*kv(kc_ref, vc_ref, slice(BLK * (j - 1), BLK * (j + 1))), mask,
                    [(0, 2 * BLK, pl.multiple_of(row0 - BLK, BLK))])
        has_prev = ((g * QB) % self.nb) > 0
        hp = has_prev.astype(jnp.int32)
        mask = jnp.logical_and(kj >= qi * hp + (self.reach * hp + BLK * (1 - hp)), kj <= qi + BLK)
        kp, vp = kv(kp_ref, vp_ref, slice(None))
        kc, vc = kv(kc_ref, vc_ref, slice(0, BLK))
        return (jnp.concatenate([kp, kc], axis=0), jnp.concatenate([vp, vc], axis=0), mask,
                [(0, BLK, pl.multiple_of(jnp.maximum(row0 - BLK, 0), BLK)), (BLK, BLK, pl.multiple_of(row0, BLK))])


def _attn_fwd(name, q, qcb, qw, k, kcb, v, vcb, kvw, *, kind, nb=1, max_dist=BLK, gqa=False, sinks=None):
    a = _Attn(kind, nb, max_dist, gqa, qw, kvw, qcb, kcb, vcb)

    def body(*refs):
        it = iter(refs)
        q_ref, kc_ref, vc_ref = next(it), next(it), next(it)
        kp_ref, vp_ref = (next(it), next(it)) if a.ext_prev else (None, None)
        sink_ref = next(it) if sinks is not None else None
        o_ref, lse_ref = next(it), next(it)
        g = pl.program_id(0)
        lo = _lane_lo()
        top = lax.broadcasted_iota(jnp.int32, (128, 1), 0) < 64
        rid = lax.broadcasted_iota(jnp.int32, (8, 128), 0)
        kq = a.masks()
        for j in range(QB):
            rows = slice(BLK * j, BLK * (j + 1))
            stat = jnp.zeros((8, 128), F32)
            for gi, pairs in a.groups:
                qs = _stack_heads([q_ref[rows, 128 * p:128 * (p + 1)] for p in pairs], lo)
                kk, vv, mask, _ = a.keys(j, gi, kc_ref, vc_ref, kp_ref, vp_ref, lo, kq, g)
                pieces = [slice(r0, r0 + BLK) for r0 in range(0, kk.shape[0], BLK)]
                ss = []
                for r in pieces:
                    s = _dot(kk[r], qs, NT)
                    ss.append(s if mask is None else jnp.where(mask[r], s, NEG))
                m = jnp.max(ss[0], axis=0, keepdims=True)
                for s in ss[1:]:
                    m = jnp.maximum(m, jnp.max(s, axis=0, keepdims=True))
                if sink_ref is not None:
                    sk = jnp.concatenate([jnp.full((1, 128), sink_ref[0, a.nh * gi + i], F32) for i in range(a.nh)],
                                         axis=1)
                    m = jnp.maximum(m, sk)
                l, ot = None, None
                for r, s in zip(pieces, ss):
                    p = jnp.exp(s - m)
                    ps = jnp.sum(p, axis=0, keepdims=True)
                    c = _dot(vv[r], p.astype(BF16), TN)
                    l, ot = (ps, c) if l is None else (l + ps, ot + c)
                if sink_ref is not None:
                    l = l + jnp.exp(sk - m)
                ot = ot * pl.reciprocal(l, approx=True)
                lse = m + jnp.log(l)
                for i, p in enumerate(pairs):
                    o2t = jnp.where(top, ot[:, 256 * i:256 * i + 128], ot[:, 256 * i + 128:256 * i + 256])
                    o_ref[rows, 128 * p:128 * (p + 1)] = o2t.T.astype(BF16)
                for i in range(a.nh):
                    stat = jnp.where(rid == a.nh * gi + i, lse[:, 128 * i:128 * (i + 1)], stat)
            lse_ref[rows, :] = jnp.concatenate([stat, jnp.zeros((120, 128), F32)], axis=0).T

    args = [q, k, v] + ([k, v] if a.ext_prev else [])
    in_specs = [a.q_spec] + a.kv_specs
    if sinks is not None:
        args.append(sinks)
        in_specs.append(pl.BlockSpec(memory_space=pltpu.SMEM))
    return pl.pallas_call(
        body, name=name, grid=(T // QR,), in_specs=in_specs, out_specs=(a.row_spec, a.stat_spec),
        out_shape=(_sds((T, qw), BF16), _sds((T, 128), F32)),
        compiler_params=_cp(("parallel",), vmem_mb=40),
    )(*_pin(*args))


def _attn_bwd(name, q, qcb, qw, k, kcb, v, vcb, kvw, do, lse, dl, *, kind, nb=1, max_dist=BLK, gqa=False,
              sinkv=None):
    a = _Attn(kind, nb, max_dist, gqa, qw, kvw, qcb, kcb, vcb)

    def body(*refs):
        it = iter(refs)
        q_ref, kc_ref, vc_ref = next(it), next(it), next(it)
        kp_ref, vp_ref = (next(it), next(it)) if a.ext_prev else (None, None)
        do_ref, lse_ref, dl_ref = next(it), next(it), next(it)
        sinkv_ref = next(it) if sinkv is not None else None
        dq_ref = next(it)
        if kind == "mem":
            dkv_ref = next(it)
        else:
            dk_ref, dv_ref = next(it), next(it)
        dsink_ref = next(it) if sinkv is not None else None
        g = pl.program_id(0)
        lo = _lane_lo()
        top = lax.broadcasted_iota(jnp.int32, (128, 1), 0) < 64

        @pl.when(g == 0)
        def _():
            if kind == "mem":
                dkv_ref[...] = jnp.zeros_like(dkv_ref)
            else:
                dk_ref[...] = jnp.zeros_like(dk_ref)
                dv_ref[...] = jnp.zeros_like(dv_ref)
            if dsink_ref is not None:
                dsink_ref[...] = jnp.zeros_like(dsink_ref)

        kq = a.masks()
        for j in range(QB):
            rows = slice(BLK * j, BLK * (j + 1))
            lse_t = lse_ref[rows, :].T
            dl_t = dl_ref[rows, :].T
            for gi, pairs in a.groups:
                heads = [a.nh * gi + i for i in range(a.nh)]
                qs = _stack_heads([q_ref[rows, 128 * p:128 * (p + 1)] for p in pairs], lo)
                dos = _stack_heads([do_ref[rows, 128 * p:128 * (p + 1)] for p in pairs], lo)
                lse_row = jnp.concatenate([lse_t[h:h + 1, :] for h in heads], axis=1)
                dl_row = jnp.concatenate([dl_t[h:h + 1, :] for h in heads], axis=1)
                kk, vv, mask, dests = a.keys(j, gi, kc_ref, vc_ref, kp_ref, vp_ref, lo, kq, g)
                s = _dot(kk, qs, NT)
                if mask is not None:
                    s = jnp.where(mask, s, NEG)
                p = jnp.exp(s - lse_row)
                ds = (p * (_dot(vv, dos, NT) - dl_row)).astype(BF16)
                dqt = _dot(kk, ds, TN)
                ck = _dot(ds, qs, NN)
                cv = _dot(p.astype(BF16), dos, NN)
                if gqa:
                    sel = lo if gi == 0 else jnp.logical_not(lo)
                    ck = jnp.where(sel, ck + pltpu.roll(ck, 64, 1), 0.0)
                    cv = jnp.where(sel, cv + pltpu.roll(cv, 64, 1), 0.0)
                    kcols = slice(0, 128)
                else:
                    kcols = slice(128 * gi, 128 * (gi + 1))
                for r0, nr, key0 in dests:
                    krows = pl.ds(key0, nr)
                    if kind == "mem":
                        dkv_ref[krows, kcols] += ck[r0:r0 + nr]
                        dkv_ref[krows, slice(kvw + kcols.start, kvw + kcols.stop)] += cv[r0:r0 + nr]
                    else:
                        dk_ref[krows, kcols] += ck[r0:r0 + nr]
                        dv_ref[krows, kcols] += cv[r0:r0 + nr]
                for i, p in enumerate(pairs):
                    dq2t = jnp.where(top, dqt[:, 256 * i:256 * i + 128], dqt[:, 256 * i + 128:256 * i + 256])
                    dq_ref[rows, 128 * p:128 * (p + 1)] = dq2t.T.astype(BF16)
        if dsink_ref is not None:
            ps = jnp.exp(sinkv_ref[...] - lse_ref[...]) * dl_ref[...]
            dsink_ref[...] += jnp.sum(ps, axis=0, keepdims=True)

    args = [q, k, v] + ([k, v] if a.ext_prev else []) + [do, lse, dl]
    in_specs = [a.q_spec] + a.kv_specs + [a.row_spec, a.stat_spec, a.stat_spec]
    if sinkv is not None:
        args.append(sinkv)
        in_specs.append(_full((1, 128)))
    out_shape = [_sds((T, qw), BF16)]
    out_specs = [a.row_spec]
    if kind == "mem":
        out_shape.append(_sds((B_LOC * MEM_LEN, 2 * kvw), F32))
        out_specs.append(_full((B_LOC * MEM_LEN, 2 * kvw)))
    else:
        out_shape += [_sds((T, kvw), F32)] * 2
        out_specs += [_full((T, kvw))] * 2
    if sinkv is not None:
        out_shape.append(_sds((1, 128), F32))
        out_specs.append(_full((1, 128)))
    return pl.pallas_call(
        body, name=name, grid=(T // QR,), in_specs=in_specs, out_specs=tuple(out_specs),
        out_shape=tuple(out_shape), compiler_params=_cp(("arbitrary",), vmem_mb=48),
    )(*_pin(*args))


def _dot2(v, w_ref):
    hi = v.astype(BF16)
    lo = (v - hi.astype(F32)).astype(BF16)
    return _dot(hi, w_ref[...], NN) + _dot(lo, w_ref[...], NN)


def _middle(oa, o1, l1, o4, l4, o16, l16, oc, z, x, tgt, g_br, ln_g, ln_b, wout, spread4, gather4, gather8):
    tm = 256
    spt = SEQ // tm

    def body(oa_ref, o1_ref, l1_ref, o4_ref, l4_ref, o16_ref, l16_ref, oc_ref, z_ref, x_ref, t_ref,
             g_ref, lg_ref, lb_ref, w_ref, sp4_ref, ga4_ref, ga8_ref,
             du_ref, dz_ref, doa_ref, dla_ref,
             dobn_ref, lsen_ref, dlbn_ref, dob4_ref, lse4_ref, dlb4_ref, dob16_ref, lse16_ref, dlb16_ref,
             doc_ref, dlc_ref, acc_ref, gout_ref, scr):
        i = pl.program_id(0)

        @pl.when(i == 0)
        def _():
            acc_ref[...] = jnp.zeros_like(acc_ref)
            gout_ref[...] = jnp.zeros_like(gout_ref)

        for res in range(4):
            rows = pl.ds(res, tm // 4, stride=4)
            for j in range(2):
                scr[j, rows, :] = o4_ref[0, res, :, 128 * j:128 * (j + 1)].astype(F32)
            scr[2, rows, :] = l4_ref[0, res]
        for res in range(16):
            rows = pl.ds(res, tm // 16, stride=16)
            for j in range(2):
                scr[3 + j, rows, :] = o16_ref[0, res, :, 128 * j:128 * (j + 1)].astype(F32)
            scr[5, rows, :] = l16_ref[0, res]
        cat = lambda a: jnp.concatenate([scr[a], scr[a + 1]], axis=1)
        o1v, o4v, o16v = o1_ref[...].astype(F32), cat(0), cat(3)
        l1v, l4v, l16v = l1_ref[...], scr[2], scr[5]
        mx = jnp.maximum(jnp.maximum(l1v, l4v), l16v)
        e1, e4, e16 = jnp.exp(l1v - mx), jnp.exp(l4v - mx), jnp.exp(l16v - mx)
        ssum = e1 + e4 + e16
        lse_b = mx + jnp.log(ssum)
        inv = 1.0 / ssum
        ob = (_dot2(e1 * inv, sp4_ref) * o1v + _dot2(e4 * inv, sp4_ref) * o4v + _dot2(e16 * inv, sp4_ref) * o16v)
        oav, ocv = oa_ref[...].astype(F32), oc_ref[...].astype(F32)

        def rms(o):
            r = lax.rsqrt(jnp.sum(o * o, axis=1, keepdims=True) * (1.0 / o.shape[1]) + RMS_EPS)
            return o * r, r

        na, ra = rms(oav)
        nb_, rb = rms(ob)
        nc, rc = rms(ocv)
        n = jnp.concatenate([na, nb_, nc], axis=1)
        zf = z_ref[...].astype(F32)
        sig = 1.0 / (1.0 + jnp.exp(-zf))
        sz = zf * sig
        gb = g_ref[...]
        yb = (n * gb * sz).astype(BF16)
        u = ALPHA * x_ref[...] + _dot(yb, w_ref[...], NN)
        inv_d = 1.0 / D_MODEL
        mu = jnp.sum(u, axis=1, keepdims=True) * inv_d
        uc = u - mu
        rstd = lax.rsqrt(jnp.sum(uc * uc, axis=1, keepdims=True) * inv_d + LN_EPS)
        xh = uc * rstd
        lg = lg_ref[...]
        diff = xh * lg + lb_ref[...] - t_ref[...]
        acc_ref[0:1, :] += jnp.sum(diff * diff, axis=0, keepdims=True) * (0.5 * inv_d)
        dout = diff * inv_d
        acc_ref[2:3, :] += jnp.sum(dout * xh, axis=0, keepdims=True)
        acc_ref[3:4, :] += jnp.sum(dout, axis=0, keepdims=True)
        dxh = dout * lg
        du = rstd * (dxh - jnp.sum(dxh, axis=1, keepdims=True) * inv_d
                     - xh * (jnp.sum(dxh * xh, axis=1, keepdims=True) * inv_d))
        dub = du.astype(BF16)
        du_ref[...] = dub
        gout_ref[...] += _dot(yb, dub, TN)
        dy = _dot(dub, w_ref[...], NT)
        t1 = dy * sz
        acc_ref[1:2, :] += jnp.sum(t1 * n, axis=0, keepdims=True)
        dn = t1 * gb
        dz_ref[...] = (dy * n * gb * (sig * (1.0 + zf * (1.0 - sig)))).astype(BF16)

        def rms_bwd(dn_, n_, r):
            return r * (dn_ - n_ * (jnp.sum(dn_ * n_, axis=1, keepdims=True) * (1.0 / n_.shape[1])))

        doa = rms_bwd(dn[:, :W_A], na, ra)
        dob = rms_bwd(dn[:, W_A:W_A + W_B], nb_, rb)
        doc = rms_bwd(dn[:, W_A + W_B:], nc, rc)
        doa_ref[...] = doa.astype(BF16)
        dla_ref[...] = _dot2(doa * oav, ga8_ref)
        doc_ref[...] = doc.astype(BF16)
        dlc_ref[...] = _dot2(doc * ocv, ga4_ref)
        dlb = _dot2(dob * ob, ga4_ref)
        dobn_ref[...] = dob.astype(BF16)
        lsen_ref[...] = lse_b
        dlbn_ref[...] = dlb
        scr[0] = dob[:, :128]
        scr[1] = dob[:, 128:]
        scr[2] = lse_b
        scr[3] = dlb
        for res in range(4):
            rows = pl.ds(res, tm // 4, stride=4)
            for j in range(2):
                dob4_ref[0, res, :, 128 * j:128 * (j + 1)] = scr[j, rows, :].astype(BF16)
            lse4_ref[0, res] = scr[2, rows, :]
            dlb4_ref[0, res] = scr[3, rows, :]
        for res in range(16):
            rows = pl.ds(res, tm // 16, stride=16)
            for j in range(2):
                dob16_ref[0, res, :, 128 * j:128 * (j + 1)] = scr[j, rows, :].astype(BF16)
            lse16_ref[0, res] = scr[2, rows, :]
            dlb16_ref[0, res] = scr[3, rows, :]

    tok = lambda w: pl.BlockSpec((tm, w), lambda i: (i, 0))
    p4 = lambda w: pl.BlockSpec((1, 4, tm // 4, w), lambda i: (i // spt, 0, i % spt, 0))
    p16 = lambda w: pl.BlockSpec((1, 16, tm // 16, w), lambda i: (i // spt, 0, i % spt, 0))
    s4 = lambda w, dt: _sds((B_LOC, 4, SEQ // 4, w), dt)
    s16 = lambda w, dt: _sds((B_LOC, 16, SEQ // 16, w), dt)
    row = _full((1, D_MODEL))
    return pl.pallas_call(
        body, name="middle", grid=(T // tm,),
        in_specs=[tok(W_A), tok(W_B), tok(128), p4(W_B), p4(128), p16(W_B), p16(128), tok(W_C), tok(D_MIX),
                  tok(D_MODEL), tok(D_MODEL), row, row, row, _full((D_MIX, D_MODEL)),
                  _full((128, W_B)), _full((W_B, 128)), _full((W_A, 128))],
        out_specs=(tok(D_MODEL), tok(D_MIX), tok(W_A), tok(128),
                   tok(W_B), tok(128), tok(128), p4(W_B), p4(128), p4(128), p16(W_B), p16(128), p16(128),
                   tok(W_C), tok(128), _full((8, D_MODEL)), _full((D_MIX, D_MODEL))),
        out_shape=(_sds((T, D_MODEL), BF16), _sds((T, D_MIX), BF16),
                   _sds((T, W_A), BF16), _sds((T, 128), F32),
                   _sds((T, W_B), BF16), _sds((T, 128), F32), _sds((T, 128), F32),
                   s4(W_B, BF16), s4(128, F32), s4(128, F32), s16(W_B, BF16), s16(128, F32), s16(128, F32),
                   _sds((T, W_C), BF16), _sds((T, 128), F32), _sds((8, D_MODEL), F32),
                   _sds((D_MIX, D_MODEL), F32)),
        scratch_shapes=[pltpu.VMEM((6, tm, 128), F32)],
        compiler_params=_cp(("arbitrary",), vmem_mb=56),
    )(*_pin(oa, o1, l1, o4, l4, o16, l16, oc, z, x, tgt, g_br, ln_g, ln_b, wout, spread4, gather4, gather8))


def _dh_dx(dqa, dka, dva, dqn, dkn, dvn, dq4, dk4, dv4, dq16, dk16, dv16, dqc, dz, du, xb, cos, sa, sb, winT):
    tm = 512
    spt = SEQ // tm

    def body(dqa_ref, dka_ref, dva_ref, dqn_ref, dkn_ref, dvn_ref, dq4_ref, dk4_ref, dv4_ref,
             dq16_ref, dk16_ref, dv16_ref, dqc_ref, dz_ref, du_ref, xb_ref, cos_ref, sa_ref, sb_ref, w_ref,
             gx_ref, db_ref, gin_ref, dh_ref, scr):
        i = pl.program_id(0)

        @pl.when(i == 0)
        def _():
            db_ref[...] = jnp.zeros_like(db_ref)
            gin_ref[...] = jnp.zeros_like(gin_ref)

        cos_t, sa_t, sb_t = cos_ref[...], sa_ref[...], sb_ref[...]

        def rope_t(t):
            return _rope(t, cos_t, sa_t, sb_t, -1)

        def put(r0, val):
            n = val.shape[1]
            dh_ref[:, r0:r0 + n] = val.astype(BF16)
            db_ref[:, r0:r0 + n] += jnp.sum(val, axis=0, keepdims=True)

        put(O_QA, rope_t(dqa_ref[...].astype(F32)) * QK_SCALE)
        put(O_KA, rope_t(dka_ref[...]))
        put(O_VA, dva_ref[...])
        put(O_QC, dqc_ref[...].astype(F32) * QK_SCALE)
        put(O_Z, dz_ref[...].astype(F32))
        for k, (n_ref, r4, r16) in enumerate(((dqn_ref, dq4_ref, dq16_ref), (dkn_ref, dk4_ref, dk16_ref),
                                               (dvn_ref, dv4_ref, dv16_ref))):
            for j in range(2):
                sl = slice(128 * j, 128 * (j + 1))
                scr[2 * k + j] = n_ref[:, sl].astype(F32)
                for res in range(4):
                    scr[2 * k + j, pl.ds(res, tm // 4, stride=4), :] += r4[0, res, :, sl].astype(F32)
                for res in range(16):
                    scr[2 * k + j, pl.ds(res, tm // 16, stride=16), :] += r16[0, res, :, sl].astype(F32)
        cat = lambda a: jnp.concatenate([scr[a], scr[a + 1]], axis=1)
        put(O_QB, rope_t(cat(0)) * QK_SCALE)
        put(O_KB, rope_t(cat(2)))
        put(O_VB, cat(4))
        gx_ref[...] = _dot(dh_ref[...], w_ref[...], NN) + ALPHA * du_ref[...].astype(F32)
        gin_ref[...] += _dot(dh_ref[...], xb_ref[...], TN)

    tok = lambda w: pl.BlockSpec((tm, w), lambda i: (i, 0))
    tab = pl.BlockSpec((tm, 128), lambda i: (i % spt, 0))
    p4 = pl.BlockSpec((1, 4, tm // 4, W_B), lambda i: (i // spt, 0, i % spt, 0))
    p16 = pl.BlockSpec((1, 16, tm // 16, W_B), lambda i: (i // spt, 0, i % spt, 0))
    once = lambda shape: pl.BlockSpec(shape, lambda i: (0, 0), pipeline_mode=pl.Buffered(1))
    return pl.pallas_call(
        body, name="dh_dx", grid=(T // tm,),
        in_specs=[tok(W_A), tok(W_KV_A), tok(W_KV_A), tok(W_B), tok(W_B), tok(W_B), p4, p4, p4, p16, p16, p16,
                  tok(W_C), tok(D_MIX), tok(D_MODEL), tok(D_MODEL), tab, tab, tab, once((D_IN, D_MODEL))],
        out_specs=(tok(D_MODEL), _full((1, D_IN)), once((D_IN, D_MODEL))),
        out_shape=(_sds((T, D_MODEL), F32), _sds((1, D_IN), F32), _sds((D_IN, D_MODEL), F32)),
        scratch_shapes=[pltpu.VMEM((tm, D_IN), BF16), pltpu.VMEM((6, tm, 128), F32)],
        compiler_params=_cp(("arbitrary",), vmem_mb=56),
    )(*_pin(dqa, dka, dva, dqn, dkn, dvn, dq4, dk4, dv4, dq16, dk16, dv16, dqc, dz, du, xb, cos, sa, sb, winT))


def _tn_matmul(name, a, b, bm, bt):
    n, m_all = a.shape
    n_cols = b.shape[1]

    def body(a_ref, b_ref, o_ref):
        @pl.when(pl.program_id(1) == 0)
        def _():
            o_ref[...] = jnp.zeros_like(o_ref)

        o_ref[...] += _dot(a_ref[...].astype(BF16), b_ref[...].astype(BF16), TN)

    return pl.pallas_call(
        body, name=name, grid=(m_all // bm, n // bt),
        in_specs=[pl.BlockSpec((bt, bm), lambda m, t: (t, m)), pl.BlockSpec((bt, n_cols), lambda m, t: (t, 0))],
        out_specs=pl.BlockSpec((bm, n_cols), lambda m, t: (m, 0)),
        out_shape=_sds((m_all, n_cols), F32),
        compiler_params=_cp(("parallel", "arbitrary"), vmem_mb=48),
    )(*_pin(a, b))


def _reduce_grads(g_in, g_out, g_mem, acc, dbin, dsink):
    shard_rows = (SH_IN, SH_OUT, SH_MEM)
    widths = (D_MODEL, D_MODEL, 2 * W_C)

    def body(ga_ref, gb_ref, gc_ref, acc_ref, dbin_ref, dsink_ref,
             ra_ref, rb_ref, rc_ref, sv_ref,
             sib_a, sib_b, sib_c, stage_a, stage_b, stage_c, land_a, land_b, land_c, sv_mine, sv_all,
             s1_send, s1_recv, s2_send, s2_recv, s3_send, s3_recv, sv_send, sv_recv):
        x, y, c = lax.axis_index("x"), lax.axis_index("y"), lax.axis_index("c")
        me, sibling = (x, y, c), (x, y, 1 - c)
        my_chip = 2 * x + y
        chips = [(1 - x, y), (x, 1 - y), (1 - x, 1 - y)]
        grads = (ga_ref, gb_ref, gc_ref)
        sibs = (sib_a, sib_b, sib_c)
        stages = (stage_a, stage_b, stage_c)
        lands = (land_a, land_b, land_c)
        res = (ra_ref, rb_ref, rc_ref)

        def half_rows(a, chip_idx, half):
            n = shard_rows[a]
            return pl.ds(pl.multiple_of(chip_idx * n + half * (n // 2), 16), n // 2)

        sv_mine[...] = jnp.zeros_like(sv_mine)
        sv_mine[0:4, 0:D_MODEL] = acc_ref[0:4, :]
        sv_mine[4:5, 0:D_IN] = dbin_ref[...]
        sv_mine[5:6, 0:128] = dsink_ref[...]
        my_dev = 4 * x + 2 * y + c
        others = [(x, y, 1 - c)] + [(*chip, cc) for chip in chips for cc in (c, 1 - c)]

        def sv_copy(j, to):
            return pltpu.make_async_remote_copy(
                src_ref=sv_mine, dst_ref=sv_all.at[my_dev], send_sem=sv_send.at[j], recv_sem=sv_recv.at[j],
                device_id=to, device_id_type=MESH)

        sv_sends = [sv_copy(j, to) for j, to in enumerate(others)]
        for cp in sv_sends:
            cp.start()

        def s1(a, k):
            return pltpu.make_async_remote_copy(
                src_ref=grads[a].at[half_rows(a, k, 1 - c), :], dst_ref=sibs[a].at[k],
                send_sem=s1_send.at[a, k], recv_sem=s1_recv.at[a, k], device_id=sibling, device_id_type=MESH)

        s1s = [s1(a, k) for a in range(3) for k in range(4)]
        for cp in s1s:
            cp.start()

        def s2(a, j, to):
            return pltpu.make_async_remote_copy(
                src_ref=stages[a].at[j], dst_ref=lands[a].at[j], send_sem=s2_send.at[a, j], recv_sem=s2_recv.at[a, j],
                device_id=to, device_id_type=MESH)

        s2s = []
        for j, chip in enumerate(chips):
            k = 2 * chip[0] + chip[1]
            for a in range(3):
                pltpu.make_async_remote_copy(
                    src_ref=grads[a].at[half_rows(a, k, c), :], dst_ref=sibs[a].at[k],
                    send_sem=s1_send.at[a, k], recv_sem=s1_recv.at[a, k], device_id=sibling,
                    device_id_type=MESH).wait_recv()
                stages[a][j] = (grads[a][half_rows(a, k, c), :] + sibs[a][k]).astype(BF16)
                cp = s2(a, j, (*chip, c))
                cp.start()
                s2s.append(cp)

        for a in range(3):
            pltpu.make_async_remote_copy(
                src_ref=grads[a].at[half_rows(a, my_chip, c), :], dst_ref=sibs[a].at[my_chip],
                send_sem=s1_send.at[a, my_chip], recv_sem=s1_recv.at[a, my_chip], device_id=sibling,
                device_id_type=MESH).wait_recv()
        for a in range(3):
            n = shard_rows[a]
            tot = grads[a][half_rows(a, my_chip, c), :] + sibs[a][my_chip]
            for j in range(3):
                s2(a, j, me).wait_recv()
                tot = tot + lands[a][j].astype(F32)
            mine = pl.ds(pl.multiple_of(c * (n // 2), 16), n // 2)
            res[a][mine, :] = tot

        def s3(a, half, to):
            n = shard_rows[a]
            blk = res[a].at[pl.ds(pl.multiple_of(half * (n // 2), 16), n // 2), :]
            return pltpu.make_async_remote_copy(
                src_ref=blk, dst_ref=blk, send_sem=s3_send.at[a], recv_sem=s3_recv.at[a],
                device_id=to, device_id_type=MESH)

        s3s = [s3(a, c, sibling) for a in range(3)]
        for cp in s3s:
            cp.start()
        for a in range(3):
            s3(a, 1 - c, me).wait_recv()

        sv_all[my_dev] = sv_mine[...]
        for j in range(7):
            sv_copy(j, me).wait_recv()
        tot = sv_all[0]
        for d in range(1, 8):
            tot = tot + sv_all[d]
        sv_ref[...] = tot
        for cp in sv_sends + s1s + s2s + s3s:
            cp.wait_send()

    vm = pl.BlockSpec(memory_space=pltpu.VMEM)
    half = lambda a: (shard_rows[a] // 2, widths[a])
    scratch = ([pltpu.VMEM((4, *half(a)), F32) for a in range(3)]
               + [pltpu.VMEM((3, *half(a)), BF16) for a in range(3)]
               + [pltpu.VMEM((3, *half(a)), BF16) for a in range(3)]
               + [pltpu.VMEM((8, SV_W), F32), pltpu.VMEM((8, 8, SV_W), F32)]
               + [pltpu.SemaphoreType.DMA((3, 4))] * 2 + [pltpu.SemaphoreType.DMA((3, 3))] * 2
               + [pltpu.SemaphoreType.DMA((3,))] * 2 + [pltpu.SemaphoreType.DMA((7,))] * 2)
    return pl.pallas_call(
        body, name="reduce_grads",
        out_shape=(_vm_sds((SH_IN, D_MODEL), F32), _vm_sds((SH_OUT, D_MODEL), F32),
                   _vm_sds((SH_MEM, 2 * W_C), F32), _vm_sds((8, SV_W), F32)),
        in_specs=[vm] * 6, out_specs=(vm, vm, vm, vm), scratch_shapes=scratch,
        compiler_params=_cp(vmem_mb=56),
    )(g_in, g_out, g_mem, acc, dbin, dsink)


def _adamw(name, w, g, m, v, rows=None, copy_g=False):
    shape = w.shape
    rows = shape[0] if rows is None else rows
    n_out = 4 if copy_g else 3

    def body(w_ref, g_ref, m_ref, v_ref, d_ref, nm_ref, nv_ref, *go_ref):
        gv = g_ref[...]
        if copy_g:
            go_ref[0][...] = gv
        nm = ADAM_B1 * m_ref[...] + (1.0 - ADAM_B1) * gv
        nv = ADAM_B2 * v_ref[...] + (1.0 - ADAM_B2) * (gv * gv)
        m_hat = nm / (1.0 - ADAM_B1 ** ADAM_STEP)
        v_hat = nv / (1.0 - ADAM_B2 ** ADAM_STEP)
        d_ref[...] = -ADAM_LR * (m_hat / (jnp.sqrt(v_hat) + ADAM_EPS) + ADAM_WD * w_ref[...])
        nm_ref[...] = nm
        nv_ref[...] = nv

    spec = pl.BlockSpec((rows, shape[1]), lambda i: (i, 0))
    return pl.pallas_call(
        body, name=name, grid=(shape[0] // rows,), in_specs=[spec] * 4, out_specs=(spec,) * n_out,
        out_shape=(_sds(shape, F32),) * n_out, compiler_params=_cp(("parallel",)),
    )(*_pin(w, g, m, v))


def _rope_tables():
    pos = jnp.arange(SEQ, dtype=F32)
    inv = ROPE_THETA ** (-jnp.arange(0, 64, 2, dtype=F32) / 64)
    ang = pos[:, None] * inv[None, :]
    ang = jnp.concatenate([ang, ang, ang, ang], axis=-1)
    low = (jnp.arange(128) % 64) < 32
    cos, sin = jnp.cos(ang), jnp.sin(ang)
    return cos, jnp.where(low, -sin, 0.0), jnp.where(low, 0.0, sin)


def _local_step(x2, mem2, tgt2, winT, wout, wmem, b_in, sinks, g_branch, ln_gain, ln_bias):
    cos, sa, sb = _rope_tables()
    sinkv = jnp.pad(sinks, ((0, 0), (0, 120)))
    head_of_lane = jnp.arange(512)[None, :] // 64
    gather8 = (head_of_lane.T == jnp.arange(128)[None, :]).astype(BF16)
    gather4 = gather8[:W_B]
    spread4 = gather4.T

    xb, qa, ka, va, bn, b4, b16, qc, z = _in_proj(x2, winT, b_in, cos, sa, sb)
    memb, mkv = _mem_kv(mem2, wmem)
    b4f, b16f = b4.reshape(T, 768), b16.reshape(T, 768)

    swa = dict(kind="band", nb=SEQ // BLK, max_dist=BLK - 1, gqa=True)
    dil = (dict(kind="band", nb=SEQ // BLK), dict(kind="band", nb=SEQ // 4 // BLK), dict(kind="band", nb=1))
    oa, lse_a = _attn_fwd("swa_fwd", qa, 0, W_A, ka, 0, va, 0, W_KV_A, sinks=sinks, **swa)
    o1, l1 = _attn_fwd("dil1_fwd", bn, 0, W_B, bn, 1, bn, 2, W_B, **dil[0])
    o4, l4 = _attn_fwd("dil4_fwd", b4f, 0, W_B, b4f, 1, b4f, 2, W_B, **dil[1])
    o16, l16 = _attn_fwd("dil16_fwd", b16f, 0, W_B, b16f, 1, b16f, 2, W_B, **dil[2])
    oc, lse_c = _attn_fwd("mem_fwd", qc, 0, W_C, mkv, 0, mkv, 1, W_C, kind="mem")

    s4 = lambda w: (B_LOC, 4, SEQ // 4, w)
    s16 = lambda w: (B_LOC, 16, SEQ // 16, w)
    (du, dz, doa, dla, dobn, lsen, dlbn, dob4, lse4, dlb4, dob16, lse16, dlb16, doc, dlc, acc, g_out) = _middle(
        oa, o1, l1, o4.reshape(s4(W_B)), l4.reshape(s4(128)), o16.reshape(s16(W_B)), l16.reshape(s16(128)), oc, z,
        x2, tgt2, g_branch, ln_gain, ln_bias, wout, spread4, gather4, gather8)

    flat = lambda a: a.reshape(T, a.shape[-1])
    dqa, dka, dva, dsink = _attn_bwd("swa_bwd", qa, 0, W_A, ka, 0, va, 0, W_KV_A, doa, lse_a, dla, sinkv=sinkv,
                                     **swa)
    dqn, dkn, dvn = _attn_bwd("dil1_bwd", bn, 0, W_B, bn, 1, bn, 2, W_B, dobn, lsen, dlbn, **dil[0])
    dq4, dk4, dv4 = _attn_bwd("dil4_bwd", b4f, 0, W_B, b4f, 1, b4f, 2, W_B, flat(dob4), flat(lse4), flat(dlb4),
                              **dil[1])
    dq16, dk16, dv16 = _attn_bwd("dil16_bwd", b16f, 0, W_B, b16f, 1, b16f, 2, W_B, flat(dob16), flat(lse16),
                                 flat(dlb16), **dil[2])
    dqc, dmkv = _attn_bwd("mem_bwd", qc, 0, W_C, mkv, 0, mkv, 1, W_C, doc, lse_c, dlc, kind="mem")

    r4 = lambda a: a.reshape(s4(W_B))
    r16 = lambda a: a.reshape(s16(W_B))
    gx, dbin, g_in = _dh_dx(dqa, dka, dva, dqn, dkn, dvn, r4(dq4), r4(dk4), r4(dv4), r16(dq16), r16(dk16),
                            r16(dv16), dqc, dz, du, xb, cos, sa, sb, winT)
    g_mem = _tn_matmul("dw_mem", memb, dmkv, D_MODEL, B_LOC * MEM_LEN)
    return gx, g_in, g_out, g_mem, acc, dbin, dsink


def kernel(x, mem, w_in, b_in, w_mem, attn_sinks, g_branch, w_out, ln_gain, ln_bias, loss_target, m_w_in, m_b_in, m_w_mem, m_attn_sinks, m_g_branch, m_w_out, m_ln_gain, m_ln_bias, v_w_in, v_b_in, v_w_mem, v_attn_sinks, v_g_branch, v_w_out, v_ln_gain, v_ln_bias):
    winT, wout, wmem = _gather_weights(w_in[0].T, w_out[0], w_mem[0])
    gx, g_in, g_out, g_mem, acc, dbin, dsink = _local_step(
        x.reshape(T, D_MODEL), mem.reshape(B_LOC * MEM_LEN, D_MODEL), loss_target.reshape(T, D_MODEL),
        winT, wout, wmem, b_in, attn_sinks, g_branch, ln_gain, ln_bias)
    r_in, r_out, r_mem, sv = _reduce_grads(g_in, g_out, g_mem, acc, dbin, dsink)

    loss = jnp.sum(sv[0, :D_MODEL])
    grads = {
        "b_in": sv[4:5, :D_IN], "w_mem": r_mem[None],
        "attn_sinks": -sv[5:6, 0:8], "g_branch": sv[1:2, :D_MODEL], "w_out": r_out[None],
        "ln_gain": sv[2:3, :D_MODEL], "ln_bias": sv[3:4, :D_MODEL],
    }
    weights = dict(w_in=w_in, b_in=b_in, w_mem=w_mem, attn_sinks=attn_sinks, g_branch=g_branch, w_out=w_out,
                   ln_gain=ln_gain, ln_bias=ln_bias)
    ms = dict(w_in=m_w_in, b_in=m_b_in, w_mem=m_w_mem, attn_sinks=m_attn_sinks, g_branch=m_g_branch, w_out=m_w_out,
              ln_gain=m_ln_gain, ln_bias=m_ln_bias)
    vs = dict(w_in=v_w_in, b_in=v_b_in, w_mem=v_w_mem, attn_sinks=v_attn_sinks, g_branch=v_g_branch, w_out=v_w_out,
              ln_gain=v_ln_gain, ln_bias=v_ln_bias)
    names = ["w_in", "b_in", "w_mem", "attn_sinks", "g_branch", "w_out", "ln_gain", "ln_bias"]
    deltas, new_m, new_v = [], [], []
    for n in names:
        shape = weights[n].shape
        two_d = lambda a: a.reshape(shape[-2], shape[-1])
        if n == "w_in":
            d, nm, nv, gw = (a.T for a in _adamw("adamw_w_in", w_in[0].T, r_in, m_w_in[0].T, v_w_in[0].T, SH_IN // 4,
                                                 copy_g=True))
            grads[n] = gw
        elif n in ("w_out", "w_mem"):
            d, nm, nv, grads[n] = _adamw("adamw_" + n, two_d(weights[n]), two_d(grads[n]), two_d(ms[n]), two_d(vs[n]),
                                         copy_g=True)
        else:
            d, nm, nv = _adamw("adamw_" + n, two_d(weights[n]), two_d(grads[n]), two_d(ms[n]), two_d(vs[n]))
        deltas.append(d.reshape(shape))
        new_m.append(nm.reshape(shape))
        new_v.append(nv.reshape(shape))
    return (loss, gx.reshape(B_LOC, SEQ, D_MODEL), *[grads[n].reshape(weights[n].shape) for n in names],
            *deltas, *new_m, *new_v)
```

```python
import functools

import jax
import jax.numpy as jnp
from jax import lax
from jax.experimental import pallas as pl
from jax.experimental.pallas import tpu as pltpu

F32, BF16 = jnp.float32, jnp.bfloat16

D_MODEL = 1024
SEQ = 2048
B_LOC = 2
T = B_LOC * SEQ
BLK = 128
MEM_LEN = 256
W_A, W_KV_A, W_B, W_C, D_MIX = 512, 128, 256, 256, 1024
D_IN = 2816
O_QA, O_KA, O_VA, O_QB, O_KB, O_VB, O_QC, O_Z = 0, 512, 640, 768, 1024, 1280, 1536, 1792
ROPE_THETA = 10000.0
LN_EPS = 1e-5
RMS_EPS = 1e-6
ALPHA = 2.0 ** 0.25
QK_SCALE = 0.125
N_CHIP = 4
SH_IN, SH_OUT, SH_MEM = D_IN // N_CHIP, D_MIX // N_CHIP, D_MODEL // N_CHIP
NEG = -1e30
ADAM_LR, ADAM_B1, ADAM_B2, ADAM_EPS, ADAM_WD, ADAM_STEP = 0.001, 0.9, 0.999, 1e-08, 0.01, 10
SV_W = 3072
MESH = pl.DeviceIdType.MESH

NN = ((1,), (0,))
NT = ((1,), (1,))
TN = ((0,), (0,))


def _dot(a, b, dims):
    return lax.dot_general(a, b, (dims, ((), ())), preferred_element_type=F32)


def _cp(sem=None, vmem_mb=None):
    kw = {}
    if sem is not None:
        kw["dimension_semantics"] = sem
    if vmem_mb is not None:
        kw["vmem_limit_bytes"] = vmem_mb * 1024 * 1024
    return pltpu.CompilerParams(**kw)


def _sds(shape, dtype):
    return pltpu.HBM(shape, dtype)


def _vm_sds(shape, dtype):
    return jax.ShapeDtypeStruct(shape, dtype)


def _pin(*args):
    return [pltpu.with_memory_space_constraint(a, pltpu.HBM) for a in args]


def _full(shape):
    n = len(shape)
    return pl.BlockSpec(shape, lambda *_: (0,) * n)


def _shard_rows(ref, n, chip, half):
    start = pl.multiple_of((2 * chip[0] + chip[1]) * n + half * (n // 2), 16)
    return ref.at[pl.ds(start, n // 2), :]


def _gather_weights(win_sh, wout_sh, wmem_sh):
    def body(a_ref, b_ref, c_ref, oa_ref, ob_ref, oc_ref, ici_send, ici_recv, d2d_send, d2d_recv):
        x, y, c = lax.axis_index("x"), lax.axis_index("y"), lax.axis_index("c")
        sibling = (x, y, 1 - c)
        chips = [(1 - x, y), (x, 1 - y), (1 - x, 1 - y)]
        for src, out, n in ((a_ref, oa_ref, SH_IN), (b_ref, ob_ref, SH_OUT), (c_ref, oc_ref, SH_MEM)):
            out[pl.ds(pl.multiple_of((2 * x + y) * n, 16), n), :] = src[...].astype(BF16)

        def copy(sems, j, chip_of_block, half, to):
            blk = _shard_rows(oa_ref, SH_IN, chip_of_block, half)
            return pltpu.make_async_remote_copy(
                src_ref=blk, dst_ref=blk, send_sem=sems[0].at[j], recv_sem=sems[1].at[j],
                device_id=to, device_id_type=MESH)

        ici, d2d = (ici_send, ici_recv), (d2d_send, d2d_recv)
        first = [copy(ici, j, (x, y), c, (*chip, c)) for j, chip in enumerate(chips)]
        for cp in first:
            cp.start()
        passed = []
        for j, chip in enumerate(chips):
            copy(ici, j, chip, c, (x, y, c)).wait_recv()
            fw = copy(d2d, j, chip, c, sibling)
            fw.start()
            passed.append(fw)
        for j, chip in enumerate(chips):
            copy(d2d, j, chip, 1 - c, (x, y, c)).wait_recv()
        for cp in first + passed:
            cp.wait_send()

    vm = pl.BlockSpec(memory_space=pltpu.VMEM)
    return pl.pallas_call(
        body, name="gather_weights",
        out_shape=(_vm_sds((D_IN, D_MODEL), BF16), _vm_sds((D_MIX, D_MODEL), BF16),
                   _vm_sds((D_MODEL, 2 * W_C), BF16)),
        in_specs=[vm, vm, vm], out_specs=(vm, vm, vm),
        scratch_shapes=[pltpu.SemaphoreType.DMA((3,))] * 4,
        compiler_params=_cp(vmem_mb=40),
    )(win_sh, wout_sh, wmem_sh)


def _rope(t, cos, sa, sb, sign):
    w = t.shape[1]
    reps = w // 128
    c, a, b = (jnp.tile(v, (1, reps)) if reps > 1 else v for v in (cos, sa, sb))
    rot = pltpu.roll(t, w - 32, 1) * a + pltpu.roll(t, 32, 1) * b
    return t * c + rot if sign > 0 else t * c - rot


def _in_proj(x, winT, b_in, cos, sa, sb, wout_own, wmem_own):
    tm = 256
    spt = SEQ // tm
    n_steps = T // tm
    forward_step = n_steps // 2

    def body(x_ref, w_ref, b_ref, cos_ref, sa_ref, sb_ref, wo_in, wm_in,
             xb_ref, qa_ref, ka_ref, va_ref, bn_ref, b4_ref, b16_ref, qc_ref, z_ref, wo_ref, wm_ref,
             scr, ici_send, ici_recv, d2d_send, d2d_recv):
        i = pl.program_id(0)
        mx, my, mc = lax.axis_index("x"), lax.axis_index("y"), lax.axis_index("c")
        chips = [(1 - mx, my), (mx, 1 - my), (1 - mx, 1 - my)]
        full = ((wo_ref, SH_OUT), (wm_ref, SH_MEM))

        def copy(sems, a, j, chip_of_block, half, to):
            blk = _shard_rows(full[a][0], full[a][1], chip_of_block, half)
            return pltpu.make_async_remote_copy(
                src_ref=blk, dst_ref=blk, send_sem=sems[0].at[a, j], recv_sem=sems[1].at[a, j],
                device_id=to, device_id_type=MESH)

        ici, d2d = (ici_send, ici_recv), (d2d_send, d2d_recv)
        pairs = [(a, j, chip) for j, chip in enumerate(chips) for a in range(2)]

        @pl.when(i == 0)
        def _():
            for a, j, chip in pairs:
                copy(ici, a, j, (mx, my), mc, (*chip, mc)).start()

        @pl.when(i == forward_step)
        def _():
            for a, j, chip in pairs:
                copy(ici, a, j, chip, mc, (mx, my, mc)).wait_recv()
                copy(d2d, a, j, chip, mc, (mx, my, 1 - mc)).start()

        @pl.when(i == n_steps - 1)
        def _():
            for a, j, chip in pairs:
                copy(d2d, a, j, chip, 1 - mc, (mx, my, mc)).wait_recv()
            for a, j, chip in pairs:
                copy(ici, a, j, (mx, my), mc, (*chip, mc)).wait_send()
                copy(d2d, a, j, chip, mc, (mx, my, 1 - mc)).wait_send()

        xb = x_ref[...].astype(BF16)
        xb_ref[...] = xb
        cos_t, sa_t, sb_t = cos_ref[...], sa_ref[...], sb_ref[...]

        def proj(r0, n):
            return _dot(xb, w_ref[r0:r0 + n, :], NT) + b_ref[:, r0:r0 + n]

        def rope(t):
            return _rope(t, cos_t, sa_t, sb_t, +1)

        qa_ref[...] = (rope(proj(O_QA, W_A)) * QK_SCALE).astype(BF16)
        ka_ref[...] = rope(proj(O_KA, W_KV_A)).astype(BF16)
        va_ref[...] = proj(O_VA, W_KV_A).astype(BF16)
        qc_ref[...] = (proj(O_QC, W_C) * QK_SCALE).astype(BF16)
        z_ref[...] = proj(O_Z, D_MIX).astype(BF16)
        parts = (rope(proj(O_QB, W_B)) * QK_SCALE, rope(proj(O_KB, W_B)), proj(O_VB, W_B))
        for k, part in enumerate(parts):
            bn_ref[:, 256 * k:256 * (k + 1)] = part.astype(BF16)
            scr[2 * k] = part[:, :128]
            scr[2 * k + 1] = part[:, 128:]
        for j in range(6):
            for res in range(4):
                b4_ref[0, res, :, 128 * j:128 * (j + 1)] = scr[j, pl.ds(res, tm // 4, stride=4), :].astype(BF16)
            for res in range(16):
                b16_ref[0, res, :, 128 * j:128 * (j + 1)] = scr[j, pl.ds(res, tm // 16, stride=16), :].astype(BF16)

    tok = lambda w: pl.BlockSpec((tm, w), lambda i: (i, 0))
    tab = pl.BlockSpec((tm, 128), lambda i: (i % spt, 0))
    hbm = pl.BlockSpec(memory_space=pl.ANY)
    return pl.pallas_call(
        body, name="in_proj", grid=(n_steps,),
        in_specs=[tok(D_MODEL), _full((D_IN, D_MODEL)), _full((1, D_IN)), tab, tab, tab, hbm, hbm],
        out_specs=(tok(D_MODEL), tok(W_A), tok(W_KV_A), tok(W_KV_A), tok(768),
                   pl.BlockSpec((1, 4, tm // 4, 768), lambda i: (i // spt, 0, i % spt, 0)),
                   pl.BlockSpec((1, 16, tm // 16, 768), lambda i: (i // spt, 0, i % spt, 0)),
                   tok(W_C), tok(D_MIX), hbm, hbm),
        out_shape=(_sds((T, D_MODEL), BF16), _sds((T, W_A), BF16), _sds((T, W_KV_A), BF16), _sds((T, W_KV_A), BF16),
                   _sds((T, 768), BF16), _sds((B_LOC, 4, SEQ // 4, 768), BF16), _sds((B_LOC, 16, SEQ // 16, 768), BF16),
                   _sds((T, W_C), BF16), _sds((T, D_MIX), BF16),
                   _sds((D_MIX, D_MODEL), BF16), _sds((D_MODEL, 2 * W_C), BF16)),
        input_output_aliases={6: 9, 7: 10},
        scratch_shapes=[pltpu.VMEM((6, tm, 128), F32)] + [pltpu.SemaphoreType.DMA((2, 3))] * 4,
        compiler_params=_cp(("arbitrary",), vmem_mb=48),
    )(*_pin(x, winT, b_in, cos, sa, sb, wout_own, wmem_own))


def _mem_kv(mem, wmem):
    def body(m_ref, w_ref, mb_ref, kv_ref):
        mb = m_ref[...].astype(BF16)
        mb_ref[...] = mb
        kv_ref[...] = _dot(mb, w_ref[...], NN).astype(BF16)

    n = B_LOC * MEM_LEN
    return pl.pallas_call(
        body, name="mem_kv",
        out_shape=(_sds((n, D_MODEL), BF16), _sds((n, 2 * W_C), BF16)),
    )(*_pin(mem, wmem))


QB = 4
QR = QB * BLK


def _lane_lo():
    return lax.broadcasted_iota(jnp.int32, (1, 128), 1) < 64


def _dup_head(k2, hk, lo):
    kf = k2.astype(F32)
    r = pltpu.roll(kf, 64, 1)
    return (jnp.where(lo, kf, r) if hk == 0 else jnp.where(lo, r, kf)).astype(BF16)


def _stack_heads(pairs, lo):
    parts = []
    for x2 in pairs:
        z = jnp.zeros_like(x2)
        parts += [jnp.where(lo, x2, z), jnp.where(lo, z, x2)]
    return jnp.concatenate(parts, axis=0)


def _prev_mode(kind, nb, j):
    if kind == "mem" or nb == 1:
        return "no"
    if nb <= QB:
        return "yes" if j % nb else "no"
    return "yes" if j else "dyn"


class _Attn:
    def __init__(self, kind, nb, max_dist, gqa, qw, kvw, qcb, kcb, vcb):
        self.kind, self.nb, self.gqa, self.qw, self.kvw = kind, nb, gqa, qw, kvw
        npairs = qw // 128
        self.groups = ([(hk, [2 * hk, 2 * hk + 1]) for hk in range(npairs // 2)] if gqa
                       else [(p, [p]) for p in range(npairs)])
        self.nh = 2 * len(self.groups[0][1])
        self.cols = 128 * self.nh
        self.reach = BLK - max_dist
        self.ext_prev = kind == "band" and nb > QB
        self.q_spec = pl.BlockSpec((QR, qw), lambda g: (g, qcb))
        self.row_spec = pl.BlockSpec((QR, qw), lambda g: (g, 0))
        self.stat_spec = pl.BlockSpec((QR, 128), lambda g: (g, 0))
        if kind == "mem":
            per = SEQ // QR
            self.kv_specs = [pl.BlockSpec((MEM_LEN, kvw), lambda g: (g // per, kcb)),
                             pl.BlockSpec((MEM_LEN, kvw), lambda g: (g // per, vcb))]
        else:
            self.kv_specs = [pl.BlockSpec((QR, kvw), lambda g: (g, kcb)), pl.BlockSpec((QR, kvw), lambda g: (g, vcb))]
            if self.ext_prev:
                self.kv_specs += [pl.BlockSpec((BLK, kvw), lambda g: (jnp.maximum(g * QB - 1, 0), kcb)),
                                  pl.BlockSpec((BLK, kvw), lambda g: (jnp.maximum(g * QB - 1, 0), vcb))]

    def masks(self):
        if self.kind == "mem":
            return None
        kj = lax.broadcasted_iota(jnp.int32, (2 * BLK, self.cols), 0)
        qi = lax.broadcasted_iota(jnp.int32, (2 * BLK, self.cols), 1) & (BLK - 1)
        kj1 = lax.broadcasted_iota(jnp.int32, (BLK, self.cols), 0)
        qi1 = lax.broadcasted_iota(jnp.int32, (BLK, self.cols), 1) & (BLK - 1)
        return kj, qi, kj1 <= qi1

    def keys(self, j, gi, kc_ref, vc_ref, kp_ref, vp_ref, lo, kq, g):
        def kv(k_ref, v_ref, r):
            if self.gqa:
                return _dup_head(k_ref[r, :], gi, lo), _dup_head(v_ref[r, :], gi, lo)
            sl = slice(128 * gi, 128 * (gi + 1))
            return k_ref[r, sl], v_ref[r, sl]

        if self.kind == "mem":
            key0 = pl.multiple_of((g // (SEQ // QR)) * MEM_LEN, MEM_LEN)
            return (*kv(kc_ref, vc_ref, slice(None)), None, [(0, MEM_LEN, key0)])
        kj, qi, cur = kq
        row0 = g * QR + BLK * j
        mode = _prev_mode(self.kind, self.nb, j)
        if mode == "no":
            return (*kv(kc_ref, vc_ref, slice(BLK * j, BLK * (j + 1))), cur, [(0, BLK, pl.multiple_of(row0, BLK))])
        if mode == "yes":
            mask = jnp.logical_and(kj >= qi + self.reach, kj <= qi + BLK)
            return (*kv(kc_ref, vc_ref, slice(BLK * (j - 1), BLK * (j + 1))), mask,
                    [(0, 2 * BLK, pl.multiple_of(row0 - BLK, BLK))])
        has_prev = ((g * QB) % self.nb) > 0
        hp = has_prev.astype(jnp.int32)
        mask = jnp.logical_and(kj >= qi * hp + (self.reach * hp + BLK * (1 - hp)), kj <= qi + BLK)
        kp, vp = kv(kp_ref, vp_ref, slice(None))
        kc, vc = kv(kc_ref, vc_ref, slice(0, BLK))
        return (jnp.concatenate([kp, kc], axis=0), jnp.concatenate([vp, vc], axis=0), mask,
                [(0, BLK, pl.multiple_of(jnp.maximum(row0 - BLK, 0), BLK)), (BLK, BLK, pl.multiple_of(row0, BLK))])


def _attn_fwd(name, q, qcb, qw, k, kcb, v, vcb, kvw, *, kind, nb=1, max_dist=BLK, gqa=False, sinks=None):
    a = _Attn(kind, nb, max_dist, gqa, qw, kvw, qcb, kcb, vcb)

    def body(*refs):
        it = iter(refs)
        q_ref, kc_ref, vc_ref = next(it), next(it), next(it)
        kp_ref, vp_ref = (next(it), next(it)) if a.ext_prev else (None, None)
        sink_ref = next(it) if sinks is not None else None
        o_ref, lse_ref = next(it), next(it)
        g = pl.program_id(0)
        lo = _lane_lo()
        top = lax.broadcasted_iota(jnp.int32, (128, 1), 0) < 64
        rid = lax.broadcasted_iota(jnp.int32, (8, 128), 0)
        kq = a.masks()
        for j in range(QB):
            rows = slice(BLK * j, BLK * (j + 1))
            stat = jnp.zeros((8, 128), F32)
            for gi, pairs in a.groups:
                qs = _stack_heads([q_ref[rows, 128 * p:128 * (p + 1)] for p in pairs], lo)
                kk, vv, mask, _ = a.keys(j, gi, kc_ref, vc_ref, kp_ref, vp_ref, lo, kq, g)
                pieces = [slice(r0, r0 + BLK) for r0 in range(0, kk.shape[0], BLK)]
                ss = []
                for r in pieces:
                    s = _dot(kk[r], qs, NT)
                    ss.append(s if mask is None else jnp.where(mask[r], s, NEG))
                m = jnp.max(ss[0], axis=0, keepdims=True)
                for s in ss[1:]:
                    m = jnp.maximum(m, jnp.max(s, axis=0, keepdims=True))
                if sink_ref is not None:
                    sk = jnp.concatenate([jnp.full((1, 128), sink_ref[0, a.nh * gi + i], F32) for i in range(a.nh)],
                                         axis=1)
                    m = jnp.maximum(m, sk)
                l, ot = None, None
                for r, s in zip(pieces, ss):
                    p = jnp.exp(s - m)
                    ps = jnp.sum(p, axis=0, keepdims=True)
                    c = _dot(vv[r], p.astype(BF16), TN)
                    l, ot = (ps, c) if l is None else (l + ps, ot + c)
                if sink_ref is not None:
                    l = l + jnp.exp(sk - m)
                ot = ot * pl.reciprocal(l, approx=True)
                lse = m + jnp.log(l)
                for i, p in enumerate(pairs):
                    o2t = jnp.where(top, ot[:, 256 * i:256 * i + 128], ot[:, 256 * i + 128:256 * i + 256])
                    o_ref[rows, 128 * p:128 * (p + 1)] = o2t.T.astype(BF16)
                for i in range(a.nh):
                    stat = jnp.where(rid == a.nh * gi + i, lse[:, 128 * i:128 * (i + 1)], stat)
            lse_ref[rows, :] = jnp.concatenate([stat, jnp.zeros((120, 128), F32)], axis=0).T

    args = [q, k, v] + ([k, v] if a.ext_prev else [])
    in_specs = [a.q_spec] + a.kv_specs
    if sinks is not None:
        args.append(sinks)
        in_specs.append(pl.BlockSpec(memory_space=pltpu.SMEM))
    return pl.pallas_call(
        body, name=name, grid=(T // QR,), in_specs=in_specs, out_specs=(a.row_spec, a.stat_spec),
        out_shape=(_sds((T, qw), BF16), _sds((T, 128), F32)),
        compiler_params=_cp(("parallel",), vmem_mb=40),
    )(*_pin(*args))


def _attn_bwd(name, q, qcb, qw, k, kcb, v, vcb, kvw, do, lse, dl, *, kind, nb=1, max_dist=BLK, gqa=False,
              sinkv=None):
    a = _Attn(kind, nb, max_dist, gqa, qw, kvw, qcb, kcb, vcb)

    def body(*refs):
        it = iter(refs)
        q_ref, kc_ref, vc_ref = next(it), next(it), next(it)
        kp_ref, vp_ref = (next(it), next(it)) if a.ext_prev else (None, None)
        do_ref, lse_ref, dl_ref = next(it), next(it), next(it)
        sinkv_ref = next(it) if sinkv is not None else None
        dq_ref = next(it)
        if kind == "mem":
            dkv_ref = next(it)
        else:
            dk_ref, dv_ref = next(it), next(it)
        dsink_ref = next(it) if sinkv is not None else None
        g = pl.program_id(0)
        lo = _lane_lo()
        top = lax.broadcasted_iota(jnp.int32, (128, 1), 0) < 64

        @pl.when(g == 0)
        def _():
            if kind == "mem":
                dkv_ref[...] = jnp.zeros_like(dkv_ref)
            else:
                dk_ref[...] = jnp.zeros_like(dk_ref)
                dv_ref[...] = jnp.zeros_like(dv_ref)
            if dsink_ref is not None:
                dsink_ref[...] = jnp.zeros_like(dsink_ref)

        kq = a.masks()
        for j in range(QB):
            rows = slice(BLK * j, BLK * (j + 1))
            lse_t = lse_ref[rows, :].T
            dl_t = dl_ref[rows, :].T
            for gi, pairs in a.groups:
                heads = [a.nh * gi + i for i in range(a.nh)]
                qs = _stack_heads([q_ref[rows, 128 * p:128 * (p + 1)] for p in pairs], lo)
                dos = _stack_heads([do_ref[rows, 128 * p:128 * (p + 1)] for p in pairs], lo)
                lse_row = jnp.concatenate([lse_t[h:h + 1, :] for h in heads], axis=1)
                dl_row = jnp.concatenate([dl_t[h:h + 1, :] for h in heads], axis=1)
                kk, vv, mask, dests = a.keys(j, gi, kc_ref, vc_ref, kp_ref, vp_ref, lo, kq, g)
                s = _dot(kk, qs, NT)
                if mask is not None:
                    s = jnp.where(mask, s, NEG)
                p = jnp.exp(s - lse_row)
                ds = (p * (_dot(vv, dos, NT) - dl_row)).astype(BF16)
                dqt = _dot(kk, ds, TN)
                ck = _dot(ds, qs, NN)
                cv = _dot(p.astype(BF16), dos, NN)
                if gqa:
                    sel = lo if gi == 0 else jnp.logical_not(lo)
                    ck = jnp.where(sel, ck + pltpu.roll(ck, 64, 1), 0.0)
                    cv = jnp.where(sel, cv + pltpu.roll(cv, 64, 1), 0.0)
                    kcols = slice(0, 128)
                else:
                    kcols = slice(128 * gi, 128 * (gi + 1))
                for r0, nr, key0 in dests:
                    krows = pl.ds(key0, nr)
                    if kind == "mem":
                        dkv_ref[krows, kcols] += ck[r0:r0 + nr]
                        dkv_ref[krows, slice(kvw + kcols.start, kvw + kcols.stop)] += cv[r0:r0 + nr]
                    else:
                        dk_ref[krows, kcols] += ck[r0:r0 + nr]
                        dv_ref[krows, kcols] += cv[r0:r0 + nr]
                for i, p in enumerate(pairs):
                    dq2t = jnp.where(top, dqt[:, 256 * i:256 * i + 128], dqt[:, 256 * i + 128:256 * i + 256])
                    dq_ref[rows, 128 * p:128 * (p + 1)] = dq2t.T.astype(BF16)
        if dsink_ref is not None:
            ps = jnp.exp(sinkv_ref[...] - lse_ref[...]) * dl_ref[...]
            dsink_ref[...] += jnp.sum(ps, axis=0, keepdims=True)

    args = [q, k, v] + ([k, v] if a.ext_prev else []) + [do, lse, dl]
    in_specs = [a.q_spec] + a.kv_specs + [a.row_spec, a.stat_spec, a.stat_spec]
    if sinkv is not None:
        args.append(sinkv)
        in_specs.append(_full((1, 128)))
    out_shape = [_sds((T, qw), BF16)]
    out_specs = [a.row_spec]
    if kind == "mem":
        out_shape.append(_sds((B_LOC * MEM_LEN, 2 * kvw), F32))
        out_specs.append(_full((B_LOC * MEM_LEN, 2 * kvw)))
    else:
        out_shape += [_sds((T, kvw), F32)] * 2
        out_specs += [_full((T, kvw))] * 2
    if sinkv is not None:
        out_shape.append(_sds((1, 128), F32))
        out_specs.append(_full((1, 128)))
    return pl.pallas_call(
        body, name=name, grid=(T // QR,), in_specs=in_specs, out_specs=tuple(out_specs),
        out_shape=tuple(out_shape), compiler_params=_cp(("arbitrary",), vmem_mb=48),
    )(*_pin(*args))


def _dot2(v, w_ref):
    hi = v.astype(BF16)
    lo = (v - hi.astype(F32)).astype(BF16)
    return _dot(hi, w_ref[...], NN) + _dot(lo, w_ref[...], NN)


def _middle(oa, o1, l1, o4, l4, o16, l16, oc, z, x, tgt, g_br, ln_g, ln_b, wout, spread4, gather4, gather8):
    tm = 256
    spt = SEQ // tm

    def body(oa_ref, o1_ref, l1_ref, o4_ref, l4_ref, o16_ref, l16_ref, oc_ref, z_ref, x_ref, t_ref,
             g_ref, lg_ref, lb_ref, w_ref, sp4_ref, ga4_ref, ga8_ref,
             du_ref, dz_ref, doa_ref, dla_ref,
             dobn_ref, lsen_ref, dlbn_ref, dob4_ref, lse4_ref, dlb4_ref, dob16_ref, lse16_ref, dlb16_ref,
             doc_ref, dlc_ref, acc_ref, gout_ref, scr):
        i = pl.program_id(0)

        @pl.when(i == 0)
        def _():
            acc_ref[...] = jnp.zeros_like(acc_ref)
            gout_ref[...] = jnp.zeros_like(gout_ref)

        for res in range(4):
            rows = pl.ds(res, tm // 4, stride=4)
            for j in range(2):
                scr[j, rows, :] = o4_ref[0, res, :, 128 * j:128 * (j + 1)].astype(F32)
            scr[2, rows, :] = l4_ref[0, res]
        for res in range(16):
            rows = pl.ds(res, tm // 16, stride=16)
            for j in range(2):
                scr[3 + j, rows, :] = o16_ref[0, res, :, 128 * j:128 * (j + 1)].astype(F32)
            scr[5, rows, :] = l16_ref[0, res]
        cat = lambda a: jnp.concatenate([scr[a], scr[a + 1]], axis=1)
        o1v, o4v, o16v = o1_ref[...].astype(F32), cat(0), cat(3)
        l1v, l4v, l16v = l1_ref[...], scr[2], scr[5]
        mx = jnp.maximum(jnp.maximum(l1v, l4v), l16v)
        e1, e4, e16 = jnp.exp(l1v - mx), jnp.exp(l4v - mx), jnp.exp(l16v - mx)
        ssum = e1 + e4 + e16
        lse_b = mx + jnp.log(ssum)
        inv = 1.0 / ssum
        ob = (_dot2(e1 * inv, sp4_ref) * o1v + _dot2(e4 * inv, sp4_ref) * o4v + _dot2(e16 * inv, sp4_ref) * o16v)
        oav, ocv = oa_ref[...].astype(F32), oc_ref[...].astype(F32)

        def rms(o):
            r = lax.rsqrt(jnp.sum(o * o, axis=1, keepdims=True) * (1.0 / o.shape[1]) + RMS_EPS)
            return o * r, r

        na, ra = rms(oav)
        nb_, rb = rms(ob)
        nc, rc = rms(ocv)
        n = jnp.concatenate([na, nb_, nc], axis=1)
        zf = z_ref[...].astype(F32)
        sig = 1.0 / (1.0 + jnp.exp(-zf))
        sz = zf * sig
        gb = g_ref[...]
        yb = (n * gb * sz).astype(BF16)
        u = ALPHA * x_ref[...] + _dot(yb, w_ref[...], NN)
        inv_d = 1.0 / D_MODEL
        mu = jnp.sum(u, axis=1, keepdims=True) * inv_d
        uc = u - mu
        rstd = lax.rsqrt(jnp.sum(uc * uc, axis=1, keepdims=True) * inv_d + LN_EPS)
        xh = uc * rstd
        lg = lg_ref[...]
        diff = xh * lg + lb_ref[...] - t_ref[...]
        acc_ref[0:1, :] += jnp.sum(diff * diff, axis=0, keepdims=True) * (0.5 * inv_d)
        dout = diff * inv_d
        acc_ref[2:3, :] += jnp.sum(dout * xh, axis=0, keepdims=True)
        acc_ref[3:4, :] += jnp.sum(dout, axis=0, keepdims=True)
        dxh = dout * lg
        du = rstd * (dxh - jnp.sum(dxh, axis=1, keepdims=True) * inv_d
                     - xh * (jnp.sum(dxh * xh, axis=1, keepdims=True) * inv_d))
        dub = du.astype(BF16)
        du_ref[...] = dub
        gout_ref[...] += _dot(yb, dub, TN)
        dy = _dot(dub, w_ref[...], NT)
        t1 = dy * sz
        acc_ref[1:2, :] += jnp.sum(t1 * n, axis=0, keepdims=True)
        dn = t1 * gb
        dz_ref[...] = (dy * n * gb * (sig * (1.0 + zf * (1.0 - sig)))).astype(BF16)

        def rms_bwd(dn_, n_, r):
            return r * (dn_ - n_ * (jnp.sum(dn_ * n_, axis=1, keepdims=True) * (1.0 / n_.shape[1])))

        doa = rms_bwd(dn[:, :W_A], na, ra)
        dob = rms_bwd(dn[:, W_A:W_A + W_B], nb_, rb)
        doc = rms_bwd(dn[:, W_A + W_B:], nc, rc)
        doa_ref[...] = doa.astype(BF16)
        dla_ref[...] = _dot2(doa * oav, ga8_ref)
        doc_ref[...] = doc.astype(BF16)
        dlc_ref[...] = _dot2(doc * ocv, ga4_ref)
        dlb = _dot2(dob * ob, ga4_ref)
        dobn_ref[...] = dob.astype(BF16)
        lsen_ref[...] = lse_b
        dlbn_ref[...] = dlb
        scr[0] = dob[:, :128]
        scr[1] = dob[:, 128:]
        scr[2] = lse_b
        scr[3] = dlb
        for res in range(4):
            rows = pl.ds(res, tm // 4, stride=4)
            for j in range(2):
                dob4_ref[0, res, :, 128 * j:128 * (j + 1)] = scr[j, rows, :].astype(BF16)
            lse4_ref[0, res] = scr[2, rows, :]
            dlb4_ref[0, res] = scr[3, rows, :]
        for res in range(16):
            rows = pl.ds(res, tm // 16, stride=16)
            for j in range(2):
                dob16_ref[0, res, :, 128 * j:128 * (j + 1)] = scr[j, rows, :].astype(BF16)
            lse16_ref[0, res] = scr[2, rows, :]
            dlb16_ref[0, res] = scr[3, rows, :]

    tok = lambda w: pl.BlockSpec((tm, w), lambda i: (i, 0))
    p4 = lambda w: pl.BlockSpec((1, 4, tm // 4, w), lambda i: (i // spt, 0, i % spt, 0))
    p16 = lambda w: pl.BlockSpec((1, 16, tm // 16, w), lambda i: (i // spt, 0, i % spt, 0))
    s4 = lambda w, dt: _sds((B_LOC, 4, SEQ // 4, w), dt)
    s16 = lambda w, dt: _sds((B_LOC, 16, SEQ // 16, w), dt)
    row = _full((1, D_MODEL))
    return pl.pallas_call(
        body, name="middle", grid=(T // tm,),
        in_specs=[tok(W_A), tok(W_B), tok(128), p4(W_B), p4(128), p16(W_B), p16(128), tok(W_C), tok(D_MIX),
                  tok(D_MODEL), tok(D_MODEL), row, row, row, _full((D_MIX, D_MODEL)),
                  _full((128, W_B)), _full((W_B, 128)), _full((W_A, 128))],
        out_specs=(tok(D_MODEL), tok(D_MIX), tok(W_A), tok(128),
                   tok(W_B), tok(128), tok(128), p4(W_B), p4(128), p4(128), p16(W_B), p16(128), p16(128),
                   tok(W_C), tok(128), _full((8, D_MODEL)), _full((D_MIX, D_MODEL))),
        out_shape=(_sds((T, D_MODEL), BF16), _sds((T, D_MIX), BF16),
                   _sds((T, W_A), BF16), _sds((T, 128), F32),
                   _sds((T, W_B), BF16), _sds((T, 128), F32), _sds((T, 128), F32),
                   s4(W_B, BF16), s4(128, F32), s4(128, F32), s16(W_B, BF16), s16(128, F32), s16(128, F32),
                   _sds((T, W_C), BF16), _sds((T, 128), F32), _sds((8, D_MODEL), F32),
                   _sds((D_MIX, D_MODEL), F32)),
        scratch_shapes=[pltpu.VMEM((6, tm, 128), F32)],
        compiler_params=_cp(("arbitrary",), vmem_mb=56),
    )(*_pin(oa, o1, l1, o4, l4, o16, l16, oc, z, x, tgt, g_br, ln_g, ln_b, wout, spread4, gather4, gather8))


def _dh_dx(dqa, dka, dva, dqn, dkn, dvn, dq4, dk4, dv4, dq16, dk16, dv16, dqc, dz, du, xb, cos, sa, sb, winT):
    tm = 512
    spt = SEQ // tm

    def body(dqa_ref, dka_ref, dva_ref, dqn_ref, dkn_ref, dvn_ref, dq4_ref, dk4_ref, dv4_ref,
             dq16_ref, dk16_ref, dv16_ref, dqc_ref, dz_ref, du_ref, xb_ref, cos_ref, sa_ref, sb_ref, w_ref,
             gx_ref, db_ref, gin_ref, dh_ref, scr):
        i = pl.program_id(0)

        @pl.when(i == 0)
        def _():
            db_ref[...] = jnp.zeros_like(db_ref)
            gin_ref[...] = jnp.zeros_like(gin_ref)

        cos_t, sa_t, sb_t = cos_ref[...], sa_ref[...], sb_ref[...]

        def rope_t(t):
            return _rope(t, cos_t, sa_t, sb_t, -1)

        def put(r0, val):
            n = val.shape[1]
            dh_ref[:, r0:r0 + n] = val.astype(BF16)
            db_ref[:, r0:r0 + n] += jnp.sum(val, axis=0, keepdims=True)

        put(O_QA, rope_t(dqa_ref[...].astype(F32)) * QK_SCALE)
        put(O_KA, rope_t(dka_ref[...]))
        put(O_VA, dva_ref[...])
        put(O_QC, dqc_ref[...].astype(F32) * QK_SCALE)
        put(O_Z, dz_ref[...].astype(F32))
        for k, (n_ref, r4, r16) in enumerate(((dqn_ref, dq4_ref, dq16_ref), (dkn_ref, dk4_ref, dk16_ref),
                                               (dvn_ref, dv4_ref, dv16_ref))):
            for j in range(2):
                sl = slice(128 * j, 128 * (j + 1))
                scr[2 * k + j] = n_ref[:, sl].astype(F32)
                for res in range(4):
                    scr[2 * k + j, pl.ds(res, tm // 4, stride=4), :] += r4[0, res, :, sl].astype(F32)
                for res in range(16):
                    scr[2 * k + j, pl.ds(res, tm // 16, stride=16), :] += r16[0, res, :, sl].astype(F32)
        cat = lambda a: jnp.concatenate([scr[a], scr[a + 1]], axis=1)
        put(O_QB, rope_t(cat(0)) * QK_SCALE)
        put(O_KB, rope_t(cat(2)))
        put(O_VB, cat(4))
        gx_ref[...] = _dot(dh_ref[...], w_ref[...], NN) + ALPHA * du_ref[...].astype(F32)
        gin_ref[...] += _dot(dh_ref[...], xb_ref[...], TN)

    tok = lambda w: pl.BlockSpec((tm, w), lambda i: (i, 0))
    tab = pl.BlockSpec((tm, 128), lambda i: (i % spt, 0))
    p4 = pl.BlockSpec((1, 4, tm // 4, W_B), lambda i: (i // spt, 0, i % spt, 0))
    p16 = pl.BlockSpec((1, 16, tm // 16, W_B), lambda i: (i // spt, 0, i % spt, 0))
    once = lambda shape: pl.BlockSpec(shape, lambda i: (0, 0), pipeline_mode=pl.Buffered(1))
    return pl.pallas_call(
        body, name="dh_dx", grid=(T // tm,),
        in_specs=[tok(W_A), tok(W_KV_A), tok(W_KV_A), tok(W_B), tok(W_B), tok(W_B), p4, p4, p4, p16, p16, p16,
                  tok(W_C), tok(D_MIX), tok(D_MODEL), tok(D_MODEL), tab, tab, tab, once((D_IN, D_MODEL))],
        out_specs=(tok(D_MODEL), _full((1, D_IN)), once((D_IN, D_MODEL))),
        out_shape=(_sds((T, D_MODEL), F32), _sds((1, D_IN), F32), _sds((D_IN, D_MODEL), F32)),
        scratch_shapes=[pltpu.VMEM((tm, D_IN), BF16), pltpu.VMEM((6, tm, 128), F32)],
        compiler_params=_cp(("arbitrary",), vmem_mb=56),
    )(*_pin(dqa, dka, dva, dqn, dkn, dvn, dq4, dk4, dv4, dq16, dk16, dv16, dqc, dz, du, xb, cos, sa, sb, winT))


def _tn_matmul(name, a, b, bm, bt):
    n, m_all = a.shape
    n_cols = b.shape[1]

    def body(a_ref, b_ref, o_ref):
        @pl.when(pl.program_id(1) == 0)
        def _():
            o_ref[...] = jnp.zeros_like(o_ref)

        o_ref[...] += _dot(a_ref[...].astype(BF16), b_ref[...].astype(BF16), TN)

    return pl.pallas_call(
        body, name=name, grid=(m_all // bm, n // bt),
        in_specs=[pl.BlockSpec((bt, bm), lambda m, t: (t, m)), pl.BlockSpec((bt, n_cols), lambda m, t: (t, 0))],
        out_specs=pl.BlockSpec((bm, n_cols), lambda m, t: (m, 0)),
        out_shape=_sds((m_all, n_cols), F32),
        compiler_params=_cp(("parallel", "arbitrary"), vmem_mb=48),
    )(*_pin(a, b))


def _reduce_grads(g_in, g_out, g_mem, acc, dbin, dsink):
    shard_rows = (SH_IN, SH_OUT, SH_MEM)
    widths = (D_MODEL, D_MODEL, 2 * W_C)

    def body(ga_ref, gb_ref, gc_ref, acc_ref, dbin_ref, dsink_ref,
             ra_ref, rb_ref, rc_ref, sv_ref,
             sib_a, sib_b, sib_c, stage_a, stage_b, stage_c, land_a, land_b, land_c, sv_mine, sv_all,
             s1_send, s1_recv, s2_send, s2_recv, s3_send, s3_recv, sv_send, sv_recv):
        x, y, c = lax.axis_index("x"), lax.axis_index("y"), lax.axis_index("c")
        me, sibling = (x, y, c), (x, y, 1 - c)
        my_chip = 2 * x + y
        chips = [(1 - x, y), (x, 1 - y), (1 - x, 1 - y)]
        grads = (ga_ref, gb_ref, gc_ref)
        sibs = (sib_a, sib_b, sib_c)
        stages = (stage_a, stage_b, stage_c)
        lands = (land_a, land_b, land_c)
        res = (ra_ref, rb_ref, rc_ref)

        def half_rows(a, chip_idx, half):
            n = shard_rows[a]
            return pl.ds(pl.multiple_of(chip_idx * n + half * (n // 2), 16), n // 2)

        sv_mine[...] = jnp.zeros_like(sv_mine)
        sv_mine[0:4, 0:D_MODEL] = acc_ref[0:4, :]
        sv_mine[4:5, 0:D_IN] = dbin_ref[...]
        sv_mine[5:6, 0:128] = dsink_ref[...]
        my_dev = 4 * x + 2 * y + c
        others = [(x, y, 1 - c)] + [(*chip, cc) for chip in chips for cc in (c, 1 - c)]

        def sv_copy(j, to):
            return pltpu.make_async_remote_copy(
                src_ref=sv_mine, dst_ref=sv_all.at[my_dev], send_sem=sv_send.at[j], recv_sem=sv_recv.at[j],
                device_id=to, device_id_type=MESH)

        sv_sends = [sv_copy(j, to) for j, to in enumerate(others)]
        for cp in sv_sends:
            cp.start()

        def s1(a, k):
            return pltpu.make_async_remote_copy(
                src_ref=grads[a].at[half_rows(a, k, 1 - c), :], dst_ref=sibs[a].at[k],
                send_sem=s1_send.at[a, k], recv_sem=s1_recv.at[a, k], device_id=sibling, device_id_type=MESH)

        s1s = [s1(a, 2 * chip[0] + chip[1]) for chip in chips + [(x, y)] for a in range(3)]
        for cp in s1s:
            cp.start()

        def s2(a, j, to):
            return pltpu.make_async_remote_copy(
                src_ref=stages[a].at[j], dst_ref=lands[a].at[j], send_sem=s2_send.at[a, j], recv_sem=s2_recv.at[a, j],
                device_id=to, device_id_type=MESH)

        s2s = []
        for j, chip in enumerate(chips):
            k = 2 * chip[0] + chip[1]
            for a in range(3):
                pltpu.make_async_remote_copy(
                    src_ref=grads[a].at[half_rows(a, k, c), :], dst_ref=sibs[a].at[k],
                    send_sem=s1_send.at[a, k], recv_sem=s1_recv.at[a, k], device_id=sibling,
                    device_id_type=MESH).wait_recv()
                stages[a][j] = (grads[a][half_rows(a, k, c), :] + sibs[a][k]).astype(BF16)
                cp = s2(a, j, (*chip, c))
                cp.start()
                s2s.append(cp)

        for a in range(3):
            pltpu.make_async_remote_copy(
                src_ref=grads[a].at[half_rows(a, my_chip, c), :], dst_ref=sibs[a].at[my_chip],
                send_sem=s1_send.at[a, my_chip], recv_sem=s1_recv.at[a, my_chip], device_id=sibling,
                device_id_type=MESH).wait_recv()
        for a in range(3):
            n = shard_rows[a]
            tot = grads[a][half_rows(a, my_chip, c), :] + sibs[a][my_chip]
            for j in range(3):
                s2(a, j, me).wait_recv()
                tot = tot + lands[a][j].astype(F32)
            mine = pl.ds(pl.multiple_of(c * (n // 2), 16), n // 2)
            res[a][mine, :] = tot

        def s3(a, half, to):
            n = shard_rows[a]
            blk = res[a].at[pl.ds(pl.multiple_of(half * (n // 2), 16), n // 2), :]
            return pltpu.make_async_remote_copy(
                src_ref=blk, dst_ref=blk, send_sem=s3_send.at[a], recv_sem=s3_recv.at[a],
                device_id=to, device_id_type=MESH)

        s3s = [s3(a, c, sibling) for a in range(3)]
        for cp in s3s:
            cp.start()
        for a in range(3):
            s3(a, 1 - c, me).wait_recv()

        sv_all[my_dev] = sv_mine[...]
        for j in range(7):
            sv_copy(j, me).wait_recv()
        tot = sv_all[0]
        for d in range(1, 8):
            tot = tot + sv_all[d]
        sv_ref[...] = tot
        for cp in sv_sends + s1s + s2s + s3s:
            cp.wait_send()

    vm = pl.BlockSpec(memory_space=pltpu.VMEM)
    half = lambda a: (shard_rows[a] // 2, widths[a])
    scratch = ([pltpu.VMEM((4, *half(a)), F32) for a in range(3)]
               + [pltpu.VMEM((3, *half(a)), BF16) for a in range(3)]
               + [pltpu.VMEM((3, *half(a)), BF16) for a in range(3)]
               + [pltpu.VMEM((8, SV_W), F32), pltpu.VMEM((8, 8, SV_W), F32)]
               + [pltpu.SemaphoreType.DMA((3, 4))] * 2 + [pltpu.SemaphoreType.DMA((3, 3))] * 2
               + [pltpu.SemaphoreType.DMA((3,))] * 2 + [pltpu.SemaphoreType.DMA((7,))] * 2)
    return pl.pallas_call(
        body, name="reduce_grads",
        out_shape=(_vm_sds((SH_IN, D_MODEL), F32), _vm_sds((SH_OUT, D_MODEL), F32),
                   _vm_sds((SH_MEM, 2 * W_C), F32), _vm_sds((8, SV_W), F32)),
        in_specs=[vm] * 6, out_specs=(vm, vm, vm, vm), scratch_shapes=scratch,
        compiler_params=_cp(vmem_mb=56),
    )(g_in, g_out, g_mem, acc, dbin, dsink)


def _adamw(name, w, g, m, v, rows=None, copy_g=False):
    shape = w.shape
    rows = shape[0] if rows is None else rows
    n_out = 4 if copy_g else 3

    def body(w_ref, g_ref, m_ref, v_ref, d_ref, nm_ref, nv_ref, *go_ref):
        gv = g_ref[...]
        if copy_g:
            go_ref[0][...] = gv
        nm = ADAM_B1 * m_ref[...] + (1.0 - ADAM_B1) * gv
        nv = ADAM_B2 * v_ref[...] + (1.0 - ADAM_B2) * (gv * gv)
        m_hat = nm / (1.0 - ADAM_B1 ** ADAM_STEP)
        v_hat = nv / (1.0 - ADAM_B2 ** ADAM_STEP)
        d_ref[...] = -ADAM_LR * (m_hat / (jnp.sqrt(v_hat) + ADAM_EPS) + ADAM_WD * w_ref[...])
        nm_ref[...] = nm
        nv_ref[...] = nv

    spec = pl.BlockSpec((rows, shape[1]), lambda i: (i, 0))
    return pl.pallas_call(
        body, name=name, grid=(shape[0] // rows,), in_specs=[spec] * 4, out_specs=(spec,) * n_out,
        out_shape=(_sds(shape, F32),) * n_out, compiler_params=_cp(("parallel",)),
    )(*_pin(w, g, m, v))


def _rope_tables():
    pos = jnp.arange(SEQ, dtype=F32)
    inv = ROPE_THETA ** (-jnp.arange(0, 64, 2, dtype=F32) / 64)
    ang = pos[:, None] * inv[None, :]
    ang = jnp.concatenate([ang, ang, ang, ang], axis=-1)
    low = (jnp.arange(128) % 64) < 32
    cos, sin = jnp.cos(ang), jnp.sin(ang)
    return cos, jnp.where(low, -sin, 0.0), jnp.where(low, 0.0, sin)


def _local_step(x2, mem2, tgt2, winT, wout, wmem, b_in, sinks, g_branch, ln_gain, ln_bias):
    cos, sa, sb = _rope_tables()
    sinkv = jnp.pad(sinks, ((0, 0), (0, 120)))
    head_of_lane = jnp.arange(512)[None, :] // 64
    gather8 = (head_of_lane.T == jnp.arange(128)[None, :]).astype(BF16)
    gather4 = gather8[:W_B]
    spread4 = gather4.T

    xb, qa, ka, va, bn, b4, b16, qc, z, wout, wmem = _in_proj(x2, winT, b_in, cos, sa, sb, wout, wmem)
    memb, mkv = _mem_kv(mem2, wmem)
    b4f, b16f = b4.reshape(T, 768), b16.reshape(T, 768)

    swa = dict(kind="band", nb=SEQ // BLK, max_dist=BLK - 1, gqa=True)
    dil = (dict(kind="band", nb=SEQ // BLK), dict(kind="band", nb=SEQ // 4 // BLK), dict(kind="band", nb=1))
    oa, lse_a = _attn_fwd("swa_fwd", qa, 0, W_A, ka, 0, va, 0, W_KV_A, sinks=sinks, **swa)
    o1, l1 = _attn_fwd("dil1_fwd", bn, 0, W_B, bn, 1, bn, 2, W_B, **dil[0])
    o4, l4 = _attn_fwd("dil4_fwd", b4f, 0, W_B, b4f, 1, b4f, 2, W_B, **dil[1])
    o16, l16 = _attn_fwd("dil16_fwd", b16f, 0, W_B, b16f, 1, b16f, 2, W_B, **dil[2])
    oc, lse_c = _attn_fwd("mem_fwd", qc, 0, W_C, mkv, 0, mkv, 1, W_C, kind="mem")

    s4 = lambda w: (B_LOC, 4, SEQ // 4, w)
    s16 = lambda w: (B_LOC, 16, SEQ // 16, w)
    (du, dz, doa, dla, dobn, lsen, dlbn, dob4, lse4, dlb4, dob16, lse16, dlb16, doc, dlc, acc, g_out) = _middle(
        oa, o1, l1, o4.reshape(s4(W_B)), l4.reshape(s4(128)), o16.reshape(s16(W_B)), l16.reshape(s16(128)), oc, z,
        x2, tgt2, g_branch, ln_gain, ln_bias, wout, spread4, gather4, gather8)

    flat = lambda a: a.reshape(T, a.shape[-1])
    dqa, dka, dva, dsink = _attn_bwd("swa_bwd", qa, 0, W_A, ka, 0, va, 0, W_KV_A, doa, lse_a, dla, sinkv=sinkv,
                                     **swa)
    dqn, dkn, dvn = _attn_bwd("dil1_bwd", bn, 0, W_B, bn, 1, bn, 2, W_B, dobn, lsen, dlbn, **dil[0])
    dq4, dk4, dv4 = _attn_bwd("dil4_bwd", b4f, 0, W_B, b4f, 1, b4f, 2, W_B, flat(dob4), flat(lse4), flat(dlb4),
                              **dil[1])
    dq16, dk16, dv16 = _attn_bwd("dil16_bwd", b16f, 0, W_B, b16f, 1, b16f, 2, W_B, flat(dob16), flat(lse16),
                                 flat(dlb16), **dil[2])
    dqc, dmkv = _attn_bwd("mem_bwd", qc, 0, W_C, mkv, 0, mkv, 1, W_C, doc, lse_c, dlc, kind="mem")

    r4 = lambda a: a.reshape(s4(W_B))
    r16 = lambda a: a.reshape(s16(W_B))
    gx, dbin, g_in = _dh_dx(dqa, dka, dva, dqn, dkn, dvn, r4(dq4), r4(dk4), r4(dv4), r16(dq16), r16(dk16),
                            r16(dv16), dqc, dz, du, xb, cos, sa, sb, winT)
    g_mem = _tn_matmul("dw_mem", memb, dmkv, D_MODEL, B_LOC * MEM_LEN)
    return gx, g_in, g_out, g_mem, acc, dbin, dsink


def kernel(x, mem, w_in, b_in, w_mem, attn_sinks, g_branch, w_out, ln_gain, ln_bias, loss_target, m_w_in, m_b_in, m_w_mem, m_attn_sinks, m_g_branch, m_w_out, m_ln_gain, m_ln_bias, v_w_in, v_b_in, v_w_mem, v_attn_sinks, v_g_branch, v_w_out, v_ln_gain, v_ln_bias):
    winT, wout, wmem = _gather_weights(w_in[0].T, w_out[0], w_mem[0])
    gx, g_in, g_out, g_mem, acc, dbin, dsink = _local_step(
        x.reshape(T, D_MODEL), mem.reshape(B_LOC * MEM_LEN, D_MODEL), loss_target.reshape(T, D_MODEL),
        winT, wout, wmem, b_in, attn_sinks, g_branch, ln_gain, ln_bias)
    r_in, r_out, r_mem, sv = _reduce_grads(g_in, g_out, g_mem, acc, dbin, dsink)

    loss = jnp.sum(sv[0, :D_MODEL])
    grads = {
        "b_in": sv[4:5, :D_IN], "w_mem": r_mem[None],
        "attn_sinks": -sv[5:6, 0:8], "g_branch": sv[1:2, :D_MODEL], "w_out": r_out[None],
        "ln_gain": sv[2:3, :D_MODEL], "ln_bias": sv[3:4, :D_MODEL],
    }
    weights = dict(w_in=w_in, b_in=b_in, w_mem=w_mem, attn_sinks=attn_sinks, g_branch=g_branch, w_out=w_out,
                   ln_gain=ln_gain, ln_bias=ln_bias)
    ms = dict(w_in=m_w_in, b_in=m_b_in, w_mem=m_w_mem, attn_sinks=m_attn_sinks, g_branch=m_g_branch, w_out=m_w_out,
              ln_gain=m_ln_gain, ln_bias=m_ln_bias)
    vs = dict(w_in=v_w_in, b_in=v_b_in, w_mem=v_w_mem, attn_sinks=v_attn_sinks, g_branch=v_g_branch, w_out=v_w_out,
              ln_gain=v_ln_gain, ln_bias=v_ln_bias)
    names = ["w_in", "b_in", "w_mem", "attn_sinks", "g_branch", "w_out", "ln_gain", "ln_bias"]
    deltas, new_m, new_v = [], [], []
    for n in names:
        shape = weights[n].shape
        two_d = lambda a: a.reshape(shape[-2], shape[-1])
        if n == "w_in":
            d, nm, nv, gw = (a.T for a in _adamw("adamw_w_in", w_in[0].T, r_in, m_w_in[0].T, v_w_in[0].T, SH_IN // 4,
                                                 copy_g=True))
            grads[n] = gw
        elif n in ("w_out", "w_mem"):
            d, nm, nv, grads[n] = _adamw("adamw_" + n, two_d(weights[n]), two_d(grads[n]), two_d(ms[n]), two_d(vs[n]),
                                         copy_g=True)
        else:
            d, nm, nv = _adamw("adamw_" + n, two_d(weights[n]), two_d(grads[n]), two_d(ms[n]), two_d(vs[n]))
        deltas.append(d.reshape(shape))
        new_m.append(nm.reshape(shape))
        new_v.append(nv.reshape(shape))
    return (loss, gx.reshape(B_LOC, SEQ, D_MODEL), *[grads[n].reshape(weights[n].shape) for n in names],
            *deltas, *new_m, *new_v)
```

```python
import functools

import jax
import jax.numpy as jnp
from jax import lax
from jax.experimental import pallas as pl
from jax.experimental.pallas import tpu as pltpu

F32, BF16 = jnp.float32, jnp.bfloat16

D_MODEL = 1024
SEQ = 2048
B_LOC = 2
T = B_LOC * SEQ
BLK = 128
MEM_LEN = 256
W_A, W_KV_A, W_B, W_C, D_MIX = 512, 128, 256, 256, 1024
D_IN = 2816
O_QA, O_KA, O_VA, O_QB, O_KB, O_VB, O_QC, O_Z = 0, 512, 640, 768, 1024, 1280, 1536, 1792
ROPE_THETA = 10000.0
LN_EPS = 1e-5
RMS_EPS = 1e-6
ALPHA = 2.0 ** 0.25
QK_SCALE = 0.125
N_CHIP = 4
SH_IN, SH_OUT, SH_MEM = D_IN // N_CHIP, D_MIX // N_CHIP, D_MODEL // N_CHIP
NEG = -1e30
ADAM_LR, ADAM_B1, ADAM_B2, ADAM_EPS, ADAM_WD, ADAM_STEP = 0.001, 0.9, 0.999, 1e-08, 0.01, 10
SV_W = 3072
MESH = pl.DeviceIdType.MESH

NN = ((1,), (0,))
NT = ((1,), (1,))
TN = ((0,), (0,))


def _dot(a, b, dims):
    return lax.dot_general(a, b, (dims, ((), ())), preferred_element_type=F32)


def _cp(sem=None, vmem_mb=None):
    kw = {}
    if sem is not None:
        kw["dimension_semantics"] = sem
    if vmem_mb is not None:
        kw["vmem_limit_bytes"] = vmem_mb * 1024 * 1024
    return pltpu.CompilerParams(**kw)


def _sds(shape, dtype):
    return pltpu.HBM(shape, dtype)


def _vm_sds(shape, dtype):
    return jax.ShapeDtypeStruct(shape, dtype)


def _pin(*args):
    return [pltpu.with_memory_space_constraint(a, pltpu.HBM) for a in args]


def _full(shape):
    n = len(shape)
    return pl.BlockSpec(shape, lambda *_: (0,) * n)


def _shard_rows(ref, n, chip, half):
    start = pl.multiple_of((2 * chip[0] + chip[1]) * n + half * (n // 2), 16)
    return ref.at[pl.ds(start, n // 2), :]


def _gather_weights(win_sh, wout_sh, wmem_sh):
    def body(a_ref, b_ref, c_ref, oa_ref, ob_ref, oc_ref, ici_send, ici_recv, d2d_send, d2d_recv):
        x, y, c = lax.axis_index("x"), lax.axis_index("y"), lax.axis_index("c")
        sibling = (x, y, 1 - c)
        chips = [(1 - x, y), (x, 1 - y), (1 - x, 1 - y)]
        for src, out, n in ((a_ref, oa_ref, SH_IN), (b_ref, ob_ref, SH_OUT), (c_ref, oc_ref, SH_MEM)):
            out[pl.ds(pl.multiple_of((2 * x + y) * n, 16), n), :] = src[...].astype(BF16)

        def copy(sems, j, chip_of_block, half, to):
            blk = _shard_rows(oa_ref, SH_IN, chip_of_block, half)
            return pltpu.make_async_remote_copy(
                src_ref=blk, dst_ref=blk, send_sem=sems[0].at[j], recv_sem=sems[1].at[j],
                device_id=to, device_id_type=MESH)

        ici, d2d = (ici_send, ici_recv), (d2d_send, d2d_recv)
        first = [copy(ici, j, (x, y), c, (*chip, c)) for j, chip in enumerate(chips)]
        for cp in first:
            cp.start()
        passed = []
        for j, chip in enumerate(chips):
            copy(ici, j, chip, c, (x, y, c)).wait_recv()
            fw = copy(d2d, j, chip, c, sibling)
            fw.start()
            passed.append(fw)
        for j, chip in enumerate(chips):
            copy(d2d, j, chip, 1 - c, (x, y, c)).wait_recv()
        for cp in first + passed:
            cp.wait_send()

    vm = pl.BlockSpec(memory_space=pltpu.VMEM)
    return pl.pallas_call(
        body, name="gather_weights",
        out_shape=(_vm_sds((D_IN, D_MODEL), BF16), _vm_sds((D_MIX, D_MODEL), BF16),
                   _vm_sds((D_MODEL, 2 * W_C), BF16)),
        in_specs=[vm, vm, vm], out_specs=(vm, vm, vm),
        scratch_shapes=[pltpu.SemaphoreType.DMA((3,))] * 4,
        compiler_params=_cp(vmem_mb=40),
    )(win_sh, wout_sh, wmem_sh)


def _rope(t, cos, sa, sb, sign):
    w = t.shape[1]
    reps = w // 128
    c, a, b = (jnp.tile(v, (1, reps)) if reps > 1 else v for v in (cos, sa, sb))
    rot = pltpu.roll(t, w - 32, 1) * a + pltpu.roll(t, 32, 1) * b
    return t * c + rot if sign > 0 else t * c - rot


def _in_proj(x, winT, b_in, cos, sa, sb, wout_own, wmem_own):
    tm = 256
    spt = SEQ // tm
    n_steps = T // tm
    forward_step = n_steps // 2

    def body(x_ref, w_ref, b_ref, cos_ref, sa_ref, sb_ref, wo_in, wm_in,
             xb_ref, qa_ref, ka_ref, va_ref, bn_ref, b4_ref, b16_ref, qc_ref, z_ref, wo_ref, wm_ref,
             scr, ici_send, ici_recv, d2d_send, d2d_recv):
        i = pl.program_id(0)
        mx, my, mc = lax.axis_index("x"), lax.axis_index("y"), lax.axis_index("c")
        chips = [(1 - mx, my), (mx, 1 - my), (1 - mx, 1 - my)]
        full = ((wo_ref, SH_OUT), (wm_ref, SH_MEM))

        def copy(sems, a, j, chip_of_block, half, to):
            blk = _shard_rows(full[a][0], full[a][1], chip_of_block, half)
            return pltpu.make_async_remote_copy(
                src_ref=blk, dst_ref=blk, send_sem=sems[0].at[a, j], recv_sem=sems[1].at[a, j],
                device_id=to, device_id_type=MESH)

        ici, d2d = (ici_send, ici_recv), (d2d_send, d2d_recv)
        pairs = [(a, j, chip) for j, chip in enumerate(chips) for a in range(2)]

        @pl.when(i == 0)
        def _():
            for a, j, chip in pairs:
                copy(ici, a, j, (mx, my), mc, (*chip, mc)).start()

        @pl.when(i == forward_step)
        def _():
            for a, j, chip in pairs:
                copy(ici, a, j, chip, mc, (mx, my, mc)).wait_recv()
                copy(d2d, a, j, chip, mc, (mx, my, 1 - mc)).start()

        @pl.when(i == n_steps - 1)
        def _():
            for a, j, chip in pairs:
                copy(d2d, a, j, chip, 1 - mc, (mx, my, mc)).wait_recv()
            for a, j, chip in pairs:
                copy(ici, a, j, (mx, my), mc, (*chip, mc)).wait_send()
                copy(d2d, a, j, chip, mc, (mx, my, 1 - mc)).wait_send()

        xb = x_ref[...].astype(BF16)
        xb_ref[...] = xb
        cos_t, sa_t, sb_t = cos_ref[...], sa_ref[...], sb_ref[...]

        def proj(r0, n):
            return _dot(xb, w_ref[r0:r0 + n, :], NT) + b_ref[:, r0:r0 + n]

        def rope(t):
            return _rope(t, cos_t, sa_t, sb_t, +1)

        qa_ref[...] = (rope(proj(O_QA, W_A)) * QK_SCALE).astype(BF16)
        ka_ref[...] = rope(proj(O_KA, W_KV_A)).astype(BF16)
        va_ref[...] = proj(O_VA, W_KV_A).astype(BF16)
        qc_ref[...] = (proj(O_QC, W_C) * QK_SCALE).astype(BF16)
        z_ref[...] = proj(O_Z, D_MIX).astype(BF16)
        parts = (rope(proj(O_QB, W_B)) * QK_SCALE, rope(proj(O_KB, W_B)), proj(O_VB, W_B))
        for k, part in enumerate(parts):
            bn_ref[:, 256 * k:256 * (k + 1)] = part.astype(BF16)
            scr[2 * k] = part[:, :128]
            scr[2 * k + 1] = part[:, 128:]
        for j in range(6):
            for res in range(4):
                b4_ref[0, res, :, 128 * j:128 * (j + 1)] = scr[j, pl.ds(res, tm // 4, stride=4), :].astype(BF16)
            for res in range(16):
                b16_ref[0, res, :, 128 * j:128 * (j + 1)] = scr[j, pl.ds(res, tm // 16, stride=16), :].astype(BF16)

    tok = lambda w: pl.BlockSpec((tm, w), lambda i: (i, 0))
    tab = pl.BlockSpec((tm, 128), lambda i: (i % spt, 0))
    hbm = pl.BlockSpec(memory_space=pl.ANY)
    return pl.pallas_call(
        body, name="in_proj", grid=(n_steps,),
        in_specs=[tok(D_MODEL), _full((D_IN, D_MODEL)), _full((1, D_IN)), tab, tab, tab, hbm, hbm],
        out_specs=(tok(D_MODEL), tok(W_A), tok(W_KV_A), tok(W_KV_A), tok(768),
                   pl.BlockSpec((1, 4, tm // 4, 768), lambda i: (i // spt, 0, i % spt, 0)),
                   pl.BlockSpec((1, 16, tm // 16, 768), lambda i: (i // spt, 0, i % spt, 0)),
                   tok(W_C), tok(D_MIX), hbm, hbm),
        out_shape=(_sds((T, D_MODEL), BF16), _sds((T, W_A), BF16), _sds((T, W_KV_A), BF16), _sds((T, W_KV_A), BF16),
                   _sds((T, 768), BF16), _sds((B_LOC, 4, SEQ // 4, 768), BF16), _sds((B_LOC, 16, SEQ // 16, 768), BF16),
                   _sds((T, W_C), BF16), _sds((T, D_MIX), BF16),
                   _sds((D_MIX, D_MODEL), BF16), _sds((D_MODEL, 2 * W_C), BF16)),
        input_output_aliases={6: 9, 7: 10},
        scratch_shapes=[pltpu.VMEM((6, tm, 128), F32)] + [pltpu.SemaphoreType.DMA((2, 3))] * 4,
        compiler_params=_cp(("arbitrary",), vmem_mb=48),
    )(*_pin(x, winT, b_in, cos, sa, sb, wout_own, wmem_own))


def _mem_kv(mem, wmem):
    def body(m_ref, w_ref, mb_ref, kv_ref):
        mb = m_ref[...].astype(BF16)
        mb_ref[...] = mb
        kv_ref[...] = _dot(mb, w_ref[...], NN).astype(BF16)

    n = B_LOC * MEM_LEN
    return pl.pallas_call(
        body, name="mem_kv",
        out_shape=(_sds((n, D_MODEL), BF16), _sds((n, 2 * W_C), BF16)),
    )(*_pin(mem, wmem))


class _Part:
    def __init__(self, body, args, in_specs, out_specs, out_shape):
        self.body, self.args, self.in_specs, self.out_specs, self.out_shape = body, args, in_specs, out_specs, out_shape


def _run_parts(name, parts, semantics, vmem_mb):
    n_in = [len(p.args) for p in parts]
    n_out = [len(p.out_shape) for p in parts]

    def body(*refs):
        ins, outs = refs[:sum(n_in)], refs[sum(n_in):]
        i0 = o0 = 0
        for p, ni, no in zip(parts, n_in, n_out):
            p.body(*ins[i0:i0 + ni], *outs[o0:o0 + no])
            i0, o0 = i0 + ni, o0 + no

    res = pl.pallas_call(
        body, name=name, grid=(T // QR,),
        in_specs=[sp for p in parts for sp in p.in_specs], out_specs=tuple(sp for p in parts for sp in p.out_specs),
        out_shape=tuple(sh for p in parts for sh in p.out_shape),
        compiler_params=_cp((semantics,), vmem_mb=vmem_mb),
    )(*_pin(*[a for p in parts for a in p.args]))
    out, o0 = [], 0
    for no in n_out:
        out.append(tuple(res[o0:o0 + no]))
        o0 += no
    return out


QB = 8
QR = QB * BLK


def _lane_lo():
    return lax.broadcasted_iota(jnp.int32, (1, 128), 1) < 64


def _dup_head(k2, hk, lo):
    kf = k2.astype(F32)
    r = pltpu.roll(kf, 64, 1)
    return (jnp.where(lo, kf, r) if hk == 0 else jnp.where(lo, r, kf)).astype(BF16)


def _stack_heads(pairs, lo):
    parts = []
    for x2 in pairs:
        z = jnp.zeros_like(x2)
        parts += [jnp.where(lo, x2, z), jnp.where(lo, z, x2)]
    return jnp.concatenate(parts, axis=0)


def _prev_mode(kind, nb, j):
    if kind == "mem" or nb == 1:
        return "no"
    if nb <= QB:
        return "yes" if j % nb else "no"
    return "yes" if j else "dyn"


class _Attn:
    def __init__(self, kind, nb, max_dist, gqa, qw, kvw, qcb, kcb, vcb):
        self.kind, self.nb, self.gqa, self.qw, self.kvw = kind, nb, gqa, qw, kvw
        npairs = qw // 128
        self.groups = ([(hk, [2 * hk, 2 * hk + 1]) for hk in range(npairs // 2)] if gqa
                       else [(p, [p]) for p in range(npairs)])
        self.nh = 2 * len(self.groups[0][1])
        self.cols = 128 * self.nh
        self.reach = BLK - max_dist
        self.ext_prev = kind == "band" and nb > QB
        self.q_spec = pl.BlockSpec((QR, qw), lambda g: (g, qcb))
        self.row_spec = pl.BlockSpec((QR, qw), lambda g: (g, 0))
        self.stat_spec = pl.BlockSpec((QR, 128), lambda g: (g, 0))
        if kind == "mem":
            per = SEQ // QR
            self.kv_specs = [pl.BlockSpec((MEM_LEN, kvw), lambda g: (g // per, kcb)),
                             pl.BlockSpec((MEM_LEN, kvw), lambda g: (g // per, vcb))]
        else:
            self.kv_specs = [pl.BlockSpec((QR, kvw), lambda g: (g, kcb)), pl.BlockSpec((QR, kvw), lambda g: (g, vcb))]
            if self.ext_prev:
                self.kv_specs += [pl.BlockSpec((BLK, kvw), lambda g: (jnp.maximum(g * QB - 1, 0), kcb)),
                                  pl.BlockSpec((BLK, kvw), lambda g: (jnp.maximum(g * QB - 1, 0), vcb))]

    def masks(self):
        if self.kind == "mem":
            return None
        kj = lax.broadcasted_iota(jnp.int32, (2 * BLK, self.cols), 0)
        qi = lax.broadcasted_iota(jnp.int32, (2 * BLK, self.cols), 1) & (BLK - 1)
        kj1 = lax.broadcasted_iota(jnp.int32, (BLK, self.cols), 0)
        qi1 = lax.broadcasted_iota(jnp.int32, (BLK, self.cols), 1) & (BLK - 1)
        return kj, qi, kj1 <= qi1

    def keys(self, j, gi, kc_ref, vc_ref, kp_ref, vp_ref, lo, kq, g):
        def kv(k_ref, v_ref, r):
            if self.gqa:
                return _dup_head(k_ref[r, :], gi, lo), _dup_head(v_ref[r, :], gi, lo)
            sl = slice(128 * gi, 128 * (gi + 1))
            return k_ref[r, sl], v_ref[r, sl]

        if self.kind == "mem":
            key0 = pl.multiple_of((g // (SEQ // QR)) * MEM_LEN, MEM_LEN)
            return (*kv(kc_ref, vc_ref, slice(None)), None, [(0, MEM_LEN, key0)])
        kj, qi, cur = kq
        row0 = g * QR + BLK * j
        mode = _prev_mode(self.kind, self.nb, j)
        if mode == "no":
            return (*kv(kc_ref, vc_ref, slice(BLK * j, BLK * (j + 1))), cur, [(0, BLK, pl.multiple_of(row0, BLK))])
        if mode == "yes":
            mask = jnp.logical_and(kj >= qi + self.reach, kj <= qi + BLK)
            return (*kv(kc_ref, vc_ref, slice(BLK * (j - 1), BLK * (j + 1))), mask,
                    [(0, 2 * BLK, pl.multiple_of(row0 - BLK, BLK))])
        has_prev = ((g * QB) % self.nb) > 0
        hp = has_prev.astype(jnp.int32)
        mask = jnp.logical_and(kj >= qi * hp + (self.reach * hp + BLK * (1 - hp)), kj <= qi + BLK)
        kp, vp = kv(kp_ref, vp_ref, slice(None))
        kc, vc = kv(kc_ref, vc_ref, slice(0, BLK))
        return (jnp.concatenate([kp, kc], axis=0), jnp.concatenate([vp, vc], axis=0), mask,
                [(0, BLK, pl.multiple_of(jnp.maximum(row0 - BLK, 0), BLK)), (BLK, BLK, pl.multiple_of(row0, BLK))])


def _attn_fwd(q, qcb, qw, k, kcb, v, vcb, kvw, *, kind, nb=1, max_dist=BLK, gqa=False, sinks=None):
    a = _Attn(kind, nb, max_dist, gqa, qw, kvw, qcb, kcb, vcb)

    def body(*refs):
        it = iter(refs)
        q_ref, kc_ref, vc_ref = next(it), next(it), next(it)
        kp_ref, vp_ref = (next(it), next(it)) if a.ext_prev else (None, None)
        sink_ref = next(it) if sinks is not None else None
        o_ref, lse_ref = next(it), next(it)
        g = pl.program_id(0)
        lo = _lane_lo()
        top = lax.broadcasted_iota(jnp.int32, (128, 1), 0) < 64
        rid = lax.broadcasted_iota(jnp.int32, (8, 128), 0)
        kq = a.masks()
        for j in range(QB):
            rows = slice(BLK * j, BLK * (j + 1))
            stat = jnp.zeros((8, 128), F32)
            for gi, pairs in a.groups:
                qs = _stack_heads([q_ref[rows, 128 * p:128 * (p + 1)] for p in pairs], lo)
                kk, vv, mask, _ = a.keys(j, gi, kc_ref, vc_ref, kp_ref, vp_ref, lo, kq, g)
                pieces = [slice(r0, r0 + BLK) for r0 in range(0, kk.shape[0], BLK)]
                ss = []
                for r in pieces:
                    s = _dot(kk[r], qs, NT)
                    ss.append(s if mask is None else jnp.where(mask[r], s, NEG))
                m = jnp.max(ss[0], axis=0, keepdims=True)
                for s in ss[1:]:
                    m = jnp.maximum(m, jnp.max(s, axis=0, keepdims=True))
                if sink_ref is not None:
                    sk = jnp.concatenate([jnp.full((1, 128), sink_ref[0, a.nh * gi + i], F32) for i in range(a.nh)],
                                         axis=1)
                    m = jnp.maximum(m, sk)
                l, ot = None, None
                for r, s in zip(pieces, ss):
                    p = jnp.exp(s - m)
                    ps = jnp.sum(p, axis=0, keepdims=True)
                    c = _dot(vv[r], p.astype(BF16), TN)
                    l, ot = (ps, c) if l is None else (l + ps, ot + c)
                if sink_ref is not None:
                    l = l + jnp.exp(sk - m)
                ot = ot * pl.reciprocal(l, approx=True)
                lse = m + jnp.log(l)
                for i, p in enumerate(pairs):
                    o2t = jnp.where(top, ot[:, 256 * i:256 * i + 128], ot[:, 256 * i + 128:256 * i + 256])
                    o_ref[rows, 128 * p:128 * (p + 1)] = o2t.T.astype(BF16)
                for i in range(a.nh):
                    stat = jnp.where(rid == a.nh * gi + i, lse[:, 128 * i:128 * (i + 1)], stat)
            lse_ref[rows, :] = jnp.concatenate([stat, jnp.zeros((120, 128), F32)], axis=0).T

    args = [q, k, v] + ([k, v] if a.ext_prev else [])
    in_specs = [a.q_spec] + a.kv_specs
    if sinks is not None:
        args.append(sinks)
        in_specs.append(pl.BlockSpec(memory_space=pltpu.SMEM))
    return _Part(body, args, in_specs, [a.row_spec, a.stat_spec], [_sds((T, qw), BF16), _sds((T, 128), F32)])


def _attn_bwd(q, qcb, qw, k, kcb, v, vcb, kvw, do, lse, dl, *, kind, nb=1, max_dist=BLK, gqa=False, sinkv=None):
    a = _Attn(kind, nb, max_dist, gqa, qw, kvw, qcb, kcb, vcb)

    def body(*refs):
        it = iter(refs)
        q_ref, kc_ref, vc_ref = next(it), next(it), next(it)
        kp_ref, vp_ref = (next(it), next(it)) if a.ext_prev else (None, None)
        do_ref, lse_ref, dl_ref = next(it), next(it), next(it)
        sinkv_ref = next(it) if sinkv is not None else None
        dq_ref = next(it)
        if kind == "mem":
            dkv_ref = next(it)
        else:
            dk_ref, dv_ref = next(it), next(it)
        dsink_ref = next(it) if sinkv is not None else None
        g = pl.program_id(0)
        lo = _lane_lo()
        top = lax.broadcasted_iota(jnp.int32, (128, 1), 0) < 64

        @pl.when(g == 0)
        def _():
            if kind == "mem":
                dkv_ref[...] = jnp.zeros_like(dkv_ref)
            else:
                dk_ref[...] = jnp.zeros_like(dk_ref)
                dv_ref[...] = jnp.zeros_like(dv_ref)
            if dsink_ref is not None:
                dsink_ref[...] = jnp.zeros_like(dsink_ref)

        kq = a.masks()
        for j in range(QB):
            rows = slice(BLK * j, BLK * (j + 1))
            lse_t = lse_ref[rows, :].T
            dl_t = dl_ref[rows, :].T
            for gi, pairs in a.groups:
                heads = [a.nh * gi + i for i in range(a.nh)]
                qs = _stack_heads([q_ref[rows, 128 * p:128 * (p + 1)] for p in pairs], lo)
                dos = _stack_heads([do_ref[rows, 128 * p:128 * (p + 1)] for p in pairs], lo)
                lse_row = jnp.concatenate([lse_t[h:h + 1, :] for h in heads], axis=1)
                dl_row = jnp.concatenate([dl_t[h:h + 1, :] for h in heads], axis=1)
                kk, vv, mask, dests = a.keys(j, gi, kc_ref, vc_ref, kp_ref, vp_ref, lo, kq, g)
                s = _dot(kk, qs, NT)
                if mask is not None:
                    s = jnp.where(mask, s, NEG)
                p = jnp.exp(s - lse_row)
                ds = (p * (_dot(vv, dos, NT) - dl_row)).astype(BF16)
                dqt = _dot(kk, ds, TN)
                ck = _dot(ds, qs, NN)
                cv = _dot(p.astype(BF16), dos, NN)
                if gqa:
                    sel = lo if gi == 0 else jnp.logical_not(lo)
                    ck = jnp.where(sel, ck + pltpu.roll(ck, 64, 1), 0.0)
                    cv = jnp.where(sel, cv + pltpu.roll(cv, 64, 1), 0.0)
                    kcols = slice(0, 128)
                else:
                    kcols = slice(128 * gi, 128 * (gi + 1))
                for r0, nr, key0 in dests:
                    krows = pl.ds(key0, nr)
                    if kind == "mem":
                        dkv_ref[krows, kcols] += ck[r0:r0 + nr]
                        dkv_ref[krows, slice(kvw + kcols.start, kvw + kcols.stop)] += cv[r0:r0 + nr]
                    else:
                        dk_ref[krows, kcols] += ck[r0:r0 + nr]
                        dv_ref[krows, kcols] += cv[r0:r0 + nr]
                for i, p in enumerate(pairs):
                    dq2t = jnp.where(top, dqt[:, 256 * i:256 * i + 128], dqt[:, 256 * i + 128:256 * i + 256])
                    dq_ref[rows, 128 * p:128 * (p + 1)] = dq2t.T.astype(BF16)
        if dsink_ref is not None:
            ps = jnp.exp(sinkv_ref[...] - lse_ref[...]) * dl_ref[...]
            dsink_ref[...] += jnp.sum(ps, axis=0, keepdims=True)

    args = [q, k, v] + ([k, v] if a.ext_prev else []) + [do, lse, dl]
    in_specs = [a.q_spec] + a.kv_specs + [a.row_spec, a.stat_spec, a.stat_spec]
    if sinkv is not None:
        args.append(sinkv)
        in_specs.append(_full((1, 128)))
    out_shape = [_sds((T, qw), BF16)]
    out_specs = [a.row_spec]
    once = lambda shape: pl.BlockSpec(shape, lambda g: (0, 0), pipeline_mode=pl.Buffered(1))
    if kind == "mem":
        out_shape.append(_sds((B_LOC * MEM_LEN, 2 * kvw), F32))
        out_specs.append(once((B_LOC * MEM_LEN, 2 * kvw)))
    else:
        out_shape += [_sds((T, kvw), F32)] * 2
        out_specs += [once((T, kvw))] * 2
    if sinkv is not None:
        out_shape.append(_sds((1, 128), F32))
        out_specs.append(_full((1, 128)))
    return _Part(body, args, in_specs, out_specs, out_shape)


def _dot2(v, w_ref):
    hi = v.astype(BF16)
    lo = (v - hi.astype(F32)).astype(BF16)
    return _dot(hi, w_ref[...], NN) + _dot(lo, w_ref[...], NN)


def _middle(oa, o1, l1, o4, l4, o16, l16, oc, z, x, tgt, g_br, ln_g, ln_b, wout, spread4, gather4, gather8):
    tm = 256
    spt = SEQ // tm

    def body(oa_ref, o1_ref, l1_ref, o4_ref, l4_ref, o16_ref, l16_ref, oc_ref, z_ref, x_ref, t_ref,
             g_ref, lg_ref, lb_ref, w_ref, sp4_ref, ga4_ref, ga8_ref,
             du_ref, dz_ref, doa_ref, dla_ref,
             dobn_ref, lsen_ref, dlbn_ref, dob4_ref, lse4_ref, dlb4_ref, dob16_ref, lse16_ref, dlb16_ref,
             doc_ref, dlc_ref, acc_ref, gout_ref, scr):
        i = pl.program_id(0)

        @pl.when(i == 0)
        def _():
            acc_ref[...] = jnp.zeros_like(acc_ref)
            gout_ref[...] = jnp.zeros_like(gout_ref)

        for res in range(4):
            rows = pl.ds(res, tm // 4, stride=4)
            for j in range(2):
                scr[j, rows, :] = o4_ref[0, res, :, 128 * j:128 * (j + 1)].astype(F32)
            scr[2, rows, :] = l4_ref[0, res]
        for res in range(16):
            rows = pl.ds(res, tm // 16, stride=16)
            for j in range(2):
                scr[3 + j, rows, :] = o16_ref[0, res, :, 128 * j:128 * (j + 1)].astype(F32)
            scr[5, rows, :] = l16_ref[0, res]
        cat = lambda a: jnp.concatenate([scr[a], scr[a + 1]], axis=1)
        o1v, o4v, o16v = o1_ref[...].astype(F32), cat(0), cat(3)
        l1v, l4v, l16v = l1_ref[...], scr[2], scr[5]
        mx = jnp.maximum(jnp.maximum(l1v, l4v), l16v)
        e1, e4, e16 = jnp.exp(l1v - mx), jnp.exp(l4v - mx), jnp.exp(l16v - mx)
        ssum = e1 + e4 + e16
        lse_b = mx + jnp.log(ssum)
        inv = 1.0 / ssum
        ob = (_dot2(e1 * inv, sp4_ref) * o1v + _dot2(e4 * inv, sp4_ref) * o4v + _dot2(e16 * inv, sp4_ref) * o16v)
        oav, ocv = oa_ref[...].astype(F32), oc_ref[...].astype(F32)

        def rms(o):
            r = lax.rsqrt(jnp.sum(o * o, axis=1, keepdims=True) * (1.0 / o.shape[1]) + RMS_EPS)
            return o * r, r

        na, ra = rms(oav)
        nb_, rb = rms(ob)
        nc, rc = rms(ocv)
        n = jnp.concatenate([na, nb_, nc], axis=1)
        zf = z_ref[...].astype(F32)
        sig = 1.0 / (1.0 + jnp.exp(-zf))
        sz = zf * sig
        gb = g_ref[...]
        yb = (n * gb * sz).astype(BF16)
        u = ALPHA * x_ref[...] + _dot(yb, w_ref[...], NN)
        inv_d = 1.0 / D_MODEL
        mu = jnp.sum(u, axis=1, keepdims=True) * inv_d
        uc = u - mu
        rstd = lax.rsqrt(jnp.sum(uc * uc, axis=1, keepdims=True) * inv_d + LN_EPS)
        xh = uc * rstd
        lg = lg_ref[...]
        diff = xh * lg + lb_ref[...] - t_ref[...]
        acc_ref[0:1, :] += jnp.sum(diff * diff, axis=0, keepdims=True) * (0.5 * inv_d)
        dout = diff * inv_d
        acc_ref[2:3, :] += jnp.sum(dout * xh, axis=0, keepdims=True)
        acc_ref[3:4, :] += jnp.sum(dout, axis=0, keepdims=True)
        dxh = dout * lg
        du = rstd * (dxh - jnp.sum(dxh, axis=1, keepdims=True) * inv_d
                     - xh * (jnp.sum(dxh * xh, axis=1, keepdims=True) * inv_d))
        dub = du.astype(BF16)
        du_ref[...] = dub
        gout_ref[...] += _dot(yb, dub, TN)
        dy = _dot(dub, w_ref[...], NT)
        t1 = dy * sz
        acc_ref[1:2, :] += jnp.sum(t1 * n, axis=0, keepdims=True)
        dn = t1 * gb
        dz_ref[...] = (dy * n * gb * (sig * (1.0 + zf * (1.0 - sig)))).astype(BF16)

        def rms_bwd(dn_, n_, r):
            return r * (dn_ - n_ * (jnp.sum(dn_ * n_, axis=1, keepdims=True) * (1.0 / n_.shape[1])))

        doa = rms_bwd(dn[:, :W_A], na, ra)
        dob = rms_bwd(dn[:, W_A:W_A + W_B], nb_, rb)
        doc = rms_bwd(dn[:, W_A + W_B:], nc, rc)
        doa_ref[...] = doa.astype(BF16)
        dla_ref[...] = _dot2(doa * oav, ga8_ref)
        doc_ref[...] = doc.astype(BF16)
        dlc_ref[...] = _dot2(doc * ocv, ga4_ref)
        dlb = _dot2(dob * ob, ga4_ref)
        dobn_ref[...] = dob.astype(BF16)
        lsen_ref[...] = lse_b
        dlbn_ref[...] = dlb
        scr[0] = dob[:, :128]
        scr[1] = dob[:, 128:]
        scr[2] = lse_b
        scr[3] = dlb
        for res in range(4):
            rows = pl.ds(res, tm // 4, stride=4)
            for j in range(2):
                dob4_ref[0, res, :, 128 * j:128 * (j + 1)] = scr[j, rows, :].astype(BF16)
            lse4_ref[0, res] = scr[2, rows, :]
            dlb4_ref[0, res] = scr[3, rows, :]
        for res in range(16):
            rows = pl.ds(res, tm // 16, stride=16)
            for j in range(2):
                dob16_ref[0, res, :, 128 * j:128 * (j + 1)] = scr[j, rows, :].astype(BF16)
            lse16_ref[0, res] = scr[2, rows, :]
            dlb16_ref[0, res] = scr[3, rows, :]

    tok = lambda w: pl.BlockSpec((tm, w), lambda i: (i, 0))
    p4 = lambda w: pl.BlockSpec((1, 4, tm // 4, w), lambda i: (i // spt, 0, i % spt, 0))
    p16 = lambda w: pl.BlockSpec((1, 16, tm // 16, w), lambda i: (i // spt, 0, i % spt, 0))
    s4 = lambda w, dt: _sds((B_LOC, 4, SEQ // 4, w), dt)
    s16 = lambda w, dt: _sds((B_LOC, 16, SEQ // 16, w), dt)
    row = _full((1, D_MODEL))
    return pl.pallas_call(
        body, name="middle", grid=(T // tm,),
        in_specs=[tok(W_A), tok(W_B), tok(128), p4(W_B), p4(128), p16(W_B), p16(128), tok(W_C), tok(D_MIX),
                  tok(D_MODEL), tok(D_MODEL), row, row, row, _full((D_MIX, D_MODEL)),
                  _full((128, W_B)), _full((W_B, 128)), _full((W_A, 128))],
        out_specs=(tok(D_MODEL), tok(D_MIX), tok(W_A), tok(128),
                   tok(W_B), tok(128), tok(128), p4(W_B), p4(128), p4(128), p16(W_B), p16(128), p16(128),
                   tok(W_C), tok(128), _full((8, D_MODEL)), _full((D_MIX, D_MODEL))),
        out_shape=(_sds((T, D_MODEL), BF16), _sds((T, D_MIX), BF16),
                   _sds((T, W_A), BF16), _sds((T, 128), F32),
                   _sds((T, W_B), BF16), _sds((T, 128), F32), _sds((T, 128), F32),
                   s4(W_B, BF16), s4(128, F32), s4(128, F32), s16(W_B, BF16), s16(128, F32), s16(128, F32),
                   _sds((T, W_C), BF16), _sds((T, 128), F32), _sds((8, D_MODEL), F32),
                   _sds((D_MIX, D_MODEL), F32)),
        scratch_shapes=[pltpu.VMEM((6, tm, 128), F32)],
        compiler_params=_cp(("arbitrary",), vmem_mb=56),
    )(*_pin(oa, o1, l1, o4, l4, o16, l16, oc, z, x, tgt, g_br, ln_g, ln_b, wout, spread4, gather4, gather8))


def _dh_dx(dqa, dka, dva, dqn, dkn, dvn, dq4, dk4, dv4, dq16, dk16, dv16, dqc, dz, du, xb, cos, sa, sb, winT):
    tm = 512
    spt = SEQ // tm

    def body(dqa_ref, dka_ref, dva_ref, dqn_ref, dkn_ref, dvn_ref, dq4_ref, dk4_ref, dv4_ref,
             dq16_ref, dk16_ref, dv16_ref, dqc_ref, dz_ref, du_ref, xb_ref, cos_ref, sa_ref, sb_ref, w_ref,
             gx_ref, db_ref, gin_ref, dh_ref, scr):
        i = pl.program_id(0)

        @pl.when(i == 0)
        def _():
            db_ref[...] = jnp.zeros_like(db_ref)
            gin_ref[...] = jnp.zeros_like(gin_ref)

        cos_t, sa_t, sb_t = cos_ref[...], sa_ref[...], sb_ref[...]

        def rope_t(t):
            return _rope(t, cos_t, sa_t, sb_t, -1)

        def put(r0, val):
            n = val.shape[1]
            dh_ref[:, r0:r0 + n] = val.astype(BF16)
            db_ref[:, r0:r0 + n] += jnp.sum(val, axis=0, keepdims=True)

        put(O_QA, rope_t(dqa_ref[...].astype(F32)) * QK_SCALE)
        put(O_KA, rope_t(dka_ref[...]))
        put(O_VA, dva_ref[...])
        put(O_QC, dqc_ref[...].astype(F32) * QK_SCALE)
        put(O_Z, dz_ref[...].astype(F32))
        for k, (n_ref, r4, r16) in enumerate(((dqn_ref, dq4_ref, dq16_ref), (dkn_ref, dk4_ref, dk16_ref),
                                               (dvn_ref, dv4_ref, dv16_ref))):
            for j in range(2):
                sl = slice(128 * j, 128 * (j + 1))
                scr[2 * k + j] = n_ref[:, sl].astype(F32)
                for res in range(4):
                    scr[2 * k + j, pl.ds(res, tm // 4, stride=4), :] += r4[0, res, :, sl].astype(F32)
                for res in range(16):
                    scr[2 * k + j, pl.ds(res, tm // 16, stride=16), :] += r16[0, res, :, sl].astype(F32)
        cat = lambda a: jnp.concatenate([scr[a], scr[a + 1]], axis=1)
        put(O_QB, rope_t(cat(0)) * QK_SCALE)
        put(O_KB, rope_t(cat(2)))
        put(O_VB, cat(4))
        gx_ref[...] = _dot(dh_ref[...], w_ref[...], NN) + ALPHA * du_ref[...].astype(F32)
        gin_ref[...] += _dot(dh_ref[...], xb_ref[...], TN)

    tok = lambda w: pl.BlockSpec((tm, w), lambda i: (i, 0))
    tab = pl.BlockSpec((tm, 128), lambda i: (i % spt, 0))
    p4 = pl.BlockSpec((1, 4, tm // 4, W_B), lambda i: (i // spt, 0, i % spt, 0))
    p16 = pl.BlockSpec((1, 16, tm // 16, W_B), lambda i: (i // spt, 0, i % spt, 0))
    once = lambda shape: pl.BlockSpec(shape, lambda i: (0, 0), pipeline_mode=pl.Buffered(1))
    return pl.pallas_call(
        body, name="dh_dx", grid=(T // tm,),
        in_specs=[tok(W_A), tok(W_KV_A), tok(W_KV_A), tok(W_B), tok(W_B), tok(W_B), p4, p4, p4, p16, p16, p16,
                  tok(W_C), tok(D_MIX), tok(D_MODEL), tok(D_MODEL), tab, tab, tab, once((D_IN, D_MODEL))],
        out_specs=(tok(D_MODEL), _full((1, D_IN)), once((D_IN, D_MODEL))),
        out_shape=(_sds((T, D_MODEL), F32), _sds((1, D_IN), F32), _sds((D_IN, D_MODEL), F32)),
        scratch_shapes=[pltpu.VMEM((tm, D_IN), BF16), pltpu.VMEM((6, tm, 128), F32)],
        compiler_params=_cp(("arbitrary",), vmem_mb=56),
    )(*_pin(dqa, dka, dva, dqn, dkn, dvn, dq4, dk4, dv4, dq16, dk16, dv16, dqc, dz, du, xb, cos, sa, sb, winT))


def _tn_matmul(name, a, b, bm, bt):
    n, m_all = a.shape
    n_cols = b.shape[1]

    def body(a_ref, b_ref, o_ref):
        @pl.when(pl.program_id(1) == 0)
        def _():
            o_ref[...] = jnp.zeros_like(o_ref)

        o_ref[...] += _dot(a_ref[...].astype(BF16), b_ref[...].astype(BF16), TN)

    return pl.pallas_call(
        body, name=name, grid=(m_all // bm, n // bt),
        in_specs=[pl.BlockSpec((bt, bm), lambda m, t: (t, m)), pl.BlockSpec((bt, n_cols), lambda m, t: (t, 0))],
        out_specs=pl.BlockSpec((bm, n_cols), lambda m, t: (m, 0)),
        out_shape=_sds((m_all, n_cols), F32),
        compiler_params=_cp(("parallel", "arbitrary"), vmem_mb=48),
    )(*_pin(a, b))


def _reduce_grads(g_in, g_out, g_mem, acc, dbin, dsink):
    shard_rows = (SH_IN, SH_OUT, SH_MEM)
    widths = (D_MODEL, D_MODEL, 2 * W_C)

    def body(ga_ref, gb_ref, gc_ref, acc_ref, dbin_ref, dsink_ref,
             ra_ref, rb_ref, rc_ref, sv_ref,
             sib_a, sib_b, sib_c, stage_a, stage_b, stage_c, land_a, land_b, land_c, sv_mine, sv_all,
             s1_send, s1_recv, s2_send, s2_recv, s3_send, s3_recv, sv_send, sv_recv):
        x, y, c = lax.axis_index("x"), lax.axis_index("y"), lax.axis_index("c")
        me, sibling = (x, y, c), (x, y, 1 - c)
        my_chip = 2 * x + y
        chips = [(1 - x, y), (x, 1 - y), (1 - x, 1 - y)]
        grads = (ga_ref, gb_ref, gc_ref)
        sibs = (sib_a, sib_b, sib_c)
        stages = (stage_a, stage_b, stage_c)
        lands = (land_a, land_b, land_c)
        res = (ra_ref, rb_ref, rc_ref)

        def half_rows(a, chip_idx, half):
            n = shard_rows[a]
            return pl.ds(pl.multiple_of(chip_idx * n + half * (n // 2), 16), n // 2)

        sv_mine[...] = jnp.zeros_like(sv_mine)
        sv_mine[0:4, 0:D_MODEL] = acc_ref[0:4, :]
        sv_mine[4:5, 0:D_IN] = dbin_ref[...]
        sv_mine[5:6, 0:128] = dsink_ref[...]
        my_dev = 4 * x + 2 * y + c
        others = [(x, y, 1 - c)] + [(*chip, cc) for chip in chips for cc in (c, 1 - c)]

        def sv_copy(j, to):
            return pltpu.make_async_remote_copy(
                src_ref=sv_mine, dst_ref=sv_all.at[my_dev], send_sem=sv_send.at[j], recv_sem=sv_recv.at[j],
                device_id=to, device_id_type=MESH)

        sv_sends = [sv_copy(j, to) for j, to in enumerate(others)]
        for cp in sv_sends:
            cp.start()

        def s1(a, k):
            return pltpu.make_async_remote_copy(
                src_ref=grads[a].at[half_rows(a, k, 1 - c), :], dst_ref=sibs[a].at[k],
                send_sem=s1_send.at[a, k], recv_sem=s1_recv.at[a, k], device_id=sibling, device_id_type=MESH)

        s1s = [s1(a, 2 * chip[0] + chip[1]) for chip in chips + [(x, y)] for a in range(3)]
        for cp in s1s:
            cp.start()

        def s2(a, j, to):
            return pltpu.make_async_remote_copy(
                src_ref=stages[a].at[j], dst_ref=lands[a].at[j], send_sem=s2_send.at[a, j], recv_sem=s2_recv.at[a, j],
                device_id=to, device_id_type=MESH)

        s2s = []
        for j, chip in enumerate(chips):
            k = 2 * chip[0] + chip[1]
            for a in range(3):
                pltpu.make_async_remote_copy(
                    src_ref=grads[a].at[half_rows(a, k, c), :], dst_ref=sibs[a].at[k],
                    send_sem=s1_send.at[a, k], recv_sem=s1_recv.at[a, k], device_id=sibling,
                    device_id_type=MESH).wait_recv()
                stages[a][j] = (grads[a][half_rows(a, k, c), :] + sibs[a][k]).astype(BF16)
                cp = s2(a, j, (*chip, c))
                cp.start()
                s2s.append(cp)

        for a in range(3):
            pltpu.make_async_remote_copy(
                src_ref=grads[a].at[half_rows(a, my_chip, c), :], dst_ref=sibs[a].at[my_chip],
                send_sem=s1_send.at[a, my_chip], recv_sem=s1_recv.at[a, my_chip], device_id=sibling,
                device_id_type=MESH).wait_recv()
        for a in range(3):
            n = shard_rows[a]
            tot = grads[a][half_rows(a, my_chip, c), :] + sibs[a][my_chip]
            for j in range(3):
                s2(a, j, me).wait_recv()
                tot = tot + lands[a][j].astype(F32)
            mine = pl.ds(pl.multiple_of(c * (n // 2), 16), n // 2)
            res[a][mine, :] = tot

        def s3(a, half, to):
            n = shard_rows[a]
            blk = res[a].at[pl.ds(pl.multiple_of(half * (n // 2), 16), n // 2), :]
            return pltpu.make_async_remote_copy(
                src_ref=blk, dst_ref=blk, send_sem=s3_send.at[a], recv_sem=s3_recv.at[a],
                device_id=to, device_id_type=MESH)

        s3s = [s3(a, c, sibling) for a in range(3)]
        for cp in s3s:
            cp.start()
        for a in range(3):
            s3(a, 1 - c, me).wait_recv()

        sv_all[my_dev] = sv_mine[...]
        for j in range(7):
            sv_copy(j, me).wait_recv()
        tot = sv_all[0]
        for d in range(1, 8):
            tot = tot + sv_all[d]
        sv_ref[...] = tot
        for cp in sv_sends + s1s + s2s + s3s:
            cp.wait_send()

    vm = pl.BlockSpec(memory_space=pltpu.VMEM)
    half = lambda a: (shard_rows[a] // 2, widths[a])
    scratch = ([pltpu.VMEM((4, *half(a)), F32) for a in range(3)]
               + [pltpu.VMEM((3, *half(a)), BF16) for a in range(3)]
               + [pltpu.VMEM((3, *half(a)), BF16) for a in range(3)]
               + [pltpu.VMEM((8, SV_W), F32), pltpu.VMEM((8, 8, SV_W), F32)]
               + [pltpu.SemaphoreType.DMA((3, 4))] * 2 + [pltpu.SemaphoreType.DMA((3, 3))] * 2
               + [pltpu.SemaphoreType.DMA((3,))] * 2 + [pltpu.SemaphoreType.DMA((7,))] * 2)
    return pl.pallas_call(
        body, name="reduce_grads",
        out_shape=(_vm_sds((SH_IN, D_MODEL), F32), _vm_sds((SH_OUT, D_MODEL), F32),
                   _vm_sds((SH_MEM, 2 * W_C), F32), _vm_sds((8, SV_W), F32)),
        in_specs=[vm] * 6, out_specs=(vm, vm, vm, vm), scratch_shapes=scratch,
        compiler_params=_cp(vmem_mb=56),
    )(g_in, g_out, g_mem, acc, dbin, dsink)


def _adamw(name, w, g, m, v, rows=None, copy_g=False):
    shape = w.shape
    rows = shape[0] if rows is None else rows
    n_out = 4 if copy_g else 3

    def body(w_ref, g_ref, m_ref, v_ref, d_ref, nm_ref, nv_ref, *go_ref):
        gv = g_ref[...]
        if copy_g:
            go_ref[0][...] = gv
        nm = ADAM_B1 * m_ref[...] + (1.0 - ADAM_B1) * gv
        nv = ADAM_B2 * v_ref[...] + (1.0 - ADAM_B2) * (gv * gv)
        m_hat = nm / (1.0 - ADAM_B1 ** ADAM_STEP)
        v_hat = nv / (1.0 - ADAM_B2 ** ADAM_STEP)
        d_ref[...] = -ADAM_LR * (m_hat / (jnp.sqrt(v_hat) + ADAM_EPS) + ADAM_WD * w_ref[...])
        nm_ref[...] = nm
        nv_ref[...] = nv

    spec = pl.BlockSpec((rows, shape[1]), lambda i: (i, 0))
    return pl.pallas_call(
        body, name=name, grid=(shape[0] // rows,), in_specs=[spec] * 4, out_specs=(spec,) * n_out,
        out_shape=(_sds(shape, F32),) * n_out, compiler_params=_cp(("parallel",)),
    )(*_pin(w, g, m, v))


def _rope_tables():
    pos = jnp.arange(SEQ, dtype=F32)
    inv = ROPE_THETA ** (-jnp.arange(0, 64, 2, dtype=F32) / 64)
    ang = pos[:, None] * inv[None, :]
    ang = jnp.concatenate([ang, ang, ang, ang], axis=-1)
    low = (jnp.arange(128) % 64) < 32
    cos, sin = jnp.cos(ang), jnp.sin(ang)
    return cos, jnp.where(low, -sin, 0.0), jnp.where(low, 0.0, sin)


def _local_step(x2, mem2, tgt2, winT, wout, wmem, b_in, sinks, g_branch, ln_gain, ln_bias):
    cos, sa, sb = _rope_tables()
    sinkv = jnp.pad(sinks, ((0, 0), (0, 120)))
    head_of_lane = jnp.arange(512)[None, :] // 64
    gather8 = (head_of_lane.T == jnp.arange(128)[None, :]).astype(BF16)
    gather4 = gather8[:W_B]
    spread4 = gather4.T

    xb, qa, ka, va, bn, b4, b16, qc, z, wout, wmem = _in_proj(x2, winT, b_in, cos, sa, sb, wout, wmem)
    memb, mkv = _mem_kv(mem2, wmem)
    b4f, b16f = b4.reshape(T, 768), b16.reshape(T, 768)

    swa = dict(kind="band", nb=SEQ // BLK, max_dist=BLK - 1, gqa=True)
    dil = (dict(kind="band", nb=SEQ // BLK), dict(kind="band", nb=SEQ // 4 // BLK), dict(kind="band", nb=1))
    (oa, lse_a), (o1, l1), (o4, l4), (o16, l16), (oc, lse_c) = _run_parts("attn_fwd", [
        _attn_fwd(qa, 0, W_A, ka, 0, va, 0, W_KV_A, sinks=sinks, **swa),
        _attn_fwd(bn, 0, W_B, bn, 1, bn, 2, W_B, **dil[0]),
        _attn_fwd(b4f, 0, W_B, b4f, 1, b4f, 2, W_B, **dil[1]),
        _attn_fwd(b16f, 0, W_B, b16f, 1, b16f, 2, W_B, **dil[2]),
        _attn_fwd(qc, 0, W_C, mkv, 0, mkv, 1, W_C, kind="mem")], "parallel", 48)

    s4 = lambda w: (B_LOC, 4, SEQ // 4, w)
    s16 = lambda w: (B_LOC, 16, SEQ // 16, w)
    (du, dz, doa, dla, dobn, lsen, dlbn, dob4, lse4, dlb4, dob16, lse16, dlb16, doc, dlc, acc, g_out) = _middle(
        oa, o1, l1, o4.reshape(s4(W_B)), l4.reshape(s4(128)), o16.reshape(s16(W_B)), l16.reshape(s16(128)), oc, z,
        x2, tgt2, g_branch, ln_gain, ln_bias, wout, spread4, gather4, gather8)

    flat = lambda a: a.reshape(T, a.shape[-1])
    (dqa, dka, dva, dsink), (dqc, dmkv) = _run_parts("attn_bwd_a", [
        _attn_bwd(qa, 0, W_A, ka, 0, va, 0, W_KV_A, doa, lse_a, dla, sinkv=sinkv, **swa),
        _attn_bwd(qc, 0, W_C, mkv, 0, mkv, 1, W_C, doc, lse_c, dlc, kind="mem")], "arbitrary", 48)
    (dqn, dkn, dvn), (dq4, dk4, dv4), (dq16, dk16, dv16) = _run_parts("attn_bwd_b", [
        _attn_bwd(bn, 0, W_B, bn, 1, bn, 2, W_B, dobn, lsen, dlbn, **dil[0]),
        _attn_bwd(b4f, 0, W_B, b4f, 1, b4f, 2, W_B, flat(dob4), flat(lse4), flat(dlb4), **dil[1]),
        _attn_bwd(b16f, 0, W_B, b16f, 1, b16f, 2, W_B, flat(dob16), flat(lse16), flat(dlb16), **dil[2])],
        "arbitrary", 56)

    r4 = lambda a: a.reshape(s4(W_B))
    r16 = lambda a: a.reshape(s16(W_B))
    gx, dbin, g_in = _dh_dx(dqa, dka, dva, dqn, dkn, dvn, r4(dq4), r4(dk4), r4(dv4), r16(dq16), r16(dk16),
                            r16(dv16), dqc, dz, du, xb, cos, sa, sb, winT)
    g_mem = _tn_matmul("dw_mem", memb, dmkv, D_MODEL, B_LOC * MEM_LEN)
    return gx, g_in, g_out, g_mem, acc, dbin, dsink


def kernel(x, mem, w_in, b_in, w_mem, attn_sinks, g_branch, w_out, ln_gain, ln_bias, loss_target, m_w_in, m_b_in, m_w_mem, m_attn_sinks, m_g_branch, m_w_out, m_ln_gain, m_ln_bias, v_w_in, v_b_in, v_w_mem, v_attn_sinks, v_g_branch, v_w_out, v_ln_gain, v_ln_bias):
    winT, wout, wmem = _gather_weights(w_in[0].T, w_out[0], w_mem[0])
    gx, g_in, g_out, g_mem, acc, dbin, dsink = _local_step(
        x.reshape(T, D_MODEL), mem.reshape(B_LOC * MEM_LEN, D_MODEL), loss_target.reshape(T, D_MODEL),
        winT, wout, wmem, b_in, attn_sinks, g_branch, ln_gain, ln_bias)
    r_in, r_out, r_mem, sv = _reduce_grads(g_in, g_out, g_mem, acc, dbin, dsink)

    loss = jnp.sum(sv[0, :D_MODEL])
    grads = {
        "b_in": sv[4:5, :D_IN], "w_mem": r_mem[None],
        "attn_sinks": -sv[5:6, 0:8], "g_branch": sv[1:2, :D_MODEL], "w_out": r_out[None],
        "ln_gain": sv[2:3, :D_MODEL], "ln_bias": sv[3:4, :D_MODEL],
    }
    weights = dict(w_in=w_in, b_in=b_in, w_mem=w_mem, attn_sinks=attn_sinks, g_branch=g_branch, w_out=w_out,
                   ln_gain=ln_gain, ln_bias=ln_bias)
    ms = dict(w_in=m_w_in, b_in=m_b_in, w_mem=m_w_mem, attn_sinks=m_attn_sinks, g_branch=m_g_branch, w_out=m_w_out,
              ln_gain=m_ln_gain, ln_bias=m_ln_bias)
    vs = dict(w_in=v_w_in, b_in=v_b_in, w_mem=v_w_mem, attn_sinks=v_attn_sinks, g_branch=v_g_branch, w_out=v_w_out,
              ln_gain=v_ln_gain, ln_bias=v_ln_bias)
    names = ["w_in", "b_in", "w_mem", "attn_sinks", "g_branch", "w_out", "ln_gain", "ln_bias"]
    deltas, new_m, new_v = [], [], []
    for n in names:
        shape = weights[n].shape
        two_d = lambda a: a.reshape(shape[-2], shape[-1])
        if n == "w_in":
            d, nm, nv, gw = (a.T for a in _adamw("adamw_w_in", w_in[0].T, r_in, m_w_in[0].T, v_w_in[0].T, SH_IN // 4,
                                                 copy_g=True))
            grads[n] = gw
        elif n in ("w_out", "w_mem"):
            d, nm, nv, grads[n] = _adamw("adamw_" + n, two_d(weights[n]), two_d(grads[n]), two_d(ms[n]), two_d(vs[n]),
                                         copy_g=True)
        else:
            d, nm, nv = _adamw("adamw_" + n, two_d(weights[n]), two_d(grads[n]), two_d(ms[n]), two_d(vs[n]))
        deltas.append(d.reshape(shape))
        new_m.append(nm.reshape(shape))
        new_v.append(nv.reshape(shape))
    return (loss, gx.reshape(B_LOC, SEQ, D_MODEL), *[grads[n].reshape(weights[n].shape) for n in names],
            *deltas, *new_m, *new_v)
```

```python
import functools

import jax
import jax.numpy as jnp
from jax import lax
from jax.experimental import pallas as pl
from jax.experimental.pallas import tpu as pltpu

F32, BF16 = jnp.float32, jnp.bfloat16

D_MODEL = 1024
SEQ = 2048
B_LOC = 2
T = B_LOC * SEQ
BLK = 128
MEM_LEN = 256
W_A, W_KV_A, W_B, W_C, D_MIX = 512, 128, 256, 256, 1024
D_IN = 2816
O_QA, O_KA, O_VA, O_QB, O_KB, O_VB, O_QC, O_Z = 0, 512, 640, 768, 1024, 1280, 1536, 1792
ROPE_THETA = 10000.0
LN_EPS = 1e-5
RMS_EPS = 1e-6
ALPHA = 2.0 ** 0.25
QK_SCALE = 0.125
N_CHIP = 4
SH_IN, SH_OUT, SH_MEM = D_IN // N_CHIP, D_MIX // N_CHIP, D_MODEL // N_CHIP
NEG = -1e30
ADAM_LR, ADAM_B1, ADAM_B2, ADAM_EPS, ADAM_WD, ADAM_STEP = 0.001, 0.9, 0.999, 1e-08, 0.01, 10
SV_W = 3072
MESH = pl.DeviceIdType.MESH

NN = ((1,), (0,))
NT = ((1,), (1,))
TN = ((0,), (0,))


def _dot(a, b, dims):
    return lax.dot_general(a, b, (dims, ((), ())), preferred_element_type=F32)


def _cp(sem=None, vmem_mb=None):
    kw = {}
    if sem is not None:
        kw["dimension_semantics"] = sem
    if vmem_mb is not None:
        kw["vmem_limit_bytes"] = vmem_mb * 1024 * 1024
    return pltpu.CompilerParams(**kw)


def _sds(shape, dtype):
    return pltpu.HBM(shape, dtype)


def _vm_sds(shape, dtype):
    return jax.ShapeDtypeStruct(shape, dtype)


def _pin(*args):
    return [pltpu.with_memory_space_constraint(a, pltpu.HBM) for a in args]


def _full(shape):
    n = len(shape)
    return pl.BlockSpec(shape, lambda *_: (0,) * n)


def _shard_rows(ref, n, chip, half):
    start = pl.multiple_of((2 * chip[0] + chip[1]) * n + half * (n // 2), 16)
    return ref.at[pl.ds(start, n // 2), :]


def _gather_weights(win_sh, wout_sh, wmem_sh):
    def body(a_ref, b_ref, c_ref, oa_ref, ob_ref, oc_ref, ici_send, ici_recv, d2d_send, d2d_recv):
        x, y, c = lax.axis_index("x"), lax.axis_index("y"), lax.axis_index("c")
        sibling = (x, y, 1 - c)
        chips = [(1 - x, y), (x, 1 - y), (1 - x, 1 - y)]
        for src, out, n in ((a_ref, oa_ref, SH_IN), (b_ref, ob_ref, SH_OUT), (c_ref, oc_ref, SH_MEM)):
            out[pl.ds(pl.multiple_of((2 * x + y) * n, 16), n), :] = src[...].astype(BF16)

        def copy(sems, j, chip_of_block, half, to):
            blk = _shard_rows(oa_ref, SH_IN, chip_of_block, half)
            return pltpu.make_async_remote_copy(
                src_ref=blk, dst_ref=blk, send_sem=sems[0].at[j], recv_sem=sems[1].at[j],
                device_id=to, device_id_type=MESH)

        ici, d2d = (ici_send, ici_recv), (d2d_send, d2d_recv)
        first = [copy(ici, j, (x, y), c, (*chip, c)) for j, chip in enumerate(chips)]
        for cp in first:
            cp.start()
        passed = []
        for j, chip in enumerate(chips):
            copy(ici, j, chip, c, (x, y, c)).wait_recv()
            fw = copy(d2d, j, chip, c, sibling)
            fw.start()
            passed.append(fw)
        for j, chip in enumerate(chips):
            copy(d2d, j, chip, 1 - c, (x, y, c)).wait_recv()
        for cp in first + passed:
            cp.wait_send()

    vm = pl.BlockSpec(memory_space=pltpu.VMEM)
    return pl.pallas_call(
        body, name="gather_weights",
        out_shape=(_vm_sds((D_IN, D_MODEL), BF16), _vm_sds((D_MIX, D_MODEL), BF16),
                   _vm_sds((D_MODEL, 2 * W_C), BF16)),
        in_specs=[vm, vm, vm], out_specs=(vm, vm, vm),
        scratch_shapes=[pltpu.SemaphoreType.DMA((3,))] * 4,
        compiler_params=_cp(vmem_mb=40),
    )(win_sh, wout_sh, wmem_sh)


def _rope(t, cos, sa, sb, sign):
    w = t.shape[1]
    reps = w // 128
    c, a, b = (jnp.tile(v, (1, reps)) if reps > 1 else v for v in (cos, sa, sb))
    rot = pltpu.roll(t, w - 32, 1) * a + pltpu.roll(t, 32, 1) * b
    return t * c + rot if sign > 0 else t * c - rot


def _in_proj(x, winT, b_in, cos, sa, sb, wout_own, wmem_own):
    tm = 256
    spt = SEQ // tm
    n_steps = T // tm
    forward_step = n_steps // 2

    def body(x_ref, w_ref, b_ref, cos_ref, sa_ref, sb_ref, wo_in, wm_in,
             xb_ref, qa_ref, ka_ref, va_ref, bn_ref, b4_ref, b16_ref, qc_ref, z_ref, wo_ref, wm_ref,
             scr, ici_send, ici_recv, d2d_send, d2d_recv):
        i = pl.program_id(0)
        mx, my, mc = lax.axis_index("x"), lax.axis_index("y"), lax.axis_index("c")
        chips = [(1 - mx, my), (mx, 1 - my), (1 - mx, 1 - my)]
        full = ((wo_ref, SH_OUT), (wm_ref, SH_MEM))

        def copy(sems, a, j, chip_of_block, half, to):
            blk = _shard_rows(full[a][0], full[a][1], chip_of_block, half)
            return pltpu.make_async_remote_copy(
                src_ref=blk, dst_ref=blk, send_sem=sems[0].at[a, j], recv_sem=sems[1].at[a, j],
                device_id=to, device_id_type=MESH)

        ici, d2d = (ici_send, ici_recv), (d2d_send, d2d_recv)
        pairs = [(a, j, chip) for j, chip in enumerate(chips) for a in range(2)]

        @pl.when(i == 0)
        def _():
            for a, j, chip in pairs:
                copy(ici, a, j, (mx, my), mc, (*chip, mc)).start()

        @pl.when(i == forward_step)
        def _():
            for a, j, chip in pairs:
                copy(ici, a, j, chip, mc, (mx, my, mc)).wait_recv()
                copy(d2d, a, j, chip, mc, (mx, my, 1 - mc)).start()

        @pl.when(i == n_steps - 1)
        def _():
            for a, j, chip in pairs:
                copy(d2d, a, j, chip, 1 - mc, (mx, my, mc)).wait_recv()
            for a, j, chip in pairs:
                copy(ici, a, j, (mx, my), mc, (*chip, mc)).wait_send()
                copy(d2d, a, j, chip, mc, (mx, my, 1 - mc)).wait_send()

        xb = x_ref[...].astype(BF16)
        xb_ref[...] = xb
        cos_t, sa_t, sb_t = cos_ref[...], sa_ref[...], sb_ref[...]

        def proj(r0, n):
            return _dot(xb, w_ref[r0:r0 + n, :], NT) + b_ref[:, r0:r0 + n]

        def rope(t):
            return _rope(t, cos_t, sa_t, sb_t, +1)

        qa_ref[...] = (rope(proj(O_QA, W_A)) * QK_SCALE).astype(BF16)
        ka_ref[...] = rope(proj(O_KA, W_KV_A)).astype(BF16)
        va_ref[...] = proj(O_VA, W_KV_A).astype(BF16)
        qc_ref[...] = (proj(O_QC, W_C) * QK_SCALE).astype(BF16)
        z_ref[...] = proj(O_Z, D_MIX).astype(BF16)
        parts = (rope(proj(O_QB, W_B)) * QK_SCALE, rope(proj(O_KB, W_B)), proj(O_VB, W_B))
        for k, part in enumerate(parts):
            bn_ref[:, 256 * k:256 * (k + 1)] = part.astype(BF16)
            scr[2 * k] = part[:, :128]
            scr[2 * k + 1] = part[:, 128:]
        for j in range(6):
            for res in range(4):
                b4_ref[0, res, :, 128 * j:128 * (j + 1)] = scr[j, pl.ds(res, tm // 4, stride=4), :].astype(BF16)
            for res in range(16):
                b16_ref[0, res, :, 128 * j:128 * (j + 1)] = scr[j, pl.ds(res, tm // 16, stride=16), :].astype(BF16)

    tok = lambda w: pl.BlockSpec((tm, w), lambda i: (i, 0))
    tab = pl.BlockSpec((tm, 128), lambda i: (i % spt, 0))
    hbm = pl.BlockSpec(memory_space=pl.ANY)
    return pl.pallas_call(
        body, name="in_proj", grid=(n_steps,),
        in_specs=[tok(D_MODEL), _full((D_IN, D_MODEL)), _full((1, D_IN)), tab, tab, tab, hbm, hbm],
        out_specs=(tok(D_MODEL), tok(W_A), tok(W_KV_A), tok(W_KV_A), tok(768),
                   pl.BlockSpec((1, 4, tm // 4, 768), lambda i: (i // spt, 0, i % spt, 0)),
                   pl.BlockSpec((1, 16, tm // 16, 768), lambda i: (i // spt, 0, i % spt, 0)),
                   tok(W_C), tok(D_MIX), hbm, hbm),
        out_shape=(_sds((T, D_MODEL), BF16), _sds((T, W_A), BF16), _sds((T, W_KV_A), BF16), _sds((T, W_KV_A), BF16),
                   _sds((T, 768), BF16), _sds((B_LOC, 4, SEQ // 4, 768), BF16), _sds((B_LOC, 16, SEQ // 16, 768), BF16),
                   _sds((T, W_C), BF16), _sds((T, D_MIX), BF16),
                   _sds((D_MIX, D_MODEL), BF16), _sds((D_MODEL, 2 * W_C), BF16)),
        input_output_aliases={6: 9, 7: 10},
        scratch_shapes=[pltpu.VMEM((6, tm, 128), F32)] + [pltpu.SemaphoreType.DMA((2, 3))] * 4,
        compiler_params=_cp(("arbitrary",), vmem_mb=48),
    )(*_pin(x, winT, b_in, cos, sa, sb, wout_own, wmem_own))


def _mem_kv(mem, wmem):
    def body(m_ref, w_ref, mb_ref, kv_ref):
        mb = m_ref[...].astype(BF16)
        mb_ref[...] = mb
        kv_ref[...] = _dot(mb, w_ref[...], NN).astype(BF16)

    n = B_LOC * MEM_LEN
    return pl.pallas_call(
        body, name="mem_kv",
        out_shape=(_sds((n, D_MODEL), BF16), _sds((n, 2 * W_C), BF16)),
    )(*_pin(mem, wmem))


class _Part:
    def __init__(self, body, args, in_specs, out_specs, out_shape):
        self.body, self.args, self.in_specs, self.out_specs, self.out_shape = body, args, in_specs, out_specs, out_shape


def _run_parts(name, parts, semantics, vmem_mb):
    n_in = [len(p.args) for p in parts]
    n_out = [len(p.out_shape) for p in parts]

    def body(*refs):
        ins, outs = refs[:sum(n_in)], refs[sum(n_in):]
        i0 = o0 = 0
        for p, ni, no in zip(parts, n_in, n_out):
            p.body(*ins[i0:i0 + ni], *outs[o0:o0 + no])
            i0, o0 = i0 + ni, o0 + no

    res = pl.pallas_call(
        body, name=name, grid=(T // QR,),
        in_specs=[sp for p in parts for sp in p.in_specs], out_specs=tuple(sp for p in parts for sp in p.out_specs),
        out_shape=tuple(sh for p in parts for sh in p.out_shape),
        compiler_params=_cp((semantics,), vmem_mb=vmem_mb),
    )(*_pin(*[a for p in parts for a in p.args]))
    out, o0 = [], 0
    for no in n_out:
        out.append(tuple(res[o0:o0 + no]))
        o0 += no
    return out


QB = 8
QR = QB * BLK


def _lane_lo():
    return lax.broadcasted_iota(jnp.int32, (1, 128), 1) < 64


def _dup_head(k2, hk, lo):
    kf = k2.astype(F32)
    r = pltpu.roll(kf, 64, 1)
    return (jnp.where(lo, kf, r) if hk == 0 else jnp.where(lo, r, kf)).astype(BF16)


def _stack_heads(pairs, lo):
    parts = []
    for x2 in pairs:
        z = jnp.zeros_like(x2)
        parts += [jnp.where(lo, x2, z), jnp.where(lo, z, x2)]
    return jnp.concatenate(parts, axis=0)


def _prev_mode(kind, nb, j):
    if kind == "mem" or nb == 1:
        return "no"
    if nb <= QB:
        return "yes" if j % nb else "no"
    return "yes" if j else "dyn"


class _Attn:
    def __init__(self, kind, nb, max_dist, gqa, qw, kvw, qcb, kcb, vcb):
        self.kind, self.nb, self.gqa, self.qw, self.kvw = kind, nb, gqa, qw, kvw
        npairs = qw // 128
        self.groups = ([(hk, [2 * hk, 2 * hk + 1]) for hk in range(npairs // 2)] if gqa
                       else [(p, [p]) for p in range(npairs)])
        self.nh = 2 * len(self.groups[0][1])
        self.cols = 128 * self.nh
        self.reach = BLK - max_dist
        self.ext_prev = kind == "band" and nb > QB
        self.q_spec = pl.BlockSpec((QR, qw), lambda g: (g, qcb))
        self.row_spec = pl.BlockSpec((QR, qw), lambda g: (g, 0))
        self.stat_spec = pl.BlockSpec((QR, 128), lambda g: (g, 0))
        if kind == "mem":
            per = SEQ // QR
            self.kv_specs = [pl.BlockSpec((MEM_LEN, kvw), lambda g: (g // per, kcb)),
                             pl.BlockSpec((MEM_LEN, kvw), lambda g: (g // per, vcb))]
        else:
            self.kv_specs = [pl.BlockSpec((QR, kvw), lambda g: (g, kcb)), pl.BlockSpec((QR, kvw), lambda g: (g, vcb))]
            if self.ext_prev:
                self.kv_specs += [pl.BlockSpec((BLK, kvw), lambda g: (jnp.maximum(g * QB - 1, 0), kcb)),
                                  pl.BlockSpec((BLK, kvw), lambda g: (jnp.maximum(g * QB - 1, 0), vcb))]

    def masks(self):
        if self.kind == "mem":
            return None
        kj = lax.broadcasted_iota(jnp.int32, (2 * BLK, self.cols), 0)
        qi = lax.broadcasted_iota(jnp.int32, (2 * BLK, self.cols), 1) & (BLK - 1)
        kj1 = lax.broadcasted_iota(jnp.int32, (BLK, self.cols), 0)
        qi1 = lax.broadcasted_iota(jnp.int32, (BLK, self.cols), 1) & (BLK - 1)
        return kj, qi, kj1 <= qi1

    def keys(self, j, gi, kc_ref, vc_ref, kp_ref, vp_ref, lo, kq, g):
        def kv(k_ref, v_ref, r):
            if self.gqa:
                return _dup_head(k_ref[r, :], gi, lo), _dup_head(v_ref[r, :], gi, lo)
            sl = slice(128 * gi, 128 * (gi + 1))
            return k_ref[r, sl], v_ref[r, sl]

        if self.kind == "mem":
            key0 = pl.multiple_of((g // (SEQ // QR)) * MEM_LEN, MEM_LEN)
            return (*kv(kc_ref, vc_ref, slice(None)), None, [(0, MEM_LEN, key0)])
        kj, qi, cur = kq
        row0 = g * QR + BLK * j
        mode = _prev_mode(self.kind, self.nb, j)
        if mode == "no":
            return (*kv(kc_ref, vc_ref, slice(BLK * j, BLK * (j + 1))), cur, [(0, BLK, pl.multiple_of(row0, BLK))])
        if mode == "yes":
            mask = jnp.logical_and(kj >= qi + self.reach, kj <= qi + BLK)
            return (*kv(kc_ref, vc_ref, slice(BLK * (j - 1), BLK * (j + 1))), mask,
                    [(0, 2 * BLK, pl.multiple_of(row0 - BLK, BLK))])
        has_prev = ((g * QB) % self.nb) > 0
        hp = has_prev.astype(jnp.int32)
        mask = jnp.logical_and(kj >= qi * hp + (self.reach * hp + BLK * (1 - hp)), kj <= qi + BLK)
        kp, vp = kv(kp_ref, vp_ref, slice(None))
        kc, vc = kv(kc_ref, vc_ref, slice(0, BLK))
        return (jnp.concatenate([kp, kc], axis=0), jnp.concatenate([vp, vc], axis=0), mask,
                [(0, BLK, pl.multiple_of(jnp.maximum(row0 - BLK, 0), BLK)), (BLK, BLK, pl.multiple_of(row0, BLK))])


def _attn_fwd(q, qcb, qw, k, kcb, v, vcb, kvw, *, kind, nb=1, max_dist=BLK, gqa=False, sinks=None):
    a = _Attn(kind, nb, max_dist, gqa, qw, kvw, qcb, kcb, vcb)

    def body(*refs):
        it = iter(refs)
        q_ref, kc_ref, vc_ref = next(it), next(it), next(it)
        kp_ref, vp_ref = (next(it), next(it)) if a.ext_prev else (None, None)
        sink_ref = next(it) if sinks is not None else None
        o_ref, lse_ref = next(it), next(it)
        g = pl.program_id(0)
        lo = _lane_lo()
        top = lax.broadcasted_iota(jnp.int32, (128, 1), 0) < 64
        rid = lax.broadcasted_iota(jnp.int32, (8, 128), 0)
        kq = a.masks()
        for j in range(QB):
            rows = slice(BLK * j, BLK * (j + 1))
            stat = jnp.zeros((8, 128), F32)
            for gi, pairs in a.groups:
                qs = _stack_heads([q_ref[rows, 128 * p:128 * (p + 1)] for p in pairs], lo)
                kk, vv, mask, _ = a.keys(j, gi, kc_ref, vc_ref, kp_ref, vp_ref, lo, kq, g)
                pieces = [slice(r0, r0 + BLK) for r0 in range(0, kk.shape[0], BLK)]
                ss = []
                for r in pieces:
                    s = _dot(kk[r], qs, NT)
                    ss.append(s if mask is None else jnp.where(mask[r], s, NEG))
                m = jnp.max(ss[0], axis=0, keepdims=True)
                for s in ss[1:]:
                    m = jnp.maximum(m, jnp.max(s, axis=0, keepdims=True))
                if sink_ref is not None:
                    sk = jnp.concatenate([jnp.full((1, 128), sink_ref[0, a.nh * gi + i], F32) for i in range(a.nh)],
                                         axis=1)
                    m = jnp.maximum(m, sk)
                l, ot = None, None
                for r, s in zip(pieces, ss):
                    p = jnp.exp(s - m)
                    ps = jnp.sum(p, axis=0, keepdims=True)
                    c = _dot(vv[r], p.astype(BF16), TN)
                    l, ot = (ps, c) if l is None else (l + ps, ot + c)
                if sink_ref is not None:
                    l = l + jnp.exp(sk - m)
                ot = ot * pl.reciprocal(l, approx=True)
                lse = m + jnp.log(l)
                for i, p in enumerate(pairs):
                    o2t = jnp.where(top, ot[:, 256 * i:256 * i + 128], ot[:, 256 * i + 128:256 * i + 256])
                    o_ref[rows, 128 * p:128 * (p + 1)] = o2t.T.astype(BF16)
                for i in range(a.nh):
                    stat = jnp.where(rid == a.nh * gi + i, lse[:, 128 * i:128 * (i + 1)], stat)
            lse_ref[rows, :] = jnp.concatenate([stat, jnp.zeros((120, 128), F32)], axis=0).T

    args = [q, k, v] + ([k, v] if a.ext_prev else [])
    in_specs = [a.q_spec] + a.kv_specs
    if sinks is not None:
        args.append(sinks)
        in_specs.append(pl.BlockSpec(memory_space=pltpu.SMEM))
    return _Part(body, args, in_specs, [a.row_spec, a.stat_spec], [_sds((T, qw), BF16), _sds((T, 128), F32)])


def _attn_bwd(q, qcb, qw, k, kcb, v, vcb, kvw, do, lse, dl, *, kind, nb=1, max_dist=BLK, gqa=False, sinkv=None):
    a = _Attn(kind, nb, max_dist, gqa, qw, kvw, qcb, kcb, vcb)

    def body(*refs):
        it = iter(refs)
        q_ref, kc_ref, vc_ref = next(it), next(it), next(it)
        kp_ref, vp_ref = (next(it), next(it)) if a.ext_prev else (None, None)
        do_ref, lse_ref, dl_ref = next(it), next(it), next(it)
        sinkv_ref = next(it) if sinkv is not None else None
        dq_ref = next(it)
        if kind == "mem":
            dkv_ref = next(it)
        else:
            dk_ref, dv_ref = next(it), next(it)
        dsink_ref = next(it) if sinkv is not None else None
        g = pl.program_id(0)
        lo = _lane_lo()
        top = lax.broadcasted_iota(jnp.int32, (128, 1), 0) < 64

        @pl.when(g == 0)
        def _():
            if kind == "mem":
                dkv_ref[...] = jnp.zeros_like(dkv_ref)
            else:
                dk_ref[...] = jnp.zeros_like(dk_ref)
                dv_ref[...] = jnp.zeros_like(dv_ref)
            if dsink_ref is not None:
                dsink_ref[...] = jnp.zeros_like(dsink_ref)

        kq = a.masks()
        for j in range(QB):
            rows = slice(BLK * j, BLK * (j + 1))
            lse_t = lse_ref[rows, :].T
            dl_t = dl_ref[rows, :].T
            for gi, pairs in a.groups:
                heads = [a.nh * gi + i for i in range(a.nh)]
                qs = _stack_heads([q_ref[rows, 128 * p:128 * (p + 1)] for p in pairs], lo)
                dos = _stack_heads([do_ref[rows, 128 * p:128 * (p + 1)] for p in pairs], lo)
                lse_row = jnp.concatenate([lse_t[h:h + 1, :] for h in heads], axis=1)
                dl_row = jnp.concatenate([dl_t[h:h + 1, :] for h in heads], axis=1)
                kk, vv, mask, dests = a.keys(j, gi, kc_ref, vc_ref, kp_ref, vp_ref, lo, kq, g)
                s = _dot(kk, qs, NT)
                if mask is not None:
                    s = jnp.where(mask, s, NEG)
                p = jnp.exp(s - lse_row)
                ds = (p * (_dot(vv, dos, NT) - dl_row)).astype(BF16)
                dqt = _dot(kk, ds, TN)
                ck = _dot(ds, qs, NN)
                cv = _dot(p.astype(BF16), dos, NN)
                if gqa:
                    sel = lo if gi == 0 else jnp.logical_not(lo)
                    ck = jnp.where(sel, ck + pltpu.roll(ck, 64, 1), 0.0)
                    cv = jnp.where(sel, cv + pltpu.roll(cv, 64, 1), 0.0)
                    kcols = slice(0, 128)
                else:
                    kcols = slice(128 * gi, 128 * (gi + 1))
                for r0, nr, key0 in dests:
                    krows = pl.ds(key0, nr)
                    if kind == "mem":
                        dkv_ref[krows, kcols] += ck[r0:r0 + nr]
                        dkv_ref[krows, slice(kvw + kcols.start, kvw + kcols.stop)] += cv[r0:r0 + nr]
                    else:
                        dk_ref[krows, kcols] += ck[r0:r0 + nr]
                        dv_ref[krows, kcols] += cv[r0:r0 + nr]
                for i, p in enumerate(pairs):
                    dq2t = jnp.where(top, dqt[:, 256 * i:256 * i + 128], dqt[:, 256 * i + 128:256 * i + 256])
                    dq_ref[rows, 128 * p:128 * (p + 1)] = dq2t.T.astype(BF16)
        if dsink_ref is not None:
            ps = jnp.exp(sinkv_ref[...] - lse_ref[...]) * dl_ref[...]
            dsink_ref[...] += jnp.sum(ps, axis=0, keepdims=True)

    args = [q, k, v] + ([k, v] if a.ext_prev else []) + [do, lse, dl]
    in_specs = [a.q_spec] + a.kv_specs + [a.row_spec, a.stat_spec, a.stat_spec]
    if sinkv is not None:
        args.append(sinkv)
        in_specs.append(_full((1, 128)))
    out_shape = [_sds((T, qw), BF16)]
    out_specs = [a.row_spec]
    once = lambda shape: pl.BlockSpec(shape, lambda g: (0, 0), pipeline_mode=pl.Buffered(1))
    if kind == "mem":
        out_shape.append(_sds((B_LOC * MEM_LEN, 2 * kvw), F32))
        out_specs.append(once((B_LOC * MEM_LEN, 2 * kvw)))
    else:
        out_shape += [_sds((T, kvw), F32)] * 2
        out_specs += [once((T, kvw))] * 2
    if sinkv is not None:
        out_shape.append(_sds((1, 128), F32))
        out_specs.append(_full((1, 128)))
    return _Part(body, args, in_specs, out_specs, out_shape)


def _dot2(v, w_ref):
    hi = v.astype(BF16)
    lo = (v - hi.astype(F32)).astype(BF16)
    return _dot(hi, w_ref[...], NN) + _dot(lo, w_ref[...], NN)


def _middle(oa, o1, l1, o4, l4, o16, l16, oc, z, x, tgt, g_br, ln_g, ln_b, wout, spread4, gather4, gather8):
    tm = 256
    spt = SEQ // tm

    def body(oa_ref, o1_ref, l1_ref, o4_ref, l4_ref, o16_ref, l16_ref, oc_ref, z_ref, x_ref, t_ref,
             g_ref, lg_ref, lb_ref, w_ref, sp4_ref, ga4_ref, ga8_ref,
             du_ref, dz_ref, doa_ref, dla_ref,
             dobn_ref, lsen_ref, dlbn_ref, dob4_ref, lse4_ref, dlb4_ref, dob16_ref, lse16_ref, dlb16_ref,
             doc_ref, dlc_ref, acc_ref, gout_ref, scr):
        i = pl.program_id(0)

        @pl.when(i == 0)
        def _():
            acc_ref[...] = jnp.zeros_like(acc_ref)
            gout_ref[...] = jnp.zeros_like(gout_ref)

        for res in range(4):
            rows = pl.ds(res, tm // 4, stride=4)
            for j in range(2):
                scr[j, rows, :] = o4_ref[0, res, :, 128 * j:128 * (j + 1)].astype(F32)
            scr[2, rows, :] = l4_ref[0, res]
        for res in range(16):
            rows = pl.ds(res, tm // 16, stride=16)
            for j in range(2):
                scr[3 + j, rows, :] = o16_ref[0, res, :, 128 * j:128 * (j + 1)].astype(F32)
            scr[5, rows, :] = l16_ref[0, res]
        cat = lambda a: jnp.concatenate([scr[a], scr[a + 1]], axis=1)
        o1v, o4v, o16v = o1_ref[...].astype(F32), cat(0), cat(3)
        l1v, l4v, l16v = l1_ref[...], scr[2], scr[5]
        mx = jnp.maximum(jnp.maximum(l1v, l4v), l16v)
        e1, e4, e16 = jnp.exp(l1v - mx), jnp.exp(l4v - mx), jnp.exp(l16v - mx)
        ssum = e1 + e4 + e16
        lse_b = mx + jnp.log(ssum)
        inv = 1.0 / ssum
        ob = (_dot2(e1 * inv, sp4_ref) * o1v + _dot2(e4 * inv, sp4_ref) * o4v + _dot2(e16 * inv, sp4_ref) * o16v)
        oav, ocv = oa_ref[...].astype(F32), oc_ref[...].astype(F32)

        def rms(o):
            r = lax.rsqrt(jnp.sum(o * o, axis=1, keepdims=True) * (1.0 / o.shape[1]) + RMS_EPS)
            return o * r, r

        na, ra = rms(oav)
        nb_, rb = rms(ob)
        nc, rc = rms(ocv)
        n = jnp.concatenate([na, nb_, nc], axis=1)
        zf = z_ref[...].astype(F32)
        sig = 1.0 / (1.0 + jnp.exp(-zf))
        sz = zf * sig
        gb = g_ref[...]
        yb = (n * gb * sz).astype(BF16)
        u = ALPHA * x_ref[...] + _dot(yb, w_ref[...], NN)
        inv_d = 1.0 / D_MODEL
        mu = jnp.sum(u, axis=1, keepdims=True) * inv_d
        uc = u - mu
        rstd = lax.rsqrt(jnp.sum(uc * uc, axis=1, keepdims=True) * inv_d + LN_EPS)
        xh = uc * rstd
        lg = lg_ref[...]
        diff = xh * lg + lb_ref[...] - t_ref[...]
        acc_ref[0:1, :] += jnp.sum(diff * diff, axis=0, keepdims=True) * (0.5 * inv_d)
        dout = diff * inv_d
        acc_ref[2:3, :] += jnp.sum(dout * xh, axis=0, keepdims=True)
        acc_ref[3:4, :] += jnp.sum(dout, axis=0, keepdims=True)
        dxh = dout * lg
        du = rstd * (dxh - jnp.sum(dxh, axis=1, keepdims=True) * inv_d
                     - xh * (jnp.sum(dxh * xh, axis=1, keepdims=True) * inv_d))
        dub = du.astype(BF16)
        du_ref[...] = dub
        gout_ref[...] += _dot(yb, dub, TN)
        dy = _dot(dub, w_ref[...], NT)
        t1 = dy * sz
        acc_ref[1:2, :] += jnp.sum(t1 * n, axis=0, keepdims=True)
        dn = t1 * gb
        dz_ref[...] = (dy * n * gb * (sig * (1.0 + zf * (1.0 - sig)))).astype(BF16)

        def rms_bwd(dn_, n_, r):
            return r * (dn_ - n_ * (jnp.sum(dn_ * n_, axis=1, keepdims=True) * (1.0 / n_.shape[1])))

        doa = rms_bwd(dn[:, :W_A], na, ra)
        dob = rms_bwd(dn[:, W_A:W_A + W_B], nb_, rb)
        doc = rms_bwd(dn[:, W_A + W_B:], nc, rc)
        doa_ref[...] = doa.astype(BF16)
        dla_ref[...] = _dot2(doa * oav, ga8_ref)
        doc_ref[...] = doc.astype(BF16)
        dlc_ref[...] = _dot2(doc * ocv, ga4_ref)
        dlb = _dot2(dob * ob, ga4_ref)
        dobn_ref[...] = dob.astype(BF16)
        lsen_ref[...] = lse_b
        dlbn_ref[...] = dlb
        scr[0] = dob[:, :128]
        scr[1] = dob[:, 128:]
        scr[2] = lse_b
        scr[3] = dlb
        for res in range(4):
            rows = pl.ds(res, tm // 4, stride=4)
            for j in range(2):
                dob4_ref[0, res, :, 128 * j:128 * (j + 1)] = scr[j, rows, :].astype(BF16)
            lse4_ref[0, res] = scr[2, rows, :]
            dlb4_ref[0, res] = scr[3, rows, :]
        for res in range(16):
            rows = pl.ds(res, tm // 16, stride=16)
            for j in range(2):
                dob16_ref[0, res, :, 128 * j:128 * (j + 1)] = scr[j, rows, :].astype(BF16)
            lse16_ref[0, res] = scr[2, rows, :]
            dlb16_ref[0, res] = scr[3, rows, :]

    tok = lambda w: pl.BlockSpec((tm, w), lambda i: (i, 0))
    p4 = lambda w: pl.BlockSpec((1, 4, tm // 4, w), lambda i: (i // spt, 0, i % spt, 0))
    p16 = lambda w: pl.BlockSpec((1, 16, tm // 16, w), lambda i: (i // spt, 0, i % spt, 0))
    s4 = lambda w, dt: _sds((B_LOC, 4, SEQ // 4, w), dt)
    s16 = lambda w, dt: _sds((B_LOC, 16, SEQ // 16, w), dt)
    row = _full((1, D_MODEL))
    return pl.pallas_call(
        body, name="middle", grid=(T // tm,),
        in_specs=[tok(W_A), tok(W_B), tok(128), p4(W_B), p4(128), p16(W_B), p16(128), tok(W_C), tok(D_MIX),
                  tok(D_MODEL), tok(D_MODEL), row, row, row, _full((D_MIX, D_MODEL)),
                  _full((128, W_B)), _full((W_B, 128)), _full((W_A, 128))],
        out_specs=(tok(D_MODEL), tok(D_MIX), tok(W_A), tok(128),
                   tok(W_B), tok(128), tok(128), p4(W_B), p4(128), p4(128), p16(W_B), p16(128), p16(128),
                   tok(W_C), tok(128), _full((8, D_MODEL)), _full((D_MIX, D_MODEL))),
        out_shape=(_sds((T, D_MODEL), BF16), _sds((T, D_MIX), BF16),
                   _sds((T, W_A), BF16), _sds((T, 128), F32),
                   _sds((T, W_B), BF16), _sds((T, 128), F32), _sds((T, 128), F32),
                   s4(W_B, BF16), s4(128, F32), s4(128, F32), s16(W_B, BF16), s16(128, F32), s16(128, F32),
                   _sds((T, W_C), BF16), _sds((T, 128), F32), _sds((8, D_MODEL), F32),
                   _sds((D_MIX, D_MODEL), F32)),
        scratch_shapes=[pltpu.VMEM((6, tm, 128), F32)],
        compiler_params=_cp(("arbitrary",), vmem_mb=56),
    )(*_pin(oa, o1, l1, o4, l4, o16, l16, oc, z, x, tgt, g_br, ln_g, ln_b, wout, spread4, gather4, gather8))


class _ReduceScatter:
    def __init__(self, shapes):
        self.shapes = shapes

    def scratch_shapes(self):
        out = []
        for n, w in self.shapes:
            h = n // 2
            out += [pltpu.VMEM((4, h, w), F32), pltpu.VMEM((4, h, w), F32), pltpu.VMEM((3, h, w), BF16),
                    pltpu.VMEM((3, h, w), BF16), pltpu.VMEM((h, w), F32)]
        na = len(self.shapes)
        dma = pltpu.SemaphoreType.DMA
        return out + [dma((na, 4)), dma((na, 4)), dma((na, 4)), dma((na, 3)), dma((na, 3)), dma((na,)), dma((na,)),
                      dma((na,))]

    def bind(self, g_refs, r_refs, scratch):
        na = len(self.shapes)
        bufs = [scratch[5 * a:5 * a + 5] for a in range(na)]
        mine, sib, stage, land, tot = (tuple(b[i] for b in bufs) for i in range(5))
        loc_sem, s1_send, s1_recv, s2_send, s2_recv, s3_send, s3_recv, st_sem = scratch[5 * na:5 * na + 8]
        x, y, c = lax.axis_index("x"), lax.axis_index("y"), lax.axis_index("c")
        me, sibling = (x, y, c), (x, y, 1 - c)
        my_chip = 2 * x + y
        chips = [(1 - x, y), (x, 1 - y), (1 - x, 1 - y)]
        order = [2 * chip[0] + chip[1] for chip in chips] + [my_chip]

        def rows(a, k, half):
            n = self.shapes[a][0]
            return pl.ds(pl.multiple_of(k * n + half * (n // 2), 8), n // 2)

        def load(a, k):
            return pltpu.make_async_copy(g_refs[a].at[rows(a, k, c), :], mine[a].at[k], loc_sem.at[a, k])

        def s1(a, k, half):
            return pltpu.make_async_remote_copy(
                src_ref=g_refs[a].at[rows(a, k, half), :], dst_ref=sib[a].at[k],
                send_sem=s1_send.at[a, k], recv_sem=s1_recv.at[a, k], device_id=sibling, device_id_type=MESH)

        def s2(a, j, to):
            return pltpu.make_async_remote_copy(
                src_ref=stage[a].at[j], dst_ref=land[a].at[j], send_sem=s2_send.at[a, j], recv_sem=s2_recv.at[a, j],
                device_id=to, device_id_type=MESH)

        def s3(a, half, to):
            return pltpu.make_async_remote_copy(
                src_ref=tot[a], dst_ref=r_refs[a].at[rows(a, 0, half), :], send_sem=s3_send.at[a],
                recv_sem=s3_recv.at[a], device_id=to, device_id_type=MESH)

        def store(a):
            return pltpu.make_async_copy(tot[a], r_refs[a].at[rows(a, 0, c), :], st_sem.at[a])

        def start():
            for k in order:
                for a in range(na):
                    load(a, k).start()
                    s1(a, k, 1 - c).start()

        def exchange():
            for j, chip in enumerate(chips):
                k = order[j]
                for a in range(na):
                    load(a, k).wait()
                    s1(a, k, c).wait_recv()
                    stage[a][j] = (mine[a][k] + sib[a][k]).astype(BF16)
                    s2(a, j, (*chip, c)).start()
            for a in range(na):
                load(a, my_chip).wait()
                s1(a, my_chip, c).wait_recv()
                tot[a][...] = mine[a][my_chip] + sib[a][my_chip]

        def finish():
            for a in range(na):
                t = tot[a][...]
                for j in range(3):
                    s2(a, j, me).wait_recv()
                    t = t + land[a][j].astype(F32)
                tot[a][...] = t
                s3(a, c, sibling).start()
                store(a).start()

        def drain():
            for a in range(na):
                s3(a, 1 - c, me).wait_recv()
                store(a).wait()
            for a in range(na):
                for k in order:
                    s1(a, k, 1 - c).wait_send()
                for j, chip in enumerate(chips):
                    s2(a, j, (*chip, c)).wait_send()
                s3(a, c, sibling).wait_send()

        return start, exchange, finish, drain


def _dh_dx(dqa, dka, dva, dqn, dkn, dvn, dq4, dk4, dv4, dq16, dk16, dv16, dqc, dz, du, xb, cos, sa, sb, winT,
           g_out, g_mem):
    tm = 512
    spt = SEQ // tm
    n_steps = T // tm
    rs = _ReduceScatter([(SH_OUT, D_MODEL), (SH_MEM, 2 * W_C)])

    def body(dqa_ref, dka_ref, dva_ref, dqn_ref, dkn_ref, dvn_ref, dq4_ref, dk4_ref, dv4_ref,
             dq16_ref, dk16_ref, dv16_ref, dqc_ref, dz_ref, du_ref, xb_ref, cos_ref, sa_ref, sb_ref, w_ref,
             gout_ref, gmem_ref, gx_ref, db_ref, gin_ref, rout_ref, rmem_ref, dh_ref, scr, *rs_scratch):
        i = pl.program_id(0)
        phases = rs.bind((gout_ref, gmem_ref), (rout_ref, rmem_ref), rs_scratch)
        for step, phase in zip((0, 1, n_steps - 2, n_steps - 1), phases):
            pl.when(i == step)(phase)

        @pl.when(i == 0)
        def _():
            db_ref[...] = jnp.zeros_like(db_ref)
            gin_ref[...] = jnp.zeros_like(gin_ref)

        cos_t, sa_t, sb_t = cos_ref[...], sa_ref[...], sb_ref[...]

        def rope_t(t):
            return _rope(t, cos_t, sa_t, sb_t, -1)

        def put(r0, val):
            n = val.shape[1]
            dh_ref[:, r0:r0 + n] = val.astype(BF16)
            db_ref[:, r0:r0 + n] += jnp.sum(val, axis=0, keepdims=True)

        put(O_QA, rope_t(dqa_ref[...].astype(F32)) * QK_SCALE)
        put(O_KA, rope_t(dka_ref[...]))
        put(O_VA, dva_ref[...])
        put(O_QC, dqc_ref[...].astype(F32) * QK_SCALE)
        put(O_Z, dz_ref[...].astype(F32))
        for k, (n_ref, r4, r16) in enumerate(((dqn_ref, dq4_ref, dq16_ref), (dkn_ref, dk4_ref, dk16_ref),
                                               (dvn_ref, dv4_ref, dv16_ref))):
            for j in range(2):
                sl = slice(128 * j, 128 * (j + 1))
                scr[2 * k + j] = n_ref[:, sl].astype(F32)
                for res in range(4):
                    scr[2 * k + j, pl.ds(res, tm // 4, stride=4), :] += r4[0, res, :, sl].astype(F32)
                for res in range(16):
                    scr[2 * k + j, pl.ds(res, tm // 16, stride=16), :] += r16[0, res, :, sl].astype(F32)
        cat = lambda a: jnp.concatenate([scr[a], scr[a + 1]], axis=1)
        put(O_QB, rope_t(cat(0)) * QK_SCALE)
        put(O_KB, rope_t(cat(2)))
        put(O_VB, cat(4))
        gx_ref[...] = _dot(dh_ref[...], w_ref[...], NN) + ALPHA * du_ref[...].astype(F32)
        gin_ref[...] += _dot(dh_ref[...], xb_ref[...], TN)

    tok = lambda w: pl.BlockSpec((tm, w), lambda i: (i, 0))
    tab = pl.BlockSpec((tm, 128), lambda i: (i % spt, 0))
    p4 = pl.BlockSpec((1, 4, tm // 4, W_B), lambda i: (i // spt, 0, i % spt, 0))
    p16 = pl.BlockSpec((1, 16, tm // 16, W_B), lambda i: (i // spt, 0, i % spt, 0))
    once = lambda shape: pl.BlockSpec(shape, lambda i: (0, 0), pipeline_mode=pl.Buffered(1))
    hbm = pl.BlockSpec(memory_space=pl.ANY)
    return pl.pallas_call(
        body, name="dh_dx", grid=(n_steps,),
        in_specs=[tok(W_A), tok(W_KV_A), tok(W_KV_A), tok(W_B), tok(W_B), tok(W_B), p4, p4, p4, p16, p16, p16,
                  tok(W_C), tok(D_MIX), tok(D_MODEL), tok(D_MODEL), tab, tab, tab, once((D_IN, D_MODEL)), hbm, hbm],
        out_specs=(tok(D_MODEL), _full((1, D_IN)), once((D_IN, D_MODEL)), hbm, hbm),
        out_shape=(_sds((T, D_MODEL), F32), _sds((1, D_IN), F32), _sds((D_IN, D_MODEL), F32),
                   _sds((SH_OUT, D_MODEL), F32), _sds((SH_MEM, 2 * W_C), F32)),
        scratch_shapes=[pltpu.VMEM((tm, D_IN), BF16), pltpu.VMEM((6, tm, 128), F32)] + rs.scratch_shapes(),
        compiler_params=_cp(("arbitrary",), vmem_mb=60),
    )(*_pin(dqa, dka, dva, dqn, dkn, dvn, dq4, dk4, dv4, dq16, dk16, dv16, dqc, dz, du, xb, cos, sa, sb, winT,
            g_out, g_mem))


def _tn_matmul(name, a, b, bm, bt):
    n, m_all = a.shape
    n_cols = b.shape[1]

    def body(a_ref, b_ref, o_ref):
        @pl.when(pl.program_id(1) == 0)
        def _():
            o_ref[...] = jnp.zeros_like(o_ref)

        o_ref[...] += _dot(a_ref[...].astype(BF16), b_ref[...].astype(BF16), TN)

    return pl.pallas_call(
        body, name=name, grid=(m_all // bm, n // bt),
        in_specs=[pl.BlockSpec((bt, bm), lambda m, t: (t, m)), pl.BlockSpec((bt, n_cols), lambda m, t: (t, 0))],
        out_specs=pl.BlockSpec((bm, n_cols), lambda m, t: (m, 0)),
        out_shape=_sds((m_all, n_cols), F32),
        compiler_params=_cp(("parallel", "arbitrary"), vmem_mb=48),
    )(*_pin(a, b))


def _reduce_grads(g_in, acc, dbin, dsink):
    rs = _ReduceScatter([(SH_IN, D_MODEL)])

    def body(g_ref, acc_ref, dbin_ref, dsink_ref, r_ref, sv_ref, sv_mine, sv_all, sv_send, sv_recv, *rs_scratch):
        x, y, c = lax.axis_index("x"), lax.axis_index("y"), lax.axis_index("c")
        chips = [(1 - x, y), (x, 1 - y), (1 - x, 1 - y)]
        start, exchange, finish, drain = rs.bind((g_ref,), (r_ref,), rs_scratch)
        start()

        sv_mine[...] = jnp.zeros_like(sv_mine)
        sv_mine[0:4, 0:D_MODEL] = acc_ref[0:4, :]
        sv_mine[4:5, 0:D_IN] = dbin_ref[...]
        sv_mine[5:6, 0:128] = dsink_ref[...]
        my_dev = 4 * x + 2 * y + c
        others = [(x, y, 1 - c)] + [(*chip, cc) for chip in chips for cc in (c, 1 - c)]

        def sv_copy(j, to):
            return pltpu.make_async_remote_copy(
                src_ref=sv_mine, dst_ref=sv_all.at[my_dev], send_sem=sv_send.at[j], recv_sem=sv_recv.at[j],
                device_id=to, device_id_type=MESH)

        sv_sends = [sv_copy(j, to) for j, to in enumerate(others)]
        for cp in sv_sends:
            cp.start()
        exchange()
        finish()
        sv_all[my_dev] = sv_mine[...]
        for j in range(7):
            sv_copy(j, (x, y, c)).wait_recv()
        tot = sv_all[0]
        for d in range(1, 8):
            tot = tot + sv_all[d]
        sv_ref[...] = tot
        drain()
        for cp in sv_sends:
            cp.wait_send()

    vm = pl.BlockSpec(memory_space=pltpu.VMEM)
    hbm = pl.BlockSpec(memory_space=pl.ANY)
    return pl.pallas_call(
        body, name="reduce_grads",
        out_shape=(_sds((SH_IN, D_MODEL), F32), _vm_sds((8, SV_W), F32)),
        in_specs=[hbm, vm, vm, vm], out_specs=(hbm, vm),
        scratch_shapes=[pltpu.VMEM((8, SV_W), F32), pltpu.VMEM((8, 8, SV_W), F32),
                        pltpu.SemaphoreType.DMA((7,)), pltpu.SemaphoreType.DMA((7,))] + rs.scratch_shapes(),
        compiler_params=_cp(vmem_mb=40),
    )(pltpu.with_memory_space_constraint(g_in, pltpu.HBM), acc, dbin, dsink)


def _adamw(name, w, g, m, v, rows=None, copy_g=False):
    shape = w.shape
    rows = shape[0] if rows is None else rows
    n_out = 4 if copy_g else 3

    def body(w_ref, g_ref, m_ref, v_ref, d_ref, nm_ref, nv_ref, *go_ref):
        gv = g_ref[...]
        if copy_g:
            go_ref[0][...] = gv
        nm = ADAM_B1 * m_ref[...] + (1.0 - ADAM_B1) * gv
        nv = ADAM_B2 * v_ref[...] + (1.0 - ADAM_B2) * (gv * gv)
        m_hat = nm / (1.0 - ADAM_B1 ** ADAM_STEP)
        v_hat = nv / (1.0 - ADAM_B2 ** ADAM_STEP)
        d_ref[...] = -ADAM_LR * (m_hat / (jnp.sqrt(v_hat) + ADAM_EPS) + ADAM_WD * w_ref[...])
        nm_ref[...] = nm
        nv_ref[...] = nv

    spec = pl.BlockSpec((rows, shape[1]), lambda i: (i, 0))
    return pl.pallas_call(
        body, name=name, grid=(shape[0] // rows,), in_specs=[spec] * 4, out_specs=(spec,) * n_out,
        out_shape=(_sds(shape, F32),) * n_out, compiler_params=_cp(("parallel",)),
    )(*_pin(w, g, m, v))


def _rope_tables():
    pos = jnp.arange(SEQ, dtype=F32)
    inv = ROPE_THETA ** (-jnp.arange(0, 64, 2, dtype=F32) / 64)
    ang = pos[:, None] * inv[None, :]
    ang = jnp.concatenate([ang, ang, ang, ang], axis=-1)
    low = (jnp.arange(128) % 64) < 32
    cos, sin = jnp.cos(ang), jnp.sin(ang)
    return cos, jnp.where(low, -sin, 0.0), jnp.where(low, 0.0, sin)


def _local_step(x2, mem2, tgt2, winT, wout, wmem, b_in, sinks, g_branch, ln_gain, ln_bias):
    cos, sa, sb = _rope_tables()
    sinkv = jnp.pad(sinks, ((0, 0), (0, 120)))
    head_of_lane = jnp.arange(512)[None, :] // 64
    gather8 = (head_of_lane.T == jnp.arange(128)[None, :]).astype(BF16)
    gather4 = gather8[:W_B]
    spread4 = gather4.T

    xb, qa, ka, va, bn, b4, b16, qc, z, wout, wmem = _in_proj(x2, winT, b_in, cos, sa, sb, wout, wmem)
    memb, mkv = _mem_kv(mem2, wmem)
    b4f, b16f = b4.reshape(T, 768), b16.reshape(T, 768)

    swa = dict(kind="band", nb=SEQ // BLK, max_dist=BLK - 1, gqa=True)
    dil = (dict(kind="band", nb=SEQ // BLK), dict(kind="band", nb=SEQ // 4 // BLK), dict(kind="band", nb=1))
    (oa, lse_a), (o1, l1), (o4, l4), (o16, l16), (oc, lse_c) = _run_parts("attn_fwd", [
        _attn_fwd(qa, 0, W_A, ka, 0, va, 0, W_KV_A, sinks=sinks, **swa),
        _attn_fwd(bn, 0, W_B, bn, 1, bn, 2, W_B, **dil[0]),
        _attn_fwd(b4f, 0, W_B, b4f, 1, b4f, 2, W_B, **dil[1]),
        _attn_fwd(b16f, 0, W_B, b16f, 1, b16f, 2, W_B, **dil[2]),
        _attn_fwd(qc, 0, W_C, mkv, 0, mkv, 1, W_C, kind="mem")], "parallel", 48)

    s4 = lambda w: (B_LOC, 4, SEQ // 4, w)
    s16 = lambda w: (B_LOC, 16, SEQ // 16, w)
    (du, dz, doa, dla, dobn, lsen, dlbn, dob4, lse4, dlb4, dob16, lse16, dlb16, doc, dlc, acc, g_out) = _middle(
        oa, o1, l1, o4.reshape(s4(W_B)), l4.reshape(s4(128)), o16.reshape(s16(W_B)), l16.reshape(s16(128)), oc, z,
        x2, tgt2, g_branch, ln_gain, ln_bias, wout, spread4, gather4, gather8)

    flat = lambda a: a.reshape(T, a.shape[-1])
    (dqa, dka, dva, dsink), (dqc, dmkv) = _run_parts("attn_bwd_a", [
        _attn_bwd(qa, 0, W_A, ka, 0, va, 0, W_KV_A, doa, lse_a, dla, sinkv=sinkv, **swa),
        _attn_bwd(qc, 0, W_C, mkv, 0, mkv, 1, W_C, doc, lse_c, dlc, kind="mem")], "arbitrary", 48)
    (dqn, dkn, dvn), (dq4, dk4, dv4), (dq16, dk16, dv16) = _run_parts("attn_bwd_b", [
        _attn_bwd(bn, 0, W_B, bn, 1, bn, 2, W_B, dobn, lsen, dlbn, **dil[0]),
        _attn_bwd(b4f, 0, W_B, b4f, 1, b4f, 2, W_B, flat(dob4), flat(lse4), flat(dlb4), **dil[1]),
        _attn_bwd(b16f, 0, W_B, b16f, 1, b16f, 2, W_B, flat(dob16), flat(lse16), flat(dlb16), **dil[2])],
        "arbitrary", 56)

    r4 = lambda a: a.reshape(s4(W_B))
    r16 = lambda a: a.reshape(s16(W_B))
    g_mem = _tn_matmul("dw_mem", memb, dmkv, D_MODEL, B_LOC * MEM_LEN)
    gx, dbin, g_in, r_out, r_mem = _dh_dx(dqa, dka, dva, dqn, dkn, dvn, r4(dq4), r4(dk4), r4(dv4), r16(dq16),
                                          r16(dk16), r16(dv16), dqc, dz, du, xb, cos, sa, sb, winT, g_out, g_mem)
    return gx, g_in, r_out, r_mem, acc, dbin, dsink


def kernel(x, mem, w_in, b_in, w_mem, attn_sinks, g_branch, w_out, ln_gain, ln_bias, loss_target, m_w_in, m_b_in, m_w_mem, m_attn_sinks, m_g_branch, m_w_out, m_ln_gain, m_ln_bias, v_w_in, v_b_in, v_w_mem, v_attn_sinks, v_g_branch, v_w_out, v_ln_gain, v_ln_bias):
    winT, wout, wmem = _gather_weights(w_in[0].T, w_out[0], w_mem[0])
    gx, g_in, r_out, r_mem, acc, dbin, dsink = _local_step(
        x.reshape(T, D_MODEL), mem.reshape(B_LOC * MEM_LEN, D_MODEL), loss_target.reshape(T, D_MODEL),
        winT, wout, wmem, b_in, attn_sinks, g_branch, ln_gain, ln_bias)
    r_in, sv = _reduce_grads(g_in, acc, dbin, dsink)

    loss = jnp.sum(sv[0, :D_MODEL])
    grads = {
        "b_in": sv[4:5, :D_IN], "w_mem": r_mem[None],
        "attn_sinks": -sv[5:6, 0:8], "g_branch": sv[1:2, :D_MODEL], "w_out": r_out[None],
        "ln_gain": sv[2:3, :D_MODEL], "ln_bias": sv[3:4, :D_MODEL],
    }
    weights = dict(w_in=w_in, b_in=b_in, w_mem=w_mem, attn_sinks=attn_sinks, g_branch=g_branch, w_out=w_out,
                   ln_gain=ln_gain, ln_bias=ln_bias)
    ms = dict(w_in=m_w_in, b_in=m_b_in, w_mem=m_w_mem, attn_sinks=m_attn_sinks, g_branch=m_g_branch, w_out=m_w_out,
              ln_gain=m_ln_gain, ln_bias=m_ln_bias)
    vs = dict(w_in=v_w_in, b_in=v_b_in, w_mem=v_w_mem, attn_sinks=v_attn_sinks, g_branch=v_g_branch, w_out=v_w_out,
              ln_gain=v_ln_gain, ln_bias=v_ln_bias)
    names = ["w_in", "b_in", "w_mem", "attn_sinks", "g_branch", "w_out", "ln_gain", "ln_bias"]
    deltas, new_m, new_v = [], [], []
    for n in names:
        shape = weights[n].shape
        two_d = lambda a: a.reshape(shape[-2], shape[-1])
        if n == "w_in":
            d, nm, nv, gw = (a.T for a in _adamw("adamw_w_in", w_in[0].T, r_in, m_w_in[0].T, v_w_in[0].T, SH_IN // 4,
                                                 copy_g=True))
            grads[n] = gw
        elif n in ("w_out", "w_mem"):
            d, nm, nv, grads[n] = _adamw("adamw_" + n, two_d(weights[n]), two_d(grads[n]), two_d(ms[n]), two_d(vs[n]),
                                         copy_g=True)
        else:
            d, nm, nv = _adamw("adamw_" + n, two_d(weights[n]), two_d(grads[n]), two_d(ms[n]), two_d(vs[n]))
        deltas.append(d.reshape(shape))
        new_m.append(nm.reshape(shape))
        new_v.append(nv.reshape(shape))
    return (loss, gx.reshape(B_LOC, SEQ, D_MODEL), *[grads[n].reshape(weights[n].shape) for n in names],
            *deltas, *new_m, *new_v)
```

```python
import functools

import jax
import jax.numpy as jnp
from jax import lax
from jax.experimental import pallas as pl
from jax.experimental.pallas import tpu as pltpu

F32, BF16 = jnp.float32, jnp.bfloat16

D_MODEL = 1024
SEQ = 2048
B_LOC = 2
T = B_LOC * SEQ
BLK = 128
MEM_LEN = 256
W_A, W_KV_A, W_B, W_C, D_MIX = 512, 128, 256, 256, 1024
D_IN = 2816
O_QA, O_KA, O_VA, O_QB, O_KB, O_VB, O_QC, O_Z = 0, 512, 640, 768, 1024, 1280, 1536, 1792
ROPE_THETA = 10000.0
LN_EPS = 1e-5
RMS_EPS = 1e-6
ALPHA = 2.0 ** 0.25
QK_SCALE = 0.125
N_CHIP = 4
SH_IN, SH_OUT, SH_MEM = D_IN // N_CHIP, D_MIX // N_CHIP, D_MODEL // N_CHIP
NEG = -1e30
ADAM_LR, ADAM_B1, ADAM_B2, ADAM_EPS, ADAM_WD, ADAM_STEP = 0.001, 0.9, 0.999, 1e-08, 0.01, 10
SV_W = 3072
MESH = pl.DeviceIdType.MESH

NN = ((1,), (0,))
NT = ((1,), (1,))
TN = ((0,), (0,))


def _dot(a, b, dims):
    return lax.dot_general(a, b, (dims, ((), ())), preferred_element_type=F32)


def _cp(sem=None, vmem_mb=None):
    kw = {}
    if sem is not None:
        kw["dimension_semantics"] = sem
    if vmem_mb is not None:
        kw["vmem_limit_bytes"] = vmem_mb * 1024 * 1024
    return pltpu.CompilerParams(**kw)


def _sds(shape, dtype):
    return pltpu.HBM(shape, dtype)


def _vm_sds(shape, dtype):
    return jax.ShapeDtypeStruct(shape, dtype)


def _pin(*args):
    return [pltpu.with_memory_space_constraint(a, pltpu.HBM) for a in args]


def _full(shape):
    n = len(shape)
    return pl.BlockSpec(shape, lambda *_: (0,) * n)


def _shard_rows(ref, n, chip, half):
    start = pl.multiple_of((2 * chip[0] + chip[1]) * n + half * (n // 2), 16)
    return ref.at[pl.ds(start, n // 2), :]


def _gather_weights(win_sh, wout_sh, wmem_sh):
    def body(a_ref, b_ref, c_ref, oa_ref, ob_ref, oc_ref, ici_send, ici_recv, d2d_send, d2d_recv):
        x, y, c = lax.axis_index("x"), lax.axis_index("y"), lax.axis_index("c")
        sibling = (x, y, 1 - c)
        chips = [(1 - x, y), (x, 1 - y), (1 - x, 1 - y)]
        for src, out, n in ((a_ref, oa_ref, SH_IN), (b_ref, ob_ref, SH_OUT), (c_ref, oc_ref, SH_MEM)):
            out[pl.ds(pl.multiple_of((2 * x + y) * n, 16), n), :] = src[...].astype(BF16)

        def copy(sems, j, chip_of_block, half, to):
            blk = _shard_rows(oa_ref, SH_IN, chip_of_block, half)
            return pltpu.make_async_remote_copy(
                src_ref=blk, dst_ref=blk, send_sem=sems[0].at[j], recv_sem=sems[1].at[j],
                device_id=to, device_id_type=MESH)

        ici, d2d = (ici_send, ici_recv), (d2d_send, d2d_recv)
        first = [copy(ici, j, (x, y), c, (*chip, c)) for j, chip in enumerate(chips)]
        for cp in first:
            cp.start()
        passed = []
        for j, chip in enumerate(chips):
            copy(ici, j, chip, c, (x, y, c)).wait_recv()
            fw = copy(d2d, j, chip, c, sibling)
            fw.start()
            passed.append(fw)
        for j, chip in enumerate(chips):
            copy(d2d, j, chip, 1 - c, (x, y, c)).wait_recv()
        for cp in first + passed:
            cp.wait_send()

    vm = pl.BlockSpec(memory_space=pltpu.VMEM)
    return pl.pallas_call(
        body, name="gather_weights",
        out_shape=(_vm_sds((D_IN, D_MODEL), BF16), _vm_sds((D_MIX, D_MODEL), BF16),
                   _vm_sds((D_MODEL, 2 * W_C), BF16)),
        in_specs=[vm, vm, vm], out_specs=(vm, vm, vm),
        scratch_shapes=[pltpu.SemaphoreType.DMA((3,))] * 4,
        compiler_params=_cp(vmem_mb=40),
    )(win_sh, wout_sh, wmem_sh)


def _rope(t, cos, sa, sb, sign):
    w = t.shape[1]
    reps = w // 128
    c, a, b = (jnp.tile(v, (1, reps)) if reps > 1 else v for v in (cos, sa, sb))
    rot = pltpu.roll(t, w - 32, 1) * a + pltpu.roll(t, 32, 1) * b
    return t * c + rot if sign > 0 else t * c - rot


def _in_proj(x, winT, b_in, cos, sa, sb, wout_own, wmem_own):
    tm = 256
    spt = SEQ // tm
    n_steps = T // tm
    forward_step = n_steps // 2

    def body(x_ref, w_ref, b_ref, cos_ref, sa_ref, sb_ref, wo_in, wm_in,
             xb_ref, qa_ref, ka_ref, va_ref, bn_ref, b4_ref, b16_ref, qc_ref, z_ref, wo_ref, wm_ref,
             scr, ici_send, ici_recv, d2d_send, d2d_recv):
        i = pl.program_id(0)
        mx, my, mc = lax.axis_index("x"), lax.axis_index("y"), lax.axis_index("c")
        chips = [(1 - mx, my), (mx, 1 - my), (1 - mx, 1 - my)]
        full = ((wo_ref, SH_OUT), (wm_ref, SH_MEM))

        def copy(sems, a, j, chip_of_block, half, to):
            blk = _shard_rows(full[a][0], full[a][1], chip_of_block, half)
            return pltpu.make_async_remote_copy(
                src_ref=blk, dst_ref=blk, send_sem=sems[0].at[a, j], recv_sem=sems[1].at[a, j],
                device_id=to, device_id_type=MESH)

        ici, d2d = (ici_send, ici_recv), (d2d_send, d2d_recv)
        pairs = [(a, j, chip) for j, chip in enumerate(chips) for a in range(2)]

        @pl.when(i == 0)
        def _():
            for a, j, chip in pairs:
                copy(ici, a, j, (mx, my), mc, (*chip, mc)).start()

        @pl.when(i == forward_step)
        def _():
            for a, j, chip in pairs:
                copy(ici, a, j, chip, mc, (mx, my, mc)).wait_recv()
                copy(d2d, a, j, chip, mc, (mx, my, 1 - mc)).start()

        @pl.when(i == n_steps - 1)
        def _():
            for a, j, chip in pairs:
                copy(d2d, a, j, chip, 1 - mc, (mx, my, mc)).wait_recv()
            for a, j, chip in pairs:
                copy(ici, a, j, (mx, my), mc, (*chip, mc)).wait_send()
                copy(d2d, a, j, chip, mc, (mx, my, 1 - mc)).wait_send()

        xb = x_ref[...].astype(BF16)
        xb_ref[...] = xb
        cos_t, sa_t, sb_t = cos_ref[...], sa_ref[...], sb_ref[...]

        def proj(r0, n):
            return _dot(xb, w_ref[r0:r0 + n, :], NT) + b_ref[:, r0:r0 + n]

        def rope(t):
            return _rope(t, cos_t, sa_t, sb_t, +1)

        qa_ref[...] = (rope(proj(O_QA, W_A)) * QK_SCALE).astype(BF16)
        ka_ref[...] = rope(proj(O_KA, W_KV_A)).astype(BF16)
        va_ref[...] = proj(O_VA, W_KV_A).astype(BF16)
        qc_ref[...] = (proj(O_QC, W_C) * QK_SCALE).astype(BF16)
        z_ref[...] = proj(O_Z, D_MIX).astype(BF16)
        parts = (rope(proj(O_QB, W_B)) * QK_SCALE, rope(proj(O_KB, W_B)), proj(O_VB, W_B))
        for k, part in enumerate(parts):
            bn_ref[:, 256 * k:256 * (k + 1)] = part.astype(BF16)
            scr[2 * k] = part[:, :128]
            scr[2 * k + 1] = part[:, 128:]
        for j in range(6):
            for res in range(4):
                b4_ref[0, res, :, 128 * j:128 * (j + 1)] = scr[j, pl.ds(res, tm // 4, stride=4), :].astype(BF16)
            for res in range(16):
                b16_ref[0, res, :, 128 * j:128 * (j + 1)] = scr[j, pl.ds(res, tm // 16, stride=16), :].astype(BF16)

    tok = lambda w: pl.BlockSpec((tm, w), lambda i: (i, 0))
    tab = pl.BlockSpec((tm, 128), lambda i: (i % spt, 0))
    hbm = pl.BlockSpec(memory_space=pl.ANY)
    return pl.pallas_call(
        body, name="in_proj", grid=(n_steps,),
        in_specs=[tok(D_MODEL), _full((D_IN, D_MODEL)), _full((1, D_IN)), tab, tab, tab, hbm, hbm],
        out_specs=(tok(D_MODEL), tok(W_A), tok(W_KV_A), tok(W_KV_A), tok(768),
                   pl.BlockSpec((1, 4, tm // 4, 768), lambda i: (i // spt, 0, i % spt, 0)),
                   pl.BlockSpec((1, 16, tm // 16, 768), lambda i: (i // spt, 0, i % spt, 0)),
                   tok(W_C), tok(D_MIX), hbm, hbm),
        out_shape=(_sds((T, D_MODEL), BF16), _sds((T, W_A), BF16), _sds((T, W_KV_A), BF16), _sds((T, W_KV_A), BF16),
                   _sds((T, 768), BF16), _sds((B_LOC, 4, SEQ // 4, 768), BF16), _sds((B_LOC, 16, SEQ // 16, 768), BF16),
                   _sds((T, W_C), BF16), _sds((T, D_MIX), BF16),
                   _sds((D_MIX, D_MODEL), BF16), _sds((D_MODEL, 2 * W_C), BF16)),
        input_output_aliases={6: 9, 7: 10},
        scratch_shapes=[pltpu.VMEM((6, tm, 128), F32)] + [pltpu.SemaphoreType.DMA((2, 3))] * 4,
        compiler_params=_cp(("arbitrary",), vmem_mb=48),
    )(*_pin(x, winT, b_in, cos, sa, sb, wout_own, wmem_own))


def _mem_kv(mem, wmem):
    def body(m_ref, w_ref, mb_ref, kv_ref):
        mb = m_ref[...].astype(BF16)
        mb_ref[...] = mb
        kv_ref[...] = _dot(mb, w_ref[...], NN).astype(BF16)

    n = B_LOC * MEM_LEN
    return pl.pallas_call(
        body, name="mem_kv",
        out_shape=(_sds((n, D_MODEL), BF16), _sds((n, 2 * W_C), BF16)),
    )(*_pin(mem, wmem))


class _Part:
    def __init__(self, body, args, in_specs, out_specs, out_shape, scratch=()):
        self.body, self.args, self.in_specs, self.out_specs, self.out_shape = body, args, in_specs, out_specs, out_shape
        self.scratch = list(scratch)


def _run_parts(name, parts, semantics, vmem_mb):
    n_in = [len(p.args) for p in parts]
    n_out = [len(p.out_shape) for p in parts]
    n_scr = [len(p.scratch) for p in parts]

    def body(*refs):
        ins, outs, scr = refs[:sum(n_in)], refs[sum(n_in):sum(n_in) + sum(n_out)], refs[sum(n_in) + sum(n_out):]
        i0 = o0 = s0 = 0
        for p, ni, no, ns in zip(parts, n_in, n_out, n_scr):
            p.body(*ins[i0:i0 + ni], *outs[o0:o0 + no], *scr[s0:s0 + ns])
            i0, o0, s0 = i0 + ni, o0 + no, s0 + ns

    res = pl.pallas_call(
        body, name=name, grid=(T // QR,),
        in_specs=[sp for p in parts for sp in p.in_specs], out_specs=tuple(sp for p in parts for sp in p.out_specs),
        out_shape=tuple(sh for p in parts for sh in p.out_shape),
        scratch_shapes=[sc for p in parts for sc in p.scratch],
        compiler_params=_cp((semantics,), vmem_mb=vmem_mb),
    )(*_pin(*[a for p in parts for a in p.args]))
    out, o0 = [], 0
    for no in n_out:
        out.append(tuple(res[o0:o0 + no]))
        o0 += no
    return out


QB = 8
QR = QB * BLK


def _lane_lo():
    return lax.broadcasted_iota(jnp.int32, (1, 128), 1) < 64


def _dup_head(k2, hk, lo):
    kf = k2.astype(F32)
    r = pltpu.roll(kf, 64, 1)
    return (jnp.where(lo, kf, r) if hk == 0 else jnp.where(lo, r, kf)).astype(BF16)


def _stack_heads(pairs, lo):
    parts = []
    for x2 in pairs:
        z = jnp.zeros_like(x2)
        parts += [jnp.where(lo, x2, z), jnp.where(lo, z, x2)]
    return jnp.concatenate(parts, axis=0)


def _prev_mode(kind, nb, j):
    if kind == "mem" or nb == 1:
        return "no"
    if nb <= QB:
        return "yes" if j % nb else "no"
    return "yes" if j else "dyn"


class _Attn:
    def __init__(self, kind, nb, max_dist, gqa, qw, kvw, qcb, kcb, vcb):
        self.kind, self.nb, self.gqa, self.qw, self.kvw = kind, nb, gqa, qw, kvw
        npairs = qw // 128
        self.groups = ([(hk, [2 * hk, 2 * hk + 1]) for hk in range(npairs // 2)] if gqa
                       else [(p, [p]) for p in range(npairs)])
        self.nh = 2 * len(self.groups[0][1])
        self.cols = 128 * self.nh
        self.reach = BLK - max_dist
        self.ext_prev = kind == "band" and nb > QB
        self.q_spec = pl.BlockSpec((QR, qw), lambda g: (g, qcb))
        self.row_spec = pl.BlockSpec((QR, qw), lambda g: (g, 0))
        self.stat_spec = pl.BlockSpec((QR, 128), lambda g: (g, 0))
        if kind == "mem":
            per = SEQ // QR
            self.kv_specs = [pl.BlockSpec((MEM_LEN, kvw), lambda g: (g // per, kcb)),
                             pl.BlockSpec((MEM_LEN, kvw), lambda g: (g // per, vcb))]
        else:
            self.kv_specs = [pl.BlockSpec((QR, kvw), lambda g: (g, kcb)), pl.BlockSpec((QR, kvw), lambda g: (g, vcb))]
            if self.ext_prev:
                self.kv_specs += [pl.BlockSpec((BLK, kvw), lambda g: (jnp.maximum(g * QB - 1, 0), kcb)),
                                  pl.BlockSpec((BLK, kvw), lambda g: (jnp.maximum(g * QB - 1, 0), vcb))]

    def masks(self):
        if self.kind == "mem":
            return None
        kj = lax.broadcasted_iota(jnp.int32, (2 * BLK, self.cols), 0)
        qi = lax.broadcasted_iota(jnp.int32, (2 * BLK, self.cols), 1) & (BLK - 1)
        kj1 = lax.broadcasted_iota(jnp.int32, (BLK, self.cols), 0)
        qi1 = lax.broadcasted_iota(jnp.int32, (BLK, self.cols), 1) & (BLK - 1)
        return kj, qi, kj1 <= qi1

    def keys(self, j, gi, kc_ref, vc_ref, kp_ref, vp_ref, lo, kq, g):
        def kv(k_ref, v_ref, r):
            if self.gqa:
                return _dup_head(k_ref[r, :], gi, lo), _dup_head(v_ref[r, :], gi, lo)
            sl = slice(128 * gi, 128 * (gi + 1))
            return k_ref[r, sl], v_ref[r, sl]

        if self.kind == "mem":
            key0 = pl.multiple_of((g // (SEQ // QR)) * MEM_LEN, MEM_LEN)
            return (*kv(kc_ref, vc_ref, slice(None)), None, [(0, MEM_LEN, key0)])
        kj, qi, cur = kq
        row0 = g * QR + BLK * j
        mode = _prev_mode(self.kind, self.nb, j)
        if mode == "no":
            return (*kv(kc_ref, vc_ref, slice(BLK * j, BLK * (j + 1))), cur, [(0, BLK, pl.multiple_of(row0, BLK))])
        if mode == "yes":
            mask = jnp.logical_and(kj >= qi + self.reach, kj <= qi + BLK)
            return (*kv(kc_ref, vc_ref, slice(BLK * (j - 1), BLK * (j + 1))), mask,
                    [(0, 2 * BLK, pl.multiple_of(row0 - BLK, BLK))])
        has_prev = ((g * QB) % self.nb) > 0
        hp = has_prev.astype(jnp.int32)
        mask = jnp.logical_and(kj >= qi * hp + (self.reach * hp + BLK * (1 - hp)), kj <= qi + BLK)
        kp, vp = kv(kp_ref, vp_ref, slice(None))
        kc, vc = kv(kc_ref, vc_ref, slice(0, BLK))
        return (jnp.concatenate([kp, kc], axis=0), jnp.concatenate([vp, vc], axis=0), mask,
                [(0, BLK, pl.multiple_of(jnp.maximum(row0 - BLK, 0), BLK)), (BLK, BLK, pl.multiple_of(row0, BLK))])


def _attn_fwd(q, qcb, qw, k, kcb, v, vcb, kvw, *, kind, nb=1, max_dist=BLK, gqa=False, sinks=None):
    a = _Attn(kind, nb, max_dist, gqa, qw, kvw, qcb, kcb, vcb)

    def body(*refs):
        it = iter(refs)
        q_ref, kc_ref, vc_ref = next(it), next(it), next(it)
        kp_ref, vp_ref = (next(it), next(it)) if a.ext_prev else (None, None)
        sink_ref = next(it) if sinks is not None else None
        o_ref, lse_ref = next(it), next(it)
        g = pl.program_id(0)
        lo = _lane_lo()
        top = lax.broadcasted_iota(jnp.int32, (128, 1), 0) < 64
        rid = lax.broadcasted_iota(jnp.int32, (8, 128), 0)
        kq = a.masks()
        for j in range(QB):
            rows = slice(BLK * j, BLK * (j + 1))
            stat = jnp.zeros((8, 128), F32)
            for gi, pairs in a.groups:
                qs = _stack_heads([q_ref[rows, 128 * p:128 * (p + 1)] for p in pairs], lo)
                kk, vv, mask, _ = a.keys(j, gi, kc_ref, vc_ref, kp_ref, vp_ref, lo, kq, g)
                pieces = [slice(r0, r0 + BLK) for r0 in range(0, kk.shape[0], BLK)]
                ss = []
                for r in pieces:
                    s = _dot(kk[r], qs, NT)
                    ss.append(s if mask is None else jnp.where(mask[r], s, NEG))
                m = jnp.max(ss[0], axis=0, keepdims=True)
                for s in ss[1:]:
                    m = jnp.maximum(m, jnp.max(s, axis=0, keepdims=True))
                if sink_ref is not None:
                    sk = jnp.concatenate([jnp.full((1, 128), sink_ref[0, a.nh * gi + i], F32) for i in range(a.nh)],
                                         axis=1)
                    m = jnp.maximum(m, sk)
                l, ot = None, None
                for r, s in zip(pieces, ss):
                    p = jnp.exp(s - m)
                    ps = jnp.sum(p, axis=0, keepdims=True)
                    c = _dot(vv[r], p.astype(BF16), TN)
                    l, ot = (ps, c) if l is None else (l + ps, ot + c)
                if sink_ref is not None:
                    l = l + jnp.exp(sk - m)
                ot = ot * pl.reciprocal(l, approx=True)
                lse = m + jnp.log(l)
                for i, p in enumerate(pairs):
                    o2t = jnp.where(top, ot[:, 256 * i:256 * i + 128], ot[:, 256 * i + 128:256 * i + 256])
                    o_ref[rows, 128 * p:128 * (p + 1)] = o2t.T.astype(BF16)
                for i in range(a.nh):
                    stat = jnp.where(rid == a.nh * gi + i, lse[:, 128 * i:128 * (i + 1)], stat)
            lse_ref[rows, :] = jnp.concatenate([stat, jnp.zeros((120, 128), F32)], axis=0).T

    args = [q, k, v] + ([k, v] if a.ext_prev else [])
    in_specs = [a.q_spec] + a.kv_specs
    if sinks is not None:
        args.append(sinks)
        in_specs.append(pl.BlockSpec(memory_space=pltpu.SMEM))
    return _Part(body, args, in_specs, [a.row_spec, a.stat_spec], [_sds((T, qw), BF16), _sds((T, 128), F32)])


def _attn_bwd(q, qcb, qw, k, kcb, v, vcb, kvw, do, lse, dl, *, kind, nb=1, max_dist=BLK, gqa=False, sinkv=None):
    a = _Attn(kind, nb, max_dist, gqa, qw, kvw, qcb, kcb, vcb)

    def body(*refs):
        it = iter(refs)
        q_ref, kc_ref, vc_ref = next(it), next(it), next(it)
        kp_ref, vp_ref = (next(it), next(it)) if a.ext_prev else (None, None)
        do_ref, lse_ref, dl_ref = next(it), next(it), next(it)
        sinkv_ref = next(it) if sinkv is not None else None
        dq_ref = next(it)
        if kind == "mem":
            dkv_ref = next(it)
        else:
            dk_ref, dv_ref = next(it), next(it)
        dsink_ref = next(it) if sinkv is not None else None
        g = pl.program_id(0)
        lo = _lane_lo()
        top = lax.broadcasted_iota(jnp.int32, (128, 1), 0) < 64

        @pl.when(g == 0)
        def _():
            if kind == "mem":
                dkv_ref[...] = jnp.zeros_like(dkv_ref)
            else:
                dk_ref[...] = jnp.zeros_like(dk_ref)
                dv_ref[...] = jnp.zeros_like(dv_ref)
            if dsink_ref is not None:
                dsink_ref[...] = jnp.zeros_like(dsink_ref)

        kq = a.masks()
        for j in range(QB):
            rows = slice(BLK * j, BLK * (j + 1))
            lse_t = lse_ref[rows, :].T
            dl_t = dl_ref[rows, :].T
            for gi, pairs in a.groups:
                heads = [a.nh * gi + i for i in range(a.nh)]
                qs = _stack_heads([q_ref[rows, 128 * p:128 * (p + 1)] for p in pairs], lo)
                dos = _stack_heads([do_ref[rows, 128 * p:128 * (p + 1)] for p in pairs], lo)
                lse_row = jnp.concatenate([lse_t[h:h + 1, :] for h in heads], axis=1)
                dl_row = jnp.concatenate([dl_t[h:h + 1, :] for h in heads], axis=1)
                kk, vv, mask, dests = a.keys(j, gi, kc_ref, vc_ref, kp_ref, vp_ref, lo, kq, g)
                s = _dot(kk, qs, NT)
                if mask is not None:
                    s = jnp.where(mask, s, NEG)
                p = jnp.exp(s - lse_row)
                ds = (p * (_dot(vv, dos, NT) - dl_row)).astype(BF16)
                dqt = _dot(kk, ds, TN)
                ck = _dot(ds, qs, NN)
                cv = _dot(p.astype(BF16), dos, NN)
                if gqa:
                    sel = lo if gi == 0 else jnp.logical_not(lo)
                    ck = jnp.where(sel, ck + pltpu.roll(ck, 64, 1), 0.0)
                    cv = jnp.where(sel, cv + pltpu.roll(cv, 64, 1), 0.0)
                    kcols = slice(0, 128)
                else:
                    kcols = slice(128 * gi, 128 * (gi + 1))
                for r0, nr, key0 in dests:
                    krows = pl.ds(key0, nr)
                    if kind == "mem":
                        dkv_ref[krows, kcols] += ck[r0:r0 + nr]
                        dkv_ref[krows, slice(kvw + kcols.start, kvw + kcols.stop)] += cv[r0:r0 + nr]
                    else:
                        dk_ref[krows, kcols] += ck[r0:r0 + nr]
                        dv_ref[krows, kcols] += cv[r0:r0 + nr]
                for i, p in enumerate(pairs):
                    dq2t = jnp.where(top, dqt[:, 256 * i:256 * i + 128], dqt[:, 256 * i + 128:256 * i + 256])
                    dq_ref[rows, 128 * p:128 * (p + 1)] = dq2t.T.astype(BF16)
        if dsink_ref is not None:
            ps = jnp.exp(sinkv_ref[...] - lse_ref[...]) * dl_ref[...]
            dsink_ref[...] += jnp.sum(ps, axis=0, keepdims=True)

    args = [q, k, v] + ([k, v] if a.ext_prev else []) + [do, lse, dl]
    in_specs = [a.q_spec] + a.kv_specs + [a.row_spec, a.stat_spec, a.stat_spec]
    if sinkv is not None:
        args.append(sinkv)
        in_specs.append(_full((1, 128)))
    out_shape = [_sds((T, qw), BF16)]
    out_specs = [a.row_spec]
    once = lambda shape: pl.BlockSpec(shape, lambda g: (0, 0), pipeline_mode=pl.Buffered(1))
    if kind == "mem":
        out_shape.append(_sds((B_LOC * MEM_LEN, 2 * kvw), F32))
        out_specs.append(once((B_LOC * MEM_LEN, 2 * kvw)))
    else:
        out_shape += [_sds((T, kvw), F32)] * 2
        out_specs += [once((T, kvw))] * 2
    if sinkv is not None:
        out_shape.append(_sds((1, 128), F32))
        out_specs.append(_full((1, 128)))
    return _Part(body, args, in_specs, out_specs, out_shape)


def _dot2(v, w_ref):
    hi = v.astype(BF16)
    lo = (v - hi.astype(F32)).astype(BF16)
    return _dot(hi, w_ref[...], NN) + _dot(lo, w_ref[...], NN)


def _middle(oa, o1, l1, o4, l4, o16, l16, oc, z, x, tgt, g_br, ln_g, ln_b, wout, spread4, gather4, gather8):
    tm = 256
    spt = SEQ // tm

    def body(oa_ref, o1_ref, l1_ref, o4_ref, l4_ref, o16_ref, l16_ref, oc_ref, z_ref, x_ref, t_ref,
             g_ref, lg_ref, lb_ref, w_ref, sp4_ref, ga4_ref, ga8_ref,
             du_ref, dz_ref, doa_ref, dla_ref,
             dobn_ref, lsen_ref, dlbn_ref, dob4_ref, lse4_ref, dlb4_ref, dob16_ref, lse16_ref, dlb16_ref,
             doc_ref, dlc_ref, acc_ref, gout_ref, scr):
        i = pl.program_id(0)

        @pl.when(i == 0)
        def _():
            acc_ref[...] = jnp.zeros_like(acc_ref)
            gout_ref[...] = jnp.zeros_like(gout_ref)

        for res in range(4):
            rows = pl.ds(res, tm // 4, stride=4)
            for j in range(2):
                scr[j, rows, :] = o4_ref[0, res, :, 128 * j:128 * (j + 1)].astype(F32)
            scr[2, rows, :] = l4_ref[0, res]
        for res in range(16):
            rows = pl.ds(res, tm // 16, stride=16)
            for j in range(2):
                scr[3 + j, rows, :] = o16_ref[0, res, :, 128 * j:128 * (j + 1)].astype(F32)
            scr[5, rows, :] = l16_ref[0, res]
        cat = lambda a: jnp.concatenate([scr[a], scr[a + 1]], axis=1)
        o1v, o4v, o16v = o1_ref[...].astype(F32), cat(0), cat(3)
        l1v, l4v, l16v = l1_ref[...], scr[2], scr[5]
        mx = jnp.maximum(jnp.maximum(l1v, l4v), l16v)
        e1, e4, e16 = jnp.exp(l1v - mx), jnp.exp(l4v - mx), jnp.exp(l16v - mx)
        ssum = e1 + e4 + e16
        lse_b = mx + jnp.log(ssum)
        inv = 1.0 / ssum
        ob = (_dot2(e1 * inv, sp4_ref) * o1v + _dot2(e4 * inv, sp4_ref) * o4v + _dot2(e16 * inv, sp4_ref) * o16v)
        oav, ocv = oa_ref[...].astype(F32), oc_ref[...].astype(F32)

        def rms(o):
            r = lax.rsqrt(jnp.sum(o * o, axis=1, keepdims=True) * (1.0 / o.shape[1]) + RMS_EPS)
            return o * r, r

        na, ra = rms(oav)
        nb_, rb = rms(ob)
        nc, rc = rms(ocv)
        n = jnp.concatenate([na, nb_, nc], axis=1)
        zf = z_ref[...].astype(F32)
        sig = 1.0 / (1.0 + jnp.exp(-zf))
        sz = zf * sig
        gb = g_ref[...]
        yb = (n * gb * sz).astype(BF16)
        u = ALPHA * x_ref[...] + _dot(yb, w_ref[...], NN)
        inv_d = 1.0 / D_MODEL
        mu = jnp.sum(u, axis=1, keepdims=True) * inv_d
        uc = u - mu
        rstd = lax.rsqrt(jnp.sum(uc * uc, axis=1, keepdims=True) * inv_d + LN_EPS)
        xh = uc * rstd
        lg = lg_ref[...]
        diff = xh * lg + lb_ref[...] - t_ref[...]
        acc_ref[0:1, :] += jnp.sum(diff * diff, axis=0, keepdims=True) * (0.5 * inv_d)
        dout = diff * inv_d
        acc_ref[2:3, :] += jnp.sum(dout * xh, axis=0, keepdims=True)
        acc_ref[3:4, :] += jnp.sum(dout, axis=0, keepdims=True)
        dxh = dout * lg
        du = rstd * (dxh - jnp.sum(dxh, axis=1, keepdims=True) * inv_d
                     - xh * (jnp.sum(dxh * xh, axis=1, keepdims=True) * inv_d))
        dub = du.astype(BF16)
        du_ref[...] = dub
        gout_ref[...] += _dot(yb, dub, TN)
        dy = _dot(dub, w_ref[...], NT)
        t1 = dy * sz
        acc_ref[1:2, :] += jnp.sum(t1 * n, axis=0, keepdims=True)
        dn = t1 * gb
        dz_ref[...] = (dy * n * gb * (sig * (1.0 + zf * (1.0 - sig)))).astype(BF16)

        def rms_bwd(dn_, n_, r):
            return r * (dn_ - n_ * (jnp.sum(dn_ * n_, axis=1, keepdims=True) * (1.0 / n_.shape[1])))

        doa = rms_bwd(dn[:, :W_A], na, ra)
        dob = rms_bwd(dn[:, W_A:W_A + W_B], nb_, rb)
        doc = rms_bwd(dn[:, W_A + W_B:], nc, rc)
        doa_ref[...] = doa.astype(BF16)
        dla_ref[...] = _dot2(doa * oav, ga8_ref)
        doc_ref[...] = doc.astype(BF16)
        dlc_ref[...] = _dot2(doc * ocv, ga4_ref)
        dlb = _dot2(dob * ob, ga4_ref)
        dobn_ref[...] = dob.astype(BF16)
        lsen_ref[...] = lse_b
        dlbn_ref[...] = dlb
        scr[0] = dob[:, :128]
        scr[1] = dob[:, 128:]
        scr[2] = lse_b
        scr[3] = dlb
        for res in range(4):
            rows = pl.ds(res, tm // 4, stride=4)
            for j in range(2):
                dob4_ref[0, res, :, 128 * j:128 * (j + 1)] = scr[j, rows, :].astype(BF16)
            lse4_ref[0, res] = scr[2, rows, :]
            dlb4_ref[0, res] = scr[3, rows, :]
        for res in range(16):
            rows = pl.ds(res, tm // 16, stride=16)
            for j in range(2):
                dob16_ref[0, res, :, 128 * j:128 * (j + 1)] = scr[j, rows, :].astype(BF16)
            lse16_ref[0, res] = scr[2, rows, :]
            dlb16_ref[0, res] = scr[3, rows, :]

    tok = lambda w: pl.BlockSpec((tm, w), lambda i: (i, 0))
    p4 = lambda w: pl.BlockSpec((1, 4, tm // 4, w), lambda i: (i // spt, 0, i % spt, 0))
    p16 = lambda w: pl.BlockSpec((1, 16, tm // 16, w), lambda i: (i // spt, 0, i % spt, 0))
    s4 = lambda w, dt: _sds((B_LOC, 4, SEQ // 4, w), dt)
    s16 = lambda w, dt: _sds((B_LOC, 16, SEQ // 16, w), dt)
    row = _full((1, D_MODEL))
    return pl.pallas_call(
        body, name="middle", grid=(T // tm,),
        in_specs=[tok(W_A), tok(W_B), tok(128), p4(W_B), p4(128), p16(W_B), p16(128), tok(W_C), tok(D_MIX),
                  tok(D_MODEL), tok(D_MODEL), row, row, row, _full((D_MIX, D_MODEL)),
                  _full((128, W_B)), _full((W_B, 128)), _full((W_A, 128))],
        out_specs=(tok(D_MODEL), tok(D_MIX), tok(W_A), tok(128),
                   tok(W_B), tok(128), tok(128), p4(W_B), p4(128), p4(128), p16(W_B), p16(128), p16(128),
                   tok(W_C), tok(128), _full((8, D_MODEL)), _full((D_MIX, D_MODEL))),
        out_shape=(_sds((T, D_MODEL), BF16), _sds((T, D_MIX), BF16),
                   _sds((T, W_A), BF16), _sds((T, 128), F32),
                   _sds((T, W_B), BF16), _sds((T, 128), F32), _sds((T, 128), F32),
                   s4(W_B, BF16), s4(128, F32), s4(128, F32), s16(W_B, BF16), s16(128, F32), s16(128, F32),
                   _sds((T, W_C), BF16), _sds((T, 128), F32), _sds((8, D_MODEL), F32),
                   _sds((D_MIX, D_MODEL), F32)),
        scratch_shapes=[pltpu.VMEM((6, tm, 128), F32)],
        compiler_params=_cp(("arbitrary",), vmem_mb=56),
    )(*_pin(oa, o1, l1, o4, l4, o16, l16, oc, z, x, tgt, g_br, ln_g, ln_b, wout, spread4, gather4, gather8))


class _ReduceScatter:
    def __init__(self, shapes):
        self.shapes = shapes

    def scratch_shapes(self):
        out = []
        for n, w in self.shapes:
            h = n // 2
            out += [pltpu.VMEM((4, h, w), F32), pltpu.VMEM((4, h, w), F32), pltpu.VMEM((3, h, w), BF16),
                    pltpu.VMEM((3, h, w), BF16), pltpu.VMEM((h, w), F32)]
        na = len(self.shapes)
        dma = pltpu.SemaphoreType.DMA
        return out + [dma((na, 4)), dma((na, 4)), dma((na, 4)), dma((na, 3)), dma((na, 3)), dma((na,)), dma((na,)),
                      dma((na,))]

    def bind(self, g_refs, r_refs, scratch):
        na = len(self.shapes)
        bufs = [scratch[5 * a:5 * a + 5] for a in range(na)]
        mine, sib, stage, land, tot = (tuple(b[i] for b in bufs) for i in range(5))
        loc_sem, s1_send, s1_recv, s2_send, s2_recv, s3_send, s3_recv, st_sem = scratch[5 * na:5 * na + 8]
        x, y, c = lax.axis_index("x"), lax.axis_index("y"), lax.axis_index("c")
        me, sibling = (x, y, c), (x, y, 1 - c)
        my_chip = 2 * x + y
        chips = [(1 - x, y), (x, 1 - y), (1 - x, 1 - y)]
        order = [2 * chip[0] + chip[1] for chip in chips] + [my_chip]

        def rows(a, k, half):
            n = self.shapes[a][0]
            return pl.ds(pl.multiple_of(k * n + half * (n // 2), 8), n // 2)

        def load(a, k):
            return pltpu.make_async_copy(g_refs[a].at[rows(a, k, c), :], mine[a].at[k], loc_sem.at[a, k])

        def s1(a, k, half):
            return pltpu.make_async_remote_copy(
                src_ref=g_refs[a].at[rows(a, k, half), :], dst_ref=sib[a].at[k],
                send_sem=s1_send.at[a, k], recv_sem=s1_recv.at[a, k], device_id=sibling, device_id_type=MESH)

        def s2(a, j, to):
            return pltpu.make_async_remote_copy(
                src_ref=stage[a].at[j], dst_ref=land[a].at[j], send_sem=s2_send.at[a, j], recv_sem=s2_recv.at[a, j],
                device_id=to, device_id_type=MESH)

        def s3(a, half, to):
            return pltpu.make_async_remote_copy(
                src_ref=tot[a], dst_ref=r_refs[a].at[rows(a, 0, half), :], send_sem=s3_send.at[a],
                recv_sem=s3_recv.at[a], device_id=to, device_id_type=MESH)

        def store(a):
            return pltpu.make_async_copy(tot[a], r_refs[a].at[rows(a, 0, c), :], st_sem.at[a])

        def start():
            for k in order:
                for a in range(na):
                    load(a, k).start()
                    s1(a, k, 1 - c).start()

        def exchange():
            for j, chip in enumerate(chips):
                k = order[j]
                for a in range(na):
                    load(a, k).wait()
                    s1(a, k, c).wait_recv()
                    stage[a][j] = (mine[a][k] + sib[a][k]).astype(BF16)
                    s2(a, j, (*chip, c)).start()
            for a in range(na):
                load(a, my_chip).wait()
                s1(a, my_chip, c).wait_recv()
                tot[a][...] = mine[a][my_chip] + sib[a][my_chip]

        def finish():
            for a in range(na):
                t = tot[a][...]
                for j in range(3):
                    s2(a, j, me).wait_recv()
                    t = t + land[a][j].astype(F32)
                tot[a][...] = t
                s3(a, c, sibling).start()
                store(a).start()

        def drain():
            for a in range(na):
                s3(a, 1 - c, me).wait_recv()
                store(a).wait()
            for a in range(na):
                for k in order:
                    s1(a, k, 1 - c).wait_send()
                for j, chip in enumerate(chips):
                    s2(a, j, (*chip, c)).wait_send()
                s3(a, c, sibling).wait_send()

        return start, exchange, finish, drain

    def part(self, grads, steps):
        def body(*refs):
            na = len(self.shapes)
            i = pl.program_id(0)
            for step, phase in zip(steps, self.bind(refs[:na], refs[na:2 * na], refs[2 * na:])):
                pl.when(i == step)(phase)

        hbm = pl.BlockSpec(memory_space=pl.ANY)
        return _Part(body, list(grads), [hbm] * len(grads), [hbm] * len(grads),
                     [_sds((n, w), F32) for n, w in self.shapes], self.scratch_shapes())


def _dh_dx(dqa, dka, dva, dqn, dkn, dvn, dq4, dk4, dv4, dq16, dk16, dv16, dqc, dz, du, xb, cos, sa, sb, winT):
    tm = 512
    spt = SEQ // tm

    def body(dqa_ref, dka_ref, dva_ref, dqn_ref, dkn_ref, dvn_ref, dq4_ref, dk4_ref, dv4_ref,
             dq16_ref, dk16_ref, dv16_ref, dqc_ref, dz_ref, du_ref, xb_ref, cos_ref, sa_ref, sb_ref, w_ref,
             gx_ref, db_ref, gin_ref, dh_ref, scr):
        i = pl.program_id(0)

        @pl.when(i == 0)
        def _():
            db_ref[...] = jnp.zeros_like(db_ref)
            gin_ref[...] = jnp.zeros_like(gin_ref)

        cos_t, sa_t, sb_t = cos_ref[...], sa_ref[...], sb_ref[...]

        def rope_t(t):
            return _rope(t, cos_t, sa_t, sb_t, -1)

        def put(r0, val):
            n = val.shape[1]
            dh_ref[:, r0:r0 + n] = val.astype(BF16)
            db_ref[:, r0:r0 + n] += jnp.sum(val, axis=0, keepdims=True)

        put(O_QA, rope_t(dqa_ref[...].astype(F32)) * QK_SCALE)
        put(O_KA, rope_t(dka_ref[...]))
        put(O_VA, dva_ref[...])
        put(O_QC, dqc_ref[...].astype(F32) * QK_SCALE)
        put(O_Z, dz_ref[...].astype(F32))
        for k, (n_ref, r4, r16) in enumerate(((dqn_ref, dq4_ref, dq16_ref), (dkn_ref, dk4_ref, dk16_ref),
                                               (dvn_ref, dv4_ref, dv16_ref))):
            for j in range(2):
                sl = slice(128 * j, 128 * (j + 1))
                scr[2 * k + j] = n_ref[:, sl].astype(F32)
                for res in range(4):
                    scr[2 * k + j, pl.ds(res, tm // 4, stride=4), :] += r4[0, res, :, sl].astype(F32)
                for res in range(16):
                    scr[2 * k + j, pl.ds(res, tm // 16, stride=16), :] += r16[0, res, :, sl].astype(F32)
        cat = lambda a: jnp.concatenate([scr[a], scr[a + 1]], axis=1)
        put(O_QB, rope_t(cat(0)) * QK_SCALE)
        put(O_KB, rope_t(cat(2)))
        put(O_VB, cat(4))
        gx_ref[...] = _dot(dh_ref[...], w_ref[...], NN) + ALPHA * du_ref[...].astype(F32)
        gin_ref[...] += _dot(dh_ref[...], xb_ref[...], TN)

    tok = lambda w: pl.BlockSpec((tm, w), lambda i: (i, 0))
    tab = pl.BlockSpec((tm, 128), lambda i: (i % spt, 0))
    p4 = pl.BlockSpec((1, 4, tm // 4, W_B), lambda i: (i // spt, 0, i % spt, 0))
    p16 = pl.BlockSpec((1, 16, tm // 16, W_B), lambda i: (i // spt, 0, i % spt, 0))
    once = lambda shape: pl.BlockSpec(shape, lambda i: (0, 0), pipeline_mode=pl.Buffered(1))
    return pl.pallas_call(
        body, name="dh_dx", grid=(T // tm,),
        in_specs=[tok(W_A), tok(W_KV_A), tok(W_KV_A), tok(W_B), tok(W_B), tok(W_B), p4, p4, p4, p16, p16, p16,
                  tok(W_C), tok(D_MIX), tok(D_MODEL), tok(D_MODEL), tab, tab, tab, once((D_IN, D_MODEL))],
        out_specs=(tok(D_MODEL), _full((1, D_IN)), once((D_IN, D_MODEL))),
        out_shape=(_sds((T, D_MODEL), F32), _sds((1, D_IN), F32), _sds((D_IN, D_MODEL), F32)),
        scratch_shapes=[pltpu.VMEM((tm, D_IN), BF16), pltpu.VMEM((6, tm, 128), F32)],
        compiler_params=_cp(("arbitrary",), vmem_mb=56),
    )(*_pin(dqa, dka, dva, dqn, dkn, dvn, dq4, dk4, dv4, dq16, dk16, dv16, dqc, dz, du, xb, cos, sa, sb, winT))


def _tn_matmul(name, a, b, bm, bt):
    n, m_all = a.shape
    n_cols = b.shape[1]

    def body(a_ref, b_ref, o_ref):
        @pl.when(pl.program_id(1) == 0)
        def _():
            o_ref[...] = jnp.zeros_like(o_ref)

        o_ref[...] += _dot(a_ref[...].astype(BF16), b_ref[...].astype(BF16), TN)

    return pl.pallas_call(
        body, name=name, grid=(m_all // bm, n // bt),
        in_specs=[pl.BlockSpec((bt, bm), lambda m, t: (t, m)), pl.BlockSpec((bt, n_cols), lambda m, t: (t, 0))],
        out_specs=pl.BlockSpec((bm, n_cols), lambda m, t: (m, 0)),
        out_shape=_sds((m_all, n_cols), F32),
        compiler_params=_cp(("parallel", "arbitrary"), vmem_mb=48),
    )(*_pin(a, b))


def _reduce_grads(g_in, acc, dbin, dsink):
    rs = _ReduceScatter([(SH_IN, D_MODEL)])

    def body(g_ref, acc_ref, dbin_ref, dsink_ref, r_ref, sv_ref, sv_mine, sv_all, sv_send, sv_recv, *rs_scratch):
        x, y, c = lax.axis_index("x"), lax.axis_index("y"), lax.axis_index("c")
        chips = [(1 - x, y), (x, 1 - y), (1 - x, 1 - y)]
        start, exchange, finish, drain = rs.bind((g_ref,), (r_ref,), rs_scratch)
        start()

        sv_mine[...] = jnp.zeros_like(sv_mine)
        sv_mine[0:4, 0:D_MODEL] = acc_ref[0:4, :]
        sv_mine[4:5, 0:D_IN] = dbin_ref[...]
        sv_mine[5:6, 0:128] = dsink_ref[...]
        my_dev = 4 * x + 2 * y + c
        others = [(x, y, 1 - c)] + [(*chip, cc) for chip in chips for cc in (c, 1 - c)]

        def sv_copy(j, to):
            return pltpu.make_async_remote_copy(
                src_ref=sv_mine, dst_ref=sv_all.at[my_dev], send_sem=sv_send.at[j], recv_sem=sv_recv.at[j],
                device_id=to, device_id_type=MESH)

        sv_sends = [sv_copy(j, to) for j, to in enumerate(others)]
        for cp in sv_sends:
            cp.start()
        exchange()
        finish()
        sv_all[my_dev] = sv_mine[...]
        for j in range(7):
            sv_copy(j, (x, y, c)).wait_recv()
        tot = sv_all[0]
        for d in range(1, 8):
            tot = tot + sv_all[d]
        sv_ref[...] = tot
        drain()
        for cp in sv_sends:
            cp.wait_send()

    vm = pl.BlockSpec(memory_space=pltpu.VMEM)
    hbm = pl.BlockSpec(memory_space=pl.ANY)
    return pl.pallas_call(
        body, name="reduce_grads",
        out_shape=(_sds((SH_IN, D_MODEL), F32), _vm_sds((8, SV_W), F32)),
        in_specs=[hbm, vm, vm, vm], out_specs=(hbm, vm),
        scratch_shapes=[pltpu.VMEM((8, SV_W), F32), pltpu.VMEM((8, 8, SV_W), F32),
                        pltpu.SemaphoreType.DMA((7,)), pltpu.SemaphoreType.DMA((7,))] + rs.scratch_shapes(),
        compiler_params=_cp(vmem_mb=40),
    )(pltpu.with_memory_space_constraint(g_in, pltpu.HBM), acc, dbin, dsink)


def _adamw(name, w, g, m, v, rows=None, copy_g=False):
    shape = w.shape
    rows = shape[0] if rows is None else rows
    n_out = 4 if copy_g else 3

    def body(w_ref, g_ref, m_ref, v_ref, d_ref, nm_ref, nv_ref, *go_ref):
        gv = g_ref[...]
        if copy_g:
            go_ref[0][...] = gv
        nm = ADAM_B1 * m_ref[...] + (1.0 - ADAM_B1) * gv
        nv = ADAM_B2 * v_ref[...] + (1.0 - ADAM_B2) * (gv * gv)
        m_hat = nm / (1.0 - ADAM_B1 ** ADAM_STEP)
        v_hat = nv / (1.0 - ADAM_B2 ** ADAM_STEP)
        d_ref[...] = -ADAM_LR * (m_hat / (jnp.sqrt(v_hat) + ADAM_EPS) + ADAM_WD * w_ref[...])
        nm_ref[...] = nm
        nv_ref[...] = nv

    spec = pl.BlockSpec((rows, shape[1]), lambda i: (i, 0))
    return pl.pallas_call(
        body, name=name, grid=(shape[0] // rows,), in_specs=[spec] * 4, out_specs=(spec,) * n_out,
        out_shape=(_sds(shape, F32),) * n_out, compiler_params=_cp(("parallel",)),
    )(*_pin(w, g, m, v))


def _rope_tables():
    pos = jnp.arange(SEQ, dtype=F32)
    inv = ROPE_THETA ** (-jnp.arange(0, 64, 2, dtype=F32) / 64)
    ang = pos[:, None] * inv[None, :]
    ang = jnp.concatenate([ang, ang, ang, ang], axis=-1)
    low = (jnp.arange(128) % 64) < 32
    cos, sin = jnp.cos(ang), jnp.sin(ang)
    return cos, jnp.where(low, -sin, 0.0), jnp.where(low, 0.0, sin)


def _local_step(x2, mem2, tgt2, winT, wout, wmem, b_in, sinks, g_branch, ln_gain, ln_bias):
    cos, sa, sb = _rope_tables()
    sinkv = jnp.pad(sinks, ((0, 0), (0, 120)))
    head_of_lane = jnp.arange(512)[None, :] // 64
    gather8 = (head_of_lane.T == jnp.arange(128)[None, :]).astype(BF16)
    gather4 = gather8[:W_B]
    spread4 = gather4.T

    xb, qa, ka, va, bn, b4, b16, qc, z, wout, wmem = _in_proj(x2, winT, b_in, cos, sa, sb, wout, wmem)
    memb, mkv = _mem_kv(mem2, wmem)
    b4f, b16f = b4.reshape(T, 768), b16.reshape(T, 768)

    swa = dict(kind="band", nb=SEQ // BLK, max_dist=BLK - 1, gqa=True)
    dil = (dict(kind="band", nb=SEQ // BLK), dict(kind="band", nb=SEQ // 4 // BLK), dict(kind="band", nb=1))
    (oa, lse_a), (o1, l1), (o4, l4), (o16, l16), (oc, lse_c) = _run_parts("attn_fwd", [
        _attn_fwd(qa, 0, W_A, ka, 0, va, 0, W_KV_A, sinks=sinks, **swa),
        _attn_fwd(bn, 0, W_B, bn, 1, bn, 2, W_B, **dil[0]),
        _attn_fwd(b4f, 0, W_B, b4f, 1, b4f, 2, W_B, **dil[1]),
        _attn_fwd(b16f, 0, W_B, b16f, 1, b16f, 2, W_B, **dil[2]),
        _attn_fwd(qc, 0, W_C, mkv, 0, mkv, 1, W_C, kind="mem")], "parallel", 48)

    s4 = lambda w: (B_LOC, 4, SEQ // 4, w)
    s16 = lambda w: (B_LOC, 16, SEQ // 16, w)
    (du, dz, doa, dla, dobn, lsen, dlbn, dob4, lse4, dlb4, dob16, lse16, dlb16, doc, dlc, acc, g_out) = _middle(
        oa, o1, l1, o4.reshape(s4(W_B)), l4.reshape(s4(128)), o16.reshape(s16(W_B)), l16.reshape(s16(128)), oc, z,
        x2, tgt2, g_branch, ln_gain, ln_bias, wout, spread4, gather4, gather8)

    flat = lambda a: a.reshape(T, a.shape[-1])
    (dqa, dka, dva, dsink), (dqc, dmkv) = _run_parts("attn_bwd_a", [
        _attn_bwd(qa, 0, W_A, ka, 0, va, 0, W_KV_A, doa, lse_a, dla, sinkv=sinkv, **swa),
        _attn_bwd(qc, 0, W_C, mkv, 0, mkv, 1, W_C, doc, lse_c, dlc, kind="mem")], "arbitrary", 48)
    g_mem = _tn_matmul("dw_mem", memb, dmkv, D_MODEL, B_LOC * MEM_LEN)
    last = T // QR - 1
    (r_out, r_mem), (dqn, dkn, dvn), (dq4, dk4, dv4), (dq16, dk16, dv16) = _run_parts("attn_bwd_b", [
        _ReduceScatter([(SH_OUT, D_MODEL), (SH_MEM, 2 * W_C)]).part((g_out, g_mem), (0, 1, last, last)),
        _attn_bwd(bn, 0, W_B, bn, 1, bn, 2, W_B, dobn, lsen, dlbn, **dil[0]),
        _attn_bwd(b4f, 0, W_B, b4f, 1, b4f, 2, W_B, flat(dob4), flat(lse4), flat(dlb4), **dil[1]),
        _attn_bwd(b16f, 0, W_B, b16f, 1, b16f, 2, W_B, flat(dob16), flat(lse16), flat(dlb16), **dil[2])],
        "arbitrary", 60)

    r4 = lambda a: a.reshape(s4(W_B))
    r16 = lambda a: a.reshape(s16(W_B))
    gx, dbin, g_in = _dh_dx(dqa, dka, dva, dqn, dkn, dvn, r4(dq4), r4(dk4), r4(dv4), r16(dq16), r16(dk16),
                            r16(dv16), dqc, dz, du, xb, cos, sa, sb, winT)
    return gx, g_in, r_out, r_mem, acc, dbin, dsink


def kernel(x, mem, w_in, b_in, w_mem, attn_sinks, g_branch, w_out, ln_gain, ln_bias, loss_target, m_w_in, m_b_in, m_w_mem, m_attn_sinks, m_g_branch, m_w_out, m_ln_gain, m_ln_bias, v_w_in, v_b_in, v_w_mem, v_attn_sinks, v_g_branch, v_w_out, v_ln_gain, v_ln_bias):
    winT, wout, wmem = _gather_weights(w_in[0].T, w_out[0], w_mem[0])
    gx, g_in, r_out, r_mem, acc, dbin, dsink = _local_step(
        x.reshape(T, D_MODEL), mem.reshape(B_LOC * MEM_LEN, D_MODEL), loss_target.reshape(T, D_MODEL),
        winT, wout, wmem, b_in, attn_sinks, g_branch, ln_gain, ln_bias)
    r_in, sv = _reduce_grads(g_in, acc, dbin, dsink)

    loss = jnp.sum(sv[0, :D_MODEL])
    grads = {
        "b_in": sv[4:5, :D_IN], "w_mem": r_mem[None],
        "attn_sinks": -sv[5:6, 0:8], "g_branch": sv[1:2, :D_MODEL], "w_out": r_out[None],
        "ln_gain": sv[2:3, :D_MODEL], "ln_bias": sv[3:4, :D_MODEL],
    }
    weights = dict(w_in=w_in, b_in=b_in, w_mem=w_mem, attn_sinks=attn_sinks, g_branch=g_branch, w_out=w_out,
                   ln_gain=ln_gain, ln_bias=ln_bias)
    ms = dict(w_in=m_w_in, b_in=m_b_in, w_mem=m_w_mem, attn_sinks=m_attn_sinks, g_branch=m_g_branch, w_out=m_w_out,
              ln_gain=m_ln_gain, ln_bias=m_ln_bias)
    vs = dict(w_in=v_w_in, b_in=v_b_in, w_mem=v_w_mem, attn_sinks=v_attn_sinks, g_branch=v_g_branch, w_out=v_w_out,
              ln_gain=v_ln_gain, ln_bias=v_ln_bias)
    names = ["w_in", "b_in", "w_mem", "attn_sinks", "g_branch", "w_out", "ln_gain", "ln_bias"]
    deltas, new_m, new_v = [], [], []
    for n in names:
        shape = weights[n].shape
        two_d = lambda a: a.reshape(shape[-2], shape[-1])
        if n == "w_in":
            d, nm, nv, gw = (a.T for a in _adamw("adamw_w_in", w_in[0].T, r_in, m_w_in[0].T, v_w_in[0].T, SH_IN // 4,
                                                 copy_g=True))
            grads[n] = gw
        elif n in ("w_out", "w_mem"):
            d, nm, nv, grads[n] = _adamw("adamw_" + n, two_d(weights[n]), two_d(grads[n]), two_d(ms[n]), two_d(vs[n]),
                                         copy_g=True)
        else:
            d, nm, nv = _adamw("adamw_" + n, two_d(weights[n]), two_d(grads[n]), two_d(ms[n]), two_d(vs[n]))
        deltas.append(d.reshape(shape))
        new_m.append(nm.reshape(shape))
        new_v.append(nv.reshape(shape))
    return (loss, gx.reshape(B_LOC, SEQ, D_MODEL), *[grads[n].reshape(weights[n].shape) for n in names],
            *deltas, *new_m, *new_v)
```

```python
import functools

import jax
import jax.numpy as jnp
from jax import lax
from jax.experimental import pallas as pl
from jax.experimental.pallas import tpu as pltpu

F32, BF16 = jnp.float32, jnp.bfloat16

D_MODEL = 1024
SEQ = 2048
B_LOC = 2
T = B_LOC * SEQ
BLK = 128
MEM_LEN = 256
W_A, W_KV_A, W_B, W_C, D_MIX = 512, 128, 256, 256, 1024
D_IN = 2816
O_QA, O_KA, O_VA, O_QB, O_KB, O_VB, O_QC, O_Z = 0, 512, 640, 768, 1024, 1280, 1536, 1792
ROPE_THETA = 10000.0
LN_EPS = 1e-5
RMS_EPS = 1e-6
ALPHA = 2.0 ** 0.25
QK_SCALE = 0.125
N_CHIP = 4
SH_IN, SH_OUT, SH_MEM = D_IN // N_CHIP, D_MIX // N_CHIP, D_MODEL // N_CHIP
NEG = -1e30
ADAM_LR, ADAM_B1, ADAM_B2, ADAM_EPS, ADAM_WD, ADAM_STEP = 0.001, 0.9, 0.999, 1e-08, 0.01, 10
SV_W = 3072
MESH = pl.DeviceIdType.MESH

NN = ((1,), (0,))
NT = ((1,), (1,))
TN = ((0,), (0,))


def _dot(a, b, dims):
    return lax.dot_general(a, b, (dims, ((), ())), preferred_element_type=F32)


def _cp(sem=None, vmem_mb=None):
    kw = {}
    if sem is not None:
        kw["dimension_semantics"] = sem
    if vmem_mb is not None:
        kw["vmem_limit_bytes"] = vmem_mb * 1024 * 1024
    return pltpu.CompilerParams(**kw)


def _sds(shape, dtype):
    return pltpu.HBM(shape, dtype)


def _vm_sds(shape, dtype):
    return jax.ShapeDtypeStruct(shape, dtype)


def _pin(*args):
    return [pltpu.with_memory_space_constraint(a, pltpu.HBM) for a in args]


def _full(shape):
    n = len(shape)
    return pl.BlockSpec(shape, lambda *_: (0,) * n)


def _shard_rows(ref, n, chip, half):
    start = pl.multiple_of((2 * chip[0] + chip[1]) * n + half * (n // 2), 16)
    return ref.at[pl.ds(start, n // 2), :]


def _gather_weights(win_sh, wout_sh, wmem_sh):
    def body(a_ref, b_ref, c_ref, oa_ref, ob_ref, oc_ref, ici_send, ici_recv, d2d_send, d2d_recv):
        x, y, c = lax.axis_index("x"), lax.axis_index("y"), lax.axis_index("c")
        sibling = (x, y, 1 - c)
        chips = [(1 - x, y), (x, 1 - y), (1 - x, 1 - y)]
        for src, out, n in ((a_ref, oa_ref, SH_IN), (b_ref, ob_ref, SH_OUT), (c_ref, oc_ref, SH_MEM)):
            out[pl.ds(pl.multiple_of((2 * x + y) * n, 16), n), :] = src[...].astype(BF16)

        def copy(sems, j, chip_of_block, half, to):
            blk = _shard_rows(oa_ref, SH_IN, chip_of_block, half)
            return pltpu.make_async_remote_copy(
                src_ref=blk, dst_ref=blk, send_sem=sems[0].at[j], recv_sem=sems[1].at[j],
                device_id=to, device_id_type=MESH)

        ici, d2d = (ici_send, ici_recv), (d2d_send, d2d_recv)
        first = [copy(ici, j, (x, y), c, (*chip, c)) for j, chip in enumerate(chips)]
        for cp in first:
            cp.start()
        passed = []
        for j, chip in enumerate(chips):
            copy(ici, j, chip, c, (x, y, c)).wait_recv()
            fw = copy(d2d, j, chip, c, sibling)
            fw.start()
            passed.append(fw)
        for j, chip in enumerate(chips):
            copy(d2d, j, chip, 1 - c, (x, y, c)).wait_recv()
        for cp in first + passed:
            cp.wait_send()

    vm = pl.BlockSpec(memory_space=pltpu.VMEM)
    return pl.pallas_call(
        body, name="gather_weights",
        out_shape=(_vm_sds((D_IN, D_MODEL), BF16), _vm_sds((D_MIX, D_MODEL), BF16),
                   _vm_sds((D_MODEL, 2 * W_C), BF16)),
        in_specs=[vm, vm, vm], out_specs=(vm, vm, vm),
        scratch_shapes=[pltpu.SemaphoreType.DMA((3,))] * 4,
        compiler_params=_cp(vmem_mb=40),
    )(win_sh, wout_sh, wmem_sh)


def _rope(t, cos, sa, sb, sign):
    w = t.shape[1]
    reps = w // 128
    c, a, b = (jnp.tile(v, (1, reps)) if reps > 1 else v for v in (cos, sa, sb))
    rot = pltpu.roll(t, w - 32, 1) * a + pltpu.roll(t, 32, 1) * b
    return t * c + rot if sign > 0 else t * c - rot


def _in_proj(x, winT, b_in, cos, sa, sb, wout_own, wmem_own):
    tm = 512
    spt = SEQ // tm
    n_steps = T // tm
    forward_step = n_steps // 2

    def body(x_ref, w_ref, b_ref, cos_ref, sa_ref, sb_ref, wo_in, wm_in,
             xb_ref, qa_ref, ka_ref, va_ref, bn_ref, b4_ref, b16_ref, qc_ref, z_ref, wo_ref, wm_ref,
             scr, ici_send, ici_recv, d2d_send, d2d_recv):
        i = pl.program_id(0)
        mx, my, mc = lax.axis_index("x"), lax.axis_index("y"), lax.axis_index("c")
        chips = [(1 - mx, my), (mx, 1 - my), (1 - mx, 1 - my)]
        full = ((wo_ref, SH_OUT), (wm_ref, SH_MEM))

        def copy(sems, a, j, chip_of_block, half, to):
            blk = _shard_rows(full[a][0], full[a][1], chip_of_block, half)
            return pltpu.make_async_remote_copy(
                src_ref=blk, dst_ref=blk, send_sem=sems[0].at[a, j], recv_sem=sems[1].at[a, j],
                device_id=to, device_id_type=MESH)

        ici, d2d = (ici_send, ici_recv), (d2d_send, d2d_recv)
        pairs = [(a, j, chip) for j, chip in enumerate(chips) for a in range(2)]

        @pl.when(i == 0)
        def _():
            for a, j, chip in pairs:
                copy(ici, a, j, (mx, my), mc, (*chip, mc)).start()

        @pl.when(i == forward_step)
        def _():
            for a, j, chip in pairs:
                copy(ici, a, j, chip, mc, (mx, my, mc)).wait_recv()
                copy(d2d, a, j, chip, mc, (mx, my, 1 - mc)).start()

        @pl.when(i == n_steps - 1)
        def _():
            for a, j, chip in pairs:
                copy(d2d, a, j, chip, 1 - mc, (mx, my, mc)).wait_recv()
            for a, j, chip in pairs:
                copy(ici, a, j, (mx, my), mc, (*chip, mc)).wait_send()
                copy(d2d, a, j, chip, mc, (mx, my, 1 - mc)).wait_send()

        xb = x_ref[...].astype(BF16)
        xb_ref[...] = xb
        cos_t, sa_t, sb_t = cos_ref[...], sa_ref[...], sb_ref[...]

        def proj(r0, n):
            return _dot(xb, w_ref[r0:r0 + n, :], NT) + b_ref[:, r0:r0 + n]

        def rope(t):
            return _rope(t, cos_t, sa_t, sb_t, +1)

        qa_ref[...] = (rope(proj(O_QA, W_A)) * QK_SCALE).astype(BF16)
        ka_ref[...] = rope(proj(O_KA, W_KV_A)).astype(BF16)
        va_ref[...] = proj(O_VA, W_KV_A).astype(BF16)
        qc_ref[...] = (proj(O_QC, W_C) * QK_SCALE).astype(BF16)
        z_ref[...] = proj(O_Z, D_MIX).astype(BF16)
        parts = (rope(proj(O_QB, W_B)) * QK_SCALE, rope(proj(O_KB, W_B)), proj(O_VB, W_B))
        for k, part in enumerate(parts):
            bn_ref[:, 256 * k:256 * (k + 1)] = part.astype(BF16)
            scr[2 * k] = part[:, :128]
            scr[2 * k + 1] = part[:, 128:]
        for j in range(6):
            for res in range(4):
                b4_ref[0, res, :, 128 * j:128 * (j + 1)] = scr[j, pl.ds(res, tm // 4, stride=4), :].astype(BF16)
            for res in range(16):
                b16_ref[0, res, :, 128 * j:128 * (j + 1)] = scr[j, pl.ds(res, tm // 16, stride=16), :].astype(BF16)

    tok = lambda w: pl.BlockSpec((tm, w), lambda i: (i, 0))
    tab = pl.BlockSpec((tm, 128), lambda i: (i % spt, 0))
    hbm = pl.BlockSpec(memory_space=pl.ANY)
    return pl.pallas_call(
        body, name="in_proj", grid=(n_steps,),
        in_specs=[tok(D_MODEL), _full((D_IN, D_MODEL)), _full((1, D_IN)), tab, tab, tab, hbm, hbm],
        out_specs=(tok(D_MODEL), tok(W_A), tok(W_KV_A), tok(W_KV_A), tok(768),
                   pl.BlockSpec((1, 4, tm // 4, 768), lambda i: (i // spt, 0, i % spt, 0)),
                   pl.BlockSpec((1, 16, tm // 16, 768), lambda i: (i // spt, 0, i % spt, 0)),
                   tok(W_C), tok(D_MIX), hbm, hbm),
        out_shape=(_sds((T, D_MODEL), BF16), _sds((T, W_A), BF16), _sds((T, W_KV_A), BF16), _sds((T, W_KV_A), BF16),
                   _sds((T, 768), BF16), _sds((B_LOC, 4, SEQ // 4, 768), BF16), _sds((B_LOC, 16, SEQ // 16, 768), BF16),
                   _sds((T, W_C), BF16), _sds((T, D_MIX), BF16),
                   _sds((D_MIX, D_MODEL), BF16), _sds((D_MODEL, 2 * W_C), BF16)),
        input_output_aliases={6: 9, 7: 10},
        scratch_shapes=[pltpu.VMEM((6, tm, 128), F32)] + [pltpu.SemaphoreType.DMA((2, 3))] * 4,
        compiler_params=_cp(("arbitrary",), vmem_mb=48),
    )(*_pin(x, winT, b_in, cos, sa, sb, wout_own, wmem_own))


def _mem_kv(mem, wmem):
    def body(m_ref, w_ref, mb_ref, kv_ref):
        mb = m_ref[...].astype(BF16)
        mb_ref[...] = mb
        kv_ref[...] = _dot(mb, w_ref[...], NN).astype(BF16)

    n = B_LOC * MEM_LEN
    return pl.pallas_call(
        body, name="mem_kv",
        out_shape=(_sds((n, D_MODEL), BF16), _sds((n, 2 * W_C), BF16)),
    )(*_pin(mem, wmem))


class _Part:
    def __init__(self, body, args, in_specs, out_specs, out_shape, scratch=()):
        self.body, self.args, self.in_specs, self.out_specs, self.out_shape = body, args, in_specs, out_specs, out_shape
        self.scratch = list(scratch)


def _run_parts(name, parts, semantics, vmem_mb):
    n_in = [len(p.args) for p in parts]
    n_out = [len(p.out_shape) for p in parts]
    n_scr = [len(p.scratch) for p in parts]

    def body(*refs):
        ins, outs, scr = refs[:sum(n_in)], refs[sum(n_in):sum(n_in) + sum(n_out)], refs[sum(n_in) + sum(n_out):]
        i0 = o0 = s0 = 0
        for p, ni, no, ns in zip(parts, n_in, n_out, n_scr):
            p.body(*ins[i0:i0 + ni], *outs[o0:o0 + no], *scr[s0:s0 + ns])
            i0, o0, s0 = i0 + ni, o0 + no, s0 + ns

    res = pl.pallas_call(
        body, name=name, grid=(T // QR,),
        in_specs=[sp for p in parts for sp in p.in_specs], out_specs=tuple(sp for p in parts for sp in p.out_specs),
        out_shape=tuple(sh for p in parts for sh in p.out_shape),
        scratch_shapes=[sc for p in parts for sc in p.scratch],
        compiler_params=_cp((semantics,), vmem_mb=vmem_mb),
    )(*_pin(*[a for p in parts for a in p.args]))
    out, o0 = [], 0
    for no in n_out:
        out.append(tuple(res[o0:o0 + no]))
        o0 += no
    return out


QB = 8
QR = QB * BLK


def _lane_lo():
    return lax.broadcasted_iota(jnp.int32, (1, 128), 1) < 64


def _dup_head(k2, hk, lo):
    kf = k2.astype(F32)
    r = pltpu.roll(kf, 64, 1)
    return (jnp.where(lo, kf, r) if hk == 0 else jnp.where(lo, r, kf)).astype(BF16)


def _stack_heads(pairs, lo):
    parts = []
    for x2 in pairs:
        z = jnp.zeros_like(x2)
        parts += [jnp.where(lo, x2, z), jnp.where(lo, z, x2)]
    return jnp.concatenate(parts, axis=0)


def _prev_mode(kind, nb, j):
    if kind == "mem" or nb == 1:
        return "no"
    if nb <= QB:
        return "yes" if j % nb else "no"
    return "yes" if j else "dyn"


class _Attn:
    def __init__(self, kind, nb, max_dist, gqa, qw, kvw, qcb, kcb, vcb):
        self.kind, self.nb, self.gqa, self.qw, self.kvw = kind, nb, gqa, qw, kvw
        npairs = qw // 128
        self.groups = ([(hk, [2 * hk, 2 * hk + 1]) for hk in range(npairs // 2)] if gqa
                       else [(p, [p]) for p in range(npairs)])
        self.nh = 2 * len(self.groups[0][1])
        self.cols = 128 * self.nh
        self.reach = BLK - max_dist
        self.ext_prev = kind == "band" and nb > QB
        self.q_spec = pl.BlockSpec((QR, qw), lambda g: (g, qcb))
        self.row_spec = pl.BlockSpec((QR, qw), lambda g: (g, 0))
        self.stat_spec = pl.BlockSpec((QR, 128), lambda g: (g, 0))
        if kind == "mem":
            per = SEQ // QR
            self.kv_specs = [pl.BlockSpec((MEM_LEN, kvw), lambda g: (g // per, kcb)),
                             pl.BlockSpec((MEM_LEN, kvw), lambda g: (g // per, vcb))]
        else:
            self.kv_specs = [pl.BlockSpec((QR, kvw), lambda g: (g, kcb)), pl.BlockSpec((QR, kvw), lambda g: (g, vcb))]
            if self.ext_prev:
                self.kv_specs += [pl.BlockSpec((BLK, kvw), lambda g: (jnp.maximum(g * QB - 1, 0), kcb)),
                                  pl.BlockSpec((BLK, kvw), lambda g: (jnp.maximum(g * QB - 1, 0), vcb))]

    def masks(self):
        if self.kind == "mem":
            return None
        kj = lax.broadcasted_iota(jnp.int32, (2 * BLK, self.cols), 0)
        qi = lax.broadcasted_iota(jnp.int32, (2 * BLK, self.cols), 1) & (BLK - 1)
        kj1 = lax.broadcasted_iota(jnp.int32, (BLK, self.cols), 0)
        qi1 = lax.broadcasted_iota(jnp.int32, (BLK, self.cols), 1) & (BLK - 1)
        return kj, qi, kj1 <= qi1

    def keys(self, j, gi, kc_ref, vc_ref, kp_ref, vp_ref, lo, kq, g):
        def kv(k_ref, v_ref, r):
            if self.gqa:
                return _dup_head(k_ref[r, :], gi, lo), _dup_head(v_ref[r, :], gi, lo)
            sl = slice(128 * gi, 128 * (gi + 1))
            return k_ref[r, sl], v_ref[r, sl]

        if self.kind == "mem":
            key0 = pl.multiple_of((g // (SEQ // QR)) * MEM_LEN, MEM_LEN)
            return (*kv(kc_ref, vc_ref, slice(None)), None, [(0, MEM_LEN, key0)])
        kj, qi, cur = kq
        row0 = g * QR + BLK * j
        mode = _prev_mode(self.kind, self.nb, j)
        if mode == "no":
            return (*kv(kc_ref, vc_ref, slice(BLK * j, BLK * (j + 1))), cur, [(0, BLK, pl.multiple_of(row0, BLK))])
        if mode == "yes":
            mask = jnp.logical_and(kj >= qi + self.reach, kj <= qi + BLK)
            return (*kv(kc_ref, vc_ref, slice(BLK * (j - 1), BLK * (j + 1))), mask,
                    [(0, 2 * BLK, pl.multiple_of(row0 - BLK, BLK))])
        has_prev = ((g * QB) % self.nb) > 0
        hp = has_prev.astype(jnp.int32)
        mask = jnp.logical_and(kj >= qi * hp + (self.reach * hp + BLK * (1 - hp)), kj <= qi + BLK)
        kp, vp = kv(kp_ref, vp_ref, slice(None))
        kc, vc = kv(kc_ref, vc_ref, slice(0, BLK))
        return (jnp.concatenate([kp, kc], axis=0), jnp.concatenate([vp, vc], axis=0), mask,
                [(0, BLK, pl.multiple_of(jnp.maximum(row0 - BLK, 0), BLK)), (BLK, BLK, pl.multiple_of(row0, BLK))])


def _attn_fwd(q, qcb, qw, k, kcb, v, vcb, kvw, *, kind, nb=1, max_dist=BLK, gqa=False, sinks=None):
    a = _Attn(kind, nb, max_dist, gqa, qw, kvw, qcb, kcb, vcb)

    def body(*refs):
        it = iter(refs)
        q_ref, kc_ref, vc_ref = next(it), next(it), next(it)
        kp_ref, vp_ref = (next(it), next(it)) if a.ext_prev else (None, None)
        sink_ref = next(it) if sinks is not None else None
        o_ref, lse_ref = next(it), next(it)
        g = pl.program_id(0)
        lo = _lane_lo()
        top = lax.broadcasted_iota(jnp.int32, (128, 1), 0) < 64
        rid = lax.broadcasted_iota(jnp.int32, (8, 128), 0)
        kq = a.masks()
        for j in range(QB):
            rows = slice(BLK * j, BLK * (j + 1))
            stat = jnp.zeros((8, 128), F32)
            for gi, pairs in a.groups:
                qs = _stack_heads([q_ref[rows, 128 * p:128 * (p + 1)] for p in pairs], lo)
                kk, vv, mask, _ = a.keys(j, gi, kc_ref, vc_ref, kp_ref, vp_ref, lo, kq, g)
                pieces = [slice(r0, r0 + BLK) for r0 in range(0, kk.shape[0], BLK)]
                ss = []
                for r in pieces:
                    s = _dot(kk[r], qs, NT)
                    ss.append(s if mask is None else jnp.where(mask[r], s, NEG))
                m = jnp.max(ss[0], axis=0, keepdims=True)
                for s in ss[1:]:
                    m = jnp.maximum(m, jnp.max(s, axis=0, keepdims=True))
                if sink_ref is not None:
                    sk = jnp.concatenate([jnp.full((1, 128), sink_ref[0, a.nh * gi + i], F32) for i in range(a.nh)],
                                         axis=1)
                    m = jnp.maximum(m, sk)
                l, ot = None, None
                for r, s in zip(pieces, ss):
                    p = jnp.exp(s - m)
                    ps = jnp.sum(p, axis=0, keepdims=True)
                    c = _dot(vv[r], p.astype(BF16), TN)
                    l, ot = (ps, c) if l is None else (l + ps, ot + c)
                if sink_ref is not None:
                    l = l + jnp.exp(sk - m)
                ot = ot * pl.reciprocal(l, approx=True)
                lse = m + jnp.log(l)
                for i, p in enumerate(pairs):
                    o2t = jnp.where(top, ot[:, 256 * i:256 * i + 128], ot[:, 256 * i + 128:256 * i + 256])
                    o_ref[rows, 128 * p:128 * (p + 1)] = o2t.T.astype(BF16)
                for i in range(a.nh):
                    stat = jnp.where(rid == a.nh * gi + i, lse[:, 128 * i:128 * (i + 1)], stat)
            lse_ref[rows, :] = jnp.concatenate([stat, jnp.zeros((120, 128), F32)], axis=0).T

    args = [q, k, v] + ([k, v] if a.ext_prev else [])
    in_specs = [a.q_spec] + a.kv_specs
    if sinks is not None:
        args.append(sinks)
        in_specs.append(pl.BlockSpec(memory_space=pltpu.SMEM))
    return _Part(body, args, in_specs, [a.row_spec, a.stat_spec], [_sds((T, qw), BF16), _sds((T, 128), F32)])


def _attn_bwd(q, qcb, qw, k, kcb, v, vcb, kvw, do, lse, dl, *, kind, nb=1, max_dist=BLK, gqa=False, sinkv=None):
    a = _Attn(kind, nb, max_dist, gqa, qw, kvw, qcb, kcb, vcb)

    def body(*refs):
        it = iter(refs)
        q_ref, kc_ref, vc_ref = next(it), next(it), next(it)
        kp_ref, vp_ref = (next(it), next(it)) if a.ext_prev else (None, None)
        do_ref, lse_ref, dl_ref = next(it), next(it), next(it)
        sinkv_ref = next(it) if sinkv is not None else None
        dq_ref = next(it)
        if kind == "mem":
            dkv_ref = next(it)
        else:
            dk_ref, dv_ref = next(it), next(it)
        dsink_ref = next(it) if sinkv is not None else None
        g = pl.program_id(0)
        lo = _lane_lo()
        top = lax.broadcasted_iota(jnp.int32, (128, 1), 0) < 64

        @pl.when(g == 0)
        def _():
            if kind == "mem":
                dkv_ref[...] = jnp.zeros_like(dkv_ref)
            else:
                dk_ref[...] = jnp.zeros_like(dk_ref)
                dv_ref[...] = jnp.zeros_like(dv_ref)
            if dsink_ref is not None:
                dsink_ref[...] = jnp.zeros_like(dsink_ref)

        kq = a.masks()
        for j in range(QB):
            rows = slice(BLK * j, BLK * (j + 1))
            lse_t = lse_ref[rows, :].T
            dl_t = dl_ref[rows, :].T
            for gi, pairs in a.groups:
                heads = [a.nh * gi + i for i in range(a.nh)]
                qs = _stack_heads([q_ref[rows, 128 * p:128 * (p + 1)] for p in pairs], lo)
                dos = _stack_heads([do_ref[rows, 128 * p:128 * (p + 1)] for p in pairs], lo)
                lse_row = jnp.concatenate([lse_t[h:h + 1, :] for h in heads], axis=1)
                dl_row = jnp.concatenate([dl_t[h:h + 1, :] for h in heads], axis=1)
                kk, vv, mask, dests = a.keys(j, gi, kc_ref, vc_ref, kp_ref, vp_ref, lo, kq, g)
                s = _dot(kk, qs, NT)
                if mask is not None:
                    s = jnp.where(mask, s, NEG)
                p = jnp.exp(s - lse_row)
                ds = (p * (_dot(vv, dos, NT) - dl_row)).astype(BF16)
                dqt = _dot(kk, ds, TN)
                ck = _dot(ds, qs, NN)
                cv = _dot(p.astype(BF16), dos, NN)
                if gqa:
                    sel = lo if gi == 0 else jnp.logical_not(lo)
                    ck = jnp.where(sel, ck + pltpu.roll(ck, 64, 1), 0.0)
                    cv = jnp.where(sel, cv + pltpu.roll(cv, 64, 1), 0.0)
                    kcols = slice(0, 128)
                else:
                    kcols = slice(128 * gi, 128 * (gi + 1))
                for r0, nr, key0 in dests:
                    krows = pl.ds(key0, nr)
                    if kind == "mem":
                        dkv_ref[krows, kcols] += ck[r0:r0 + nr]
                        dkv_ref[krows, slice(kvw + kcols.start, kvw + kcols.stop)] += cv[r0:r0 + nr]
                    else:
                        dk_ref[krows, kcols] += ck[r0:r0 + nr]
                        dv_ref[krows, kcols] += cv[r0:r0 + nr]
                for i, p in enumerate(pairs):
                    dq2t = jnp.where(top, dqt[:, 256 * i:256 * i + 128], dqt[:, 256 * i + 128:256 * i + 256])
                    dq_ref[rows, 128 * p:128 * (p + 1)] = dq2t.T.astype(BF16)
        if dsink_ref is not None:
            ps = jnp.exp(sinkv_ref[...] - lse_ref[...]) * dl_ref[...]
            dsink_ref[...] += jnp.sum(ps, axis=0, keepdims=True)

    args = [q, k, v] + ([k, v] if a.ext_prev else []) + [do, lse, dl]
    in_specs = [a.q_spec] + a.kv_specs + [a.row_spec, a.stat_spec, a.stat_spec]
    if sinkv is not None:
        args.append(sinkv)
        in_specs.append(_full((1, 128)))
    out_shape = [_sds((T, qw), BF16)]
    out_specs = [a.row_spec]
    once = lambda shape: pl.BlockSpec(shape, lambda g: (0, 0), pipeline_mode=pl.Buffered(1))
    if kind == "mem":
        out_shape.append(_sds((B_LOC * MEM_LEN, 2 * kvw), F32))
        out_specs.append(once((B_LOC * MEM_LEN, 2 * kvw)))
    else:
        out_shape += [_sds((T, kvw), F32)] * 2
        out_specs += [once((T, kvw))] * 2
    if sinkv is not None:
        out_shape.append(_sds((1, 128), F32))
        out_specs.append(_full((1, 128)))
    return _Part(body, args, in_specs, out_specs, out_shape)


def _dot2(v, w_ref):
    hi = v.astype(BF16)
    lo = (v - hi.astype(F32)).astype(BF16)
    return _dot(hi, w_ref[...], NN) + _dot(lo, w_ref[...], NN)


def _middle(oa, o1, l1, o4, l4, o16, l16, oc, z, x, tgt, g_br, ln_g, ln_b, wout, spread4, gather4, gather8):
    tm = 512
    spt = SEQ // tm

    def body(oa_ref, o1_ref, l1_ref, o4_ref, l4_ref, o16_ref, l16_ref, oc_ref, z_ref, x_ref, t_ref,
             g_ref, lg_ref, lb_ref, w_ref, sp4_ref, ga4_ref, ga8_ref,
             du_ref, dz_ref, doa_ref, dla_ref,
             dobn_ref, lsen_ref, dlbn_ref, dob4_ref, lse4_ref, dlb4_ref, dob16_ref, lse16_ref, dlb16_ref,
             doc_ref, dlc_ref, acc_ref, gout_ref, scr):
        i = pl.program_id(0)

        @pl.when(i == 0)
        def _():
            acc_ref[...] = jnp.zeros_like(acc_ref)
            gout_ref[...] = jnp.zeros_like(gout_ref)

        for res in range(4):
            rows = pl.ds(res, tm // 4, stride=4)
            for j in range(2):
                scr[j, rows, :] = o4_ref[0, res, :, 128 * j:128 * (j + 1)].astype(F32)
            scr[2, rows, :] = l4_ref[0, res]
        for res in range(16):
            rows = pl.ds(res, tm // 16, stride=16)
            for j in range(2):
                scr[3 + j, rows, :] = o16_ref[0, res, :, 128 * j:128 * (j + 1)].astype(F32)
            scr[5, rows, :] = l16_ref[0, res]
        cat = lambda a: jnp.concatenate([scr[a], scr[a + 1]], axis=1)
        o1v, o4v, o16v = o1_ref[...].astype(F32), cat(0), cat(3)
        l1v, l4v, l16v = l1_ref[...], scr[2], scr[5]
        mx = jnp.maximum(jnp.maximum(l1v, l4v), l16v)
        e1, e4, e16 = jnp.exp(l1v - mx), jnp.exp(l4v - mx), jnp.exp(l16v - mx)
        ssum = e1 + e4 + e16
        lse_b = mx + jnp.log(ssum)
        inv = 1.0 / ssum
        ob = (_dot2(e1 * inv, sp4_ref) * o1v + _dot2(e4 * inv, sp4_ref) * o4v + _dot2(e16 * inv, sp4_ref) * o16v)
        oav, ocv = oa_ref[...].astype(F32), oc_ref[...].astype(F32)

        def rms(o):
            r = lax.rsqrt(jnp.sum(o * o, axis=1, keepdims=True) * (1.0 / o.shape[1]) + RMS_EPS)
            return o * r, r

        na, ra = rms(oav)
        nb_, rb = rms(ob)
        nc, rc = rms(ocv)
        n = jnp.concatenate([na, nb_, nc], axis=1)
        zf = z_ref[...].astype(F32)
        sig = 1.0 / (1.0 + jnp.exp(-zf))
        sz = zf * sig
        gb = g_ref[...]
        yb = (n * gb * sz).astype(BF16)
        u = ALPHA * x_ref[...] + _dot(yb, w_ref[...], NN)
        inv_d = 1.0 / D_MODEL
        mu = jnp.sum(u, axis=1, keepdims=True) * inv_d
        uc = u - mu
        rstd = lax.rsqrt(jnp.sum(uc * uc, axis=1, keepdims=True) * inv_d + LN_EPS)
        xh = uc * rstd
        lg = lg_ref[...]
        diff = xh * lg + lb_ref[...] - t_ref[...]
        acc_ref[0:1, :] += jnp.sum(diff * diff, axis=0, keepdims=True) * (0.5 * inv_d)
        dout = diff * inv_d
        acc_ref[2:3, :] += jnp.sum(dout * xh, axis=0, keepdims=True)
        acc_ref[3:4, :] += jnp.sum(dout, axis=0, keepdims=True)
        dxh = dout * lg
        du = rstd * (dxh - jnp.sum(dxh, axis=1, keepdims=True) * inv_d
                     - xh * (jnp.sum(dxh * xh, axis=1, keepdims=True) * inv_d))
        dub = du.astype(BF16)
        du_ref[...] = dub
        gout_ref[...] += _dot(yb, dub, TN)
        dy = _dot(dub, w_ref[...], NT)
        t1 = dy * sz
        acc_ref[1:2, :] += jnp.sum(t1 * n, axis=0, keepdims=True)
        dn = t1 * gb
        dz_ref[...] = (dy * n * gb * (sig * (1.0 + zf * (1.0 - sig)))).astype(BF16)

        def rms_bwd(dn_, n_, r):
            return r * (dn_ - n_ * (jnp.sum(dn_ * n_, axis=1, keepdims=True) * (1.0 / n_.shape[1])))

        doa = rms_bwd(dn[:, :W_A], na, ra)
        dob = rms_bwd(dn[:, W_A:W_A + W_B], nb_, rb)
        doc = rms_bwd(dn[:, W_A + W_B:], nc, rc)
        doa_ref[...] = doa.astype(BF16)
        dla_ref[...] = _dot2(doa * oav, ga8_ref)
        doc_ref[...] = doc.astype(BF16)
        dlc_ref[...] = _dot2(doc * ocv, ga4_ref)
        dlb = _dot2(dob * ob, ga4_ref)
        dobn_ref[...] = dob.astype(BF16)
        lsen_ref[...] = lse_b
        dlbn_ref[...] = dlb
        scr[0] = dob[:, :128]
        scr[1] = dob[:, 128:]
        scr[2] = lse_b
        scr[3] = dlb
        for res in range(4):
            rows = pl.ds(res, tm // 4, stride=4)
            for j in range(2):
                dob4_ref[0, res, :, 128 * j:128 * (j + 1)] = scr[j, rows, :].astype(BF16)
            lse4_ref[0, res] = scr[2, rows, :]
            dlb4_ref[0, res] = scr[3, rows, :]
        for res in range(16):
            rows = pl.ds(res, tm // 16, stride=16)
            for j in range(2):
                dob16_ref[0, res, :, 128 * j:128 * (j + 1)] = scr[j, rows, :].astype(BF16)
            lse16_ref[0, res] = scr[2, rows, :]
            dlb16_ref[0, res] = scr[3, rows, :]

    tok = lambda w: pl.BlockSpec((tm, w), lambda i: (i, 0))
    p4 = lambda w: pl.BlockSpec((1, 4, tm // 4, w), lambda i: (i // spt, 0, i % spt, 0))
    p16 = lambda w: pl.BlockSpec((1, 16, tm // 16, w), lambda i: (i // spt, 0, i % spt, 0))
    s4 = lambda w, dt: _sds((B_LOC, 4, SEQ // 4, w), dt)
    s16 = lambda w, dt: _sds((B_LOC, 16, SEQ // 16, w), dt)
    row = _full((1, D_MODEL))
    return pl.pallas_call(
        body, name="middle", grid=(T // tm,),
        in_specs=[tok(W_A), tok(W_B), tok(128), p4(W_B), p4(128), p16(W_B), p16(128), tok(W_C), tok(D_MIX),
                  tok(D_MODEL), tok(D_MODEL), row, row, row, _full((D_MIX, D_MODEL)),
                  _full((128, W_B)), _full((W_B, 128)), _full((W_A, 128))],
        out_specs=(tok(D_MODEL), tok(D_MIX), tok(W_A), tok(128),
                   tok(W_B), tok(128), tok(128), p4(W_B), p4(128), p4(128), p16(W_B), p16(128), p16(128),
                   tok(W_C), tok(128), _full((8, D_MODEL)), _full((D_MIX, D_MODEL))),
        out_shape=(_sds((T, D_MODEL), BF16), _sds((T, D_MIX), BF16),
                   _sds((T, W_A), BF16), _sds((T, 128), F32),
                   _sds((T, W_B), BF16), _sds((T, 128), F32), _sds((T, 128), F32),
                   s4(W_B, BF16), s4(128, F32), s4(128, F32), s16(W_B, BF16), s16(128, F32), s16(128, F32),
                   _sds((T, W_C), BF16), _sds((T, 128), F32), _sds((8, D_MODEL), F32),
                   _sds((D_MIX, D_MODEL), F32)),
        scratch_shapes=[pltpu.VMEM((6, tm, 128), F32)],
        compiler_params=_cp(("arbitrary",), vmem_mb=56),
    )(*_pin(oa, o1, l1, o4, l4, o16, l16, oc, z, x, tgt, g_br, ln_g, ln_b, wout, spread4, gather4, gather8))


class _ReduceScatter:
    def __init__(self, shapes):
        self.shapes = shapes

    def scratch_shapes(self):
        out = []
        for n, w in self.shapes:
            h = n // 2
            out += [pltpu.VMEM((4, h, w), F32), pltpu.VMEM((4, h, w), F32), pltpu.VMEM((3, h, w), BF16),
                    pltpu.VMEM((3, h, w), BF16), pltpu.VMEM((h, w), F32)]
        na = len(self.shapes)
        dma = pltpu.SemaphoreType.DMA
        return out + [dma((na, 4)), dma((na, 4)), dma((na, 4)), dma((na, 3)), dma((na, 3)), dma((na,)), dma((na,)),
                      dma((na,))]

    def bind(self, g_refs, r_refs, scratch):
        na = len(self.shapes)
        bufs = [scratch[5 * a:5 * a + 5] for a in range(na)]
        mine, sib, stage, land, tot = (tuple(b[i] for b in bufs) for i in range(5))
        loc_sem, s1_send, s1_recv, s2_send, s2_recv, s3_send, s3_recv, st_sem = scratch[5 * na:5 * na + 8]
        x, y, c = lax.axis_index("x"), lax.axis_index("y"), lax.axis_index("c")
        me, sibling = (x, y, c), (x, y, 1 - c)
        my_chip = 2 * x + y
        chips = [(1 - x, y), (x, 1 - y), (1 - x, 1 - y)]
        order = [2 * chip[0] + chip[1] for chip in chips] + [my_chip]

        def rows(a, k, half):
            n = self.shapes[a][0]
            return pl.ds(pl.multiple_of(k * n + half * (n // 2), 8), n // 2)

        def load(a, k):
            return pltpu.make_async_copy(g_refs[a].at[rows(a, k, c), :], mine[a].at[k], loc_sem.at[a, k])

        def s1(a, k, half):
            return pltpu.make_async_remote_copy(
                src_ref=g_refs[a].at[rows(a, k, half), :], dst_ref=sib[a].at[k],
                send_sem=s1_send.at[a, k], recv_sem=s1_recv.at[a, k], device_id=sibling, device_id_type=MESH)

        def s2(a, j, to):
            return pltpu.make_async_remote_copy(
                src_ref=stage[a].at[j], dst_ref=land[a].at[j], send_sem=s2_send.at[a, j], recv_sem=s2_recv.at[a, j],
                device_id=to, device_id_type=MESH)

        def s3(a, half, to):
            return pltpu.make_async_remote_copy(
                src_ref=tot[a], dst_ref=r_refs[a].at[rows(a, 0, half), :], send_sem=s3_send.at[a],
                recv_sem=s3_recv.at[a], device_id=to, device_id_type=MESH)

        def store(a):
            return pltpu.make_async_copy(tot[a], r_refs[a].at[rows(a, 0, c), :], st_sem.at[a])

        def start():
            for k in order:
                for a in range(na):
                    load(a, k).start()
                    s1(a, k, 1 - c).start()

        def exchange():
            for j, chip in enumerate(chips):
                k = order[j]
                for a in range(na):
                    load(a, k).wait()
                    s1(a, k, c).wait_recv()
                    stage[a][j] = (mine[a][k] + sib[a][k]).astype(BF16)
                    s2(a, j, (*chip, c)).start()
            for a in range(na):
                load(a, my_chip).wait()
                s1(a, my_chip, c).wait_recv()
                tot[a][...] = mine[a][my_chip] + sib[a][my_chip]

        def finish():
            for a in range(na):
                t = tot[a][...]
                for j in range(3):
                    s2(a, j, me).wait_recv()
                    t = t + land[a][j].astype(F32)
                tot[a][...] = t
                s3(a, c, sibling).start()
                store(a).start()

        def drain():
            for a in range(na):
                s3(a, 1 - c, me).wait_recv()
                store(a).wait()
            for a in range(na):
                for k in order:
                    s1(a, k, 1 - c).wait_send()
                for j, chip in enumerate(chips):
                    s2(a, j, (*chip, c)).wait_send()
                s3(a, c, sibling).wait_send()

        return start, exchange, finish, drain

    def part(self, grads, steps):
        def body(*refs):
            na = len(self.shapes)
            i = pl.program_id(0)
            for step, phase in zip(steps, self.bind(refs[:na], refs[na:2 * na], refs[2 * na:])):
                pl.when(i == step)(phase)

        hbm = pl.BlockSpec(memory_space=pl.ANY)
        return _Part(body, list(grads), [hbm] * len(grads), [hbm] * len(grads),
                     [_sds((n, w), F32) for n, w in self.shapes], self.scratch_shapes())


def _dh_dx(dqa, dka, dva, dqn, dkn, dvn, dq4, dk4, dv4, dq16, dk16, dv16, dqc, dz, du, xb, cos, sa, sb, winT):
    tm = 512
    spt = SEQ // tm

    def body(dqa_ref, dka_ref, dva_ref, dqn_ref, dkn_ref, dvn_ref, dq4_ref, dk4_ref, dv4_ref,
             dq16_ref, dk16_ref, dv16_ref, dqc_ref, dz_ref, du_ref, xb_ref, cos_ref, sa_ref, sb_ref, w_ref,
             gx_ref, db_ref, gin_ref, dh_ref, scr):
        i = pl.program_id(0)

        @pl.when(i == 0)
        def _():
            db_ref[...] = jnp.zeros_like(db_ref)
            gin_ref[...] = jnp.zeros_like(gin_ref)

        cos_t, sa_t, sb_t = cos_ref[...], sa_ref[...], sb_ref[...]

        def rope_t(t):
            return _rope(t, cos_t, sa_t, sb_t, -1)

        def put(r0, val):
            n = val.shape[1]
            dh_ref[:, r0:r0 + n] = val.astype(BF16)
            db_ref[:, r0:r0 + n] += jnp.sum(val, axis=0, keepdims=True)

        put(O_QA, rope_t(dqa_ref[...].astype(F32)) * QK_SCALE)
        put(O_KA, rope_t(dka_ref[...]))
        put(O_VA, dva_ref[...])
        put(O_QC, dqc_ref[...].astype(F32) * QK_SCALE)
        put(O_Z, dz_ref[...].astype(F32))
        for k, (n_ref, r4, r16) in enumerate(((dqn_ref, dq4_ref, dq16_ref), (dkn_ref, dk4_ref, dk16_ref),
                                               (dvn_ref, dv4_ref, dv16_ref))):
            for j in range(2):
                sl = slice(128 * j, 128 * (j + 1))
                scr[2 * k + j] = n_ref[:, sl].astype(F32)
                for res in range(4):
                    scr[2 * k + j, pl.ds(res, tm // 4, stride=4), :] += r4[0, res, :, sl].astype(F32)
                for res in range(16):
                    scr[2 * k + j, pl.ds(res, tm // 16, stride=16), :] += r16[0, res, :, sl].astype(F32)
        cat = lambda a: jnp.concatenate([scr[a], scr[a + 1]], axis=1)
        put(O_QB, rope_t(cat(0)) * QK_SCALE)
        put(O_KB, rope_t(cat(2)))
        put(O_VB, cat(4))
        gx_ref[...] = _dot(dh_ref[...], w_ref[...], NN) + ALPHA * du_ref[...].astype(F32)
        gin_ref[...] += _dot(dh_ref[...], xb_ref[...], TN)

    tok = lambda w: pl.BlockSpec((tm, w), lambda i: (i, 0))
    tab = pl.BlockSpec((tm, 128), lambda i: (i % spt, 0))
    p4 = pl.BlockSpec((1, 4, tm // 4, W_B), lambda i: (i // spt, 0, i % spt, 0))
    p16 = pl.BlockSpec((1, 16, tm // 16, W_B), lambda i: (i // spt, 0, i % spt, 0))
    once = lambda shape: pl.BlockSpec(shape, lambda i: (0, 0), pipeline_mode=pl.Buffered(1))
    return pl.pallas_call(
        body, name="dh_dx", grid=(T // tm,),
        in_specs=[tok(W_A), tok(W_KV_A), tok(W_KV_A), tok(W_B), tok(W_B), tok(W_B), p4, p4, p4, p16, p16, p16,
                  tok(W_C), tok(D_MIX), tok(D_MODEL), tok(D_MODEL), tab, tab, tab, once((D_IN, D_MODEL))],
        out_specs=(tok(D_MODEL), _full((1, D_IN)), once((D_IN, D_MODEL))),
        out_shape=(_sds((T, D_MODEL), F32), _sds((1, D_IN), F32), _sds((D_IN, D_MODEL), F32)),
        scratch_shapes=[pltpu.VMEM((tm, D_IN), BF16), pltpu.VMEM((6, tm, 128), F32)],
        compiler_params=_cp(("arbitrary",), vmem_mb=56),
    )(*_pin(dqa, dka, dva, dqn, dkn, dvn, dq4, dk4, dv4, dq16, dk16, dv16, dqc, dz, du, xb, cos, sa, sb, winT))


def _tn_matmul(name, a, b, bm, bt):
    n, m_all = a.shape
    n_cols = b.shape[1]

    def body(a_ref, b_ref, o_ref):
        @pl.when(pl.program_id(1) == 0)
        def _():
            o_ref[...] = jnp.zeros_like(o_ref)

        o_ref[...] += _dot(a_ref[...].astype(BF16), b_ref[...].astype(BF16), TN)

    return pl.pallas_call(
        body, name=name, grid=(m_all // bm, n // bt),
        in_specs=[pl.BlockSpec((bt, bm), lambda m, t: (t, m)), pl.BlockSpec((bt, n_cols), lambda m, t: (t, 0))],
        out_specs=pl.BlockSpec((bm, n_cols), lambda m, t: (m, 0)),
        out_shape=_sds((m_all, n_cols), F32),
        compiler_params=_cp(("parallel", "arbitrary"), vmem_mb=48),
    )(*_pin(a, b))


def _reduce_grads(g_in, acc, dbin, dsink):
    rs = _ReduceScatter([(SH_IN, D_MODEL)])

    def body(g_ref, acc_ref, dbin_ref, dsink_ref, r_ref, sv_ref, sv_mine, sv_all, sv_send, sv_recv, *rs_scratch):
        x, y, c = lax.axis_index("x"), lax.axis_index("y"), lax.axis_index("c")
        chips = [(1 - x, y), (x, 1 - y), (1 - x, 1 - y)]
        start, exchange, finish, drain = rs.bind((g_ref,), (r_ref,), rs_scratch)
        start()

        sv_mine[...] = jnp.zeros_like(sv_mine)
        sv_mine[0:4, 0:D_MODEL] = acc_ref[0:4, :]
        sv_mine[4:5, 0:D_IN] = dbin_ref[...]
        sv_mine[5:6, 0:128] = dsink_ref[...]
        my_dev = 4 * x + 2 * y + c
        others = [(x, y, 1 - c)] + [(*chip, cc) for chip in chips for cc in (c, 1 - c)]

        def sv_copy(j, to):
            return pltpu.make_async_remote_copy(
                src_ref=sv_mine, dst_ref=sv_all.at[my_dev], send_sem=sv_send.at[j], recv_sem=sv_recv.at[j],
                device_id=to, device_id_type=MESH)

        sv_sends = [sv_copy(j, to) for j, to in enumerate(others)]
        for cp in sv_sends:
            cp.start()
        exchange()
        finish()
        sv_all[my_dev] = sv_mine[...]
        for j in range(7):
            sv_copy(j, (x, y, c)).wait_recv()
        tot = sv_all[0]
        for d in range(1, 8):
            tot = tot + sv_all[d]
        sv_ref[...] = tot
        drain()
        for cp in sv_sends:
            cp.wait_send()

    vm = pl.BlockSpec(memory_space=pltpu.VMEM)
    hbm = pl.BlockSpec(memory_space=pl.ANY)
    return pl.pallas_call(
        body, name="reduce_grads",
        out_shape=(_sds((SH_IN, D_MODEL), F32), _vm_sds((8, SV_W), F32)),
        in_specs=[hbm, vm, vm, vm], out_specs=(hbm, vm),
        scratch_shapes=[pltpu.VMEM((8, SV_W), F32), pltpu.VMEM((8, 8, SV_W), F32),
                        pltpu.SemaphoreType.DMA((7,)), pltpu.SemaphoreType.DMA((7,))] + rs.scratch_shapes(),
        compiler_params=_cp(vmem_mb=40),
    )(pltpu.with_memory_space_constraint(g_in, pltpu.HBM), acc, dbin, dsink)


def _adamw(name, w, g, m, v, rows=None, copy_g=False):
    shape = w.shape
    rows = shape[0] if rows is None else rows
    n_out = 4 if copy_g else 3

    def body(w_ref, g_ref, m_ref, v_ref, d_ref, nm_ref, nv_ref, *go_ref):
        gv = g_ref[...]
        if copy_g:
            go_ref[0][...] = gv
        nm = ADAM_B1 * m_ref[...] + (1.0 - ADAM_B1) * gv
        nv = ADAM_B2 * v_ref[...] + (1.0 - ADAM_B2) * (gv * gv)
        m_hat = nm / (1.0 - ADAM_B1 ** ADAM_STEP)
        v_hat = nv / (1.0 - ADAM_B2 ** ADAM_STEP)
        d_ref[...] = -ADAM_LR * (m_hat / (jnp.sqrt(v_hat) + ADAM_EPS) + ADAM_WD * w_ref[...])
        nm_ref[...] = nm
        nv_ref[...] = nv

    spec = pl.BlockSpec((rows, shape[1]), lambda i: (i, 0))
    return pl.pallas_call(
        body, name=name, grid=(shape[0] // rows,), in_specs=[spec] * 4, out_specs=(spec,) * n_out,
        out_shape=(_sds(shape, F32),) * n_out, compiler_params=_cp(("parallel",)),
    )(*_pin(w, g, m, v))


def _rope_tables():
    pos = jnp.arange(SEQ, dtype=F32)
    inv = ROPE_THETA ** (-jnp.arange(0, 64, 2, dtype=F32) / 64)
    ang = pos[:, None] * inv[None, :]
    cos, sin = lax.optimization_barrier((jnp.cos(ang), jnp.sin(ang)))
    cos, sin = jnp.tile(cos, (1, 4)), jnp.tile(sin, (1, 4))
    low = (jnp.arange(128) % 64) < 32
    return cos, jnp.where(low, -sin, 0.0), jnp.where(low, 0.0, sin)


def _local_step(x2, mem2, tgt2, winT, wout, wmem, b_in, sinks, g_branch, ln_gain, ln_bias):
    cos, sa, sb = _rope_tables()
    sinkv = jnp.pad(sinks, ((0, 0), (0, 120)))
    head_of_lane = jnp.arange(512)[None, :] // 64
    gather8 = (head_of_lane.T == jnp.arange(128)[None, :]).astype(BF16)
    gather4 = gather8[:W_B]
    spread4 = gather4.T

    xb, qa, ka, va, bn, b4, b16, qc, z, wout, wmem = _in_proj(x2, winT, b_in, cos, sa, sb, wout, wmem)
    memb, mkv = _mem_kv(mem2, wmem)
    b4f, b16f = b4.reshape(T, 768), b16.reshape(T, 768)

    swa = dict(kind="band", nb=SEQ // BLK, max_dist=BLK - 1, gqa=True)
    dil = (dict(kind="band", nb=SEQ // BLK), dict(kind="band", nb=SEQ // 4 // BLK), dict(kind="band", nb=1))
    (oa, lse_a), (o1, l1), (o4, l4), (o16, l16), (oc, lse_c) = _run_parts("attn_fwd", [
        _attn_fwd(qa, 0, W_A, ka, 0, va, 0, W_KV_A, sinks=sinks, **swa),
        _attn_fwd(bn, 0, W_B, bn, 1, bn, 2, W_B, **dil[0]),
        _attn_fwd(b4f, 0, W_B, b4f, 1, b4f, 2, W_B, **dil[1]),
        _attn_fwd(b16f, 0, W_B, b16f, 1, b16f, 2, W_B, **dil[2]),
        _attn_fwd(qc, 0, W_C, mkv, 0, mkv, 1, W_C, kind="mem")], "parallel", 48)

    s4 = lambda w: (B_LOC, 4, SEQ // 4, w)
    s16 = lambda w: (B_LOC, 16, SEQ // 16, w)
    (du, dz, doa, dla, dobn, lsen, dlbn, dob4, lse4, dlb4, dob16, lse16, dlb16, doc, dlc, acc, g_out) = _middle(
        oa, o1, l1, o4.reshape(s4(W_B)), l4.reshape(s4(128)), o16.reshape(s16(W_B)), l16.reshape(s16(128)), oc, z,
        x2, tgt2, g_branch, ln_gain, ln_bias, wout, spread4, gather4, gather8)

    flat = lambda a: a.reshape(T, a.shape[-1])
    (dqa, dka, dva, dsink), (dqc, dmkv) = _run_parts("attn_bwd_a", [
        _attn_bwd(qa, 0, W_A, ka, 0, va, 0, W_KV_A, doa, lse_a, dla, sinkv=sinkv, **swa),
        _attn_bwd(qc, 0, W_C, mkv, 0, mkv, 1, W_C, doc, lse_c, dlc, kind="mem")], "arbitrary", 48)
    g_mem = _tn_matmul("dw_mem", memb, dmkv, D_MODEL, B_LOC * MEM_LEN)
    last = T // QR - 1
    (r_out, r_mem), (dqn, dkn, dvn), (dq4, dk4, dv4), (dq16, dk16, dv16) = _run_parts("attn_bwd_b", [
        _ReduceScatter([(SH_OUT, D_MODEL), (SH_MEM, 2 * W_C)]).part((g_out, g_mem), (0, 1, last, last)),
        _attn_bwd(bn, 0, W_B, bn, 1, bn, 2, W_B, dobn, lsen, dlbn, **dil[0]),
        _attn_bwd(b4f, 0, W_B, b4f, 1, b4f, 2, W_B, flat(dob4), flat(lse4), flat(dlb4), **dil[1]),
        _attn_bwd(b16f, 0, W_B, b16f, 1, b16f, 2, W_B, flat(dob16), flat(lse16), flat(dlb16), **dil[2])],
        "arbitrary", 60)

    r4 = lambda a: a.reshape(s4(W_B))
    r16 = lambda a: a.reshape(s16(W_B))
    gx, dbin, g_in = _dh_dx(dqa, dka, dva, dqn, dkn, dvn, r4(dq4), r4(dk4), r4(dv4), r16(dq16), r16(dk16),
                            r16(dv16), dqc, dz, du, xb, cos, sa, sb, winT)
    return gx, g_in, r_out, r_mem, acc, dbin, dsink


def kernel(x, mem, w_in, b_in, w_mem, attn_sinks, g_branch, w_out, ln_gain, ln_bias, loss_target, m_w_in, m_b_in, m_w_mem, m_attn_sinks, m_g_branch, m_w_out, m_ln_gain, m_ln_bias, v_w_in, v_b_in, v_w_mem, v_attn_sinks, v_g_branch, v_w_out, v_ln_gain, v_ln_bias):
    winT, wout, wmem = _gather_weights(w_in[0].T, w_out[0], w_mem[0])
    gx, g_in, r_out, r_mem, acc, dbin, dsink = _local_step(
        x.reshape(T, D_MODEL), mem.reshape(B_LOC * MEM_LEN, D_MODEL), loss_target.reshape(T, D_MODEL),
        winT, wout, wmem, b_in, attn_sinks, g_branch, ln_gain, ln_bias)
    r_in, sv = _reduce_grads(g_in, acc, dbin, dsink)

    loss = jnp.sum(sv[0, :D_MODEL])
    grads = {
        "b_in": sv[4:5, :D_IN], "w_mem": r_mem[None],
        "attn_sinks": -sv[5:6, 0:8], "g_branch": sv[1:2, :D_MODEL], "w_out": r_out[None],
        "ln_gain": sv[2:3, :D_MODEL], "ln_bias": sv[3:4, :D_MODEL],
    }
    weights = dict(w_in=w_in, b_in=b_in, w_mem=w_mem, attn_sinks=attn_sinks, g_branch=g_branch, w_out=w_out,
                   ln_gain=ln_gain, ln_bias=ln_bias)
    ms = dict(w_in=m_w_in, b_in=m_b_in, w_mem=m_w_mem, attn_sinks=m_attn_sinks, g_branch=m_g_branch, w_out=m_w_out,
              ln_gain=m_ln_gain, ln_bias=m_ln_bias)
    vs = dict(w_in=v_w_in, b_in=v_b_in, w_mem=v_w_mem, attn_sinks=v_attn_sinks, g_branch=v_g_branch, w_out=v_w_out,
              ln_gain=v_ln_gain, ln_bias=v_ln_bias)
    names = ["w_in", "b_in", "w_mem", "attn_sinks", "g_branch", "w_out", "ln_gain", "ln_bias"]
    deltas, new_m, new_v = [], [], []
    for n in names:
        shape = weights[n].shape
        two_d = lambda a: a.reshape(shape[-2], shape[-1])
        if n == "w_in":
            d, nm, nv, gw = (a.T for a in _adamw("adamw_w_in", w_in[0].T, r_in, m_w_in[0].T, v_w_in[0].T, SH_IN // 4,
                                                 copy_g=True))
            grads[n] = gw
        elif n in ("w_out", "w_mem"):
            d, nm, nv, grads[n] = _adamw("adamw_" + n, two_d(weights[n]), two_d(grads[n]), two_d(ms[n]), two_d(vs[n]),
                                         copy_g=True)
        else:
            d, nm, nv = _adamw("adamw_" + n, two_d(weights[n]), two_d(grads[n]), two_d(ms[n]), two_d(vs[n]))
        deltas.append(d.reshape(shape))
        new_m.append(nm.reshape(shape))
        new_v.append(nv.reshape(shape))
    return (loss, gx.reshape(B_LOC, SEQ, D_MODEL), *[grads[n].reshape(weights[n].shape) for n in names],
            *deltas, *new_m, *new_v)
```

```python
import functools

import jax
import jax.numpy as jnp
from jax import lax
from jax.experimental import pallas as pl
from jax.experimental.pallas import tpu as pltpu

F32, BF16 = jnp.float32, jnp.bfloat16

D_MODEL = 1024
SEQ = 2048
B_LOC = 2
T = B_LOC * SEQ
BLK = 128
MEM_LEN = 256
W_A, W_KV_A, W_B, W_C, D_MIX = 512, 128, 256, 256, 1024
D_IN = 2816
O_QA, O_KA, O_VA, O_QB, O_KB, O_VB, O_QC, O_Z = 0, 512, 640, 768, 1024, 1280, 1536, 1792
ROPE_THETA = 10000.0
LN_EPS = 1e-5
RMS_EPS = 1e-6
ALPHA = 2.0 ** 0.25
QK_SCALE = 0.125
N_CHIP = 4
SH_IN, SH_OUT, SH_MEM = D_IN // N_CHIP, D_MIX // N_CHIP, D_MODEL // N_CHIP
NEG = -1e30
ADAM_LR, ADAM_B1, ADAM_B2, ADAM_EPS, ADAM_WD, ADAM_STEP = 0.001, 0.9, 0.999, 1e-08, 0.01, 10
SV_W = 3072
MESH = pl.DeviceIdType.MESH

NN = ((1,), (0,))
NT = ((1,), (1,))
TN = ((0,), (0,))


def _dot(a, b, dims):
    return lax.dot_general(a, b, (dims, ((), ())), preferred_element_type=F32)


def _cp(sem=None, vmem_mb=None):
    kw = {}
    if sem is not None:
        kw["dimension_semantics"] = sem
    if vmem_mb is not None:
        kw["vmem_limit_bytes"] = vmem_mb * 1024 * 1024
    return pltpu.CompilerParams(**kw)


def _sds(shape, dtype):
    return pltpu.HBM(shape, dtype)


def _vm_sds(shape, dtype):
    return jax.ShapeDtypeStruct(shape, dtype)


def _pin(*args):
    return [pltpu.with_memory_space_constraint(a, pltpu.HBM) for a in args]


def _full(shape):
    n = len(shape)
    return pl.BlockSpec(shape, lambda *_: (0,) * n)


def _shard_rows(ref, n, chip, half):
    start = pl.multiple_of((2 * chip[0] + chip[1]) * n + half * (n // 2), 16)
    return ref.at[pl.ds(start, n // 2), :]


def _gather_weights(win_sh, wout_sh, wmem_sh):
    def body(a_ref, b_ref, c_ref, oa_ref, ob_ref, oc_ref, ici_send, ici_recv, d2d_send, d2d_recv):
        x, y, c = lax.axis_index("x"), lax.axis_index("y"), lax.axis_index("c")
        sibling = (x, y, 1 - c)
        chips = [(1 - x, y), (x, 1 - y), (1 - x, 1 - y)]
        for src, out, n in ((a_ref, oa_ref, SH_IN), (b_ref, ob_ref, SH_OUT), (c_ref, oc_ref, SH_MEM)):
            out[pl.ds(pl.multiple_of((2 * x + y) * n, 16), n), :] = src[...].astype(BF16)

        def copy(sems, j, chip_of_block, half, to):
            blk = _shard_rows(oa_ref, SH_IN, chip_of_block, half)
            return pltpu.make_async_remote_copy(
                src_ref=blk, dst_ref=blk, send_sem=sems[0].at[j], recv_sem=sems[1].at[j],
                device_id=to, device_id_type=MESH)

        ici, d2d = (ici_send, ici_recv), (d2d_send, d2d_recv)
        first = [copy(ici, j, (x, y), c, (*chip, c)) for j, chip in enumerate(chips)]
        for cp in first:
            cp.start()
        passed = []
        for j, chip in enumerate(chips):
            copy(ici, j, chip, c, (x, y, c)).wait_recv()
            fw = copy(d2d, j, chip, c, sibling)
            fw.start()
            passed.append(fw)
        for j, chip in enumerate(chips):
            copy(d2d, j, chip, 1 - c, (x, y, c)).wait_recv()
        for cp in first + passed:
            cp.wait_send()

    vm = pl.BlockSpec(memory_space=pltpu.VMEM)
    return pl.pallas_call(
        body, name="gather_weights",
        out_shape=(_vm_sds((D_IN, D_MODEL), BF16), _vm_sds((D_MIX, D_MODEL), BF16),
                   _vm_sds((D_MODEL, 2 * W_C), BF16)),
        in_specs=[vm, vm, vm], out_specs=(vm, vm, vm),
        scratch_shapes=[pltpu.SemaphoreType.DMA((3,))] * 4,
        compiler_params=_cp(vmem_mb=40),
    )(win_sh, wout_sh, wmem_sh)


def _rope(t, cos, sa, sb, sign):
    w = t.shape[1]
    reps = w // 128
    c, a, b = (jnp.tile(v, (1, reps)) if reps > 1 else v for v in (cos, sa, sb))
    rot = pltpu.roll(t, w - 32, 1) * a + pltpu.roll(t, 32, 1) * b
    return t * c + rot if sign > 0 else t * c - rot


def _in_proj(x, winT, b_in, cos, sa, sb, wout_own, wmem_own):
    tm = 512
    spt = SEQ // tm
    n_steps = T // tm
    forward_step = n_steps // 2

    def body(x_ref, w_ref, b_ref, cos_ref, sa_ref, sb_ref, wo_in, wm_in,
             xb_ref, qa_ref, ka_ref, va_ref, bn_ref, b4_ref, b16_ref, qc_ref, z_ref, wo_ref, wm_ref,
             scr, ici_send, ici_recv, d2d_send, d2d_recv):
        i = pl.program_id(0)
        mx, my, mc = lax.axis_index("x"), lax.axis_index("y"), lax.axis_index("c")
        chips = [(1 - mx, my), (mx, 1 - my), (1 - mx, 1 - my)]
        full = ((wo_ref, SH_OUT), (wm_ref, SH_MEM))

        def copy(sems, a, j, chip_of_block, half, to):
            blk = _shard_rows(full[a][0], full[a][1], chip_of_block, half)
            return pltpu.make_async_remote_copy(
                src_ref=blk, dst_ref=blk, send_sem=sems[0].at[a, j], recv_sem=sems[1].at[a, j],
                device_id=to, device_id_type=MESH)

        ici, d2d = (ici_send, ici_recv), (d2d_send, d2d_recv)
        pairs = [(a, j, chip) for j, chip in enumerate(chips) for a in range(2)]

        @pl.when(i == 0)
        def _():
            for a, j, chip in pairs:
                copy(ici, a, j, (mx, my), mc, (*chip, mc)).start()

        @pl.when(i == forward_step)
        def _():
            for a, j, chip in pairs:
                copy(ici, a, j, chip, mc, (mx, my, mc)).wait_recv()
                copy(d2d, a, j, chip, mc, (mx, my, 1 - mc)).start()

        @pl.when(i == n_steps - 1)
        def _():
            for a, j, chip in pairs:
                copy(d2d, a, j, chip, 1 - mc, (mx, my, mc)).wait_recv()
            for a, j, chip in pairs:
                copy(ici, a, j, (mx, my), mc, (*chip, mc)).wait_send()
                copy(d2d, a, j, chip, mc, (mx, my, 1 - mc)).wait_send()

        xb = x_ref[...].astype(BF16)
        xb_ref[...] = xb
        cos_t, sa_t, sb_t = cos_ref[...], sa_ref[...], sb_ref[...]

        def proj(r0, n):
            return _dot(xb, w_ref[r0:r0 + n, :], NT) + b_ref[:, r0:r0 + n]

        def rope(t):
            return _rope(t, cos_t, sa_t, sb_t, +1)

        qa_ref[...] = (rope(proj(O_QA, W_A)) * QK_SCALE).astype(BF16)
        ka_ref[...] = rope(proj(O_KA, W_KV_A)).astype(BF16)
        va_ref[...] = proj(O_VA, W_KV_A).astype(BF16)
        qc_ref[...] = (proj(O_QC, W_C) * QK_SCALE).astype(BF16)
        z_ref[...] = proj(O_Z, D_MIX).astype(BF16)
        parts = (rope(proj(O_QB, W_B)) * QK_SCALE, rope(proj(O_KB, W_B)), proj(O_VB, W_B))
        for k, part in enumerate(parts):
            bn_ref[:, 256 * k:256 * (k + 1)] = part.astype(BF16)
            scr[2 * k] = part[:, :128]
            scr[2 * k + 1] = part[:, 128:]
        for j in range(6):
            for res in range(4):
                b4_ref[0, res, :, 128 * j:128 * (j + 1)] = scr[j, pl.ds(res, tm // 4, stride=4), :].astype(BF16)
            for res in range(16):
                b16_ref[0, res, :, 128 * j:128 * (j + 1)] = scr[j, pl.ds(res, tm // 16, stride=16), :].astype(BF16)

    tok = lambda w: pl.BlockSpec((tm, w), lambda i: (i, 0))
    tab = pl.BlockSpec((tm, 128), lambda i: (i % spt, 0))
    hbm = pl.BlockSpec(memory_space=pl.ANY)
    return pl.pallas_call(
        body, name="in_proj", grid=(n_steps,),
        in_specs=[tok(D_MODEL), _full((D_IN, D_MODEL)), _full((1, D_IN)), tab, tab, tab, hbm, hbm],
        out_specs=(tok(D_MODEL), tok(W_A), tok(W_KV_A), tok(W_KV_A), tok(768),
                   pl.BlockSpec((1, 4, tm // 4, 768), lambda i: (i // spt, 0, i % spt, 0)),
                   pl.BlockSpec((1, 16, tm // 16, 768), lambda i: (i // spt, 0, i % spt, 0)),
                   tok(W_C), tok(D_MIX), hbm, hbm),
        out_shape=(_sds((T, D_MODEL), BF16), _sds((T, W_A), BF16), _sds((T, W_KV_A), BF16), _sds((T, W_KV_A), BF16),
                   _sds((T, 768), BF16), _sds((B_LOC, 4, SEQ // 4, 768), BF16), _sds((B_LOC, 16, SEQ // 16, 768), BF16),
                   _sds((T, W_C), BF16), _sds((T, D_MIX), BF16),
                   _sds((D_MIX, D_MODEL), BF16), _sds((D_MODEL, 2 * W_C), BF16)),
        input_output_aliases={6: 9, 7: 10},
        scratch_shapes=[pltpu.VMEM((6, tm, 128), F32)] + [pltpu.SemaphoreType.DMA((2, 3))] * 4,
        compiler_params=_cp(("arbitrary",), vmem_mb=48),
    )(*_pin(x, winT, b_in, cos, sa, sb, wout_own, wmem_own))


def _mem_kv(mem, wmem):
    def body(m_ref, w_ref, mb_ref, kv_ref):
        mb = m_ref[...].astype(BF16)
        mb_ref[...] = mb
        kv_ref[...] = _dot(mb, w_ref[...], NN).astype(BF16)

    n = B_LOC * MEM_LEN
    return pl.pallas_call(
        body, name="mem_kv",
        out_shape=(_sds((n, D_MODEL), BF16), _sds((n, 2 * W_C), BF16)),
    )(*_pin(mem, wmem))


class _Part:
    def __init__(self, body, args, in_specs, out_specs, out_shape, scratch=()):
        self.body, self.args, self.in_specs, self.out_specs, self.out_shape = body, args, in_specs, out_specs, out_shape
        self.scratch = list(scratch)


def _run_parts(name, parts, semantics, vmem_mb):
    n_in = [len(p.args) for p in parts]
    n_out = [len(p.out_shape) for p in parts]
    n_scr = [len(p.scratch) for p in parts]

    def body(*refs):
        ins, outs, scr = refs[:sum(n_in)], refs[sum(n_in):sum(n_in) + sum(n_out)], refs[sum(n_in) + sum(n_out):]
        i0 = o0 = s0 = 0
        for p, ni, no, ns in zip(parts, n_in, n_out, n_scr):
            p.body(*ins[i0:i0 + ni], *outs[o0:o0 + no], *scr[s0:s0 + ns])
            i0, o0, s0 = i0 + ni, o0 + no, s0 + ns

    res = pl.pallas_call(
        body, name=name, grid=(T // QR,),
        in_specs=[sp for p in parts for sp in p.in_specs], out_specs=tuple(sp for p in parts for sp in p.out_specs),
        out_shape=tuple(sh for p in parts for sh in p.out_shape),
        scratch_shapes=[sc for p in parts for sc in p.scratch],
        compiler_params=_cp((semantics,), vmem_mb=vmem_mb),
    )(*_pin(*[a for p in parts for a in p.args]))
    out, o0 = [], 0
    for no in n_out:
        out.append(tuple(res[o0:o0 + no]))
        o0 += no
    return out


QB = 8
QR = QB * BLK


def _lane_lo():
    return lax.broadcasted_iota(jnp.int32, (1, 128), 1) < 64


def _dup_head(k2, hk, lo):
    kf = k2.astype(F32)
    r = pltpu.roll(kf, 64, 1)
    return (jnp.where(lo, kf, r) if hk == 0 else jnp.where(lo, r, kf)).astype(BF16)


def _stack_heads(pairs, lo):
    parts = []
    for x2 in pairs:
        z = jnp.zeros_like(x2)
        parts += [jnp.where(lo, x2, z), jnp.where(lo, z, x2)]
    return jnp.concatenate(parts, axis=0)


def _prev_mode(kind, nb, j):
    if kind == "mem" or nb == 1:
        return "no"
    if nb <= QB:
        return "yes" if j % nb else "no"
    return "yes" if j else "dyn"


class _Attn:
    def __init__(self, kind, nb, max_dist, gqa, qw, kvw, qcb, kcb, vcb):
        self.kind, self.nb, self.gqa, self.qw, self.kvw = kind, nb, gqa, qw, kvw
        npairs = qw // 128
        self.groups = ([(hk, [2 * hk, 2 * hk + 1]) for hk in range(npairs // 2)] if gqa
                       else [(p, [p]) for p in range(npairs)])
        self.nh = 2 * len(self.groups[0][1])
        self.cols = 128 * self.nh
        self.reach = BLK - max_dist
        self.ext_prev = kind == "band" and nb > QB
        self.q_spec = pl.BlockSpec((QR, qw), lambda g: (g, qcb))
        self.row_spec = pl.BlockSpec((QR, qw), lambda g: (g, 0))
        self.stat_spec = pl.BlockSpec((QR, 128), lambda g: (g, 0))
        if kind == "mem":
            per = SEQ // QR
            self.kv_specs = [pl.BlockSpec((MEM_LEN, kvw), lambda g: (g // per, kcb)),
                             pl.BlockSpec((MEM_LEN, kvw), lambda g: (g // per, vcb))]
        else:
            self.kv_specs = [pl.BlockSpec((QR, kvw), lambda g: (g, kcb)), pl.BlockSpec((QR, kvw), lambda g: (g, vcb))]
            if self.ext_prev:
                self.kv_specs += [pl.BlockSpec((BLK, kvw), lambda g: (jnp.maximum(g * QB - 1, 0), kcb)),
                                  pl.BlockSpec((BLK, kvw), lambda g: (jnp.maximum(g * QB - 1, 0), vcb))]

    def masks(self):
        if self.kind == "mem":
            return None
        kj = lax.broadcasted_iota(jnp.int32, (2 * BLK, self.cols), 0)
        qi = lax.broadcasted_iota(jnp.int32, (2 * BLK, self.cols), 1) & (BLK - 1)
        kj1 = lax.broadcasted_iota(jnp.int32, (BLK, self.cols), 0)
        qi1 = lax.broadcasted_iota(jnp.int32, (BLK, self.cols), 1) & (BLK - 1)
        return kj, qi, kj1 <= qi1

    def keys(self, j, gi, kc_ref, vc_ref, kp_ref, vp_ref, lo, kq, g):
        def kv(k_ref, v_ref, r):
            if self.gqa:
                return _dup_head(k_ref[r, :], gi, lo), _dup_head(v_ref[r, :], gi, lo)
            sl = slice(128 * gi, 128 * (gi + 1))
            return k_ref[r, sl], v_ref[r, sl]

        if self.kind == "mem":
            key0 = pl.multiple_of((g // (SEQ // QR)) * MEM_LEN, MEM_LEN)
            return (*kv(kc_ref, vc_ref, slice(None)), None, [(0, MEM_LEN, key0)])
        kj, qi, cur = kq
        row0 = g * QR + BLK * j
        mode = _prev_mode(self.kind, self.nb, j)
        if mode == "no":
            return (*kv(kc_ref, vc_ref, slice(BLK * j, BLK * (j + 1))), cur, [(0, BLK, pl.multiple_of(row0, BLK))])
        if mode == "yes":
            mask = jnp.logical_and(kj >= qi + self.reach, kj <= qi + BLK)
            return (*kv(kc_ref, vc_ref, slice(BLK * (j - 1), BLK * (j + 1))), mask,
                    [(0, 2 * BLK, pl.multiple_of(row0 - BLK, BLK))])
        has_prev = ((g * QB) % self.nb) > 0
        hp = has_prev.astype(jnp.int32)
        mask = jnp.logical_and(kj >= qi * hp + (self.reach * hp + BLK * (1 - hp)), kj <= qi + BLK)
        kp, vp = kv(kp_ref, vp_ref, slice(None))
        kc, vc = kv(kc_ref, vc_ref, slice(0, BLK))
        return (jnp.concatenate([kp, kc], axis=0), jnp.concatenate([vp, vc], axis=0), mask,
                [(0, BLK, pl.multiple_of(jnp.maximum(row0 - BLK, 0), BLK)), (BLK, BLK, pl.multiple_of(row0, BLK))])


def _attn_fwd(q, qcb, qw, k, kcb, v, vcb, kvw, *, kind, nb=1, max_dist=BLK, gqa=False, sinks=None):
    a = _Attn(kind, nb, max_dist, gqa, qw, kvw, qcb, kcb, vcb)

    def body(*refs):
        it = iter(refs)
        q_ref, kc_ref, vc_ref = next(it), next(it), next(it)
        kp_ref, vp_ref = (next(it), next(it)) if a.ext_prev else (None, None)
        sink_ref = next(it) if sinks is not None else None
        o_ref, lse_ref = next(it), next(it)
        g = pl.program_id(0)
        lo = _lane_lo()
        top = lax.broadcasted_iota(jnp.int32, (128, 1), 0) < 64
        rid = lax.broadcasted_iota(jnp.int32, (8, 128), 0)
        kq = a.masks()
        for j in range(QB):
            rows = slice(BLK * j, BLK * (j + 1))
            stat = jnp.zeros((8, 128), F32)
            for gi, pairs in a.groups:
                qs = _stack_heads([q_ref[rows, 128 * p:128 * (p + 1)] for p in pairs], lo)
                kk, vv, mask, _ = a.keys(j, gi, kc_ref, vc_ref, kp_ref, vp_ref, lo, kq, g)
                pieces = [slice(r0, r0 + BLK) for r0 in range(0, kk.shape[0], BLK)]
                ss = []
                for r in pieces:
                    s = _dot(kk[r], qs, NT)
                    ss.append(s if mask is None else jnp.where(mask[r], s, NEG))
                m = jnp.max(ss[0], axis=0, keepdims=True)
                for s in ss[1:]:
                    m = jnp.maximum(m, jnp.max(s, axis=0, keepdims=True))
                if sink_ref is not None:
                    sk = jnp.concatenate([jnp.full((1, 128), sink_ref[0, a.nh * gi + i], F32) for i in range(a.nh)],
                                         axis=1)
                    m = jnp.maximum(m, sk)
                l, ot = None, None
                for r, s in zip(pieces, ss):
                    p = jnp.exp(s - m)
                    ps = jnp.sum(p, axis=0, keepdims=True)
                    c = _dot(vv[r], p.astype(BF16), TN)
                    l, ot = (ps, c) if l is None else (l + ps, ot + c)
                if sink_ref is not None:
                    l = l + jnp.exp(sk - m)
                ot = ot * pl.reciprocal(l, approx=True)
                lse = m + jnp.log(l)
                for i, p in enumerate(pairs):
                    o2t = jnp.where(top, ot[:, 256 * i:256 * i + 128], ot[:, 256 * i + 128:256 * i + 256])
                    o_ref[rows, 128 * p:128 * (p + 1)] = o2t.T.astype(BF16)
                for i in range(a.nh):
                    stat = jnp.where(rid == a.nh * gi + i, lse[:, 128 * i:128 * (i + 1)], stat)
            lse_ref[rows, :] = jnp.concatenate([stat, jnp.zeros((120, 128), F32)], axis=0).T

    args = [q, k, v] + ([k, v] if a.ext_prev else [])
    in_specs = [a.q_spec] + a.kv_specs
    if sinks is not None:
        args.append(sinks)
        in_specs.append(pl.BlockSpec(memory_space=pltpu.SMEM))
    return _Part(body, args, in_specs, [a.row_spec, a.stat_spec], [_sds((T, qw), BF16), _sds((T, 128), F32)])


def _attn_bwd(q, qcb, qw, k, kcb, v, vcb, kvw, do, lse, dl, *, kind, nb=1, max_dist=BLK, gqa=False, sinkv=None):
    a = _Attn(kind, nb, max_dist, gqa, qw, kvw, qcb, kcb, vcb)

    def body(*refs):
        it = iter(refs)
        q_ref, kc_ref, vc_ref = next(it), next(it), next(it)
        kp_ref, vp_ref = (next(it), next(it)) if a.ext_prev else (None, None)
        do_ref, lse_ref, dl_ref = next(it), next(it), next(it)
        sinkv_ref = next(it) if sinkv is not None else None
        dq_ref = next(it)
        if kind == "mem":
            dkv_ref = next(it)
        else:
            dk_out, dv_out = next(it), next(it)
        dsink_ref = next(it) if sinkv is not None else None
        if kind != "mem":
            dk_ref, dv_ref, stage_k, stage_v, flush_sem = next(it), next(it), next(it), next(it), next(it)
        g = pl.program_id(0)
        lo = _lane_lo()
        top = lax.broadcasted_iota(jnp.int32, (128, 1), 0) < 64

        @pl.when(g == 0)
        def _():
            if kind == "mem":
                dkv_ref[...] = jnp.zeros_like(dkv_ref)
            else:
                dk_ref[...] = jnp.zeros_like(dk_ref)
                dv_ref[...] = jnp.zeros_like(dv_ref)
            if dsink_ref is not None:
                dsink_ref[...] = jnp.zeros_like(dsink_ref)

        kq = a.masks()
        for j in range(QB):
            rows = slice(BLK * j, BLK * (j + 1))
            lse_t = lse_ref[rows, :].T
            dl_t = dl_ref[rows, :].T
            for gi, pairs in a.groups:
                heads = [a.nh * gi + i for i in range(a.nh)]
                qs = _stack_heads([q_ref[rows, 128 * p:128 * (p + 1)] for p in pairs], lo)
                dos = _stack_heads([do_ref[rows, 128 * p:128 * (p + 1)] for p in pairs], lo)
                lse_row = jnp.concatenate([lse_t[h:h + 1, :] for h in heads], axis=1)
                dl_row = jnp.concatenate([dl_t[h:h + 1, :] for h in heads], axis=1)
                kk, vv, mask, dests = a.keys(j, gi, kc_ref, vc_ref, kp_ref, vp_ref, lo, kq, g)
                s = _dot(kk, qs, NT)
                if mask is not None:
                    s = jnp.where(mask, s, NEG)
                p = jnp.exp(s - lse_row)
                ds = (p * (_dot(vv, dos, NT) - dl_row)).astype(BF16)
                dqt = _dot(kk, ds, TN)
                ck = _dot(ds, qs, NN)
                cv = _dot(p.astype(BF16), dos, NN)
                if gqa:
                    sel = lo if gi == 0 else jnp.logical_not(lo)
                    ck = jnp.where(sel, ck + pltpu.roll(ck, 64, 1), 0.0)
                    cv = jnp.where(sel, cv + pltpu.roll(cv, 64, 1), 0.0)
                    kcols = slice(0, 128)
                else:
                    kcols = slice(128 * gi, 128 * (gi + 1))
                for r0, nr, key0 in dests:
                    krows = pl.ds(key0, nr)
                    if kind == "mem":
                        dkv_ref[krows, kcols] += ck[r0:r0 + nr]
                        dkv_ref[krows, slice(kvw + kcols.start, kvw + kcols.stop)] += cv[r0:r0 + nr]
                    else:
                        dk_ref[krows, kcols] += ck[r0:r0 + nr]
                        dv_ref[krows, kcols] += cv[r0:r0 + nr]
                for i, p in enumerate(pairs):
                    dq2t = jnp.where(top, dqt[:, 256 * i:256 * i + 128], dqt[:, 256 * i + 128:256 * i + 256])
                    dq_ref[rows, 128 * p:128 * (p + 1)] = dq2t.T.astype(BF16)
        if dsink_ref is not None:
            ps = jnp.exp(sinkv_ref[...] - lse_ref[...]) * dl_ref[...]
            dsink_ref[...] += jnp.sum(ps, axis=0, keepdims=True)
        if kind != "mem":
            n_steps = T // QR

            def flush(step):
                rows = pl.ds(pl.multiple_of(step * QR, QR), QR)
                out = []
                for acc, stage, dst, i in ((dk_ref, stage_k, dk_out, 0), (dv_ref, stage_v, dv_out, 1)):
                    stage[...] = acc[rows, :].astype(BF16)
                    out.append(pltpu.make_async_copy(stage, dst.at[rows, :], flush_sem.at[i]))
                return out

            def flushed(step):
                rows = pl.ds(pl.multiple_of(step * QR, QR), QR)
                return [pltpu.make_async_copy(stage, dst.at[rows, :], flush_sem.at[i])
                        for stage, dst, i in ((stage_k, dk_out, 0), (stage_v, dv_out, 1))]

            @pl.when(g >= 2)
            def _():
                for cp in flushed(g - 2):
                    cp.wait()

            @pl.when(g >= 1)
            def _():
                for cp in flush(g - 1):
                    cp.start()

            @pl.when(g == n_steps - 1)
            def _():
                for cp in flushed(g - 1):
                    cp.wait()
                for cp in flush(g):
                    cp.start()
                for cp in flushed(g):
                    cp.wait()

    args = [q, k, v] + ([k, v] if a.ext_prev else []) + [do, lse, dl]
    in_specs = [a.q_spec] + a.kv_specs + [a.row_spec, a.stat_spec, a.stat_spec]
    if sinkv is not None:
        args.append(sinkv)
        in_specs.append(_full((1, 128)))
    out_shape = [_sds((T, qw), BF16)]
    out_specs = [a.row_spec]
    scratch = []
    if kind == "mem":
        out_shape.append(_sds((B_LOC * MEM_LEN, 2 * kvw), F32))
        out_specs.append(pl.BlockSpec((B_LOC * MEM_LEN, 2 * kvw), lambda g: (0, 0), pipeline_mode=pl.Buffered(1)))
    else:
        out_shape += [_sds((T, kvw), BF16)] * 2
        out_specs += [pl.BlockSpec(memory_space=pl.ANY)] * 2
        scratch = [pltpu.VMEM((T, kvw), F32)] * 2 + [pltpu.VMEM((QR, kvw), BF16)] * 2 + [pltpu.SemaphoreType.DMA((2,))]
    if sinkv is not None:
        out_shape.append(_sds((1, 128), F32))
        out_specs.append(_full((1, 128)))
    return _Part(body, args, in_specs, out_specs, out_shape, scratch)


def _dot2(v, w_ref):
    hi = v.astype(BF16)
    lo = (v - hi.astype(F32)).astype(BF16)
    return _dot(hi, w_ref[...], NN) + _dot(lo, w_ref[...], NN)


def _middle(oa, o1, l1, o4, l4, o16, l16, oc, z, x, tgt, g_br, ln_g, ln_b, wout, spread4, gather4, gather8):
    tm = 512
    spt = SEQ // tm

    def body(oa_ref, o1_ref, l1_ref, o4_ref, l4_ref, o16_ref, l16_ref, oc_ref, z_ref, x_ref, t_ref,
             g_ref, lg_ref, lb_ref, w_ref, sp4_ref, ga4_ref, ga8_ref,
             du_ref, dz_ref, doa_ref, dla_ref,
             dobn_ref, lsen_ref, dlbn_ref, dob4_ref, lse4_ref, dlb4_ref, dob16_ref, lse16_ref, dlb16_ref,
             doc_ref, dlc_ref, acc_ref, gout_ref, scr):
        i = pl.program_id(0)

        @pl.when(i == 0)
        def _():
            acc_ref[...] = jnp.zeros_like(acc_ref)
            gout_ref[...] = jnp.zeros_like(gout_ref)

        for res in range(4):
            rows = pl.ds(res, tm // 4, stride=4)
            for j in range(2):
                scr[j, rows, :] = o4_ref[0, res, :, 128 * j:128 * (j + 1)].astype(F32)
            scr[2, rows, :] = l4_ref[0, res]
        for res in range(16):
            rows = pl.ds(res, tm // 16, stride=16)
            for j in range(2):
                scr[3 + j, rows, :] = o16_ref[0, res, :, 128 * j:128 * (j + 1)].astype(F32)
            scr[5, rows, :] = l16_ref[0, res]
        cat = lambda a: jnp.concatenate([scr[a], scr[a + 1]], axis=1)
        o1v, o4v, o16v = o1_ref[...].astype(F32), cat(0), cat(3)
        l1v, l4v, l16v = l1_ref[...], scr[2], scr[5]
        mx = jnp.maximum(jnp.maximum(l1v, l4v), l16v)
        e1, e4, e16 = jnp.exp(l1v - mx), jnp.exp(l4v - mx), jnp.exp(l16v - mx)
        ssum = e1 + e4 + e16
        lse_b = mx + jnp.log(ssum)
        inv = 1.0 / ssum
        ob = (_dot2(e1 * inv, sp4_ref) * o1v + _dot2(e4 * inv, sp4_ref) * o4v + _dot2(e16 * inv, sp4_ref) * o16v)
        oav, ocv = oa_ref[...].astype(F32), oc_ref[...].astype(F32)

        def rms(o):
            r = lax.rsqrt(jnp.sum(o * o, axis=1, keepdims=True) * (1.0 / o.shape[1]) + RMS_EPS)
            return o * r, r

        na, ra = rms(oav)
        nb_, rb = rms(ob)
        nc, rc = rms(ocv)
        n = jnp.concatenate([na, nb_, nc], axis=1)
        zf = z_ref[...].astype(F32)
        sig = 1.0 / (1.0 + jnp.exp(-zf))
        sz = zf * sig
        gb = g_ref[...]
        yb = (n * gb * sz).astype(BF16)
        u = ALPHA * x_ref[...] + _dot(yb, w_ref[...], NN)
        inv_d = 1.0 / D_MODEL
        mu = jnp.sum(u, axis=1, keepdims=True) * inv_d
        uc = u - mu
        rstd = lax.rsqrt(jnp.sum(uc * uc, axis=1, keepdims=True) * inv_d + LN_EPS)
        xh = uc * rstd
        lg = lg_ref[...]
        diff = xh * lg + lb_ref[...] - t_ref[...]
        acc_ref[0:1, :] += jnp.sum(diff * diff, axis=0, keepdims=True) * (0.5 * inv_d)
        dout = diff * inv_d
        acc_ref[2:3, :] += jnp.sum(dout * xh, axis=0, keepdims=True)
        acc_ref[3:4, :] += jnp.sum(dout, axis=0, keepdims=True)
        dxh = dout * lg
        du = rstd * (dxh - jnp.sum(dxh, axis=1, keepdims=True) * inv_d
                     - xh * (jnp.sum(dxh * xh, axis=1, keepdims=True) * inv_d))
        dub = du.astype(BF16)
        du_ref[...] = dub
        gout_ref[...] += _dot(yb, dub, TN)
        dy = _dot(dub, w_ref[...], NT)
        t1 = dy * sz
        acc_ref[1:2, :] += jnp.sum(t1 * n, axis=0, keepdims=True)
        dn = t1 * gb
        dz_ref[...] = (dy * n * gb * (sig * (1.0 + zf * (1.0 - sig)))).astype(BF16)

        def rms_bwd(dn_, n_, r):
            return r * (dn_ - n_ * (jnp.sum(dn_ * n_, axis=1, keepdims=True) * (1.0 / n_.shape[1])))

        doa = rms_bwd(dn[:, :W_A], na, ra)
        dob = rms_bwd(dn[:, W_A:W_A + W_B], nb_, rb)
        doc = rms_bwd(dn[:, W_A + W_B:], nc, rc)
        doa_ref[...] = doa.astype(BF16)
        dla_ref[...] = _dot2(doa * oav, ga8_ref)
        doc_ref[...] = doc.astype(BF16)
        dlc_ref[...] = _dot2(doc * ocv, ga4_ref)
        dlb = _dot2(dob * ob, ga4_ref)
        dobn_ref[...] = dob.astype(BF16)
        lsen_ref[...] = lse_b
        dlbn_ref[...] = dlb
        scr[0] = dob[:, :128]
        scr[1] = dob[:, 128:]
        scr[2] = lse_b
        scr[3] = dlb
        for res in range(4):
            rows = pl.ds(res, tm // 4, stride=4)
            for j in range(2):
                dob4_ref[0, res, :, 128 * j:128 * (j + 1)] = scr[j, rows, :].astype(BF16)
            lse4_ref[0, res] = scr[2, rows, :]
            dlb4_ref[0, res] = scr[3, rows, :]
        for res in range(16):
            rows = pl.ds(res, tm // 16, stride=16)
            for j in range(2):
                dob16_ref[0, res, :, 128 * j:128 * (j + 1)] = scr[j, rows, :].astype(BF16)
            lse16_ref[0, res] = scr[2, rows, :]
            dlb16_ref[0, res] = scr[3, rows, :]

    tok = lambda w: pl.BlockSpec((tm, w), lambda i: (i, 0))
    p4 = lambda w: pl.BlockSpec((1, 4, tm // 4, w), lambda i: (i // spt, 0, i % spt, 0))
    p16 = lambda w: pl.BlockSpec((1, 16, tm // 16, w), lambda i: (i // spt, 0, i % spt, 0))
    s4 = lambda w, dt: _sds((B_LOC, 4, SEQ // 4, w), dt)
    s16 = lambda w, dt: _sds((B_LOC, 16, SEQ // 16, w), dt)
    row = _full((1, D_MODEL))
    return pl.pallas_call(
        body, name="middle", grid=(T // tm,),
        in_specs=[tok(W_A), tok(W_B), tok(128), p4(W_B), p4(128), p16(W_B), p16(128), tok(W_C), tok(D_MIX),
                  tok(D_MODEL), tok(D_MODEL), row, row, row, _full((D_MIX, D_MODEL)),
                  _full((128, W_B)), _full((W_B, 128)), _full((W_A, 128))],
        out_specs=(tok(D_MODEL), tok(D_MIX), tok(W_A), tok(128),
                   tok(W_B), tok(128), tok(128), p4(W_B), p4(128), p4(128), p16(W_B), p16(128), p16(128),
                   tok(W_C), tok(128), _full((8, D_MODEL)), _full((D_MIX, D_MODEL))),
        out_shape=(_sds((T, D_MODEL), BF16), _sds((T, D_MIX), BF16),
                   _sds((T, W_A), BF16), _sds((T, 128), F32),
                   _sds((T, W_B), BF16), _sds((T, 128), F32), _sds((T, 128), F32),
                   s4(W_B, BF16), s4(128, F32), s4(128, F32), s16(W_B, BF16), s16(128, F32), s16(128, F32),
                   _sds((T, W_C), BF16), _sds((T, 128), F32), _sds((8, D_MODEL), F32),
                   _sds((D_MIX, D_MODEL), F32)),
        scratch_shapes=[pltpu.VMEM((6, tm, 128), F32)],
        compiler_params=_cp(("arbitrary",), vmem_mb=56),
    )(*_pin(oa, o1, l1, o4, l4, o16, l16, oc, z, x, tgt, g_br, ln_g, ln_b, wout, spread4, gather4, gather8))


class _ReduceScatter:
    def __init__(self, shapes):
        self.shapes = shapes

    def scratch_shapes(self):
        out = []
        for n, w in self.shapes:
            h = n // 2
            out += [pltpu.VMEM((4, h, w), F32), pltpu.VMEM((4, h, w), F32), pltpu.VMEM((3, h, w), BF16),
                    pltpu.VMEM((3, h, w), BF16), pltpu.VMEM((h, w), F32)]
        na = len(self.shapes)
        dma = pltpu.SemaphoreType.DMA
        return out + [dma((na, 4)), dma((na, 4)), dma((na, 4)), dma((na, 3)), dma((na, 3)), dma((na,)), dma((na,)),
                      dma((na,))]

    def bind(self, g_refs, r_refs, scratch):
        na = len(self.shapes)
        bufs = [scratch[5 * a:5 * a + 5] for a in range(na)]
        mine, sib, stage, land, tot = (tuple(b[i] for b in bufs) for i in range(5))
        loc_sem, s1_send, s1_recv, s2_send, s2_recv, s3_send, s3_recv, st_sem = scratch[5 * na:5 * na + 8]
        x, y, c = lax.axis_index("x"), lax.axis_index("y"), lax.axis_index("c")
        me, sibling = (x, y, c), (x, y, 1 - c)
        my_chip = 2 * x + y
        chips = [(1 - x, y), (x, 1 - y), (1 - x, 1 - y)]
        order = [2 * chip[0] + chip[1] for chip in chips] + [my_chip]

        def rows(a, k, half):
            n = self.shapes[a][0]
            return pl.ds(pl.multiple_of(k * n + half * (n // 2), 8), n // 2)

        def load(a, k):
            return pltpu.make_async_copy(g_refs[a].at[rows(a, k, c), :], mine[a].at[k], loc_sem.at[a, k])

        def s1(a, k, half):
            return pltpu.make_async_remote_copy(
                src_ref=g_refs[a].at[rows(a, k, half), :], dst_ref=sib[a].at[k],
                send_sem=s1_send.at[a, k], recv_sem=s1_recv.at[a, k], device_id=sibling, device_id_type=MESH)

        def s2(a, j, to):
            return pltpu.make_async_remote_copy(
                src_ref=stage[a].at[j], dst_ref=land[a].at[j], send_sem=s2_send.at[a, j], recv_sem=s2_recv.at[a, j],
                device_id=to, device_id_type=MESH)

        def s3(a, half, to):
            return pltpu.make_async_remote_copy(
                src_ref=tot[a], dst_ref=r_refs[a].at[rows(a, 0, half), :], send_sem=s3_send.at[a],
                recv_sem=s3_recv.at[a], device_id=to, device_id_type=MESH)

        def store(a):
            return pltpu.make_async_copy(tot[a], r_refs[a].at[rows(a, 0, c), :], st_sem.at[a])

        def start():
            for k in order:
                for a in range(na):
                    load(a, k).start()
                    s1(a, k, 1 - c).start()

        def exchange():
            for j, chip in enumerate(chips):
                k = order[j]
                for a in range(na):
                    load(a, k).wait()
                    s1(a, k, c).wait_recv()
                    stage[a][j] = (mine[a][k] + sib[a][k]).astype(BF16)
                    s2(a, j, (*chip, c)).start()
            for a in range(na):
                load(a, my_chip).wait()
                s1(a, my_chip, c).wait_recv()
                tot[a][...] = mine[a][my_chip] + sib[a][my_chip]

        def finish():
            for a in range(na):
                t = tot[a][...]
                for j in range(3):
                    s2(a, j, me).wait_recv()
                    t = t + land[a][j].astype(F32)
                tot[a][...] = t
                s3(a, c, sibling).start()
                store(a).start()

        def drain():
            for a in range(na):
                s3(a, 1 - c, me).wait_recv()
                store(a).wait()
            for a in range(na):
                for k in order:
                    s1(a, k, 1 - c).wait_send()
                for j, chip in enumerate(chips):
                    s2(a, j, (*chip, c)).wait_send()
                s3(a, c, sibling).wait_send()

        return start, exchange, finish, drain

    def part(self, grads, steps):
        def body(*refs):
            na = len(self.shapes)
            i = pl.program_id(0)
            for step, phase in zip(steps, self.bind(refs[:na], refs[na:2 * na], refs[2 * na:])):
                pl.when(i == step)(phase)

        hbm = pl.BlockSpec(memory_space=pl.ANY)
        return _Part(body, list(grads), [hbm] * len(grads), [hbm] * len(grads),
                     [_sds((n, w), F32) for n, w in self.shapes], self.scratch_shapes())


def _dh_dx(dqa, dka, dva, dqn, dkn, dvn, dq4, dk4, dv4, dq16, dk16, dv16, dqc, dz, du, xb, cos, sa, sb, winT):
    tm = 512
    spt = SEQ // tm

    def body(dqa_ref, dka_ref, dva_ref, dqn_ref, dkn_ref, dvn_ref, dq4_ref, dk4_ref, dv4_ref,
             dq16_ref, dk16_ref, dv16_ref, dqc_ref, dz_ref, du_ref, xb_ref, cos_ref, sa_ref, sb_ref, w_ref,
             gx_ref, db_ref, gin_ref, dh_ref, scr):
        i = pl.program_id(0)

        @pl.when(i == 0)
        def _():
            db_ref[...] = jnp.zeros_like(db_ref)
            gin_ref[...] = jnp.zeros_like(gin_ref)

        cos_t, sa_t, sb_t = cos_ref[...], sa_ref[...], sb_ref[...]

        def rope_t(t):
            return _rope(t, cos_t, sa_t, sb_t, -1)

        def put(r0, val):
            n = val.shape[1]
            dh_ref[:, r0:r0 + n] = val.astype(BF16)
            db_ref[:, r0:r0 + n] += jnp.sum(val, axis=0, keepdims=True)

        put(O_QA, rope_t(dqa_ref[...].astype(F32)) * QK_SCALE)
        put(O_KA, rope_t(dka_ref[...].astype(F32)))
        put(O_VA, dva_ref[...].astype(F32))
        put(O_QC, dqc_ref[...].astype(F32) * QK_SCALE)
        put(O_Z, dz_ref[...].astype(F32))
        for k, (n_ref, r4, r16) in enumerate(((dqn_ref, dq4_ref, dq16_ref), (dkn_ref, dk4_ref, dk16_ref),
                                               (dvn_ref, dv4_ref, dv16_ref))):
            for j in range(2):
                sl = slice(128 * j, 128 * (j + 1))
                scr[2 * k + j] = n_ref[:, sl].astype(F32)
                for res in range(4):
                    scr[2 * k + j, pl.ds(res, tm // 4, stride=4), :] += r4[0, res, :, sl].astype(F32)
                for res in range(16):
                    scr[2 * k + j, pl.ds(res, tm // 16, stride=16), :] += r16[0, res, :, sl].astype(F32)
        cat = lambda a: jnp.concatenate([scr[a], scr[a + 1]], axis=1)
        put(O_QB, rope_t(cat(0)) * QK_SCALE)
        put(O_KB, rope_t(cat(2)))
        put(O_VB, cat(4))
        gx_ref[...] = _dot(dh_ref[...], w_ref[...], NN) + ALPHA * du_ref[...].astype(F32)
        gin_ref[...] += _dot(dh_ref[...], xb_ref[...], TN)

    tok = lambda w: pl.BlockSpec((tm, w), lambda i: (i, 0))
    tab = pl.BlockSpec((tm, 128), lambda i: (i % spt, 0))
    p4 = pl.BlockSpec((1, 4, tm // 4, W_B), lambda i: (i // spt, 0, i % spt, 0))
    p16 = pl.BlockSpec((1, 16, tm // 16, W_B), lambda i: (i // spt, 0, i % spt, 0))
    once = lambda shape: pl.BlockSpec(shape, lambda i: (0, 0), pipeline_mode=pl.Buffered(1))
    return pl.pallas_call(
        body, name="dh_dx", grid=(T // tm,),
        in_specs=[tok(W_A), tok(W_KV_A), tok(W_KV_A), tok(W_B), tok(W_B), tok(W_B), p4, p4, p4, p16, p16, p16,
                  tok(W_C), tok(D_MIX), tok(D_MODEL), tok(D_MODEL), tab, tab, tab, once((D_IN, D_MODEL))],
        out_specs=(tok(D_MODEL), _full((1, D_IN)), once((D_IN, D_MODEL))),
        out_shape=(_sds((T, D_MODEL), F32), _sds((1, D_IN), F32), _sds((D_IN, D_MODEL), F32)),
        scratch_shapes=[pltpu.VMEM((tm, D_IN), BF16), pltpu.VMEM((6, tm, 128), F32)],
        compiler_params=_cp(("arbitrary",), vmem_mb=56),
    )(*_pin(dqa, dka, dva, dqn, dkn, dvn, dq4, dk4, dv4, dq16, dk16, dv16, dqc, dz, du, xb, cos, sa, sb, winT))


def _tn_matmul(name, a, b, bm, bt):
    n, m_all = a.shape
    n_cols = b.shape[1]

    def body(a_ref, b_ref, o_ref):
        @pl.when(pl.program_id(1) == 0)
        def _():
            o_ref[...] = jnp.zeros_like(o_ref)

        o_ref[...] += _dot(a_ref[...].astype(BF16), b_ref[...].astype(BF16), TN)

    return pl.pallas_call(
        body, name=name, grid=(m_all // bm, n // bt),
        in_specs=[pl.BlockSpec((bt, bm), lambda m, t: (t, m)), pl.BlockSpec((bt, n_cols), lambda m, t: (t, 0))],
        out_specs=pl.BlockSpec((bm, n_cols), lambda m, t: (m, 0)),
        out_shape=_sds((m_all, n_cols), F32),
        compiler_params=_cp(("parallel", "arbitrary"), vmem_mb=48),
    )(*_pin(a, b))


def _reduce_grads(g_in, acc, dbin, dsink):
    rs = _ReduceScatter([(SH_IN, D_MODEL)])

    def body(g_ref, acc_ref, dbin_ref, dsink_ref, r_ref, sv_ref, sv_mine, sv_all, sv_send, sv_recv, *rs_scratch):
        x, y, c = lax.axis_index("x"), lax.axis_index("y"), lax.axis_index("c")
        chips = [(1 - x, y), (x, 1 - y), (1 - x, 1 - y)]
        start, exchange, finish, drain = rs.bind((g_ref,), (r_ref,), rs_scratch)
        start()

        sv_mine[...] = jnp.zeros_like(sv_mine)
        sv_mine[0:4, 0:D_MODEL] = acc_ref[0:4, :]
        sv_mine[4:5, 0:D_IN] = dbin_ref[...]
        sv_mine[5:6, 0:128] = dsink_ref[...]
        my_dev = 4 * x + 2 * y + c
        others = [(x, y, 1 - c)] + [(*chip, cc) for chip in chips for cc in (c, 1 - c)]

        def sv_copy(j, to):
            return pltpu.make_async_remote_copy(
                src_ref=sv_mine, dst_ref=sv_all.at[my_dev], send_sem=sv_send.at[j], recv_sem=sv_recv.at[j],
                device_id=to, device_id_type=MESH)

        sv_sends = [sv_copy(j, to) for j, to in enumerate(others)]
        for cp in sv_sends:
            cp.start()
        exchange()
        finish()
        sv_all[my_dev] = sv_mine[...]
        for j in range(7):
            sv_copy(j, (x, y, c)).wait_recv()
        tot = sv_all[0]
        for d in range(1, 8):
            tot = tot + sv_all[d]
        sv_ref[...] = tot
        drain()
        for cp in sv_sends:
            cp.wait_send()

    vm = pl.BlockSpec(memory_space=pltpu.VMEM)
    hbm = pl.BlockSpec(memory_space=pl.ANY)
    return pl.pallas_call(
        body, name="reduce_grads",
        out_shape=(_sds((SH_IN, D_MODEL), F32), _vm_sds((8, SV_W), F32)),
        in_specs=[hbm, vm, vm, vm], out_specs=(hbm, vm),
        scratch_shapes=[pltpu.VMEM((8, SV_W), F32), pltpu.VMEM((8, 8, SV_W), F32),
                        pltpu.SemaphoreType.DMA((7,)), pltpu.SemaphoreType.DMA((7,))] + rs.scratch_shapes(),
        compiler_params=_cp(vmem_mb=40),
    )(pltpu.with_memory_space_constraint(g_in, pltpu.HBM), acc, dbin, dsink)


def _adamw(name, w, g, m, v, rows=None, copy_g=False):
    shape = w.shape
    rows = shape[0] if rows is None else rows
    n_out = 4 if copy_g else 3

    def body(w_ref, g_ref, m_ref, v_ref, d_ref, nm_ref, nv_ref, *go_ref):
        gv = g_ref[...]
        if copy_g:
            go_ref[0][...] = gv
        nm = ADAM_B1 * m_ref[...] + (1.0 - ADAM_B1) * gv
        nv = ADAM_B2 * v_ref[...] + (1.0 - ADAM_B2) * (gv * gv)
        m_hat = nm / (1.0 - ADAM_B1 ** ADAM_STEP)
        v_hat = nv / (1.0 - ADAM_B2 ** ADAM_STEP)
        d_ref[...] = -ADAM_LR * (m_hat / (jnp.sqrt(v_hat) + ADAM_EPS) + ADAM_WD * w_ref[...])
        nm_ref[...] = nm
        nv_ref[...] = nv

    spec = pl.BlockSpec((rows, shape[1]), lambda i: (i, 0))
    return pl.pallas_call(
        body, name=name, grid=(shape[0] // rows,), in_specs=[spec] * 4, out_specs=(spec,) * n_out,
        out_shape=(_sds(shape, F32),) * n_out, compiler_params=_cp(("parallel",)),
    )(*_pin(w, g, m, v))


def _rope_tables():
    pos = jnp.arange(SEQ, dtype=F32)
    inv = ROPE_THETA ** (-jnp.arange(0, 64, 2, dtype=F32) / 64)
    ang = pos[:, None] * inv[None, :]
    cos, sin = lax.optimization_barrier((jnp.cos(ang), jnp.sin(ang)))
    cos, sin = jnp.tile(cos, (1, 4)), jnp.tile(sin, (1, 4))
    low = (jnp.arange(128) % 64) < 32
    return cos, jnp.where(low, -sin, 0.0), jnp.where(low, 0.0, sin)


def _local_step(x2, mem2, tgt2, winT, wout, wmem, b_in, sinks, g_branch, ln_gain, ln_bias):
    cos, sa, sb = _rope_tables()
    sinkv = jnp.pad(sinks, ((0, 0), (0, 120)))
    head_of_lane = jnp.arange(512)[None, :] // 64
    gather8 = (head_of_lane.T == jnp.arange(128)[None, :]).astype(BF16)
    gather4 = gather8[:W_B]
    spread4 = gather4.T

    xb, qa, ka, va, bn, b4, b16, qc, z, wout, wmem = _in_proj(x2, winT, b_in, cos, sa, sb, wout, wmem)
    memb, mkv = _mem_kv(mem2, wmem)
    b4f, b16f = b4.reshape(T, 768), b16.reshape(T, 768)

    swa = dict(kind="band", nb=SEQ // BLK, max_dist=BLK - 1, gqa=True)
    dil = (dict(kind="band", nb=SEQ // BLK), dict(kind="band", nb=SEQ // 4 // BLK), dict(kind="band", nb=1))
    (oa, lse_a), (o1, l1), (o4, l4), (o16, l16), (oc, lse_c) = _run_parts("attn_fwd", [
        _attn_fwd(qa, 0, W_A, ka, 0, va, 0, W_KV_A, sinks=sinks, **swa),
        _attn_fwd(bn, 0, W_B, bn, 1, bn, 2, W_B, **dil[0]),
        _attn_fwd(b4f, 0, W_B, b4f, 1, b4f, 2, W_B, **dil[1]),
        _attn_fwd(b16f, 0, W_B, b16f, 1, b16f, 2, W_B, **dil[2]),
        _attn_fwd(qc, 0, W_C, mkv, 0, mkv, 1, W_C, kind="mem")], "parallel", 48)

    s4 = lambda w: (B_LOC, 4, SEQ // 4, w)
    s16 = lambda w: (B_LOC, 16, SEQ // 16, w)
    (du, dz, doa, dla, dobn, lsen, dlbn, dob4, lse4, dlb4, dob16, lse16, dlb16, doc, dlc, acc, g_out) = _middle(
        oa, o1, l1, o4.reshape(s4(W_B)), l4.reshape(s4(128)), o16.reshape(s16(W_B)), l16.reshape(s16(128)), oc, z,
        x2, tgt2, g_branch, ln_gain, ln_bias, wout, spread4, gather4, gather8)

    flat = lambda a: a.reshape(T, a.shape[-1])
    (dqa, dka, dva, dsink), (dqc, dmkv) = _run_parts("attn_bwd_a", [
        _attn_bwd(qa, 0, W_A, ka, 0, va, 0, W_KV_A, doa, lse_a, dla, sinkv=sinkv, **swa),
        _attn_bwd(qc, 0, W_C, mkv, 0, mkv, 1, W_C, doc, lse_c, dlc, kind="mem")], "arbitrary", 48)
    g_mem = _tn_matmul("dw_mem", memb, dmkv, D_MODEL, B_LOC * MEM_LEN)
    last = T // QR - 1
    (r_out, r_mem), (dqn, dkn, dvn), (dq4, dk4, dv4), (dq16, dk16, dv16) = _run_parts("attn_bwd_b", [
        _ReduceScatter([(SH_OUT, D_MODEL), (SH_MEM, 2 * W_C)]).part((g_out, g_mem), (0, 1, last, last)),
        _attn_bwd(bn, 0, W_B, bn, 1, bn, 2, W_B, dobn, lsen, dlbn, **dil[0]),
        _attn_bwd(b4f, 0, W_B, b4f, 1, b4f, 2, W_B, flat(dob4), flat(lse4), flat(dlb4), **dil[1]),
        _attn_bwd(b16f, 0, W_B, b16f, 1, b16f, 2, W_B, flat(dob16), flat(lse16), flat(dlb16), **dil[2])],
        "arbitrary", 60)

    r4 = lambda a: a.reshape(s4(W_B))
    r16 = lambda a: a.reshape(s16(W_B))
    gx, dbin, g_in = _dh_dx(dqa, dka, dva, dqn, dkn, dvn, r4(dq4), r4(dk4), r4(dv4), r16(dq16), r16(dk16),
                            r16(dv16), dqc, dz, du, xb, cos, sa, sb, winT)
    return gx, g_in, r_out, r_mem, acc, dbin, dsink


def kernel(x, mem, w_in, b_in, w_mem, attn_sinks, g_branch, w_out, ln_gain, ln_bias, loss_target, m_w_in, m_b_in, m_w_mem, m_attn_sinks, m_g_branch, m_w_out, m_ln_gain, m_ln_bias, v_w_in, v_b_in, v_w_mem, v_attn_sinks, v_g_branch, v_w_out, v_ln_gain, v_ln_bias):
    winT, wout, wmem = _gather_weights(w_in[0].T, w_out[0], w_mem[0])
    gx, g_in, r_out, r_mem, acc, dbin, dsink = _local_step(
        x.reshape(T, D_MODEL), mem.reshape(B_LOC * MEM_LEN, D_MODEL), loss_target.reshape(T, D_MODEL),
        winT, wout, wmem, b_in, attn_sinks, g_branch, ln_gain, ln_bias)
    r_in, sv = _reduce_grads(g_in, acc, dbin, dsink)

    loss = jnp.sum(sv[0, :D_MODEL])
    grads = {
        "b_in": sv[4:5, :D_IN], "w_mem": r_mem[None],
        "attn_sinks": -sv[5:6, 0:8], "g_branch": sv[1:2, :D_MODEL], "w_out": r_out[None],
        "ln_gain": sv[2:3, :D_MODEL], "ln_bias": sv[3:4, :D_MODEL],
    }
    weights = dict(w_in=w_in, b_in=b_in, w_mem=w_mem, attn_sinks=attn_sinks, g_branch=g_branch, w_out=w_out,
                   ln_gain=ln_gain, ln_bias=ln_bias)
    ms = dict(w_in=m_w_in, b_in=m_b_in, w_mem=m_w_mem, attn_sinks=m_attn_sinks, g_branch=m_g_branch, w_out=m_w_out,
              ln_gain=m_ln_gain, ln_bias=m_ln_bias)
    vs = dict(w_in=v_w_in, b_in=v_b_in, w_mem=v_w_mem, attn_sinks=v_attn_sinks, g_branch=v_g_branch, w_out=v_w_out,
              ln_gain=v_ln_gain, ln_bias=v_ln_bias)
    names = ["w_in", "b_in", "w_mem", "attn_sinks", "g_branch", "w_out", "ln_gain", "ln_bias"]
    deltas, new_m, new_v = [], [], []
    for n in names:
        shape = weights[n].shape
        two_d = lambda a: a.reshape(shape[-2], shape[-1])
        if n == "w_in":
            d, nm, nv, gw = (a.T for a in _adamw("adamw_w_in", w_in[0].T, r_in, m_w_in[0].T, v_w_in[0].T, SH_IN // 4,
                                                 copy_g=True))
            grads[n] = gw
        elif n in ("w_out", "w_mem"):
            d, nm, nv, grads[n] = _adamw("adamw_" + n, two_d(weights[n]), two_d(grads[n]), two_d(ms[n]), two_d(vs[n]),
                                         copy_g=True)
        else:
            d, nm, nv = _adamw("adamw_" + n, two_d(weights[n]), two_d(grads[n]), two_d(ms[n]), two_d(vs[n]))
        deltas.append(d.reshape(shape))
        new_m.append(nm.reshape(shape))
        new_v.append(nv.reshape(shape))
    return (loss, gx.reshape(B_LOC, SEQ, D_MODEL), *[grads[n].reshape(weights[n].shape) for n in names],
            *deltas, *new_m, *new_v)
```

```python
import functools

import jax
import jax.numpy as jnp
from jax import lax
from jax.experimental import pallas as pl
from jax.experimental.pallas import tpu as pltpu

F32, BF16 = jnp.float32, jnp.bfloat16

D_MODEL = 1024
SEQ = 2048
B_LOC = 2
T = B_LOC * SEQ
BLK = 128
MEM_LEN = 256
W_A, W_KV_A, W_B, W_C, D_MIX = 512, 128, 256, 256, 1024
D_IN = 2816
O_QA, O_KA, O_VA, O_QB, O_KB, O_VB, O_QC, O_Z = 0, 512, 640, 768, 1024, 1280, 1536, 1792
ROPE_THETA = 10000.0
LN_EPS = 1e-5
RMS_EPS = 1e-6
ALPHA = 2.0 ** 0.25
QK_SCALE = 0.125
N_CHIP = 4
SH_IN, SH_OUT, SH_MEM = D_IN // N_CHIP, D_MIX // N_CHIP, D_MODEL // N_CHIP
NEG = -1e30
ADAM_LR, ADAM_B1, ADAM_B2, ADAM_EPS, ADAM_WD, ADAM_STEP = 0.001, 0.9, 0.999, 1e-08, 0.01, 10
SV_W = 3072
MESH = pl.DeviceIdType.MESH

NN = ((1,), (0,))
NT = ((1,), (1,))
TN = ((0,), (0,))


def _dot(a, b, dims):
    return lax.dot_general(a, b, (dims, ((), ())), preferred_element_type=F32)


def _cp(sem=None, vmem_mb=None):
    kw = {}
    if sem is not None:
        kw["dimension_semantics"] = sem
    if vmem_mb is not None:
        kw["vmem_limit_bytes"] = vmem_mb * 1024 * 1024
    return pltpu.CompilerParams(**kw)


def _sds(shape, dtype):
    return pltpu.HBM(shape, dtype)


def _vm_sds(shape, dtype):
    return jax.ShapeDtypeStruct(shape, dtype)


def _pin(*args):
    return [pltpu.with_memory_space_constraint(a, pltpu.HBM) for a in args]


def _full(shape):
    n = len(shape)
    return pl.BlockSpec(shape, lambda *_: (0,) * n)


def _shard_rows(ref, n, chip, half):
    start = pl.multiple_of((2 * chip[0] + chip[1]) * n + half * (n // 2), 16)
    return ref.at[pl.ds(start, n // 2), :]


def _gather_weights(win_sh, wout_sh, wmem_sh):
    def body(a_ref, b_ref, c_ref, oa_ref, ob_ref, oc_ref, ici_send, ici_recv, d2d_send, d2d_recv):
        x, y, c = lax.axis_index("x"), lax.axis_index("y"), lax.axis_index("c")
        sibling = (x, y, 1 - c)
        chips = [(1 - x, y), (x, 1 - y), (1 - x, 1 - y)]
        for src, out, n in ((a_ref, oa_ref, SH_IN), (b_ref, ob_ref, SH_OUT), (c_ref, oc_ref, SH_MEM)):
            out[pl.ds(pl.multiple_of((2 * x + y) * n, 16), n), :] = src[...].astype(BF16)

        def copy(sems, j, chip_of_block, half, to):
            blk = _shard_rows(oa_ref, SH_IN, chip_of_block, half)
            return pltpu.make_async_remote_copy(
                src_ref=blk, dst_ref=blk, send_sem=sems[0].at[j], recv_sem=sems[1].at[j],
                device_id=to, device_id_type=MESH)

        ici, d2d = (ici_send, ici_recv), (d2d_send, d2d_recv)
        first = [copy(ici, j, (x, y), c, (*chip, c)) for j, chip in enumerate(chips)]
        for cp in first:
            cp.start()
        passed = []
        for j, chip in enumerate(chips):
            copy(ici, j, chip, c, (x, y, c)).wait_recv()
            fw = copy(d2d, j, chip, c, sibling)
            fw.start()
            passed.append(fw)
        for j, chip in enumerate(chips):
            copy(d2d, j, chip, 1 - c, (x, y, c)).wait_recv()
        for cp in first + passed:
            cp.wait_send()

    vm = pl.BlockSpec(memory_space=pltpu.VMEM)
    return pl.pallas_call(
        body, name="gather_weights",
        out_shape=(_vm_sds((D_IN, D_MODEL), BF16), _vm_sds((D_MIX, D_MODEL), BF16),
                   _vm_sds((D_MODEL, 2 * W_C), BF16)),
        in_specs=[vm, vm, vm], out_specs=(vm, vm, vm),
        scratch_shapes=[pltpu.SemaphoreType.DMA((3,))] * 4,
        compiler_params=_cp(vmem_mb=40),
    )(win_sh, wout_sh, wmem_sh)


def _rope(t, cos, sa, sb, sign):
    w = t.shape[1]
    reps = w // 128
    c, a, b = (jnp.tile(v, (1, reps)) if reps > 1 else v for v in (cos, sa, sb))
    rot = pltpu.roll(t, w - 32, 1) * a + pltpu.roll(t, 32, 1) * b
    return t * c + rot if sign > 0 else t * c - rot


def _in_proj(x, winT, b_in, cos, sa, sb, wout_own, wmem_own):
    tm = 512
    spt = SEQ // tm
    n_steps = T // tm
    forward_step = n_steps // 2

    def body(x_ref, w_ref, b_ref, cos_ref, sa_ref, sb_ref, wo_in, wm_in,
             xb_ref, qa_ref, ka_ref, va_ref, bn_ref, b4_ref, b16_ref, qc_ref, z_ref, wo_ref, wm_ref,
             scr, ici_send, ici_recv, d2d_send, d2d_recv):
        i = pl.program_id(0)
        mx, my, mc = lax.axis_index("x"), lax.axis_index("y"), lax.axis_index("c")
        chips = [(1 - mx, my), (mx, 1 - my), (1 - mx, 1 - my)]
        full = ((wo_ref, SH_OUT), (wm_ref, SH_MEM))

        def copy(sems, a, j, chip_of_block, half, to):
            blk = _shard_rows(full[a][0], full[a][1], chip_of_block, half)
            return pltpu.make_async_remote_copy(
                src_ref=blk, dst_ref=blk, send_sem=sems[0].at[a, j], recv_sem=sems[1].at[a, j],
                device_id=to, device_id_type=MESH)

        ici, d2d = (ici_send, ici_recv), (d2d_send, d2d_recv)
        pairs = [(a, j, chip) for j, chip in enumerate(chips) for a in range(2)]

        @pl.when(i == 0)
        def _():
            for a, j, chip in pairs:
                copy(ici, a, j, (mx, my), mc, (*chip, mc)).start()

        @pl.when(i == forward_step)
        def _():
            for a, j, chip in pairs:
                copy(ici, a, j, chip, mc, (mx, my, mc)).wait_recv()
                copy(d2d, a, j, chip, mc, (mx, my, 1 - mc)).start()

        @pl.when(i == n_steps - 1)
        def _():
            for a, j, chip in pairs:
                copy(d2d, a, j, chip, 1 - mc, (mx, my, mc)).wait_recv()
            for a, j, chip in pairs:
                copy(ici, a, j, (mx, my), mc, (*chip, mc)).wait_send()
                copy(d2d, a, j, chip, mc, (mx, my, 1 - mc)).wait_send()

        xb = x_ref[...].astype(BF16)
        xb_ref[...] = xb
        cos_t, sa_t, sb_t = cos_ref[...], sa_ref[...], sb_ref[...]

        def proj(r0, n):
            return _dot(xb, w_ref[r0:r0 + n, :], NT) + b_ref[:, r0:r0 + n]

        def rope(t):
            return _rope(t, cos_t, sa_t, sb_t, +1)

        qa_ref[...] = (rope(proj(O_QA, W_A)) * QK_SCALE).astype(BF16)
        ka_ref[...] = rope(proj(O_KA, W_KV_A)).astype(BF16)
        va_ref[...] = proj(O_VA, W_KV_A).astype(BF16)
        qc_ref[...] = (proj(O_QC, W_C) * QK_SCALE).astype(BF16)
        z_ref[...] = proj(O_Z, D_MIX).astype(BF16)
        parts = (rope(proj(O_QB, W_B)) * QK_SCALE, rope(proj(O_KB, W_B)), proj(O_VB, W_B))
        for k, part in enumerate(parts):
            bn_ref[:, 256 * k:256 * (k + 1)] = part.astype(BF16)
            scr[2 * k] = part[:, :128]
            scr[2 * k + 1] = part[:, 128:]
        for j in range(6):
            for res in range(4):
                b4_ref[0, res, :, 128 * j:128 * (j + 1)] = scr[j, pl.ds(res, tm // 4, stride=4), :].astype(BF16)
            for res in range(16):
                b16_ref[0, res, :, 128 * j:128 * (j + 1)] = scr[j, pl.ds(res, tm // 16, stride=16), :].astype(BF16)

    tok = lambda w: pl.BlockSpec((tm, w), lambda i: (i, 0))
    tab = pl.BlockSpec((tm, 128), lambda i: (i % spt, 0))
    hbm = pl.BlockSpec(memory_space=pl.ANY)
    return pl.pallas_call(
        body, name="in_proj", grid=(n_steps,),
        in_specs=[tok(D_MODEL), _full((D_IN, D_MODEL)), _full((1, D_IN)), tab, tab, tab, hbm, hbm],
        out_specs=(tok(D_MODEL), tok(W_A), tok(W_KV_A), tok(W_KV_A), tok(768),
                   pl.BlockSpec((1, 4, tm // 4, 768), lambda i: (i // spt, 0, i % spt, 0)),
                   pl.BlockSpec((1, 16, tm // 16, 768), lambda i: (i // spt, 0, i % spt, 0)),
                   tok(W_C), tok(D_MIX), hbm, hbm),
        out_shape=(_sds((T, D_MODEL), BF16), _sds((T, W_A), BF16), _sds((T, W_KV_A), BF16), _sds((T, W_KV_A), BF16),
                   _sds((T, 768), BF16), _sds((B_LOC, 4, SEQ // 4, 768), BF16), _sds((B_LOC, 16, SEQ // 16, 768), BF16),
                   _sds((T, W_C), BF16), _sds((T, D_MIX), BF16),
                   _sds((D_MIX, D_MODEL), BF16), _sds((D_MODEL, 2 * W_C), BF16)),
        input_output_aliases={6: 9, 7: 10},
        scratch_shapes=[pltpu.VMEM((6, tm, 128), F32)] + [pltpu.SemaphoreType.DMA((2, 3))] * 4,
        compiler_params=_cp(("arbitrary",), vmem_mb=48),
    )(*_pin(x, winT, b_in, cos, sa, sb, wout_own, wmem_own))


def _mem_kv(mem, wmem):
    def body(m_ref, w_ref, mb_ref, kv_ref):
        mb = m_ref[...].astype(BF16)
        mb_ref[...] = mb
        kv_ref[...] = _dot(mb, w_ref[...], NN).astype(BF16)

    n = B_LOC * MEM_LEN
    return pl.pallas_call(
        body, name="mem_kv",
        out_shape=(_sds((n, D_MODEL), BF16), _sds((n, 2 * W_C), BF16)),
    )(*_pin(mem, wmem))


class _Part:
    def __init__(self, body, args, in_specs, out_specs, out_shape, scratch=()):
        self.body, self.args, self.in_specs, self.out_specs, self.out_shape = body, args, in_specs, out_specs, out_shape
        self.scratch = list(scratch)


def _run_parts(name, parts, semantics, vmem_mb):
    n_in = [len(p.args) for p in parts]
    n_out = [len(p.out_shape) for p in parts]
    n_scr = [len(p.scratch) for p in parts]

    def body(*refs):
        ins, outs, scr = refs[:sum(n_in)], refs[sum(n_in):sum(n_in) + sum(n_out)], refs[sum(n_in) + sum(n_out):]
        i0 = o0 = s0 = 0
        for p, ni, no, ns in zip(parts, n_in, n_out, n_scr):
            p.body(*ins[i0:i0 + ni], *outs[o0:o0 + no], *scr[s0:s0 + ns])
            i0, o0, s0 = i0 + ni, o0 + no, s0 + ns

    res = pl.pallas_call(
        body, name=name, grid=(T // QR,),
        in_specs=[sp for p in parts for sp in p.in_specs], out_specs=tuple(sp for p in parts for sp in p.out_specs),
        out_shape=tuple(sh for p in parts for sh in p.out_shape),
        scratch_shapes=[sc for p in parts for sc in p.scratch],
        compiler_params=_cp((semantics,), vmem_mb=vmem_mb),
    )(*_pin(*[a for p in parts for a in p.args]))
    out, o0 = [], 0
    for no in n_out:
        out.append(tuple(res[o0:o0 + no]))
        o0 += no
    return out


QB = 8
QR = QB * BLK


def _lane_lo():
    return lax.broadcasted_iota(jnp.int32, (1, 128), 1) < 64


def _dup_head(k2, hk, lo):
    kf = k2.astype(F32)
    r = pltpu.roll(kf, 64, 1)
    return (jnp.where(lo, kf, r) if hk == 0 else jnp.where(lo, r, kf)).astype(BF16)


def _stack_heads(pairs, lo):
    parts = []
    for x2 in pairs:
        z = jnp.zeros_like(x2)
        parts += [jnp.where(lo, x2, z), jnp.where(lo, z, x2)]
    return jnp.concatenate(parts, axis=0)


def _prev_mode(kind, nb, j):
    if kind == "mem" or nb == 1:
        return "no"
    if nb <= QB:
        return "yes" if j % nb else "no"
    return "yes" if j else "dyn"


class _Attn:
    def __init__(self, kind, nb, max_dist, gqa, qw, kvw, qcb, kcb, vcb):
        self.kind, self.nb, self.gqa, self.qw, self.kvw = kind, nb, gqa, qw, kvw
        npairs = qw // 128
        self.groups = ([(hk, [2 * hk, 2 * hk + 1]) for hk in range(npairs // 2)] if gqa
                       else [(p, [p]) for p in range(npairs)])
        self.nh = 2 * len(self.groups[0][1])
        self.cols = 128 * self.nh
        self.reach = BLK - max_dist
        self.ext_prev = kind == "band" and nb > QB
        self.q_spec = pl.BlockSpec((QR, qw), lambda g: (g, qcb))
        self.row_spec = pl.BlockSpec((QR, qw), lambda g: (g, 0))
        self.stat_spec = pl.BlockSpec((QR, 128), lambda g: (g, 0))
        if kind == "mem":
            per = SEQ // QR
            self.kv_specs = [pl.BlockSpec((MEM_LEN, kvw), lambda g: (g // per, kcb)),
                             pl.BlockSpec((MEM_LEN, kvw), lambda g: (g // per, vcb))]
        else:
            self.kv_specs = [pl.BlockSpec((QR, kvw), lambda g: (g, kcb)), pl.BlockSpec((QR, kvw), lambda g: (g, vcb))]
            if self.ext_prev:
                self.kv_specs += [pl.BlockSpec((BLK, kvw), lambda g: (jnp.maximum(g * QB - 1, 0), kcb)),
                                  pl.BlockSpec((BLK, kvw), lambda g: (jnp.maximum(g * QB - 1, 0), vcb))]

    def masks(self):
        if self.kind == "mem":
            return None
        kj = lax.broadcasted_iota(jnp.int32, (2 * BLK, self.cols), 0)
        qi = lax.broadcasted_iota(jnp.int32, (2 * BLK, self.cols), 1) & (BLK - 1)
        kj1 = lax.broadcasted_iota(jnp.int32, (BLK, self.cols), 0)
        qi1 = lax.broadcasted_iota(jnp.int32, (BLK, self.cols), 1) & (BLK - 1)
        return kj, qi, kj1 <= qi1

    def keys(self, j, gi, kc_ref, vc_ref, kp_ref, vp_ref, lo, kq, g):
        def kv(k_ref, v_ref, r):
            if self.gqa:
                return _dup_head(k_ref[r, :], gi, lo), _dup_head(v_ref[r, :], gi, lo)
            sl = slice(128 * gi, 128 * (gi + 1))
            return k_ref[r, sl], v_ref[r, sl]

        if self.kind == "mem":
            key0 = pl.multiple_of((g // (SEQ // QR)) * MEM_LEN, MEM_LEN)
            return (*kv(kc_ref, vc_ref, slice(None)), None, [(0, MEM_LEN, key0)])
        kj, qi, cur = kq
        row0 = g * QR + BLK * j
        mode = _prev_mode(self.kind, self.nb, j)
        if mode == "no":
            return (*kv(kc_ref, vc_ref, slice(BLK * j, BLK * (j + 1))), cur, [(0, BLK, pl.multiple_of(row0, BLK))])
        if mode == "yes":
            mask = jnp.logical_and(kj >= qi + self.reach, kj <= qi + BLK)
            return (*kv(kc_ref, vc_ref, slice(BLK * (j - 1), BLK * (j + 1))), mask,
                    [(0, 2 * BLK, pl.multiple_of(row0 - BLK, BLK))])
        has_prev = ((g * QB) % self.nb) > 0
        hp = has_prev.astype(jnp.int32)
        mask = jnp.logical_and(kj >= qi * hp + (self.reach * hp + BLK * (1 - hp)), kj <= qi + BLK)
        kp, vp = kv(kp_ref, vp_ref, slice(None))
        kc, vc = kv(kc_ref, vc_ref, slice(0, BLK))
        return (jnp.concatenate([kp, kc], axis=0), jnp.concatenate([vp, vc], axis=0), mask,
                [(0, BLK, pl.multiple_of(jnp.maximum(row0 - BLK, 0), BLK)), (BLK, BLK, pl.multiple_of(row0, BLK))])


def _attn_fwd(q, qcb, qw, k, kcb, v, vcb, kvw, *, kind, nb=1, max_dist=BLK, gqa=False, sinks=None):
    a = _Attn(kind, nb, max_dist, gqa, qw, kvw, qcb, kcb, vcb)

    def body(*refs):
        it = iter(refs)
        q_ref, kc_ref, vc_ref = next(it), next(it), next(it)
        kp_ref, vp_ref = (next(it), next(it)) if a.ext_prev else (None, None)
        sink_ref = next(it) if sinks is not None else None
        o_ref, lse_ref = next(it), next(it)
        g = pl.program_id(0)
        lo = _lane_lo()
        top = lax.broadcasted_iota(jnp.int32, (128, 1), 0) < 64
        rid = lax.broadcasted_iota(jnp.int32, (8, 128), 0)
        kq = a.masks()
        stats = {}

        def scores(j, gi, pairs):
            rows = slice(BLK * j, BLK * (j + 1))
            qs = _stack_heads([q_ref[rows, 128 * p:128 * (p + 1)] for p in pairs], lo)
            kk, vv, mask, _ = a.keys(j, gi, kc_ref, vc_ref, kp_ref, vp_ref, lo, kq, g)
            pieces = [slice(r0, r0 + BLK) for r0 in range(0, kk.shape[0], BLK)]
            return dict(j=j, gi=gi, pairs=pairs, rows=rows, vv=vv, mask=mask, pieces=pieces,
                        ss=[_dot(kk[r], qs, NT) for r in pieces])

        def softmax(c):
            gi, mask = c["gi"], c["mask"]
            ss = [s if mask is None else jnp.where(mask[r], s, NEG) for r, s in zip(c["pieces"], c.pop("ss"))]
            m = jnp.max(ss[0], axis=0, keepdims=True)
            for s in ss[1:]:
                m = jnp.maximum(m, jnp.max(s, axis=0, keepdims=True))
            if sink_ref is not None:
                sk = jnp.concatenate([jnp.full((1, 128), sink_ref[0, a.nh * gi + i], F32) for i in range(a.nh)], axis=1)
                m = jnp.maximum(m, sk)
            ps = [jnp.exp(s - m) for s in ss]
            l = sum(jnp.sum(p, axis=0, keepdims=True) for p in ps)
            if sink_ref is not None:
                l = l + jnp.exp(sk - m)
            c["ps"] = [p.astype(BF16) for p in ps]
            c["l"], c["lse"] = l, m + jnp.log(l)

        def outputs(c):
            j, gi, rows = c["j"], c["gi"], c["rows"]
            ot = sum(_dot(c["vv"][r], p, TN) for r, p in zip(c["pieces"], c["ps"]))
            ot = ot * pl.reciprocal(c["l"], approx=True)
            for i, p in enumerate(c["pairs"]):
                o2t = jnp.where(top, ot[:, 256 * i:256 * i + 128], ot[:, 256 * i + 128:256 * i + 256])
                o_ref[rows, 128 * p:128 * (p + 1)] = o2t.T.astype(BF16)
            stat = stats.get(j, jnp.zeros((8, 128), F32))
            for i in range(a.nh):
                stat = jnp.where(rid == a.nh * gi + i, c["lse"][:, 128 * i:128 * (i + 1)], stat)
            stats[j] = stat
            if gi == a.groups[-1][0]:
                lse_ref[rows, :] = jnp.concatenate([stats.pop(j), jnp.zeros((120, 128), F32)], axis=0).T

        chains = [(j, gi, pairs) for j in range(QB) for gi, pairs in a.groups]
        live = {}
        for t in range(len(chains) + 2):
            if t < len(chains):
                live[t] = scores(*chains[t])
            if 0 <= t - 1 < len(chains):
                softmax(live[t - 1])
            if 0 <= t - 2 < len(chains):
                outputs(live.pop(t - 2))


    args = [q, k, v] + ([k, v] if a.ext_prev else [])
    in_specs = [a.q_spec] + a.kv_specs
    if sinks is not None:
        args.append(sinks)
        in_specs.append(pl.BlockSpec(memory_space=pltpu.SMEM))
    return _Part(body, args, in_specs, [a.row_spec, a.stat_spec], [_sds((T, qw), BF16), _sds((T, 128), F32)])


def _attn_bwd(q, qcb, qw, k, kcb, v, vcb, kvw, do, lse, dl, *, kind, nb=1, max_dist=BLK, gqa=False, sinkv=None):
    a = _Attn(kind, nb, max_dist, gqa, qw, kvw, qcb, kcb, vcb)

    def body(*refs):
        it = iter(refs)
        q_ref, kc_ref, vc_ref = next(it), next(it), next(it)
        kp_ref, vp_ref = (next(it), next(it)) if a.ext_prev else (None, None)
        do_ref, lse_ref, dl_ref = next(it), next(it), next(it)
        sinkv_ref = next(it) if sinkv is not None else None
        dq_ref = next(it)
        if kind == "mem":
            dkv_ref = next(it)
        else:
            dk_out, dv_out = next(it), next(it)
        dsink_ref = next(it) if sinkv is not None else None
        if kind != "mem":
            dk_ref, dv_ref, stage_k, stage_v, flush_sem = next(it), next(it), next(it), next(it), next(it)
        g = pl.program_id(0)
        lo = _lane_lo()
        top = lax.broadcasted_iota(jnp.int32, (128, 1), 0) < 64

        @pl.when(g == 0)
        def _():
            if kind == "mem":
                dkv_ref[...] = jnp.zeros_like(dkv_ref)
            else:
                dk_ref[...] = jnp.zeros_like(dk_ref)
                dv_ref[...] = jnp.zeros_like(dv_ref)
            if dsink_ref is not None:
                dsink_ref[...] = jnp.zeros_like(dsink_ref)

        kq = a.masks()
        stats_t = {}

        def first_matmuls(j, gi, pairs):
            rows = slice(BLK * j, BLK * (j + 1))
            if j not in stats_t:
                stats_t[j] = (lse_ref[rows, :].T, dl_ref[rows, :].T)
            lse_t, dl_t = stats_t[j]
            heads = [a.nh * gi + i for i in range(a.nh)]
            c = dict(rows=rows, gi=gi, pairs=pairs)
            c["qs"] = _stack_heads([q_ref[rows, 128 * p:128 * (p + 1)] for p in pairs], lo)
            c["dos"] = _stack_heads([do_ref[rows, 128 * p:128 * (p + 1)] for p in pairs], lo)
            c["lse_row"] = jnp.concatenate([lse_t[h:h + 1, :] for h in heads], axis=1)
            c["dl_row"] = jnp.concatenate([dl_t[h:h + 1, :] for h in heads], axis=1)
            c["kk"], vv, c["mask"], c["dests"] = a.keys(j, gi, kc_ref, vc_ref, kp_ref, vp_ref, lo, kq, g)
            c["s"] = _dot(c["kk"], c["qs"], NT)
            c["dp"] = _dot(vv, c["dos"], NT)
            return c

        def elementwise(c):
            s = c.pop("s")
            if c["mask"] is not None:
                s = jnp.where(c["mask"], s, NEG)
            p = jnp.exp(s - c["lse_row"])
            c["ds"] = (p * (c.pop("dp") - c["dl_row"])).astype(BF16)
            c["p"] = p.astype(BF16)

        def last_matmuls(c):
            gi, rows = c["gi"], c["rows"]
            dqt = _dot(c["kk"], c["ds"], TN)
            ck = _dot(c["ds"], c["qs"], NN)
            cv = _dot(c["p"], c["dos"], NN)
            if gqa:
                sel = lo if gi == 0 else jnp.logical_not(lo)
                ck = jnp.where(sel, ck + pltpu.roll(ck, 64, 1), 0.0)
                cv = jnp.where(sel, cv + pltpu.roll(cv, 64, 1), 0.0)
                kcols = slice(0, 128)
            else:
                kcols = slice(128 * gi, 128 * (gi + 1))
            for r0, nr, key0 in c["dests"]:
                krows = pl.ds(key0, nr)
                if kind == "mem":
                    dkv_ref[krows, kcols] += ck[r0:r0 + nr]
                    dkv_ref[krows, slice(kvw + kcols.start, kvw + kcols.stop)] += cv[r0:r0 + nr]
                else:
                    dk_ref[krows, kcols] += ck[r0:r0 + nr]
                    dv_ref[krows, kcols] += cv[r0:r0 + nr]
            for i, p in enumerate(c["pairs"]):
                dq2t = jnp.where(top, dqt[:, 256 * i:256 * i + 128], dqt[:, 256 * i + 128:256 * i + 256])
                dq_ref[rows, 128 * p:128 * (p + 1)] = dq2t.T.astype(BF16)

        chains = [(j, gi, pairs) for j in range(QB) for gi, pairs in a.groups]
        live = {}
        for t in range(len(chains) + 2):
            if t < len(chains):
                live[t] = first_matmuls(*chains[t])
            if 0 <= t - 1 < len(chains):
                elementwise(live[t - 1])
            if 0 <= t - 2 < len(chains):
                last_matmuls(live.pop(t - 2))
        if dsink_ref is not None:
            ps = jnp.exp(sinkv_ref[...] - lse_ref[...]) * dl_ref[...]
            dsink_ref[...] += jnp.sum(ps, axis=0, keepdims=True)
        if kind != "mem":
            n_steps = T // QR

            def flush(step):
                rows = pl.ds(pl.multiple_of(step * QR, QR), QR)
                out = []
                for acc, stage, dst, i in ((dk_ref, stage_k, dk_out, 0), (dv_ref, stage_v, dv_out, 1)):
                    stage[...] = acc[rows, :].astype(BF16)
                    out.append(pltpu.make_async_copy(stage, dst.at[rows, :], flush_sem.at[i]))
                return out

            def flushed(step):
                rows = pl.ds(pl.multiple_of(step * QR, QR), QR)
                return [pltpu.make_async_copy(stage, dst.at[rows, :], flush_sem.at[i])
                        for stage, dst, i in ((stage_k, dk_out, 0), (stage_v, dv_out, 1))]

            @pl.when(g >= 2)
            def _():
                for cp in flushed(g - 2):
                    cp.wait()

            @pl.when(g >= 1)
            def _():
                for cp in flush(g - 1):
                    cp.start()

            @pl.when(g == n_steps - 1)
            def _():
                for cp in flushed(g - 1):
                    cp.wait()
                for cp in flush(g):
                    cp.start()
                for cp in flushed(g):
                    cp.wait()

    args = [q, k, v] + ([k, v] if a.ext_prev else []) + [do, lse, dl]
    in_specs = [a.q_spec] + a.kv_specs + [a.row_spec, a.stat_spec, a.stat_spec]
    if sinkv is not None:
        args.append(sinkv)
        in_specs.append(_full((1, 128)))
    out_shape = [_sds((T, qw), BF16)]
    out_specs = [a.row_spec]
    scratch = []
    if kind == "mem":
        out_shape.append(_sds((B_LOC * MEM_LEN, 2 * kvw), F32))
        out_specs.append(pl.BlockSpec((B_LOC * MEM_LEN, 2 * kvw), lambda g: (0, 0), pipeline_mode=pl.Buffered(1)))
    else:
        out_shape += [_sds((T, kvw), BF16)] * 2
        out_specs += [pl.BlockSpec(memory_space=pl.ANY)] * 2
        scratch = [pltpu.VMEM((T, kvw), F32)] * 2 + [pltpu.VMEM((QR, kvw), BF16)] * 2 + [pltpu.SemaphoreType.DMA((2,))]
    if sinkv is not None:
        out_shape.append(_sds((1, 128), F32))
        out_specs.append(_full((1, 128)))
    return _Part(body, args, in_specs, out_specs, out_shape, scratch)


def _dot2(v, w_ref):
    hi = v.astype(BF16)
    lo = (v - hi.astype(F32)).astype(BF16)
    return _dot(hi, w_ref[...], NN) + _dot(lo, w_ref[...], NN)


def _middle(oa, o1, l1, o4, l4, o16, l16, oc, z, x, tgt, g_br, ln_g, ln_b, wout, spread4, gather4, gather8):
    tm = 512
    spt = SEQ // tm

    def body(oa_ref, o1_ref, l1_ref, o4_ref, l4_ref, o16_ref, l16_ref, oc_ref, z_ref, x_ref, t_ref,
             g_ref, lg_ref, lb_ref, w_ref, sp4_ref, ga4_ref, ga8_ref,
             du_ref, dz_ref, doa_ref, dla_ref,
             dobn_ref, lsen_ref, dlbn_ref, dob4_ref, lse4_ref, dlb4_ref, dob16_ref, lse16_ref, dlb16_ref,
             doc_ref, dlc_ref, acc_ref, gout_ref, scr):
        i = pl.program_id(0)

        @pl.when(i == 0)
        def _():
            acc_ref[...] = jnp.zeros_like(acc_ref)
            gout_ref[...] = jnp.zeros_like(gout_ref)

        for res in range(4):
            rows = pl.ds(res, tm // 4, stride=4)
            for j in range(2):
                scr[j, rows, :] = o4_ref[0, res, :, 128 * j:128 * (j + 1)].astype(F32)
            scr[2, rows, :] = l4_ref[0, res]
        for res in range(16):
            rows = pl.ds(res, tm // 16, stride=16)
            for j in range(2):
                scr[3 + j, rows, :] = o16_ref[0, res, :, 128 * j:128 * (j + 1)].astype(F32)
            scr[5, rows, :] = l16_ref[0, res]
        cat = lambda a: jnp.concatenate([scr[a], scr[a + 1]], axis=1)
        o1v, o4v, o16v = o1_ref[...].astype(F32), cat(0), cat(3)
        l1v, l4v, l16v = l1_ref[...], scr[2], scr[5]
        mx = jnp.maximum(jnp.maximum(l1v, l4v), l16v)
        e1, e4, e16 = jnp.exp(l1v - mx), jnp.exp(l4v - mx), jnp.exp(l16v - mx)
        ssum = e1 + e4 + e16
        lse_b = mx + jnp.log(ssum)
        inv = 1.0 / ssum
        ob = (_dot2(e1 * inv, sp4_ref) * o1v + _dot2(e4 * inv, sp4_ref) * o4v + _dot2(e16 * inv, sp4_ref) * o16v)
        oav, ocv = oa_ref[...].astype(F32), oc_ref[...].astype(F32)

        def rms(o):
            r = lax.rsqrt(jnp.sum(o * o, axis=1, keepdims=True) * (1.0 / o.shape[1]) + RMS_EPS)
            return o * r, r

        na, ra = rms(oav)
        nb_, rb = rms(ob)
        nc, rc = rms(ocv)
        n = jnp.concatenate([na, nb_, nc], axis=1)
        zf = z_ref[...].astype(F32)
        sig = 1.0 / (1.0 + jnp.exp(-zf))
        sz = zf * sig
        gb = g_ref[...]
        yb = (n * gb * sz).astype(BF16)
        u = ALPHA * x_ref[...] + _dot(yb, w_ref[...], NN)
        inv_d = 1.0 / D_MODEL
        mu = jnp.sum(u, axis=1, keepdims=True) * inv_d
        uc = u - mu
        rstd = lax.rsqrt(jnp.sum(uc * uc, axis=1, keepdims=True) * inv_d + LN_EPS)
        xh = uc * rstd
        lg = lg_ref[...]
        diff = xh * lg + lb_ref[...] - t_ref[...]
        acc_ref[0:1, :] += jnp.sum(diff * diff, axis=0, keepdims=True) * (0.5 * inv_d)
        dout = diff * inv_d
        acc_ref[2:3, :] += jnp.sum(dout * xh, axis=0, keepdims=True)
        acc_ref[3:4, :] += jnp.sum(dout, axis=0, keepdims=True)
        dxh = dout * lg
        du = rstd * (dxh - jnp.sum(dxh, axis=1, keepdims=True) * inv_d
                     - xh * (jnp.sum(dxh * xh, axis=1, keepdims=True) * inv_d))
        dub = du.astype(BF16)
        du_ref[...] = dub
        gout_ref[...] += _dot(yb, dub, TN)
        dy = _dot(dub, w_ref[...], NT)
        t1 = dy * sz
        acc_ref[1:2, :] += jnp.sum(t1 * n, axis=0, keepdims=True)
        dn = t1 * gb
        dz_ref[...] = (dy * n * gb * (sig * (1.0 + zf * (1.0 - sig)))).astype(BF16)

        def rms_bwd(dn_, n_, r):
            return r * (dn_ - n_ * (jnp.sum(dn_ * n_, axis=1, keepdims=True) * (1.0 / n_.shape[1])))

        doa = rms_bwd(dn[:, :W_A], na, ra)
        dob = rms_bwd(dn[:, W_A:W_A + W_B], nb_, rb)
        doc = rms_bwd(dn[:, W_A + W_B:], nc, rc)
        doa_ref[...] = doa.astype(BF16)
        dla_ref[...] = _dot2(doa * oav, ga8_ref)
        doc_ref[...] = doc.astype(BF16)
        dlc_ref[...] = _dot2(doc * ocv, ga4_ref)
        dlb = _dot2(dob * ob, ga4_ref)
        dobn_ref[...] = dob.astype(BF16)
        lsen_ref[...] = lse_b
        dlbn_ref[...] = dlb
        scr[0] = dob[:, :128]
        scr[1] = dob[:, 128:]
        scr[2] = lse_b
        scr[3] = dlb
        for res in range(4):
            rows = pl.ds(res, tm // 4, stride=4)
            for j in range(2):
                dob4_ref[0, res, :, 128 * j:128 * (j + 1)] = scr[j, rows, :].astype(BF16)
            lse4_ref[0, res] = scr[2, rows, :]
            dlb4_ref[0, res] = scr[3, rows, :]
        for res in range(16):
            rows = pl.ds(res, tm // 16, stride=16)
            for j in range(2):
                dob16_ref[0, res, :, 128 * j:128 * (j + 1)] = scr[j, rows, :].astype(BF16)
            lse16_ref[0, res] = scr[2, rows, :]
            dlb16_ref[0, res] = scr[3, rows, :]

    tok = lambda w: pl.BlockSpec((tm, w), lambda i: (i, 0))
    p4 = lambda w: pl.BlockSpec((1, 4, tm // 4, w), lambda i: (i // spt, 0, i % spt, 0))
    p16 = lambda w: pl.BlockSpec((1, 16, tm // 16, w), lambda i: (i // spt, 0, i % spt, 0))
    s4 = lambda w, dt: _sds((B_LOC, 4, SEQ // 4, w), dt)
    s16 = lambda w, dt: _sds((B_LOC, 16, SEQ // 16, w), dt)
    row = _full((1, D_MODEL))
    return pl.pallas_call(
        body, name="middle", grid=(T // tm,),
        in_specs=[tok(W_A), tok(W_B), tok(128), p4(W_B), p4(128), p16(W_B), p16(128), tok(W_C), tok(D_MIX),
                  tok(D_MODEL), tok(D_MODEL), row, row, row, _full((D_MIX, D_MODEL)),
                  _full((128, W_B)), _full((W_B, 128)), _full((W_A, 128))],
        out_specs=(tok(D_MODEL), tok(D_MIX), tok(W_A), tok(128),
                   tok(W_B), tok(128), tok(128), p4(W_B), p4(128), p4(128), p16(W_B), p16(128), p16(128),
                   tok(W_C), tok(128), _full((8, D_MODEL)), _full((D_MIX, D_MODEL))),
        out_shape=(_sds((T, D_MODEL), BF16), _sds((T, D_MIX), BF16),
                   _sds((T, W_A), BF16), _sds((T, 128), F32),
                   _sds((T, W_B), BF16), _sds((T, 128), F32), _sds((T, 128), F32),
                   s4(W_B, BF16), s4(128, F32), s4(128, F32), s16(W_B, BF16), s16(128, F32), s16(128, F32),
                   _sds((T, W_C), BF16), _sds((T, 128), F32), _sds((8, D_MODEL), F32),
                   _sds((D_MIX, D_MODEL), F32)),
        scratch_shapes=[pltpu.VMEM((6, tm, 128), F32)],
        compiler_params=_cp(("arbitrary",), vmem_mb=56),
    )(*_pin(oa, o1, l1, o4, l4, o16, l16, oc, z, x, tgt, g_br, ln_g, ln_b, wout, spread4, gather4, gather8))


class _ReduceScatter:
    def __init__(self, shapes):
        self.shapes = shapes

    def scratch_shapes(self):
        out = []
        for n, w in self.shapes:
            h = n // 2
            out += [pltpu.VMEM((4, h, w), F32), pltpu.VMEM((4, h, w), F32), pltpu.VMEM((3, h, w), BF16),
                    pltpu.VMEM((3, h, w), BF16), pltpu.VMEM((h, w), F32)]
        na = len(self.shapes)
        dma = pltpu.SemaphoreType.DMA
        return out + [dma((na, 4)), dma((na, 4)), dma((na, 4)), dma((na, 3)), dma((na, 3)), dma((na,)), dma((na,)),
                      dma((na,))]

    def bind(self, g_refs, r_refs, scratch):
        na = len(self.shapes)
        bufs = [scratch[5 * a:5 * a + 5] for a in range(na)]
        mine, sib, stage, land, tot = (tuple(b[i] for b in bufs) for i in range(5))
        loc_sem, s1_send, s1_recv, s2_send, s2_recv, s3_send, s3_recv, st_sem = scratch[5 * na:5 * na + 8]
        x, y, c = lax.axis_index("x"), lax.axis_index("y"), lax.axis_index("c")
        me, sibling = (x, y, c), (x, y, 1 - c)
        my_chip = 2 * x + y
        chips = [(1 - x, y), (x, 1 - y), (1 - x, 1 - y)]
        order = [2 * chip[0] + chip[1] for chip in chips] + [my_chip]

        def rows(a, k, half):
            n = self.shapes[a][0]
            return pl.ds(pl.multiple_of(k * n + half * (n // 2), 8), n // 2)

        def load(a, k):
            return pltpu.make_async_copy(g_refs[a].at[rows(a, k, c), :], mine[a].at[k], loc_sem.at[a, k])

        def s1(a, k, half):
            return pltpu.make_async_remote_copy(
                src_ref=g_refs[a].at[rows(a, k, half), :], dst_ref=sib[a].at[k],
                send_sem=s1_send.at[a, k], recv_sem=s1_recv.at[a, k], device_id=sibling, device_id_type=MESH)

        def s2(a, j, to):
            return pltpu.make_async_remote_copy(
                src_ref=stage[a].at[j], dst_ref=land[a].at[j], send_sem=s2_send.at[a, j], recv_sem=s2_recv.at[a, j],
                device_id=to, device_id_type=MESH)

        def s3(a, half, to):
            return pltpu.make_async_remote_copy(
                src_ref=tot[a], dst_ref=r_refs[a].at[rows(a, 0, half), :], send_sem=s3_send.at[a],
                recv_sem=s3_recv.at[a], device_id=to, device_id_type=MESH)

        def store(a):
            return pltpu.make_async_copy(tot[a], r_refs[a].at[rows(a, 0, c), :], st_sem.at[a])

        def start():
            for k in order:
                for a in range(na):
                    load(a, k).start()
                    s1(a, k, 1 - c).start()

        def exchange():
            for j, chip in enumerate(chips):
                k = order[j]
                for a in range(na):
                    load(a, k).wait()
                    s1(a, k, c).wait_recv()
                    stage[a][j] = (mine[a][k] + sib[a][k]).astype(BF16)
                    s2(a, j, (*chip, c)).start()
            for a in range(na):
                load(a, my_chip).wait()
                s1(a, my_chip, c).wait_recv()
                tot[a][...] = mine[a][my_chip] + sib[a][my_chip]

        def finish():
            for a in range(na):
                t = tot[a][...]
                for j in range(3):
                    s2(a, j, me).wait_recv()
                    t = t + land[a][j].astype(F32)
                tot[a][...] = t
                s3(a, c, sibling).start()
                store(a).start()

        def drain():
            for a in range(na):
                s3(a, 1 - c, me).wait_recv()
                store(a).wait()
            for a in range(na):
                for k in order:
                    s1(a, k, 1 - c).wait_send()
                for j, chip in enumerate(chips):
                    s2(a, j, (*chip, c)).wait_send()
                s3(a, c, sibling).wait_send()

        return start, exchange, finish, drain

    def part(self, grads, steps):
        def body(*refs):
            na = len(self.shapes)
            i = pl.program_id(0)
            for step, phase in zip(steps, self.bind(refs[:na], refs[na:2 * na], refs[2 * na:])):
                pl.when(i == step)(phase)

        hbm = pl.BlockSpec(memory_space=pl.ANY)
        return _Part(body, list(grads), [hbm] * len(grads), [hbm] * len(grads),
                     [_sds((n, w), F32) for n, w in self.shapes], self.scratch_shapes())


def _dh_dx(dqa, dka, dva, dqn, dkn, dvn, dq4, dk4, dv4, dq16, dk16, dv16, dqc, dz, du, xb, cos, sa, sb, winT):
    tm = 512
    spt = SEQ // tm

    def body(dqa_ref, dka_ref, dva_ref, dqn_ref, dkn_ref, dvn_ref, dq4_ref, dk4_ref, dv4_ref,
             dq16_ref, dk16_ref, dv16_ref, dqc_ref, dz_ref, du_ref, xb_ref, cos_ref, sa_ref, sb_ref, w_ref,
             gx_ref, db_ref, gin_ref, dh_ref, scr):
        i = pl.program_id(0)

        @pl.when(i == 0)
        def _():
            db_ref[...] = jnp.zeros_like(db_ref)
            gin_ref[...] = jnp.zeros_like(gin_ref)

        cos_t, sa_t, sb_t = cos_ref[...], sa_ref[...], sb_ref[...]

        def rope_t(t):
            return _rope(t, cos_t, sa_t, sb_t, -1)

        def put(r0, val):
            n = val.shape[1]
            dh_ref[:, r0:r0 + n] = val.astype(BF16)
            db_ref[:, r0:r0 + n] += jnp.sum(val, axis=0, keepdims=True)

        put(O_QA, rope_t(dqa_ref[...].astype(F32)) * QK_SCALE)
        put(O_KA, rope_t(dka_ref[...].astype(F32)))
        put(O_VA, dva_ref[...].astype(F32))
        put(O_QC, dqc_ref[...].astype(F32) * QK_SCALE)
        put(O_Z, dz_ref[...].astype(F32))
        for k, (n_ref, r4, r16) in enumerate(((dqn_ref, dq4_ref, dq16_ref), (dkn_ref, dk4_ref, dk16_ref),
                                               (dvn_ref, dv4_ref, dv16_ref))):
            for j in range(2):
                sl = slice(128 * j, 128 * (j + 1))
                scr[2 * k + j] = n_ref[:, sl].astype(F32)
                for res in range(4):
                    scr[2 * k + j, pl.ds(res, tm // 4, stride=4), :] += r4[0, res, :, sl].astype(F32)
                for res in range(16):
                    scr[2 * k + j, pl.ds(res, tm // 16, stride=16), :] += r16[0, res, :, sl].astype(F32)
        cat = lambda a: jnp.concatenate([scr[a], scr[a + 1]], axis=1)
        put(O_QB, rope_t(cat(0)) * QK_SCALE)
        put(O_KB, rope_t(cat(2)))
        put(O_VB, cat(4))
        gx_ref[...] = _dot(dh_ref[...], w_ref[...], NN) + ALPHA * du_ref[...].astype(F32)
        gin_ref[...] += _dot(dh_ref[...], xb_ref[...], TN)

    tok = lambda w: pl.BlockSpec((tm, w), lambda i: (i, 0))
    tab = pl.BlockSpec((tm, 128), lambda i: (i % spt, 0))
    p4 = pl.BlockSpec((1, 4, tm // 4, W_B), lambda i: (i // spt, 0, i % spt, 0))
    p16 = pl.BlockSpec((1, 16, tm // 16, W_B), lambda i: (i // spt, 0, i % spt, 0))
    once = lambda shape: pl.BlockSpec(shape, lambda i: (0, 0), pipeline_mode=pl.Buffered(1))
    return pl.pallas_call(
        body, name="dh_dx", grid=(T // tm,),
        in_specs=[tok(W_A), tok(W_KV_A), tok(W_KV_A), tok(W_B), tok(W_B), tok(W_B), p4, p4, p4, p16, p16, p16,
                  tok(W_C), tok(D_MIX), tok(D_MODEL), tok(D_MODEL), tab, tab, tab, once((D_IN, D_MODEL))],
        out_specs=(tok(D_MODEL), _full((1, D_IN)), once((D_IN, D_MODEL))),
        out_shape=(_sds((T, D_MODEL), F32), _sds((1, D_IN), F32), _sds((D_IN, D_MODEL), F32)),
        scratch_shapes=[pltpu.VMEM((tm, D_IN), BF16), pltpu.VMEM((6, tm, 128), F32)],
        compiler_params=_cp(("arbitrary",), vmem_mb=56),
    )(*_pin(dqa, dka, dva, dqn, dkn, dvn, dq4, dk4, dv4, dq16, dk16, dv16, dqc, dz, du, xb, cos, sa, sb, winT))


def _tn_matmul(name, a, b, bm, bt):
    n, m_all = a.shape
    n_cols = b.shape[1]

    def body(a_ref, b_ref, o_ref):
        @pl.when(pl.program_id(1) == 0)
        def _():
            o_ref[...] = jnp.zeros_like(o_ref)

        o_ref[...] += _dot(a_ref[...].astype(BF16), b_ref[...].astype(BF16), TN)

    return pl.pallas_call(
        body, name=name, grid=(m_all // bm, n // bt),
        in_specs=[pl.BlockSpec((bt, bm), lambda m, t: (t, m)), pl.BlockSpec((bt, n_cols), lambda m, t: (t, 0))],
        out_specs=pl.BlockSpec((bm, n_cols), lambda m, t: (m, 0)),
        out_shape=_sds((m_all, n_cols), F32),
        compiler_params=_cp(("parallel", "arbitrary"), vmem_mb=48),
    )(*_pin(a, b))


def _reduce_grads(g_in, acc, dbin, dsink):
    rs = _ReduceScatter([(SH_IN, D_MODEL)])

    def body(g_ref, acc_ref, dbin_ref, dsink_ref, r_ref, sv_ref, sv_mine, sv_all, sv_send, sv_recv, *rs_scratch):
        x, y, c = lax.axis_index("x"), lax.axis_index("y"), lax.axis_index("c")
        chips = [(1 - x, y), (x, 1 - y), (1 - x, 1 - y)]
        start, exchange, finish, drain = rs.bind((g_ref,), (r_ref,), rs_scratch)
        start()

        sv_mine[...] = jnp.zeros_like(sv_mine)
        sv_mine[0:4, 0:D_MODEL] = acc_ref[0:4, :]
        sv_mine[4:5, 0:D_IN] = dbin_ref[...]
        sv_mine[5:6, 0:128] = dsink_ref[...]
        my_dev = 4 * x + 2 * y + c
        others = [(x, y, 1 - c)] + [(*chip, cc) for chip in chips for cc in (c, 1 - c)]

        def sv_copy(j, to):
            return pltpu.make_async_remote_copy(
                src_ref=sv_mine, dst_ref=sv_all.at[my_dev], send_sem=sv_send.at[j], recv_sem=sv_recv.at[j],
                device_id=to, device_id_type=MESH)

        sv_sends = [sv_copy(j, to) for j, to in enumerate(others)]
        for cp in sv_sends:
            cp.start()
        exchange()
        finish()
        sv_all[my_dev] = sv_mine[...]
        for j in range(7):
            sv_copy(j, (x, y, c)).wait_recv()
        tot = sv_all[0]
        for d in range(1, 8):
            tot = tot + sv_all[d]
        sv_ref[...] = tot
        drain()
        for cp in sv_sends:
            cp.wait_send()

    vm = pl.BlockSpec(memory_space=pltpu.VMEM)
    hbm = pl.BlockSpec(memory_space=pl.ANY)
    return pl.pallas_call(
        body, name="reduce_grads",
        out_shape=(_sds((SH_IN, D_MODEL), F32), _vm_sds((8, SV_W), F32)),
        in_specs=[hbm, vm, vm, vm], out_specs=(hbm, vm),
        scratch_shapes=[pltpu.VMEM((8, SV_W), F32), pltpu.VMEM((8, 8, SV_W), F32),
                        pltpu.SemaphoreType.DMA((7,)), pltpu.SemaphoreType.DMA((7,))] + rs.scratch_shapes(),
        compiler_params=_cp(vmem_mb=40),
    )(pltpu.with_memory_space_constraint(g_in, pltpu.HBM), acc, dbin, dsink)


def _adamw(name, w, g, m, v, rows=None, copy_g=False):
    shape = w.shape
    rows = shape[0] if rows is None else rows
    n_out = 4 if copy_g else 3

    def body(w_ref, g_ref, m_ref, v_ref, d_ref, nm_ref, nv_ref, *go_ref):
        gv = g_ref[...]
        if copy_g:
            go_ref[0][...] = gv
        nm = ADAM_B1 * m_ref[...] + (1.0 - ADAM_B1) * gv
        nv = ADAM_B2 * v_ref[...] + (1.0 - ADAM_B2) * (gv * gv)
        m_hat = nm / (1.0 - ADAM_B1 ** ADAM_STEP)
        v_hat = nv / (1.0 - ADAM_B2 ** ADAM_STEP)
        d_ref[...] = -ADAM_LR * (m_hat / (jnp.sqrt(v_hat) + ADAM_EPS) + ADAM_WD * w_ref[...])
        nm_ref[...] = nm
        nv_ref[...] = nv

    spec = pl.BlockSpec((rows, shape[1]), lambda i: (i, 0))
    return pl.pallas_call(
        body, name=name, grid=(shape[0] // rows,), in_specs=[spec] * 4, out_specs=(spec,) * n_out,
        out_shape=(_sds(shape, F32),) * n_out, compiler_params=_cp(("parallel",)),
    )(*_pin(w, g, m, v))


def _rope_tables():
    pos = jnp.arange(SEQ, dtype=F32)
    inv = ROPE_THETA ** (-jnp.arange(0, 64, 2, dtype=F32) / 64)
    ang = pos[:, None] * inv[None, :]
    cos, sin = lax.optimization_barrier((jnp.cos(ang), jnp.sin(ang)))
    cos, sin = jnp.tile(cos, (1, 4)), jnp.tile(sin, (1, 4))
    low = (jnp.arange(128) % 64) < 32
    return cos, jnp.where(low, -sin, 0.0), jnp.where(low, 0.0, sin)


def _local_step(x2, mem2, tgt2, winT, wout, wmem, b_in, sinks, g_branch, ln_gain, ln_bias):
    cos, sa, sb = _rope_tables()
    sinkv = jnp.pad(sinks, ((0, 0), (0, 120)))
    head_of_lane = jnp.arange(512)[None, :] // 64
    gather8 = (head_of_lane.T == jnp.arange(128)[None, :]).astype(BF16)
    gather4 = gather8[:W_B]
    spread4 = gather4.T

    xb, qa, ka, va, bn, b4, b16, qc, z, wout, wmem = _in_proj(x2, winT, b_in, cos, sa, sb, wout, wmem)
    memb, mkv = _mem_kv(mem2, wmem)
    b4f, b16f = b4.reshape(T, 768), b16.reshape(T, 768)

    swa = dict(kind="band", nb=SEQ // BLK, max_dist=BLK - 1, gqa=True)
    dil = (dict(kind="band", nb=SEQ // BLK), dict(kind="band", nb=SEQ // 4 // BLK), dict(kind="band", nb=1))
    (oa, lse_a), (o1, l1), (o4, l4), (o16, l16), (oc, lse_c) = _run_parts("attn_fwd", [
        _attn_fwd(qa, 0, W_A, ka, 0, va, 0, W_KV_A, sinks=sinks, **swa),
        _attn_fwd(bn, 0, W_B, bn, 1, bn, 2, W_B, **dil[0]),
        _attn_fwd(b4f, 0, W_B, b4f, 1, b4f, 2, W_B, **dil[1]),
        _attn_fwd(b16f, 0, W_B, b16f, 1, b16f, 2, W_B, **dil[2]),
        _attn_fwd(qc, 0, W_C, mkv, 0, mkv, 1, W_C, kind="mem")], "parallel", 48)

    s4 = lambda w: (B_LOC, 4, SEQ // 4, w)
    s16 = lambda w: (B_LOC, 16, SEQ // 16, w)
    (du, dz, doa, dla, dobn, lsen, dlbn, dob4, lse4, dlb4, dob16, lse16, dlb16, doc, dlc, acc, g_out) = _middle(
        oa, o1, l1, o4.reshape(s4(W_B)), l4.reshape(s4(128)), o16.reshape(s16(W_B)), l16.reshape(s16(128)), oc, z,
        x2, tgt2, g_branch, ln_gain, ln_bias, wout, spread4, gather4, gather8)

    flat = lambda a: a.reshape(T, a.shape[-1])
    (dqa, dka, dva, dsink), (dqc, dmkv) = _run_parts("attn_bwd_a", [
        _attn_bwd(qa, 0, W_A, ka, 0, va, 0, W_KV_A, doa, lse_a, dla, sinkv=sinkv, **swa),
        _attn_bwd(qc, 0, W_C, mkv, 0, mkv, 1, W_C, doc, lse_c, dlc, kind="mem")], "arbitrary", 48)
    g_mem = _tn_matmul("dw_mem", memb, dmkv, D_MODEL, B_LOC * MEM_LEN)
    last = T // QR - 1
    (r_out, r_mem), (dqn, dkn, dvn), (dq4, dk4, dv4), (dq16, dk16, dv16) = _run_parts("attn_bwd_b", [
        _ReduceScatter([(SH_OUT, D_MODEL), (SH_MEM, 2 * W_C)]).part((g_out, g_mem), (0, 1, last, last)),
        _attn_bwd(bn, 0, W_B, bn, 1, bn, 2, W_B, dobn, lsen, dlbn, **dil[0]),
        _attn_bwd(b4f, 0, W_B, b4f, 1, b4f, 2, W_B, flat(dob4), flat(lse4), flat(dlb4), **dil[1]),
        _attn_bwd(b16f, 0, W_B, b16f, 1, b16f, 2, W_B, flat(dob16), flat(lse16), flat(dlb16), **dil[2])],
        "arbitrary", 60)

    r4 = lambda a: a.reshape(s4(W_B))
    r16 = lambda a: a.reshape(s16(W_B))
    gx, dbin, g_in = _dh_dx(dqa, dka, dva, dqn, dkn, dvn, r4(dq4), r4(dk4), r4(dv4), r16(dq16), r16(dk16),
                            r16(dv16), dqc, dz, du, xb, cos, sa, sb, winT)
    return gx, g_in, r_out, r_mem, acc, dbin, dsink


def kernel(x, mem, w_in, b_in, w_mem, attn_sinks, g_branch, w_out, ln_gain, ln_bias, loss_target, m_w_in, m_b_in, m_w_mem, m_attn_sinks, m_g_branch, m_w_out, m_ln_gain, m_ln_bias, v_w_in, v_b_in, v_w_mem, v_attn_sinks, v_g_branch, v_w_out, v_ln_gain, v_ln_bias):
    winT, wout, wmem = _gather_weights(w_in[0].T, w_out[0], w_mem[0])
    gx, g_in, r_out, r_mem, acc, dbin, dsink = _local_step(
        x.reshape(T, D_MODEL), mem.reshape(B_LOC * MEM_LEN, D_MODEL), loss_target.reshape(T, D_MODEL),
        winT, wout, wmem, b_in, attn_sinks, g_branch, ln_gain, ln_bias)
    r_in, sv = _reduce_grads(g_in, acc, dbin, dsink)

    loss = jnp.sum(sv[0, :D_MODEL])
    grads = {
        "b_in": sv[4:5, :D_IN], "w_mem": r_mem[None],
        "attn_sinks": -sv[5:6, 0:8], "g_branch": sv[1:2, :D_MODEL], "w_out": r_out[None],
        "ln_gain": sv[2:3, :D_MODEL], "ln_bias": sv[3:4, :D_MODEL],
    }
    weights = dict(w_in=w_in, b_in=b_in, w_mem=w_mem, attn_sinks=attn_sinks, g_branch=g_branch, w_out=w_out,
                   ln_gain=ln_gain, ln_bias=ln_bias)
    ms = dict(w_in=m_w_in, b_in=m_b_in, w_mem=m_w_mem, attn_sinks=m_attn_sinks, g_branch=m_g_branch, w_out=m_w_out,
              ln_gain=m_ln_gain, ln_bias=m_ln_bias)
    vs = dict(w_in=v_w_in, b_in=v_b_in, w_mem=v_w_mem, attn_sinks=v_attn_sinks, g_branch=v_g_branch, w_out=v_w_out,
              ln_gain=v_ln_gain, ln_bias=v_ln_bias)
    names = ["w_in", "b_in", "w_mem", "attn_sinks", "g_branch", "w_out", "ln_gain", "ln_bias"]
    deltas, new_m, new_v = [], [], []
    for n in names:
        shape = weights[n].shape
        two_d = lambda a: a.reshape(shape[-2], shape[-1])
        if n == "w_in":
            d, nm, nv, gw = (a.T for a in _adamw("adamw_w_in", w_in[0].T, r_in, m_w_in[0].T, v_w_in[0].T, SH_IN // 4,
                                                 copy_g=True))
            grads[n] = gw
        elif n in ("w_out", "w_mem"):
            d, nm, nv, grads[n] = _adamw("adamw_" + n, two_d(weights[n]), two_d(grads[n]), two_d(ms[n]), two_d(vs[n]),
                                         copy_g=True)
        else:
            d, nm, nv = _adamw("adamw_" + n, two_d(weights[n]), two_d(grads[n]), two_d(ms[n]), two_d(vs[n]))
        deltas.append(d.reshape(shape))
        new_m.append(nm.reshape(shape))
        new_v.append(nv.reshape(shape))
    return (loss, gx.reshape(B_LOC, SEQ, D_MODEL), *[grads[n].reshape(weights[n].shape) for n in names],
            *deltas, *new_m, *new_v)
```

```python
import functools

import jax
import jax.numpy as jnp
from jax import lax
from jax.experimental import pallas as pl
from jax.experimental.pallas import tpu as pltpu

F32, BF16 = jnp.float32, jnp.bfloat16

D_MODEL = 1024
SEQ = 2048
B_LOC = 2
T = B_LOC * SEQ
BLK = 128
MEM_LEN = 256
W_A, W_KV_A, W_B, W_C, D_MIX = 512, 128, 256, 256, 1024
D_IN = 2816
O_QA, O_KA, O_VA, O_QB, O_KB, O_VB, O_QC, O_Z = 0, 512, 640, 768, 1024, 1280, 1536, 1792
ROPE_THETA = 10000.0
LN_EPS = 1e-5
RMS_EPS = 1e-6
ALPHA = 2.0 ** 0.25
QK_SCALE = 0.125
N_CHIP = 4
SH_IN, SH_OUT, SH_MEM = D_IN // N_CHIP, D_MIX // N_CHIP, D_MODEL // N_CHIP
NEG = -1e30
ADAM_LR, ADAM_B1, ADAM_B2, ADAM_EPS, ADAM_WD, ADAM_STEP = 0.001, 0.9, 0.999, 1e-08, 0.01, 10
SV_W = 3072
MESH = pl.DeviceIdType.MESH

NN = ((1,), (0,))
NT = ((1,), (1,))
TN = ((0,), (0,))


def _dot(a, b, dims):
    return lax.dot_general(a, b, (dims, ((), ())), preferred_element_type=F32)


def _cp(sem=None, vmem_mb=None):
    kw = {}
    if sem is not None:
        kw["dimension_semantics"] = sem
    if vmem_mb is not None:
        kw["vmem_limit_bytes"] = vmem_mb * 1024 * 1024
    return pltpu.CompilerParams(**kw)


def _sds(shape, dtype):
    return pltpu.HBM(shape, dtype)


def _vm_sds(shape, dtype):
    return jax.ShapeDtypeStruct(shape, dtype)


def _pin(*args):
    return [pltpu.with_memory_space_constraint(a, pltpu.HBM) for a in args]


def _full(shape):
    n = len(shape)
    return pl.BlockSpec(shape, lambda *_: (0,) * n)


def _shard_rows(ref, n, chip, half):
    start = pl.multiple_of((2 * chip[0] + chip[1]) * n + half * (n // 2), 16)
    return ref.at[pl.ds(start, n // 2), :]


def _gather_weights(win_sh, wout_sh, wmem_sh):
    def body(a_ref, b_ref, c_ref, oa_ref, ob_ref, oc_ref, ici_send, ici_recv, d2d_send, d2d_recv):
        x, y, c = lax.axis_index("x"), lax.axis_index("y"), lax.axis_index("c")
        sibling = (x, y, 1 - c)
        chips = [(1 - x, y), (x, 1 - y), (1 - x, 1 - y)]
        for src, out, n in ((a_ref, oa_ref, SH_IN), (b_ref, ob_ref, SH_OUT), (c_ref, oc_ref, SH_MEM)):
            out[pl.ds(pl.multiple_of((2 * x + y) * n, 16), n), :] = src[...].astype(BF16)

        def copy(sems, j, chip_of_block, half, to):
            blk = _shard_rows(oa_ref, SH_IN, chip_of_block, half)
            return pltpu.make_async_remote_copy(
                src_ref=blk, dst_ref=blk, send_sem=sems[0].at[j], recv_sem=sems[1].at[j],
                device_id=to, device_id_type=MESH)

        ici, d2d = (ici_send, ici_recv), (d2d_send, d2d_recv)
        first = [copy(ici, j, (x, y), c, (*chip, c)) for j, chip in enumerate(chips)]
        for cp in first:
            cp.start()
        passed = []
        for j, chip in enumerate(chips):
            copy(ici, j, chip, c, (x, y, c)).wait_recv()
            fw = copy(d2d, j, chip, c, sibling)
            fw.start()
            passed.append(fw)
        for j, chip in enumerate(chips):
            copy(d2d, j, chip, 1 - c, (x, y, c)).wait_recv()
        for cp in first + passed:
            cp.wait_send()

    vm = pl.BlockSpec(memory_space=pltpu.VMEM)
    return pl.pallas_call(
        body, name="gather_weights",
        out_shape=(_vm_sds((D_IN, D_MODEL), BF16), _vm_sds((D_MIX, D_MODEL), BF16),
                   _vm_sds((D_MODEL, 2 * W_C), BF16)),
        in_specs=[vm, vm, vm], out_specs=(vm, vm, vm),
        scratch_shapes=[pltpu.SemaphoreType.DMA((3,))] * 4,
        compiler_params=_cp(vmem_mb=40),
    )(win_sh, wout_sh, wmem_sh)


def _rope(t, cos, sa, sb, sign):
    w = t.shape[1]
    reps = w // 128
    c, a, b = (jnp.tile(v, (1, reps)) if reps > 1 else v for v in (cos, sa, sb))
    rot = pltpu.roll(t, w - 32, 1) * a + pltpu.roll(t, 32, 1) * b
    return t * c + rot if sign > 0 else t * c - rot


def _in_proj(x, winT, b_in, cos, sa, sb, wout_own, wmem_own):
    tm = 512
    spt = SEQ // tm
    n_steps = T // tm
    forward_step = n_steps // 2

    def body(x_ref, w_ref, b_ref, cos_ref, sa_ref, sb_ref, wo_in, wm_in,
             xb_ref, qa_ref, ka_ref, va_ref, bn_ref, b4_ref, b16_ref, qc_ref, z_ref, wo_ref, wm_ref,
             scr, ici_send, ici_recv, d2d_send, d2d_recv):
        i = pl.program_id(0)
        mx, my, mc = lax.axis_index("x"), lax.axis_index("y"), lax.axis_index("c")
        chips = [(1 - mx, my), (mx, 1 - my), (1 - mx, 1 - my)]
        full = ((wo_ref, SH_OUT), (wm_ref, SH_MEM))

        def copy(sems, a, j, chip_of_block, half, to):
            blk = _shard_rows(full[a][0], full[a][1], chip_of_block, half)
            return pltpu.make_async_remote_copy(
                src_ref=blk, dst_ref=blk, send_sem=sems[0].at[a, j], recv_sem=sems[1].at[a, j],
                device_id=to, device_id_type=MESH)

        ici, d2d = (ici_send, ici_recv), (d2d_send, d2d_recv)
        pairs = [(a, j, chip) for j, chip in enumerate(chips) for a in range(2)]

        @pl.when(i == 0)
        def _():
            for a, j, chip in pairs:
                copy(ici, a, j, (mx, my), mc, (*chip, mc)).start()

        @pl.when(i == forward_step)
        def _():
            for a, j, chip in pairs:
                copy(ici, a, j, chip, mc, (mx, my, mc)).wait_recv()
                copy(d2d, a, j, chip, mc, (mx, my, 1 - mc)).start()

        @pl.when(i == n_steps - 1)
        def _():
            for a, j, chip in pairs:
                copy(d2d, a, j, chip, 1 - mc, (mx, my, mc)).wait_recv()
            for a, j, chip in pairs:
                copy(ici, a, j, (mx, my), mc, (*chip, mc)).wait_send()
                copy(d2d, a, j, chip, mc, (mx, my, 1 - mc)).wait_send()

        xb = x_ref[...].astype(BF16)
        xb_ref[...] = xb
        cos_t, sa_t, sb_t = cos_ref[...], sa_ref[...], sb_ref[...]

        def proj(r0, n):
            return _dot(xb, w_ref[r0:r0 + n, :], NT) + b_ref[:, r0:r0 + n]

        def rope(t):
            return _rope(t, cos_t, sa_t, sb_t, +1)

        qa_ref[...] = (rope(proj(O_QA, W_A)) * QK_SCALE).astype(BF16)
        ka_ref[...] = rope(proj(O_KA, W_KV_A)).astype(BF16)
        va_ref[...] = proj(O_VA, W_KV_A).astype(BF16)
        qc_ref[...] = (proj(O_QC, W_C) * QK_SCALE).astype(BF16)
        z_ref[...] = proj(O_Z, D_MIX).astype(BF16)
        parts = (rope(proj(O_QB, W_B)) * QK_SCALE, rope(proj(O_KB, W_B)), proj(O_VB, W_B))
        for k, part in enumerate(parts):
            bn_ref[:, 256 * k:256 * (k + 1)] = part.astype(BF16)
            scr[2 * k] = part[:, :128]
            scr[2 * k + 1] = part[:, 128:]
        for j in range(6):
            for res in range(4):
                b4_ref[0, res, :, 128 * j:128 * (j + 1)] = scr[j, pl.ds(res, tm // 4, stride=4), :].astype(BF16)
            for res in range(16):
                b16_ref[0, res, :, 128 * j:128 * (j + 1)] = scr[j, pl.ds(res, tm // 16, stride=16), :].astype(BF16)

    tok = lambda w: pl.BlockSpec((tm, w), lambda i: (i, 0))
    tab = pl.BlockSpec((tm, 128), lambda i: (i % spt, 0))
    hbm = pl.BlockSpec(memory_space=pl.ANY)
    return pl.pallas_call(
        body, name="in_proj", grid=(n_steps,),
        in_specs=[tok(D_MODEL), _full((D_IN, D_MODEL)), _full((1, D_IN)), tab, tab, tab, hbm, hbm],
        out_specs=(tok(D_MODEL), tok(W_A), tok(W_KV_A), tok(W_KV_A), tok(768),
                   pl.BlockSpec((1, 4, tm // 4, 768), lambda i: (i // spt, 0, i % spt, 0)),
                   pl.BlockSpec((1, 16, tm // 16, 768), lambda i: (i // spt, 0, i % spt, 0)),
                   tok(W_C), tok(D_MIX), hbm, hbm),
        out_shape=(_sds((T, D_MODEL), BF16), _sds((T, W_A), BF16), _sds((T, W_KV_A), BF16), _sds((T, W_KV_A), BF16),
                   _sds((T, 768), BF16), _sds((B_LOC, 4, SEQ // 4, 768), BF16), _sds((B_LOC, 16, SEQ // 16, 768), BF16),
                   _sds((T, W_C), BF16), _sds((T, D_MIX), BF16),
                   _sds((D_MIX, D_MODEL), BF16), _sds((D_MODEL, 2 * W_C), BF16)),
        input_output_aliases={6: 9, 7: 10},
        scratch_shapes=[pltpu.VMEM((6, tm, 128), F32)] + [pltpu.SemaphoreType.DMA((2, 3))] * 4,
        compiler_params=_cp(("arbitrary",), vmem_mb=48),
    )(*_pin(x, winT, b_in, cos, sa, sb, wout_own, wmem_own))


def _mem_kv(mem, wmem):
    def body(m_ref, w_ref, mb_ref, kv_ref):
        mb = m_ref[...].astype(BF16)
        mb_ref[...] = mb
        kv_ref[...] = _dot(mb, w_ref[...], NN).astype(BF16)

    n = B_LOC * MEM_LEN
    return pl.pallas_call(
        body, name="mem_kv",
        out_shape=(_sds((n, D_MODEL), BF16), _sds((n, 2 * W_C), BF16)),
    )(*_pin(mem, wmem))


class _Part:
    def __init__(self, body, args, in_specs, out_specs, out_shape, scratch=()):
        self.body, self.args, self.in_specs, self.out_specs, self.out_shape = body, args, in_specs, out_specs, out_shape
        self.scratch = list(scratch)


def _run_parts(name, parts, semantics, vmem_mb):
    n_in = [len(p.args) for p in parts]
    n_out = [len(p.out_shape) for p in parts]
    n_scr = [len(p.scratch) for p in parts]

    def body(*refs):
        ins, outs, scr = refs[:sum(n_in)], refs[sum(n_in):sum(n_in) + sum(n_out)], refs[sum(n_in) + sum(n_out):]
        i0 = o0 = s0 = 0
        for p, ni, no, ns in zip(parts, n_in, n_out, n_scr):
            p.body(*ins[i0:i0 + ni], *outs[o0:o0 + no], *scr[s0:s0 + ns])
            i0, o0, s0 = i0 + ni, o0 + no, s0 + ns

    res = pl.pallas_call(
        body, name=name, grid=(T // QR,),
        in_specs=[sp for p in parts for sp in p.in_specs], out_specs=tuple(sp for p in parts for sp in p.out_specs),
        out_shape=tuple(sh for p in parts for sh in p.out_shape),
        scratch_shapes=[sc for p in parts for sc in p.scratch],
        compiler_params=_cp((semantics,), vmem_mb=vmem_mb),
    )(*_pin(*[a for p in parts for a in p.args]))
    out, o0 = [], 0
    for no in n_out:
        out.append(tuple(res[o0:o0 + no]))
        o0 += no
    return out


QB = 8
QR = QB * BLK


def _lane_lo():
    return lax.broadcasted_iota(jnp.int32, (1, 128), 1) < 64


def _dup_head(k2, hk, lo):
    kf = k2.astype(F32)
    r = pltpu.roll(kf, 64, 1)
    return (jnp.where(lo, kf, r) if hk == 0 else jnp.where(lo, r, kf)).astype(BF16)


def _stack_heads(pairs, lo):
    parts = []
    for x2 in pairs:
        z = jnp.zeros_like(x2)
        parts += [jnp.where(lo, x2, z), jnp.where(lo, z, x2)]
    return jnp.concatenate(parts, axis=0)


def _prev_mode(kind, nb, j):
    if kind == "mem" or nb == 1:
        return "no"
    if nb <= QB:
        return "yes" if j % nb else "no"
    return "yes" if j else "dyn"


class _Attn:
    def __init__(self, kind, nb, max_dist, gqa, qw, kvw, qcb, kcb, vcb):
        self.kind, self.nb, self.gqa, self.qw, self.kvw = kind, nb, gqa, qw, kvw
        npairs = qw // 128
        self.groups = ([(hk, [2 * hk, 2 * hk + 1]) for hk in range(npairs // 2)] if gqa
                       else [(p, [p]) for p in range(npairs)])
        self.nh = 2 * len(self.groups[0][1])
        self.cols = 128 * self.nh
        self.reach = BLK - max_dist
        self.ext_prev = kind == "band" and nb > QB
        self.q_spec = pl.BlockSpec((QR, qw), lambda g: (g, qcb))
        self.row_spec = pl.BlockSpec((QR, qw), lambda g: (g, 0))
        self.stat_spec = pl.BlockSpec((QR, 128), lambda g: (g, 0))
        if kind == "mem":
            per = SEQ // QR
            self.kv_specs = [pl.BlockSpec((MEM_LEN, kvw), lambda g: (g // per, kcb)),
                             pl.BlockSpec((MEM_LEN, kvw), lambda g: (g // per, vcb))]
        else:
            self.kv_specs = [pl.BlockSpec((QR, kvw), lambda g: (g, kcb)), pl.BlockSpec((QR, kvw), lambda g: (g, vcb))]
            if self.ext_prev:
                self.kv_specs += [pl.BlockSpec((BLK, kvw), lambda g: (jnp.maximum(g * QB - 1, 0), kcb)),
                                  pl.BlockSpec((BLK, kvw), lambda g: (jnp.maximum(g * QB - 1, 0), vcb))]

    def masks(self):
        if self.kind == "mem":
            return None
        kj = lax.broadcasted_iota(jnp.int32, (2 * BLK, self.cols), 0)
        qi = lax.broadcasted_iota(jnp.int32, (2 * BLK, self.cols), 1) & (BLK - 1)
        kj1 = lax.broadcasted_iota(jnp.int32, (BLK, self.cols), 0)
        qi1 = lax.broadcasted_iota(jnp.int32, (BLK, self.cols), 1) & (BLK - 1)
        return kj, qi, kj1 <= qi1

    def keys(self, j, gi, kc_ref, vc_ref, kp_ref, vp_ref, lo, kq, g):
        def kv(k_ref, v_ref, r):
            if self.gqa:
                return _dup_head(k_ref[r, :], gi, lo), _dup_head(v_ref[r, :], gi, lo)
            sl = slice(128 * gi, 128 * (gi + 1))
            return k_ref[r, sl], v_ref[r, sl]

        if self.kind == "mem":
            key0 = pl.multiple_of((g // (SEQ // QR)) * MEM_LEN, MEM_LEN)
            return (*kv(kc_ref, vc_ref, slice(None)), None, [(0, MEM_LEN, key0)])
        kj, qi, cur = kq
        row0 = g * QR + BLK * j
        mode = _prev_mode(self.kind, self.nb, j)
        if mode == "no":
            return (*kv(kc_ref, vc_ref, slice(BLK * j, BLK * (j + 1))), cur, [(0, BLK, pl.multiple_of(row0, BLK))])
        if mode == "yes":
            mask = jnp.logical_and(kj >= qi + self.reach, kj <= qi + BLK)
            return (*kv(kc_ref, vc_ref, slice(BLK * (j - 1), BLK * (j + 1))), mask,
                    [(0, 2 * BLK, pl.multiple_of(row0 - BLK, BLK))])
        has_prev = ((g * QB) % self.nb) > 0
        hp = has_prev.astype(jnp.int32)
        mask = jnp.logical_and(kj >= qi * hp + (self.reach * hp + BLK * (1 - hp)), kj <= qi + BLK)
        kp, vp = kv(kp_ref, vp_ref, slice(None))
        kc, vc = kv(kc_ref, vc_ref, slice(0, BLK))
        return (jnp.concatenate([kp, kc], axis=0), jnp.concatenate([vp, vc], axis=0), mask,
                [(0, BLK, pl.multiple_of(jnp.maximum(row0 - BLK, 0), BLK)), (BLK, BLK, pl.multiple_of(row0, BLK))])


def _attn_fwd(q, qcb, qw, k, kcb, v, vcb, kvw, *, kind, nb=1, max_dist=BLK, gqa=False, sinks=None):
    a = _Attn(kind, nb, max_dist, gqa, qw, kvw, qcb, kcb, vcb)

    def body(*refs):
        it = iter(refs)
        q_ref, kc_ref, vc_ref = next(it), next(it), next(it)
        kp_ref, vp_ref = (next(it), next(it)) if a.ext_prev else (None, None)
        sink_ref = next(it) if sinks is not None else None
        o_ref, lse_ref = next(it), next(it)
        g = pl.program_id(0)
        lo = _lane_lo()
        top = lax.broadcasted_iota(jnp.int32, (128, 1), 0) < 64
        rid = lax.broadcasted_iota(jnp.int32, (8, 128), 0)
        kq = a.masks()
        stats = {}

        def scores(j, gi, pairs):
            rows = slice(BLK * j, BLK * (j + 1))
            qs = _stack_heads([q_ref[rows, 128 * p:128 * (p + 1)] for p in pairs], lo)
            kk, vv, mask, _ = a.keys(j, gi, kc_ref, vc_ref, kp_ref, vp_ref, lo, kq, g)
            pieces = [slice(r0, r0 + BLK) for r0 in range(0, kk.shape[0], BLK)]
            return dict(j=j, gi=gi, pairs=pairs, rows=rows, vv=vv, mask=mask, pieces=pieces,
                        ss=[_dot(kk[r], qs, NT) for r in pieces])

        def softmax(c):
            gi, mask = c["gi"], c["mask"]
            ss = [s if mask is None else jnp.where(mask[r], s, NEG) for r, s in zip(c["pieces"], c.pop("ss"))]
            m = jnp.max(ss[0], axis=0, keepdims=True)
            for s in ss[1:]:
                m = jnp.maximum(m, jnp.max(s, axis=0, keepdims=True))
            if sink_ref is not None:
                sk = jnp.concatenate([jnp.full((1, 128), sink_ref[0, a.nh * gi + i], F32) for i in range(a.nh)], axis=1)
                m = jnp.maximum(m, sk)
            ps = [jnp.exp(s - m) for s in ss]
            l = sum(jnp.sum(p, axis=0, keepdims=True) for p in ps)
            if sink_ref is not None:
                l = l + jnp.exp(sk - m)
            c["ps"] = [p.astype(BF16) for p in ps]
            c["l"], c["lse"] = l, m + jnp.log(l)

        def outputs(c):
            j, gi, rows = c["j"], c["gi"], c["rows"]
            ot = sum(_dot(c["vv"][r], p, TN) for r, p in zip(c["pieces"], c["ps"]))
            ot = ot * pl.reciprocal(c["l"], approx=True)
            for i, p in enumerate(c["pairs"]):
                o2t = jnp.where(top, ot[:, 256 * i:256 * i + 128], ot[:, 256 * i + 128:256 * i + 256])
                o_ref[rows, 128 * p:128 * (p + 1)] = o2t.T.astype(BF16)
            stat = stats.get(j, jnp.zeros((8, 128), F32))
            for i in range(a.nh):
                stat = jnp.where(rid == a.nh * gi + i, c["lse"][:, 128 * i:128 * (i + 1)], stat)
            stats[j] = stat
            if gi == a.groups[-1][0]:
                lse_ref[rows, :] = jnp.concatenate([stats.pop(j), jnp.zeros((120, 128), F32)], axis=0).T

        chains = [(j, gi, pairs) for j in range(QB) for gi, pairs in a.groups]
        live = {}
        for t in range(len(chains) + 2):
            if t < len(chains):
                live[t] = scores(*chains[t])
            if 0 <= t - 1 < len(chains):
                softmax(live[t - 1])
            if 0 <= t - 2 < len(chains):
                outputs(live.pop(t - 2))


    args = [q, k, v] + ([k, v] if a.ext_prev else [])
    in_specs = [a.q_spec] + a.kv_specs
    if sinks is not None:
        args.append(sinks)
        in_specs.append(pl.BlockSpec(memory_space=pltpu.SMEM))
    return _Part(body, args, in_specs, [a.row_spec, a.stat_spec], [_sds((T, qw), BF16), _sds((T, 128), F32)])


def _attn_bwd(q, qcb, qw, k, kcb, v, vcb, kvw, do, lse, dl, *, kind, nb=1, max_dist=BLK, gqa=False, sinkv=None):
    a = _Attn(kind, nb, max_dist, gqa, qw, kvw, qcb, kcb, vcb)

    def body(*refs):
        it = iter(refs)
        q_ref, kc_ref, vc_ref = next(it), next(it), next(it)
        kp_ref, vp_ref = (next(it), next(it)) if a.ext_prev else (None, None)
        do_ref, lse_ref, dl_ref = next(it), next(it), next(it)
        sinkv_ref = next(it) if sinkv is not None else None
        dq_ref = next(it)
        if kind == "mem":
            dkv_ref = next(it)
        else:
            dk_out, dv_out = next(it), next(it)
        dsink_ref = next(it) if sinkv is not None else None
        if kind != "mem":
            dk_ref, dv_ref, stage_k, stage_v, flush_sem = next(it), next(it), next(it), next(it), next(it)
        g = pl.program_id(0)
        lo = _lane_lo()
        top = lax.broadcasted_iota(jnp.int32, (128, 1), 0) < 64

        @pl.when(g == 0)
        def _():
            if kind == "mem":
                dkv_ref[...] = jnp.zeros_like(dkv_ref)
            else:
                dk_ref[...] = jnp.zeros_like(dk_ref)
                dv_ref[...] = jnp.zeros_like(dv_ref)
            if dsink_ref is not None:
                dsink_ref[...] = jnp.zeros_like(dsink_ref)

        kq = a.masks()
        stats_t = {}

        def first_matmuls(j, gi, pairs):
            rows = slice(BLK * j, BLK * (j + 1))
            if j not in stats_t:
                stats_t[j] = (lse_ref[rows, :].T, dl_ref[rows, :].T)
            lse_t, dl_t = stats_t[j]
            heads = [a.nh * gi + i for i in range(a.nh)]
            c = dict(rows=rows, gi=gi, pairs=pairs)
            c["qs"] = _stack_heads([q_ref[rows, 128 * p:128 * (p + 1)] for p in pairs], lo)
            c["dos"] = _stack_heads([do_ref[rows, 128 * p:128 * (p + 1)] for p in pairs], lo)
            c["lse_row"] = jnp.concatenate([lse_t[h:h + 1, :] for h in heads], axis=1)
            c["dl_row"] = jnp.concatenate([dl_t[h:h + 1, :] for h in heads], axis=1)
            c["kk"], vv, c["mask"], c["dests"] = a.keys(j, gi, kc_ref, vc_ref, kp_ref, vp_ref, lo, kq, g)
            c["s"] = _dot(c["kk"], c["qs"], NT)
            c["dp"] = _dot(vv, c["dos"], NT)
            return c

        def elementwise(c):
            s = c.pop("s")
            if c["mask"] is not None:
                s = jnp.where(c["mask"], s, NEG)
            p = jnp.exp(s - c["lse_row"])
            c["ds"] = (p * (c.pop("dp") - c["dl_row"])).astype(BF16)
            c["p"] = p.astype(BF16)

        def last_matmuls(c):
            gi, rows = c["gi"], c["rows"]
            dqt = _dot(c["kk"], c["ds"], TN)
            ck = _dot(c["ds"], c["qs"], NN)
            cv = _dot(c["p"], c["dos"], NN)
            if gqa:
                sel = lo if gi == 0 else jnp.logical_not(lo)
                ck = jnp.where(sel, ck + pltpu.roll(ck, 64, 1), 0.0)
                cv = jnp.where(sel, cv + pltpu.roll(cv, 64, 1), 0.0)
                kcols = slice(0, 128)
            else:
                kcols = slice(128 * gi, 128 * (gi + 1))
            for r0, nr, key0 in c["dests"]:
                krows = pl.ds(key0, nr)
                if kind == "mem":
                    dkv_ref[krows, kcols] += ck[r0:r0 + nr]
                    dkv_ref[krows, slice(kvw + kcols.start, kvw + kcols.stop)] += cv[r0:r0 + nr]
                else:
                    dk_ref[krows, kcols] += ck[r0:r0 + nr]
                    dv_ref[krows, kcols] += cv[r0:r0 + nr]
            for i, p in enumerate(c["pairs"]):
                dq2t = jnp.where(top, dqt[:, 256 * i:256 * i + 128], dqt[:, 256 * i + 128:256 * i + 256])
                dq_ref[rows, 128 * p:128 * (p + 1)] = dq2t.T.astype(BF16)

        chains = [(j, gi, pairs) for j in range(QB) for gi, pairs in a.groups]
        live = {}
        for t in range(len(chains) + 2):
            if t < len(chains):
                live[t] = first_matmuls(*chains[t])
            if 0 <= t - 1 < len(chains):
                elementwise(live[t - 1])
            if 0 <= t - 2 < len(chains):
                last_matmuls(live.pop(t - 2))
        if dsink_ref is not None:
            ps = jnp.exp(sinkv_ref[...] - lse_ref[...]) * dl_ref[...]
            dsink_ref[...] += jnp.sum(ps, axis=0, keepdims=True)
        if kind != "mem":
            n_steps = T // QR

            def flush(step):
                rows = pl.ds(pl.multiple_of(step * QR, QR), QR)
                out = []
                for acc, stage, dst, i in ((dk_ref, stage_k, dk_out, 0), (dv_ref, stage_v, dv_out, 1)):
                    stage[...] = acc[rows, :].astype(BF16)
                    out.append(pltpu.make_async_copy(stage, dst.at[rows, :], flush_sem.at[i]))
                return out

            def flushed(step):
                rows = pl.ds(pl.multiple_of(step * QR, QR), QR)
                return [pltpu.make_async_copy(stage, dst.at[rows, :], flush_sem.at[i])
                        for stage, dst, i in ((stage_k, dk_out, 0), (stage_v, dv_out, 1))]

            @pl.when(g >= 2)
            def _():
                for cp in flushed(g - 2):
                    cp.wait()

            @pl.when(g >= 1)
            def _():
                for cp in flush(g - 1):
                    cp.start()

            @pl.when(g == n_steps - 1)
            def _():
                for cp in flushed(g - 1):
                    cp.wait()
                for cp in flush(g):
                    cp.start()
                for cp in flushed(g):
                    cp.wait()

    args = [q, k, v] + ([k, v] if a.ext_prev else []) + [do, lse, dl]
    in_specs = [a.q_spec] + a.kv_specs + [a.row_spec, a.stat_spec, a.stat_spec]
    if sinkv is not None:
        args.append(sinkv)
        in_specs.append(_full((1, 128)))
    out_shape = [_sds((T, qw), BF16)]
    out_specs = [a.row_spec]
    scratch = []
    if kind == "mem":
        out_shape.append(_sds((B_LOC * MEM_LEN, 2 * kvw), F32))
        out_specs.append(pl.BlockSpec((B_LOC * MEM_LEN, 2 * kvw), lambda g: (0, 0), pipeline_mode=pl.Buffered(1)))
    else:
        out_shape += [_sds((T, kvw), BF16)] * 2
        out_specs += [pl.BlockSpec(memory_space=pl.ANY)] * 2
        scratch = [pltpu.VMEM((T, kvw), F32)] * 2 + [pltpu.VMEM((QR, kvw), BF16)] * 2 + [pltpu.SemaphoreType.DMA((2,))]
    if sinkv is not None:
        out_shape.append(_sds((1, 128), F32))
        out_specs.append(_full((1, 128)))
    return _Part(body, args, in_specs, out_specs, out_shape, scratch)


def _dot2(v, w_ref):
    hi = v.astype(BF16)
    lo = (v - hi.astype(F32)).astype(BF16)
    return _dot(hi, w_ref[...], NN) + _dot(lo, w_ref[...], NN)


def _middle(oa, o1, l1, o4, l4, o16, l16, oc, z, x, tgt, g_br, ln_g, ln_b, wout, spread4, gather4, gather8):
    tm = 512
    spt = SEQ // tm

    def body(oa_ref, o1_ref, l1_ref, o4_ref, l4_ref, o16_ref, l16_ref, oc_ref, z_ref, x_ref, t_ref,
             g_ref, lg_ref, lb_ref, w_ref, sp4_ref, ga4_ref, ga8_ref,
             du_ref, dz_ref, doa_ref, dla_ref,
             dobn_ref, lsen_ref, dlbn_ref, dob4_ref, lse4_ref, dlb4_ref, dob16_ref, lse16_ref, dlb16_ref,
             doc_ref, dlc_ref, acc_ref, gout_ref, scr):
        i = pl.program_id(0)

        @pl.when(i == 0)
        def _():
            acc_ref[...] = jnp.zeros_like(acc_ref)
            gout_ref[...] = jnp.zeros_like(gout_ref)

        for res in range(4):
            rows = pl.ds(res, tm // 4, stride=4)
            for j in range(2):
                scr[j, rows, :] = o4_ref[0, res, :, 128 * j:128 * (j + 1)].astype(F32)
            scr[2, rows, :] = l4_ref[0, res]
        for res in range(16):
            rows = pl.ds(res, tm // 16, stride=16)
            for j in range(2):
                scr[3 + j, rows, :] = o16_ref[0, res, :, 128 * j:128 * (j + 1)].astype(F32)
            scr[5, rows, :] = l16_ref[0, res]
        inv_d = 1.0 / D_MODEL
        gb, lg, lb = g_ref[...], lg_ref[...], lb_ref[...]

        def rms(o):
            r = lax.rsqrt(jnp.sum(o * o, axis=1, keepdims=True) * (1.0 / o.shape[1]) + RMS_EPS)
            return o * r, r

        def rms_bwd(dn_, n_, r):
            return r * (dn_ - n_ * (jnp.sum(dn_ * n_, axis=1, keepdims=True) * (1.0 / n_.shape[1])))

        def forward(rs):
            o4v = jnp.concatenate([scr[0, rs, :], scr[1, rs, :]], axis=1)
            o16v = jnp.concatenate([scr[3, rs, :], scr[4, rs, :]], axis=1)
            l1v, l4v, l16v = l1_ref[rs, :], scr[2, rs, :], scr[5, rs, :]
            mx = jnp.maximum(jnp.maximum(l1v, l4v), l16v)
            e1, e4, e16 = jnp.exp(l1v - mx), jnp.exp(l4v - mx), jnp.exp(l16v - mx)
            ssum = e1 + e4 + e16
            inv = 1.0 / ssum
            c = dict(rs=rs, lse_b=mx + jnp.log(ssum))
            c["ob"] = (_dot2(e1 * inv, sp4_ref) * o1_ref[rs, :].astype(F32) + _dot2(e4 * inv, sp4_ref) * o4v
                       + _dot2(e16 * inv, sp4_ref) * o16v)
            c["oa"], c["oc"] = oa_ref[rs, :].astype(F32), oc_ref[rs, :].astype(F32)
            na, c["ra"] = rms(c["oa"])
            nb_, c["rb"] = rms(c["ob"])
            nc, c["rc"] = rms(c["oc"])
            c["n"] = jnp.concatenate([na, nb_, nc], axis=1)
            c["zf"] = z_ref[rs, :].astype(F32)
            c["sig"] = 1.0 / (1.0 + jnp.exp(-c["zf"]))
            c["sz"] = c["zf"] * c["sig"]
            c["yb"] = (c["n"] * gb * c["sz"]).astype(BF16)
            c["y2"] = _dot(c["yb"], w_ref[...], NN)
            return c

        def norm(c):
            rs = c["rs"]
            u = ALPHA * x_ref[rs, :] + c.pop("y2")
            mu = jnp.sum(u, axis=1, keepdims=True) * inv_d
            uc = u - mu
            rstd = lax.rsqrt(jnp.sum(uc * uc, axis=1, keepdims=True) * inv_d + LN_EPS)
            xh = uc * rstd
            diff = xh * lg + lb - t_ref[rs, :]
            acc_ref[0:1, :] += jnp.sum(diff * diff, axis=0, keepdims=True) * (0.5 * inv_d)
            dout = diff * inv_d
            acc_ref[2:3, :] += jnp.sum(dout * xh, axis=0, keepdims=True)
            acc_ref[3:4, :] += jnp.sum(dout, axis=0, keepdims=True)
            dxh = dout * lg
            du = rstd * (dxh - jnp.sum(dxh, axis=1, keepdims=True) * inv_d
                         - xh * (jnp.sum(dxh * xh, axis=1, keepdims=True) * inv_d))
            dub = du.astype(BF16)
            du_ref[rs, :] = dub
            c["dy"] = _dot(dub, w_ref[...], NT)
            gout_ref[...] += _dot(c.pop("yb"), dub, TN)

        def backward(c):
            rs, n, dy, zf, sig = c["rs"], c["n"], c["dy"], c["zf"], c["sig"]
            t1 = dy * c["sz"]
            acc_ref[1:2, :] += jnp.sum(t1 * n, axis=0, keepdims=True)
            dn = t1 * gb
            dz_ref[rs, :] = (dy * n * gb * (sig * (1.0 + zf * (1.0 - sig)))).astype(BF16)
            doa = rms_bwd(dn[:, :W_A], n[:, :W_A], c["ra"])
            dob = rms_bwd(dn[:, W_A:W_A + W_B], n[:, W_A:W_A + W_B], c["rb"])
            doc = rms_bwd(dn[:, W_A + W_B:], n[:, W_A + W_B:], c["rc"])
            doa_ref[rs, :] = doa.astype(BF16)
            dla_ref[rs, :] = _dot2(doa * c["oa"], ga8_ref)
            doc_ref[rs, :] = doc.astype(BF16)
            dlc_ref[rs, :] = _dot2(doc * c["oc"], ga4_ref)
            dobn_ref[rs, :] = dob.astype(BF16)
            lsen_ref[rs, :] = c["lse_b"]
            dlbn_ref[rs, :] = _dot2(dob * c["ob"], ga4_ref)
            scr[0, rs, :] = dob[:, :128]
            scr[1, rs, :] = dob[:, 128:]

        halves = [slice(h * (tm // 2), (h + 1) * (tm // 2)) for h in range(2)]
        live = {}
        for t in range(len(halves) + 2):
            if t < len(halves):
                live[t] = forward(halves[t])
            if 0 <= t - 1 < len(halves):
                norm(live[t - 1])
            if 0 <= t - 2 < len(halves):
                backward(live.pop(t - 2))
        for j in range(2):
            sl = slice(128 * j, 128 * (j + 1))
            for res in range(4):
                dob4_ref[0, res, :, sl] = scr[j, pl.ds(res, tm // 4, stride=4), :].astype(BF16)
            for res in range(16):
                dob16_ref[0, res, :, sl] = scr[j, pl.ds(res, tm // 16, stride=16), :].astype(BF16)
        for res in range(4):
            rows = pl.ds(res, tm // 4, stride=4)
            lse4_ref[0, res] = lsen_ref[rows, :]
            dlb4_ref[0, res] = dlbn_ref[rows, :]
        for res in range(16):
            rows = pl.ds(res, tm // 16, stride=16)
            lse16_ref[0, res] = lsen_ref[rows, :]
            dlb16_ref[0, res] = dlbn_ref[rows, :]


    tok = lambda w: pl.BlockSpec((tm, w), lambda i: (i, 0))
    p4 = lambda w: pl.BlockSpec((1, 4, tm // 4, w), lambda i: (i // spt, 0, i % spt, 0))
    p16 = lambda w: pl.BlockSpec((1, 16, tm // 16, w), lambda i: (i // spt, 0, i % spt, 0))
    s4 = lambda w, dt: _sds((B_LOC, 4, SEQ // 4, w), dt)
    s16 = lambda w, dt: _sds((B_LOC, 16, SEQ // 16, w), dt)
    row = _full((1, D_MODEL))
    return pl.pallas_call(
        body, name="middle", grid=(T // tm,),
        in_specs=[tok(W_A), tok(W_B), tok(128), p4(W_B), p4(128), p16(W_B), p16(128), tok(W_C), tok(D_MIX),
                  tok(D_MODEL), tok(D_MODEL), row, row, row, _full((D_MIX, D_MODEL)),
                  _full((128, W_B)), _full((W_B, 128)), _full((W_A, 128))],
        out_specs=(tok(D_MODEL), tok(D_MIX), tok(W_A), tok(128),
                   tok(W_B), tok(128), tok(128), p4(W_B), p4(128), p4(128), p16(W_B), p16(128), p16(128),
                   tok(W_C), tok(128), _full((8, D_MODEL)), _full((D_MIX, D_MODEL))),
        out_shape=(_sds((T, D_MODEL), BF16), _sds((T, D_MIX), BF16),
                   _sds((T, W_A), BF16), _sds((T, 128), F32),
                   _sds((T, W_B), BF16), _sds((T, 128), F32), _sds((T, 128), F32),
                   s4(W_B, BF16), s4(128, F32), s4(128, F32), s16(W_B, BF16), s16(128, F32), s16(128, F32),
                   _sds((T, W_C), BF16), _sds((T, 128), F32), _sds((8, D_MODEL), F32),
                   _sds((D_MIX, D_MODEL), F32)),
        scratch_shapes=[pltpu.VMEM((6, tm, 128), F32)],
        compiler_params=_cp(("arbitrary",), vmem_mb=56),
    )(*_pin(oa, o1, l1, o4, l4, o16, l16, oc, z, x, tgt, g_br, ln_g, ln_b, wout, spread4, gather4, gather8))


class _ReduceScatter:
    def __init__(self, shapes):
        self.shapes = shapes

    def scratch_shapes(self):
        out = []
        for n, w in self.shapes:
            h = n // 2
            out += [pltpu.VMEM((4, h, w), F32), pltpu.VMEM((4, h, w), F32), pltpu.VMEM((3, h, w), BF16),
                    pltpu.VMEM((3, h, w), BF16), pltpu.VMEM((h, w), F32)]
        na = len(self.shapes)
        dma = pltpu.SemaphoreType.DMA
        return out + [dma((na, 4)), dma((na, 4)), dma((na, 4)), dma((na, 3)), dma((na, 3)), dma((na,)), dma((na,)),
                      dma((na,))]

    def bind(self, g_refs, r_refs, scratch):
        na = len(self.shapes)
        bufs = [scratch[5 * a:5 * a + 5] for a in range(na)]
        mine, sib, stage, land, tot = (tuple(b[i] for b in bufs) for i in range(5))
        loc_sem, s1_send, s1_recv, s2_send, s2_recv, s3_send, s3_recv, st_sem = scratch[5 * na:5 * na + 8]
        x, y, c = lax.axis_index("x"), lax.axis_index("y"), lax.axis_index("c")
        me, sibling = (x, y, c), (x, y, 1 - c)
        my_chip = 2 * x + y
        chips = [(1 - x, y), (x, 1 - y), (1 - x, 1 - y)]
        order = [2 * chip[0] + chip[1] for chip in chips] + [my_chip]

        def rows(a, k, half):
            n = self.shapes[a][0]
            return pl.ds(pl.multiple_of(k * n + half * (n // 2), 8), n // 2)

        def load(a, k):
            return pltpu.make_async_copy(g_refs[a].at[rows(a, k, c), :], mine[a].at[k], loc_sem.at[a, k])

        def s1(a, k, half):
            return pltpu.make_async_remote_copy(
                src_ref=g_refs[a].at[rows(a, k, half), :], dst_ref=sib[a].at[k],
                send_sem=s1_send.at[a, k], recv_sem=s1_recv.at[a, k], device_id=sibling, device_id_type=MESH)

        def s2(a, j, to):
            return pltpu.make_async_remote_copy(
                src_ref=stage[a].at[j], dst_ref=land[a].at[j], send_sem=s2_send.at[a, j], recv_sem=s2_recv.at[a, j],
                device_id=to, device_id_type=MESH)

        def s3(a, half, to):
            return pltpu.make_async_remote_copy(
                src_ref=tot[a], dst_ref=r_refs[a].at[rows(a, 0, half), :], send_sem=s3_send.at[a],
                recv_sem=s3_recv.at[a], device_id=to, device_id_type=MESH)

        def store(a):
            return pltpu.make_async_copy(tot[a], r_refs[a].at[rows(a, 0, c), :], st_sem.at[a])

        def start():
            for k in order:
                for a in range(na):
                    load(a, k).start()
                    s1(a, k, 1 - c).start()

        def exchange():
            for j, chip in enumerate(chips):
                k = order[j]
                for a in range(na):
                    load(a, k).wait()
                    s1(a, k, c).wait_recv()
                    stage[a][j] = (mine[a][k] + sib[a][k]).astype(BF16)
                    s2(a, j, (*chip, c)).start()
            for a in range(na):
                load(a, my_chip).wait()
                s1(a, my_chip, c).wait_recv()
                tot[a][...] = mine[a][my_chip] + sib[a][my_chip]

        def finish():
            for a in range(na):
                t = tot[a][...]
                for j in range(3):
                    s2(a, j, me).wait_recv()
                    t = t + land[a][j].astype(F32)
                tot[a][...] = t
                s3(a, c, sibling).start()
                store(a).start()

        def drain():
            for a in range(na):
                s3(a, 1 - c, me).wait_recv()
                store(a).wait()
            for a in range(na):
                for k in order:
                    s1(a, k, 1 - c).wait_send()
                for j, chip in enumerate(chips):
                    s2(a, j, (*chip, c)).wait_send()
                s3(a, c, sibling).wait_send()

        return start, exchange, finish, drain

    def part(self, grads, steps):
        def body(*refs):
            na = len(self.shapes)
            i = pl.program_id(0)
            for step, phase in zip(steps, self.bind(refs[:na], refs[na:2 * na], refs[2 * na:])):
                pl.when(i == step)(phase)

        hbm = pl.BlockSpec(memory_space=pl.ANY)
        return _Part(body, list(grads), [hbm] * len(grads), [hbm] * len(grads),
                     [_sds((n, w), F32) for n, w in self.shapes], self.scratch_shapes())


def _dh_dx(dqa, dka, dva, dqn, dkn, dvn, dq4, dk4, dv4, dq16, dk16, dv16, dqc, dz, du, xb, cos, sa, sb, winT):
    tm = 512
    spt = SEQ // tm

    def body(dqa_ref, dka_ref, dva_ref, dqn_ref, dkn_ref, dvn_ref, dq4_ref, dk4_ref, dv4_ref,
             dq16_ref, dk16_ref, dv16_ref, dqc_ref, dz_ref, du_ref, xb_ref, cos_ref, sa_ref, sb_ref, w_ref,
             gx_ref, db_ref, gin_ref, dh_ref, scr):
        i = pl.program_id(0)

        @pl.when(i == 0)
        def _():
            db_ref[...] = jnp.zeros_like(db_ref)
            gin_ref[...] = jnp.zeros_like(gin_ref)

        cos_t, sa_t, sb_t = cos_ref[...], sa_ref[...], sb_ref[...]

        def rope_t(t):
            return _rope(t, cos_t, sa_t, sb_t, -1)

        def put(r0, val):
            n = val.shape[1]
            dh_ref[:, r0:r0 + n] = val.astype(BF16)
            db_ref[:, r0:r0 + n] += jnp.sum(val, axis=0, keepdims=True)

        put(O_QA, rope_t(dqa_ref[...].astype(F32)) * QK_SCALE)
        put(O_KA, rope_t(dka_ref[...].astype(F32)))
        put(O_VA, dva_ref[...].astype(F32))
        put(O_QC, dqc_ref[...].astype(F32) * QK_SCALE)
        put(O_Z, dz_ref[...].astype(F32))
        for k, (n_ref, r4, r16) in enumerate(((dqn_ref, dq4_ref, dq16_ref), (dkn_ref, dk4_ref, dk16_ref),
                                               (dvn_ref, dv4_ref, dv16_ref))):
            for j in range(2):
                sl = slice(128 * j, 128 * (j + 1))
                scr[2 * k + j] = n_ref[:, sl].astype(F32)
                for res in range(4):
                    scr[2 * k + j, pl.ds(res, tm // 4, stride=4), :] += r4[0, res, :, sl].astype(F32)
                for res in range(16):
                    scr[2 * k + j, pl.ds(res, tm // 16, stride=16), :] += r16[0, res, :, sl].astype(F32)
        cat = lambda a: jnp.concatenate([scr[a], scr[a + 1]], axis=1)
        put(O_QB, rope_t(cat(0)) * QK_SCALE)
        put(O_KB, rope_t(cat(2)))
        put(O_VB, cat(4))
        gx_ref[...] = _dot(dh_ref[...], w_ref[...], NN) + ALPHA * du_ref[...].astype(F32)
        gin_ref[...] += _dot(dh_ref[...], xb_ref[...], TN)

    tok = lambda w: pl.BlockSpec((tm, w), lambda i: (i, 0))
    tab = pl.BlockSpec((tm, 128), lambda i: (i % spt, 0))
    p4 = pl.BlockSpec((1, 4, tm // 4, W_B), lambda i: (i // spt, 0, i % spt, 0))
    p16 = pl.BlockSpec((1, 16, tm // 16, W_B), lambda i: (i // spt, 0, i % spt, 0))
    once = lambda shape: pl.BlockSpec(shape, lambda i: (0, 0), pipeline_mode=pl.Buffered(1))
    return pl.pallas_call(
        body, name="dh_dx", grid=(T // tm,),
        in_specs=[tok(W_A), tok(W_KV_A), tok(W_KV_A), tok(W_B), tok(W_B), tok(W_B), p4, p4, p4, p16, p16, p16,
                  tok(W_C), tok(D_MIX), tok(D_MODEL), tok(D_MODEL), tab, tab, tab, once((D_IN, D_MODEL))],
        out_specs=(tok(D_MODEL), _full((1, D_IN)), once((D_IN, D_MODEL))),
        out_shape=(_sds((T, D_MODEL), F32), _sds((1, D_IN), F32), _sds((D_IN, D_MODEL), F32)),
        scratch_shapes=[pltpu.VMEM((tm, D_IN), BF16), pltpu.VMEM((6, tm, 128), F32)],
        compiler_params=_cp(("arbitrary",), vmem_mb=56),
    )(*_pin(dqa, dka, dva, dqn, dkn, dvn, dq4, dk4, dv4, dq16, dk16, dv16, dqc, dz, du, xb, cos, sa, sb, winT))


def _tn_matmul(name, a, b, bm, bt):
    n, m_all = a.shape
    n_cols = b.shape[1]

    def body(a_ref, b_ref, o_ref):
        @pl.when(pl.program_id(1) == 0)
        def _():
            o_ref[...] = jnp.zeros_like(o_ref)

        o_ref[...] += _dot(a_ref[...].astype(BF16), b_ref[...].astype(BF16), TN)

    return pl.pallas_call(
        body, name=name, grid=(m_all // bm, n // bt),
        in_specs=[pl.BlockSpec((bt, bm), lambda m, t: (t, m)), pl.BlockSpec((bt, n_cols), lambda m, t: (t, 0))],
        out_specs=pl.BlockSpec((bm, n_cols), lambda m, t: (m, 0)),
        out_shape=_sds((m_all, n_cols), F32),
        compiler_params=_cp(("parallel", "arbitrary"), vmem_mb=48),
    )(*_pin(a, b))


def _reduce_grads(g_in, acc, dbin, dsink):
    rs = _ReduceScatter([(SH_IN, D_MODEL)])

    def body(g_ref, acc_ref, dbin_ref, dsink_ref, r_ref, sv_ref, sv_mine, sv_all, sv_send, sv_recv, *rs_scratch):
        x, y, c = lax.axis_index("x"), lax.axis_index("y"), lax.axis_index("c")
        chips = [(1 - x, y), (x, 1 - y), (1 - x, 1 - y)]
        start, exchange, finish, drain = rs.bind((g_ref,), (r_ref,), rs_scratch)
        start()

        sv_mine[...] = jnp.zeros_like(sv_mine)
        sv_mine[0:4, 0:D_MODEL] = acc_ref[0:4, :]
        sv_mine[4:5, 0:D_IN] = dbin_ref[...]
        sv_mine[5:6, 0:128] = dsink_ref[...]
        my_dev = 4 * x + 2 * y + c
        others = [(x, y, 1 - c)] + [(*chip, cc) for chip in chips for cc in (c, 1 - c)]

        def sv_copy(j, to):
            return pltpu.make_async_remote_copy(
                src_ref=sv_mine, dst_ref=sv_all.at[my_dev], send_sem=sv_send.at[j], recv_sem=sv_recv.at[j],
                device_id=to, device_id_type=MESH)

        sv_sends = [sv_copy(j, to) for j, to in enumerate(others)]
        for cp in sv_sends:
            cp.start()
        exchange()
        finish()
        sv_all[my_dev] = sv_mine[...]
        for j in range(7):
            sv_copy(j, (x, y, c)).wait_recv()
        tot = sv_all[0]
        for d in range(1, 8):
            tot = tot + sv_all[d]
        sv_ref[...] = tot
        drain()
        for cp in sv_sends:
            cp.wait_send()

    vm = pl.BlockSpec(memory_space=pltpu.VMEM)
    hbm = pl.BlockSpec(memory_space=pl.ANY)
    return pl.pallas_call(
        body, name="reduce_grads",
        out_shape=(_sds((SH_IN, D_MODEL), F32), _vm_sds((8, SV_W), F32)),
        in_specs=[hbm, vm, vm, vm], out_specs=(hbm, vm),
        scratch_shapes=[pltpu.VMEM((8, SV_W), F32), pltpu.VMEM((8, 8, SV_W), F32),
                        pltpu.SemaphoreType.DMA((7,)), pltpu.SemaphoreType.DMA((7,))] + rs.scratch_shapes(),
        compiler_params=_cp(vmem_mb=40),
    )(pltpu.with_memory_space_constraint(g_in, pltpu.HBM), acc, dbin, dsink)


def _adamw(name, w, g, m, v, rows=None, copy_g=False):
    shape = w.shape
    rows = shape[0] if rows is None else rows
    n_out = 4 if copy_g else 3

    def body(w_ref, g_ref, m_ref, v_ref, d_ref, nm_ref, nv_ref, *go_ref):
        gv = g_ref[...]
        if copy_g:
            go_ref[0][...] = gv
        nm = ADAM_B1 * m_ref[...] + (1.0 - ADAM_B1) * gv
        nv = ADAM_B2 * v_ref[...] + (1.0 - ADAM_B2) * (gv * gv)
        m_hat = nm / (1.0 - ADAM_B1 ** ADAM_STEP)
        v_hat = nv / (1.0 - ADAM_B2 ** ADAM_STEP)
        d_ref[...] = -ADAM_LR * (m_hat / (jnp.sqrt(v_hat) + ADAM_EPS) + ADAM_WD * w_ref[...])
        nm_ref[...] = nm
        nv_ref[...] = nv

    spec = pl.BlockSpec((rows, shape[1]), lambda i: (i, 0))
    return pl.pallas_call(
        body, name=name, grid=(shape[0] // rows,), in_specs=[spec] * 4, out_specs=(spec,) * n_out,
        out_shape=(_sds(shape, F32),) * n_out, compiler_params=_cp(("parallel",)),
    )(*_pin(w, g, m, v))


def _adamw_small(sv, ws, ms, vs):
    where = ((4, D_IN, 1.0), (5, 8, -1.0), (1, D_MIX, 1.0), (2, D_MODEL, 1.0), (3, D_MODEL, 1.0))

    def body(sv_ref, *refs):
        ins, outs = refs[:15], refs[15:]
        for p, (row, width, sign) in enumerate(where):
            gv = sign * sv_ref[row:row + 1, 0:width]
            w_ref, m_ref, v_ref = ins[p], ins[5 + p], ins[10 + p]
            nm = ADAM_B1 * m_ref[...] + (1.0 - ADAM_B1) * gv
            nv = ADAM_B2 * v_ref[...] + (1.0 - ADAM_B2) * (gv * gv)
            m_hat = nm / (1.0 - ADAM_B1 ** ADAM_STEP)
            v_hat = nv / (1.0 - ADAM_B2 ** ADAM_STEP)
            outs[4 * p][...] = gv
            outs[4 * p + 1][...] = -ADAM_LR * (m_hat / (jnp.sqrt(v_hat) + ADAM_EPS) + ADAM_WD * w_ref[...])
            outs[4 * p + 2][...] = nm
            outs[4 * p + 3][...] = nv

    res = pl.pallas_call(
        body, name="adamw_small", out_shape=tuple(_vm_sds(w.shape, F32) for w in ws for _ in range(4)),
    )(sv, *ws, *ms, *vs)
    return [tuple(res[4 * p:4 * p + 4]) for p in range(5)]


def _rope_tables():
    pos = jnp.arange(SEQ, dtype=F32)
    inv = ROPE_THETA ** (-jnp.arange(0, 64, 2, dtype=F32) / 64)
    ang = pos[:, None] * inv[None, :]
    cos, sin = lax.optimization_barrier((jnp.cos(ang), jnp.sin(ang)))
    cos, sin = jnp.tile(cos, (1, 4)), jnp.tile(sin, (1, 4))
    low = (jnp.arange(128) % 64) < 32
    return cos, jnp.where(low, -sin, 0.0), jnp.where(low, 0.0, sin)


def _local_step(x2, mem2, tgt2, winT, wout, wmem, b_in, sinks, g_branch, ln_gain, ln_bias):
    cos, sa, sb = _rope_tables()
    sinkv = jnp.pad(sinks, ((0, 0), (0, 120)))
    head_of_lane = jnp.arange(512)[None, :] // 64
    gather8 = (head_of_lane.T == jnp.arange(128)[None, :]).astype(BF16)
    gather4 = gather8[:W_B]
    spread4 = gather4.T

    xb, qa, ka, va, bn, b4, b16, qc, z, wout, wmem = _in_proj(x2, winT, b_in, cos, sa, sb, wout, wmem)
    memb, mkv = _mem_kv(mem2, wmem)
    b4f, b16f = b4.reshape(T, 768), b16.reshape(T, 768)

    swa = dict(kind="band", nb=SEQ // BLK, max_dist=BLK - 1, gqa=True)
    dil = (dict(kind="band", nb=SEQ // BLK), dict(kind="band", nb=SEQ // 4 // BLK), dict(kind="band", nb=1))
    (oa, lse_a), (o1, l1), (o4, l4), (o16, l16), (oc, lse_c) = _run_parts("attn_fwd", [
        _attn_fwd(qa, 0, W_A, ka, 0, va, 0, W_KV_A, sinks=sinks, **swa),
        _attn_fwd(bn, 0, W_B, bn, 1, bn, 2, W_B, **dil[0]),
        _attn_fwd(b4f, 0, W_B, b4f, 1, b4f, 2, W_B, **dil[1]),
        _attn_fwd(b16f, 0, W_B, b16f, 1, b16f, 2, W_B, **dil[2]),
        _attn_fwd(qc, 0, W_C, mkv, 0, mkv, 1, W_C, kind="mem")], "parallel", 48)

    s4 = lambda w: (B_LOC, 4, SEQ // 4, w)
    s16 = lambda w: (B_LOC, 16, SEQ // 16, w)
    (du, dz, doa, dla, dobn, lsen, dlbn, dob4, lse4, dlb4, dob16, lse16, dlb16, doc, dlc, acc, g_out) = _middle(
        oa, o1, l1, o4.reshape(s4(W_B)), l4.reshape(s4(128)), o16.reshape(s16(W_B)), l16.reshape(s16(128)), oc, z,
        x2, tgt2, g_branch, ln_gain, ln_bias, wout, spread4, gather4, gather8)

    flat = lambda a: a.reshape(T, a.shape[-1])
    (dqa, dka, dva, dsink), (dqc, dmkv) = _run_parts("attn_bwd_a", [
        _attn_bwd(qa, 0, W_A, ka, 0, va, 0, W_KV_A, doa, lse_a, dla, sinkv=sinkv, **swa),
        _attn_bwd(qc, 0, W_C, mkv, 0, mkv, 1, W_C, doc, lse_c, dlc, kind="mem")], "arbitrary", 48)
    g_mem = _tn_matmul("dw_mem", memb, dmkv, D_MODEL, B_LOC * MEM_LEN)
    last = T // QR - 1
    (r_out, r_mem), (dqn, dkn, dvn), (dq4, dk4, dv4), (dq16, dk16, dv16) = _run_parts("attn_bwd_b", [
        _ReduceScatter([(SH_OUT, D_MODEL), (SH_MEM, 2 * W_C)]).part((g_out, g_mem), (0, 1, last, last)),
        _attn_bwd(bn, 0, W_B, bn, 1, bn, 2, W_B, dobn, lsen, dlbn, **dil[0]),
        _attn_bwd(b4f, 0, W_B, b4f, 1, b4f, 2, W_B, flat(dob4), flat(lse4), flat(dlb4), **dil[1]),
        _attn_bwd(b16f, 0, W_B, b16f, 1, b16f, 2, W_B, flat(dob16), flat(lse16), flat(dlb16), **dil[2])],
        "arbitrary", 60)

    r4 = lambda a: a.reshape(s4(W_B))
    r16 = lambda a: a.reshape(s16(W_B))
    gx, dbin, g_in = _dh_dx(dqa, dka, dva, dqn, dkn, dvn, r4(dq4), r4(dk4), r4(dv4), r16(dq16), r16(dk16),
                            r16(dv16), dqc, dz, du, xb, cos, sa, sb, winT)
    return gx, g_in, r_out, r_mem, acc, dbin, dsink


def kernel(x, mem, w_in, b_in, w_mem, attn_sinks, g_branch, w_out, ln_gain, ln_bias, loss_target, m_w_in, m_b_in, m_w_mem, m_attn_sinks, m_g_branch, m_w_out, m_ln_gain, m_ln_bias, v_w_in, v_b_in, v_w_mem, v_attn_sinks, v_g_branch, v_w_out, v_ln_gain, v_ln_bias):
    winT, wout, wmem = _gather_weights(w_in[0].T, w_out[0], w_mem[0])
    gx, g_in, r_out, r_mem, acc, dbin, dsink = _local_step(
        x.reshape(T, D_MODEL), mem.reshape(B_LOC * MEM_LEN, D_MODEL), loss_target.reshape(T, D_MODEL),
        winT, wout, wmem, b_in, attn_sinks, g_branch, ln_gain, ln_bias)
    r_in, sv = _reduce_grads(g_in, acc, dbin, dsink)

    loss = jnp.sum(sv[0, :D_MODEL])
    small = ["b_in", "attn_sinks", "g_branch", "ln_gain", "ln_bias"]
    weights = dict(w_in=w_in, b_in=b_in, w_mem=w_mem, attn_sinks=attn_sinks, g_branch=g_branch, w_out=w_out,
                   ln_gain=ln_gain, ln_bias=ln_bias)
    ms = dict(w_in=m_w_in, b_in=m_b_in, w_mem=m_w_mem, attn_sinks=m_attn_sinks, g_branch=m_g_branch, w_out=m_w_out,
              ln_gain=m_ln_gain, ln_bias=m_ln_bias)
    vs = dict(w_in=v_w_in, b_in=v_b_in, w_mem=v_w_mem, attn_sinks=v_attn_sinks, g_branch=v_g_branch, w_out=v_w_out,
              ln_gain=v_ln_gain, ln_bias=v_ln_bias)
    out = dict(zip(small, _adamw_small(sv, [weights[n] for n in small], [ms[n] for n in small],
                                       [vs[n] for n in small])))
    d, nm, nv, g = (a.T[None] for a in _adamw("adamw_w_in", w_in[0].T, r_in, m_w_in[0].T, v_w_in[0].T, SH_IN // 4,
                                              copy_g=True))
    out["w_in"] = (g, d, nm, nv)
    for n, r in (("w_out", r_out), ("w_mem", r_mem)):
        d, nm, nv, g = (a[None] for a in _adamw("adamw_" + n, weights[n][0], r, ms[n][0], vs[n][0], copy_g=True))
        out[n] = (g, d, nm, nv)
    names = ["w_in", "b_in", "w_mem", "attn_sinks", "g_branch", "w_out", "ln_gain", "ln_bias"]
    return (loss, gx.reshape(B_LOC, SEQ, D_MODEL), *[out[n][k] for k in range(4) for n in names])
```

```python
import functools

import jax
import jax.numpy as jnp
from jax import lax
from jax.experimental import pallas as pl
from jax.experimental.pallas import tpu as pltpu

F32, BF16 = jnp.float32, jnp.bfloat16

D_MODEL = 1024
SEQ = 2048
B_LOC = 2
T = B_LOC * SEQ
BLK = 128
MEM_LEN = 256
W_A, W_KV_A, W_B, W_C, D_MIX = 512, 128, 256, 256, 1024
D_IN = 2816
O_QA, O_KA, O_VA, O_QB, O_KB, O_VB, O_QC, O_Z = 0, 512, 640, 768, 1024, 1280, 1536, 1792
ROPE_THETA = 10000.0
LN_EPS = 1e-5
RMS_EPS = 1e-6
ALPHA = 2.0 ** 0.25
QK_SCALE = 0.125
N_CHIP = 4
SH_IN, SH_OUT, SH_MEM = D_IN // N_CHIP, D_MIX // N_CHIP, D_MODEL // N_CHIP
NEG = -1e30
ADAM_LR, ADAM_B1, ADAM_B2, ADAM_EPS, ADAM_WD, ADAM_STEP = 0.001, 0.9, 0.999, 1e-08, 0.01, 10
SV_W = 3072
MESH = pl.DeviceIdType.MESH

NN = ((1,), (0,))
NT = ((1,), (1,))
TN = ((0,), (0,))


def _dot(a, b, dims):
    return lax.dot_general(a, b, (dims, ((), ())), preferred_element_type=F32)


def _cp(sem=None, vmem_mb=None):
    kw = {}
    if sem is not None:
        kw["dimension_semantics"] = sem
    if vmem_mb is not None:
        kw["vmem_limit_bytes"] = vmem_mb * 1024 * 1024
    return pltpu.CompilerParams(**kw)


def _sds(shape, dtype):
    return pltpu.HBM(shape, dtype)


def _vm_sds(shape, dtype):
    return jax.ShapeDtypeStruct(shape, dtype)


def _pin(*args):
    return [pltpu.with_memory_space_constraint(a, pltpu.HBM) for a in args]


def _full(shape):
    n = len(shape)
    return pl.BlockSpec(shape, lambda *_: (0,) * n)


def _shard_rows(ref, n, chip, half):
    start = pl.multiple_of((2 * chip[0] + chip[1]) * n + half * (n // 2), 16)
    return ref.at[pl.ds(start, n // 2), :]


def _gather_weights(win_sh, wout_sh, wmem_sh):
    half, piece = SH_IN // 2, SH_IN // 4

    def body(a_ref, b_ref, c_ref, oa_ref, ob_ref, oc_ref, ici_send, ici_recv, d2d_send, d2d_recv):
        x, y, c = lax.axis_index("x"), lax.axis_index("y"), lax.axis_index("c")
        me, sibling = (x, y, c), (x, y, 1 - c)
        xn, yn, dg = (1 - x, y), (x, 1 - y), (1 - x, 1 - y)
        for src, out, n in ((a_ref, oa_ref, SH_IN), (b_ref, ob_ref, SH_OUT), (c_ref, oc_ref, SH_MEM)):
            out[pl.ds(pl.multiple_of((2 * x + y) * n, 16), n), :] = src[...].astype(BF16)

        def rows(chip, hf, q):
            start = pl.multiple_of((2 * chip[0] + chip[1]) * SH_IN + hf * half + q * piece, 16)
            return oa_ref.at[pl.ds(start, piece), :]

        def copy(sems, k, chip, hf, q, to):
            blk = rows(chip, hf, q)
            return pltpu.make_async_remote_copy(
                src_ref=blk, dst_ref=blk, send_sem=sems[0].at[k], recv_sem=sems[1].at[k],
                device_id=to, device_id_type=MESH)

        ici, d2d = (ici_send, ici_recv), (d2d_send, d2d_recv)
        direct = [copy(ici, 0, (x, y), c, 0, (*xn, c)), copy(ici, 1, (x, y), c, 1, (*xn, c)),
                  copy(ici, 3, (x, y), c, 0, (*yn, c)), copy(ici, 4, (x, y), c, 1, (*yn, c))]
        for cp in direct:
            cp.start()
        arrivals = [(0, xn, 0), (1, xn, 1), (3, yn, 0), (4, yn, 1), (2, dg, 1), (5, dg, 0)]
        passed = []
        for k, chip, q in arrivals:
            copy(ici, k, chip, c, q, me).wait_recv()
            if k == 0:
                passed.append(copy(ici, 5, xn, c, 0, (*yn, c)))
                passed[-1].start()
            if k == 4:
                passed.append(copy(ici, 2, yn, c, 1, (*xn, c)))
                passed[-1].start()
            passed.append(copy(d2d, k, chip, c, q, sibling))
            passed[-1].start()
        for k, chip, q in arrivals:
            copy(d2d, k, chip, 1 - c, q, me).wait_recv()
        for cp in direct + passed:
            cp.wait_send()

    vm = pl.BlockSpec(memory_space=pltpu.VMEM)
    return pl.pallas_call(
        body, name="gather_weights",
        out_shape=(_vm_sds((D_IN, D_MODEL), BF16), _vm_sds((D_MIX, D_MODEL), BF16),
                   _vm_sds((D_MODEL, 2 * W_C), BF16)),
        in_specs=[vm, vm, vm], out_specs=(vm, vm, vm),
        scratch_shapes=[pltpu.SemaphoreType.DMA((6,))] * 4,
        compiler_params=_cp(vmem_mb=40),
    )(win_sh, wout_sh, wmem_sh)


def _rope(t, cos, sa, sb, sign):
    w = t.shape[1]
    reps = w // 128
    c, a, b = (jnp.tile(v, (1, reps)) if reps > 1 else v for v in (cos, sa, sb))
    rot = pltpu.roll(t, w - 32, 1) * a + pltpu.roll(t, 32, 1) * b
    return t * c + rot if sign > 0 else t * c - rot


def _in_proj(x, winT, b_in, cos, sa, sb, wout_own, wmem_own):
    tm = 512
    spt = SEQ // tm
    n_steps = T // tm
    forward_step = n_steps // 2

    def body(x_ref, w_ref, b_ref, cos_ref, sa_ref, sb_ref, wo_in, wm_in,
             xb_ref, qa_ref, ka_ref, va_ref, bn_ref, b4_ref, b16_ref, qc_ref, z_ref, wo_ref, wm_ref,
             scr, ici_send, ici_recv, d2d_send, d2d_recv):
        i = pl.program_id(0)
        mx, my, mc = lax.axis_index("x"), lax.axis_index("y"), lax.axis_index("c")
        chips = [(1 - mx, my), (mx, 1 - my), (1 - mx, 1 - my)]
        full = ((wo_ref, SH_OUT), (wm_ref, SH_MEM))

        def copy(sems, a, j, chip_of_block, half, to):
            blk = _shard_rows(full[a][0], full[a][1], chip_of_block, half)
            return pltpu.make_async_remote_copy(
                src_ref=blk, dst_ref=blk, send_sem=sems[0].at[a, j], recv_sem=sems[1].at[a, j],
                device_id=to, device_id_type=MESH)

        ici, d2d = (ici_send, ici_recv), (d2d_send, d2d_recv)
        pairs = [(a, j, chip) for j, chip in enumerate(chips) for a in range(2)]

        @pl.when(i == 0)
        def _():
            for a, j, chip in pairs:
                copy(ici, a, j, (mx, my), mc, (*chip, mc)).start()

        @pl.when(i == forward_step)
        def _():
            for a, j, chip in pairs:
                copy(ici, a, j, chip, mc, (mx, my, mc)).wait_recv()
                copy(d2d, a, j, chip, mc, (mx, my, 1 - mc)).start()

        @pl.when(i == n_steps - 1)
        def _():
            for a, j, chip in pairs:
                copy(d2d, a, j, chip, 1 - mc, (mx, my, mc)).wait_recv()
            for a, j, chip in pairs:
                copy(ici, a, j, (mx, my), mc, (*chip, mc)).wait_send()
                copy(d2d, a, j, chip, mc, (mx, my, 1 - mc)).wait_send()

        xb = x_ref[...].astype(BF16)
        xb_ref[...] = xb
        cos_t, sa_t, sb_t = cos_ref[...], sa_ref[...], sb_ref[...]

        def proj(r0, n):
            return _dot(xb, w_ref[r0:r0 + n, :], NT) + b_ref[:, r0:r0 + n]

        def rope(t):
            return _rope(t, cos_t, sa_t, sb_t, +1)

        qa_ref[...] = (rope(proj(O_QA, W_A)) * QK_SCALE).astype(BF16)
        ka_ref[...] = rope(proj(O_KA, W_KV_A)).astype(BF16)
        va_ref[...] = proj(O_VA, W_KV_A).astype(BF16)
        qc_ref[...] = (proj(O_QC, W_C) * QK_SCALE).astype(BF16)
        z_ref[...] = proj(O_Z, D_MIX).astype(BF16)
        parts = (rope(proj(O_QB, W_B)) * QK_SCALE, rope(proj(O_KB, W_B)), proj(O_VB, W_B))
        for k, part in enumerate(parts):
            bn_ref[:, 256 * k:256 * (k + 1)] = part.astype(BF16)
            scr[2 * k] = part[:, :128]
            scr[2 * k + 1] = part[:, 128:]
        for j in range(6):
            for res in range(4):
                b4_ref[0, res, :, 128 * j:128 * (j + 1)] = scr[j, pl.ds(res, tm // 4, stride=4), :].astype(BF16)
            for res in range(16):
                b16_ref[0, res, :, 128 * j:128 * (j + 1)] = scr[j, pl.ds(res, tm // 16, stride=16), :].astype(BF16)

    tok = lambda w: pl.BlockSpec((tm, w), lambda i: (i, 0))
    tab = pl.BlockSpec((tm, 128), lambda i: (i % spt, 0))
    hbm = pl.BlockSpec(memory_space=pl.ANY)
    return pl.pallas_call(
        body, name="in_proj", grid=(n_steps,),
        in_specs=[tok(D_MODEL), _full((D_IN, D_MODEL)), _full((1, D_IN)), tab, tab, tab, hbm, hbm],
        out_specs=(tok(D_MODEL), tok(W_A), tok(W_KV_A), tok(W_KV_A), tok(768),
                   pl.BlockSpec((1, 4, tm // 4, 768), lambda i: (i // spt, 0, i % spt, 0)),
                   pl.BlockSpec((1, 16, tm // 16, 768), lambda i: (i // spt, 0, i % spt, 0)),
                   tok(W_C), tok(D_MIX), hbm, hbm),
        out_shape=(_sds((T, D_MODEL), BF16), _sds((T, W_A), BF16), _sds((T, W_KV_A), BF16), _sds((T, W_KV_A), BF16),
                   _sds((T, 768), BF16), _sds((B_LOC, 4, SEQ // 4, 768), BF16), _sds((B_LOC, 16, SEQ // 16, 768), BF16),
                   _sds((T, W_C), BF16), _sds((T, D_MIX), BF16),
                   _sds((D_MIX, D_MODEL), BF16), _sds((D_MODEL, 2 * W_C), BF16)),
        input_output_aliases={6: 9, 7: 10},
        scratch_shapes=[pltpu.VMEM((6, tm, 128), F32)] + [pltpu.SemaphoreType.DMA((2, 3))] * 4,
        compiler_params=_cp(("arbitrary",), vmem_mb=48),
    )(*_pin(x, winT, b_in, cos, sa, sb, wout_own, wmem_own))


def _mem_kv(mem, wmem):
    def body(m_ref, w_ref, mb_ref, kv_ref):
        mb = m_ref[...].astype(BF16)
        mb_ref[...] = mb
        kv_ref[...] = _dot(mb, w_ref[...], NN).astype(BF16)

    n = B_LOC * MEM_LEN
    return pl.pallas_call(
        body, name="mem_kv",
        out_shape=(_sds((n, D_MODEL), BF16), _sds((n, 2 * W_C), BF16)),
    )(*_pin(mem, wmem))


class _Part:
    def __init__(self, body, args, in_specs, out_specs, out_shape, scratch=()):
        self.body, self.args, self.in_specs, self.out_specs, self.out_shape = body, args, in_specs, out_specs, out_shape
        self.scratch = list(scratch)


def _run_parts(name, parts, semantics, vmem_mb):
    n_in = [len(p.args) for p in parts]
    n_out = [len(p.out_shape) for p in parts]
    n_scr = [len(p.scratch) for p in parts]

    def body(*refs):
        ins, outs, scr = refs[:sum(n_in)], refs[sum(n_in):sum(n_in) + sum(n_out)], refs[sum(n_in) + sum(n_out):]
        i0 = o0 = s0 = 0
        for p, ni, no, ns in zip(parts, n_in, n_out, n_scr):
            p.body(*ins[i0:i0 + ni], *outs[o0:o0 + no], *scr[s0:s0 + ns])
            i0, o0, s0 = i0 + ni, o0 + no, s0 + ns

    res = pl.pallas_call(
        body, name=name, grid=(T // QR,),
        in_specs=[sp for p in parts for sp in p.in_specs], out_specs=tuple(sp for p in parts for sp in p.out_specs),
        out_shape=tuple(sh for p in parts for sh in p.out_shape),
        scratch_shapes=[sc for p in parts for sc in p.scratch],
        compiler_params=_cp((semantics,), vmem_mb=vmem_mb),
    )(*_pin(*[a for p in parts for a in p.args]))
    out, o0 = [], 0
    for no in n_out:
        out.append(tuple(res[o0:o0 + no]))
        o0 += no
    return out


QB = 8
QR = QB * BLK


def _lane_lo():
    return lax.broadcasted_iota(jnp.int32, (1, 128), 1) < 64


def _dup_head(k2, hk, lo):
    kf = k2.astype(F32)
    r = pltpu.roll(kf, 64, 1)
    return (jnp.where(lo, kf, r) if hk == 0 else jnp.where(lo, r, kf)).astype(BF16)


def _stack_heads(pairs, lo):
    parts = []
    for x2 in pairs:
        z = jnp.zeros_like(x2)
        parts += [jnp.where(lo, x2, z), jnp.where(lo, z, x2)]
    return jnp.concatenate(parts, axis=0)


def _prev_mode(kind, nb, j):
    if kind == "mem" or nb == 1:
        return "no"
    if nb <= QB:
        return "yes" if j % nb else "no"
    return "yes" if j else "dyn"


class _Attn:
    def __init__(self, kind, nb, max_dist, gqa, qw, kvw, qcb, kcb, vcb):
        self.kind, self.nb, self.gqa, self.qw, self.kvw = kind, nb, gqa, qw, kvw
        npairs = qw // 128
        self.groups = ([(hk, [2 * hk, 2 * hk + 1]) for hk in range(npairs // 2)] if gqa
                       else [(p, [p]) for p in range(npairs)])
        self.nh = 2 * len(self.groups[0][1])
        self.cols = 128 * self.nh
        self.reach = BLK - max_dist
        self.ext_prev = kind == "band" and nb > QB
        self.q_spec = pl.BlockSpec((QR, qw), lambda g: (g, qcb))
        self.row_spec = pl.BlockSpec((QR, qw), lambda g: (g, 0))
        self.stat_spec = pl.BlockSpec((QR, 128), lambda g: (g, 0))
        if kind == "mem":
            per = SEQ // QR
            self.kv_specs = [pl.BlockSpec((MEM_LEN, kvw), lambda g: (g // per, kcb)),
                             pl.BlockSpec((MEM_LEN, kvw), lambda g: (g // per, vcb))]
        else:
            self.kv_specs = [pl.BlockSpec((QR, kvw), lambda g: (g, kcb)), pl.BlockSpec((QR, kvw), lambda g: (g, vcb))]
            if self.ext_prev:
                self.kv_specs += [pl.BlockSpec((BLK, kvw), lambda g: (jnp.maximum(g * QB - 1, 0), kcb)),
                                  pl.BlockSpec((BLK, kvw), lambda g: (jnp.maximum(g * QB - 1, 0), vcb))]

    def masks(self):
        if self.kind == "mem":
            return None
        kj = lax.broadcasted_iota(jnp.int32, (2 * BLK, self.cols), 0)
        qi = lax.broadcasted_iota(jnp.int32, (2 * BLK, self.cols), 1) & (BLK - 1)
        kj1 = lax.broadcasted_iota(jnp.int32, (BLK, self.cols), 0)
        qi1 = lax.broadcasted_iota(jnp.int32, (BLK, self.cols), 1) & (BLK - 1)
        return kj, qi, kj1 <= qi1

    def keys(self, j, gi, kc_ref, vc_ref, kp_ref, vp_ref, lo, kq, g):
        def kv(k_ref, v_ref, r):
            if self.gqa:
                return _dup_head(k_ref[r, :], gi, lo), _dup_head(v_ref[r, :], gi, lo)
            sl = slice(128 * gi, 128 * (gi + 1))
            return k_ref[r, sl], v_ref[r, sl]

        if self.kind == "mem":
            key0 = pl.multiple_of((g // (SEQ // QR)) * MEM_LEN, MEM_LEN)
            return (*kv(kc_ref, vc_ref, slice(None)), None, [(0, MEM_LEN, key0)])
        kj, qi, cur = kq
        row0 = g * QR + BLK * j
        mode = _prev_mode(self.kind, self.nb, j)
        if mode == "no":
            return (*kv(kc_ref, vc_ref, slice(BLK * j, BLK * (j + 1))), cur, [(0, BLK, pl.multiple_of(row0, BLK))])
        if mode == "yes":
            mask = jnp.logical_and(kj >= qi + self.reach, kj <= qi + BLK)
            return (*kv(kc_ref, vc_ref, slice(BLK * (j - 1), BLK * (j + 1))), mask,
                    [(0, 2 * BLK, pl.multiple_of(row0 - BLK, BLK))])
        has_prev = ((g * QB) % self.nb) > 0
        hp = has_prev.astype(jnp.int32)
        mask = jnp.logical_and(kj >= qi * hp + (self.reach * hp + BLK * (1 - hp)), kj <= qi + BLK)
        kp, vp = kv(kp_ref, vp_ref, slice(None))
        kc, vc = kv(kc_ref, vc_ref, slice(0, BLK))
        return (jnp.concatenate([kp, kc], axis=0), jnp.concatenate([vp, vc], axis=0), mask,
                [(0, BLK, pl.multiple_of(jnp.maximum(row0 - BLK, 0), BLK)), (BLK, BLK, pl.multiple_of(row0, BLK))])


def _attn_fwd(q, qcb, qw, k, kcb, v, vcb, kvw, *, kind, nb=1, max_dist=BLK, gqa=False, sinks=None):
    a = _Attn(kind, nb, max_dist, gqa, qw, kvw, qcb, kcb, vcb)

    def body(*refs):
        it = iter(refs)
        q_ref, kc_ref, vc_ref = next(it), next(it), next(it)
        kp_ref, vp_ref = (next(it), next(it)) if a.ext_prev else (None, None)
        sink_ref = next(it) if sinks is not None else None
        o_ref, lse_ref = next(it), next(it)
        g = pl.program_id(0)
        lo = _lane_lo()
        top = lax.broadcasted_iota(jnp.int32, (128, 1), 0) < 64
        rid = lax.broadcasted_iota(jnp.int32, (8, 128), 0)
        kq = a.masks()
        stats = {}

        def scores(j, gi, pairs):
            rows = slice(BLK * j, BLK * (j + 1))
            qs = _stack_heads([q_ref[rows, 128 * p:128 * (p + 1)] for p in pairs], lo)
            kk, vv, mask, _ = a.keys(j, gi, kc_ref, vc_ref, kp_ref, vp_ref, lo, kq, g)
            pieces = [slice(r0, r0 + BLK) for r0 in range(0, kk.shape[0], BLK)]
            return dict(j=j, gi=gi, pairs=pairs, rows=rows, vv=vv, mask=mask, pieces=pieces,
                        ss=[_dot(kk[r], qs, NT) for r in pieces])

        def softmax(c):
            gi, mask = c["gi"], c["mask"]
            ss = [s if mask is None else jnp.where(mask[r], s, NEG) for r, s in zip(c["pieces"], c.pop("ss"))]
            m = jnp.max(ss[0], axis=0, keepdims=True)
            for s in ss[1:]:
                m = jnp.maximum(m, jnp.max(s, axis=0, keepdims=True))
            if sink_ref is not None:
                sk = jnp.concatenate([jnp.full((1, 128), sink_ref[0, a.nh * gi + i], F32) for i in range(a.nh)], axis=1)
                m = jnp.maximum(m, sk)
            ps = [jnp.exp(s - m) for s in ss]
            l = sum(jnp.sum(p, axis=0, keepdims=True) for p in ps)
            if sink_ref is not None:
                l = l + jnp.exp(sk - m)
            c["ps"] = [p.astype(BF16) for p in ps]
            c["l"], c["lse"] = l, m + jnp.log(l)

        def outputs(c):
            j, gi, rows = c["j"], c["gi"], c["rows"]
            ot = sum(_dot(c["vv"][r], p, TN) for r, p in zip(c["pieces"], c["ps"]))
            ot = ot * pl.reciprocal(c["l"], approx=True)
            for i, p in enumerate(c["pairs"]):
                o2t = jnp.where(top, ot[:, 256 * i:256 * i + 128], ot[:, 256 * i + 128:256 * i + 256])
                o_ref[rows, 128 * p:128 * (p + 1)] = o2t.T.astype(BF16)
            stat = stats.get(j, jnp.zeros((8, 128), F32))
            for i in range(a.nh):
                stat = jnp.where(rid == a.nh * gi + i, c["lse"][:, 128 * i:128 * (i + 1)], stat)
            stats[j] = stat
            if gi == a.groups[-1][0]:
                lse_ref[rows, :] = jnp.concatenate([stats.pop(j), jnp.zeros((120, 128), F32)], axis=0).T

        chains = [(j, gi, pairs) for j in range(QB) for gi, pairs in a.groups]
        live = {}
        for t in range(len(chains) + 2):
            if t < len(chains):
                live[t] = scores(*chains[t])
            if 0 <= t - 1 < len(chains):
                softmax(live[t - 1])
            if 0 <= t - 2 < len(chains):
                outputs(live.pop(t - 2))


    args = [q, k, v] + ([k, v] if a.ext_prev else [])
    in_specs = [a.q_spec] + a.kv_specs
    if sinks is not None:
        args.append(sinks)
        in_specs.append(pl.BlockSpec(memory_space=pltpu.SMEM))
    return _Part(body, args, in_specs, [a.row_spec, a.stat_spec], [_sds((T, qw), BF16), _sds((T, 128), F32)])


def _attn_bwd(q, qcb, qw, k, kcb, v, vcb, kvw, do, lse, dl, *, kind, nb=1, max_dist=BLK, gqa=False, sinkv=None):
    a = _Attn(kind, nb, max_dist, gqa, qw, kvw, qcb, kcb, vcb)

    def body(*refs):
        it = iter(refs)
        q_ref, kc_ref, vc_ref = next(it), next(it), next(it)
        kp_ref, vp_ref = (next(it), next(it)) if a.ext_prev else (None, None)
        do_ref, lse_ref, dl_ref = next(it), next(it), next(it)
        sinkv_ref = next(it) if sinkv is not None else None
        dq_ref = next(it)
        if kind == "mem":
            dkv_ref = next(it)
        else:
            dk_out, dv_out = next(it), next(it)
        dsink_ref = next(it) if sinkv is not None else None
        if kind != "mem":
            dk_ref, dv_ref, stage_k, stage_v, flush_sem = next(it), next(it), next(it), next(it), next(it)
        g = pl.program_id(0)
        lo = _lane_lo()
        top = lax.broadcasted_iota(jnp.int32, (128, 1), 0) < 64

        @pl.when(g == 0)
        def _():
            if kind == "mem":
                dkv_ref[...] = jnp.zeros_like(dkv_ref)
            else:
                dk_ref[...] = jnp.zeros_like(dk_ref)
                dv_ref[...] = jnp.zeros_like(dv_ref)
            if dsink_ref is not None:
                dsink_ref[...] = jnp.zeros_like(dsink_ref)

        kq = a.masks()
        stats_t = {}

        def first_matmuls(j, gi, pairs):
            rows = slice(BLK * j, BLK * (j + 1))
            if j not in stats_t:
                stats_t[j] = (lse_ref[rows, :].T, dl_ref[rows, :].T)
            lse_t, dl_t = stats_t[j]
            heads = [a.nh * gi + i for i in range(a.nh)]
            c = dict(rows=rows, gi=gi, pairs=pairs)
            c["qs"] = _stack_heads([q_ref[rows, 128 * p:128 * (p + 1)] for p in pairs], lo)
            c["dos"] = _stack_heads([do_ref[rows, 128 * p:128 * (p + 1)] for p in pairs], lo)
            c["lse_row"] = jnp.concatenate([lse_t[h:h + 1, :] for h in heads], axis=1)
            c["dl_row"] = jnp.concatenate([dl_t[h:h + 1, :] for h in heads], axis=1)
            c["kk"], vv, c["mask"], c["dests"] = a.keys(j, gi, kc_ref, vc_ref, kp_ref, vp_ref, lo, kq, g)
            c["s"] = _dot(c["kk"], c["qs"], NT)
            c["dp"] = _dot(vv, c["dos"], NT)
            return c

        def elementwise(c):
            s = c.pop("s")
            if c["mask"] is not None:
                s = jnp.where(c["mask"], s, NEG)
            p = jnp.exp(s - c["lse_row"])
            c["ds"] = (p * (c.pop("dp") - c["dl_row"])).astype(BF16)
            c["p"] = p.astype(BF16)

        def last_matmuls(c):
            gi, rows = c["gi"], c["rows"]
            dqt = _dot(c["kk"], c["ds"], TN)
            ck = _dot(c["ds"], c["qs"], NN)
            cv = _dot(c["p"], c["dos"], NN)
            if gqa:
                sel = lo if gi == 0 else jnp.logical_not(lo)
                ck = jnp.where(sel, ck + pltpu.roll(ck, 64, 1), 0.0)
                cv = jnp.where(sel, cv + pltpu.roll(cv, 64, 1), 0.0)
                kcols = slice(0, 128)
            else:
                kcols = slice(128 * gi, 128 * (gi + 1))
            for r0, nr, key0 in c["dests"]:
                krows = pl.ds(key0, nr)
                if kind == "mem":
                    dkv_ref[krows, kcols] += ck[r0:r0 + nr]
                    dkv_ref[krows, slice(kvw + kcols.start, kvw + kcols.stop)] += cv[r0:r0 + nr]
                else:
                    dk_ref[krows, kcols] += ck[r0:r0 + nr]
                    dv_ref[krows, kcols] += cv[r0:r0 + nr]
            for i, p in enumerate(c["pairs"]):
                dq2t = jnp.where(top, dqt[:, 256 * i:256 * i + 128], dqt[:, 256 * i + 128:256 * i + 256])
                dq_ref[rows, 128 * p:128 * (p + 1)] = dq2t.T.astype(BF16)

        chains = [(j, gi, pairs) for j in range(QB) for gi, pairs in a.groups]
        live = {}
        for t in range(len(chains) + 2):
            if t < len(chains):
                live[t] = first_matmuls(*chains[t])
            if 0 <= t - 1 < len(chains):
                elementwise(live[t - 1])
            if 0 <= t - 2 < len(chains):
                last_matmuls(live.pop(t - 2))
        if dsink_ref is not None:
            ps = jnp.exp(sinkv_ref[...] - lse_ref[...]) * dl_ref[...]
            dsink_ref[...] += jnp.sum(ps, axis=0, keepdims=True)
        if kind != "mem":
            n_steps = T // QR

            def flush(step):
                rows = pl.ds(pl.multiple_of(step * QR, QR), QR)
                out = []
                for acc, stage, dst, i in ((dk_ref, stage_k, dk_out, 0), (dv_ref, stage_v, dv_out, 1)):
                    stage[...] = acc[rows, :].astype(BF16)
                    out.append(pltpu.make_async_copy(stage, dst.at[rows, :], flush_sem.at[i]))
                return out

            def flushed(step):
                rows = pl.ds(pl.multiple_of(step * QR, QR), QR)
                return [pltpu.make_async_copy(stage, dst.at[rows, :], flush_sem.at[i])
                        for stage, dst, i in ((stage_k, dk_out, 0), (stage_v, dv_out, 1))]

            @pl.when(g >= 2)
            def _():
                for cp in flushed(g - 2):
                    cp.wait()

            @pl.when(g >= 1)
            def _():
                for cp in flush(g - 1):
                    cp.start()

            @pl.when(g == n_steps - 1)
            def _():
                for cp in flushed(g - 1):
                    cp.wait()
                for cp in flush(g):
                    cp.start()
                for cp in flushed(g):
                    cp.wait()

    args = [q, k, v] + ([k, v] if a.ext_prev else []) + [do, lse, dl]
    in_specs = [a.q_spec] + a.kv_specs + [a.row_spec, a.stat_spec, a.stat_spec]
    if sinkv is not None:
        args.append(sinkv)
        in_specs.append(_full((1, 128)))
    out_shape = [_sds((T, qw), BF16)]
    out_specs = [a.row_spec]
    scratch = []
    if kind == "mem":
        out_shape.append(_sds((B_LOC * MEM_LEN, 2 * kvw), F32))
        out_specs.append(pl.BlockSpec((B_LOC * MEM_LEN, 2 * kvw), lambda g: (0, 0), pipeline_mode=pl.Buffered(1)))
    else:
        out_shape += [_sds((T, kvw), BF16)] * 2
        out_specs += [pl.BlockSpec(memory_space=pl.ANY)] * 2
        scratch = [pltpu.VMEM((T, kvw), F32)] * 2 + [pltpu.VMEM((QR, kvw), BF16)] * 2 + [pltpu.SemaphoreType.DMA((2,))]
    if sinkv is not None:
        out_shape.append(_sds((1, 128), F32))
        out_specs.append(_full((1, 128)))
    return _Part(body, args, in_specs, out_specs, out_shape, scratch)


def _dot2(v, w_ref):
    hi = v.astype(BF16)
    lo = (v - hi.astype(F32)).astype(BF16)
    return _dot(hi, w_ref[...], NN) + _dot(lo, w_ref[...], NN)


def _middle(oa, o1, l1, o4, l4, o16, l16, oc, z, x, tgt, g_br, ln_g, ln_b, wout, spread4, gather4, gather8):
    tm = 512
    spt = SEQ // tm

    def body(oa_ref, o1_ref, l1_ref, o4_ref, l4_ref, o16_ref, l16_ref, oc_ref, z_ref, x_ref, t_ref,
             g_ref, lg_ref, lb_ref, w_ref, sp4_ref, ga4_ref, ga8_ref,
             du_ref, dz_ref, doa_ref, dla_ref,
             dobn_ref, lsen_ref, dlbn_ref, dob4_ref, lse4_ref, dlb4_ref, dob16_ref, lse16_ref, dlb16_ref,
             doc_ref, dlc_ref, acc_ref, gout_ref, scr):
        i = pl.program_id(0)

        @pl.when(i == 0)
        def _():
            acc_ref[...] = jnp.zeros_like(acc_ref)
            gout_ref[...] = jnp.zeros_like(gout_ref)

        for res in range(4):
            rows = pl.ds(res, tm // 4, stride=4)
            for j in range(2):
                scr[j, rows, :] = o4_ref[0, res, :, 128 * j:128 * (j + 1)].astype(F32)
            scr[2, rows, :] = l4_ref[0, res]
        for res in range(16):
            rows = pl.ds(res, tm // 16, stride=16)
            for j in range(2):
                scr[3 + j, rows, :] = o16_ref[0, res, :, 128 * j:128 * (j + 1)].astype(F32)
            scr[5, rows, :] = l16_ref[0, res]
        inv_d = 1.0 / D_MODEL
        gb, lg, lb = g_ref[...], lg_ref[...], lb_ref[...]

        def rms(o):
            r = lax.rsqrt(jnp.sum(o * o, axis=1, keepdims=True) * (1.0 / o.shape[1]) + RMS_EPS)
            return o * r, r

        def rms_bwd(dn_, n_, r):
            return r * (dn_ - n_ * (jnp.sum(dn_ * n_, axis=1, keepdims=True) * (1.0 / n_.shape[1])))

        def forward(rs):
            o4v = jnp.concatenate([scr[0, rs, :], scr[1, rs, :]], axis=1)
            o16v = jnp.concatenate([scr[3, rs, :], scr[4, rs, :]], axis=1)
            l1v, l4v, l16v = l1_ref[rs, :], scr[2, rs, :], scr[5, rs, :]
            mx = jnp.maximum(jnp.maximum(l1v, l4v), l16v)
            e1, e4, e16 = jnp.exp(l1v - mx), jnp.exp(l4v - mx), jnp.exp(l16v - mx)
            ssum = e1 + e4 + e16
            inv = 1.0 / ssum
            c = dict(rs=rs, lse_b=mx + jnp.log(ssum))
            c["ob"] = (_dot2(e1 * inv, sp4_ref) * o1_ref[rs, :].astype(F32) + _dot2(e4 * inv, sp4_ref) * o4v
                       + _dot2(e16 * inv, sp4_ref) * o16v)
            c["oa"], c["oc"] = oa_ref[rs, :].astype(F32), oc_ref[rs, :].astype(F32)
            na, c["ra"] = rms(c["oa"])
            nb_, c["rb"] = rms(c["ob"])
            nc, c["rc"] = rms(c["oc"])
            c["n"] = jnp.concatenate([na, nb_, nc], axis=1)
            c["zf"] = z_ref[rs, :].astype(F32)
            c["sig"] = 1.0 / (1.0 + jnp.exp(-c["zf"]))
            c["sz"] = c["zf"] * c["sig"]
            c["yb"] = (c["n"] * gb * c["sz"]).astype(BF16)
            c["y2"] = _dot(c["yb"], w_ref[...], NN)
            return c

        def norm(c):
            rs = c["rs"]
            u = ALPHA * x_ref[rs, :] + c.pop("y2")
            mu = jnp.sum(u, axis=1, keepdims=True) * inv_d
            uc = u - mu
            rstd = lax.rsqrt(jnp.sum(uc * uc, axis=1, keepdims=True) * inv_d + LN_EPS)
            xh = uc * rstd
            diff = xh * lg + lb - t_ref[rs, :]
            acc_ref[0:1, :] += jnp.sum(diff * diff, axis=0, keepdims=True) * (0.5 * inv_d)
            dout = diff * inv_d
            acc_ref[2:3, :] += jnp.sum(dout * xh, axis=0, keepdims=True)
            acc_ref[3:4, :] += jnp.sum(dout, axis=0, keepdims=True)
            dxh = dout * lg
            du = rstd * (dxh - jnp.sum(dxh, axis=1, keepdims=True) * inv_d
                         - xh * (jnp.sum(dxh * xh, axis=1, keepdims=True) * inv_d))
            dub = du.astype(BF16)
            du_ref[rs, :] = dub
            c["dy"] = _dot(dub, w_ref[...], NT)
            gout_ref[...] += _dot(c.pop("yb"), dub, TN)

        def backward(c):
            rs, n, dy, zf, sig = c["rs"], c["n"], c["dy"], c["zf"], c["sig"]
            t1 = dy * c["sz"]
            acc_ref[1:2, :] += jnp.sum(t1 * n, axis=0, keepdims=True)
            dn = t1 * gb
            dz_ref[rs, :] = (dy * n * gb * (sig * (1.0 + zf * (1.0 - sig)))).astype(BF16)
            doa = rms_bwd(dn[:, :W_A], n[:, :W_A], c["ra"])
            dob = rms_bwd(dn[:, W_A:W_A + W_B], n[:, W_A:W_A + W_B], c["rb"])
            doc = rms_bwd(dn[:, W_A + W_B:], n[:, W_A + W_B:], c["rc"])
            doa_ref[rs, :] = doa.astype(BF16)
            dla_ref[rs, :] = _dot2(doa * c["oa"], ga8_ref)
            doc_ref[rs, :] = doc.astype(BF16)
            dlc_ref[rs, :] = _dot2(doc * c["oc"], ga4_ref)
            dobn_ref[rs, :] = dob.astype(BF16)
            lsen_ref[rs, :] = c["lse_b"]
            dlbn_ref[rs, :] = _dot2(dob * c["ob"], ga4_ref)
            scr[0, rs, :] = dob[:, :128]
            scr[1, rs, :] = dob[:, 128:]

        halves = [slice(h * (tm // 2), (h + 1) * (tm // 2)) for h in range(2)]
        live = {}
        for t in range(len(halves) + 2):
            if t < len(halves):
                live[t] = forward(halves[t])
            if 0 <= t - 1 < len(halves):
                norm(live[t - 1])
            if 0 <= t - 2 < len(halves):
                backward(live.pop(t - 2))
        for j in range(2):
            sl = slice(128 * j, 128 * (j + 1))
            for res in range(4):
                dob4_ref[0, res, :, sl] = scr[j, pl.ds(res, tm // 4, stride=4), :].astype(BF16)
            for res in range(16):
                dob16_ref[0, res, :, sl] = scr[j, pl.ds(res, tm // 16, stride=16), :].astype(BF16)
        for res in range(4):
            rows = pl.ds(res, tm // 4, stride=4)
            lse4_ref[0, res] = lsen_ref[rows, :]
            dlb4_ref[0, res] = dlbn_ref[rows, :]
        for res in range(16):
            rows = pl.ds(res, tm // 16, stride=16)
            lse16_ref[0, res] = lsen_ref[rows, :]
            dlb16_ref[0, res] = dlbn_ref[rows, :]


    tok = lambda w: pl.BlockSpec((tm, w), lambda i: (i, 0))
    p4 = lambda w: pl.BlockSpec((1, 4, tm // 4, w), lambda i: (i // spt, 0, i % spt, 0))
    p16 = lambda w: pl.BlockSpec((1, 16, tm // 16, w), lambda i: (i // spt, 0, i % spt, 0))
    s4 = lambda w, dt: _sds((B_LOC, 4, SEQ // 4, w), dt)
    s16 = lambda w, dt: _sds((B_LOC, 16, SEQ // 16, w), dt)
    row = _full((1, D_MODEL))
    return pl.pallas_call(
        body, name="middle", grid=(T // tm,),
        in_specs=[tok(W_A), tok(W_B), tok(128), p4(W_B), p4(128), p16(W_B), p16(128), tok(W_C), tok(D_MIX),
                  tok(D_MODEL), tok(D_MODEL), row, row, row, _full((D_MIX, D_MODEL)),
                  _full((128, W_B)), _full((W_B, 128)), _full((W_A, 128))],
        out_specs=(tok(D_MODEL), tok(D_MIX), tok(W_A), tok(128),
                   tok(W_B), tok(128), tok(128), p4(W_B), p4(128), p4(128), p16(W_B), p16(128), p16(128),
                   tok(W_C), tok(128), _full((8, D_MODEL)), _full((D_MIX, D_MODEL))),
        out_shape=(_sds((T, D_MODEL), BF16), _sds((T, D_MIX), BF16),
                   _sds((T, W_A), BF16), _sds((T, 128), F32),
                   _sds((T, W_B), BF16), _sds((T, 128), F32), _sds((T, 128), F32),
                   s4(W_B, BF16), s4(128, F32), s4(128, F32), s16(W_B, BF16), s16(128, F32), s16(128, F32),
                   _sds((T, W_C), BF16), _sds((T, 128), F32), _sds((8, D_MODEL), F32),
                   _sds((D_MIX, D_MODEL), F32)),
        scratch_shapes=[pltpu.VMEM((6, tm, 128), F32)],
        compiler_params=_cp(("arbitrary",), vmem_mb=56),
    )(*_pin(oa, o1, l1, o4, l4, o16, l16, oc, z, x, tgt, g_br, ln_g, ln_b, wout, spread4, gather4, gather8))


class _ReduceScatter:
    def __init__(self, shapes):
        self.shapes = shapes

    def scratch_shapes(self):
        out = []
        for n, w in self.shapes:
            h = n // 2
            out += [pltpu.VMEM((4, h, w), F32), pltpu.VMEM((4, h, w), F32), pltpu.VMEM((3, h, w), BF16),
                    pltpu.VMEM((3, h, w), BF16), pltpu.VMEM((h, w), F32)]
        na = len(self.shapes)
        dma = pltpu.SemaphoreType.DMA
        return out + [dma((na, 4)), dma((na, 4)), dma((na, 4)), dma((na, 3)), dma((na, 3)), dma((na,)), dma((na,)),
                      dma((na,))]

    def bind(self, g_refs, r_refs, scratch):
        na = len(self.shapes)
        bufs = [scratch[5 * a:5 * a + 5] for a in range(na)]
        mine, sib, stage, land, tot = (tuple(b[i] for b in bufs) for i in range(5))
        loc_sem, s1_send, s1_recv, s2_send, s2_recv, s3_send, s3_recv, st_sem = scratch[5 * na:5 * na + 8]
        x, y, c = lax.axis_index("x"), lax.axis_index("y"), lax.axis_index("c")
        me, sibling = (x, y, c), (x, y, 1 - c)
        my_chip = 2 * x + y
        chips = [(1 - x, y), (x, 1 - y), (1 - x, 1 - y)]
        order = [2 * chip[0] + chip[1] for chip in chips] + [my_chip]

        def rows(a, k, half):
            n = self.shapes[a][0]
            return pl.ds(pl.multiple_of(k * n + half * (n // 2), 8), n // 2)

        def load(a, k):
            return pltpu.make_async_copy(g_refs[a].at[rows(a, k, c), :], mine[a].at[k], loc_sem.at[a, k])

        def s1(a, k, half):
            return pltpu.make_async_remote_copy(
                src_ref=g_refs[a].at[rows(a, k, half), :], dst_ref=sib[a].at[k],
                send_sem=s1_send.at[a, k], recv_sem=s1_recv.at[a, k], device_id=sibling, device_id_type=MESH)

        def s2(a, j, to):
            return pltpu.make_async_remote_copy(
                src_ref=stage[a].at[j], dst_ref=land[a].at[j], send_sem=s2_send.at[a, j], recv_sem=s2_recv.at[a, j],
                device_id=to, device_id_type=MESH)

        def s3(a, half, to):
            return pltpu.make_async_remote_copy(
                src_ref=tot[a], dst_ref=r_refs[a].at[rows(a, 0, half), :], send_sem=s3_send.at[a],
                recv_sem=s3_recv.at[a], device_id=to, device_id_type=MESH)

        def store(a):
            return pltpu.make_async_copy(tot[a], r_refs[a].at[rows(a, 0, c), :], st_sem.at[a])

        def start():
            for k in order:
                for a in range(na):
                    load(a, k).start()
                    s1(a, k, 1 - c).start()

        def exchange():
            for j, chip in enumerate(chips):
                k = order[j]
                for a in range(na):
                    load(a, k).wait()
                    s1(a, k, c).wait_recv()
                    stage[a][j] = (mine[a][k] + sib[a][k]).astype(BF16)
                    s2(a, j, (*chip, c)).start()
            for a in range(na):
                load(a, my_chip).wait()
                s1(a, my_chip, c).wait_recv()
                tot[a][...] = mine[a][my_chip] + sib[a][my_chip]

        def finish():
            for a in range(na):
                t = tot[a][...]
                for j in range(3):
                    s2(a, j, me).wait_recv()
                    t = t + land[a][j].astype(F32)
                tot[a][...] = t
                s3(a, c, sibling).start()
                store(a).start()

        def drain():
            for a in range(na):
                s3(a, 1 - c, me).wait_recv()
                store(a).wait()
            for a in range(na):
                for k in order:
                    s1(a, k, 1 - c).wait_send()
                for j, chip in enumerate(chips):
                    s2(a, j, (*chip, c)).wait_send()
                s3(a, c, sibling).wait_send()

        return start, exchange, finish, drain

    def part(self, grads, steps):
        def body(*refs):
            na = len(self.shapes)
            i = pl.program_id(0)
            for step, phase in zip(steps, self.bind(refs[:na], refs[na:2 * na], refs[2 * na:])):
                pl.when(i == step)(phase)

        hbm = pl.BlockSpec(memory_space=pl.ANY)
        return _Part(body, list(grads), [hbm] * len(grads), [hbm] * len(grads),
                     [_sds((n, w), F32) for n, w in self.shapes], self.scratch_shapes())


def _dh_dx(dqa, dka, dva, dqn, dkn, dvn, dq4, dk4, dv4, dq16, dk16, dv16, dqc, dz, du, xb, cos, sa, sb, winT):
    tm = 512
    spt = SEQ // tm

    def body(dqa_ref, dka_ref, dva_ref, dqn_ref, dkn_ref, dvn_ref, dq4_ref, dk4_ref, dv4_ref,
             dq16_ref, dk16_ref, dv16_ref, dqc_ref, dz_ref, du_ref, xb_ref, cos_ref, sa_ref, sb_ref, w_ref,
             gx_ref, db_ref, gin_ref, dh_ref, scr):
        i = pl.program_id(0)

        @pl.when(i == 0)
        def _():
            db_ref[...] = jnp.zeros_like(db_ref)
            gin_ref[...] = jnp.zeros_like(gin_ref)

        cos_t, sa_t, sb_t = cos_ref[...], sa_ref[...], sb_ref[...]

        def rope_t(t):
            return _rope(t, cos_t, sa_t, sb_t, -1)

        def put(r0, val):
            n = val.shape[1]
            dh_ref[:, r0:r0 + n] = val.astype(BF16)
            db_ref[:, r0:r0 + n] += jnp.sum(val, axis=0, keepdims=True)

        put(O_QA, rope_t(dqa_ref[...].astype(F32)) * QK_SCALE)
        put(O_KA, rope_t(dka_ref[...].astype(F32)))
        put(O_VA, dva_ref[...].astype(F32))
        put(O_QC, dqc_ref[...].astype(F32) * QK_SCALE)
        put(O_Z, dz_ref[...].astype(F32))
        for k, (n_ref, r4, r16) in enumerate(((dqn_ref, dq4_ref, dq16_ref), (dkn_ref, dk4_ref, dk16_ref),
                                               (dvn_ref, dv4_ref, dv16_ref))):
            for j in range(2):
                sl = slice(128 * j, 128 * (j + 1))
                scr[2 * k + j] = n_ref[:, sl].astype(F32)
                for res in range(4):
                    scr[2 * k + j, pl.ds(res, tm // 4, stride=4), :] += r4[0, res, :, sl].astype(F32)
                for res in range(16):
                    scr[2 * k + j, pl.ds(res, tm // 16, stride=16), :] += r16[0, res, :, sl].astype(F32)
        cat = lambda a: jnp.concatenate([scr[a], scr[a + 1]], axis=1)
        put(O_QB, rope_t(cat(0)) * QK_SCALE)
        put(O_KB, rope_t(cat(2)))
        put(O_VB, cat(4))
        gx_ref[...] = _dot(dh_ref[...], w_ref[...], NN) + ALPHA * du_ref[...].astype(F32)
        gin_ref[...] += _dot(dh_ref[...], xb_ref[...], TN)

    tok = lambda w: pl.BlockSpec((tm, w), lambda i: (i, 0))
    tab = pl.BlockSpec((tm, 128), lambda i: (i % spt, 0))
    p4 = pl.BlockSpec((1, 4, tm // 4, W_B), lambda i: (i // spt, 0, i % spt, 0))
    p16 = pl.BlockSpec((1, 16, tm // 16, W_B), lambda i: (i // spt, 0, i % spt, 0))
    once = lambda shape: pl.BlockSpec(shape, lambda i: (0, 0), pipeline_mode=pl.Buffered(1))
    return pl.pallas_call(
        body, name="dh_dx", grid=(T // tm,),
        in_specs=[tok(W_A), tok(W_KV_A), tok(W_KV_A), tok(W_B), tok(W_B), tok(W_B), p4, p4, p4, p16, p16, p16,
                  tok(W_C), tok(D_MIX), tok(D_MODEL), tok(D_MODEL), tab, tab, tab, once((D_IN, D_MODEL))],
        out_specs=(tok(D_MODEL), _full((1, D_IN)), once((D_IN, D_MODEL))),
        out_shape=(_sds((T, D_MODEL), F32), _sds((1, D_IN), F32), _sds((D_IN, D_MODEL), F32)),
        scratch_shapes=[pltpu.VMEM((tm, D_IN), BF16), pltpu.VMEM((6, tm, 128), F32)],
        compiler_params=_cp(("arbitrary",), vmem_mb=56),
    )(*_pin(dqa, dka, dva, dqn, dkn, dvn, dq4, dk4, dv4, dq16, dk16, dv16, dqc, dz, du, xb, cos, sa, sb, winT))


def _tn_matmul(name, a, b, bm, bt):
    n, m_all = a.shape
    n_cols = b.shape[1]

    def body(a_ref, b_ref, o_ref):
        @pl.when(pl.program_id(1) == 0)
        def _():
            o_ref[...] = jnp.zeros_like(o_ref)

        o_ref[...] += _dot(a_ref[...].astype(BF16), b_ref[...].astype(BF16), TN)

    return pl.pallas_call(
        body, name=name, grid=(m_all // bm, n // bt),
        in_specs=[pl.BlockSpec((bt, bm), lambda m, t: (t, m)), pl.BlockSpec((bt, n_cols), lambda m, t: (t, 0))],
        out_specs=pl.BlockSpec((bm, n_cols), lambda m, t: (m, 0)),
        out_shape=_sds((m_all, n_cols), F32),
        compiler_params=_cp(("parallel", "arbitrary"), vmem_mb=48),
    )(*_pin(a, b))


def _reduce_grads(g_in, acc, dbin, dsink):
    rs = _ReduceScatter([(SH_IN, D_MODEL)])

    def body(g_ref, acc_ref, dbin_ref, dsink_ref, r_ref, sv_ref, sv_mine, sv_all, sv_send, sv_recv, *rs_scratch):
        x, y, c = lax.axis_index("x"), lax.axis_index("y"), lax.axis_index("c")
        chips = [(1 - x, y), (x, 1 - y), (1 - x, 1 - y)]
        start, exchange, finish, drain = rs.bind((g_ref,), (r_ref,), rs_scratch)
        start()

        sv_mine[...] = jnp.zeros_like(sv_mine)
        sv_mine[0:4, 0:D_MODEL] = acc_ref[0:4, :]
        sv_mine[4:5, 0:D_IN] = dbin_ref[...]
        sv_mine[5:6, 0:128] = dsink_ref[...]
        my_dev = 4 * x + 2 * y + c
        others = [(x, y, 1 - c)] + [(*chip, cc) for chip in chips for cc in (c, 1 - c)]

        def sv_copy(j, to):
            return pltpu.make_async_remote_copy(
                src_ref=sv_mine, dst_ref=sv_all.at[my_dev], send_sem=sv_send.at[j], recv_sem=sv_recv.at[j],
                device_id=to, device_id_type=MESH)

        sv_sends = [sv_copy(j, to) for j, to in enumerate(others)]
        for cp in sv_sends:
            cp.start()
        exchange()
        finish()
        sv_all[my_dev] = sv_mine[...]
        for j in range(7):
            sv_copy(j, (x, y, c)).wait_recv()
        tot = sv_all[0]
        for d in range(1, 8):
            tot = tot + sv_all[d]
        sv_ref[...] = tot
        drain()
        for cp in sv_sends:
            cp.wait_send()

    vm = pl.BlockSpec(memory_space=pltpu.VMEM)
    hbm = pl.BlockSpec(memory_space=pl.ANY)
    return pl.pallas_call(
        body, name="reduce_grads",
        out_shape=(_sds((SH_IN, D_MODEL), F32), _vm_sds((8, SV_W), F32)),
        in_specs=[hbm, vm, vm, vm], out_specs=(hbm, vm),
        scratch_shapes=[pltpu.VMEM((8, SV_W), F32), pltpu.VMEM((8, 8, SV_W), F32),
                        pltpu.SemaphoreType.DMA((7,)), pltpu.SemaphoreType.DMA((7,))] + rs.scratch_shapes(),
        compiler_params=_cp(vmem_mb=40),
    )(pltpu.with_memory_space_constraint(g_in, pltpu.HBM), acc, dbin, dsink)


def _adamw(name, w, g, m, v, rows=None, copy_g=False):
    shape = w.shape
    rows = shape[0] if rows is None else rows
    n_out = 4 if copy_g else 3

    def body(w_ref, g_ref, m_ref, v_ref, d_ref, nm_ref, nv_ref, *go_ref):
        gv = g_ref[...]
        if copy_g:
            go_ref[0][...] = gv
        nm = ADAM_B1 * m_ref[...] + (1.0 - ADAM_B1) * gv
        nv = ADAM_B2 * v_ref[...] + (1.0 - ADAM_B2) * (gv * gv)
        m_hat = nm / (1.0 - ADAM_B1 ** ADAM_STEP)
        v_hat = nv / (1.0 - ADAM_B2 ** ADAM_STEP)
        d_ref[...] = -ADAM_LR * (m_hat / (jnp.sqrt(v_hat) + ADAM_EPS) + ADAM_WD * w_ref[...])
        nm_ref[...] = nm
        nv_ref[...] = nv

    spec = pl.BlockSpec((rows, shape[1]), lambda i: (i, 0))
    return pl.pallas_call(
        body, name=name, grid=(shape[0] // rows,), in_specs=[spec] * 4, out_specs=(spec,) * n_out,
        out_shape=(_sds(shape, F32),) * n_out, compiler_params=_cp(("parallel",)),
    )(*_pin(w, g, m, v))


def _adamw_small(sv, ws, ms, vs):
    where = ((4, D_IN, 1.0), (5, 8, -1.0), (1, D_MIX, 1.0), (2, D_MODEL, 1.0), (3, D_MODEL, 1.0))

    def body(sv_ref, *refs):
        ins, outs = refs[:15], refs[15:]
        for p, (row, width, sign) in enumerate(where):
            gv = sign * sv_ref[row:row + 1, 0:width]
            w_ref, m_ref, v_ref = ins[p], ins[5 + p], ins[10 + p]
            nm = ADAM_B1 * m_ref[...] + (1.0 - ADAM_B1) * gv
            nv = ADAM_B2 * v_ref[...] + (1.0 - ADAM_B2) * (gv * gv)
            m_hat = nm / (1.0 - ADAM_B1 ** ADAM_STEP)
            v_hat = nv / (1.0 - ADAM_B2 ** ADAM_STEP)
            outs[4 * p][...] = gv
            outs[4 * p + 1][...] = -ADAM_LR * (m_hat / (jnp.sqrt(v_hat) + ADAM_EPS) + ADAM_WD * w_ref[...])
            outs[4 * p + 2][...] = nm
            outs[4 * p + 3][...] = nv

    res = pl.pallas_call(
        body, name="adamw_small", out_shape=tuple(_vm_sds(w.shape, F32) for w in ws for _ in range(4)),
    )(sv, *ws, *ms, *vs)
    return [tuple(res[4 * p:4 * p + 4]) for p in range(5)]


def _rope_tables():
    pos = jnp.arange(SEQ, dtype=F32)
    inv = ROPE_THETA ** (-jnp.arange(0, 64, 2, dtype=F32) / 64)
    ang = pos[:, None] * inv[None, :]
    cos, sin = lax.optimization_barrier((jnp.cos(ang), jnp.sin(ang)))
    cos, sin = jnp.tile(cos, (1, 4)), jnp.tile(sin, (1, 4))
    low = (jnp.arange(128) % 64) < 32
    return cos, jnp.where(low, -sin, 0.0), jnp.where(low, 0.0, sin)


def _local_step(x2, mem2, tgt2, winT, wout, wmem, b_in, sinks, g_branch, ln_gain, ln_bias):
    cos, sa, sb = _rope_tables()
    sinkv = jnp.pad(sinks, ((0, 0), (0, 120)))
    head_of_lane = jnp.arange(512)[None, :] // 64
    gather8 = (head_of_lane.T == jnp.arange(128)[None, :]).astype(BF16)
    gather4 = gather8[:W_B]
    spread4 = gather4.T

    xb, qa, ka, va, bn, b4, b16, qc, z, wout, wmem = _in_proj(x2, winT, b_in, cos, sa, sb, wout, wmem)
    memb, mkv = _mem_kv(mem2, wmem)
    b4f, b16f = b4.reshape(T, 768), b16.reshape(T, 768)

    swa = dict(kind="band", nb=SEQ // BLK, max_dist=BLK - 1, gqa=True)
    dil = (dict(kind="band", nb=SEQ // BLK), dict(kind="band", nb=SEQ // 4 // BLK), dict(kind="band", nb=1))
    (oa, lse_a), (o1, l1), (o4, l4), (o16, l16), (oc, lse_c) = _run_parts("attn_fwd", [
        _attn_fwd(qa, 0, W_A, ka, 0, va, 0, W_KV_A, sinks=sinks, **swa),
        _attn_fwd(bn, 0, W_B, bn, 1, bn, 2, W_B, **dil[0]),
        _attn_fwd(b4f, 0, W_B, b4f, 1, b4f, 2, W_B, **dil[1]),
        _attn_fwd(b16f, 0, W_B, b16f, 1, b16f, 2, W_B, **dil[2]),
        _attn_fwd(qc, 0, W_C, mkv, 0, mkv, 1, W_C, kind="mem")], "parallel", 48)

    s4 = lambda w: (B_LOC, 4, SEQ // 4, w)
    s16 = lambda w: (B_LOC, 16, SEQ // 16, w)
    (du, dz, doa, dla, dobn, lsen, dlbn, dob4, lse4, dlb4, dob16, lse16, dlb16, doc, dlc, acc, g_out) = _middle(
        oa, o1, l1, o4.reshape(s4(W_B)), l4.reshape(s4(128)), o16.reshape(s16(W_B)), l16.reshape(s16(128)), oc, z,
        x2, tgt2, g_branch, ln_gain, ln_bias, wout, spread4, gather4, gather8)

    flat = lambda a: a.reshape(T, a.shape[-1])
    (dqa, dka, dva, dsink), (dqc, dmkv) = _run_parts("attn_bwd_a", [
        _attn_bwd(qa, 0, W_A, ka, 0, va, 0, W_KV_A, doa, lse_a, dla, sinkv=sinkv, **swa),
        _attn_bwd(qc, 0, W_C, mkv, 0, mkv, 1, W_C, doc, lse_c, dlc, kind="mem")], "arbitrary", 48)
    g_mem = _tn_matmul("dw_mem", memb, dmkv, D_MODEL, B_LOC * MEM_LEN)
    last = T // QR - 1
    (r_out, r_mem), (dqn, dkn, dvn), (dq4, dk4, dv4), (dq16, dk16, dv16) = _run_parts("attn_bwd_b", [
        _ReduceScatter([(SH_OUT, D_MODEL), (SH_MEM, 2 * W_C)]).part((g_out, g_mem), (0, 1, last, last)),
        _attn_bwd(bn, 0, W_B, bn, 1, bn, 2, W_B, dobn, lsen, dlbn, **dil[0]),
        _attn_bwd(b4f, 0, W_B, b4f, 1, b4f, 2, W_B, flat(dob4), flat(lse4), flat(dlb4), **dil[1]),
        _attn_bwd(b16f, 0, W_B, b16f, 1, b16f, 2, W_B, flat(dob16), flat(lse16), flat(dlb16), **dil[2])],
        "arbitrary", 60)

    r4 = lambda a: a.reshape(s4(W_B))
    r16 = lambda a: a.reshape(s16(W_B))
    gx, dbin, g_in = _dh_dx(dqa, dka, dva, dqn, dkn, dvn, r4(dq4), r4(dk4), r4(dv4), r16(dq16), r16(dk16),
                            r16(dv16), dqc, dz, du, xb, cos, sa, sb, winT)
    return gx, g_in, r_out, r_mem, acc, dbin, dsink


def kernel(x, mem, w_in, b_in, w_mem, attn_sinks, g_branch, w_out, ln_gain, ln_bias, loss_target, m_w_in, m_b_in, m_w_mem, m_attn_sinks, m_g_branch, m_w_out, m_ln_gain, m_ln_bias, v_w_in, v_b_in, v_w_mem, v_attn_sinks, v_g_branch, v_w_out, v_ln_gain, v_ln_bias):
    winT, wout, wmem = _gather_weights(w_in[0].T, w_out[0], w_mem[0])
    gx, g_in, r_out, r_mem, acc, dbin, dsink = _local_step(
        x.reshape(T, D_MODEL), mem.reshape(B_LOC * MEM_LEN, D_MODEL), loss_target.reshape(T, D_MODEL),
        winT, wout, wmem, b_in, attn_sinks, g_branch, ln_gain, ln_bias)
    r_in, sv = _reduce_grads(g_in, acc, dbin, dsink)

    loss = jnp.sum(sv[0, :D_MODEL])
    small = ["b_in", "attn_sinks", "g_branch", "ln_gain", "ln_bias"]
    weights = dict(w_in=w_in, b_in=b_in, w_mem=w_mem, attn_sinks=attn_sinks, g_branch=g_branch, w_out=w_out,
                   ln_gain=ln_gain, ln_bias=ln_bias)
    ms = dict(w_in=m_w_in, b_in=m_b_in, w_mem=m_w_mem, attn_sinks=m_attn_sinks, g_branch=m_g_branch, w_out=m_w_out,
              ln_gain=m_ln_gain, ln_bias=m_ln_bias)
    vs = dict(w_in=v_w_in, b_in=v_b_in, w_mem=v_w_mem, attn_sinks=v_attn_sinks, g_branch=v_g_branch, w_out=v_w_out,
              ln_gain=v_ln_gain, ln_bias=v_ln_bias)
    out = dict(zip(small, _adamw_small(sv, [weights[n] for n in small], [ms[n] for n in small],
                                       [vs[n] for n in small])))
    d, nm, nv, g = (a.T[None] for a in _adamw("adamw_w_in", w_in[0].T, r_in, m_w_in[0].T, v_w_in[0].T, SH_IN // 4,
                                              copy_g=True))
    out["w_in"] = (g, d, nm, nv)
    for n, r in (("w_out", r_out), ("w_mem", r_mem)):
        d, nm, nv, g = (a[None] for a in _adamw("adamw_" + n, weights[n][0], r, ms[n][0], vs[n][0], copy_g=True))
        out[n] = (g, d, nm, nv)
    names = ["w_in", "b_in", "w_mem", "attn_sinks", "g_branch", "w_out", "ln_gain", "ln_bias"]
    return (loss, gx.reshape(B_LOC, SEQ, D_MODEL), *[out[n][k] for k in range(4) for n in names])
```

```python
import functools

import jax
import jax.numpy as jnp
from jax import lax
from jax.experimental import pallas as pl
from jax.experimental.pallas import tpu as pltpu

F32, BF16 = jnp.float32, jnp.bfloat16

D_MODEL = 1024
SEQ = 2048
B_LOC = 2
T = B_LOC * SEQ
BLK = 128
MEM_LEN = 256
W_A, W_KV_A, W_B, W_C, D_MIX = 512, 128, 256, 256, 1024
D_IN = 2816
O_QA, O_KA, O_VA, O_QB, O_KB, O_VB, O_QC, O_Z = 0, 512, 640, 768, 1024, 1280, 1536, 1792
ROPE_THETA = 10000.0
LN_EPS = 1e-5
RMS_EPS = 1e-6
ALPHA = 2.0 ** 0.25
QK_SCALE = 0.125
N_CHIP = 4
SH_IN, SH_OUT, SH_MEM = D_IN // N_CHIP, D_MIX // N_CHIP, D_MODEL // N_CHIP
NEG = -1e30
ADAM_LR, ADAM_B1, ADAM_B2, ADAM_EPS, ADAM_WD, ADAM_STEP = 0.001, 0.9, 0.999, 1e-08, 0.01, 10
SV_W = 3072
MESH = pl.DeviceIdType.MESH

NN = ((1,), (0,))
NT = ((1,), (1,))
TN = ((0,), (0,))


def _dot(a, b, dims):
    return lax.dot_general(a, b, (dims, ((), ())), preferred_element_type=F32)


def _cp(sem=None, vmem_mb=None):
    kw = {}
    if sem is not None:
        kw["dimension_semantics"] = sem
    if vmem_mb is not None:
        kw["vmem_limit_bytes"] = vmem_mb * 1024 * 1024
    return pltpu.CompilerParams(**kw)


def _sds(shape, dtype):
    return pltpu.HBM(shape, dtype)


def _vm_sds(shape, dtype):
    return jax.ShapeDtypeStruct(shape, dtype)


def _pin(*args):
    return [pltpu.with_memory_space_constraint(a, pltpu.HBM) for a in args]


def _full(shape):
    n = len(shape)
    return pl.BlockSpec(shape, lambda *_: (0,) * n)


def _shard_rows(ref, n, chip, half):
    start = pl.multiple_of((2 * chip[0] + chip[1]) * n + half * (n // 2), 16)
    return ref.at[pl.ds(start, n // 2), :]


def _gather_weights(win_sh, wout_sh, wmem_sh):
    half, piece = SH_IN // 2, SH_IN // 4

    def body(a_ref, b_ref, c_ref, oa_ref, ob_ref, oc_ref, ici_send, ici_recv, d2d_send, d2d_recv):
        x, y, c = lax.axis_index("x"), lax.axis_index("y"), lax.axis_index("c")
        me, sibling = (x, y, c), (x, y, 1 - c)
        xn, yn, dg = (1 - x, y), (x, 1 - y), (1 - x, 1 - y)
        for src, out, n in ((a_ref, oa_ref, SH_IN), (b_ref, ob_ref, SH_OUT), (c_ref, oc_ref, SH_MEM)):
            out[pl.ds(pl.multiple_of((2 * x + y) * n, 16), n), :] = src[...].astype(BF16)

        def rows(chip, hf, q):
            start = pl.multiple_of((2 * chip[0] + chip[1]) * SH_IN + hf * half + q * piece, 16)
            return oa_ref.at[pl.ds(start, piece), :]

        def copy(sems, k, chip, hf, q, to):
            blk = rows(chip, hf, q)
            return pltpu.make_async_remote_copy(
                src_ref=blk, dst_ref=blk, send_sem=sems[0].at[k], recv_sem=sems[1].at[k],
                device_id=to, device_id_type=MESH)

        ici, d2d = (ici_send, ici_recv), (d2d_send, d2d_recv)
        direct = [copy(ici, 0, (x, y), c, 0, (*xn, c)), copy(ici, 1, (x, y), c, 1, (*xn, c)),
                  copy(ici, 3, (x, y), c, 0, (*yn, c)), copy(ici, 4, (x, y), c, 1, (*yn, c))]
        for cp in direct:
            cp.start()
        arrivals = [(0, xn, 0), (1, xn, 1), (3, yn, 0), (4, yn, 1), (2, dg, 1), (5, dg, 0)]
        passed = []
        for k, chip, q in arrivals:
            copy(ici, k, chip, c, q, me).wait_recv()
            if k == 0:
                passed.append(copy(ici, 5, xn, c, 0, (*yn, c)))
                passed[-1].start()
            if k == 4:
                passed.append(copy(ici, 2, yn, c, 1, (*xn, c)))
                passed[-1].start()
            passed.append(copy(d2d, k, chip, c, q, sibling))
            passed[-1].start()
        for k, chip, q in arrivals:
            copy(d2d, k, chip, 1 - c, q, me).wait_recv()
        for cp in direct + passed:
            cp.wait_send()

    vm = pl.BlockSpec(memory_space=pltpu.VMEM)
    return pl.pallas_call(
        body, name="gather_weights",
        out_shape=(_vm_sds((D_IN, D_MODEL), BF16), _vm_sds((D_MIX, D_MODEL), BF16),
                   _vm_sds((D_MODEL, 2 * W_C), BF16)),
        in_specs=[vm, vm, vm], out_specs=(vm, vm, vm),
        scratch_shapes=[pltpu.SemaphoreType.DMA((6,))] * 4,
        compiler_params=_cp(vmem_mb=40),
    )(win_sh, wout_sh, wmem_sh)


def _rope(t, cos, sa, sb, sign):
    w = t.shape[1]
    reps = w // 128
    c, a, b = (jnp.tile(v, (1, reps)) if reps > 1 else v for v in (cos, sa, sb))
    rot = pltpu.roll(t, w - 32, 1) * a + pltpu.roll(t, 32, 1) * b
    return t * c + rot if sign > 0 else t * c - rot


def _in_proj(x, winT, b_in, cos, sa, sb, wout_own, wmem_own):
    tm = 512
    spt = SEQ // tm
    n_steps = T // tm
    forward_step = n_steps // 2

    def body(x_ref, w_ref, b_ref, cos_ref, sa_ref, sb_ref, wo_in, wm_in,
             xb_ref, qa_ref, ka_ref, va_ref, bn_ref, b4_ref, b16_ref, qc_ref, z_ref, wo_ref, wm_ref,
             scr, ici_send, ici_recv, d2d_send, d2d_recv):
        i = pl.program_id(0)
        mx, my, mc = lax.axis_index("x"), lax.axis_index("y"), lax.axis_index("c")
        chips = [(1 - mx, my), (mx, 1 - my), (1 - mx, 1 - my)]
        full = ((wo_ref, SH_OUT), (wm_ref, SH_MEM))

        def copy(sems, a, j, chip_of_block, half, to):
            blk = _shard_rows(full[a][0], full[a][1], chip_of_block, half)
            return pltpu.make_async_remote_copy(
                src_ref=blk, dst_ref=blk, send_sem=sems[0].at[a, j], recv_sem=sems[1].at[a, j],
                device_id=to, device_id_type=MESH)

        ici, d2d = (ici_send, ici_recv), (d2d_send, d2d_recv)
        pairs = [(a, j, chip) for j, chip in enumerate(chips) for a in range(2)]

        @pl.when(i == 0)
        def _():
            for a, j, chip in pairs:
                copy(ici, a, j, (mx, my), mc, (*chip, mc)).start()

        @pl.when(i == forward_step)
        def _():
            for a, j, chip in pairs:
                copy(ici, a, j, chip, mc, (mx, my, mc)).wait_recv()
                copy(d2d, a, j, chip, mc, (mx, my, 1 - mc)).start()

        @pl.when(i == n_steps - 1)
        def _():
            for a, j, chip in pairs:
                copy(d2d, a, j, chip, 1 - mc, (mx, my, mc)).wait_recv()
            for a, j, chip in pairs:
                copy(ici, a, j, (mx, my), mc, (*chip, mc)).wait_send()
                copy(d2d, a, j, chip, mc, (mx, my, 1 - mc)).wait_send()

        xb = x_ref[...].astype(BF16)
        xb_ref[...] = xb
        cos_t, sa_t, sb_t = cos_ref[...], sa_ref[...], sb_ref[...]

        def proj(r0, n):
            return _dot(xb, w_ref[r0:r0 + n, :], NT) + b_ref[:, r0:r0 + n]

        def rope(t):
            return _rope(t, cos_t, sa_t, sb_t, +1)

        qa_ref[...] = (rope(proj(O_QA, W_A)) * QK_SCALE).astype(BF16)
        ka_ref[...] = rope(proj(O_KA, W_KV_A)).astype(BF16)
        va_ref[...] = proj(O_VA, W_KV_A).astype(BF16)
        qc_ref[...] = (proj(O_QC, W_C) * QK_SCALE).astype(BF16)
        z_ref[...] = proj(O_Z, D_MIX).astype(BF16)
        parts = (rope(proj(O_QB, W_B)) * QK_SCALE, rope(proj(O_KB, W_B)), proj(O_VB, W_B))
        for k, part in enumerate(parts):
            bn_ref[:, 256 * k:256 * (k + 1)] = part.astype(BF16)
            scr[2 * k] = part[:, :128]
            scr[2 * k + 1] = part[:, 128:]
        for j in range(6):
            for res in range(4):
                b4_ref[0, res, :, 128 * j:128 * (j + 1)] = scr[j, pl.ds(res, tm // 4, stride=4), :].astype(BF16)
            for res in range(16):
                b16_ref[0, res, :, 128 * j:128 * (j + 1)] = scr[j, pl.ds(res, tm // 16, stride=16), :].astype(BF16)

    tok = lambda w: pl.BlockSpec((tm, w), lambda i: (i, 0))
    tab = pl.BlockSpec((tm, 128), lambda i: (i % spt, 0))
    hbm = pl.BlockSpec(memory_space=pl.ANY)
    return pl.pallas_call(
        body, name="in_proj", grid=(n_steps,),
        in_specs=[tok(D_MODEL), _full((D_IN, D_MODEL)), _full((1, D_IN)), tab, tab, tab, hbm, hbm],
        out_specs=(tok(D_MODEL), tok(W_A), tok(W_KV_A), tok(W_KV_A), tok(768),
                   pl.BlockSpec((1, 4, tm // 4, 768), lambda i: (i // spt, 0, i % spt, 0)),
                   pl.BlockSpec((1, 16, tm // 16, 768), lambda i: (i // spt, 0, i % spt, 0)),
                   tok(W_C), tok(D_MIX), hbm, hbm),
        out_shape=(_sds((T, D_MODEL), BF16), _sds((T, W_A), BF16), _sds((T, W_KV_A), BF16), _sds((T, W_KV_A), BF16),
                   _sds((T, 768), BF16), _sds((B_LOC, 4, SEQ // 4, 768), BF16), _sds((B_LOC, 16, SEQ // 16, 768), BF16),
                   _sds((T, W_C), BF16), _sds((T, D_MIX), BF16),
                   _sds((D_MIX, D_MODEL), BF16), _sds((D_MODEL, 2 * W_C), BF16)),
        input_output_aliases={6: 9, 7: 10},
        scratch_shapes=[pltpu.VMEM((6, tm, 128), F32)] + [pltpu.SemaphoreType.DMA((2, 3))] * 4,
        compiler_params=_cp(("arbitrary",), vmem_mb=48),
    )(*_pin(x, winT, b_in, cos, sa, sb, wout_own, wmem_own))


def _mem_kv(mem, wmem):
    def body(m_ref, w_ref, mb_ref, kv_ref):
        mb = m_ref[...].astype(BF16)
        mb_ref[...] = mb
        kv_ref[...] = _dot(mb, w_ref[...], NN).astype(BF16)

    n = B_LOC * MEM_LEN
    return pl.pallas_call(
        body, name="mem_kv",
        out_shape=(_sds((n, D_MODEL), BF16), _sds((n, 2 * W_C), BF16)),
    )(*_pin(mem, wmem))


class _Part:
    def __init__(self, body, args, in_specs, out_specs, out_shape, scratch=()):
        self.body, self.args, self.in_specs, self.out_specs, self.out_shape = body, args, in_specs, out_specs, out_shape
        self.scratch = list(scratch)


def _run_parts(name, parts, semantics, vmem_mb):
    n_in = [len(p.args) for p in parts]
    n_out = [len(p.out_shape) for p in parts]
    n_scr = [len(p.scratch) for p in parts]

    def body(*refs):
        ins, outs, scr = refs[:sum(n_in)], refs[sum(n_in):sum(n_in) + sum(n_out)], refs[sum(n_in) + sum(n_out):]
        i0 = o0 = s0 = 0
        for p, ni, no, ns in zip(parts, n_in, n_out, n_scr):
            p.body(*ins[i0:i0 + ni], *outs[o0:o0 + no], *scr[s0:s0 + ns])
            i0, o0, s0 = i0 + ni, o0 + no, s0 + ns

    res = pl.pallas_call(
        body, name=name, grid=(T // QR,),
        in_specs=[sp for p in parts for sp in p.in_specs], out_specs=tuple(sp for p in parts for sp in p.out_specs),
        out_shape=tuple(sh for p in parts for sh in p.out_shape),
        scratch_shapes=[sc for p in parts for sc in p.scratch],
        compiler_params=_cp((semantics,), vmem_mb=vmem_mb),
    )(*_pin(*[a for p in parts for a in p.args]))
    out, o0 = [], 0
    for no in n_out:
        out.append(tuple(res[o0:o0 + no]))
        o0 += no
    return out


QB = 8
QR = QB * BLK


def _lane_lo():
    return lax.broadcasted_iota(jnp.int32, (1, 128), 1) < 64


def _dup_head(k2, hk, lo):
    kf = k2.astype(F32)
    r = pltpu.roll(kf, 64, 1)
    return (jnp.where(lo, kf, r) if hk == 0 else jnp.where(lo, r, kf)).astype(BF16)


def _stack_heads(pairs, lo):
    parts = []
    for x2 in pairs:
        z = jnp.zeros_like(x2)
        parts += [jnp.where(lo, x2, z), jnp.where(lo, z, x2)]
    return jnp.concatenate(parts, axis=0)


def _prev_mode(kind, nb, j):
    if kind == "mem" or nb == 1:
        return "no"
    if nb <= QB:
        return "yes" if j % nb else "no"
    return "yes" if j else "dyn"


class _Attn:
    def __init__(self, kind, nb, max_dist, gqa, qw, kvw, qcb, kcb, vcb):
        self.kind, self.nb, self.gqa, self.qw, self.kvw = kind, nb, gqa, qw, kvw
        npairs = qw // 128
        self.groups = ([(hk, [2 * hk, 2 * hk + 1]) for hk in range(npairs // 2)] if gqa
                       else [(p, [p]) for p in range(npairs)])
        self.nh = 2 * len(self.groups[0][1])
        self.cols = 128 * self.nh
        self.reach = BLK - max_dist
        self.ext_prev = kind == "band" and nb > QB
        self.q_spec = pl.BlockSpec((QR, qw), lambda g: (g, qcb))
        self.row_spec = pl.BlockSpec((QR, qw), lambda g: (g, 0))
        self.stat_spec = pl.BlockSpec((QR, 128), lambda g: (g, 0))
        if kind == "mem":
            per = SEQ // QR
            self.kv_specs = [pl.BlockSpec((MEM_LEN, kvw), lambda g: (g // per, kcb)),
                             pl.BlockSpec((MEM_LEN, kvw), lambda g: (g // per, vcb))]
        else:
            self.kv_specs = [pl.BlockSpec((QR, kvw), lambda g: (g, kcb)), pl.BlockSpec((QR, kvw), lambda g: (g, vcb))]
            if self.ext_prev:
                self.kv_specs += [pl.BlockSpec((BLK, kvw), lambda g: (jnp.maximum(g * QB - 1, 0), kcb)),
                                  pl.BlockSpec((BLK, kvw), lambda g: (jnp.maximum(g * QB - 1, 0), vcb))]

    def masks(self):
        if self.kind == "mem":
            return None
        kj = lax.broadcasted_iota(jnp.int32, (2 * BLK, self.cols), 0)
        qi = lax.broadcasted_iota(jnp.int32, (2 * BLK, self.cols), 1) & (BLK - 1)
        kj1 = lax.broadcasted_iota(jnp.int32, (BLK, self.cols), 0)
        qi1 = lax.broadcasted_iota(jnp.int32, (BLK, self.cols), 1) & (BLK - 1)
        return kj, qi, kj1 <= qi1

    def keys(self, j, gi, kc_ref, vc_ref, kp_ref, vp_ref, lo, kq, g):
        def kv(k_ref, v_ref, r):
            if self.gqa:
                return _dup_head(k_ref[r, :], gi, lo), _dup_head(v_ref[r, :], gi, lo)
            sl = slice(128 * gi, 128 * (gi + 1))
            return k_ref[r, sl], v_ref[r, sl]

        if self.kind == "mem":
            key0 = pl.multiple_of((g // (SEQ // QR)) * MEM_LEN, MEM_LEN)
            return (*kv(kc_ref, vc_ref, slice(None)), None, [(0, MEM_LEN, key0)])
        kj, qi, cur = kq
        row0 = g * QR + BLK * j
        mode = _prev_mode(self.kind, self.nb, j)
        if mode == "no":
            return (*kv(kc_ref, vc_ref, slice(BLK * j, BLK * (j + 1))), cur, [(0, BLK, pl.multiple_of(row0, BLK))])
        if mode == "yes":
            mask = jnp.logical_and(kj >= qi + self.reach, kj <= qi + BLK)
            return (*kv(kc_ref, vc_ref, slice(BLK * (j - 1), BLK * (j + 1))), mask,
                    [(0, 2 * BLK, pl.multiple_of(row0 - BLK, BLK))])
        has_prev = ((g * QB) % self.nb) > 0
        hp = has_prev.astype(jnp.int32)
        mask = jnp.logical_and(kj >= qi * hp + (self.reach * hp + BLK * (1 - hp)), kj <= qi + BLK)
        kp, vp = kv(kp_ref, vp_ref, slice(None))
        kc, vc = kv(kc_ref, vc_ref, slice(0, BLK))
        return (jnp.concatenate([kp, kc], axis=0), jnp.concatenate([vp, vc], axis=0), mask,
                [(0, BLK, pl.multiple_of(jnp.maximum(row0 - BLK, 0), BLK)), (BLK, BLK, pl.multiple_of(row0, BLK))])


def _attn_fwd(q, qcb, qw, k, kcb, v, vcb, kvw, *, kind, nb=1, max_dist=BLK, gqa=False, sinks=None):
    a = _Attn(kind, nb, max_dist, gqa, qw, kvw, qcb, kcb, vcb)

    def body(*refs):
        it = iter(refs)
        q_ref, kc_ref, vc_ref = next(it), next(it), next(it)
        kp_ref, vp_ref = (next(it), next(it)) if a.ext_prev else (None, None)
        sink_ref = next(it) if sinks is not None else None
        o_ref, lse_ref = next(it), next(it)
        g = pl.program_id(0)
        lo = _lane_lo()
        top = lax.broadcasted_iota(jnp.int32, (128, 1), 0) < 64
        rid = lax.broadcasted_iota(jnp.int32, (8, 128), 0)
        kq = a.masks()
        stats = {}

        def scores(j, gi, pairs):
            rows = slice(BLK * j, BLK * (j + 1))
            qs = _stack_heads([q_ref[rows, 128 * p:128 * (p + 1)] for p in pairs], lo)
            kk, vv, mask, _ = a.keys(j, gi, kc_ref, vc_ref, kp_ref, vp_ref, lo, kq, g)
            pieces = [slice(r0, r0 + BLK) for r0 in range(0, kk.shape[0], BLK)]
            return dict(j=j, gi=gi, pairs=pairs, rows=rows, vv=vv, mask=mask, pieces=pieces,
                        ss=[_dot(kk[r], qs, NT) for r in pieces])

        def softmax(c):
            gi, mask = c["gi"], c["mask"]
            ss = [s if mask is None else jnp.where(mask[r], s, NEG) for r, s in zip(c["pieces"], c.pop("ss"))]
            m = jnp.max(ss[0], axis=0, keepdims=True)
            for s in ss[1:]:
                m = jnp.maximum(m, jnp.max(s, axis=0, keepdims=True))
            if sink_ref is not None:
                sk = jnp.concatenate([jnp.full((1, 128), sink_ref[0, a.nh * gi + i], F32) for i in range(a.nh)], axis=1)
                m = jnp.maximum(m, sk)
            ps = [jnp.exp(s - m) for s in ss]
            l = sum(jnp.sum(p, axis=0, keepdims=True) for p in ps)
            if sink_ref is not None:
                l = l + jnp.exp(sk - m)
            c["ps"] = [p.astype(BF16) for p in ps]
            c["l"], c["lse"] = l, m + jnp.log(l)

        def outputs(c):
            j, gi, rows = c["j"], c["gi"], c["rows"]
            ot = sum(_dot(c["vv"][r], p, TN) for r, p in zip(c["pieces"], c["ps"]))
            ot = ot * pl.reciprocal(c["l"], approx=True)
            for i, p in enumerate(c["pairs"]):
                o2t = jnp.where(top, ot[:, 256 * i:256 * i + 128], ot[:, 256 * i + 128:256 * i + 256])
                o_ref[rows, 128 * p:128 * (p + 1)] = o2t.T.astype(BF16)
            stat = stats.get(j, jnp.zeros((8, 128), F32))
            for i in range(a.nh):
                stat = jnp.where(rid == a.nh * gi + i, c["lse"][:, 128 * i:128 * (i + 1)], stat)
            stats[j] = stat
            if gi == a.groups[-1][0]:
                lse_ref[rows, :] = jnp.concatenate([stats.pop(j), jnp.zeros((120, 128), F32)], axis=0).T

        chains = [(j, gi, pairs) for j in range(QB) for gi, pairs in a.groups]
        live = {}
        for t in range(len(chains) + 2):
            if t < len(chains):
                live[t] = scores(*chains[t])
            if 0 <= t - 1 < len(chains):
                softmax(live[t - 1])
            if 0 <= t - 2 < len(chains):
                outputs(live.pop(t - 2))


    args = [q, k, v] + ([k, v] if a.ext_prev else [])
    in_specs = [a.q_spec] + a.kv_specs
    if sinks is not None:
        args.append(sinks)
        in_specs.append(pl.BlockSpec(memory_space=pltpu.SMEM))
    return _Part(body, args, in_specs, [a.row_spec, a.stat_spec], [_sds((T, qw), BF16), _sds((T, 128), F32)])


def _attn_bwd(q, qcb, qw, k, kcb, v, vcb, kvw, do, lse, dl, *, kind, nb=1, max_dist=BLK, gqa=False, sinkv=None):
    a = _Attn(kind, nb, max_dist, gqa, qw, kvw, qcb, kcb, vcb)

    def body(*refs):
        it = iter(refs)
        q_ref, kc_ref, vc_ref = next(it), next(it), next(it)
        kp_ref, vp_ref = (next(it), next(it)) if a.ext_prev else (None, None)
        do_ref, lse_ref, dl_ref = next(it), next(it), next(it)
        sinkv_ref = next(it) if sinkv is not None else None
        dq_ref = next(it)
        if kind == "mem":
            dkv_ref = next(it)
        else:
            dk_out, dv_out = next(it), next(it)
        dsink_ref = next(it) if sinkv is not None else None
        if kind != "mem":
            dk_ref, dv_ref, stage_k, stage_v, flush_sem = next(it), next(it), next(it), next(it), next(it)
        g = pl.program_id(0)
        lo = _lane_lo()
        top = lax.broadcasted_iota(jnp.int32, (128, 1), 0) < 64

        @pl.when(g == 0)
        def _():
            if kind == "mem":
                dkv_ref[...] = jnp.zeros_like(dkv_ref)
            else:
                dk_ref[...] = jnp.zeros_like(dk_ref)
                dv_ref[...] = jnp.zeros_like(dv_ref)
            if dsink_ref is not None:
                dsink_ref[...] = jnp.zeros_like(dsink_ref)

        kq = a.masks()
        stats_t = {}

        def first_matmuls(j, gi, pairs):
            rows = slice(BLK * j, BLK * (j + 1))
            if j not in stats_t:
                stats_t[j] = (lse_ref[rows, :].T, dl_ref[rows, :].T)
            lse_t, dl_t = stats_t[j]
            heads = [a.nh * gi + i for i in range(a.nh)]
            c = dict(rows=rows, gi=gi, pairs=pairs)
            c["qs"] = _stack_heads([q_ref[rows, 128 * p:128 * (p + 1)] for p in pairs], lo)
            c["dos"] = _stack_heads([do_ref[rows, 128 * p:128 * (p + 1)] for p in pairs], lo)
            c["lse_row"] = jnp.concatenate([lse_t[h:h + 1, :] for h in heads], axis=1)
            c["dl_row"] = jnp.concatenate([dl_t[h:h + 1, :] for h in heads], axis=1)
            c["kk"], vv, c["mask"], c["dests"] = a.keys(j, gi, kc_ref, vc_ref, kp_ref, vp_ref, lo, kq, g)
            c["s"] = _dot(c["kk"], c["qs"], NT)
            c["dp"] = _dot(vv, c["dos"], NT)
            return c

        def elementwise(c):
            s = c.pop("s")
            if c["mask"] is not None:
                s = jnp.where(c["mask"], s, NEG)
            p = jnp.exp(s - c["lse_row"])
            c["ds"] = (p * (c.pop("dp") - c["dl_row"])).astype(BF16)
            c["p"] = p.astype(BF16)

        def last_matmuls(c):
            gi, rows = c["gi"], c["rows"]
            dqt = _dot(c["kk"], c["ds"], TN)
            ck = _dot(c["ds"], c["qs"], NN)
            cv = _dot(c["p"], c["dos"], NN)
            if gqa:
                sel = lo if gi == 0 else jnp.logical_not(lo)
                ck = jnp.where(sel, ck + pltpu.roll(ck, 64, 1), 0.0)
                cv = jnp.where(sel, cv + pltpu.roll(cv, 64, 1), 0.0)
                kcols = slice(0, 128)
            else:
                kcols = slice(128 * gi, 128 * (gi + 1))
            for r0, nr, key0 in c["dests"]:
                krows = pl.ds(key0, nr)
                if kind == "mem":
                    dkv_ref[krows, kcols] += ck[r0:r0 + nr]
                    dkv_ref[krows, slice(kvw + kcols.start, kvw + kcols.stop)] += cv[r0:r0 + nr]
                else:
                    dk_ref[krows, kcols] += ck[r0:r0 + nr]
                    dv_ref[krows, kcols] += cv[r0:r0 + nr]
            for i, p in enumerate(c["pairs"]):
                dq2t = jnp.where(top, dqt[:, 256 * i:256 * i + 128], dqt[:, 256 * i + 128:256 * i + 256])
                dq_ref[rows, 128 * p:128 * (p + 1)] = dq2t.T.astype(BF16)

        chains = [(j, gi, pairs) for j in range(QB) for gi, pairs in a.groups]
        live = {}
        for t in range(len(chains) + 2):
            if t < len(chains):
                live[t] = first_matmuls(*chains[t])
            if 0 <= t - 1 < len(chains):
                elementwise(live[t - 1])
            if 0 <= t - 2 < len(chains):
                last_matmuls(live.pop(t - 2))
        if dsink_ref is not None:
            ps = jnp.exp(sinkv_ref[...] - lse_ref[...]) * dl_ref[...]
            dsink_ref[...] += jnp.sum(ps, axis=0, keepdims=True)
        if kind != "mem":
            n_steps = T // QR

            def flush(step):
                rows = pl.ds(pl.multiple_of(step * QR, QR), QR)
                out = []
                for acc, stage, dst, i in ((dk_ref, stage_k, dk_out, 0), (dv_ref, stage_v, dv_out, 1)):
                    stage[...] = acc[rows, :].astype(BF16)
                    out.append(pltpu.make_async_copy(stage, dst.at[rows, :], flush_sem.at[i]))
                return out

            def flushed(step):
                rows = pl.ds(pl.multiple_of(step * QR, QR), QR)
                return [pltpu.make_async_copy(stage, dst.at[rows, :], flush_sem.at[i])
                        for stage, dst, i in ((stage_k, dk_out, 0), (stage_v, dv_out, 1))]

            @pl.when(g >= 2)
            def _():
                for cp in flushed(g - 2):
                    cp.wait()

            @pl.when(g >= 1)
            def _():
                for cp in flush(g - 1):
                    cp.start()

            @pl.when(g == n_steps - 1)
            def _():
                for cp in flushed(g - 1):
                    cp.wait()
                for cp in flush(g):
                    cp.start()
                for cp in flushed(g):
                    cp.wait()

    args = [q, k, v] + ([k, v] if a.ext_prev else []) + [do, lse, dl]
    in_specs = [a.q_spec] + a.kv_specs + [a.row_spec, a.stat_spec, a.stat_spec]
    if sinkv is not None:
        args.append(sinkv)
        in_specs.append(_full((1, 128)))
    out_shape = [_sds((T, qw), BF16)]
    out_specs = [a.row_spec]
    scratch = []
    if kind == "mem":
        out_shape.append(_sds((B_LOC * MEM_LEN, 2 * kvw), F32))
        out_specs.append(pl.BlockSpec((B_LOC * MEM_LEN, 2 * kvw), lambda g: (0, 0), pipeline_mode=pl.Buffered(1)))
    else:
        out_shape += [_sds((T, kvw), BF16)] * 2
        out_specs += [pl.BlockSpec(memory_space=pl.ANY)] * 2
        scratch = [pltpu.VMEM((T, kvw), F32)] * 2 + [pltpu.VMEM((QR, kvw), BF16)] * 2 + [pltpu.SemaphoreType.DMA((2,))]
    if sinkv is not None:
        out_shape.append(_sds((1, 128), F32))
        out_specs.append(_full((1, 128)))
    return _Part(body, args, in_specs, out_specs, out_shape, scratch)


def _dot2(v, w_ref):
    hi = v.astype(BF16)
    lo = (v - hi.astype(F32)).astype(BF16)
    return _dot(hi, w_ref[...], NN) + _dot(lo, w_ref[...], NN)


def _middle(oa, o1, l1, o4, l4, o16, l16, oc, z, x, tgt, g_br, ln_g, ln_b, wout, spread4, gather4, gather8):
    tm = 512
    spt = SEQ // tm

    def body(oa_ref, o1_ref, l1_ref, o4_ref, l4_ref, o16_ref, l16_ref, oc_ref, z_ref, x_ref, t_ref,
             g_ref, lg_ref, lb_ref, w_ref, sp4_ref, ga4_ref, ga8_ref,
             du_ref, dz_ref, doa_ref, dla_ref,
             dobn_ref, lsen_ref, dlbn_ref, dob4_ref, lse4_ref, dlb4_ref, dob16_ref, lse16_ref, dlb16_ref,
             doc_ref, dlc_ref, acc_ref, gout_ref, scr):
        i = pl.program_id(0)

        @pl.when(i == 0)
        def _():
            acc_ref[...] = jnp.zeros_like(acc_ref)
            gout_ref[...] = jnp.zeros_like(gout_ref)

        for res in range(4):
            rows = pl.ds(res, tm // 4, stride=4)
            for j in range(2):
                scr[j, rows, :] = o4_ref[0, res, :, 128 * j:128 * (j + 1)].astype(F32)
            scr[2, rows, :] = l4_ref[0, res]
        for res in range(16):
            rows = pl.ds(res, tm // 16, stride=16)
            for j in range(2):
                scr[3 + j, rows, :] = o16_ref[0, res, :, 128 * j:128 * (j + 1)].astype(F32)
            scr[5, rows, :] = l16_ref[0, res]
        inv_d = 1.0 / D_MODEL
        gb, lg, lb = g_ref[...], lg_ref[...], lb_ref[...]

        def rms(o):
            r = lax.rsqrt(jnp.sum(o * o, axis=1, keepdims=True) * (1.0 / o.shape[1]) + RMS_EPS)
            return o * r, r

        def rms_bwd(dn_, n_, r):
            return r * (dn_ - n_ * (jnp.sum(dn_ * n_, axis=1, keepdims=True) * (1.0 / n_.shape[1])))

        def forward(rs):
            o4v = jnp.concatenate([scr[0, rs, :], scr[1, rs, :]], axis=1)
            o16v = jnp.concatenate([scr[3, rs, :], scr[4, rs, :]], axis=1)
            l1v, l4v, l16v = l1_ref[rs, :], scr[2, rs, :], scr[5, rs, :]
            mx = jnp.maximum(jnp.maximum(l1v, l4v), l16v)
            e1, e4, e16 = jnp.exp(l1v - mx), jnp.exp(l4v - mx), jnp.exp(l16v - mx)
            ssum = e1 + e4 + e16
            inv = 1.0 / ssum
            c = dict(rs=rs, lse_b=mx + jnp.log(ssum))
            c["ob"] = (_dot2(e1 * inv, sp4_ref) * o1_ref[rs, :].astype(F32) + _dot2(e4 * inv, sp4_ref) * o4v
                       + _dot2(e16 * inv, sp4_ref) * o16v)
            c["oa"], c["oc"] = oa_ref[rs, :].astype(F32), oc_ref[rs, :].astype(F32)
            na, c["ra"] = rms(c["oa"])
            nb_, c["rb"] = rms(c["ob"])
            nc, c["rc"] = rms(c["oc"])
            c["n"] = jnp.concatenate([na, nb_, nc], axis=1)
            c["zf"] = z_ref[rs, :].astype(F32)
            c["sig"] = 1.0 / (1.0 + jnp.exp(-c["zf"]))
            c["sz"] = c["zf"] * c["sig"]
            c["yb"] = (c["n"] * gb * c["sz"]).astype(BF16)
            c["y2"] = _dot(c["yb"], w_ref[...], NN)
            return c

        def norm(c):
            rs = c["rs"]
            u = ALPHA * x_ref[rs, :] + c.pop("y2")
            mu = jnp.sum(u, axis=1, keepdims=True) * inv_d
            uc = u - mu
            rstd = lax.rsqrt(jnp.sum(uc * uc, axis=1, keepdims=True) * inv_d + LN_EPS)
            xh = uc * rstd
            diff = xh * lg + lb - t_ref[rs, :]
            acc_ref[0:1, :] += jnp.sum(diff * diff, axis=0, keepdims=True) * (0.5 * inv_d)
            dout = diff * inv_d
            acc_ref[2:3, :] += jnp.sum(dout * xh, axis=0, keepdims=True)
            acc_ref[3:4, :] += jnp.sum(dout, axis=0, keepdims=True)
            dxh = dout * lg
            du = rstd * (dxh - jnp.sum(dxh, axis=1, keepdims=True) * inv_d
                         - xh * (jnp.sum(dxh * xh, axis=1, keepdims=True) * inv_d))
            dub = du.astype(BF16)
            du_ref[rs, :] = dub
            c["dy"] = _dot(dub, w_ref[...], NT)
            gout_ref[...] += _dot(c.pop("yb"), dub, TN)

        def backward(c):
            rs, n, dy, zf, sig = c["rs"], c["n"], c["dy"], c["zf"], c["sig"]
            t1 = dy * c["sz"]
            acc_ref[1:2, :] += jnp.sum(t1 * n, axis=0, keepdims=True)
            dn = t1 * gb
            dz_ref[rs, :] = (dy * n * gb * (sig * (1.0 + zf * (1.0 - sig)))).astype(BF16)
            doa = rms_bwd(dn[:, :W_A], n[:, :W_A], c["ra"])
            dob = rms_bwd(dn[:, W_A:W_A + W_B], n[:, W_A:W_A + W_B], c["rb"])
            doc = rms_bwd(dn[:, W_A + W_B:], n[:, W_A + W_B:], c["rc"])
            doa_ref[rs, :] = doa.astype(BF16)
            dla_ref[rs, :] = _dot2(doa * c["oa"], ga8_ref)
            doc_ref[rs, :] = doc.astype(BF16)
            dlc_ref[rs, :] = _dot2(doc * c["oc"], ga4_ref)
            dobn_ref[rs, :] = dob.astype(BF16)
            lsen_ref[rs, :] = c["lse_b"]
            dlbn_ref[rs, :] = _dot2(dob * c["ob"], ga4_ref)
            scr[0, rs, :] = dob[:, :128]
            scr[1, rs, :] = dob[:, 128:]

        halves = [slice(h * (tm // 2), (h + 1) * (tm // 2)) for h in range(2)]
        live = {}
        for t in range(len(halves) + 2):
            if t < len(halves):
                live[t] = forward(halves[t])
            if 0 <= t - 1 < len(halves):
                norm(live[t - 1])
            if 0 <= t - 2 < len(halves):
                backward(live.pop(t - 2))
        for j in range(2):
            sl = slice(128 * j, 128 * (j + 1))
            for res in range(4):
                dob4_ref[0, res, :, sl] = scr[j, pl.ds(res, tm // 4, stride=4), :].astype(BF16)
            for res in range(16):
                dob16_ref[0, res, :, sl] = scr[j, pl.ds(res, tm // 16, stride=16), :].astype(BF16)
        for res in range(4):
            rows = pl.ds(res, tm // 4, stride=4)
            lse4_ref[0, res] = lsen_ref[rows, :]
            dlb4_ref[0, res] = dlbn_ref[rows, :]
        for res in range(16):
            rows = pl.ds(res, tm // 16, stride=16)
            lse16_ref[0, res] = lsen_ref[rows, :]
            dlb16_ref[0, res] = dlbn_ref[rows, :]


    tok = lambda w: pl.BlockSpec((tm, w), lambda i: (i, 0))
    p4 = lambda w: pl.BlockSpec((1, 4, tm // 4, w), lambda i: (i // spt, 0, i % spt, 0))
    p16 = lambda w: pl.BlockSpec((1, 16, tm // 16, w), lambda i: (i // spt, 0, i % spt, 0))
    s4 = lambda w, dt: _sds((B_LOC, 4, SEQ // 4, w), dt)
    s16 = lambda w, dt: _sds((B_LOC, 16, SEQ // 16, w), dt)
    row = _full((1, D_MODEL))
    return pl.pallas_call(
        body, name="middle", grid=(T // tm,),
        in_specs=[tok(W_A), tok(W_B), tok(128), p4(W_B), p4(128), p16(W_B), p16(128), tok(W_C), tok(D_MIX),
                  tok(D_MODEL), tok(D_MODEL), row, row, row, _full((D_MIX, D_MODEL)),
                  _full((128, W_B)), _full((W_B, 128)), _full((W_A, 128))],
        out_specs=(tok(D_MODEL), tok(D_MIX), tok(W_A), tok(128),
                   tok(W_B), tok(128), tok(128), p4(W_B), p4(128), p4(128), p16(W_B), p16(128), p16(128),
                   tok(W_C), tok(128), _full((8, D_MODEL)), _full((D_MIX, D_MODEL))),
        out_shape=(_sds((T, D_MODEL), BF16), _sds((T, D_MIX), BF16),
                   _sds((T, W_A), BF16), _sds((T, 128), F32),
                   _sds((T, W_B), BF16), _sds((T, 128), F32), _sds((T, 128), F32),
                   s4(W_B, BF16), s4(128, F32), s4(128, F32), s16(W_B, BF16), s16(128, F32), s16(128, F32),
                   _sds((T, W_C), BF16), _sds((T, 128), F32), _sds((8, D_MODEL), F32),
                   _sds((D_MIX, D_MODEL), F32)),
        scratch_shapes=[pltpu.VMEM((6, tm, 128), F32)],
        compiler_params=_cp(("arbitrary",), vmem_mb=56),
    )(*_pin(oa, o1, l1, o4, l4, o16, l16, oc, z, x, tgt, g_br, ln_g, ln_b, wout, spread4, gather4, gather8))


class _ReduceScatter:
    def __init__(self, shapes):
        self.shapes = shapes

    def scratch_shapes(self):
        out = []
        for n, w in self.shapes:
            h, p = n // 2, n // 4
            out += [pltpu.VMEM((4, h, w), F32), pltpu.VMEM((4, h, w), F32), pltpu.VMEM((6, p, w), BF16),
                    pltpu.VMEM((6, p, w), BF16), pltpu.VMEM((2, p, w), F32), pltpu.VMEM((h, w), F32)]
        na = len(self.shapes)
        dma = pltpu.SemaphoreType.DMA
        return out + [dma((na, 4)), dma((na, 4)), dma((na, 4)), dma((na, 6)), dma((na, 6)), dma((na,)), dma((na,)),
                      dma((na,))]

    def bind(self, g_refs, r_refs, scratch):
        na = len(self.shapes)
        bufs = [scratch[6 * a:6 * a + 6] for a in range(na)]
        mine, sib, stage, land, keep, tot = (tuple(b[i] for b in bufs) for i in range(6))
        loc_sem, s1_send, s1_recv, s2_send, s2_recv, s3_send, s3_recv, st_sem = scratch[6 * na:6 * na + 8]
        x, y, c = lax.axis_index("x"), lax.axis_index("y"), lax.axis_index("c")
        me, sibling = (x, y, c), (x, y, 1 - c)
        xn, yn, dg = (1 - x, y), (x, 1 - y), (1 - x, 1 - y)
        idx = lambda chip: 2 * chip[0] + chip[1]
        my_chip = idx((x, y))
        order = [idx(xn), idx(dg), idx(yn), my_chip]

        def rows(a, k, half):
            n = self.shapes[a][0]
            return pl.ds(pl.multiple_of(k * n + half * (n // 2), 8), n // 2)

        def piece(a, q):
            p = self.shapes[a][0] // 4
            return slice(q * p, (q + 1) * p)

        def load(a, k):
            return pltpu.make_async_copy(g_refs[a].at[rows(a, k, c), :], mine[a].at[k], loc_sem.at[a, k])

        def s1(a, k, half):
            return pltpu.make_async_remote_copy(
                src_ref=g_refs[a].at[rows(a, k, half), :], dst_ref=sib[a].at[k],
                send_sem=s1_send.at[a, k], recv_sem=s1_recv.at[a, k], device_id=sibling, device_id_type=MESH)

        def s2(a, i, to):
            return pltpu.make_async_remote_copy(
                src_ref=stage[a].at[i], dst_ref=land[a].at[i], send_sem=s2_send.at[a, i], recv_sem=s2_recv.at[a, i],
                device_id=to, device_id_type=MESH)

        via = {0: xn, 1: xn, 2: yn, 3: yn, 4: yn, 5: xn}

        def s3(a, half, to):
            return pltpu.make_async_remote_copy(
                src_ref=tot[a], dst_ref=r_refs[a].at[rows(a, 0, half), :], send_sem=s3_send.at[a],
                recv_sem=s3_recv.at[a], device_id=to, device_id_type=MESH)

        def store(a):
            return pltpu.make_async_copy(tot[a], r_refs[a].at[rows(a, 0, c), :], st_sem.at[a])

        def start():
            for k in order:
                for a in range(na):
                    load(a, k).start()
                    s1(a, k, 1 - c).start()

        def chip_sum(a, k):
            load(a, k).wait()
            s1(a, k, c).wait_recv()
            return mine[a][k] + sib[a][k]

        def exchange():
            for a in range(na):
                P, Q = piece(a, 0), piece(a, 1)
                s_xn = chip_sum(a, idx(xn))
                stage[a][0] = s_xn[P].astype(BF16)
                keep[a][1] = s_xn[Q]
                s_dg = chip_sum(a, idx(dg))
                stage[a][1] = s_dg[P].astype(BF16)
                s2(a, 0, (*xn, c)).start()
                s2(a, 1, (*xn, c)).start()
                stage[a][3] = s_dg[Q].astype(BF16)
                s_yn = chip_sum(a, idx(yn))
                stage[a][2] = s_yn[Q].astype(BF16)
                keep[a][0] = s_yn[P]
                s2(a, 2, (*yn, c)).start()
                s2(a, 3, (*yn, c)).start()
                tot[a][...] = chip_sum(a, my_chip)

        def relay():
            for a in range(na):
                P, Q = piece(a, 0), piece(a, 1)
                s2(a, 1, me).wait_recv()
                stage[a][4] = (keep[a][0] + land[a][1].astype(F32)).astype(BF16)
                s2(a, 4, (*yn, c)).start()
                s2(a, 3, me).wait_recv()
                stage[a][5] = (keep[a][1] + land[a][3].astype(F32)).astype(BF16)
                s2(a, 5, (*xn, c)).start()
                s2(a, 0, me).wait_recv()
                tot[a][P, :] += land[a][0].astype(F32)
                s2(a, 2, me).wait_recv()
                tot[a][Q, :] += land[a][2].astype(F32)

        def finish():
            for a in range(na):
                P, Q = piece(a, 0), piece(a, 1)
                s2(a, 4, me).wait_recv()
                tot[a][P, :] += land[a][4].astype(F32)
                s2(a, 5, me).wait_recv()
                tot[a][Q, :] += land[a][5].astype(F32)
                s3(a, c, sibling).start()
                store(a).start()

        def drain():
            for a in range(na):
                s3(a, 1 - c, me).wait_recv()
                store(a).wait()
            for a in range(na):
                for k in order:
                    s1(a, k, 1 - c).wait_send()
                for i in range(6):
                    s2(a, i, (*via[i], c)).wait_send()
                s3(a, c, sibling).wait_send()

        return start, exchange, relay, finish, drain

    def part(self, grads, steps):
        def body(*refs):
            na = len(self.shapes)
            i = pl.program_id(0)
            for step, phase in zip(steps, self.bind(refs[:na], refs[na:2 * na], refs[2 * na:])):
                pl.when(i == step)(phase)

        hbm = pl.BlockSpec(memory_space=pl.ANY)
        return _Part(body, list(grads), [hbm] * len(grads), [hbm] * len(grads),
                     [_sds((n, w), F32) for n, w in self.shapes], self.scratch_shapes())


def _dh_dx(dqa, dka, dva, dqn, dkn, dvn, dq4, dk4, dv4, dq16, dk16, dv16, dqc, dz, du, xb, cos, sa, sb, winT):
    tm = 512
    spt = SEQ // tm

    def body(dqa_ref, dka_ref, dva_ref, dqn_ref, dkn_ref, dvn_ref, dq4_ref, dk4_ref, dv4_ref,
             dq16_ref, dk16_ref, dv16_ref, dqc_ref, dz_ref, du_ref, xb_ref, cos_ref, sa_ref, sb_ref, w_ref,
             gx_ref, db_ref, gin_ref, dh_ref, scr):
        i = pl.program_id(0)

        @pl.when(i == 0)
        def _():
            db_ref[...] = jnp.zeros_like(db_ref)
            gin_ref[...] = jnp.zeros_like(gin_ref)

        cos_t, sa_t, sb_t = cos_ref[...], sa_ref[...], sb_ref[...]

        def rope_t(t):
            return _rope(t, cos_t, sa_t, sb_t, -1)

        def put(r0, val):
            n = val.shape[1]
            dh_ref[:, r0:r0 + n] = val.astype(BF16)
            db_ref[:, r0:r0 + n] += jnp.sum(val, axis=0, keepdims=True)

        put(O_QA, rope_t(dqa_ref[...].astype(F32)) * QK_SCALE)
        put(O_KA, rope_t(dka_ref[...].astype(F32)))
        put(O_VA, dva_ref[...].astype(F32))
        put(O_QC, dqc_ref[...].astype(F32) * QK_SCALE)
        put(O_Z, dz_ref[...].astype(F32))
        for k, (n_ref, r4, r16) in enumerate(((dqn_ref, dq4_ref, dq16_ref), (dkn_ref, dk4_ref, dk16_ref),
                                               (dvn_ref, dv4_ref, dv16_ref))):
            for j in range(2):
                sl = slice(128 * j, 128 * (j + 1))
                scr[2 * k + j] = n_ref[:, sl].astype(F32)
                for res in range(4):
                    scr[2 * k + j, pl.ds(res, tm // 4, stride=4), :] += r4[0, res, :, sl].astype(F32)
                for res in range(16):
                    scr[2 * k + j, pl.ds(res, tm // 16, stride=16), :] += r16[0, res, :, sl].astype(F32)
        cat = lambda a: jnp.concatenate([scr[a], scr[a + 1]], axis=1)
        put(O_QB, rope_t(cat(0)) * QK_SCALE)
        put(O_KB, rope_t(cat(2)))
        put(O_VB, cat(4))
        gx_ref[...] = _dot(dh_ref[...], w_ref[...], NN) + ALPHA * du_ref[...].astype(F32)
        gin_ref[...] += _dot(dh_ref[...], xb_ref[...], TN)

    tok = lambda w: pl.BlockSpec((tm, w), lambda i: (i, 0))
    tab = pl.BlockSpec((tm, 128), lambda i: (i % spt, 0))
    p4 = pl.BlockSpec((1, 4, tm // 4, W_B), lambda i: (i // spt, 0, i % spt, 0))
    p16 = pl.BlockSpec((1, 16, tm // 16, W_B), lambda i: (i // spt, 0, i % spt, 0))
    once = lambda shape: pl.BlockSpec(shape, lambda i: (0, 0), pipeline_mode=pl.Buffered(1))
    return pl.pallas_call(
        body, name="dh_dx", grid=(T // tm,),
        in_specs=[tok(W_A), tok(W_KV_A), tok(W_KV_A), tok(W_B), tok(W_B), tok(W_B), p4, p4, p4, p16, p16, p16,
                  tok(W_C), tok(D_MIX), tok(D_MODEL), tok(D_MODEL), tab, tab, tab, once((D_IN, D_MODEL))],
        out_specs=(tok(D_MODEL), _full((1, D_IN)), once((D_IN, D_MODEL))),
        out_shape=(_sds((T, D_MODEL), F32), _sds((1, D_IN), F32), _sds((D_IN, D_MODEL), F32)),
        scratch_shapes=[pltpu.VMEM((tm, D_IN), BF16), pltpu.VMEM((6, tm, 128), F32)],
        compiler_params=_cp(("arbitrary",), vmem_mb=56),
    )(*_pin(dqa, dka, dva, dqn, dkn, dvn, dq4, dk4, dv4, dq16, dk16, dv16, dqc, dz, du, xb, cos, sa, sb, winT))


def _tn_matmul(name, a, b, bm, bt):
    n, m_all = a.shape
    n_cols = b.shape[1]

    def body(a_ref, b_ref, o_ref):
        @pl.when(pl.program_id(1) == 0)
        def _():
            o_ref[...] = jnp.zeros_like(o_ref)

        o_ref[...] += _dot(a_ref[...].astype(BF16), b_ref[...].astype(BF16), TN)

    return pl.pallas_call(
        body, name=name, grid=(m_all // bm, n // bt),
        in_specs=[pl.BlockSpec((bt, bm), lambda m, t: (t, m)), pl.BlockSpec((bt, n_cols), lambda m, t: (t, 0))],
        out_specs=pl.BlockSpec((bm, n_cols), lambda m, t: (m, 0)),
        out_shape=_sds((m_all, n_cols), F32),
        compiler_params=_cp(("parallel", "arbitrary"), vmem_mb=48),
    )(*_pin(a, b))


def _reduce_grads(g_in, acc, dbin, dsink):
    rs = _ReduceScatter([(SH_IN, D_MODEL)])

    def body(g_ref, acc_ref, dbin_ref, dsink_ref, r_ref, sv_ref, sv_mine, sv_all, sv_send, sv_recv, *rs_scratch):
        x, y, c = lax.axis_index("x"), lax.axis_index("y"), lax.axis_index("c")
        chips = [(1 - x, y), (x, 1 - y), (1 - x, 1 - y)]
        start, exchange, relay, finish, drain = rs.bind((g_ref,), (r_ref,), rs_scratch)
        start()

        sv_mine[...] = jnp.zeros_like(sv_mine)
        sv_mine[0:4, 0:D_MODEL] = acc_ref[0:4, :]
        sv_mine[4:5, 0:D_IN] = dbin_ref[...]
        sv_mine[5:6, 0:128] = dsink_ref[...]
        my_dev = 4 * x + 2 * y + c
        others = [(x, y, 1 - c)] + [(*chip, cc) for chip in chips for cc in (c, 1 - c)]

        def sv_copy(j, to):
            return pltpu.make_async_remote_copy(
                src_ref=sv_mine, dst_ref=sv_all.at[my_dev], send_sem=sv_send.at[j], recv_sem=sv_recv.at[j],
                device_id=to, device_id_type=MESH)

        sv_sends = [sv_copy(j, to) for j, to in enumerate(others)]
        for cp in sv_sends:
            cp.start()
        exchange()
        relay()
        finish()
        sv_all[my_dev] = sv_mine[...]
        for j in range(7):
            sv_copy(j, (x, y, c)).wait_recv()
        tot = sv_all[0]
        for d in range(1, 8):
            tot = tot + sv_all[d]
        sv_ref[...] = tot
        drain()
        for cp in sv_sends:
            cp.wait_send()

    vm = pl.BlockSpec(memory_space=pltpu.VMEM)
    hbm = pl.BlockSpec(memory_space=pl.ANY)
    return pl.pallas_call(
        body, name="reduce_grads",
        out_shape=(_sds((SH_IN, D_MODEL), F32), _vm_sds((8, SV_W), F32)),
        in_specs=[hbm, vm, vm, vm], out_specs=(hbm, vm),
        scratch_shapes=[pltpu.VMEM((8, SV_W), F32), pltpu.VMEM((8, 8, SV_W), F32),
                        pltpu.SemaphoreType.DMA((7,)), pltpu.SemaphoreType.DMA((7,))] + rs.scratch_shapes(),
        compiler_params=_cp(vmem_mb=40),
    )(pltpu.with_memory_space_constraint(g_in, pltpu.HBM), acc, dbin, dsink)


def _adamw(name, w, g, m, v, rows=None, copy_g=False):
    shape = w.shape
    rows = shape[0] if rows is None else rows
    n_out = 4 if copy_g else 3

    def body(w_ref, g_ref, m_ref, v_ref, d_ref, nm_ref, nv_ref, *go_ref):
        gv = g_ref[...]
        if copy_g:
            go_ref[0][...] = gv
        nm = ADAM_B1 * m_ref[...] + (1.0 - ADAM_B1) * gv
        nv = ADAM_B2 * v_ref[...] + (1.0 - ADAM_B2) * (gv * gv)
        m_hat = nm / (1.0 - ADAM_B1 ** ADAM_STEP)
        v_hat = nv / (1.0 - ADAM_B2 ** ADAM_STEP)
        d_ref[...] = -ADAM_LR * (m_hat / (jnp.sqrt(v_hat) + ADAM_EPS) + ADAM_WD * w_ref[...])
        nm_ref[...] = nm
        nv_ref[...] = nv

    spec = pl.BlockSpec((rows, shape[1]), lambda i: (i, 0))
    return pl.pallas_call(
        body, name=name, grid=(shape[0] // rows,), in_specs=[spec] * 4, out_specs=(spec,) * n_out,
        out_shape=(_sds(shape, F32),) * n_out, compiler_params=_cp(("parallel",)),
    )(*_pin(w, g, m, v))


def _adamw_small(sv, ws, ms, vs):
    where = ((4, D_IN, 1.0), (5, 8, -1.0), (1, D_MIX, 1.0), (2, D_MODEL, 1.0), (3, D_MODEL, 1.0))

    def body(sv_ref, *refs):
        ins, outs = refs[:15], refs[15:]
        for p, (row, width, sign) in enumerate(where):
            gv = sign * sv_ref[row:row + 1, 0:width]
            w_ref, m_ref, v_ref = ins[p], ins[5 + p], ins[10 + p]
            nm = ADAM_B1 * m_ref[...] + (1.0 - ADAM_B1) * gv
            nv = ADAM_B2 * v_ref[...] + (1.0 - ADAM_B2) * (gv * gv)
            m_hat = nm / (1.0 - ADAM_B1 ** ADAM_STEP)
            v_hat = nv / (1.0 - ADAM_B2 ** ADAM_STEP)
            outs[4 * p][...] = gv
            outs[4 * p + 1][...] = -ADAM_LR * (m_hat / (jnp.sqrt(v_hat) + ADAM_EPS) + ADAM_WD * w_ref[...])
            outs[4 * p + 2][...] = nm
            outs[4 * p + 3][...] = nv

    res = pl.pallas_call(
        body, name="adamw_small", out_shape=tuple(_vm_sds(w.shape, F32) for w in ws for _ in range(4)),
    )(sv, *ws, *ms, *vs)
    return [tuple(res[4 * p:4 * p + 4]) for p in range(5)]


def _rope_tables():
    pos = jnp.arange(SEQ, dtype=F32)
    inv = ROPE_THETA ** (-jnp.arange(0, 64, 2, dtype=F32) / 64)
    ang = pos[:, None] * inv[None, :]
    cos, sin = lax.optimization_barrier((jnp.cos(ang), jnp.sin(ang)))
    cos, sin = jnp.tile(cos, (1, 4)), jnp.tile(sin, (1, 4))
    low = (jnp.arange(128) % 64) < 32
    return cos, jnp.where(low, -sin, 0.0), jnp.where(low, 0.0, sin)


def _local_step(x2, mem2, tgt2, winT, wout, wmem, b_in, sinks, g_branch, ln_gain, ln_bias):
    cos, sa, sb = _rope_tables()
    sinkv = jnp.pad(sinks, ((0, 0), (0, 120)))
    head_of_lane = jnp.arange(512)[None, :] // 64
    gather8 = (head_of_lane.T == jnp.arange(128)[None, :]).astype(BF16)
    gather4 = gather8[:W_B]
    spread4 = gather4.T

    xb, qa, ka, va, bn, b4, b16, qc, z, wout, wmem = _in_proj(x2, winT, b_in, cos, sa, sb, wout, wmem)
    memb, mkv = _mem_kv(mem2, wmem)
    b4f, b16f = b4.reshape(T, 768), b16.reshape(T, 768)

    swa = dict(kind="band", nb=SEQ // BLK, max_dist=BLK - 1, gqa=True)
    dil = (dict(kind="band", nb=SEQ // BLK), dict(kind="band", nb=SEQ // 4 // BLK), dict(kind="band", nb=1))
    (oa, lse_a), (o1, l1), (o4, l4), (o16, l16), (oc, lse_c) = _run_parts("attn_fwd", [
        _attn_fwd(qa, 0, W_A, ka, 0, va, 0, W_KV_A, sinks=sinks, **swa),
        _attn_fwd(bn, 0, W_B, bn, 1, bn, 2, W_B, **dil[0]),
        _attn_fwd(b4f, 0, W_B, b4f, 1, b4f, 2, W_B, **dil[1]),
        _attn_fwd(b16f, 0, W_B, b16f, 1, b16f, 2, W_B, **dil[2]),
        _attn_fwd(qc, 0, W_C, mkv, 0, mkv, 1, W_C, kind="mem")], "parallel", 48)

    s4 = lambda w: (B_LOC, 4, SEQ // 4, w)
    s16 = lambda w: (B_LOC, 16, SEQ // 16, w)
    (du, dz, doa, dla, dobn, lsen, dlbn, dob4, lse4, dlb4, dob16, lse16, dlb16, doc, dlc, acc, g_out) = _middle(
        oa, o1, l1, o4.reshape(s4(W_B)), l4.reshape(s4(128)), o16.reshape(s16(W_B)), l16.reshape(s16(128)), oc, z,
        x2, tgt2, g_branch, ln_gain, ln_bias, wout, spread4, gather4, gather8)

    flat = lambda a: a.reshape(T, a.shape[-1])
    (dqa, dka, dva, dsink), (dqc, dmkv) = _run_parts("attn_bwd_a", [
        _attn_bwd(qa, 0, W_A, ka, 0, va, 0, W_KV_A, doa, lse_a, dla, sinkv=sinkv, **swa),
        _attn_bwd(qc, 0, W_C, mkv, 0, mkv, 1, W_C, doc, lse_c, dlc, kind="mem")], "arbitrary", 48)
    g_mem = _tn_matmul("dw_mem", memb, dmkv, D_MODEL, B_LOC * MEM_LEN)
    last = T // QR - 1
    (r_out, r_mem), (dqn, dkn, dvn), (dq4, dk4, dv4), (dq16, dk16, dv16) = _run_parts("attn_bwd_b", [
        _ReduceScatter([(SH_OUT, D_MODEL), (SH_MEM, 2 * W_C)]).part((g_out, g_mem), (0, 1, 2, last, last)),
        _attn_bwd(bn, 0, W_B, bn, 1, bn, 2, W_B, dobn, lsen, dlbn, **dil[0]),
        _attn_bwd(b4f, 0, W_B, b4f, 1, b4f, 2, W_B, flat(dob4), flat(lse4), flat(dlb4), **dil[1]),
        _attn_bwd(b16f, 0, W_B, b16f, 1, b16f, 2, W_B, flat(dob16), flat(lse16), flat(dlb16), **dil[2])],
        "arbitrary", 62)

    r4 = lambda a: a.reshape(s4(W_B))
    r16 = lambda a: a.reshape(s16(W_B))
    gx, dbin, g_in = _dh_dx(dqa, dka, dva, dqn, dkn, dvn, r4(dq4), r4(dk4), r4(dv4), r16(dq16), r16(dk16),
                            r16(dv16), dqc, dz, du, xb, cos, sa, sb, winT)
    return gx, g_in, r_out, r_mem, acc, dbin, dsink


def kernel(x, mem, w_in, b_in, w_mem, attn_sinks, g_branch, w_out, ln_gain, ln_bias, loss_target, m_w_in, m_b_in, m_w_mem, m_attn_sinks, m_g_branch, m_w_out, m_ln_gain, m_ln_bias, v_w_in, v_b_in, v_w_mem, v_attn_sinks, v_g_branch, v_w_out, v_ln_gain, v_ln_bias):
    winT, wout, wmem = _gather_weights(w_in[0].T, w_out[0], w_mem[0])
    gx, g_in, r_out, r_mem, acc, dbin, dsink = _local_step(
        x.reshape(T, D_MODEL), mem.reshape(B_LOC * MEM_LEN, D_MODEL), loss_target.reshape(T, D_MODEL),
        winT, wout, wmem, b_in, attn_sinks, g_branch, ln_gain, ln_bias)
    r_in, sv = _reduce_grads(g_in, acc, dbin, dsink)

    loss = jnp.sum(sv[0, :D_MODEL])
    small = ["b_in", "attn_sinks", "g_branch", "ln_gain", "ln_bias"]
    weights = dict(w_in=w_in, b_in=b_in, w_mem=w_mem, attn_sinks=attn_sinks, g_branch=g_branch, w_out=w_out,
                   ln_gain=ln_gain, ln_bias=ln_bias)
    ms = dict(w_in=m_w_in, b_in=m_b_in, w_mem=m_w_mem, attn_sinks=m_attn_sinks, g_branch=m_g_branch, w_out=m_w_out,
              ln_gain=m_ln_gain, ln_bias=m_ln_bias)
    vs = dict(w_in=v_w_in, b_in=v_b_in, w_mem=v_w_mem, attn_sinks=v_attn_sinks, g_branch=v_g_branch, w_out=v_w_out,
              ln_gain=v_ln_gain, ln_bias=v_ln_bias)
    out = dict(zip(small, _adamw_small(sv, [weights[n] for n in small], [ms[n] for n in small],
                                       [vs[n] for n in small])))
    d, nm, nv, g = (a.T[None] for a in _adamw("adamw_w_in", w_in[0].T, r_in, m_w_in[0].T, v_w_in[0].T, SH_IN // 4,
                                              copy_g=True))
    out["w_in"] = (g, d, nm, nv)
    for n, r in (("w_out", r_out), ("w_mem", r_mem)):
        d, nm, nv, g = (a[None] for a in _adamw("adamw_" + n, weights[n][0], r, ms[n][0], vs[n][0], copy_g=True))
        out[n] = (g, d, nm, nv)
    names = ["w_in", "b_in", "w_mem", "attn_sinks", "g_branch", "w_out", "ln_gain", "ln_bias"]
    return (loss, gx.reshape(B_LOC, SEQ, D_MODEL), *[out[n][k] for k in range(4) for n in names])
```

```python
import functools

import jax
import jax.numpy as jnp
from jax import lax
from jax.experimental import pallas as pl
from jax.experimental.pallas import tpu as pltpu

F32, BF16 = jnp.float32, jnp.bfloat16

D_MODEL = 1024
SEQ = 2048
B_LOC = 2
T = B_LOC * SEQ
BLK = 128
MEM_LEN = 256
W_A, W_KV_A, W_B, W_C, D_MIX = 512, 128, 256, 256, 1024
D_IN = 2816
O_QA, O_KA, O_VA, O_QB, O_KB, O_VB, O_QC, O_Z = 0, 512, 640, 768, 1024, 1280, 1536, 1792
ROPE_THETA = 10000.0
LN_EPS = 1e-5
RMS_EPS = 1e-6
ALPHA = 2.0 ** 0.25
QK_SCALE = 0.125
N_CHIP = 4
SH_IN, SH_OUT, SH_MEM = D_IN // N_CHIP, D_MIX // N_CHIP, D_MODEL // N_CHIP
NEG = -1e30
ADAM_LR, ADAM_B1, ADAM_B2, ADAM_EPS, ADAM_WD, ADAM_STEP = 0.001, 0.9, 0.999, 1e-08, 0.01, 10
SV_W = 3072
MESH = pl.DeviceIdType.MESH

NN = ((1,), (0,))
NT = ((1,), (1,))
TN = ((0,), (0,))


def _dot(a, b, dims):
    return lax.dot_general(a, b, (dims, ((), ())), preferred_element_type=F32)


def _cp(sem=None, vmem_mb=None):
    kw = {}
    if sem is not None:
        kw["dimension_semantics"] = sem
    if vmem_mb is not None:
        kw["vmem_limit_bytes"] = vmem_mb * 1024 * 1024
    return pltpu.CompilerParams(**kw)


def _sds(shape, dtype):
    return pltpu.HBM(shape, dtype)


def _vm_sds(shape, dtype):
    return jax.ShapeDtypeStruct(shape, dtype)


def _pin(*args):
    return [pltpu.with_memory_space_constraint(a, pltpu.HBM) for a in args]


def _full(shape):
    n = len(shape)
    return pl.BlockSpec(shape, lambda *_: (0,) * n)


def _shard_rows(ref, n, chip, half):
    start = pl.multiple_of((2 * chip[0] + chip[1]) * n + half * (n // 2), 16)
    return ref.at[pl.ds(start, n // 2), :]


def _gather_weights(win_sh, wout_sh, wmem_sh):
    half, piece = SH_IN // 2, SH_IN // 4

    def body(a_ref, b_ref, c_ref, oa_ref, ob_ref, oc_ref, ici_send, ici_recv, d2d_send, d2d_recv):
        x, y, c = lax.axis_index("x"), lax.axis_index("y"), lax.axis_index("c")
        me, sibling = (x, y, c), (x, y, 1 - c)
        xn, yn, dg = (1 - x, y), (x, 1 - y), (1 - x, 1 - y)
        for src, out, n in ((a_ref, oa_ref, SH_IN), (b_ref, ob_ref, SH_OUT), (c_ref, oc_ref, SH_MEM)):
            out[pl.ds(pl.multiple_of((2 * x + y) * n, 16), n), :] = src[...].astype(BF16)

        def rows(chip, hf, q):
            start = pl.multiple_of((2 * chip[0] + chip[1]) * SH_IN + hf * half + q * piece, 16)
            return oa_ref.at[pl.ds(start, piece), :]

        def copy(sems, k, chip, hf, q, to):
            blk = rows(chip, hf, q)
            return pltpu.make_async_remote_copy(
                src_ref=blk, dst_ref=blk, send_sem=sems[0].at[k], recv_sem=sems[1].at[k],
                device_id=to, device_id_type=MESH)

        ici, d2d = (ici_send, ici_recv), (d2d_send, d2d_recv)
        direct = [copy(ici, 0, (x, y), c, 0, (*xn, c)), copy(ici, 1, (x, y), c, 1, (*xn, c)),
                  copy(ici, 3, (x, y), c, 0, (*yn, c)), copy(ici, 4, (x, y), c, 1, (*yn, c))]
        for cp in direct:
            cp.start()
        arrivals = [(0, xn, 0), (1, xn, 1), (3, yn, 0), (4, yn, 1), (2, dg, 1), (5, dg, 0)]
        passed = []
        for k, chip, q in arrivals:
            copy(ici, k, chip, c, q, me).wait_recv()
            if k == 0:
                passed.append(copy(ici, 5, xn, c, 0, (*yn, c)))
                passed[-1].start()
            if k == 4:
                passed.append(copy(ici, 2, yn, c, 1, (*xn, c)))
                passed[-1].start()
            passed.append(copy(d2d, k, chip, c, q, sibling))
            passed[-1].start()
        for k, chip, q in arrivals:
            copy(d2d, k, chip, 1 - c, q, me).wait_recv()
        for cp in direct + passed:
            cp.wait_send()

    vm = pl.BlockSpec(memory_space=pltpu.VMEM)
    return pl.pallas_call(
        body, name="gather_weights",
        out_shape=(_vm_sds((D_IN, D_MODEL), BF16), _vm_sds((D_MIX, D_MODEL), BF16),
                   _vm_sds((D_MODEL, 2 * W_C), BF16)),
        in_specs=[vm, vm, vm], out_specs=(vm, vm, vm),
        scratch_shapes=[pltpu.SemaphoreType.DMA((6,))] * 4,
        compiler_params=_cp(vmem_mb=40),
    )(win_sh, wout_sh, wmem_sh)


def _rope(t, cos, sa, sb, sign):
    w = t.shape[1]
    reps = w // 128
    c, a, b = (jnp.tile(v, (1, reps)) if reps > 1 else v for v in (cos, sa, sb))
    rot = pltpu.roll(t, w - 32, 1) * a + pltpu.roll(t, 32, 1) * b
    return t * c + rot if sign > 0 else t * c - rot


def _in_proj(x, winT, b_in, cos, sa, sb, wout_own, wmem_own):
    tm = 512
    spt = SEQ // tm
    n_steps = T // tm
    forward_step = n_steps // 2

    def body(x_ref, w_ref, b_ref, cos_ref, sa_ref, sb_ref, wo_in, wm_in,
             xb_ref, qa_ref, ka_ref, va_ref, bn_ref, b4_ref, b16_ref, qc_ref, z_ref, wo_ref, wm_ref,
             scr, ici_send, ici_recv, d2d_send, d2d_recv):
        i = pl.program_id(0)
        mx, my, mc = lax.axis_index("x"), lax.axis_index("y"), lax.axis_index("c")
        chips = [(1 - mx, my), (mx, 1 - my), (1 - mx, 1 - my)]
        full = ((wo_ref, SH_OUT), (wm_ref, SH_MEM))

        def copy(sems, a, j, chip_of_block, half, to):
            blk = _shard_rows(full[a][0], full[a][1], chip_of_block, half)
            return pltpu.make_async_remote_copy(
                src_ref=blk, dst_ref=blk, send_sem=sems[0].at[a, j], recv_sem=sems[1].at[a, j],
                device_id=to, device_id_type=MESH)

        ici, d2d = (ici_send, ici_recv), (d2d_send, d2d_recv)
        pairs = [(a, j, chip) for j, chip in enumerate(chips) for a in range(2)]

        @pl.when(i == 0)
        def _():
            for a, j, chip in pairs:
                copy(ici, a, j, (mx, my), mc, (*chip, mc)).start()

        @pl.when(i == forward_step)
        def _():
            for a, j, chip in pairs:
                copy(ici, a, j, chip, mc, (mx, my, mc)).wait_recv()
                copy(d2d, a, j, chip, mc, (mx, my, 1 - mc)).start()

        @pl.when(i == n_steps - 1)
        def _():
            for a, j, chip in pairs:
                copy(d2d, a, j, chip, 1 - mc, (mx, my, mc)).wait_recv()
            for a, j, chip in pairs:
                copy(ici, a, j, (mx, my), mc, (*chip, mc)).wait_send()
                copy(d2d, a, j, chip, mc, (mx, my, 1 - mc)).wait_send()

        xb = x_ref[...].astype(BF16)
        xb_ref[...] = xb
        cos_t, sa_t, sb_t = cos_ref[...], sa_ref[...], sb_ref[...]

        def proj(r0, n):
            return _dot(xb, w_ref[r0:r0 + n, :], NT) + b_ref[:, r0:r0 + n]

        def rope(t):
            return _rope(t, cos_t, sa_t, sb_t, +1)

        qa_ref[...] = (rope(proj(O_QA, W_A)) * QK_SCALE).astype(BF16)
        ka_ref[...] = rope(proj(O_KA, W_KV_A)).astype(BF16)
        va_ref[...] = proj(O_VA, W_KV_A).astype(BF16)
        qc_ref[...] = (proj(O_QC, W_C) * QK_SCALE).astype(BF16)
        z_ref[...] = proj(O_Z, D_MIX).astype(BF16)
        parts = (rope(proj(O_QB, W_B)) * QK_SCALE, rope(proj(O_KB, W_B)), proj(O_VB, W_B))
        for k, part in enumerate(parts):
            bn_ref[:, 256 * k:256 * (k + 1)] = part.astype(BF16)
            scr[2 * k] = part[:, :128]
            scr[2 * k + 1] = part[:, 128:]
        for j in range(6):
            for res in range(4):
                b4_ref[0, res, :, 128 * j:128 * (j + 1)] = scr[j, pl.ds(res, tm // 4, stride=4), :].astype(BF16)
            for res in range(16):
                b16_ref[0, res, :, 128 * j:128 * (j + 1)] = scr[j, pl.ds(res, tm // 16, stride=16), :].astype(BF16)

    tok = lambda w: pl.BlockSpec((tm, w), lambda i: (i, 0))
    tab = pl.BlockSpec((tm, 128), lambda i: (i % spt, 0))
    hbm = pl.BlockSpec(memory_space=pl.ANY)
    return pl.pallas_call(
        body, name="in_proj", grid=(n_steps,),
        in_specs=[tok(D_MODEL), _full((D_IN, D_MODEL)), _full((1, D_IN)), tab, tab, tab, hbm, hbm],
        out_specs=(tok(D_MODEL), tok(W_A), tok(W_KV_A), tok(W_KV_A), tok(768),
                   pl.BlockSpec((1, 4, tm // 4, 768), lambda i: (i // spt, 0, i % spt, 0)),
                   pl.BlockSpec((1, 16, tm // 16, 768), lambda i: (i // spt, 0, i % spt, 0)),
                   tok(W_C), tok(D_MIX), hbm, hbm),
        out_shape=(_sds((T, D_MODEL), BF16), _sds((T, W_A), BF16), _sds((T, W_KV_A), BF16), _sds((T, W_KV_A), BF16),
                   _sds((T, 768), BF16), _sds((B_LOC, 4, SEQ // 4, 768), BF16), _sds((B_LOC, 16, SEQ // 16, 768), BF16),
                   _sds((T, W_C), BF16), _sds((T, D_MIX), BF16),
                   _sds((D_MIX, D_MODEL), BF16), _sds((D_MODEL, 2 * W_C), BF16)),
        input_output_aliases={6: 9, 7: 10},
        scratch_shapes=[pltpu.VMEM((6, tm, 128), F32)] + [pltpu.SemaphoreType.DMA((2, 3))] * 4,
        compiler_params=_cp(("arbitrary",), vmem_mb=48),
    )(*_pin(x, winT, b_in, cos, sa, sb, wout_own, wmem_own))


def _mem_kv(mem, wmem):
    def body(m_ref, w_ref, mb_ref, kv_ref):
        mb = m_ref[...].astype(BF16)
        mb_ref[...] = mb
        kv_ref[...] = _dot(mb, w_ref[...], NN).astype(BF16)

    n = B_LOC * MEM_LEN
    return pl.pallas_call(
        body, name="mem_kv",
        out_shape=(_sds((n, D_MODEL), BF16), _sds((n, 2 * W_C), BF16)),
    )(*_pin(mem, wmem))


class _Part:
    def __init__(self, body, args, in_specs, out_specs, out_shape, scratch=()):
        self.body, self.args, self.in_specs, self.out_specs, self.out_shape = body, args, in_specs, out_specs, out_shape
        self.scratch = list(scratch)


def _run_parts(name, parts, semantics, vmem_mb):
    n_in = [len(p.args) for p in parts]
    n_out = [len(p.out_shape) for p in parts]
    n_scr = [len(p.scratch) for p in parts]

    def body(*refs):
        ins, outs, scr = refs[:sum(n_in)], refs[sum(n_in):sum(n_in) + sum(n_out)], refs[sum(n_in) + sum(n_out):]
        i0 = o0 = s0 = 0
        for p, ni, no, ns in zip(parts, n_in, n_out, n_scr):
            p.body(*ins[i0:i0 + ni], *outs[o0:o0 + no], *scr[s0:s0 + ns])
            i0, o0, s0 = i0 + ni, o0 + no, s0 + ns

    res = pl.pallas_call(
        body, name=name, grid=(T // QR,),
        in_specs=[sp for p in parts for sp in p.in_specs], out_specs=tuple(sp for p in parts for sp in p.out_specs),
        out_shape=tuple(sh for p in parts for sh in p.out_shape),
        scratch_shapes=[sc for p in parts for sc in p.scratch],
        compiler_params=_cp((semantics,), vmem_mb=vmem_mb),
    )(*_pin(*[a for p in parts for a in p.args]))
    out, o0 = [], 0
    for no in n_out:
        out.append(tuple(res[o0:o0 + no]))
        o0 += no
    return out


QB = 8
QR = QB * BLK


def _lane_lo():
    return lax.broadcasted_iota(jnp.int32, (1, 128), 1) < 64


def _dup_head(k2, hk, lo):
    kf = k2.astype(F32)
    r = pltpu.roll(kf, 64, 1)
    return (jnp.where(lo, kf, r) if hk == 0 else jnp.where(lo, r, kf)).astype(BF16)


def _stack_heads(pairs, lo):
    parts = []
    for x2 in pairs:
        z = jnp.zeros_like(x2)
        parts += [jnp.where(lo, x2, z), jnp.where(lo, z, x2)]
    return jnp.concatenate(parts, axis=0)


def _prev_mode(kind, nb, j):
    if kind == "mem" or nb == 1:
        return "no"
    if nb <= QB:
        return "yes" if j % nb else "no"
    return "yes" if j else "dyn"


class _Attn:
    def __init__(self, kind, nb, max_dist, gqa, qw, kvw, qcb, kcb, vcb):
        self.kind, self.nb, self.gqa, self.qw, self.kvw = kind, nb, gqa, qw, kvw
        npairs = qw // 128
        self.groups = ([(hk, [2 * hk, 2 * hk + 1]) for hk in range(npairs // 2)] if gqa
                       else [(p, [p]) for p in range(npairs)])
        self.nh = 2 * len(self.groups[0][1])
        self.cols = 128 * self.nh
        self.reach = BLK - max_dist
        self.ext_prev = kind == "band" and nb > QB
        self.q_spec = pl.BlockSpec((QR, qw), lambda g: (g, qcb))
        self.row_spec = pl.BlockSpec((QR, qw), lambda g: (g, 0))
        self.stat_spec = pl.BlockSpec((QR, 128), lambda g: (g, 0))
        if kind == "mem":
            per = SEQ // QR
            self.kv_specs = [pl.BlockSpec((MEM_LEN, kvw), lambda g: (g // per, kcb)),
                             pl.BlockSpec((MEM_LEN, kvw), lambda g: (g // per, vcb))]
        else:
            self.kv_specs = [pl.BlockSpec((QR, kvw), lambda g: (g, kcb)), pl.BlockSpec((QR, kvw), lambda g: (g, vcb))]
            if self.ext_prev:
                self.kv_specs += [pl.BlockSpec((BLK, kvw), lambda g: (jnp.maximum(g * QB - 1, 0), kcb)),
                                  pl.BlockSpec((BLK, kvw), lambda g: (jnp.maximum(g * QB - 1, 0), vcb))]

    def masks(self):
        if self.kind == "mem":
            return None
        kj = lax.broadcasted_iota(jnp.int32, (2 * BLK, self.cols), 0)
        qi = lax.broadcasted_iota(jnp.int32, (2 * BLK, self.cols), 1) & (BLK - 1)
        kj1 = lax.broadcasted_iota(jnp.int32, (BLK, self.cols), 0)
        qi1 = lax.broadcasted_iota(jnp.int32, (BLK, self.cols), 1) & (BLK - 1)
        return kj, qi, kj1 <= qi1

    def keys(self, j, gi, kc_ref, vc_ref, kp_ref, vp_ref, lo, kq, g):
        def kv(k_ref, v_ref, r):
            if self.gqa:
                return _dup_head(k_ref[r, :], gi, lo), _dup_head(v_ref[r, :], gi, lo)
            sl = slice(128 * gi, 128 * (gi + 1))
            return k_ref[r, sl], v_ref[r, sl]

        if self.kind == "mem":
            key0 = pl.multiple_of((g // (SEQ // QR)) * MEM_LEN, MEM_LEN)
            return (*kv(kc_ref, vc_ref, slice(None)), None, [(0, MEM_LEN, key0)])
        kj, qi, cur = kq
        row0 = g * QR + BLK * j
        mode = _prev_mode(self.kind, self.nb, j)
        if mode == "no":
            return (*kv(kc_ref, vc_ref, slice(BLK * j, BLK * (j + 1))), cur, [(0, BLK, pl.multiple_of(row0, BLK))])
        if mode == "yes":
            mask = jnp.logical_and(kj >= qi + self.reach, kj <= qi + BLK)
            return (*kv(kc_ref, vc_ref, slice(BLK * (j - 1), BLK * (j + 1))), mask,
                    [(0, 2 * BLK, pl.multiple_of(row0 - BLK, BLK))])
        has_prev = ((g * QB) % self.nb) > 0
        hp = has_prev.astype(jnp.int32)
        mask = jnp.logical_and(kj >= qi * hp + (self.reach * hp + BLK * (1 - hp)), kj <= qi + BLK)
        kp, vp = kv(kp_ref, vp_ref, slice(None))
        kc, vc = kv(kc_ref, vc_ref, slice(0, BLK))
        return (jnp.concatenate([kp, kc], axis=0), jnp.concatenate([vp, vc], axis=0), mask,
                [(0, BLK, pl.multiple_of(jnp.maximum(row0 - BLK, 0), BLK)), (BLK, BLK, pl.multiple_of(row0, BLK))])


def _attn_fwd(q, qcb, qw, k, kcb, v, vcb, kvw, *, kind, nb=1, max_dist=BLK, gqa=False, sinks=None):
    a = _Attn(kind, nb, max_dist, gqa, qw, kvw, qcb, kcb, vcb)

    def body(*refs):
        it = iter(refs)
        q_ref, kc_ref, vc_ref = next(it), next(it), next(it)
        kp_ref, vp_ref = (next(it), next(it)) if a.ext_prev else (None, None)
        sink_ref = next(it) if sinks is not None else None
        o_ref, lse_ref = next(it), next(it)
        g = pl.program_id(0)
        lo = _lane_lo()
        top = lax.broadcasted_iota(jnp.int32, (128, 1), 0) < 64
        rid = lax.broadcasted_iota(jnp.int32, (8, 128), 0)
        kq = a.masks()
        stats = {}

        def scores(j, gi, pairs):
            rows = slice(BLK * j, BLK * (j + 1))
            qs = _stack_heads([q_ref[rows, 128 * p:128 * (p + 1)] for p in pairs], lo)
            kk, vv, mask, _ = a.keys(j, gi, kc_ref, vc_ref, kp_ref, vp_ref, lo, kq, g)
            pieces = [slice(r0, r0 + BLK) for r0 in range(0, kk.shape[0], BLK)]
            return dict(j=j, gi=gi, pairs=pairs, rows=rows, vv=vv, mask=mask, pieces=pieces,
                        ss=[_dot(kk[r], qs, NT) for r in pieces])

        def softmax(c):
            gi, mask = c["gi"], c["mask"]
            ss = [s if mask is None else jnp.where(mask[r], s, NEG) for r, s in zip(c["pieces"], c.pop("ss"))]
            m = jnp.max(ss[0], axis=0, keepdims=True)
            for s in ss[1:]:
                m = jnp.maximum(m, jnp.max(s, axis=0, keepdims=True))
            if sink_ref is not None:
                sk = jnp.concatenate([jnp.full((1, 128), sink_ref[0, a.nh * gi + i], F32) for i in range(a.nh)], axis=1)
                m = jnp.maximum(m, sk)
            ps = [jnp.exp(s - m) for s in ss]
            l = sum(jnp.sum(p, axis=0, keepdims=True) for p in ps)
            if sink_ref is not None:
                l = l + jnp.exp(sk - m)
            c["ps"] = [p.astype(BF16) for p in ps]
            c["l"], c["lse"] = l, m + jnp.log(l)

        def outputs(c):
            j, gi, rows = c["j"], c["gi"], c["rows"]
            ot = sum(_dot(c["vv"][r], p, TN) for r, p in zip(c["pieces"], c["ps"]))
            ot = ot * pl.reciprocal(c["l"], approx=True)
            for i, p in enumerate(c["pairs"]):
                o2t = jnp.where(top, ot[:, 256 * i:256 * i + 128], ot[:, 256 * i + 128:256 * i + 256])
                o_ref[rows, 128 * p:128 * (p + 1)] = o2t.T.astype(BF16)
            stat = stats.get(j, jnp.zeros((8, 128), F32))
            for i in range(a.nh):
                stat = jnp.where(rid == a.nh * gi + i, c["lse"][:, 128 * i:128 * (i + 1)], stat)
            stats[j] = stat
            if gi == a.groups[-1][0]:
                lse_ref[rows, :] = jnp.concatenate([stats.pop(j), jnp.zeros((120, 128), F32)], axis=0).T

        chains = [(j, gi, pairs) for j in range(QB) for gi, pairs in a.groups]
        live = {}
        for t in range(len(chains) + 2):
            if t < len(chains):
                live[t] = scores(*chains[t])
            if 0 <= t - 1 < len(chains):
                softmax(live[t - 1])
            if 0 <= t - 2 < len(chains):
                outputs(live.pop(t - 2))


    args = [q, k, v] + ([k, v] if a.ext_prev else [])
    in_specs = [a.q_spec] + a.kv_specs
    if sinks is not None:
        args.append(sinks)
        in_specs.append(pl.BlockSpec(memory_space=pltpu.SMEM))
    return _Part(body, args, in_specs, [a.row_spec, a.stat_spec], [_sds((T, qw), BF16), _sds((T, 128), F32)])


def _attn_bwd(q, qcb, qw, k, kcb, v, vcb, kvw, do, lse, dl, *, kind, nb=1, max_dist=BLK, gqa=False, sinkv=None,
              mem_in=None):
    a = _Attn(kind, nb, max_dist, gqa, qw, kvw, qcb, kcb, vcb)

    def body(*refs):
        it = iter(refs)
        q_ref, kc_ref, vc_ref = next(it), next(it), next(it)
        kp_ref, vp_ref = (next(it), next(it)) if a.ext_prev else (None, None)
        do_ref, lse_ref, dl_ref = next(it), next(it), next(it)
        sinkv_ref = next(it) if sinkv is not None else None
        mem_ref = next(it) if kind == "mem" else None
        dq_ref = next(it)
        if kind == "mem":
            gmem_ref = next(it)
        else:
            dk_out, dv_out = next(it), next(it)
        dsink_ref = next(it) if sinkv is not None else None
        if kind != "mem":
            dk_ref, dv_ref, stage_k, stage_v, flush_sem = next(it), next(it), next(it), next(it), next(it)
        else:
            dkv_ref = next(it)
        g = pl.program_id(0)
        lo = _lane_lo()
        top = lax.broadcasted_iota(jnp.int32, (128, 1), 0) < 64

        @pl.when(g == 0)
        def _():
            if kind == "mem":
                dkv_ref[...] = jnp.zeros_like(dkv_ref)
            else:
                dk_ref[...] = jnp.zeros_like(dk_ref)
                dv_ref[...] = jnp.zeros_like(dv_ref)
            if dsink_ref is not None:
                dsink_ref[...] = jnp.zeros_like(dsink_ref)

        kq = a.masks()
        stats_t = {}

        def first_matmuls(j, gi, pairs):
            rows = slice(BLK * j, BLK * (j + 1))
            if j not in stats_t:
                stats_t[j] = (lse_ref[rows, :].T, dl_ref[rows, :].T)
            lse_t, dl_t = stats_t[j]
            heads = [a.nh * gi + i for i in range(a.nh)]
            c = dict(rows=rows, gi=gi, pairs=pairs)
            c["qs"] = _stack_heads([q_ref[rows, 128 * p:128 * (p + 1)] for p in pairs], lo)
            c["dos"] = _stack_heads([do_ref[rows, 128 * p:128 * (p + 1)] for p in pairs], lo)
            c["lse_row"] = jnp.concatenate([lse_t[h:h + 1, :] for h in heads], axis=1)
            c["dl_row"] = jnp.concatenate([dl_t[h:h + 1, :] for h in heads], axis=1)
            c["kk"], vv, c["mask"], c["dests"] = a.keys(j, gi, kc_ref, vc_ref, kp_ref, vp_ref, lo, kq, g)
            c["s"] = _dot(c["kk"], c["qs"], NT)
            c["dp"] = _dot(vv, c["dos"], NT)
            return c

        def elementwise(c):
            s = c.pop("s")
            if c["mask"] is not None:
                s = jnp.where(c["mask"], s, NEG)
            p = jnp.exp(s - c["lse_row"])
            c["ds"] = (p * (c.pop("dp") - c["dl_row"])).astype(BF16)
            c["p"] = p.astype(BF16)

        def last_matmuls(c):
            gi, rows = c["gi"], c["rows"]
            dqt = _dot(c["kk"], c["ds"], TN)
            ck = _dot(c["ds"], c["qs"], NN)
            cv = _dot(c["p"], c["dos"], NN)
            if gqa:
                sel = lo if gi == 0 else jnp.logical_not(lo)
                ck = jnp.where(sel, ck + pltpu.roll(ck, 64, 1), 0.0)
                cv = jnp.where(sel, cv + pltpu.roll(cv, 64, 1), 0.0)
                kcols = slice(0, 128)
            else:
                kcols = slice(128 * gi, 128 * (gi + 1))
            for r0, nr, key0 in c["dests"]:
                krows = pl.ds(key0, nr)
                if kind == "mem":
                    dkv_ref[krows, kcols] += ck[r0:r0 + nr]
                    dkv_ref[krows, slice(kvw + kcols.start, kvw + kcols.stop)] += cv[r0:r0 + nr]
                else:
                    dk_ref[krows, kcols] += ck[r0:r0 + nr]
                    dv_ref[krows, kcols] += cv[r0:r0 + nr]
            for i, p in enumerate(c["pairs"]):
                dq2t = jnp.where(top, dqt[:, 256 * i:256 * i + 128], dqt[:, 256 * i + 128:256 * i + 256])
                dq_ref[rows, 128 * p:128 * (p + 1)] = dq2t.T.astype(BF16)

        chains = [(j, gi, pairs) for j in range(QB) for gi, pairs in a.groups]
        live = {}
        for t in range(len(chains) + 2):
            if t < len(chains):
                live[t] = first_matmuls(*chains[t])
            if 0 <= t - 1 < len(chains):
                elementwise(live[t - 1])
            if 0 <= t - 2 < len(chains):
                last_matmuls(live.pop(t - 2))
        if dsink_ref is not None:
            ps = jnp.exp(sinkv_ref[...] - lse_ref[...]) * dl_ref[...]
            dsink_ref[...] += jnp.sum(ps, axis=0, keepdims=True)
        if kind == "mem":
            @pl.when(g == T // QR - 1)
            def _():
                gmem_ref[...] = _dot(mem_ref[...], dkv_ref[...].astype(BF16), TN)
        else:
            n_steps = T // QR

            def flush(step):
                rows = pl.ds(pl.multiple_of(step * QR, QR), QR)
                out = []
                for acc, stage, dst, i in ((dk_ref, stage_k, dk_out, 0), (dv_ref, stage_v, dv_out, 1)):
                    stage[...] = acc[rows, :].astype(BF16)
                    out.append(pltpu.make_async_copy(stage, dst.at[rows, :], flush_sem.at[i]))
                return out

            def flushed(step):
                rows = pl.ds(pl.multiple_of(step * QR, QR), QR)
                return [pltpu.make_async_copy(stage, dst.at[rows, :], flush_sem.at[i])
                        for stage, dst, i in ((stage_k, dk_out, 0), (stage_v, dv_out, 1))]

            @pl.when(g >= 2)
            def _():
                for cp in flushed(g - 2):
                    cp.wait()

            @pl.when(g >= 1)
            def _():
                for cp in flush(g - 1):
                    cp.start()

            @pl.when(g == n_steps - 1)
            def _():
                for cp in flushed(g - 1):
                    cp.wait()
                for cp in flush(g):
                    cp.start()
                for cp in flushed(g):
                    cp.wait()

    args = [q, k, v] + ([k, v] if a.ext_prev else []) + [do, lse, dl]
    in_specs = [a.q_spec] + a.kv_specs + [a.row_spec, a.stat_spec, a.stat_spec]
    if sinkv is not None:
        args.append(sinkv)
        in_specs.append(_full((1, 128)))
    if kind == "mem":
        args.append(mem_in)
        in_specs.append(pl.BlockSpec(mem_in.shape, lambda g: (0, 0), pipeline_mode=pl.Buffered(1)))
    out_shape = [_sds((T, qw), BF16)]
    out_specs = [a.row_spec]
    scratch = []
    if kind == "mem":
        out_shape.append(_sds((D_MODEL, 2 * kvw), F32))
        out_specs.append(pl.BlockSpec((D_MODEL, 2 * kvw), lambda g: (0, 0), pipeline_mode=pl.Buffered(1)))
        scratch = [pltpu.VMEM((B_LOC * MEM_LEN, 2 * kvw), F32)]
    else:
        out_shape += [_sds((T, kvw), BF16)] * 2
        out_specs += [pl.BlockSpec(memory_space=pl.ANY)] * 2
        scratch = [pltpu.VMEM((T, kvw), F32)] * 2 + [pltpu.VMEM((QR, kvw), BF16)] * 2 + [pltpu.SemaphoreType.DMA((2,))]
    if sinkv is not None:
        out_shape.append(_sds((1, 128), F32))
        out_specs.append(_full((1, 128)))
    return _Part(body, args, in_specs, out_specs, out_shape, scratch)


def _dot2(v, w_ref):
    hi = v.astype(BF16)
    lo = (v - hi.astype(F32)).astype(BF16)
    return _dot(hi, w_ref[...], NN) + _dot(lo, w_ref[...], NN)


def _middle(oa, o1, l1, o4, l4, o16, l16, oc, z, x, tgt, g_br, ln_g, ln_b, wout, spread4, gather4, gather8):
    tm = 512
    spt = SEQ // tm

    def body(oa_ref, o1_ref, l1_ref, o4_ref, l4_ref, o16_ref, l16_ref, oc_ref, z_ref, x_ref, t_ref,
             g_ref, lg_ref, lb_ref, w_ref, sp4_ref, ga4_ref, ga8_ref,
             du_ref, dz_ref, doa_ref, dla_ref,
             dobn_ref, lsen_ref, dlbn_ref, dob4_ref, lse4_ref, dlb4_ref, dob16_ref, lse16_ref, dlb16_ref,
             doc_ref, dlc_ref, acc_ref, gout_ref, scr):
        i = pl.program_id(0)

        @pl.when(i == 0)
        def _():
            acc_ref[...] = jnp.zeros_like(acc_ref)
            gout_ref[...] = jnp.zeros_like(gout_ref)

        for res in range(4):
            rows = pl.ds(res, tm // 4, stride=4)
            for j in range(2):
                scr[j, rows, :] = o4_ref[0, res, :, 128 * j:128 * (j + 1)].astype(F32)
            scr[2, rows, :] = l4_ref[0, res]
        for res in range(16):
            rows = pl.ds(res, tm // 16, stride=16)
            for j in range(2):
                scr[3 + j, rows, :] = o16_ref[0, res, :, 128 * j:128 * (j + 1)].astype(F32)
            scr[5, rows, :] = l16_ref[0, res]
        inv_d = 1.0 / D_MODEL
        gb, lg, lb = g_ref[...], lg_ref[...], lb_ref[...]

        def rms(o):
            r = lax.rsqrt(jnp.sum(o * o, axis=1, keepdims=True) * (1.0 / o.shape[1]) + RMS_EPS)
            return o * r, r

        def rms_bwd(dn_, n_, r):
            return r * (dn_ - n_ * (jnp.sum(dn_ * n_, axis=1, keepdims=True) * (1.0 / n_.shape[1])))

        def forward(rs):
            o4v = jnp.concatenate([scr[0, rs, :], scr[1, rs, :]], axis=1)
            o16v = jnp.concatenate([scr[3, rs, :], scr[4, rs, :]], axis=1)
            l1v, l4v, l16v = l1_ref[rs, :], scr[2, rs, :], scr[5, rs, :]
            mx = jnp.maximum(jnp.maximum(l1v, l4v), l16v)
            e1, e4, e16 = jnp.exp(l1v - mx), jnp.exp(l4v - mx), jnp.exp(l16v - mx)
            ssum = e1 + e4 + e16
            inv = 1.0 / ssum
            c = dict(rs=rs, lse_b=mx + jnp.log(ssum))
            c["ob"] = (_dot2(e1 * inv, sp4_ref) * o1_ref[rs, :].astype(F32) + _dot2(e4 * inv, sp4_ref) * o4v
                       + _dot2(e16 * inv, sp4_ref) * o16v)
            c["oa"], c["oc"] = oa_ref[rs, :].astype(F32), oc_ref[rs, :].astype(F32)
            na, c["ra"] = rms(c["oa"])
            nb_, c["rb"] = rms(c["ob"])
            nc, c["rc"] = rms(c["oc"])
            c["n"] = jnp.concatenate([na, nb_, nc], axis=1)
            c["zf"] = z_ref[rs, :].astype(F32)
            c["sig"] = 1.0 / (1.0 + jnp.exp(-c["zf"]))
            c["sz"] = c["zf"] * c["sig"]
            c["yb"] = (c["n"] * gb * c["sz"]).astype(BF16)
            c["y2"] = _dot(c["yb"], w_ref[...], NN)
            return c

        def norm(c):
            rs = c["rs"]
            u = ALPHA * x_ref[rs, :] + c.pop("y2")
            mu = jnp.sum(u, axis=1, keepdims=True) * inv_d
            uc = u - mu
            rstd = lax.rsqrt(jnp.sum(uc * uc, axis=1, keepdims=True) * inv_d + LN_EPS)
            xh = uc * rstd
            diff = xh * lg + lb - t_ref[rs, :]
            acc_ref[0:1, :] += jnp.sum(diff * diff, axis=0, keepdims=True) * (0.5 * inv_d)
            dout = diff * inv_d
            acc_ref[2:3, :] += jnp.sum(dout * xh, axis=0, keepdims=True)
            acc_ref[3:4, :] += jnp.sum(dout, axis=0, keepdims=True)
            dxh = dout * lg
            du = rstd * (dxh - jnp.sum(dxh, axis=1, keepdims=True) * inv_d
                         - xh * (jnp.sum(dxh * xh, axis=1, keepdims=True) * inv_d))
            dub = du.astype(BF16)
            du_ref[rs, :] = dub
            c["dy"] = _dot(dub, w_ref[...], NT)
            gout_ref[...] += _dot(c.pop("yb"), dub, TN)

        def backward(c):
            rs, n, dy, zf, sig = c["rs"], c["n"], c["dy"], c["zf"], c["sig"]
            t1 = dy * c["sz"]
            acc_ref[1:2, :] += jnp.sum(t1 * n, axis=0, keepdims=True)
            dn = t1 * gb
            dz_ref[rs, :] = (dy * n * gb * (sig * (1.0 + zf * (1.0 - sig)))).astype(BF16)
            doa = rms_bwd(dn[:, :W_A], n[:, :W_A], c["ra"])
            dob = rms_bwd(dn[:, W_A:W_A + W_B], n[:, W_A:W_A + W_B], c["rb"])
            doc = rms_bwd(dn[:, W_A + W_B:], n[:, W_A + W_B:], c["rc"])
            doa_ref[rs, :] = doa.astype(BF16)
            dla_ref[rs, :] = _dot2(doa * c["oa"], ga8_ref)
            doc_ref[rs, :] = doc.astype(BF16)
            dlc_ref[rs, :] = _dot2(doc * c["oc"], ga4_ref)
            dobn_ref[rs, :] = dob.astype(BF16)
            lsen_ref[rs, :] = c["lse_b"]
            dlbn_ref[rs, :] = _dot2(dob * c["ob"], ga4_ref)
            scr[0, rs, :] = dob[:, :128]
            scr[1, rs, :] = dob[:, 128:]

        halves = [slice(h * (tm // 2), (h + 1) * (tm // 2)) for h in range(2)]
        live = {}
        for t in range(len(halves) + 2):
            if t < len(halves):
                live[t] = forward(halves[t])
            if 0 <= t - 1 < len(halves):
                norm(live[t - 1])
            if 0 <= t - 2 < len(halves):
                backward(live.pop(t - 2))
        for j in range(2):
            sl = slice(128 * j, 128 * (j + 1))
            for res in range(4):
                dob4_ref[0, res, :, sl] = scr[j, pl.ds(res, tm // 4, stride=4), :].astype(BF16)
            for res in range(16):
                dob16_ref[0, res, :, sl] = scr[j, pl.ds(res, tm // 16, stride=16), :].astype(BF16)
        for res in range(4):
            rows = pl.ds(res, tm // 4, stride=4)
            lse4_ref[0, res] = lsen_ref[rows, :]
            dlb4_ref[0, res] = dlbn_ref[rows, :]
        for res in range(16):
            rows = pl.ds(res, tm // 16, stride=16)
            lse16_ref[0, res] = lsen_ref[rows, :]
            dlb16_ref[0, res] = dlbn_ref[rows, :]


    tok = lambda w: pl.BlockSpec((tm, w), lambda i: (i, 0))
    p4 = lambda w: pl.BlockSpec((1, 4, tm // 4, w), lambda i: (i // spt, 0, i % spt, 0))
    p16 = lambda w: pl.BlockSpec((1, 16, tm // 16, w), lambda i: (i // spt, 0, i % spt, 0))
    s4 = lambda w, dt: _sds((B_LOC, 4, SEQ // 4, w), dt)
    s16 = lambda w, dt: _sds((B_LOC, 16, SEQ // 16, w), dt)
    row = _full((1, D_MODEL))
    return pl.pallas_call(
        body, name="middle", grid=(T // tm,),
        in_specs=[tok(W_A), tok(W_B), tok(128), p4(W_B), p4(128), p16(W_B), p16(128), tok(W_C), tok(D_MIX),
                  tok(D_MODEL), tok(D_MODEL), row, row, row, _full((D_MIX, D_MODEL)),
                  _full((128, W_B)), _full((W_B, 128)), _full((W_A, 128))],
        out_specs=(tok(D_MODEL), tok(D_MIX), tok(W_A), tok(128),
                   tok(W_B), tok(128), tok(128), p4(W_B), p4(128), p4(128), p16(W_B), p16(128), p16(128),
                   tok(W_C), tok(128), _full((8, D_MODEL)), _full((D_MIX, D_MODEL))),
        out_shape=(_sds((T, D_MODEL), BF16), _sds((T, D_MIX), BF16),
                   _sds((T, W_A), BF16), _sds((T, 128), F32),
                   _sds((T, W_B), BF16), _sds((T, 128), F32), _sds((T, 128), F32),
                   s4(W_B, BF16), s4(128, F32), s4(128, F32), s16(W_B, BF16), s16(128, F32), s16(128, F32),
                   _sds((T, W_C), BF16), _sds((T, 128), F32), _sds((8, D_MODEL), F32),
                   _sds((D_MIX, D_MODEL), F32)),
        scratch_shapes=[pltpu.VMEM((6, tm, 128), F32)],
        compiler_params=_cp(("arbitrary",), vmem_mb=56),
    )(*_pin(oa, o1, l1, o4, l4, o16, l16, oc, z, x, tgt, g_br, ln_g, ln_b, wout, spread4, gather4, gather8))


class _ReduceScatter:
    def __init__(self, shapes):
        self.shapes = shapes

    def scratch_shapes(self):
        out = []
        for n, w in self.shapes:
            h, p = n // 2, n // 4
            out += [pltpu.VMEM((4, h, w), F32), pltpu.VMEM((4, h, w), F32), pltpu.VMEM((6, p, w), BF16),
                    pltpu.VMEM((6, p, w), BF16), pltpu.VMEM((2, p, w), F32), pltpu.VMEM((h, w), F32)]
        na = len(self.shapes)
        dma = pltpu.SemaphoreType.DMA
        return out + [dma((na, 4)), dma((na, 4)), dma((na, 4)), dma((na, 6)), dma((na, 6)), dma((na,)), dma((na,)),
                      dma((na,))]

    def bind(self, g_refs, r_refs, scratch):
        na = len(self.shapes)
        bufs = [scratch[6 * a:6 * a + 6] for a in range(na)]
        mine, sib, stage, land, keep, tot = (tuple(b[i] for b in bufs) for i in range(6))
        loc_sem, s1_send, s1_recv, s2_send, s2_recv, s3_send, s3_recv, st_sem = scratch[6 * na:6 * na + 8]
        x, y, c = lax.axis_index("x"), lax.axis_index("y"), lax.axis_index("c")
        me, sibling = (x, y, c), (x, y, 1 - c)
        xn, yn, dg = (1 - x, y), (x, 1 - y), (1 - x, 1 - y)
        idx = lambda chip: 2 * chip[0] + chip[1]
        my_chip = idx((x, y))
        order = [idx(xn), idx(dg), idx(yn), my_chip]

        def rows(a, k, half):
            n = self.shapes[a][0]
            return pl.ds(pl.multiple_of(k * n + half * (n // 2), 8), n // 2)

        def piece(a, q):
            p = self.shapes[a][0] // 4
            return slice(q * p, (q + 1) * p)

        def load(a, k):
            return pltpu.make_async_copy(g_refs[a].at[rows(a, k, c), :], mine[a].at[k], loc_sem.at[a, k])

        def s1(a, k, half):
            return pltpu.make_async_remote_copy(
                src_ref=g_refs[a].at[rows(a, k, half), :], dst_ref=sib[a].at[k],
                send_sem=s1_send.at[a, k], recv_sem=s1_recv.at[a, k], device_id=sibling, device_id_type=MESH)

        def s2(a, i, to):
            return pltpu.make_async_remote_copy(
                src_ref=stage[a].at[i], dst_ref=land[a].at[i], send_sem=s2_send.at[a, i], recv_sem=s2_recv.at[a, i],
                device_id=to, device_id_type=MESH)

        via = {0: xn, 1: xn, 2: yn, 3: yn, 4: yn, 5: xn}

        def s3(a, half, to):
            return pltpu.make_async_remote_copy(
                src_ref=tot[a], dst_ref=r_refs[a].at[rows(a, 0, half), :], send_sem=s3_send.at[a],
                recv_sem=s3_recv.at[a], device_id=to, device_id_type=MESH)

        def store(a):
            return pltpu.make_async_copy(tot[a], r_refs[a].at[rows(a, 0, c), :], st_sem.at[a])

        def start():
            for k in order:
                for a in range(na):
                    load(a, k).start()
                    s1(a, k, 1 - c).start()

        def chip_sum(a, k):
            load(a, k).wait()
            s1(a, k, c).wait_recv()
            return mine[a][k] + sib[a][k]

        def exchange():
            for a in range(na):
                P, Q = piece(a, 0), piece(a, 1)
                s_xn = chip_sum(a, idx(xn))
                stage[a][0] = s_xn[P].astype(BF16)
                keep[a][1] = s_xn[Q]
                s_dg = chip_sum(a, idx(dg))
                stage[a][1] = s_dg[P].astype(BF16)
                s2(a, 0, (*xn, c)).start()
                s2(a, 1, (*xn, c)).start()
                stage[a][3] = s_dg[Q].astype(BF16)
                s_yn = chip_sum(a, idx(yn))
                stage[a][2] = s_yn[Q].astype(BF16)
                keep[a][0] = s_yn[P]
                s2(a, 2, (*yn, c)).start()
                s2(a, 3, (*yn, c)).start()
                tot[a][...] = chip_sum(a, my_chip)

        def relay():
            for a in range(na):
                P, Q = piece(a, 0), piece(a, 1)
                s2(a, 1, me).wait_recv()
                stage[a][4] = (keep[a][0] + land[a][1].astype(F32)).astype(BF16)
                s2(a, 4, (*yn, c)).start()
                s2(a, 3, me).wait_recv()
                stage[a][5] = (keep[a][1] + land[a][3].astype(F32)).astype(BF16)
                s2(a, 5, (*xn, c)).start()
                s2(a, 0, me).wait_recv()
                tot[a][P, :] += land[a][0].astype(F32)
                s2(a, 2, me).wait_recv()
                tot[a][Q, :] += land[a][2].astype(F32)

        def finish():
            for a in range(na):
                P, Q = piece(a, 0), piece(a, 1)
                s2(a, 4, me).wait_recv()
                tot[a][P, :] += land[a][4].astype(F32)
                s2(a, 5, me).wait_recv()
                tot[a][Q, :] += land[a][5].astype(F32)
                s3(a, c, sibling).start()
                store(a).start()

        def drain():
            for a in range(na):
                s3(a, 1 - c, me).wait_recv()
                store(a).wait()
            for a in range(na):
                for k in order:
                    s1(a, k, 1 - c).wait_send()
                for i in range(6):
                    s2(a, i, (*via[i], c)).wait_send()
                s3(a, c, sibling).wait_send()

        return start, exchange, relay, finish, drain

    def part(self, grads, steps):
        def body(*refs):
            na = len(self.shapes)
            i = pl.program_id(0)
            for step, phase in zip(steps, self.bind(refs[:na], refs[na:2 * na], refs[2 * na:])):
                pl.when(i == step)(phase)

        hbm = pl.BlockSpec(memory_space=pl.ANY)
        return _Part(body, list(grads), [hbm] * len(grads), [hbm] * len(grads),
                     [_sds((n, w), F32) for n, w in self.shapes], self.scratch_shapes())


def _dh_dx(dqa, dka, dva, dqn, dkn, dvn, dq4, dk4, dv4, dq16, dk16, dv16, dqc, dz, du, xb, cos, sa, sb, winT):
    tm = 512
    spt = SEQ // tm

    def body(dqa_ref, dka_ref, dva_ref, dqn_ref, dkn_ref, dvn_ref, dq4_ref, dk4_ref, dv4_ref,
             dq16_ref, dk16_ref, dv16_ref, dqc_ref, dz_ref, du_ref, xb_ref, cos_ref, sa_ref, sb_ref, w_ref,
             gx_ref, db_ref, gin_ref, dh_ref, scr):
        i = pl.program_id(0)

        @pl.when(i == 0)
        def _():
            db_ref[...] = jnp.zeros_like(db_ref)
            gin_ref[...] = jnp.zeros_like(gin_ref)

        cos_t, sa_t, sb_t = cos_ref[...], sa_ref[...], sb_ref[...]

        def rope_t(t):
            return _rope(t, cos_t, sa_t, sb_t, -1)

        def put(r0, val):
            n = val.shape[1]
            dh_ref[:, r0:r0 + n] = val.astype(BF16)
            db_ref[:, r0:r0 + n] += jnp.sum(val, axis=0, keepdims=True)

        put(O_QA, rope_t(dqa_ref[...].astype(F32)) * QK_SCALE)
        put(O_KA, rope_t(dka_ref[...].astype(F32)))
        put(O_VA, dva_ref[...].astype(F32))
        put(O_QC, dqc_ref[...].astype(F32) * QK_SCALE)
        put(O_Z, dz_ref[...].astype(F32))
        for k, (n_ref, r4, r16) in enumerate(((dqn_ref, dq4_ref, dq16_ref), (dkn_ref, dk4_ref, dk16_ref),
                                               (dvn_ref, dv4_ref, dv16_ref))):
            for j in range(2):
                sl = slice(128 * j, 128 * (j + 1))
                scr[2 * k + j] = n_ref[:, sl].astype(F32)
                for res in range(4):
                    scr[2 * k + j, pl.ds(res, tm // 4, stride=4), :] += r4[0, res, :, sl].astype(F32)
                for res in range(16):
                    scr[2 * k + j, pl.ds(res, tm // 16, stride=16), :] += r16[0, res, :, sl].astype(F32)
        cat = lambda a: jnp.concatenate([scr[a], scr[a + 1]], axis=1)
        put(O_QB, rope_t(cat(0)) * QK_SCALE)
        put(O_KB, rope_t(cat(2)))
        put(O_VB, cat(4))
        gx_ref[...] = _dot(dh_ref[...], w_ref[...], NN) + ALPHA * du_ref[...].astype(F32)
        gin_ref[...] += _dot(dh_ref[...], xb_ref[...], TN)

    tok = lambda w: pl.BlockSpec((tm, w), lambda i: (i, 0))
    tab = pl.BlockSpec((tm, 128), lambda i: (i % spt, 0))
    p4 = pl.BlockSpec((1, 4, tm // 4, W_B), lambda i: (i // spt, 0, i % spt, 0))
    p16 = pl.BlockSpec((1, 16, tm // 16, W_B), lambda i: (i // spt, 0, i % spt, 0))
    once = lambda shape: pl.BlockSpec(shape, lambda i: (0, 0), pipeline_mode=pl.Buffered(1))
    return pl.pallas_call(
        body, name="dh_dx", grid=(T // tm,),
        in_specs=[tok(W_A), tok(W_KV_A), tok(W_KV_A), tok(W_B), tok(W_B), tok(W_B), p4, p4, p4, p16, p16, p16,
                  tok(W_C), tok(D_MIX), tok(D_MODEL), tok(D_MODEL), tab, tab, tab, once((D_IN, D_MODEL))],
        out_specs=(tok(D_MODEL), _full((1, D_IN)), once((D_IN, D_MODEL))),
        out_shape=(_sds((T, D_MODEL), F32), _sds((1, D_IN), F32), _sds((D_IN, D_MODEL), F32)),
        scratch_shapes=[pltpu.VMEM((tm, D_IN), BF16), pltpu.VMEM((6, tm, 128), F32)],
        compiler_params=_cp(("arbitrary",), vmem_mb=56),
    )(*_pin(dqa, dka, dva, dqn, dkn, dvn, dq4, dk4, dv4, dq16, dk16, dv16, dqc, dz, du, xb, cos, sa, sb, winT))


def _reduce_grads(g_in, acc, dbin, dsink):
    rs = _ReduceScatter([(SH_IN, D_MODEL)])

    def body(g_ref, acc_ref, dbin_ref, dsink_ref, r_ref, sv_ref, sv_mine, sv_all, sv_send, sv_recv, *rs_scratch):
        x, y, c = lax.axis_index("x"), lax.axis_index("y"), lax.axis_index("c")
        chips = [(1 - x, y), (x, 1 - y), (1 - x, 1 - y)]
        start, exchange, relay, finish, drain = rs.bind((g_ref,), (r_ref,), rs_scratch)
        start()

        sv_mine[...] = jnp.zeros_like(sv_mine)
        sv_mine[0:4, 0:D_MODEL] = acc_ref[0:4, :]
        sv_mine[4:5, 0:D_IN] = dbin_ref[...]
        sv_mine[5:6, 0:128] = dsink_ref[...]
        my_dev = 4 * x + 2 * y + c
        others = [(x, y, 1 - c)] + [(*chip, cc) for chip in chips for cc in (c, 1 - c)]

        def sv_copy(j, to):
            return pltpu.make_async_remote_copy(
                src_ref=sv_mine, dst_ref=sv_all.at[my_dev], send_sem=sv_send.at[j], recv_sem=sv_recv.at[j],
                device_id=to, device_id_type=MESH)

        sv_sends = [sv_copy(j, to) for j, to in enumerate(others)]
        for cp in sv_sends:
            cp.start()
        exchange()
        relay()
        finish()
        sv_all[my_dev] = sv_mine[...]
        for j in range(7):
            sv_copy(j, (x, y, c)).wait_recv()
        tot = sv_all[0]
        for d in range(1, 8):
            tot = tot + sv_all[d]
        sv_ref[...] = tot
        drain()
        for cp in sv_sends:
            cp.wait_send()

    vm = pl.BlockSpec(memory_space=pltpu.VMEM)
    hbm = pl.BlockSpec(memory_space=pl.ANY)
    return pl.pallas_call(
        body, name="reduce_grads",
        out_shape=(_sds((SH_IN, D_MODEL), F32), _vm_sds((8, SV_W), F32)),
        in_specs=[hbm, vm, vm, vm], out_specs=(hbm, vm),
        scratch_shapes=[pltpu.VMEM((8, SV_W), F32), pltpu.VMEM((8, 8, SV_W), F32),
                        pltpu.SemaphoreType.DMA((7,)), pltpu.SemaphoreType.DMA((7,))] + rs.scratch_shapes(),
        compiler_params=_cp(vmem_mb=40),
    )(pltpu.with_memory_space_constraint(g_in, pltpu.HBM), acc, dbin, dsink)


def _adamw(name, w, g, m, v, rows=None, copy_g=False):
    shape = w.shape
    rows = shape[0] if rows is None else rows
    n_out = 4 if copy_g else 3

    def body(w_ref, g_ref, m_ref, v_ref, d_ref, nm_ref, nv_ref, *go_ref):
        gv = g_ref[...]
        if copy_g:
            go_ref[0][...] = gv
        nm = ADAM_B1 * m_ref[...] + (1.0 - ADAM_B1) * gv
        nv = ADAM_B2 * v_ref[...] + (1.0 - ADAM_B2) * (gv * gv)
        m_hat = nm / (1.0 - ADAM_B1 ** ADAM_STEP)
        v_hat = nv / (1.0 - ADAM_B2 ** ADAM_STEP)
        d_ref[...] = -ADAM_LR * (m_hat / (jnp.sqrt(v_hat) + ADAM_EPS) + ADAM_WD * w_ref[...])
        nm_ref[...] = nm
        nv_ref[...] = nv

    spec = pl.BlockSpec((rows, shape[1]), lambda i: (i, 0))
    return pl.pallas_call(
        body, name=name, grid=(shape[0] // rows,), in_specs=[spec] * 4, out_specs=(spec,) * n_out,
        out_shape=(_sds(shape, F32),) * n_out, compiler_params=_cp(("parallel",)),
    )(*_pin(w, g, m, v))


def _adamw_small(sv, ws, ms, vs):
    where = ((4, D_IN, 1.0), (5, 8, -1.0), (1, D_MIX, 1.0), (2, D_MODEL, 1.0), (3, D_MODEL, 1.0))

    def body(sv_ref, *refs):
        ins, outs = refs[:15], refs[15:]
        for p, (row, width, sign) in enumerate(where):
            gv = sign * sv_ref[row:row + 1, 0:width]
            w_ref, m_ref, v_ref = ins[p], ins[5 + p], ins[10 + p]
            nm = ADAM_B1 * m_ref[...] + (1.0 - ADAM_B1) * gv
            nv = ADAM_B2 * v_ref[...] + (1.0 - ADAM_B2) * (gv * gv)
            m_hat = nm / (1.0 - ADAM_B1 ** ADAM_STEP)
            v_hat = nv / (1.0 - ADAM_B2 ** ADAM_STEP)
            outs[4 * p][...] = gv
            outs[4 * p + 1][...] = -ADAM_LR * (m_hat / (jnp.sqrt(v_hat) + ADAM_EPS) + ADAM_WD * w_ref[...])
            outs[4 * p + 2][...] = nm
            outs[4 * p + 3][...] = nv

    res = pl.pallas_call(
        body, name="adamw_small", out_shape=tuple(_vm_sds(w.shape, F32) for w in ws for _ in range(4)),
    )(sv, *ws, *ms, *vs)
    return [tuple(res[4 * p:4 * p + 4]) for p in range(5)]


def _rope_tables():
    pos = jnp.arange(SEQ, dtype=F32)
    inv = ROPE_THETA ** (-jnp.arange(0, 64, 2, dtype=F32) / 64)
    ang = pos[:, None] * inv[None, :]
    cos, sin = lax.optimization_barrier((jnp.cos(ang), jnp.sin(ang)))
    cos, sin = jnp.tile(cos, (1, 4)), jnp.tile(sin, (1, 4))
    low = (jnp.arange(128) % 64) < 32
    return cos, jnp.where(low, -sin, 0.0), jnp.where(low, 0.0, sin)


def _local_step(x2, mem2, tgt2, winT, wout, wmem, b_in, sinks, g_branch, ln_gain, ln_bias):
    cos, sa, sb = _rope_tables()
    sinkv = jnp.pad(sinks, ((0, 0), (0, 120)))
    head_of_lane = jnp.arange(512)[None, :] // 64
    gather8 = (head_of_lane.T == jnp.arange(128)[None, :]).astype(BF16)
    gather4 = gather8[:W_B]
    spread4 = gather4.T

    xb, qa, ka, va, bn, b4, b16, qc, z, wout, wmem = _in_proj(x2, winT, b_in, cos, sa, sb, wout, wmem)
    memb, mkv = _mem_kv(mem2, wmem)
    b4f, b16f = b4.reshape(T, 768), b16.reshape(T, 768)

    swa = dict(kind="band", nb=SEQ // BLK, max_dist=BLK - 1, gqa=True)
    dil = (dict(kind="band", nb=SEQ // BLK), dict(kind="band", nb=SEQ // 4 // BLK), dict(kind="band", nb=1))
    (oa, lse_a), (o1, l1), (o4, l4), (o16, l16), (oc, lse_c) = _run_parts("attn_fwd", [
        _attn_fwd(qa, 0, W_A, ka, 0, va, 0, W_KV_A, sinks=sinks, **swa),
        _attn_fwd(bn, 0, W_B, bn, 1, bn, 2, W_B, **dil[0]),
        _attn_fwd(b4f, 0, W_B, b4f, 1, b4f, 2, W_B, **dil[1]),
        _attn_fwd(b16f, 0, W_B, b16f, 1, b16f, 2, W_B, **dil[2]),
        _attn_fwd(qc, 0, W_C, mkv, 0, mkv, 1, W_C, kind="mem")], "parallel", 48)

    s4 = lambda w: (B_LOC, 4, SEQ // 4, w)
    s16 = lambda w: (B_LOC, 16, SEQ // 16, w)
    (du, dz, doa, dla, dobn, lsen, dlbn, dob4, lse4, dlb4, dob16, lse16, dlb16, doc, dlc, acc, g_out) = _middle(
        oa, o1, l1, o4.reshape(s4(W_B)), l4.reshape(s4(128)), o16.reshape(s16(W_B)), l16.reshape(s16(128)), oc, z,
        x2, tgt2, g_branch, ln_gain, ln_bias, wout, spread4, gather4, gather8)

    flat = lambda a: a.reshape(T, a.shape[-1])
    (dqa, dka, dva, dsink), (dqc, g_mem) = _run_parts("attn_bwd_a", [
        _attn_bwd(qa, 0, W_A, ka, 0, va, 0, W_KV_A, doa, lse_a, dla, sinkv=sinkv, **swa),
        _attn_bwd(qc, 0, W_C, mkv, 0, mkv, 1, W_C, doc, lse_c, dlc, kind="mem", mem_in=memb)], "arbitrary", 48)
    last = T // QR - 1
    (r_out, r_mem), (dqn, dkn, dvn), (dq4, dk4, dv4), (dq16, dk16, dv16) = _run_parts("attn_bwd_b", [
        _ReduceScatter([(SH_OUT, D_MODEL), (SH_MEM, 2 * W_C)]).part((g_out, g_mem), (0, 1, 2, last, last)),
        _attn_bwd(bn, 0, W_B, bn, 1, bn, 2, W_B, dobn, lsen, dlbn, **dil[0]),
        _attn_bwd(b4f, 0, W_B, b4f, 1, b4f, 2, W_B, flat(dob4), flat(lse4), flat(dlb4), **dil[1]),
        _attn_bwd(b16f, 0, W_B, b16f, 1, b16f, 2, W_B, flat(dob16), flat(lse16), flat(dlb16), **dil[2])],
        "arbitrary", 62)

    r4 = lambda a: a.reshape(s4(W_B))
    r16 = lambda a: a.reshape(s16(W_B))
    gx, dbin, g_in = _dh_dx(dqa, dka, dva, dqn, dkn, dvn, r4(dq4), r4(dk4), r4(dv4), r16(dq16), r16(dk16),
                            r16(dv16), dqc, dz, du, xb, cos, sa, sb, winT)
    return gx, g_in, r_out, r_mem, acc, dbin, dsink


def kernel(x, mem, w_in, b_in, w_mem, attn_sinks, g_branch, w_out, ln_gain, ln_bias, loss_target, m_w_in, m_b_in, m_w_mem, m_attn_sinks, m_g_branch, m_w_out, m_ln_gain, m_ln_bias, v_w_in, v_b_in, v_w_mem, v_attn_sinks, v_g_branch, v_w_out, v_ln_gain, v_ln_bias):
    winT, wout, wmem = _gather_weights(w_in[0].T, w_out[0], w_mem[0])
    gx, g_in, r_out, r_mem, acc, dbin, dsink = _local_step(
        x.reshape(T, D_MODEL), mem.reshape(B_LOC * MEM_LEN, D_MODEL), loss_target.reshape(T, D_MODEL),
        winT, wout, wmem, b_in, attn_sinks, g_branch, ln_gain, ln_bias)
    r_in, sv = _reduce_grads(g_in, acc, dbin, dsink)

    loss = jnp.sum(sv[0, :D_MODEL])
    small = ["b_in", "attn_sinks", "g_branch", "ln_gain", "ln_bias"]
    weights = dict(w_in=w_in, b_in=b_in, w_mem=w_mem, attn_sinks=attn_sinks, g_branch=g_branch, w_out=w_out,
                   ln_gain=ln_gain, ln_bias=ln_bias)
    ms = dict(w_in=m_w_in, b_in=m_b_in, w_mem=m_w_mem, attn_sinks=m_attn_sinks, g_branch=m_g_branch, w_out=m_w_out,
              ln_gain=m_ln_gain, ln_bias=m_ln_bias)
    vs = dict(w_in=v_w_in, b_in=v_b_in, w_mem=v_w_mem, attn_sinks=v_attn_sinks, g_branch=v_g_branch, w_out=v_w_out,
              ln_gain=v_ln_gain, ln_bias=v_ln_bias)
    out = dict(zip(small, _adamw_small(sv, [weights[n] for n in small], [ms[n] for n in small],
                                       [vs[n] for n in small])))
    d, nm, nv, g = (a.T[None] for a in _adamw("adamw_w_in", w_in[0].T, r_in, m_w_in[0].T, v_w_in[0].T, SH_IN // 4,
                                              copy_g=True))
    out["w_in"] = (g, d, nm, nv)
    for n, r in (("w_out", r_out), ("w_mem", r_mem)):
        d, nm, nv, g = (a[None] for a in _adamw("adamw_" + n, weights[n][0], r, ms[n][0], vs[n][0], copy_g=True))
        out[n] = (g, d, nm, nv)
    names = ["w_in", "b_in", "w_mem", "attn_sinks", "g_branch", "w_out", "ln_gain", "ln_bias"]
    return (loss, gx.reshape(B_LOC, SEQ, D_MODEL), *[out[n][k] for k in range(4) for n in names])
```

```python
import functools

import jax
import jax.numpy as jnp
from jax import lax
from jax.experimental import pallas as pl
from jax.experimental.pallas import tpu as pltpu

F32, BF16 = jnp.float32, jnp.bfloat16

D_MODEL = 1024
SEQ = 2048
B_LOC = 2
T = B_LOC * SEQ
BLK = 128
MEM_LEN = 256
W_A, W_KV_A, W_B, W_C, D_MIX = 512, 128, 256, 256, 1024
D_IN = 2816
O_QA, O_KA, O_VA, O_QB, O_KB, O_VB, O_QC, O_Z = 0, 512, 640, 768, 1024, 1280, 1536, 1792
ROPE_THETA = 10000.0
LN_EPS = 1e-5
RMS_EPS = 1e-6
ALPHA = 2.0 ** 0.25
QK_SCALE = 0.125
N_CHIP = 4
SH_IN, SH_OUT, SH_MEM = D_IN // N_CHIP, D_MIX // N_CHIP, D_MODEL // N_CHIP
NEG = -1e30
ADAM_LR, ADAM_B1, ADAM_B2, ADAM_EPS, ADAM_WD, ADAM_STEP = 0.001, 0.9, 0.999, 1e-08, 0.01, 10
SV_W = 3072
MESH = pl.DeviceIdType.MESH

NN = ((1,), (0,))
NT = ((1,), (1,))
TN = ((0,), (0,))


def _dot(a, b, dims):
    return lax.dot_general(a, b, (dims, ((), ())), preferred_element_type=F32)


def _cp(sem=None, vmem_mb=None):
    kw = {}
    if sem is not None:
        kw["dimension_semantics"] = sem
    if vmem_mb is not None:
        kw["vmem_limit_bytes"] = vmem_mb * 1024 * 1024
    return pltpu.CompilerParams(**kw)


def _sds(shape, dtype):
    return pltpu.HBM(shape, dtype)


def _vm_sds(shape, dtype):
    return jax.ShapeDtypeStruct(shape, dtype)


def _pin(*args):
    return [pltpu.with_memory_space_constraint(a, pltpu.HBM) for a in args]


def _full(shape):
    n = len(shape)
    return pl.BlockSpec(shape, lambda *_: (0,) * n)


def _shard_rows(ref, n, chip, half):
    start = pl.multiple_of((2 * chip[0] + chip[1]) * n + half * (n // 2), 16)
    return ref.at[pl.ds(start, n // 2), :]


def _gather_weights(win_sh, wout_sh, wmem_sh):
    half, piece = SH_IN // 2, SH_IN // 4
    shards = ((SH_IN, D_MODEL), (SH_OUT, D_MODEL), (SH_MEM, 2 * W_C))

    def body(a_ref, b_ref, c_ref, oa_ref, ob_ref, oc_ref, raw_a, raw_b, raw_c, own_a, own_b, own_c,
             load_sem, store_sem, ici_send, ici_recv, d2d_send, d2d_recv):
        x, y, c = lax.axis_index("x"), lax.axis_index("y"), lax.axis_index("c")
        me, sibling = (x, y, c), (x, y, 1 - c)
        xn, yn, dg = (1 - x, y), (x, 1 - y), (1 - x, 1 - y)
        srcs, raws = (a_ref, b_ref, c_ref), (raw_a, raw_b, raw_c)
        owns, outs = (own_a, own_b, own_c), (oa_ref, ob_ref, oc_ref)
        loads = [pltpu.make_async_copy(srcs[a], raws[a], load_sem.at[a]) for a in range(3)]
        for cp in loads:
            cp.start()

        def rows(chip, hf, q):
            start = pl.multiple_of((2 * chip[0] + chip[1]) * SH_IN + hf * half + q * piece, 16)
            return oa_ref.at[pl.ds(start, piece), :]

        def copy(sems, k, chip, hf, q, to, src=None):
            blk = rows(chip, hf, q)
            return pltpu.make_async_remote_copy(
                src_ref=blk if src is None else src, dst_ref=blk, send_sem=sems[0].at[k], recv_sem=sems[1].at[k],
                device_id=to, device_id_type=MESH)

        def my_piece(q):
            return own_a.at[pl.ds(pl.multiple_of(c * half + q * piece, 16), piece), :]

        ici, d2d = (ici_send, ici_recv), (d2d_send, d2d_recv)
        stores, direct = [], []
        for a, (n, _) in enumerate(shards):
            loads[a].wait()
            owns[a][...] = raws[a][...].astype(BF16)
            mine = pl.ds(pl.multiple_of((2 * x + y) * n, 16), n)
            stores.append(pltpu.make_async_copy(owns[a], outs[a].at[mine, :], store_sem.at[a]))
            stores[-1].start()
            if a == 0:
                direct = [copy(ici, 0, (x, y), c, 0, (*xn, c), my_piece(0)),
                          copy(ici, 1, (x, y), c, 1, (*xn, c), my_piece(1)),
                          copy(ici, 3, (x, y), c, 0, (*yn, c), my_piece(0)),
                          copy(ici, 4, (x, y), c, 1, (*yn, c), my_piece(1))]
                for cp in direct:
                    cp.start()
        arrivals = [(0, xn, 0), (1, xn, 1), (3, yn, 0), (4, yn, 1), (2, dg, 1), (5, dg, 0)]
        passed = []
        for k, chip, q in arrivals:
            copy(ici, k, chip, c, q, me).wait_recv()
            if k == 0:
                passed.append(copy(ici, 5, xn, c, 0, (*yn, c)))
                passed[-1].start()
            if k == 4:
                passed.append(copy(ici, 2, yn, c, 1, (*xn, c)))
                passed[-1].start()
            passed.append(copy(d2d, k, chip, c, q, sibling))
            passed[-1].start()
        for k, chip, q in arrivals:
            copy(d2d, k, chip, 1 - c, q, me).wait_recv()
        for cp in direct + passed:
            cp.wait_send()
        for cp in stores:
            cp.wait()

    hbm = pl.BlockSpec(memory_space=pl.ANY)
    return pl.pallas_call(
        body, name="gather_weights",
        out_shape=(_sds((D_IN, D_MODEL), BF16), _sds((D_MIX, D_MODEL), BF16), _sds((D_MODEL, 2 * W_C), BF16)),
        in_specs=[hbm, hbm, hbm], out_specs=(hbm, hbm, hbm),
        scratch_shapes=([pltpu.VMEM(sh, F32) for sh in shards] + [pltpu.VMEM(sh, BF16) for sh in shards]
                        + [pltpu.SemaphoreType.DMA((3,))] * 2 + [pltpu.SemaphoreType.DMA((6,))] * 4),
        compiler_params=_cp(vmem_mb=40),
    )(*_pin(win_sh, wout_sh, wmem_sh))


def _rope(t, cos, sa, sb, sign):
    w = t.shape[1]
    reps = w // 128
    c, a, b = (jnp.tile(v, (1, reps)) if reps > 1 else v for v in (cos, sa, sb))
    rot = pltpu.roll(t, w - 32, 1) * a + pltpu.roll(t, 32, 1) * b
    return t * c + rot if sign > 0 else t * c - rot


def _in_proj(x, winT, b_in, cos, sa, sb, wout_own, wmem_own):
    tm = 512
    spt = SEQ // tm
    n_steps = T // tm
    forward_step = n_steps // 2

    def body(x_ref, w_ref, b_ref, cos_ref, sa_ref, sb_ref, wo_in, wm_in,
             xb_ref, qa_ref, ka_ref, va_ref, bn_ref, b4_ref, b16_ref, qc_ref, z_ref, wo_ref, wm_ref,
             scr, ici_send, ici_recv, d2d_send, d2d_recv):
        i = pl.program_id(0)
        mx, my, mc = lax.axis_index("x"), lax.axis_index("y"), lax.axis_index("c")
        chips = [(1 - mx, my), (mx, 1 - my), (1 - mx, 1 - my)]
        full = ((wo_ref, SH_OUT), (wm_ref, SH_MEM))

        def copy(sems, a, j, chip_of_block, half, to):
            blk = _shard_rows(full[a][0], full[a][1], chip_of_block, half)
            return pltpu.make_async_remote_copy(
                src_ref=blk, dst_ref=blk, send_sem=sems[0].at[a, j], recv_sem=sems[1].at[a, j],
                device_id=to, device_id_type=MESH)

        ici, d2d = (ici_send, ici_recv), (d2d_send, d2d_recv)
        pairs = [(a, j, chip) for j, chip in enumerate(chips) for a in range(2)]

        @pl.when(i == 0)
        def _():
            for a, j, chip in pairs:
                copy(ici, a, j, (mx, my), mc, (*chip, mc)).start()

        @pl.when(i == forward_step)
        def _():
            for a, j, chip in pairs:
                copy(ici, a, j, chip, mc, (mx, my, mc)).wait_recv()
                copy(d2d, a, j, chip, mc, (mx, my, 1 - mc)).start()

        @pl.when(i == n_steps - 1)
        def _():
            for a, j, chip in pairs:
                copy(d2d, a, j, chip, 1 - mc, (mx, my, mc)).wait_recv()
            for a, j, chip in pairs:
                copy(ici, a, j, (mx, my), mc, (*chip, mc)).wait_send()
                copy(d2d, a, j, chip, mc, (mx, my, 1 - mc)).wait_send()

        xb = x_ref[...].astype(BF16)
        xb_ref[...] = xb
        cos_t, sa_t, sb_t = cos_ref[...], sa_ref[...], sb_ref[...]

        def proj(r0, n):
            return _dot(xb, w_ref[r0:r0 + n, :], NT) + b_ref[:, r0:r0 + n]

        def rope(t):
            return _rope(t, cos_t, sa_t, sb_t, +1)

        qa_ref[...] = (rope(proj(O_QA, W_A)) * QK_SCALE).astype(BF16)
        ka_ref[...] = rope(proj(O_KA, W_KV_A)).astype(BF16)
        va_ref[...] = proj(O_VA, W_KV_A).astype(BF16)
        qc_ref[...] = (proj(O_QC, W_C) * QK_SCALE).astype(BF16)
        z_ref[...] = proj(O_Z, D_MIX).astype(BF16)
        parts = (rope(proj(O_QB, W_B)) * QK_SCALE, rope(proj(O_KB, W_B)), proj(O_VB, W_B))
        for k, part in enumerate(parts):
            bn_ref[:, 256 * k:256 * (k + 1)] = part.astype(BF16)
            scr[2 * k] = part[:, :128]
            scr[2 * k + 1] = part[:, 128:]
        for j in range(6):
            for res in range(4):
                b4_ref[0, res, :, 128 * j:128 * (j + 1)] = scr[j, pl.ds(res, tm // 4, stride=4), :].astype(BF16)
            for res in range(16):
                b16_ref[0, res, :, 128 * j:128 * (j + 1)] = scr[j, pl.ds(res, tm // 16, stride=16), :].astype(BF16)

    tok = lambda w: pl.BlockSpec((tm, w), lambda i: (i, 0))
    tab = pl.BlockSpec((tm, 128), lambda i: (i % spt, 0))
    hbm = pl.BlockSpec(memory_space=pl.ANY)
    return pl.pallas_call(
        body, name="in_proj", grid=(n_steps,),
        in_specs=[tok(D_MODEL), _full((D_IN, D_MODEL)), _full((1, D_IN)), tab, tab, tab, hbm, hbm],
        out_specs=(tok(D_MODEL), tok(W_A), tok(W_KV_A), tok(W_KV_A), tok(768),
                   pl.BlockSpec((1, 4, tm // 4, 768), lambda i: (i // spt, 0, i % spt, 0)),
                   pl.BlockSpec((1, 16, tm // 16, 768), lambda i: (i // spt, 0, i % spt, 0)),
                   tok(W_C), tok(D_MIX), hbm, hbm),
        out_shape=(_sds((T, D_MODEL), BF16), _sds((T, W_A), BF16), _sds((T, W_KV_A), BF16), _sds((T, W_KV_A), BF16),
                   _sds((T, 768), BF16), _sds((B_LOC, 4, SEQ // 4, 768), BF16), _sds((B_LOC, 16, SEQ // 16, 768), BF16),
                   _sds((T, W_C), BF16), _sds((T, D_MIX), BF16),
                   _sds((D_MIX, D_MODEL), BF16), _sds((D_MODEL, 2 * W_C), BF16)),
        input_output_aliases={6: 9, 7: 10},
        scratch_shapes=[pltpu.VMEM((6, tm, 128), F32)] + [pltpu.SemaphoreType.DMA((2, 3))] * 4,
        compiler_params=_cp(("arbitrary",), vmem_mb=48),
    )(*_pin(x, winT, b_in, cos, sa, sb, wout_own, wmem_own))


def _mem_kv(mem, wmem):
    def body(m_ref, w_ref, mb_ref, kv_ref):
        mb = m_ref[...].astype(BF16)
        mb_ref[...] = mb
        kv_ref[...] = _dot(mb, w_ref[...], NN).astype(BF16)

    n = B_LOC * MEM_LEN
    return pl.pallas_call(
        body, name="mem_kv",
        out_shape=(_sds((n, D_MODEL), BF16), _sds((n, 2 * W_C), BF16)),
    )(*_pin(mem, wmem))


class _Part:
    def __init__(self, body, args, in_specs, out_specs, out_shape, scratch=()):
        self.body, self.args, self.in_specs, self.out_specs, self.out_shape = body, args, in_specs, out_specs, out_shape
        self.scratch = list(scratch)


def _run_parts(name, parts, semantics, vmem_mb):
    n_in = [len(p.args) for p in parts]
    n_out = [len(p.out_shape) for p in parts]
    n_scr = [len(p.scratch) for p in parts]

    def body(*refs):
        ins, outs, scr = refs[:sum(n_in)], refs[sum(n_in):sum(n_in) + sum(n_out)], refs[sum(n_in) + sum(n_out):]
        i0 = o0 = s0 = 0
        for p, ni, no, ns in zip(parts, n_in, n_out, n_scr):
            p.body(*ins[i0:i0 + ni], *outs[o0:o0 + no], *scr[s0:s0 + ns])
            i0, o0, s0 = i0 + ni, o0 + no, s0 + ns

    res = pl.pallas_call(
        body, name=name, grid=(T // QR,),
        in_specs=[sp for p in parts for sp in p.in_specs], out_specs=tuple(sp for p in parts for sp in p.out_specs),
        out_shape=tuple(sh for p in parts for sh in p.out_shape),
        scratch_shapes=[sc for p in parts for sc in p.scratch],
        compiler_params=_cp((semantics,), vmem_mb=vmem_mb),
    )(*_pin(*[a for p in parts for a in p.args]))
    out, o0 = [], 0
    for no in n_out:
        out.append(tuple(res[o0:o0 + no]))
        o0 += no
    return out


QB = 8
QR = QB * BLK


def _lane_lo():
    return lax.broadcasted_iota(jnp.int32, (1, 128), 1) < 64


def _dup_head(k2, hk, lo):
    kf = k2.astype(F32)
    r = pltpu.roll(kf, 64, 1)
    return (jnp.where(lo, kf, r) if hk == 0 else jnp.where(lo, r, kf)).astype(BF16)


def _stack_heads(pairs, lo):
    parts = []
    for x2 in pairs:
        z = jnp.zeros_like(x2)
        parts += [jnp.where(lo, x2, z), jnp.where(lo, z, x2)]
    return jnp.concatenate(parts, axis=0)


def _prev_mode(kind, nb, j):
    if kind == "mem" or nb == 1:
        return "no"
    if nb <= QB:
        return "yes" if j % nb else "no"
    return "yes" if j else "dyn"


class _Attn:
    def __init__(self, kind, nb, max_dist, gqa, qw, kvw, qcb, kcb, vcb):
        self.kind, self.nb, self.gqa, self.qw, self.kvw = kind, nb, gqa, qw, kvw
        npairs = qw // 128
        self.groups = ([(hk, [2 * hk, 2 * hk + 1]) for hk in range(npairs // 2)] if gqa
                       else [(p, [p]) for p in range(npairs)])
        self.nh = 2 * len(self.groups[0][1])
        self.cols = 128 * self.nh
        self.reach = BLK - max_dist
        self.ext_prev = kind == "band" and nb > QB
        self.q_spec = pl.BlockSpec((QR, qw), lambda g: (g, qcb))
        self.row_spec = pl.BlockSpec((QR, qw), lambda g: (g, 0))
        self.stat_spec = pl.BlockSpec((QR, 128), lambda g: (g, 0))
        if kind == "mem":
            per = SEQ // QR
            self.kv_specs = [pl.BlockSpec((MEM_LEN, kvw), lambda g: (g // per, kcb)),
                             pl.BlockSpec((MEM_LEN, kvw), lambda g: (g // per, vcb))]
        else:
            self.kv_specs = [pl.BlockSpec((QR, kvw), lambda g: (g, kcb)), pl.BlockSpec((QR, kvw), lambda g: (g, vcb))]
            if self.ext_prev:
                self.kv_specs += [pl.BlockSpec((BLK, kvw), lambda g: (jnp.maximum(g * QB - 1, 0), kcb)),
                                  pl.BlockSpec((BLK, kvw), lambda g: (jnp.maximum(g * QB - 1, 0), vcb))]

    def masks(self):
        if self.kind == "mem":
            return None
        kj = lax.broadcasted_iota(jnp.int32, (2 * BLK, self.cols), 0)
        qi = lax.broadcasted_iota(jnp.int32, (2 * BLK, self.cols), 1) & (BLK - 1)
        kj1 = lax.broadcasted_iota(jnp.int32, (BLK, self.cols), 0)
        qi1 = lax.broadcasted_iota(jnp.int32, (BLK, self.cols), 1) & (BLK - 1)
        return kj, qi, kj1 <= qi1

    def keys(self, j, gi, kc_ref, vc_ref, kp_ref, vp_ref, lo, kq, g):
        def kv(k_ref, v_ref, r):
            if self.gqa:
                return _dup_head(k_ref[r, :], gi, lo), _dup_head(v_ref[r, :], gi, lo)
            sl = slice(128 * gi, 128 * (gi + 1))
            return k_ref[r, sl], v_ref[r, sl]

        if self.kind == "mem":
            key0 = pl.multiple_of((g // (SEQ // QR)) * MEM_LEN, MEM_LEN)
            return (*kv(kc_ref, vc_ref, slice(None)), None, [(0, MEM_LEN, key0)])
        kj, qi, cur = kq
        row0 = g * QR + BLK * j
        mode = _prev_mode(self.kind, self.nb, j)
        if mode == "no":
            return (*kv(kc_ref, vc_ref, slice(BLK * j, BLK * (j + 1))), cur, [(0, BLK, pl.multiple_of(row0, BLK))])
        if mode == "yes":
            mask = jnp.logical_and(kj >= qi + self.reach, kj <= qi + BLK)
            return (*kv(kc_ref, vc_ref, slice(BLK * (j - 1), BLK * (j + 1))), mask,
                    [(0, 2 * BLK, pl.multiple_of(row0 - BLK, BLK))])
        has_prev = ((g * QB) % self.nb) > 0
        hp = has_prev.astype(jnp.int32)
        mask = jnp.logical_and(kj >= qi * hp + (self.reach * hp + BLK * (1 - hp)), kj <= qi + BLK)
        kp, vp = kv(kp_ref, vp_ref, slice(None))
        kc, vc = kv(kc_ref, vc_ref, slice(0, BLK))
        return (jnp.concatenate([kp, kc], axis=0), jnp.concatenate([vp, vc], axis=0), mask,
                [(0, BLK, pl.multiple_of(jnp.maximum(row0 - BLK, 0), BLK)), (BLK, BLK, pl.multiple_of(row0, BLK))])


def _attn_fwd(q, qcb, qw, k, kcb, v, vcb, kvw, *, kind, nb=1, max_dist=BLK, gqa=False, sinks=None):
    a = _Attn(kind, nb, max_dist, gqa, qw, kvw, qcb, kcb, vcb)

    def body(*refs):
        it = iter(refs)
        q_ref, kc_ref, vc_ref = next(it), next(it), next(it)
        kp_ref, vp_ref = (next(it), next(it)) if a.ext_prev else (None, None)
        sink_ref = next(it) if sinks is not None else None
        o_ref, lse_ref = next(it), next(it)
        g = pl.program_id(0)
        lo = _lane_lo()
        top = lax.broadcasted_iota(jnp.int32, (128, 1), 0) < 64
        rid = lax.broadcasted_iota(jnp.int32, (8, 128), 0)
        kq = a.masks()
        stats = {}

        def scores(j, gi, pairs):
            rows = slice(BLK * j, BLK * (j + 1))
            qs = _stack_heads([q_ref[rows, 128 * p:128 * (p + 1)] for p in pairs], lo)
            kk, vv, mask, _ = a.keys(j, gi, kc_ref, vc_ref, kp_ref, vp_ref, lo, kq, g)
            pieces = [slice(r0, r0 + BLK) for r0 in range(0, kk.shape[0], BLK)]
            return dict(j=j, gi=gi, pairs=pairs, rows=rows, vv=vv, mask=mask, pieces=pieces,
                        ss=[_dot(kk[r], qs, NT) for r in pieces])

        def softmax(c):
            gi, mask = c["gi"], c["mask"]
            ss = [s if mask is None else jnp.where(mask[r], s, NEG) for r, s in zip(c["pieces"], c.pop("ss"))]
            m = jnp.max(ss[0], axis=0, keepdims=True)
            for s in ss[1:]:
                m = jnp.maximum(m, jnp.max(s, axis=0, keepdims=True))
            if sink_ref is not None:
                sk = jnp.concatenate([jnp.full((1, 128), sink_ref[0, a.nh * gi + i], F32) for i in range(a.nh)], axis=1)
                m = jnp.maximum(m, sk)
            ps = [jnp.exp(s - m) for s in ss]
            l = sum(jnp.sum(p, axis=0, keepdims=True) for p in ps)
            if sink_ref is not None:
                l = l + jnp.exp(sk - m)
            c["ps"] = [p.astype(BF16) for p in ps]
            c["l"], c["lse"] = l, m + jnp.log(l)

        def outputs(c):
            j, gi, rows = c["j"], c["gi"], c["rows"]
            ot = sum(_dot(c["vv"][r], p, TN) for r, p in zip(c["pieces"], c["ps"]))
            ot = ot * pl.reciprocal(c["l"], approx=True)
            for i, p in enumerate(c["pairs"]):
                o2t = jnp.where(top, ot[:, 256 * i:256 * i + 128], ot[:, 256 * i + 128:256 * i + 256])
                o_ref[rows, 128 * p:128 * (p + 1)] = o2t.T.astype(BF16)
            stat = stats.get(j, jnp.zeros((8, 128), F32))
            for i in range(a.nh):
                stat = jnp.where(rid == a.nh * gi + i, c["lse"][:, 128 * i:128 * (i + 1)], stat)
            stats[j] = stat
            if gi == a.groups[-1][0]:
                lse_ref[rows, :] = jnp.concatenate([stats.pop(j), jnp.zeros((120, 128), F32)], axis=0).T

        chains = [(j, gi, pairs) for j in range(QB) for gi, pairs in a.groups]
        live = {}
        for t in range(len(chains) + 2):
            if t < len(chains):
                live[t] = scores(*chains[t])
            if 0 <= t - 1 < len(chains):
                softmax(live[t - 1])
            if 0 <= t - 2 < len(chains):
                outputs(live.pop(t - 2))


    args = [q, k, v] + ([k, v] if a.ext_prev else [])
    in_specs = [a.q_spec] + a.kv_specs
    if sinks is not None:
        args.append(sinks)
        in_specs.append(pl.BlockSpec(memory_space=pltpu.SMEM))
    return _Part(body, args, in_specs, [a.row_spec, a.stat_spec], [_sds((T, qw), BF16), _sds((T, 128), F32)])


def _attn_bwd(q, qcb, qw, k, kcb, v, vcb, kvw, do, lse, dl, *, kind, nb=1, max_dist=BLK, gqa=False, sinkv=None,
              mem_in=None):
    a = _Attn(kind, nb, max_dist, gqa, qw, kvw, qcb, kcb, vcb)

    def body(*refs):
        it = iter(refs)
        q_ref, kc_ref, vc_ref = next(it), next(it), next(it)
        kp_ref, vp_ref = (next(it), next(it)) if a.ext_prev else (None, None)
        do_ref, lse_ref, dl_ref = next(it), next(it), next(it)
        sinkv_ref = next(it) if sinkv is not None else None
        mem_ref = next(it) if kind == "mem" else None
        dq_ref = next(it)
        if kind == "mem":
            gmem_ref = next(it)
        else:
            dk_out, dv_out = next(it), next(it)
        dsink_ref = next(it) if sinkv is not None else None
        if kind != "mem":
            dk_ref, dv_ref, stage_k, stage_v, flush_sem = next(it), next(it), next(it), next(it), next(it)
        else:
            dkv_ref = next(it)
        g = pl.program_id(0)
        lo = _lane_lo()
        top = lax.broadcasted_iota(jnp.int32, (128, 1), 0) < 64

        @pl.when(g == 0)
        def _():
            if kind == "mem":
                dkv_ref[...] = jnp.zeros_like(dkv_ref)
            else:
                dk_ref[...] = jnp.zeros_like(dk_ref)
                dv_ref[...] = jnp.zeros_like(dv_ref)
            if dsink_ref is not None:
                dsink_ref[...] = jnp.zeros_like(dsink_ref)

        kq = a.masks()
        stats_t = {}

        def first_matmuls(j, gi, pairs):
            rows = slice(BLK * j, BLK * (j + 1))
            if j not in stats_t:
                stats_t[j] = (lse_ref[rows, :].T, dl_ref[rows, :].T)
            lse_t, dl_t = stats_t[j]
            heads = [a.nh * gi + i for i in range(a.nh)]
            c = dict(rows=rows, gi=gi, pairs=pairs)
            c["qs"] = _stack_heads([q_ref[rows, 128 * p:128 * (p + 1)] for p in pairs], lo)
            c["dos"] = _stack_heads([do_ref[rows, 128 * p:128 * (p + 1)] for p in pairs], lo)
            c["lse_row"] = jnp.concatenate([lse_t[h:h + 1, :] for h in heads], axis=1)
            c["dl_row"] = jnp.concatenate([dl_t[h:h + 1, :] for h in heads], axis=1)
            c["kk"], vv, c["mask"], c["dests"] = a.keys(j, gi, kc_ref, vc_ref, kp_ref, vp_ref, lo, kq, g)
            c["s"] = _dot(c["kk"], c["qs"], NT)
            c["dp"] = _dot(vv, c["dos"], NT)
            return c

        def elementwise(c):
            s = c.pop("s")
            if c["mask"] is not None:
                s = jnp.where(c["mask"], s, NEG)
            p = jnp.exp(s - c["lse_row"])
            c["ds"] = (p * (c.pop("dp") - c["dl_row"])).astype(BF16)
            c["p"] = p.astype(BF16)

        def last_matmuls(c):
            gi, rows = c["gi"], c["rows"]
            dqt = _dot(c["kk"], c["ds"], TN)
            ck = _dot(c["ds"], c["qs"], NN)
            cv = _dot(c["p"], c["dos"], NN)
            if gqa:
                sel = lo if gi == 0 else jnp.logical_not(lo)
                ck = jnp.where(sel, ck + pltpu.roll(ck, 64, 1), 0.0)
                cv = jnp.where(sel, cv + pltpu.roll(cv, 64, 1), 0.0)
                kcols = slice(0, 128)
            else:
                kcols = slice(128 * gi, 128 * (gi + 1))
            for r0, nr, key0 in c["dests"]:
                krows = pl.ds(key0, nr)
                if kind == "mem":
                    dkv_ref[krows, kcols] += ck[r0:r0 + nr]
                    dkv_ref[krows, slice(kvw + kcols.start, kvw + kcols.stop)] += cv[r0:r0 + nr]
                else:
                    dk_ref[krows, kcols] += ck[r0:r0 + nr]
                    dv_ref[krows, kcols] += cv[r0:r0 + nr]
            for i, p in enumerate(c["pairs"]):
                dq2t = jnp.where(top, dqt[:, 256 * i:256 * i + 128], dqt[:, 256 * i + 128:256 * i + 256])
                dq_ref[rows, 128 * p:128 * (p + 1)] = dq2t.T.astype(BF16)

        chains = [(j, gi, pairs) for j in range(QB) for gi, pairs in a.groups]
        live = {}
        for t in range(len(chains) + 2):
            if t < len(chains):
                live[t] = first_matmuls(*chains[t])
            if 0 <= t - 1 < len(chains):
                elementwise(live[t - 1])
            if 0 <= t - 2 < len(chains):
                last_matmuls(live.pop(t - 2))
        if dsink_ref is not None:
            ps = jnp.exp(sinkv_ref[...] - lse_ref[...]) * dl_ref[...]
            dsink_ref[...] += jnp.sum(ps, axis=0, keepdims=True)
        if kind == "mem":
            @pl.when(g == T // QR - 1)
            def _():
                gmem_ref[...] = _dot(mem_ref[...], dkv_ref[...].astype(BF16), TN)
        else:
            n_steps = T // QR

            def flush(step):
                rows = pl.ds(pl.multiple_of(step * QR, QR), QR)
                out = []
                for acc, stage, dst, i in ((dk_ref, stage_k, dk_out, 0), (dv_ref, stage_v, dv_out, 1)):
                    stage[...] = acc[rows, :].astype(BF16)
                    out.append(pltpu.make_async_copy(stage, dst.at[rows, :], flush_sem.at[i]))
                return out

            def flushed(step):
                rows = pl.ds(pl.multiple_of(step * QR, QR), QR)
                return [pltpu.make_async_copy(stage, dst.at[rows, :], flush_sem.at[i])
                        for stage, dst, i in ((stage_k, dk_out, 0), (stage_v, dv_out, 1))]

            @pl.when(g >= 2)
            def _():
                for cp in flushed(g - 2):
                    cp.wait()

            @pl.when(g >= 1)
            def _():
                for cp in flush(g - 1):
                    cp.start()

            @pl.when(g == n_steps - 1)
            def _():
                for cp in flushed(g - 1):
                    cp.wait()
                for cp in flush(g):
                    cp.start()
                for cp in flushed(g):
                    cp.wait()

    args = [q, k, v] + ([k, v] if a.ext_prev else []) + [do, lse, dl]
    in_specs = [a.q_spec] + a.kv_specs + [a.row_spec, a.stat_spec, a.stat_spec]
    if sinkv is not None:
        args.append(sinkv)
        in_specs.append(_full((1, 128)))
    if kind == "mem":
        args.append(mem_in)
        in_specs.append(pl.BlockSpec(mem_in.shape, lambda g: (0, 0), pipeline_mode=pl.Buffered(1)))
    out_shape = [_sds((T, qw), BF16)]
    out_specs = [a.row_spec]
    scratch = []
    if kind == "mem":
        out_shape.append(_sds((D_MODEL, 2 * kvw), F32))
        out_specs.append(pl.BlockSpec((D_MODEL, 2 * kvw), lambda g: (0, 0), pipeline_mode=pl.Buffered(1)))
        scratch = [pltpu.VMEM((B_LOC * MEM_LEN, 2 * kvw), F32)]
    else:
        out_shape += [_sds((T, kvw), BF16)] * 2
        out_specs += [pl.BlockSpec(memory_space=pl.ANY)] * 2
        scratch = [pltpu.VMEM((T, kvw), F32)] * 2 + [pltpu.VMEM((QR, kvw), BF16)] * 2 + [pltpu.SemaphoreType.DMA((2,))]
    if sinkv is not None:
        out_shape.append(_sds((1, 128), F32))
        out_specs.append(_full((1, 128)))
    return _Part(body, args, in_specs, out_specs, out_shape, scratch)


def _dot2(v, w_ref):
    hi = v.astype(BF16)
    lo = (v - hi.astype(F32)).astype(BF16)
    return _dot(hi, w_ref[...], NN) + _dot(lo, w_ref[...], NN)


def _middle(oa, o1, l1, o4, l4, o16, l16, oc, z, x, tgt, g_br, ln_g, ln_b, wout, spread4, gather4, gather8):
    tm = 512
    spt = SEQ // tm

    def body(oa_ref, o1_ref, l1_ref, o4_ref, l4_ref, o16_ref, l16_ref, oc_ref, z_ref, x_ref, t_ref,
             g_ref, lg_ref, lb_ref, w_ref, sp4_ref, ga4_ref, ga8_ref,
             du_ref, dz_ref, doa_ref, dla_ref,
             dobn_ref, lsen_ref, dlbn_ref, dob4_ref, lse4_ref, dlb4_ref, dob16_ref, lse16_ref, dlb16_ref,
             doc_ref, dlc_ref, acc_ref, gout_ref, scr):
        i = pl.program_id(0)

        @pl.when(i == 0)
        def _():
            acc_ref[...] = jnp.zeros_like(acc_ref)
            gout_ref[...] = jnp.zeros_like(gout_ref)

        for res in range(4):
            rows = pl.ds(res, tm // 4, stride=4)
            for j in range(2):
                scr[j, rows, :] = o4_ref[0, res, :, 128 * j:128 * (j + 1)].astype(F32)
            scr[2, rows, :] = l4_ref[0, res]
        for res in range(16):
            rows = pl.ds(res, tm // 16, stride=16)
            for j in range(2):
                scr[3 + j, rows, :] = o16_ref[0, res, :, 128 * j:128 * (j + 1)].astype(F32)
            scr[5, rows, :] = l16_ref[0, res]
        inv_d = 1.0 / D_MODEL
        gb, lg, lb = g_ref[...], lg_ref[...], lb_ref[...]

        def rms(o):
            r = lax.rsqrt(jnp.sum(o * o, axis=1, keepdims=True) * (1.0 / o.shape[1]) + RMS_EPS)
            return o * r, r

        def rms_bwd(dn_, n_, r):
            return r * (dn_ - n_ * (jnp.sum(dn_ * n_, axis=1, keepdims=True) * (1.0 / n_.shape[1])))

        def forward(rs):
            o4v = jnp.concatenate([scr[0, rs, :], scr[1, rs, :]], axis=1)
            o16v = jnp.concatenate([scr[3, rs, :], scr[4, rs, :]], axis=1)
            l1v, l4v, l16v = l1_ref[rs, :], scr[2, rs, :], scr[5, rs, :]
            mx = jnp.maximum(jnp.maximum(l1v, l4v), l16v)
            e1, e4, e16 = jnp.exp(l1v - mx), jnp.exp(l4v - mx), jnp.exp(l16v - mx)
            ssum = e1 + e4 + e16
            inv = 1.0 / ssum
            c = dict(rs=rs, lse_b=mx + jnp.log(ssum))
            c["ob"] = (_dot2(e1 * inv, sp4_ref) * o1_ref[rs, :].astype(F32) + _dot2(e4 * inv, sp4_ref) * o4v
                       + _dot2(e16 * inv, sp4_ref) * o16v)
            c["oa"], c["oc"] = oa_ref[rs, :].astype(F32), oc_ref[rs, :].astype(F32)
            na, c["ra"] = rms(c["oa"])
            nb_, c["rb"] = rms(c["ob"])
            nc, c["rc"] = rms(c["oc"])
            c["n"] = jnp.concatenate([na, nb_, nc], axis=1)
            c["zf"] = z_ref[rs, :].astype(F32)
            c["sig"] = 1.0 / (1.0 + jnp.exp(-c["zf"]))
            c["sz"] = c["zf"] * c["sig"]
            c["yb"] = (c["n"] * gb * c["sz"]).astype(BF16)
            c["y2"] = _dot(c["yb"], w_ref[...], NN)
            return c

        def norm(c):
            rs = c["rs"]
            u = ALPHA * x_ref[rs, :] + c.pop("y2")
            mu = jnp.sum(u, axis=1, keepdims=True) * inv_d
            uc = u - mu
            rstd = lax.rsqrt(jnp.sum(uc * uc, axis=1, keepdims=True) * inv_d + LN_EPS)
            xh = uc * rstd
            diff = xh * lg + lb - t_ref[rs, :]
            acc_ref[0:1, :] += jnp.sum(diff * diff, axis=0, keepdims=True) * (0.5 * inv_d)
            dout = diff * inv_d
            acc_ref[2:3, :] += jnp.sum(dout * xh, axis=0, keepdims=True)
            acc_ref[3:4, :] += jnp.sum(dout, axis=0, keepdims=True)
            dxh = dout * lg
            du = rstd * (dxh - jnp.sum(dxh, axis=1, keepdims=True) * inv_d
                         - xh * (jnp.sum(dxh * xh, axis=1, keepdims=True) * inv_d))
            dub = du.astype(BF16)
            du_ref[rs, :] = dub
            c["dy"] = _dot(dub, w_ref[...], NT)
            gout_ref[...] += _dot(c.pop("yb"), dub, TN)

        def backward(c):
            rs, n, dy, zf, sig = c["rs"], c["n"], c["dy"], c["zf"], c["sig"]
            t1 = dy * c["sz"]
            acc_ref[1:2, :] += jnp.sum(t1 * n, axis=0, keepdims=True)
            dn = t1 * gb
            dz_ref[rs, :] = (dy * n * gb * (sig * (1.0 + zf * (1.0 - sig)))).astype(BF16)
            doa = rms_bwd(dn[:, :W_A], n[:, :W_A], c["ra"])
            dob = rms_bwd(dn[:, W_A:W_A + W_B], n[:, W_A:W_A + W_B], c["rb"])
            doc = rms_bwd(dn[:, W_A + W_B:], n[:, W_A + W_B:], c["rc"])
            doa_ref[rs, :] = doa.astype(BF16)
            dla_ref[rs, :] = _dot2(doa * c["oa"], ga8_ref)
            doc_ref[rs, :] = doc.astype(BF16)
            dlc_ref[rs, :] = _dot2(doc * c["oc"], ga4_ref)
            dobn_ref[rs, :] = dob.astype(BF16)
            lsen_ref[rs, :] = c["lse_b"]
            dlbn_ref[rs, :] = _dot2(dob * c["ob"], ga4_ref)
            scr[0, rs, :] = dob[:, :128]
            scr[1, rs, :] = dob[:, 128:]

        halves = [slice(h * (tm // 2), (h + 1) * (tm // 2)) for h in range(2)]
        live = {}
        for t in range(len(halves) + 2):
            if t < len(halves):
                live[t] = forward(halves[t])
            if 0 <= t - 1 < len(halves):
                norm(live[t - 1])
            if 0 <= t - 2 < len(halves):
                backward(live.pop(t - 2))
        for j in range(2):
            sl = slice(128 * j, 128 * (j + 1))
            for res in range(4):
                dob4_ref[0, res, :, sl] = scr[j, pl.ds(res, tm // 4, stride=4), :].astype(BF16)
            for res in range(16):
                dob16_ref[0, res, :, sl] = scr[j, pl.ds(res, tm // 16, stride=16), :].astype(BF16)
        for res in range(4):
            rows = pl.ds(res, tm // 4, stride=4)
            lse4_ref[0, res] = lsen_ref[rows, :]
            dlb4_ref[0, res] = dlbn_ref[rows, :]
        for res in range(16):
            rows = pl.ds(res, tm // 16, stride=16)
            lse16_ref[0, res] = lsen_ref[rows, :]
            dlb16_ref[0, res] = dlbn_ref[rows, :]


    tok = lambda w: pl.BlockSpec((tm, w), lambda i: (i, 0))
    p4 = lambda w: pl.BlockSpec((1, 4, tm // 4, w), lambda i: (i // spt, 0, i % spt, 0))
    p16 = lambda w: pl.BlockSpec((1, 16, tm // 16, w), lambda i: (i // spt, 0, i % spt, 0))
    s4 = lambda w, dt: _sds((B_LOC, 4, SEQ // 4, w), dt)
    s16 = lambda w, dt: _sds((B_LOC, 16, SEQ // 16, w), dt)
    row = _full((1, D_MODEL))
    return pl.pallas_call(
        body, name="middle", grid=(T // tm,),
        in_specs=[tok(W_A), tok(W_B), tok(128), p4(W_B), p4(128), p16(W_B), p16(128), tok(W_C), tok(D_MIX),
                  tok(D_MODEL), tok(D_MODEL), row, row, row, _full((D_MIX, D_MODEL)),
                  _full((128, W_B)), _full((W_B, 128)), _full((W_A, 128))],
        out_specs=(tok(D_MODEL), tok(D_MIX), tok(W_A), tok(128),
                   tok(W_B), tok(128), tok(128), p4(W_B), p4(128), p4(128), p16(W_B), p16(128), p16(128),
                   tok(W_C), tok(128), _full((8, D_MODEL)), _full((D_MIX, D_MODEL))),
        out_shape=(_sds((T, D_MODEL), BF16), _sds((T, D_MIX), BF16),
                   _sds((T, W_A), BF16), _sds((T, 128), F32),
                   _sds((T, W_B), BF16), _sds((T, 128), F32), _sds((T, 128), F32),
                   s4(W_B, BF16), s4(128, F32), s4(128, F32), s16(W_B, BF16), s16(128, F32), s16(128, F32),
                   _sds((T, W_C), BF16), _sds((T, 128), F32), _sds((8, D_MODEL), F32),
                   _sds((D_MIX, D_MODEL), F32)),
        scratch_shapes=[pltpu.VMEM((6, tm, 128), F32)],
        compiler_params=_cp(("arbitrary",), vmem_mb=56),
    )(*_pin(oa, o1, l1, o4, l4, o16, l16, oc, z, x, tgt, g_br, ln_g, ln_b, wout, spread4, gather4, gather8))


class _ReduceScatter:
    def __init__(self, shapes):
        self.shapes = shapes

    def scratch_shapes(self):
        out = []
        for n, w in self.shapes:
            h, p = n // 2, n // 4
            out += [pltpu.VMEM((4, h, w), F32), pltpu.VMEM((4, h, w), F32), pltpu.VMEM((6, p, w), BF16),
                    pltpu.VMEM((6, p, w), BF16), pltpu.VMEM((2, p, w), F32), pltpu.VMEM((h, w), F32)]
        na = len(self.shapes)
        dma = pltpu.SemaphoreType.DMA
        return out + [dma((na, 4)), dma((na, 4)), dma((na, 4)), dma((na, 6)), dma((na, 6)), dma((na,)), dma((na,)),
                      dma((na,))]

    def bind(self, g_refs, r_refs, scratch):
        na = len(self.shapes)
        bufs = [scratch[6 * a:6 * a + 6] for a in range(na)]
        mine, sib, stage, land, keep, tot = (tuple(b[i] for b in bufs) for i in range(6))
        loc_sem, s1_send, s1_recv, s2_send, s2_recv, s3_send, s3_recv, st_sem = scratch[6 * na:6 * na + 8]
        x, y, c = lax.axis_index("x"), lax.axis_index("y"), lax.axis_index("c")
        me, sibling = (x, y, c), (x, y, 1 - c)
        xn, yn, dg = (1 - x, y), (x, 1 - y), (1 - x, 1 - y)
        idx = lambda chip: 2 * chip[0] + chip[1]
        my_chip = idx((x, y))
        order = [idx(xn), idx(dg), idx(yn), my_chip]

        def rows(a, k, half):
            n = self.shapes[a][0]
            return pl.ds(pl.multiple_of(k * n + half * (n // 2), 8), n // 2)

        def piece(a, q):
            p = self.shapes[a][0] // 4
            return slice(q * p, (q + 1) * p)

        def load(a, k):
            return pltpu.make_async_copy(g_refs[a].at[rows(a, k, c), :], mine[a].at[k], loc_sem.at[a, k])

        def s1(a, k, half):
            return pltpu.make_async_remote_copy(
                src_ref=g_refs[a].at[rows(a, k, half), :], dst_ref=sib[a].at[k],
                send_sem=s1_send.at[a, k], recv_sem=s1_recv.at[a, k], device_id=sibling, device_id_type=MESH)

        def s2(a, i, to):
            return pltpu.make_async_remote_copy(
                src_ref=stage[a].at[i], dst_ref=land[a].at[i], send_sem=s2_send.at[a, i], recv_sem=s2_recv.at[a, i],
                device_id=to, device_id_type=MESH)

        via = {0: xn, 1: xn, 2: yn, 3: yn, 4: yn, 5: xn}

        def s3(a, half, to):
            return pltpu.make_async_remote_copy(
                src_ref=tot[a], dst_ref=r_refs[a].at[rows(a, 0, half), :], send_sem=s3_send.at[a],
                recv_sem=s3_recv.at[a], device_id=to, device_id_type=MESH)

        def store(a):
            return pltpu.make_async_copy(tot[a], r_refs[a].at[rows(a, 0, c), :], st_sem.at[a])

        def start():
            for k in order:
                for a in range(na):
                    load(a, k).start()
                    s1(a, k, 1 - c).start()

        def chip_sum(a, k):
            load(a, k).wait()
            s1(a, k, c).wait_recv()
            return mine[a][k] + sib[a][k]

        def exchange():
            for a in range(na):
                P, Q = piece(a, 0), piece(a, 1)
                s_xn = chip_sum(a, idx(xn))
                stage[a][0] = s_xn[P].astype(BF16)
                keep[a][1] = s_xn[Q]
                s_dg = chip_sum(a, idx(dg))
                stage[a][1] = s_dg[P].astype(BF16)
                s2(a, 0, (*xn, c)).start()
                s2(a, 1, (*xn, c)).start()
                stage[a][3] = s_dg[Q].astype(BF16)
                s_yn = chip_sum(a, idx(yn))
                stage[a][2] = s_yn[Q].astype(BF16)
                keep[a][0] = s_yn[P]
                s2(a, 2, (*yn, c)).start()
                s2(a, 3, (*yn, c)).start()
                tot[a][...] = chip_sum(a, my_chip)

        def relay():
            for a in range(na):
                P, Q = piece(a, 0), piece(a, 1)
                s2(a, 1, me).wait_recv()
                stage[a][4] = (keep[a][0] + land[a][1].astype(F32)).astype(BF16)
                s2(a, 4, (*yn, c)).start()
                s2(a, 3, me).wait_recv()
                stage[a][5] = (keep[a][1] + land[a][3].astype(F32)).astype(BF16)
                s2(a, 5, (*xn, c)).start()
                s2(a, 0, me).wait_recv()
                tot[a][P, :] += land[a][0].astype(F32)
                s2(a, 2, me).wait_recv()
                tot[a][Q, :] += land[a][2].astype(F32)

        def finish():
            for a in range(na):
                P, Q = piece(a, 0), piece(a, 1)
                s2(a, 4, me).wait_recv()
                tot[a][P, :] += land[a][4].astype(F32)
                s2(a, 5, me).wait_recv()
                tot[a][Q, :] += land[a][5].astype(F32)
                s3(a, c, sibling).start()
                store(a).start()

        def drain():
            for a in range(na):
                s3(a, 1 - c, me).wait_recv()
                store(a).wait()
            for a in range(na):
                for k in order:
                    s1(a, k, 1 - c).wait_send()
                for i in range(6):
                    s2(a, i, (*via[i], c)).wait_send()
                s3(a, c, sibling).wait_send()

        return start, exchange, relay, finish, drain

    def part(self, grads, steps):
        def body(*refs):
            na = len(self.shapes)
            i = pl.program_id(0)
            for step, phase in zip(steps, self.bind(refs[:na], refs[na:2 * na], refs[2 * na:])):
                pl.when(i == step)(phase)

        hbm = pl.BlockSpec(memory_space=pl.ANY)
        return _Part(body, list(grads), [hbm] * len(grads), [hbm] * len(grads),
                     [_sds((n, w), F32) for n, w in self.shapes], self.scratch_shapes())


def _dh_dx(dqa, dka, dva, dqn, dkn, dvn, dq4, dk4, dv4, dq16, dk16, dv16, dqc, dz, du, xb, cos, sa, sb, winT):
    tm = 512
    spt = SEQ // tm

    def body(dqa_ref, dka_ref, dva_ref, dqn_ref, dkn_ref, dvn_ref, dq4_ref, dk4_ref, dv4_ref,
             dq16_ref, dk16_ref, dv16_ref, dqc_ref, dz_ref, du_ref, xb_ref, cos_ref, sa_ref, sb_ref, w_ref,
             gx_ref, db_ref, gin_ref, dh_ref, scr):
        i = pl.program_id(0)

        @pl.when(i == 0)
        def _():
            db_ref[...] = jnp.zeros_like(db_ref)
            gin_ref[...] = jnp.zeros_like(gin_ref)

        cos_t, sa_t, sb_t = cos_ref[...], sa_ref[...], sb_ref[...]

        def rope_t(t):
            return _rope(t, cos_t, sa_t, sb_t, -1)

        def put(r0, val):
            n = val.shape[1]
            dh_ref[:, r0:r0 + n] = val.astype(BF16)
            db_ref[:, r0:r0 + n] += jnp.sum(val, axis=0, keepdims=True)

        put(O_QA, rope_t(dqa_ref[...].astype(F32)) * QK_SCALE)
        put(O_KA, rope_t(dka_ref[...].astype(F32)))
        put(O_VA, dva_ref[...].astype(F32))
        put(O_QC, dqc_ref[...].astype(F32) * QK_SCALE)
        put(O_Z, dz_ref[...].astype(F32))
        for k, (n_ref, r4, r16) in enumerate(((dqn_ref, dq4_ref, dq16_ref), (dkn_ref, dk4_ref, dk16_ref),
                                               (dvn_ref, dv4_ref, dv16_ref))):
            for j in range(2):
                sl = slice(128 * j, 128 * (j + 1))
                scr[2 * k + j] = n_ref[:, sl].astype(F32)
                for res in range(4):
                    scr[2 * k + j, pl.ds(res, tm // 4, stride=4), :] += r4[0, res, :, sl].astype(F32)
                for res in range(16):
                    scr[2 * k + j, pl.ds(res, tm // 16, stride=16), :] += r16[0, res, :, sl].astype(F32)
        cat = lambda a: jnp.concatenate([scr[a], scr[a + 1]], axis=1)
        put(O_QB, rope_t(cat(0)) * QK_SCALE)
        put(O_KB, rope_t(cat(2)))
        put(O_VB, cat(4))
        gx_ref[...] = _dot(dh_ref[...], w_ref[...], NN) + ALPHA * du_ref[...].astype(F32)
        gin_ref[...] += _dot(dh_ref[...], xb_ref[...], TN)

    tok = lambda w: pl.BlockSpec((tm, w), lambda i: (i, 0))
    tab = pl.BlockSpec((tm, 128), lambda i: (i % spt, 0))
    p4 = pl.BlockSpec((1, 4, tm // 4, W_B), lambda i: (i // spt, 0, i % spt, 0))
    p16 = pl.BlockSpec((1, 16, tm // 16, W_B), lambda i: (i // spt, 0, i % spt, 0))
    once = lambda shape: pl.BlockSpec(shape, lambda i: (0, 0), pipeline_mode=pl.Buffered(1))
    return pl.pallas_call(
        body, name="dh_dx", grid=(T // tm,),
        in_specs=[tok(W_A), tok(W_KV_A), tok(W_KV_A), tok(W_B), tok(W_B), tok(W_B), p4, p4, p4, p16, p16, p16,
                  tok(W_C), tok(D_MIX), tok(D_MODEL), tok(D_MODEL), tab, tab, tab, once((D_IN, D_MODEL))],
        out_specs=(tok(D_MODEL), _full((1, D_IN)), once((D_IN, D_MODEL))),
        out_shape=(_sds((T, D_MODEL), F32), _sds((1, D_IN), F32), _sds((D_IN, D_MODEL), F32)),
        scratch_shapes=[pltpu.VMEM((tm, D_IN), BF16), pltpu.VMEM((6, tm, 128), F32)],
        compiler_params=_cp(("arbitrary",), vmem_mb=56),
    )(*_pin(dqa, dka, dva, dqn, dkn, dvn, dq4, dk4, dv4, dq16, dk16, dv16, dqc, dz, du, xb, cos, sa, sb, winT))


def _reduce_grads(g_in, acc, dbin, dsink):
    rs = _ReduceScatter([(SH_IN, D_MODEL)])

    def body(g_ref, acc_ref, dbin_ref, dsink_ref, r_ref, sv_ref, sv_mine, sv_all, sv_send, sv_recv, *rs_scratch):
        x, y, c = lax.axis_index("x"), lax.axis_index("y"), lax.axis_index("c")
        chips = [(1 - x, y), (x, 1 - y), (1 - x, 1 - y)]
        start, exchange, relay, finish, drain = rs.bind((g_ref,), (r_ref,), rs_scratch)
        start()

        sv_mine[...] = jnp.zeros_like(sv_mine)
        sv_mine[0:4, 0:D_MODEL] = acc_ref[0:4, :]
        sv_mine[4:5, 0:D_IN] = dbin_ref[...]
        sv_mine[5:6, 0:128] = dsink_ref[...]
        my_dev = 4 * x + 2 * y + c
        others = [(x, y, 1 - c)] + [(*chip, cc) for chip in chips for cc in (c, 1 - c)]

        def sv_copy(j, to):
            return pltpu.make_async_remote_copy(
                src_ref=sv_mine, dst_ref=sv_all.at[my_dev], send_sem=sv_send.at[j], recv_sem=sv_recv.at[j],
                device_id=to, device_id_type=MESH)

        sv_sends = [sv_copy(j, to) for j, to in enumerate(others)]
        for cp in sv_sends:
            cp.start()
        exchange()
        relay()
        finish()
        sv_all[my_dev] = sv_mine[...]
        for j in range(7):
            sv_copy(j, (x, y, c)).wait_recv()
        tot = sv_all[0]
        for d in range(1, 8):
            tot = tot + sv_all[d]
        sv_ref[...] = tot
        drain()
        for cp in sv_sends:
            cp.wait_send()

    vm = pl.BlockSpec(memory_space=pltpu.VMEM)
    hbm = pl.BlockSpec(memory_space=pl.ANY)
    return pl.pallas_call(
        body, name="reduce_grads",
        out_shape=(_sds((SH_IN, D_MODEL), F32), _vm_sds((8, SV_W), F32)),
        in_specs=[hbm, vm, vm, vm], out_specs=(hbm, vm),
        scratch_shapes=[pltpu.VMEM((8, SV_W), F32), pltpu.VMEM((8, 8, SV_W), F32),
                        pltpu.SemaphoreType.DMA((7,)), pltpu.SemaphoreType.DMA((7,))] + rs.scratch_shapes(),
        compiler_params=_cp(vmem_mb=40),
    )(pltpu.with_memory_space_constraint(g_in, pltpu.HBM), acc, dbin, dsink)


def _adamw_update(w, g, m, v):
    nm = ADAM_B1 * m + (1.0 - ADAM_B1) * g
    nv = ADAM_B2 * v + (1.0 - ADAM_B2) * (g * g)
    m_hat = nm / (1.0 - ADAM_B1 ** ADAM_STEP)
    v_hat = nv / (1.0 - ADAM_B2 ** ADAM_STEP)
    return -ADAM_LR * (m_hat / (jnp.sqrt(v_hat) + ADAM_EPS) + ADAM_WD * w), nm, nv


def _adamw_big(items, n_steps=4):
    def body(*refs):
        ins, outs = refs[:4 * len(items)], refs[4 * len(items):]
        for p in range(len(items)):
            w_ref, g_ref, m_ref, v_ref = ins[4 * p:4 * p + 4]
            gv = g_ref[...]
            outs[4 * p][...] = gv
            outs[4 * p + 1][...], outs[4 * p + 2][...], outs[4 * p + 3][...] = _adamw_update(
                w_ref[...], gv, m_ref[...], v_ref[...])

    specs, shapes, args = [], [], []
    for w, g, m, v in items:
        rows, width = w.shape
        specs += [pl.BlockSpec((rows // n_steps, width), lambda i: (i, 0))] * 4
        shapes += [_sds((rows, width), F32)] * 4
        args += [w, g, m, v]
    res = pl.pallas_call(
        body, name="adamw_big", grid=(n_steps,), in_specs=specs, out_specs=tuple(specs), out_shape=tuple(shapes),
        compiler_params=_cp(("parallel",), vmem_mb=40),
    )(*_pin(*args))
    return [tuple(res[4 * p:4 * p + 4]) for p in range(len(items))]


def _adamw_small(sv, ws, ms, vs):
    where = ((4, D_IN, 1.0), (5, 8, -1.0), (1, D_MIX, 1.0), (2, D_MODEL, 1.0), (3, D_MODEL, 1.0))

    def body(sv_ref, *refs):
        ins, outs = refs[:15], refs[15:]
        for p, (row, width, sign) in enumerate(where):
            gv = sign * sv_ref[row:row + 1, 0:width]
            outs[4 * p][...] = gv
            outs[4 * p + 1][...], outs[4 * p + 2][...], outs[4 * p + 3][...] = _adamw_update(
                ins[p][...], gv, ins[5 + p][...], ins[10 + p][...])

    res = pl.pallas_call(
        body, name="adamw_small", out_shape=tuple(_vm_sds(w.shape, F32) for w in ws for _ in range(4)),
    )(sv, *ws, *ms, *vs)
    return [tuple(res[4 * p:4 * p + 4]) for p in range(5)]


def _rope_tables():
    pos = jnp.arange(SEQ, dtype=F32)
    inv = ROPE_THETA ** (-jnp.arange(0, 64, 2, dtype=F32) / 64)
    ang = pos[:, None] * inv[None, :]
    cos, sin = lax.optimization_barrier((jnp.cos(ang), jnp.sin(ang)))
    cos, sin = jnp.tile(cos, (1, 4)), jnp.tile(sin, (1, 4))
    low = (jnp.arange(128) % 64) < 32
    return cos, jnp.where(low, -sin, 0.0), jnp.where(low, 0.0, sin)


def _local_step(x2, mem2, tgt2, winT, wout, wmem, b_in, sinks, g_branch, ln_gain, ln_bias):
    cos, sa, sb = _rope_tables()
    sinkv = jnp.pad(sinks, ((0, 0), (0, 120)))
    head_of_lane = jnp.arange(512)[None, :] // 64
    gather8 = (head_of_lane.T == jnp.arange(128)[None, :]).astype(BF16)
    gather4 = gather8[:W_B]
    spread4 = gather4.T

    xb, qa, ka, va, bn, b4, b16, qc, z, wout, wmem = _in_proj(x2, winT, b_in, cos, sa, sb, wout, wmem)
    memb, mkv = _mem_kv(mem2, wmem)
    b4f, b16f = b4.reshape(T, 768), b16.reshape(T, 768)

    swa = dict(kind="band", nb=SEQ // BLK, max_dist=BLK - 1, gqa=True)
    dil = (dict(kind="band", nb=SEQ // BLK), dict(kind="band", nb=SEQ // 4 // BLK), dict(kind="band", nb=1))
    (oa, lse_a), (o1, l1), (o4, l4), (o16, l16), (oc, lse_c) = _run_parts("attn_fwd", [
        _attn_fwd(qa, 0, W_A, ka, 0, va, 0, W_KV_A, sinks=sinks, **swa),
        _attn_fwd(bn, 0, W_B, bn, 1, bn, 2, W_B, **dil[0]),
        _attn_fwd(b4f, 0, W_B, b4f, 1, b4f, 2, W_B, **dil[1]),
        _attn_fwd(b16f, 0, W_B, b16f, 1, b16f, 2, W_B, **dil[2]),
        _attn_fwd(qc, 0, W_C, mkv, 0, mkv, 1, W_C, kind="mem")], "parallel", 48)

    s4 = lambda w: (B_LOC, 4, SEQ // 4, w)
    s16 = lambda w: (B_LOC, 16, SEQ // 16, w)
    (du, dz, doa, dla, dobn, lsen, dlbn, dob4, lse4, dlb4, dob16, lse16, dlb16, doc, dlc, acc, g_out) = _middle(
        oa, o1, l1, o4.reshape(s4(W_B)), l4.reshape(s4(128)), o16.reshape(s16(W_B)), l16.reshape(s16(128)), oc, z,
        x2, tgt2, g_branch, ln_gain, ln_bias, wout, spread4, gather4, gather8)

    flat = lambda a: a.reshape(T, a.shape[-1])
    (dqa, dka, dva, dsink), (dqc, g_mem) = _run_parts("attn_bwd_a", [
        _attn_bwd(qa, 0, W_A, ka, 0, va, 0, W_KV_A, doa, lse_a, dla, sinkv=sinkv, **swa),
        _attn_bwd(qc, 0, W_C, mkv, 0, mkv, 1, W_C, doc, lse_c, dlc, kind="mem", mem_in=memb)], "arbitrary", 48)
    last = T // QR - 1
    (r_out, r_mem), (dqn, dkn, dvn), (dq4, dk4, dv4), (dq16, dk16, dv16) = _run_parts("attn_bwd_b", [
        _ReduceScatter([(SH_OUT, D_MODEL), (SH_MEM, 2 * W_C)]).part((g_out, g_mem), (0, 1, 2, last, last)),
        _attn_bwd(bn, 0, W_B, bn, 1, bn, 2, W_B, dobn, lsen, dlbn, **dil[0]),
        _attn_bwd(b4f, 0, W_B, b4f, 1, b4f, 2, W_B, flat(dob4), flat(lse4), flat(dlb4), **dil[1]),
        _attn_bwd(b16f, 0, W_B, b16f, 1, b16f, 2, W_B, flat(dob16), flat(lse16), flat(dlb16), **dil[2])],
        "arbitrary", 62)

    r4 = lambda a: a.reshape(s4(W_B))
    r16 = lambda a: a.reshape(s16(W_B))
    gx, dbin, g_in = _dh_dx(dqa, dka, dva, dqn, dkn, dvn, r4(dq4), r4(dk4), r4(dv4), r16(dq16), r16(dk16),
                            r16(dv16), dqc, dz, du, xb, cos, sa, sb, winT)
    return gx, g_in, r_out, r_mem, acc, dbin, dsink


def kernel(x, mem, w_in, b_in, w_mem, attn_sinks, g_branch, w_out, ln_gain, ln_bias, loss_target, m_w_in, m_b_in, m_w_mem, m_attn_sinks, m_g_branch, m_w_out, m_ln_gain, m_ln_bias, v_w_in, v_b_in, v_w_mem, v_attn_sinks, v_g_branch, v_w_out, v_ln_gain, v_ln_bias):
    winT, wout, wmem = _gather_weights(w_in[0].T, w_out[0], w_mem[0])
    gx, g_in, r_out, r_mem, acc, dbin, dsink = _local_step(
        x.reshape(T, D_MODEL), mem.reshape(B_LOC * MEM_LEN, D_MODEL), loss_target.reshape(T, D_MODEL),
        winT, wout, wmem, b_in, attn_sinks, g_branch, ln_gain, ln_bias)
    r_in, sv = _reduce_grads(g_in, acc, dbin, dsink)

    loss = jnp.sum(sv[0, :D_MODEL])
    small = ["b_in", "attn_sinks", "g_branch", "ln_gain", "ln_bias"]
    weights = dict(w_in=w_in, b_in=b_in, w_mem=w_mem, attn_sinks=attn_sinks, g_branch=g_branch, w_out=w_out,
                   ln_gain=ln_gain, ln_bias=ln_bias)
    ms = dict(w_in=m_w_in, b_in=m_b_in, w_mem=m_w_mem, attn_sinks=m_attn_sinks, g_branch=m_g_branch, w_out=m_w_out,
              ln_gain=m_ln_gain, ln_bias=m_ln_bias)
    vs = dict(w_in=v_w_in, b_in=v_b_in, w_mem=v_w_mem, attn_sinks=v_attn_sinks, g_branch=v_g_branch, w_out=v_w_out,
              ln_gain=v_ln_gain, ln_bias=v_ln_bias)
    out = dict(zip(small, _adamw_small(sv, [weights[n] for n in small], [ms[n] for n in small],
                                       [vs[n] for n in small])))
    big = _adamw_big([(w_in[0].T, r_in, m_w_in[0].T, v_w_in[0].T), (w_out[0], r_out, m_w_out[0], v_w_out[0]),
                      (w_mem[0], r_mem, m_w_mem[0], v_w_mem[0])])
    out["w_in"] = tuple(a.T[None] for a in big[0])
    out["w_out"], out["w_mem"] = (tuple(a[None] for a in st) for st in big[1:])
    names = ["w_in", "b_in", "w_mem", "attn_sinks", "g_branch", "w_out", "ln_gain", "ln_bias"]
    return (loss, gx.reshape(B_LOC, SEQ, D_MODEL), *[out[n][k] for k in range(4) for n in names])
```

```python
import jax
import jax.numpy as jnp
import numpy as np
from jax import lax
from jax.experimental import pallas as pl
from jax.experimental.pallas import tpu as pltpu

F32, BF16 = jnp.float32, jnp.bfloat16

D_MODEL = 1024
SEQ = 2048
B_LOC = 2
T = B_LOC * SEQ
BLK = 128
MEM_LEN = 256
W_A, W_KV_A, W_B, W_C, D_MIX = 512, 128, 256, 256, 1024
D_IN = 2816
O_QA, O_KA, O_VA, O_QB, O_KB, O_VB, O_QC, O_Z = 0, 512, 640, 768, 1024, 1280, 1536, 1792
ROPE_THETA = 10000.0
LN_EPS = 1e-5
RMS_EPS = 1e-6
ALPHA = 2.0 ** 0.25
QK_SCALE = 0.125
N_CHIP = 4
SH_IN, SH_OUT, SH_MEM = D_IN // N_CHIP, D_MIX // N_CHIP, D_MODEL // N_CHIP
NEG = -1e30
ADAM_LR, ADAM_B1, ADAM_B2, ADAM_EPS, ADAM_WD, ADAM_STEP = 0.001, 0.9, 0.999, 1e-08, 0.01, 10
SV_W = 3072
MESH = pl.DeviceIdType.MESH

NN = ((1,), (0,))
NT = ((1,), (1,))
TN = ((0,), (0,))


def _dot(a, b, dims):
    return lax.dot_general(a, b, (dims, ((), ())), preferred_element_type=F32)


def _cp(sem=None, vmem_mb=None):
    kw = {}
    if sem is not None:
        kw["dimension_semantics"] = sem
    if vmem_mb is not None:
        kw["vmem_limit_bytes"] = vmem_mb * 1024 * 1024
    return pltpu.CompilerParams(**kw)


def _sds(shape, dtype):
    return pltpu.HBM(shape, dtype)


def _vm_sds(shape, dtype):
    return jax.ShapeDtypeStruct(shape, dtype)


def _pin(*args):
    return [pltpu.with_memory_space_constraint(a, pltpu.HBM) for a in args]


def _full(shape):
    n = len(shape)
    return pl.BlockSpec(shape, lambda *_: (0,) * n)


def _shard_rows(ref, n, chip, half):
    start = pl.multiple_of((2 * chip[0] + chip[1]) * n + half * (n // 2), 16)
    return ref.at[pl.ds(start, n // 2), :]


def _gather_weights(win_sh, wout_sh, wmem_sh):
    half, piece = SH_IN // 2, SH_IN // 4
    shards = ((SH_IN, D_MODEL), (SH_OUT, D_MODEL), (SH_MEM, 2 * W_C))

    def body(a_ref, b_ref, c_ref, oa_ref, ob_ref, oc_ref, raw_a, raw_b, raw_c, own_a, own_b, own_c,
             load_sem, store_sem, ici_send, ici_recv, d2d_send, d2d_recv):
        x, y, c = lax.axis_index("x"), lax.axis_index("y"), lax.axis_index("c")
        me, sibling = (x, y, c), (x, y, 1 - c)
        xn, yn, dg = (1 - x, y), (x, 1 - y), (1 - x, 1 - y)
        srcs, raws = (a_ref, b_ref, c_ref), (raw_a, raw_b, raw_c)
        owns, outs = (own_a, own_b, own_c), (oa_ref, ob_ref, oc_ref)
        loads = [pltpu.make_async_copy(srcs[a], raws[a], load_sem.at[a]) for a in range(3)]
        for cp in loads:
            cp.start()

        def rows(chip, hf, q):
            start = pl.multiple_of((2 * chip[0] + chip[1]) * SH_IN + hf * half + q * piece, 16)
            return oa_ref.at[pl.ds(start, piece), :]

        def copy(sems, k, chip, hf, q, to, src=None):
            blk = rows(chip, hf, q)
            return pltpu.make_async_remote_copy(
                src_ref=blk if src is None else src, dst_ref=blk, send_sem=sems[0].at[k], recv_sem=sems[1].at[k],
                device_id=to, device_id_type=MESH)

        def my_piece(q):
            return own_a.at[pl.ds(pl.multiple_of(c * half + q * piece, 16), piece), :]

        ici, d2d = (ici_send, ici_recv), (d2d_send, d2d_recv)
        stores, direct = [], []
        for a, (n, _) in enumerate(shards):
            loads[a].wait()
            owns[a][...] = raws[a][...].astype(BF16)
            mine = pl.ds(pl.multiple_of((2 * x + y) * n, 16), n)
            stores.append(pltpu.make_async_copy(owns[a], outs[a].at[mine, :], store_sem.at[a]))
            stores[-1].start()
            if a == 0:
                direct = [copy(ici, 0, (x, y), c, 0, (*xn, c), my_piece(0)),
                          copy(ici, 1, (x, y), c, 1, (*xn, c), my_piece(1)),
                          copy(ici, 3, (x, y), c, 0, (*yn, c), my_piece(0)),
                          copy(ici, 4, (x, y), c, 1, (*yn, c), my_piece(1))]
                for cp in direct:
                    cp.start()
        arrivals = [(0, xn, 0), (1, xn, 1), (3, yn, 0), (4, yn, 1), (2, dg, 1), (5, dg, 0)]
        passed = []
        for k, chip, q in arrivals:
            copy(ici, k, chip, c, q, me).wait_recv()
            if k == 0:
                passed.append(copy(ici, 5, xn, c, 0, (*yn, c)))
                passed[-1].start()
            if k == 4:
                passed.append(copy(ici, 2, yn, c, 1, (*xn, c)))
                passed[-1].start()
            passed.append(copy(d2d, k, chip, c, q, sibling))
            passed[-1].start()
        for k, chip, q in arrivals:
            copy(d2d, k, chip, 1 - c, q, me).wait_recv()
        for cp in direct + passed:
            cp.wait_send()
        for cp in stores:
            cp.wait()

    hbm = pl.BlockSpec(memory_space=pl.ANY)
    return pl.pallas_call(
        body, name="gather_weights",
        out_shape=(_sds((D_IN, D_MODEL), BF16), _sds((D_MIX, D_MODEL), BF16), _sds((D_MODEL, 2 * W_C), BF16)),
        in_specs=[hbm, hbm, hbm], out_specs=(hbm, hbm, hbm),
        scratch_shapes=([pltpu.VMEM(sh, F32) for sh in shards] + [pltpu.VMEM(sh, BF16) for sh in shards]
                        + [pltpu.SemaphoreType.DMA((3,))] * 2 + [pltpu.SemaphoreType.DMA((6,))] * 4),
        compiler_params=_cp(vmem_mb=40),
    )(*_pin(win_sh, wout_sh, wmem_sh))


def _rope(t, cos, sa, sb, sign):
    w = t.shape[1]
    reps = w // 128
    c, a, b = (jnp.tile(v, (1, reps)) if reps > 1 else v for v in (cos, sa, sb))
    rot = pltpu.roll(t, w - 32, 1) * a + pltpu.roll(t, 32, 1) * b
    return t * c + rot if sign > 0 else t * c - rot


def _in_proj(x, winT, b_in, cos, sa, sb, wout_own, wmem_own):
    tm = 512
    spt = SEQ // tm
    n_steps = T // tm
    forward_step = n_steps // 2

    def body(x_ref, w_ref, b_ref, cos_ref, sa_ref, sb_ref, wo_in, wm_in,
             xb_ref, qa_ref, ka_ref, va_ref, bn_ref, b4_ref, b16_ref, qc_ref, z_ref, wo_ref, wm_ref,
             scr, ici_send, ici_recv, d2d_send, d2d_recv):
        i = pl.program_id(0)
        mx, my, mc = lax.axis_index("x"), lax.axis_index("y"), lax.axis_index("c")
        chips = [(1 - mx, my), (mx, 1 - my), (1 - mx, 1 - my)]
        full = ((wo_ref, SH_OUT), (wm_ref, SH_MEM))

        def copy(sems, a, j, chip_of_block, half, to):
            blk = _shard_rows(full[a][0], full[a][1], chip_of_block, half)
            return pltpu.make_async_remote_copy(
                src_ref=blk, dst_ref=blk, send_sem=sems[0].at[a, j], recv_sem=sems[1].at[a, j],
                device_id=to, device_id_type=MESH)

        ici, d2d = (ici_send, ici_recv), (d2d_send, d2d_recv)
        pairs = [(a, j, chip) for j, chip in enumerate(chips) for a in range(2)]

        @pl.when(i == 0)
        def _():
            for a, j, chip in pairs:
                copy(ici, a, j, (mx, my), mc, (*chip, mc)).start()

        @pl.when(i == forward_step)
        def _():
            for a, j, chip in pairs:
                copy(ici, a, j, chip, mc, (mx, my, mc)).wait_recv()
                copy(d2d, a, j, chip, mc, (mx, my, 1 - mc)).start()

        @pl.when(i == n_steps - 1)
        def _():
            for a, j, chip in pairs:
                copy(d2d, a, j, chip, 1 - mc, (mx, my, mc)).wait_recv()
            for a, j, chip in pairs:
                copy(ici, a, j, (mx, my), mc, (*chip, mc)).wait_send()
                copy(d2d, a, j, chip, mc, (mx, my, 1 - mc)).wait_send()

        xb = x_ref[...].astype(BF16)
        xb_ref[...] = xb
        cos_t, sa_t, sb_t = cos_ref[...], sa_ref[...], sb_ref[...]

        def proj(r0, n):
            return _dot(xb, w_ref[r0:r0 + n, :], NT) + b_ref[:, r0:r0 + n]

        def rope(t):
            return _rope(t, cos_t, sa_t, sb_t, +1)

        qa_ref[...] = (rope(proj(O_QA, W_A)) * QK_SCALE).astype(BF16)
        ka_ref[...] = rope(proj(O_KA, W_KV_A)).astype(BF16)
        va_ref[...] = proj(O_VA, W_KV_A).astype(BF16)
        qc_ref[...] = (proj(O_QC, W_C) * QK_SCALE).astype(BF16)
        z_ref[...] = proj(O_Z, D_MIX).astype(BF16)
        parts = (rope(proj(O_QB, W_B)) * QK_SCALE, rope(proj(O_KB, W_B)), proj(O_VB, W_B))
        for k, part in enumerate(parts):
            bn_ref[:, 256 * k:256 * (k + 1)] = part.astype(BF16)
            scr[2 * k] = part[:, :128]
            scr[2 * k + 1] = part[:, 128:]
        for j in range(6):
            for res in range(4):
                b4_ref[0, res, :, 128 * j:128 * (j + 1)] = scr[j, pl.ds(res, tm // 4, stride=4), :].astype(BF16)
            for res in range(16):
                b16_ref[0, res, :, 128 * j:128 * (j + 1)] = scr[j, pl.ds(res, tm // 16, stride=16), :].astype(BF16)

    tok = lambda w: pl.BlockSpec((tm, w), lambda i: (i, 0))
    tab = pl.BlockSpec((tm, 128), lambda i: (i % spt, 0))
    hbm = pl.BlockSpec(memory_space=pl.ANY)
    return pl.pallas_call(
        body, name="in_proj", grid=(n_steps,),
        in_specs=[tok(D_MODEL), _full((D_IN, D_MODEL)), _full((1, D_IN)), tab, tab, tab, hbm, hbm],
        out_specs=(tok(D_MODEL), tok(W_A), tok(W_KV_A), tok(W_KV_A), tok(768),
                   pl.BlockSpec((1, 4, tm // 4, 768), lambda i: (i // spt, 0, i % spt, 0)),
                   pl.BlockSpec((1, 16, tm // 16, 768), lambda i: (i // spt, 0, i % spt, 0)),
                   tok(W_C), tok(D_MIX), hbm, hbm),
        out_shape=(_sds((T, D_MODEL), BF16), _sds((T, W_A), BF16), _sds((T, W_KV_A), BF16), _sds((T, W_KV_A), BF16),
                   _sds((T, 768), BF16), _sds((B_LOC, 4, SEQ // 4, 768), BF16), _sds((B_LOC, 16, SEQ // 16, 768), BF16),
                   _sds((T, W_C), BF16), _sds((T, D_MIX), BF16),
                   _sds((D_MIX, D_MODEL), BF16), _sds((D_MODEL, 2 * W_C), BF16)),
        input_output_aliases={6: 9, 7: 10},
        scratch_shapes=[pltpu.VMEM((6, tm, 128), F32)] + [pltpu.SemaphoreType.DMA((2, 3))] * 4,
        compiler_params=_cp(("arbitrary",), vmem_mb=48),
    )(*_pin(x, winT, b_in, cos, sa, sb, wout_own, wmem_own))


def _mem_kv(mem, wmem):
    def body(m_ref, w_ref, mb_ref, kv_ref):
        mb = m_ref[...].astype(BF16)
        mb_ref[...] = mb
        kv_ref[...] = _dot(mb, w_ref[...], NN).astype(BF16)

    n = B_LOC * MEM_LEN
    return pl.pallas_call(
        body, name="mem_kv",
        out_shape=(_sds((n, D_MODEL), BF16), _sds((n, 2 * W_C), BF16)),
    )(*_pin(mem, wmem))


class _Part:
    def __init__(self, body, args, in_specs, out_specs, out_shape, scratch=()):
        self.body, self.args, self.in_specs, self.out_specs, self.out_shape = body, args, in_specs, out_specs, out_shape
        self.scratch = list(scratch)


def _run_parts(name, parts, semantics, vmem_mb):
    n_in = [len(p.args) for p in parts]
    n_out = [len(p.out_shape) for p in parts]
    n_scr = [len(p.scratch) for p in parts]

    def body(*refs):
        ins, outs, scr = refs[:sum(n_in)], refs[sum(n_in):sum(n_in) + sum(n_out)], refs[sum(n_in) + sum(n_out):]
        i0 = o0 = s0 = 0
        for p, ni, no, ns in zip(parts, n_in, n_out, n_scr):
            p.body(*ins[i0:i0 + ni], *outs[o0:o0 + no], *scr[s0:s0 + ns])
            i0, o0, s0 = i0 + ni, o0 + no, s0 + ns

    res = pl.pallas_call(
        body, name=name, grid=(T // QR,),
        in_specs=[sp for p in parts for sp in p.in_specs], out_specs=tuple(sp for p in parts for sp in p.out_specs),
        out_shape=tuple(sh for p in parts for sh in p.out_shape),
        scratch_shapes=[sc for p in parts for sc in p.scratch],
        compiler_params=_cp((semantics,), vmem_mb=vmem_mb),
    )(*_pin(*[a for p in parts for a in p.args]))
    out, o0 = [], 0
    for no in n_out:
        out.append(tuple(res[o0:o0 + no]))
        o0 += no
    return out


QB = 8
QR = QB * BLK


def _lane_lo():
    return lax.broadcasted_iota(jnp.int32, (1, 128), 1) < 64


def _dup_head(k2, hk, lo):
    kf = k2.astype(F32)
    r = pltpu.roll(kf, 64, 1)
    return (jnp.where(lo, kf, r) if hk == 0 else jnp.where(lo, r, kf)).astype(BF16)


def _stack_heads(pairs, lo):
    parts = []
    for x2 in pairs:
        z = jnp.zeros_like(x2)
        parts += [jnp.where(lo, x2, z), jnp.where(lo, z, x2)]
    return jnp.concatenate(parts, axis=0)


def _prev_mode(kind, nb, j):
    if kind == "mem" or nb == 1:
        return "no"
    if nb <= QB:
        return "yes" if j % nb else "no"
    return "yes" if j else "dyn"


class _Attn:
    def __init__(self, kind, nb, max_dist, gqa, qw, kvw, qcb, kcb, vcb):
        self.kind, self.nb, self.gqa, self.qw, self.kvw = kind, nb, gqa, qw, kvw
        npairs = qw // 128
        self.groups = ([(hk, [2 * hk, 2 * hk + 1]) for hk in range(npairs // 2)] if gqa
                       else [(p, [p]) for p in range(npairs)])
        self.nh = 2 * len(self.groups[0][1])
        self.cols = 128 * self.nh
        self.reach = BLK - max_dist
        self.ext_prev = kind == "band" and nb > QB
        self.q_spec = pl.BlockSpec((QR, qw), lambda g: (g, qcb))
        self.row_spec = pl.BlockSpec((QR, qw), lambda g: (g, 0))
        self.stat_spec = pl.BlockSpec((QR, 128), lambda g: (g, 0))
        if kind == "mem":
            per = SEQ // QR
            self.kv_specs = [pl.BlockSpec((MEM_LEN, kvw), lambda g: (g // per, kcb)),
                             pl.BlockSpec((MEM_LEN, kvw), lambda g: (g // per, vcb))]
        else:
            self.kv_specs = [pl.BlockSpec((QR, kvw), lambda g: (g, kcb)), pl.BlockSpec((QR, kvw), lambda g: (g, vcb))]
            if self.ext_prev:
                self.kv_specs += [pl.BlockSpec((BLK, kvw), lambda g: (jnp.maximum(g * QB - 1, 0), kcb)),
                                  pl.BlockSpec((BLK, kvw), lambda g: (jnp.maximum(g * QB - 1, 0), vcb))]

    def masks(self):
        if self.kind == "mem":
            return None
        kj = lax.broadcasted_iota(jnp.int32, (2 * BLK, self.cols), 0)
        qi = lax.broadcasted_iota(jnp.int32, (2 * BLK, self.cols), 1) & (BLK - 1)
        kj1 = lax.broadcasted_iota(jnp.int32, (BLK, self.cols), 0)
        qi1 = lax.broadcasted_iota(jnp.int32, (BLK, self.cols), 1) & (BLK - 1)
        return kj, qi, kj1 <= qi1

    def keys(self, j, gi, kc_ref, vc_ref, kp_ref, vp_ref, lo, kq, g):
        def kv(k_ref, v_ref, r):
            if self.gqa:
                return _dup_head(k_ref[r, :], gi, lo), _dup_head(v_ref[r, :], gi, lo)
            sl = slice(128 * gi, 128 * (gi + 1))
            return k_ref[r, sl], v_ref[r, sl]

        if self.kind == "mem":
            key0 = pl.multiple_of((g // (SEQ // QR)) * MEM_LEN, MEM_LEN)
            return (*kv(kc_ref, vc_ref, slice(None)), None, [(0, MEM_LEN, key0)])
        kj, qi, cur = kq
        row0 = g * QR + BLK * j
        mode = _prev_mode(self.kind, self.nb, j)
        if mode == "no":
            return (*kv(kc_ref, vc_ref, slice(BLK * j, BLK * (j + 1))), cur, [(0, BLK, pl.multiple_of(row0, BLK))])
        if mode == "yes":
            mask = jnp.logical_and(kj >= qi + self.reach, kj <= qi + BLK)
            return (*kv(kc_ref, vc_ref, slice(BLK * (j - 1), BLK * (j + 1))), mask,
                    [(0, 2 * BLK, pl.multiple_of(row0 - BLK, BLK))])
        has_prev = ((g * QB) % self.nb) > 0
        hp = has_prev.astype(jnp.int32)
        mask = jnp.logical_and(kj >= qi * hp + (self.reach * hp + BLK * (1 - hp)), kj <= qi + BLK)
        kp, vp = kv(kp_ref, vp_ref, slice(None))
        kc, vc = kv(kc_ref, vc_ref, slice(0, BLK))
        return (jnp.concatenate([kp, kc], axis=0), jnp.concatenate([vp, vc], axis=0), mask,
                [(0, BLK, pl.multiple_of(jnp.maximum(row0 - BLK, 0), BLK)), (BLK, BLK, pl.multiple_of(row0, BLK))])


def _attn_fwd(q, qcb, qw, k, kcb, v, vcb, kvw, *, kind, nb=1, max_dist=BLK, gqa=False, sinks=None):
    a = _Attn(kind, nb, max_dist, gqa, qw, kvw, qcb, kcb, vcb)

    def body(*refs):
        it = iter(refs)
        q_ref, kc_ref, vc_ref = next(it), next(it), next(it)
        kp_ref, vp_ref = (next(it), next(it)) if a.ext_prev else (None, None)
        sink_ref = next(it) if sinks is not None else None
        o_ref, lse_ref = next(it), next(it)
        g = pl.program_id(0)
        lo = _lane_lo()
        top = lax.broadcasted_iota(jnp.int32, (128, 1), 0) < 64
        rid = lax.broadcasted_iota(jnp.int32, (8, 128), 0)
        kq = a.masks()
        stats = {}

        def scores(j, gi, pairs):
            rows = slice(BLK * j, BLK * (j + 1))
            qs = _stack_heads([q_ref[rows, 128 * p:128 * (p + 1)] for p in pairs], lo)
            kk, vv, mask, _ = a.keys(j, gi, kc_ref, vc_ref, kp_ref, vp_ref, lo, kq, g)
            pieces = [slice(r0, r0 + BLK) for r0 in range(0, kk.shape[0], BLK)]
            return dict(j=j, gi=gi, pairs=pairs, rows=rows, vv=vv, mask=mask, pieces=pieces,
                        ss=[_dot(kk[r], qs, NT) for r in pieces])

        def softmax(c):
            gi, mask = c["gi"], c["mask"]
            ss = [s if mask is None else jnp.where(mask[r], s, NEG) for r, s in zip(c["pieces"], c.pop("ss"))]
            m = jnp.max(ss[0], axis=0, keepdims=True)
            for s in ss[1:]:
                m = jnp.maximum(m, jnp.max(s, axis=0, keepdims=True))
            if sink_ref is not None:
                sk = jnp.concatenate([jnp.full((1, 128), sink_ref[0, a.nh * gi + i], F32) for i in range(a.nh)], axis=1)
                m = jnp.maximum(m, sk)
            ps = [jnp.exp(s - m) for s in ss]
            l = sum(jnp.sum(p, axis=0, keepdims=True) for p in ps)
            if sink_ref is not None:
                l = l + jnp.exp(sk - m)
            c["ps"] = [p.astype(BF16) for p in ps]
            c["l"], c["lse"] = l, m + jnp.log(l)

        def outputs(c):
            j, gi, rows = c["j"], c["gi"], c["rows"]
            ot = sum(_dot(c["vv"][r], p, TN) for r, p in zip(c["pieces"], c["ps"]))
            ot = ot * pl.reciprocal(c["l"], approx=True)
            for i, p in enumerate(c["pairs"]):
                o2t = jnp.where(top, ot[:, 256 * i:256 * i + 128], ot[:, 256 * i + 128:256 * i + 256])
                o_ref[rows, 128 * p:128 * (p + 1)] = o2t.T.astype(BF16)
            stat = stats.get(j, jnp.zeros((8, 128), F32))
            for i in range(a.nh):
                stat = jnp.where(rid == a.nh * gi + i, c["lse"][:, 128 * i:128 * (i + 1)], stat)
            stats[j] = stat
            if gi == a.groups[-1][0]:
                lse_ref[rows, :] = jnp.concatenate([stats.pop(j), jnp.zeros((120, 128), F32)], axis=0).T

        chains = [(j, gi, pairs) for j in range(QB) for gi, pairs in a.groups]
        live = {}
        for t in range(len(chains) + 2):
            if t < len(chains):
                live[t] = scores(*chains[t])
            if 0 <= t - 1 < len(chains):
                softmax(live[t - 1])
            if 0 <= t - 2 < len(chains):
                outputs(live.pop(t - 2))


    args = [q, k, v] + ([k, v] if a.ext_prev else [])
    in_specs = [a.q_spec] + a.kv_specs
    if sinks is not None:
        args.append(sinks)
        in_specs.append(pl.BlockSpec(memory_space=pltpu.SMEM))
    return _Part(body, args, in_specs, [a.row_spec, a.stat_spec], [_sds((T, qw), BF16), _sds((T, 128), F32)])


def _attn_bwd(q, qcb, qw, k, kcb, v, vcb, kvw, do, lse, dl, *, kind, nb=1, max_dist=BLK, gqa=False, sinkv=None,
              mem_in=None):
    a = _Attn(kind, nb, max_dist, gqa, qw, kvw, qcb, kcb, vcb)

    def body(*refs):
        it = iter(refs)
        q_ref, kc_ref, vc_ref = next(it), next(it), next(it)
        kp_ref, vp_ref = (next(it), next(it)) if a.ext_prev else (None, None)
        do_ref, lse_ref, dl_ref = next(it), next(it), next(it)
        sinkv_ref = next(it) if sinkv is not None else None
        mem_ref = next(it) if kind == "mem" else None
        dq_ref = next(it)
        if kind == "mem":
            gmem_ref = next(it)
        else:
            dk_out, dv_out = next(it), next(it)
        dsink_ref = next(it) if sinkv is not None else None
        if kind != "mem":
            dk_ref, dv_ref, stage_k, stage_v, flush_sem = next(it), next(it), next(it), next(it), next(it)
        else:
            dkv_ref = next(it)
        g = pl.program_id(0)
        lo = _lane_lo()
        top = lax.broadcasted_iota(jnp.int32, (128, 1), 0) < 64

        @pl.when(g == 0)
        def _():
            if kind == "mem":
                dkv_ref[...] = jnp.zeros_like(dkv_ref)
            else:
                dk_ref[...] = jnp.zeros_like(dk_ref)
                dv_ref[...] = jnp.zeros_like(dv_ref)
            if dsink_ref is not None:
                dsink_ref[...] = jnp.zeros_like(dsink_ref)

        kq = a.masks()
        stats_t = {}

        def first_matmuls(j, gi, pairs):
            rows = slice(BLK * j, BLK * (j + 1))
            if j not in stats_t:
                stats_t[j] = (lse_ref[rows, :].T, dl_ref[rows, :].T)
            lse_t, dl_t = stats_t[j]
            heads = [a.nh * gi + i for i in range(a.nh)]
            c = dict(rows=rows, gi=gi, pairs=pairs)
            c["qs"] = _stack_heads([q_ref[rows, 128 * p:128 * (p + 1)] for p in pairs], lo)
            c["dos"] = _stack_heads([do_ref[rows, 128 * p:128 * (p + 1)] for p in pairs], lo)
            c["lse_row"] = jnp.concatenate([lse_t[h:h + 1, :] for h in heads], axis=1)
            c["dl_row"] = jnp.concatenate([dl_t[h:h + 1, :] for h in heads], axis=1)
            c["kk"], vv, c["mask"], c["dests"] = a.keys(j, gi, kc_ref, vc_ref, kp_ref, vp_ref, lo, kq, g)
            c["s"] = _dot(c["kk"], c["qs"], NT)
            c["dp"] = _dot(vv, c["dos"], NT)
            return c

        def elementwise(c):
            s = c.pop("s")
            if c["mask"] is not None:
                s = jnp.where(c["mask"], s, NEG)
            p = jnp.exp(s - c["lse_row"])
            c["ds"] = (p * (c.pop("dp") - c["dl_row"])).astype(BF16)
            c["p"] = p.astype(BF16)

        def last_matmuls(c):
            gi, rows = c["gi"], c["rows"]
            dqt = _dot(c["kk"], c["ds"], TN)
            ck = _dot(c["ds"], c["qs"], NN)
            cv = _dot(c["p"], c["dos"], NN)
            if gqa:
                sel = lo if gi == 0 else jnp.logical_not(lo)
                ck = jnp.where(sel, ck + pltpu.roll(ck, 64, 1), 0.0)
                cv = jnp.where(sel, cv + pltpu.roll(cv, 64, 1), 0.0)
                kcols = slice(0, 128)
            else:
                kcols = slice(128 * gi, 128 * (gi + 1))
            for r0, nr, key0 in c["dests"]:
                krows = pl.ds(key0, nr)
                if kind == "mem":
                    dkv_ref[krows, kcols] += ck[r0:r0 + nr]
                    dkv_ref[krows, slice(kvw + kcols.start, kvw + kcols.stop)] += cv[r0:r0 + nr]
                else:
                    dk_ref[krows, kcols] += ck[r0:r0 + nr]
                    dv_ref[krows, kcols] += cv[r0:r0 + nr]
            for i, p in enumerate(c["pairs"]):
                dq2t = jnp.where(top, dqt[:, 256 * i:256 * i + 128], dqt[:, 256 * i + 128:256 * i + 256])
                dq_ref[rows, 128 * p:128 * (p + 1)] = dq2t.T.astype(BF16)

        chains = [(j, gi, pairs) for j in range(QB) for gi, pairs in a.groups]
        live = {}
        for t in range(len(chains) + 2):
            if t < len(chains):
                live[t] = first_matmuls(*chains[t])
            if 0 <= t - 1 < len(chains):
                elementwise(live[t - 1])
            if 0 <= t - 2 < len(chains):
                last_matmuls(live.pop(t - 2))
        if dsink_ref is not None:
            ps = jnp.exp(sinkv_ref[...] - lse_ref[...]) * dl_ref[...]
            dsink_ref[...] += jnp.sum(ps, axis=0, keepdims=True)
        if kind == "mem":
            @pl.when(g == T // QR - 1)
            def _():
                gmem_ref[...] = _dot(mem_ref[...], dkv_ref[...].astype(BF16), TN)
        else:
            n_steps = T // QR

            def flush(step):
                rows = pl.ds(pl.multiple_of(step * QR, QR), QR)
                out = []
                for acc, stage, dst, i in ((dk_ref, stage_k, dk_out, 0), (dv_ref, stage_v, dv_out, 1)):
                    stage[...] = acc[rows, :].astype(BF16)
                    out.append(pltpu.make_async_copy(stage, dst.at[rows, :], flush_sem.at[i]))
                return out

            def flushed(step):
                rows = pl.ds(pl.multiple_of(step * QR, QR), QR)
                return [pltpu.make_async_copy(stage, dst.at[rows, :], flush_sem.at[i])
                        for stage, dst, i in ((stage_k, dk_out, 0), (stage_v, dv_out, 1))]

            @pl.when(g >= 2)
            def _():
                for cp in flushed(g - 2):
                    cp.wait()

            @pl.when(g >= 1)
            def _():
                for cp in flush(g - 1):
                    cp.start()

            @pl.when(g == n_steps - 1)
            def _():
                for cp in flushed(g - 1):
                    cp.wait()
                for cp in flush(g):
                    cp.start()
                for cp in flushed(g):
                    cp.wait()

    args = [q, k, v] + ([k, v] if a.ext_prev else []) + [do, lse, dl]
    in_specs = [a.q_spec] + a.kv_specs + [a.row_spec, a.stat_spec, a.stat_spec]
    if sinkv is not None:
        args.append(sinkv)
        in_specs.append(_full((1, 128)))
    if kind == "mem":
        args.append(mem_in)
        in_specs.append(pl.BlockSpec(mem_in.shape, lambda g: (0, 0), pipeline_mode=pl.Buffered(1)))
    out_shape = [_sds((T, qw), BF16)]
    out_specs = [a.row_spec]
    scratch = []
    if kind == "mem":
        out_shape.append(_sds((D_MODEL, 2 * kvw), F32))
        out_specs.append(pl.BlockSpec((D_MODEL, 2 * kvw), lambda g: (0, 0), pipeline_mode=pl.Buffered(1)))
        scratch = [pltpu.VMEM((B_LOC * MEM_LEN, 2 * kvw), F32)]
    else:
        out_shape += [_sds((T, kvw), BF16)] * 2
        out_specs += [pl.BlockSpec(memory_space=pl.ANY)] * 2
        scratch = [pltpu.VMEM((T, kvw), F32)] * 2 + [pltpu.VMEM((QR, kvw), BF16)] * 2 + [pltpu.SemaphoreType.DMA((2,))]
    if sinkv is not None:
        out_shape.append(_sds((1, 128), F32))
        out_specs.append(_full((1, 128)))
    return _Part(body, args, in_specs, out_specs, out_shape, scratch)


def _dot2(v, w_ref):
    hi = v.astype(BF16)
    lo = (v - hi.astype(F32)).astype(BF16)
    return _dot(hi, w_ref[...], NN) + _dot(lo, w_ref[...], NN)


def _middle(oa, o1, l1, o4, l4, o16, l16, oc, z, x, tgt, g_br, ln_g, ln_b, wout, spread4, gather4, gather8):
    tm = 512
    spt = SEQ // tm

    def body(oa_ref, o1_ref, l1_ref, o4_ref, l4_ref, o16_ref, l16_ref, oc_ref, z_ref, x_ref, t_ref,
             g_ref, lg_ref, lb_ref, w_ref, sp4_ref, ga4_ref, ga8_ref,
             du_ref, dz_ref, doa_ref, dla_ref,
             dobn_ref, lsen_ref, dlbn_ref, dob4_ref, lse4_ref, dlb4_ref, dob16_ref, lse16_ref, dlb16_ref,
             doc_ref, dlc_ref, acc_ref, gout_ref, scr):
        i = pl.program_id(0)

        @pl.when(i == 0)
        def _():
            acc_ref[...] = jnp.zeros_like(acc_ref)
            gout_ref[...] = jnp.zeros_like(gout_ref)

        for res in range(4):
            rows = pl.ds(res, tm // 4, stride=4)
            for j in range(2):
                scr[j, rows, :] = o4_ref[0, res, :, 128 * j:128 * (j + 1)].astype(F32)
            scr[2, rows, :] = l4_ref[0, res]
        for res in range(16):
            rows = pl.ds(res, tm // 16, stride=16)
            for j in range(2):
                scr[3 + j, rows, :] = o16_ref[0, res, :, 128 * j:128 * (j + 1)].astype(F32)
            scr[5, rows, :] = l16_ref[0, res]
        inv_d = 1.0 / D_MODEL
        gb, lg, lb = g_ref[...], lg_ref[...], lb_ref[...]

        def rms(o):
            r = lax.rsqrt(jnp.sum(o * o, axis=1, keepdims=True) * (1.0 / o.shape[1]) + RMS_EPS)
            return o * r, r

        def rms_bwd(dn_, n_, r):
            return r * (dn_ - n_ * (jnp.sum(dn_ * n_, axis=1, keepdims=True) * (1.0 / n_.shape[1])))

        def forward(rs):
            o4v = jnp.concatenate([scr[0, rs, :], scr[1, rs, :]], axis=1)
            o16v = jnp.concatenate([scr[3, rs, :], scr[4, rs, :]], axis=1)
            l1v, l4v, l16v = l1_ref[rs, :], scr[2, rs, :], scr[5, rs, :]
            mx = jnp.maximum(jnp.maximum(l1v, l4v), l16v)
            e1, e4, e16 = jnp.exp(l1v - mx), jnp.exp(l4v - mx), jnp.exp(l16v - mx)
            ssum = e1 + e4 + e16
            inv = 1.0 / ssum
            c = dict(rs=rs, lse_b=mx + jnp.log(ssum))
            c["ob"] = (_dot2(e1 * inv, sp4_ref) * o1_ref[rs, :].astype(F32) + _dot2(e4 * inv, sp4_ref) * o4v
                       + _dot2(e16 * inv, sp4_ref) * o16v)
            c["oa"], c["oc"] = oa_ref[rs, :].astype(F32), oc_ref[rs, :].astype(F32)
            na, c["ra"] = rms(c["oa"])
            nb_, c["rb"] = rms(c["ob"])
            nc, c["rc"] = rms(c["oc"])
            c["n"] = jnp.concatenate([na, nb_, nc], axis=1)
            c["zf"] = z_ref[rs, :].astype(F32)
            c["sig"] = 1.0 / (1.0 + jnp.exp(-c["zf"]))
            c["sz"] = c["zf"] * c["sig"]
            c["yb"] = (c["n"] * gb * c["sz"]).astype(BF16)
            c["y2"] = _dot(c["yb"], w_ref[...], NN)
            return c

        def norm(c):
            rs = c["rs"]
            u = ALPHA * x_ref[rs, :] + c.pop("y2")
            mu = jnp.sum(u, axis=1, keepdims=True) * inv_d
            uc = u - mu
            rstd = lax.rsqrt(jnp.sum(uc * uc, axis=1, keepdims=True) * inv_d + LN_EPS)
            xh = uc * rstd
            diff = xh * lg + lb - t_ref[rs, :]
            acc_ref[0:1, :] += jnp.sum(diff * diff, axis=0, keepdims=True) * (0.5 * inv_d)
            dout = diff * inv_d
            acc_ref[2:3, :] += jnp.sum(dout * xh, axis=0, keepdims=True)
            acc_ref[3:4, :] += jnp.sum(dout, axis=0, keepdims=True)
            dxh = dout * lg
            du = rstd * (dxh - jnp.sum(dxh, axis=1, keepdims=True) * inv_d
                         - xh * (jnp.sum(dxh * xh, axis=1, keepdims=True) * inv_d))
            dub = du.astype(BF16)
            du_ref[rs, :] = dub
            c["dy"] = _dot(dub, w_ref[...], NT)
            gout_ref[...] += _dot(c.pop("yb"), dub, TN)

        def backward(c):
            rs, n, dy, zf, sig = c["rs"], c["n"], c["dy"], c["zf"], c["sig"]
            t1 = dy * c["sz"]
            acc_ref[1:2, :] += jnp.sum(t1 * n, axis=0, keepdims=True)
            dn = t1 * gb
            dz_ref[rs, :] = (dy * n * gb * (sig * (1.0 + zf * (1.0 - sig)))).astype(BF16)
            doa = rms_bwd(dn[:, :W_A], n[:, :W_A], c["ra"])
            dob = rms_bwd(dn[:, W_A:W_A + W_B], n[:, W_A:W_A + W_B], c["rb"])
            doc = rms_bwd(dn[:, W_A + W_B:], n[:, W_A + W_B:], c["rc"])
            doa_ref[rs, :] = doa.astype(BF16)
            dla_ref[rs, :] = _dot2(doa * c["oa"], ga8_ref)
            doc_ref[rs, :] = doc.astype(BF16)
            dlc_ref[rs, :] = _dot2(doc * c["oc"], ga4_ref)
            dobn_ref[rs, :] = dob.astype(BF16)
            lsen_ref[rs, :] = c["lse_b"]
            dlbn_ref[rs, :] = _dot2(dob * c["ob"], ga4_ref)
            scr[0, rs, :] = dob[:, :128]
            scr[1, rs, :] = dob[:, 128:]

        halves = [slice(h * (tm // 2), (h + 1) * (tm // 2)) for h in range(2)]
        live = {}
        for t in range(len(halves) + 2):
            if t < len(halves):
                live[t] = forward(halves[t])
            if 0 <= t - 1 < len(halves):
                norm(live[t - 1])
            if 0 <= t - 2 < len(halves):
                backward(live.pop(t - 2))
        for j in range(2):
            sl = slice(128 * j, 128 * (j + 1))
            for res in range(4):
                dob4_ref[0, res, :, sl] = scr[j, pl.ds(res, tm // 4, stride=4), :].astype(BF16)
            for res in range(16):
                dob16_ref[0, res, :, sl] = scr[j, pl.ds(res, tm // 16, stride=16), :].astype(BF16)
        for res in range(4):
            rows = pl.ds(res, tm // 4, stride=4)
            lse4_ref[0, res] = lsen_ref[rows, :]
            dlb4_ref[0, res] = dlbn_ref[rows, :]
        for res in range(16):
            rows = pl.ds(res, tm // 16, stride=16)
            lse16_ref[0, res] = lsen_ref[rows, :]
            dlb16_ref[0, res] = dlbn_ref[rows, :]


    tok = lambda w: pl.BlockSpec((tm, w), lambda i: (i, 0))
    p4 = lambda w: pl.BlockSpec((1, 4, tm // 4, w), lambda i: (i // spt, 0, i % spt, 0))
    p16 = lambda w: pl.BlockSpec((1, 16, tm // 16, w), lambda i: (i // spt, 0, i % spt, 0))
    s4 = lambda w, dt: _sds((B_LOC, 4, SEQ // 4, w), dt)
    s16 = lambda w, dt: _sds((B_LOC, 16, SEQ // 16, w), dt)
    row = _full((1, D_MODEL))
    return pl.pallas_call(
        body, name="middle", grid=(T // tm,),
        in_specs=[tok(W_A), tok(W_B), tok(128), p4(W_B), p4(128), p16(W_B), p16(128), tok(W_C), tok(D_MIX),
                  tok(D_MODEL), tok(D_MODEL), row, row, row, _full((D_MIX, D_MODEL)),
                  _full((128, W_B)), _full((W_B, 128)), _full((W_A, 128))],
        out_specs=(tok(D_MODEL), tok(D_MIX), tok(W_A), tok(128),
                   tok(W_B), tok(128), tok(128), p4(W_B), p4(128), p4(128), p16(W_B), p16(128), p16(128),
                   tok(W_C), tok(128), _full((8, D_MODEL)), _full((D_MIX, D_MODEL))),
        out_shape=(_sds((T, D_MODEL), BF16), _sds((T, D_MIX), BF16),
                   _sds((T, W_A), BF16), _sds((T, 128), F32),
                   _sds((T, W_B), BF16), _sds((T, 128), F32), _sds((T, 128), F32),
                   s4(W_B, BF16), s4(128, F32), s4(128, F32), s16(W_B, BF16), s16(128, F32), s16(128, F32),
                   _sds((T, W_C), BF16), _sds((T, 128), F32), _sds((8, D_MODEL), F32),
                   _sds((D_MIX, D_MODEL), F32)),
        scratch_shapes=[pltpu.VMEM((6, tm, 128), F32)],
        compiler_params=_cp(("arbitrary",), vmem_mb=56),
    )(*_pin(oa, o1, l1, o4, l4, o16, l16, oc, z, x, tgt, g_br, ln_g, ln_b, wout, spread4, gather4, gather8))


class _ReduceScatter:
    def __init__(self, shapes):
        self.shapes = shapes

    def scratch_shapes(self):
        out = []
        for n, w in self.shapes:
            h, p = n // 2, n // 4
            out += [pltpu.VMEM((4, h, w), F32), pltpu.VMEM((4, h, w), F32), pltpu.VMEM((6, p, w), BF16),
                    pltpu.VMEM((6, p, w), BF16), pltpu.VMEM((2, p, w), F32), pltpu.VMEM((h, w), F32)]
        na = len(self.shapes)
        dma = pltpu.SemaphoreType.DMA
        return out + [dma((na, 4)), dma((na, 4)), dma((na, 4)), dma((na, 6)), dma((na, 6)), dma((na,)), dma((na,)),
                      dma((na,))]

    def bind(self, g_refs, r_refs, scratch):
        na = len(self.shapes)
        bufs = [scratch[6 * a:6 * a + 6] for a in range(na)]
        mine, sib, stage, land, keep, tot = (tuple(b[i] for b in bufs) for i in range(6))
        loc_sem, s1_send, s1_recv, s2_send, s2_recv, s3_send, s3_recv, st_sem = scratch[6 * na:6 * na + 8]
        x, y, c = lax.axis_index("x"), lax.axis_index("y"), lax.axis_index("c")
        me, sibling = (x, y, c), (x, y, 1 - c)
        xn, yn, dg = (1 - x, y), (x, 1 - y), (1 - x, 1 - y)
        idx = lambda chip: 2 * chip[0] + chip[1]
        my_chip = idx((x, y))
        order = [idx(xn), idx(dg), idx(yn), my_chip]

        def rows(a, k, half):
            n = self.shapes[a][0]
            return pl.ds(pl.multiple_of(k * n + half * (n // 2), 8), n // 2)

        def piece(a, q):
            p = self.shapes[a][0] // 4
            return slice(q * p, (q + 1) * p)

        def load(a, k):
            return pltpu.make_async_copy(g_refs[a].at[rows(a, k, c), :], mine[a].at[k], loc_sem.at[a, k])

        def s1(a, k, half):
            return pltpu.make_async_remote_copy(
                src_ref=g_refs[a].at[rows(a, k, half), :], dst_ref=sib[a].at[k],
                send_sem=s1_send.at[a, k], recv_sem=s1_recv.at[a, k], device_id=sibling, device_id_type=MESH)

        def s2(a, i, to):
            return pltpu.make_async_remote_copy(
                src_ref=stage[a].at[i], dst_ref=land[a].at[i], send_sem=s2_send.at[a, i], recv_sem=s2_recv.at[a, i],
                device_id=to, device_id_type=MESH)

        via = {0: xn, 1: xn, 2: yn, 3: yn, 4: yn, 5: xn}

        def s3(a, half, to):
            return pltpu.make_async_remote_copy(
                src_ref=tot[a], dst_ref=r_refs[a].at[rows(a, 0, half), :], send_sem=s3_send.at[a],
                recv_sem=s3_recv.at[a], device_id=to, device_id_type=MESH)

        def store(a):
            return pltpu.make_async_copy(tot[a], r_refs[a].at[rows(a, 0, c), :], st_sem.at[a])

        def start():
            for k in order:
                for a in range(na):
                    load(a, k).start()
                    s1(a, k, 1 - c).start()

        def chip_sum(a, k):
            load(a, k).wait()
            s1(a, k, c).wait_recv()
            return mine[a][k] + sib[a][k]

        def exchange():
            for a in range(na):
                P, Q = piece(a, 0), piece(a, 1)
                s_xn = chip_sum(a, idx(xn))
                stage[a][0] = s_xn[P].astype(BF16)
                keep[a][1] = s_xn[Q]
                s_dg = chip_sum(a, idx(dg))
                stage[a][1] = s_dg[P].astype(BF16)
                s2(a, 0, (*xn, c)).start()
                s2(a, 1, (*xn, c)).start()
                stage[a][3] = s_dg[Q].astype(BF16)
                s_yn = chip_sum(a, idx(yn))
                stage[a][2] = s_yn[Q].astype(BF16)
                keep[a][0] = s_yn[P]
                s2(a, 2, (*yn, c)).start()
                s2(a, 3, (*yn, c)).start()
                tot[a][...] = chip_sum(a, my_chip)

        def relay():
            for a in range(na):
                P, Q = piece(a, 0), piece(a, 1)
                s2(a, 1, me).wait_recv()
                stage[a][4] = (keep[a][0] + land[a][1].astype(F32)).astype(BF16)
                s2(a, 4, (*yn, c)).start()
                s2(a, 3, me).wait_recv()
                stage[a][5] = (keep[a][1] + land[a][3].astype(F32)).astype(BF16)
                s2(a, 5, (*xn, c)).start()
                s2(a, 0, me).wait_recv()
                tot[a][P, :] += land[a][0].astype(F32)
                s2(a, 2, me).wait_recv()
                tot[a][Q, :] += land[a][2].astype(F32)

        def finish():
            for a in range(na):
                P, Q = piece(a, 0), piece(a, 1)
                s2(a, 4, me).wait_recv()
                tot[a][P, :] += land[a][4].astype(F32)
                s2(a, 5, me).wait_recv()
                tot[a][Q, :] += land[a][5].astype(F32)
                s3(a, c, sibling).start()
                store(a).start()

        def drain():
            for a in range(na):
                s3(a, 1 - c, me).wait_recv()
                store(a).wait()
            for a in range(na):
                for k in order:
                    s1(a, k, 1 - c).wait_send()
                for i in range(6):
                    s2(a, i, (*via[i], c)).wait_send()
                s3(a, c, sibling).wait_send()

        return start, exchange, relay, finish, drain

    def part(self, grads, steps):
        def body(*refs):
            na = len(self.shapes)
            i = pl.program_id(0)
            for step, phase in zip(steps, self.bind(refs[:na], refs[na:2 * na], refs[2 * na:])):
                pl.when(i == step)(phase)

        hbm = pl.BlockSpec(memory_space=pl.ANY)
        return _Part(body, list(grads), [hbm] * len(grads), [hbm] * len(grads),
                     [_sds((n, w), F32) for n, w in self.shapes], self.scratch_shapes())


def _dh_dx(dqa, dka, dva, dqn, dkn, dvn, dq4, dk4, dv4, dq16, dk16, dv16, dqc, dz, du, xb, cos, sa, sb, winT):
    tm = 512
    spt = SEQ // tm

    def body(dqa_ref, dka_ref, dva_ref, dqn_ref, dkn_ref, dvn_ref, dq4_ref, dk4_ref, dv4_ref,
             dq16_ref, dk16_ref, dv16_ref, dqc_ref, dz_ref, du_ref, xb_ref, cos_ref, sa_ref, sb_ref, w_ref,
             gx_ref, db_ref, gin_ref, dh_ref, scr):
        i = pl.program_id(0)

        @pl.when(i == 0)
        def _():
            db_ref[...] = jnp.zeros_like(db_ref)
            gin_ref[...] = jnp.zeros_like(gin_ref)

        cos_t, sa_t, sb_t = cos_ref[...], sa_ref[...], sb_ref[...]

        def rope_t(t):
            return _rope(t, cos_t, sa_t, sb_t, -1)

        def put(r0, val):
            n = val.shape[1]
            dh_ref[:, r0:r0 + n] = val.astype(BF16)
            db_ref[:, r0:r0 + n] += jnp.sum(val, axis=0, keepdims=True)

        put(O_QA, rope_t(dqa_ref[...].astype(F32)) * QK_SCALE)
        put(O_KA, rope_t(dka_ref[...].astype(F32)))
        put(O_VA, dva_ref[...].astype(F32))
        put(O_QC, dqc_ref[...].astype(F32) * QK_SCALE)
        put(O_Z, dz_ref[...].astype(F32))
        for k, (n_ref, r4, r16) in enumerate(((dqn_ref, dq4_ref, dq16_ref), (dkn_ref, dk4_ref, dk16_ref),
                                               (dvn_ref, dv4_ref, dv16_ref))):
            for j in range(2):
                sl = slice(128 * j, 128 * (j + 1))
                scr[2 * k + j] = n_ref[:, sl].astype(F32)
                for res in range(4):
                    scr[2 * k + j, pl.ds(res, tm // 4, stride=4), :] += r4[0, res, :, sl].astype(F32)
                for res in range(16):
                    scr[2 * k + j, pl.ds(res, tm // 16, stride=16), :] += r16[0, res, :, sl].astype(F32)
        cat = lambda a: jnp.concatenate([scr[a], scr[a + 1]], axis=1)
        put(O_QB, rope_t(cat(0)) * QK_SCALE)
        put(O_KB, rope_t(cat(2)))
        put(O_VB, cat(4))
        gx_ref[...] = _dot(dh_ref[...], w_ref[...], NN) + ALPHA * du_ref[...].astype(F32)
        gin_ref[...] += _dot(dh_ref[...], xb_ref[...], TN)

    tok = lambda w: pl.BlockSpec((tm, w), lambda i: (i, 0))
    tab = pl.BlockSpec((tm, 128), lambda i: (i % spt, 0))
    p4 = pl.BlockSpec((1, 4, tm // 4, W_B), lambda i: (i // spt, 0, i % spt, 0))
    p16 = pl.BlockSpec((1, 16, tm // 16, W_B), lambda i: (i // spt, 0, i % spt, 0))
    once = lambda shape: pl.BlockSpec(shape, lambda i: (0, 0), pipeline_mode=pl.Buffered(1))
    return pl.pallas_call(
        body, name="dh_dx", grid=(T // tm,),
        in_specs=[tok(W_A), tok(W_KV_A), tok(W_KV_A), tok(W_B), tok(W_B), tok(W_B), p4, p4, p4, p16, p16, p16,
                  tok(W_C), tok(D_MIX), tok(D_MODEL), tok(D_MODEL), tab, tab, tab, once((D_IN, D_MODEL))],
        out_specs=(tok(D_MODEL), _full((1, D_IN)), once((D_IN, D_MODEL))),
        out_shape=(_sds((T, D_MODEL), F32), _sds((1, D_IN), F32), _sds((D_IN, D_MODEL), F32)),
        scratch_shapes=[pltpu.VMEM((tm, D_IN), BF16), pltpu.VMEM((6, tm, 128), F32)],
        compiler_params=_cp(("arbitrary",), vmem_mb=56),
    )(*_pin(dqa, dka, dva, dqn, dkn, dvn, dq4, dk4, dv4, dq16, dk16, dv16, dqc, dz, du, xb, cos, sa, sb, winT))


def _reduce_grads(g_in, acc, dbin, dsink):
    rs = _ReduceScatter([(SH_IN, D_MODEL)])

    def body(g_ref, acc_ref, dbin_ref, dsink_ref, r_ref, sv_ref, sv_mine, sv_all, sv_send, sv_recv, *rs_scratch):
        x, y, c = lax.axis_index("x"), lax.axis_index("y"), lax.axis_index("c")
        chips = [(1 - x, y), (x, 1 - y), (1 - x, 1 - y)]
        start, exchange, relay, finish, drain = rs.bind((g_ref,), (r_ref,), rs_scratch)
        start()

        sv_mine[...] = jnp.zeros_like(sv_mine)
        sv_mine[0:4, 0:D_MODEL] = acc_ref[0:4, :]
        sv_mine[4:5, 0:D_IN] = dbin_ref[...]
        sv_mine[5:6, 0:128] = dsink_ref[...]
        my_dev = 4 * x + 2 * y + c
        others = [(x, y, 1 - c)] + [(*chip, cc) for chip in chips for cc in (c, 1 - c)]

        def sv_copy(j, to):
            return pltpu.make_async_remote_copy(
                src_ref=sv_mine, dst_ref=sv_all.at[my_dev], send_sem=sv_send.at[j], recv_sem=sv_recv.at[j],
                device_id=to, device_id_type=MESH)

        sv_sends = [sv_copy(j, to) for j, to in enumerate(others)]
        for cp in sv_sends:
            cp.start()
        exchange()
        relay()
        finish()
        sv_all[my_dev] = sv_mine[...]
        for j in range(7):
            sv_copy(j, (x, y, c)).wait_recv()
        tot = sv_all[0]
        for d in range(1, 8):
            tot = tot + sv_all[d]
        sv_ref[...] = tot
        drain()
        for cp in sv_sends:
            cp.wait_send()

    vm = pl.BlockSpec(memory_space=pltpu.VMEM)
    hbm = pl.BlockSpec(memory_space=pl.ANY)
    return pl.pallas_call(
        body, name="reduce_grads",
        out_shape=(_sds((SH_IN, D_MODEL), F32), _vm_sds((8, SV_W), F32)),
        in_specs=[hbm, vm, vm, vm], out_specs=(hbm, vm),
        scratch_shapes=[pltpu.VMEM((8, SV_W), F32), pltpu.VMEM((8, 8, SV_W), F32),
                        pltpu.SemaphoreType.DMA((7,)), pltpu.SemaphoreType.DMA((7,))] + rs.scratch_shapes(),
        compiler_params=_cp(vmem_mb=40),
    )(pltpu.with_memory_space_constraint(g_in, pltpu.HBM), acc, dbin, dsink)


def _adamw_update(w, g, m, v):
    nm = ADAM_B1 * m + (1.0 - ADAM_B1) * g
    nv = ADAM_B2 * v + (1.0 - ADAM_B2) * (g * g)
    m_hat = nm / (1.0 - ADAM_B1 ** ADAM_STEP)
    v_hat = nv / (1.0 - ADAM_B2 ** ADAM_STEP)
    return -ADAM_LR * (m_hat / (jnp.sqrt(v_hat) + ADAM_EPS) + ADAM_WD * w), nm, nv


def _adamw_big(items, n_steps=4):
    def body(*refs):
        ins, outs = refs[:4 * len(items)], refs[4 * len(items):]
        for p in range(len(items)):
            w_ref, g_ref, m_ref, v_ref = ins[4 * p:4 * p + 4]
            gv = g_ref[...]
            outs[4 * p][...] = gv
            outs[4 * p + 1][...], outs[4 * p + 2][...], outs[4 * p + 3][...] = _adamw_update(
                w_ref[...], gv, m_ref[...], v_ref[...])

    specs, shapes, args = [], [], []
    for w, g, m, v in items:
        rows, width = w.shape
        specs += [pl.BlockSpec((rows // n_steps, width), lambda i: (i, 0))] * 4
        shapes += [_sds((rows, width), F32)] * 4
        args += [w, g, m, v]
    res = pl.pallas_call(
        body, name="adamw_big", grid=(n_steps,), in_specs=specs, out_specs=tuple(specs), out_shape=tuple(shapes),
        compiler_params=_cp(("parallel",), vmem_mb=40),
    )(*_pin(*args))
    return [tuple(res[4 * p:4 * p + 4]) for p in range(len(items))]


def _adamw_small(sv, ws, ms, vs):
    where = ((4, D_IN, 1.0), (5, 8, -1.0), (1, D_MIX, 1.0), (2, D_MODEL, 1.0), (3, D_MODEL, 1.0))

    def body(sv_ref, *refs):
        ins, loss_ref, outs = refs[:15], refs[15], refs[16:]
        loss_ref[...] = jnp.sum(sv_ref[0:1, 0:D_MODEL], axis=1, keepdims=True)
        for p, (row, width, sign) in enumerate(where):
            gv = sign * sv_ref[row:row + 1, 0:width]
            outs[4 * p][...] = gv
            outs[4 * p + 1][...], outs[4 * p + 2][...], outs[4 * p + 3][...] = _adamw_update(
                ins[p][...], gv, ins[5 + p][...], ins[10 + p][...])

    res = pl.pallas_call(
        body, name="adamw_small",
        out_shape=(_vm_sds((1, 1), F32), *[_vm_sds(w.shape, F32) for w in ws for _ in range(4)]),
    )(sv, *ws, *ms, *vs)
    return res[0], [tuple(res[1 + 4 * p:5 + 4 * p]) for p in range(5)]


def _rope_tables():
    pos = np.arange(SEQ, dtype=np.float32)
    inv = (np.float32(ROPE_THETA) ** (-np.arange(0, 64, 2, dtype=np.float32) / np.float32(64))).astype(np.float32)
    ang = np.tile(pos[:, None] * inv[None, :], (1, 4))
    cos, sin = np.cos(ang).astype(np.float32), np.sin(ang).astype(np.float32)
    low = (np.arange(128) % 64) < 32
    zero = np.float32(0.0)
    return jnp.asarray(cos), jnp.asarray(np.where(low, -sin, zero)), jnp.asarray(np.where(low, zero, sin))


def _local_step(x2, mem2, tgt2, winT, wout, wmem, b_in, sinks, g_branch, ln_gain, ln_bias):
    cos, sa, sb = _rope_tables()
    sinkv = jnp.pad(sinks, ((0, 0), (0, 120)))
    head_of_lane = np.arange(512)[:, None] // 64
    gather8 = jnp.asarray(head_of_lane == np.arange(128)[None, :], BF16)
    gather4 = jnp.asarray(head_of_lane[:W_B] == np.arange(128)[None, :], BF16)
    spread4 = jnp.asarray((head_of_lane[:W_B] == np.arange(128)[None, :]).T, BF16)

    xb, qa, ka, va, bn, b4, b16, qc, z, wout, wmem = _in_proj(x2, winT, b_in, cos, sa, sb, wout, wmem)
    memb, mkv = _mem_kv(mem2, wmem)
    b4f, b16f = b4.reshape(T, 768), b16.reshape(T, 768)

    swa = dict(kind="band", nb=SEQ // BLK, max_dist=BLK - 1, gqa=True)
    dil = (dict(kind="band", nb=SEQ // BLK), dict(kind="band", nb=SEQ // 4 // BLK), dict(kind="band", nb=1))
    (oa, lse_a), (o1, l1), (o4, l4), (o16, l16), (oc, lse_c) = _run_parts("attn_fwd", [
        _attn_fwd(qa, 0, W_A, ka, 0, va, 0, W_KV_A, sinks=sinks, **swa),
        _attn_fwd(bn, 0, W_B, bn, 1, bn, 2, W_B, **dil[0]),
        _attn_fwd(b4f, 0, W_B, b4f, 1, b4f, 2, W_B, **dil[1]),
        _attn_fwd(b16f, 0, W_B, b16f, 1, b16f, 2, W_B, **dil[2]),
        _attn_fwd(qc, 0, W_C, mkv, 0, mkv, 1, W_C, kind="mem")], "parallel", 48)

    s4 = lambda w: (B_LOC, 4, SEQ // 4, w)
    s16 = lambda w: (B_LOC, 16, SEQ // 16, w)
    (du, dz, doa, dla, dobn, lsen, dlbn, dob4, lse4, dlb4, dob16, lse16, dlb16, doc, dlc, acc, g_out) = _middle(
        oa, o1, l1, o4.reshape(s4(W_B)), l4.reshape(s4(128)), o16.reshape(s16(W_B)), l16.reshape(s16(128)), oc, z,
        x2, tgt2, g_branch, ln_gain, ln_bias, wout, spread4, gather4, gather8)

    flat = lambda a: a.reshape(T, a.shape[-1])
    (dqa, dka, dva, dsink), (dqc, g_mem) = _run_parts("attn_bwd_a", [
        _attn_bwd(qa, 0, W_A, ka, 0, va, 0, W_KV_A, doa, lse_a, dla, sinkv=sinkv, **swa),
        _attn_bwd(qc, 0, W_C, mkv, 0, mkv, 1, W_C, doc, lse_c, dlc, kind="mem", mem_in=memb)], "arbitrary", 48)
    last = T // QR - 1
    (r_out, r_mem), (dqn, dkn, dvn), (dq4, dk4, dv4), (dq16, dk16, dv16) = _run_parts("attn_bwd_b", [
        _ReduceScatter([(SH_OUT, D_MODEL), (SH_MEM, 2 * W_C)]).part((g_out, g_mem), (0, 1, 2, last, last)),
        _attn_bwd(bn, 0, W_B, bn, 1, bn, 2, W_B, dobn, lsen, dlbn, **dil[0]),
        _attn_bwd(b4f, 0, W_B, b4f, 1, b4f, 2, W_B, flat(dob4), flat(lse4), flat(dlb4), **dil[1]),
        _attn_bwd(b16f, 0, W_B, b16f, 1, b16f, 2, W_B, flat(dob16), flat(lse16), flat(dlb16), **dil[2])],
        "arbitrary", 62)

    r4 = lambda a: a.reshape(s4(W_B))
    r16 = lambda a: a.reshape(s16(W_B))
    gx, dbin, g_in = _dh_dx(dqa, dka, dva, dqn, dkn, dvn, r4(dq4), r4(dk4), r4(dv4), r16(dq16), r16(dk16),
                            r16(dv16), dqc, dz, du, xb, cos, sa, sb, winT)
    return gx, g_in, r_out, r_mem, acc, dbin, dsink


def kernel(x, mem, w_in, b_in, w_mem, attn_sinks, g_branch, w_out, ln_gain, ln_bias, loss_target, m_w_in, m_b_in, m_w_mem, m_attn_sinks, m_g_branch, m_w_out, m_ln_gain, m_ln_bias, v_w_in, v_b_in, v_w_mem, v_attn_sinks, v_g_branch, v_w_out, v_ln_gain, v_ln_bias):
    winT, wout, wmem = _gather_weights(w_in[0].T, w_out[0], w_mem[0])
    gx, g_in, r_out, r_mem, acc, dbin, dsink = _local_step(
        x.reshape(T, D_MODEL), mem.reshape(B_LOC * MEM_LEN, D_MODEL), loss_target.reshape(T, D_MODEL),
        winT, wout, wmem, b_in, attn_sinks, g_branch, ln_gain, ln_bias)
    r_in, sv = _reduce_grads(g_in, acc, dbin, dsink)

    small = ["b_in", "attn_sinks", "g_branch", "ln_gain", "ln_bias"]
    loss, steps = _adamw_small(sv, [b_in, attn_sinks, g_branch, ln_gain, ln_bias],
                               [m_b_in, m_attn_sinks, m_g_branch, m_ln_gain, m_ln_bias],
                               [v_b_in, v_attn_sinks, v_g_branch, v_ln_gain, v_ln_bias])
    out = dict(zip(small, steps))
    big = _adamw_big([(w_in[0].T, r_in, m_w_in[0].T, v_w_in[0].T), (w_out[0], r_out, m_w_out[0], v_w_out[0]),
                      (w_mem[0], r_mem, m_w_mem[0], v_w_mem[0])])
    out["w_in"] = tuple(a.T[None] for a in big[0])
    out["w_out"], out["w_mem"] = (tuple(a[None] for a in st) for st in big[1:])
    names = ["w_in", "b_in", "w_mem", "attn_sinks", "g_branch", "w_out", "ln_gain", "ln_bias"]
    return (loss.reshape(()), gx.reshape(B_LOC, SEQ, D_MODEL), *[out[n][k] for k in range(4) for n in names])
```

```python
import jax
import jax.numpy as jnp
import numpy as np
from jax import lax
from jax.experimental import pallas as pl
from jax.experimental.pallas import tpu as pltpu

F32, BF16 = jnp.float32, jnp.bfloat16

D_MODEL = 1024
SEQ = 2048
B_LOC = 2
T = B_LOC * SEQ
BLK = 128
MEM_LEN = 256
W_A, W_KV_A, W_B, W_C, D_MIX = 512, 128, 256, 256, 1024
D_IN = 2816
O_QA, O_KA, O_VA, O_QB, O_KB, O_VB, O_QC, O_Z = 0, 512, 640, 768, 1024, 1280, 1536, 1792
ROPE_THETA = 10000.0
LN_EPS = 1e-5
RMS_EPS = 1e-6
ALPHA = 2.0 ** 0.25
QK_SCALE = 0.125
N_CHIP = 4
SH_IN, SH_OUT, SH_MEM = D_IN // N_CHIP, D_MIX // N_CHIP, D_MODEL // N_CHIP
NEG = -1e30
ADAM_LR, ADAM_B1, ADAM_B2, ADAM_EPS, ADAM_WD, ADAM_STEP = 0.001, 0.9, 0.999, 1e-08, 0.01, 10
SV_W = 3072
MESH = pl.DeviceIdType.MESH

NN = ((1,), (0,))
NT = ((1,), (1,))
TN = ((0,), (0,))


def _dot(a, b, dims):
    return lax.dot_general(a, b, (dims, ((), ())), preferred_element_type=F32)


def _cp(sem=None, vmem_mb=None):
    kw = {}
    if sem is not None:
        kw["dimension_semantics"] = sem
    if vmem_mb is not None:
        kw["vmem_limit_bytes"] = vmem_mb * 1024 * 1024
    return pltpu.CompilerParams(**kw)


def _sds(shape, dtype):
    return pltpu.HBM(shape, dtype)


def _vm_sds(shape, dtype):
    return jax.ShapeDtypeStruct(shape, dtype)


def _pin(*args):
    return [pltpu.with_memory_space_constraint(a, pltpu.HBM) for a in args]


def _full(shape):
    n = len(shape)
    return pl.BlockSpec(shape, lambda *_: (0,) * n)


def _shard_rows(ref, n, chip, half):
    start = pl.multiple_of((2 * chip[0] + chip[1]) * n + half * (n // 2), 16)
    return ref.at[pl.ds(start, n // 2), :]


def _gather_weights(win_sh, wout_sh, wmem_sh):
    half, piece = SH_IN // 2, SH_IN // 4
    shards = ((SH_IN, D_MODEL), (SH_OUT, D_MODEL), (SH_MEM, 2 * W_C))

    def body(a_ref, b_ref, c_ref, oa_ref, ob_ref, oc_ref, raw_a, raw_b, raw_c, own_a, own_b, own_c,
             load_sem, store_sem, ici_send, ici_recv, d2d_send, d2d_recv):
        x, y, c = lax.axis_index("x"), lax.axis_index("y"), lax.axis_index("c")
        me, sibling = (x, y, c), (x, y, 1 - c)
        xn, yn, dg = (1 - x, y), (x, 1 - y), (1 - x, 1 - y)
        srcs, raws = (a_ref, b_ref, c_ref), (raw_a, raw_b, raw_c)
        owns, outs = (own_a, own_b, own_c), (oa_ref, ob_ref, oc_ref)
        loads = [pltpu.make_async_copy(srcs[a], raws[a], load_sem.at[a]) for a in range(3)]
        for cp in loads:
            cp.start()

        def rows(chip, hf, q):
            start = pl.multiple_of((2 * chip[0] + chip[1]) * SH_IN + hf * half + q * piece, 16)
            return oa_ref.at[pl.ds(start, piece), :]

        def copy(sems, k, chip, hf, q, to, src=None):
            blk = rows(chip, hf, q)
            return pltpu.make_async_remote_copy(
                src_ref=blk if src is None else src, dst_ref=blk, send_sem=sems[0].at[k], recv_sem=sems[1].at[k],
                device_id=to, device_id_type=MESH)

        def my_piece(q):
            return own_a.at[pl.ds(pl.multiple_of(c * half + q * piece, 16), piece), :]

        ici, d2d = (ici_send, ici_recv), (d2d_send, d2d_recv)
        stores, direct = [], []
        for a, (n, _) in enumerate(shards):
            loads[a].wait()
            owns[a][...] = raws[a][...].astype(BF16)
            mine = pl.ds(pl.multiple_of((2 * x + y) * n, 16), n)
            stores.append(pltpu.make_async_copy(owns[a], outs[a].at[mine, :], store_sem.at[a]))
            stores[-1].start()
            if a == 0:
                direct = [copy(ici, 0, (x, y), c, 0, (*xn, c), my_piece(0)),
                          copy(ici, 1, (x, y), c, 1, (*xn, c), my_piece(1)),
                          copy(ici, 3, (x, y), c, 0, (*yn, c), my_piece(0)),
                          copy(ici, 4, (x, y), c, 1, (*yn, c), my_piece(1))]
                for cp in direct:
                    cp.start()
        arrivals = [(0, xn, 0), (1, xn, 1), (3, yn, 0), (4, yn, 1), (2, dg, 1), (5, dg, 0)]
        passed = []
        for k, chip, q in arrivals:
            copy(ici, k, chip, c, q, me).wait_recv()
            if k == 0:
                passed.append(copy(ici, 5, xn, c, 0, (*yn, c)))
                passed[-1].start()
            if k == 4:
                passed.append(copy(ici, 2, yn, c, 1, (*xn, c)))
                passed[-1].start()
            passed.append(copy(d2d, k, chip, c, q, sibling))
            passed[-1].start()
        for k, chip, q in arrivals:
            copy(d2d, k, chip, 1 - c, q, me).wait_recv()
        for cp in direct + passed:
            cp.wait_send()
        for cp in stores:
            cp.wait()

    hbm = pl.BlockSpec(memory_space=pl.ANY)
    return pl.pallas_call(
        body, name="gather_weights",
        out_shape=(_sds((D_IN, D_MODEL), BF16), _sds((D_MIX, D_MODEL), BF16), _sds((D_MODEL, 2 * W_C), BF16)),
        in_specs=[hbm, hbm, hbm], out_specs=(hbm, hbm, hbm),
        scratch_shapes=([pltpu.VMEM(sh, F32) for sh in shards] + [pltpu.VMEM(sh, BF16) for sh in shards]
                        + [pltpu.SemaphoreType.DMA((3,))] * 2 + [pltpu.SemaphoreType.DMA((6,))] * 4),
        compiler_params=_cp(vmem_mb=40),
    )(*_pin(win_sh, wout_sh, wmem_sh))


def _rope(t, cos, sa, sb, sign):
    w = t.shape[1]
    reps = w // 128
    c, a, b = (jnp.tile(v, (1, reps)) if reps > 1 else v for v in (cos, sa, sb))
    rot = pltpu.roll(t, w - 32, 1) * a + pltpu.roll(t, 32, 1) * b
    return t * c + rot if sign > 0 else t * c - rot


def _in_proj(x, winT, b_in, cos, sa, sb, wout_own, wmem_own):
    tm = 512
    spt = SEQ // tm
    n_steps = T // tm
    forward_step = n_steps // 2

    def body(x_ref, w_ref, b_ref, cos_ref, sa_ref, sb_ref, wo_in, wm_in,
             xb_ref, qa_ref, ka_ref, va_ref, bn_ref, b4_ref, b16_ref, qc_ref, z_ref, wo_ref, wm_ref,
             scr, ici_send, ici_recv, d2d_send, d2d_recv):
        i = pl.program_id(0)
        mx, my, mc = lax.axis_index("x"), lax.axis_index("y"), lax.axis_index("c")
        chips = [(1 - mx, my), (mx, 1 - my), (1 - mx, 1 - my)]
        full = ((wo_ref, SH_OUT), (wm_ref, SH_MEM))

        def copy(sems, a, j, chip_of_block, half, to):
            blk = _shard_rows(full[a][0], full[a][1], chip_of_block, half)
            return pltpu.make_async_remote_copy(
                src_ref=blk, dst_ref=blk, send_sem=sems[0].at[a, j], recv_sem=sems[1].at[a, j],
                device_id=to, device_id_type=MESH)

        ici, d2d = (ici_send, ici_recv), (d2d_send, d2d_recv)
        pairs = [(a, j, chip) for j, chip in enumerate(chips) for a in range(2)]

        @pl.when(i == 0)
        def _():
            for a, j, chip in pairs:
                copy(ici, a, j, (mx, my), mc, (*chip, mc)).start()

        @pl.when(i == forward_step)
        def _():
            for a, j, chip in pairs:
                copy(ici, a, j, chip, mc, (mx, my, mc)).wait_recv()
                copy(d2d, a, j, chip, mc, (mx, my, 1 - mc)).start()

        @pl.when(i == n_steps - 1)
        def _():
            for a, j, chip in pairs:
                copy(d2d, a, j, chip, 1 - mc, (mx, my, mc)).wait_recv()
            for a, j, chip in pairs:
                copy(ici, a, j, (mx, my), mc, (*chip, mc)).wait_send()
                copy(d2d, a, j, chip, mc, (mx, my, 1 - mc)).wait_send()

        xb = x_ref[...].astype(BF16)
        xb_ref[...] = xb
        cos_t, sa_t, sb_t = cos_ref[...], sa_ref[...], sb_ref[...]

        def proj(r0, n):
            return _dot(xb, w_ref[r0:r0 + n, :], NT) + b_ref[:, r0:r0 + n]

        def rope(t):
            return _rope(t, cos_t, sa_t, sb_t, +1)

        qa_ref[...] = (rope(proj(O_QA, W_A)) * QK_SCALE).astype(BF16)
        ka_ref[...] = rope(proj(O_KA, W_KV_A)).astype(BF16)
        va_ref[...] = proj(O_VA, W_KV_A).astype(BF16)
        qc_ref[...] = (proj(O_QC, W_C) * QK_SCALE).astype(BF16)
        z_ref[...] = proj(O_Z, D_MIX).astype(BF16)
        parts = (rope(proj(O_QB, W_B)) * QK_SCALE, rope(proj(O_KB, W_B)), proj(O_VB, W_B))
        for k, part in enumerate(parts):
            bn_ref[:, 256 * k:256 * (k + 1)] = part.astype(BF16)
            scr[2 * k] = part[:, :128]
            scr[2 * k + 1] = part[:, 128:]
        for j in range(6):
            for res in range(4):
                b4_ref[0, res, :, 128 * j:128 * (j + 1)] = scr[j, pl.ds(res, tm // 4, stride=4), :].astype(BF16)
            for res in range(16):
                b16_ref[0, res, :, 128 * j:128 * (j + 1)] = scr[j, pl.ds(res, tm // 16, stride=16), :].astype(BF16)

    tok = lambda w: pl.BlockSpec((tm, w), lambda i: (i, 0))
    tab = pl.BlockSpec((tm, 128), lambda i: (i % spt, 0))
    hbm = pl.BlockSpec(memory_space=pl.ANY)
    return pl.pallas_call(
        body, name="in_proj", grid=(n_steps,),
        in_specs=[tok(D_MODEL), _full((D_IN, D_MODEL)), _full((1, D_IN)), tab, tab, tab, hbm, hbm],
        out_specs=(tok(D_MODEL), tok(W_A), tok(W_KV_A), tok(W_KV_A), tok(768),
                   pl.BlockSpec((1, 4, tm // 4, 768), lambda i: (i // spt, 0, i % spt, 0)),
                   pl.BlockSpec((1, 16, tm // 16, 768), lambda i: (i // spt, 0, i % spt, 0)),
                   tok(W_C), tok(D_MIX), hbm, hbm),
        out_shape=(_sds((T, D_MODEL), BF16), _sds((T, W_A), BF16), _sds((T, W_KV_A), BF16), _sds((T, W_KV_A), BF16),
                   _sds((T, 768), BF16), _sds((B_LOC, 4, SEQ // 4, 768), BF16), _sds((B_LOC, 16, SEQ // 16, 768), BF16),
                   _sds((T, W_C), BF16), _sds((T, D_MIX), BF16),
                   _sds((D_MIX, D_MODEL), BF16), _sds((D_MODEL, 2 * W_C), BF16)),
        input_output_aliases={6: 9, 7: 10},
        scratch_shapes=[pltpu.VMEM((6, tm, 128), F32)] + [pltpu.SemaphoreType.DMA((2, 3))] * 4,
        compiler_params=_cp(("arbitrary",), vmem_mb=48),
    )(*_pin(x, winT, b_in, cos, sa, sb, wout_own, wmem_own))


def _mem_kv(mem, wmem):
    def body(m_ref, w_ref, mb_ref, kv_ref):
        mb = m_ref[...].astype(BF16)
        mb_ref[...] = mb
        kv_ref[...] = _dot(mb, w_ref[...], NN).astype(BF16)

    n = B_LOC * MEM_LEN
    return pl.pallas_call(
        body, name="mem_kv",
        out_shape=(_sds((n, D_MODEL), BF16), _sds((n, 2 * W_C), BF16)),
    )(*_pin(mem, wmem))


class _Part:
    def __init__(self, body, args, in_specs, out_specs, out_shape, scratch=()):
        self.body, self.args, self.in_specs, self.out_specs, self.out_shape = body, args, in_specs, out_specs, out_shape
        self.scratch = list(scratch)


def _run_parts(name, parts, semantics, vmem_mb):
    n_in = [len(p.args) for p in parts]
    n_out = [len(p.out_shape) for p in parts]
    n_scr = [len(p.scratch) for p in parts]

    def body(*refs):
        ins, outs, scr = refs[:sum(n_in)], refs[sum(n_in):sum(n_in) + sum(n_out)], refs[sum(n_in) + sum(n_out):]
        i0 = o0 = s0 = 0
        for p, ni, no, ns in zip(parts, n_in, n_out, n_scr):
            p.body(*ins[i0:i0 + ni], *outs[o0:o0 + no], *scr[s0:s0 + ns])
            i0, o0, s0 = i0 + ni, o0 + no, s0 + ns

    res = pl.pallas_call(
        body, name=name, grid=(T // QR,),
        in_specs=[sp for p in parts for sp in p.in_specs], out_specs=tuple(sp for p in parts for sp in p.out_specs),
        out_shape=tuple(sh for p in parts for sh in p.out_shape),
        scratch_shapes=[sc for p in parts for sc in p.scratch],
        compiler_params=_cp((semantics,), vmem_mb=vmem_mb),
    )(*_pin(*[a for p in parts for a in p.args]))
    out, o0 = [], 0
    for no in n_out:
        out.append(tuple(res[o0:o0 + no]))
        o0 += no
    return out


QB = 8
QR = QB * BLK


def _lane_lo():
    return lax.broadcasted_iota(jnp.int32, (1, 128), 1) < 64


def _dup_head(k2, hk, lo):
    kf = k2.astype(F32)
    r = pltpu.roll(kf, 64, 1)
    return (jnp.where(lo, kf, r) if hk == 0 else jnp.where(lo, r, kf)).astype(BF16)


def _stack_heads(pairs, lo):
    parts = []
    for x2 in pairs:
        z = jnp.zeros_like(x2)
        parts += [jnp.where(lo, x2, z), jnp.where(lo, z, x2)]
    return jnp.concatenate(parts, axis=0)


def _prev_mode(kind, nb, j):
    if kind == "mem" or nb == 1:
        return "no"
    if nb <= QB:
        return "yes" if j % nb else "no"
    return "yes" if j else "dyn"


class _Attn:
    def __init__(self, kind, nb, max_dist, gqa, qw, kvw, qcb, kcb, vcb):
        self.kind, self.nb, self.gqa, self.qw, self.kvw = kind, nb, gqa, qw, kvw
        npairs = qw // 128
        self.groups = ([(hk, [2 * hk, 2 * hk + 1]) for hk in range(npairs // 2)] if gqa
                       else [(p, [p]) for p in range(npairs)])
        self.nh = 2 * len(self.groups[0][1])
        self.cols = 128 * self.nh
        self.reach = BLK - max_dist
        self.ext_prev = kind == "band" and nb > QB
        self.q_spec = pl.BlockSpec((QR, qw), lambda g: (g, qcb))
        self.row_spec = pl.BlockSpec((QR, qw), lambda g: (g, 0))
        self.stat_spec = pl.BlockSpec((QR, 128), lambda g: (g, 0))
        if kind == "mem":
            per = SEQ // QR
            self.kv_specs = [pl.BlockSpec((MEM_LEN, kvw), lambda g: (g // per, kcb)),
                             pl.BlockSpec((MEM_LEN, kvw), lambda g: (g // per, vcb))]
        else:
            self.kv_specs = [pl.BlockSpec((QR, kvw), lambda g: (g, kcb)), pl.BlockSpec((QR, kvw), lambda g: (g, vcb))]
            if self.ext_prev:
                self.kv_specs += [pl.BlockSpec((BLK, kvw), lambda g: (jnp.maximum(g * QB - 1, 0), kcb)),
                                  pl.BlockSpec((BLK, kvw), lambda g: (jnp.maximum(g * QB - 1, 0), vcb))]

    def masks(self):
        if self.kind == "mem":
            return None
        kj = lax.broadcasted_iota(jnp.int32, (2 * BLK, self.cols), 0)
        qi = lax.broadcasted_iota(jnp.int32, (2 * BLK, self.cols), 1) & (BLK - 1)
        kj1 = lax.broadcasted_iota(jnp.int32, (BLK, self.cols), 0)
        qi1 = lax.broadcasted_iota(jnp.int32, (BLK, self.cols), 1) & (BLK - 1)
        return kj, qi, kj1 <= qi1

    def keys(self, j, gi, kc_ref, vc_ref, kp_ref, vp_ref, lo, kq, g):
        def kv(k_ref, v_ref, r):
            if self.gqa:
                return _dup_head(k_ref[r, :], gi, lo), _dup_head(v_ref[r, :], gi, lo)
            sl = slice(128 * gi, 128 * (gi + 1))
            return k_ref[r, sl], v_ref[r, sl]

        if self.kind == "mem":
            key0 = pl.multiple_of((g // (SEQ // QR)) * MEM_LEN, MEM_LEN)
            return (*kv(kc_ref, vc_ref, slice(None)), None, [(0, MEM_LEN, key0)])
        kj, qi, cur = kq
        row0 = g * QR + BLK * j
        mode = _prev_mode(self.kind, self.nb, j)
        if mode == "no":
            return (*kv(kc_ref, vc_ref, slice(BLK * j, BLK * (j + 1))), cur, [(0, BLK, pl.multiple_of(row0, BLK))])
        if mode == "yes":
            mask = jnp.logical_and(kj >= qi + self.reach, kj <= qi + BLK)
            return (*kv(kc_ref, vc_ref, slice(BLK * (j - 1), BLK * (j + 1))), mask,
                    [(0, 2 * BLK, pl.multiple_of(row0 - BLK, BLK))])
        has_prev = ((g * QB) % self.nb) > 0
        hp = has_prev.astype(jnp.int32)
        mask = jnp.logical_and(kj >= qi * hp + (self.reach * hp + BLK * (1 - hp)), kj <= qi + BLK)
        kp, vp = kv(kp_ref, vp_ref, slice(None))
        kc, vc = kv(kc_ref, vc_ref, slice(0, BLK))
        return (jnp.concatenate([kp, kc], axis=0), jnp.concatenate([vp, vc], axis=0), mask,
                [(0, BLK, pl.multiple_of(jnp.maximum(row0 - BLK, 0), BLK)), (BLK, BLK, pl.multiple_of(row0, BLK))])


def _attn_fwd(q, qcb, qw, k, kcb, v, vcb, kvw, *, kind, nb=1, max_dist=BLK, gqa=False, sinks=None):
    a = _Attn(kind, nb, max_dist, gqa, qw, kvw, qcb, kcb, vcb)

    def body(*refs):
        it = iter(refs)
        q_ref, kc_ref, vc_ref = next(it), next(it), next(it)
        kp_ref, vp_ref = (next(it), next(it)) if a.ext_prev else (None, None)
        sink_ref = next(it) if sinks is not None else None
        o_ref, lse_ref = next(it), next(it)
        g = pl.program_id(0)
        lo = _lane_lo()
        top = lax.broadcasted_iota(jnp.int32, (128, 1), 0) < 64
        rid = lax.broadcasted_iota(jnp.int32, (8, 128), 0)
        kq = a.masks()
        stats = {}

        def scores(j, gi, pairs):
            rows = slice(BLK * j, BLK * (j + 1))
            qs = _stack_heads([q_ref[rows, 128 * p:128 * (p + 1)] for p in pairs], lo)
            kk, vv, mask, _ = a.keys(j, gi, kc_ref, vc_ref, kp_ref, vp_ref, lo, kq, g)
            pieces = [slice(r0, r0 + BLK) for r0 in range(0, kk.shape[0], BLK)]
            return dict(j=j, gi=gi, pairs=pairs, rows=rows, vv=vv, mask=mask, pieces=pieces,
                        ss=[_dot(kk[r], qs, NT) for r in pieces])

        def softmax(c):
            gi, mask = c["gi"], c["mask"]
            ss = [s if mask is None else jnp.where(mask[r], s, NEG) for r, s in zip(c["pieces"], c.pop("ss"))]
            m = jnp.max(ss[0], axis=0, keepdims=True)
            for s in ss[1:]:
                m = jnp.maximum(m, jnp.max(s, axis=0, keepdims=True))
            if sink_ref is not None:
                sk = jnp.concatenate([jnp.full((1, 128), sink_ref[0, a.nh * gi + i], F32) for i in range(a.nh)], axis=1)
                m = jnp.maximum(m, sk)
            ps = [jnp.exp(s - m) for s in ss]
            l = sum(jnp.sum(p, axis=0, keepdims=True) for p in ps)
            if sink_ref is not None:
                l = l + jnp.exp(sk - m)
            c["ps"] = [p.astype(BF16) for p in ps]
            c["l"], c["lse"] = l, m + jnp.log(l)

        def outputs(c):
            j, gi, rows = c["j"], c["gi"], c["rows"]
            ot = sum(_dot(c["vv"][r], p, TN) for r, p in zip(c["pieces"], c["ps"]))
            ot = ot * pl.reciprocal(c["l"], approx=True)
            for i, p in enumerate(c["pairs"]):
                o2t = jnp.where(top, ot[:, 256 * i:256 * i + 128], ot[:, 256 * i + 128:256 * i + 256])
                o_ref[rows, 128 * p:128 * (p + 1)] = o2t.T.astype(BF16)
            stat = stats.get(j, jnp.zeros((8, 128), F32))
            for i in range(a.nh):
                stat = jnp.where(rid == a.nh * gi + i, c["lse"][:, 128 * i:128 * (i + 1)], stat)
            stats[j] = stat
            if gi == a.groups[-1][0]:
                lse_ref[rows, :] = jnp.concatenate([stats.pop(j), jnp.zeros((120, 128), F32)], axis=0).T

        chains = [(j, gi, pairs) for j in range(QB) for gi, pairs in a.groups]
        live = {}
        for t in range(len(chains) + 2):
            if t < len(chains):
                live[t] = scores(*chains[t])
            if 0 <= t - 1 < len(chains):
                softmax(live[t - 1])
            if 0 <= t - 2 < len(chains):
                outputs(live.pop(t - 2))


    args = [q, k, v] + ([k, v] if a.ext_prev else [])
    in_specs = [a.q_spec] + a.kv_specs
    if sinks is not None:
        args.append(sinks)
        in_specs.append(pl.BlockSpec(memory_space=pltpu.SMEM))
    return _Part(body, args, in_specs, [a.row_spec, a.stat_spec], [_sds((T, qw), BF16), _sds((T, 128), F32)])


def _attn_bwd(q, qcb, qw, k, kcb, v, vcb, kvw, do, lse, dl, *, kind, nb=1, max_dist=BLK, gqa=False, sinkv=None,
              mem_in=None):
    a = _Attn(kind, nb, max_dist, gqa, qw, kvw, qcb, kcb, vcb)

    def body(*refs):
        it = iter(refs)
        q_ref, kc_ref, vc_ref = next(it), next(it), next(it)
        kp_ref, vp_ref = (next(it), next(it)) if a.ext_prev else (None, None)
        do_ref, lse_ref, dl_ref = next(it), next(it), next(it)
        sinkv_ref = next(it) if sinkv is not None else None
        mem_ref = next(it) if kind == "mem" else None
        dq_ref = next(it)
        if kind == "mem":
            gmem_ref = next(it)
        else:
            dk_out, dv_out = next(it), next(it)
        dsink_ref = next(it) if sinkv is not None else None
        if kind != "mem":
            dk_ref, dv_ref, stage_k, stage_v, flush_sem = next(it), next(it), next(it), next(it), next(it)
        else:
            dkv_ref = next(it)
        g = pl.program_id(0)
        lo = _lane_lo()
        top = lax.broadcasted_iota(jnp.int32, (128, 1), 0) < 64

        @pl.when(g == 0)
        def _():
            if kind == "mem":
                dkv_ref[...] = jnp.zeros_like(dkv_ref)
            else:
                dk_ref[...] = jnp.zeros_like(dk_ref)
                dv_ref[...] = jnp.zeros_like(dv_ref)
            if dsink_ref is not None:
                dsink_ref[...] = jnp.zeros_like(dsink_ref)

        kq = a.masks()
        stats_t = {}

        def first_matmuls(j, gi, pairs):
            rows = slice(BLK * j, BLK * (j + 1))
            if j not in stats_t:
                stats_t[j] = (lse_ref[rows, :].T, dl_ref[rows, :].T)
            lse_t, dl_t = stats_t[j]
            heads = [a.nh * gi + i for i in range(a.nh)]
            c = dict(rows=rows, gi=gi, pairs=pairs)
            c["qs"] = _stack_heads([q_ref[rows, 128 * p:128 * (p + 1)] for p in pairs], lo)
            c["dos"] = _stack_heads([do_ref[rows, 128 * p:128 * (p + 1)] for p in pairs], lo)
            c["lse_row"] = jnp.concatenate([lse_t[h:h + 1, :] for h in heads], axis=1)
            c["dl_row"] = jnp.concatenate([dl_t[h:h + 1, :] for h in heads], axis=1)
            c["kk"], vv, c["mask"], c["dests"] = a.keys(j, gi, kc_ref, vc_ref, kp_ref, vp_ref, lo, kq, g)
            c["s"] = _dot(c["kk"], c["qs"], NT)
            c["dp"] = _dot(vv, c["dos"], NT)
            return c

        def elementwise(c):
            s = c.pop("s")
            if c["mask"] is not None:
                s = jnp.where(c["mask"], s, NEG)
            p = jnp.exp(s - c["lse_row"])
            c["ds"] = (p * (c.pop("dp") - c["dl_row"])).astype(BF16)
            c["p"] = p.astype(BF16)

        def last_matmuls(c):
            gi, rows = c["gi"], c["rows"]
            dqt = _dot(c["kk"], c["ds"], TN)
            ck = _dot(c["ds"], c["qs"], NN)
            cv = _dot(c["p"], c["dos"], NN)
            if gqa:
                sel = lo if gi == 0 else jnp.logical_not(lo)
                ck = jnp.where(sel, ck + pltpu.roll(ck, 64, 1), 0.0)
                cv = jnp.where(sel, cv + pltpu.roll(cv, 64, 1), 0.0)
                kcols = slice(0, 128)
            else:
                kcols = slice(128 * gi, 128 * (gi + 1))
            for r0, nr, key0 in c["dests"]:
                krows = pl.ds(key0, nr)
                if kind == "mem":
                    dkv_ref[krows, kcols] += ck[r0:r0 + nr]
                    dkv_ref[krows, slice(kvw + kcols.start, kvw + kcols.stop)] += cv[r0:r0 + nr]
                else:
                    dk_ref[krows, kcols] += ck[r0:r0 + nr]
                    dv_ref[krows, kcols] += cv[r0:r0 + nr]
            for i, p in enumerate(c["pairs"]):
                dq2t = jnp.where(top, dqt[:, 256 * i:256 * i + 128], dqt[:, 256 * i + 128:256 * i + 256])
                dq_ref[rows, 128 * p:128 * (p + 1)] = dq2t.T.astype(BF16)

        chains = [(j, gi, pairs) for j in range(QB) for gi, pairs in a.groups]
        live = {}
        for t in range(len(chains) + 2):
            if t < len(chains):
                live[t] = first_matmuls(*chains[t])
            if 0 <= t - 1 < len(chains):
                elementwise(live[t - 1])
            if 0 <= t - 2 < len(chains):
                last_matmuls(live.pop(t - 2))
        if dsink_ref is not None:
            ps = jnp.exp(sinkv_ref[...] - lse_ref[...]) * dl_ref[...]
            dsink_ref[...] += jnp.sum(ps, axis=0, keepdims=True)
        if kind == "mem":
            @pl.when(g == T // QR - 1)
            def _():
                gmem_ref[...] = _dot(mem_ref[...], dkv_ref[...].astype(BF16), TN)
        else:
            n_steps = T // QR

            def flush(step):
                rows = pl.ds(pl.multiple_of(step * QR, QR), QR)
                out = []
                for acc, stage, dst, i in ((dk_ref, stage_k, dk_out, 0), (dv_ref, stage_v, dv_out, 1)):
                    stage[...] = acc[rows, :].astype(BF16)
                    out.append(pltpu.make_async_copy(stage, dst.at[rows, :], flush_sem.at[i]))
                return out

            def flushed(step):
                rows = pl.ds(pl.multiple_of(step * QR, QR), QR)
                return [pltpu.make_async_copy(stage, dst.at[rows, :], flush_sem.at[i])
                        for stage, dst, i in ((stage_k, dk_out, 0), (stage_v, dv_out, 1))]

            @pl.when(g >= 2)
            def _():
                for cp in flushed(g - 2):
                    cp.wait()

            @pl.when(g >= 1)
            def _():
                for cp in flush(g - 1):
                    cp.start()

            @pl.when(g == n_steps - 1)
            def _():
                for cp in flushed(g - 1):
                    cp.wait()
                for cp in flush(g):
                    cp.start()
                for cp in flushed(g):
                    cp.wait()

    args = [q, k, v] + ([k, v] if a.ext_prev else []) + [do, lse, dl]
    in_specs = [a.q_spec] + a.kv_specs + [a.row_spec, a.stat_spec, a.stat_spec]
    if sinkv is not None:
        args.append(sinkv)
        in_specs.append(_full((1, 128)))
    if kind == "mem":
        args.append(mem_in)
        in_specs.append(pl.BlockSpec(mem_in.shape, lambda g: (0, 0), pipeline_mode=pl.Buffered(1)))
    out_shape = [_sds((T, qw), BF16)]
    out_specs = [a.row_spec]
    scratch = []
    if kind == "mem":
        out_shape.append(_sds((D_MODEL, 2 * kvw), F32))
        out_specs.append(pl.BlockSpec((D_MODEL, 2 * kvw), lambda g: (0, 0), pipeline_mode=pl.Buffered(1)))
        scratch = [pltpu.VMEM((B_LOC * MEM_LEN, 2 * kvw), F32)]
    else:
        out_shape += [_sds((T, kvw), BF16)] * 2
        out_specs += [pl.BlockSpec(memory_space=pl.ANY)] * 2
        scratch = [pltpu.VMEM((T, kvw), F32)] * 2 + [pltpu.VMEM((QR, kvw), BF16)] * 2 + [pltpu.SemaphoreType.DMA((2,))]
    if sinkv is not None:
        out_shape.append(_sds((1, 128), F32))
        out_specs.append(_full((1, 128)))
    return _Part(body, args, in_specs, out_specs, out_shape, scratch)


def _dot2(v, w_ref):
    hi = v.astype(BF16)
    lo = (v - hi.astype(F32)).astype(BF16)
    return _dot(hi, w_ref[...], NN) + _dot(lo, w_ref[...], NN)


def _middle(oa, o1, l1, o4, l4, o16, l16, oc, z, x, tgt, g_br, ln_g, ln_b, wout, spread4, gather4, gather8):
    tm = 512
    spt = SEQ // tm

    def body(oa_ref, o1_ref, l1_ref, o4_ref, l4_ref, o16_ref, l16_ref, oc_ref, z_ref, x_ref, t_ref,
             g_ref, lg_ref, lb_ref, w_ref, sp4_ref, ga4_ref, ga8_ref,
             du_ref, dz_ref, doa_ref, dla_ref,
             dobn_ref, lsen_ref, dlbn_ref, dob4_ref, lse4_ref, dlb4_ref, dob16_ref, lse16_ref, dlb16_ref,
             doc_ref, dlc_ref, acc_ref, gout_ref, scr):
        i = pl.program_id(0)

        @pl.when(i == 0)
        def _():
            acc_ref[...] = jnp.zeros_like(acc_ref)
            gout_ref[...] = jnp.zeros_like(gout_ref)

        for res in range(4):
            rows = pl.ds(res, tm // 4, stride=4)
            for j in range(2):
                scr[j, rows, :] = o4_ref[0, res, :, 128 * j:128 * (j + 1)].astype(F32)
            scr[2, rows, :] = l4_ref[0, res]
        for res in range(16):
            rows = pl.ds(res, tm // 16, stride=16)
            for j in range(2):
                scr[3 + j, rows, :] = o16_ref[0, res, :, 128 * j:128 * (j + 1)].astype(F32)
            scr[5, rows, :] = l16_ref[0, res]
        inv_d = 1.0 / D_MODEL
        gb, lg, lb = g_ref[...], lg_ref[...], lb_ref[...]

        def rms(o):
            r = lax.rsqrt(jnp.sum(o * o, axis=1, keepdims=True) * (1.0 / o.shape[1]) + RMS_EPS)
            return o * r, r

        def rms_bwd(dn_, n_, r):
            return r * (dn_ - n_ * (jnp.sum(dn_ * n_, axis=1, keepdims=True) * (1.0 / n_.shape[1])))

        def forward(rs):
            o4v = jnp.concatenate([scr[0, rs, :], scr[1, rs, :]], axis=1)
            o16v = jnp.concatenate([scr[3, rs, :], scr[4, rs, :]], axis=1)
            l1v, l4v, l16v = l1_ref[rs, :], scr[2, rs, :], scr[5, rs, :]
            mx = jnp.maximum(jnp.maximum(l1v, l4v), l16v)
            e1, e4, e16 = jnp.exp(l1v - mx), jnp.exp(l4v - mx), jnp.exp(l16v - mx)
            ssum = e1 + e4 + e16
            inv = 1.0 / ssum
            c = dict(rs=rs, lse_b=mx + jnp.log(ssum))
            c["ob"] = (_dot2(e1 * inv, sp4_ref) * o1_ref[rs, :].astype(F32) + _dot2(e4 * inv, sp4_ref) * o4v
                       + _dot2(e16 * inv, sp4_ref) * o16v)
            c["oa"], c["oc"] = oa_ref[rs, :].astype(F32), oc_ref[rs, :].astype(F32)
            na, c["ra"] = rms(c["oa"])
            nb_, c["rb"] = rms(c["ob"])
            nc, c["rc"] = rms(c["oc"])
            c["n"] = jnp.concatenate([na, nb_, nc], axis=1)
            c["zf"] = z_ref[rs, :].astype(F32)
            c["sig"] = 1.0 / (1.0 + jnp.exp(-c["zf"]))
            c["sz"] = c["zf"] * c["sig"]
            c["yb"] = (c["n"] * gb * c["sz"]).astype(BF16)
            c["y2"] = _dot(c["yb"], w_ref[...], NN)
            return c

        def norm(c):
            rs = c["rs"]
            u = ALPHA * x_ref[rs, :] + c.pop("y2")
            mu = jnp.sum(u, axis=1, keepdims=True) * inv_d
            uc = u - mu
            rstd = lax.rsqrt(jnp.sum(uc * uc, axis=1, keepdims=True) * inv_d + LN_EPS)
            xh = uc * rstd
            diff = xh * lg + lb - t_ref[rs, :]
            acc_ref[0:1, :] += jnp.sum(diff * diff, axis=0, keepdims=True) * (0.5 * inv_d)
            dout = diff * inv_d
            acc_ref[2:3, :] += jnp.sum(dout * xh, axis=0, keepdims=True)
            acc_ref[3:4, :] += jnp.sum(dout, axis=0, keepdims=True)
            dxh = dout * lg
            du = rstd * (dxh - jnp.sum(dxh, axis=1, keepdims=True) * inv_d
                         - xh * (jnp.sum(dxh * xh, axis=1, keepdims=True) * inv_d))
            dub = du.astype(BF16)
            du_ref[rs, :] = dub
            c["dy"] = _dot(dub, w_ref[...], NT)
            gout_ref[...] += _dot(c.pop("yb"), dub, TN)

        def backward(c):
            rs, n, dy, zf, sig = c["rs"], c["n"], c["dy"], c["zf"], c["sig"]
            t1 = dy * c["sz"]
            acc_ref[1:2, :] += jnp.sum(t1 * n, axis=0, keepdims=True)
            dn = t1 * gb
            dz_ref[rs, :] = (dy * n * gb * (sig * (1.0 + zf * (1.0 - sig)))).astype(BF16)
            doa = rms_bwd(dn[:, :W_A], n[:, :W_A], c["ra"])
            dob = rms_bwd(dn[:, W_A:W_A + W_B], n[:, W_A:W_A + W_B], c["rb"])
            doc = rms_bwd(dn[:, W_A + W_B:], n[:, W_A + W_B:], c["rc"])
            doa_ref[rs, :] = doa.astype(BF16)
            dla_ref[rs, :] = _dot2(doa * c["oa"], ga8_ref)
            doc_ref[rs, :] = doc.astype(BF16)
            dlc_ref[rs, :] = _dot2(doc * c["oc"], ga4_ref)
            dobn_ref[rs, :] = dob.astype(BF16)
            lsen_ref[rs, :] = c["lse_b"]
            dlbn_ref[rs, :] = _dot2(dob * c["ob"], ga4_ref)
            scr[0, rs, :] = dob[:, :128]
            scr[1, rs, :] = dob[:, 128:]

        halves = [slice(h * (tm // 2), (h + 1) * (tm // 2)) for h in range(2)]
        live = {}
        for t in range(len(halves) + 2):
            if t < len(halves):
                live[t] = forward(halves[t])
            if 0 <= t - 1 < len(halves):
                norm(live[t - 1])
            if 0 <= t - 2 < len(halves):
                backward(live.pop(t - 2))
        for j in range(2):
            sl = slice(128 * j, 128 * (j + 1))
            for res in range(4):
                dob4_ref[0, res, :, sl] = scr[j, pl.ds(res, tm // 4, stride=4), :].astype(BF16)
            for res in range(16):
                dob16_ref[0, res, :, sl] = scr[j, pl.ds(res, tm // 16, stride=16), :].astype(BF16)
        for res in range(4):
            rows = pl.ds(res, tm // 4, stride=4)
            lse4_ref[0, res] = lsen_ref[rows, :]
            dlb4_ref[0, res] = dlbn_ref[rows, :]
        for res in range(16):
            rows = pl.ds(res, tm // 16, stride=16)
            lse16_ref[0, res] = lsen_ref[rows, :]
            dlb16_ref[0, res] = dlbn_ref[rows, :]


    tok = lambda w: pl.BlockSpec((tm, w), lambda i: (i, 0))
    p4 = lambda w: pl.BlockSpec((1, 4, tm // 4, w), lambda i: (i // spt, 0, i % spt, 0))
    p16 = lambda w: pl.BlockSpec((1, 16, tm // 16, w), lambda i: (i // spt, 0, i % spt, 0))
    s4 = lambda w, dt: _sds((B_LOC, 4, SEQ // 4, w), dt)
    s16 = lambda w, dt: _sds((B_LOC, 16, SEQ // 16, w), dt)
    row = _full((1, D_MODEL))
    return pl.pallas_call(
        body, name="middle", grid=(T // tm,),
        in_specs=[tok(W_A), tok(W_B), tok(128), p4(W_B), p4(128), p16(W_B), p16(128), tok(W_C), tok(D_MIX),
                  tok(D_MODEL), tok(D_MODEL), row, row, row, _full((D_MIX, D_MODEL)),
                  _full((128, W_B)), _full((W_B, 128)), _full((W_A, 128))],
        out_specs=(tok(D_MODEL), tok(D_MIX), tok(W_A), tok(128),
                   tok(W_B), tok(128), tok(128), p4(W_B), p4(128), p4(128), p16(W_B), p16(128), p16(128),
                   tok(W_C), tok(128), _full((8, D_MODEL)), _full((D_MIX, D_MODEL))),
        out_shape=(_sds((T, D_MODEL), BF16), _sds((T, D_MIX), BF16),
                   _sds((T, W_A), BF16), _sds((T, 128), F32),
                   _sds((T, W_B), BF16), _sds((T, 128), F32), _sds((T, 128), F32),
                   s4(W_B, BF16), s4(128, F32), s4(128, F32), s16(W_B, BF16), s16(128, F32), s16(128, F32),
                   _sds((T, W_C), BF16), _sds((T, 128), F32), _sds((8, D_MODEL), F32),
                   _sds((D_MIX, D_MODEL), F32)),
        scratch_shapes=[pltpu.VMEM((6, tm, 128), F32)],
        compiler_params=_cp(("arbitrary",), vmem_mb=56),
    )(*_pin(oa, o1, l1, o4, l4, o16, l16, oc, z, x, tgt, g_br, ln_g, ln_b, wout, spread4, gather4, gather8))


class _ReduceScatter:
    def __init__(self, shapes):
        self.shapes = shapes

    def scratch_shapes(self):
        out = []
        for n, w in self.shapes:
            h, p = n // 2, n // 4
            out += [pltpu.VMEM((4, h, w), F32), pltpu.VMEM((4, h, w), F32), pltpu.VMEM((6, p, w), BF16),
                    pltpu.VMEM((6, p, w), BF16), pltpu.VMEM((2, p, w), F32), pltpu.VMEM((h, w), F32)]
        na = len(self.shapes)
        dma = pltpu.SemaphoreType.DMA
        return out + [dma((na, 4, 2)), dma((na, 4, 2)), dma((na, 4, 2)), dma((na, 6)), dma((na, 6)), dma((na,)),
                      dma((na,)), dma((na,))]

    def bind(self, g_refs, r_refs, scratch):
        na = len(self.shapes)
        bufs = [scratch[6 * a:6 * a + 6] for a in range(na)]
        mine, sib, stage, land, keep, tot = (tuple(b[i] for b in bufs) for i in range(6))
        loc_sem, s1_send, s1_recv, s2_send, s2_recv, s3_send, s3_recv, st_sem = scratch[6 * na:6 * na + 8]
        x, y, c = lax.axis_index("x"), lax.axis_index("y"), lax.axis_index("c")
        me, sibling = (x, y, c), (x, y, 1 - c)
        xn, yn, dg = (1 - x, y), (x, 1 - y), (1 - x, 1 - y)
        idx = lambda chip: 2 * chip[0] + chip[1]
        my_chip = idx((x, y))
        order = [(idx(xn), 0), (idx(dg), 0), (idx(dg), 1), (idx(yn), 1), (idx(xn), 1), (idx(yn), 0), (my_chip, 0),
                 (my_chip, 1)]

        def rows(a, k, half):
            n = self.shapes[a][0]
            return pl.ds(pl.multiple_of(k * n + half * (n // 2), 8), n // 2)

        def piece(a, q):
            p = self.shapes[a][0] // 4
            return slice(q * p, (q + 1) * p)

        def piece_rows(a, k, half, q):
            n = self.shapes[a][0]
            return pl.ds(pl.multiple_of(k * n + half * (n // 2) + q * (n // 4), 8), n // 4)

        def load(a, k, q):
            return pltpu.make_async_copy(g_refs[a].at[piece_rows(a, k, c, q), :], mine[a].at[k, piece(a, q), :],
                                         loc_sem.at[a, k, q])

        def s1(a, k, q, half):
            return pltpu.make_async_remote_copy(
                src_ref=g_refs[a].at[piece_rows(a, k, half, q), :], dst_ref=sib[a].at[k, piece(a, q), :],
                send_sem=s1_send.at[a, k, q], recv_sem=s1_recv.at[a, k, q], device_id=sibling, device_id_type=MESH)

        def s2(a, i, to):
            return pltpu.make_async_remote_copy(
                src_ref=stage[a].at[i], dst_ref=land[a].at[i], send_sem=s2_send.at[a, i], recv_sem=s2_recv.at[a, i],
                device_id=to, device_id_type=MESH)

        via = {0: xn, 1: xn, 2: yn, 3: yn, 4: yn, 5: xn}

        def s3(a, half, to):
            return pltpu.make_async_remote_copy(
                src_ref=tot[a], dst_ref=r_refs[a].at[rows(a, 0, half), :], send_sem=s3_send.at[a],
                recv_sem=s3_recv.at[a], device_id=to, device_id_type=MESH)

        def store(a):
            return pltpu.make_async_copy(tot[a], r_refs[a].at[rows(a, 0, c), :], st_sem.at[a])

        def start():
            for k, q in order:
                for a in range(na):
                    load(a, k, q).start()
                    s1(a, k, q, 1 - c).start()

        def chip_sum(a, k, q):
            load(a, k, q).wait()
            s1(a, k, q, c).wait_recv()
            return mine[a][k, piece(a, q), :] + sib[a][k, piece(a, q), :]

        def exchange():
            for a in range(na):
                stage[a][0] = chip_sum(a, idx(xn), 0).astype(BF16)
                s2(a, 0, (*xn, c)).start()
                stage[a][1] = chip_sum(a, idx(dg), 0).astype(BF16)
                s2(a, 1, (*xn, c)).start()
                stage[a][3] = chip_sum(a, idx(dg), 1).astype(BF16)
                s2(a, 3, (*yn, c)).start()
                stage[a][2] = chip_sum(a, idx(yn), 1).astype(BF16)
                s2(a, 2, (*yn, c)).start()
                keep[a][1] = chip_sum(a, idx(xn), 1)
                keep[a][0] = chip_sum(a, idx(yn), 0)
                tot[a][piece(a, 0), :] = chip_sum(a, my_chip, 0)
                tot[a][piece(a, 1), :] = chip_sum(a, my_chip, 1)

        def relay():
            for a in range(na):
                P, Q = piece(a, 0), piece(a, 1)
                s2(a, 1, me).wait_recv()
                stage[a][4] = (keep[a][0] + land[a][1].astype(F32)).astype(BF16)
                s2(a, 4, (*yn, c)).start()
                s2(a, 3, me).wait_recv()
                stage[a][5] = (keep[a][1] + land[a][3].astype(F32)).astype(BF16)
                s2(a, 5, (*xn, c)).start()
                s2(a, 0, me).wait_recv()
                tot[a][P, :] += land[a][0].astype(F32)
                s2(a, 2, me).wait_recv()
                tot[a][Q, :] += land[a][2].astype(F32)

        def finish():
            for a in range(na):
                P, Q = piece(a, 0), piece(a, 1)
                s2(a, 4, me).wait_recv()
                tot[a][P, :] += land[a][4].astype(F32)
                s2(a, 5, me).wait_recv()
                tot[a][Q, :] += land[a][5].astype(F32)
                s3(a, c, sibling).start()
                store(a).start()

        def drain():
            for a in range(na):
                s3(a, 1 - c, me).wait_recv()
                store(a).wait()
            for a in range(na):
                for k, q in order:
                    s1(a, k, q, 1 - c).wait_send()
                for i in range(6):
                    s2(a, i, (*via[i], c)).wait_send()
                s3(a, c, sibling).wait_send()

        return start, exchange, relay, finish, drain

    def part(self, grads, steps):
        def body(*refs):
            na = len(self.shapes)
            i = pl.program_id(0)
            for step, phase in zip(steps, self.bind(refs[:na], refs[na:2 * na], refs[2 * na:])):
                pl.when(i == step)(phase)

        hbm = pl.BlockSpec(memory_space=pl.ANY)
        return _Part(body, list(grads), [hbm] * len(grads), [hbm] * len(grads),
                     [_sds((n, w), F32) for n, w in self.shapes], self.scratch_shapes())


def _dh_dx(dqa, dka, dva, dqn, dkn, dvn, dq4, dk4, dv4, dq16, dk16, dv16, dqc, dz, du, xb, cos, sa, sb, winT):
    tm = 512
    spt = SEQ // tm

    def body(dqa_ref, dka_ref, dva_ref, dqn_ref, dkn_ref, dvn_ref, dq4_ref, dk4_ref, dv4_ref,
             dq16_ref, dk16_ref, dv16_ref, dqc_ref, dz_ref, du_ref, xb_ref, cos_ref, sa_ref, sb_ref, w_ref,
             gx_ref, db_ref, gin_ref, dh_ref, scr):
        i = pl.program_id(0)

        @pl.when(i == 0)
        def _():
            db_ref[...] = jnp.zeros_like(db_ref)
            gin_ref[...] = jnp.zeros_like(gin_ref)

        cos_t, sa_t, sb_t = cos_ref[...], sa_ref[...], sb_ref[...]

        def rope_t(t):
            return _rope(t, cos_t, sa_t, sb_t, -1)

        def put(r0, val):
            n = val.shape[1]
            dh_ref[:, r0:r0 + n] = val.astype(BF16)
            db_ref[:, r0:r0 + n] += jnp.sum(val, axis=0, keepdims=True)

        put(O_QA, rope_t(dqa_ref[...].astype(F32)) * QK_SCALE)
        put(O_KA, rope_t(dka_ref[...].astype(F32)))
        put(O_VA, dva_ref[...].astype(F32))
        put(O_QC, dqc_ref[...].astype(F32) * QK_SCALE)
        put(O_Z, dz_ref[...].astype(F32))
        for k, (n_ref, r4, r16) in enumerate(((dqn_ref, dq4_ref, dq16_ref), (dkn_ref, dk4_ref, dk16_ref),
                                               (dvn_ref, dv4_ref, dv16_ref))):
            for j in range(2):
                sl = slice(128 * j, 128 * (j + 1))
                scr[2 * k + j] = n_ref[:, sl].astype(F32)
                for res in range(4):
                    scr[2 * k + j, pl.ds(res, tm // 4, stride=4), :] += r4[0, res, :, sl].astype(F32)
                for res in range(16):
                    scr[2 * k + j, pl.ds(res, tm // 16, stride=16), :] += r16[0, res, :, sl].astype(F32)
        cat = lambda a: jnp.concatenate([scr[a], scr[a + 1]], axis=1)
        put(O_QB, rope_t(cat(0)) * QK_SCALE)
        put(O_KB, rope_t(cat(2)))
        put(O_VB, cat(4))
        gx_ref[...] = _dot(dh_ref[...], w_ref[...], NN) + ALPHA * du_ref[...].astype(F32)
        gin_ref[...] += _dot(dh_ref[...], xb_ref[...], TN)

    tok = lambda w: pl.BlockSpec((tm, w), lambda i: (i, 0))
    tab = pl.BlockSpec((tm, 128), lambda i: (i % spt, 0))
    p4 = pl.BlockSpec((1, 4, tm // 4, W_B), lambda i: (i // spt, 0, i % spt, 0))
    p16 = pl.BlockSpec((1, 16, tm // 16, W_B), lambda i: (i // spt, 0, i % spt, 0))
    once = lambda shape: pl.BlockSpec(shape, lambda i: (0, 0), pipeline_mode=pl.Buffered(1))
    return pl.pallas_call(
        body, name="dh_dx", grid=(T // tm,),
        in_specs=[tok(W_A), tok(W_KV_A), tok(W_KV_A), tok(W_B), tok(W_B), tok(W_B), p4, p4, p4, p16, p16, p16,
                  tok(W_C), tok(D_MIX), tok(D_MODEL), tok(D_MODEL), tab, tab, tab, once((D_IN, D_MODEL))],
        out_specs=(tok(D_MODEL), _full((1, D_IN)), once((D_IN, D_MODEL))),
        out_shape=(_sds((T, D_MODEL), F32), _sds((1, D_IN), F32), _sds((D_IN, D_MODEL), F32)),
        scratch_shapes=[pltpu.VMEM((tm, D_IN), BF16), pltpu.VMEM((6, tm, 128), F32)],
        compiler_params=_cp(("arbitrary",), vmem_mb=56),
    )(*_pin(dqa, dka, dva, dqn, dkn, dvn, dq4, dk4, dv4, dq16, dk16, dv16, dqc, dz, du, xb, cos, sa, sb, winT))


def _reduce_grads(g_in, acc, dbin, dsink):
    rs = _ReduceScatter([(SH_IN, D_MODEL)])

    def body(g_ref, acc_ref, dbin_ref, dsink_ref, r_ref, sv_ref, sv_mine, sv_all, sv_send, sv_recv, *rs_scratch):
        x, y, c = lax.axis_index("x"), lax.axis_index("y"), lax.axis_index("c")
        chips = [(1 - x, y), (x, 1 - y), (1 - x, 1 - y)]
        start, exchange, relay, finish, drain = rs.bind((g_ref,), (r_ref,), rs_scratch)
        start()

        sv_mine[...] = jnp.zeros_like(sv_mine)
        sv_mine[0:4, 0:D_MODEL] = acc_ref[0:4, :]
        sv_mine[4:5, 0:D_IN] = dbin_ref[...]
        sv_mine[5:6, 0:128] = dsink_ref[...]
        my_dev = 4 * x + 2 * y + c
        others = [(x, y, 1 - c)] + [(*chip, cc) for chip in chips for cc in (c, 1 - c)]

        def sv_copy(j, to):
            return pltpu.make_async_remote_copy(
                src_ref=sv_mine, dst_ref=sv_all.at[my_dev], send_sem=sv_send.at[j], recv_sem=sv_recv.at[j],
                device_id=to, device_id_type=MESH)

        sv_sends = [sv_copy(j, to) for j, to in enumerate(others)]
        for cp in sv_sends:
            cp.start()
        exchange()
        relay()
        finish()
        sv_all[my_dev] = sv_mine[...]
        for j in range(7):
            sv_copy(j, (x, y, c)).wait_recv()
        tot = sv_all[0]
        for d in range(1, 8):
            tot = tot + sv_all[d]
        sv_ref[...] = tot
        drain()
        for cp in sv_sends:
            cp.wait_send()

    vm = pl.BlockSpec(memory_space=pltpu.VMEM)
    hbm = pl.BlockSpec(memory_space=pl.ANY)
    return pl.pallas_call(
        body, name="reduce_grads",
        out_shape=(_sds((SH_IN, D_MODEL), F32), _vm_sds((8, SV_W), F32)),
        in_specs=[hbm, vm, vm, vm], out_specs=(hbm, vm),
        scratch_shapes=[pltpu.VMEM((8, SV_W), F32), pltpu.VMEM((8, 8, SV_W), F32),
                        pltpu.SemaphoreType.DMA((7,)), pltpu.SemaphoreType.DMA((7,))] + rs.scratch_shapes(),
        compiler_params=_cp(vmem_mb=40),
    )(pltpu.with_memory_space_constraint(g_in, pltpu.HBM), acc, dbin, dsink)


def _adamw_update(w, g, m, v):
    nm = ADAM_B1 * m + (1.0 - ADAM_B1) * g
    nv = ADAM_B2 * v + (1.0 - ADAM_B2) * (g * g)
    m_hat = nm / (1.0 - ADAM_B1 ** ADAM_STEP)
    v_hat = nv / (1.0 - ADAM_B2 ** ADAM_STEP)
    return -ADAM_LR * (m_hat / (jnp.sqrt(v_hat) + ADAM_EPS) + ADAM_WD * w), nm, nv


def _adamw_big(items, n_steps=4):
    def body(*refs):
        ins, outs = refs[:4 * len(items)], refs[4 * len(items):]
        for p in range(len(items)):
            w_ref, g_ref, m_ref, v_ref = ins[4 * p:4 * p + 4]
            gv = g_ref[...]
            outs[4 * p][...] = gv
            outs[4 * p + 1][...], outs[4 * p + 2][...], outs[4 * p + 3][...] = _adamw_update(
                w_ref[...], gv, m_ref[...], v_ref[...])

    specs, shapes, args = [], [], []
    for w, g, m, v in items:
        rows, width = w.shape
        specs += [pl.BlockSpec((rows // n_steps, width), lambda i: (i, 0))] * 4
        shapes += [_sds((rows, width), F32)] * 4
        args += [w, g, m, v]
    res = pl.pallas_call(
        body, name="adamw_big", grid=(n_steps,), in_specs=specs, out_specs=tuple(specs), out_shape=tuple(shapes),
        compiler_params=_cp(("parallel",), vmem_mb=40),
    )(*_pin(*args))
    return [tuple(res[4 * p:4 * p + 4]) for p in range(len(items))]


def _adamw_small(sv, ws, ms, vs):
    where = ((4, D_IN, 1.0), (5, 8, -1.0), (1, D_MIX, 1.0), (2, D_MODEL, 1.0), (3, D_MODEL, 1.0))

    def body(sv_ref, *refs):
        ins, loss_ref, outs = refs[:15], refs[15], refs[16:]
        loss_ref[...] = jnp.sum(sv_ref[0:1, 0:D_MODEL], axis=1, keepdims=True)
        for p, (row, width, sign) in enumerate(where):
            gv = sign * sv_ref[row:row + 1, 0:width]
            outs[4 * p][...] = gv
            outs[4 * p + 1][...], outs[4 * p + 2][...], outs[4 * p + 3][...] = _adamw_update(
                ins[p][...], gv, ins[5 + p][...], ins[10 + p][...])

    res = pl.pallas_call(
        body, name="adamw_small",
        out_shape=(_vm_sds((1, 1), F32), *[_vm_sds(w.shape, F32) for w in ws for _ in range(4)]),
    )(sv, *ws, *ms, *vs)
    return res[0], [tuple(res[1 + 4 * p:5 + 4 * p]) for p in range(5)]


def _rope_tables():
    pos = np.arange(SEQ, dtype=np.float32)
    inv = (np.float32(ROPE_THETA) ** (-np.arange(0, 64, 2, dtype=np.float32) / np.float32(64))).astype(np.float32)
    ang = np.tile(pos[:, None] * inv[None, :], (1, 4))
    cos, sin = np.cos(ang).astype(np.float32), np.sin(ang).astype(np.float32)
    low = (np.arange(128) % 64) < 32
    zero = np.float32(0.0)
    return jnp.asarray(cos), jnp.asarray(np.where(low, -sin, zero)), jnp.asarray(np.where(low, zero, sin))


def _local_step(x2, mem2, tgt2, winT, wout, wmem, b_in, sinks, g_branch, ln_gain, ln_bias):
    cos, sa, sb = _rope_tables()
    sinkv = jnp.pad(sinks, ((0, 0), (0, 120)))
    head_of_lane = np.arange(512)[:, None] // 64
    gather8 = jnp.asarray(head_of_lane == np.arange(128)[None, :], BF16)
    gather4 = jnp.asarray(head_of_lane[:W_B] == np.arange(128)[None, :], BF16)
    spread4 = jnp.asarray((head_of_lane[:W_B] == np.arange(128)[None, :]).T, BF16)

    xb, qa, ka, va, bn, b4, b16, qc, z, wout, wmem = _in_proj(x2, winT, b_in, cos, sa, sb, wout, wmem)
    memb, mkv = _mem_kv(mem2, wmem)
    b4f, b16f = b4.reshape(T, 768), b16.reshape(T, 768)

    swa = dict(kind="band", nb=SEQ // BLK, max_dist=BLK - 1, gqa=True)
    dil = (dict(kind="band", nb=SEQ // BLK), dict(kind="band", nb=SEQ // 4 // BLK), dict(kind="band", nb=1))
    (oa, lse_a), (o1, l1), (o4, l4), (o16, l16), (oc, lse_c) = _run_parts("attn_fwd", [
        _attn_fwd(qa, 0, W_A, ka, 0, va, 0, W_KV_A, sinks=sinks, **swa),
        _attn_fwd(bn, 0, W_B, bn, 1, bn, 2, W_B, **dil[0]),
        _attn_fwd(b4f, 0, W_B, b4f, 1, b4f, 2, W_B, **dil[1]),
        _attn_fwd(b16f, 0, W_B, b16f, 1, b16f, 2, W_B, **dil[2]),
        _attn_fwd(qc, 0, W_C, mkv, 0, mkv, 1, W_C, kind="mem")], "parallel", 48)

    s4 = lambda w: (B_LOC, 4, SEQ // 4, w)
    s16 = lambda w: (B_LOC, 16, SEQ // 16, w)
    (du, dz, doa, dla, dobn, lsen, dlbn, dob4, lse4, dlb4, dob16, lse16, dlb16, doc, dlc, acc, g_out) = _middle(
        oa, o1, l1, o4.reshape(s4(W_B)), l4.reshape(s4(128)), o16.reshape(s16(W_B)), l16.reshape(s16(128)), oc, z,
        x2, tgt2, g_branch, ln_gain, ln_bias, wout, spread4, gather4, gather8)

    flat = lambda a: a.reshape(T, a.shape[-1])
    (dqa, dka, dva, dsink), (dqc, g_mem) = _run_parts("attn_bwd_a", [
        _attn_bwd(qa, 0, W_A, ka, 0, va, 0, W_KV_A, doa, lse_a, dla, sinkv=sinkv, **swa),
        _attn_bwd(qc, 0, W_C, mkv, 0, mkv, 1, W_C, doc, lse_c, dlc, kind="mem", mem_in=memb)], "arbitrary", 48)
    last = T // QR - 1
    (r_out, r_mem), (dqn, dkn, dvn), (dq4, dk4, dv4), (dq16, dk16, dv16) = _run_parts("attn_bwd_b", [
        _ReduceScatter([(SH_OUT, D_MODEL), (SH_MEM, 2 * W_C)]).part((g_out, g_mem), (0, 1, 2, last, last)),
        _attn_bwd(bn, 0, W_B, bn, 1, bn, 2, W_B, dobn, lsen, dlbn, **dil[0]),
        _attn_bwd(b4f, 0, W_B, b4f, 1, b4f, 2, W_B, flat(dob4), flat(lse4), flat(dlb4), **dil[1]),
        _attn_bwd(b16f, 0, W_B, b16f, 1, b16f, 2, W_B, flat(dob16), flat(lse16), flat(dlb16), **dil[2])],
        "arbitrary", 62)

    r4 = lambda a: a.reshape(s4(W_B))
    r16 = lambda a: a.reshape(s16(W_B))
    gx, dbin, g_in = _dh_dx(dqa, dka, dva, dqn, dkn, dvn, r4(dq4), r4(dk4), r4(dv4), r16(dq16), r16(dk16),
                            r16(dv16), dqc, dz, du, xb, cos, sa, sb, winT)
    return gx, g_in, r_out, r_mem, acc, dbin, dsink


def kernel(x, mem, w_in, b_in, w_mem, attn_sinks, g_branch, w_out, ln_gain, ln_bias, loss_target, m_w_in, m_b_in, m_w_mem, m_attn_sinks, m_g_branch, m_w_out, m_ln_gain, m_ln_bias, v_w_in, v_b_in, v_w_mem, v_attn_sinks, v_g_branch, v_w_out, v_ln_gain, v_ln_bias):
    winT, wout, wmem = _gather_weights(w_in[0].T, w_out[0], w_mem[0])
    gx, g_in, r_out, r_mem, acc, dbin, dsink = _local_step(
        x.reshape(T, D_MODEL), mem.reshape(B_LOC * MEM_LEN, D_MODEL), loss_target.reshape(T, D_MODEL),
        winT, wout, wmem, b_in, attn_sinks, g_branch, ln_gain, ln_bias)
    r_in, sv = _reduce_grads(g_in, acc, dbin, dsink)

    small = ["b_in", "attn_sinks", "g_branch", "ln_gain", "ln_bias"]
    loss, steps = _adamw_small(sv, [b_in, attn_sinks, g_branch, ln_gain, ln_bias],
                               [m_b_in, m_attn_sinks, m_g_branch, m_ln_gain, m_ln_bias],
                               [v_b_in, v_attn_sinks, v_g_branch, v_ln_gain, v_ln_bias])
    out = dict(zip(small, steps))
    big = _adamw_big([(w_in[0].T, r_in, m_w_in[0].T, v_w_in[0].T), (w_out[0], r_out, m_w_out[0], v_w_out[0]),
                      (w_mem[0], r_mem, m_w_mem[0], v_w_mem[0])])
    out["w_in"] = tuple(a.T[None] for a in big[0])
    out["w_out"], out["w_mem"] = (tuple(a[None] for a in st) for st in big[1:])
    names = ["w_in", "b_in", "w_mem", "attn_sinks", "g_branch", "w_out", "ln_gain", "ln_bias"]
    return (loss.reshape(()), gx.reshape(B_LOC, SEQ, D_MODEL), *[out[n][k] for k in range(4) for n in names])
```

```python
import jax
import jax.numpy as jnp
import numpy as np
from jax import lax
from jax.experimental import pallas as pl
from jax.experimental.pallas import tpu as pltpu

F32, BF16 = jnp.float32, jnp.bfloat16

D_MODEL = 1024
SEQ = 2048
B_LOC = 2
T = B_LOC * SEQ
BLK = 128
MEM_LEN = 256
W_A, W_KV_A, W_B, W_C, D_MIX = 512, 128, 256, 256, 1024
D_IN = 2816
O_QA, O_KA, O_VA, O_QB, O_KB, O_VB, O_QC, O_Z = 0, 512, 640, 768, 1024, 1280, 1536, 1792
ROPE_THETA = 10000.0
LN_EPS = 1e-5
RMS_EPS = 1e-6
ALPHA = 2.0 ** 0.25
QK_SCALE = 0.125
N_CHIP = 4
SH_IN, SH_OUT, SH_MEM = D_IN // N_CHIP, D_MIX // N_CHIP, D_MODEL // N_CHIP
NEG = -1e30
ADAM_LR, ADAM_B1, ADAM_B2, ADAM_EPS, ADAM_WD, ADAM_STEP = 0.001, 0.9, 0.999, 1e-08, 0.01, 10
SV_W = 3072
MESH = pl.DeviceIdType.MESH

NN = ((1,), (0,))
NT = ((1,), (1,))
TN = ((0,), (0,))


def _dot(a, b, dims):
    return lax.dot_general(a, b, (dims, ((), ())), preferred_element_type=F32)


def _cp(sem=None, vmem_mb=None):
    kw = {}
    if sem is not None:
        kw["dimension_semantics"] = sem
    if vmem_mb is not None:
        kw["vmem_limit_bytes"] = vmem_mb * 1024 * 1024
    return pltpu.CompilerParams(**kw)


def _sds(shape, dtype):
    return pltpu.HBM(shape, dtype)


def _vm_sds(shape, dtype):
    return jax.ShapeDtypeStruct(shape, dtype)


def _pin(*args):
    return [pltpu.with_memory_space_constraint(a, pltpu.HBM) for a in args]


def _full(shape):
    n = len(shape)
    return pl.BlockSpec(shape, lambda *_: (0,) * n)


def _shard_rows(ref, n, chip, half):
    start = pl.multiple_of((2 * chip[0] + chip[1]) * n + half * (n // 2), 16)
    return ref.at[pl.ds(start, n // 2), :]


def _gather_weights(win_sh, wout_sh, wmem_sh):
    half, piece = SH_IN // 2, SH_IN // 4
    shards = ((SH_IN, D_MODEL), (SH_OUT, D_MODEL), (SH_MEM, 2 * W_C))

    def body(a_ref, b_ref, c_ref, oa_ref, ob_ref, oc_ref, raw_a, raw_b, raw_c, own_a, own_b, own_c,
             load_sem, store_sem, ici_send, ici_recv, d2d_send, d2d_recv):
        x, y, c = lax.axis_index("x"), lax.axis_index("y"), lax.axis_index("c")
        me, sibling = (x, y, c), (x, y, 1 - c)
        xn, yn, dg = (1 - x, y), (x, 1 - y), (1 - x, 1 - y)
        srcs, raws = (a_ref, b_ref, c_ref), (raw_a, raw_b, raw_c)
        owns, outs = (own_a, own_b, own_c), (oa_ref, ob_ref, oc_ref)
        loads = [pltpu.make_async_copy(srcs[a], raws[a], load_sem.at[a]) for a in range(3)]
        for cp in loads:
            cp.start()

        def rows(chip, hf, q):
            start = pl.multiple_of((2 * chip[0] + chip[1]) * SH_IN + hf * half + q * piece, 16)
            return oa_ref.at[pl.ds(start, piece), :]

        def copy(sems, k, chip, hf, q, to, src=None):
            blk = rows(chip, hf, q)
            return pltpu.make_async_remote_copy(
                src_ref=blk if src is None else src, dst_ref=blk, send_sem=sems[0].at[k], recv_sem=sems[1].at[k],
                device_id=to, device_id_type=MESH)

        def my_piece(q):
            return own_a.at[pl.ds(pl.multiple_of(c * half + q * piece, 16), piece), :]

        ici, d2d = (ici_send, ici_recv), (d2d_send, d2d_recv)
        stores, direct = [], []
        for a, (n, _) in enumerate(shards):
            loads[a].wait()
            owns[a][...] = raws[a][...].astype(BF16)
            mine = pl.ds(pl.multiple_of((2 * x + y) * n, 16), n)
            stores.append(pltpu.make_async_copy(owns[a], outs[a].at[mine, :], store_sem.at[a]))
            stores[-1].start()
            if a == 0:
                direct = [copy(ici, 0, (x, y), c, 0, (*xn, c), my_piece(0)),
                          copy(ici, 1, (x, y), c, 1, (*xn, c), my_piece(1)),
                          copy(ici, 3, (x, y), c, 0, (*yn, c), my_piece(0)),
                          copy(ici, 4, (x, y), c, 1, (*yn, c), my_piece(1))]
                for cp in direct:
                    cp.start()
        arrivals = [(0, xn, 0), (1, xn, 1), (3, yn, 0), (4, yn, 1), (2, dg, 1), (5, dg, 0)]
        passed = []
        for k, chip, q in arrivals:
            copy(ici, k, chip, c, q, me).wait_recv()
            if k == 0:
                passed.append(copy(ici, 5, xn, c, 0, (*yn, c)))
                passed[-1].start()
            if k == 4:
                passed.append(copy(ici, 2, yn, c, 1, (*xn, c)))
                passed[-1].start()
            passed.append(copy(d2d, k, chip, c, q, sibling))
            passed[-1].start()
        for k, chip, q in arrivals:
            copy(d2d, k, chip, 1 - c, q, me).wait_recv()
        for cp in direct + passed:
            cp.wait_send()
        for cp in stores:
            cp.wait()

    hbm = pl.BlockSpec(memory_space=pl.ANY)
    return pl.pallas_call(
        body, name="gather_weights",
        out_shape=(_sds((D_IN, D_MODEL), BF16), _sds((D_MIX, D_MODEL), BF16), _sds((D_MODEL, 2 * W_C), BF16)),
        in_specs=[hbm, hbm, hbm], out_specs=(hbm, hbm, hbm),
        scratch_shapes=([pltpu.VMEM(sh, F32) for sh in shards] + [pltpu.VMEM(sh, BF16) for sh in shards]
                        + [pltpu.SemaphoreType.DMA((3,))] * 2 + [pltpu.SemaphoreType.DMA((6,))] * 4),
        compiler_params=_cp(vmem_mb=40),
    )(*_pin(win_sh, wout_sh, wmem_sh))


def _rope(t, cos, sa, sb, sign):
    w = t.shape[1]
    reps = w // 128
    c, a, b = (jnp.tile(v, (1, reps)) if reps > 1 else v for v in (cos, sa, sb))
    rot = pltpu.roll(t, w - 32, 1) * a + pltpu.roll(t, 32, 1) * b
    return t * c + rot if sign > 0 else t * c - rot


def _in_proj(x, winT, b_in, cos, sa, sb, wout_own, wmem_own, mem):
    tm = 512
    spt = SEQ // tm
    n_steps = T // tm
    forward_step = n_steps // 2
    n_mem = B_LOC * MEM_LEN

    def body(x_ref, w_ref, b_ref, cos_ref, sa_ref, sb_ref, wo_in, wm_in, mem_ref,
             xb_ref, qa_ref, ka_ref, va_ref, bn_ref, b4_ref, b16_ref, qc_ref, z_ref, wo_ref, wm_ref, memb_ref, mkv_ref,
             scr, wm_vmem, ici_send, ici_recv, d2d_send, d2d_recv):
        i = pl.program_id(0)
        mx, my, mc = lax.axis_index("x"), lax.axis_index("y"), lax.axis_index("c")
        chips = [(1 - mx, my), (mx, 1 - my), (1 - mx, 1 - my)]
        full = ((wo_ref, SH_OUT), (wm_ref, SH_MEM))

        def copy(sems, a, j, chip_of_block, half, to):
            blk = _shard_rows(full[a][0], full[a][1], chip_of_block, half)
            return pltpu.make_async_remote_copy(
                src_ref=blk, dst_ref=blk, send_sem=sems[0].at[a, j], recv_sem=sems[1].at[a, j],
                device_id=to, device_id_type=MESH)

        ici, d2d = (ici_send, ici_recv), (d2d_send, d2d_recv)
        pairs = [(a, j, chip) for j, chip in enumerate(chips) for a in range(2)]

        @pl.when(i == 0)
        def _():
            for a, j, chip in pairs:
                copy(ici, a, j, (mx, my), mc, (*chip, mc)).start()

        @pl.when(i == forward_step)
        def _():
            for a, j, chip in pairs:
                copy(ici, a, j, chip, mc, (mx, my, mc)).wait_recv()
                copy(d2d, a, j, chip, mc, (mx, my, 1 - mc)).start()

        @pl.when(i == n_steps - 1)
        def _():
            for a, j, chip in pairs:
                copy(d2d, a, j, chip, 1 - mc, (mx, my, mc)).wait_recv()
            for a, j, chip in pairs:
                copy(ici, a, j, (mx, my), mc, (*chip, mc)).wait_send()
                copy(d2d, a, j, chip, mc, (mx, my, 1 - mc)).wait_send()
            pltpu.sync_copy(wm_ref, wm_vmem)
            mb = mem_ref[...].astype(BF16)
            memb_ref[...] = mb
            mkv_ref[...] = _dot(mb, wm_vmem[...], NN).astype(BF16)

        xb = x_ref[...].astype(BF16)
        xb_ref[...] = xb
        cos_t, sa_t, sb_t = cos_ref[...], sa_ref[...], sb_ref[...]

        def proj(r0, n):
            return _dot(xb, w_ref[r0:r0 + n, :], NT) + b_ref[:, r0:r0 + n]

        def rope(t):
            return _rope(t, cos_t, sa_t, sb_t, +1)

        qa_ref[...] = (rope(proj(O_QA, W_A)) * QK_SCALE).astype(BF16)
        ka_ref[...] = rope(proj(O_KA, W_KV_A)).astype(BF16)
        va_ref[...] = proj(O_VA, W_KV_A).astype(BF16)
        qc_ref[...] = (proj(O_QC, W_C) * QK_SCALE).astype(BF16)
        z_ref[...] = proj(O_Z, D_MIX).astype(BF16)
        parts = (rope(proj(O_QB, W_B)) * QK_SCALE, rope(proj(O_KB, W_B)), proj(O_VB, W_B))
        for k, part in enumerate(parts):
            bn_ref[:, 256 * k:256 * (k + 1)] = part.astype(BF16)
            scr[2 * k] = part[:, :128]
            scr[2 * k + 1] = part[:, 128:]
        for j in range(6):
            for res in range(4):
                b4_ref[0, res, :, 128 * j:128 * (j + 1)] = scr[j, pl.ds(res, tm // 4, stride=4), :].astype(BF16)
            for res in range(16):
                b16_ref[0, res, :, 128 * j:128 * (j + 1)] = scr[j, pl.ds(res, tm // 16, stride=16), :].astype(BF16)

    tok = lambda w: pl.BlockSpec((tm, w), lambda i: (i, 0))
    tab = pl.BlockSpec((tm, 128), lambda i: (i % spt, 0))
    hbm = pl.BlockSpec(memory_space=pl.ANY)
    once = lambda shape: pl.BlockSpec(shape, lambda i: (0, 0), pipeline_mode=pl.Buffered(1))
    return pl.pallas_call(
        body, name="in_proj", grid=(n_steps,),
        in_specs=[tok(D_MODEL), _full((D_IN, D_MODEL)), _full((1, D_IN)), tab, tab, tab, hbm, hbm,
                  once((n_mem, D_MODEL))],
        out_specs=(tok(D_MODEL), tok(W_A), tok(W_KV_A), tok(W_KV_A), tok(768),
                   pl.BlockSpec((1, 4, tm // 4, 768), lambda i: (i // spt, 0, i % spt, 0)),
                   pl.BlockSpec((1, 16, tm // 16, 768), lambda i: (i // spt, 0, i % spt, 0)),
                   tok(W_C), tok(D_MIX), hbm, hbm, once((n_mem, D_MODEL)), once((n_mem, 2 * W_C))),
        out_shape=(_sds((T, D_MODEL), BF16), _sds((T, W_A), BF16), _sds((T, W_KV_A), BF16), _sds((T, W_KV_A), BF16),
                   _sds((T, 768), BF16), _sds((B_LOC, 4, SEQ // 4, 768), BF16), _sds((B_LOC, 16, SEQ // 16, 768), BF16),
                   _sds((T, W_C), BF16), _sds((T, D_MIX), BF16),
                   _sds((D_MIX, D_MODEL), BF16), _sds((D_MODEL, 2 * W_C), BF16),
                   _sds((n_mem, D_MODEL), BF16), _sds((n_mem, 2 * W_C), BF16)),
        input_output_aliases={6: 9, 7: 10},
        scratch_shapes=([pltpu.VMEM((6, tm, 128), F32), pltpu.VMEM((D_MODEL, 2 * W_C), BF16)]
                        + [pltpu.SemaphoreType.DMA((2, 3))] * 4),
        compiler_params=_cp(("arbitrary",), vmem_mb=52),
    )(*_pin(x, winT, b_in, cos, sa, sb, wout_own, wmem_own, mem))


class _Part:
    def __init__(self, body, args, in_specs, out_specs, out_shape, scratch=()):
        self.body, self.args, self.in_specs, self.out_specs, self.out_shape = body, args, in_specs, out_specs, out_shape
        self.scratch = list(scratch)


def _run_parts(name, parts, semantics, vmem_mb):
    n_in = [len(p.args) for p in parts]
    n_out = [len(p.out_shape) for p in parts]
    n_scr = [len(p.scratch) for p in parts]

    def body(*refs):
        ins, outs, scr = refs[:sum(n_in)], refs[sum(n_in):sum(n_in) + sum(n_out)], refs[sum(n_in) + sum(n_out):]
        i0 = o0 = s0 = 0
        for p, ni, no, ns in zip(parts, n_in, n_out, n_scr):
            p.body(*ins[i0:i0 + ni], *outs[o0:o0 + no], *scr[s0:s0 + ns])
            i0, o0, s0 = i0 + ni, o0 + no, s0 + ns

    res = pl.pallas_call(
        body, name=name, grid=(T // QR,),
        in_specs=[sp for p in parts for sp in p.in_specs], out_specs=tuple(sp for p in parts for sp in p.out_specs),
        out_shape=tuple(sh for p in parts for sh in p.out_shape),
        scratch_shapes=[sc for p in parts for sc in p.scratch],
        compiler_params=_cp((semantics,), vmem_mb=vmem_mb),
    )(*_pin(*[a for p in parts for a in p.args]))
    out, o0 = [], 0
    for no in n_out:
        out.append(tuple(res[o0:o0 + no]))
        o0 += no
    return out


QB = 8
QR = QB * BLK


def _lane_lo():
    return lax.broadcasted_iota(jnp.int32, (1, 128), 1) < 64


def _dup_head(k2, hk, lo):
    kf = k2.astype(F32)
    r = pltpu.roll(kf, 64, 1)
    return (jnp.where(lo, kf, r) if hk == 0 else jnp.where(lo, r, kf)).astype(BF16)


def _stack_heads(pairs, lo):
    parts = []
    for x2 in pairs:
        z = jnp.zeros_like(x2)
        parts += [jnp.where(lo, x2, z), jnp.where(lo, z, x2)]
    return jnp.concatenate(parts, axis=0)


def _prev_mode(kind, nb, j):
    if kind == "mem" or nb == 1:
        return "no"
    if nb <= QB:
        return "yes" if j % nb else "no"
    return "yes" if j else "dyn"


class _Attn:
    def __init__(self, kind, nb, max_dist, gqa, qw, kvw, qcb, kcb, vcb):
        self.kind, self.nb, self.gqa, self.qw, self.kvw = kind, nb, gqa, qw, kvw
        npairs = qw // 128
        self.groups = ([(hk, [2 * hk, 2 * hk + 1]) for hk in range(npairs // 2)] if gqa
                       else [(p, [p]) for p in range(npairs)])
        self.nh = 2 * len(self.groups[0][1])
        self.cols = 128 * self.nh
        self.reach = BLK - max_dist
        self.ext_prev = kind == "band" and nb > QB
        self.q_spec = pl.BlockSpec((QR, qw), lambda g: (g, qcb))
        self.row_spec = pl.BlockSpec((QR, qw), lambda g: (g, 0))
        self.stat_spec = pl.BlockSpec((QR, 128), lambda g: (g, 0))
        if kind == "mem":
            per = SEQ // QR
            self.kv_specs = [pl.BlockSpec((MEM_LEN, kvw), lambda g: (g // per, kcb)),
                             pl.BlockSpec((MEM_LEN, kvw), lambda g: (g // per, vcb))]
        else:
            self.kv_specs = [pl.BlockSpec((QR, kvw), lambda g: (g, kcb)), pl.BlockSpec((QR, kvw), lambda g: (g, vcb))]
            if self.ext_prev:
                self.kv_specs += [pl.BlockSpec((BLK, kvw), lambda g: (jnp.maximum(g * QB - 1, 0), kcb)),
                                  pl.BlockSpec((BLK, kvw), lambda g: (jnp.maximum(g * QB - 1, 0), vcb))]

    def masks(self):
        if self.kind == "mem":
            return None
        kj = lax.broadcasted_iota(jnp.int32, (2 * BLK, self.cols), 0)
        qi = lax.broadcasted_iota(jnp.int32, (2 * BLK, self.cols), 1) & (BLK - 1)
        kj1 = lax.broadcasted_iota(jnp.int32, (BLK, self.cols), 0)
        qi1 = lax.broadcasted_iota(jnp.int32, (BLK, self.cols), 1) & (BLK - 1)
        return kj, qi, kj1 <= qi1

    def keys(self, j, gi, kc_ref, vc_ref, kp_ref, vp_ref, lo, kq, g):
        def kv(k_ref, v_ref, r):
            if self.gqa:
                return _dup_head(k_ref[r, :], gi, lo), _dup_head(v_ref[r, :], gi, lo)
            sl = slice(128 * gi, 128 * (gi + 1))
            return k_ref[r, sl], v_ref[r, sl]

        if self.kind == "mem":
            key0 = pl.multiple_of((g // (SEQ // QR)) * MEM_LEN, MEM_LEN)
            return (*kv(kc_ref, vc_ref, slice(None)), None, [(0, MEM_LEN, key0)])
        kj, qi, cur = kq
        row0 = g * QR + BLK * j
        mode = _prev_mode(self.kind, self.nb, j)
        if mode == "no":
            return (*kv(kc_ref, vc_ref, slice(BLK * j, BLK * (j + 1))), cur, [(0, BLK, pl.multiple_of(row0, BLK))])
        if mode == "yes":
            mask = jnp.logical_and(kj >= qi + self.reach, kj <= qi + BLK)
            return (*kv(kc_ref, vc_ref, slice(BLK * (j - 1), BLK * (j + 1))), mask,
                    [(0, 2 * BLK, pl.multiple_of(row0 - BLK, BLK))])
        has_prev = ((g * QB) % self.nb) > 0
        hp = has_prev.astype(jnp.int32)
        mask = jnp.logical_and(kj >= qi * hp + (self.reach * hp + BLK * (1 - hp)), kj <= qi + BLK)
        kp, vp = kv(kp_ref, vp_ref, slice(None))
        kc, vc = kv(kc_ref, vc_ref, slice(0, BLK))
        return (jnp.concatenate([kp, kc], axis=0), jnp.concatenate([vp, vc], axis=0), mask,
                [(0, BLK, pl.multiple_of(jnp.maximum(row0 - BLK, 0), BLK)), (BLK, BLK, pl.multiple_of(row0, BLK))])


def _attn_fwd(q, qcb, qw, k, kcb, v, vcb, kvw, *, kind, nb=1, max_dist=BLK, gqa=False, sinks=None):
    a = _Attn(kind, nb, max_dist, gqa, qw, kvw, qcb, kcb, vcb)

    def body(*refs):
        it = iter(refs)
        q_ref, kc_ref, vc_ref = next(it), next(it), next(it)
        kp_ref, vp_ref = (next(it), next(it)) if a.ext_prev else (None, None)
        sink_ref = next(it) if sinks is not None else None
        o_ref, lse_ref = next(it), next(it)
        g = pl.program_id(0)
        lo = _lane_lo()
        top = lax.broadcasted_iota(jnp.int32, (128, 1), 0) < 64
        rid = lax.broadcasted_iota(jnp.int32, (8, 128), 0)
        kq = a.masks()
        stats = {}

        def scores(j, gi, pairs):
            rows = slice(BLK * j, BLK * (j + 1))
            qs = _stack_heads([q_ref[rows, 128 * p:128 * (p + 1)] for p in pairs], lo)
            kk, vv, mask, _ = a.keys(j, gi, kc_ref, vc_ref, kp_ref, vp_ref, lo, kq, g)
            pieces = [slice(r0, r0 + BLK) for r0 in range(0, kk.shape[0], BLK)]
            return dict(j=j, gi=gi, pairs=pairs, rows=rows, vv=vv, mask=mask, pieces=pieces,
                        ss=[_dot(kk[r], qs, NT) for r in pieces])

        def softmax(c):
            gi, mask = c["gi"], c["mask"]
            ss = [s if mask is None else jnp.where(mask[r], s, NEG) for r, s in zip(c["pieces"], c.pop("ss"))]
            m = jnp.max(ss[0], axis=0, keepdims=True)
            for s in ss[1:]:
                m = jnp.maximum(m, jnp.max(s, axis=0, keepdims=True))
            if sink_ref is not None:
                sk = jnp.concatenate([jnp.full((1, 128), sink_ref[0, a.nh * gi + i], F32) for i in range(a.nh)], axis=1)
                m = jnp.maximum(m, sk)
            ps = [jnp.exp(s - m) for s in ss]
            l = sum(jnp.sum(p, axis=0, keepdims=True) for p in ps)
            if sink_ref is not None:
                l = l + jnp.exp(sk - m)
            c["ps"] = [p.astype(BF16) for p in ps]
            c["l"], c["lse"] = l, m + jnp.log(l)

        def outputs(c):
            j, gi, rows = c["j"], c["gi"], c["rows"]
            ot = sum(_dot(c["vv"][r], p, TN) for r, p in zip(c["pieces"], c["ps"]))
            ot = ot * pl.reciprocal(c["l"], approx=True)
            for i, p in enumerate(c["pairs"]):
                o2t = jnp.where(top, ot[:, 256 * i:256 * i + 128], ot[:, 256 * i + 128:256 * i + 256])
                o_ref[rows, 128 * p:128 * (p + 1)] = o2t.T.astype(BF16)
            stat = stats.get(j, jnp.zeros((8, 128), F32))
            for i in range(a.nh):
                stat = jnp.where(rid == a.nh * gi + i, c["lse"][:, 128 * i:128 * (i + 1)], stat)
            stats[j] = stat
            if gi == a.groups[-1][0]:
                lse_ref[rows, :] = jnp.concatenate([stats.pop(j), jnp.zeros((120, 128), F32)], axis=0).T

        chains = [(j, gi, pairs) for j in range(QB) for gi, pairs in a.groups]
        live = {}
        for t in range(len(chains) + 2):
            if t < len(chains):
                live[t] = scores(*chains[t])
            if 0 <= t - 1 < len(chains):
                softmax(live[t - 1])
            if 0 <= t - 2 < len(chains):
                outputs(live.pop(t - 2))


    args = [q, k, v] + ([k, v] if a.ext_prev else [])
    in_specs = [a.q_spec] + a.kv_specs
    if sinks is not None:
        args.append(sinks)
        in_specs.append(pl.BlockSpec(memory_space=pltpu.SMEM))
    return _Part(body, args, in_specs, [a.row_spec, a.stat_spec], [_sds((T, qw), BF16), _sds((T, 128), F32)])


def _attn_bwd(q, qcb, qw, k, kcb, v, vcb, kvw, do, lse, dl, *, kind, nb=1, max_dist=BLK, gqa=False, sinkv=None,
              mem_in=None):
    a = _Attn(kind, nb, max_dist, gqa, qw, kvw, qcb, kcb, vcb)

    def body(*refs):
        it = iter(refs)
        q_ref, kc_ref, vc_ref = next(it), next(it), next(it)
        kp_ref, vp_ref = (next(it), next(it)) if a.ext_prev else (None, None)
        do_ref, lse_ref, dl_ref = next(it), next(it), next(it)
        sinkv_ref = next(it) if sinkv is not None else None
        mem_ref = next(it) if kind == "mem" else None
        dq_ref = next(it)
        if kind == "mem":
            gmem_ref = next(it)
        else:
            dk_out, dv_out = next(it), next(it)
        dsink_ref = next(it) if sinkv is not None else None
        if kind != "mem":
            dk_ref, dv_ref, stage_k, stage_v, flush_sem = next(it), next(it), next(it), next(it), next(it)
        else:
            dkv_ref = next(it)
        g = pl.program_id(0)
        lo = _lane_lo()
        top = lax.broadcasted_iota(jnp.int32, (128, 1), 0) < 64

        @pl.when(g == 0)
        def _():
            if kind == "mem":
                dkv_ref[...] = jnp.zeros_like(dkv_ref)
            else:
                dk_ref[...] = jnp.zeros_like(dk_ref)
                dv_ref[...] = jnp.zeros_like(dv_ref)
            if dsink_ref is not None:
                dsink_ref[...] = jnp.zeros_like(dsink_ref)

        kq = a.masks()
        stats_t = {}

        def first_matmuls(j, gi, pairs):
            rows = slice(BLK * j, BLK * (j + 1))
            if j not in stats_t:
                stats_t[j] = (lse_ref[rows, :].T, dl_ref[rows, :].T)
            lse_t, dl_t = stats_t[j]
            heads = [a.nh * gi + i for i in range(a.nh)]
            c = dict(rows=rows, gi=gi, pairs=pairs)
            c["qs"] = _stack_heads([q_ref[rows, 128 * p:128 * (p + 1)] for p in pairs], lo)
            c["dos"] = _stack_heads([do_ref[rows, 128 * p:128 * (p + 1)] for p in pairs], lo)
            c["lse_row"] = jnp.concatenate([lse_t[h:h + 1, :] for h in heads], axis=1)
            c["dl_row"] = jnp.concatenate([dl_t[h:h + 1, :] for h in heads], axis=1)
            c["kk"], vv, c["mask"], c["dests"] = a.keys(j, gi, kc_ref, vc_ref, kp_ref, vp_ref, lo, kq, g)
            c["s"] = _dot(c["kk"], c["qs"], NT)
            c["dp"] = _dot(vv, c["dos"], NT)
            return c

        def elementwise(c):
            s = c.pop("s")
            if c["mask"] is not None:
                s = jnp.where(c["mask"], s, NEG)
            p = jnp.exp(s - c["lse_row"])
            c["ds"] = (p * (c.pop("dp") - c["dl_row"])).astype(BF16)
            c["p"] = p.astype(BF16)

        def last_matmuls(c):
            gi, rows = c["gi"], c["rows"]
            dqt = _dot(c["kk"], c["ds"], TN)
            ck = _dot(c["ds"], c["qs"], NN)
            cv = _dot(c["p"], c["dos"], NN)
            if gqa:
                sel = lo if gi == 0 else jnp.logical_not(lo)
                ck = jnp.where(sel, ck + pltpu.roll(ck, 64, 1), 0.0)
                cv = jnp.where(sel, cv + pltpu.roll(cv, 64, 1), 0.0)
                kcols = slice(0, 128)
            else:
                kcols = slice(128 * gi, 128 * (gi + 1))
            for r0, nr, key0 in c["dests"]:
                krows = pl.ds(key0, nr)
                if kind == "mem":
                    dkv_ref[krows, kcols] += ck[r0:r0 + nr]
                    dkv_ref[krows, slice(kvw + kcols.start, kvw + kcols.stop)] += cv[r0:r0 + nr]
                else:
                    dk_ref[krows, kcols] += ck[r0:r0 + nr]
                    dv_ref[krows, kcols] += cv[r0:r0 + nr]
            for i, p in enumerate(c["pairs"]):
                dq2t = jnp.where(top, dqt[:, 256 * i:256 * i + 128], dqt[:, 256 * i + 128:256 * i + 256])
                dq_ref[rows, 128 * p:128 * (p + 1)] = dq2t.T.astype(BF16)

        chains = [(j, gi, pairs) for j in range(QB) for gi, pairs in a.groups]
        live = {}
        for t in range(len(chains) + 2):
            if t < len(chains):
                live[t] = first_matmuls(*chains[t])
            if 0 <= t - 1 < len(chains):
                elementwise(live[t - 1])
            if 0 <= t - 2 < len(chains):
                last_matmuls(live.pop(t - 2))
        if dsink_ref is not None:
            ps = jnp.exp(sinkv_ref[...] - lse_ref[...]) * dl_ref[...]
            dsink_ref[...] += jnp.sum(ps, axis=0, keepdims=True)
        if kind == "mem":
            @pl.when(g == T // QR - 1)
            def _():
                gmem_ref[...] = _dot(mem_ref[...], dkv_ref[...].astype(BF16), TN)
        else:
            n_steps = T // QR

            def flush(step):
                rows = pl.ds(pl.multiple_of(step * QR, QR), QR)
                out = []
                for acc, stage, dst, i in ((dk_ref, stage_k, dk_out, 0), (dv_ref, stage_v, dv_out, 1)):
                    stage[...] = acc[rows, :].astype(BF16)
                    out.append(pltpu.make_async_copy(stage, dst.at[rows, :], flush_sem.at[i]))
                return out

            def flushed(step):
                rows = pl.ds(pl.multiple_of(step * QR, QR), QR)
                return [pltpu.make_async_copy(stage, dst.at[rows, :], flush_sem.at[i])
                        for stage, dst, i in ((stage_k, dk_out, 0), (stage_v, dv_out, 1))]

            @pl.when(g >= 2)
            def _():
                for cp in flushed(g - 2):
                    cp.wait()

            @pl.when(g >= 1)
            def _():
                for cp in flush(g - 1):
                    cp.start()

            @pl.when(g == n_steps - 1)
            def _():
                for cp in flushed(g - 1):
                    cp.wait()
                for cp in flush(g):
                    cp.start()
                for cp in flushed(g):
                    cp.wait()

    args = [q, k, v] + ([k, v] if a.ext_prev else []) + [do, lse, dl]
    in_specs = [a.q_spec] + a.kv_specs + [a.row_spec, a.stat_spec, a.stat_spec]
    if sinkv is not None:
        args.append(sinkv)
        in_specs.append(_full((1, 128)))
    if kind == "mem":
        args.append(mem_in)
        in_specs.append(pl.BlockSpec(mem_in.shape, lambda g: (0, 0), pipeline_mode=pl.Buffered(1)))
    out_shape = [_sds((T, qw), BF16)]
    out_specs = [a.row_spec]
    scratch = []
    if kind == "mem":
        out_shape.append(_sds((D_MODEL, 2 * kvw), F32))
        out_specs.append(pl.BlockSpec((D_MODEL, 2 * kvw), lambda g: (0, 0), pipeline_mode=pl.Buffered(1)))
        scratch = [pltpu.VMEM((B_LOC * MEM_LEN, 2 * kvw), F32)]
    else:
        out_shape += [_sds((T, kvw), BF16)] * 2
        out_specs += [pl.BlockSpec(memory_space=pl.ANY)] * 2
        scratch = [pltpu.VMEM((T, kvw), F32)] * 2 + [pltpu.VMEM((QR, kvw), BF16)] * 2 + [pltpu.SemaphoreType.DMA((2,))]
    if sinkv is not None:
        out_shape.append(_sds((1, 128), F32))
        out_specs.append(_full((1, 128)))
    return _Part(body, args, in_specs, out_specs, out_shape, scratch)


def _dot2(v, w_ref):
    hi = v.astype(BF16)
    lo = (v - hi.astype(F32)).astype(BF16)
    return _dot(hi, w_ref[...], NN) + _dot(lo, w_ref[...], NN)


def _middle(oa, o1, l1, o4, l4, o16, l16, oc, z, x, tgt, g_br, ln_g, ln_b, wout, spread4, gather4, gather8):
    tm = 512
    spt = SEQ // tm

    def body(oa_ref, o1_ref, l1_ref, o4_ref, l4_ref, o16_ref, l16_ref, oc_ref, z_ref, x_ref, t_ref,
             g_ref, lg_ref, lb_ref, w_ref, sp4_ref, ga4_ref, ga8_ref,
             du_ref, dz_ref, doa_ref, dla_ref,
             dobn_ref, lsen_ref, dlbn_ref, dob4_ref, lse4_ref, dlb4_ref, dob16_ref, lse16_ref, dlb16_ref,
             doc_ref, dlc_ref, acc_ref, gout_ref, scr):
        i = pl.program_id(0)

        @pl.when(i == 0)
        def _():
            acc_ref[...] = jnp.zeros_like(acc_ref)
            gout_ref[...] = jnp.zeros_like(gout_ref)

        for res in range(4):
            rows = pl.ds(res, tm // 4, stride=4)
            for j in range(2):
                scr[j, rows, :] = o4_ref[0, res, :, 128 * j:128 * (j + 1)].astype(F32)
            scr[2, rows, :] = l4_ref[0, res]
        for res in range(16):
            rows = pl.ds(res, tm // 16, stride=16)
            for j in range(2):
                scr[3 + j, rows, :] = o16_ref[0, res, :, 128 * j:128 * (j + 1)].astype(F32)
            scr[5, rows, :] = l16_ref[0, res]
        inv_d = 1.0 / D_MODEL
        gb, lg, lb = g_ref[...], lg_ref[...], lb_ref[...]

        def rms(o):
            r = lax.rsqrt(jnp.sum(o * o, axis=1, keepdims=True) * (1.0 / o.shape[1]) + RMS_EPS)
            return o * r, r

        def rms_bwd(dn_, n_, r):
            return r * (dn_ - n_ * (jnp.sum(dn_ * n_, axis=1, keepdims=True) * (1.0 / n_.shape[1])))

        def forward(rs):
            o4v = jnp.concatenate([scr[0, rs, :], scr[1, rs, :]], axis=1)
            o16v = jnp.concatenate([scr[3, rs, :], scr[4, rs, :]], axis=1)
            l1v, l4v, l16v = l1_ref[rs, :], scr[2, rs, :], scr[5, rs, :]
            mx = jnp.maximum(jnp.maximum(l1v, l4v), l16v)
            e1, e4, e16 = jnp.exp(l1v - mx), jnp.exp(l4v - mx), jnp.exp(l16v - mx)
            ssum = e1 + e4 + e16
            inv = 1.0 / ssum
            c = dict(rs=rs, lse_b=mx + jnp.log(ssum))
            c["ob"] = (_dot2(e1 * inv, sp4_ref) * o1_ref[rs, :].astype(F32) + _dot2(e4 * inv, sp4_ref) * o4v
                       + _dot2(e16 * inv, sp4_ref) * o16v)
            c["oa"], c["oc"] = oa_ref[rs, :].astype(F32), oc_ref[rs, :].astype(F32)
            na, c["ra"] = rms(c["oa"])
            nb_, c["rb"] = rms(c["ob"])
            nc, c["rc"] = rms(c["oc"])
            c["n"] = jnp.concatenate([na, nb_, nc], axis=1)
            c["zf"] = z_ref[rs, :].astype(F32)
            c["sig"] = 1.0 / (1.0 + jnp.exp(-c["zf"]))
            c["sz"] = c["zf"] * c["sig"]
            c["yb"] = (c["n"] * gb * c["sz"]).astype(BF16)
            c["y2"] = _dot(c["yb"], w_ref[...], NN)
            return c

        def norm(c):
            rs = c["rs"]
            u = ALPHA * x_ref[rs, :] + c.pop("y2")
            mu = jnp.sum(u, axis=1, keepdims=True) * inv_d
            uc = u - mu
            rstd = lax.rsqrt(jnp.sum(uc * uc, axis=1, keepdims=True) * inv_d + LN_EPS)
            xh = uc * rstd
            diff = xh * lg + lb - t_ref[rs, :]
            acc_ref[0:1, :] += jnp.sum(diff * diff, axis=0, keepdims=True) * (0.5 * inv_d)
            dout = diff * inv_d
            acc_ref[2:3, :] += jnp.sum(dout * xh, axis=0, keepdims=True)
            acc_ref[3:4, :] += jnp.sum(dout, axis=0, keepdims=True)
            dxh = dout * lg
            du = rstd * (dxh - jnp.sum(dxh, axis=1, keepdims=True) * inv_d
                         - xh * (jnp.sum(dxh * xh, axis=1, keepdims=True) * inv_d))
            dub = du.astype(BF16)
            du_ref[rs, :] = dub
            c["dy"] = _dot(dub, w_ref[...], NT)
            gout_ref[...] += _dot(c.pop("yb"), dub, TN)

        def backward(c):
            rs, n, dy, zf, sig = c["rs"], c["n"], c["dy"], c["zf"], c["sig"]
            t1 = dy * c["sz"]
            acc_ref[1:2, :] += jnp.sum(t1 * n, axis=0, keepdims=True)
            dn = t1 * gb
            dz_ref[rs, :] = (dy * n * gb * (sig * (1.0 + zf * (1.0 - sig)))).astype(BF16)
            doa = rms_bwd(dn[:, :W_A], n[:, :W_A], c["ra"])
            dob = rms_bwd(dn[:, W_A:W_A + W_B], n[:, W_A:W_A + W_B], c["rb"])
            doc = rms_bwd(dn[:, W_A + W_B:], n[:, W_A + W_B:], c["rc"])
            doa_ref[rs, :] = doa.astype(BF16)
            dla_ref[rs, :] = _dot2(doa * c["oa"], ga8_ref)
            doc_ref[rs, :] = doc.astype(BF16)
            dlc_ref[rs, :] = _dot2(doc * c["oc"], ga4_ref)
            dobn_ref[rs, :] = dob.astype(BF16)
            lsen_ref[rs, :] = c["lse_b"]
            dlbn_ref[rs, :] = _dot2(dob * c["ob"], ga4_ref)
            scr[0, rs, :] = dob[:, :128]
            scr[1, rs, :] = dob[:, 128:]

        halves = [slice(h * (tm // 2), (h + 1) * (tm // 2)) for h in range(2)]
        live = {}
        for t in range(len(halves) + 2):
            if t < len(halves):
                live[t] = forward(halves[t])
            if 0 <= t - 1 < len(halves):
                norm(live[t - 1])
            if 0 <= t - 2 < len(halves):
                backward(live.pop(t - 2))
        for j in range(2):
            sl = slice(128 * j, 128 * (j + 1))
            for res in range(4):
                dob4_ref[0, res, :, sl] = scr[j, pl.ds(res, tm // 4, stride=4), :].astype(BF16)
            for res in range(16):
                dob16_ref[0, res, :, sl] = scr[j, pl.ds(res, tm // 16, stride=16), :].astype(BF16)
        for res in range(4):
            rows = pl.ds(res, tm // 4, stride=4)
            lse4_ref[0, res] = lsen_ref[rows, :]
            dlb4_ref[0, res] = dlbn_ref[rows, :]
        for res in range(16):
            rows = pl.ds(res, tm // 16, stride=16)
            lse16_ref[0, res] = lsen_ref[rows, :]
            dlb16_ref[0, res] = dlbn_ref[rows, :]


    tok = lambda w: pl.BlockSpec((tm, w), lambda i: (i, 0))
    p4 = lambda w: pl.BlockSpec((1, 4, tm // 4, w), lambda i: (i // spt, 0, i % spt, 0))
    p16 = lambda w: pl.BlockSpec((1, 16, tm // 16, w), lambda i: (i // spt, 0, i % spt, 0))
    s4 = lambda w, dt: _sds((B_LOC, 4, SEQ // 4, w), dt)
    s16 = lambda w, dt: _sds((B_LOC, 16, SEQ // 16, w), dt)
    row = _full((1, D_MODEL))
    return pl.pallas_call(
        body, name="middle", grid=(T // tm,),
        in_specs=[tok(W_A), tok(W_B), tok(128), p4(W_B), p4(128), p16(W_B), p16(128), tok(W_C), tok(D_MIX),
                  tok(D_MODEL), tok(D_MODEL), row, row, row, _full((D_MIX, D_MODEL)),
                  _full((128, W_B)), _full((W_B, 128)), _full((W_A, 128))],
        out_specs=(tok(D_MODEL), tok(D_MIX), tok(W_A), tok(128),
                   tok(W_B), tok(128), tok(128), p4(W_B), p4(128), p4(128), p16(W_B), p16(128), p16(128),
                   tok(W_C), tok(128), _full((8, D_MODEL)), _full((D_MIX, D_MODEL))),
        out_shape=(_sds((T, D_MODEL), BF16), _sds((T, D_MIX), BF16),
                   _sds((T, W_A), BF16), _sds((T, 128), F32),
                   _sds((T, W_B), BF16), _sds((T, 128), F32), _sds((T, 128), F32),
                   s4(W_B, BF16), s4(128, F32), s4(128, F32), s16(W_B, BF16), s16(128, F32), s16(128, F32),
                   _sds((T, W_C), BF16), _sds((T, 128), F32), _sds((8, D_MODEL), F32),
                   _sds((D_MIX, D_MODEL), F32)),
        scratch_shapes=[pltpu.VMEM((6, tm, 128), F32)],
        compiler_params=_cp(("arbitrary",), vmem_mb=56),
    )(*_pin(oa, o1, l1, o4, l4, o16, l16, oc, z, x, tgt, g_br, ln_g, ln_b, wout, spread4, gather4, gather8))


class _ReduceScatter:
    def __init__(self, shapes):
        self.shapes = shapes

    def scratch_shapes(self):
        out = []
        for n, w in self.shapes:
            h, p = n // 2, n // 4
            out += [pltpu.VMEM((4, h, w), F32), pltpu.VMEM((4, h, w), F32), pltpu.VMEM((6, p, w), BF16),
                    pltpu.VMEM((6, p, w), BF16), pltpu.VMEM((2, p, w), F32), pltpu.VMEM((h, w), F32)]
        na = len(self.shapes)
        dma = pltpu.SemaphoreType.DMA
        return out + [dma((na, 4)), dma((na, 4)), dma((na, 4)), dma((na, 6)), dma((na, 6)), dma((na,)), dma((na,)),
                      dma((na,))]

    def bind(self, g_refs, r_refs, scratch):
        na = len(self.shapes)
        bufs = [scratch[6 * a:6 * a + 6] for a in range(na)]
        mine, sib, stage, land, keep, tot = (tuple(b[i] for b in bufs) for i in range(6))
        loc_sem, s1_send, s1_recv, s2_send, s2_recv, s3_send, s3_recv, st_sem = scratch[6 * na:6 * na + 8]
        x, y, c = lax.axis_index("x"), lax.axis_index("y"), lax.axis_index("c")
        me, sibling = (x, y, c), (x, y, 1 - c)
        xn, yn, dg = (1 - x, y), (x, 1 - y), (1 - x, 1 - y)
        idx = lambda chip: 2 * chip[0] + chip[1]
        my_chip = idx((x, y))
        order = [idx(xn), idx(dg), idx(yn), my_chip]

        def rows(a, k, half):
            n = self.shapes[a][0]
            return pl.ds(pl.multiple_of(k * n + half * (n // 2), 8), n // 2)

        def piece(a, q):
            p = self.shapes[a][0] // 4
            return slice(q * p, (q + 1) * p)

        def load(a, k):
            return pltpu.make_async_copy(g_refs[a].at[rows(a, k, c), :], mine[a].at[k], loc_sem.at[a, k])

        def s1(a, k, half):
            return pltpu.make_async_remote_copy(
                src_ref=g_refs[a].at[rows(a, k, half), :], dst_ref=sib[a].at[k],
                send_sem=s1_send.at[a, k], recv_sem=s1_recv.at[a, k], device_id=sibling, device_id_type=MESH)

        def s2(a, i, to):
            return pltpu.make_async_remote_copy(
                src_ref=stage[a].at[i], dst_ref=land[a].at[i], send_sem=s2_send.at[a, i], recv_sem=s2_recv.at[a, i],
                device_id=to, device_id_type=MESH)

        via = {0: xn, 1: xn, 2: yn, 3: yn, 4: yn, 5: xn}

        def s3(a, half, to):
            return pltpu.make_async_remote_copy(
                src_ref=tot[a], dst_ref=r_refs[a].at[rows(a, 0, half), :], send_sem=s3_send.at[a],
                recv_sem=s3_recv.at[a], device_id=to, device_id_type=MESH)

        def store(a):
            return pltpu.make_async_copy(tot[a], r_refs[a].at[rows(a, 0, c), :], st_sem.at[a])

        def start():
            for k in order:
                for a in range(na):
                    load(a, k).start()
                    s1(a, k, 1 - c).start()

        def chip_sum(a, k):
            load(a, k).wait()
            s1(a, k, c).wait_recv()
            return mine[a][k] + sib[a][k]

        def exchange():
            for a in range(na):
                P, Q = piece(a, 0), piece(a, 1)
                s_xn = chip_sum(a, idx(xn))
                stage[a][0] = s_xn[P].astype(BF16)
                keep[a][1] = s_xn[Q]
                s_dg = chip_sum(a, idx(dg))
                stage[a][1] = s_dg[P].astype(BF16)
                s2(a, 0, (*xn, c)).start()
                s2(a, 1, (*xn, c)).start()
                stage[a][3] = s_dg[Q].astype(BF16)
                s_yn = chip_sum(a, idx(yn))
                stage[a][2] = s_yn[Q].astype(BF16)
                keep[a][0] = s_yn[P]
                s2(a, 2, (*yn, c)).start()
                s2(a, 3, (*yn, c)).start()
                tot[a][...] = chip_sum(a, my_chip)

        def relay():
            for a in range(na):
                P, Q = piece(a, 0), piece(a, 1)
                s2(a, 1, me).wait_recv()
                stage[a][4] = (keep[a][0] + land[a][1].astype(F32)).astype(BF16)
                s2(a, 4, (*yn, c)).start()
                s2(a, 3, me).wait_recv()
                stage[a][5] = (keep[a][1] + land[a][3].astype(F32)).astype(BF16)
                s2(a, 5, (*xn, c)).start()
                s2(a, 0, me).wait_recv()
                tot[a][P, :] += land[a][0].astype(F32)
                s2(a, 2, me).wait_recv()
                tot[a][Q, :] += land[a][2].astype(F32)

        def finish():
            for a in range(na):
                P, Q = piece(a, 0), piece(a, 1)
                s2(a, 4, me).wait_recv()
                tot[a][P, :] += land[a][4].astype(F32)
                s2(a, 5, me).wait_recv()
                tot[a][Q, :] += land[a][5].astype(F32)
                s3(a, c, sibling).start()
                store(a).start()

        def drain():
            for a in range(na):
                s3(a, 1 - c, me).wait_recv()
                store(a).wait()
            for a in range(na):
                for k in order:
                    s1(a, k, 1 - c).wait_send()
                for i in range(6):
                    s2(a, i, (*via[i], c)).wait_send()
                s3(a, c, sibling).wait_send()

        return start, exchange, relay, finish, drain

    def part(self, grads, steps):
        def body(*refs):
            na = len(self.shapes)
            i = pl.program_id(0)
            for step, phase in zip(steps, self.bind(refs[:na], refs[na:2 * na], refs[2 * na:])):
                pl.when(i == step)(phase)

        hbm = pl.BlockSpec(memory_space=pl.ANY)
        return _Part(body, list(grads), [hbm] * len(grads), [hbm] * len(grads),
                     [_sds((n, w), F32) for n, w in self.shapes], self.scratch_shapes())


def _dh_dx(dqa, dka, dva, dqn, dkn, dvn, dq4, dk4, dv4, dq16, dk16, dv16, dqc, dz, du, xb, cos, sa, sb, winT):
    tm = 512
    spt = SEQ // tm

    def body(dqa_ref, dka_ref, dva_ref, dqn_ref, dkn_ref, dvn_ref, dq4_ref, dk4_ref, dv4_ref,
             dq16_ref, dk16_ref, dv16_ref, dqc_ref, dz_ref, du_ref, xb_ref, cos_ref, sa_ref, sb_ref, w_ref,
             gx_ref, db_ref, gin_ref, dh_ref, scr):
        i = pl.program_id(0)

        @pl.when(i == 0)
        def _():
            db_ref[...] = jnp.zeros_like(db_ref)
            gin_ref[...] = jnp.zeros_like(gin_ref)

        cos_t, sa_t, sb_t = cos_ref[...], sa_ref[...], sb_ref[...]

        def rope_t(t):
            return _rope(t, cos_t, sa_t, sb_t, -1)

        def put(r0, val):
            n = val.shape[1]
            dh_ref[:, r0:r0 + n] = val.astype(BF16)
            db_ref[:, r0:r0 + n] += jnp.sum(val, axis=0, keepdims=True)

        put(O_QA, rope_t(dqa_ref[...].astype(F32)) * QK_SCALE)
        put(O_KA, rope_t(dka_ref[...].astype(F32)))
        put(O_VA, dva_ref[...].astype(F32))
        put(O_QC, dqc_ref[...].astype(F32) * QK_SCALE)
        put(O_Z, dz_ref[...].astype(F32))
        for k, (n_ref, r4, r16) in enumerate(((dqn_ref, dq4_ref, dq16_ref), (dkn_ref, dk4_ref, dk16_ref),
                                               (dvn_ref, dv4_ref, dv16_ref))):
            for j in range(2):
                sl = slice(128 * j, 128 * (j + 1))
                scr[2 * k + j] = n_ref[:, sl].astype(F32)
                for res in range(4):
                    scr[2 * k + j, pl.ds(res, tm // 4, stride=4), :] += r4[0, res, :, sl].astype(F32)
                for res in range(16):
                    scr[2 * k + j, pl.ds(res, tm // 16, stride=16), :] += r16[0, res, :, sl].astype(F32)
        cat = lambda a: jnp.concatenate([scr[a], scr[a + 1]], axis=1)
        put(O_QB, rope_t(cat(0)) * QK_SCALE)
        put(O_KB, rope_t(cat(2)))
        put(O_VB, cat(4))
        gx_ref[...] = _dot(dh_ref[...], w_ref[...], NN) + ALPHA * du_ref[...].astype(F32)
        gin_ref[...] += _dot(dh_ref[...], xb_ref[...], TN)

    tok = lambda w: pl.BlockSpec((tm, w), lambda i: (i, 0))
    tab = pl.BlockSpec((tm, 128), lambda i: (i % spt, 0))
    p4 = pl.BlockSpec((1, 4, tm // 4, W_B), lambda i: (i // spt, 0, i % spt, 0))
    p16 = pl.BlockSpec((1, 16, tm // 16, W_B), lambda i: (i // spt, 0, i % spt, 0))
    once = lambda shape: pl.BlockSpec(shape, lambda i: (0, 0), pipeline_mode=pl.Buffered(1))
    return pl.pallas_call(
        body, name="dh_dx", grid=(T // tm,),
        in_specs=[tok(W_A), tok(W_KV_A), tok(W_KV_A), tok(W_B), tok(W_B), tok(W_B), p4, p4, p4, p16, p16, p16,
                  tok(W_C), tok(D_MIX), tok(D_MODEL), tok(D_MODEL), tab, tab, tab, once((D_IN, D_MODEL))],
        out_specs=(tok(D_MODEL), _full((1, D_IN)), once((D_IN, D_MODEL))),
        out_shape=(_sds((T, D_MODEL), F32), _sds((1, D_IN), F32), _sds((D_IN, D_MODEL), F32)),
        scratch_shapes=[pltpu.VMEM((tm, D_IN), BF16), pltpu.VMEM((6, tm, 128), F32)],
        compiler_params=_cp(("arbitrary",), vmem_mb=56),
    )(*_pin(dqa, dka, dva, dqn, dkn, dvn, dq4, dk4, dv4, dq16, dk16, dv16, dqc, dz, du, xb, cos, sa, sb, winT))


def _reduce_grads(g_in, acc, dbin, dsink):
    rs = _ReduceScatter([(SH_IN, D_MODEL)])

    def body(g_ref, acc_ref, dbin_ref, dsink_ref, r_ref, sv_ref, sv_mine, sv_all, sv_send, sv_recv, *rs_scratch):
        x, y, c = lax.axis_index("x"), lax.axis_index("y"), lax.axis_index("c")
        chips = [(1 - x, y), (x, 1 - y), (1 - x, 1 - y)]
        start, exchange, relay, finish, drain = rs.bind((g_ref,), (r_ref,), rs_scratch)
        start()

        sv_mine[...] = jnp.zeros_like(sv_mine)
        sv_mine[0:4, 0:D_MODEL] = acc_ref[0:4, :]
        sv_mine[4:5, 0:D_IN] = dbin_ref[...]
        sv_mine[5:6, 0:128] = dsink_ref[...]
        my_dev = 4 * x + 2 * y + c
        others = [(x, y, 1 - c)] + [(*chip, cc) for chip in chips for cc in (c, 1 - c)]

        def sv_copy(j, to):
            return pltpu.make_async_remote_copy(
                src_ref=sv_mine, dst_ref=sv_all.at[my_dev], send_sem=sv_send.at[j], recv_sem=sv_recv.at[j],
                device_id=to, device_id_type=MESH)

        sv_sends = [sv_copy(j, to) for j, to in enumerate(others)]
        for cp in sv_sends:
            cp.start()
        exchange()
        relay()
        finish()
        sv_all[my_dev] = sv_mine[...]
        for j in range(7):
            sv_copy(j, (x, y, c)).wait_recv()
        tot = sv_all[0]
        for d in range(1, 8):
            tot = tot + sv_all[d]
        sv_ref[...] = tot
        drain()
        for cp in sv_sends:
            cp.wait_send()

    vm = pl.BlockSpec(memory_space=pltpu.VMEM)
    hbm = pl.BlockSpec(memory_space=pl.ANY)
    return pl.pallas_call(
        body, name="reduce_grads",
        out_shape=(_sds((SH_IN, D_MODEL), F32), _vm_sds((8, SV_W), F32)),
        in_specs=[hbm, vm, vm, vm], out_specs=(hbm, vm),
        scratch_shapes=[pltpu.VMEM((8, SV_W), F32), pltpu.VMEM((8, 8, SV_W), F32),
                        pltpu.SemaphoreType.DMA((7,)), pltpu.SemaphoreType.DMA((7,))] + rs.scratch_shapes(),
        compiler_params=_cp(vmem_mb=40),
    )(pltpu.with_memory_space_constraint(g_in, pltpu.HBM), acc, dbin, dsink)


def _adamw_update(w, g, m, v):
    nm = ADAM_B1 * m + (1.0 - ADAM_B1) * g
    nv = ADAM_B2 * v + (1.0 - ADAM_B2) * (g * g)
    m_hat = nm / (1.0 - ADAM_B1 ** ADAM_STEP)
    v_hat = nv / (1.0 - ADAM_B2 ** ADAM_STEP)
    return -ADAM_LR * (m_hat / (jnp.sqrt(v_hat) + ADAM_EPS) + ADAM_WD * w), nm, nv


SMALL = ((4, D_IN, 1.0), (5, 8, -1.0), (1, D_MIX, 1.0), (2, D_MODEL, 1.0), (3, D_MODEL, 1.0))


def _adamw_all(items, sv, ws, ms, vs, n_steps=4):
    nb, ns = 4 * len(items), len(SMALL)

    def body(*refs):
        ins, sv_ref, small_in = refs[:nb], refs[nb], refs[nb + 1:nb + 1 + 3 * ns]
        outs = refs[nb + 1 + 3 * ns:]
        big_out, loss_ref, small_out = outs[:nb], outs[nb], outs[nb + 1:]
        for p in range(len(items)):
            w_ref, g_ref, m_ref, v_ref = ins[4 * p:4 * p + 4]
            gv = g_ref[...]
            big_out[4 * p][...] = gv
            big_out[4 * p + 1][...], big_out[4 * p + 2][...], big_out[4 * p + 3][...] = _adamw_update(
                w_ref[...], gv, m_ref[...], v_ref[...])

        @pl.when(pl.program_id(0) == 0)
        def _():
            loss_ref[...] = jnp.sum(sv_ref[0:1, 0:D_MODEL], axis=1, keepdims=True)
            for p, (row, width, sign) in enumerate(SMALL):
                gv = sign * sv_ref[row:row + 1, 0:width]
                small_out[4 * p][...] = gv
                small_out[4 * p + 1][...], small_out[4 * p + 2][...], small_out[4 * p + 3][...] = _adamw_update(
                    small_in[p][...], gv, small_in[ns + p][...], small_in[2 * ns + p][...])

    specs, shapes, args = [], [], []
    for w, g, m, v in items:
        rows, width = w.shape
        specs += [pl.BlockSpec((rows // n_steps, width), lambda i: (i, 0))] * 4
        shapes += [_sds((rows, width), F32)] * 4
        args += [w, g, m, v]
    small_args = [*ws, *ms, *vs]
    whole = lambda a: _full(a.shape)
    res = pl.pallas_call(
        body, name="adamw", grid=(n_steps,),
        in_specs=specs + [whole(sv)] + [whole(a) for a in small_args],
        out_specs=tuple(specs + [_full((1, 1))] + [whole(w) for w in ws for _ in range(4)]),
        out_shape=tuple(shapes + [_sds((1, 1), F32)] + [_sds(w.shape, F32) for w in ws for _ in range(4)]),
        compiler_params=_cp(("arbitrary",), vmem_mb=40),
    )(*_pin(*args, sv, *small_args))
    big = [tuple(res[4 * p:4 * p + 4]) for p in range(len(items))]
    return big, res[nb], [tuple(res[nb + 1 + 4 * p:nb + 5 + 4 * p]) for p in range(ns)]


def _rope_tables():
    pos = np.arange(SEQ, dtype=np.float32)
    inv = (np.float32(ROPE_THETA) ** (-np.arange(0, 64, 2, dtype=np.float32) / np.float32(64))).astype(np.float32)
    ang = np.tile(pos[:, None] * inv[None, :], (1, 4))
    cos, sin = np.cos(ang).astype(np.float32), np.sin(ang).astype(np.float32)
    low = (np.arange(128) % 64) < 32
    zero = np.float32(0.0)
    return jnp.asarray(cos), jnp.asarray(np.where(low, -sin, zero)), jnp.asarray(np.where(low, zero, sin))


def _local_step(x2, mem2, tgt2, winT, wout, wmem, b_in, sinks, g_branch, ln_gain, ln_bias):
    cos, sa, sb = _rope_tables()
    sinkv = jnp.pad(sinks, ((0, 0), (0, 120)))
    head_of_lane = np.arange(512)[:, None] // 64
    gather8 = jnp.asarray(head_of_lane == np.arange(128)[None, :], BF16)
    gather4 = jnp.asarray(head_of_lane[:W_B] == np.arange(128)[None, :], BF16)
    spread4 = jnp.asarray((head_of_lane[:W_B] == np.arange(128)[None, :]).T, BF16)

    xb, qa, ka, va, bn, b4, b16, qc, z, wout, wmem, memb, mkv = _in_proj(x2, winT, b_in, cos, sa, sb, wout, wmem,
                                                                         mem2)
    b4f, b16f = b4.reshape(T, 768), b16.reshape(T, 768)

    swa = dict(kind="band", nb=SEQ // BLK, max_dist=BLK - 1, gqa=True)
    dil = (dict(kind="band", nb=SEQ // BLK), dict(kind="band", nb=SEQ // 4 // BLK), dict(kind="band", nb=1))
    (oa, lse_a), (o1, l1), (o4, l4), (o16, l16), (oc, lse_c) = _run_parts("attn_fwd", [
        _attn_fwd(qa, 0, W_A, ka, 0, va, 0, W_KV_A, sinks=sinks, **swa),
        _attn_fwd(bn, 0, W_B, bn, 1, bn, 2, W_B, **dil[0]),
        _attn_fwd(b4f, 0, W_B, b4f, 1, b4f, 2, W_B, **dil[1]),
        _attn_fwd(b16f, 0, W_B, b16f, 1, b16f, 2, W_B, **dil[2]),
        _attn_fwd(qc, 0, W_C, mkv, 0, mkv, 1, W_C, kind="mem")], "parallel", 48)

    s4 = lambda w: (B_LOC, 4, SEQ // 4, w)
    s16 = lambda w: (B_LOC, 16, SEQ // 16, w)
    (du, dz, doa, dla, dobn, lsen, dlbn, dob4, lse4, dlb4, dob16, lse16, dlb16, doc, dlc, acc, g_out) = _middle(
        oa, o1, l1, o4.reshape(s4(W_B)), l4.reshape(s4(128)), o16.reshape(s16(W_B)), l16.reshape(s16(128)), oc, z,
        x2, tgt2, g_branch, ln_gain, ln_bias, wout, spread4, gather4, gather8)

    flat = lambda a: a.reshape(T, a.shape[-1])
    (dqa, dka, dva, dsink), (dqc, g_mem) = _run_parts("attn_bwd_a", [
        _attn_bwd(qa, 0, W_A, ka, 0, va, 0, W_KV_A, doa, lse_a, dla, sinkv=sinkv, **swa),
        _attn_bwd(qc, 0, W_C, mkv, 0, mkv, 1, W_C, doc, lse_c, dlc, kind="mem", mem_in=memb)], "arbitrary", 48)
    last = T // QR - 1
    (r_out, r_mem), (dqn, dkn, dvn), (dq4, dk4, dv4), (dq16, dk16, dv16) = _run_parts("attn_bwd_b", [
        _ReduceScatter([(SH_OUT, D_MODEL), (SH_MEM, 2 * W_C)]).part((g_out, g_mem), (0, 1, 2, last, last)),
        _attn_bwd(bn, 0, W_B, bn, 1, bn, 2, W_B, dobn, lsen, dlbn, **dil[0]),
        _attn_bwd(b4f, 0, W_B, b4f, 1, b4f, 2, W_B, flat(dob4), flat(lse4), flat(dlb4), **dil[1]),
        _attn_bwd(b16f, 0, W_B, b16f, 1, b16f, 2, W_B, flat(dob16), flat(lse16), flat(dlb16), **dil[2])],
        "arbitrary", 62)

    r4 = lambda a: a.reshape(s4(W_B))
    r16 = lambda a: a.reshape(s16(W_B))
    gx, dbin, g_in = _dh_dx(dqa, dka, dva, dqn, dkn, dvn, r4(dq4), r4(dk4), r4(dv4), r16(dq16), r16(dk16),
                            r16(dv16), dqc, dz, du, xb, cos, sa, sb, winT)
    return gx, g_in, r_out, r_mem, acc, dbin, dsink


def kernel(x, mem, w_in, b_in, w_mem, attn_sinks, g_branch, w_out, ln_gain, ln_bias, loss_target, m_w_in, m_b_in, m_w_mem, m_attn_sinks, m_g_branch, m_w_out, m_ln_gain, m_ln_bias, v_w_in, v_b_in, v_w_mem, v_attn_sinks, v_g_branch, v_w_out, v_ln_gain, v_ln_bias):
    winT, wout, wmem = _gather_weights(w_in[0].T, w_out[0], w_mem[0])
    gx, g_in, r_out, r_mem, acc, dbin, dsink = _local_step(
        x.reshape(T, D_MODEL), mem.reshape(B_LOC * MEM_LEN, D_MODEL), loss_target.reshape(T, D_MODEL),
        winT, wout, wmem, b_in, attn_sinks, g_branch, ln_gain, ln_bias)
    r_in, sv = _reduce_grads(g_in, acc, dbin, dsink)

    small = ["b_in", "attn_sinks", "g_branch", "ln_gain", "ln_bias"]
    big, loss, steps = _adamw_all(
        [(w_in[0].T, r_in, m_w_in[0].T, v_w_in[0].T), (w_out[0], r_out, m_w_out[0], v_w_out[0]),
         (w_mem[0], r_mem, m_w_mem[0], v_w_mem[0])],
        sv, [b_in, attn_sinks, g_branch, ln_gain, ln_bias], [m_b_in, m_attn_sinks, m_g_branch, m_ln_gain, m_ln_bias],
        [v_b_in, v_attn_sinks, v_g_branch, v_ln_gain, v_ln_bias])
    out = dict(zip(small, steps))
    out["w_in"] = tuple(a.T[None] for a in big[0])
    out["w_out"], out["w_mem"] = (tuple(a[None] for a in st) for st in big[1:])
    names = ["w_in", "b_in", "w_mem", "attn_sinks", "g_branch", "w_out", "ln_gain", "ln_bias"]
    return (loss.reshape(()), gx.reshape(B_LOC, SEQ, D_MODEL), *[out[n][k] for k in range(4) for n in names])
```

```python
import jax
import jax.numpy as jnp
import numpy as np
from jax import lax
from jax.experimental import pallas as pl
from jax.experimental.pallas import tpu as pltpu

F32, BF16 = jnp.float32, jnp.bfloat16

D_MODEL = 1024
SEQ = 2048
B_LOC = 2
T = B_LOC * SEQ
BLK = 128
MEM_LEN = 256
W_A, W_KV_A, W_B, W_C, D_MIX = 512, 128, 256, 256, 1024
D_IN = 2816
O_QA, O_KA, O_VA, O_QB, O_KB, O_VB, O_QC, O_Z = 0, 512, 640, 768, 1024, 1280, 1536, 1792
ROPE_THETA = 10000.0
LN_EPS = 1e-5
RMS_EPS = 1e-6
ALPHA = 2.0 ** 0.25
QK_SCALE = 0.125
N_CHIP = 4
SH_IN, SH_OUT, SH_MEM = D_IN // N_CHIP, D_MIX // N_CHIP, D_MODEL // N_CHIP
NEG = -1e30
ADAM_LR, ADAM_B1, ADAM_B2, ADAM_EPS, ADAM_WD, ADAM_STEP = 0.001, 0.9, 0.999, 1e-08, 0.01, 10
SV_W = 3072
MESH = pl.DeviceIdType.MESH

NN = ((1,), (0,))
NT = ((1,), (1,))
TN = ((0,), (0,))


def _dot(a, b, dims):
    return lax.dot_general(a, b, (dims, ((), ())), preferred_element_type=F32)


def _cp(sem=None, vmem_mb=None):
    kw = {}
    if sem is not None:
        kw["dimension_semantics"] = sem
    if vmem_mb is not None:
        kw["vmem_limit_bytes"] = vmem_mb * 1024 * 1024
    return pltpu.CompilerParams(**kw)


def _sds(shape, dtype):
    return pltpu.HBM(shape, dtype)


def _vm_sds(shape, dtype):
    return jax.ShapeDtypeStruct(shape, dtype)


def _pin(*args):
    return [pltpu.with_memory_space_constraint(a, pltpu.HBM) for a in args]


def _full(shape):
    n = len(shape)
    return pl.BlockSpec(shape, lambda *_: (0,) * n)


def _shard_rows(ref, n, chip, half):
    start = pl.multiple_of((2 * chip[0] + chip[1]) * n + half * (n // 2), 16)
    return ref.at[pl.ds(start, n // 2), :]


def _gather_weights(win_sh, wout_sh, wmem_sh):
    half, piece = SH_IN // 2, SH_IN // 4
    shards = ((SH_IN, D_MODEL), (SH_OUT, D_MODEL), (SH_MEM, 2 * W_C))

    def body(a_ref, b_ref, c_ref, oa_ref, ob_ref, oc_ref, raw_a, raw_b, raw_c, own_a, own_b, own_c,
             load_sem, store_sem, ici_send, ici_recv, d2d_send, d2d_recv):
        x, y, c = lax.axis_index("x"), lax.axis_index("y"), lax.axis_index("c")
        me, sibling = (x, y, c), (x, y, 1 - c)
        xn, yn, dg = (1 - x, y), (x, 1 - y), (1 - x, 1 - y)
        srcs, raws = (a_ref, b_ref, c_ref), (raw_a, raw_b, raw_c)
        owns, outs = (own_a, own_b, own_c), (oa_ref, ob_ref, oc_ref)
        loads = [pltpu.make_async_copy(srcs[a], raws[a], load_sem.at[a]) for a in range(3)]
        for cp in loads:
            cp.start()

        def rows(chip, hf, q):
            start = pl.multiple_of((2 * chip[0] + chip[1]) * SH_IN + hf * half + q * piece, 16)
            return oa_ref.at[pl.ds(start, piece), :]

        def copy(sems, k, chip, hf, q, to, src=None):
            blk = rows(chip, hf, q)
            return pltpu.make_async_remote_copy(
                src_ref=blk if src is None else src, dst_ref=blk, send_sem=sems[0].at[k], recv_sem=sems[1].at[k],
                device_id=to, device_id_type=MESH)

        def my_piece(q):
            return own_a.at[pl.ds(pl.multiple_of(c * half + q * piece, 16), piece), :]

        ici, d2d = (ici_send, ici_recv), (d2d_send, d2d_recv)
        stores, direct = [], []
        for a, (n, _) in enumerate(shards):
            loads[a].wait()
            owns[a][...] = raws[a][...].astype(BF16)
            mine = pl.ds(pl.multiple_of((2 * x + y) * n, 16), n)
            stores.append(pltpu.make_async_copy(owns[a], outs[a].at[mine, :], store_sem.at[a]))
            stores[-1].start()
            if a == 0:
                direct = [copy(ici, 0, (x, y), c, 0, (*xn, c), my_piece(0)),
                          copy(ici, 1, (x, y), c, 1, (*xn, c), my_piece(1)),
                          copy(ici, 3, (x, y), c, 0, (*yn, c), my_piece(0)),
                          copy(ici, 4, (x, y), c, 1, (*yn, c), my_piece(1))]
                for cp in direct:
                    cp.start()
        arrivals = [(0, xn, 0), (1, xn, 1), (3, yn, 0), (4, yn, 1), (2, dg, 1), (5, dg, 0)]
        passed = []
        for k, chip, q in arrivals:
            copy(ici, k, chip, c, q, me).wait_recv()
            if k == 0:
                passed.append(copy(ici, 5, xn, c, 0, (*yn, c)))
                passed[-1].start()
            if k == 4:
                passed.append(copy(ici, 2, yn, c, 1, (*xn, c)))
                passed[-1].start()
            passed.append(copy(d2d, k, chip, c, q, sibling))
            passed[-1].start()
        for k, chip, q in arrivals:
            copy(d2d, k, chip, 1 - c, q, me).wait_recv()
        for cp in direct + passed:
            cp.wait_send()
        for cp in stores:
            cp.wait()

    hbm = pl.BlockSpec(memory_space=pl.ANY)
    return pl.pallas_call(
        body, name="gather_weights",
        out_shape=(_sds((D_IN, D_MODEL), BF16), _sds((D_MIX, D_MODEL), BF16), _sds((D_MODEL, 2 * W_C), BF16)),
        in_specs=[hbm, hbm, hbm], out_specs=(hbm, hbm, hbm),
        scratch_shapes=([pltpu.VMEM(sh, F32) for sh in shards] + [pltpu.VMEM(sh, BF16) for sh in shards]
                        + [pltpu.SemaphoreType.DMA((3,))] * 2 + [pltpu.SemaphoreType.DMA((6,))] * 4),
        compiler_params=_cp(vmem_mb=40),
    )(*_pin(win_sh, wout_sh, wmem_sh))


def _rope(t, cos, sa, sb, sign):
    w = t.shape[1]
    reps = w // 128
    c, a, b = (jnp.tile(v, (1, reps)) if reps > 1 else v for v in (cos, sa, sb))
    rot = pltpu.roll(t, w - 32, 1) * a + pltpu.roll(t, 32, 1) * b
    return t * c + rot if sign > 0 else t * c - rot


def _in_proj(x, winT, b_in, cos, sa, sb, wout_own, wmem_own):
    tm = 512
    spt = SEQ // tm
    n_steps = T // tm
    forward_step = n_steps // 2

    def body(x_ref, w_ref, b_ref, cos_ref, sa_ref, sb_ref, wo_in, wm_in,
             xb_ref, qa_ref, ka_ref, va_ref, bn_ref, b4_ref, b16_ref, qc_ref, z_ref, wo_ref, wm_ref,
             scr, ici_send, ici_recv, d2d_send, d2d_recv):
        i = pl.program_id(0)
        mx, my, mc = lax.axis_index("x"), lax.axis_index("y"), lax.axis_index("c")
        chips = [(1 - mx, my), (mx, 1 - my), (1 - mx, 1 - my)]
        full = ((wo_ref, SH_OUT), (wm_ref, SH_MEM))

        def copy(sems, a, j, chip_of_block, half, to):
            blk = _shard_rows(full[a][0], full[a][1], chip_of_block, half)
            return pltpu.make_async_remote_copy(
                src_ref=blk, dst_ref=blk, send_sem=sems[0].at[a, j], recv_sem=sems[1].at[a, j],
                device_id=to, device_id_type=MESH)

        ici, d2d = (ici_send, ici_recv), (d2d_send, d2d_recv)
        pairs = [(a, j, chip) for j, chip in enumerate(chips) for a in range(2)]

        @pl.when(i == 0)
        def _():
            for a, j, chip in pairs:
                copy(ici, a, j, (mx, my), mc, (*chip, mc)).start()

        @pl.when(i == forward_step)
        def _():
            for a, j, chip in pairs:
                copy(ici, a, j, chip, mc, (mx, my, mc)).wait_recv()
                copy(d2d, a, j, chip, mc, (mx, my, 1 - mc)).start()

        @pl.when(i == n_steps - 1)
        def _():
            for a, j, chip in pairs:
                copy(d2d, a, j, chip, 1 - mc, (mx, my, mc)).wait_recv()
            for a, j, chip in pairs:
                copy(ici, a, j, (mx, my), mc, (*chip, mc)).wait_send()
                copy(d2d, a, j, chip, mc, (mx, my, 1 - mc)).wait_send()

        xb = x_ref[...].astype(BF16)
        xb_ref[...] = xb
        cos_t, sa_t, sb_t = cos_ref[...], sa_ref[...], sb_ref[...]

        def proj(r0, n):
            return _dot(xb, w_ref[r0:r0 + n, :], NT) + b_ref[:, r0:r0 + n]

        def rope(t):
            return _rope(t, cos_t, sa_t, sb_t, +1)

        parts = (rope(proj(O_QB, W_B)) * QK_SCALE, rope(proj(O_KB, W_B)), proj(O_VB, W_B))
        for k, part in enumerate(parts):
            bn_ref[:, 256 * k:256 * (k + 1)] = part.astype(BF16)
            scr[2 * k] = part[:, :128]
            scr[2 * k + 1] = part[:, 128:]
        for j in range(6):
            lanes = slice(128 * j, 128 * (j + 1))
            for res in range(4):
                t = scr[j, pl.ds(res, tm // 4, stride=4), :]
                b4_ref[0, res, :, lanes] = t.astype(BF16)
                scr[6 + j, res * (tm // 4):(res + 1) * (tm // 4), :] = t
            for res in range(16):
                b16_ref[0, res, :, lanes] = scr[6 + j, pl.ds((res % 4) * (tm // 4) + res // 4, tm // 16, stride=4),
                                                :].astype(BF16)
        qa_ref[...] = (rope(proj(O_QA, W_A)) * QK_SCALE).astype(BF16)
        ka_ref[...] = rope(proj(O_KA, W_KV_A)).astype(BF16)
        va_ref[...] = proj(O_VA, W_KV_A).astype(BF16)
        qc_ref[...] = (proj(O_QC, W_C) * QK_SCALE).astype(BF16)
        z_ref[...] = proj(O_Z, D_MIX).astype(BF16)

    tok = lambda w: pl.BlockSpec((tm, w), lambda i: (i, 0))
    tab = pl.BlockSpec((tm, 128), lambda i: (i % spt, 0))
    hbm = pl.BlockSpec(memory_space=pl.ANY)
    return pl.pallas_call(
        body, name="in_proj", grid=(n_steps,),
        in_specs=[tok(D_MODEL), _full((D_IN, D_MODEL)), _full((1, D_IN)), tab, tab, tab, hbm, hbm],
        out_specs=(tok(D_MODEL), tok(W_A), tok(W_KV_A), tok(W_KV_A), tok(768),
                   pl.BlockSpec((1, 4, tm // 4, 768), lambda i: (i // spt, 0, i % spt, 0)),
                   pl.BlockSpec((1, 16, tm // 16, 768), lambda i: (i // spt, 0, i % spt, 0)),
                   tok(W_C), tok(D_MIX), hbm, hbm),
        out_shape=(_sds((T, D_MODEL), BF16), _sds((T, W_A), BF16), _sds((T, W_KV_A), BF16), _sds((T, W_KV_A), BF16),
                   _sds((T, 768), BF16), _sds((B_LOC, 4, SEQ // 4, 768), BF16), _sds((B_LOC, 16, SEQ // 16, 768), BF16),
                   _sds((T, W_C), BF16), _sds((T, D_MIX), BF16),
                   _sds((D_MIX, D_MODEL), BF16), _sds((D_MODEL, 2 * W_C), BF16)),
        input_output_aliases={6: 9, 7: 10},
        scratch_shapes=[pltpu.VMEM((12, tm, 128), F32)] + [pltpu.SemaphoreType.DMA((2, 3))] * 4,
        compiler_params=_cp(("arbitrary",), vmem_mb=48),
    )(*_pin(x, winT, b_in, cos, sa, sb, wout_own, wmem_own))


def _mem_kv(mem, wmem):
    def body(m_ref, w_ref, mb_ref, kv_ref):
        mb = m_ref[...].astype(BF16)
        mb_ref[...] = mb
        kv_ref[...] = _dot(mb, w_ref[...], NN).astype(BF16)

    n = B_LOC * MEM_LEN
    return pl.pallas_call(
        body, name="mem_kv",
        out_shape=(_sds((n, D_MODEL), BF16), _sds((n, 2 * W_C), BF16)),
    )(*_pin(mem, wmem))


class _Part:
    def __init__(self, body, args, in_specs, out_specs, out_shape, scratch=()):
        self.body, self.args, self.in_specs, self.out_specs, self.out_shape = body, args, in_specs, out_specs, out_shape
        self.scratch = list(scratch)


def _run_parts(name, parts, semantics, vmem_mb):
    n_in = [len(p.args) for p in parts]
    n_out = [len(p.out_shape) for p in parts]
    n_scr = [len(p.scratch) for p in parts]

    def body(*refs):
        ins, outs, scr = refs[:sum(n_in)], refs[sum(n_in):sum(n_in) + sum(n_out)], refs[sum(n_in) + sum(n_out):]
        i0 = o0 = s0 = 0
        for p, ni, no, ns in zip(parts, n_in, n_out, n_scr):
            p.body(*ins[i0:i0 + ni], *outs[o0:o0 + no], *scr[s0:s0 + ns])
            i0, o0, s0 = i0 + ni, o0 + no, s0 + ns

    res = pl.pallas_call(
        body, name=name, grid=(T // QR,),
        in_specs=[sp for p in parts for sp in p.in_specs], out_specs=tuple(sp for p in parts for sp in p.out_specs),
        out_shape=tuple(sh for p in parts for sh in p.out_shape),
        scratch_shapes=[sc for p in parts for sc in p.scratch],
        compiler_params=_cp((semantics,), vmem_mb=vmem_mb),
    )(*_pin(*[a for p in parts for a in p.args]))
    out, o0 = [], 0
    for no in n_out:
        out.append(tuple(res[o0:o0 + no]))
        o0 += no
    return out


QB = 8
QR = QB * BLK


def _lane_lo():
    return lax.broadcasted_iota(jnp.int32, (1, 128), 1) < 64


def _dup_head(k2, hk, lo):
    kf = k2.astype(F32)
    r = pltpu.roll(kf, 64, 1)
    return (jnp.where(lo, kf, r) if hk == 0 else jnp.where(lo, r, kf)).astype(BF16)


def _stack_heads(pairs, lo):
    parts = []
    for x2 in pairs:
        z = jnp.zeros_like(x2)
        parts += [jnp.where(lo, x2, z), jnp.where(lo, z, x2)]
    return jnp.concatenate(parts, axis=0)


def _prev_mode(kind, nb, j):
    if kind == "mem" or nb == 1:
        return "no"
    if nb <= QB:
        return "yes" if j % nb else "no"
    return "yes" if j else "dyn"


class _Attn:
    def __init__(self, kind, nb, max_dist, gqa, qw, kvw, qcb, kcb, vcb):
        self.kind, self.nb, self.gqa, self.qw, self.kvw = kind, nb, gqa, qw, kvw
        npairs = qw // 128
        self.groups = ([(hk, [2 * hk, 2 * hk + 1]) for hk in range(npairs // 2)] if gqa
                       else [(p, [p]) for p in range(npairs)])
        self.nh = 2 * len(self.groups[0][1])
        self.cols = 128 * self.nh
        self.reach = BLK - max_dist
        self.ext_prev = kind == "band" and nb > QB
        self.q_spec = pl.BlockSpec((QR, qw), lambda g: (g, qcb))
        self.row_spec = pl.BlockSpec((QR, qw), lambda g: (g, 0))
        self.stat_spec = pl.BlockSpec((QR, 128), lambda g: (g, 0))
        if kind == "mem":
            per = SEQ // QR
            self.kv_specs = [pl.BlockSpec((MEM_LEN, kvw), lambda g: (g // per, kcb)),
                             pl.BlockSpec((MEM_LEN, kvw), lambda g: (g // per, vcb))]
        else:
            self.kv_specs = [pl.BlockSpec((QR, kvw), lambda g: (g, kcb)), pl.BlockSpec((QR, kvw), lambda g: (g, vcb))]
            if self.ext_prev:
                self.kv_specs += [pl.BlockSpec((BLK, kvw), lambda g: (jnp.maximum(g * QB - 1, 0), kcb)),
                                  pl.BlockSpec((BLK, kvw), lambda g: (jnp.maximum(g * QB - 1, 0), vcb))]

    def masks(self):
        if self.kind == "mem":
            return None
        kj = lax.broadcasted_iota(jnp.int32, (2 * BLK, self.cols), 0)
        qi = lax.broadcasted_iota(jnp.int32, (2 * BLK, self.cols), 1) & (BLK - 1)
        kj1 = lax.broadcasted_iota(jnp.int32, (BLK, self.cols), 0)
        qi1 = lax.broadcasted_iota(jnp.int32, (BLK, self.cols), 1) & (BLK - 1)
        return kj, qi, kj1 <= qi1

    def keys(self, j, gi, kc_ref, vc_ref, kp_ref, vp_ref, lo, kq, g):
        def kv(k_ref, v_ref, r):
            if self.gqa:
                return _dup_head(k_ref[r, :], gi, lo), _dup_head(v_ref[r, :], gi, lo)
            sl = slice(128 * gi, 128 * (gi + 1))
            return k_ref[r, sl], v_ref[r, sl]

        if self.kind == "mem":
            key0 = pl.multiple_of((g // (SEQ // QR)) * MEM_LEN, MEM_LEN)
            return (*kv(kc_ref, vc_ref, slice(None)), None, [(0, MEM_LEN, key0)])
        kj, qi, cur = kq
        row0 = g * QR + BLK * j
        mode = _prev_mode(self.kind, self.nb, j)
        if mode == "no":
            return (*kv(kc_ref, vc_ref, slice(BLK * j, BLK * (j + 1))), cur, [(0, BLK, pl.multiple_of(row0, BLK))])
        if mode == "yes":
            mask = jnp.logical_and(kj >= qi + self.reach, kj <= qi + BLK)
            return (*kv(kc_ref, vc_ref, slice(BLK * (j - 1), BLK * (j + 1))), mask,
                    [(0, 2 * BLK, pl.multiple_of(row0 - BLK, BLK))])
        has_prev = ((g * QB) % self.nb) > 0
        hp = has_prev.astype(jnp.int32)
        mask = jnp.logical_and(kj >= qi * hp + (self.reach * hp + BLK * (1 - hp)), kj <= qi + BLK)
        kp, vp = kv(kp_ref, vp_ref, slice(None))
        kc, vc = kv(kc_ref, vc_ref, slice(0, BLK))
        return (jnp.concatenate([kp, kc], axis=0), jnp.concatenate([vp, vc], axis=0), mask,
                [(0, BLK, pl.multiple_of(jnp.maximum(row0 - BLK, 0), BLK)), (BLK, BLK, pl.multiple_of(row0, BLK))])


def _attn_fwd(q, qcb, qw, k, kcb, v, vcb, kvw, *, kind, nb=1, max_dist=BLK, gqa=False, sinks=None):
    a = _Attn(kind, nb, max_dist, gqa, qw, kvw, qcb, kcb, vcb)

    def body(*refs):
        it = iter(refs)
        q_ref, kc_ref, vc_ref = next(it), next(it), next(it)
        kp_ref, vp_ref = (next(it), next(it)) if a.ext_prev else (None, None)
        sink_ref = next(it) if sinks is not None else None
        o_ref, lse_ref = next(it), next(it)
        g = pl.program_id(0)
        lo = _lane_lo()
        top = lax.broadcasted_iota(jnp.int32, (128, 1), 0) < 64
        rid = lax.broadcasted_iota(jnp.int32, (8, 128), 0)
        kq = a.masks()
        stats = {}

        def scores(j, gi, pairs):
            rows = slice(BLK * j, BLK * (j + 1))
            qs = _stack_heads([q_ref[rows, 128 * p:128 * (p + 1)] for p in pairs], lo)
            kk, vv, mask, _ = a.keys(j, gi, kc_ref, vc_ref, kp_ref, vp_ref, lo, kq, g)
            pieces = [slice(r0, r0 + BLK) for r0 in range(0, kk.shape[0], BLK)]
            return dict(j=j, gi=gi, pairs=pairs, rows=rows, vv=vv, mask=mask, pieces=pieces,
                        ss=[_dot(kk[r], qs, NT) for r in pieces])

        def softmax(c):
            gi, mask = c["gi"], c["mask"]
            ss = [s if mask is None else jnp.where(mask[r], s, NEG) for r, s in zip(c["pieces"], c.pop("ss"))]
            m = jnp.max(ss[0], axis=0, keepdims=True)
            for s in ss[1:]:
                m = jnp.maximum(m, jnp.max(s, axis=0, keepdims=True))
            if sink_ref is not None:
                sk = jnp.concatenate([jnp.full((1, 128), sink_ref[0, a.nh * gi + i], F32) for i in range(a.nh)], axis=1)
                m = jnp.maximum(m, sk)
            ps = [jnp.exp(s - m) for s in ss]
            l = sum(jnp.sum(p, axis=0, keepdims=True) for p in ps)
            if sink_ref is not None:
                l = l + jnp.exp(sk - m)
            c["ps"] = [p.astype(BF16) for p in ps]
            c["l"], c["lse"] = l, m + jnp.log(l)

        def outputs(c):
            j, gi, rows = c["j"], c["gi"], c["rows"]
            ot = sum(_dot(c["vv"][r], p, TN) for r, p in zip(c["pieces"], c["ps"]))
            ot = ot * pl.reciprocal(c["l"], approx=True)
            for i, p in enumerate(c["pairs"]):
                o2t = jnp.where(top, ot[:, 256 * i:256 * i + 128], ot[:, 256 * i + 128:256 * i + 256])
                o_ref[rows, 128 * p:128 * (p + 1)] = o2t.T.astype(BF16)
            stat = stats.get(j, jnp.zeros((8, 128), F32))
            for i in range(a.nh):
                stat = jnp.where(rid == a.nh * gi + i, c["lse"][:, 128 * i:128 * (i + 1)], stat)
            stats[j] = stat
            if gi == a.groups[-1][0]:
                lse_ref[rows, :] = jnp.concatenate([stats.pop(j), jnp.zeros((120, 128), F32)], axis=0).T

        chains = [(j, gi, pairs) for j in range(QB) for gi, pairs in a.groups]
        live = {}
        for t in range(len(chains) + 2):
            if t < len(chains):
                live[t] = scores(*chains[t])
            if 0 <= t - 1 < len(chains):
                softmax(live[t - 1])
            if 0 <= t - 2 < len(chains):
                outputs(live.pop(t - 2))


    args = [q, k, v] + ([k, v] if a.ext_prev else [])
    in_specs = [a.q_spec] + a.kv_specs
    if sinks is not None:
        args.append(sinks)
        in_specs.append(pl.BlockSpec(memory_space=pltpu.SMEM))
    return _Part(body, args, in_specs, [a.row_spec, a.stat_spec], [_sds((T, qw), BF16), _sds((T, 128), F32)])


def _attn_bwd(q, qcb, qw, k, kcb, v, vcb, kvw, do, lse, dl, *, kind, nb=1, max_dist=BLK, gqa=False, sinkv=None,
              mem_in=None):
    a = _Attn(kind, nb, max_dist, gqa, qw, kvw, qcb, kcb, vcb)

    def body(*refs):
        it = iter(refs)
        q_ref, kc_ref, vc_ref = next(it), next(it), next(it)
        kp_ref, vp_ref = (next(it), next(it)) if a.ext_prev else (None, None)
        do_ref, lse_ref, dl_ref = next(it), next(it), next(it)
        sinkv_ref = next(it) if sinkv is not None else None
        mem_ref = next(it) if kind == "mem" else None
        dq_ref = next(it)
        if kind == "mem":
            gmem_ref = next(it)
        else:
            dk_out, dv_out = next(it), next(it)
        dsink_ref = next(it) if sinkv is not None else None
        if kind != "mem":
            dk_ref, dv_ref, stage_k, stage_v, flush_sem = next(it), next(it), next(it), next(it), next(it)
        else:
            dkv_ref = next(it)
        g = pl.program_id(0)
        lo = _lane_lo()
        top = lax.broadcasted_iota(jnp.int32, (128, 1), 0) < 64

        @pl.when(g == 0)
        def _():
            if kind == "mem":
                dkv_ref[...] = jnp.zeros_like(dkv_ref)
            else:
                dk_ref[...] = jnp.zeros_like(dk_ref)
                dv_ref[...] = jnp.zeros_like(dv_ref)
            if dsink_ref is not None:
                dsink_ref[...] = jnp.zeros_like(dsink_ref)

        kq = a.masks()
        stats_t = {}

        def first_matmuls(j, gi, pairs):
            rows = slice(BLK * j, BLK * (j + 1))
            if j not in stats_t:
                stats_t[j] = (lse_ref[rows, :].T, dl_ref[rows, :].T)
            lse_t, dl_t = stats_t[j]
            heads = [a.nh * gi + i for i in range(a.nh)]
            c = dict(rows=rows, gi=gi, pairs=pairs)
            c["qs"] = _stack_heads([q_ref[rows, 128 * p:128 * (p + 1)] for p in pairs], lo)
            c["dos"] = _stack_heads([do_ref[rows, 128 * p:128 * (p + 1)] for p in pairs], lo)
            c["lse_row"] = jnp.concatenate([lse_t[h:h + 1, :] for h in heads], axis=1)
            c["dl_row"] = jnp.concatenate([dl_t[h:h + 1, :] for h in heads], axis=1)
            c["kk"], vv, c["mask"], c["dests"] = a.keys(j, gi, kc_ref, vc_ref, kp_ref, vp_ref, lo, kq, g)
            c["s"] = _dot(c["kk"], c["qs"], NT)
            c["dp"] = _dot(vv, c["dos"], NT)
            return c

        def elementwise(c):
            s = c.pop("s")
            if c["mask"] is not None:
                s = jnp.where(c["mask"], s, NEG)
            p = jnp.exp(s - c["lse_row"])
            c["ds"] = (p * (c.pop("dp") - c["dl_row"])).astype(BF16)
            c["p"] = p.astype(BF16)

        def last_matmuls(c):
            gi, rows = c["gi"], c["rows"]
            dqt = _dot(c["kk"], c["ds"], TN)
            ck = _dot(c["ds"], c["qs"], NN)
            cv = _dot(c["p"], c["dos"], NN)
            if gqa:
                sel = lo if gi == 0 else jnp.logical_not(lo)
                ck = jnp.where(sel, ck + pltpu.roll(ck, 64, 1), 0.0)
                cv = jnp.where(sel, cv + pltpu.roll(cv, 64, 1), 0.0)
                kcols = slice(0, 128)
            else:
                kcols = slice(128 * gi, 128 * (gi + 1))
            for r0, nr, key0 in c["dests"]:
                krows = pl.ds(key0, nr)
                if kind == "mem":
                    dkv_ref[krows, kcols] += ck[r0:r0 + nr]
                    dkv_ref[krows, slice(kvw + kcols.start, kvw + kcols.stop)] += cv[r0:r0 + nr]
                else:
                    dk_ref[krows, kcols] += ck[r0:r0 + nr]
                    dv_ref[krows, kcols] += cv[r0:r0 + nr]
            for i, p in enumerate(c["pairs"]):
                dq2t = jnp.where(top, dqt[:, 256 * i:256 * i + 128], dqt[:, 256 * i + 128:256 * i + 256])
                dq_ref[rows, 128 * p:128 * (p + 1)] = dq2t.T.astype(BF16)

        chains = [(j, gi, pairs) for j in range(QB) for gi, pairs in a.groups]
        live = {}
        for t in range(len(chains) + 2):
            if t < len(chains):
                live[t] = first_matmuls(*chains[t])
            if 0 <= t - 1 < len(chains):
                elementwise(live[t - 1])
            if 0 <= t - 2 < len(chains):
                last_matmuls(live.pop(t - 2))
        if dsink_ref is not None:
            ps = jnp.exp(sinkv_ref[...] - lse_ref[...]) * dl_ref[...]
            dsink_ref[...] += jnp.sum(ps, axis=0, keepdims=True)
        if kind == "mem":
            @pl.when(g == T // QR - 1)
            def _():
                gmem_ref[...] = _dot(mem_ref[...], dkv_ref[...].astype(BF16), TN)
        else:
            n_steps = T // QR

            def flush(step):
                rows = pl.ds(pl.multiple_of(step * QR, QR), QR)
                out = []
                for acc, stage, dst, i in ((dk_ref, stage_k, dk_out, 0), (dv_ref, stage_v, dv_out, 1)):
                    stage[...] = acc[rows, :].astype(BF16)
                    out.append(pltpu.make_async_copy(stage, dst.at[rows, :], flush_sem.at[i]))
                return out

            def flushed(step):
                rows = pl.ds(pl.multiple_of(step * QR, QR), QR)
                return [pltpu.make_async_copy(stage, dst.at[rows, :], flush_sem.at[i])
                        for stage, dst, i in ((stage_k, dk_out, 0), (stage_v, dv_out, 1))]

            @pl.when(g >= 2)
            def _():
                for cp in flushed(g - 2):
                    cp.wait()

            @pl.when(g >= 1)
            def _():
                for cp in flush(g - 1):
                    cp.start()

            @pl.when(g == n_steps - 1)
            def _():
                for cp in flushed(g - 1):
                    cp.wait()
                for cp in flush(g):
                    cp.start()
                for cp in flushed(g):
                    cp.wait()

    args = [q, k, v] + ([k, v] if a.ext_prev else []) + [do, lse, dl]
    in_specs = [a.q_spec] + a.kv_specs + [a.row_spec, a.stat_spec, a.stat_spec]
    if sinkv is not None:
        args.append(sinkv)
        in_specs.append(_full((1, 128)))
    if kind == "mem":
        args.append(mem_in)
        in_specs.append(pl.BlockSpec(mem_in.shape, lambda g: (0, 0), pipeline_mode=pl.Buffered(1)))
    out_shape = [_sds((T, qw), BF16)]
    out_specs = [a.row_spec]
    scratch = []
    if kind == "mem":
        out_shape.append(_sds((D_MODEL, 2 * kvw), F32))
        out_specs.append(pl.BlockSpec((D_MODEL, 2 * kvw), lambda g: (0, 0), pipeline_mode=pl.Buffered(1)))
        scratch = [pltpu.VMEM((B_LOC * MEM_LEN, 2 * kvw), F32)]
    else:
        out_shape += [_sds((T, kvw), BF16)] * 2
        out_specs += [pl.BlockSpec(memory_space=pl.ANY)] * 2
        scratch = [pltpu.VMEM((T, kvw), F32)] * 2 + [pltpu.VMEM((QR, kvw), BF16)] * 2 + [pltpu.SemaphoreType.DMA((2,))]
    if sinkv is not None:
        out_shape.append(_sds((1, 128), F32))
        out_specs.append(_full((1, 128)))
    return _Part(body, args, in_specs, out_specs, out_shape, scratch)


def _dot2(v, w_ref):
    hi = v.astype(BF16)
    lo = (v - hi.astype(F32)).astype(BF16)
    return _dot(hi, w_ref[...], NN) + _dot(lo, w_ref[...], NN)


def _middle(oa, o1, l1, o4, l4, o16, l16, oc, z, x, tgt, g_br, ln_g, ln_b, wout, spread4, gather4, gather8):
    tm = 512
    spt = SEQ // tm

    def body(oa_ref, o1_ref, l1_ref, o4_ref, l4_ref, o16_ref, l16_ref, oc_ref, z_ref, x_ref, t_ref,
             g_ref, lg_ref, lb_ref, w_ref, sp4_ref, ga4_ref, ga8_ref,
             du_ref, dz_ref, doa_ref, dla_ref,
             dobn_ref, lsen_ref, dlbn_ref, dob4_ref, lse4_ref, dlb4_ref, dob16_ref, lse16_ref, dlb16_ref,
             doc_ref, dlc_ref, acc_ref, gout_ref, scr):
        i = pl.program_id(0)

        @pl.when(i == 0)
        def _():
            acc_ref[...] = jnp.zeros_like(acc_ref)
            gout_ref[...] = jnp.zeros_like(gout_ref)

        for res in range(4):
            rows = pl.ds(res, tm // 4, stride=4)
            for j in range(2):
                scr[j, rows, :] = o4_ref[0, res, :, 128 * j:128 * (j + 1)].astype(F32)
            scr[2, rows, :] = l4_ref[0, res]
        for res in range(16):
            rows = pl.ds(res, tm // 16, stride=16)
            for j in range(2):
                scr[3 + j, rows, :] = o16_ref[0, res, :, 128 * j:128 * (j + 1)].astype(F32)
            scr[5, rows, :] = l16_ref[0, res]
        inv_d = 1.0 / D_MODEL
        gb, lg, lb = g_ref[...], lg_ref[...], lb_ref[...]

        def rms(o):
            r = lax.rsqrt(jnp.sum(o * o, axis=1, keepdims=True) * (1.0 / o.shape[1]) + RMS_EPS)
            return o * r, r

        def rms_bwd(dn_, n_, r):
            return r * (dn_ - n_ * (jnp.sum(dn_ * n_, axis=1, keepdims=True) * (1.0 / n_.shape[1])))

        def forward(rs):
            o4v = jnp.concatenate([scr[0, rs, :], scr[1, rs, :]], axis=1)
            o16v = jnp.concatenate([scr[3, rs, :], scr[4, rs, :]], axis=1)
            l1v, l4v, l16v = l1_ref[rs, :], scr[2, rs, :], scr[5, rs, :]
            mx = jnp.maximum(jnp.maximum(l1v, l4v), l16v)
            e1, e4, e16 = jnp.exp(l1v - mx), jnp.exp(l4v - mx), jnp.exp(l16v - mx)
            ssum = e1 + e4 + e16
            inv = 1.0 / ssum
            c = dict(rs=rs, lse_b=mx + jnp.log(ssum))
            c["ob"] = (_dot2(e1 * inv, sp4_ref) * o1_ref[rs, :].astype(F32) + _dot2(e4 * inv, sp4_ref) * o4v
                       + _dot2(e16 * inv, sp4_ref) * o16v)
            c["oa"], c["oc"] = oa_ref[rs, :].astype(F32), oc_ref[rs, :].astype(F32)
            na, c["ra"] = rms(c["oa"])
            nb_, c["rb"] = rms(c["ob"])
            nc, c["rc"] = rms(c["oc"])
            c["n"] = jnp.concatenate([na, nb_, nc], axis=1)
            c["zf"] = z_ref[rs, :].astype(F32)
            c["sig"] = 1.0 / (1.0 + jnp.exp(-c["zf"]))
            c["sz"] = c["zf"] * c["sig"]
            c["yb"] = (c["n"] * gb * c["sz"]).astype(BF16)
            c["y2"] = _dot(c["yb"], w_ref[...], NN)
            return c

        def norm(c):
            rs = c["rs"]
            u = ALPHA * x_ref[rs, :] + c.pop("y2")
            mu = jnp.sum(u, axis=1, keepdims=True) * inv_d
            uc = u - mu
            rstd = lax.rsqrt(jnp.sum(uc * uc, axis=1, keepdims=True) * inv_d + LN_EPS)
            xh = uc * rstd
            diff = xh * lg + lb - t_ref[rs, :]
            acc_ref[0:1, :] += jnp.sum(diff * diff, axis=0, keepdims=True) * (0.5 * inv_d)
            dout = diff * inv_d
            acc_ref[2:3, :] += jnp.sum(dout * xh, axis=0, keepdims=True)
            acc_ref[3:4, :] += jnp.sum(dout, axis=0, keepdims=True)
            dxh = dout * lg
            du = rstd * (dxh - jnp.sum(dxh, axis=1, keepdims=True) * inv_d
                         - xh * (jnp.sum(dxh * xh, axis=1, keepdims=True) * inv_d))
            dub = du.astype(BF16)
            du_ref[rs, :] = dub
            c["dy"] = _dot(dub, w_ref[...], NT)
            gout_ref[...] += _dot(c.pop("yb"), dub, TN)

        def backward(c):
            rs, n, dy, zf, sig = c["rs"], c["n"], c["dy"], c["zf"], c["sig"]
            t1 = dy * c["sz"]
            acc_ref[1:2, :] += jnp.sum(t1 * n, axis=0, keepdims=True)
            dn = t1 * gb
            dz_ref[rs, :] = (dy * n * gb * (sig * (1.0 + zf * (1.0 - sig)))).astype(BF16)
            doa = rms_bwd(dn[:, :W_A], n[:, :W_A], c["ra"])
            dob = rms_bwd(dn[:, W_A:W_A + W_B], n[:, W_A:W_A + W_B], c["rb"])
            doc = rms_bwd(dn[:, W_A + W_B:], n[:, W_A + W_B:], c["rc"])
            doa_ref[rs, :] = doa.astype(BF16)
            dla_ref[rs, :] = _dot2(doa * c["oa"], ga8_ref)
            doc_ref[rs, :] = doc.astype(BF16)
            dlc_ref[rs, :] = _dot2(doc * c["oc"], ga4_ref)
            dobn_ref[rs, :] = dob.astype(BF16)
            lsen_ref[rs, :] = c["lse_b"]
            dlbn_ref[rs, :] = _dot2(dob * c["ob"], ga4_ref)
            scr[0, rs, :] = dob[:, :128]
            scr[1, rs, :] = dob[:, 128:]

        halves = [slice(h * (tm // 2), (h + 1) * (tm // 2)) for h in range(2)]
        live = {}
        for t in range(len(halves) + 2):
            if t < len(halves):
                live[t] = forward(halves[t])
            if 0 <= t - 1 < len(halves):
                norm(live[t - 1])
            if 0 <= t - 2 < len(halves):
                backward(live.pop(t - 2))
        for j in range(2):
            sl = slice(128 * j, 128 * (j + 1))
            for res in range(4):
                dob4_ref[0, res, :, sl] = scr[j, pl.ds(res, tm // 4, stride=4), :].astype(BF16)
            for res in range(16):
                dob16_ref[0, res, :, sl] = scr[j, pl.ds(res, tm // 16, stride=16), :].astype(BF16)
        for res in range(4):
            rows = pl.ds(res, tm // 4, stride=4)
            lse4_ref[0, res] = lsen_ref[rows, :]
            dlb4_ref[0, res] = dlbn_ref[rows, :]
        for res in range(16):
            rows = pl.ds(res, tm // 16, stride=16)
            lse16_ref[0, res] = lsen_ref[rows, :]
            dlb16_ref[0, res] = dlbn_ref[rows, :]


    tok = lambda w: pl.BlockSpec((tm, w), lambda i: (i, 0))
    p4 = lambda w: pl.BlockSpec((1, 4, tm // 4, w), lambda i: (i // spt, 0, i % spt, 0))
    p16 = lambda w: pl.BlockSpec((1, 16, tm // 16, w), lambda i: (i // spt, 0, i % spt, 0))
    s4 = lambda w, dt: _sds((B_LOC, 4, SEQ // 4, w), dt)
    s16 = lambda w, dt: _sds((B_LOC, 16, SEQ // 16, w), dt)
    row = _full((1, D_MODEL))
    return pl.pallas_call(
        body, name="middle", grid=(T // tm,),
        in_specs=[tok(W_A), tok(W_B), tok(128), p4(W_B), p4(128), p16(W_B), p16(128), tok(W_C), tok(D_MIX),
                  tok(D_MODEL), tok(D_MODEL), row, row, row, _full((D_MIX, D_MODEL)),
                  _full((128, W_B)), _full((W_B, 128)), _full((W_A, 128))],
        out_specs=(tok(D_MODEL), tok(D_MIX), tok(W_A), tok(128),
                   tok(W_B), tok(128), tok(128), p4(W_B), p4(128), p4(128), p16(W_B), p16(128), p16(128),
                   tok(W_C), tok(128), _full((8, D_MODEL)), _full((D_MIX, D_MODEL))),
        out_shape=(_sds((T, D_MODEL), BF16), _sds((T, D_MIX), BF16),
                   _sds((T, W_A), BF16), _sds((T, 128), F32),
                   _sds((T, W_B), BF16), _sds((T, 128), F32), _sds((T, 128), F32),
                   s4(W_B, BF16), s4(128, F32), s4(128, F32), s16(W_B, BF16), s16(128, F32), s16(128, F32),
                   _sds((T, W_C), BF16), _sds((T, 128), F32), _sds((8, D_MODEL), F32),
                   _sds((D_MIX, D_MODEL), F32)),
        scratch_shapes=[pltpu.VMEM((6, tm, 128), F32)],
        compiler_params=_cp(("arbitrary",), vmem_mb=56),
    )(*_pin(oa, o1, l1, o4, l4, o16, l16, oc, z, x, tgt, g_br, ln_g, ln_b, wout, spread4, gather4, gather8))


class _ReduceScatter:
    def __init__(self, shapes):
        self.shapes = shapes

    def scratch_shapes(self):
        out = []
        for n, w in self.shapes:
            h, p = n // 2, n // 4
            out += [pltpu.VMEM((4, h, w), F32), pltpu.VMEM((4, h, w), F32), pltpu.VMEM((6, p, w), BF16),
                    pltpu.VMEM((6, p, w), BF16), pltpu.VMEM((2, p, w), F32), pltpu.VMEM((h, w), F32)]
        na = len(self.shapes)
        dma = pltpu.SemaphoreType.DMA
        return out + [dma((na, 4)), dma((na, 4)), dma((na, 4)), dma((na, 6)), dma((na, 6)), dma((na,)), dma((na,)),
                      dma((na,))]

    def bind(self, g_refs, r_refs, scratch):
        na = len(self.shapes)
        bufs = [scratch[6 * a:6 * a + 6] for a in range(na)]
        mine, sib, stage, land, keep, tot = (tuple(b[i] for b in bufs) for i in range(6))
        loc_sem, s1_send, s1_recv, s2_send, s2_recv, s3_send, s3_recv, st_sem = scratch[6 * na:6 * na + 8]
        x, y, c = lax.axis_index("x"), lax.axis_index("y"), lax.axis_index("c")
        me, sibling = (x, y, c), (x, y, 1 - c)
        xn, yn, dg = (1 - x, y), (x, 1 - y), (1 - x, 1 - y)
        idx = lambda chip: 2 * chip[0] + chip[1]
        my_chip = idx((x, y))
        order = [idx(xn), idx(dg), idx(yn), my_chip]

        def rows(a, k, half):
            n = self.shapes[a][0]
            return pl.ds(pl.multiple_of(k * n + half * (n // 2), 8), n // 2)

        def piece(a, q):
            p = self.shapes[a][0] // 4
            return slice(q * p, (q + 1) * p)

        def load(a, k):
            return pltpu.make_async_copy(g_refs[a].at[rows(a, k, c), :], mine[a].at[k], loc_sem.at[a, k])

        def s1(a, k, half):
            return pltpu.make_async_remote_copy(
                src_ref=g_refs[a].at[rows(a, k, half), :], dst_ref=sib[a].at[k],
                send_sem=s1_send.at[a, k], recv_sem=s1_recv.at[a, k], device_id=sibling, device_id_type=MESH)

        def s2(a, i, to):
            return pltpu.make_async_remote_copy(
                src_ref=stage[a].at[i], dst_ref=land[a].at[i], send_sem=s2_send.at[a, i], recv_sem=s2_recv.at[a, i],
                device_id=to, device_id_type=MESH)

        via = {0: xn, 1: xn, 2: yn, 3: yn, 4: yn, 5: xn}

        def s3(a, half, to):
            return pltpu.make_async_remote_copy(
                src_ref=tot[a], dst_ref=r_refs[a].at[rows(a, 0, half), :], send_sem=s3_send.at[a],
                recv_sem=s3_recv.at[a], device_id=to, device_id_type=MESH)

        def store(a):
            return pltpu.make_async_copy(tot[a], r_refs[a].at[rows(a, 0, c), :], st_sem.at[a])

        def start():
            for k in order:
                for a in range(na):
                    load(a, k).start()
                    s1(a, k, 1 - c).start()

        def chip_sum(a, k):
            load(a, k).wait()
            s1(a, k, c).wait_recv()
            return mine[a][k] + sib[a][k]

        def exchange():
            for a in range(na):
                P, Q = piece(a, 0), piece(a, 1)
                s_xn = chip_sum(a, idx(xn))
                stage[a][0] = s_xn[P].astype(BF16)
                keep[a][1] = s_xn[Q]
                s_dg = chip_sum(a, idx(dg))
                stage[a][1] = s_dg[P].astype(BF16)
                s2(a, 0, (*xn, c)).start()
                s2(a, 1, (*xn, c)).start()
                stage[a][3] = s_dg[Q].astype(BF16)
                s_yn = chip_sum(a, idx(yn))
                stage[a][2] = s_yn[Q].astype(BF16)
                keep[a][0] = s_yn[P]
                s2(a, 2, (*yn, c)).start()
                s2(a, 3, (*yn, c)).start()
                tot[a][...] = chip_sum(a, my_chip)

        def relay():
            for a in range(na):
                P, Q = piece(a, 0), piece(a, 1)
                s2(a, 1, me).wait_recv()
                stage[a][4] = (keep[a][0] + land[a][1].astype(F32)).astype(BF16)
                s2(a, 4, (*yn, c)).start()
                s2(a, 3, me).wait_recv()
                stage[a][5] = (keep[a][1] + land[a][3].astype(F32)).astype(BF16)
                s2(a, 5, (*xn, c)).start()
                s2(a, 0, me).wait_recv()
                tot[a][P, :] += land[a][0].astype(F32)
                s2(a, 2, me).wait_recv()
                tot[a][Q, :] += land[a][2].astype(F32)

        def finish():
            for a in range(na):
                P, Q = piece(a, 0), piece(a, 1)
                s2(a, 4, me).wait_recv()
                tot[a][P, :] += land[a][4].astype(F32)
                s2(a, 5, me).wait_recv()
                tot[a][Q, :] += land[a][5].astype(F32)
                s3(a, c, sibling).start()
                store(a).start()

        def drain():
            for a in range(na):
                s3(a, 1 - c, me).wait_recv()
                store(a).wait()
            for a in range(na):
                for k in order:
                    s1(a, k, 1 - c).wait_send()
                for i in range(6):
                    s2(a, i, (*via[i], c)).wait_send()
                s3(a, c, sibling).wait_send()

        return start, exchange, relay, finish, drain

    def part(self, grads, steps):
        def body(*refs):
            na = len(self.shapes)
            i = pl.program_id(0)
            for step, phase in zip(steps, self.bind(refs[:na], refs[na:2 * na], refs[2 * na:])):
                pl.when(i == step)(phase)

        hbm = pl.BlockSpec(memory_space=pl.ANY)
        return _Part(body, list(grads), [hbm] * len(grads), [hbm] * len(grads),
                     [_sds((n, w), F32) for n, w in self.shapes], self.scratch_shapes())


def _dh_dx(dqa, dka, dva, dqn, dkn, dvn, dq4, dk4, dv4, dq16, dk16, dv16, dqc, dz, du, xb, cos, sa, sb, winT):
    tm = 512
    spt = SEQ // tm

    def body(dqa_ref, dka_ref, dva_ref, dqn_ref, dkn_ref, dvn_ref, dq4_ref, dk4_ref, dv4_ref,
             dq16_ref, dk16_ref, dv16_ref, dqc_ref, dz_ref, du_ref, xb_ref, cos_ref, sa_ref, sb_ref, w_ref,
             gx_ref, db_ref, gin_ref, dh_ref, scr):
        i = pl.program_id(0)

        @pl.when(i == 0)
        def _():
            db_ref[...] = jnp.zeros_like(db_ref)
            gin_ref[...] = jnp.zeros_like(gin_ref)

        cos_t, sa_t, sb_t = cos_ref[...], sa_ref[...], sb_ref[...]

        def rope_t(t):
            return _rope(t, cos_t, sa_t, sb_t, -1)

        def put(r0, val):
            n = val.shape[1]
            dh_ref[:, r0:r0 + n] = val.astype(BF16)
            db_ref[:, r0:r0 + n] += jnp.sum(val, axis=0, keepdims=True)

        put(O_QA, rope_t(dqa_ref[...].astype(F32)) * QK_SCALE)
        put(O_KA, rope_t(dka_ref[...].astype(F32)))
        put(O_VA, dva_ref[...].astype(F32))
        put(O_QC, dqc_ref[...].astype(F32) * QK_SCALE)
        put(O_Z, dz_ref[...].astype(F32))
        for k, (n_ref, r4, r16) in enumerate(((dqn_ref, dq4_ref, dq16_ref), (dkn_ref, dk4_ref, dk16_ref),
                                               (dvn_ref, dv4_ref, dv16_ref))):
            for j in range(2):
                sl = slice(128 * j, 128 * (j + 1))
                scr[2 * k + j] = n_ref[:, sl].astype(F32)
                for res in range(4):
                    scr[2 * k + j, pl.ds(res, tm // 4, stride=4), :] += r4[0, res, :, sl].astype(F32)
                for res in range(16):
                    scr[2 * k + j, pl.ds(res, tm // 16, stride=16), :] += r16[0, res, :, sl].astype(F32)
        cat = lambda a: jnp.concatenate([scr[a], scr[a + 1]], axis=1)
        put(O_QB, rope_t(cat(0)) * QK_SCALE)
        put(O_KB, rope_t(cat(2)))
        put(O_VB, cat(4))
        gx_ref[...] = _dot(dh_ref[...], w_ref[...], NN) + ALPHA * du_ref[...].astype(F32)
        gin_ref[...] += _dot(dh_ref[...], xb_ref[...], TN)

    tok = lambda w: pl.BlockSpec((tm, w), lambda i: (i, 0))
    tab = pl.BlockSpec((tm, 128), lambda i: (i % spt, 0))
    p4 = pl.BlockSpec((1, 4, tm // 4, W_B), lambda i: (i // spt, 0, i % spt, 0))
    p16 = pl.BlockSpec((1, 16, tm // 16, W_B), lambda i: (i // spt, 0, i % spt, 0))
    once = lambda shape: pl.BlockSpec(shape, lambda i: (0, 0), pipeline_mode=pl.Buffered(1))
    return pl.pallas_call(
        body, name="dh_dx", grid=(T // tm,),
        in_specs=[tok(W_A), tok(W_KV_A), tok(W_KV_A), tok(W_B), tok(W_B), tok(W_B), p4, p4, p4, p16, p16, p16,
                  tok(W_C), tok(D_MIX), tok(D_MODEL), tok(D_MODEL), tab, tab, tab, once((D_IN, D_MODEL))],
        out_specs=(tok(D_MODEL), _full((1, D_IN)), once((D_IN, D_MODEL))),
        out_shape=(_sds((T, D_MODEL), F32), _sds((1, D_IN), F32), _sds((D_IN, D_MODEL), F32)),
        scratch_shapes=[pltpu.VMEM((tm, D_IN), BF16), pltpu.VMEM((6, tm, 128), F32)],
        compiler_params=_cp(("arbitrary",), vmem_mb=56),
    )(*_pin(dqa, dka, dva, dqn, dkn, dvn, dq4, dk4, dv4, dq16, dk16, dv16, dqc, dz, du, xb, cos, sa, sb, winT))


def _reduce_grads(g_in, acc, dbin, dsink):
    rs = _ReduceScatter([(SH_IN, D_MODEL)])

    def body(g_ref, acc_ref, dbin_ref, dsink_ref, r_ref, sv_ref, sv_mine, sv_all, sv_send, sv_recv, *rs_scratch):
        x, y, c = lax.axis_index("x"), lax.axis_index("y"), lax.axis_index("c")
        chips = [(1 - x, y), (x, 1 - y), (1 - x, 1 - y)]
        start, exchange, relay, finish, drain = rs.bind((g_ref,), (r_ref,), rs_scratch)
        start()

        sv_mine[...] = jnp.zeros_like(sv_mine)
        sv_mine[0:4, 0:D_MODEL] = acc_ref[0:4, :]
        sv_mine[4:5, 0:D_IN] = dbin_ref[...]
        sv_mine[5:6, 0:128] = dsink_ref[...]
        my_dev = 4 * x + 2 * y + c
        others = [(x, y, 1 - c)] + [(*chip, cc) for chip in chips for cc in (c, 1 - c)]

        def sv_copy(j, to):
            return pltpu.make_async_remote_copy(
                src_ref=sv_mine, dst_ref=sv_all.at[my_dev], send_sem=sv_send.at[j], recv_sem=sv_recv.at[j],
                device_id=to, device_id_type=MESH)

        sv_sends = [sv_copy(j, to) for j, to in enumerate(others)]
        for cp in sv_sends:
            cp.start()
        exchange()
        relay()
        finish()
        sv_all[my_dev] = sv_mine[...]
        for j in range(7):
            sv_copy(j, (x, y, c)).wait_recv()
        tot = sv_all[0]
        for d in range(1, 8):
            tot = tot + sv_all[d]
        sv_ref[...] = tot
        drain()
        for cp in sv_sends:
            cp.wait_send()

    vm = pl.BlockSpec(memory_space=pltpu.VMEM)
    hbm = pl.BlockSpec(memory_space=pl.ANY)
    return pl.pallas_call(
        body, name="reduce_grads",
        out_shape=(_sds((SH_IN, D_MODEL), F32), _vm_sds((8, SV_W), F32)),
        in_specs=[hbm, vm, vm, vm], out_specs=(hbm, vm),
        scratch_shapes=[pltpu.VMEM((8, SV_W), F32), pltpu.VMEM((8, 8, SV_W), F32),
                        pltpu.SemaphoreType.DMA((7,)), pltpu.SemaphoreType.DMA((7,))] + rs.scratch_shapes(),
        compiler_params=_cp(vmem_mb=40),
    )(pltpu.with_memory_space_constraint(g_in, pltpu.HBM), acc, dbin, dsink)


def _adamw_update(w, g, m, v):
    nm = ADAM_B1 * m + (1.0 - ADAM_B1) * g
    nv = ADAM_B2 * v + (1.0 - ADAM_B2) * (g * g)
    m_hat = nm / (1.0 - ADAM_B1 ** ADAM_STEP)
    v_hat = nv / (1.0 - ADAM_B2 ** ADAM_STEP)
    return -ADAM_LR * (m_hat / (jnp.sqrt(v_hat) + ADAM_EPS) + ADAM_WD * w), nm, nv


def _adamw_big(items, n_steps=4):
    def body(*refs):
        ins, outs = refs[:4 * len(items)], refs[4 * len(items):]
        for p in range(len(items)):
            w_ref, g_ref, m_ref, v_ref = ins[4 * p:4 * p + 4]
            gv = g_ref[...]
            outs[4 * p][...] = gv
            outs[4 * p + 1][...], outs[4 * p + 2][...], outs[4 * p + 3][...] = _adamw_update(
                w_ref[...], gv, m_ref[...], v_ref[...])

    specs, shapes, args = [], [], []
    for w, g, m, v in items:
        rows, width = w.shape
        specs += [pl.BlockSpec((rows // n_steps, width), lambda i: (i, 0))] * 4
        shapes += [_sds((rows, width), F32)] * 4
        args += [w, g, m, v]
    res = pl.pallas_call(
        body, name="adamw_big", grid=(n_steps,), in_specs=specs, out_specs=tuple(specs), out_shape=tuple(shapes),
        compiler_params=_cp(("parallel",), vmem_mb=40),
    )(*_pin(*args))
    return [tuple(res[4 * p:4 * p + 4]) for p in range(len(items))]


def _adamw_small(sv, ws, ms, vs):
    where = ((4, D_IN, 1.0), (5, 8, -1.0), (1, D_MIX, 1.0), (2, D_MODEL, 1.0), (3, D_MODEL, 1.0))

    def body(sv_ref, *refs):
        ins, loss_ref, outs = refs[:15], refs[15], refs[16:]
        loss_ref[...] = jnp.sum(sv_ref[0:1, 0:D_MODEL], axis=1, keepdims=True)
        for p, (row, width, sign) in enumerate(where):
            gv = sign * sv_ref[row:row + 1, 0:width]
            outs[4 * p][...] = gv
            outs[4 * p + 1][...], outs[4 * p + 2][...], outs[4 * p + 3][...] = _adamw_update(
                ins[p][...], gv, ins[5 + p][...], ins[10 + p][...])

    res = pl.pallas_call(
        body, name="adamw_small",
        out_shape=(_vm_sds((1, 1), F32), *[_vm_sds(w.shape, F32) for w in ws for _ in range(4)]),
    )(sv, *ws, *ms, *vs)
    return res[0], [tuple(res[1 + 4 * p:5 + 4 * p]) for p in range(5)]


def _rope_tables():
    pos = np.arange(SEQ, dtype=np.float32)
    inv = (np.float32(ROPE_THETA) ** (-np.arange(0, 64, 2, dtype=np.float32) / np.float32(64))).astype(np.float32)
    ang = np.tile(pos[:, None] * inv[None, :], (1, 4))
    cos, sin = np.cos(ang).astype(np.float32), np.sin(ang).astype(np.float32)
    low = (np.arange(128) % 64) < 32
    zero = np.float32(0.0)
    return jnp.asarray(cos), jnp.asarray(np.where(low, -sin, zero)), jnp.asarray(np.where(low, zero, sin))


def _local_step(x2, mem2, tgt2, winT, wout, wmem, b_in, sinks, g_branch, ln_gain, ln_bias):
    cos, sa, sb = _rope_tables()
    sinkv = jnp.pad(sinks, ((0, 0), (0, 120)))
    head_of_lane = np.arange(512)[:, None] // 64
    gather8 = jnp.asarray(head_of_lane == np.arange(128)[None, :], BF16)
    gather4 = jnp.asarray(head_of_lane[:W_B] == np.arange(128)[None, :], BF16)
    spread4 = jnp.asarray((head_of_lane[:W_B] == np.arange(128)[None, :]).T, BF16)

    xb, qa, ka, va, bn, b4, b16, qc, z, wout, wmem = _in_proj(x2, winT, b_in, cos, sa, sb, wout, wmem)
    memb, mkv = _mem_kv(mem2, wmem)
    b4f, b16f = b4.reshape(T, 768), b16.reshape(T, 768)

    swa = dict(kind="band", nb=SEQ // BLK, max_dist=BLK - 1, gqa=True)
    dil = (dict(kind="band", nb=SEQ // BLK), dict(kind="band", nb=SEQ // 4 // BLK), dict(kind="band", nb=1))
    (oa, lse_a), (o1, l1), (o4, l4), (o16, l16), (oc, lse_c) = _run_parts("attn_fwd", [
        _attn_fwd(qa, 0, W_A, ka, 0, va, 0, W_KV_A, sinks=sinks, **swa),
        _attn_fwd(bn, 0, W_B, bn, 1, bn, 2, W_B, **dil[0]),
        _attn_fwd(b4f, 0, W_B, b4f, 1, b4f, 2, W_B, **dil[1]),
        _attn_fwd(b16f, 0, W_B, b16f, 1, b16f, 2, W_B, **dil[2]),
        _attn_fwd(qc, 0, W_C, mkv, 0, mkv, 1, W_C, kind="mem")], "parallel", 48)

    s4 = lambda w: (B_LOC, 4, SEQ // 4, w)
    s16 = lambda w: (B_LOC, 16, SEQ // 16, w)
    (du, dz, doa, dla, dobn, lsen, dlbn, dob4, lse4, dlb4, dob16, lse16, dlb16, doc, dlc, acc, g_out) = _middle(
        oa, o1, l1, o4.reshape(s4(W_B)), l4.reshape(s4(128)), o16.reshape(s16(W_B)), l16.reshape(s16(128)), oc, z,
        x2, tgt2, g_branch, ln_gain, ln_bias, wout, spread4, gather4, gather8)

    flat = lambda a: a.reshape(T, a.shape[-1])
    (dqa, dka, dva, dsink), (dqc, g_mem) = _run_parts("attn_bwd_a", [
        _attn_bwd(qa, 0, W_A, ka, 0, va, 0, W_KV_A, doa, lse_a, dla, sinkv=sinkv, **swa),
        _attn_bwd(qc, 0, W_C, mkv, 0, mkv, 1, W_C, doc, lse_c, dlc, kind="mem", mem_in=memb)], "arbitrary", 48)
    last = T // QR - 1
    (r_out, r_mem), (dqn, dkn, dvn), (dq4, dk4, dv4), (dq16, dk16, dv16) = _run_parts("attn_bwd_b", [
        _ReduceScatter([(SH_OUT, D_MODEL), (SH_MEM, 2 * W_C)]).part((g_out, g_mem), (0, 1, 2, last, last)),
        _attn_bwd(bn, 0, W_B, bn, 1, bn, 2, W_B, dobn, lsen, dlbn, **dil[0]),
        _attn_bwd(b4f, 0, W_B, b4f, 1, b4f, 2, W_B, flat(dob4), flat(lse4), flat(dlb4), **dil[1]),
        _attn_bwd(b16f, 0, W_B, b16f, 1, b16f, 2, W_B, flat(dob16), flat(lse16), flat(dlb16), **dil[2])],
        "arbitrary", 62)

    r4 = lambda a: a.reshape(s4(W_B))
    r16 = lambda a: a.reshape(s16(W_B))
    gx, dbin, g_in = _dh_dx(dqa, dka, dva, dqn, dkn, dvn, r4(dq4), r4(dk4), r4(dv4), r16(dq16), r16(dk16),
                            r16(dv16), dqc, dz, du, xb, cos, sa, sb, winT)
    return gx, g_in, r_out, r_mem, acc, dbin, dsink


def kernel(x, mem, w_in, b_in, w_mem, attn_sinks, g_branch, w_out, ln_gain, ln_bias, loss_target, m_w_in, m_b_in, m_w_mem, m_attn_sinks, m_g_branch, m_w_out, m_ln_gain, m_ln_bias, v_w_in, v_b_in, v_w_mem, v_attn_sinks, v_g_branch, v_w_out, v_ln_gain, v_ln_bias):
    winT, wout, wmem = _gather_weights(w_in[0].T, w_out[0], w_mem[0])
    gx, g_in, r_out, r_mem, acc, dbin, dsink = _local_step(
        x.reshape(T, D_MODEL), mem.reshape(B_LOC * MEM_LEN, D_MODEL), loss_target.reshape(T, D_MODEL),
        winT, wout, wmem, b_in, attn_sinks, g_branch, ln_gain, ln_bias)
    r_in, sv = _reduce_grads(g_in, acc, dbin, dsink)

    small = ["b_in", "attn_sinks", "g_branch", "ln_gain", "ln_bias"]
    loss, steps = _adamw_small(sv, [b_in, attn_sinks, g_branch, ln_gain, ln_bias],
                               [m_b_in, m_attn_sinks, m_g_branch, m_ln_gain, m_ln_bias],
                               [v_b_in, v_attn_sinks, v_g_branch, v_ln_gain, v_ln_bias])
    out = dict(zip(small, steps))
    big = _adamw_big([(w_in[0].T, r_in, m_w_in[0].T, v_w_in[0].T), (w_out[0], r_out, m_w_out[0], v_w_out[0]),
                      (w_mem[0], r_mem, m_w_mem[0], v_w_mem[0])])
    out["w_in"] = tuple(a.T[None] for a in big[0])
    out["w_out"], out["w_mem"] = (tuple(a[None] for a in st) for st in big[1:])
    names = ["w_in", "b_in", "w_mem", "attn_sinks", "g_branch", "w_out", "ln_gain", "ln_bias"]
    return (loss.reshape(()), gx.reshape(B_LOC, SEQ, D_MODEL), *[out[n][k] for k in range(4) for n in names])
```

```python
import jax
import jax.numpy as jnp
import numpy as np
from jax import lax
from jax.experimental import pallas as pl
from jax.experimental.pallas import tpu as pltpu

F32, BF16 = jnp.float32, jnp.bfloat16

D_MODEL = 1024
SEQ = 2048
B_LOC = 2
T = B_LOC * SEQ
BLK = 128
MEM_LEN = 256
W_A, W_KV_A, W_B, W_C, D_MIX = 512, 128, 256, 256, 1024
D_IN = 2816
O_QA, O_KA, O_VA, O_QB, O_KB, O_VB, O_QC, O_Z = 0, 512, 640, 768, 1024, 1280, 1536, 1792
ROPE_THETA = 10000.0
LN_EPS = 1e-5
RMS_EPS = 1e-6
ALPHA = 2.0 ** 0.25
QK_SCALE = 0.125
N_CHIP = 4
SH_IN, SH_OUT, SH_MEM = D_IN // N_CHIP, D_MIX // N_CHIP, D_MODEL // N_CHIP
NEG = -1e30
ADAM_LR, ADAM_B1, ADAM_B2, ADAM_EPS, ADAM_WD, ADAM_STEP = 0.001, 0.9, 0.999, 1e-08, 0.01, 10
SV_W = 3072
MESH = pl.DeviceIdType.MESH

NN = ((1,), (0,))
NT = ((1,), (1,))
TN = ((0,), (0,))


def _dot(a, b, dims):
    return lax.dot_general(a, b, (dims, ((), ())), preferred_element_type=F32)


def _cp(sem=None, vmem_mb=None):
    kw = {}
    if sem is not None:
        kw["dimension_semantics"] = sem
    if vmem_mb is not None:
        kw["vmem_limit_bytes"] = vmem_mb * 1024 * 1024
    return pltpu.CompilerParams(**kw)


def _sds(shape, dtype):
    return pltpu.HBM(shape, dtype)


def _vm_sds(shape, dtype):
    return jax.ShapeDtypeStruct(shape, dtype)


def _pin(*args):
    return [pltpu.with_memory_space_constraint(a, pltpu.HBM) for a in args]


def _full(shape):
    n = len(shape)
    return pl.BlockSpec(shape, lambda *_: (0,) * n)


def _shard_rows(ref, n, chip, half):
    start = pl.multiple_of((2 * chip[0] + chip[1]) * n + half * (n // 2), 16)
    return ref.at[pl.ds(start, n // 2), :]


def _gather_weights(win_sh, wout_sh, wmem_sh):
    half, piece = SH_IN // 2, SH_IN // 4
    shards = ((SH_IN, D_MODEL), (SH_OUT, D_MODEL), (SH_MEM, 2 * W_C))

    def body(a_ref, b_ref, c_ref, oa_ref, ob_ref, oc_ref, raw_a, raw_b, raw_c, own_a, own_b, own_c,
             load_sem, store_sem, ici_send, ici_recv, d2d_send, d2d_recv):
        x, y, c = lax.axis_index("x"), lax.axis_index("y"), lax.axis_index("c")
        me, sibling = (x, y, c), (x, y, 1 - c)
        xn, yn, dg = (1 - x, y), (x, 1 - y), (1 - x, 1 - y)
        srcs, raws = (a_ref, b_ref, c_ref), (raw_a, raw_b, raw_c)
        owns, outs = (own_a, own_b, own_c), (oa_ref, ob_ref, oc_ref)
        loads = [pltpu.make_async_copy(srcs[a], raws[a], load_sem.at[a]) for a in range(3)]
        for cp in loads:
            cp.start()

        def rows(chip, hf, q):
            start = pl.multiple_of((2 * chip[0] + chip[1]) * SH_IN + hf * half + q * piece, 16)
            return oa_ref.at[pl.ds(start, piece), :]

        def copy(sems, k, chip, hf, q, to, src=None):
            blk = rows(chip, hf, q)
            return pltpu.make_async_remote_copy(
                src_ref=blk if src is None else src, dst_ref=blk, send_sem=sems[0].at[k], recv_sem=sems[1].at[k],
                device_id=to, device_id_type=MESH)

        def my_piece(q):
            return own_a.at[pl.ds(pl.multiple_of(c * half + q * piece, 16), piece), :]

        ici, d2d = (ici_send, ici_recv), (d2d_send, d2d_recv)
        stores, direct = [], []
        for a, (n, _) in enumerate(shards):
            loads[a].wait()
            owns[a][...] = raws[a][...].astype(BF16)
            mine = pl.ds(pl.multiple_of((2 * x + y) * n, 16), n)
            stores.append(pltpu.make_async_copy(owns[a], outs[a].at[mine, :], store_sem.at[a]))
            stores[-1].start()
            if a == 0:
                direct = [copy(ici, 0, (x, y), c, 0, (*xn, c), my_piece(0)),
                          copy(ici, 1, (x, y), c, 1, (*xn, c), my_piece(1)),
                          copy(ici, 3, (x, y), c, 0, (*yn, c), my_piece(0)),
                          copy(ici, 4, (x, y), c, 1, (*yn, c), my_piece(1))]
                for cp in direct:
                    cp.start()
        arrivals = [(0, xn, 0), (1, xn, 1), (3, yn, 0), (4, yn, 1), (2, dg, 1), (5, dg, 0)]
        passed = []
        for k, chip, q in arrivals:
            copy(ici, k, chip, c, q, me).wait_recv()
            if k == 0:
                passed.append(copy(ici, 5, xn, c, 0, (*yn, c)))
                passed[-1].start()
            if k == 4:
                passed.append(copy(ici, 2, yn, c, 1, (*xn, c)))
                passed[-1].start()
            passed.append(copy(d2d, k, chip, c, q, sibling))
            passed[-1].start()
        for k, chip, q in arrivals:
            copy(d2d, k, chip, 1 - c, q, me).wait_recv()
        for cp in direct + passed:
            cp.wait_send()
        for cp in stores:
            cp.wait()

    hbm = pl.BlockSpec(memory_space=pl.ANY)
    return pl.pallas_call(
        body, name="gather_weights",
        out_shape=(_sds((D_IN, D_MODEL), BF16), _sds((D_MIX, D_MODEL), BF16), _sds((D_MODEL, 2 * W_C), BF16)),
        in_specs=[hbm, hbm, hbm], out_specs=(hbm, hbm, hbm),
        scratch_shapes=([pltpu.VMEM(sh, F32) for sh in shards] + [pltpu.VMEM(sh, BF16) for sh in shards]
                        + [pltpu.SemaphoreType.DMA((3,))] * 2 + [pltpu.SemaphoreType.DMA((6,))] * 4),
        compiler_params=_cp(vmem_mb=40),
    )(*_pin(win_sh, wout_sh, wmem_sh))


def _rope(t, cos, sa, sb, sign):
    w = t.shape[1]
    reps = w // 128
    c, a, b = (jnp.tile(v, (1, reps)) if reps > 1 else v for v in (cos, sa, sb))
    rot = pltpu.roll(t, w - 32, 1) * a + pltpu.roll(t, 32, 1) * b
    return t * c + rot if sign > 0 else t * c - rot


def _in_proj(x, winT, b_in, cos, sa, sb, wout_own, wmem_own):
    tm = 512
    spt = SEQ // tm
    n_steps = T // tm
    forward_step = n_steps // 2

    def body(x_ref, w_ref, b_ref, cos_ref, sa_ref, sb_ref, wo_in, wm_in,
             xb_ref, qa_ref, ka_ref, va_ref, bn_ref, b4_ref, b16_ref, qc_ref, z_ref, wo_ref, wm_ref,
             scr, ici_send, ici_recv, d2d_send, d2d_recv):
        i = pl.program_id(0)
        mx, my, mc = lax.axis_index("x"), lax.axis_index("y"), lax.axis_index("c")
        chips = [(1 - mx, my), (mx, 1 - my), (1 - mx, 1 - my)]
        full = ((wo_ref, SH_OUT), (wm_ref, SH_MEM))

        def copy(sems, a, j, chip_of_block, half, to):
            blk = _shard_rows(full[a][0], full[a][1], chip_of_block, half)
            return pltpu.make_async_remote_copy(
                src_ref=blk, dst_ref=blk, send_sem=sems[0].at[a, j], recv_sem=sems[1].at[a, j],
                device_id=to, device_id_type=MESH)

        ici, d2d = (ici_send, ici_recv), (d2d_send, d2d_recv)
        pairs = [(a, j, chip) for j, chip in enumerate(chips) for a in range(2)]

        @pl.when(i == 0)
        def _():
            for a, j, chip in pairs:
                copy(ici, a, j, (mx, my), mc, (*chip, mc)).start()

        @pl.when(i == forward_step)
        def _():
            for a, j, chip in pairs:
                copy(ici, a, j, chip, mc, (mx, my, mc)).wait_recv()
                copy(d2d, a, j, chip, mc, (mx, my, 1 - mc)).start()

        @pl.when(i == n_steps - 1)
        def _():
            for a, j, chip in pairs:
                copy(d2d, a, j, chip, 1 - mc, (mx, my, mc)).wait_recv()
            for a, j, chip in pairs:
                copy(ici, a, j, (mx, my), mc, (*chip, mc)).wait_send()
                copy(d2d, a, j, chip, mc, (mx, my, 1 - mc)).wait_send()

        xb = x_ref[...].astype(BF16)
        xb_ref[...] = xb
        cos_t, sa_t, sb_t = cos_ref[...], sa_ref[...], sb_ref[...]

        def proj(r0, n):
            return _dot(xb, w_ref[r0:r0 + n, :], NT) + b_ref[:, r0:r0 + n]

        def rope(t):
            return _rope(t, cos_t, sa_t, sb_t, +1)

        parts = (rope(proj(O_QB, W_B)) * QK_SCALE, rope(proj(O_KB, W_B)), proj(O_VB, W_B))
        for k, part in enumerate(parts):
            bn_ref[:, 256 * k:256 * (k + 1)] = part.astype(BF16)
            scr[2 * k] = part[:, :128]
            scr[2 * k + 1] = part[:, 128:]
        for j in range(6):
            lanes = slice(128 * j, 128 * (j + 1))
            for res in range(4):
                t = scr[j, pl.ds(res, tm // 4, stride=4), :]
                b4_ref[0, res, :, lanes] = t.astype(BF16)
                scr[6 + j, res * (tm // 4):(res + 1) * (tm // 4), :] = t
            for res in range(16):
                b16_ref[0, res, :, lanes] = scr[6 + j, pl.ds((res % 4) * (tm // 4) + res // 4, tm // 16, stride=4),
                                                :].astype(BF16)
        qa_ref[...] = (rope(proj(O_QA, W_A)) * QK_SCALE).astype(BF16)
        ka_ref[...] = rope(proj(O_KA, W_KV_A)).astype(BF16)
        va_ref[...] = proj(O_VA, W_KV_A).astype(BF16)
        qc_ref[...] = (proj(O_QC, W_C) * QK_SCALE).astype(BF16)
        z_ref[...] = proj(O_Z, D_MIX).astype(BF16)

    tok = lambda w: pl.BlockSpec((tm, w), lambda i: (i, 0))
    tab = pl.BlockSpec((tm, 128), lambda i: (i % spt, 0))
    hbm = pl.BlockSpec(memory_space=pl.ANY)
    return pl.pallas_call(
        body, name="in_proj", grid=(n_steps,),
        in_specs=[tok(D_MODEL), _full((D_IN, D_MODEL)), _full((1, D_IN)), tab, tab, tab, hbm, hbm],
        out_specs=(tok(D_MODEL), tok(W_A), tok(W_KV_A), tok(W_KV_A), tok(768),
                   pl.BlockSpec((1, 4, tm // 4, 768), lambda i: (i // spt, 0, i % spt, 0)),
                   pl.BlockSpec((1, 16, tm // 16, 768), lambda i: (i // spt, 0, i % spt, 0)),
                   tok(W_C), tok(D_MIX), hbm, hbm),
        out_shape=(_sds((T, D_MODEL), BF16), _sds((T, W_A), BF16), _sds((T, W_KV_A), BF16), _sds((T, W_KV_A), BF16),
                   _sds((T, 768), BF16), _sds((B_LOC, 4, SEQ // 4, 768), BF16), _sds((B_LOC, 16, SEQ // 16, 768), BF16),
                   _sds((T, W_C), BF16), _sds((T, D_MIX), BF16),
                   _sds((D_MIX, D_MODEL), BF16), _sds((D_MODEL, 2 * W_C), BF16)),
        input_output_aliases={6: 9, 7: 10},
        scratch_shapes=[pltpu.VMEM((12, tm, 128), F32)] + [pltpu.SemaphoreType.DMA((2, 3))] * 4,
        compiler_params=_cp(("arbitrary",), vmem_mb=48),
    )(*_pin(x, winT, b_in, cos, sa, sb, wout_own, wmem_own))


def _mem_kv(mem, wmem):
    def body(m_ref, w_ref, mb_ref, kv_ref):
        mb = m_ref[...].astype(BF16)
        mb_ref[...] = mb
        kv_ref[...] = _dot(mb, w_ref[...], NN).astype(BF16)

    n = B_LOC * MEM_LEN
    return pl.pallas_call(
        body, name="mem_kv",
        out_shape=(_sds((n, D_MODEL), BF16), _sds((n, 2 * W_C), BF16)),
    )(*_pin(mem, wmem))


class _Part:
    def __init__(self, body, args, in_specs, out_specs, out_shape, scratch=()):
        self.body, self.args, self.in_specs, self.out_specs, self.out_shape = body, args, in_specs, out_specs, out_shape
        self.scratch = list(scratch)


def _run_parts(name, parts, semantics, vmem_mb):
    n_in = [len(p.args) for p in parts]
    n_out = [len(p.out_shape) for p in parts]
    n_scr = [len(p.scratch) for p in parts]

    def body(*refs):
        ins, outs, scr = refs[:sum(n_in)], refs[sum(n_in):sum(n_in) + sum(n_out)], refs[sum(n_in) + sum(n_out):]
        i0 = o0 = s0 = 0
        for p, ni, no, ns in zip(parts, n_in, n_out, n_scr):
            p.body(*ins[i0:i0 + ni], *outs[o0:o0 + no], *scr[s0:s0 + ns])
            i0, o0, s0 = i0 + ni, o0 + no, s0 + ns

    res = pl.pallas_call(
        body, name=name, grid=(T // QR,),
        in_specs=[sp for p in parts for sp in p.in_specs], out_specs=tuple(sp for p in parts for sp in p.out_specs),
        out_shape=tuple(sh for p in parts for sh in p.out_shape),
        scratch_shapes=[sc for p in parts for sc in p.scratch],
        compiler_params=_cp((semantics,), vmem_mb=vmem_mb),
    )(*_pin(*[a for p in parts for a in p.args]))
    out, o0 = [], 0
    for no in n_out:
        out.append(tuple(res[o0:o0 + no]))
        o0 += no
    return out


QB = 8
QR = QB * BLK


def _lane_lo():
    return lax.broadcasted_iota(jnp.int32, (1, 128), 1) < 64


def _dup_head(k2, hk, lo):
    kf = k2.astype(F32)
    r = pltpu.roll(kf, 64, 1)
    return (jnp.where(lo, kf, r) if hk == 0 else jnp.where(lo, r, kf)).astype(BF16)


def _stack_heads(pairs, lo):
    parts = []
    for x2 in pairs:
        z = jnp.zeros_like(x2)
        parts += [jnp.where(lo, x2, z), jnp.where(lo, z, x2)]
    return jnp.concatenate(parts, axis=0)


def _prev_mode(kind, nb, j):
    if kind == "mem" or nb == 1:
        return "no"
    if nb <= QB:
        return "yes" if j % nb else "no"
    return "yes" if j else "dyn"


class _Attn:
    def __init__(self, kind, nb, max_dist, gqa, qw, kvw, qcb, kcb, vcb):
        self.kind, self.nb, self.gqa, self.qw, self.kvw = kind, nb, gqa, qw, kvw
        npairs = qw // 128
        self.groups = ([(hk, [2 * hk, 2 * hk + 1]) for hk in range(npairs // 2)] if gqa
                       else [(p, [p]) for p in range(npairs)])
        self.nh = 2 * len(self.groups[0][1])
        self.cols = 128 * self.nh
        self.reach = BLK - max_dist
        self.ext_prev = kind == "band" and nb > QB
        self.q_spec = pl.BlockSpec((QR, qw), lambda g: (g, qcb))
        self.row_spec = pl.BlockSpec((QR, qw), lambda g: (g, 0))
        self.stat_spec = pl.BlockSpec((QR, 128), lambda g: (g, 0))
        if kind == "mem":
            per = SEQ // QR
            self.kv_specs = [pl.BlockSpec((MEM_LEN, kvw), lambda g: (g // per, kcb)),
                             pl.BlockSpec((MEM_LEN, kvw), lambda g: (g // per, vcb))]
        else:
            self.kv_specs = [pl.BlockSpec((QR, kvw), lambda g: (g, kcb)), pl.BlockSpec((QR, kvw), lambda g: (g, vcb))]
            if self.ext_prev:
                self.kv_specs += [pl.BlockSpec((BLK, kvw), lambda g: (jnp.maximum(g * QB - 1, 0), kcb)),
                                  pl.BlockSpec((BLK, kvw), lambda g: (jnp.maximum(g * QB - 1, 0), vcb))]

    def masks(self):
        if self.kind == "mem":
            return None
        kj = lax.broadcasted_iota(jnp.int32, (2 * BLK, self.cols), 0)
        qi = lax.broadcasted_iota(jnp.int32, (2 * BLK, self.cols), 1) & (BLK - 1)
        kj1 = lax.broadcasted_iota(jnp.int32, (BLK, self.cols), 0)
        qi1 = lax.broadcasted_iota(jnp.int32, (BLK, self.cols), 1) & (BLK - 1)
        return kj, qi, kj1 <= qi1

    def keys(self, j, gi, kc_ref, vc_ref, kp_ref, vp_ref, lo, kq, g):
        def kv(k_ref, v_ref, r):
            if self.gqa:
                return _dup_head(k_ref[r, :], gi, lo), _dup_head(v_ref[r, :], gi, lo)
            sl = slice(128 * gi, 128 * (gi + 1))
            return k_ref[r, sl], v_ref[r, sl]

        if self.kind == "mem":
            key0 = pl.multiple_of((g // (SEQ // QR)) * MEM_LEN, MEM_LEN)
            return (*kv(kc_ref, vc_ref, slice(None)), None, [(0, MEM_LEN, key0)])
        kj, qi, cur = kq
        row0 = g * QR + BLK * j
        mode = _prev_mode(self.kind, self.nb, j)
        if mode == "no":
            return (*kv(kc_ref, vc_ref, slice(BLK * j, BLK * (j + 1))), cur, [(0, BLK, pl.multiple_of(row0, BLK))])
        if mode == "yes":
            mask = jnp.logical_and(kj >= qi + self.reach, kj <= qi + BLK)
            return (*kv(kc_ref, vc_ref, slice(BLK * (j - 1), BLK * (j + 1))), mask,
                    [(0, 2 * BLK, pl.multiple_of(row0 - BLK, BLK))])
        has_prev = ((g * QB) % self.nb) > 0
        hp = has_prev.astype(jnp.int32)
        mask = jnp.logical_and(kj >= qi * hp + (self.reach * hp + BLK * (1 - hp)), kj <= qi + BLK)
        kp, vp = kv(kp_ref, vp_ref, slice(None))
        kc, vc = kv(kc_ref, vc_ref, slice(0, BLK))
        return (jnp.concatenate([kp, kc], axis=0), jnp.concatenate([vp, vc], axis=0), mask,
                [(0, BLK, pl.multiple_of(jnp.maximum(row0 - BLK, 0), BLK)), (BLK, BLK, pl.multiple_of(row0, BLK))])


def _attn_fwd(q, qcb, qw, k, kcb, v, vcb, kvw, *, kind, nb=1, max_dist=BLK, gqa=False, sinks=None):
    a = _Attn(kind, nb, max_dist, gqa, qw, kvw, qcb, kcb, vcb)

    def body(*refs):
        it = iter(refs)
        q_ref, kc_ref, vc_ref = next(it), next(it), next(it)
        kp_ref, vp_ref = (next(it), next(it)) if a.ext_prev else (None, None)
        sink_ref = next(it) if sinks is not None else None
        o_ref, lse_ref = next(it), next(it)
        g = pl.program_id(0)
        lo = _lane_lo()
        top = lax.broadcasted_iota(jnp.int32, (128, 1), 0) < 64
        rid = lax.broadcasted_iota(jnp.int32, (8, 128), 0)
        kq = a.masks()
        stats = {}

        def scores(j, gi, pairs):
            rows = slice(BLK * j, BLK * (j + 1))
            qs = _stack_heads([q_ref[rows, 128 * p:128 * (p + 1)] for p in pairs], lo)
            kk, vv, mask, _ = a.keys(j, gi, kc_ref, vc_ref, kp_ref, vp_ref, lo, kq, g)
            pieces = [slice(r0, r0 + BLK) for r0 in range(0, kk.shape[0], BLK)]
            return dict(j=j, gi=gi, pairs=pairs, rows=rows, vv=vv, mask=mask, pieces=pieces,
                        ss=[_dot(kk[r], qs, NT) for r in pieces])

        def softmax(c):
            gi, mask = c["gi"], c["mask"]
            ss = [s if mask is None else jnp.where(mask[r], s, NEG) for r, s in zip(c["pieces"], c.pop("ss"))]
            m = jnp.max(ss[0], axis=0, keepdims=True)
            for s in ss[1:]:
                m = jnp.maximum(m, jnp.max(s, axis=0, keepdims=True))
            if sink_ref is not None:
                sk = jnp.concatenate([jnp.full((1, 128), sink_ref[0, a.nh * gi + i], F32) for i in range(a.nh)], axis=1)
                m = jnp.maximum(m, sk)
            ps = [jnp.exp(s - m) for s in ss]
            l = sum(jnp.sum(p, axis=0, keepdims=True) for p in ps)
            if sink_ref is not None:
                l = l + jnp.exp(sk - m)
            c["ps"] = [p.astype(BF16) for p in ps]
            c["l"], c["lse"] = l, m + jnp.log(l)

        def outputs(c):
            j, gi, rows = c["j"], c["gi"], c["rows"]
            ot = sum(_dot(c["vv"][r], p, TN) for r, p in zip(c["pieces"], c["ps"]))
            ot = ot * pl.reciprocal(c["l"], approx=True)
            for i, p in enumerate(c["pairs"]):
                o2t = jnp.where(top, ot[:, 256 * i:256 * i + 128], ot[:, 256 * i + 128:256 * i + 256])
                o_ref[rows, 128 * p:128 * (p + 1)] = o2t.T.astype(BF16)
            stat = stats.get(j, jnp.zeros((8, 128), F32))
            for i in range(a.nh):
                stat = jnp.where(rid == a.nh * gi + i, c["lse"][:, 128 * i:128 * (i + 1)], stat)
            stats[j] = stat
            if gi == a.groups[-1][0]:
                lse_ref[rows, :] = jnp.concatenate([stats.pop(j), jnp.zeros((120, 128), F32)], axis=0).T

        chains = [(j, gi, pairs) for j in range(QB) for gi, pairs in a.groups]
        live = {}
        for t in range(len(chains) + 2):
            if t < len(chains):
                live[t] = scores(*chains[t])
            if 0 <= t - 1 < len(chains):
                softmax(live[t - 1])
            if 0 <= t - 2 < len(chains):
                outputs(live.pop(t - 2))


    args = [q, k, v] + ([k, v] if a.ext_prev else [])
    in_specs = [a.q_spec] + a.kv_specs
    if sinks is not None:
        args.append(sinks)
        in_specs.append(pl.BlockSpec(memory_space=pltpu.SMEM))
    return _Part(body, args, in_specs, [a.row_spec, a.stat_spec], [_sds((T, qw), BF16), _sds((T, 128), F32)])


def _attn_bwd(q, qcb, qw, k, kcb, v, vcb, kvw, do, lse, dl, *, kind, nb=1, max_dist=BLK, gqa=False, sinkv=None,
              mem_in=None):
    a = _Attn(kind, nb, max_dist, gqa, qw, kvw, qcb, kcb, vcb)

    def body(*refs):
        it = iter(refs)
        q_ref, kc_ref, vc_ref = next(it), next(it), next(it)
        kp_ref, vp_ref = (next(it), next(it)) if a.ext_prev else (None, None)
        do_ref, lse_ref, dl_ref = next(it), next(it), next(it)
        sinkv_ref = next(it) if sinkv is not None else None
        mem_ref = next(it) if kind == "mem" else None
        dq_ref = next(it)
        if kind == "mem":
            gmem_ref = next(it)
        else:
            dk_out, dv_out = next(it), next(it)
        dsink_ref = next(it) if sinkv is not None else None
        if kind != "mem":
            dk_ref, dv_ref, stage_k, stage_v, flush_sem = next(it), next(it), next(it), next(it), next(it)
        else:
            dkv_ref = next(it)
        g = pl.program_id(0)
        lo = _lane_lo()
        top = lax.broadcasted_iota(jnp.int32, (128, 1), 0) < 64

        @pl.when(g == 0)
        def _():
            if kind == "mem":
                dkv_ref[...] = jnp.zeros_like(dkv_ref)
            else:
                dk_ref[...] = jnp.zeros_like(dk_ref)
                dv_ref[...] = jnp.zeros_like(dv_ref)
            if dsink_ref is not None:
                dsink_ref[...] = jnp.zeros_like(dsink_ref)

        kq = a.masks()
        stats_t = {}

        def first_matmuls(j, gi, pairs):
            rows = slice(BLK * j, BLK * (j + 1))
            if j not in stats_t:
                stats_t[j] = (lse_ref[rows, :].T, dl_ref[rows, :].T)
            lse_t, dl_t = stats_t[j]
            heads = [a.nh * gi + i for i in range(a.nh)]
            c = dict(rows=rows, gi=gi, pairs=pairs)
            c["qs"] = _stack_heads([q_ref[rows, 128 * p:128 * (p + 1)] for p in pairs], lo)
            c["dos"] = _stack_heads([do_ref[rows, 128 * p:128 * (p + 1)] for p in pairs], lo)
            c["lse_row"] = jnp.concatenate([lse_t[h:h + 1, :] for h in heads], axis=1)
            c["dl_row"] = jnp.concatenate([dl_t[h:h + 1, :] for h in heads], axis=1)
            c["kk"], vv, c["mask"], c["dests"] = a.keys(j, gi, kc_ref, vc_ref, kp_ref, vp_ref, lo, kq, g)
            c["s"] = _dot(c["kk"], c["qs"], NT)
            c["dp"] = _dot(vv, c["dos"], NT)
            return c

        def elementwise(c):
            s = c.pop("s")
            if c["mask"] is not None:
                s = jnp.where(c["mask"], s, NEG)
            p = jnp.exp(s - c["lse_row"])
            c["ds"] = (p * (c.pop("dp") - c["dl_row"])).astype(BF16)
            c["p"] = p.astype(BF16)

        def last_matmuls(c):
            gi, rows = c["gi"], c["rows"]
            dqt = _dot(c["kk"], c["ds"], TN)
            ck = _dot(c["ds"], c["qs"], NN)
            cv = _dot(c["p"], c["dos"], NN)
            if gqa:
                sel = lo if gi == 0 else jnp.logical_not(lo)
                ck = jnp.where(sel, ck + pltpu.roll(ck, 64, 1), 0.0)
                cv = jnp.where(sel, cv + pltpu.roll(cv, 64, 1), 0.0)
                kcols = slice(0, 128)
            else:
                kcols = slice(128 * gi, 128 * (gi + 1))
            for r0, nr, key0 in c["dests"]:
                krows = pl.ds(key0, nr)
                if kind == "mem":
                    dkv_ref[krows, kcols] += ck[r0:r0 + nr]
                    dkv_ref[krows, slice(kvw + kcols.start, kvw + kcols.stop)] += cv[r0:r0 + nr]
                else:
                    dk_ref[krows, kcols] += ck[r0:r0 + nr]
                    dv_ref[krows, kcols] += cv[r0:r0 + nr]
            for i, p in enumerate(c["pairs"]):
                dq2t = jnp.where(top, dqt[:, 256 * i:256 * i + 128], dqt[:, 256 * i + 128:256 * i + 256])
                dq_ref[rows, 128 * p:128 * (p + 1)] = dq2t.T.astype(BF16)

        chains = [(j, gi, pairs) for j in range(QB) for gi, pairs in a.groups]
        live = {}
        for t in range(len(chains) + 2):
            if t < len(chains):
                live[t] = first_matmuls(*chains[t])
            if 0 <= t - 1 < len(chains):
                elementwise(live[t - 1])
            if 0 <= t - 2 < len(chains):
                last_matmuls(live.pop(t - 2))
        if dsink_ref is not None:
            ps = jnp.exp(sinkv_ref[...] - lse_ref[...]) * dl_ref[...]
            dsink_ref[...] += jnp.sum(ps, axis=0, keepdims=True)
        if kind == "mem":
            @pl.when(g == T // QR - 1)
            def _():
                gmem_ref[...] = _dot(mem_ref[...], dkv_ref[...].astype(BF16), TN)
        else:
            n_steps = T // QR

            def flush(step):
                rows = pl.ds(pl.multiple_of(step * QR, QR), QR)
                out = []
                for acc, stage, dst, i in ((dk_ref, stage_k, dk_out, 0), (dv_ref, stage_v, dv_out, 1)):
                    stage[...] = acc[rows, :].astype(BF16)
                    out.append(pltpu.make_async_copy(stage, dst.at[rows, :], flush_sem.at[i]))
                return out

            def flushed(step):
                rows = pl.ds(pl.multiple_of(step * QR, QR), QR)
                return [pltpu.make_async_copy(stage, dst.at[rows, :], flush_sem.at[i])
                        for stage, dst, i in ((stage_k, dk_out, 0), (stage_v, dv_out, 1))]

            @pl.when(g >= 2)
            def _():
                for cp in flushed(g - 2):
                    cp.wait()

            @pl.when(g >= 1)
            def _():
                for cp in flush(g - 1):
                    cp.start()

            @pl.when(g == n_steps - 1)
            def _():
                for cp in flushed(g - 1):
                    cp.wait()
                for cp in flush(g):
                    cp.start()
                for cp in flushed(g):
                    cp.wait()

    args = [q, k, v] + ([k, v] if a.ext_prev else []) + [do, lse, dl]
    in_specs = [a.q_spec] + a.kv_specs + [a.row_spec, a.stat_spec, a.stat_spec]
    if sinkv is not None:
        args.append(sinkv)
        in_specs.append(_full((1, 128)))
    if kind == "mem":
        args.append(mem_in)
        in_specs.append(pl.BlockSpec(mem_in.shape, lambda g: (0, 0), pipeline_mode=pl.Buffered(1)))
    out_shape = [_sds((T, qw), BF16)]
    out_specs = [a.row_spec]
    scratch = []
    if kind == "mem":
        out_shape.append(_sds((D_MODEL, 2 * kvw), F32))
        out_specs.append(pl.BlockSpec((D_MODEL, 2 * kvw), lambda g: (0, 0), pipeline_mode=pl.Buffered(1)))
        scratch = [pltpu.VMEM((B_LOC * MEM_LEN, 2 * kvw), F32)]
    else:
        out_shape += [_sds((T, kvw), BF16)] * 2
        out_specs += [pl.BlockSpec(memory_space=pl.ANY)] * 2
        scratch = [pltpu.VMEM((T, kvw), F32)] * 2 + [pltpu.VMEM((QR, kvw), BF16)] * 2 + [pltpu.SemaphoreType.DMA((2,))]
    if sinkv is not None:
        out_shape.append(_sds((1, 128), F32))
        out_specs.append(_full((1, 128)))
    return _Part(body, args, in_specs, out_specs, out_shape, scratch)


def _dot2(v, w_ref):
    hi = v.astype(BF16)
    lo = (v - hi.astype(F32)).astype(BF16)
    return _dot(hi, w_ref[...], NN) + _dot(lo, w_ref[...], NN)


def _middle(oa, o1, l1, o4, l4, o16, l16, oc, z, x, tgt, g_br, ln_g, ln_b, wout, spread4, gather4, gather8):
    tm = 512
    spt = SEQ // tm

    def body(oa_ref, o1_ref, l1_ref, o4_ref, l4_ref, o16_ref, l16_ref, oc_ref, z_ref, x_ref, t_ref,
             g_ref, lg_ref, lb_ref, w_ref, sp4_ref, ga4_ref, ga8_ref,
             du_ref, dz_ref, doa_ref, dla_ref,
             dobn_ref, lsen_ref, dlbn_ref, dob4_ref, lse4_ref, dlb4_ref, dob16_ref, lse16_ref, dlb16_ref,
             doc_ref, dlc_ref, acc_ref, gout_ref, scr):
        i = pl.program_id(0)

        @pl.when(i == 0)
        def _():
            acc_ref[...] = jnp.zeros_like(acc_ref)
            gout_ref[...] = jnp.zeros_like(gout_ref)

        for res in range(4):
            rows = pl.ds(res, tm // 4, stride=4)
            for j in range(2):
                scr[j, rows, :] = o4_ref[0, res, :, 128 * j:128 * (j + 1)].astype(F32)
            scr[2, rows, :] = l4_ref[0, res]
        for res in range(16):
            rows = pl.ds(res, tm // 16, stride=16)
            for j in range(2):
                scr[3 + j, rows, :] = o16_ref[0, res, :, 128 * j:128 * (j + 1)].astype(F32)
            scr[5, rows, :] = l16_ref[0, res]
        inv_d = 1.0 / D_MODEL
        gb, lg, lb = g_ref[...], lg_ref[...], lb_ref[...]

        def rms(o):
            r = lax.rsqrt(jnp.sum(o * o, axis=1, keepdims=True) * (1.0 / o.shape[1]) + RMS_EPS)
            return o * r, r

        def rms_bwd(dn_, n_, r):
            return r * (dn_ - n_ * (jnp.sum(dn_ * n_, axis=1, keepdims=True) * (1.0 / n_.shape[1])))

        def forward(rs):
            o4v = jnp.concatenate([scr[0, rs, :], scr[1, rs, :]], axis=1)
            o16v = jnp.concatenate([scr[3, rs, :], scr[4, rs, :]], axis=1)
            l1v, l4v, l16v = l1_ref[rs, :], scr[2, rs, :], scr[5, rs, :]
            mx = jnp.maximum(jnp.maximum(l1v, l4v), l16v)
            e1, e4, e16 = jnp.exp(l1v - mx), jnp.exp(l4v - mx), jnp.exp(l16v - mx)
            ssum = e1 + e4 + e16
            inv = 1.0 / ssum
            c = dict(rs=rs, lse_b=mx + jnp.log(ssum))
            c["ob"] = (_dot2(e1 * inv, sp4_ref) * o1_ref[rs, :].astype(F32) + _dot2(e4 * inv, sp4_ref) * o4v
                       + _dot2(e16 * inv, sp4_ref) * o16v)
            c["oa"], c["oc"] = oa_ref[rs, :].astype(F32), oc_ref[rs, :].astype(F32)
            na, c["ra"] = rms(c["oa"])
            nb_, c["rb"] = rms(c["ob"])
            nc, c["rc"] = rms(c["oc"])
            c["n"] = jnp.concatenate([na, nb_, nc], axis=1)
            c["zf"] = z_ref[rs, :].astype(F32)
            c["sig"] = 1.0 / (1.0 + jnp.exp(-c["zf"]))
            c["sz"] = c["zf"] * c["sig"]
            c["yb"] = (c["n"] * gb * c["sz"]).astype(BF16)
            c["y2"] = _dot(c["yb"], w_ref[...], NN)
            return c

        def norm(c):
            rs = c["rs"]
            u = ALPHA * x_ref[rs, :] + c.pop("y2")
            mu = jnp.sum(u, axis=1, keepdims=True) * inv_d
            uc = u - mu
            rstd = lax.rsqrt(jnp.sum(uc * uc, axis=1, keepdims=True) * inv_d + LN_EPS)
            xh = uc * rstd
            diff = xh * lg + lb - t_ref[rs, :]
            acc_ref[0:1, :] += jnp.sum(diff * diff, axis=0, keepdims=True) * (0.5 * inv_d)
            dout = diff * inv_d
            acc_ref[2:3, :] += jnp.sum(dout * xh, axis=0, keepdims=True)
            acc_ref[3:4, :] += jnp.sum(dout, axis=0, keepdims=True)
            dxh = dout * lg
            du = rstd * (dxh - jnp.sum(dxh, axis=1, keepdims=True) * inv_d
                         - xh * (jnp.sum(dxh * xh, axis=1, keepdims=True) * inv_d))
            dub = du.astype(BF16)
            du_ref[rs, :] = dub
            c["dy"] = _dot(dub, w_ref[...], NT)
            gout_ref[...] += _dot(c.pop("yb"), dub, TN)

        def backward(c):
            rs, n, dy, zf, sig = c["rs"], c["n"], c["dy"], c["zf"], c["sig"]
            t1 = dy * c["sz"]
            acc_ref[1:2, :] += jnp.sum(t1 * n, axis=0, keepdims=True)
            dn = t1 * gb
            dz_ref[rs, :] = (dy * n * gb * (sig * (1.0 + zf * (1.0 - sig)))).astype(BF16)
            doa = rms_bwd(dn[:, :W_A], n[:, :W_A], c["ra"])
            dob = rms_bwd(dn[:, W_A:W_A + W_B], n[:, W_A:W_A + W_B], c["rb"])
            doc = rms_bwd(dn[:, W_A + W_B:], n[:, W_A + W_B:], c["rc"])
            doa_ref[rs, :] = doa.astype(BF16)
            dla_ref[rs, :] = _dot2(doa * c["oa"], ga8_ref)
            doc_ref[rs, :] = doc.astype(BF16)
            dlc_ref[rs, :] = _dot2(doc * c["oc"], ga4_ref)
            dobn_ref[rs, :] = dob.astype(BF16)
            lsen_ref[rs, :] = c["lse_b"]
            dlbn_ref[rs, :] = _dot2(dob * c["ob"], ga4_ref)
            scr[0, rs, :] = dob[:, :128]
            scr[1, rs, :] = dob[:, 128:]

        halves = [slice(h * (tm // 2), (h + 1) * (tm // 2)) for h in range(2)]
        live = {}
        for t in range(len(halves) + 2):
            if t < len(halves):
                live[t] = forward(halves[t])
            if 0 <= t - 1 < len(halves):
                norm(live[t - 1])
            if 0 <= t - 2 < len(halves):
                backward(live.pop(t - 2))
        for j in range(2):
            sl = slice(128 * j, 128 * (j + 1))
            for res in range(4):
                dob4_ref[0, res, :, sl] = scr[j, pl.ds(res, tm // 4, stride=4), :].astype(BF16)
            for res in range(16):
                dob16_ref[0, res, :, sl] = scr[j, pl.ds(res, tm // 16, stride=16), :].astype(BF16)
        for res in range(4):
            rows = pl.ds(res, tm // 4, stride=4)
            lse4_ref[0, res] = lsen_ref[rows, :]
            dlb4_ref[0, res] = dlbn_ref[rows, :]
        for res in range(16):
            rows = pl.ds(res, tm // 16, stride=16)
            lse16_ref[0, res] = lsen_ref[rows, :]
            dlb16_ref[0, res] = dlbn_ref[rows, :]


    tok = lambda w: pl.BlockSpec((tm, w), lambda i: (i, 0))
    p4 = lambda w: pl.BlockSpec((1, 4, tm // 4, w), lambda i: (i // spt, 0, i % spt, 0))
    p16 = lambda w: pl.BlockSpec((1, 16, tm // 16, w), lambda i: (i // spt, 0, i % spt, 0))
    s4 = lambda w, dt: _sds((B_LOC, 4, SEQ // 4, w), dt)
    s16 = lambda w, dt: _sds((B_LOC, 16, SEQ // 16, w), dt)
    row = _full((1, D_MODEL))
    return pl.pallas_call(
        body, name="middle", grid=(T // tm,),
        in_specs=[tok(W_A), tok(W_B), tok(128), p4(W_B), p4(128), p16(W_B), p16(128), tok(W_C), tok(D_MIX),
                  tok(D_MODEL), tok(D_MODEL), row, row, row, _full((D_MIX, D_MODEL)),
                  _full((128, W_B)), _full((W_B, 128)), _full((W_A, 128))],
        out_specs=(tok(D_MODEL), tok(D_MIX), tok(W_A), tok(128),
                   tok(W_B), tok(128), tok(128), p4(W_B), p4(128), p4(128), p16(W_B), p16(128), p16(128),
                   tok(W_C), tok(128), _full((8, D_MODEL)), _full((D_MIX, D_MODEL))),
        out_shape=(_sds((T, D_MODEL), BF16), _sds((T, D_MIX), BF16),
                   _sds((T, W_A), BF16), _sds((T, 128), F32),
                   _sds((T, W_B), BF16), _sds((T, 128), F32), _sds((T, 128), F32),
                   s4(W_B, BF16), s4(128, F32), s4(128, F32), s16(W_B, BF16), s16(128, F32), s16(128, F32),
                   _sds((T, W_C), BF16), _sds((T, 128), F32), _sds((8, D_MODEL), F32),
                   _sds((D_MIX, D_MODEL), F32)),
        scratch_shapes=[pltpu.VMEM((6, tm, 128), F32)],
        compiler_params=_cp(("arbitrary",), vmem_mb=56),
    )(*_pin(oa, o1, l1, o4, l4, o16, l16, oc, z, x, tgt, g_br, ln_g, ln_b, wout, spread4, gather4, gather8))


class _ReduceScatter:
    def __init__(self, shapes):
        self.shapes = shapes

    def scratch_shapes(self):
        out = []
        for n, w in self.shapes:
            h, p = n // 2, n // 4
            out += [pltpu.VMEM((4, h, w), F32), pltpu.VMEM((4, h, w), F32), pltpu.VMEM((6, p, w), BF16),
                    pltpu.VMEM((6, p, w), BF16), pltpu.VMEM((2, p, w), F32), pltpu.VMEM((h, w), F32)]
        na = len(self.shapes)
        dma = pltpu.SemaphoreType.DMA
        return out + [dma((na, 4)), dma((na, 4)), dma((na, 4)), dma((na, 6)), dma((na, 6)), dma((na,)), dma((na,)),
                      dma((na,))]

    def bind(self, g_refs, r_refs, scratch):
        na = len(self.shapes)
        bufs = [scratch[6 * a:6 * a + 6] for a in range(na)]
        mine, sib, stage, land, keep, tot = (tuple(b[i] for b in bufs) for i in range(6))
        loc_sem, s1_send, s1_recv, s2_send, s2_recv, s3_send, s3_recv, st_sem = scratch[6 * na:6 * na + 8]
        x, y, c = lax.axis_index("x"), lax.axis_index("y"), lax.axis_index("c")
        me, sibling = (x, y, c), (x, y, 1 - c)
        xn, yn, dg = (1 - x, y), (x, 1 - y), (1 - x, 1 - y)
        idx = lambda chip: 2 * chip[0] + chip[1]
        my_chip = idx((x, y))
        order = [idx(xn), idx(dg), idx(yn), my_chip]

        def rows(a, k, half):
            n = self.shapes[a][0]
            return pl.ds(pl.multiple_of(k * n + half * (n // 2), 8), n // 2)

        def piece(a, q):
            p = self.shapes[a][0] // 4
            return slice(q * p, (q + 1) * p)

        def load(a, k):
            return pltpu.make_async_copy(g_refs[a].at[rows(a, k, c), :], mine[a].at[k], loc_sem.at[a, k])

        def s1(a, k, half):
            return pltpu.make_async_remote_copy(
                src_ref=g_refs[a].at[rows(a, k, half), :], dst_ref=sib[a].at[k],
                send_sem=s1_send.at[a, k], recv_sem=s1_recv.at[a, k], device_id=sibling, device_id_type=MESH)

        def s2(a, i, to):
            return pltpu.make_async_remote_copy(
                src_ref=stage[a].at[i], dst_ref=land[a].at[i], send_sem=s2_send.at[a, i], recv_sem=s2_recv.at[a, i],
                device_id=to, device_id_type=MESH)

        via = {0: xn, 1: xn, 2: yn, 3: yn, 4: yn, 5: xn}

        def s3(a, half, to):
            return pltpu.make_async_remote_copy(
                src_ref=tot[a], dst_ref=r_refs[a].at[rows(a, 0, half), :], send_sem=s3_send.at[a],
                recv_sem=s3_recv.at[a], device_id=to, device_id_type=MESH)

        def store(a):
            return pltpu.make_async_copy(tot[a], r_refs[a].at[rows(a, 0, c), :], st_sem.at[a])

        def start():
            for k in order:
                for a in range(na):
                    load(a, k).start()
                    s1(a, k, 1 - c).start()

        def chip_sum(a, k):
            load(a, k).wait()
            s1(a, k, c).wait_recv()
            return mine[a][k] + sib[a][k]

        def exchange():
            for a in range(na):
                P, Q = piece(a, 0), piece(a, 1)
                s_xn = chip_sum(a, idx(xn))
                stage[a][0] = s_xn[P].astype(BF16)
                keep[a][1] = s_xn[Q]
                s_dg = chip_sum(a, idx(dg))
                stage[a][1] = s_dg[P].astype(BF16)
                s2(a, 0, (*xn, c)).start()
                s2(a, 1, (*xn, c)).start()
                stage[a][3] = s_dg[Q].astype(BF16)
                s_yn = chip_sum(a, idx(yn))
                stage[a][2] = s_yn[Q].astype(BF16)
                keep[a][0] = s_yn[P]
                s2(a, 2, (*yn, c)).start()
                s2(a, 3, (*yn, c)).start()
                tot[a][...] = chip_sum(a, my_chip)

        def relay():
            for a in range(na):
                P, Q = piece(a, 0), piece(a, 1)
                s2(a, 1, me).wait_recv()
                stage[a][4] = (keep[a][0] + land[a][1].astype(F32)).astype(BF16)
                s2(a, 4, (*yn, c)).start()
                s2(a, 3, me).wait_recv()
                stage[a][5] = (keep[a][1] + land[a][3].astype(F32)).astype(BF16)
                s2(a, 5, (*xn, c)).start()
                s2(a, 0, me).wait_recv()
                tot[a][P, :] += land[a][0].astype(F32)
                s2(a, 2, me).wait_recv()
                tot[a][Q, :] += land[a][2].astype(F32)

        def finish():
            for a in range(na):
                P, Q = piece(a, 0), piece(a, 1)
                s2(a, 4, me).wait_recv()
                tot[a][P, :] += land[a][4].astype(F32)
                s2(a, 5, me).wait_recv()
                tot[a][Q, :] += land[a][5].astype(F32)
                s3(a, c, sibling).start()
                store(a).start()

        def drain():
            for a in range(na):
                s3(a, 1 - c, me).wait_recv()
                store(a).wait()
            for a in range(na):
                for k in order:
                    s1(a, k, 1 - c).wait_send()
                for i in range(6):
                    s2(a, i, (*via[i], c)).wait_send()
                s3(a, c, sibling).wait_send()

        return start, exchange, relay, finish, drain

    def part(self, grads, steps):
        def body(*refs):
            na = len(self.shapes)
            i = pl.program_id(0)
            for step, phase in zip(steps, self.bind(refs[:na], refs[na:2 * na], refs[2 * na:])):
                pl.when(i == step)(phase)

        hbm = pl.BlockSpec(memory_space=pl.ANY)
        return _Part(body, list(grads), [hbm] * len(grads), [hbm] * len(grads),
                     [_sds((n, w), F32) for n, w in self.shapes], self.scratch_shapes())


def _dh_dx(dqa, dka, dva, dqn, dkn, dvn, dq4, dk4, dv4, dq16, dk16, dv16, dqc, dz, du, xb, cos, sa, sb, winT):
    tm = 512
    spt = SEQ // tm

    def body(dqa_ref, dka_ref, dva_ref, dqn_ref, dkn_ref, dvn_ref, dq4_ref, dk4_ref, dv4_ref,
             dq16_ref, dk16_ref, dv16_ref, dqc_ref, dz_ref, du_ref, xb_ref, cos_ref, sa_ref, sb_ref, w_ref,
             gx_ref, db_ref, gin_ref, dh_ref, scr):
        i = pl.program_id(0)

        @pl.when(i == 0)
        def _():
            db_ref[...] = jnp.zeros_like(db_ref)
            gin_ref[...] = jnp.zeros_like(gin_ref)

        cos_t, sa_t, sb_t = cos_ref[...], sa_ref[...], sb_ref[...]

        def rope_t(t):
            return _rope(t, cos_t, sa_t, sb_t, -1)

        def put(r0, val):
            n = val.shape[1]
            dh_ref[:, r0:r0 + n] = val.astype(BF16)
            db_ref[:, r0:r0 + n] += jnp.sum(val, axis=0, keepdims=True)

        put(O_QA, rope_t(dqa_ref[...].astype(F32)) * QK_SCALE)
        put(O_KA, rope_t(dka_ref[...].astype(F32)))
        put(O_VA, dva_ref[...].astype(F32))
        put(O_QC, dqc_ref[...].astype(F32) * QK_SCALE)
        put(O_Z, dz_ref[...].astype(F32))
        for k, (n_ref, r4, r16) in enumerate(((dqn_ref, dq4_ref, dq16_ref), (dkn_ref, dk4_ref, dk16_ref),
                                               (dvn_ref, dv4_ref, dv16_ref))):
            for j in range(2):
                sl = slice(128 * j, 128 * (j + 1))
                scr[2 * k + j] = n_ref[:, sl].astype(F32)
                for res in range(4):
                    scr[2 * k + j, pl.ds(res, tm // 4, stride=4), :] += r4[0, res, :, sl].astype(F32)
                for res in range(16):
                    scr[2 * k + j, pl.ds(res, tm // 16, stride=16), :] += r16[0, res, :, sl].astype(F32)
        cat = lambda a: jnp.concatenate([scr[a], scr[a + 1]], axis=1)
        put(O_QB, rope_t(cat(0)) * QK_SCALE)
        put(O_KB, rope_t(cat(2)))
        put(O_VB, cat(4))
        gx_ref[...] = _dot(dh_ref[...], w_ref[...], NN) + ALPHA * du_ref[...].astype(F32)
        gin_ref[...] += _dot(dh_ref[...], xb_ref[...], TN)

    tok = lambda w: pl.BlockSpec((tm, w), lambda i: (i, 0))
    tab = pl.BlockSpec((tm, 128), lambda i: (i % spt, 0))
    p4 = pl.BlockSpec((1, 4, tm // 4, W_B), lambda i: (i // spt, 0, i % spt, 0))
    p16 = pl.BlockSpec((1, 16, tm // 16, W_B), lambda i: (i // spt, 0, i % spt, 0))
    once = lambda shape: pl.BlockSpec(shape, lambda i: (0, 0), pipeline_mode=pl.Buffered(1))
    return pl.pallas_call(
        body, name="dh_dx", grid=(T // tm,),
        in_specs=[tok(W_A), tok(W_KV_A), tok(W_KV_A), tok(W_B), tok(W_B), tok(W_B), p4, p4, p4, p16, p16, p16,
                  tok(W_C), tok(D_MIX), tok(D_MODEL), tok(D_MODEL), tab, tab, tab, once((D_IN, D_MODEL))],
        out_specs=(tok(D_MODEL), _full((1, D_IN)), once((D_IN, D_MODEL))),
        out_shape=(_sds((T, D_MODEL), F32), _sds((1, D_IN), F32), _sds((D_IN, D_MODEL), F32)),
        scratch_shapes=[pltpu.VMEM((tm, D_IN), BF16), pltpu.VMEM((6, tm, 128), F32)],
        compiler_params=_cp(("arbitrary",), vmem_mb=56),
    )(*_pin(dqa, dka, dva, dqn, dkn, dvn, dq4, dk4, dv4, dq16, dk16, dv16, dqc, dz, du, xb, cos, sa, sb, winT))


def _reduce_grads(g_in, acc, dbin, dsink):
    rs = _ReduceScatter([(SH_IN, D_MODEL)])

    def body(g_ref, acc_ref, dbin_ref, dsink_ref, r_ref, sv_ref, sv_mine, sv_all, sv_send, sv_recv, *rs_scratch):
        x, y, c = lax.axis_index("x"), lax.axis_index("y"), lax.axis_index("c")
        chips = [(1 - x, y), (x, 1 - y), (1 - x, 1 - y)]
        start, exchange, relay, finish, drain = rs.bind((g_ref,), (r_ref,), rs_scratch)
        start()

        sv_mine[...] = jnp.zeros_like(sv_mine)
        sv_mine[0:4, 0:D_MODEL] = acc_ref[0:4, :]
        sv_mine[4:5, 0:D_IN] = dbin_ref[...]
        sv_mine[5:6, 0:128] = dsink_ref[...]
        my_dev = 4 * x + 2 * y + c
        others = [(x, y, 1 - c)] + [(*chip, cc) for chip in chips for cc in (c, 1 - c)]

        def sv_copy(j, to):
            return pltpu.make_async_remote_copy(
                src_ref=sv_mine, dst_ref=sv_all.at[my_dev], send_sem=sv_send.at[j], recv_sem=sv_recv.at[j],
                device_id=to, device_id_type=MESH)

        sv_sends = [sv_copy(j, to) for j, to in enumerate(others)]
        for cp in sv_sends:
            cp.start()
        exchange()
        relay()
        finish()
        sv_all[my_dev] = sv_mine[...]
        for j in range(7):
            sv_copy(j, (x, y, c)).wait_recv()
        tot = sv_all[0]
        for d in range(1, 8):
            tot = tot + sv_all[d]
        sv_ref[...] = tot
        drain()
        for cp in sv_sends:
            cp.wait_send()

    vm = pl.BlockSpec(memory_space=pltpu.VMEM)
    hbm = pl.BlockSpec(memory_space=pl.ANY)
    return pl.pallas_call(
        body, name="reduce_grads",
        out_shape=(_sds((SH_IN, D_MODEL), F32), _vm_sds((8, SV_W), F32)),
        in_specs=[hbm, vm, vm, vm], out_specs=(hbm, vm),
        scratch_shapes=[pltpu.VMEM((8, SV_W), F32), pltpu.VMEM((8, 8, SV_W), F32),
                        pltpu.SemaphoreType.DMA((7,)), pltpu.SemaphoreType.DMA((7,))] + rs.scratch_shapes(),
        compiler_params=_cp(vmem_mb=40),
    )(pltpu.with_memory_space_constraint(g_in, pltpu.HBM), acc, dbin, dsink)


def _adamw_update(w, g, m, v):
    nm = ADAM_B1 * m + (1.0 - ADAM_B1) * g
    nv = ADAM_B2 * v + (1.0 - ADAM_B2) * (g * g)
    m_hat = nm / (1.0 - ADAM_B1 ** ADAM_STEP)
    v_hat = nv / (1.0 - ADAM_B2 ** ADAM_STEP)
    return -ADAM_LR * (m_hat / (jnp.sqrt(v_hat) + ADAM_EPS) + ADAM_WD * w), nm, nv


SMALL = ((4, D_IN, 1.0), (5, 8, -1.0), (1, D_MIX, 1.0), (2, D_MODEL, 1.0), (3, D_MODEL, 1.0))


def _adamw_all(items, sv, ws, ms, vs, n_steps=4):
    nb, ns = 4 * len(items), len(SMALL)

    def body(*refs):
        ins, sv_ref, small_in = refs[:nb], refs[nb], refs[nb + 1:nb + 1 + 3 * ns]
        outs = refs[nb + 1 + 3 * ns:]
        big_out, loss_ref, small_out = outs[:nb], outs[nb], outs[nb + 1:]
        for p in range(len(items)):
            w_ref, g_ref, m_ref, v_ref = ins[4 * p:4 * p + 4]
            gv = g_ref[...]
            big_out[4 * p][...] = gv
            big_out[4 * p + 1][...], big_out[4 * p + 2][...], big_out[4 * p + 3][...] = _adamw_update(
                w_ref[...], gv, m_ref[...], v_ref[...])

        @pl.when(pl.program_id(0) == 0)
        def _():
            loss_ref[...] = jnp.sum(sv_ref[0:1, 0:D_MODEL], axis=1, keepdims=True)
            for p, (row, width, sign) in enumerate(SMALL):
                gv = sign * sv_ref[row:row + 1, 0:width]
                small_out[4 * p][...] = gv
                small_out[4 * p + 1][...], small_out[4 * p + 2][...], small_out[4 * p + 3][...] = _adamw_update(
                    small_in[p][...], gv, small_in[ns + p][...], small_in[2 * ns + p][...])

    specs, shapes, args = [], [], []
    for w, g, m, v in items:
        rows, width = w.shape
        specs += [pl.BlockSpec((rows // n_steps, width), lambda i: (i, 0))] * 4
        shapes += [_sds((rows, width), F32)] * 4
        args += [w, g, m, v]
    small_args = [*ws, *ms, *vs]
    whole = lambda a: _full(a.shape)
    res = pl.pallas_call(
        body, name="adamw", grid=(n_steps,),
        in_specs=specs + [whole(sv)] + [whole(a) for a in small_args],
        out_specs=tuple(specs + [_full((1, 1))] + [whole(w) for w in ws for _ in range(4)]),
        out_shape=tuple(shapes + [_sds((1, 1), F32)] + [_sds(w.shape, F32) for w in ws for _ in range(4)]),
        compiler_params=_cp(("arbitrary",), vmem_mb=40),
    )(*_pin(*args, sv, *small_args))
    big = [tuple(res[4 * p:4 * p + 4]) for p in range(len(items))]
    return big, res[nb], [tuple(res[nb + 1 + 4 * p:nb + 5 + 4 * p]) for p in range(ns)]


def _rope_tables():
    pos = np.arange(SEQ, dtype=np.float32)
    inv = (np.float32(ROPE_THETA) ** (-np.arange(0, 64, 2, dtype=np.float32) / np.float32(64))).astype(np.float32)
    ang = np.tile(pos[:, None] * inv[None, :], (1, 4))
    cos, sin = np.cos(ang).astype(np.float32), np.sin(ang).astype(np.float32)
    low = (np.arange(128) % 64) < 32
    zero = np.float32(0.0)
    return jnp.asarray(cos), jnp.asarray(np.where(low, -sin, zero)), jnp.asarray(np.where(low, zero, sin))


def _local_step(x2, mem2, tgt2, winT, wout, wmem, b_in, sinks, g_branch, ln_gain, ln_bias):
    cos, sa, sb = _rope_tables()
    sinkv = jnp.pad(sinks, ((0, 0), (0, 120)))
    head_of_lane = np.arange(512)[:, None] // 64
    gather8 = jnp.asarray(head_of_lane == np.arange(128)[None, :], BF16)
    gather4 = jnp.asarray(head_of_lane[:W_B] == np.arange(128)[None, :], BF16)
    spread4 = jnp.asarray((head_of_lane[:W_B] == np.arange(128)[None, :]).T, BF16)

    xb, qa, ka, va, bn, b4, b16, qc, z, wout, wmem = _in_proj(x2, winT, b_in, cos, sa, sb, wout, wmem)
    memb, mkv = _mem_kv(mem2, wmem)
    b4f, b16f = b4.reshape(T, 768), b16.reshape(T, 768)

    swa = dict(kind="band", nb=SEQ // BLK, max_dist=BLK - 1, gqa=True)
    dil = (dict(kind="band", nb=SEQ // BLK), dict(kind="band", nb=SEQ // 4 // BLK), dict(kind="band", nb=1))
    (oa, lse_a), (o1, l1), (o4, l4), (o16, l16), (oc, lse_c) = _run_parts("attn_fwd", [
        _attn_fwd(qa, 0, W_A, ka, 0, va, 0, W_KV_A, sinks=sinks, **swa),
        _attn_fwd(bn, 0, W_B, bn, 1, bn, 2, W_B, **dil[0]),
        _attn_fwd(b4f, 0, W_B, b4f, 1, b4f, 2, W_B, **dil[1]),
        _attn_fwd(b16f, 0, W_B, b16f, 1, b16f, 2, W_B, **dil[2]),
        _attn_fwd(qc, 0, W_C, mkv, 0, mkv, 1, W_C, kind="mem")], "parallel", 48)

    s4 = lambda w: (B_LOC, 4, SEQ // 4, w)
    s16 = lambda w: (B_LOC, 16, SEQ // 16, w)
    (du, dz, doa, dla, dobn, lsen, dlbn, dob4, lse4, dlb4, dob16, lse16, dlb16, doc, dlc, acc, g_out) = _middle(
        oa, o1, l1, o4.reshape(s4(W_B)), l4.reshape(s4(128)), o16.reshape(s16(W_B)), l16.reshape(s16(128)), oc, z,
        x2, tgt2, g_branch, ln_gain, ln_bias, wout, spread4, gather4, gather8)

    flat = lambda a: a.reshape(T, a.shape[-1])
    (dqa, dka, dva, dsink), (dqc, g_mem) = _run_parts("attn_bwd_a", [
        _attn_bwd(qa, 0, W_A, ka, 0, va, 0, W_KV_A, doa, lse_a, dla, sinkv=sinkv, **swa),
        _attn_bwd(qc, 0, W_C, mkv, 0, mkv, 1, W_C, doc, lse_c, dlc, kind="mem", mem_in=memb)], "arbitrary", 48)
    last = T // QR - 1
    (r_out, r_mem), (dqn, dkn, dvn), (dq4, dk4, dv4), (dq16, dk16, dv16) = _run_parts("attn_bwd_b", [
        _ReduceScatter([(SH_OUT, D_MODEL), (SH_MEM, 2 * W_C)]).part((g_out, g_mem), (0, 1, 2, last, last)),
        _attn_bwd(bn, 0, W_B, bn, 1, bn, 2, W_B, dobn, lsen, dlbn, **dil[0]),
        _attn_bwd(b4f, 0, W_B, b4f, 1, b4f, 2, W_B, flat(dob4), flat(lse4), flat(dlb4), **dil[1]),
        _attn_bwd(b16f, 0, W_B, b16f, 1, b16f, 2, W_B, flat(dob16), flat(lse16), flat(dlb16), **dil[2])],
        "arbitrary", 62)

    r4 = lambda a: a.reshape(s4(W_B))
    r16 = lambda a: a.reshape(s16(W_B))
    gx, dbin, g_in = _dh_dx(dqa, dka, dva, dqn, dkn, dvn, r4(dq4), r4(dk4), r4(dv4), r16(dq16), r16(dk16),
                            r16(dv16), dqc, dz, du, xb, cos, sa, sb, winT)
    return gx, g_in, r_out, r_mem, acc, dbin, dsink


def kernel(x, mem, w_in, b_in, w_mem, attn_sinks, g_branch, w_out, ln_gain, ln_bias, loss_target, m_w_in, m_b_in, m_w_mem, m_attn_sinks, m_g_branch, m_w_out, m_ln_gain, m_ln_bias, v_w_in, v_b_in, v_w_mem, v_attn_sinks, v_g_branch, v_w_out, v_ln_gain, v_ln_bias):
    winT, wout, wmem = _gather_weights(w_in[0].T, w_out[0], w_mem[0])
    gx, g_in, r_out, r_mem, acc, dbin, dsink = _local_step(
        x.reshape(T, D_MODEL), mem.reshape(B_LOC * MEM_LEN, D_MODEL), loss_target.reshape(T, D_MODEL),
        winT, wout, wmem, b_in, attn_sinks, g_branch, ln_gain, ln_bias)
    r_in, sv = _reduce_grads(g_in, acc, dbin, dsink)

    small = ["b_in", "attn_sinks", "g_branch", "ln_gain", "ln_bias"]
    big, loss, steps = _adamw_all(
        [(w_in[0].T, r_in, m_w_in[0].T, v_w_in[0].T), (w_out[0], r_out, m_w_out[0], v_w_out[0]),
         (w_mem[0], r_mem, m_w_mem[0], v_w_mem[0])],
        sv, [b_in, attn_sinks, g_branch, ln_gain, ln_bias], [m_b_in, m_attn_sinks, m_g_branch, m_ln_gain, m_ln_bias],
        [v_b_in, v_attn_sinks, v_g_branch, v_ln_gain, v_ln_bias])
    out = dict(zip(small, steps))
    out["w_in"] = tuple(a.T[None] for a in big[0])
    out["w_out"], out["w_mem"] = (tuple(a[None] for a in st) for st in big[1:])
    names = ["w_in", "b_in", "w_mem", "attn_sinks", "g_branch", "w_out", "ln_gain", "ln_bias"]
    return (loss.reshape(()), gx.reshape(B_LOC, SEQ, D_MODEL), *[out[n][k] for k in range(4) for n in names])
```

```python
import jax
import jax.numpy as jnp
import numpy as np
from jax import lax
from jax.experimental import pallas as pl
from jax.experimental.pallas import tpu as pltpu

F32, BF16 = jnp.float32, jnp.bfloat16

D_MODEL = 1024
SEQ = 2048
B_LOC = 2
T = B_LOC * SEQ
BLK = 128
MEM_LEN = 256
W_A, W_KV_A, W_B, W_C, D_MIX = 512, 128, 256, 256, 1024
D_IN = 2816
O_QA, O_KA, O_VA, O_QB, O_KB, O_VB, O_QC, O_Z = 0, 512, 640, 768, 1024, 1280, 1536, 1792
ROPE_THETA = 10000.0
LN_EPS = 1e-5
RMS_EPS = 1e-6
ALPHA = 2.0 ** 0.25
QK_SCALE = 0.125
N_CHIP = 4
SH_IN, SH_OUT, SH_MEM = D_IN // N_CHIP, D_MIX // N_CHIP, D_MODEL // N_CHIP
NEG = -1e30
ADAM_LR, ADAM_B1, ADAM_B2, ADAM_EPS, ADAM_WD, ADAM_STEP = 0.001, 0.9, 0.999, 1e-08, 0.01, 10
SV_W = 3072
MESH = pl.DeviceIdType.MESH

NN = ((1,), (0,))
NT = ((1,), (1,))
TN = ((0,), (0,))


def _dot(a, b, dims):
    return lax.dot_general(a, b, (dims, ((), ())), preferred_element_type=F32)


def _cp(sem=None, vmem_mb=None):
    kw = {}
    if sem is not None:
        kw["dimension_semantics"] = sem
    if vmem_mb is not None:
        kw["vmem_limit_bytes"] = vmem_mb * 1024 * 1024
    return pltpu.CompilerParams(**kw)


def _sds(shape, dtype):
    return pltpu.HBM(shape, dtype)


def _vm_sds(shape, dtype):
    return jax.ShapeDtypeStruct(shape, dtype)


def _pin(*args):
    return [pltpu.with_memory_space_constraint(a, pltpu.HBM) for a in args]


def _full(shape):
    n = len(shape)
    return pl.BlockSpec(shape, lambda *_: (0,) * n)


def _shard_rows(ref, n, chip, half):
    start = pl.multiple_of((2 * chip[0] + chip[1]) * n + half * (n // 2), 16)
    return ref.at[pl.ds(start, n // 2), :]


def _gather_weights(win_sh, wout_sh, wmem_sh):
    half, piece = SH_IN // 2, SH_IN // 4
    shards = ((SH_IN, D_MODEL), (SH_OUT, D_MODEL), (SH_MEM, 2 * W_C))

    def body(a_ref, b_ref, c_ref, oa_ref, ob_ref, oc_ref, raw_a, raw_b, raw_c, own_a, own_b, own_c,
             load_sem, store_sem, ici_send, ici_recv, d2d_send, d2d_recv):
        x, y, c = lax.axis_index("x"), lax.axis_index("y"), lax.axis_index("c")
        me, sibling = (x, y, c), (x, y, 1 - c)
        xn, yn, dg = (1 - x, y), (x, 1 - y), (1 - x, 1 - y)
        srcs, raws = (a_ref, b_ref, c_ref), (raw_a, raw_b, raw_c)
        owns, outs = (own_a, own_b, own_c), (oa_ref, ob_ref, oc_ref)
        loads = [pltpu.make_async_copy(srcs[a], raws[a], load_sem.at[a]) for a in range(3)]
        for cp in loads:
            cp.start()

        def rows(chip, hf, q):
            start = pl.multiple_of((2 * chip[0] + chip[1]) * SH_IN + hf * half + q * piece, 16)
            return oa_ref.at[pl.ds(start, piece), :]

        def copy(sems, k, chip, hf, q, to, src=None):
            blk = rows(chip, hf, q)
            return pltpu.make_async_remote_copy(
                src_ref=blk if src is None else src, dst_ref=blk, send_sem=sems[0].at[k], recv_sem=sems[1].at[k],
                device_id=to, device_id_type=MESH)

        def my_piece(q):
            return own_a.at[pl.ds(pl.multiple_of(c * half + q * piece, 16), piece), :]

        ici, d2d = (ici_send, ici_recv), (d2d_send, d2d_recv)
        stores, direct = [], []
        for a, (n, _) in enumerate(shards):
            loads[a].wait()
            owns[a][...] = raws[a][...].astype(BF16)
            mine = pl.ds(pl.multiple_of((2 * x + y) * n, 16), n)
            stores.append(pltpu.make_async_copy(owns[a], outs[a].at[mine, :], store_sem.at[a]))
            stores[-1].start()
            if a == 0:
                direct = [copy(ici, 0, (x, y), c, 0, (*xn, c), my_piece(0)),
                          copy(ici, 1, (x, y), c, 1, (*xn, c), my_piece(1)),
                          copy(ici, 3, (x, y), c, 0, (*yn, c), my_piece(0)),
                          copy(ici, 4, (x, y), c, 1, (*yn, c), my_piece(1))]
                for cp in direct:
                    cp.start()
        arrivals = [(0, xn, 0), (1, xn, 1), (3, yn, 0), (4, yn, 1), (2, dg, 1), (5, dg, 0)]
        passed = []
        for k, chip, q in arrivals:
            copy(ici, k, chip, c, q, me).wait_recv()
            if k == 0:
                passed.append(copy(ici, 5, xn, c, 0, (*yn, c)))
                passed[-1].start()
            if k == 4:
                passed.append(copy(ici, 2, yn, c, 1, (*xn, c)))
                passed[-1].start()
            passed.append(copy(d2d, k, chip, c, q, sibling))
            passed[-1].start()
        for k, chip, q in arrivals:
            copy(d2d, k, chip, 1 - c, q, me).wait_recv()
        for cp in direct + passed:
            cp.wait_send()
        for cp in stores:
            cp.wait()

    hbm = pl.BlockSpec(memory_space=pl.ANY)
    return pl.pallas_call(
        body, name="gather_weights",
        out_shape=(_sds((D_IN, D_MODEL), BF16), _sds((D_MIX, D_MODEL), BF16), _sds((D_MODEL, 2 * W_C), BF16)),
        in_specs=[hbm, hbm, hbm], out_specs=(hbm, hbm, hbm),
        scratch_shapes=([pltpu.VMEM(sh, F32) for sh in shards] + [pltpu.VMEM(sh, BF16) for sh in shards]
                        + [pltpu.SemaphoreType.DMA((3,))] * 2 + [pltpu.SemaphoreType.DMA((6,))] * 4),
        compiler_params=_cp(vmem_mb=40),
    )(*_pin(win_sh, wout_sh, wmem_sh))


def _rope(t, cos, sa, sb, sign):
    w = t.shape[1]
    reps = w // 128
    c, a, b = (jnp.tile(v, (1, reps)) if reps > 1 else v for v in (cos, sa, sb))
    rot = pltpu.roll(t, w - 32, 1) * a + pltpu.roll(t, 32, 1) * b
    return t * c + rot if sign > 0 else t * c - rot


def _in_proj(x, winT, b_in, cos, sa, sb, wout_own, wmem_own):
    tm = 512
    spt = SEQ // tm
    n_steps = T // tm
    forward_step = n_steps // 2

    def body(x_ref, w_ref, b_ref, cos_ref, sa_ref, sb_ref, wo_in, wm_in,
             xb_ref, qa_ref, ka_ref, va_ref, bn_ref, b4_ref, b16_ref, qc_ref, z_ref, wo_ref, wm_ref,
             scr, ici_send, ici_recv, d2d_send, d2d_recv):
        i = pl.program_id(0)
        mx, my, mc = lax.axis_index("x"), lax.axis_index("y"), lax.axis_index("c")
        chips = [(1 - mx, my), (mx, 1 - my), (1 - mx, 1 - my)]
        full = ((wo_ref, SH_OUT), (wm_ref, SH_MEM))

        def copy(sems, a, j, chip_of_block, half, to):
            blk = _shard_rows(full[a][0], full[a][1], chip_of_block, half)
            return pltpu.make_async_remote_copy(
                src_ref=blk, dst_ref=blk, send_sem=sems[0].at[a, j], recv_sem=sems[1].at[a, j],
                device_id=to, device_id_type=MESH)

        ici, d2d = (ici_send, ici_recv), (d2d_send, d2d_recv)
        pairs = [(a, j, chip) for j, chip in enumerate(chips) for a in range(2)]

        @pl.when(i == 0)
        def _():
            for a, j, chip in pairs:
                copy(ici, a, j, (mx, my), mc, (*chip, mc)).start()

        @pl.when(i == forward_step)
        def _():
            for a, j, chip in pairs:
                copy(ici, a, j, chip, mc, (mx, my, mc)).wait_recv()
                copy(d2d, a, j, chip, mc, (mx, my, 1 - mc)).start()

        @pl.when(i == n_steps - 1)
        def _():
            for a, j, chip in pairs:
                copy(d2d, a, j, chip, 1 - mc, (mx, my, mc)).wait_recv()
            for a, j, chip in pairs:
                copy(ici, a, j, (mx, my), mc, (*chip, mc)).wait_send()
                copy(d2d, a, j, chip, mc, (mx, my, 1 - mc)).wait_send()

        xb = x_ref[...].astype(BF16)
        xb_ref[...] = xb
        cos_t, sa_t, sb_t = cos_ref[...], sa_ref[...], sb_ref[...]

        def proj(r0, n):
            return _dot(xb, w_ref[r0:r0 + n, :], NT) + b_ref[:, r0:r0 + n]

        def rope(t):
            return _rope(t, cos_t, sa_t, sb_t, +1)

        parts = (rope(proj(O_QB, W_B)) * QK_SCALE, rope(proj(O_KB, W_B)), proj(O_VB, W_B))
        for k, part in enumerate(parts):
            bn_ref[:, 256 * k:256 * (k + 1)] = part.astype(BF16)
            scr[2 * k] = part[:, :128]
            scr[2 * k + 1] = part[:, 128:]
        for j in range(6):
            lanes = slice(128 * j, 128 * (j + 1))
            for res in range(4):
                t = scr[j, pl.ds(res, tm // 4, stride=4), :]
                b4_ref[0, res, :, lanes] = t.astype(BF16)
                scr[6 + j, res * (tm // 4):(res + 1) * (tm // 4), :] = t
            for res in range(16):
                b16_ref[0, res, :, lanes] = scr[6 + j, pl.ds((res % 4) * (tm // 4) + res // 4, tm // 16, stride=4),
                                                :].astype(BF16)
        qa_ref[...] = (rope(proj(O_QA, W_A)) * QK_SCALE).astype(BF16)
        ka_ref[...] = rope(proj(O_KA, W_KV_A)).astype(BF16)
        va_ref[...] = proj(O_VA, W_KV_A).astype(BF16)
        qc_ref[...] = (proj(O_QC, W_C) * QK_SCALE).astype(BF16)
        z_ref[...] = proj(O_Z, D_MIX).astype(BF16)

    tok = lambda w: pl.BlockSpec((tm, w), lambda i: (i, 0))
    tab = pl.BlockSpec((tm, 128), lambda i: (i % spt, 0))
    hbm = pl.BlockSpec(memory_space=pl.ANY)
    return pl.pallas_call(
        body, name="in_proj", grid=(n_steps,),
        in_specs=[tok(D_MODEL), _full((D_IN, D_MODEL)), _full((1, D_IN)), tab, tab, tab, hbm, hbm],
        out_specs=(tok(D_MODEL), tok(W_A), tok(W_KV_A), tok(W_KV_A), tok(768),
                   pl.BlockSpec((1, 4, tm // 4, 768), lambda i: (i // spt, 0, i % spt, 0)),
                   pl.BlockSpec((1, 16, tm // 16, 768), lambda i: (i // spt, 0, i % spt, 0)),
                   tok(W_C), tok(D_MIX), hbm, hbm),
        out_shape=(_sds((T, D_MODEL), BF16), _sds((T, W_A), BF16), _sds((T, W_KV_A), BF16), _sds((T, W_KV_A), BF16),
                   _sds((T, 768), BF16), _sds((B_LOC, 4, SEQ // 4, 768), BF16), _sds((B_LOC, 16, SEQ // 16, 768), BF16),
                   _sds((T, W_C), BF16), _sds((T, D_MIX), BF16),
                   _sds((D_MIX, D_MODEL), BF16), _sds((D_MODEL, 2 * W_C), BF16)),
        input_output_aliases={6: 9, 7: 10},
        scratch_shapes=[pltpu.VMEM((12, tm, 128), F32)] + [pltpu.SemaphoreType.DMA((2, 3))] * 4,
        compiler_params=_cp(("arbitrary",), vmem_mb=48),
    )(*_pin(x, winT, b_in, cos, sa, sb, wout_own, wmem_own))


def _mem_kv(mem, wmem):
    def body(m_ref, w_ref, mb_ref, kv_ref):
        mb = m_ref[...].astype(BF16)
        mb_ref[...] = mb
        kv_ref[...] = _dot(mb, w_ref[...], NN).astype(BF16)

    n = B_LOC * MEM_LEN
    return pl.pallas_call(
        body, name="mem_kv",
        out_shape=(_sds((n, D_MODEL), BF16), _sds((n, 2 * W_C), BF16)),
    )(*_pin(mem, wmem))


class _Part:
    def __init__(self, body, args, in_specs, out_specs, out_shape, scratch=()):
        self.body, self.args, self.in_specs, self.out_specs, self.out_shape = body, args, in_specs, out_specs, out_shape
        self.scratch = list(scratch)


def _run_parts(name, parts, semantics, vmem_mb):
    n_in = [len(p.args) for p in parts]
    n_out = [len(p.out_shape) for p in parts]
    n_scr = [len(p.scratch) for p in parts]

    def body(*refs):
        ins, outs, scr = refs[:sum(n_in)], refs[sum(n_in):sum(n_in) + sum(n_out)], refs[sum(n_in) + sum(n_out):]
        i0 = o0 = s0 = 0
        for p, ni, no, ns in zip(parts, n_in, n_out, n_scr):
            p.body(*ins[i0:i0 + ni], *outs[o0:o0 + no], *scr[s0:s0 + ns])
            i0, o0, s0 = i0 + ni, o0 + no, s0 + ns

    res = pl.pallas_call(
        body, name=name, grid=(T // QR,),
        in_specs=[sp for p in parts for sp in p.in_specs], out_specs=tuple(sp for p in parts for sp in p.out_specs),
        out_shape=tuple(sh for p in parts for sh in p.out_shape),
        scratch_shapes=[sc for p in parts for sc in p.scratch],
        compiler_params=_cp((semantics,), vmem_mb=vmem_mb),
    )(*_pin(*[a for p in parts for a in p.args]))
    out, o0 = [], 0
    for no in n_out:
        out.append(tuple(res[o0:o0 + no]))
        o0 += no
    return out


QB = 8
QR = QB * BLK


def _lane_lo():
    return lax.broadcasted_iota(jnp.int32, (1, 128), 1) < 64


def _dup_head(k2, hk, lo):
    kf = k2.astype(F32)
    r = pltpu.roll(kf, 64, 1)
    return (jnp.where(lo, kf, r) if hk == 0 else jnp.where(lo, r, kf)).astype(BF16)


def _stack_heads(pairs, lo):
    parts = []
    for x2 in pairs:
        z = jnp.zeros_like(x2)
        parts += [jnp.where(lo, x2, z), jnp.where(lo, z, x2)]
    return jnp.concatenate(parts, axis=0)


def _prev_mode(kind, nb, j):
    if kind == "mem" or nb == 1:
        return "no"
    if nb <= QB:
        return "yes" if j % nb else "no"
    return "yes" if j else "dyn"


class _Attn:
    def __init__(self, kind, nb, max_dist, gqa, qw, kvw, qcb, kcb, vcb):
        self.kind, self.nb, self.gqa, self.qw, self.kvw = kind, nb, gqa, qw, kvw
        npairs = qw // 128
        self.groups = ([(hk, [2 * hk, 2 * hk + 1]) for hk in range(npairs // 2)] if gqa
                       else [(p, [p]) for p in range(npairs)])
        self.nh = 2 * len(self.groups[0][1])
        self.cols = 128 * self.nh
        self.reach = BLK - max_dist
        self.ext_prev = kind == "band" and nb > QB
        self.q_spec = pl.BlockSpec((QR, qw), lambda g: (g, qcb))
        self.row_spec = pl.BlockSpec((QR, qw), lambda g: (g, 0))
        self.stat_spec = pl.BlockSpec((QR, 128), lambda g: (g, 0))
        if kind == "mem":
            per = SEQ // QR
            self.kv_specs = [pl.BlockSpec((MEM_LEN, kvw), lambda g: (g // per, kcb)),
                             pl.BlockSpec((MEM_LEN, kvw), lambda g: (g // per, vcb))]
        else:
            self.kv_specs = [pl.BlockSpec((QR, kvw), lambda g: (g, kcb)), pl.BlockSpec((QR, kvw), lambda g: (g, vcb))]
            if self.ext_prev:
                self.kv_specs += [pl.BlockSpec((BLK, kvw), lambda g: (jnp.maximum(g * QB - 1, 0), kcb)),
                                  pl.BlockSpec((BLK, kvw), lambda g: (jnp.maximum(g * QB - 1, 0), vcb))]

    def masks(self):
        if self.kind == "mem":
            return None
        kj = lax.broadcasted_iota(jnp.int32, (2 * BLK, self.cols), 0)
        qi = lax.broadcasted_iota(jnp.int32, (2 * BLK, self.cols), 1) & (BLK - 1)
        kj1 = lax.broadcasted_iota(jnp.int32, (BLK, self.cols), 0)
        qi1 = lax.broadcasted_iota(jnp.int32, (BLK, self.cols), 1) & (BLK - 1)
        return kj, qi, kj1 <= qi1

    def keys(self, j, gi, kc_ref, vc_ref, kp_ref, vp_ref, lo, kq, g):
        def kv(k_ref, v_ref, r):
            if self.gqa:
                return _dup_head(k_ref[r, :], gi, lo), _dup_head(v_ref[r, :], gi, lo)
            sl = slice(128 * gi, 128 * (gi + 1))
            return k_ref[r, sl], v_ref[r, sl]

        if self.kind == "mem":
            key0 = pl.multiple_of((g // (SEQ // QR)) * MEM_LEN, MEM_LEN)
            return (*kv(kc_ref, vc_ref, slice(None)), None, [(0, MEM_LEN, key0)])
        kj, qi, cur = kq
        row0 = g * QR + BLK * j
        mode = _prev_mode(self.kind, self.nb, j)
        if mode == "no":
            return (*kv(kc_ref, vc_ref, slice(BLK * j, BLK * (j + 1))), cur, [(0, BLK, pl.multiple_of(row0, BLK))])
        if mode == "yes":
            mask = jnp.logical_and(kj >= qi + self.reach, kj <= qi + BLK)
            return (*kv(kc_ref, vc_ref, slice(BLK * (j - 1), BLK * (j + 1))), mask,
                    [(0, 2 * BLK, pl.multiple_of(row0 - BLK, BLK))])
        has_prev = ((g * QB) % self.nb) > 0
        hp = has_prev.astype(jnp.int32)
        mask = jnp.logical_and(kj >= qi * hp + (self.reach * hp + BLK * (1 - hp)), kj <= qi + BLK)
        kp, vp = kv(kp_ref, vp_ref, slice(None))
        kc, vc = kv(kc_ref, vc_ref, slice(0, BLK))
        return (jnp.concatenate([kp, kc], axis=0), jnp.concatenate([vp, vc], axis=0), mask,
                [(0, BLK, pl.multiple_of(jnp.maximum(row0 - BLK, 0), BLK)), (BLK, BLK, pl.multiple_of(row0, BLK))])


def _attn_fwd(q, qcb, qw, k, kcb, v, vcb, kvw, *, kind, nb=1, max_dist=BLK, gqa=False, sinks=None):
    a = _Attn(kind, nb, max_dist, gqa, qw, kvw, qcb, kcb, vcb)

    def body(*refs):
        it = iter(refs)
        q_ref, kc_ref, vc_ref = next(it), next(it), next(it)
        kp_ref, vp_ref = (next(it), next(it)) if a.ext_prev else (None, None)
        sink_ref = next(it) if sinks is not None else None
        o_ref, lse_ref = next(it), next(it)
        g = pl.program_id(0)
        lo = _lane_lo()
        top = lax.broadcasted_iota(jnp.int32, (128, 1), 0) < 64
        rid = lax.broadcasted_iota(jnp.int32, (8, 128), 0)
        kq = a.masks()
        stats = {}

        def scores(j, gi, pairs):
            rows = slice(BLK * j, BLK * (j + 1))
            qs = _stack_heads([q_ref[rows, 128 * p:128 * (p + 1)] for p in pairs], lo)
            kk, vv, mask, _ = a.keys(j, gi, kc_ref, vc_ref, kp_ref, vp_ref, lo, kq, g)
            pieces = [slice(r0, r0 + BLK) for r0 in range(0, kk.shape[0], BLK)]
            return dict(j=j, gi=gi, pairs=pairs, rows=rows, vv=vv, mask=mask, pieces=pieces,
                        ss=[_dot(kk[r], qs, NT) for r in pieces])

        def softmax(c):
            gi, mask = c["gi"], c["mask"]
            ss = [s if mask is None else jnp.where(mask[r], s, NEG) for r, s in zip(c["pieces"], c.pop("ss"))]
            m = jnp.max(ss[0], axis=0, keepdims=True)
            for s in ss[1:]:
                m = jnp.maximum(m, jnp.max(s, axis=0, keepdims=True))
            if sink_ref is not None:
                sk = jnp.concatenate([jnp.full((1, 128), sink_ref[0, a.nh * gi + i], F32) for i in range(a.nh)], axis=1)
                m = jnp.maximum(m, sk)
            ps = [jnp.exp(s - m) for s in ss]
            l = sum(jnp.sum(p, axis=0, keepdims=True) for p in ps)
            if sink_ref is not None:
                l = l + jnp.exp(sk - m)
            c["ps"] = [p.astype(BF16) for p in ps]
            c["l"], c["lse"] = l, m + jnp.log(l)

        def outputs(c):
            j, gi, rows = c["j"], c["gi"], c["rows"]
            ot = sum(_dot(c["vv"][r], p, TN) for r, p in zip(c["pieces"], c["ps"]))
            ot = ot * pl.reciprocal(c["l"], approx=True)
            for i, p in enumerate(c["pairs"]):
                o2t = jnp.where(top, ot[:, 256 * i:256 * i + 128], ot[:, 256 * i + 128:256 * i + 256])
                o_ref[rows, 128 * p:128 * (p + 1)] = o2t.T.astype(BF16)
            stat = stats.get(j, jnp.zeros((8, 128), F32))
            for i in range(a.nh):
                stat = jnp.where(rid == a.nh * gi + i, c["lse"][:, 128 * i:128 * (i + 1)], stat)
            stats[j] = stat
            if gi == a.groups[-1][0]:
                lse_ref[rows, :] = jnp.concatenate([stats.pop(j), jnp.zeros((120, 128), F32)], axis=0).T

        chains = [(j, gi, pairs) for j in range(QB) for gi, pairs in a.groups]
        live = {}
        for t in range(len(chains) + 2):
            if t < len(chains):
                live[t] = scores(*chains[t])
            if 0 <= t - 1 < len(chains):
                softmax(live[t - 1])
            if 0 <= t - 2 < len(chains):
                outputs(live.pop(t - 2))


    args = [q, k, v] + ([k, v] if a.ext_prev else [])
    in_specs = [a.q_spec] + a.kv_specs
    if sinks is not None:
        args.append(sinks)
        in_specs.append(pl.BlockSpec(memory_space=pltpu.SMEM))
    return _Part(body, args, in_specs, [a.row_spec, a.stat_spec], [_sds((T, qw), BF16), _sds((T, 128), F32)])


def _attn_bwd(q, qcb, qw, k, kcb, v, vcb, kvw, do, lse, dl, *, kind, nb=1, max_dist=BLK, gqa=False, sinkv=None,
              mem_in=None):
    a = _Attn(kind, nb, max_dist, gqa, qw, kvw, qcb, kcb, vcb)

    def body(*refs):
        it = iter(refs)
        q_ref, kc_ref, vc_ref = next(it), next(it), next(it)
        kp_ref, vp_ref = (next(it), next(it)) if a.ext_prev else (None, None)
        do_ref, lse_ref, dl_ref = next(it), next(it), next(it)
        sinkv_ref = next(it) if sinkv is not None else None
        mem_ref = next(it) if kind == "mem" else None
        dq_ref = next(it)
        if kind == "mem":
            gmem_ref = next(it)
        else:
            dk_out, dv_out = next(it), next(it)
        dsink_ref = next(it) if sinkv is not None else None
        if kind != "mem":
            dk_ref, dv_ref, stage_k, stage_v, flush_sem = next(it), next(it), next(it), next(it), next(it)
        else:
            dkv_ref = next(it)
        g = pl.program_id(0)
        lo = _lane_lo()
        top = lax.broadcasted_iota(jnp.int32, (128, 1), 0) < 64

        @pl.when(g == 0)
        def _():
            if kind == "mem":
                dkv_ref[...] = jnp.zeros_like(dkv_ref)
            else:
                dk_ref[...] = jnp.zeros_like(dk_ref)
                dv_ref[...] = jnp.zeros_like(dv_ref)
            if dsink_ref is not None:
                dsink_ref[...] = jnp.zeros_like(dsink_ref)

        kq = a.masks()
        stats_t = {}

        def first_matmuls(j, gi, pairs):
            rows = slice(BLK * j, BLK * (j + 1))
            if j not in stats_t:
                stats_t[j] = (lse_ref[rows, :].T, dl_ref[rows, :].T)
            lse_t, dl_t = stats_t[j]
            heads = [a.nh * gi + i for i in range(a.nh)]
            c = dict(rows=rows, gi=gi, pairs=pairs)
            c["qs"] = _stack_heads([q_ref[rows, 128 * p:128 * (p + 1)] for p in pairs], lo)
            c["dos"] = _stack_heads([do_ref[rows, 128 * p:128 * (p + 1)] for p in pairs], lo)
            c["lse_row"] = jnp.concatenate([lse_t[h:h + 1, :] for h in heads], axis=1)
            c["dl_row"] = jnp.concatenate([dl_t[h:h + 1, :] for h in heads], axis=1)
            c["kk"], vv, c["mask"], c["dests"] = a.keys(j, gi, kc_ref, vc_ref, kp_ref, vp_ref, lo, kq, g)
            c["s"] = _dot(c["kk"], c["qs"], NT)
            c["dp"] = _dot(vv, c["dos"], NT)
            return c

        def elementwise(c):
            s = c.pop("s")
            if c["mask"] is not None:
                s = jnp.where(c["mask"], s, NEG)
            p = jnp.exp(s - c["lse_row"])
            c["ds"] = (p * (c.pop("dp") - c["dl_row"])).astype(BF16)
            c["p"] = p.astype(BF16)

        def last_matmuls(c):
            gi, rows = c["gi"], c["rows"]
            dqt = _dot(c["kk"], c["ds"], TN)
            ck = _dot(c["ds"], c["qs"], NN)
            cv = _dot(c["p"], c["dos"], NN)
            if gqa:
                sel = lo if gi == 0 else jnp.logical_not(lo)
                ck = jnp.where(sel, ck + pltpu.roll(ck, 64, 1), 0.0)
                cv = jnp.where(sel, cv + pltpu.roll(cv, 64, 1), 0.0)
                kcols = slice(0, 128)
            else:
                kcols = slice(128 * gi, 128 * (gi + 1))
            for r0, nr, key0 in c["dests"]:
                krows = pl.ds(key0, nr)
                if kind == "mem":
                    dkv_ref[krows, kcols] += ck[r0:r0 + nr]
                    dkv_ref[krows, slice(kvw + kcols.start, kvw + kcols.stop)] += cv[r0:r0 + nr]
                else:
                    dk_ref[krows, kcols] += ck[r0:r0 + nr]
                    dv_ref[krows, kcols] += cv[r0:r0 + nr]
            for i, p in enumerate(c["pairs"]):
                dq2t = jnp.where(top, dqt[:, 256 * i:256 * i + 128], dqt[:, 256 * i + 128:256 * i + 256])
                dq_ref[rows, 128 * p:128 * (p + 1)] = dq2t.T.astype(BF16)

        chains = [(j, gi, pairs) for j in range(QB) for gi, pairs in a.groups]
        live = {}
        for t in range(len(chains) + 2):
            if t < len(chains):
                live[t] = first_matmuls(*chains[t])
            if 0 <= t - 1 < len(chains):
                elementwise(live[t - 1])
            if 0 <= t - 2 < len(chains):
                last_matmuls(live.pop(t - 2))
        if dsink_ref is not None:
            ps = jnp.exp(sinkv_ref[...] - lse_ref[...]) * dl_ref[...]
            dsink_ref[...] += jnp.sum(ps, axis=0, keepdims=True)
        if kind == "mem":
            @pl.when(g == T // QR - 1)
            def _():
                gmem_ref[...] = _dot(mem_ref[...], dkv_ref[...].astype(BF16), TN)
        else:
            n_steps = T // QR

            def flush(step):
                rows = pl.ds(pl.multiple_of(step * QR, QR), QR)
                out = []
                for acc, stage, dst, i in ((dk_ref, stage_k, dk_out, 0), (dv_ref, stage_v, dv_out, 1)):
                    stage[...] = acc[rows, :].astype(BF16)
                    out.append(pltpu.make_async_copy(stage, dst.at[rows, :], flush_sem.at[i]))
                return out

            def flushed(step):
                rows = pl.ds(pl.multiple_of(step * QR, QR), QR)
                return [pltpu.make_async_copy(stage, dst.at[rows, :], flush_sem.at[i])
                        for stage, dst, i in ((stage_k, dk_out, 0), (stage_v, dv_out, 1))]

            @pl.when(g >= 2)
            def _():
                for cp in flushed(g - 2):
                    cp.wait()

            @pl.when(g >= 1)
            def _():
                for cp in flush(g - 1):
                    cp.start()

            @pl.when(g == n_steps - 1)
            def _():
                for cp in flushed(g - 1):
                    cp.wait()
                for cp in flush(g):
                    cp.start()
                for cp in flushed(g):
                    cp.wait()

    args = [q, k, v] + ([k, v] if a.ext_prev else []) + [do, lse, dl]
    in_specs = [a.q_spec] + a.kv_specs + [a.row_spec, a.stat_spec, a.stat_spec]
    if sinkv is not None:
        args.append(sinkv)
        in_specs.append(_full((1, 128)))
    if kind == "mem":
        args.append(mem_in)
        in_specs.append(pl.BlockSpec(mem_in.shape, lambda g: (0, 0), pipeline_mode=pl.Buffered(1)))
    out_shape = [_sds((T, qw), BF16)]
    out_specs = [a.row_spec]
    scratch = []
    if kind == "mem":
        out_shape.append(_sds((D_MODEL, 2 * kvw), F32))
        out_specs.append(pl.BlockSpec((D_MODEL, 2 * kvw), lambda g: (0, 0), pipeline_mode=pl.Buffered(1)))
        scratch = [pltpu.VMEM((B_LOC * MEM_LEN, 2 * kvw), F32)]
    else:
        out_shape += [_sds((T, kvw), BF16)] * 2
        out_specs += [pl.BlockSpec(memory_space=pl.ANY)] * 2
        scratch = [pltpu.VMEM((T, kvw), F32)] * 2 + [pltpu.VMEM((QR, kvw), BF16)] * 2 + [pltpu.SemaphoreType.DMA((2,))]
    if sinkv is not None:
        out_shape.append(_sds((1, 128), F32))
        out_specs.append(_full((1, 128)))
    return _Part(body, args, in_specs, out_specs, out_shape, scratch)


def _dot2(v, w_ref):
    hi = v.astype(BF16)
    lo = (v - hi.astype(F32)).astype(BF16)
    return _dot(hi, w_ref[...], NN) + _dot(lo, w_ref[...], NN)


def _middle(oa, o1, l1, o4, l4, o16, l16, oc, z, x, tgt, g_br, ln_g, ln_b, wout, spread4, gather4, gather8):
    tm = 512
    spt = SEQ // tm

    def body(oa_ref, o1_ref, l1_ref, o4_ref, l4_ref, o16_ref, l16_ref, oc_ref, z_ref, x_ref, t_ref,
             g_ref, lg_ref, lb_ref, w_ref, sp4_ref, ga4_ref, ga8_ref,
             du_ref, dz_ref, doa_ref, dla_ref,
             dobn_ref, lsen_ref, dlbn_ref, dob4_ref, lse4_ref, dlb4_ref, dob16_ref, lse16_ref, dlb16_ref,
             doc_ref, dlc_ref, acc_ref, gout_ref, scr):
        i = pl.program_id(0)

        @pl.when(i == 0)
        def _():
            acc_ref[...] = jnp.zeros_like(acc_ref)
            gout_ref[...] = jnp.zeros_like(gout_ref)

        for res in range(4):
            rows = pl.ds(res, tm // 4, stride=4)
            for j in range(2):
                scr[j, rows, :] = o4_ref[0, res, :, 128 * j:128 * (j + 1)].astype(F32)
            scr[2, rows, :] = l4_ref[0, res]
        for res in range(16):
            rows = pl.ds(res, tm // 16, stride=16)
            for j in range(2):
                scr[3 + j, rows, :] = o16_ref[0, res, :, 128 * j:128 * (j + 1)].astype(F32)
            scr[5, rows, :] = l16_ref[0, res]
        inv_d = 1.0 / D_MODEL
        gb, lg, lb = g_ref[...], lg_ref[...], lb_ref[...]

        def rms(o):
            r = lax.rsqrt(jnp.sum(o * o, axis=1, keepdims=True) * (1.0 / o.shape[1]) + RMS_EPS)
            return o * r, r

        def rms_bwd(dn_, n_, r):
            return r * (dn_ - n_ * (jnp.sum(dn_ * n_, axis=1, keepdims=True) * (1.0 / n_.shape[1])))

        def forward(rs):
            o4v = jnp.concatenate([scr[0, rs, :], scr[1, rs, :]], axis=1)
            o16v = jnp.concatenate([scr[3, rs, :], scr[4, rs, :]], axis=1)
            l1v, l4v, l16v = l1_ref[rs, :], scr[2, rs, :], scr[5, rs, :]
            mx = jnp.maximum(jnp.maximum(l1v, l4v), l16v)
            e1, e4, e16 = jnp.exp(l1v - mx), jnp.exp(l4v - mx), jnp.exp(l16v - mx)
            ssum = e1 + e4 + e16
            inv = 1.0 / ssum
            c = dict(rs=rs, lse_b=mx + jnp.log(ssum))
            c["ob"] = (_dot2(e1 * inv, sp4_ref) * o1_ref[rs, :].astype(F32) + _dot2(e4 * inv, sp4_ref) * o4v
                       + _dot2(e16 * inv, sp4_ref) * o16v)
            c["oa"], c["oc"] = oa_ref[rs, :].astype(F32), oc_ref[rs, :].astype(F32)
            na, c["ra"] = rms(c["oa"])
            nb_, c["rb"] = rms(c["ob"])
            nc, c["rc"] = rms(c["oc"])
            c["n"] = jnp.concatenate([na, nb_, nc], axis=1)
            c["zf"] = z_ref[rs, :].astype(F32)
            c["sig"] = 1.0 / (1.0 + jnp.exp(-c["zf"]))
            c["sz"] = c["zf"] * c["sig"]
            c["yb"] = (c["n"] * gb * c["sz"]).astype(BF16)
            c["y2"] = _dot(c["yb"], w_ref[...], NN)
            return c

        def norm(c):
            rs = c["rs"]
            u = ALPHA * x_ref[rs, :] + c.pop("y2")
            mu = jnp.sum(u, axis=1, keepdims=True) * inv_d
            uc = u - mu
            rstd = lax.rsqrt(jnp.sum(uc * uc, axis=1, keepdims=True) * inv_d + LN_EPS)
            xh = uc * rstd
            diff = xh * lg + lb - t_ref[rs, :]
            acc_ref[0:1, :] += jnp.sum(diff * diff, axis=0, keepdims=True) * (0.5 * inv_d)
            dout = diff * inv_d
            acc_ref[2:3, :] += jnp.sum(dout * xh, axis=0, keepdims=True)
            acc_ref[3:4, :] += jnp.sum(dout, axis=0, keepdims=True)
            dxh = dout * lg
            du = rstd * (dxh - jnp.sum(dxh, axis=1, keepdims=True) * inv_d
                         - xh * (jnp.sum(dxh * xh, axis=1, keepdims=True) * inv_d))
            dub = du.astype(BF16)
            du_ref[rs, :] = dub
            c["dy"] = _dot(dub, w_ref[...], NT)
            gout_ref[...] += _dot(c.pop("yb"), dub, TN)

        def backward(c):
            rs, n, dy, zf, sig = c["rs"], c["n"], c["dy"], c["zf"], c["sig"]
            t1 = dy * c["sz"]
            acc_ref[1:2, :] += jnp.sum(t1 * n, axis=0, keepdims=True)
            dn = t1 * gb
            dz_ref[rs, :] = (dy * n * gb * (sig * (1.0 + zf * (1.0 - sig)))).astype(BF16)
            doa = rms_bwd(dn[:, :W_A], n[:, :W_A], c["ra"])
            dob = rms_bwd(dn[:, W_A:W_A + W_B], n[:, W_A:W_A + W_B], c["rb"])
            doc = rms_bwd(dn[:, W_A + W_B:], n[:, W_A + W_B:], c["rc"])
            doa_ref[rs, :] = doa.astype(BF16)
            dla_ref[rs, :] = _dot2(doa * c["oa"], ga8_ref)
            doc_ref[rs, :] = doc.astype(BF16)
            dlc_ref[rs, :] = _dot2(doc * c["oc"], ga4_ref)
            dobn_ref[rs, :] = dob.astype(BF16)
            lsen_ref[rs, :] = c["lse_b"]
            dlbn_ref[rs, :] = _dot2(dob * c["ob"], ga4_ref)
            scr[0, rs, :] = dob[:, :128]
            scr[1, rs, :] = dob[:, 128:]

        halves = [slice(h * (tm // 2), (h + 1) * (tm // 2)) for h in range(2)]
        live = {}
        for t in range(len(halves) + 2):
            if t < len(halves):
                live[t] = forward(halves[t])
            if 0 <= t - 1 < len(halves):
                norm(live[t - 1])
            if 0 <= t - 2 < len(halves):
                backward(live.pop(t - 2))
        for j in range(2):
            sl = slice(128 * j, 128 * (j + 1))
            for res in range(4):
                dob4_ref[0, res, :, sl] = scr[j, pl.ds(res, tm // 4, stride=4), :].astype(BF16)
            for res in range(16):
                dob16_ref[0, res, :, sl] = scr[j, pl.ds(res, tm // 16, stride=16), :].astype(BF16)
        for res in range(4):
            rows = pl.ds(res, tm // 4, stride=4)
            lse4_ref[0, res] = lsen_ref[rows, :]
            dlb4_ref[0, res] = dlbn_ref[rows, :]
        for res in range(16):
            rows = pl.ds(res, tm // 16, stride=16)
            lse16_ref[0, res] = lsen_ref[rows, :]
            dlb16_ref[0, res] = dlbn_ref[rows, :]


    tok = lambda w: pl.BlockSpec((tm, w), lambda i: (i, 0))
    p4 = lambda w: pl.BlockSpec((1, 4, tm // 4, w), lambda i: (i // spt, 0, i % spt, 0))
    p16 = lambda w: pl.BlockSpec((1, 16, tm // 16, w), lambda i: (i // spt, 0, i % spt, 0))
    s4 = lambda w, dt: _sds((B_LOC, 4, SEQ // 4, w), dt)
    s16 = lambda w, dt: _sds((B_LOC, 16, SEQ // 16, w), dt)
    row = _full((1, D_MODEL))
    return pl.pallas_call(
        body, name="middle", grid=(T // tm,),
        in_specs=[tok(W_A), tok(W_B), tok(128), p4(W_B), p4(128), p16(W_B), p16(128), tok(W_C), tok(D_MIX),
                  tok(D_MODEL), tok(D_MODEL), row, row, row, _full((D_MIX, D_MODEL)),
                  _full((128, W_B)), _full((W_B, 128)), _full((W_A, 128))],
        out_specs=(tok(D_MODEL), tok(D_MIX), tok(W_A), tok(128),
                   tok(W_B), tok(128), tok(128), p4(W_B), p4(128), p4(128), p16(W_B), p16(128), p16(128),
                   tok(W_C), tok(128), _full((8, D_MODEL)), _full((D_MIX, D_MODEL))),
        out_shape=(_sds((T, D_MODEL), BF16), _sds((T, D_MIX), BF16),
                   _sds((T, W_A), BF16), _sds((T, 128), F32),
                   _sds((T, W_B), BF16), _sds((T, 128), F32), _sds((T, 128), F32),
                   s4(W_B, BF16), s4(128, F32), s4(128, F32), s16(W_B, BF16), s16(128, F32), s16(128, F32),
                   _sds((T, W_C), BF16), _sds((T, 128), F32), _sds((8, D_MODEL), F32),
                   _sds((D_MIX, D_MODEL), F32)),
        scratch_shapes=[pltpu.VMEM((6, tm, 128), F32)],
        compiler_params=_cp(("arbitrary",), vmem_mb=56),
    )(*_pin(oa, o1, l1, o4, l4, o16, l16, oc, z, x, tgt, g_br, ln_g, ln_b, wout, spread4, gather4, gather8))


class _ReduceScatter:
    def __init__(self, shapes):
        self.shapes = shapes

    def scratch_shapes(self):
        out = []
        for n, w in self.shapes:
            h, p = n // 2, n // 4
            out += [pltpu.VMEM((4, h, w), F32), pltpu.VMEM((4, h, w), F32), pltpu.VMEM((6, p, w), BF16),
                    pltpu.VMEM((6, p, w), BF16), pltpu.VMEM((2, p, w), F32), pltpu.VMEM((h, w), F32)]
        na = len(self.shapes)
        dma = pltpu.SemaphoreType.DMA
        return out + [dma((na, 4)), dma((na, 4)), dma((na, 4)), dma((na, 6)), dma((na, 6)), dma((na,)), dma((na,)),
                      dma((na,))]

    def bind(self, g_refs, r_refs, scratch):
        na = len(self.shapes)
        bufs = [scratch[6 * a:6 * a + 6] for a in range(na)]
        mine, sib, stage, land, keep, tot = (tuple(b[i] for b in bufs) for i in range(6))
        loc_sem, s1_send, s1_recv, s2_send, s2_recv, s3_send, s3_recv, st_sem = scratch[6 * na:6 * na + 8]
        x, y, c = lax.axis_index("x"), lax.axis_index("y"), lax.axis_index("c")
        me, sibling = (x, y, c), (x, y, 1 - c)
        xn, yn, dg = (1 - x, y), (x, 1 - y), (1 - x, 1 - y)
        idx = lambda chip: 2 * chip[0] + chip[1]
        my_chip = idx((x, y))
        order = [idx(xn), idx(dg), idx(yn), my_chip]

        def rows(a, k, half):
            n = self.shapes[a][0]
            return pl.ds(pl.multiple_of(k * n + half * (n // 2), 8), n // 2)

        def piece(a, q):
            p = self.shapes[a][0] // 4
            return slice(q * p, (q + 1) * p)

        def load(a, k):
            return pltpu.make_async_copy(g_refs[a].at[rows(a, k, c), :], mine[a].at[k], loc_sem.at[a, k])

        def s1(a, k, half):
            return pltpu.make_async_remote_copy(
                src_ref=g_refs[a].at[rows(a, k, half), :], dst_ref=sib[a].at[k],
                send_sem=s1_send.at[a, k], recv_sem=s1_recv.at[a, k], device_id=sibling, device_id_type=MESH)

        def s2(a, i, to):
            return pltpu.make_async_remote_copy(
                src_ref=stage[a].at[i], dst_ref=land[a].at[i], send_sem=s2_send.at[a, i], recv_sem=s2_recv.at[a, i],
                device_id=to, device_id_type=MESH)

        via = {0: xn, 1: xn, 2: yn, 3: yn, 4: yn, 5: xn}

        def s3(a, half, to):
            return pltpu.make_async_remote_copy(
                src_ref=tot[a], dst_ref=r_refs[a].at[rows(a, 0, half), :], send_sem=s3_send.at[a],
                recv_sem=s3_recv.at[a], device_id=to, device_id_type=MESH)

        def store(a):
            return pltpu.make_async_copy(tot[a], r_refs[a].at[rows(a, 0, c), :], st_sem.at[a])

        def start():
            for k in order:
                for a in range(na):
                    load(a, k).start()
                    s1(a, k, 1 - c).start()

        def chip_sum(a, k):
            load(a, k).wait()
            s1(a, k, c).wait_recv()
            return mine[a][k] + sib[a][k]

        def exchange():
            for a in range(na):
                P, Q = piece(a, 0), piece(a, 1)
                s_xn = chip_sum(a, idx(xn))
                stage[a][0] = s_xn[P].astype(BF16)
                keep[a][1] = s_xn[Q]
                s_dg = chip_sum(a, idx(dg))
                stage[a][1] = s_dg[P].astype(BF16)
                s2(a, 0, (*xn, c)).start()
                s2(a, 1, (*xn, c)).start()
                stage[a][3] = s_dg[Q].astype(BF16)
                s_yn = chip_sum(a, idx(yn))
                stage[a][2] = s_yn[Q].astype(BF16)
                keep[a][0] = s_yn[P]
                s2(a, 2, (*yn, c)).start()
                s2(a, 3, (*yn, c)).start()
                tot[a][...] = chip_sum(a, my_chip)

        def relay():
            for a in range(na):
                P, Q = piece(a, 0), piece(a, 1)
                s2(a, 1, me).wait_recv()
                stage[a][4] = (keep[a][0] + land[a][1].astype(F32)).astype(BF16)
                s2(a, 4, (*yn, c)).start()
                s2(a, 3, me).wait_recv()
                stage[a][5] = (keep[a][1] + land[a][3].astype(F32)).astype(BF16)
                s2(a, 5, (*xn, c)).start()
                s2(a, 0, me).wait_recv()
                tot[a][P, :] += land[a][0].astype(F32)
                s2(a, 2, me).wait_recv()
                tot[a][Q, :] += land[a][2].astype(F32)

        def finish():
            for a in range(na):
                P, Q = piece(a, 0), piece(a, 1)
                s2(a, 4, me).wait_recv()
                tot[a][P, :] += land[a][4].astype(F32)
                s2(a, 5, me).wait_recv()
                tot[a][Q, :] += land[a][5].astype(F32)
                s3(a, c, sibling).start()
                store(a).start()

        def drain():
            for a in range(na):
                s3(a, 1 - c, me).wait_recv()
                store(a).wait()
            for a in range(na):
                for k in order:
                    s1(a, k, 1 - c).wait_send()
                for i in range(6):
                    s2(a, i, (*via[i], c)).wait_send()
                s3(a, c, sibling).wait_send()

        return start, exchange, relay, finish, drain

    def part(self, grads, steps):
        def body(*refs):
            na = len(self.shapes)
            i = pl.program_id(0)
            for step, phase in zip(steps, self.bind(refs[:na], refs[na:2 * na], refs[2 * na:])):
                pl.when(i == step)(phase)

        hbm = pl.BlockSpec(memory_space=pl.ANY)
        return _Part(body, list(grads), [hbm] * len(grads), [hbm] * len(grads),
                     [_sds((n, w), F32) for n, w in self.shapes], self.scratch_shapes())


def _dh_dx(dqa, dka, dva, dqn, dkn, dvn, dq4, dk4, dv4, dq16, dk16, dv16, dqc, dz, du, xb, cos, sa, sb, winT):
    tm = 512
    spt = SEQ // tm

    def body(dqa_ref, dka_ref, dva_ref, dqn_ref, dkn_ref, dvn_ref, dq4_ref, dk4_ref, dv4_ref,
             dq16_ref, dk16_ref, dv16_ref, dqc_ref, dz_ref, du_ref, xb_ref, cos_ref, sa_ref, sb_ref, w_ref,
             gx_ref, db_ref, gin_ref, dh_ref, scr):
        i = pl.program_id(0)

        @pl.when(i == 0)
        def _():
            db_ref[...] = jnp.zeros_like(db_ref)
            gin_ref[...] = jnp.zeros_like(gin_ref)

        cos_t, sa_t, sb_t = cos_ref[...], sa_ref[...], sb_ref[...]

        def rope_t(t):
            return _rope(t, cos_t, sa_t, sb_t, -1)

        def put(r0, val):
            n = val.shape[1]
            dh_ref[:, r0:r0 + n] = val.astype(BF16)
            db_ref[:, r0:r0 + n] += jnp.sum(val, axis=0, keepdims=True)

        put(O_QA, rope_t(dqa_ref[...].astype(F32)) * QK_SCALE)
        put(O_KA, rope_t(dka_ref[...].astype(F32)))
        put(O_VA, dva_ref[...].astype(F32))
        put(O_QC, dqc_ref[...].astype(F32) * QK_SCALE)
        put(O_Z, dz_ref[...].astype(F32))
        for k, (n_ref, r4, r16) in enumerate(((dqn_ref, dq4_ref, dq16_ref), (dkn_ref, dk4_ref, dk16_ref),
                                               (dvn_ref, dv4_ref, dv16_ref))):
            for j in range(2):
                sl = slice(128 * j, 128 * (j + 1))
                a, q = 2 * k + j, tm // 4
                scr[a] = n_ref[:, sl].astype(F32)
                for res in range(16):
                    scr[6 + a, pl.ds((res % 4) * q + res // 4, tm // 16, stride=4), :] = r16[0, res, :, sl].astype(F32)
                for res in range(4):
                    scr[a, pl.ds(res, q, stride=4), :] += (scr[6 + a, res * q:(res + 1) * q, :]
                                                           + r4[0, res, :, sl].astype(F32))
        cat = lambda a: jnp.concatenate([scr[a], scr[a + 1]], axis=1)
        put(O_QB, rope_t(cat(0)) * QK_SCALE)
        put(O_KB, rope_t(cat(2)))
        put(O_VB, cat(4))
        gx_ref[...] = _dot(dh_ref[...], w_ref[...], NN) + ALPHA * du_ref[...].astype(F32)
        gin_ref[...] += _dot(dh_ref[...], xb_ref[...], TN)

    tok = lambda w: pl.BlockSpec((tm, w), lambda i: (i, 0))
    tab = pl.BlockSpec((tm, 128), lambda i: (i % spt, 0))
    p4 = pl.BlockSpec((1, 4, tm // 4, W_B), lambda i: (i // spt, 0, i % spt, 0))
    p16 = pl.BlockSpec((1, 16, tm // 16, W_B), lambda i: (i // spt, 0, i % spt, 0))
    once = lambda shape: pl.BlockSpec(shape, lambda i: (0, 0), pipeline_mode=pl.Buffered(1))
    return pl.pallas_call(
        body, name="dh_dx", grid=(T // tm,),
        in_specs=[tok(W_A), tok(W_KV_A), tok(W_KV_A), tok(W_B), tok(W_B), tok(W_B), p4, p4, p4, p16, p16, p16,
                  tok(W_C), tok(D_MIX), tok(D_MODEL), tok(D_MODEL), tab, tab, tab, once((D_IN, D_MODEL))],
        out_specs=(tok(D_MODEL), _full((1, D_IN)), once((D_IN, D_MODEL))),
        out_shape=(_sds((T, D_MODEL), F32), _sds((1, D_IN), F32), _sds((D_IN, D_MODEL), F32)),
        scratch_shapes=[pltpu.VMEM((tm, D_IN), BF16), pltpu.VMEM((12, tm, 128), F32)],
        compiler_params=_cp(("arbitrary",), vmem_mb=56),
    )(*_pin(dqa, dka, dva, dqn, dkn, dvn, dq4, dk4, dv4, dq16, dk16, dv16, dqc, dz, du, xb, cos, sa, sb, winT))


def _reduce_grads(g_in, acc, dbin, dsink):
    rs = _ReduceScatter([(SH_IN, D_MODEL)])

    def body(g_ref, acc_ref, dbin_ref, dsink_ref, r_ref, sv_ref, sv_mine, sv_all, sv_send, sv_recv, *rs_scratch):
        x, y, c = lax.axis_index("x"), lax.axis_index("y"), lax.axis_index("c")
        chips = [(1 - x, y), (x, 1 - y), (1 - x, 1 - y)]
        start, exchange, relay, finish, drain = rs.bind((g_ref,), (r_ref,), rs_scratch)
        start()

        sv_mine[...] = jnp.zeros_like(sv_mine)
        sv_mine[0:4, 0:D_MODEL] = acc_ref[0:4, :]
        sv_mine[4:5, 0:D_IN] = dbin_ref[...]
        sv_mine[5:6, 0:128] = dsink_ref[...]
        my_dev = 4 * x + 2 * y + c
        others = [(x, y, 1 - c)] + [(*chip, cc) for chip in chips for cc in (c, 1 - c)]

        def sv_copy(j, to):
            return pltpu.make_async_remote_copy(
                src_ref=sv_mine, dst_ref=sv_all.at[my_dev], send_sem=sv_send.at[j], recv_sem=sv_recv.at[j],
                device_id=to, device_id_type=MESH)

        sv_sends = [sv_copy(j, to) for j, to in enumerate(others)]
        for cp in sv_sends:
            cp.start()
        exchange()
        relay()
        finish()
        sv_all[my_dev] = sv_mine[...]
        for j in range(7):
            sv_copy(j, (x, y, c)).wait_recv()
        tot = sv_all[0]
        for d in range(1, 8):
            tot = tot + sv_all[d]
        sv_ref[...] = tot
        drain()
        for cp in sv_sends:
            cp.wait_send()

    vm = pl.BlockSpec(memory_space=pltpu.VMEM)
    hbm = pl.BlockSpec(memory_space=pl.ANY)
    return pl.pallas_call(
        body, name="reduce_grads",
        out_shape=(_sds((SH_IN, D_MODEL), F32), _vm_sds((8, SV_W), F32)),
        in_specs=[hbm, vm, vm, vm], out_specs=(hbm, vm),
        scratch_shapes=[pltpu.VMEM((8, SV_W), F32), pltpu.VMEM((8, 8, SV_W), F32),
                        pltpu.SemaphoreType.DMA((7,)), pltpu.SemaphoreType.DMA((7,))] + rs.scratch_shapes(),
        compiler_params=_cp(vmem_mb=40),
    )(pltpu.with_memory_space_constraint(g_in, pltpu.HBM), acc, dbin, dsink)


def _adamw_update(w, g, m, v):
    nm = ADAM_B1 * m + (1.0 - ADAM_B1) * g
    nv = ADAM_B2 * v + (1.0 - ADAM_B2) * (g * g)
    m_hat = nm / (1.0 - ADAM_B1 ** ADAM_STEP)
    v_hat = nv / (1.0 - ADAM_B2 ** ADAM_STEP)
    return -ADAM_LR * (m_hat / (jnp.sqrt(v_hat) + ADAM_EPS) + ADAM_WD * w), nm, nv


SMALL = ((4, D_IN, 1.0), (5, 8, -1.0), (1, D_MIX, 1.0), (2, D_MODEL, 1.0), (3, D_MODEL, 1.0))


def _adamw_all(items, sv, ws, ms, vs, n_steps=4):
    nb, ns = 4 * len(items), len(SMALL)

    def body(*refs):
        ins, sv_ref, small_in = refs[:nb], refs[nb], refs[nb + 1:nb + 1 + 3 * ns]
        outs = refs[nb + 1 + 3 * ns:]
        big_out, loss_ref, small_out = outs[:nb], outs[nb], outs[nb + 1:]
        for p in range(len(items)):
            w_ref, g_ref, m_ref, v_ref = ins[4 * p:4 * p + 4]
            gv = g_ref[...]
            big_out[4 * p][...] = gv
            big_out[4 * p + 1][...], big_out[4 * p + 2][...], big_out[4 * p + 3][...] = _adamw_update(
                w_ref[...], gv, m_ref[...], v_ref[...])

        @pl.when(pl.program_id(0) == 0)
        def _():
            loss_ref[...] = jnp.sum(sv_ref[0:1, 0:D_MODEL], axis=1, keepdims=True)
            for p, (row, width, sign) in enumerate(SMALL):
                gv = sign * sv_ref[row:row + 1, 0:width]
                small_out[4 * p][...] = gv
                small_out[4 * p + 1][...], small_out[4 * p + 2][...], small_out[4 * p + 3][...] = _adamw_update(
                    small_in[p][...], gv, small_in[ns + p][...], small_in[2 * ns + p][...])

    specs, shapes, args = [], [], []
    for w, g, m, v in items:
        rows, width = w.shape
        specs += [pl.BlockSpec((rows // n_steps, width), lambda i: (i, 0))] * 4
        shapes += [_sds((rows, width), F32)] * 4
        args += [w, g, m, v]
    small_args = [*ws, *ms, *vs]
    whole = lambda a: _full(a.shape)
    res = pl.pallas_call(
        body, name="adamw", grid=(n_steps,),
        in_specs=specs + [whole(sv)] + [whole(a) for a in small_args],
        out_specs=tuple(specs + [_full((1, 1))] + [whole(w) for w in ws for _ in range(4)]),
        out_shape=tuple(shapes + [_sds((1, 1), F32)] + [_sds(w.shape, F32) for w in ws for _ in range(4)]),
        compiler_params=_cp(("arbitrary",), vmem_mb=40),
    )(*_pin(*args, sv, *small_args))
    big = [tuple(res[4 * p:4 * p + 4]) for p in range(len(items))]
    return big, res[nb], [tuple(res[nb + 1 + 4 * p:nb + 5 + 4 * p]) for p in range(ns)]


def _rope_tables():
    pos = np.arange(SEQ, dtype=np.float32)
    inv = (np.float32(ROPE_THETA) ** (-np.arange(0, 64, 2, dtype=np.float32) / np.float32(64))).astype(np.float32)
    ang = np.tile(pos[:, None] * inv[None, :], (1, 4))
    cos, sin = np.cos(ang).astype(np.float32), np.sin(ang).astype(np.float32)
    low = (np.arange(128) % 64) < 32
    zero = np.float32(0.0)
    return jnp.asarray(cos), jnp.asarray(np.where(low, -sin, zero)), jnp.asarray(np.where(low, zero, sin))


def _local_step(x2, mem2, tgt2, winT, wout, wmem, b_in, sinks, g_branch, ln_gain, ln_bias):
    cos, sa, sb = _rope_tables()
    sinkv = jnp.pad(sinks, ((0, 0), (0, 120)))
    head_of_lane = np.arange(512)[:, None] // 64
    gather8 = jnp.asarray(head_of_lane == np.arange(128)[None, :], BF16)
    gather4 = jnp.asarray(head_of_lane[:W_B] == np.arange(128)[None, :], BF16)
    spread4 = jnp.asarray((head_of_lane[:W_B] == np.arange(128)[None, :]).T, BF16)

    xb, qa, ka, va, bn, b4, b16, qc, z, wout, wmem = _in_proj(x2, winT, b_in, cos, sa, sb, wout, wmem)
    memb, mkv = _mem_kv(mem2, wmem)
    b4f, b16f = b4.reshape(T, 768), b16.reshape(T, 768)

    swa = dict(kind="band", nb=SEQ // BLK, max_dist=BLK - 1, gqa=True)
    dil = (dict(kind="band", nb=SEQ // BLK), dict(kind="band", nb=SEQ // 4 // BLK), dict(kind="band", nb=1))
    (oa, lse_a), (o1, l1), (o4, l4), (o16, l16), (oc, lse_c) = _run_parts("attn_fwd", [
        _attn_fwd(qa, 0, W_A, ka, 0, va, 0, W_KV_A, sinks=sinks, **swa),
        _attn_fwd(bn, 0, W_B, bn, 1, bn, 2, W_B, **dil[0]),
        _attn_fwd(b4f, 0, W_B, b4f, 1, b4f, 2, W_B, **dil[1]),
        _attn_fwd(b16f, 0, W_B, b16f, 1, b16f, 2, W_B, **dil[2]),
        _attn_fwd(qc, 0, W_C, mkv, 0, mkv, 1, W_C, kind="mem")], "parallel", 48)

    s4 = lambda w: (B_LOC, 4, SEQ // 4, w)
    s16 = lambda w: (B_LOC, 16, SEQ // 16, w)
    (du, dz, doa, dla, dobn, lsen, dlbn, dob4, lse4, dlb4, dob16, lse16, dlb16, doc, dlc, acc, g_out) = _middle(
        oa, o1, l1, o4.reshape(s4(W_B)), l4.reshape(s4(128)), o16.reshape(s16(W_B)), l16.reshape(s16(128)), oc, z,
        x2, tgt2, g_branch, ln_gain, ln_bias, wout, spread4, gather4, gather8)

    flat = lambda a: a.reshape(T, a.shape[-1])
    (dqa, dka, dva, dsink), (dqc, g_mem) = _run_parts("attn_bwd_a", [
        _attn_bwd(qa, 0, W_A, ka, 0, va, 0, W_KV_A, doa, lse_a, dla, sinkv=sinkv, **swa),
        _attn_bwd(qc, 0, W_C, mkv, 0, mkv, 1, W_C, doc, lse_c, dlc, kind="mem", mem_in=memb)], "arbitrary", 48)
    last = T // QR - 1
    (r_out, r_mem), (dqn, dkn, dvn), (dq4, dk4, dv4), (dq16, dk16, dv16) = _run_parts("attn_bwd_b", [
        _ReduceScatter([(SH_OUT, D_MODEL), (SH_MEM, 2 * W_C)]).part((g_out, g_mem), (0, 1, 2, last, last)),
        _attn_bwd(bn, 0, W_B, bn, 1, bn, 2, W_B, dobn, lsen, dlbn, **dil[0]),
        _attn_bwd(b4f, 0, W_B, b4f, 1, b4f, 2, W_B, flat(dob4), flat(lse4), flat(dlb4), **dil[1]),
        _attn_bwd(b16f, 0, W_B, b16f, 1, b16f, 2, W_B, flat(dob16), flat(lse16), flat(dlb16), **dil[2])],
        "arbitrary", 62)

    r4 = lambda a: a.reshape(s4(W_B))
    r16 = lambda a: a.reshape(s16(W_B))
    gx, dbin, g_in = _dh_dx(dqa, dka, dva, dqn, dkn, dvn, r4(dq4), r4(dk4), r4(dv4), r16(dq16), r16(dk16),
                            r16(dv16), dqc, dz, du, xb, cos, sa, sb, winT)
    return gx, g_in, r_out, r_mem, acc, dbin, dsink


def kernel(x, mem, w_in, b_in, w_mem, attn_sinks, g_branch, w_out, ln_gain, ln_bias, loss_target, m_w_in, m_b_in, m_w_mem, m_attn_sinks, m_g_branch, m_w_out, m_ln_gain, m_ln_bias, v_w_in, v_b_in, v_w_mem, v_attn_sinks, v_g_branch, v_w_out, v_ln_gain, v_ln_bias):
    winT, wout, wmem = _gather_weights(w_in[0].T, w_out[0], w_mem[0])
    gx, g_in, r_out, r_mem, acc, dbin, dsink = _local_step(
        x.reshape(T, D_MODEL), mem.reshape(B_LOC * MEM_LEN, D_MODEL), loss_target.reshape(T, D_MODEL),
        winT, wout, wmem, b_in, attn_sinks, g_branch, ln_gain, ln_bias)
    r_in, sv = _reduce_grads(g_in, acc, dbin, dsink)

    small = ["b_in", "attn_sinks", "g_branch", "ln_gain", "ln_bias"]
    big, loss, steps = _adamw_all(
        [(w_in[0].T, r_in, m_w_in[0].T, v_w_in[0].T), (w_out[0], r_out, m_w_out[0], v_w_out[0]),
         (w_mem[0], r_mem, m_w_mem[0], v_w_mem[0])],
        sv, [b_in, attn_sinks, g_branch, ln_gain, ln_bias], [m_b_in, m_attn_sinks, m_g_branch, m_ln_gain, m_ln_bias],
        [v_b_in, v_attn_sinks, v_g_branch, v_ln_gain, v_ln_bias])
    out = dict(zip(small, steps))
    out["w_in"] = tuple(a.T[None] for a in big[0])
    out["w_out"], out["w_mem"] = (tuple(a[None] for a in st) for st in big[1:])
    names = ["w_in", "b_in", "w_mem", "attn_sinks", "g_branch", "w_out", "ln_gain", "ln_bias"]
    return (loss.reshape(()), gx.reshape(B_LOC, SEQ, D_MODEL), *[out[n][k] for k in range(4) for n in names])
```

```python
import jax
import jax.numpy as jnp
import numpy as np
from jax import lax
from jax.experimental import pallas as pl
from jax.experimental.pallas import tpu as pltpu

F32, BF16 = jnp.float32, jnp.bfloat16

D_MODEL = 1024
SEQ = 2048
B_LOC = 2
T = B_LOC * SEQ
BLK = 128
MEM_LEN = 256
W_A, W_KV_A, W_B, W_C, D_MIX = 512, 128, 256, 256, 1024
D_IN = 2816
O_QA, O_KA, O_VA, O_QB, O_KB, O_VB, O_QC, O_Z = 0, 512, 640, 768, 1024, 1280, 1536, 1792
ROPE_THETA = 10000.0
LN_EPS = 1e-5
RMS_EPS = 1e-6
ALPHA = 2.0 ** 0.25
QK_SCALE = 0.125
N_CHIP = 4
SH_IN, SH_OUT, SH_MEM = D_IN // N_CHIP, D_MIX // N_CHIP, D_MODEL // N_CHIP
NEG = -1e30
ADAM_LR, ADAM_B1, ADAM_B2, ADAM_EPS, ADAM_WD, ADAM_STEP = 0.001, 0.9, 0.999, 1e-08, 0.01, 10
SV_W = 3072
MESH = pl.DeviceIdType.MESH

NN = ((1,), (0,))
NT = ((1,), (1,))
TN = ((0,), (0,))


def _dot(a, b, dims):
    return lax.dot_general(a, b, (dims, ((), ())), preferred_element_type=F32)


def _cp(sem=None, vmem_mb=None):
    kw = {}
    if sem is not None:
        kw["dimension_semantics"] = sem
    if vmem_mb is not None:
        kw["vmem_limit_bytes"] = vmem_mb * 1024 * 1024
    return pltpu.CompilerParams(**kw)


def _sds(shape, dtype):
    return pltpu.HBM(shape, dtype)


def _vm_sds(shape, dtype):
    return jax.ShapeDtypeStruct(shape, dtype)


def _pin(*args):
    return [pltpu.with_memory_space_constraint(a, pltpu.HBM) for a in args]


def _full(shape):
    n = len(shape)
    return pl.BlockSpec(shape, lambda *_: (0,) * n)


def _shard_rows(ref, n, chip, half):
    start = pl.multiple_of((2 * chip[0] + chip[1]) * n + half * (n // 2), 16)
    return ref.at[pl.ds(start, n // 2), :]


def _gather_weights(win_sh, wout_sh, wmem_sh):
    half, piece = SH_IN // 2, SH_IN // 4
    shards = ((SH_IN, D_MODEL), (SH_OUT, D_MODEL), (SH_MEM, 2 * W_C))

    def body(a_ref, b_ref, c_ref, oa_ref, ob_ref, oc_ref, raw_a, raw_b, raw_c, own_a, own_b, own_c,
             load_sem, store_sem, ici_send, ici_recv, d2d_send, d2d_recv):
        x, y, c = lax.axis_index("x"), lax.axis_index("y"), lax.axis_index("c")
        me, sibling = (x, y, c), (x, y, 1 - c)
        xn, yn, dg = (1 - x, y), (x, 1 - y), (1 - x, 1 - y)
        srcs, raws = (a_ref, b_ref, c_ref), (raw_a, raw_b, raw_c)
        owns, outs = (own_a, own_b, own_c), (oa_ref, ob_ref, oc_ref)
        loads = [pltpu.make_async_copy(srcs[a], raws[a], load_sem.at[a]) for a in range(3)]
        for cp in loads:
            cp.start()

        def rows(chip, hf, q):
            start = pl.multiple_of((2 * chip[0] + chip[1]) * SH_IN + hf * half + q * piece, 16)
            return oa_ref.at[pl.ds(start, piece), :]

        def copy(sems, k, chip, hf, q, to, src=None):
            blk = rows(chip, hf, q)
            return pltpu.make_async_remote_copy(
                src_ref=blk if src is None else src, dst_ref=blk, send_sem=sems[0].at[k], recv_sem=sems[1].at[k],
                device_id=to, device_id_type=MESH)

        def my_piece(q):
            return own_a.at[pl.ds(pl.multiple_of(c * half + q * piece, 16), piece), :]

        ici, d2d = (ici_send, ici_recv), (d2d_send, d2d_recv)
        stores, direct = [], []
        for a, (n, _) in enumerate(shards):
            loads[a].wait()
            owns[a][...] = raws[a][...].astype(BF16)
            mine = pl.ds(pl.multiple_of((2 * x + y) * n, 16), n)
            stores.append(pltpu.make_async_copy(owns[a], outs[a].at[mine, :], store_sem.at[a]))
            stores[-1].start()
            if a == 0:
                direct = [copy(ici, 0, (x, y), c, 0, (*xn, c), my_piece(0)),
                          copy(ici, 1, (x, y), c, 1, (*xn, c), my_piece(1)),
                          copy(ici, 3, (x, y), c, 0, (*yn, c), my_piece(0)),
                          copy(ici, 4, (x, y), c, 1, (*yn, c), my_piece(1))]
                for cp in direct:
                    cp.start()
        arrivals = [(0, xn, 0), (1, xn, 1), (3, yn, 0), (4, yn, 1), (2, dg, 1), (5, dg, 0)]
        passed = []
        for k, chip, q in arrivals:
            copy(ici, k, chip, c, q, me).wait_recv()
            if k == 0:
                passed.append(copy(ici, 5, xn, c, 0, (*yn, c)))
                passed[-1].start()
            if k == 4:
                passed.append(copy(ici, 2, yn, c, 1, (*xn, c)))
                passed[-1].start()
            passed.append(copy(d2d, k, chip, c, q, sibling))
            passed[-1].start()
        for k, chip, q in arrivals:
            copy(d2d, k, chip, 1 - c, q, me).wait_recv()
        for cp in direct + passed:
            cp.wait_send()
        for cp in stores:
            cp.wait()

    hbm = pl.BlockSpec(memory_space=pl.ANY)
    return pl.pallas_call(
        body, name="gather_weights",
        out_shape=(_sds((D_IN, D_MODEL), BF16), _sds((D_MIX, D_MODEL), BF16), _sds((D_MODEL, 2 * W_C), BF16)),
        in_specs=[hbm, hbm, hbm], out_specs=(hbm, hbm, hbm),
        scratch_shapes=([pltpu.VMEM(sh, F32) for sh in shards] + [pltpu.VMEM(sh, BF16) for sh in shards]
                        + [pltpu.SemaphoreType.DMA((3,))] * 2 + [pltpu.SemaphoreType.DMA((6,))] * 4),
        compiler_params=_cp(vmem_mb=40),
    )(*_pin(win_sh, wout_sh, wmem_sh))


def _rope(t, cos, sa, sb, sign):
    w = t.shape[1]
    reps = w // 128
    c, a, b = (jnp.tile(v, (1, reps)) if reps > 1 else v for v in (cos, sa, sb))
    rot = pltpu.roll(t, w - 32, 1) * a + pltpu.roll(t, 32, 1) * b
    return t * c + rot if sign > 0 else t * c - rot


def _in_proj(x, winT, b_in, cos, sa, sb, wout_own, wmem_own):
    tm = 512
    spt = SEQ // tm
    n_steps = T // tm
    forward_step = n_steps // 2

    def body(x_ref, w_ref, b_ref, cos_ref, sa_ref, sb_ref, wo_in, wm_in,
             xb_ref, qa_ref, ka_ref, va_ref, bn_ref, b4_ref, b16_ref, qc_ref, z_ref, wo_ref, wm_ref,
             scr, ici_send, ici_recv, d2d_send, d2d_recv):
        i = pl.program_id(0)
        mx, my, mc = lax.axis_index("x"), lax.axis_index("y"), lax.axis_index("c")
        chips = [(1 - mx, my), (mx, 1 - my), (1 - mx, 1 - my)]
        full = ((wo_ref, SH_OUT), (wm_ref, SH_MEM))

        def copy(sems, a, j, chip_of_block, half, to):
            blk = _shard_rows(full[a][0], full[a][1], chip_of_block, half)
            return pltpu.make_async_remote_copy(
                src_ref=blk, dst_ref=blk, send_sem=sems[0].at[a, j], recv_sem=sems[1].at[a, j],
                device_id=to, device_id_type=MESH)

        ici, d2d = (ici_send, ici_recv), (d2d_send, d2d_recv)
        pairs = [(a, j, chip) for j, chip in enumerate(chips) for a in range(2)]

        @pl.when(i == 0)
        def _():
            for a, j, chip in pairs:
                copy(ici, a, j, (mx, my), mc, (*chip, mc)).start()

        @pl.when(i == forward_step)
        def _():
            for a, j, chip in pairs:
                copy(ici, a, j, chip, mc, (mx, my, mc)).wait_recv()
                copy(d2d, a, j, chip, mc, (mx, my, 1 - mc)).start()

        @pl.when(i == n_steps - 1)
        def _():
            for a, j, chip in pairs:
                copy(d2d, a, j, chip, 1 - mc, (mx, my, mc)).wait_recv()
            for a, j, chip in pairs:
                copy(ici, a, j, (mx, my), mc, (*chip, mc)).wait_send()
                copy(d2d, a, j, chip, mc, (mx, my, 1 - mc)).wait_send()

        xb = x_ref[...].astype(BF16)
        xb_ref[...] = xb
        cos_t, sa_t, sb_t = cos_ref[...], sa_ref[...], sb_ref[...]

        def proj(r0, n):
            return _dot(xb, w_ref[r0:r0 + n, :], NT) + b_ref[:, r0:r0 + n]

        def rope(t):
            return _rope(t, cos_t, sa_t, sb_t, +1)

        parts = (rope(proj(O_QB, W_B)) * QK_SCALE, rope(proj(O_KB, W_B)), proj(O_VB, W_B))
        for k, part in enumerate(parts):
            bn_ref[:, 256 * k:256 * (k + 1)] = part.astype(BF16)
            scr[2 * k] = part[:, :128]
            scr[2 * k + 1] = part[:, 128:]
        for j in range(6):
            lanes = slice(128 * j, 128 * (j + 1))
            for res in range(4):
                t = scr[j, pl.ds(res, tm // 4, stride=4), :]
                b4_ref[0, res, :, lanes] = t.astype(BF16)
                scr[6 + j, res * (tm // 4):(res + 1) * (tm // 4), :] = t
            for res in range(16):
                b16_ref[0, res, :, lanes] = scr[6 + j, pl.ds((res % 4) * (tm // 4) + res // 4, tm // 16, stride=4),
                                                :].astype(BF16)
        qa_ref[...] = (rope(proj(O_QA, W_A)) * QK_SCALE).astype(BF16)
        ka_ref[...] = rope(proj(O_KA, W_KV_A)).astype(BF16)
        va_ref[...] = proj(O_VA, W_KV_A).astype(BF16)
        qc_ref[...] = (proj(O_QC, W_C) * QK_SCALE).astype(BF16)
        z_ref[...] = proj(O_Z, D_MIX).astype(BF16)

    tok = lambda w: pl.BlockSpec((tm, w), lambda i: (i, 0))
    tab = pl.BlockSpec((tm, 128), lambda i: (i % spt, 0))
    hbm = pl.BlockSpec(memory_space=pl.ANY)
    return pl.pallas_call(
        body, name="in_proj", grid=(n_steps,),
        in_specs=[tok(D_MODEL), _full((D_IN, D_MODEL)), _full((1, D_IN)), tab, tab, tab, hbm, hbm],
        out_specs=(tok(D_MODEL), tok(W_A), tok(W_KV_A), tok(W_KV_A), tok(768),
                   pl.BlockSpec((1, 4, tm // 4, 768), lambda i: (i // spt, 0, i % spt, 0)),
                   pl.BlockSpec((1, 16, tm // 16, 768), lambda i: (i // spt, 0, i % spt, 0)),
                   tok(W_C), tok(D_MIX), hbm, hbm),
        out_shape=(_sds((T, D_MODEL), BF16), _sds((T, W_A), BF16), _sds((T, W_KV_A), BF16), _sds((T, W_KV_A), BF16),
                   _sds((T, 768), BF16), _sds((B_LOC, 4, SEQ // 4, 768), BF16), _sds((B_LOC, 16, SEQ // 16, 768), BF16),
                   _sds((T, W_C), BF16), _sds((T, D_MIX), BF16),
                   _sds((D_MIX, D_MODEL), BF16), _sds((D_MODEL, 2 * W_C), BF16)),
        input_output_aliases={6: 9, 7: 10},
        scratch_shapes=[pltpu.VMEM((12, tm, 128), F32)] + [pltpu.SemaphoreType.DMA((2, 3))] * 4,
        compiler_params=_cp(("arbitrary",), vmem_mb=48),
    )(*_pin(x, winT, b_in, cos, sa, sb, wout_own, wmem_own))


def _mem_kv(mem, wmem):
    def body(m_ref, w_ref, mb_ref, kv_ref):
        mb = m_ref[...].astype(BF16)
        mb_ref[...] = mb
        kv_ref[...] = _dot(mb, w_ref[...], NN).astype(BF16)

    n = B_LOC * MEM_LEN
    return pl.pallas_call(
        body, name="mem_kv",
        out_shape=(_sds((n, D_MODEL), BF16), _sds((n, 2 * W_C), BF16)),
    )(*_pin(mem, wmem))


class _Part:
    def __init__(self, body, args, in_specs, out_specs, out_shape, scratch=()):
        self.body, self.args, self.in_specs, self.out_specs, self.out_shape = body, args, in_specs, out_specs, out_shape
        self.scratch = list(scratch)


def _run_parts(name, parts, semantics, vmem_mb):
    n_in = [len(p.args) for p in parts]
    n_out = [len(p.out_shape) for p in parts]
    n_scr = [len(p.scratch) for p in parts]

    def body(*refs):
        ins, outs, scr = refs[:sum(n_in)], refs[sum(n_in):sum(n_in) + sum(n_out)], refs[sum(n_in) + sum(n_out):]
        i0 = o0 = s0 = 0
        for p, ni, no, ns in zip(parts, n_in, n_out, n_scr):
            p.body(*ins[i0:i0 + ni], *outs[o0:o0 + no], *scr[s0:s0 + ns])
            i0, o0, s0 = i0 + ni, o0 + no, s0 + ns

    res = pl.pallas_call(
        body, name=name, grid=(T // QR,),
        in_specs=[sp for p in parts for sp in p.in_specs], out_specs=tuple(sp for p in parts for sp in p.out_specs),
        out_shape=tuple(sh for p in parts for sh in p.out_shape),
        scratch_shapes=[sc for p in parts for sc in p.scratch],
        compiler_params=_cp((semantics,), vmem_mb=vmem_mb),
    )(*_pin(*[a for p in parts for a in p.args]))
    out, o0 = [], 0
    for no in n_out:
        out.append(tuple(res[o0:o0 + no]))
        o0 += no
    return out


QB = 8
QR = QB * BLK


def _lane_lo():
    return lax.broadcasted_iota(jnp.int32, (1, 128), 1) < 64


def _dup_head(k2, hk, lo):
    kf = k2.astype(F32)
    r = pltpu.roll(kf, 64, 1)
    return (jnp.where(lo, kf, r) if hk == 0 else jnp.where(lo, r, kf)).astype(BF16)


def _stack_heads(pairs, lo):
    parts = []
    for x2 in pairs:
        z = jnp.zeros_like(x2)
        parts += [jnp.where(lo, x2, z), jnp.where(lo, z, x2)]
    return jnp.concatenate(parts, axis=0)


def _prev_mode(kind, nb, j):
    if kind == "mem" or nb == 1:
        return "no"
    if nb <= QB:
        return "yes" if j % nb else "no"
    return "yes" if j else "dyn"


class _Attn:
    def __init__(self, kind, nb, max_dist, gqa, qw, kvw, qcb, kcb, vcb):
        self.kind, self.nb, self.gqa, self.qw, self.kvw = kind, nb, gqa, qw, kvw
        npairs = qw // 128
        self.groups = ([(hk, [2 * hk, 2 * hk + 1]) for hk in range(npairs // 2)] if gqa
                       else [(p, [p]) for p in range(npairs)])
        self.nh = 2 * len(self.groups[0][1])
        self.cols = 128 * self.nh
        self.reach = BLK - max_dist
        self.ext_prev = kind == "band" and nb > QB
        self.q_spec = pl.BlockSpec((QR, qw), lambda g: (g, qcb))
        self.row_spec = pl.BlockSpec((QR, qw), lambda g: (g, 0))
        self.stat_spec = pl.BlockSpec((QR, 128), lambda g: (g, 0))
        if kind == "mem":
            per = SEQ // QR
            self.kv_specs = [pl.BlockSpec((MEM_LEN, kvw), lambda g: (g // per, kcb)),
                             pl.BlockSpec((MEM_LEN, kvw), lambda g: (g // per, vcb))]
        else:
            self.kv_specs = [pl.BlockSpec((QR, kvw), lambda g: (g, kcb)), pl.BlockSpec((QR, kvw), lambda g: (g, vcb))]
            if self.ext_prev:
                self.kv_specs += [pl.BlockSpec((BLK, kvw), lambda g: (jnp.maximum(g * QB - 1, 0), kcb)),
                                  pl.BlockSpec((BLK, kvw), lambda g: (jnp.maximum(g * QB - 1, 0), vcb))]

    def masks(self):
        if self.kind == "mem":
            return None
        kj = lax.broadcasted_iota(jnp.int32, (2 * BLK, self.cols), 0)
        qi = lax.broadcasted_iota(jnp.int32, (2 * BLK, self.cols), 1) & (BLK - 1)
        kj1 = lax.broadcasted_iota(jnp.int32, (BLK, self.cols), 0)
        qi1 = lax.broadcasted_iota(jnp.int32, (BLK, self.cols), 1) & (BLK - 1)
        return kj, qi, kj1 <= qi1

    def keys(self, j, gi, kc_ref, vc_ref, kp_ref, vp_ref, lo, kq, g):
        def kv(k_ref, v_ref, r):
            if self.gqa:
                return _dup_head(k_ref[r, :], gi, lo), _dup_head(v_ref[r, :], gi, lo)
            sl = slice(128 * gi, 128 * (gi + 1))
            return k_ref[r, sl], v_ref[r, sl]

        if self.kind == "mem":
            key0 = pl.multiple_of((g // (SEQ // QR)) * MEM_LEN, MEM_LEN)
            return (*kv(kc_ref, vc_ref, slice(None)), None, [(0, MEM_LEN, key0)])
        kj, qi, cur = kq
        row0 = g * QR + BLK * j
        mode = _prev_mode(self.kind, self.nb, j)
        if mode == "no":
            return (*kv(kc_ref, vc_ref, slice(BLK * j, BLK * (j + 1))), cur, [(0, BLK, pl.multiple_of(row0, BLK))])
        if mode == "yes":
            mask = jnp.logical_and(kj >= qi + self.reach, kj <= qi + BLK)
            return (*kv(kc_ref, vc_ref, slice(BLK * (j - 1), BLK * (j + 1))), mask,
                    [(0, 2 * BLK, pl.multiple_of(row0 - BLK, BLK))])
        has_prev = ((g * QB) % self.nb) > 0
        hp = has_prev.astype(jnp.int32)
        mask = jnp.logical_and(kj >= qi * hp + (self.reach * hp + BLK * (1 - hp)), kj <= qi + BLK)
        kp, vp = kv(kp_ref, vp_ref, slice(None))
        kc, vc = kv(kc_ref, vc_ref, slice(0, BLK))
        return (jnp.concatenate([kp, kc], axis=0), jnp.concatenate([vp, vc], axis=0), mask,
                [(0, BLK, pl.multiple_of(jnp.maximum(row0 - BLK, 0), BLK)), (BLK, BLK, pl.multiple_of(row0, BLK))])


def _attn_fwd(q, qcb, qw, k, kcb, v, vcb, kvw, *, kind, nb=1, max_dist=BLK, gqa=False, sinks=None):
    a = _Attn(kind, nb, max_dist, gqa, qw, kvw, qcb, kcb, vcb)

    def body(*refs):
        it = iter(refs)
        q_ref, kc_ref, vc_ref = next(it), next(it), next(it)
        kp_ref, vp_ref = (next(it), next(it)) if a.ext_prev else (None, None)
        sink_ref = next(it) if sinks is not None else None
        o_ref, lse_ref = next(it), next(it)
        g = pl.program_id(0)
        lo = _lane_lo()
        top = lax.broadcasted_iota(jnp.int32, (128, 1), 0) < 64
        rid = lax.broadcasted_iota(jnp.int32, (8, 128), 0)
        kq = a.masks()
        stats = {}

        def scores(j, gi, pairs):
            rows = slice(BLK * j, BLK * (j + 1))
            qs = _stack_heads([q_ref[rows, 128 * p:128 * (p + 1)] for p in pairs], lo)
            kk, vv, mask, _ = a.keys(j, gi, kc_ref, vc_ref, kp_ref, vp_ref, lo, kq, g)
            pieces = [slice(r0, r0 + BLK) for r0 in range(0, kk.shape[0], BLK)]
            return dict(j=j, gi=gi, pairs=pairs, rows=rows, vv=vv, mask=mask, pieces=pieces,
                        ss=[_dot(kk[r], qs, NT) for r in pieces])

        def softmax(c):
            gi, mask = c["gi"], c["mask"]
            ss = [s if mask is None else jnp.where(mask[r], s, NEG) for r, s in zip(c["pieces"], c.pop("ss"))]
            m = jnp.max(ss[0], axis=0, keepdims=True)
            for s in ss[1:]:
                m = jnp.maximum(m, jnp.max(s, axis=0, keepdims=True))
            if sink_ref is not None:
                sk = jnp.concatenate([jnp.full((1, 128), sink_ref[0, a.nh * gi + i], F32) for i in range(a.nh)], axis=1)
                m = jnp.maximum(m, sk)
            ps = [jnp.exp(s - m) for s in ss]
            l = sum(jnp.sum(p, axis=0, keepdims=True) for p in ps)
            if sink_ref is not None:
                l = l + jnp.exp(sk - m)
            c["ps"] = [p.astype(BF16) for p in ps]
            c["l"], c["lse"] = l, m + jnp.log(l)

        def outputs(c):
            j, gi, rows = c["j"], c["gi"], c["rows"]
            ot = sum(_dot(c["vv"][r], p, TN) for r, p in zip(c["pieces"], c["ps"]))
            ot = ot * pl.reciprocal(c["l"], approx=True)
            for i, p in enumerate(c["pairs"]):
                o2t = jnp.where(top, ot[:, 256 * i:256 * i + 128], ot[:, 256 * i + 128:256 * i + 256])
                o_ref[rows, 128 * p:128 * (p + 1)] = o2t.T.astype(BF16)
            stat = stats.get(j, jnp.zeros((8, 128), F32))
            for i in range(a.nh):
                stat = jnp.where(rid == a.nh * gi + i, c["lse"][:, 128 * i:128 * (i + 1)], stat)
            stats[j] = stat
            if gi == a.groups[-1][0]:
                lse_ref[rows, :] = jnp.concatenate([stats.pop(j), jnp.zeros((120, 128), F32)], axis=0).T

        chains = [(j, gi, pairs) for j in range(QB) for gi, pairs in a.groups]
        live = {}
        for t in range(len(chains) + 2):
            if t < len(chains):
                live[t] = scores(*chains[t])
            if 0 <= t - 1 < len(chains):
                softmax(live[t - 1])
            if 0 <= t - 2 < len(chains):
                outputs(live.pop(t - 2))


    args = [q, k, v] + ([k, v] if a.ext_prev else [])
    in_specs = [a.q_spec] + a.kv_specs
    if sinks is not None:
        args.append(sinks)
        in_specs.append(pl.BlockSpec(memory_space=pltpu.SMEM))
    return _Part(body, args, in_specs, [a.row_spec, a.stat_spec], [_sds((T, qw), BF16), _sds((T, 128), F32)])


def _attn_bwd(q, qcb, qw, k, kcb, v, vcb, kvw, do, lse, dl, *, kind, nb=1, max_dist=BLK, gqa=False, sinkv=None,
              mem_in=None):
    a = _Attn(kind, nb, max_dist, gqa, qw, kvw, qcb, kcb, vcb)

    def body(*refs):
        it = iter(refs)
        q_ref, kc_ref, vc_ref = next(it), next(it), next(it)
        kp_ref, vp_ref = (next(it), next(it)) if a.ext_prev else (None, None)
        do_ref, lse_ref, dl_ref = next(it), next(it), next(it)
        sinkv_ref = next(it) if sinkv is not None else None
        mem_ref = next(it) if kind == "mem" else None
        dq_ref = next(it)
        if kind == "mem":
            gmem_ref = next(it)
        else:
            dk_out, dv_out = next(it), next(it)
        dsink_ref = next(it) if sinkv is not None else None
        if kind != "mem":
            dk_ref, dv_ref, stage_k, stage_v, flush_sem = next(it), next(it), next(it), next(it), next(it)
        else:
            dkv_ref = next(it)
        g = pl.program_id(0)
        lo = _lane_lo()
        top = lax.broadcasted_iota(jnp.int32, (128, 1), 0) < 64

        @pl.when(g == 0)
        def _():
            if kind == "mem":
                dkv_ref[...] = jnp.zeros_like(dkv_ref)
            else:
                dk_ref[...] = jnp.zeros_like(dk_ref)
                dv_ref[...] = jnp.zeros_like(dv_ref)
            if dsink_ref is not None:
                dsink_ref[...] = jnp.zeros_like(dsink_ref)

        kq = a.masks()
        stats_t = {}

        def first_matmuls(j, gi, pairs):
            rows = slice(BLK * j, BLK * (j + 1))
            if j not in stats_t:
                stats_t[j] = (lse_ref[rows, :].T, dl_ref[rows, :].T)
            lse_t, dl_t = stats_t[j]
            heads = [a.nh * gi + i for i in range(a.nh)]
            c = dict(rows=rows, gi=gi, pairs=pairs)
            c["qs"] = _stack_heads([q_ref[rows, 128 * p:128 * (p + 1)] for p in pairs], lo)
            c["dos"] = _stack_heads([do_ref[rows, 128 * p:128 * (p + 1)] for p in pairs], lo)
            c["lse_row"] = jnp.concatenate([lse_t[h:h + 1, :] for h in heads], axis=1)
            c["dl_row"] = jnp.concatenate([dl_t[h:h + 1, :] for h in heads], axis=1)
            c["kk"], vv, c["mask"], c["dests"] = a.keys(j, gi, kc_ref, vc_ref, kp_ref, vp_ref, lo, kq, g)
            c["s"] = _dot(c["kk"], c["qs"], NT)
            c["dp"] = _dot(vv, c["dos"], NT)
            return c

        def elementwise(c):
            s = c.pop("s")
            if c["mask"] is not None:
                s = jnp.where(c["mask"], s, NEG)
            p = jnp.exp(s - c["lse_row"])
            c["ds"] = (p * (c.pop("dp") - c["dl_row"])).astype(BF16)
            c["p"] = p.astype(BF16)

        def last_matmuls(c):
            gi, rows = c["gi"], c["rows"]
            dqt = _dot(c["kk"], c["ds"], TN)
            ck = _dot(c["ds"], c["qs"], NN)
            cv = _dot(c["p"], c["dos"], NN)
            if gqa:
                sel = lo if gi == 0 else jnp.logical_not(lo)
                ck = jnp.where(sel, ck + pltpu.roll(ck, 64, 1), 0.0)
                cv = jnp.where(sel, cv + pltpu.roll(cv, 64, 1), 0.0)
                kcols = slice(0, 128)
            else:
                kcols = slice(128 * gi, 128 * (gi + 1))
            for r0, nr, key0 in c["dests"]:
                krows = pl.ds(key0, nr)
                if kind == "mem":
                    dkv_ref[krows, kcols] += ck[r0:r0 + nr]
                    dkv_ref[krows, slice(kvw + kcols.start, kvw + kcols.stop)] += cv[r0:r0 + nr]
                else:
                    dk_ref[krows, kcols] += ck[r0:r0 + nr]
                    dv_ref[krows, kcols] += cv[r0:r0 + nr]
            for i, p in enumerate(c["pairs"]):
                dq2t = jnp.where(top, dqt[:, 256 * i:256 * i + 128], dqt[:, 256 * i + 128:256 * i + 256])
                dq_ref[rows, 128 * p:128 * (p + 1)] = dq2t.T.astype(BF16)

        chains = [(j, gi, pairs) for j in range(QB) for gi, pairs in a.groups]
        live = {}
        for t in range(len(chains) + 2):
            if t < len(chains):
                live[t] = first_matmuls(*chains[t])
            if 0 <= t - 1 < len(chains):
                elementwise(live[t - 1])
            if 0 <= t - 2 < len(chains):
                last_matmuls(live.pop(t - 2))
        if dsink_ref is not None:
            ps = jnp.exp(sinkv_ref[...] - lse_ref[...]) * dl_ref[...]
            dsink_ref[...] += jnp.sum(ps, axis=0, keepdims=True)
        if kind == "mem":
            @pl.when(g == T // QR - 1)
            def _():
                gmem_ref[...] = _dot(mem_ref[...], dkv_ref[...].astype(BF16), TN)
        else:
            n_steps = T // QR

            def flush(step):
                rows = pl.ds(pl.multiple_of(step * QR, QR), QR)
                out = []
                for acc, stage, dst, i in ((dk_ref, stage_k, dk_out, 0), (dv_ref, stage_v, dv_out, 1)):
                    stage[...] = acc[rows, :].astype(BF16)
                    out.append(pltpu.make_async_copy(stage, dst.at[rows, :], flush_sem.at[i]))
                return out

            def flushed(step):
                rows = pl.ds(pl.multiple_of(step * QR, QR), QR)
                return [pltpu.make_async_copy(stage, dst.at[rows, :], flush_sem.at[i])
                        for stage, dst, i in ((stage_k, dk_out, 0), (stage_v, dv_out, 1))]

            @pl.when(g >= 2)
            def _():
                for cp in flushed(g - 2):
                    cp.wait()

            @pl.when(g >= 1)
            def _():
                for cp in flush(g - 1):
                    cp.start()

            @pl.when(g == n_steps - 1)
            def _():
                for cp in flushed(g - 1):
                    cp.wait()
                for cp in flush(g):
                    cp.start()
                for cp in flushed(g):
                    cp.wait()

    args = [q, k, v] + ([k, v] if a.ext_prev else []) + [do, lse, dl]
    in_specs = [a.q_spec] + a.kv_specs + [a.row_spec, a.stat_spec, a.stat_spec]
    if sinkv is not None:
        args.append(sinkv)
        in_specs.append(_full((1, 128)))
    if kind == "mem":
        args.append(mem_in)
        in_specs.append(pl.BlockSpec(mem_in.shape, lambda g: (0, 0), pipeline_mode=pl.Buffered(1)))
    out_shape = [_sds((T, qw), BF16)]
    out_specs = [a.row_spec]
    scratch = []
    if kind == "mem":
        out_shape.append(_sds((D_MODEL, 2 * kvw), F32))
        out_specs.append(pl.BlockSpec((D_MODEL, 2 * kvw), lambda g: (0, 0), pipeline_mode=pl.Buffered(1)))
        scratch = [pltpu.VMEM((B_LOC * MEM_LEN, 2 * kvw), F32)]
    else:
        out_shape += [_sds((T, kvw), BF16)] * 2
        out_specs += [pl.BlockSpec(memory_space=pl.ANY)] * 2
        scratch = [pltpu.VMEM((T, kvw), F32)] * 2 + [pltpu.VMEM((QR, kvw), BF16)] * 2 + [pltpu.SemaphoreType.DMA((2,))]
    if sinkv is not None:
        out_shape.append(_sds((1, 128), F32))
        out_specs.append(_full((1, 128)))
    return _Part(body, args, in_specs, out_specs, out_shape, scratch)


def _dot2(v, w_ref):
    hi = v.astype(BF16)
    lo = (v - hi.astype(F32)).astype(BF16)
    return _dot(hi, w_ref[...], NN) + _dot(lo, w_ref[...], NN)


def _middle(oa, o1, l1, o4, l4, o16, l16, oc, z, x, tgt, g_br, ln_g, ln_b, wout, spread4, gather4, gather8):
    tm = 512
    spt = SEQ // tm

    def body(oa_ref, o1_ref, l1_ref, o4_ref, l4_ref, o16_ref, l16_ref, oc_ref, z_ref, x_ref, t_ref,
             g_ref, lg_ref, lb_ref, w_ref, sp4_ref, ga4_ref, ga8_ref,
             du_ref, dz_ref, doa_ref, dla_ref,
             dobn_ref, lsen_ref, dlbn_ref, dob4_ref, lse4_ref, dlb4_ref, dob16_ref, lse16_ref, dlb16_ref,
             doc_ref, dlc_ref, acc_ref, gout_ref, scr):
        i = pl.program_id(0)

        @pl.when(i == 0)
        def _():
            acc_ref[...] = jnp.zeros_like(acc_ref)
            gout_ref[...] = jnp.zeros_like(gout_ref)

        q = tm // 4
        for res in range(16):
            rows = pl.ds((res % 4) * q + res // 4, tm // 16, stride=4)
            for j in range(2):
                scr[6 + j, rows, :] = o16_ref[0, res, :, 128 * j:128 * (j + 1)].astype(F32)
            scr[8, rows, :] = l16_ref[0, res]
        for res in range(4):
            rows, blk = pl.ds(res, q, stride=4), slice(res * q, (res + 1) * q)
            for j in range(2):
                scr[j, rows, :] = o4_ref[0, res, :, 128 * j:128 * (j + 1)].astype(F32)
                scr[3 + j, rows, :] = scr[6 + j, blk, :]
            scr[2, rows, :] = l4_ref[0, res]
            scr[5, rows, :] = scr[8, blk, :]
        inv_d = 1.0 / D_MODEL
        gb, lg, lb = g_ref[...], lg_ref[...], lb_ref[...]

        def rms(o):
            r = lax.rsqrt(jnp.sum(o * o, axis=1, keepdims=True) * (1.0 / o.shape[1]) + RMS_EPS)
            return o * r, r

        def rms_bwd(dn_, n_, r):
            return r * (dn_ - n_ * (jnp.sum(dn_ * n_, axis=1, keepdims=True) * (1.0 / n_.shape[1])))

        def forward(rs):
            o4v = jnp.concatenate([scr[0, rs, :], scr[1, rs, :]], axis=1)
            o16v = jnp.concatenate([scr[3, rs, :], scr[4, rs, :]], axis=1)
            l1v, l4v, l16v = l1_ref[rs, :], scr[2, rs, :], scr[5, rs, :]
            mx = jnp.maximum(jnp.maximum(l1v, l4v), l16v)
            e1, e4, e16 = jnp.exp(l1v - mx), jnp.exp(l4v - mx), jnp.exp(l16v - mx)
            ssum = e1 + e4 + e16
            inv = 1.0 / ssum
            c = dict(rs=rs, lse_b=mx + jnp.log(ssum))
            c["ob"] = (_dot2(e1 * inv, sp4_ref) * o1_ref[rs, :].astype(F32) + _dot2(e4 * inv, sp4_ref) * o4v
                       + _dot2(e16 * inv, sp4_ref) * o16v)
            c["oa"], c["oc"] = oa_ref[rs, :].astype(F32), oc_ref[rs, :].astype(F32)
            na, c["ra"] = rms(c["oa"])
            nb_, c["rb"] = rms(c["ob"])
            nc, c["rc"] = rms(c["oc"])
            c["n"] = jnp.concatenate([na, nb_, nc], axis=1)
            c["zf"] = z_ref[rs, :].astype(F32)
            c["sig"] = 1.0 / (1.0 + jnp.exp(-c["zf"]))
            c["sz"] = c["zf"] * c["sig"]
            c["yb"] = (c["n"] * gb * c["sz"]).astype(BF16)
            c["y2"] = _dot(c["yb"], w_ref[...], NN)
            return c

        def norm(c):
            rs = c["rs"]
            u = ALPHA * x_ref[rs, :] + c.pop("y2")
            mu = jnp.sum(u, axis=1, keepdims=True) * inv_d
            uc = u - mu
            rstd = lax.rsqrt(jnp.sum(uc * uc, axis=1, keepdims=True) * inv_d + LN_EPS)
            xh = uc * rstd
            diff = xh * lg + lb - t_ref[rs, :]
            acc_ref[0:1, :] += jnp.sum(diff * diff, axis=0, keepdims=True) * (0.5 * inv_d)
            dout = diff * inv_d
            acc_ref[2:3, :] += jnp.sum(dout * xh, axis=0, keepdims=True)
            acc_ref[3:4, :] += jnp.sum(dout, axis=0, keepdims=True)
            dxh = dout * lg
            du = rstd * (dxh - jnp.sum(dxh, axis=1, keepdims=True) * inv_d
                         - xh * (jnp.sum(dxh * xh, axis=1, keepdims=True) * inv_d))
            dub = du.astype(BF16)
            du_ref[rs, :] = dub
            c["dy"] = _dot(dub, w_ref[...], NT)
            gout_ref[...] += _dot(c.pop("yb"), dub, TN)

        def backward(c):
            rs, n, dy, zf, sig = c["rs"], c["n"], c["dy"], c["zf"], c["sig"]
            t1 = dy * c["sz"]
            acc_ref[1:2, :] += jnp.sum(t1 * n, axis=0, keepdims=True)
            dn = t1 * gb
            dz_ref[rs, :] = (dy * n * gb * (sig * (1.0 + zf * (1.0 - sig)))).astype(BF16)
            doa = rms_bwd(dn[:, :W_A], n[:, :W_A], c["ra"])
            dob = rms_bwd(dn[:, W_A:W_A + W_B], n[:, W_A:W_A + W_B], c["rb"])
            doc = rms_bwd(dn[:, W_A + W_B:], n[:, W_A + W_B:], c["rc"])
            doa_ref[rs, :] = doa.astype(BF16)
            dla_ref[rs, :] = _dot2(doa * c["oa"], ga8_ref)
            doc_ref[rs, :] = doc.astype(BF16)
            dlc_ref[rs, :] = _dot2(doc * c["oc"], ga4_ref)
            dobn_ref[rs, :] = dob.astype(BF16)
            lsen_ref[rs, :] = c["lse_b"]
            dlbn_ref[rs, :] = _dot2(dob * c["ob"], ga4_ref)
            scr[0, rs, :] = dob[:, :128]
            scr[1, rs, :] = dob[:, 128:]

        halves = [slice(h * (tm // 2), (h + 1) * (tm // 2)) for h in range(2)]
        live = {}
        for t in range(len(halves) + 2):
            if t < len(halves):
                live[t] = forward(halves[t])
            if 0 <= t - 1 < len(halves):
                norm(live[t - 1])
            if 0 <= t - 2 < len(halves):
                backward(live.pop(t - 2))
        for j in range(2):
            sl = slice(128 * j, 128 * (j + 1))
            for res in range(4):
                t = scr[j, pl.ds(res, q, stride=4), :]
                dob4_ref[0, res, :, sl] = t.astype(BF16)
                scr[6 + j, res * q:(res + 1) * q, :] = t
            for res in range(16):
                dob16_ref[0, res, :, sl] = scr[6 + j, pl.ds((res % 4) * q + res // 4, tm // 16, stride=4),
                                               :].astype(BF16)
        for res in range(4):
            rows = pl.ds(res, q, stride=4)
            lse4_ref[0, res] = lsen_ref[rows, :]
            dlb4_ref[0, res] = dlbn_ref[rows, :]
        for res in range(16):
            rows = pl.ds(res // 4, tm // 16, stride=4)
            lse16_ref[0, res] = lse4_ref[0, res % 4, rows, :]
            dlb16_ref[0, res] = dlb4_ref[0, res % 4, rows, :]


    tok = lambda w: pl.BlockSpec((tm, w), lambda i: (i, 0))
    p4 = lambda w: pl.BlockSpec((1, 4, tm // 4, w), lambda i: (i // spt, 0, i % spt, 0))
    p16 = lambda w: pl.BlockSpec((1, 16, tm // 16, w), lambda i: (i // spt, 0, i % spt, 0))
    s4 = lambda w, dt: _sds((B_LOC, 4, SEQ // 4, w), dt)
    s16 = lambda w, dt: _sds((B_LOC, 16, SEQ // 16, w), dt)
    row = _full((1, D_MODEL))
    return pl.pallas_call(
        body, name="middle", grid=(T // tm,),
        in_specs=[tok(W_A), tok(W_B), tok(128), p4(W_B), p4(128), p16(W_B), p16(128), tok(W_C), tok(D_MIX),
                  tok(D_MODEL), tok(D_MODEL), row, row, row, _full((D_MIX, D_MODEL)),
                  _full((128, W_B)), _full((W_B, 128)), _full((W_A, 128))],
        out_specs=(tok(D_MODEL), tok(D_MIX), tok(W_A), tok(128),
                   tok(W_B), tok(128), tok(128), p4(W_B), p4(128), p4(128), p16(W_B), p16(128), p16(128),
                   tok(W_C), tok(128), _full((8, D_MODEL)), _full((D_MIX, D_MODEL))),
        out_shape=(_sds((T, D_MODEL), BF16), _sds((T, D_MIX), BF16),
                   _sds((T, W_A), BF16), _sds((T, 128), F32),
                   _sds((T, W_B), BF16), _sds((T, 128), F32), _sds((T, 128), F32),
                   s4(W_B, BF16), s4(128, F32), s4(128, F32), s16(W_B, BF16), s16(128, F32), s16(128, F32),
                   _sds((T, W_C), BF16), _sds((T, 128), F32), _sds((8, D_MODEL), F32),
                   _sds((D_MIX, D_MODEL), F32)),
        scratch_shapes=[pltpu.VMEM((9, tm, 128), F32)],
        compiler_params=_cp(("arbitrary",), vmem_mb=56),
    )(*_pin(oa, o1, l1, o4, l4, o16, l16, oc, z, x, tgt, g_br, ln_g, ln_b, wout, spread4, gather4, gather8))


class _ReduceScatter:
    def __init__(self, shapes):
        self.shapes = shapes

    def scratch_shapes(self):
        out = []
        for n, w in self.shapes:
            h, p = n // 2, n // 4
            out += [pltpu.VMEM((4, h, w), F32), pltpu.VMEM((4, h, w), F32), pltpu.VMEM((6, p, w), BF16),
                    pltpu.VMEM((6, p, w), BF16), pltpu.VMEM((2, p, w), F32), pltpu.VMEM((h, w), F32)]
        na = len(self.shapes)
        dma = pltpu.SemaphoreType.DMA
        return out + [dma((na, 4)), dma((na, 4)), dma((na, 4)), dma((na, 6)), dma((na, 6)), dma((na,)), dma((na,)),
                      dma((na,))]

    def bind(self, g_refs, r_refs, scratch):
        na = len(self.shapes)
        bufs = [scratch[6 * a:6 * a + 6] for a in range(na)]
        mine, sib, stage, land, keep, tot = (tuple(b[i] for b in bufs) for i in range(6))
        loc_sem, s1_send, s1_recv, s2_send, s2_recv, s3_send, s3_recv, st_sem = scratch[6 * na:6 * na + 8]
        x, y, c = lax.axis_index("x"), lax.axis_index("y"), lax.axis_index("c")
        me, sibling = (x, y, c), (x, y, 1 - c)
        xn, yn, dg = (1 - x, y), (x, 1 - y), (1 - x, 1 - y)
        idx = lambda chip: 2 * chip[0] + chip[1]
        my_chip = idx((x, y))
        order = [idx(xn), idx(dg), idx(yn), my_chip]

        def rows(a, k, half):
            n = self.shapes[a][0]
            return pl.ds(pl.multiple_of(k * n + half * (n // 2), 8), n // 2)

        def piece(a, q):
            p = self.shapes[a][0] // 4
            return slice(q * p, (q + 1) * p)

        def load(a, k):
            return pltpu.make_async_copy(g_refs[a].at[rows(a, k, c), :], mine[a].at[k], loc_sem.at[a, k])

        def s1(a, k, half):
            return pltpu.make_async_remote_copy(
                src_ref=g_refs[a].at[rows(a, k, half), :], dst_ref=sib[a].at[k],
                send_sem=s1_send.at[a, k], recv_sem=s1_recv.at[a, k], device_id=sibling, device_id_type=MESH)

        def s2(a, i, to):
            return pltpu.make_async_remote_copy(
                src_ref=stage[a].at[i], dst_ref=land[a].at[i], send_sem=s2_send.at[a, i], recv_sem=s2_recv.at[a, i],
                device_id=to, device_id_type=MESH)

        via = {0: xn, 1: xn, 2: yn, 3: yn, 4: yn, 5: xn}

        def s3(a, half, to):
            return pltpu.make_async_remote_copy(
                src_ref=tot[a], dst_ref=r_refs[a].at[rows(a, 0, half), :], send_sem=s3_send.at[a],
                recv_sem=s3_recv.at[a], device_id=to, device_id_type=MESH)

        def store(a):
            return pltpu.make_async_copy(tot[a], r_refs[a].at[rows(a, 0, c), :], st_sem.at[a])

        def start():
            for k in order:
                for a in range(na):
                    load(a, k).start()
                    s1(a, k, 1 - c).start()

        def chip_sum(a, k):
            load(a, k).wait()
            s1(a, k, c).wait_recv()
            return mine[a][k] + sib[a][k]

        def exchange():
            for a in range(na):
                P, Q = piece(a, 0), piece(a, 1)
                s_xn = chip_sum(a, idx(xn))
                stage[a][0] = s_xn[P].astype(BF16)
                keep[a][1] = s_xn[Q]
                s_dg = chip_sum(a, idx(dg))
                stage[a][1] = s_dg[P].astype(BF16)
                s2(a, 0, (*xn, c)).start()
                s2(a, 1, (*xn, c)).start()
                stage[a][3] = s_dg[Q].astype(BF16)
                s_yn = chip_sum(a, idx(yn))
                stage[a][2] = s_yn[Q].astype(BF16)
                keep[a][0] = s_yn[P]
                s2(a, 2, (*yn, c)).start()
                s2(a, 3, (*yn, c)).start()
                tot[a][...] = chip_sum(a, my_chip)

        def relay():
            for a in range(na):
                P, Q = piece(a, 0), piece(a, 1)
                s2(a, 1, me).wait_recv()
                stage[a][4] = (keep[a][0] + land[a][1].astype(F32)).astype(BF16)
                s2(a, 4, (*yn, c)).start()
                s2(a, 3, me).wait_recv()
                stage[a][5] = (keep[a][1] + land[a][3].astype(F32)).astype(BF16)
                s2(a, 5, (*xn, c)).start()
                s2(a, 0, me).wait_recv()
                tot[a][P, :] += land[a][0].astype(F32)
                s2(a, 2, me).wait_recv()
                tot[a][Q, :] += land[a][2].astype(F32)

        def finish():
            for a in range(na):
                P, Q = piece(a, 0), piece(a, 1)
                s2(a, 4, me).wait_recv()
                tot[a][P, :] += land[a][4].astype(F32)
                s2(a, 5, me).wait_recv()
                tot[a][Q, :] += land[a][5].astype(F32)
                s3(a, c, sibling).start()
                store(a).start()

        def drain():
            for a in range(na):
                s3(a, 1 - c, me).wait_recv()
                store(a).wait()
            for a in range(na):
                for k in order:
                    s1(a, k, 1 - c).wait_send()
                for i in range(6):
                    s2(a, i, (*via[i], c)).wait_send()
                s3(a, c, sibling).wait_send()

        return start, exchange, relay, finish, drain

    def part(self, grads, steps):
        def body(*refs):
            na = len(self.shapes)
            i = pl.program_id(0)
            for step, phase in zip(steps, self.bind(refs[:na], refs[na:2 * na], refs[2 * na:])):
                pl.when(i == step)(phase)

        hbm = pl.BlockSpec(memory_space=pl.ANY)
        return _Part(body, list(grads), [hbm] * len(grads), [hbm] * len(grads),
                     [_sds((n, w), F32) for n, w in self.shapes], self.scratch_shapes())


def _dh_dx(dqa, dka, dva, dqn, dkn, dvn, dq4, dk4, dv4, dq16, dk16, dv16, dqc, dz, du, xb, cos, sa, sb, winT):
    tm = 512
    spt = SEQ // tm

    def body(dqa_ref, dka_ref, dva_ref, dqn_ref, dkn_ref, dvn_ref, dq4_ref, dk4_ref, dv4_ref,
             dq16_ref, dk16_ref, dv16_ref, dqc_ref, dz_ref, du_ref, xb_ref, cos_ref, sa_ref, sb_ref, w_ref,
             gx_ref, db_ref, gin_ref, dh_ref, scr):
        i = pl.program_id(0)

        @pl.when(i == 0)
        def _():
            db_ref[...] = jnp.zeros_like(db_ref)
            gin_ref[...] = jnp.zeros_like(gin_ref)

        cos_t, sa_t, sb_t = cos_ref[...], sa_ref[...], sb_ref[...]

        def rope_t(t):
            return _rope(t, cos_t, sa_t, sb_t, -1)

        def put(r0, val):
            n = val.shape[1]
            dh_ref[:, r0:r0 + n] = val.astype(BF16)
            db_ref[:, r0:r0 + n] += jnp.sum(val, axis=0, keepdims=True)

        put(O_QA, rope_t(dqa_ref[...].astype(F32)) * QK_SCALE)
        put(O_KA, rope_t(dka_ref[...].astype(F32)))
        put(O_VA, dva_ref[...].astype(F32))
        put(O_QC, dqc_ref[...].astype(F32) * QK_SCALE)
        put(O_Z, dz_ref[...].astype(F32))
        for k, (n_ref, r4, r16) in enumerate(((dqn_ref, dq4_ref, dq16_ref), (dkn_ref, dk4_ref, dk16_ref),
                                               (dvn_ref, dv4_ref, dv16_ref))):
            for j in range(2):
                sl = slice(128 * j, 128 * (j + 1))
                a, q = 2 * k + j, tm // 4
                scr[a] = n_ref[:, sl].astype(F32)
                for res in range(16):
                    scr[6 + a, pl.ds((res % 4) * q + res // 4, tm // 16, stride=4), :] = r16[0, res, :, sl].astype(F32)
                for res in range(4):
                    scr[a, pl.ds(res, q, stride=4), :] += (scr[6 + a, res * q:(res + 1) * q, :]
                                                           + r4[0, res, :, sl].astype(F32))
        cat = lambda a: jnp.concatenate([scr[a], scr[a + 1]], axis=1)
        put(O_QB, rope_t(cat(0)) * QK_SCALE)
        put(O_KB, rope_t(cat(2)))
        put(O_VB, cat(4))
        gx_ref[...] = _dot(dh_ref[...], w_ref[...], NN) + ALPHA * du_ref[...].astype(F32)
        gin_ref[...] += _dot(dh_ref[...], xb_ref[...], TN)

    tok = lambda w: pl.BlockSpec((tm, w), lambda i: (i, 0))
    tab = pl.BlockSpec((tm, 128), lambda i: (i % spt, 0))
    p4 = pl.BlockSpec((1, 4, tm // 4, W_B), lambda i: (i // spt, 0, i % spt, 0))
    p16 = pl.BlockSpec((1, 16, tm // 16, W_B), lambda i: (i // spt, 0, i % spt, 0))
    once = lambda shape: pl.BlockSpec(shape, lambda i: (0, 0), pipeline_mode=pl.Buffered(1))
    return pl.pallas_call(
        body, name="dh_dx", grid=(T // tm,),
        in_specs=[tok(W_A), tok(W_KV_A), tok(W_KV_A), tok(W_B), tok(W_B), tok(W_B), p4, p4, p4, p16, p16, p16,
                  tok(W_C), tok(D_MIX), tok(D_MODEL), tok(D_MODEL), tab, tab, tab, once((D_IN, D_MODEL))],
        out_specs=(tok(D_MODEL), _full((1, D_IN)), once((D_IN, D_MODEL))),
        out_shape=(_sds((T, D_MODEL), F32), _sds((1, D_IN), F32), _sds((D_IN, D_MODEL), F32)),
        scratch_shapes=[pltpu.VMEM((tm, D_IN), BF16), pltpu.VMEM((12, tm, 128), F32)],
        compiler_params=_cp(("arbitrary",), vmem_mb=56),
    )(*_pin(dqa, dka, dva, dqn, dkn, dvn, dq4, dk4, dv4, dq16, dk16, dv16, dqc, dz, du, xb, cos, sa, sb, winT))


def _reduce_grads(g_in, acc, dbin, dsink):
    rs = _ReduceScatter([(SH_IN, D_MODEL)])

    def body(g_ref, acc_ref, dbin_ref, dsink_ref, r_ref, sv_ref, sv_mine, sv_all, sv_send, sv_recv, *rs_scratch):
        x, y, c = lax.axis_index("x"), lax.axis_index("y"), lax.axis_index("c")
        chips = [(1 - x, y), (x, 1 - y), (1 - x, 1 - y)]
        start, exchange, relay, finish, drain = rs.bind((g_ref,), (r_ref,), rs_scratch)
        start()

        sv_mine[...] = jnp.zeros_like(sv_mine)
        sv_mine[0:4, 0:D_MODEL] = acc_ref[0:4, :]
        sv_mine[4:5, 0:D_IN] = dbin_ref[...]
        sv_mine[5:6, 0:128] = dsink_ref[...]
        my_dev = 4 * x + 2 * y + c
        others = [(x, y, 1 - c)] + [(*chip, cc) for chip in chips for cc in (c, 1 - c)]

        def sv_copy(j, to):
            return pltpu.make_async_remote_copy(
                src_ref=sv_mine, dst_ref=sv_all.at[my_dev], send_sem=sv_send.at[j], recv_sem=sv_recv.at[j],
                device_id=to, device_id_type=MESH)

        sv_sends = [sv_copy(j, to) for j, to in enumerate(others)]
        for cp in sv_sends:
            cp.start()
        exchange()
        relay()
        finish()
        sv_all[my_dev] = sv_mine[...]
        for j in range(7):
            sv_copy(j, (x, y, c)).wait_recv()
        tot = sv_all[0]
        for d in range(1, 8):
            tot = tot + sv_all[d]
        sv_ref[...] = tot
        drain()
        for cp in sv_sends:
            cp.wait_send()

    vm = pl.BlockSpec(memory_space=pltpu.VMEM)
    hbm = pl.BlockSpec(memory_space=pl.ANY)
    return pl.pallas_call(
        body, name="reduce_grads",
        out_shape=(_sds((SH_IN, D_MODEL), F32), _vm_sds((8, SV_W), F32)),
        in_specs=[hbm, vm, vm, vm], out_specs=(hbm, vm),
        scratch_shapes=[pltpu.VMEM((8, SV_W), F32), pltpu.VMEM((8, 8, SV_W), F32),
                        pltpu.SemaphoreType.DMA((7,)), pltpu.SemaphoreType.DMA((7,))] + rs.scratch_shapes(),
        compiler_params=_cp(vmem_mb=40),
    )(pltpu.with_memory_space_constraint(g_in, pltpu.HBM), acc, dbin, dsink)


def _adamw_update(w, g, m, v):
    nm = ADAM_B1 * m + (1.0 - ADAM_B1) * g
    nv = ADAM_B2 * v + (1.0 - ADAM_B2) * (g * g)
    m_hat = nm / (1.0 - ADAM_B1 ** ADAM_STEP)
    v_hat = nv / (1.0 - ADAM_B2 ** ADAM_STEP)
    return -ADAM_LR * (m_hat / (jnp.sqrt(v_hat) + ADAM_EPS) + ADAM_WD * w), nm, nv


SMALL = ((4, D_IN, 1.0), (5, 8, -1.0), (1, D_MIX, 1.0), (2, D_MODEL, 1.0), (3, D_MODEL, 1.0))


def _adamw_all(items, sv, ws, ms, vs, n_steps=4):
    nb, ns = 4 * len(items), len(SMALL)

    def body(*refs):
        ins, sv_ref, small_in = refs[:nb], refs[nb], refs[nb + 1:nb + 1 + 3 * ns]
        outs = refs[nb + 1 + 3 * ns:]
        big_out, loss_ref, small_out = outs[:nb], outs[nb], outs[nb + 1:]
        for p in range(len(items)):
            w_ref, g_ref, m_ref, v_ref = ins[4 * p:4 * p + 4]
            gv = g_ref[...]
            big_out[4 * p][...] = gv
            big_out[4 * p + 1][...], big_out[4 * p + 2][...], big_out[4 * p + 3][...] = _adamw_update(
                w_ref[...], gv, m_ref[...], v_ref[...])

        @pl.when(pl.program_id(0) == 0)
        def _():
            loss_ref[...] = jnp.sum(sv_ref[0:1, 0:D_MODEL], axis=1, keepdims=True)
            for p, (row, width, sign) in enumerate(SMALL):
                gv = sign * sv_ref[row:row + 1, 0:width]
                small_out[4 * p][...] = gv
                small_out[4 * p + 1][...], small_out[4 * p + 2][...], small_out[4 * p + 3][...] = _adamw_update(
                    small_in[p][...], gv, small_in[ns + p][...], small_in[2 * ns + p][...])

    specs, shapes, args = [], [], []
    for w, g, m, v in items:
        rows, width = w.shape
        specs += [pl.BlockSpec((rows // n_steps, width), lambda i: (i, 0))] * 4
        shapes += [_sds((rows, width), F32)] * 4
        args += [w, g, m, v]
    small_args = [*ws, *ms, *vs]
    whole = lambda a: _full(a.shape)
    res = pl.pallas_call(
        body, name="adamw", grid=(n_steps,),
        in_specs=specs + [whole(sv)] + [whole(a) for a in small_args],
        out_specs=tuple(specs + [_full((1, 1))] + [whole(w) for w in ws for _ in range(4)]),
        out_shape=tuple(shapes + [_sds((1, 1), F32)] + [_sds(w.shape, F32) for w in ws for _ in range(4)]),
        compiler_params=_cp(("arbitrary",), vmem_mb=40),
    )(*_pin(*args, sv, *small_args))
    big = [tuple(res[4 * p:4 * p + 4]) for p in range(len(items))]
    return big, res[nb], [tuple(res[nb + 1 + 4 * p:nb + 5 + 4 * p]) for p in range(ns)]


def _rope_tables():
    pos = np.arange(SEQ, dtype=np.float32)
    inv = (np.float32(ROPE_THETA) ** (-np.arange(0, 64, 2, dtype=np.float32) / np.float32(64))).astype(np.float32)
    ang = np.tile(pos[:, None] * inv[None, :], (1, 4))
    cos, sin = np.cos(ang).astype(np.float32), np.sin(ang).astype(np.float32)
    low = (np.arange(128) % 64) < 32
    zero = np.float32(0.0)
    return jnp.asarray(cos), jnp.asarray(np.where(low, -sin, zero)), jnp.asarray(np.where(low, zero, sin))


def _local_step(x2, mem2, tgt2, winT, wout, wmem, b_in, sinks, g_branch, ln_gain, ln_bias):
    cos, sa, sb = _rope_tables()
    sinkv = jnp.pad(sinks, ((0, 0), (0, 120)))
    head_of_lane = np.arange(512)[:, None] // 64
    gather8 = jnp.asarray(head_of_lane == np.arange(128)[None, :], BF16)
    gather4 = jnp.asarray(head_of_lane[:W_B] == np.arange(128)[None, :], BF16)
    spread4 = jnp.asarray((head_of_lane[:W_B] == np.arange(128)[None, :]).T, BF16)

    xb, qa, ka, va, bn, b4, b16, qc, z, wout, wmem = _in_proj(x2, winT, b_in, cos, sa, sb, wout, wmem)
    memb, mkv = _mem_kv(mem2, wmem)
    b4f, b16f = b4.reshape(T, 768), b16.reshape(T, 768)

    swa = dict(kind="band", nb=SEQ // BLK, max_dist=BLK - 1, gqa=True)
    dil = (dict(kind="band", nb=SEQ // BLK), dict(kind="band", nb=SEQ // 4 // BLK), dict(kind="band", nb=1))
    (oa, lse_a), (o1, l1), (o4, l4), (o16, l16), (oc, lse_c) = _run_parts("attn_fwd", [
        _attn_fwd(qa, 0, W_A, ka, 0, va, 0, W_KV_A, sinks=sinks, **swa),
        _attn_fwd(bn, 0, W_B, bn, 1, bn, 2, W_B, **dil[0]),
        _attn_fwd(b4f, 0, W_B, b4f, 1, b4f, 2, W_B, **dil[1]),
        _attn_fwd(b16f, 0, W_B, b16f, 1, b16f, 2, W_B, **dil[2]),
        _attn_fwd(qc, 0, W_C, mkv, 0, mkv, 1, W_C, kind="mem")], "parallel", 48)

    s4 = lambda w: (B_LOC, 4, SEQ // 4, w)
    s16 = lambda w: (B_LOC, 16, SEQ // 16, w)
    (du, dz, doa, dla, dobn, lsen, dlbn, dob4, lse4, dlb4, dob16, lse16, dlb16, doc, dlc, acc, g_out) = _middle(
        oa, o1, l1, o4.reshape(s4(W_B)), l4.reshape(s4(128)), o16.reshape(s16(W_B)), l16.reshape(s16(128)), oc, z,
        x2, tgt2, g_branch, ln_gain, ln_bias, wout, spread4, gather4, gather8)

    flat = lambda a: a.reshape(T, a.shape[-1])
    (dqa, dka, dva, dsink), (dqc, g_mem) = _run_parts("attn_bwd_a", [
        _attn_bwd(qa, 0, W_A, ka, 0, va, 0, W_KV_A, doa, lse_a, dla, sinkv=sinkv, **swa),
        _attn_bwd(qc, 0, W_C, mkv, 0, mkv, 1, W_C, doc, lse_c, dlc, kind="mem", mem_in=memb)], "arbitrary", 48)
    last = T // QR - 1
    (r_out, r_mem), (dqn, dkn, dvn), (dq4, dk4, dv4), (dq16, dk16, dv16) = _run_parts("attn_bwd_b", [
        _ReduceScatter([(SH_OUT, D_MODEL), (SH_MEM, 2 * W_C)]).part((g_out, g_mem), (0, 1, 2, last, last)),
        _attn_bwd(bn, 0, W_B, bn, 1, bn, 2, W_B, dobn, lsen, dlbn, **dil[0]),
        _attn_bwd(b4f, 0, W_B, b4f, 1, b4f, 2, W_B, flat(dob4), flat(lse4), flat(dlb4), **dil[1]),
        _attn_bwd(b16f, 0, W_B, b16f, 1, b16f, 2, W_B, flat(dob16), flat(lse16), flat(dlb16), **dil[2])],
        "arbitrary", 62)

    r4 = lambda a: a.reshape(s4(W_B))
    r16 = lambda a: a.reshape(s16(W_B))
    gx, dbin, g_in = _dh_dx(dqa, dka, dva, dqn, dkn, dvn, r4(dq4), r4(dk4), r4(dv4), r16(dq16), r16(dk16),
                            r16(dv16), dqc, dz, du, xb, cos, sa, sb, winT)
    return gx, g_in, r_out, r_mem, acc, dbin, dsink


def kernel(x, mem, w_in, b_in, w_mem, attn_sinks, g_branch, w_out, ln_gain, ln_bias, loss_target, m_w_in, m_b_in, m_w_mem, m_attn_sinks, m_g_branch, m_w_out, m_ln_gain, m_ln_bias, v_w_in, v_b_in, v_w_mem, v_attn_sinks, v_g_branch, v_w_out, v_ln_gain, v_ln_bias):
    winT, wout, wmem = _gather_weights(w_in[0].T, w_out[0], w_mem[0])
    gx, g_in, r_out, r_mem, acc, dbin, dsink = _local_step(
        x.reshape(T, D_MODEL), mem.reshape(B_LOC * MEM_LEN, D_MODEL), loss_target.reshape(T, D_MODEL),
        winT, wout, wmem, b_in, attn_sinks, g_branch, ln_gain, ln_bias)
    r_in, sv = _reduce_grads(g_in, acc, dbin, dsink)

    small = ["b_in", "attn_sinks", "g_branch", "ln_gain", "ln_bias"]
    big, loss, steps = _adamw_all(
        [(w_in[0].T, r_in, m_w_in[0].T, v_w_in[0].T), (w_out[0], r_out, m_w_out[0], v_w_out[0]),
         (w_mem[0], r_mem, m_w_mem[0], v_w_mem[0])],
        sv, [b_in, attn_sinks, g_branch, ln_gain, ln_bias], [m_b_in, m_attn_sinks, m_g_branch, m_ln_gain, m_ln_bias],
        [v_b_in, v_attn_sinks, v_g_branch, v_ln_gain, v_ln_bias])
    out = dict(zip(small, steps))
    out["w_in"] = tuple(a.T[None] for a in big[0])
    out["w_out"], out["w_mem"] = (tuple(a[None] for a in st) for st in big[1:])
    names = ["w_in", "b_in", "w_mem", "attn_sinks", "g_branch", "w_out", "ln_gain", "ln_bias"]
    return (loss.reshape(()), gx.reshape(B_LOC, SEQ, D_MODEL), *[out[n][k] for k in range(4) for n in names])
```

```python
import jax
import jax.numpy as jnp
import numpy as np
from jax import lax
from jax.experimental import pallas as pl
from jax.experimental.pallas import tpu as pltpu

F32, BF16 = jnp.float32, jnp.bfloat16

D_MODEL = 1024
SEQ = 2048
B_LOC = 2
T = B_LOC * SEQ
BLK = 128
MEM_LEN = 256
W_A, W_KV_A, W_B, W_C, D_MIX = 512, 128, 256, 256, 1024
D_IN = 2816
O_QA, O_KA, O_VA, O_QB, O_KB, O_VB, O_QC, O_Z = 0, 512, 640, 768, 1024, 1280, 1536, 1792
ROPE_THETA = 10000.0
LN_EPS = 1e-5
RMS_EPS = 1e-6
ALPHA = 2.0 ** 0.25
QK_SCALE = 0.125
N_CHIP = 4
SH_IN, SH_OUT, SH_MEM = D_IN // N_CHIP, D_MIX // N_CHIP, D_MODEL // N_CHIP
NEG = -1e30
ADAM_LR, ADAM_B1, ADAM_B2, ADAM_EPS, ADAM_WD, ADAM_STEP = 0.001, 0.9, 0.999, 1e-08, 0.01, 10
SV_W = 3072
MESH = pl.DeviceIdType.MESH

NN = ((1,), (0,))
NT = ((1,), (1,))
TN = ((0,), (0,))


def _dot(a, b, dims):
    return lax.dot_general(a, b, (dims, ((), ())), preferred_element_type=F32)


def _cp(sem=None, vmem_mb=None):
    kw = {}
    if sem is not None:
        kw["dimension_semantics"] = sem
    if vmem_mb is not None:
        kw["vmem_limit_bytes"] = vmem_mb * 1024 * 1024
    return pltpu.CompilerParams(**kw)


def _sds(shape, dtype):
    return pltpu.HBM(shape, dtype)


def _vm_sds(shape, dtype):
    return jax.ShapeDtypeStruct(shape, dtype)


def _pin(*args):
    return [pltpu.with_memory_space_constraint(a, pltpu.HBM) for a in args]


def _full(shape):
    n = len(shape)
    return pl.BlockSpec(shape, lambda *_: (0,) * n)


def _shard_rows(ref, n, chip, half):
    start = pl.multiple_of((2 * chip[0] + chip[1]) * n + half * (n // 2), 16)
    return ref.at[pl.ds(start, n // 2), :]


def _gather_weights(win_sh, wout_sh, wmem_sh):
    half, piece = SH_IN // 2, SH_IN // 4
    shards = ((SH_IN, D_MODEL), (SH_OUT, D_MODEL), (SH_MEM, 2 * W_C))

    def body(a_ref, b_ref, c_ref, oa_ref, ob_ref, oc_ref, raw_a, raw_b, raw_c, own_a, own_b, own_c,
             load_sem, store_sem, ici_send, ici_recv, d2d_send, d2d_recv):
        x, y, c = lax.axis_index("x"), lax.axis_index("y"), lax.axis_index("c")
        me, sibling = (x, y, c), (x, y, 1 - c)
        xn, yn, dg = (1 - x, y), (x, 1 - y), (1 - x, 1 - y)
        srcs, raws = (a_ref, b_ref, c_ref), (raw_a, raw_b, raw_c)
        owns, outs = (own_a, own_b, own_c), (oa_ref, ob_ref, oc_ref)
        loads = [pltpu.make_async_copy(srcs[a], raws[a], load_sem.at[a]) for a in range(3)]
        for cp in loads:
            cp.start()

        def rows(chip, hf, q):
            start = pl.multiple_of((2 * chip[0] + chip[1]) * SH_IN + hf * half + q * piece, 16)
            return oa_ref.at[pl.ds(start, piece), :]

        def copy(sems, k, chip, hf, q, to, src=None):
            blk = rows(chip, hf, q)
            return pltpu.make_async_remote_copy(
                src_ref=blk if src is None else src, dst_ref=blk, send_sem=sems[0].at[k], recv_sem=sems[1].at[k],
                device_id=to, device_id_type=MESH)

        def my_piece(q):
            return own_a.at[pl.ds(pl.multiple_of(c * half + q * piece, 16), piece), :]

        ici, d2d = (ici_send, ici_recv), (d2d_send, d2d_recv)
        stores, direct = [], []
        for a, (n, _) in enumerate(shards):
            loads[a].wait()
            owns[a][...] = raws[a][...].astype(BF16)
            mine = pl.ds(pl.multiple_of((2 * x + y) * n, 16), n)
            stores.append(pltpu.make_async_copy(owns[a], outs[a].at[mine, :], store_sem.at[a]))
            stores[-1].start()
            if a == 0:
                direct = [copy(ici, 0, (x, y), c, 0, (*xn, c), my_piece(0)),
                          copy(ici, 1, (x, y), c, 1, (*xn, c), my_piece(1)),
                          copy(ici, 3, (x, y), c, 0, (*yn, c), my_piece(0)),
                          copy(ici, 4, (x, y), c, 1, (*yn, c), my_piece(1))]
                for cp in direct:
                    cp.start()
        arrivals = [(0, xn, 0), (1, xn, 1), (3, yn, 0), (4, yn, 1), (2, dg, 1), (5, dg, 0)]
        passed = []
        for k, chip, q in arrivals:
            copy(ici, k, chip, c, q, me).wait_recv()
            if k == 0:
                passed.append(copy(ici, 5, xn, c, 0, (*yn, c)))
                passed[-1].start()
            if k == 4:
                passed.append(copy(ici, 2, yn, c, 1, (*xn, c)))
                passed[-1].start()
            passed.append(copy(d2d, k, chip, c, q, sibling))
            passed[-1].start()
        for k, chip, q in arrivals:
            copy(d2d, k, chip, 1 - c, q, me).wait_recv()
        for cp in direct + passed:
            cp.wait_send()
        for cp in stores:
            cp.wait()

    hbm = pl.BlockSpec(memory_space=pl.ANY)
    return pl.pallas_call(
        body, name="gather_weights",
        out_shape=(_sds((D_IN, D_MODEL), BF16), _sds((D_MIX, D_MODEL), BF16), _sds((D_MODEL, 2 * W_C), BF16)),
        in_specs=[hbm, hbm, hbm], out_specs=(hbm, hbm, hbm),
        scratch_shapes=([pltpu.VMEM(sh, F32) for sh in shards] + [pltpu.VMEM(sh, BF16) for sh in shards]
                        + [pltpu.SemaphoreType.DMA((3,))] * 2 + [pltpu.SemaphoreType.DMA((6,))] * 4),
        compiler_params=_cp(vmem_mb=40),
    )(*_pin(win_sh, wout_sh, wmem_sh))


def _rope(t, cos, sa, sb, sign):
    w = t.shape[1]
    reps = w // 128
    c, a, b = (jnp.tile(v, (1, reps)) if reps > 1 else v for v in (cos, sa, sb))
    rot = pltpu.roll(t, w - 32, 1) * a + pltpu.roll(t, 32, 1) * b
    return t * c + rot if sign > 0 else t * c - rot


def _in_proj(x, winT, b_in, cos, sa, sb, wout_own, wmem_own):
    tm = 512
    spt = SEQ // tm
    n_steps = T // tm
    forward_step = n_steps // 2

    def body(x_ref, w_ref, b_ref, cos_ref, sa_ref, sb_ref, wo_in, wm_in,
             xb_ref, qa_ref, ka_ref, va_ref, bn_ref, b4_ref, b16_ref, qc_ref, z_ref, wo_ref, wm_ref,
             scr, ici_send, ici_recv, d2d_send, d2d_recv):
        i = pl.program_id(0)
        mx, my, mc = lax.axis_index("x"), lax.axis_index("y"), lax.axis_index("c")
        chips = [(1 - mx, my), (mx, 1 - my), (1 - mx, 1 - my)]
        full = ((wo_ref, SH_OUT), (wm_ref, SH_MEM))

        def copy(sems, a, j, chip_of_block, half, to):
            blk = _shard_rows(full[a][0], full[a][1], chip_of_block, half)
            return pltpu.make_async_remote_copy(
                src_ref=blk, dst_ref=blk, send_sem=sems[0].at[a, j], recv_sem=sems[1].at[a, j],
                device_id=to, device_id_type=MESH)

        ici, d2d = (ici_send, ici_recv), (d2d_send, d2d_recv)
        pairs = [(a, j, chip) for j, chip in enumerate(chips) for a in range(2)]

        @pl.when(i == 0)
        def _():
            for a, j, chip in pairs:
                copy(ici, a, j, (mx, my), mc, (*chip, mc)).start()

        @pl.when(i == forward_step)
        def _():
            for a, j, chip in pairs:
                copy(ici, a, j, chip, mc, (mx, my, mc)).wait_recv()
                copy(d2d, a, j, chip, mc, (mx, my, 1 - mc)).start()

        @pl.when(i == n_steps - 1)
        def _():
            for a, j, chip in pairs:
                copy(d2d, a, j, chip, 1 - mc, (mx, my, mc)).wait_recv()
            for a, j, chip in pairs:
                copy(ici, a, j, (mx, my), mc, (*chip, mc)).wait_send()
                copy(d2d, a, j, chip, mc, (mx, my, 1 - mc)).wait_send()

        xb = x_ref[...].astype(BF16)
        xb_ref[...] = xb
        cos_t, sa_t, sb_t = cos_ref[...], sa_ref[...], sb_ref[...]

        def proj(r0, n):
            return _dot(xb, w_ref[r0:r0 + n, :], NT) + b_ref[:, r0:r0 + n]

        def rope(t):
            return _rope(t, cos_t, sa_t, sb_t, +1)

        parts = (rope(proj(O_QB, W_B)) * QK_SCALE, rope(proj(O_KB, W_B)), proj(O_VB, W_B))
        for k, part in enumerate(parts):
            bn_ref[:, 256 * k:256 * (k + 1)] = part.astype(BF16)
            scr[2 * k] = part[:, :128]
            scr[2 * k + 1] = part[:, 128:]
        for j in range(6):
            lanes = slice(128 * j, 128 * (j + 1))
            for res in range(4):
                t = scr[j, pl.ds(res, tm // 4, stride=4), :]
                b4_ref[0, res, :, lanes] = t.astype(BF16)
                scr[6 + j, res * (tm // 4):(res + 1) * (tm // 4), :] = t
            for res in range(16):
                b16_ref[0, res, :, lanes] = scr[6 + j, pl.ds((res % 4) * (tm // 4) + res // 4, tm // 16, stride=4),
                                                :].astype(BF16)
        qa_ref[...] = (rope(proj(O_QA, W_A)) * QK_SCALE).astype(BF16)
        assert O_VA == O_KA + W_KV_A
        kv = proj(O_KA, 2 * W_KV_A)
        ka_ref[...] = rope(kv[:, :W_KV_A]).astype(BF16)
        va_ref[...] = kv[:, W_KV_A:].astype(BF16)
        qc_ref[...] = (proj(O_QC, W_C) * QK_SCALE).astype(BF16)
        z_ref[...] = proj(O_Z, D_MIX).astype(BF16)

    tok = lambda w: pl.BlockSpec((tm, w), lambda i: (i, 0))
    tab = pl.BlockSpec((tm, 128), lambda i: (i % spt, 0))
    hbm = pl.BlockSpec(memory_space=pl.ANY)
    return pl.pallas_call(
        body, name="in_proj", grid=(n_steps,),
        in_specs=[tok(D_MODEL), _full((D_IN, D_MODEL)), _full((1, D_IN)), tab, tab, tab, hbm, hbm],
        out_specs=(tok(D_MODEL), tok(W_A), tok(W_KV_A), tok(W_KV_A), tok(768),
                   pl.BlockSpec((1, 4, tm // 4, 768), lambda i: (i // spt, 0, i % spt, 0)),
                   pl.BlockSpec((1, 16, tm // 16, 768), lambda i: (i // spt, 0, i % spt, 0)),
                   tok(W_C), tok(D_MIX), hbm, hbm),
        out_shape=(_sds((T, D_MODEL), BF16), _sds((T, W_A), BF16), _sds((T, W_KV_A), BF16), _sds((T, W_KV_A), BF16),
                   _sds((T, 768), BF16), _sds((B_LOC, 4, SEQ // 4, 768), BF16), _sds((B_LOC, 16, SEQ // 16, 768), BF16),
                   _sds((T, W_C), BF16), _sds((T, D_MIX), BF16),
                   _sds((D_MIX, D_MODEL), BF16), _sds((D_MODEL, 2 * W_C), BF16)),
        input_output_aliases={6: 9, 7: 10},
        scratch_shapes=[pltpu.VMEM((12, tm, 128), F32)] + [pltpu.SemaphoreType.DMA((2, 3))] * 4,
        compiler_params=_cp(("arbitrary",), vmem_mb=48),
    )(*_pin(x, winT, b_in, cos, sa, sb, wout_own, wmem_own))


def _mem_kv(mem, wmem):
    def body(m_ref, w_ref, mb_ref, kv_ref):
        mb = m_ref[...].astype(BF16)
        mb_ref[...] = mb
        kv_ref[...] = _dot(mb, w_ref[...], NN).astype(BF16)

    n = B_LOC * MEM_LEN
    return pl.pallas_call(
        body, name="mem_kv",
        out_shape=(_sds((n, D_MODEL), BF16), _sds((n, 2 * W_C), BF16)),
    )(*_pin(mem, wmem))


class _Part:
    def __init__(self, body, args, in_specs, out_specs, out_shape, scratch=()):
        self.body, self.args, self.in_specs, self.out_specs, self.out_shape = body, args, in_specs, out_specs, out_shape
        self.scratch = list(scratch)


def _run_parts(name, parts, semantics, vmem_mb):
    n_in = [len(p.args) for p in parts]
    n_out = [len(p.out_shape) for p in parts]
    n_scr = [len(p.scratch) for p in parts]

    def body(*refs):
        ins, outs, scr = refs[:sum(n_in)], refs[sum(n_in):sum(n_in) + sum(n_out)], refs[sum(n_in) + sum(n_out):]
        i0 = o0 = s0 = 0
        for p, ni, no, ns in zip(parts, n_in, n_out, n_scr):
            p.body(*ins[i0:i0 + ni], *outs[o0:o0 + no], *scr[s0:s0 + ns])
            i0, o0, s0 = i0 + ni, o0 + no, s0 + ns

    res = pl.pallas_call(
        body, name=name, grid=(T // QR,),
        in_specs=[sp for p in parts for sp in p.in_specs], out_specs=tuple(sp for p in parts for sp in p.out_specs),
        out_shape=tuple(sh for p in parts for sh in p.out_shape),
        scratch_shapes=[sc for p in parts for sc in p.scratch],
        compiler_params=_cp((semantics,), vmem_mb=vmem_mb),
    )(*_pin(*[a for p in parts for a in p.args]))
    out, o0 = [], 0
    for no in n_out:
        out.append(tuple(res[o0:o0 + no]))
        o0 += no
    return out


QB = 8
QR = QB * BLK


def _lane_lo():
    return lax.broadcasted_iota(jnp.int32, (1, 128), 1) < 64


def _dup_head(k2, hk, lo):
    kf = k2.astype(F32)
    r = pltpu.roll(kf, 64, 1)
    return (jnp.where(lo, kf, r) if hk == 0 else jnp.where(lo, r, kf)).astype(BF16)


def _stack_heads(pairs, lo):
    parts = []
    for x2 in pairs:
        z = jnp.zeros_like(x2)
        parts += [jnp.where(lo, x2, z), jnp.where(lo, z, x2)]
    return jnp.concatenate(parts, axis=0)


def _prev_mode(kind, nb, j):
    if kind == "mem" or nb == 1:
        return "no"
    if nb <= QB:
        return "yes" if j % nb else "no"
    return "yes" if j else "dyn"


class _Attn:
    def __init__(self, kind, nb, max_dist, gqa, qw, kvw, qcb, kcb, vcb):
        self.kind, self.nb, self.gqa, self.qw, self.kvw = kind, nb, gqa, qw, kvw
        npairs = qw // 128
        self.groups = ([(hk, [2 * hk, 2 * hk + 1]) for hk in range(npairs // 2)] if gqa
                       else [(p, [p]) for p in range(npairs)])
        self.nh = 2 * len(self.groups[0][1])
        self.cols = 128 * self.nh
        self.reach = BLK - max_dist
        self.ext_prev = kind == "band" and nb > QB
        self.q_spec = pl.BlockSpec((QR, qw), lambda g: (g, qcb))
        self.row_spec = pl.BlockSpec((QR, qw), lambda g: (g, 0))
        self.stat_spec = pl.BlockSpec((QR, 128), lambda g: (g, 0))
        if kind == "mem":
            per = SEQ // QR
            self.kv_specs = [pl.BlockSpec((MEM_LEN, kvw), lambda g: (g // per, kcb)),
                             pl.BlockSpec((MEM_LEN, kvw), lambda g: (g // per, vcb))]
        else:
            self.kv_specs = [pl.BlockSpec((QR, kvw), lambda g: (g, kcb)), pl.BlockSpec((QR, kvw), lambda g: (g, vcb))]
            if self.ext_prev:
                self.kv_specs += [pl.BlockSpec((BLK, kvw), lambda g: (jnp.maximum(g * QB - 1, 0), kcb)),
                                  pl.BlockSpec((BLK, kvw), lambda g: (jnp.maximum(g * QB - 1, 0), vcb))]

    def masks(self):
        if self.kind == "mem":
            return None
        kj = lax.broadcasted_iota(jnp.int32, (2 * BLK, self.cols), 0)
        qi = lax.broadcasted_iota(jnp.int32, (2 * BLK, self.cols), 1) & (BLK - 1)
        kj1 = lax.broadcasted_iota(jnp.int32, (BLK, self.cols), 0)
        qi1 = lax.broadcasted_iota(jnp.int32, (BLK, self.cols), 1) & (BLK - 1)
        return kj, qi, kj1 <= qi1

    def keys(self, j, gi, kc_ref, vc_ref, kp_ref, vp_ref, lo, kq, g):
        def kv(k_ref, v_ref, r):
            if self.gqa:
                return _dup_head(k_ref[r, :], gi, lo), _dup_head(v_ref[r, :], gi, lo)
            sl = slice(128 * gi, 128 * (gi + 1))
            return k_ref[r, sl], v_ref[r, sl]

        if self.kind == "mem":
            key0 = pl.multiple_of((g // (SEQ // QR)) * MEM_LEN, MEM_LEN)
            return (*kv(kc_ref, vc_ref, slice(None)), None, [(0, MEM_LEN, key0)])
        kj, qi, cur = kq
        row0 = g * QR + BLK * j
        mode = _prev_mode(self.kind, self.nb, j)
        if mode == "no":
            return (*kv(kc_ref, vc_ref, slice(BLK * j, BLK * (j + 1))), cur, [(0, BLK, pl.multiple_of(row0, BLK))])
        if mode == "yes":
            mask = jnp.logical_and(kj >= qi + self.reach, kj <= qi + BLK)
            return (*kv(kc_ref, vc_ref, slice(BLK * (j - 1), BLK * (j + 1))), mask,
                    [(0, 2 * BLK, pl.multiple_of(row0 - BLK, BLK))])
        has_prev = ((g * QB) % self.nb) > 0
        hp = has_prev.astype(jnp.int32)
        mask = jnp.logical_and(kj >= qi * hp + (self.reach * hp + BLK * (1 - hp)), kj <= qi + BLK)
        kp, vp = kv(kp_ref, vp_ref, slice(None))
        kc, vc = kv(kc_ref, vc_ref, slice(0, BLK))
        return (jnp.concatenate([kp, kc], axis=0), jnp.concatenate([vp, vc], axis=0), mask,
                [(0, BLK, pl.multiple_of(jnp.maximum(row0 - BLK, 0), BLK)), (BLK, BLK, pl.multiple_of(row0, BLK))])


def _attn_fwd(q, qcb, qw, k, kcb, v, vcb, kvw, *, kind, nb=1, max_dist=BLK, gqa=False, sinks=None):
    a = _Attn(kind, nb, max_dist, gqa, qw, kvw, qcb, kcb, vcb)

    def body(*refs):
        it = iter(refs)
        q_ref, kc_ref, vc_ref = next(it), next(it), next(it)
        kp_ref, vp_ref = (next(it), next(it)) if a.ext_prev else (None, None)
        sink_ref = next(it) if sinks is not None else None
        o_ref, lse_ref = next(it), next(it)
        g = pl.program_id(0)
        lo = _lane_lo()
        top = lax.broadcasted_iota(jnp.int32, (128, 1), 0) < 64
        rid = lax.broadcasted_iota(jnp.int32, (8, 128), 0)
        kq = a.masks()
        stats = {}

        def scores(j, gi, pairs):
            rows = slice(BLK * j, BLK * (j + 1))
            qs = _stack_heads([q_ref[rows, 128 * p:128 * (p + 1)] for p in pairs], lo)
            kk, vv, mask, _ = a.keys(j, gi, kc_ref, vc_ref, kp_ref, vp_ref, lo, kq, g)
            pieces = [slice(r0, r0 + BLK) for r0 in range(0, kk.shape[0], BLK)]
            return dict(j=j, gi=gi, pairs=pairs, rows=rows, vv=vv, mask=mask, pieces=pieces,
                        ss=[_dot(kk[r], qs, NT) for r in pieces])

        def softmax(c):
            gi, mask = c["gi"], c["mask"]
            ss = [s if mask is None else jnp.where(mask[r], s, NEG) for r, s in zip(c["pieces"], c.pop("ss"))]
            m = jnp.max(ss[0], axis=0, keepdims=True)
            for s in ss[1:]:
                m = jnp.maximum(m, jnp.max(s, axis=0, keepdims=True))
            if sink_ref is not None:
                sk = jnp.concatenate([jnp.full((1, 128), sink_ref[0, a.nh * gi + i], F32) for i in range(a.nh)], axis=1)
                m = jnp.maximum(m, sk)
            ps = [jnp.exp(s - m) for s in ss]
            l = sum(jnp.sum(p, axis=0, keepdims=True) for p in ps)
            if sink_ref is not None:
                l = l + jnp.exp(sk - m)
            c["ps"] = [p.astype(BF16) for p in ps]
            c["l"], c["lse"] = l, m + jnp.log(l)

        def outputs(c):
            j, gi, rows = c["j"], c["gi"], c["rows"]
            ot = sum(_dot(c["vv"][r], p, TN) for r, p in zip(c["pieces"], c["ps"]))
            ot = ot * pl.reciprocal(c["l"], approx=True)
            for i, p in enumerate(c["pairs"]):
                o2t = jnp.where(top, ot[:, 256 * i:256 * i + 128], ot[:, 256 * i + 128:256 * i + 256])
                o_ref[rows, 128 * p:128 * (p + 1)] = o2t.T.astype(BF16)
            stat = stats.get(j, jnp.zeros((8, 128), F32))
            for i in range(a.nh):
                stat = jnp.where(rid == a.nh * gi + i, c["lse"][:, 128 * i:128 * (i + 1)], stat)
            stats[j] = stat
            if gi == a.groups[-1][0]:
                lse_ref[rows, :] = jnp.concatenate([stats.pop(j), jnp.zeros((120, 128), F32)], axis=0).T

        chains = [(j, gi, pairs) for j in range(QB) for gi, pairs in a.groups]
        live = {}
        for t in range(len(chains) + 2):
            if t < len(chains):
                live[t] = scores(*chains[t])
            if 0 <= t - 1 < len(chains):
                softmax(live[t - 1])
            if 0 <= t - 2 < len(chains):
                outputs(live.pop(t - 2))


    args = [q, k, v] + ([k, v] if a.ext_prev else [])
    in_specs = [a.q_spec] + a.kv_specs
    if sinks is not None:
        args.append(sinks)
        in_specs.append(pl.BlockSpec(memory_space=pltpu.SMEM))
    return _Part(body, args, in_specs, [a.row_spec, a.stat_spec], [_sds((T, qw), BF16), _sds((T, 128), F32)])


def _attn_bwd(q, qcb, qw, k, kcb, v, vcb, kvw, do, lse, dl, *, kind, nb=1, max_dist=BLK, gqa=False, sinkv=None,
              mem_in=None):
    a = _Attn(kind, nb, max_dist, gqa, qw, kvw, qcb, kcb, vcb)

    def body(*refs):
        it = iter(refs)
        q_ref, kc_ref, vc_ref = next(it), next(it), next(it)
        kp_ref, vp_ref = (next(it), next(it)) if a.ext_prev else (None, None)
        do_ref, lse_ref, dl_ref = next(it), next(it), next(it)
        sinkv_ref = next(it) if sinkv is not None else None
        mem_ref = next(it) if kind == "mem" else None
        dq_ref = next(it)
        if kind == "mem":
            gmem_ref = next(it)
        else:
            dk_out, dv_out = next(it), next(it)
        dsink_ref = next(it) if sinkv is not None else None
        if kind != "mem":
            dk_ref, dv_ref, stage_k, stage_v, flush_sem = next(it), next(it), next(it), next(it), next(it)
        else:
            dkv_ref = next(it)
        g = pl.program_id(0)
        lo = _lane_lo()
        top = lax.broadcasted_iota(jnp.int32, (128, 1), 0) < 64

        @pl.when(g == 0)
        def _():
            if kind == "mem":
                dkv_ref[...] = jnp.zeros_like(dkv_ref)
            else:
                dk_ref[...] = jnp.zeros_like(dk_ref)
                dv_ref[...] = jnp.zeros_like(dv_ref)
            if dsink_ref is not None:
                dsink_ref[...] = jnp.zeros_like(dsink_ref)

        kq = a.masks()
        stats_t = {}

        def first_matmuls(j, gi, pairs):
            rows = slice(BLK * j, BLK * (j + 1))
            if j not in stats_t:
                stats_t[j] = (lse_ref[rows, :].T, dl_ref[rows, :].T)
            lse_t, dl_t = stats_t[j]
            heads = [a.nh * gi + i for i in range(a.nh)]
            c = dict(rows=rows, gi=gi, pairs=pairs)
            c["qs"] = _stack_heads([q_ref[rows, 128 * p:128 * (p + 1)] for p in pairs], lo)
            c["dos"] = _stack_heads([do_ref[rows, 128 * p:128 * (p + 1)] for p in pairs], lo)
            c["lse_row"] = jnp.concatenate([lse_t[h:h + 1, :] for h in heads], axis=1)
            c["dl_row"] = jnp.concatenate([dl_t[h:h + 1, :] for h in heads], axis=1)
            c["kk"], vv, c["mask"], c["dests"] = a.keys(j, gi, kc_ref, vc_ref, kp_ref, vp_ref, lo, kq, g)
            c["s"] = _dot(c["kk"], c["qs"], NT)
            c["dp"] = _dot(vv, c["dos"], NT)
            return c

        def elementwise(c):
            s = c.pop("s")
            if c["mask"] is not None:
                s = jnp.where(c["mask"], s, NEG)
            p = jnp.exp(s - c["lse_row"])
            c["ds"] = (p * (c.pop("dp") - c["dl_row"])).astype(BF16)
            c["p"] = p.astype(BF16)

        def last_matmuls(c):
            gi, rows = c["gi"], c["rows"]
            dqt = _dot(c["kk"], c["ds"], TN)
            ck = _dot(c["ds"], c["qs"], NN)
            cv = _dot(c["p"], c["dos"], NN)
            if gqa:
                sel = lo if gi == 0 else jnp.logical_not(lo)
                ck = jnp.where(sel, ck + pltpu.roll(ck, 64, 1), 0.0)
                cv = jnp.where(sel, cv + pltpu.roll(cv, 64, 1), 0.0)
                kcols = slice(0, 128)
            else:
                kcols = slice(128 * gi, 128 * (gi + 1))
            for r0, nr, key0 in c["dests"]:
                krows = pl.ds(key0, nr)
                if kind == "mem":
                    dkv_ref[krows, kcols] += ck[r0:r0 + nr]
                    dkv_ref[krows, slice(kvw + kcols.start, kvw + kcols.stop)] += cv[r0:r0 + nr]
                else:
                    dk_ref[krows, kcols] += ck[r0:r0 + nr]
                    dv_ref[krows, kcols] += cv[r0:r0 + nr]
            for i, p in enumerate(c["pairs"]):
                dq2t = jnp.where(top, dqt[:, 256 * i:256 * i + 128], dqt[:, 256 * i + 128:256 * i + 256])
                dq_ref[rows, 128 * p:128 * (p + 1)] = dq2t.T.astype(BF16)

        chains = [(j, gi, pairs) for j in range(QB) for gi, pairs in a.groups]
        live = {}
        for t in range(len(chains) + 2):
            if t < len(chains):
                live[t] = first_matmuls(*chains[t])
            if 0 <= t - 1 < len(chains):
                elementwise(live[t - 1])
            if 0 <= t - 2 < len(chains):
                last_matmuls(live.pop(t - 2))
        if dsink_ref is not None:
            ps = jnp.exp(sinkv_ref[...] - lse_ref[...]) * dl_ref[...]
            dsink_ref[...] += jnp.sum(ps, axis=0, keepdims=True)
        if kind == "mem":
            @pl.when(g == T // QR - 1)
            def _():
                gmem_ref[...] = _dot(mem_ref[...], dkv_ref[...].astype(BF16), TN)
        else:
            n_steps = T // QR

            def flush(step):
                rows = pl.ds(pl.multiple_of(step * QR, QR), QR)
                out = []
                for acc, stage, dst, i in ((dk_ref, stage_k, dk_out, 0), (dv_ref, stage_v, dv_out, 1)):
                    stage[...] = acc[rows, :].astype(BF16)
                    out.append(pltpu.make_async_copy(stage, dst.at[rows, :], flush_sem.at[i]))
                return out

            def flushed(step):
                rows = pl.ds(pl.multiple_of(step * QR, QR), QR)
                return [pltpu.make_async_copy(stage, dst.at[rows, :], flush_sem.at[i])
                        for stage, dst, i in ((stage_k, dk_out, 0), (stage_v, dv_out, 1))]

            @pl.when(g >= 2)
            def _():
                for cp in flushed(g - 2):
                    cp.wait()

            @pl.when(g >= 1)
            def _():
                for cp in flush(g - 1):
                    cp.start()

            @pl.when(g == n_steps - 1)
            def _():
                for cp in flushed(g - 1):
                    cp.wait()
                for cp in flush(g):
                    cp.start()
                for cp in flushed(g):
                    cp.wait()

    args = [q, k, v] + ([k, v] if a.ext_prev else []) + [do, lse, dl]
    in_specs = [a.q_spec] + a.kv_specs + [a.row_spec, a.stat_spec, a.stat_spec]
    if sinkv is not None:
        args.append(sinkv)
        in_specs.append(_full((1, 128)))
    if kind == "mem":
        args.append(mem_in)
        in_specs.append(pl.BlockSpec(mem_in.shape, lambda g: (0, 0), pipeline_mode=pl.Buffered(1)))
    out_shape = [_sds((T, qw), BF16)]
    out_specs = [a.row_spec]
    scratch = []
    if kind == "mem":
        out_shape.append(_sds((D_MODEL, 2 * kvw), F32))
        out_specs.append(pl.BlockSpec((D_MODEL, 2 * kvw), lambda g: (0, 0), pipeline_mode=pl.Buffered(1)))
        scratch = [pltpu.VMEM((B_LOC * MEM_LEN, 2 * kvw), F32)]
    else:
        out_shape += [_sds((T, kvw), BF16)] * 2
        out_specs += [pl.BlockSpec(memory_space=pl.ANY)] * 2
        scratch = [pltpu.VMEM((T, kvw), F32)] * 2 + [pltpu.VMEM((QR, kvw), BF16)] * 2 + [pltpu.SemaphoreType.DMA((2,))]
    if sinkv is not None:
        out_shape.append(_sds((1, 128), F32))
        out_specs.append(_full((1, 128)))
    return _Part(body, args, in_specs, out_specs, out_shape, scratch)


def _dot2(v, w_ref):
    hi = v.astype(BF16)
    lo = (v - hi.astype(F32)).astype(BF16)
    return _dot(hi, w_ref[...], NN) + _dot(lo, w_ref[...], NN)


def _middle(oa, o1, l1, o4, l4, o16, l16, oc, z, x, tgt, g_br, ln_g, ln_b, wout, spread4, gather4, gather8):
    tm = 512
    spt = SEQ // tm

    def body(oa_ref, o1_ref, l1_ref, o4_ref, l4_ref, o16_ref, l16_ref, oc_ref, z_ref, x_ref, t_ref,
             g_ref, lg_ref, lb_ref, w_ref, sp4_ref, ga4_ref, ga8_ref,
             du_ref, dz_ref, doa_ref, dla_ref,
             dobn_ref, lsen_ref, dlbn_ref, dob4_ref, lse4_ref, dlb4_ref, dob16_ref, lse16_ref, dlb16_ref,
             doc_ref, dlc_ref, acc_ref, gout_ref, scr):
        i = pl.program_id(0)

        @pl.when(i == 0)
        def _():
            acc_ref[...] = jnp.zeros_like(acc_ref)
            gout_ref[...] = jnp.zeros_like(gout_ref)

        q = tm // 4
        for res in range(16):
            rows = pl.ds((res % 4) * q + res // 4, tm // 16, stride=4)
            for j in range(2):
                scr[6 + j, rows, :] = o16_ref[0, res, :, 128 * j:128 * (j + 1)].astype(F32)
            scr[8, rows, :] = l16_ref[0, res]
        for res in range(4):
            rows, blk = pl.ds(res, q, stride=4), slice(res * q, (res + 1) * q)
            for j in range(2):
                scr[j, rows, :] = o4_ref[0, res, :, 128 * j:128 * (j + 1)].astype(F32)
                scr[3 + j, rows, :] = scr[6 + j, blk, :]
            scr[2, rows, :] = l4_ref[0, res]
            scr[5, rows, :] = scr[8, blk, :]
        inv_d = 1.0 / D_MODEL
        gb, lg, lb = g_ref[...], lg_ref[...], lb_ref[...]

        def rms(o):
            r = lax.rsqrt(jnp.sum(o * o, axis=1, keepdims=True) * (1.0 / o.shape[1]) + RMS_EPS)
            return o * r, r

        def rms_bwd(dn_, n_, r):
            return r * (dn_ - n_ * (jnp.sum(dn_ * n_, axis=1, keepdims=True) * (1.0 / n_.shape[1])))

        def forward(rs):
            o4v = jnp.concatenate([scr[0, rs, :], scr[1, rs, :]], axis=1)
            o16v = jnp.concatenate([scr[3, rs, :], scr[4, rs, :]], axis=1)
            l1v, l4v, l16v = l1_ref[rs, :], scr[2, rs, :], scr[5, rs, :]
            mx = jnp.maximum(jnp.maximum(l1v, l4v), l16v)
            e1, e4, e16 = jnp.exp(l1v - mx), jnp.exp(l4v - mx), jnp.exp(l16v - mx)
            ssum = e1 + e4 + e16
            inv = 1.0 / ssum
            c = dict(rs=rs, lse_b=mx + jnp.log(ssum))
            c["ob"] = (_dot2(e1 * inv, sp4_ref) * o1_ref[rs, :].astype(F32) + _dot2(e4 * inv, sp4_ref) * o4v
                       + _dot2(e16 * inv, sp4_ref) * o16v)
            c["oa"], c["oc"] = oa_ref[rs, :].astype(F32), oc_ref[rs, :].astype(F32)
            na, c["ra"] = rms(c["oa"])
            nb_, c["rb"] = rms(c["ob"])
            nc, c["rc"] = rms(c["oc"])
            c["n"] = jnp.concatenate([na, nb_, nc], axis=1)
            c["zf"] = z_ref[rs, :].astype(F32)
            c["sig"] = 1.0 / (1.0 + jnp.exp(-c["zf"]))
            c["sz"] = c["zf"] * c["sig"]
            c["yb"] = (c["n"] * gb * c["sz"]).astype(BF16)
            c["y2"] = _dot(c["yb"], w_ref[...], NN)
            return c

        def norm(c):
            rs = c["rs"]
            u = ALPHA * x_ref[rs, :] + c.pop("y2")
            mu = jnp.sum(u, axis=1, keepdims=True) * inv_d
            uc = u - mu
            rstd = lax.rsqrt(jnp.sum(uc * uc, axis=1, keepdims=True) * inv_d + LN_EPS)
            xh = uc * rstd
            diff = xh * lg + lb - t_ref[rs, :]
            acc_ref[0:1, :] += jnp.sum(diff * diff, axis=0, keepdims=True) * (0.5 * inv_d)
            dout = diff * inv_d
            acc_ref[2:3, :] += jnp.sum(dout * xh, axis=0, keepdims=True)
            acc_ref[3:4, :] += jnp.sum(dout, axis=0, keepdims=True)
            dxh = dout * lg
            du = rstd * (dxh - jnp.sum(dxh, axis=1, keepdims=True) * inv_d
                         - xh * (jnp.sum(dxh * xh, axis=1, keepdims=True) * inv_d))
            dub = du.astype(BF16)
            du_ref[rs, :] = dub
            c["dy"] = _dot(dub, w_ref[...], NT)
            gout_ref[...] += _dot(c.pop("yb"), dub, TN)

        def backward(c):
            rs, n, dy, zf, sig = c["rs"], c["n"], c["dy"], c["zf"], c["sig"]
            t1 = dy * c["sz"]
            acc_ref[1:2, :] += jnp.sum(t1 * n, axis=0, keepdims=True)
            dn = t1 * gb
            dz_ref[rs, :] = (dy * n * gb * (sig * (1.0 + zf * (1.0 - sig)))).astype(BF16)
            doa = rms_bwd(dn[:, :W_A], n[:, :W_A], c["ra"])
            dob = rms_bwd(dn[:, W_A:W_A + W_B], n[:, W_A:W_A + W_B], c["rb"])
            doc = rms_bwd(dn[:, W_A + W_B:], n[:, W_A + W_B:], c["rc"])
            doa_ref[rs, :] = doa.astype(BF16)
            dla_ref[rs, :] = _dot2(doa * c["oa"], ga8_ref)
            doc_ref[rs, :] = doc.astype(BF16)
            dlc_ref[rs, :] = _dot2(doc * c["oc"], ga4_ref)
            dobn_ref[rs, :] = dob.astype(BF16)
            lsen_ref[rs, :] = c["lse_b"]
            dlbn_ref[rs, :] = _dot2(dob * c["ob"], ga4_ref)
            scr[0, rs, :] = dob[:, :128]
            scr[1, rs, :] = dob[:, 128:]

        halves = [slice(h * (tm // 2), (h + 1) * (tm // 2)) for h in range(2)]
        live = {}
        for t in range(len(halves) + 2):
            if t < len(halves):
                live[t] = forward(halves[t])
            if 0 <= t - 1 < len(halves):
                norm(live[t - 1])
            if 0 <= t - 2 < len(halves):
                backward(live.pop(t - 2))
        for j in range(2):
            sl = slice(128 * j, 128 * (j + 1))
            for res in range(4):
                t = scr[j, pl.ds(res, q, stride=4), :]
                dob4_ref[0, res, :, sl] = t.astype(BF16)
                scr[6 + j, res * q:(res + 1) * q, :] = t
            for res in range(16):
                dob16_ref[0, res, :, sl] = scr[6 + j, pl.ds((res % 4) * q + res // 4, tm // 16, stride=4),
                                               :].astype(BF16)
        for res in range(4):
            rows = pl.ds(res, q, stride=4)
            lse4_ref[0, res] = lsen_ref[rows, :]
            dlb4_ref[0, res] = dlbn_ref[rows, :]
        for res in range(16):
            rows = pl.ds(res // 4, tm // 16, stride=4)
            lse16_ref[0, res] = lse4_ref[0, res % 4, rows, :]
            dlb16_ref[0, res] = dlb4_ref[0, res % 4, rows, :]


    tok = lambda w: pl.BlockSpec((tm, w), lambda i: (i, 0))
    p4 = lambda w: pl.BlockSpec((1, 4, tm // 4, w), lambda i: (i // spt, 0, i % spt, 0))
    p16 = lambda w: pl.BlockSpec((1, 16, tm // 16, w), lambda i: (i // spt, 0, i % spt, 0))
    s4 = lambda w, dt: _sds((B_LOC, 4, SEQ // 4, w), dt)
    s16 = lambda w, dt: _sds((B_LOC, 16, SEQ // 16, w), dt)
    row = _full((1, D_MODEL))
    return pl.pallas_call(
        body, name="middle", grid=(T // tm,),
        in_specs=[tok(W_A), tok(W_B), tok(128), p4(W_B), p4(128), p16(W_B), p16(128), tok(W_C), tok(D_MIX),
                  tok(D_MODEL), tok(D_MODEL), row, row, row, _full((D_MIX, D_MODEL)),
                  _full((128, W_B)), _full((W_B, 128)), _full((W_A, 128))],
        out_specs=(tok(D_MODEL), tok(D_MIX), tok(W_A), tok(128),
                   tok(W_B), tok(128), tok(128), p4(W_B), p4(128), p4(128), p16(W_B), p16(128), p16(128),
                   tok(W_C), tok(128), _full((8, D_MODEL)), _full((D_MIX, D_MODEL))),
        out_shape=(_sds((T, D_MODEL), BF16), _sds((T, D_MIX), BF16),
                   _sds((T, W_A), BF16), _sds((T, 128), F32),
                   _sds((T, W_B), BF16), _sds((T, 128), F32), _sds((T, 128), F32),
                   s4(W_B, BF16), s4(128, F32), s4(128, F32), s16(W_B, BF16), s16(128, F32), s16(128, F32),
                   _sds((T, W_C), BF16), _sds((T, 128), F32), _sds((8, D_MODEL), F32),
                   _sds((D_MIX, D_MODEL), F32)),
        scratch_shapes=[pltpu.VMEM((9, tm, 128), F32)],
        compiler_params=_cp(("arbitrary",), vmem_mb=56),
    )(*_pin(oa, o1, l1, o4, l4, o16, l16, oc, z, x, tgt, g_br, ln_g, ln_b, wout, spread4, gather4, gather8))


class _ReduceScatter:
    def __init__(self, shapes):
        self.shapes = shapes

    def scratch_shapes(self):
        out = []
        for n, w in self.shapes:
            h, p = n // 2, n // 4
            out += [pltpu.VMEM((4, h, w), F32), pltpu.VMEM((4, h, w), F32), pltpu.VMEM((6, p, w), BF16),
                    pltpu.VMEM((6, p, w), BF16), pltpu.VMEM((2, p, w), F32), pltpu.VMEM((h, w), F32)]
        na = len(self.shapes)
        dma = pltpu.SemaphoreType.DMA
        return out + [dma((na, 4)), dma((na, 4)), dma((na, 4)), dma((na, 6)), dma((na, 6)), dma((na,)), dma((na,)),
                      dma((na,))]

    def bind(self, g_refs, r_refs, scratch):
        na = len(self.shapes)
        bufs = [scratch[6 * a:6 * a + 6] for a in range(na)]
        mine, sib, stage, land, keep, tot = (tuple(b[i] for b in bufs) for i in range(6))
        loc_sem, s1_send, s1_recv, s2_send, s2_recv, s3_send, s3_recv, st_sem = scratch[6 * na:6 * na + 8]
        x, y, c = lax.axis_index("x"), lax.axis_index("y"), lax.axis_index("c")
        me, sibling = (x, y, c), (x, y, 1 - c)
        xn, yn, dg = (1 - x, y), (x, 1 - y), (1 - x, 1 - y)
        idx = lambda chip: 2 * chip[0] + chip[1]
        my_chip = idx((x, y))
        order = [idx(xn), idx(dg), idx(yn), my_chip]

        def rows(a, k, half):
            n = self.shapes[a][0]
            return pl.ds(pl.multiple_of(k * n + half * (n // 2), 8), n // 2)

        def piece(a, q):
            p = self.shapes[a][0] // 4
            return slice(q * p, (q + 1) * p)

        def load(a, k):
            return pltpu.make_async_copy(g_refs[a].at[rows(a, k, c), :], mine[a].at[k], loc_sem.at[a, k])

        def s1(a, k, half):
            return pltpu.make_async_remote_copy(
                src_ref=g_refs[a].at[rows(a, k, half), :], dst_ref=sib[a].at[k],
                send_sem=s1_send.at[a, k], recv_sem=s1_recv.at[a, k], device_id=sibling, device_id_type=MESH)

        def s2(a, i, to):
            return pltpu.make_async_remote_copy(
                src_ref=stage[a].at[i], dst_ref=land[a].at[i], send_sem=s2_send.at[a, i], recv_sem=s2_recv.at[a, i],
                device_id=to, device_id_type=MESH)

        via = {0: xn, 1: xn, 2: yn, 3: yn, 4: yn, 5: xn}

        def s3(a, half, to):
            return pltpu.make_async_remote_copy(
                src_ref=tot[a], dst_ref=r_refs[a].at[rows(a, 0, half), :], send_sem=s3_send.at[a],
                recv_sem=s3_recv.at[a], device_id=to, device_id_type=MESH)

        def store(a):
            return pltpu.make_async_copy(tot[a], r_refs[a].at[rows(a, 0, c), :], st_sem.at[a])

        def start():
            for k in order:
                for a in range(na):
                    load(a, k).start()
                    s1(a, k, 1 - c).start()

        def chip_sum(a, k):
            load(a, k).wait()
            s1(a, k, c).wait_recv()
            return mine[a][k] + sib[a][k]

        def exchange():
            for a in range(na):
                P, Q = piece(a, 0), piece(a, 1)
                s_xn = chip_sum(a, idx(xn))
                stage[a][0] = s_xn[P].astype(BF16)
                keep[a][1] = s_xn[Q]
                s_dg = chip_sum(a, idx(dg))
                stage[a][1] = s_dg[P].astype(BF16)
                s2(a, 0, (*xn, c)).start()
                s2(a, 1, (*xn, c)).start()
                stage[a][3] = s_dg[Q].astype(BF16)
                s_yn = chip_sum(a, idx(yn))
                stage[a][2] = s_yn[Q].astype(BF16)
                keep[a][0] = s_yn[P]
                s2(a, 2, (*yn, c)).start()
                s2(a, 3, (*yn, c)).start()
                tot[a][...] = chip_sum(a, my_chip)

        def relay():
            for a in range(na):
                P, Q = piece(a, 0), piece(a, 1)
                s2(a, 1, me).wait_recv()
                stage[a][4] = (keep[a][0] + land[a][1].astype(F32)).astype(BF16)
                s2(a, 4, (*yn, c)).start()
                s2(a, 3, me).wait_recv()
                stage[a][5] = (keep[a][1] + land[a][3].astype(F32)).astype(BF16)
                s2(a, 5, (*xn, c)).start()
                s2(a, 0, me).wait_recv()
                tot[a][P, :] += land[a][0].astype(F32)
                s2(a, 2, me).wait_recv()
                tot[a][Q, :] += land[a][2].astype(F32)

        def finish():
            for a in range(na):
                P, Q = piece(a, 0), piece(a, 1)
                s2(a, 4, me).wait_recv()
                tot[a][P, :] += land[a][4].astype(F32)
                s2(a, 5, me).wait_recv()
                tot[a][Q, :] += land[a][5].astype(F32)
                s3(a, c, sibling).start()
                store(a).start()

        def drain():
            for a in range(na):
                s3(a, 1 - c, me).wait_recv()
                store(a).wait()
            for a in range(na):
                for k in order:
                    s1(a, k, 1 - c).wait_send()
                for i in range(6):
                    s2(a, i, (*via[i], c)).wait_send()
                s3(a, c, sibling).wait_send()

        return start, exchange, relay, finish, drain

    def part(self, grads, steps):
        def body(*refs):
            na = len(self.shapes)
            i = pl.program_id(0)
            for step, phase in zip(steps, self.bind(refs[:na], refs[na:2 * na], refs[2 * na:])):
                pl.when(i == step)(phase)

        hbm = pl.BlockSpec(memory_space=pl.ANY)
        return _Part(body, list(grads), [hbm] * len(grads), [hbm] * len(grads),
                     [_sds((n, w), F32) for n, w in self.shapes], self.scratch_shapes())


def _dh_dx(dqa, dka, dva, dqn, dkn, dvn, dq4, dk4, dv4, dq16, dk16, dv16, dqc, dz, du, xb, cos, sa, sb, winT):
    tm = 512
    spt = SEQ // tm

    def body(dqa_ref, dka_ref, dva_ref, dqn_ref, dkn_ref, dvn_ref, dq4_ref, dk4_ref, dv4_ref,
             dq16_ref, dk16_ref, dv16_ref, dqc_ref, dz_ref, du_ref, xb_ref, cos_ref, sa_ref, sb_ref, w_ref,
             gx_ref, db_ref, gin_ref, dh_ref, scr):
        i = pl.program_id(0)

        @pl.when(i == 0)
        def _():
            db_ref[...] = jnp.zeros_like(db_ref)
            gin_ref[...] = jnp.zeros_like(gin_ref)

        cos_t, sa_t, sb_t = cos_ref[...], sa_ref[...], sb_ref[...]

        def rope_t(t):
            return _rope(t, cos_t, sa_t, sb_t, -1)

        def put(r0, val):
            n = val.shape[1]
            dh_ref[:, r0:r0 + n] = val.astype(BF16)
            db_ref[:, r0:r0 + n] += jnp.sum(val, axis=0, keepdims=True)

        put(O_QA, rope_t(dqa_ref[...].astype(F32)) * QK_SCALE)
        put(O_KA, rope_t(dka_ref[...].astype(F32)))
        put(O_VA, dva_ref[...].astype(F32))
        put(O_QC, dqc_ref[...].astype(F32) * QK_SCALE)
        put(O_Z, dz_ref[...].astype(F32))
        for k, (n_ref, r4, r16) in enumerate(((dqn_ref, dq4_ref, dq16_ref), (dkn_ref, dk4_ref, dk16_ref),
                                               (dvn_ref, dv4_ref, dv16_ref))):
            for j in range(2):
                sl = slice(128 * j, 128 * (j + 1))
                a, q = 2 * k + j, tm // 4
                scr[a] = n_ref[:, sl].astype(F32)
                for res in range(16):
                    scr[6 + a, pl.ds((res % 4) * q + res // 4, tm // 16, stride=4), :] = r16[0, res, :, sl].astype(F32)
                for res in range(4):
                    scr[a, pl.ds(res, q, stride=4), :] += (scr[6 + a, res * q:(res + 1) * q, :]
                                                           + r4[0, res, :, sl].astype(F32))
        cat = lambda a: jnp.concatenate([scr[a], scr[a + 1]], axis=1)
        put(O_QB, rope_t(cat(0)) * QK_SCALE)
        put(O_KB, rope_t(cat(2)))
        put(O_VB, cat(4))
        gx_ref[...] = _dot(dh_ref[...], w_ref[...], NN) + ALPHA * du_ref[...].astype(F32)
        gin_ref[...] += _dot(dh_ref[...], xb_ref[...], TN)

    tok = lambda w: pl.BlockSpec((tm, w), lambda i: (i, 0))
    tab = pl.BlockSpec((tm, 128), lambda i: (i % spt, 0))
    p4 = pl.BlockSpec((1, 4, tm // 4, W_B), lambda i: (i // spt, 0, i % spt, 0))
    p16 = pl.BlockSpec((1, 16, tm // 16, W_B), lambda i: (i // spt, 0, i % spt, 0))
    once = lambda shape: pl.BlockSpec(shape, lambda i: (0, 0), pipeline_mode=pl.Buffered(1))
    return pl.pallas_call(
        body, name="dh_dx", grid=(T // tm,),
        in_specs=[tok(W_A), tok(W_KV_A), tok(W_KV_A), tok(W_B), tok(W_B), tok(W_B), p4, p4, p4, p16, p16, p16,
                  tok(W_C), tok(D_MIX), tok(D_MODEL), tok(D_MODEL), tab, tab, tab, once((D_IN, D_MODEL))],
        out_specs=(tok(D_MODEL), _full((1, D_IN)), once((D_IN, D_MODEL))),
        out_shape=(_sds((T, D_MODEL), F32), _sds((1, D_IN), F32), _sds((D_IN, D_MODEL), F32)),
        scratch_shapes=[pltpu.VMEM((tm, D_IN), BF16), pltpu.VMEM((12, tm, 128), F32)],
        compiler_params=_cp(("arbitrary",), vmem_mb=56),
    )(*_pin(dqa, dka, dva, dqn, dkn, dvn, dq4, dk4, dv4, dq16, dk16, dv16, dqc, dz, du, xb, cos, sa, sb, winT))


def _reduce_grads(g_in, acc, dbin, dsink):
    rs = _ReduceScatter([(SH_IN, D_MODEL)])

    def body(g_ref, acc_ref, dbin_ref, dsink_ref, r_ref, sv_ref, sv_mine, sv_all, sv_send, sv_recv, *rs_scratch):
        x, y, c = lax.axis_index("x"), lax.axis_index("y"), lax.axis_index("c")
        chips = [(1 - x, y), (x, 1 - y), (1 - x, 1 - y)]
        start, exchange, relay, finish, drain = rs.bind((g_ref,), (r_ref,), rs_scratch)
        start()

        sv_mine[...] = jnp.zeros_like(sv_mine)
        sv_mine[0:4, 0:D_MODEL] = acc_ref[0:4, :]
        sv_mine[4:5, 0:D_IN] = dbin_ref[...]
        sv_mine[5:6, 0:128] = dsink_ref[...]
        my_dev = 4 * x + 2 * y + c
        others = [(x, y, 1 - c)] + [(*chip, cc) for chip in chips for cc in (c, 1 - c)]

        def sv_copy(j, to):
            return pltpu.make_async_remote_copy(
                src_ref=sv_mine, dst_ref=sv_all.at[my_dev], send_sem=sv_send.at[j], recv_sem=sv_recv.at[j],
                device_id=to, device_id_type=MESH)

        sv_sends = [sv_copy(j, to) for j, to in enumerate(others)]
        for cp in sv_sends:
            cp.start()
        exchange()
        relay()
        finish()
        sv_all[my_dev] = sv_mine[...]
        for j in range(7):
            sv_copy(j, (x, y, c)).wait_recv()
        tot = sv_all[0]
        for d in range(1, 8):
            tot = tot + sv_all[d]
        sv_ref[...] = tot
        drain()
        for cp in sv_sends:
            cp.wait_send()

    vm = pl.BlockSpec(memory_space=pltpu.VMEM)
    hbm = pl.BlockSpec(memory_space=pl.ANY)
    return pl.pallas_call(
        body, name="reduce_grads",
        out_shape=(_sds((SH_IN, D_MODEL), F32), _vm_sds((8, SV_W), F32)),
        in_specs=[hbm, vm, vm, vm], out_specs=(hbm, vm),
        scratch_shapes=[pltpu.VMEM((8, SV_W), F32), pltpu.VMEM((8, 8, SV_W), F32),
                        pltpu.SemaphoreType.DMA((7,)), pltpu.SemaphoreType.DMA((7,))] + rs.scratch_shapes(),
        compiler_params=_cp(vmem_mb=40),
    )(pltpu.with_memory_space_constraint(g_in, pltpu.HBM), acc, dbin, dsink)


def _adamw_update(w, g, m, v):
    nm = ADAM_B1 * m + (1.0 - ADAM_B1) * g
    nv = ADAM_B2 * v + (1.0 - ADAM_B2) * (g * g)
    m_hat = nm / (1.0 - ADAM_B1 ** ADAM_STEP)
    v_hat = nv / (1.0 - ADAM_B2 ** ADAM_STEP)
    return -ADAM_LR * (m_hat / (jnp.sqrt(v_hat) + ADAM_EPS) + ADAM_WD * w), nm, nv


SMALL = ((4, D_IN, 1.0), (5, 8, -1.0), (1, D_MIX, 1.0), (2, D_MODEL, 1.0), (3, D_MODEL, 1.0))


def _adamw_all(items, sv, ws, ms, vs, n_steps=4):
    nb, ns = 4 * len(items), len(SMALL)

    def body(*refs):
        ins, sv_ref, small_in = refs[:nb], refs[nb], refs[nb + 1:nb + 1 + 3 * ns]
        outs = refs[nb + 1 + 3 * ns:]
        big_out, loss_ref, small_out = outs[:nb], outs[nb], outs[nb + 1:]
        for p in range(len(items)):
            w_ref, g_ref, m_ref, v_ref = ins[4 * p:4 * p + 4]
            gv = g_ref[...]
            big_out[4 * p][...] = gv
            big_out[4 * p + 1][...], big_out[4 * p + 2][...], big_out[4 * p + 3][...] = _adamw_update(
                w_ref[...], gv, m_ref[...], v_ref[...])

        @pl.when(pl.program_id(0) == 0)
        def _():
            loss_ref[...] = jnp.sum(sv_ref[0:1, 0:D_MODEL], axis=1, keepdims=True)
            for p, (row, width, sign) in enumerate(SMALL):
                gv = sign * sv_ref[row:row + 1, 0:width]
                small_out[4 * p][...] = gv
                small_out[4 * p + 1][...], small_out[4 * p + 2][...], small_out[4 * p + 3][...] = _adamw_update(
                    small_in[p][...], gv, small_in[ns + p][...], small_in[2 * ns + p][...])

    specs, shapes, args = [], [], []
    for w, g, m, v in items:
        rows, width = w.shape
        specs += [pl.BlockSpec((rows // n_steps, width), lambda i: (i, 0))] * 4
        shapes += [_sds((rows, width), F32)] * 4
        args += [w, g, m, v]
    small_args = [*ws, *ms, *vs]
    whole = lambda a: _full(a.shape)
    res = pl.pallas_call(
        body, name="adamw", grid=(n_steps,),
        in_specs=specs + [whole(sv)] + [whole(a) for a in small_args],
        out_specs=tuple(specs + [_full((1, 1))] + [whole(w) for w in ws for _ in range(4)]),
        out_shape=tuple(shapes + [_sds((1, 1), F32)] + [_sds(w.shape, F32) for w in ws for _ in range(4)]),
        compiler_params=_cp(("arbitrary",), vmem_mb=40),
    )(*_pin(*args, sv, *small_args))
    big = [tuple(res[4 * p:4 * p + 4]) for p in range(len(items))]
    return big, res[nb], [tuple(res[nb + 1 + 4 * p:nb + 5 + 4 * p]) for p in range(ns)]


def _rope_tables():
    pos = np.arange(SEQ, dtype=np.float32)
    inv = (np.float32(ROPE_THETA) ** (-np.arange(0, 64, 2, dtype=np.float32) / np.float32(64))).astype(np.float32)
    ang = np.tile(pos[:, None] * inv[None, :], (1, 4))
    cos, sin = np.cos(ang).astype(np.float32), np.sin(ang).astype(np.float32)
    low = (np.arange(128) % 64) < 32
    zero = np.float32(0.0)
    return jnp.asarray(cos), jnp.asarray(np.where(low, -sin, zero)), jnp.asarray(np.where(low, zero, sin))


def _local_step(x2, mem2, tgt2, winT, wout, wmem, b_in, sinks, g_branch, ln_gain, ln_bias):
    cos, sa, sb = _rope_tables()
    sinkv = jnp.pad(sinks, ((0, 0), (0, 120)))
    head_of_lane = np.arange(512)[:, None] // 64
    gather8 = jnp.asarray(head_of_lane == np.arange(128)[None, :], BF16)
    gather4 = jnp.asarray(head_of_lane[:W_B] == np.arange(128)[None, :], BF16)
    spread4 = jnp.asarray((head_of_lane[:W_B] == np.arange(128)[None, :]).T, BF16)

    xb, qa, ka, va, bn, b4, b16, qc, z, wout, wmem = _in_proj(x2, winT, b_in, cos, sa, sb, wout, wmem)
    memb, mkv = _mem_kv(mem2, wmem)
    b4f, b16f = b4.reshape(T, 768), b16.reshape(T, 768)

    swa = dict(kind="band", nb=SEQ // BLK, max_dist=BLK - 1, gqa=True)
    dil = (dict(kind="band", nb=SEQ // BLK), dict(kind="band", nb=SEQ // 4 // BLK), dict(kind="band", nb=1))
    (oa, lse_a), (o1, l1), (o4, l4), (o16, l16), (oc, lse_c) = _run_parts("attn_fwd", [
        _attn_fwd(qa, 0, W_A, ka, 0, va, 0, W_KV_A, sinks=sinks, **swa),
        _attn_fwd(bn, 0, W_B, bn, 1, bn, 2, W_B, **dil[0]),
        _attn_fwd(b4f, 0, W_B, b4f, 1, b4f, 2, W_B, **dil[1]),
        _attn_fwd(b16f, 0, W_B, b16f, 1, b16f, 2, W_B, **dil[2]),
        _attn_fwd(qc, 0, W_C, mkv, 0, mkv, 1, W_C, kind="mem")], "parallel", 48)

    s4 = lambda w: (B_LOC, 4, SEQ // 4, w)
    s16 = lambda w: (B_LOC, 16, SEQ // 16, w)
    (du, dz, doa, dla, dobn, lsen, dlbn, dob4, lse4, dlb4, dob16, lse16, dlb16, doc, dlc, acc, g_out) = _middle(
        oa, o1, l1, o4.reshape(s4(W_B)), l4.reshape(s4(128)), o16.reshape(s16(W_B)), l16.reshape(s16(128)), oc, z,
        x2, tgt2, g_branch, ln_gain, ln_bias, wout, spread4, gather4, gather8)

    flat = lambda a: a.reshape(T, a.shape[-1])
    (dqa, dka, dva, dsink), (dqc, g_mem) = _run_parts("attn_bwd_a", [
        _attn_bwd(qa, 0, W_A, ka, 0, va, 0, W_KV_A, doa, lse_a, dla, sinkv=sinkv, **swa),
        _attn_bwd(qc, 0, W_C, mkv, 0, mkv, 1, W_C, doc, lse_c, dlc, kind="mem", mem_in=memb)], "arbitrary", 48)
    last = T // QR - 1
    (r_out, r_mem), (dqn, dkn, dvn), (dq4, dk4, dv4), (dq16, dk16, dv16) = _run_parts("attn_bwd_b", [
        _ReduceScatter([(SH_OUT, D_MODEL), (SH_MEM, 2 * W_C)]).part((g_out, g_mem), (0, 1, 2, last, last)),
        _attn_bwd(bn, 0, W_B, bn, 1, bn, 2, W_B, dobn, lsen, dlbn, **dil[0]),
        _attn_bwd(b4f, 0, W_B, b4f, 1, b4f, 2, W_B, flat(dob4), flat(lse4), flat(dlb4), **dil[1]),
        _attn_bwd(b16f, 0, W_B, b16f, 1, b16f, 2, W_B, flat(dob16), flat(lse16), flat(dlb16), **dil[2])],
        "arbitrary", 62)

    r4 = lambda a: a.reshape(s4(W_B))
    r16 = lambda a: a.reshape(s16(W_B))
    gx, dbin, g_in = _dh_dx(dqa, dka, dva, dqn, dkn, dvn, r4(dq4), r4(dk4), r4(dv4), r16(dq16), r16(dk16),
                            r16(dv16), dqc, dz, du, xb, cos, sa, sb, winT)
    return gx, g_in, r_out, r_mem, acc, dbin, dsink


def kernel(x, mem, w_in, b_in, w_mem, attn_sinks, g_branch, w_out, ln_gain, ln_bias, loss_target, m_w_in, m_b_in, m_w_mem, m_attn_sinks, m_g_branch, m_w_out, m_ln_gain, m_ln_bias, v_w_in, v_b_in, v_w_mem, v_attn_sinks, v_g_branch, v_w_out, v_ln_gain, v_ln_bias):
    winT, wout, wmem = _gather_weights(w_in[0].T, w_out[0], w_mem[0])
    gx, g_in, r_out, r_mem, acc, dbin, dsink = _local_step(
        x.reshape(T, D_MODEL), mem.reshape(B_LOC * MEM_LEN, D_MODEL), loss_target.reshape(T, D_MODEL),
        winT, wout, wmem, b_in, attn_sinks, g_branch, ln_gain, ln_bias)
    r_in, sv = _reduce_grads(g_in, acc, dbin, dsink)

    small = ["b_in", "attn_sinks", "g_branch", "ln_gain", "ln_bias"]
    big, loss, steps = _adamw_all(
        [(w_in[0].T, r_in, m_w_in[0].T, v_w_in[0].T), (w_out[0], r_out, m_w_out[0], v_w_out[0]),
         (w_mem[0], r_mem, m_w_mem[0], v_w_mem[0])],
        sv, [b_in, attn_sinks, g_branch, ln_gain, ln_bias], [m_b_in, m_attn_sinks, m_g_branch, m_ln_gain, m_ln_bias],
        [v_b_in, v_attn_sinks, v_g_branch, v_ln_gain, v_ln_bias])
    out = dict(zip(small, steps))
    out["w_in"] = tuple(a.T[None] for a in big[0])
    out["w_out"], out["w_mem"] = (tuple(a[None] for a in st) for st in big[1:])
    names = ["w_in", "b_in", "w_mem", "attn_sinks", "g_branch", "w_out", "ln_gain", "ln_bias"]
    return (loss.reshape(()), gx.reshape(B_LOC, SEQ, D_MODEL), *[out[n][k] for k in range(4) for n in names])
```

```python
import jax
import jax.numpy as jnp
import numpy as np
from jax import lax
from jax.experimental import pallas as pl
from jax.experimental.pallas import tpu as pltpu

F32, BF16 = jnp.float32, jnp.bfloat16

D_MODEL = 1024
SEQ = 2048
B_LOC = 2
T = B_LOC * SEQ
BLK = 128
MEM_LEN = 256
W_A, W_KV_A, W_B, W_C, D_MIX = 512, 128, 256, 256, 1024
D_IN = 2816
O_QA, O_KA, O_VA, O_QB, O_KB, O_VB, O_QC, O_Z = 0, 512, 640, 768, 1024, 1280, 1536, 1792
ROPE_THETA = 10000.0
LN_EPS = 1e-5
RMS_EPS = 1e-6
ALPHA = 2.0 ** 0.25
QK_SCALE = 0.125
N_CHIP = 4
SH_IN, SH_OUT, SH_MEM = D_IN // N_CHIP, D_MIX // N_CHIP, D_MODEL // N_CHIP
NEG = -1e30
ADAM_LR, ADAM_B1, ADAM_B2, ADAM_EPS, ADAM_WD, ADAM_STEP = 0.001, 0.9, 0.999, 1e-08, 0.01, 10
SV_W = 3072
MESH = pl.DeviceIdType.MESH

NN = ((1,), (0,))
NT = ((1,), (1,))
TN = ((0,), (0,))


def _dot(a, b, dims):
    return lax.dot_general(a, b, (dims, ((), ())), preferred_element_type=F32)


def _cp(sem=None, vmem_mb=None):
    kw = {}
    if sem is not None:
        kw["dimension_semantics"] = sem
    if vmem_mb is not None:
        kw["vmem_limit_bytes"] = vmem_mb * 1024 * 1024
    return pltpu.CompilerParams(**kw)


def _sds(shape, dtype):
    return pltpu.HBM(shape, dtype)


def _vm_sds(shape, dtype):
    return jax.ShapeDtypeStruct(shape, dtype)


def _pin(*args):
    return [pltpu.with_memory_space_constraint(a, pltpu.HBM) for a in args]


def _full(shape):
    n = len(shape)
    return pl.BlockSpec(shape, lambda *_: (0,) * n)


def _shard_rows(ref, n, chip, half):
    start = pl.multiple_of((2 * chip[0] + chip[1]) * n + half * (n // 2), 16)
    return ref.at[pl.ds(start, n // 2), :]


def _gather_weights(win_sh, wout_sh, wmem_sh):
    half, piece = SH_IN // 2, SH_IN // 4
    shards = ((SH_IN, D_MODEL), (SH_OUT, D_MODEL), (SH_MEM, 2 * W_C))

    def body(a_ref, b_ref, c_ref, oa_ref, ob_ref, oc_ref, raw_a, raw_b, raw_c, own_a, own_b, own_c,
             load_sem, store_sem, ici_send, ici_recv, d2d_send, d2d_recv):
        x, y, c = lax.axis_index("x"), lax.axis_index("y"), lax.axis_index("c")
        me, sibling = (x, y, c), (x, y, 1 - c)
        xn, yn, dg = (1 - x, y), (x, 1 - y), (1 - x, 1 - y)
        srcs, raws = (a_ref, b_ref, c_ref), (raw_a, raw_b, raw_c)
        owns, outs = (own_a, own_b, own_c), (oa_ref, ob_ref, oc_ref)
        loads = [pltpu.make_async_copy(srcs[a], raws[a], load_sem.at[a]) for a in range(3)]
        for cp in loads:
            cp.start()

        def rows(chip, hf, q):
            start = pl.multiple_of((2 * chip[0] + chip[1]) * SH_IN + hf * half + q * piece, 16)
            return oa_ref.at[pl.ds(start, piece), :]

        def copy(sems, k, chip, hf, q, to, src=None):
            blk = rows(chip, hf, q)
            return pltpu.make_async_remote_copy(
                src_ref=blk if src is None else src, dst_ref=blk, send_sem=sems[0].at[k], recv_sem=sems[1].at[k],
                device_id=to, device_id_type=MESH)

        def my_piece(q):
            return own_a.at[pl.ds(pl.multiple_of(c * half + q * piece, 16), piece), :]

        ici, d2d = (ici_send, ici_recv), (d2d_send, d2d_recv)
        stores, direct = [], []
        for a, (n, _) in enumerate(shards):
            loads[a].wait()
            owns[a][...] = raws[a][...].astype(BF16)
            mine = pl.ds(pl.multiple_of((2 * x + y) * n, 16), n)
            stores.append(pltpu.make_async_copy(owns[a], outs[a].at[mine, :], store_sem.at[a]))
            stores[-1].start()
            if a == 0:
                direct = [copy(ici, 0, (x, y), c, 0, (*xn, c), my_piece(0)),
                          copy(ici, 1, (x, y), c, 1, (*xn, c), my_piece(1)),
                          copy(ici, 3, (x, y), c, 0, (*yn, c), my_piece(0)),
                          copy(ici, 4, (x, y), c, 1, (*yn, c), my_piece(1))]
                for cp in direct:
                    cp.start()
        arrivals = [(0, xn, 0), (1, xn, 1), (3, yn, 0), (4, yn, 1), (2, dg, 1), (5, dg, 0)]
        passed = []
        for k, chip, q in arrivals:
            copy(ici, k, chip, c, q, me).wait_recv()
            if k == 0:
                passed.append(copy(ici, 5, xn, c, 0, (*yn, c)))
                passed[-1].start()
            if k == 4:
                passed.append(copy(ici, 2, yn, c, 1, (*xn, c)))
                passed[-1].start()
            passed.append(copy(d2d, k, chip, c, q, sibling))
            passed[-1].start()
        for k, chip, q in arrivals:
            copy(d2d, k, chip, 1 - c, q, me).wait_recv()
        for cp in direct + passed:
            cp.wait_send()
        for cp in stores:
            cp.wait()

    hbm = pl.BlockSpec(memory_space=pl.ANY)
    return pl.pallas_call(
        body, name="gather_weights",
        out_shape=(_sds((D_IN, D_MODEL), BF16), _sds((D_MIX, D_MODEL), BF16), _sds((D_MODEL, 2 * W_C), BF16)),
        in_specs=[hbm, hbm, hbm], out_specs=(hbm, hbm, hbm),
        scratch_shapes=([pltpu.VMEM(sh, F32) for sh in shards] + [pltpu.VMEM(sh, BF16) for sh in shards]
                        + [pltpu.SemaphoreType.DMA((3,))] * 2 + [pltpu.SemaphoreType.DMA((6,))] * 4),
        compiler_params=_cp(vmem_mb=40),
    )(*_pin(win_sh, wout_sh, wmem_sh))


def _rope(t, cos, sa, sb, sign):
    w = t.shape[1]
    reps = w // 128
    c, a, b = (jnp.tile(v, (1, reps)) if reps > 1 else v for v in (cos, sa, sb))
    rot = pltpu.roll(t, w - 32, 1) * a + pltpu.roll(t, 32, 1) * b
    return t * c + rot if sign > 0 else t * c - rot


def _in_proj(x, winT, b_in, cos, sa, sb, wout_own, wmem_own):
    tm = 512
    spt = SEQ // tm
    n_steps = T // tm
    forward_step = n_steps // 2

    def body(x_ref, w_ref, b_ref, cos_ref, sa_ref, sb_ref, wo_in, wm_in,
             xb_ref, qa_ref, ka_ref, va_ref, bn_ref, b4_ref, b16_ref, qc_ref, z_ref, wo_ref, wm_ref,
             scr, ici_send, ici_recv, d2d_send, d2d_recv):
        i = pl.program_id(0)
        mx, my, mc = lax.axis_index("x"), lax.axis_index("y"), lax.axis_index("c")
        chips = [(1 - mx, my), (mx, 1 - my), (1 - mx, 1 - my)]
        full = ((wo_ref, SH_OUT), (wm_ref, SH_MEM))

        def copy(sems, a, j, chip_of_block, half, to):
            blk = _shard_rows(full[a][0], full[a][1], chip_of_block, half)
            return pltpu.make_async_remote_copy(
                src_ref=blk, dst_ref=blk, send_sem=sems[0].at[a, j], recv_sem=sems[1].at[a, j],
                device_id=to, device_id_type=MESH)

        ici, d2d = (ici_send, ici_recv), (d2d_send, d2d_recv)
        pairs = [(a, j, chip) for j, chip in enumerate(chips) for a in range(2)]

        @pl.when(i == 0)
        def _():
            for a, j, chip in pairs:
                copy(ici, a, j, (mx, my), mc, (*chip, mc)).start()

        @pl.when(i == forward_step)
        def _():
            for a, j, chip in pairs:
                copy(ici, a, j, chip, mc, (mx, my, mc)).wait_recv()
                copy(d2d, a, j, chip, mc, (mx, my, 1 - mc)).start()

        @pl.when(i == n_steps - 1)
        def _():
            for a, j, chip in pairs:
                copy(d2d, a, j, chip, 1 - mc, (mx, my, mc)).wait_recv()
            for a, j, chip in pairs:
                copy(ici, a, j, (mx, my), mc, (*chip, mc)).wait_send()
                copy(d2d, a, j, chip, mc, (mx, my, 1 - mc)).wait_send()

        xb = x_ref[...].astype(BF16)
        xb_ref[...] = xb
        cos_t, sa_t, sb_t = cos_ref[...], sa_ref[...], sb_ref[...]

        def proj(r0, n):
            return _dot(xb, w_ref[r0:r0 + n, :], NT) + b_ref[:, r0:r0 + n]

        def rope(t):
            return _rope(t, cos_t, sa_t, sb_t, +1)

        parts = (rope(proj(O_QB, W_B)) * QK_SCALE, rope(proj(O_KB, W_B)), proj(O_VB, W_B))
        for k, part in enumerate(parts):
            bn_ref[:, 256 * k:256 * (k + 1)] = part.astype(BF16)
            scr[2 * k] = part[:, :128]
            scr[2 * k + 1] = part[:, 128:]
        for j in range(6):
            lanes = slice(128 * j, 128 * (j + 1))
            for res in range(4):
                t = scr[j, pl.ds(res, tm // 4, stride=4), :]
                b4_ref[0, res, :, lanes] = t.astype(BF16)
                scr[6 + j, res * (tm // 4):(res + 1) * (tm // 4), :] = t
            for res in range(16):
                b16_ref[0, res, :, lanes] = scr[6 + j, pl.ds((res % 4) * (tm // 4) + res // 4, tm // 16, stride=4),
                                                :].astype(BF16)
        qa_ref[...] = (rope(proj(O_QA, W_A)) * QK_SCALE).astype(BF16)
        assert O_VA == O_KA + W_KV_A
        kv = proj(O_KA, 2 * W_KV_A)
        ka_ref[...] = rope(kv[:, :W_KV_A]).astype(BF16)
        va_ref[...] = kv[:, W_KV_A:].astype(BF16)
        qc_ref[...] = (proj(O_QC, W_C) * QK_SCALE).astype(BF16)
        z_ref[...] = proj(O_Z, D_MIX).astype(BF16)

    tok = lambda w: pl.BlockSpec((tm, w), lambda i: (i, 0))
    tab = pl.BlockSpec((tm, 128), lambda i: (i % spt, 0))
    hbm = pl.BlockSpec(memory_space=pl.ANY)
    return pl.pallas_call(
        body, name="in_proj", grid=(n_steps,),
        in_specs=[tok(D_MODEL), _full((D_IN, D_MODEL)), _full((1, D_IN)), tab, tab, tab, hbm, hbm],
        out_specs=(tok(D_MODEL), tok(W_A), tok(W_KV_A), tok(W_KV_A), tok(768),
                   pl.BlockSpec((1, 4, tm // 4, 768), lambda i: (i // spt, 0, i % spt, 0)),
                   pl.BlockSpec((1, 16, tm // 16, 768), lambda i: (i // spt, 0, i % spt, 0)),
                   tok(W_C), tok(D_MIX), hbm, hbm),
        out_shape=(_sds((T, D_MODEL), BF16), _sds((T, W_A), BF16), _sds((T, W_KV_A), BF16), _sds((T, W_KV_A), BF16),
                   _sds((T, 768), BF16), _sds((B_LOC, 4, SEQ // 4, 768), BF16), _sds((B_LOC, 16, SEQ // 16, 768), BF16),
                   _sds((T, W_C), BF16), _sds((T, D_MIX), BF16),
                   _sds((D_MIX, D_MODEL), BF16), _sds((D_MODEL, 2 * W_C), BF16)),
        input_output_aliases={6: 9, 7: 10},
        scratch_shapes=[pltpu.VMEM((12, tm, 128), F32)] + [pltpu.SemaphoreType.DMA((2, 3))] * 4,
        compiler_params=_cp(("arbitrary",), vmem_mb=48),
    )(*_pin(x, winT, b_in, cos, sa, sb, wout_own, wmem_own))


def _mem_kv(mem, wmem):
    def body(m_ref, w_ref, mb_ref, kv_ref):
        mb = m_ref[...].astype(BF16)
        mb_ref[...] = mb
        kv_ref[...] = _dot(mb, w_ref[...], NN).astype(BF16)

    n = B_LOC * MEM_LEN
    return pl.pallas_call(
        body, name="mem_kv",
        out_shape=(_sds((n, D_MODEL), BF16), _sds((n, 2 * W_C), BF16)),
    )(*_pin(mem, wmem))


class _Part:
    def __init__(self, body, args, in_specs, out_specs, out_shape, scratch=()):
        self.body, self.args, self.in_specs, self.out_specs, self.out_shape = body, args, in_specs, out_specs, out_shape
        self.scratch = list(scratch)


def _run_parts(name, parts, semantics, vmem_mb):
    n_in = [len(p.args) for p in parts]
    n_out = [len(p.out_shape) for p in parts]
    n_scr = [len(p.scratch) for p in parts]

    def body(*refs):
        ins, outs, scr = refs[:sum(n_in)], refs[sum(n_in):sum(n_in) + sum(n_out)], refs[sum(n_in) + sum(n_out):]
        i0 = o0 = s0 = 0
        for p, ni, no, ns in zip(parts, n_in, n_out, n_scr):
            p.body(*ins[i0:i0 + ni], *outs[o0:o0 + no], *scr[s0:s0 + ns])
            i0, o0, s0 = i0 + ni, o0 + no, s0 + ns

    res = pl.pallas_call(
        body, name=name, grid=(T // QR,),
        in_specs=[sp for p in parts for sp in p.in_specs], out_specs=tuple(sp for p in parts for sp in p.out_specs),
        out_shape=tuple(sh for p in parts for sh in p.out_shape),
        scratch_shapes=[sc for p in parts for sc in p.scratch],
        compiler_params=_cp((semantics,), vmem_mb=vmem_mb),
    )(*_pin(*[a for p in parts for a in p.args]))
    out, o0 = [], 0
    for no in n_out:
        out.append(tuple(res[o0:o0 + no]))
        o0 += no
    return out


QB = 8
QR = QB * BLK


def _lane_lo():
    return lax.broadcasted_iota(jnp.int32, (1, 128), 1) < 64


def _dup_head(k2, hk, lo):
    kf = k2.astype(F32)
    r = pltpu.roll(kf, 64, 1)
    return (jnp.where(lo, kf, r) if hk == 0 else jnp.where(lo, r, kf)).astype(BF16)


def _stack_heads(pairs, lo):
    parts = []
    for x2 in pairs:
        z = jnp.zeros_like(x2)
        parts += [jnp.where(lo, x2, z), jnp.where(lo, z, x2)]
    return jnp.concatenate(parts, axis=0)


def _prev_mode(kind, nb, j):
    if kind == "mem" or nb == 1:
        return "no"
    if nb <= QB:
        return "yes" if j % nb else "no"
    return "yes" if j else "dyn"


class _Attn:
    def __init__(self, kind, nb, max_dist, gqa, qw, kvw, qcb, kcb, vcb):
        self.kind, self.nb, self.gqa, self.qw, self.kvw = kind, nb, gqa, qw, kvw
        npairs = qw // 128
        self.groups = ([(hk, [2 * hk, 2 * hk + 1]) for hk in range(npairs // 2)] if gqa
                       else [(p, [p]) for p in range(npairs)])
        self.nh = 2 * len(self.groups[0][1])
        self.cols = 128 * self.nh
        self.reach = BLK - max_dist
        self.ext_prev = kind == "band" and nb > QB
        self.q_spec = pl.BlockSpec((QR, qw), lambda g: (g, qcb))
        self.row_spec = pl.BlockSpec((QR, qw), lambda g: (g, 0))
        self.stat_spec = pl.BlockSpec((QR, 128), lambda g: (g, 0))
        if kind == "mem":
            per = SEQ // QR
            self.kv_specs = [pl.BlockSpec((MEM_LEN, kvw), lambda g: (g // per, kcb)),
                             pl.BlockSpec((MEM_LEN, kvw), lambda g: (g // per, vcb))]
        else:
            self.kv_specs = [pl.BlockSpec((QR, kvw), lambda g: (g, kcb)), pl.BlockSpec((QR, kvw), lambda g: (g, vcb))]
            if self.ext_prev:
                self.kv_specs += [pl.BlockSpec((BLK, kvw), lambda g: (jnp.maximum(g * QB - 1, 0), kcb)),
                                  pl.BlockSpec((BLK, kvw), lambda g: (jnp.maximum(g * QB - 1, 0), vcb))]

    def masks(self):
        if self.kind == "mem":
            return None
        kj = lax.broadcasted_iota(jnp.int32, (2 * BLK, self.cols), 0)
        qi = lax.broadcasted_iota(jnp.int32, (2 * BLK, self.cols), 1) & (BLK - 1)
        kj1 = lax.broadcasted_iota(jnp.int32, (BLK, self.cols), 0)
        qi1 = lax.broadcasted_iota(jnp.int32, (BLK, self.cols), 1) & (BLK - 1)
        return kj, qi, kj1 <= qi1

    def keys(self, j, gi, kc_ref, vc_ref, kp_ref, vp_ref, lo, kq, g):
        def kv(k_ref, v_ref, r):
            if self.gqa:
                return _dup_head(k_ref[r, :], gi, lo), _dup_head(v_ref[r, :], gi, lo)
            sl = slice(128 * gi, 128 * (gi + 1))
            return k_ref[r, sl], v_ref[r, sl]

        if self.kind == "mem":
            key0 = pl.multiple_of((g // (SEQ // QR)) * MEM_LEN, MEM_LEN)
            return (*kv(kc_ref, vc_ref, slice(None)), None, [(0, MEM_LEN, key0)])
        kj, qi, cur = kq
        row0 = g * QR + BLK * j
        mode = _prev_mode(self.kind, self.nb, j)
        if mode == "no":
            return (*kv(kc_ref, vc_ref, slice(BLK * j, BLK * (j + 1))), cur, [(0, BLK, pl.multiple_of(row0, BLK))])
        if mode == "yes":
            mask = jnp.logical_and(kj >= qi + self.reach, kj <= qi + BLK)
            return (*kv(kc_ref, vc_ref, slice(BLK * (j - 1), BLK * (j + 1))), mask,
                    [(0, 2 * BLK, pl.multiple_of(row0 - BLK, BLK))])
        has_prev = ((g * QB) % self.nb) > 0
        hp = has_prev.astype(jnp.int32)
        mask = jnp.logical_and(kj >= qi * hp + (self.reach * hp + BLK * (1 - hp)), kj <= qi + BLK)
        kp, vp = kv(kp_ref, vp_ref, slice(None))
        kc, vc = kv(kc_ref, vc_ref, slice(0, BLK))
        return (jnp.concatenate([kp, kc], axis=0), jnp.concatenate([vp, vc], axis=0), mask,
                [(0, BLK, pl.multiple_of(jnp.maximum(row0 - BLK, 0), BLK)), (BLK, BLK, pl.multiple_of(row0, BLK))])


def _attn_fwd(q, qcb, qw, k, kcb, v, vcb, kvw, *, kind, nb=1, max_dist=BLK, gqa=False, sinks=None):
    a = _Attn(kind, nb, max_dist, gqa, qw, kvw, qcb, kcb, vcb)

    def body(*refs):
        it = iter(refs)
        q_ref, kc_ref, vc_ref = next(it), next(it), next(it)
        kp_ref, vp_ref = (next(it), next(it)) if a.ext_prev else (None, None)
        sink_ref = next(it) if sinks is not None else None
        o_ref, lse_ref = next(it), next(it)
        g = pl.program_id(0)
        lo = _lane_lo()
        top = lax.broadcasted_iota(jnp.int32, (128, 1), 0) < 64
        rid = lax.broadcasted_iota(jnp.int32, (8, 128), 0)
        kq = a.masks()
        stats = {}

        def scores(j, gi, pairs):
            rows = slice(BLK * j, BLK * (j + 1))
            qs = _stack_heads([q_ref[rows, 128 * p:128 * (p + 1)] for p in pairs], lo)
            kk, vv, mask, _ = a.keys(j, gi, kc_ref, vc_ref, kp_ref, vp_ref, lo, kq, g)
            pieces = [slice(r0, r0 + BLK) for r0 in range(0, kk.shape[0], BLK)]
            return dict(j=j, gi=gi, pairs=pairs, rows=rows, vv=vv, mask=mask, pieces=pieces,
                        ss=[_dot(kk[r], qs, NT) for r in pieces])

        def softmax(c):
            gi, mask = c["gi"], c["mask"]
            ss = [s if mask is None else jnp.where(mask[r], s, NEG) for r, s in zip(c["pieces"], c.pop("ss"))]
            m = jnp.max(ss[0], axis=0, keepdims=True)
            for s in ss[1:]:
                m = jnp.maximum(m, jnp.max(s, axis=0, keepdims=True))
            if sink_ref is not None:
                sk = jnp.concatenate([jnp.full((1, 128), sink_ref[0, a.nh * gi + i], F32) for i in range(a.nh)], axis=1)
                m = jnp.maximum(m, sk)
            ps = [jnp.exp(s - m) for s in ss]
            l = sum(jnp.sum(p, axis=0, keepdims=True) for p in ps)
            if sink_ref is not None:
                l = l + jnp.exp(sk - m)
            c["ps"] = [p.astype(BF16) for p in ps]
            c["l"], c["lse"] = l, m + jnp.log(l)

        def outputs(c):
            j, gi, rows = c["j"], c["gi"], c["rows"]
            ot = sum(_dot(c["vv"][r], p, TN) for r, p in zip(c["pieces"], c["ps"]))
            ot = ot * pl.reciprocal(c["l"], approx=True)
            for i, p in enumerate(c["pairs"]):
                o2t = jnp.where(top, ot[:, 256 * i:256 * i + 128], ot[:, 256 * i + 128:256 * i + 256])
                o_ref[rows, 128 * p:128 * (p + 1)] = o2t.T.astype(BF16)
            stat = stats.get(j, jnp.zeros((8, 128), F32))
            for i in range(a.nh):
                stat = jnp.where(rid == a.nh * gi + i, c["lse"][:, 128 * i:128 * (i + 1)], stat)
            stats[j] = stat
            if gi == a.groups[-1][0]:
                lse_ref[rows, :] = jnp.concatenate([stats.pop(j), jnp.zeros((120, 128), F32)], axis=0).T

        chains = [(j, gi, pairs) for j in range(QB) for gi, pairs in a.groups]
        live = {}
        for t in range(len(chains) + 2):
            if t < len(chains):
                live[t] = scores(*chains[t])
            if 0 <= t - 1 < len(chains):
                softmax(live[t - 1])
            if 0 <= t - 2 < len(chains):
                outputs(live.pop(t - 2))


    args = [q, k, v] + ([k, v] if a.ext_prev else [])
    in_specs = [a.q_spec] + a.kv_specs
    if sinks is not None:
        args.append(sinks)
        in_specs.append(pl.BlockSpec(memory_space=pltpu.SMEM))
    return _Part(body, args, in_specs, [a.row_spec, a.stat_spec], [_sds((T, qw), BF16), _sds((T, 128), F32)])


def _attn_bwd(q, qcb, qw, k, kcb, v, vcb, kvw, do, lse, dl, *, kind, nb=1, max_dist=BLK, gqa=False, sinkv=None,
              mem_in=None):
    a = _Attn(kind, nb, max_dist, gqa, qw, kvw, qcb, kcb, vcb)

    def body(*refs):
        it = iter(refs)
        q_ref, kc_ref, vc_ref = next(it), next(it), next(it)
        kp_ref, vp_ref = (next(it), next(it)) if a.ext_prev else (None, None)
        do_ref, lse_ref, dl_ref = next(it), next(it), next(it)
        sinkv_ref = next(it) if sinkv is not None else None
        mem_ref = next(it) if kind == "mem" else None
        dq_ref = next(it)
        if kind == "mem":
            gmem_ref = next(it)
        else:
            dk_out, dv_out = next(it), next(it)
        dsink_ref = next(it) if sinkv is not None else None
        if kind != "mem":
            dk_ref, dv_ref, stage_k, stage_v, flush_sem = next(it), next(it), next(it), next(it), next(it)
        else:
            dkv_ref = next(it)
        g = pl.program_id(0)
        lo = _lane_lo()
        top = lax.broadcasted_iota(jnp.int32, (128, 1), 0) < 64

        @pl.when(g == 0)
        def _():
            if kind == "mem":
                dkv_ref[...] = jnp.zeros_like(dkv_ref)
            else:
                dk_ref[...] = jnp.zeros_like(dk_ref)
                dv_ref[...] = jnp.zeros_like(dv_ref)
            if dsink_ref is not None:
                dsink_ref[...] = jnp.zeros_like(dsink_ref)

        kq = a.masks()
        stats_t = {}

        def first_matmuls(j, gi, pairs):
            rows = slice(BLK * j, BLK * (j + 1))
            if j not in stats_t:
                stats_t[j] = (lse_ref[rows, :].T, dl_ref[rows, :].T)
            lse_t, dl_t = stats_t[j]
            heads = [a.nh * gi + i for i in range(a.nh)]
            c = dict(rows=rows, gi=gi, pairs=pairs)
            c["qs"] = _stack_heads([q_ref[rows, 128 * p:128 * (p + 1)] for p in pairs], lo)
            c["dos"] = _stack_heads([do_ref[rows, 128 * p:128 * (p + 1)] for p in pairs], lo)
            c["lse_row"] = jnp.concatenate([lse_t[h:h + 1, :] for h in heads], axis=1)
            c["dl_row"] = jnp.concatenate([dl_t[h:h + 1, :] for h in heads], axis=1)
            c["kk"], vv, c["mask"], c["dests"] = a.keys(j, gi, kc_ref, vc_ref, kp_ref, vp_ref, lo, kq, g)
            c["s"] = _dot(c["kk"], c["qs"], NT)
            c["dp"] = _dot(vv, c["dos"], NT)
            return c

        def elementwise(c):
            s = c.pop("s")
            if c["mask"] is not None:
                s = jnp.where(c["mask"], s, NEG)
            p = jnp.exp(s - c["lse_row"])
            c["ds"] = (p * (c.pop("dp") - c["dl_row"])).astype(BF16)
            c["p"] = p.astype(BF16)

        def last_matmuls(c):
            gi, rows = c["gi"], c["rows"]
            dqt = _dot(c["kk"], c["ds"], TN)
            ck = _dot(c["ds"], c["qs"], NN)
            cv = _dot(c["p"], c["dos"], NN)
            if gqa:
                sel = lo if gi == 0 else jnp.logical_not(lo)
                ck = jnp.where(sel, ck + pltpu.roll(ck, 64, 1), 0.0)
                cv = jnp.where(sel, cv + pltpu.roll(cv, 64, 1), 0.0)
                kcols = slice(0, 128)
            else:
                kcols = slice(128 * gi, 128 * (gi + 1))
            for r0, nr, key0 in c["dests"]:
                krows = pl.ds(key0, nr)
                if kind == "mem":
                    dkv_ref[krows, kcols] += ck[r0:r0 + nr]
                    dkv_ref[krows, slice(kvw + kcols.start, kvw + kcols.stop)] += cv[r0:r0 + nr]
                else:
                    dk_ref[krows, kcols] += ck[r0:r0 + nr]
                    dv_ref[krows, kcols] += cv[r0:r0 + nr]
            for i, p in enumerate(c["pairs"]):
                dq2t = jnp.where(top, dqt[:, 256 * i:256 * i + 128], dqt[:, 256 * i + 128:256 * i + 256])
                dq_ref[rows, 128 * p:128 * (p + 1)] = dq2t.T.astype(BF16)

        chains = [(j, gi, pairs) for j in range(QB) for gi, pairs in a.groups]
        live = {}
        for t in range(len(chains) + 2):
            if t < len(chains):
                live[t] = first_matmuls(*chains[t])
            if 0 <= t - 1 < len(chains):
                elementwise(live[t - 1])
            if 0 <= t - 2 < len(chains):
                last_matmuls(live.pop(t - 2))
        if dsink_ref is not None:
            ps = jnp.exp(sinkv_ref[...] - lse_ref[...]) * dl_ref[...]
            dsink_ref[...] += jnp.sum(ps, axis=0, keepdims=True)
        if kind == "mem":
            @pl.when(g == T // QR - 1)
            def _():
                gmem_ref[...] = _dot(mem_ref[...], dkv_ref[...].astype(BF16), TN)
        else:
            n_steps = T // QR

            def flush(step):
                rows = pl.ds(pl.multiple_of(step * QR, QR), QR)
                out = []
                for acc, stage, dst, i in ((dk_ref, stage_k, dk_out, 0), (dv_ref, stage_v, dv_out, 1)):
                    stage[...] = acc[rows, :].astype(BF16)
                    out.append(pltpu.make_async_copy(stage, dst.at[rows, :], flush_sem.at[i]))
                return out

            def flushed(step):
                rows = pl.ds(pl.multiple_of(step * QR, QR), QR)
                return [pltpu.make_async_copy(stage, dst.at[rows, :], flush_sem.at[i])
                        for stage, dst, i in ((stage_k, dk_out, 0), (stage_v, dv_out, 1))]

            @pl.when(g >= 2)
            def _():
                for cp in flushed(g - 2):
                    cp.wait()

            @pl.when(g >= 1)
            def _():
                for cp in flush(g - 1):
                    cp.start()

            @pl.when(g == n_steps - 1)
            def _():
                for cp in flushed(g - 1):
                    cp.wait()
                for cp in flush(g):
                    cp.start()
                for cp in flushed(g):
                    cp.wait()

    args = [q, k, v] + ([k, v] if a.ext_prev else []) + [do, lse, dl]
    in_specs = [a.q_spec] + a.kv_specs + [a.row_spec, a.stat_spec, a.stat_spec]
    if sinkv is not None:
        args.append(sinkv)
        in_specs.append(_full((1, 128)))
    if kind == "mem":
        args.append(mem_in)
        in_specs.append(pl.BlockSpec(mem_in.shape, lambda g: (0, 0), pipeline_mode=pl.Buffered(1)))
    out_shape = [_sds((T, qw), BF16)]
    out_specs = [a.row_spec]
    scratch = []
    if kind == "mem":
        out_shape.append(_sds((D_MODEL, 2 * kvw), F32))
        out_specs.append(pl.BlockSpec((D_MODEL, 2 * kvw), lambda g: (0, 0), pipeline_mode=pl.Buffered(1)))
        scratch = [pltpu.VMEM((B_LOC * MEM_LEN, 2 * kvw), F32)]
    else:
        out_shape += [_sds((T, kvw), BF16)] * 2
        out_specs += [pl.BlockSpec(memory_space=pl.ANY)] * 2
        scratch = [pltpu.VMEM((T, kvw), F32)] * 2 + [pltpu.VMEM((QR, kvw), BF16)] * 2 + [pltpu.SemaphoreType.DMA((2,))]
    if sinkv is not None:
        out_shape.append(_sds((1, 128), F32))
        out_specs.append(_full((1, 128)))
    return _Part(body, args, in_specs, out_specs, out_shape, scratch)


def _dot2(v, w_ref):
    hi = v.astype(BF16)
    lo = (v - hi.astype(F32)).astype(BF16)
    return _dot(hi, w_ref[...], NN) + _dot(lo, w_ref[...], NN)


def _middle(oa, o1, l1, o4, l4, o16, l16, oc, z, x, tgt, g_br, ln_g, ln_b, wout, spread4, gather4, gather8):
    tm = 512
    spt = SEQ // tm

    def body(oa_ref, o1_ref, l1_ref, o4_ref, l4_ref, o16_ref, l16_ref, oc_ref, z_ref, x_ref, t_ref,
             g_ref, lg_ref, lb_ref, w_ref, sp4_ref, ga4_ref, ga8_ref,
             du_ref, dz_ref, doa_ref, dla_ref,
             dobn_ref, lsen_ref, dlbn_ref, dob4_ref, lse4_ref, dlb4_ref, dob16_ref, lse16_ref, dlb16_ref,
             doc_ref, dlc_ref, acc_ref, gout_ref, scr):
        i = pl.program_id(0)

        @pl.when(i == 0)
        def _():
            acc_ref[...] = jnp.zeros_like(acc_ref)
            gout_ref[...] = jnp.zeros_like(gout_ref)

        q = tm // 4
        for res in range(16):
            rows = pl.ds((res % 4) * q + res // 4, tm // 16, stride=4)
            for j in range(2):
                scr[6 + j, rows, :] = o16_ref[0, res, :, 128 * j:128 * (j + 1)].astype(F32)
            scr[8, rows, :] = l16_ref[0, res]
        for res in range(4):
            rows, blk = pl.ds(res, q, stride=4), slice(res * q, (res + 1) * q)
            for j in range(2):
                scr[j, rows, :] = o4_ref[0, res, :, 128 * j:128 * (j + 1)].astype(F32)
                scr[3 + j, rows, :] = scr[6 + j, blk, :]
            scr[2, rows, :] = l4_ref[0, res]
            scr[5, rows, :] = scr[8, blk, :]
        inv_d = 1.0 / D_MODEL
        gb, lg, lb = g_ref[...], lg_ref[...], lb_ref[...]

        def rms(o):
            r = lax.rsqrt(jnp.sum(o * o, axis=1, keepdims=True) * (1.0 / o.shape[1]) + RMS_EPS)
            return o * r, r

        def rms_bwd(dn_, n_, r):
            return r * (dn_ - n_ * (jnp.sum(dn_ * n_, axis=1, keepdims=True) * (1.0 / n_.shape[1])))

        def forward(rs):
            o4v = jnp.concatenate([scr[0, rs, :], scr[1, rs, :]], axis=1)
            o16v = jnp.concatenate([scr[3, rs, :], scr[4, rs, :]], axis=1)
            l1v, l4v, l16v = l1_ref[rs, :], scr[2, rs, :], scr[5, rs, :]
            mx = jnp.maximum(jnp.maximum(l1v, l4v), l16v)
            e1, e4, e16 = jnp.exp(l1v - mx), jnp.exp(l4v - mx), jnp.exp(l16v - mx)
            ssum = e1 + e4 + e16
            inv = 1.0 / ssum
            c = dict(rs=rs, lse_b=mx + jnp.log(ssum))
            c["ob"] = (_dot2(e1 * inv, sp4_ref) * o1_ref[rs, :].astype(F32) + _dot2(e4 * inv, sp4_ref) * o4v
                       + _dot2(e16 * inv, sp4_ref) * o16v)
            c["oa"], c["oc"] = oa_ref[rs, :].astype(F32), oc_ref[rs, :].astype(F32)
            na, c["ra"] = rms(c["oa"])
            nb_, c["rb"] = rms(c["ob"])
            nc, c["rc"] = rms(c["oc"])
            c["n"] = jnp.concatenate([na, nb_, nc], axis=1)
            c["zf"] = z_ref[rs, :].astype(F32)
            c["sig"] = 1.0 / (1.0 + jnp.exp(-c["zf"]))
            c["sz"] = c["zf"] * c["sig"]
            c["yb"] = (c["n"] * gb * c["sz"]).astype(BF16)
            c["y2"] = _dot(c["yb"], w_ref[...], NN)
            return c

        def norm(c):
            rs = c["rs"]
            u = ALPHA * x_ref[rs, :] + c.pop("y2")
            mu = jnp.sum(u, axis=1, keepdims=True) * inv_d
            uc = u - mu
            rstd = lax.rsqrt(jnp.sum(uc * uc, axis=1, keepdims=True) * inv_d + LN_EPS)
            xh = uc * rstd
            diff = xh * lg + lb - t_ref[rs, :]
            acc_ref[0:1, :] += jnp.sum(diff * diff, axis=0, keepdims=True) * (0.5 * inv_d)
            dout = diff * inv_d
            acc_ref[2:3, :] += jnp.sum(dout * xh, axis=0, keepdims=True)
            acc_ref[3:4, :] += jnp.sum(dout, axis=0, keepdims=True)
            dxh = dout * lg
            du = rstd * (dxh - jnp.sum(dxh, axis=1, keepdims=True) * inv_d
                         - xh * (jnp.sum(dxh * xh, axis=1, keepdims=True) * inv_d))
            dub = du.astype(BF16)
            du_ref[rs, :] = dub
            c["dy"] = _dot(dub, w_ref[...], NT)
            gout_ref[...] += _dot(c.pop("yb"), dub, TN)

        def backward(c):
            rs, n, dy, zf, sig = c["rs"], c["n"], c["dy"], c["zf"], c["sig"]
            t1 = dy * c["sz"]
            acc_ref[1:2, :] += jnp.sum(t1 * n, axis=0, keepdims=True)
            dn = t1 * gb
            dz_ref[rs, :] = (dy * n * gb * (sig * (1.0 + zf * (1.0 - sig)))).astype(BF16)
            doa = rms_bwd(dn[:, :W_A], n[:, :W_A], c["ra"])
            dob = rms_bwd(dn[:, W_A:W_A + W_B], n[:, W_A:W_A + W_B], c["rb"])
            doc = rms_bwd(dn[:, W_A + W_B:], n[:, W_A + W_B:], c["rc"])
            doa_ref[rs, :] = doa.astype(BF16)
            dla_ref[rs, :] = _dot2(doa * c["oa"], ga8_ref)
            doc_ref[rs, :] = doc.astype(BF16)
            dlc_ref[rs, :] = _dot2(doc * c["oc"], ga4_ref)
            dobn_ref[rs, :] = dob.astype(BF16)
            lsen_ref[rs, :] = c["lse_b"]
            dlbn_ref[rs, :] = _dot2(dob * c["ob"], ga4_ref)
            scr[0, rs, :] = dob[:, :128]
            scr[1, rs, :] = dob[:, 128:]

        halves = [slice(h * (tm // 2), (h + 1) * (tm // 2)) for h in range(2)]
        live = {}
        for t in range(len(halves) + 2):
            if t < len(halves):
                live[t] = forward(halves[t])
            if 0 <= t - 1 < len(halves):
                norm(live[t - 1])
            if 0 <= t - 2 < len(halves):
                backward(live.pop(t - 2))
        for j in range(2):
            sl = slice(128 * j, 128 * (j + 1))
            for res in range(4):
                t = scr[j, pl.ds(res, q, stride=4), :]
                dob4_ref[0, res, :, sl] = t.astype(BF16)
                scr[6 + j, res * q:(res + 1) * q, :] = t
            for res in range(16):
                dob16_ref[0, res, :, sl] = scr[6 + j, pl.ds((res % 4) * q + res // 4, tm // 16, stride=4),
                                               :].astype(BF16)
        for res in range(4):
            rows = pl.ds(res, q, stride=4)
            lse4_ref[0, res] = lsen_ref[rows, :]
            dlb4_ref[0, res] = dlbn_ref[rows, :]
        for res in range(16):
            rows = pl.ds(res // 4, tm // 16, stride=4)
            lse16_ref[0, res] = lse4_ref[0, res % 4, rows, :]
            dlb16_ref[0, res] = dlb4_ref[0, res % 4, rows, :]


    tok = lambda w: pl.BlockSpec((tm, w), lambda i: (i, 0))
    p4 = lambda w: pl.BlockSpec((1, 4, tm // 4, w), lambda i: (i // spt, 0, i % spt, 0))
    p16 = lambda w: pl.BlockSpec((1, 16, tm // 16, w), lambda i: (i // spt, 0, i % spt, 0))
    s4 = lambda w, dt: _sds((B_LOC, 4, SEQ // 4, w), dt)
    s16 = lambda w, dt: _sds((B_LOC, 16, SEQ // 16, w), dt)
    row = _full((1, D_MODEL))
    return pl.pallas_call(
        body, name="middle", grid=(T // tm,),
        in_specs=[tok(W_A), tok(W_B), tok(128), p4(W_B), p4(128), p16(W_B), p16(128), tok(W_C), tok(D_MIX),
                  tok(D_MODEL), tok(D_MODEL), row, row, row, _full((D_MIX, D_MODEL)),
                  _full((128, W_B)), _full((W_B, 128)), _full((W_A, 128))],
        out_specs=(tok(D_MODEL), tok(D_MIX), tok(W_A), tok(128),
                   tok(W_B), tok(128), tok(128), p4(W_B), p4(128), p4(128), p16(W_B), p16(128), p16(128),
                   tok(W_C), tok(128), _full((8, D_MODEL)), _full((D_MIX, D_MODEL))),
        out_shape=(_sds((T, D_MODEL), BF16), _sds((T, D_MIX), BF16),
                   _sds((T, W_A), BF16), _sds((T, 128), F32),
                   _sds((T, W_B), BF16), _sds((T, 128), F32), _sds((T, 128), F32),
                   s4(W_B, BF16), s4(128, F32), s4(128, F32), s16(W_B, BF16), s16(128, F32), s16(128, F32),
                   _sds((T, W_C), BF16), _sds((T, 128), F32), _sds((8, D_MODEL), F32),
                   _sds((D_MIX, D_MODEL), F32)),
        scratch_shapes=[pltpu.VMEM((9, tm, 128), F32)],
        compiler_params=_cp(("arbitrary",), vmem_mb=56),
    )(*_pin(oa, o1, l1, o4, l4, o16, l16, oc, z, x, tgt, g_br, ln_g, ln_b, wout, spread4, gather4, gather8))


class _ReduceScatter:
    def __init__(self, shapes):
        self.shapes = shapes

    def scratch_shapes(self):
        out = []
        for n, w in self.shapes:
            h, p = n // 2, n // 4
            out += [pltpu.VMEM((4, h, w), F32), pltpu.VMEM((4, h, w), F32), pltpu.VMEM((6, p, w), BF16),
                    pltpu.VMEM((6, p, w), BF16), pltpu.VMEM((2, p, w), F32), pltpu.VMEM((h, w), F32)]
        na = len(self.shapes)
        dma = pltpu.SemaphoreType.DMA
        return out + [dma((na, 4)), dma((na, 4)), dma((na, 4)), dma((na, 6)), dma((na, 6)), dma((na,)), dma((na,)),
                      dma((na,))]

    def bind(self, g_refs, r_refs, scratch):
        na = len(self.shapes)
        bufs = [scratch[6 * a:6 * a + 6] for a in range(na)]
        mine, sib, stage, land, keep, tot = (tuple(b[i] for b in bufs) for i in range(6))
        loc_sem, s1_send, s1_recv, s2_send, s2_recv, s3_send, s3_recv, st_sem = scratch[6 * na:6 * na + 8]
        x, y, c = lax.axis_index("x"), lax.axis_index("y"), lax.axis_index("c")
        me, sibling = (x, y, c), (x, y, 1 - c)
        xn, yn, dg = (1 - x, y), (x, 1 - y), (1 - x, 1 - y)
        idx = lambda chip: 2 * chip[0] + chip[1]
        my_chip = idx((x, y))
        order = [idx(xn), idx(dg), idx(yn), my_chip]

        def rows(a, k, half):
            n = self.shapes[a][0]
            return pl.ds(pl.multiple_of(k * n + half * (n // 2), 8), n // 2)

        def piece(a, q):
            p = self.shapes[a][0] // 4
            return slice(q * p, (q + 1) * p)

        def load(a, k):
            return pltpu.make_async_copy(g_refs[a].at[rows(a, k, c), :], mine[a].at[k], loc_sem.at[a, k])

        def s1(a, k, half):
            return pltpu.make_async_remote_copy(
                src_ref=g_refs[a].at[rows(a, k, half), :], dst_ref=sib[a].at[k],
                send_sem=s1_send.at[a, k], recv_sem=s1_recv.at[a, k], device_id=sibling, device_id_type=MESH)

        def s2(a, i, to):
            return pltpu.make_async_remote_copy(
                src_ref=stage[a].at[i], dst_ref=land[a].at[i], send_sem=s2_send.at[a, i], recv_sem=s2_recv.at[a, i],
                device_id=to, device_id_type=MESH)

        via = {0: xn, 1: xn, 2: yn, 3: yn, 4: yn, 5: xn}

        def s3(a, half, to):
            return pltpu.make_async_remote_copy(
                src_ref=tot[a], dst_ref=r_refs[a].at[rows(a, 0, half), :], send_sem=s3_send.at[a],
                recv_sem=s3_recv.at[a], device_id=to, device_id_type=MESH)

        def store(a):
            return pltpu.make_async_copy(tot[a], r_refs[a].at[rows(a, 0, c), :], st_sem.at[a])

        def start():
            for k in order:
                for a in range(na):
                    load(a, k).start()
                    s1(a, k, 1 - c).start()

        def chip_sum(a, k):
            load(a, k).wait()
            s1(a, k, c).wait_recv()
            return mine[a][k] + sib[a][k]

        def exchange():
            for a in range(na):
                P, Q = piece(a, 0), piece(a, 1)
                s_xn = chip_sum(a, idx(xn))
                stage[a][0] = s_xn[P].astype(BF16)
                keep[a][1] = s_xn[Q]
                s_dg = chip_sum(a, idx(dg))
                stage[a][1] = s_dg[P].astype(BF16)
                s2(a, 0, (*xn, c)).start()
                s2(a, 1, (*xn, c)).start()
                stage[a][3] = s_dg[Q].astype(BF16)
                s_yn = chip_sum(a, idx(yn))
                stage[a][2] = s_yn[Q].astype(BF16)
                keep[a][0] = s_yn[P]
                s2(a, 2, (*yn, c)).start()
                s2(a, 3, (*yn, c)).start()
                tot[a][...] = chip_sum(a, my_chip)

        def relay():
            for a in range(na):
                P, Q = piece(a, 0), piece(a, 1)
                s2(a, 1, me).wait_recv()
                stage[a][4] = (keep[a][0] + land[a][1].astype(F32)).astype(BF16)
                s2(a, 4, (*yn, c)).start()
                s2(a, 3, me).wait_recv()
                stage[a][5] = (keep[a][1] + land[a][3].astype(F32)).astype(BF16)
                s2(a, 5, (*xn, c)).start()
                s2(a, 0, me).wait_recv()
                tot[a][P, :] += land[a][0].astype(F32)
                s2(a, 2, me).wait_recv()
                tot[a][Q, :] += land[a][2].astype(F32)

        def finish():
            for a in range(na):
                P, Q = piece(a, 0), piece(a, 1)
                s2(a, 4, me).wait_recv()
                tot[a][P, :] += land[a][4].astype(F32)
                s2(a, 5, me).wait_recv()
                tot[a][Q, :] += land[a][5].astype(F32)
                s3(a, c, sibling).start()
                store(a).start()

        def drain():
            for a in range(na):
                s3(a, 1 - c, me).wait_recv()
                store(a).wait()
            for a in range(na):
                for k in order:
                    s1(a, k, 1 - c).wait_send()
                for i in range(6):
                    s2(a, i, (*via[i], c)).wait_send()
                s3(a, c, sibling).wait_send()

        return start, exchange, relay, finish, drain

    def part(self, grads, steps):
        def body(*refs):
            na = len(self.shapes)
            i = pl.program_id(0)
            for step, phase in zip(steps, self.bind(refs[:na], refs[na:2 * na], refs[2 * na:])):
                pl.when(i == step)(phase)

        hbm = pl.BlockSpec(memory_space=pl.ANY)
        return _Part(body, list(grads), [hbm] * len(grads), [hbm] * len(grads),
                     [_sds((n, w), F32) for n, w in self.shapes], self.scratch_shapes())


def _dh_dx(dqa, dka, dva, dqn, dkn, dvn, dq4, dk4, dv4, dq16, dk16, dv16, dqc, dz, du, xb, cos, sa, sb, winT):
    tm = 512
    spt = SEQ // tm

    def body(dqa_ref, dka_ref, dva_ref, dqn_ref, dkn_ref, dvn_ref, dq4_ref, dk4_ref, dv4_ref,
             dq16_ref, dk16_ref, dv16_ref, dqc_ref, dz_ref, du_ref, xb_ref, cos_ref, sa_ref, sb_ref, w_ref,
             gx_ref, db_ref, gin_ref, dh_ref, scr):
        i = pl.program_id(0)

        @pl.when(i == 0)
        def _():
            db_ref[...] = jnp.zeros_like(db_ref)
            gin_ref[...] = jnp.zeros_like(gin_ref)

        cos_t, sa_t, sb_t = cos_ref[...], sa_ref[...], sb_ref[...]

        def rope_t(t):
            return _rope(t, cos_t, sa_t, sb_t, -1)

        def put(r0, val):
            n = val.shape[1]
            dh_ref[:, r0:r0 + n] = val.astype(BF16)
            db_ref[:, r0:r0 + n] += jnp.sum(val, axis=0, keepdims=True)

        put(O_QA, rope_t(dqa_ref[...].astype(F32)) * QK_SCALE)
        put(O_KA, rope_t(dka_ref[...].astype(F32)))
        put(O_VA, dva_ref[...].astype(F32))
        put(O_QC, dqc_ref[...].astype(F32) * QK_SCALE)
        put(O_Z, dz_ref[...].astype(F32))
        for k, (n_ref, r4, r16) in enumerate(((dqn_ref, dq4_ref, dq16_ref), (dkn_ref, dk4_ref, dk16_ref),
                                               (dvn_ref, dv4_ref, dv16_ref))):
            for j in range(2):
                sl = slice(128 * j, 128 * (j + 1))
                a, q = 2 * k + j, tm // 4
                scr[a] = n_ref[:, sl].astype(F32)
                for res in range(16):
                    scr[6 + a, pl.ds((res % 4) * q + res // 4, tm // 16, stride=4), :] = r16[0, res, :, sl].astype(F32)
                for res in range(4):
                    scr[a, pl.ds(res, q, stride=4), :] += (scr[6 + a, res * q:(res + 1) * q, :]
                                                           + r4[0, res, :, sl].astype(F32))
        cat = lambda a: jnp.concatenate([scr[a], scr[a + 1]], axis=1)
        put(O_QB, rope_t(cat(0)) * QK_SCALE)
        put(O_KB, rope_t(cat(2)))
        put(O_VB, cat(4))
        gx_ref[...] = _dot(dh_ref[...], w_ref[...], NN) + ALPHA * du_ref[...].astype(F32)
        gin_ref[...] += _dot(dh_ref[...], xb_ref[...], TN)

    tok = lambda w: pl.BlockSpec((tm, w), lambda i: (i, 0))
    tab = pl.BlockSpec((tm, 128), lambda i: (i % spt, 0))
    p4 = pl.BlockSpec((1, 4, tm // 4, W_B), lambda i: (i // spt, 0, i % spt, 0))
    p16 = pl.BlockSpec((1, 16, tm // 16, W_B), lambda i: (i // spt, 0, i % spt, 0))
    once = lambda shape: pl.BlockSpec(shape, lambda i: (0, 0), pipeline_mode=pl.Buffered(1))
    return pl.pallas_call(
        body, name="dh_dx", grid=(T // tm,),
        in_specs=[tok(W_A), tok(W_KV_A), tok(W_KV_A), tok(W_B), tok(W_B), tok(W_B), p4, p4, p4, p16, p16, p16,
                  tok(W_C), tok(D_MIX), tok(D_MODEL), tok(D_MODEL), tab, tab, tab, once((D_IN, D_MODEL))],
        out_specs=(tok(D_MODEL), _full((1, D_IN)), once((D_IN, D_MODEL))),
        out_shape=(_sds((T, D_MODEL), F32), _sds((1, D_IN), F32), _sds((D_IN, D_MODEL), F32)),
        scratch_shapes=[pltpu.VMEM((tm, D_IN), BF16), pltpu.VMEM((12, tm, 128), F32)],
        compiler_params=_cp(("arbitrary",), vmem_mb=56),
    )(*_pin(dqa, dka, dva, dqn, dkn, dvn, dq4, dk4, dv4, dq16, dk16, dv16, dqc, dz, du, xb, cos, sa, sb, winT))


def _reduce_grads(g_in, acc, dbin, dsink):
    rs = _ReduceScatter([(SH_IN, D_MODEL)])

    def body(g_ref, acc_ref, dbin_ref, dsink_ref, r_ref, sv_ref, sv_mine, sv_all, sv_send, sv_recv, *rs_scratch):
        x, y, c = lax.axis_index("x"), lax.axis_index("y"), lax.axis_index("c")
        chips = [(1 - x, y), (x, 1 - y), (1 - x, 1 - y)]
        start, exchange, relay, finish, drain = rs.bind((g_ref,), (r_ref,), rs_scratch)
        start()

        sv_mine[...] = jnp.zeros_like(sv_mine)
        sv_mine[0:4, 0:D_MODEL] = acc_ref[0:4, :]
        sv_mine[4:5, 0:D_IN] = dbin_ref[...]
        sv_mine[5:6, 0:128] = dsink_ref[...]
        my_dev = 4 * x + 2 * y + c
        others = [(x, y, 1 - c)] + [(*chip, cc) for chip in chips for cc in (c, 1 - c)]

        def sv_copy(j, to):
            return pltpu.make_async_remote_copy(
                src_ref=sv_mine, dst_ref=sv_all.at[my_dev], send_sem=sv_send.at[j], recv_sem=sv_recv.at[j],
                device_id=to, device_id_type=MESH)

        sv_sends = [sv_copy(j, to) for j, to in enumerate(others)]
        for cp in sv_sends:
            cp.start()
        exchange()
        relay()
        finish()
        sv_all[my_dev] = sv_mine[...]
        for j in range(7):
            sv_copy(j, (x, y, c)).wait_recv()
        tot = sv_all[0]
        for d in range(1, 8):
            tot = tot + sv_all[d]
        sv_ref[...] = tot
        drain()
        for cp in sv_sends:
            cp.wait_send()

    vm = pl.BlockSpec(memory_space=pltpu.VMEM)
    hbm = pl.BlockSpec(memory_space=pl.ANY)
    return pl.pallas_call(
        body, name="reduce_grads",
        out_shape=(_sds((SH_IN, D_MODEL), F32), _vm_sds((8, SV_W), F32)),
        in_specs=[hbm, vm, vm, vm], out_specs=(hbm, vm),
        scratch_shapes=[pltpu.VMEM((8, SV_W), F32), pltpu.VMEM((8, 8, SV_W), F32),
                        pltpu.SemaphoreType.DMA((7,)), pltpu.SemaphoreType.DMA((7,))] + rs.scratch_shapes(),
        compiler_params=_cp(vmem_mb=40),
    )(pltpu.with_memory_space_constraint(g_in, pltpu.HBM), acc, dbin, dsink)


def _adamw_update(w, g, m, v):
    nm = ADAM_B1 * m + (1.0 - ADAM_B1) * g
    nv = ADAM_B2 * v + (1.0 - ADAM_B2) * (g * g)
    m_hat = nm / (1.0 - ADAM_B1 ** ADAM_STEP)
    v_hat = nv / (1.0 - ADAM_B2 ** ADAM_STEP)
    return -ADAM_LR * (m_hat / (jnp.sqrt(v_hat) + ADAM_EPS) + ADAM_WD * w), nm, nv


SMALL = ((4, D_IN, 1.0), (5, 8, -1.0), (1, D_MIX, 1.0), (2, D_MODEL, 1.0), (3, D_MODEL, 1.0))


def _adamw_all(items, sv, ws, ms, vs, n_steps=8):
    nb, ns = 4 * len(items), len(SMALL)

    def body(*refs):
        ins, sv_ref, small_in = refs[:nb], refs[nb], refs[nb + 1:nb + 1 + 3 * ns]
        outs = refs[nb + 1 + 3 * ns:]
        big_out, loss_ref, small_out = outs[:nb], outs[nb], outs[nb + 1:]
        for p in range(len(items)):
            w_ref, g_ref, m_ref, v_ref = ins[4 * p:4 * p + 4]
            gv = g_ref[...]
            big_out[4 * p][...] = gv
            big_out[4 * p + 1][...], big_out[4 * p + 2][...], big_out[4 * p + 3][...] = _adamw_update(
                w_ref[...], gv, m_ref[...], v_ref[...])

        @pl.when(pl.program_id(0) == 0)
        def _():
            loss_ref[...] = jnp.sum(sv_ref[0:1, 0:D_MODEL], axis=1, keepdims=True)
            for p, (row, width, sign) in enumerate(SMALL):
                gv = sign * sv_ref[row:row + 1, 0:width]
                small_out[4 * p][...] = gv
                small_out[4 * p + 1][...], small_out[4 * p + 2][...], small_out[4 * p + 3][...] = _adamw_update(
                    small_in[p][...], gv, small_in[ns + p][...], small_in[2 * ns + p][...])

    specs, shapes, args = [], [], []
    for w, g, m, v in items:
        rows, width = w.shape
        specs += [pl.BlockSpec((rows // n_steps, width), lambda i: (i, 0))] * 4
        shapes += [_sds((rows, width), F32)] * 4
        args += [w, g, m, v]
    small_args = [*ws, *ms, *vs]
    whole = lambda a: _full(a.shape)
    res = pl.pallas_call(
        body, name="adamw", grid=(n_steps,),
        in_specs=specs + [whole(sv)] + [whole(a) for a in small_args],
        out_specs=tuple(specs + [_full((1, 1))] + [whole(w) for w in ws for _ in range(4)]),
        out_shape=tuple(shapes + [_sds((1, 1), F32)] + [_sds(w.shape, F32) for w in ws for _ in range(4)]),
        compiler_params=_cp(("arbitrary",), vmem_mb=40),
    )(*_pin(*args, sv, *small_args))
    big = [tuple(res[4 * p:4 * p + 4]) for p in range(len(items))]
    return big, res[nb], [tuple(res[nb + 1 + 4 * p:nb + 5 + 4 * p]) for p in range(ns)]


def _rope_tables():
    pos = np.arange(SEQ, dtype=np.float32)
    inv = (np.float32(ROPE_THETA) ** (-np.arange(0, 64, 2, dtype=np.float32) / np.float32(64))).astype(np.float32)
    ang = np.tile(pos[:, None] * inv[None, :], (1, 4))
    cos, sin = np.cos(ang).astype(np.float32), np.sin(ang).astype(np.float32)
    low = (np.arange(128) % 64) < 32
    zero = np.float32(0.0)
    return jnp.asarray(cos), jnp.asarray(np.where(low, -sin, zero)), jnp.asarray(np.where(low, zero, sin))


def _local_step(x2, mem2, tgt2, winT, wout, wmem, b_in, sinks, g_branch, ln_gain, ln_bias):
    cos, sa, sb = _rope_tables()
    sinkv = jnp.pad(sinks, ((0, 0), (0, 120)))
    head_of_lane = np.arange(512)[:, None] // 64
    gather8 = jnp.asarray(head_of_lane == np.arange(128)[None, :], BF16)
    gather4 = jnp.asarray(head_of_lane[:W_B] == np.arange(128)[None, :], BF16)
    spread4 = jnp.asarray((head_of_lane[:W_B] == np.arange(128)[None, :]).T, BF16)

    xb, qa, ka, va, bn, b4, b16, qc, z, wout, wmem = _in_proj(x2, winT, b_in, cos, sa, sb, wout, wmem)
    memb, mkv = _mem_kv(mem2, wmem)
    b4f, b16f = b4.reshape(T, 768), b16.reshape(T, 768)

    swa = dict(kind="band", nb=SEQ // BLK, max_dist=BLK - 1, gqa=True)
    dil = (dict(kind="band", nb=SEQ // BLK), dict(kind="band", nb=SEQ // 4 // BLK), dict(kind="band", nb=1))
    (oa, lse_a), (o1, l1), (o4, l4), (o16, l16), (oc, lse_c) = _run_parts("attn_fwd", [
        _attn_fwd(qa, 0, W_A, ka, 0, va, 0, W_KV_A, sinks=sinks, **swa),
        _attn_fwd(bn, 0, W_B, bn, 1, bn, 2, W_B, **dil[0]),
        _attn_fwd(b4f, 0, W_B, b4f, 1, b4f, 2, W_B, **dil[1]),
        _attn_fwd(b16f, 0, W_B, b16f, 1, b16f, 2, W_B, **dil[2]),
        _attn_fwd(qc, 0, W_C, mkv, 0, mkv, 1, W_C, kind="mem")], "parallel", 48)

    s4 = lambda w: (B_LOC, 4, SEQ // 4, w)
    s16 = lambda w: (B_LOC, 16, SEQ // 16, w)
    (du, dz, doa, dla, dobn, lsen, dlbn, dob4, lse4, dlb4, dob16, lse16, dlb16, doc, dlc, acc, g_out) = _middle(
        oa, o1, l1, o4.reshape(s4(W_B)), l4.reshape(s4(128)), o16.reshape(s16(W_B)), l16.reshape(s16(128)), oc, z,
        x2, tgt2, g_branch, ln_gain, ln_bias, wout, spread4, gather4, gather8)

    flat = lambda a: a.reshape(T, a.shape[-1])
    (dqa, dka, dva, dsink), (dqc, g_mem) = _run_parts("attn_bwd_a", [
        _attn_bwd(qa, 0, W_A, ka, 0, va, 0, W_KV_A, doa, lse_a, dla, sinkv=sinkv, **swa),
        _attn_bwd(qc, 0, W_C, mkv, 0, mkv, 1, W_C, doc, lse_c, dlc, kind="mem", mem_in=memb)], "arbitrary", 48)
    last = T // QR - 1
    (r_out, r_mem), (dqn, dkn, dvn), (dq4, dk4, dv4), (dq16, dk16, dv16) = _run_parts("attn_bwd_b", [
        _ReduceScatter([(SH_OUT, D_MODEL), (SH_MEM, 2 * W_C)]).part((g_out, g_mem), (0, 1, 2, last, last)),
        _attn_bwd(bn, 0, W_B, bn, 1, bn, 2, W_B, dobn, lsen, dlbn, **dil[0]),
        _attn_bwd(b4f, 0, W_B, b4f, 1, b4f, 2, W_B, flat(dob4), flat(lse4), flat(dlb4), **dil[1]),
        _attn_bwd(b16f, 0, W_B, b16f, 1, b16f, 2, W_B, flat(dob16), flat(lse16), flat(dlb16), **dil[2])],
        "arbitrary", 62)

    r4 = lambda a: a.reshape(s4(W_B))
    r16 = lambda a: a.reshape(s16(W_B))
    gx, dbin, g_in = _dh_dx(dqa, dka, dva, dqn, dkn, dvn, r4(dq4), r4(dk4), r4(dv4), r16(dq16), r16(dk16),
                            r16(dv16), dqc, dz, du, xb, cos, sa, sb, winT)
    return gx, g_in, r_out, r_mem, acc, dbin, dsink


def kernel(x, mem, w_in, b_in, w_mem, attn_sinks, g_branch, w_out, ln_gain, ln_bias, loss_target, m_w_in, m_b_in, m_w_mem, m_attn_sinks, m_g_branch, m_w_out, m_ln_gain, m_ln_bias, v_w_in, v_b_in, v_w_mem, v_attn_sinks, v_g_branch, v_w_out, v_ln_gain, v_ln_bias):
    winT, wout, wmem = _gather_weights(w_in[0].T, w_out[0], w_mem[0])
    gx, g_in, r_out, r_mem, acc, dbin, dsink = _local_step(
        x.reshape(T, D_MODEL), mem.reshape(B_LOC * MEM_LEN, D_MODEL), loss_target.reshape(T, D_MODEL),
        winT, wout, wmem, b_in, attn_sinks, g_branch, ln_gain, ln_bias)
    r_in, sv = _reduce_grads(g_in, acc, dbin, dsink)

    small = ["b_in", "attn_sinks", "g_branch", "ln_gain", "ln_bias"]
    big, loss, steps = _adamw_all(
        [(w_in[0].T, r_in, m_w_in[0].T, v_w_in[0].T), (w_out[0], r_out, m_w_out[0], v_w_out[0]),
         (w_mem[0], r_mem, m_w_mem[0], v_w_mem[0])],
        sv, [b_in, attn_sinks, g_branch, ln_gain, ln_bias], [m_b_in, m_attn_sinks, m_g_branch, m_ln_gain, m_ln_bias],
        [v_b_in, v_attn_sinks, v_g_branch, v_ln_gain, v_ln_bias])
    out = dict(zip(small, steps))
    out["w_in"] = tuple(a.T[None] for a in big[0])
    out["w_out"], out["w_mem"] = (tuple(a[None] for a in st) for st in big[1:])
    names = ["w_in", "b_in", "w_mem", "attn_sinks", "g_branch", "w_out", "ln_gain", "ln_bias"]
    return (loss.reshape(()), gx.reshape(B_LOC, SEQ, D_MODEL), *[out[n][k] for k in range(4) for n in names])
```

```python
import jax
import jax.numpy as jnp
import numpy as np
from jax import lax
from jax.experimental import pallas as pl
from jax.experimental.pallas import tpu as pltpu

F32, BF16 = jnp.float32, jnp.bfloat16

D_MODEL = 1024
SEQ = 2048
B_LOC = 2
T = B_LOC * SEQ
BLK = 128
MEM_LEN = 256
W_A, W_KV_A, W_B, W_C, D_MIX = 512, 128, 256, 256, 1024
D_IN = 2816
O_QA, O_KA, O_VA, O_QB, O_KB, O_VB, O_QC, O_Z = 0, 512, 640, 768, 1024, 1280, 1536, 1792
ROPE_THETA = 10000.0
LN_EPS = 1e-5
RMS_EPS = 1e-6
ALPHA = 2.0 ** 0.25
QK_SCALE = 0.125
N_CHIP = 4
SH_IN, SH_OUT, SH_MEM = D_IN // N_CHIP, D_MIX // N_CHIP, D_MODEL // N_CHIP
NEG = -1e30
ADAM_LR, ADAM_B1, ADAM_B2, ADAM_EPS, ADAM_WD, ADAM_STEP = 0.001, 0.9, 0.999, 1e-08, 0.01, 10
SV_W = 3072
MESH = pl.DeviceIdType.MESH

NN = ((1,), (0,))
NT = ((1,), (1,))
TN = ((0,), (0,))


def _dot(a, b, dims):
    return lax.dot_general(a, b, (dims, ((), ())), preferred_element_type=F32)


def _cp(sem=None, vmem_mb=None):
    kw = {}
    if sem is not None:
        kw["dimension_semantics"] = sem
    if vmem_mb is not None:
        kw["vmem_limit_bytes"] = vmem_mb * 1024 * 1024
    return pltpu.CompilerParams(**kw)


def _sds(shape, dtype):
    return pltpu.HBM(shape, dtype)


def _vm_sds(shape, dtype):
    return jax.ShapeDtypeStruct(shape, dtype)


def _pin(*args):
    return [pltpu.with_memory_space_constraint(a, pltpu.HBM) for a in args]


def _full(shape):
    n = len(shape)
    return pl.BlockSpec(shape, lambda *_: (0,) * n)


def _shard_rows(ref, n, chip, half):
    start = pl.multiple_of((2 * chip[0] + chip[1]) * n + half * (n // 2), 16)
    return ref.at[pl.ds(start, n // 2), :]


def _gather_weights(win_sh, wout_sh, wmem_sh):
    half, piece = SH_IN // 2, SH_IN // 4
    shards = ((SH_IN, D_MODEL), (SH_OUT, D_MODEL), (SH_MEM, 2 * W_C))

    def body(a_ref, b_ref, c_ref, oa_ref, ob_ref, oc_ref, raw_a, raw_b, raw_c, own_a, own_b, own_c,
             load_sem, store_sem, ici_send, ici_recv, d2d_send, d2d_recv):
        x, y, c = lax.axis_index("x"), lax.axis_index("y"), lax.axis_index("c")
        me, sibling = (x, y, c), (x, y, 1 - c)
        xn, yn, dg = (1 - x, y), (x, 1 - y), (1 - x, 1 - y)
        srcs, raws = (a_ref, b_ref, c_ref), (raw_a, raw_b, raw_c)
        owns, outs = (own_a, own_b, own_c), (oa_ref, ob_ref, oc_ref)
        loads = [pltpu.make_async_copy(srcs[a], raws[a], load_sem.at[a]) for a in range(3)]
        for cp in loads:
            cp.start()

        def rows(chip, hf, q):
            start = pl.multiple_of((2 * chip[0] + chip[1]) * SH_IN + hf * half + q * piece, 16)
            return oa_ref.at[pl.ds(start, piece), :]

        def copy(sems, k, chip, hf, q, to, src=None):
            blk = rows(chip, hf, q)
            return pltpu.make_async_remote_copy(
                src_ref=blk if src is None else src, dst_ref=blk, send_sem=sems[0].at[k], recv_sem=sems[1].at[k],
                device_id=to, device_id_type=MESH)

        def my_piece(q):
            return own_a.at[pl.ds(pl.multiple_of(c * half + q * piece, 16), piece), :]

        ici, d2d = (ici_send, ici_recv), (d2d_send, d2d_recv)
        stores, direct = [], []
        for a, (n, _) in enumerate(shards):
            loads[a].wait()
            owns[a][...] = raws[a][...].astype(BF16)
            mine = pl.ds(pl.multiple_of((2 * x + y) * n, 16), n)
            stores.append(pltpu.make_async_copy(owns[a], outs[a].at[mine, :], store_sem.at[a]))
            stores[-1].start()
            if a == 0:
                direct = [copy(ici, 0, (x, y), c, 0, (*xn, c), my_piece(0)),
                          copy(ici, 1, (x, y), c, 1, (*xn, c), my_piece(1)),
                          copy(ici, 3, (x, y), c, 0, (*yn, c), my_piece(0)),
                          copy(ici, 4, (x, y), c, 1, (*yn, c), my_piece(1))]
                for cp in direct:
                    cp.start()
        arrivals = [(0, xn, 0), (1, xn, 1), (3, yn, 0), (4, yn, 1), (2, dg, 1), (5, dg, 0)]
        passed = []
        for k, chip, q in arrivals:
            copy(ici, k, chip, c, q, me).wait_recv()
            if k == 0:
                passed.append(copy(ici, 5, xn, c, 0, (*yn, c)))
                passed[-1].start()
            if k == 4:
                passed.append(copy(ici, 2, yn, c, 1, (*xn, c)))
                passed[-1].start()
            passed.append(copy(d2d, k, chip, c, q, sibling))
            passed[-1].start()
        for k, chip, q in arrivals:
            copy(d2d, k, chip, 1 - c, q, me).wait_recv()
        for cp in direct + passed:
            cp.wait_send()
        for cp in stores:
            cp.wait()

    hbm = pl.BlockSpec(memory_space=pl.ANY)
    return pl.pallas_call(
        body, name="gather_weights",
        out_shape=(_sds((D_IN, D_MODEL), BF16), _sds((D_MIX, D_MODEL), BF16), _sds((D_MODEL, 2 * W_C), BF16)),
        in_specs=[hbm, hbm, hbm], out_specs=(hbm, hbm, hbm),
        scratch_shapes=([pltpu.VMEM(sh, F32) for sh in shards] + [pltpu.VMEM(sh, BF16) for sh in shards]
                        + [pltpu.SemaphoreType.DMA((3,))] * 2 + [pltpu.SemaphoreType.DMA((6,))] * 4),
        compiler_params=_cp(vmem_mb=40),
    )(*_pin(win_sh, wout_sh, wmem_sh))


def _rope(t, cos, sa, sb, sign):
    w = t.shape[1]
    reps = w // 128
    c, a, b = (jnp.tile(v, (1, reps)) if reps > 1 else v for v in (cos, sa, sb))
    rot = pltpu.roll(t, w - 32, 1) * a + pltpu.roll(t, 32, 1) * b
    return t * c + rot if sign > 0 else t * c - rot


def _in_proj(x, winT, b_in, cos, sa, sb, wout_own, wmem_own):
    tm = 512
    spt = SEQ // tm
    n_steps = T // tm
    forward_step = n_steps // 2

    def body(x_ref, w_ref, b_ref, cos_ref, sa_ref, sb_ref, wo_in, wm_in,
             xb_ref, qa_ref, ka_ref, va_ref, bn_ref, b4_ref, b16_ref, qc_ref, z_ref, wo_ref, wm_ref,
             scr, ici_send, ici_recv, d2d_send, d2d_recv):
        i = pl.program_id(0)
        mx, my, mc = lax.axis_index("x"), lax.axis_index("y"), lax.axis_index("c")
        chips = [(1 - mx, my), (mx, 1 - my), (1 - mx, 1 - my)]
        full = ((wo_ref, SH_OUT), (wm_ref, SH_MEM))

        def copy(sems, a, j, chip_of_block, half, to):
            blk = _shard_rows(full[a][0], full[a][1], chip_of_block, half)
            return pltpu.make_async_remote_copy(
                src_ref=blk, dst_ref=blk, send_sem=sems[0].at[a, j], recv_sem=sems[1].at[a, j],
                device_id=to, device_id_type=MESH)

        ici, d2d = (ici_send, ici_recv), (d2d_send, d2d_recv)
        pairs = [(a, j, chip) for j, chip in enumerate(chips) for a in range(2)]

        @pl.when(i == 0)
        def _():
            for a, j, chip in pairs:
                copy(ici, a, j, (mx, my), mc, (*chip, mc)).start()

        @pl.when(i == forward_step)
        def _():
            for a, j, chip in pairs:
                copy(ici, a, j, chip, mc, (mx, my, mc)).wait_recv()
                copy(d2d, a, j, chip, mc, (mx, my, 1 - mc)).start()

        @pl.when(i == n_steps - 1)
        def _():
            for a, j, chip in pairs:
                copy(d2d, a, j, chip, 1 - mc, (mx, my, mc)).wait_recv()
            for a, j, chip in pairs:
                copy(ici, a, j, (mx, my), mc, (*chip, mc)).wait_send()
                copy(d2d, a, j, chip, mc, (mx, my, 1 - mc)).wait_send()

        xb = x_ref[...].astype(BF16)
        xb_ref[...] = xb
        cos_t, sa_t, sb_t = cos_ref[...], sa_ref[...], sb_ref[...]

        def proj(r0, n):
            return _dot(xb, w_ref[r0:r0 + n, :], NT) + b_ref[:, r0:r0 + n]

        def rope(t):
            return _rope(t, cos_t, sa_t, sb_t, +1)

        parts = (rope(proj(O_QB, W_B)) * QK_SCALE, rope(proj(O_KB, W_B)), proj(O_VB, W_B))
        for k, part in enumerate(parts):
            bn_ref[:, 256 * k:256 * (k + 1)] = part.astype(BF16)
            scr[2 * k] = part[:, :128]
            scr[2 * k + 1] = part[:, 128:]
        for j in range(6):
            lanes = slice(128 * j, 128 * (j + 1))
            for res in range(4):
                t = scr[j, pl.ds(res, tm // 4, stride=4), :]
                b4_ref[0, res, :, lanes] = t.astype(BF16)
                scr[6 + j, res * (tm // 4):(res + 1) * (tm // 4), :] = t
            for res in range(16):
                b16_ref[0, res, :, lanes] = scr[6 + j, pl.ds((res % 4) * (tm // 4) + res // 4, tm // 16, stride=4),
                                                :].astype(BF16)
        qa_ref[...] = (rope(proj(O_QA, W_A)) * QK_SCALE).astype(BF16)
        assert O_VA == O_KA + W_KV_A
        kv = proj(O_KA, 2 * W_KV_A)
        ka_ref[...] = rope(kv[:, :W_KV_A]).astype(BF16)
        va_ref[...] = kv[:, W_KV_A:].astype(BF16)
        qc_ref[...] = (proj(O_QC, W_C) * QK_SCALE).astype(BF16)
        z_ref[...] = proj(O_Z, D_MIX).astype(BF16)

    tok = lambda w: pl.BlockSpec((tm, w), lambda i: (i, 0))
    tab = pl.BlockSpec((tm, 128), lambda i: (i % spt, 0))
    hbm = pl.BlockSpec(memory_space=pl.ANY)
    return pl.pallas_call(
        body, name="in_proj", grid=(n_steps,),
        in_specs=[tok(D_MODEL), _full((D_IN, D_MODEL)), _full((1, D_IN)), tab, tab, tab, hbm, hbm],
        out_specs=(tok(D_MODEL), tok(W_A), tok(W_KV_A), tok(W_KV_A), tok(768),
                   pl.BlockSpec((1, 4, tm // 4, 768), lambda i: (i // spt, 0, i % spt, 0)),
                   pl.BlockSpec((1, 16, tm // 16, 768), lambda i: (i // spt, 0, i % spt, 0)),
                   tok(W_C), tok(D_MIX), hbm, hbm),
        out_shape=(_sds((T, D_MODEL), BF16), _sds((T, W_A), BF16), _sds((T, W_KV_A), BF16), _sds((T, W_KV_A), BF16),
                   _sds((T, 768), BF16), _sds((B_LOC, 4, SEQ // 4, 768), BF16), _sds((B_LOC, 16, SEQ // 16, 768), BF16),
                   _sds((T, W_C), BF16), _sds((T, D_MIX), BF16),
                   _sds((D_MIX, D_MODEL), BF16), _sds((D_MODEL, 2 * W_C), BF16)),
        input_output_aliases={6: 9, 7: 10},
        scratch_shapes=[pltpu.VMEM((12, tm, 128), F32)] + [pltpu.SemaphoreType.DMA((2, 3))] * 4,
        compiler_params=_cp(("arbitrary",), vmem_mb=48),
    )(*_pin(x, winT, b_in, cos, sa, sb, wout_own, wmem_own))


def _mem_kv(mem, wmem):
    def body(m_ref, w_ref, mb_ref, kv_ref):
        mb = m_ref[...].astype(BF16)
        mb_ref[...] = mb
        kv_ref[...] = _dot(mb, w_ref[...], NN).astype(BF16)

    n = B_LOC * MEM_LEN
    return pl.pallas_call(
        body, name="mem_kv",
        out_shape=(_sds((n, D_MODEL), BF16), _sds((n, 2 * W_C), BF16)),
    )(*_pin(mem, wmem))


class _Part:
    def __init__(self, body, args, in_specs, out_specs, out_shape, scratch=()):
        self.body, self.args, self.in_specs, self.out_specs, self.out_shape = body, args, in_specs, out_specs, out_shape
        self.scratch = list(scratch)


def _run_parts(name, parts, semantics, vmem_mb):
    n_in = [len(p.args) for p in parts]
    n_out = [len(p.out_shape) for p in parts]
    n_scr = [len(p.scratch) for p in parts]

    def body(*refs):
        ins, outs, scr = refs[:sum(n_in)], refs[sum(n_in):sum(n_in) + sum(n_out)], refs[sum(n_in) + sum(n_out):]
        i0 = o0 = s0 = 0
        for p, ni, no, ns in zip(parts, n_in, n_out, n_scr):
            p.body(*ins[i0:i0 + ni], *outs[o0:o0 + no], *scr[s0:s0 + ns])
            i0, o0, s0 = i0 + ni, o0 + no, s0 + ns

    res = pl.pallas_call(
        body, name=name, grid=(T // QR,),
        in_specs=[sp for p in parts for sp in p.in_specs], out_specs=tuple(sp for p in parts for sp in p.out_specs),
        out_shape=tuple(sh for p in parts for sh in p.out_shape),
        scratch_shapes=[sc for p in parts for sc in p.scratch],
        compiler_params=_cp((semantics,), vmem_mb=vmem_mb),
    )(*_pin(*[a for p in parts for a in p.args]))
    out, o0 = [], 0
    for no in n_out:
        out.append(tuple(res[o0:o0 + no]))
        o0 += no
    return out


QB = 8
QR = QB * BLK


def _lane_lo():
    return lax.broadcasted_iota(jnp.int32, (1, 128), 1) < 64


def _dup_head(k2, hk, lo):
    kf = k2.astype(F32)
    r = pltpu.roll(kf, 64, 1)
    return (jnp.where(lo, kf, r) if hk == 0 else jnp.where(lo, r, kf)).astype(BF16)


def _stack_heads(pairs, lo):
    parts = []
    for x2 in pairs:
        z = jnp.zeros_like(x2)
        parts += [jnp.where(lo, x2, z), jnp.where(lo, z, x2)]
    return jnp.concatenate(parts, axis=0)


def _prev_mode(kind, nb, j):
    if kind == "mem" or nb == 1:
        return "no"
    if nb <= QB:
        return "yes" if j % nb else "no"
    return "yes" if j else "dyn"


class _Attn:
    def __init__(self, kind, nb, max_dist, gqa, qw, kvw, qcb, kcb, vcb):
        self.kind, self.nb, self.gqa, self.qw, self.kvw = kind, nb, gqa, qw, kvw
        npairs = qw // 128
        self.groups = ([(hk, [2 * hk, 2 * hk + 1]) for hk in range(npairs // 2)] if gqa
                       else [(p, [p]) for p in range(npairs)])
        self.nh = 2 * len(self.groups[0][1])
        self.cols = 128 * self.nh
        self.reach = BLK - max_dist
        self.ext_prev = kind == "band" and nb > QB
        self.q_spec = pl.BlockSpec((QR, qw), lambda g: (g, qcb))
        self.row_spec = pl.BlockSpec((QR, qw), lambda g: (g, 0))
        self.stat_spec = pl.BlockSpec((QR, 128), lambda g: (g, 0))
        if kind == "mem":
            per = SEQ // QR
            self.kv_specs = [pl.BlockSpec((MEM_LEN, kvw), lambda g: (g // per, kcb)),
                             pl.BlockSpec((MEM_LEN, kvw), lambda g: (g // per, vcb))]
        else:
            self.kv_specs = [pl.BlockSpec((QR, kvw), lambda g: (g, kcb)), pl.BlockSpec((QR, kvw), lambda g: (g, vcb))]
            if self.ext_prev:
                self.kv_specs += [pl.BlockSpec((BLK, kvw), lambda g: (jnp.maximum(g * QB - 1, 0), kcb)),
                                  pl.BlockSpec((BLK, kvw), lambda g: (jnp.maximum(g * QB - 1, 0), vcb))]

    def masks(self):
        if self.kind == "mem":
            return None
        kj = lax.broadcasted_iota(jnp.int32, (2 * BLK, self.cols), 0)
        qi = lax.broadcasted_iota(jnp.int32, (2 * BLK, self.cols), 1) & (BLK - 1)
        both = jnp.logical_and(kj >= qi + self.reach, kj <= qi + BLK)
        return kj, qi, self.as_mask(both), self.as_mask(kj[:BLK] <= qi[:BLK])

    def as_mask(self, in_reach):
        return jnp.where(in_reach, 0.0, NEG) if self.gqa else in_reach

    def hide(self, s, mask):
        return s + mask if self.gqa else jnp.where(mask, s, NEG)

    def keys(self, j, gi, kc_ref, vc_ref, kp_ref, vp_ref, lo, kq, g):
        def kv(k_ref, v_ref, r):
            if self.gqa:
                return _dup_head(k_ref[r, :], gi, lo), _dup_head(v_ref[r, :], gi, lo)
            sl = slice(128 * gi, 128 * (gi + 1))
            return k_ref[r, sl], v_ref[r, sl]

        if self.kind == "mem":
            key0 = pl.multiple_of((g // (SEQ // QR)) * MEM_LEN, MEM_LEN)
            return (*kv(kc_ref, vc_ref, slice(None)), None, [(0, MEM_LEN, key0)])
        kj, qi, both, cur = kq
        row0 = g * QR + BLK * j
        mode = _prev_mode(self.kind, self.nb, j)
        if mode == "no":
            return (*kv(kc_ref, vc_ref, slice(BLK * j, BLK * (j + 1))), cur, [(0, BLK, pl.multiple_of(row0, BLK))])
        if mode == "yes":
            return (*kv(kc_ref, vc_ref, slice(BLK * (j - 1), BLK * (j + 1))), both,
                    [(0, 2 * BLK, pl.multiple_of(row0 - BLK, BLK))])
        has_prev = ((g * QB) % self.nb) > 0
        hp = has_prev.astype(jnp.int32)
        mask = self.as_mask(jnp.logical_and(kj >= qi * hp + (self.reach * hp + BLK * (1 - hp)), kj <= qi + BLK))
        kp, vp = kv(kp_ref, vp_ref, slice(None))
        kc, vc = kv(kc_ref, vc_ref, slice(0, BLK))
        return (jnp.concatenate([kp, kc], axis=0), jnp.concatenate([vp, vc], axis=0), mask,
                [(0, BLK, pl.multiple_of(jnp.maximum(row0 - BLK, 0), BLK)), (BLK, BLK, pl.multiple_of(row0, BLK))])


def _attn_fwd(q, qcb, qw, k, kcb, v, vcb, kvw, *, kind, nb=1, max_dist=BLK, gqa=False, sinks=None):
    a = _Attn(kind, nb, max_dist, gqa, qw, kvw, qcb, kcb, vcb)

    def body(*refs):
        it = iter(refs)
        q_ref, kc_ref, vc_ref = next(it), next(it), next(it)
        kp_ref, vp_ref = (next(it), next(it)) if a.ext_prev else (None, None)
        sink_ref = next(it) if sinks is not None else None
        o_ref, lse_ref = next(it), next(it)
        g = pl.program_id(0)
        lo = _lane_lo()
        top = lax.broadcasted_iota(jnp.int32, (128, 1), 0) < 64
        rid = lax.broadcasted_iota(jnp.int32, (8, 128), 0)
        kq = a.masks()
        stats = {}

        def scores(j, gi, pairs):
            rows = slice(BLK * j, BLK * (j + 1))
            qs = _stack_heads([q_ref[rows, 128 * p:128 * (p + 1)] for p in pairs], lo)
            kk, vv, mask, _ = a.keys(j, gi, kc_ref, vc_ref, kp_ref, vp_ref, lo, kq, g)
            pieces = [slice(r0, r0 + BLK) for r0 in range(0, kk.shape[0], BLK)]
            return dict(j=j, gi=gi, pairs=pairs, rows=rows, vv=vv, mask=mask, pieces=pieces,
                        ss=[_dot(kk[r], qs, NT) for r in pieces])

        def softmax(c):
            gi, mask = c["gi"], c["mask"]
            ss = [s if mask is None else a.hide(s, mask[r]) for r, s in zip(c["pieces"], c.pop("ss"))]
            m = jnp.max(ss[0], axis=0, keepdims=True)
            for s in ss[1:]:
                m = jnp.maximum(m, jnp.max(s, axis=0, keepdims=True))
            if sink_ref is not None:
                sk = jnp.concatenate([jnp.full((1, 128), sink_ref[0, a.nh * gi + i], F32) for i in range(a.nh)], axis=1)
                m = jnp.maximum(m, sk)
            ps = [jnp.exp(s - m) for s in ss]
            l = sum(jnp.sum(p, axis=0, keepdims=True) for p in ps)
            if sink_ref is not None:
                l = l + jnp.exp(sk - m)
            c["ps"] = [p.astype(BF16) for p in ps]
            c["l"], c["lse"] = l, m + jnp.log(l)

        def outputs(c):
            j, gi, rows = c["j"], c["gi"], c["rows"]
            ot = sum(_dot(c["vv"][r], p, TN) for r, p in zip(c["pieces"], c["ps"]))
            ot = ot * pl.reciprocal(c["l"], approx=True)
            for i, p in enumerate(c["pairs"]):
                o2t = jnp.where(top, ot[:, 256 * i:256 * i + 128], ot[:, 256 * i + 128:256 * i + 256])
                o_ref[rows, 128 * p:128 * (p + 1)] = o2t.T.astype(BF16)
            stat = stats.get(j, jnp.zeros((8, 128), F32))
            for i in range(a.nh):
                stat = jnp.where(rid == a.nh * gi + i, c["lse"][:, 128 * i:128 * (i + 1)], stat)
            stats[j] = stat
            if gi == a.groups[-1][0]:
                lse_ref[rows, :] = jnp.concatenate([stats.pop(j), jnp.zeros((120, 128), F32)], axis=0).T

        chains = [(j, gi, pairs) for j in range(QB) for gi, pairs in a.groups]
        live = {}
        for t in range(len(chains) + 2):
            if t < len(chains):
                live[t] = scores(*chains[t])
            if 0 <= t - 1 < len(chains):
                softmax(live[t - 1])
            if 0 <= t - 2 < len(chains):
                outputs(live.pop(t - 2))


    args = [q, k, v] + ([k, v] if a.ext_prev else [])
    in_specs = [a.q_spec] + a.kv_specs
    if sinks is not None:
        args.append(sinks)
        in_specs.append(pl.BlockSpec(memory_space=pltpu.SMEM))
    return _Part(body, args, in_specs, [a.row_spec, a.stat_spec], [_sds((T, qw), BF16), _sds((T, 128), F32)])


def _attn_bwd(q, qcb, qw, k, kcb, v, vcb, kvw, do, lse, dl, *, kind, nb=1, max_dist=BLK, gqa=False, sinkv=None,
              mem_in=None):
    a = _Attn(kind, nb, max_dist, gqa, qw, kvw, qcb, kcb, vcb)

    def body(*refs):
        it = iter(refs)
        q_ref, kc_ref, vc_ref = next(it), next(it), next(it)
        kp_ref, vp_ref = (next(it), next(it)) if a.ext_prev else (None, None)
        do_ref, lse_ref, dl_ref = next(it), next(it), next(it)
        sinkv_ref = next(it) if sinkv is not None else None
        mem_ref = next(it) if kind == "mem" else None
        dq_ref = next(it)
        if kind == "mem":
            gmem_ref = next(it)
        else:
            dk_out, dv_out = next(it), next(it)
        dsink_ref = next(it) if sinkv is not None else None
        if kind != "mem":
            dk_ref, dv_ref, stage_k, stage_v, flush_sem = next(it), next(it), next(it), next(it), next(it)
        else:
            dkv_ref = next(it)
        g = pl.program_id(0)
        lo = _lane_lo()
        top = lax.broadcasted_iota(jnp.int32, (128, 1), 0) < 64

        @pl.when(g == 0)
        def _():
            if kind == "mem":
                dkv_ref[...] = jnp.zeros_like(dkv_ref)
            else:
                dk_ref[...] = jnp.zeros_like(dk_ref)
                dv_ref[...] = jnp.zeros_like(dv_ref)
            if dsink_ref is not None:
                dsink_ref[...] = jnp.zeros_like(dsink_ref)

        kq = a.masks()
        stats_t = {}

        def first_matmuls(j, gi, pairs):
            rows = slice(BLK * j, BLK * (j + 1))
            if j not in stats_t:
                stats_t[j] = (lse_ref[rows, :].T, dl_ref[rows, :].T)
            lse_t, dl_t = stats_t[j]
            heads = [a.nh * gi + i for i in range(a.nh)]
            c = dict(rows=rows, gi=gi, pairs=pairs)
            c["qs"] = _stack_heads([q_ref[rows, 128 * p:128 * (p + 1)] for p in pairs], lo)
            c["dos"] = _stack_heads([do_ref[rows, 128 * p:128 * (p + 1)] for p in pairs], lo)
            c["lse_row"] = jnp.concatenate([lse_t[h:h + 1, :] for h in heads], axis=1)
            c["dl_row"] = jnp.concatenate([dl_t[h:h + 1, :] for h in heads], axis=1)
            c["kk"], vv, c["mask"], c["dests"] = a.keys(j, gi, kc_ref, vc_ref, kp_ref, vp_ref, lo, kq, g)
            c["s"] = _dot(c["kk"], c["qs"], NT)
            c["dp"] = _dot(vv, c["dos"], NT)
            return c

        def elementwise(c):
            s = c.pop("s")
            if c["mask"] is not None:
                s = a.hide(s, c["mask"])
            p = jnp.exp(s - c["lse_row"])
            c["ds"] = (p * (c.pop("dp") - c["dl_row"])).astype(BF16)
            c["p"] = p.astype(BF16)

        def last_matmuls(c):
            gi, rows = c["gi"], c["rows"]
            dqt = _dot(c["kk"], c["ds"], TN)
            ck = _dot(c["ds"], c["qs"], NN)
            cv = _dot(c["p"], c["dos"], NN)
            if gqa:
                sel = lo if gi == 0 else jnp.logical_not(lo)
                ck = jnp.where(sel, ck + pltpu.roll(ck, 64, 1), 0.0)
                cv = jnp.where(sel, cv + pltpu.roll(cv, 64, 1), 0.0)
                kcols = slice(0, 128)
            else:
                kcols = slice(128 * gi, 128 * (gi + 1))
            for r0, nr, key0 in c["dests"]:
                krows = pl.ds(key0, nr)
                if kind == "mem":
                    dkv_ref[krows, kcols] += ck[r0:r0 + nr]
                    dkv_ref[krows, slice(kvw + kcols.start, kvw + kcols.stop)] += cv[r0:r0 + nr]
                else:
                    dk_ref[krows, kcols] += ck[r0:r0 + nr]
                    dv_ref[krows, kcols] += cv[r0:r0 + nr]
            for i, p in enumerate(c["pairs"]):
                dq2t = jnp.where(top, dqt[:, 256 * i:256 * i + 128], dqt[:, 256 * i + 128:256 * i + 256])
                dq_ref[rows, 128 * p:128 * (p + 1)] = dq2t.T.astype(BF16)

        chains = [(j, gi, pairs) for j in range(QB) for gi, pairs in a.groups]
        live = {}
        for t in range(len(chains) + 2):
            if t < len(chains):
                live[t] = first_matmuls(*chains[t])
            if 0 <= t - 1 < len(chains):
                elementwise(live[t - 1])
            if 0 <= t - 2 < len(chains):
                last_matmuls(live.pop(t - 2))
        if dsink_ref is not None:
            ps = jnp.exp(sinkv_ref[...] - lse_ref[...]) * dl_ref[...]
            dsink_ref[...] += jnp.sum(ps, axis=0, keepdims=True)
        if kind == "mem":
            @pl.when(g == T // QR - 1)
            def _():
                gmem_ref[...] = _dot(mem_ref[...], dkv_ref[...].astype(BF16), TN)
        else:
            n_steps = T // QR

            def flush(step):
                rows = pl.ds(pl.multiple_of(step * QR, QR), QR)
                out = []
                for acc, stage, dst, i in ((dk_ref, stage_k, dk_out, 0), (dv_ref, stage_v, dv_out, 1)):
                    stage[...] = acc[rows, :].astype(BF16)
                    out.append(pltpu.make_async_copy(stage, dst.at[rows, :], flush_sem.at[i]))
                return out

            def flushed(step):
                rows = pl.ds(pl.multiple_of(step * QR, QR), QR)
                return [pltpu.make_async_copy(stage, dst.at[rows, :], flush_sem.at[i])
                        for stage, dst, i in ((stage_k, dk_out, 0), (stage_v, dv_out, 1))]

            @pl.when(g >= 2)
            def _():
                for cp in flushed(g - 2):
                    cp.wait()

            @pl.when(g >= 1)
            def _():
                for cp in flush(g - 1):
                    cp.start()

            @pl.when(g == n_steps - 1)
            def _():
                for cp in flushed(g - 1):
                    cp.wait()
                for cp in flush(g):
                    cp.start()
                for cp in flushed(g):
                    cp.wait()

    args = [q, k, v] + ([k, v] if a.ext_prev else []) + [do, lse, dl]
    in_specs = [a.q_spec] + a.kv_specs + [a.row_spec, a.stat_spec, a.stat_spec]
    if sinkv is not None:
        args.append(sinkv)
        in_specs.append(_full((1, 128)))
    if kind == "mem":
        args.append(mem_in)
        in_specs.append(pl.BlockSpec(mem_in.shape, lambda g: (0, 0), pipeline_mode=pl.Buffered(1)))
    out_shape = [_sds((T, qw), BF16)]
    out_specs = [a.row_spec]
    scratch = []
    if kind == "mem":
        out_shape.append(_sds((D_MODEL, 2 * kvw), F32))
        out_specs.append(pl.BlockSpec((D_MODEL, 2 * kvw), lambda g: (0, 0), pipeline_mode=pl.Buffered(1)))
        scratch = [pltpu.VMEM((B_LOC * MEM_LEN, 2 * kvw), F32)]
    else:
        out_shape += [_sds((T, kvw), BF16)] * 2
        out_specs += [pl.BlockSpec(memory_space=pl.ANY)] * 2
        scratch = [pltpu.VMEM((T, kvw), F32)] * 2 + [pltpu.VMEM((QR, kvw), BF16)] * 2 + [pltpu.SemaphoreType.DMA((2,))]
    if sinkv is not None:
        out_shape.append(_sds((1, 128), F32))
        out_specs.append(_full((1, 128)))
    return _Part(body, args, in_specs, out_specs, out_shape, scratch)


def _dot2(v, w_ref):
    hi = v.astype(BF16)
    lo = (v - hi.astype(F32)).astype(BF16)
    return _dot(hi, w_ref[...], NN) + _dot(lo, w_ref[...], NN)


def _middle(oa, o1, l1, o4, l4, o16, l16, oc, z, x, tgt, g_br, ln_g, ln_b, wout, spread4, gather4, gather8):
    tm = 512
    spt = SEQ // tm

    def body(oa_ref, o1_ref, l1_ref, o4_ref, l4_ref, o16_ref, l16_ref, oc_ref, z_ref, x_ref, t_ref,
             g_ref, lg_ref, lb_ref, w_ref, sp4_ref, ga4_ref, ga8_ref,
             du_ref, dz_ref, doa_ref, dla_ref,
             dobn_ref, lsen_ref, dlbn_ref, dob4_ref, lse4_ref, dlb4_ref, dob16_ref, lse16_ref, dlb16_ref,
             doc_ref, dlc_ref, acc_ref, gout_ref, scr):
        i = pl.program_id(0)

        @pl.when(i == 0)
        def _():
            acc_ref[...] = jnp.zeros_like(acc_ref)
            gout_ref[...] = jnp.zeros_like(gout_ref)

        q = tm // 4
        for res in range(16):
            rows = pl.ds((res % 4) * q + res // 4, tm // 16, stride=4)
            for j in range(2):
                scr[6 + j, rows, :] = o16_ref[0, res, :, 128 * j:128 * (j + 1)].astype(F32)
            scr[8, rows, :] = l16_ref[0, res]
        for res in range(4):
            rows, blk = pl.ds(res, q, stride=4), slice(res * q, (res + 1) * q)
            for j in range(2):
                scr[j, rows, :] = o4_ref[0, res, :, 128 * j:128 * (j + 1)].astype(F32)
                scr[3 + j, rows, :] = scr[6 + j, blk, :]
            scr[2, rows, :] = l4_ref[0, res]
            scr[5, rows, :] = scr[8, blk, :]
        inv_d = 1.0 / D_MODEL
        gb, lg, lb = g_ref[...], lg_ref[...], lb_ref[...]

        def rms(o):
            r = lax.rsqrt(jnp.sum(o * o, axis=1, keepdims=True) * (1.0 / o.shape[1]) + RMS_EPS)
            return o * r, r

        def rms_bwd(dn_, n_, r):
            return r * (dn_ - n_ * (jnp.sum(dn_ * n_, axis=1, keepdims=True) * (1.0 / n_.shape[1])))

        def forward(rs):
            o4v = jnp.concatenate([scr[0, rs, :], scr[1, rs, :]], axis=1)
            o16v = jnp.concatenate([scr[3, rs, :], scr[4, rs, :]], axis=1)
            l1v, l4v, l16v = l1_ref[rs, :], scr[2, rs, :], scr[5, rs, :]
            mx = jnp.maximum(jnp.maximum(l1v, l4v), l16v)
            e1, e4, e16 = jnp.exp(l1v - mx), jnp.exp(l4v - mx), jnp.exp(l16v - mx)
            ssum = e1 + e4 + e16
            inv = 1.0 / ssum
            c = dict(rs=rs, lse_b=mx + jnp.log(ssum))
            c["ob"] = (_dot2(e1 * inv, sp4_ref) * o1_ref[rs, :].astype(F32) + _dot2(e4 * inv, sp4_ref) * o4v
                       + _dot2(e16 * inv, sp4_ref) * o16v)
            c["oa"], c["oc"] = oa_ref[rs, :].astype(F32), oc_ref[rs, :].astype(F32)
            na, c["ra"] = rms(c["oa"])
            nb_, c["rb"] = rms(c["ob"])
            nc, c["rc"] = rms(c["oc"])
            c["n"] = jnp.concatenate([na, nb_, nc], axis=1)
            c["zf"] = z_ref[rs, :].astype(F32)
            c["sig"] = 1.0 / (1.0 + jnp.exp(-c["zf"]))
            c["sz"] = c["zf"] * c["sig"]
            c["yb"] = (c["n"] * gb * c["sz"]).astype(BF16)
            c["y2"] = _dot(c["yb"], w_ref[...], NN)
            return c

        def norm(c):
            rs = c["rs"]
            u = ALPHA * x_ref[rs, :] + c.pop("y2")
            mu = jnp.sum(u, axis=1, keepdims=True) * inv_d
            uc = u - mu
            rstd = lax.rsqrt(jnp.sum(uc * uc, axis=1, keepdims=True) * inv_d + LN_EPS)
            xh = uc * rstd
            diff = xh * lg + lb - t_ref[rs, :]
            acc_ref[0:1, :] += jnp.sum(diff * diff, axis=0, keepdims=True) * (0.5 * inv_d)
            dout = diff * inv_d
            acc_ref[2:3, :] += jnp.sum(dout * xh, axis=0, keepdims=True)
            acc_ref[3:4, :] += jnp.sum(dout, axis=0, keepdims=True)
            dxh = dout * lg
            du = rstd * (dxh - jnp.sum(dxh, axis=1, keepdims=True) * inv_d
                         - xh * (jnp.sum(dxh * xh, axis=1, keepdims=True) * inv_d))
            dub = du.astype(BF16)
            du_ref[rs, :] = dub
            c["dy"] = _dot(dub, w_ref[...], NT)
            gout_ref[...] += _dot(c.pop("yb"), dub, TN)

        def backward(c):
            rs, n, dy, zf, sig = c["rs"], c["n"], c["dy"], c["zf"], c["sig"]
            t1 = dy * c["sz"]
            acc_ref[1:2, :] += jnp.sum(t1 * n, axis=0, keepdims=True)
            dn = t1 * gb
            dz_ref[rs, :] = (dy * n * gb * (sig * (1.0 + zf * (1.0 - sig)))).astype(BF16)
            doa = rms_bwd(dn[:, :W_A], n[:, :W_A], c["ra"])
            dob = rms_bwd(dn[:, W_A:W_A + W_B], n[:, W_A:W_A + W_B], c["rb"])
            doc = rms_bwd(dn[:, W_A + W_B:], n[:, W_A + W_B:], c["rc"])
            doa_ref[rs, :] = doa.astype(BF16)
            dla_ref[rs, :] = _dot2(doa * c["oa"], ga8_ref)
            doc_ref[rs, :] = doc.astype(BF16)
            dlc_ref[rs, :] = _dot2(doc * c["oc"], ga4_ref)
            dobn_ref[rs, :] = dob.astype(BF16)
            lsen_ref[rs, :] = c["lse_b"]
            dlbn_ref[rs, :] = _dot2(dob * c["ob"], ga4_ref)
            scr[0, rs, :] = dob[:, :128]
            scr[1, rs, :] = dob[:, 128:]

        halves = [slice(h * (tm // 2), (h + 1) * (tm // 2)) for h in range(2)]
        live = {}
        for t in range(len(halves) + 2):
            if t < len(halves):
                live[t] = forward(halves[t])
            if 0 <= t - 1 < len(halves):
                norm(live[t - 1])
            if 0 <= t - 2 < len(halves):
                backward(live.pop(t - 2))
        for j in range(2):
            sl = slice(128 * j, 128 * (j + 1))
            for res in range(4):
                t = scr[j, pl.ds(res, q, stride=4), :]
                dob4_ref[0, res, :, sl] = t.astype(BF16)
                scr[6 + j, res * q:(res + 1) * q, :] = t
            for res in range(16):
                dob16_ref[0, res, :, sl] = scr[6 + j, pl.ds((res % 4) * q + res // 4, tm // 16, stride=4),
                                               :].astype(BF16)
        for res in range(4):
            rows = pl.ds(res, q, stride=4)
            lse4_ref[0, res] = lsen_ref[rows, :]
            dlb4_ref[0, res] = dlbn_ref[rows, :]
        for res in range(16):
            rows = pl.ds(res // 4, tm // 16, stride=4)
            lse16_ref[0, res] = lse4_ref[0, res % 4, rows, :]
            dlb16_ref[0, res] = dlb4_ref[0, res % 4, rows, :]


    tok = lambda w: pl.BlockSpec((tm, w), lambda i: (i, 0))
    p4 = lambda w: pl.BlockSpec((1, 4, tm // 4, w), lambda i: (i // spt, 0, i % spt, 0))
    p16 = lambda w: pl.BlockSpec((1, 16, tm // 16, w), lambda i: (i // spt, 0, i % spt, 0))
    s4 = lambda w, dt: _sds((B_LOC, 4, SEQ // 4, w), dt)
    s16 = lambda w, dt: _sds((B_LOC, 16, SEQ // 16, w), dt)
    row = _full((1, D_MODEL))
    return pl.pallas_call(
        body, name="middle", grid=(T // tm,),
        in_specs=[tok(W_A), tok(W_B), tok(128), p4(W_B), p4(128), p16(W_B), p16(128), tok(W_C), tok(D_MIX),
                  tok(D_MODEL), tok(D_MODEL), row, row, row, _full((D_MIX, D_MODEL)),
                  _full((128, W_B)), _full((W_B, 128)), _full((W_A, 128))],
        out_specs=(tok(D_MODEL), tok(D_MIX), tok(W_A), tok(128),
                   tok(W_B), tok(128), tok(128), p4(W_B), p4(128), p4(128), p16(W_B), p16(128), p16(128),
                   tok(W_C), tok(128), _full((8, D_MODEL)), _full((D_MIX, D_MODEL))),
        out_shape=(_sds((T, D_MODEL), BF16), _sds((T, D_MIX), BF16),
                   _sds((T, W_A), BF16), _sds((T, 128), F32),
                   _sds((T, W_B), BF16), _sds((T, 128), F32), _sds((T, 128), F32),
                   s4(W_B, BF16), s4(128, F32), s4(128, F32), s16(W_B, BF16), s16(128, F32), s16(128, F32),
                   _sds((T, W_C), BF16), _sds((T, 128), F32), _sds((8, D_MODEL), F32),
                   _sds((D_MIX, D_MODEL), F32)),
        scratch_shapes=[pltpu.VMEM((9, tm, 128), F32)],
        compiler_params=_cp(("arbitrary",), vmem_mb=56),
    )(*_pin(oa, o1, l1, o4, l4, o16, l16, oc, z, x, tgt, g_br, ln_g, ln_b, wout, spread4, gather4, gather8))


class _ReduceScatter:
    def __init__(self, shapes):
        self.shapes = shapes

    def scratch_shapes(self):
        out = []
        for n, w in self.shapes:
            h, p = n // 2, n // 4
            out += [pltpu.VMEM((4, h, w), F32), pltpu.VMEM((4, h, w), F32), pltpu.VMEM((6, p, w), BF16),
                    pltpu.VMEM((6, p, w), BF16), pltpu.VMEM((2, p, w), F32), pltpu.VMEM((h, w), F32)]
        na = len(self.shapes)
        dma = pltpu.SemaphoreType.DMA
        return out + [dma((na, 4)), dma((na, 4)), dma((na, 4)), dma((na, 6)), dma((na, 6)), dma((na,)), dma((na,)),
                      dma((na,))]

    def bind(self, g_refs, r_refs, scratch):
        na = len(self.shapes)
        bufs = [scratch[6 * a:6 * a + 6] for a in range(na)]
        mine, sib, stage, land, keep, tot = (tuple(b[i] for b in bufs) for i in range(6))
        loc_sem, s1_send, s1_recv, s2_send, s2_recv, s3_send, s3_recv, st_sem = scratch[6 * na:6 * na + 8]
        x, y, c = lax.axis_index("x"), lax.axis_index("y"), lax.axis_index("c")
        me, sibling = (x, y, c), (x, y, 1 - c)
        xn, yn, dg = (1 - x, y), (x, 1 - y), (1 - x, 1 - y)
        idx = lambda chip: 2 * chip[0] + chip[1]
        my_chip = idx((x, y))
        order = [idx(xn), idx(dg), idx(yn), my_chip]

        def rows(a, k, half):
            n = self.shapes[a][0]
            return pl.ds(pl.multiple_of(k * n + half * (n // 2), 8), n // 2)

        def piece(a, q):
            p = self.shapes[a][0] // 4
            return slice(q * p, (q + 1) * p)

        def load(a, k):
            return pltpu.make_async_copy(g_refs[a].at[rows(a, k, c), :], mine[a].at[k], loc_sem.at[a, k])

        def s1(a, k, half):
            return pltpu.make_async_remote_copy(
                src_ref=g_refs[a].at[rows(a, k, half), :], dst_ref=sib[a].at[k],
                send_sem=s1_send.at[a, k], recv_sem=s1_recv.at[a, k], device_id=sibling, device_id_type=MESH)

        def s2(a, i, to):
            return pltpu.make_async_remote_copy(
                src_ref=stage[a].at[i], dst_ref=land[a].at[i], send_sem=s2_send.at[a, i], recv_sem=s2_recv.at[a, i],
                device_id=to, device_id_type=MESH)

        via = {0: xn, 1: xn, 2: yn, 3: yn, 4: yn, 5: xn}

        def s3(a, half, to):
            return pltpu.make_async_remote_copy(
                src_ref=tot[a], dst_ref=r_refs[a].at[rows(a, 0, half), :], send_sem=s3_send.at[a],
                recv_sem=s3_recv.at[a], device_id=to, device_id_type=MESH)

        def store(a):
            return pltpu.make_async_copy(tot[a], r_refs[a].at[rows(a, 0, c), :], st_sem.at[a])

        def start():
            for k in order:
                for a in range(na):
                    load(a, k).start()
                    s1(a, k, 1 - c).start()

        def chip_sum(a, k):
            load(a, k).wait()
            s1(a, k, c).wait_recv()
            return mine[a][k] + sib[a][k]

        def exchange():
            for a in range(na):
                P, Q = piece(a, 0), piece(a, 1)
                s_xn = chip_sum(a, idx(xn))
                stage[a][0] = s_xn[P].astype(BF16)
                keep[a][1] = s_xn[Q]
                s_dg = chip_sum(a, idx(dg))
                stage[a][1] = s_dg[P].astype(BF16)
                s2(a, 0, (*xn, c)).start()
                s2(a, 1, (*xn, c)).start()
                stage[a][3] = s_dg[Q].astype(BF16)
                s_yn = chip_sum(a, idx(yn))
                stage[a][2] = s_yn[Q].astype(BF16)
                keep[a][0] = s_yn[P]
                s2(a, 2, (*yn, c)).start()
                s2(a, 3, (*yn, c)).start()
                tot[a][...] = chip_sum(a, my_chip)

        def relay():
            for a in range(na):
                P, Q = piece(a, 0), piece(a, 1)
                s2(a, 1, me).wait_recv()
                stage[a][4] = (keep[a][0] + land[a][1].astype(F32)).astype(BF16)
                s2(a, 4, (*yn, c)).start()
                s2(a, 3, me).wait_recv()
                stage[a][5] = (keep[a][1] + land[a][3].astype(F32)).astype(BF16)
                s2(a, 5, (*xn, c)).start()
                s2(a, 0, me).wait_recv()
                tot[a][P, :] += land[a][0].astype(F32)
                s2(a, 2, me).wait_recv()
                tot[a][Q, :] += land[a][2].astype(F32)

        def finish():
            for a in range(na):
                P, Q = piece(a, 0), piece(a, 1)
                s2(a, 4, me).wait_recv()
                tot[a][P, :] += land[a][4].astype(F32)
                s2(a, 5, me).wait_recv()
                tot[a][Q, :] += land[a][5].astype(F32)
                s3(a, c, sibling).start()
                store(a).start()

        def drain():
            for a in range(na):
                s3(a, 1 - c, me).wait_recv()
                store(a).wait()
            for a in range(na):
                for k in order:
                    s1(a, k, 1 - c).wait_send()
                for i in range(6):
                    s2(a, i, (*via[i], c)).wait_send()
                s3(a, c, sibling).wait_send()

        return start, exchange, relay, finish, drain

    def part(self, grads, steps):
        def body(*refs):
            na = len(self.shapes)
            i = pl.program_id(0)
            for step, phase in zip(steps, self.bind(refs[:na], refs[na:2 * na], refs[2 * na:])):
                pl.when(i == step)(phase)

        hbm = pl.BlockSpec(memory_space=pl.ANY)
        return _Part(body, list(grads), [hbm] * len(grads), [hbm] * len(grads),
                     [_sds((n, w), F32) for n, w in self.shapes], self.scratch_shapes())


def _dh_dx(dqa, dka, dva, dqn, dkn, dvn, dq4, dk4, dv4, dq16, dk16, dv16, dqc, dz, du, xb, cos, sa, sb, winT):
    tm = 512
    spt = SEQ // tm

    def body(dqa_ref, dka_ref, dva_ref, dqn_ref, dkn_ref, dvn_ref, dq4_ref, dk4_ref, dv4_ref,
             dq16_ref, dk16_ref, dv16_ref, dqc_ref, dz_ref, du_ref, xb_ref, cos_ref, sa_ref, sb_ref, w_ref,
             gx_ref, db_ref, gin_ref, dh_ref, scr):
        i = pl.program_id(0)

        @pl.when(i == 0)
        def _():
            db_ref[...] = jnp.zeros_like(db_ref)
            gin_ref[...] = jnp.zeros_like(gin_ref)

        cos_t, sa_t, sb_t = cos_ref[...], sa_ref[...], sb_ref[...]

        def rope_t(t):
            return _rope(t, cos_t, sa_t, sb_t, -1)

        def put(r0, val):
            n = val.shape[1]
            dh_ref[:, r0:r0 + n] = val.astype(BF16)
            db_ref[:, r0:r0 + n] += jnp.sum(val, axis=0, keepdims=True)

        put(O_QA, rope_t(dqa_ref[...].astype(F32)) * QK_SCALE)
        put(O_KA, rope_t(dka_ref[...].astype(F32)))
        put(O_VA, dva_ref[...].astype(F32))
        put(O_QC, dqc_ref[...].astype(F32) * QK_SCALE)
        put(O_Z, dz_ref[...].astype(F32))
        for k, (n_ref, r4, r16) in enumerate(((dqn_ref, dq4_ref, dq16_ref), (dkn_ref, dk4_ref, dk16_ref),
                                               (dvn_ref, dv4_ref, dv16_ref))):
            for j in range(2):
                sl = slice(128 * j, 128 * (j + 1))
                a, q = 2 * k + j, tm // 4
                scr[a] = n_ref[:, sl].astype(F32)
                for res in range(16):
                    scr[6 + a, pl.ds((res % 4) * q + res // 4, tm // 16, stride=4), :] = r16[0, res, :, sl].astype(F32)
                for res in range(4):
                    scr[a, pl.ds(res, q, stride=4), :] += (scr[6 + a, res * q:(res + 1) * q, :]
                                                           + r4[0, res, :, sl].astype(F32))
        cat = lambda a: jnp.concatenate([scr[a], scr[a + 1]], axis=1)
        put(O_QB, rope_t(cat(0)) * QK_SCALE)
        put(O_KB, rope_t(cat(2)))
        put(O_VB, cat(4))
        gx_ref[...] = _dot(dh_ref[...], w_ref[...], NN) + ALPHA * du_ref[...].astype(F32)
        gin_ref[...] += _dot(dh_ref[...], xb_ref[...], TN)

    tok = lambda w: pl.BlockSpec((tm, w), lambda i: (i, 0))
    tab = pl.BlockSpec((tm, 128), lambda i: (i % spt, 0))
    p4 = pl.BlockSpec((1, 4, tm // 4, W_B), lambda i: (i // spt, 0, i % spt, 0))
    p16 = pl.BlockSpec((1, 16, tm // 16, W_B), lambda i: (i // spt, 0, i % spt, 0))
    once = lambda shape: pl.BlockSpec(shape, lambda i: (0, 0), pipeline_mode=pl.Buffered(1))
    return pl.pallas_call(
        body, name="dh_dx", grid=(T // tm,),
        in_specs=[tok(W_A), tok(W_KV_A), tok(W_KV_A), tok(W_B), tok(W_B), tok(W_B), p4, p4, p4, p16, p16, p16,
                  tok(W_C), tok(D_MIX), tok(D_MODEL), tok(D_MODEL), tab, tab, tab, once((D_IN, D_MODEL))],
        out_specs=(tok(D_MODEL), _full((1, D_IN)), once((D_IN, D_MODEL))),
        out_shape=(_sds((T, D_MODEL), F32), _sds((1, D_IN), F32), _sds((D_IN, D_MODEL), F32)),
        scratch_shapes=[pltpu.VMEM((tm, D_IN), BF16), pltpu.VMEM((12, tm, 128), F32)],
        compiler_params=_cp(("arbitrary",), vmem_mb=56),
    )(*_pin(dqa, dka, dva, dqn, dkn, dvn, dq4, dk4, dv4, dq16, dk16, dv16, dqc, dz, du, xb, cos, sa, sb, winT))


def _reduce_grads(g_in, acc, dbin, dsink):
    rs = _ReduceScatter([(SH_IN, D_MODEL)])

    def body(g_ref, acc_ref, dbin_ref, dsink_ref, r_ref, sv_ref, sv_mine, sv_all, sv_send, sv_recv, *rs_scratch):
        x, y, c = lax.axis_index("x"), lax.axis_index("y"), lax.axis_index("c")
        chips = [(1 - x, y), (x, 1 - y), (1 - x, 1 - y)]
        start, exchange, relay, finish, drain = rs.bind((g_ref,), (r_ref,), rs_scratch)
        start()

        sv_mine[...] = jnp.zeros_like(sv_mine)
        sv_mine[0:4, 0:D_MODEL] = acc_ref[0:4, :]
        sv_mine[4:5, 0:D_IN] = dbin_ref[...]
        sv_mine[5:6, 0:128] = dsink_ref[...]
        my_dev = 4 * x + 2 * y + c
        others = [(x, y, 1 - c)] + [(*chip, cc) for chip in chips for cc in (c, 1 - c)]

        def sv_copy(j, to):
            return pltpu.make_async_remote_copy(
                src_ref=sv_mine, dst_ref=sv_all.at[my_dev], send_sem=sv_send.at[j], recv_sem=sv_recv.at[j],
                device_id=to, device_id_type=MESH)

        sv_sends = [sv_copy(j, to) for j, to in enumerate(others)]
        for cp in sv_sends:
            cp.start()
        exchange()
        relay()
        finish()
        sv_all[my_dev] = sv_mine[...]
        for j in range(7):
            sv_copy(j, (x, y, c)).wait_recv()
        tot = sv_all[0]
        for d in range(1, 8):
            tot = tot + sv_all[d]
        sv_ref[...] = tot
        drain()
        for cp in sv_sends:
            cp.wait_send()

    vm = pl.BlockSpec(memory_space=pltpu.VMEM)
    hbm = pl.BlockSpec(memory_space=pl.ANY)
    return pl.pallas_call(
        body, name="reduce_grads",
        out_shape=(_sds((SH_IN, D_MODEL), F32), _vm_sds((8, SV_W), F32)),
        in_specs=[hbm, vm, vm, vm], out_specs=(hbm, vm),
        scratch_shapes=[pltpu.VMEM((8, SV_W), F32), pltpu.VMEM((8, 8, SV_W), F32),
                        pltpu.SemaphoreType.DMA((7,)), pltpu.SemaphoreType.DMA((7,))] + rs.scratch_shapes(),
        compiler_params=_cp(vmem_mb=40),
    )(pltpu.with_memory_space_constraint(g_in, pltpu.HBM), acc, dbin, dsink)


def _adamw_update(w, g, m, v):
    nm = ADAM_B1 * m + (1.0 - ADAM_B1) * g
    nv = ADAM_B2 * v + (1.0 - ADAM_B2) * (g * g)
    m_hat = nm / (1.0 - ADAM_B1 ** ADAM_STEP)
    v_hat = nv / (1.0 - ADAM_B2 ** ADAM_STEP)
    return -ADAM_LR * (m_hat / (jnp.sqrt(v_hat) + ADAM_EPS) + ADAM_WD * w), nm, nv


SMALL = ((4, D_IN, 1.0), (5, 8, -1.0), (1, D_MIX, 1.0), (2, D_MODEL, 1.0), (3, D_MODEL, 1.0))


def _adamw_all(items, sv, ws, ms, vs, n_steps=4):
    nb, ns = 4 * len(items), len(SMALL)

    def body(*refs):
        ins, sv_ref, small_in = refs[:nb], refs[nb], refs[nb + 1:nb + 1 + 3 * ns]
        outs = refs[nb + 1 + 3 * ns:]
        big_out, loss_ref, small_out = outs[:nb], outs[nb], outs[nb + 1:]
        for p in range(len(items)):
            w_ref, g_ref, m_ref, v_ref = ins[4 * p:4 * p + 4]
            gv = g_ref[...]
            big_out[4 * p][...] = gv
            big_out[4 * p + 1][...], big_out[4 * p + 2][...], big_out[4 * p + 3][...] = _adamw_update(
                w_ref[...], gv, m_ref[...], v_ref[...])

        @pl.when(pl.program_id(0) == 0)
        def _():
            loss_ref[...] = jnp.sum(sv_ref[0:1, 0:D_MODEL], axis=1, keepdims=True)
            for p, (row, width, sign) in enumerate(SMALL):
                gv = sign * sv_ref[row:row + 1, 0:width]
                small_out[4 * p][...] = gv
                small_out[4 * p + 1][...], small_out[4 * p + 2][...], small_out[4 * p + 3][...] = _adamw_update(
                    small_in[p][...], gv, small_in[ns + p][...], small_in[2 * ns + p][...])

    specs, shapes, args = [], [], []
    for w, g, m, v in items:
        rows, width = w.shape
        specs += [pl.BlockSpec((rows // n_steps, width), lambda i: (i, 0))] * 4
        shapes += [_sds((rows, width), F32)] * 4
        args += [w, g, m, v]
    small_args = [*ws, *ms, *vs]
    whole = lambda a: _full(a.shape)
    res = pl.pallas_call(
        body, name="adamw", grid=(n_steps,),
        in_specs=specs + [whole(sv)] + [whole(a) for a in small_args],
        out_specs=tuple(specs + [_full((1, 1))] + [whole(w) for w in ws for _ in range(4)]),
        out_shape=tuple(shapes + [_sds((1, 1), F32)] + [_sds(w.shape, F32) for w in ws for _ in range(4)]),
        compiler_params=_cp(("arbitrary",), vmem_mb=40),
    )(*_pin(*args, sv, *small_args))
    big = [tuple(res[4 * p:4 * p + 4]) for p in range(len(items))]
    return big, res[nb], [tuple(res[nb + 1 + 4 * p:nb + 5 + 4 * p]) for p in range(ns)]


def _rope_tables():
    pos = np.arange(SEQ, dtype=np.float32)
    inv = (np.float32(ROPE_THETA) ** (-np.arange(0, 64, 2, dtype=np.float32) / np.float32(64))).astype(np.float32)
    ang = np.tile(pos[:, None] * inv[None, :], (1, 4))
    cos, sin = np.cos(ang).astype(np.float32), np.sin(ang).astype(np.float32)
    low = (np.arange(128) % 64) < 32
    zero = np.float32(0.0)
    return jnp.asarray(cos), jnp.asarray(np.where(low, -sin, zero)), jnp.asarray(np.where(low, zero, sin))


def _local_step(x2, mem2, tgt2, winT, wout, wmem, b_in, sinks, g_branch, ln_gain, ln_bias):
    cos, sa, sb = _rope_tables()
    sinkv = jnp.pad(sinks, ((0, 0), (0, 120)))
    head_of_lane = np.arange(512)[:, None] // 64
    gather8 = jnp.asarray(head_of_lane == np.arange(128)[None, :], BF16)
    gather4 = jnp.asarray(head_of_lane[:W_B] == np.arange(128)[None, :], BF16)
    spread4 = jnp.asarray((head_of_lane[:W_B] == np.arange(128)[None, :]).T, BF16)

    xb, qa, ka, va, bn, b4, b16, qc, z, wout, wmem = _in_proj(x2, winT, b_in, cos, sa, sb, wout, wmem)
    memb, mkv = _mem_kv(mem2, wmem)
    b4f, b16f = b4.reshape(T, 768), b16.reshape(T, 768)

    swa = dict(kind="band", nb=SEQ // BLK, max_dist=BLK - 1, gqa=True)
    dil = (dict(kind="band", nb=SEQ // BLK), dict(kind="band", nb=SEQ // 4 // BLK), dict(kind="band", nb=1))
    (oa, lse_a), (o1, l1), (o4, l4), (o16, l16), (oc, lse_c) = _run_parts("attn_fwd", [
        _attn_fwd(qa, 0, W_A, ka, 0, va, 0, W_KV_A, sinks=sinks, **swa),
        _attn_fwd(bn, 0, W_B, bn, 1, bn, 2, W_B, **dil[0]),
        _attn_fwd(b4f, 0, W_B, b4f, 1, b4f, 2, W_B, **dil[1]),
        _attn_fwd(b16f, 0, W_B, b16f, 1, b16f, 2, W_B, **dil[2]),
        _attn_fwd(qc, 0, W_C, mkv, 0, mkv, 1, W_C, kind="mem")], "parallel", 48)

    s4 = lambda w: (B_LOC, 4, SEQ // 4, w)
    s16 = lambda w: (B_LOC, 16, SEQ // 16, w)
    (du, dz, doa, dla, dobn, lsen, dlbn, dob4, lse4, dlb4, dob16, lse16, dlb16, doc, dlc, acc, g_out) = _middle(
        oa, o1, l1, o4.reshape(s4(W_B)), l4.reshape(s4(128)), o16.reshape(s16(W_B)), l16.reshape(s16(128)), oc, z,
        x2, tgt2, g_branch, ln_gain, ln_bias, wout, spread4, gather4, gather8)

    flat = lambda a: a.reshape(T, a.shape[-1])
    (dqa, dka, dva, dsink), (dqc, g_mem) = _run_parts("attn_bwd_a", [
        _attn_bwd(qa, 0, W_A, ka, 0, va, 0, W_KV_A, doa, lse_a, dla, sinkv=sinkv, **swa),
        _attn_bwd(qc, 0, W_C, mkv, 0, mkv, 1, W_C, doc, lse_c, dlc, kind="mem", mem_in=memb)], "arbitrary", 48)
    last = T // QR - 1
    (r_out, r_mem), (dqn, dkn, dvn), (dq4, dk4, dv4), (dq16, dk16, dv16) = _run_parts("attn_bwd_b", [
        _ReduceScatter([(SH_OUT, D_MODEL), (SH_MEM, 2 * W_C)]).part((g_out, g_mem), (0, 1, 2, last, last)),
        _attn_bwd(bn, 0, W_B, bn, 1, bn, 2, W_B, dobn, lsen, dlbn, **dil[0]),
        _attn_bwd(b4f, 0, W_B, b4f, 1, b4f, 2, W_B, flat(dob4), flat(lse4), flat(dlb4), **dil[1]),
        _attn_bwd(b16f, 0, W_B, b16f, 1, b16f, 2, W_B, flat(dob16), flat(lse16), flat(dlb16), **dil[2])],
        "arbitrary", 62)

    r4 = lambda a: a.reshape(s4(W_B))
    r16 = lambda a: a.reshape(s16(W_B))
    gx, dbin, g_in = _dh_dx(dqa, dka, dva, dqn, dkn, dvn, r4(dq4), r4(dk4), r4(dv4), r16(dq16), r16(dk16),
                            r16(dv16), dqc, dz, du, xb, cos, sa, sb, winT)
    return gx, g_in, r_out, r_mem, acc, dbin, dsink


def kernel(x, mem, w_in, b_in, w_mem, attn_sinks, g_branch, w_out, ln_gain, ln_bias, loss_target, m_w_in, m_b_in, m_w_mem, m_attn_sinks, m_g_branch, m_w_out, m_ln_gain, m_ln_bias, v_w_in, v_b_in, v_w_mem, v_attn_sinks, v_g_branch, v_w_out, v_ln_gain, v_ln_bias):
    winT, wout, wmem = _gather_weights(w_in[0].T, w_out[0], w_mem[0])
    gx, g_in, r_out, r_mem, acc, dbin, dsink = _local_step(
        x.reshape(T, D_MODEL), mem.reshape(B_LOC * MEM_LEN, D_MODEL), loss_target.reshape(T, D_MODEL),
        winT, wout, wmem, b_in, attn_sinks, g_branch, ln_gain, ln_bias)
    r_in, sv = _reduce_grads(g_in, acc, dbin, dsink)

    small = ["b_in", "attn_sinks", "g_branch", "ln_gain", "ln_bias"]
    big, loss, steps = _adamw_all(
        [(w_in[0].T, r_in, m_w_in[0].T, v_w_in[0].T), (w_out[0], r_out, m_w_out[0], v_w_out[0]),
         (w_mem[0], r_mem, m_w_mem[0], v_w_mem[0])],
        sv, [b_in, attn_sinks, g_branch, ln_gain, ln_bias], [m_b_in, m_attn_sinks, m_g_branch, m_ln_gain, m_ln_bias],
        [v_b_in, v_attn_sinks, v_g_branch, v_ln_gain, v_ln_bias])
    out = dict(zip(small, steps))
    out["w_in"] = tuple(a.T[None] for a in big[0])
    out["w_out"], out["w_mem"] = (tuple(a[None] for a in st) for st in big[1:])
    names = ["w_in", "b_in", "w_mem", "attn_sinks", "g_branch", "w_out", "ln_gain", "ln_bias"]
    return (loss.reshape(()), gx.reshape(B_LOC, SEQ, D_MODEL), *[out[n][k] for k in range(4) for n in names])
```

```python
import jax
import jax.numpy as jnp
import numpy as np
from jax import lax
from jax.experimental import pallas as pl
from jax.experimental.pallas import tpu as pltpu

F32, BF16 = jnp.float32, jnp.bfloat16

D_MODEL = 1024
SEQ = 2048
B_LOC = 2
T = B_LOC * SEQ
BLK = 128
MEM_LEN = 256
W_A, W_KV_A, W_B, W_C, D_MIX = 512, 128, 256, 256, 1024
D_IN = 2816
O_QA, O_KA, O_VA, O_QB, O_KB, O_VB, O_QC, O_Z = 0, 512, 640, 768, 1024, 1280, 1536, 1792
ROPE_THETA = 10000.0
LN_EPS = 1e-5
RMS_EPS = 1e-6
ALPHA = 2.0 ** 0.25
QK_SCALE = 0.125
N_CHIP = 4
SH_IN, SH_OUT, SH_MEM = D_IN // N_CHIP, D_MIX // N_CHIP, D_MODEL // N_CHIP
NEG = -1e30
ADAM_LR, ADAM_B1, ADAM_B2, ADAM_EPS, ADAM_WD, ADAM_STEP = 0.001, 0.9, 0.999, 1e-08, 0.01, 10
SV_W = 3072
MESH = pl.DeviceIdType.MESH

NN = ((1,), (0,))
NT = ((1,), (1,))
TN = ((0,), (0,))


def _dot(a, b, dims):
    return lax.dot_general(a, b, (dims, ((), ())), preferred_element_type=F32)


def _cp(sem=None, vmem_mb=None):
    kw = {}
    if sem is not None:
        kw["dimension_semantics"] = sem
    if vmem_mb is not None:
        kw["vmem_limit_bytes"] = vmem_mb * 1024 * 1024
    return pltpu.CompilerParams(**kw)


def _sds(shape, dtype):
    return pltpu.HBM(shape, dtype)


def _vm_sds(shape, dtype):
    return jax.ShapeDtypeStruct(shape, dtype)


def _pin(*args):
    return [pltpu.with_memory_space_constraint(a, pltpu.HBM) for a in args]


def _full(shape):
    n = len(shape)
    return pl.BlockSpec(shape, lambda *_: (0,) * n)


def _shard_rows(ref, n, chip, half):
    start = pl.multiple_of((2 * chip[0] + chip[1]) * n + half * (n // 2), 16)
    return ref.at[pl.ds(start, n // 2), :]


def _gather_weights(win_sh, wout_sh, wmem_sh):
    half, piece = SH_IN // 2, SH_IN // 4
    shards = ((SH_IN, D_MODEL), (SH_OUT, D_MODEL), (SH_MEM, 2 * W_C))

    def body(a_ref, b_ref, c_ref, oa_ref, ob_ref, oc_ref, raw_a, raw_b, raw_c, own_a, own_b, own_c,
             load_sem, store_sem, ici_send, ici_recv, d2d_send, d2d_recv):
        x, y, c = lax.axis_index("x"), lax.axis_index("y"), lax.axis_index("c")
        me, sibling = (x, y, c), (x, y, 1 - c)
        xn, yn, dg = (1 - x, y), (x, 1 - y), (1 - x, 1 - y)
        srcs, raws = (a_ref, b_ref, c_ref), (raw_a, raw_b, raw_c)
        owns, outs = (own_a, own_b, own_c), (oa_ref, ob_ref, oc_ref)
        loads = [pltpu.make_async_copy(srcs[a], raws[a], load_sem.at[a]) for a in range(3)]
        for cp in loads:
            cp.start()

        def rows(chip, hf, q):
            start = pl.multiple_of((2 * chip[0] + chip[1]) * SH_IN + hf * half + q * piece, 16)
            return oa_ref.at[pl.ds(start, piece), :]

        def copy(sems, k, chip, hf, q, to, src=None):
            blk = rows(chip, hf, q)
            return pltpu.make_async_remote_copy(
                src_ref=blk if src is None else src, dst_ref=blk, send_sem=sems[0].at[k], recv_sem=sems[1].at[k],
                device_id=to, device_id_type=MESH)

        def my_piece(q):
            return own_a.at[pl.ds(pl.multiple_of(c * half + q * piece, 16), piece), :]

        ici, d2d = (ici_send, ici_recv), (d2d_send, d2d_recv)
        stores, direct = [], []
        for a, (n, _) in enumerate(shards):
            loads[a].wait()
            owns[a][...] = raws[a][...].astype(BF16)
            mine = pl.ds(pl.multiple_of((2 * x + y) * n, 16), n)
            stores.append(pltpu.make_async_copy(owns[a], outs[a].at[mine, :], store_sem.at[a]))
            stores[-1].start()
            if a == 0:
                direct = [copy(ici, 0, (x, y), c, 0, (*xn, c), my_piece(0)),
                          copy(ici, 1, (x, y), c, 1, (*xn, c), my_piece(1)),
                          copy(ici, 3, (x, y), c, 0, (*yn, c), my_piece(0)),
                          copy(ici, 4, (x, y), c, 1, (*yn, c), my_piece(1))]
                for cp in direct:
                    cp.start()
        arrivals = [(0, xn, 0), (1, xn, 1), (3, yn, 0), (4, yn, 1), (2, dg, 1), (5, dg, 0)]
        passed = []
        for k, chip, q in arrivals:
            copy(ici, k, chip, c, q, me).wait_recv()
            if k == 0:
                passed.append(copy(ici, 5, xn, c, 0, (*yn, c)))
                passed[-1].start()
            if k == 4:
                passed.append(copy(ici, 2, yn, c, 1, (*xn, c)))
                passed[-1].start()
            passed.append(copy(d2d, k, chip, c, q, sibling))
            passed[-1].start()
        for k, chip, q in arrivals:
            copy(d2d, k, chip, 1 - c, q, me).wait_recv()
        for cp in direct + passed:
            cp.wait_send()
        for cp in stores:
            cp.wait()

    hbm = pl.BlockSpec(memory_space=pl.ANY)
    return pl.pallas_call(
        body, name="gather_weights",
        out_shape=(_sds((D_IN, D_MODEL), BF16), _sds((D_MIX, D_MODEL), BF16), _sds((D_MODEL, 2 * W_C), BF16)),
        in_specs=[hbm, hbm, hbm], out_specs=(hbm, hbm, hbm),
        scratch_shapes=([pltpu.VMEM(sh, F32) for sh in shards] + [pltpu.VMEM(sh, BF16) for sh in shards]
                        + [pltpu.SemaphoreType.DMA((3,))] * 2 + [pltpu.SemaphoreType.DMA((6,))] * 4),
        compiler_params=_cp(vmem_mb=40),
    )(*_pin(win_sh, wout_sh, wmem_sh))


def _rope(t, cos, sa, sb, sign):
    w = t.shape[1]
    reps = w // 128
    c, a, b = (jnp.tile(v, (1, reps)) if reps > 1 else v for v in (cos, sa, sb))
    rot = pltpu.roll(t, w - 32, 1) * a + pltpu.roll(t, 32, 1) * b
    return t * c + rot if sign > 0 else t * c - rot


def _in_proj(x, winT, b_in, cos, sa, sb, wout_own, wmem_own):
    tm = 512
    spt = SEQ // tm
    n_steps = T // tm
    forward_step = n_steps // 2

    def body(x_ref, w_ref, b_ref, cos_ref, sa_ref, sb_ref, wo_in, wm_in,
             xb_ref, qa_ref, ka_ref, va_ref, bn_ref, b4_ref, b16_ref, qc_ref, z_ref, wo_ref, wm_ref,
             scr, ici_send, ici_recv, d2d_send, d2d_recv):
        i = pl.program_id(0)
        mx, my, mc = lax.axis_index("x"), lax.axis_index("y"), lax.axis_index("c")
        chips = [(1 - mx, my), (mx, 1 - my), (1 - mx, 1 - my)]
        full = ((wo_ref, SH_OUT), (wm_ref, SH_MEM))

        def copy(sems, a, j, chip_of_block, half, to):
            blk = _shard_rows(full[a][0], full[a][1], chip_of_block, half)
            return pltpu.make_async_remote_copy(
                src_ref=blk, dst_ref=blk, send_sem=sems[0].at[a, j], recv_sem=sems[1].at[a, j],
                device_id=to, device_id_type=MESH)

        ici, d2d = (ici_send, ici_recv), (d2d_send, d2d_recv)
        pairs = [(a, j, chip) for j, chip in enumerate(chips) for a in range(2)]

        @pl.when(i == 0)
        def _():
            for a, j, chip in pairs:
                copy(ici, a, j, (mx, my), mc, (*chip, mc)).start()

        @pl.when(i == forward_step)
        def _():
            for a, j, chip in pairs:
                copy(ici, a, j, chip, mc, (mx, my, mc)).wait_recv()
                copy(d2d, a, j, chip, mc, (mx, my, 1 - mc)).start()

        @pl.when(i == n_steps - 1)
        def _():
            for a, j, chip in pairs:
                copy(d2d, a, j, chip, 1 - mc, (mx, my, mc)).wait_recv()
            for a, j, chip in pairs:
                copy(ici, a, j, (mx, my), mc, (*chip, mc)).wait_send()
                copy(d2d, a, j, chip, mc, (mx, my, 1 - mc)).wait_send()

        xb = x_ref[...].astype(BF16)
        xb_ref[...] = xb
        cos_t, sa_t, sb_t = cos_ref[...], sa_ref[...], sb_ref[...]

        def proj(r0, n):
            return _dot(xb, w_ref[r0:r0 + n, :], NT) + b_ref[:, r0:r0 + n]

        def rope(t):
            return _rope(t, cos_t, sa_t, sb_t, +1)

        parts = (rope(proj(O_QB, W_B)) * QK_SCALE, rope(proj(O_KB, W_B)), proj(O_VB, W_B))
        for k, part in enumerate(parts):
            bn_ref[:, 256 * k:256 * (k + 1)] = part.astype(BF16)
            scr[2 * k] = part[:, :128]
            scr[2 * k + 1] = part[:, 128:]
        for j in range(6):
            lanes = slice(128 * j, 128 * (j + 1))
            for res in range(4):
                t = scr[j, pl.ds(res, tm // 4, stride=4), :]
                b4_ref[0, res, :, lanes] = t.astype(BF16)
                scr[6 + j, res * (tm // 4):(res + 1) * (tm // 4), :] = t
            for res in range(16):
                b16_ref[0, res, :, lanes] = scr[6 + j, pl.ds((res % 4) * (tm // 4) + res // 4, tm // 16, stride=4),
                                                :].astype(BF16)
        qa_ref[...] = (rope(proj(O_QA, W_A)) * QK_SCALE).astype(BF16)
        assert O_VA == O_KA + W_KV_A
        kv = proj(O_KA, 2 * W_KV_A)
        ka_ref[...] = rope(kv[:, :W_KV_A]).astype(BF16)
        va_ref[...] = kv[:, W_KV_A:].astype(BF16)
        qc_ref[...] = (proj(O_QC, W_C) * QK_SCALE).astype(BF16)
        z_ref[...] = proj(O_Z, D_MIX).astype(BF16)

    tok = lambda w: pl.BlockSpec((tm, w), lambda i: (i, 0))
    tab = pl.BlockSpec((tm, 128), lambda i: (i % spt, 0))
    hbm = pl.BlockSpec(memory_space=pl.ANY)
    return pl.pallas_call(
        body, name="in_proj", grid=(n_steps,),
        in_specs=[tok(D_MODEL), _full((D_IN, D_MODEL)), _full((1, D_IN)), tab, tab, tab, hbm, hbm],
        out_specs=(tok(D_MODEL), tok(W_A), tok(W_KV_A), tok(W_KV_A), tok(768),
                   pl.BlockSpec((1, 4, tm // 4, 768), lambda i: (i // spt, 0, i % spt, 0)),
                   pl.BlockSpec((1, 16, tm // 16, 768), lambda i: (i // spt, 0, i % spt, 0)),
                   tok(W_C), tok(D_MIX), hbm, hbm),
        out_shape=(_sds((T, D_MODEL), BF16), _sds((T, W_A), BF16), _sds((T, W_KV_A), BF16), _sds((T, W_KV_A), BF16),
                   _sds((T, 768), BF16), _sds((B_LOC, 4, SEQ // 4, 768), BF16), _sds((B_LOC, 16, SEQ // 16, 768), BF16),
                   _sds((T, W_C), BF16), _sds((T, D_MIX), BF16),
                   _sds((D_MIX, D_MODEL), BF16), _sds((D_MODEL, 2 * W_C), BF16)),
        input_output_aliases={6: 9, 7: 10},
        scratch_shapes=[pltpu.VMEM((12, tm, 128), F32)] + [pltpu.SemaphoreType.DMA((2, 3))] * 4,
        compiler_params=_cp(("arbitrary",), vmem_mb=48),
    )(*_pin(x, winT, b_in, cos, sa, sb, wout_own, wmem_own))


def _mem_kv(mem, wmem):
    def body(m_ref, w_ref, mb_ref, kv_ref):
        mb = m_ref[...].astype(BF16)
        mb_ref[...] = mb
        kv_ref[...] = _dot(mb, w_ref[...], NN).astype(BF16)

    n = B_LOC * MEM_LEN
    return pl.pallas_call(
        body, name="mem_kv",
        out_shape=(_sds((n, D_MODEL), BF16), _sds((n, 2 * W_C), BF16)),
    )(*_pin(mem, wmem))


class _Part:
    def __init__(self, body, args, in_specs, out_specs, out_shape, scratch=()):
        self.body, self.args, self.in_specs, self.out_specs, self.out_shape = body, args, in_specs, out_specs, out_shape
        self.scratch = list(scratch)


def _run_parts(name, parts, semantics, vmem_mb):
    n_in = [len(p.args) for p in parts]
    n_out = [len(p.out_shape) for p in parts]
    n_scr = [len(p.scratch) for p in parts]

    def body(*refs):
        ins, outs, scr = refs[:sum(n_in)], refs[sum(n_in):sum(n_in) + sum(n_out)], refs[sum(n_in) + sum(n_out):]
        i0 = o0 = s0 = 0
        for p, ni, no, ns in zip(parts, n_in, n_out, n_scr):
            p.body(*ins[i0:i0 + ni], *outs[o0:o0 + no], *scr[s0:s0 + ns])
            i0, o0, s0 = i0 + ni, o0 + no, s0 + ns

    res = pl.pallas_call(
        body, name=name, grid=(T // QR,),
        in_specs=[sp for p in parts for sp in p.in_specs], out_specs=tuple(sp for p in parts for sp in p.out_specs),
        out_shape=tuple(sh for p in parts for sh in p.out_shape),
        scratch_shapes=[sc for p in parts for sc in p.scratch],
        compiler_params=_cp((semantics,), vmem_mb=vmem_mb),
    )(*_pin(*[a for p in parts for a in p.args]))
    out, o0 = [], 0
    for no in n_out:
        out.append(tuple(res[o0:o0 + no]))
        o0 += no
    return out


QB = 8
QR = QB * BLK


def _lane_lo():
    return lax.broadcasted_iota(jnp.int32, (1, 128), 1) < 64


def _dup_head(k2, hk, lo):
    kf = k2.astype(F32)
    r = pltpu.roll(kf, 64, 1)
    return (jnp.where(lo, kf, r) if hk == 0 else jnp.where(lo, r, kf)).astype(BF16)


def _stack_heads(pairs, lo):
    parts = []
    for x2 in pairs:
        z = jnp.zeros_like(x2)
        parts += [jnp.where(lo, x2, z), jnp.where(lo, z, x2)]
    return jnp.concatenate(parts, axis=0)


def _prev_mode(kind, nb, j):
    if kind == "mem" or nb == 1:
        return "no"
    if nb <= QB:
        return "yes" if j % nb else "no"
    return "yes" if j else "dyn"


class _Attn:
    def __init__(self, kind, nb, max_dist, gqa, qw, kvw, qcb, kcb, vcb):
        self.kind, self.nb, self.gqa, self.qw, self.kvw = kind, nb, gqa, qw, kvw
        npairs = qw // 128
        self.groups = ([(hk, [2 * hk, 2 * hk + 1]) for hk in range(npairs // 2)] if gqa
                       else [(p, [p]) for p in range(npairs)])
        self.nh = 2 * len(self.groups[0][1])
        self.cols = 128 * self.nh
        self.reach = BLK - max_dist
        self.ext_prev = kind == "band" and nb > QB
        self.q_spec = pl.BlockSpec((QR, qw), lambda g: (g, qcb))
        self.row_spec = pl.BlockSpec((QR, qw), lambda g: (g, 0))
        self.stat_spec = pl.BlockSpec((QR, 128), lambda g: (g, 0))
        if kind == "mem":
            per = SEQ // QR
            self.kv_specs = [pl.BlockSpec((MEM_LEN, kvw), lambda g: (g // per, kcb)),
                             pl.BlockSpec((MEM_LEN, kvw), lambda g: (g // per, vcb))]
        else:
            self.kv_specs = [pl.BlockSpec((QR, kvw), lambda g: (g, kcb)), pl.BlockSpec((QR, kvw), lambda g: (g, vcb))]
            if self.ext_prev:
                self.kv_specs += [pl.BlockSpec((BLK, kvw), lambda g: (jnp.maximum(g * QB - 1, 0), kcb)),
                                  pl.BlockSpec((BLK, kvw), lambda g: (jnp.maximum(g * QB - 1, 0), vcb))]

    def masks(self):
        if self.kind == "mem":
            return None
        kj = lax.broadcasted_iota(jnp.int32, (2 * BLK, self.cols), 0)
        qi = lax.broadcasted_iota(jnp.int32, (2 * BLK, self.cols), 1) & (BLK - 1)
        both = jnp.logical_and(kj >= qi + self.reach, kj <= qi + BLK)
        return kj, qi, self.as_mask(both), self.as_mask(kj[:BLK] <= qi[:BLK])

    def as_mask(self, in_reach):
        return jnp.where(in_reach, 0.0, NEG) if self.gqa else in_reach

    def hide(self, s, mask):
        return s + mask if self.gqa else jnp.where(mask, s, NEG)

    def keys(self, j, gi, kc_ref, vc_ref, kp_ref, vp_ref, lo, kq, g, dup):
        def kv(k_ref, v_ref, r):
            if self.gqa:
                return _dup_head(k_ref[r, :], gi, lo), _dup_head(v_ref[r, :], gi, lo)
            sl = slice(128 * gi, 128 * (gi + 1))
            return k_ref[r, sl], v_ref[r, sl]

        def blocks(b0, b1):
            if not self.gqa:
                return kv(kc_ref, vc_ref, slice(BLK * b0, BLK * b1))
            for b in range(b0, b1):
                if (b, gi) not in dup:
                    dup[b, gi] = kv(kc_ref, vc_ref, slice(BLK * b, BLK * (b + 1)))
            ks, vs = zip(*(dup[b, gi] for b in range(b0, b1)))
            return jnp.concatenate(ks, axis=0), jnp.concatenate(vs, axis=0)

        if self.kind == "mem":
            key0 = pl.multiple_of((g // (SEQ // QR)) * MEM_LEN, MEM_LEN)
            return (*kv(kc_ref, vc_ref, slice(None)), None, [(0, MEM_LEN, key0)])
        kj, qi, both, cur = kq
        row0 = g * QR + BLK * j
        mode = _prev_mode(self.kind, self.nb, j)
        if mode == "no":
            return (*blocks(j, j + 1), cur, [(0, BLK, pl.multiple_of(row0, BLK))])
        if mode == "yes":
            return (*blocks(j - 1, j + 1), both, [(0, 2 * BLK, pl.multiple_of(row0 - BLK, BLK))])
        has_prev = ((g * QB) % self.nb) > 0
        hp = has_prev.astype(jnp.int32)
        mask = self.as_mask(jnp.logical_and(kj >= qi * hp + (self.reach * hp + BLK * (1 - hp)), kj <= qi + BLK))
        kp, vp = kv(kp_ref, vp_ref, slice(None))
        kc, vc = blocks(0, 1)
        return (jnp.concatenate([kp, kc], axis=0), jnp.concatenate([vp, vc], axis=0), mask,
                [(0, BLK, pl.multiple_of(jnp.maximum(row0 - BLK, 0), BLK)), (BLK, BLK, pl.multiple_of(row0, BLK))])


def _attn_fwd(q, qcb, qw, k, kcb, v, vcb, kvw, *, kind, nb=1, max_dist=BLK, gqa=False, sinks=None):
    a = _Attn(kind, nb, max_dist, gqa, qw, kvw, qcb, kcb, vcb)

    def body(*refs):
        it = iter(refs)
        q_ref, kc_ref, vc_ref = next(it), next(it), next(it)
        kp_ref, vp_ref = (next(it), next(it)) if a.ext_prev else (None, None)
        sink_ref = next(it) if sinks is not None else None
        o_ref, lse_ref = next(it), next(it)
        g = pl.program_id(0)
        lo = _lane_lo()
        top = lax.broadcasted_iota(jnp.int32, (128, 1), 0) < 64
        rid = lax.broadcasted_iota(jnp.int32, (8, 128), 0)
        kq, dup = a.masks(), {}
        stats = {}

        def scores(j, gi, pairs):
            rows = slice(BLK * j, BLK * (j + 1))
            qs = _stack_heads([q_ref[rows, 128 * p:128 * (p + 1)] for p in pairs], lo)
            kk, vv, mask, _ = a.keys(j, gi, kc_ref, vc_ref, kp_ref, vp_ref, lo, kq, g, dup)
            pieces = [slice(r0, r0 + BLK) for r0 in range(0, kk.shape[0], BLK)]
            return dict(j=j, gi=gi, pairs=pairs, rows=rows, vv=vv, mask=mask, pieces=pieces,
                        ss=[_dot(kk[r], qs, NT) for r in pieces])

        def softmax(c):
            gi, mask = c["gi"], c["mask"]
            ss = [s if mask is None else a.hide(s, mask[r]) for r, s in zip(c["pieces"], c.pop("ss"))]
            m = jnp.max(ss[0], axis=0, keepdims=True)
            for s in ss[1:]:
                m = jnp.maximum(m, jnp.max(s, axis=0, keepdims=True))
            if sink_ref is not None:
                sk = jnp.concatenate([jnp.full((1, 128), sink_ref[0, a.nh * gi + i], F32) for i in range(a.nh)], axis=1)
                m = jnp.maximum(m, sk)
            ps = [jnp.exp(s - m) for s in ss]
            l = sum(jnp.sum(p, axis=0, keepdims=True) for p in ps)
            if sink_ref is not None:
                l = l + jnp.exp(sk - m)
            c["ps"] = [p.astype(BF16) for p in ps]
            c["l"], c["lse"] = l, m + jnp.log(l)

        def outputs(c):
            j, gi, rows = c["j"], c["gi"], c["rows"]
            ot = sum(_dot(c["vv"][r], p, TN) for r, p in zip(c["pieces"], c["ps"]))
            ot = ot * pl.reciprocal(c["l"], approx=True)
            for i, p in enumerate(c["pairs"]):
                o2t = jnp.where(top, ot[:, 256 * i:256 * i + 128], ot[:, 256 * i + 128:256 * i + 256])
                o_ref[rows, 128 * p:128 * (p + 1)] = o2t.T.astype(BF16)
            stat = stats.get(j, jnp.zeros((8, 128), F32))
            for i in range(a.nh):
                stat = jnp.where(rid == a.nh * gi + i, c["lse"][:, 128 * i:128 * (i + 1)], stat)
            stats[j] = stat
            if gi == a.groups[-1][0]:
                lse_ref[rows, :] = jnp.concatenate([stats.pop(j), jnp.zeros((120, 128), F32)], axis=0).T

        chains = [(j, gi, pairs) for j in range(QB) for gi, pairs in a.groups]
        live = {}
        for t in range(len(chains) + 2):
            if t < len(chains):
                live[t] = scores(*chains[t])
            if 0 <= t - 1 < len(chains):
                softmax(live[t - 1])
            if 0 <= t - 2 < len(chains):
                outputs(live.pop(t - 2))


    args = [q, k, v] + ([k, v] if a.ext_prev else [])
    in_specs = [a.q_spec] + a.kv_specs
    if sinks is not None:
        args.append(sinks)
        in_specs.append(pl.BlockSpec(memory_space=pltpu.SMEM))
    return _Part(body, args, in_specs, [a.row_spec, a.stat_spec], [_sds((T, qw), BF16), _sds((T, 128), F32)])


def _attn_bwd(q, qcb, qw, k, kcb, v, vcb, kvw, do, lse, dl, *, kind, nb=1, max_dist=BLK, gqa=False, sinkv=None,
              mem_in=None):
    a = _Attn(kind, nb, max_dist, gqa, qw, kvw, qcb, kcb, vcb)

    def body(*refs):
        it = iter(refs)
        q_ref, kc_ref, vc_ref = next(it), next(it), next(it)
        kp_ref, vp_ref = (next(it), next(it)) if a.ext_prev else (None, None)
        do_ref, lse_ref, dl_ref = next(it), next(it), next(it)
        sinkv_ref = next(it) if sinkv is not None else None
        mem_ref = next(it) if kind == "mem" else None
        dq_ref = next(it)
        if kind == "mem":
            gmem_ref = next(it)
        else:
            dk_out, dv_out = next(it), next(it)
        dsink_ref = next(it) if sinkv is not None else None
        if kind != "mem":
            dk_ref, dv_ref, stage_k, stage_v, flush_sem = next(it), next(it), next(it), next(it), next(it)
        else:
            dkv_ref = next(it)
        g = pl.program_id(0)
        lo = _lane_lo()
        top = lax.broadcasted_iota(jnp.int32, (128, 1), 0) < 64

        @pl.when(g == 0)
        def _():
            if kind == "mem":
                dkv_ref[...] = jnp.zeros_like(dkv_ref)
            else:
                dk_ref[...] = jnp.zeros_like(dk_ref)
                dv_ref[...] = jnp.zeros_like(dv_ref)
            if dsink_ref is not None:
                dsink_ref[...] = jnp.zeros_like(dsink_ref)

        kq, dup = a.masks(), {}
        stats_t = {}

        def first_matmuls(j, gi, pairs):
            rows = slice(BLK * j, BLK * (j + 1))
            if j not in stats_t:
                stats_t[j] = (lse_ref[rows, :].T, dl_ref[rows, :].T)
            lse_t, dl_t = stats_t[j]
            heads = [a.nh * gi + i for i in range(a.nh)]
            c = dict(rows=rows, gi=gi, pairs=pairs)
            c["qs"] = _stack_heads([q_ref[rows, 128 * p:128 * (p + 1)] for p in pairs], lo)
            c["dos"] = _stack_heads([do_ref[rows, 128 * p:128 * (p + 1)] for p in pairs], lo)
            c["lse_row"] = jnp.concatenate([lse_t[h:h + 1, :] for h in heads], axis=1)
            c["dl_row"] = jnp.concatenate([dl_t[h:h + 1, :] for h in heads], axis=1)
            c["kk"], vv, c["mask"], c["dests"] = a.keys(j, gi, kc_ref, vc_ref, kp_ref, vp_ref, lo, kq, g, dup)
            c["s"] = _dot(c["kk"], c["qs"], NT)
            c["dp"] = _dot(vv, c["dos"], NT)
            return c

        def elementwise(c):
            s = c.pop("s")
            if c["mask"] is not None:
                s = a.hide(s, c["mask"])
            p = jnp.exp(s - c["lse_row"])
            c["ds"] = (p * (c.pop("dp") - c["dl_row"])).astype(BF16)
            c["p"] = p.astype(BF16)

        def last_matmuls(c):
            gi, rows = c["gi"], c["rows"]
            dqt = _dot(c["kk"], c["ds"], TN)
            ck = _dot(c["ds"], c["qs"], NN)
            cv = _dot(c["p"], c["dos"], NN)
            if gqa:
                sel = lo if gi == 0 else jnp.logical_not(lo)
                ck = jnp.where(sel, ck + pltpu.roll(ck, 64, 1), 0.0)
                cv = jnp.where(sel, cv + pltpu.roll(cv, 64, 1), 0.0)
                kcols = slice(0, 128)
            else:
                kcols = slice(128 * gi, 128 * (gi + 1))
            for r0, nr, key0 in c["dests"]:
                krows = pl.ds(key0, nr)
                if kind == "mem":
                    dkv_ref[krows, kcols] += ck[r0:r0 + nr]
                    dkv_ref[krows, slice(kvw + kcols.start, kvw + kcols.stop)] += cv[r0:r0 + nr]
                else:
                    dk_ref[krows, kcols] += ck[r0:r0 + nr]
                    dv_ref[krows, kcols] += cv[r0:r0 + nr]
            for i, p in enumerate(c["pairs"]):
                dq2t = jnp.where(top, dqt[:, 256 * i:256 * i + 128], dqt[:, 256 * i + 128:256 * i + 256])
                dq_ref[rows, 128 * p:128 * (p + 1)] = dq2t.T.astype(BF16)

        chains = [(j, gi, pairs) for j in range(QB) for gi, pairs in a.groups]
        live = {}
        for t in range(len(chains) + 2):
            if t < len(chains):
                live[t] = first_matmuls(*chains[t])
            if 0 <= t - 1 < len(chains):
                elementwise(live[t - 1])
            if 0 <= t - 2 < len(chains):
                last_matmuls(live.pop(t - 2))
        if dsink_ref is not None:
            ps = jnp.exp(sinkv_ref[...] - lse_ref[...]) * dl_ref[...]
            dsink_ref[...] += jnp.sum(ps, axis=0, keepdims=True)
        if kind == "mem":
            @pl.when(g == T // QR - 1)
            def _():
                gmem_ref[...] = _dot(mem_ref[...], dkv_ref[...].astype(BF16), TN)
        else:
            n_steps = T // QR

            def flush(step):
                rows = pl.ds(pl.multiple_of(step * QR, QR), QR)
                out = []
                for acc, stage, dst, i in ((dk_ref, stage_k, dk_out, 0), (dv_ref, stage_v, dv_out, 1)):
                    stage[...] = acc[rows, :].astype(BF16)
                    out.append(pltpu.make_async_copy(stage, dst.at[rows, :], flush_sem.at[i]))
                return out

            def flushed(step):
                rows = pl.ds(pl.multiple_of(step * QR, QR), QR)
                return [pltpu.make_async_copy(stage, dst.at[rows, :], flush_sem.at[i])
                        for stage, dst, i in ((stage_k, dk_out, 0), (stage_v, dv_out, 1))]

            @pl.when(g >= 2)
            def _():
                for cp in flushed(g - 2):
                    cp.wait()

            @pl.when(g >= 1)
            def _():
                for cp in flush(g - 1):
                    cp.start()

            @pl.when(g == n_steps - 1)
            def _():
                for cp in flushed(g - 1):
                    cp.wait()
                for cp in flush(g):
                    cp.start()
                for cp in flushed(g):
                    cp.wait()

    args = [q, k, v] + ([k, v] if a.ext_prev else []) + [do, lse, dl]
    in_specs = [a.q_spec] + a.kv_specs + [a.row_spec, a.stat_spec, a.stat_spec]
    if sinkv is not None:
        args.append(sinkv)
        in_specs.append(_full((1, 128)))
    if kind == "mem":
        args.append(mem_in)
        in_specs.append(pl.BlockSpec(mem_in.shape, lambda g: (0, 0), pipeline_mode=pl.Buffered(1)))
    out_shape = [_sds((T, qw), BF16)]
    out_specs = [a.row_spec]
    scratch = []
    if kind == "mem":
        out_shape.append(_sds((D_MODEL, 2 * kvw), F32))
        out_specs.append(pl.BlockSpec((D_MODEL, 2 * kvw), lambda g: (0, 0), pipeline_mode=pl.Buffered(1)))
        scratch = [pltpu.VMEM((B_LOC * MEM_LEN, 2 * kvw), F32)]
    else:
        out_shape += [_sds((T, kvw), BF16)] * 2
        out_specs += [pl.BlockSpec(memory_space=pl.ANY)] * 2
        scratch = [pltpu.VMEM((T, kvw), F32)] * 2 + [pltpu.VMEM((QR, kvw), BF16)] * 2 + [pltpu.SemaphoreType.DMA((2,))]
    if sinkv is not None:
        out_shape.append(_sds((1, 128), F32))
        out_specs.append(_full((1, 128)))
    return _Part(body, args, in_specs, out_specs, out_shape, scratch)


def _dot2(v, w_ref):
    hi = v.astype(BF16)
    lo = (v - hi.astype(F32)).astype(BF16)
    return _dot(hi, w_ref[...], NN) + _dot(lo, w_ref[...], NN)


def _middle(oa, o1, l1, o4, l4, o16, l16, oc, z, x, tgt, g_br, ln_g, ln_b, wout, spread4, gather4, gather8):
    tm = 512
    spt = SEQ // tm

    def body(oa_ref, o1_ref, l1_ref, o4_ref, l4_ref, o16_ref, l16_ref, oc_ref, z_ref, x_ref, t_ref,
             g_ref, lg_ref, lb_ref, w_ref, sp4_ref, ga4_ref, ga8_ref,
             du_ref, dz_ref, doa_ref, dla_ref,
             dobn_ref, lsen_ref, dlbn_ref, dob4_ref, lse4_ref, dlb4_ref, dob16_ref, lse16_ref, dlb16_ref,
             doc_ref, dlc_ref, acc_ref, gout_ref, scr):
        i = pl.program_id(0)

        @pl.when(i == 0)
        def _():
            acc_ref[...] = jnp.zeros_like(acc_ref)
            gout_ref[...] = jnp.zeros_like(gout_ref)

        q = tm // 4
        for res in range(16):
            rows = pl.ds((res % 4) * q + res // 4, tm // 16, stride=4)
            for j in range(2):
                scr[6 + j, rows, :] = o16_ref[0, res, :, 128 * j:128 * (j + 1)].astype(F32)
            scr[8, rows, :] = l16_ref[0, res]
        for res in range(4):
            rows, blk = pl.ds(res, q, stride=4), slice(res * q, (res + 1) * q)
            for j in range(2):
                scr[j, rows, :] = o4_ref[0, res, :, 128 * j:128 * (j + 1)].astype(F32)
                scr[3 + j, rows, :] = scr[6 + j, blk, :]
            scr[2, rows, :] = l4_ref[0, res]
            scr[5, rows, :] = scr[8, blk, :]
        inv_d = 1.0 / D_MODEL
        gb, lg, lb = g_ref[...], lg_ref[...], lb_ref[...]

        def rms(o):
            r = lax.rsqrt(jnp.sum(o * o, axis=1, keepdims=True) * (1.0 / o.shape[1]) + RMS_EPS)
            return o * r, r

        def rms_bwd(dn_, n_, r):
            return r * (dn_ - n_ * (jnp.sum(dn_ * n_, axis=1, keepdims=True) * (1.0 / n_.shape[1])))

        def forward(rs):
            o4v = jnp.concatenate([scr[0, rs, :], scr[1, rs, :]], axis=1)
            o16v = jnp.concatenate([scr[3, rs, :], scr[4, rs, :]], axis=1)
            l1v, l4v, l16v = l1_ref[rs, :], scr[2, rs, :], scr[5, rs, :]
            mx = jnp.maximum(jnp.maximum(l1v, l4v), l16v)
            e1, e4, e16 = jnp.exp(l1v - mx), jnp.exp(l4v - mx), jnp.exp(l16v - mx)
            ssum = e1 + e4 + e16
            inv = 1.0 / ssum
            c = dict(rs=rs, lse_b=mx + jnp.log(ssum))
            c["ob"] = (_dot2(e1 * inv, sp4_ref) * o1_ref[rs, :].astype(F32) + _dot2(e4 * inv, sp4_ref) * o4v
                       + _dot2(e16 * inv, sp4_ref) * o16v)
            c["oa"], c["oc"] = oa_ref[rs, :].astype(F32), oc_ref[rs, :].astype(F32)
            na, c["ra"] = rms(c["oa"])
            nb_, c["rb"] = rms(c["ob"])
            nc, c["rc"] = rms(c["oc"])
            c["n"] = jnp.concatenate([na, nb_, nc], axis=1)
            c["zf"] = z_ref[rs, :].astype(F32)
            c["sig"] = 1.0 / (1.0 + jnp.exp(-c["zf"]))
            c["sz"] = c["zf"] * c["sig"]
            c["yb"] = (c["n"] * gb * c["sz"]).astype(BF16)
            c["y2"] = _dot(c["yb"], w_ref[...], NN)
            return c

        def norm(c):
            rs = c["rs"]
            u = ALPHA * x_ref[rs, :] + c.pop("y2")
            mu = jnp.sum(u, axis=1, keepdims=True) * inv_d
            uc = u - mu
            rstd = lax.rsqrt(jnp.sum(uc * uc, axis=1, keepdims=True) * inv_d + LN_EPS)
            xh = uc * rstd
            diff = xh * lg + lb - t_ref[rs, :]
            acc_ref[0:1, :] += jnp.sum(diff * diff, axis=0, keepdims=True) * (0.5 * inv_d)
            dout = diff * inv_d
            acc_ref[2:3, :] += jnp.sum(dout * xh, axis=0, keepdims=True)
            acc_ref[3:4, :] += jnp.sum(dout, axis=0, keepdims=True)
            dxh = dout * lg
            du = rstd * (dxh - jnp.sum(dxh, axis=1, keepdims=True) * inv_d
                         - xh * (jnp.sum(dxh * xh, axis=1, keepdims=True) * inv_d))
            dub = du.astype(BF16)
            du_ref[rs, :] = dub
            c["dy"] = _dot(dub, w_ref[...], NT)
            gout_ref[...] += _dot(c.pop("yb"), dub, TN)

        def backward(c):
            rs, n, dy, zf, sig = c["rs"], c["n"], c["dy"], c["zf"], c["sig"]
            t1 = dy * c["sz"]
            acc_ref[1:2, :] += jnp.sum(t1 * n, axis=0, keepdims=True)
            dn = t1 * gb
            dz_ref[rs, :] = (dy * n * gb * (sig * (1.0 + zf * (1.0 - sig)))).astype(BF16)
            doa = rms_bwd(dn[:, :W_A], n[:, :W_A], c["ra"])
            dob = rms_bwd(dn[:, W_A:W_A + W_B], n[:, W_A:W_A + W_B], c["rb"])
            doc = rms_bwd(dn[:, W_A + W_B:], n[:, W_A + W_B:], c["rc"])
            doa_ref[rs, :] = doa.astype(BF16)
            dla_ref[rs, :] = _dot2(doa * c["oa"], ga8_ref)
            doc_ref[rs, :] = doc.astype(BF16)
            dlc_ref[rs, :] = _dot2(doc * c["oc"], ga4_ref)
            dobn_ref[rs, :] = dob.astype(BF16)
            lsen_ref[rs, :] = c["lse_b"]
            dlbn_ref[rs, :] = _dot2(dob * c["ob"], ga4_ref)
            scr[0, rs, :] = dob[:, :128]
            scr[1, rs, :] = dob[:, 128:]

        halves = [slice(h * (tm // 2), (h + 1) * (tm // 2)) for h in range(2)]
        live = {}
        for t in range(len(halves) + 2):
            if t < len(halves):
                live[t] = forward(halves[t])
            if 0 <= t - 1 < len(halves):
                norm(live[t - 1])
            if 0 <= t - 2 < len(halves):
                backward(live.pop(t - 2))
        for j in range(2):
            sl = slice(128 * j, 128 * (j + 1))
            for res in range(4):
                t = scr[j, pl.ds(res, q, stride=4), :]
                dob4_ref[0, res, :, sl] = t.astype(BF16)
                scr[6 + j, res * q:(res + 1) * q, :] = t
            for res in range(16):
                dob16_ref[0, res, :, sl] = scr[6 + j, pl.ds((res % 4) * q + res // 4, tm // 16, stride=4),
                                               :].astype(BF16)
        for res in range(4):
            rows = pl.ds(res, q, stride=4)
            lse4_ref[0, res] = lsen_ref[rows, :]
            dlb4_ref[0, res] = dlbn_ref[rows, :]
        for res in range(16):
            rows = pl.ds(res // 4, tm // 16, stride=4)
            lse16_ref[0, res] = lse4_ref[0, res % 4, rows, :]
            dlb16_ref[0, res] = dlb4_ref[0, res % 4, rows, :]


    tok = lambda w: pl.BlockSpec((tm, w), lambda i: (i, 0))
    p4 = lambda w: pl.BlockSpec((1, 4, tm // 4, w), lambda i: (i // spt, 0, i % spt, 0))
    p16 = lambda w: pl.BlockSpec((1, 16, tm // 16, w), lambda i: (i // spt, 0, i % spt, 0))
    s4 = lambda w, dt: _sds((B_LOC, 4, SEQ // 4, w), dt)
    s16 = lambda w, dt: _sds((B_LOC, 16, SEQ // 16, w), dt)
    row = _full((1, D_MODEL))
    return pl.pallas_call(
        body, name="middle", grid=(T // tm,),
        in_specs=[tok(W_A), tok(W_B), tok(128), p4(W_B), p4(128), p16(W_B), p16(128), tok(W_C), tok(D_MIX),
                  tok(D_MODEL), tok(D_MODEL), row, row, row, _full((D_MIX, D_MODEL)),
                  _full((128, W_B)), _full((W_B, 128)), _full((W_A, 128))],
        out_specs=(tok(D_MODEL), tok(D_MIX), tok(W_A), tok(128),
                   tok(W_B), tok(128), tok(128), p4(W_B), p4(128), p4(128), p16(W_B), p16(128), p16(128),
                   tok(W_C), tok(128), _full((8, D_MODEL)), _full((D_MIX, D_MODEL))),
        out_shape=(_sds((T, D_MODEL), BF16), _sds((T, D_MIX), BF16),
                   _sds((T, W_A), BF16), _sds((T, 128), F32),
                   _sds((T, W_B), BF16), _sds((T, 128), F32), _sds((T, 128), F32),
                   s4(W_B, BF16), s4(128, F32), s4(128, F32), s16(W_B, BF16), s16(128, F32), s16(128, F32),
                   _sds((T, W_C), BF16), _sds((T, 128), F32), _sds((8, D_MODEL), F32),
                   _sds((D_MIX, D_MODEL), F32)),
        scratch_shapes=[pltpu.VMEM((9, tm, 128), F32)],
        compiler_params=_cp(("arbitrary",), vmem_mb=56),
    )(*_pin(oa, o1, l1, o4, l4, o16, l16, oc, z, x, tgt, g_br, ln_g, ln_b, wout, spread4, gather4, gather8))


class _ReduceScatter:
    def __init__(self, shapes):
        self.shapes = shapes

    def scratch_shapes(self):
        out = []
        for n, w in self.shapes:
            h, p = n // 2, n // 4
            out += [pltpu.VMEM((4, h, w), F32), pltpu.VMEM((4, h, w), F32), pltpu.VMEM((6, p, w), BF16),
                    pltpu.VMEM((6, p, w), BF16), pltpu.VMEM((2, p, w), F32), pltpu.VMEM((h, w), F32)]
        na = len(self.shapes)
        dma = pltpu.SemaphoreType.DMA
        return out + [dma((na, 4)), dma((na, 4)), dma((na, 4)), dma((na, 6)), dma((na, 6)), dma((na,)), dma((na,)),
                      dma((na,))]

    def bind(self, g_refs, r_refs, scratch):
        na = len(self.shapes)
        bufs = [scratch[6 * a:6 * a + 6] for a in range(na)]
        mine, sib, stage, land, keep, tot = (tuple(b[i] for b in bufs) for i in range(6))
        loc_sem, s1_send, s1_recv, s2_send, s2_recv, s3_send, s3_recv, st_sem = scratch[6 * na:6 * na + 8]
        x, y, c = lax.axis_index("x"), lax.axis_index("y"), lax.axis_index("c")
        me, sibling = (x, y, c), (x, y, 1 - c)
        xn, yn, dg = (1 - x, y), (x, 1 - y), (1 - x, 1 - y)
        idx = lambda chip: 2 * chip[0] + chip[1]
        my_chip = idx((x, y))
        order = [idx(xn), idx(dg), idx(yn), my_chip]

        def rows(a, k, half):
            n = self.shapes[a][0]
            return pl.ds(pl.multiple_of(k * n + half * (n // 2), 8), n // 2)

        def piece(a, q):
            p = self.shapes[a][0] // 4
            return slice(q * p, (q + 1) * p)

        def load(a, k):
            return pltpu.make_async_copy(g_refs[a].at[rows(a, k, c), :], mine[a].at[k], loc_sem.at[a, k])

        def s1(a, k, half):
            return pltpu.make_async_remote_copy(
                src_ref=g_refs[a].at[rows(a, k, half), :], dst_ref=sib[a].at[k],
                send_sem=s1_send.at[a, k], recv_sem=s1_recv.at[a, k], device_id=sibling, device_id_type=MESH)

        def s2(a, i, to):
            return pltpu.make_async_remote_copy(
                src_ref=stage[a].at[i], dst_ref=land[a].at[i], send_sem=s2_send.at[a, i], recv_sem=s2_recv.at[a, i],
                device_id=to, device_id_type=MESH)

        via = {0: xn, 1: xn, 2: yn, 3: yn, 4: yn, 5: xn}

        def s3(a, half, to):
            return pltpu.make_async_remote_copy(
                src_ref=tot[a], dst_ref=r_refs[a].at[rows(a, 0, half), :], send_sem=s3_send.at[a],
                recv_sem=s3_recv.at[a], device_id=to, device_id_type=MESH)

        def store(a):
            return pltpu.make_async_copy(tot[a], r_refs[a].at[rows(a, 0, c), :], st_sem.at[a])

        def start():
            for k in order:
                for a in range(na):
                    load(a, k).start()
                    s1(a, k, 1 - c).start()

        def chip_sum(a, k):
            load(a, k).wait()
            s1(a, k, c).wait_recv()
            return mine[a][k] + sib[a][k]

        def exchange():
            for a in range(na):
                P, Q = piece(a, 0), piece(a, 1)
                s_xn = chip_sum(a, idx(xn))
                stage[a][0] = s_xn[P].astype(BF16)
                keep[a][1] = s_xn[Q]
                s_dg = chip_sum(a, idx(dg))
                stage[a][1] = s_dg[P].astype(BF16)
                s2(a, 0, (*xn, c)).start()
                s2(a, 1, (*xn, c)).start()
                stage[a][3] = s_dg[Q].astype(BF16)
                s_yn = chip_sum(a, idx(yn))
                stage[a][2] = s_yn[Q].astype(BF16)
                keep[a][0] = s_yn[P]
                s2(a, 2, (*yn, c)).start()
                s2(a, 3, (*yn, c)).start()
                tot[a][...] = chip_sum(a, my_chip)

        def relay():
            for a in range(na):
                P, Q = piece(a, 0), piece(a, 1)
                s2(a, 1, me).wait_recv()
                stage[a][4] = (keep[a][0] + land[a][1].astype(F32)).astype(BF16)
                s2(a, 4, (*yn, c)).start()
                s2(a, 3, me).wait_recv()
                stage[a][5] = (keep[a][1] + land[a][3].astype(F32)).astype(BF16)
                s2(a, 5, (*xn, c)).start()
                s2(a, 0, me).wait_recv()
                tot[a][P, :] += land[a][0].astype(F32)
                s2(a, 2, me).wait_recv()
                tot[a][Q, :] += land[a][2].astype(F32)

        def finish():
            for a in range(na):
                P, Q = piece(a, 0), piece(a, 1)
                s2(a, 4, me).wait_recv()
                tot[a][P, :] += land[a][4].astype(F32)
                s2(a, 5, me).wait_recv()
                tot[a][Q, :] += land[a][5].astype(F32)
                s3(a, c, sibling).start()
                store(a).start()

        def drain():
            for a in range(na):
                s3(a, 1 - c, me).wait_recv()
                store(a).wait()
            for a in range(na):
                for k in order:
                    s1(a, k, 1 - c).wait_send()
                for i in range(6):
                    s2(a, i, (*via[i], c)).wait_send()
                s3(a, c, sibling).wait_send()

        return start, exchange, relay, finish, drain

    def part(self, grads, steps):
        def body(*refs):
            na = len(self.shapes)
            i = pl.program_id(0)
            for step, phase in zip(steps, self.bind(refs[:na], refs[na:2 * na], refs[2 * na:])):
                pl.when(i == step)(phase)

        hbm = pl.BlockSpec(memory_space=pl.ANY)
        return _Part(body, list(grads), [hbm] * len(grads), [hbm] * len(grads),
                     [_sds((n, w), F32) for n, w in self.shapes], self.scratch_shapes())


def _dh_dx(dqa, dka, dva, dqn, dkn, dvn, dq4, dk4, dv4, dq16, dk16, dv16, dqc, dz, du, xb, cos, sa, sb, winT):
    tm = 512
    spt = SEQ // tm

    def body(dqa_ref, dka_ref, dva_ref, dqn_ref, dkn_ref, dvn_ref, dq4_ref, dk4_ref, dv4_ref,
             dq16_ref, dk16_ref, dv16_ref, dqc_ref, dz_ref, du_ref, xb_ref, cos_ref, sa_ref, sb_ref, w_ref,
             gx_ref, db_ref, gin_ref, dh_ref, scr):
        i = pl.program_id(0)

        @pl.when(i == 0)
        def _():
            db_ref[...] = jnp.zeros_like(db_ref)
            gin_ref[...] = jnp.zeros_like(gin_ref)

        cos_t, sa_t, sb_t = cos_ref[...], sa_ref[...], sb_ref[...]

        def rope_t(t):
            return _rope(t, cos_t, sa_t, sb_t, -1)

        def put(r0, val):
            n = val.shape[1]
            dh_ref[:, r0:r0 + n] = val.astype(BF16)
            db_ref[:, r0:r0 + n] += jnp.sum(val, axis=0, keepdims=True)

        put(O_QA, rope_t(dqa_ref[...].astype(F32)) * QK_SCALE)
        put(O_KA, rope_t(dka_ref[...].astype(F32)))
        put(O_VA, dva_ref[...].astype(F32))
        put(O_QC, dqc_ref[...].astype(F32) * QK_SCALE)
        put(O_Z, dz_ref[...].astype(F32))
        for k, (n_ref, r4, r16) in enumerate(((dqn_ref, dq4_ref, dq16_ref), (dkn_ref, dk4_ref, dk16_ref),
                                               (dvn_ref, dv4_ref, dv16_ref))):
            for j in range(2):
                sl = slice(128 * j, 128 * (j + 1))
                a, q = 2 * k + j, tm // 4
                scr[a] = n_ref[:, sl].astype(F32)
                for res in range(16):
                    scr[6 + a, pl.ds((res % 4) * q + res // 4, tm // 16, stride=4), :] = r16[0, res, :, sl].astype(F32)
                for res in range(4):
                    scr[a, pl.ds(res, q, stride=4), :] += (scr[6 + a, res * q:(res + 1) * q, :]
                                                           + r4[0, res, :, sl].astype(F32))
        cat = lambda a: jnp.concatenate([scr[a], scr[a + 1]], axis=1)
        put(O_QB, rope_t(cat(0)) * QK_SCALE)
        put(O_KB, rope_t(cat(2)))
        put(O_VB, cat(4))
        gx_ref[...] = _dot(dh_ref[...], w_ref[...], NN) + ALPHA * du_ref[...].astype(F32)
        gin_ref[...] += _dot(dh_ref[...], xb_ref[...], TN)

    tok = lambda w: pl.BlockSpec((tm, w), lambda i: (i, 0))
    tab = pl.BlockSpec((tm, 128), lambda i: (i % spt, 0))
    p4 = pl.BlockSpec((1, 4, tm // 4, W_B), lambda i: (i // spt, 0, i % spt, 0))
    p16 = pl.BlockSpec((1, 16, tm // 16, W_B), lambda i: (i // spt, 0, i % spt, 0))
    once = lambda shape: pl.BlockSpec(shape, lambda i: (0, 0), pipeline_mode=pl.Buffered(1))
    return pl.pallas_call(
        body, name="dh_dx", grid=(T // tm,),
        in_specs=[tok(W_A), tok(W_KV_A), tok(W_KV_A), tok(W_B), tok(W_B), tok(W_B), p4, p4, p4, p16, p16, p16,
                  tok(W_C), tok(D_MIX), tok(D_MODEL), tok(D_MODEL), tab, tab, tab, once((D_IN, D_MODEL))],
        out_specs=(tok(D_MODEL), _full((1, D_IN)), once((D_IN, D_MODEL))),
        out_shape=(_sds((T, D_MODEL), F32), _sds((1, D_IN), F32), _sds((D_IN, D_MODEL), F32)),
        scratch_shapes=[pltpu.VMEM((tm, D_IN), BF16), pltpu.VMEM((12, tm, 128), F32)],
        compiler_params=_cp(("arbitrary",), vmem_mb=56),
    )(*_pin(dqa, dka, dva, dqn, dkn, dvn, dq4, dk4, dv4, dq16, dk16, dv16, dqc, dz, du, xb, cos, sa, sb, winT))


def _reduce_grads(g_in, acc, dbin, dsink):
    rs = _ReduceScatter([(SH_IN, D_MODEL)])

    def body(g_ref, acc_ref, dbin_ref, dsink_ref, r_ref, sv_ref, sv_mine, sv_all, sv_send, sv_recv, *rs_scratch):
        x, y, c = lax.axis_index("x"), lax.axis_index("y"), lax.axis_index("c")
        chips = [(1 - x, y), (x, 1 - y), (1 - x, 1 - y)]
        start, exchange, relay, finish, drain = rs.bind((g_ref,), (r_ref,), rs_scratch)
        start()

        sv_mine[...] = jnp.zeros_like(sv_mine)
        sv_mine[0:4, 0:D_MODEL] = acc_ref[0:4, :]
        sv_mine[4:5, 0:D_IN] = dbin_ref[...]
        sv_mine[5:6, 0:128] = dsink_ref[...]
        my_dev = 4 * x + 2 * y + c
        others = [(x, y, 1 - c)] + [(*chip, cc) for chip in chips for cc in (c, 1 - c)]

        def sv_copy(j, to):
            return pltpu.make_async_remote_copy(
                src_ref=sv_mine, dst_ref=sv_all.at[my_dev], send_sem=sv_send.at[j], recv_sem=sv_recv.at[j],
                device_id=to, device_id_type=MESH)

        sv_sends = [sv_copy(j, to) for j, to in enumerate(others)]
        for cp in sv_sends:
            cp.start()
        exchange()
        relay()
        finish()
        sv_all[my_dev] = sv_mine[...]
        for j in range(7):
            sv_copy(j, (x, y, c)).wait_recv()
        tot = sv_all[0]
        for d in range(1, 8):
            tot = tot + sv_all[d]
        sv_ref[...] = tot
        drain()
        for cp in sv_sends:
            cp.wait_send()

    vm = pl.BlockSpec(memory_space=pltpu.VMEM)
    hbm = pl.BlockSpec(memory_space=pl.ANY)
    return pl.pallas_call(
        body, name="reduce_grads",
        out_shape=(_sds((SH_IN, D_MODEL), F32), _vm_sds((8, SV_W), F32)),
        in_specs=[hbm, vm, vm, vm], out_specs=(hbm, vm),
        scratch_shapes=[pltpu.VMEM((8, SV_W), F32), pltpu.VMEM((8, 8, SV_W), F32),
                        pltpu.SemaphoreType.DMA((7,)), pltpu.SemaphoreType.DMA((7,))] + rs.scratch_shapes(),
        compiler_params=_cp(vmem_mb=40),
    )(pltpu.with_memory_space_constraint(g_in, pltpu.HBM), acc, dbin, dsink)


def _adamw_update(w, g, m, v):
    nm = ADAM_B1 * m + (1.0 - ADAM_B1) * g
    nv = ADAM_B2 * v + (1.0 - ADAM_B2) * (g * g)
    m_hat = nm / (1.0 - ADAM_B1 ** ADAM_STEP)
    v_hat = nv / (1.0 - ADAM_B2 ** ADAM_STEP)
    return -ADAM_LR * (m_hat / (jnp.sqrt(v_hat) + ADAM_EPS) + ADAM_WD * w), nm, nv


SMALL = ((4, D_IN, 1.0), (5, 8, -1.0), (1, D_MIX, 1.0), (2, D_MODEL, 1.0), (3, D_MODEL, 1.0))


def _adamw_all(items, sv, ws, ms, vs, n_steps=4):
    nb, ns = 4 * len(items), len(SMALL)

    def body(*refs):
        ins, sv_ref, small_in = refs[:nb], refs[nb], refs[nb + 1:nb + 1 + 3 * ns]
        outs = refs[nb + 1 + 3 * ns:]
        big_out, loss_ref, small_out = outs[:nb], outs[nb], outs[nb + 1:]
        for p in range(len(items)):
            w_ref, g_ref, m_ref, v_ref = ins[4 * p:4 * p + 4]
            gv = g_ref[...]
            big_out[4 * p][...] = gv
            big_out[4 * p + 1][...], big_out[4 * p + 2][...], big_out[4 * p + 3][...] = _adamw_update(
                w_ref[...], gv, m_ref[...], v_ref[...])

        @pl.when(pl.program_id(0) == 0)
        def _():
            loss_ref[...] = jnp.sum(sv_ref[0:1, 0:D_MODEL], axis=1, keepdims=True)
            for p, (row, width, sign) in enumerate(SMALL):
                gv = sign * sv_ref[row:row + 1, 0:width]
                small_out[4 * p][...] = gv
                small_out[4 * p + 1][...], small_out[4 * p + 2][...], small_out[4 * p + 3][...] = _adamw_update(
                    small_in[p][...], gv, small_in[ns + p][...], small_in[2 * ns + p][...])

    specs, shapes, args = [], [], []
    for w, g, m, v in items:
        rows, width = w.shape
        specs += [pl.BlockSpec((rows // n_steps, width), lambda i: (i, 0))] * 4
        shapes += [_sds((rows, width), F32)] * 4
        args += [w, g, m, v]
    small_args = [*ws, *ms, *vs]
    whole = lambda a: _full(a.shape)
    res = pl.pallas_call(
        body, name="adamw", grid=(n_steps,),
        in_specs=specs + [whole(sv)] + [whole(a) for a in small_args],
        out_specs=tuple(specs + [_full((1, 1))] + [whole(w) for w in ws for _ in range(4)]),
        out_shape=tuple(shapes + [_sds((1, 1), F32)] + [_sds(w.shape, F32) for w in ws for _ in range(4)]),
        compiler_params=_cp(("arbitrary",), vmem_mb=40),
    )(*_pin(*args, sv, *small_args))
    big = [tuple(res[4 * p:4 * p + 4]) for p in range(len(items))]
    return big, res[nb], [tuple(res[nb + 1 + 4 * p:nb + 5 + 4 * p]) for p in range(ns)]


def _rope_tables():
    pos = np.arange(SEQ, dtype=np.float32)
    inv = (np.float32(ROPE_THETA) ** (-np.arange(0, 64, 2, dtype=np.float32) / np.float32(64))).astype(np.float32)
    ang = np.tile(pos[:, None] * inv[None, :], (1, 4))
    cos, sin = np.cos(ang).astype(np.float32), np.sin(ang).astype(np.float32)
    low = (np.arange(128) % 64) < 32
    zero = np.float32(0.0)
    return jnp.asarray(cos), jnp.asarray(np.where(low, -sin, zero)), jnp.asarray(np.where(low, zero, sin))


def _local_step(x2, mem2, tgt2, winT, wout, wmem, b_in, sinks, g_branch, ln_gain, ln_bias):
    cos, sa, sb = _rope_tables()
    sinkv = jnp.pad(sinks, ((0, 0), (0, 120)))
    head_of_lane = np.arange(512)[:, None] // 64
    gather8 = jnp.asarray(head_of_lane == np.arange(128)[None, :], BF16)
    gather4 = jnp.asarray(head_of_lane[:W_B] == np.arange(128)[None, :], BF16)
    spread4 = jnp.asarray((head_of_lane[:W_B] == np.arange(128)[None, :]).T, BF16)

    xb, qa, ka, va, bn, b4, b16, qc, z, wout, wmem = _in_proj(x2, winT, b_in, cos, sa, sb, wout, wmem)
    memb, mkv = _mem_kv(mem2, wmem)
    b4f, b16f = b4.reshape(T, 768), b16.reshape(T, 768)

    swa = dict(kind="band", nb=SEQ // BLK, max_dist=BLK - 1, gqa=True)
    dil = (dict(kind="band", nb=SEQ // BLK), dict(kind="band", nb=SEQ // 4 // BLK), dict(kind="band", nb=1))
    (oa, lse_a), (o1, l1), (o4, l4), (o16, l16), (oc, lse_c) = _run_parts("attn_fwd", [
        _attn_fwd(qa, 0, W_A, ka, 0, va, 0, W_KV_A, sinks=sinks, **swa),
        _attn_fwd(bn, 0, W_B, bn, 1, bn, 2, W_B, **dil[0]),
        _attn_fwd(b4f, 0, W_B, b4f, 1, b4f, 2, W_B, **dil[1]),
        _attn_fwd(b16f, 0, W_B, b16f, 1, b16f, 2, W_B, **dil[2]),
        _attn_fwd(qc, 0, W_C, mkv, 0, mkv, 1, W_C, kind="mem")], "parallel", 48)

    s4 = lambda w: (B_LOC, 4, SEQ // 4, w)
    s16 = lambda w: (B_LOC, 16, SEQ // 16, w)
    (du, dz, doa, dla, dobn, lsen, dlbn, dob4, lse4, dlb4, dob16, lse16, dlb16, doc, dlc, acc, g_out) = _middle(
        oa, o1, l1, o4.reshape(s4(W_B)), l4.reshape(s4(128)), o16.reshape(s16(W_B)), l16.reshape(s16(128)), oc, z,
        x2, tgt2, g_branch, ln_gain, ln_bias, wout, spread4, gather4, gather8)

    flat = lambda a: a.reshape(T, a.shape[-1])
    (dqa, dka, dva, dsink), (dqc, g_mem) = _run_parts("attn_bwd_a", [
        _attn_bwd(qa, 0, W_A, ka, 0, va, 0, W_KV_A, doa, lse_a, dla, sinkv=sinkv, **swa),
        _attn_bwd(qc, 0, W_C, mkv, 0, mkv, 1, W_C, doc, lse_c, dlc, kind="mem", mem_in=memb)], "arbitrary", 48)
    last = T // QR - 1
    (r_out, r_mem), (dqn, dkn, dvn), (dq4, dk4, dv4), (dq16, dk16, dv16) = _run_parts("attn_bwd_b", [
        _ReduceScatter([(SH_OUT, D_MODEL), (SH_MEM, 2 * W_C)]).part((g_out, g_mem), (0, 1, 2, last, last)),
        _attn_bwd(bn, 0, W_B, bn, 1, bn, 2, W_B, dobn, lsen, dlbn, **dil[0]),
        _attn_bwd(b4f, 0, W_B, b4f, 1, b4f, 2, W_B, flat(dob4), flat(lse4), flat(dlb4), **dil[1]),
        _attn_bwd(b16f, 0, W_B, b16f, 1, b16f, 2, W_B, flat(dob16), flat(lse16), flat(dlb16), **dil[2])],
        "arbitrary", 62)

    r4 = lambda a: a.reshape(s4(W_B))
    r16 = lambda a: a.reshape(s16(W_B))
    gx, dbin, g_in = _dh_dx(dqa, dka, dva, dqn, dkn, dvn, r4(dq4), r4(dk4), r4(dv4), r16(dq16), r16(dk16),
                            r16(dv16), dqc, dz, du, xb, cos, sa, sb, winT)
    return gx, g_in, r_out, r_mem, acc, dbin, dsink


def kernel(x, mem, w_in, b_in, w_mem, attn_sinks, g_branch, w_out, ln_gain, ln_bias, loss_target, m_w_in, m_b_in, m_w_mem, m_attn_sinks, m_g_branch, m_w_out, m_ln_gain, m_ln_bias, v_w_in, v_b_in, v_w_mem, v_attn_sinks, v_g_branch, v_w_out, v_ln_gain, v_ln_bias):
    winT, wout, wmem = _gather_weights(w_in[0].T, w_out[0], w_mem[0])
    gx, g_in, r_out, r_mem, acc, dbin, dsink = _local_step(
        x.reshape(T, D_MODEL), mem.reshape(B_LOC * MEM_LEN, D_MODEL), loss_target.reshape(T, D_MODEL),
        winT, wout, wmem, b_in, attn_sinks, g_branch, ln_gain, ln_bias)
    r_in, sv = _reduce_grads(g_in, acc, dbin, dsink)

    small = ["b_in", "attn_sinks", "g_branch", "ln_gain", "ln_bias"]
    big, loss, steps = _adamw_all(
        [(w_in[0].T, r_in, m_w_in[0].T, v_w_in[0].T), (w_out[0], r_out, m_w_out[0], v_w_out[0]),
         (w_mem[0], r_mem, m_w_mem[0], v_w_mem[0])],
        sv, [b_in, attn_sinks, g_branch, ln_gain, ln_bias], [m_b_in, m_attn_sinks, m_g_branch, m_ln_gain, m_ln_bias],
        [v_b_in, v_attn_sinks, v_g_branch, v_ln_gain, v_ln_bias])
    out = dict(zip(small, steps))
    out["w_in"] = tuple(a.T[None] for a in big[0])
    out["w_out"], out["w_mem"] = (tuple(a[None] for a in st) for st in big[1:])
    names = ["w_in", "b_in", "w_mem", "attn_sinks", "g_branch", "w_out", "ln_gain", "ln_bias"]
    return (loss.reshape(()), gx.reshape(B_LOC, SEQ, D_MODEL), *[out[n][k] for k in range(4) for n in names])
```

```python
import jax
import jax.numpy as jnp
import numpy as np
from jax import lax
from jax.experimental import pallas as pl
from jax.experimental.pallas import tpu as pltpu

F32, BF16 = jnp.float32, jnp.bfloat16

D_MODEL = 1024
SEQ = 2048
B_LOC = 2
T = B_LOC * SEQ
BLK = 128
MEM_LEN = 256
W_A, W_KV_A, W_B, W_C, D_MIX = 512, 128, 256, 256, 1024
D_IN = 2816
O_QA, O_KA, O_VA, O_QB, O_KB, O_VB, O_QC, O_Z = 0, 512, 640, 768, 1024, 1280, 1536, 1792
ROPE_THETA = 10000.0
LN_EPS = 1e-5
RMS_EPS = 1e-6
ALPHA = 2.0 ** 0.25
QK_SCALE = 0.125
N_CHIP = 4
SH_IN, SH_OUT, SH_MEM = D_IN // N_CHIP, D_MIX // N_CHIP, D_MODEL // N_CHIP
NEG = -1e30
ADAM_LR, ADAM_B1, ADAM_B2, ADAM_EPS, ADAM_WD, ADAM_STEP = 0.001, 0.9, 0.999, 1e-08, 0.01, 10
SV_W = 1024
SV_DB, SV_SINK = 4, 7
assert D_MODEL == D_MIX == SV_W and D_IN <= (SV_SINK - SV_DB) * SV_W
MESH = pl.DeviceIdType.MESH

NN = ((1,), (0,))
NT = ((1,), (1,))
TN = ((0,), (0,))


def _dot(a, b, dims):
    return lax.dot_general(a, b, (dims, ((), ())), preferred_element_type=F32)


def _cp(sem=None, vmem_mb=None):
    kw = {}
    if sem is not None:
        kw["dimension_semantics"] = sem
    if vmem_mb is not None:
        kw["vmem_limit_bytes"] = vmem_mb * 1024 * 1024
    return pltpu.CompilerParams(**kw)


def _sds(shape, dtype):
    return pltpu.HBM(shape, dtype)


def _vm_sds(shape, dtype):
    return jax.ShapeDtypeStruct(shape, dtype)


def _pin(*args):
    return [pltpu.with_memory_space_constraint(a, pltpu.HBM) for a in args]


def _full(shape):
    n = len(shape)
    return pl.BlockSpec(shape, lambda *_: (0,) * n)


def _shard_rows(ref, n, chip, half):
    start = pl.multiple_of((2 * chip[0] + chip[1]) * n + half * (n // 2), 16)
    return ref.at[pl.ds(start, n // 2), :]


def _gather_weights(win_sh, wout_sh, wmem_sh):
    half, piece = SH_IN // 2, SH_IN // 4
    shards = ((SH_IN, D_MODEL), (SH_OUT, D_MODEL), (SH_MEM, 2 * W_C))

    def body(a_ref, b_ref, c_ref, oa_ref, ob_ref, oc_ref, raw_a, raw_b, raw_c, own_a, own_b, own_c,
             load_sem, store_sem, ici_send, ici_recv, d2d_send, d2d_recv):
        x, y, c = lax.axis_index("x"), lax.axis_index("y"), lax.axis_index("c")
        me, sibling = (x, y, c), (x, y, 1 - c)
        xn, yn, dg = (1 - x, y), (x, 1 - y), (1 - x, 1 - y)
        srcs, raws = (a_ref, b_ref, c_ref), (raw_a, raw_b, raw_c)
        owns, outs = (own_a, own_b, own_c), (oa_ref, ob_ref, oc_ref)
        loads = [pltpu.make_async_copy(srcs[a], raws[a], load_sem.at[a]) for a in range(3)]
        for cp in loads:
            cp.start()

        def rows(chip, hf, q):
            start = pl.multiple_of((2 * chip[0] + chip[1]) * SH_IN + hf * half + q * piece, 16)
            return oa_ref.at[pl.ds(start, piece), :]

        def copy(sems, k, chip, hf, q, to, src=None):
            blk = rows(chip, hf, q)
            return pltpu.make_async_remote_copy(
                src_ref=blk if src is None else src, dst_ref=blk, send_sem=sems[0].at[k], recv_sem=sems[1].at[k],
                device_id=to, device_id_type=MESH)

        def my_piece(q):
            return own_a.at[pl.ds(pl.multiple_of(c * half + q * piece, 16), piece), :]

        ici, d2d = (ici_send, ici_recv), (d2d_send, d2d_recv)
        stores, direct = [], []
        for a, (n, _) in enumerate(shards):
            loads[a].wait()
            owns[a][...] = raws[a][...].astype(BF16)
            mine = pl.ds(pl.multiple_of((2 * x + y) * n, 16), n)
            stores.append(pltpu.make_async_copy(owns[a], outs[a].at[mine, :], store_sem.at[a]))
            stores[-1].start()
            if a == 0:
                direct = [copy(ici, 0, (x, y), c, 0, (*xn, c), my_piece(0)),
                          copy(ici, 1, (x, y), c, 1, (*xn, c), my_piece(1)),
                          copy(ici, 3, (x, y), c, 0, (*yn, c), my_piece(0)),
                          copy(ici, 4, (x, y), c, 1, (*yn, c), my_piece(1))]
                for cp in direct:
                    cp.start()
        arrivals = [(0, xn, 0), (1, xn, 1), (3, yn, 0), (4, yn, 1), (2, dg, 1), (5, dg, 0)]
        passed = []
        for k, chip, q in arrivals:
            copy(ici, k, chip, c, q, me).wait_recv()
            if k == 0:
                passed.append(copy(ici, 5, xn, c, 0, (*yn, c)))
                passed[-1].start()
            if k == 4:
                passed.append(copy(ici, 2, yn, c, 1, (*xn, c)))
                passed[-1].start()
            passed.append(copy(d2d, k, chip, c, q, sibling))
            passed[-1].start()
        for k, chip, q in arrivals:
            copy(d2d, k, chip, 1 - c, q, me).wait_recv()
        for cp in direct + passed:
            cp.wait_send()
        for cp in stores:
            cp.wait()

    hbm = pl.BlockSpec(memory_space=pl.ANY)
    return pl.pallas_call(
        body, name="gather_weights",
        out_shape=(_sds((D_IN, D_MODEL), BF16), _sds((D_MIX, D_MODEL), BF16), _sds((D_MODEL, 2 * W_C), BF16)),
        in_specs=[hbm, hbm, hbm], out_specs=(hbm, hbm, hbm),
        scratch_shapes=([pltpu.VMEM(sh, F32) for sh in shards] + [pltpu.VMEM(sh, BF16) for sh in shards]
                        + [pltpu.SemaphoreType.DMA((3,))] * 2 + [pltpu.SemaphoreType.DMA((6,))] * 4),
        compiler_params=_cp(vmem_mb=40),
    )(*_pin(win_sh, wout_sh, wmem_sh))


def _rope(t, cos, sa, sb, sign):
    w = t.shape[1]
    reps = w // 128
    c, a, b = (jnp.tile(v, (1, reps)) if reps > 1 else v for v in (cos, sa, sb))
    rot = pltpu.roll(t, w - 32, 1) * a + pltpu.roll(t, 32, 1) * b
    return t * c + rot if sign > 0 else t * c - rot


def _in_proj(x, winT, b_in, cos, sa, sb, wout_own, wmem_own):
    tm = 512
    spt = SEQ // tm
    n_steps = T // tm
    forward_step = n_steps // 2

    def body(x_ref, w_ref, b_ref, cos_ref, sa_ref, sb_ref, wo_in, wm_in,
             xb_ref, qa_ref, ka_ref, va_ref, bn_ref, b4_ref, b16_ref, qc_ref, z_ref, wo_ref, wm_ref,
             scr, ici_send, ici_recv, d2d_send, d2d_recv):
        i = pl.program_id(0)
        mx, my, mc = lax.axis_index("x"), lax.axis_index("y"), lax.axis_index("c")
        chips = [(1 - mx, my), (mx, 1 - my), (1 - mx, 1 - my)]
        full = ((wo_ref, SH_OUT), (wm_ref, SH_MEM))

        def copy(sems, a, j, chip_of_block, half, to):
            blk = _shard_rows(full[a][0], full[a][1], chip_of_block, half)
            return pltpu.make_async_remote_copy(
                src_ref=blk, dst_ref=blk, send_sem=sems[0].at[a, j], recv_sem=sems[1].at[a, j],
                device_id=to, device_id_type=MESH)

        ici, d2d = (ici_send, ici_recv), (d2d_send, d2d_recv)
        pairs = [(a, j, chip) for j, chip in enumerate(chips) for a in range(2)]

        @pl.when(i == 0)
        def _():
            for a, j, chip in pairs:
                copy(ici, a, j, (mx, my), mc, (*chip, mc)).start()

        @pl.when(i == forward_step)
        def _():
            for a, j, chip in pairs:
                copy(ici, a, j, chip, mc, (mx, my, mc)).wait_recv()
                copy(d2d, a, j, chip, mc, (mx, my, 1 - mc)).start()

        @pl.when(i == n_steps - 1)
        def _():
            for a, j, chip in pairs:
                copy(d2d, a, j, chip, 1 - mc, (mx, my, mc)).wait_recv()
            for a, j, chip in pairs:
                copy(ici, a, j, (mx, my), mc, (*chip, mc)).wait_send()
                copy(d2d, a, j, chip, mc, (mx, my, 1 - mc)).wait_send()

        xb = x_ref[...].astype(BF16)
        xb_ref[...] = xb
        cos_t, sa_t, sb_t = cos_ref[...], sa_ref[...], sb_ref[...]

        def proj(r0, n):
            return _dot(xb, w_ref[r0:r0 + n, :], NT) + b_ref[:, r0:r0 + n]

        def rope(t):
            return _rope(t, cos_t, sa_t, sb_t, +1)

        parts = (rope(proj(O_QB, W_B)) * QK_SCALE, rope(proj(O_KB, W_B)), proj(O_VB, W_B))
        for k, part in enumerate(parts):
            bn_ref[:, 256 * k:256 * (k + 1)] = part.astype(BF16)
            scr[2 * k] = part[:, :128]
            scr[2 * k + 1] = part[:, 128:]
        for j in range(6):
            lanes = slice(128 * j, 128 * (j + 1))
            for res in range(4):
                t = scr[j, pl.ds(res, tm // 4, stride=4), :]
                b4_ref[0, res, :, lanes] = t.astype(BF16)
                scr[6 + j, res * (tm // 4):(res + 1) * (tm // 4), :] = t
            for res in range(16):
                b16_ref[0, res, :, lanes] = scr[6 + j, pl.ds((res % 4) * (tm // 4) + res // 4, tm // 16, stride=4),
                                                :].astype(BF16)
        qa_ref[...] = (rope(proj(O_QA, W_A)) * QK_SCALE).astype(BF16)
        assert O_VA == O_KA + W_KV_A
        kv = proj(O_KA, 2 * W_KV_A)
        ka_ref[...] = rope(kv[:, :W_KV_A]).astype(BF16)
        va_ref[...] = kv[:, W_KV_A:].astype(BF16)
        qc_ref[...] = (proj(O_QC, W_C) * QK_SCALE).astype(BF16)
        z_ref[...] = proj(O_Z, D_MIX).astype(BF16)

    tok = lambda w: pl.BlockSpec((tm, w), lambda i: (i, 0))
    tab = pl.BlockSpec((tm, 128), lambda i: (i % spt, 0))
    hbm = pl.BlockSpec(memory_space=pl.ANY)
    return pl.pallas_call(
        body, name="in_proj", grid=(n_steps,),
        in_specs=[tok(D_MODEL), _full((D_IN, D_MODEL)), _full((1, D_IN)), tab, tab, tab, hbm, hbm],
        out_specs=(tok(D_MODEL), tok(W_A), tok(W_KV_A), tok(W_KV_A), tok(768),
                   pl.BlockSpec((1, 4, tm // 4, 768), lambda i: (i // spt, 0, i % spt, 0)),
                   pl.BlockSpec((1, 16, tm // 16, 768), lambda i: (i // spt, 0, i % spt, 0)),
                   tok(W_C), tok(D_MIX), hbm, hbm),
        out_shape=(_sds((T, D_MODEL), BF16), _sds((T, W_A), BF16), _sds((T, W_KV_A), BF16), _sds((T, W_KV_A), BF16),
                   _sds((T, 768), BF16), _sds((B_LOC, 4, SEQ // 4, 768), BF16), _sds((B_LOC, 16, SEQ // 16, 768), BF16),
                   _sds((T, W_C), BF16), _sds((T, D_MIX), BF16),
                   _sds((D_MIX, D_MODEL), BF16), _sds((D_MODEL, 2 * W_C), BF16)),
        input_output_aliases={6: 9, 7: 10},
        scratch_shapes=[pltpu.VMEM((12, tm, 128), F32)] + [pltpu.SemaphoreType.DMA((2, 3))] * 4,
        compiler_params=_cp(("arbitrary",), vmem_mb=48),
    )(*_pin(x, winT, b_in, cos, sa, sb, wout_own, wmem_own))


def _mem_kv(mem, wmem):
    def body(m_ref, w_ref, mb_ref, kv_ref):
        mb = m_ref[...].astype(BF16)
        mb_ref[...] = mb
        kv_ref[...] = _dot(mb, w_ref[...], NN).astype(BF16)

    n = B_LOC * MEM_LEN
    return pl.pallas_call(
        body, name="mem_kv",
        out_shape=(_sds((n, D_MODEL), BF16), _sds((n, 2 * W_C), BF16)),
    )(*_pin(mem, wmem))


class _Part:
    def __init__(self, body, args, in_specs, out_specs, out_shape, scratch=()):
        self.body, self.args, self.in_specs, self.out_specs, self.out_shape = body, args, in_specs, out_specs, out_shape
        self.scratch = list(scratch)


def _run_parts(name, parts, semantics, vmem_mb):
    n_in = [len(p.args) for p in parts]
    n_out = [len(p.out_shape) for p in parts]
    n_scr = [len(p.scratch) for p in parts]

    def body(*refs):
        ins, outs, scr = refs[:sum(n_in)], refs[sum(n_in):sum(n_in) + sum(n_out)], refs[sum(n_in) + sum(n_out):]
        i0 = o0 = s0 = 0
        for p, ni, no, ns in zip(parts, n_in, n_out, n_scr):
            p.body(*ins[i0:i0 + ni], *outs[o0:o0 + no], *scr[s0:s0 + ns])
            i0, o0, s0 = i0 + ni, o0 + no, s0 + ns

    res = pl.pallas_call(
        body, name=name, grid=(T // QR,),
        in_specs=[sp for p in parts for sp in p.in_specs], out_specs=tuple(sp for p in parts for sp in p.out_specs),
        out_shape=tuple(sh for p in parts for sh in p.out_shape),
        scratch_shapes=[sc for p in parts for sc in p.scratch],
        compiler_params=_cp((semantics,), vmem_mb=vmem_mb),
    )(*_pin(*[a for p in parts for a in p.args]))
    out, o0 = [], 0
    for no in n_out:
        out.append(tuple(res[o0:o0 + no]))
        o0 += no
    return out


QB = 8
QR = QB * BLK


def _lane_lo():
    return lax.broadcasted_iota(jnp.int32, (1, 128), 1) < 64


def _dup_head(k2, hk, lo):
    kf = k2.astype(F32)
    r = pltpu.roll(kf, 64, 1)
    return (jnp.where(lo, kf, r) if hk == 0 else jnp.where(lo, r, kf)).astype(BF16)


def _stack_heads(pairs, lo):
    parts = []
    for x2 in pairs:
        z = jnp.zeros_like(x2)
        parts += [jnp.where(lo, x2, z), jnp.where(lo, z, x2)]
    return jnp.concatenate(parts, axis=0)


def _prev_mode(kind, nb, j):
    if kind == "mem" or nb == 1:
        return "no"
    if nb <= QB:
        return "yes" if j % nb else "no"
    return "yes" if j else "dyn"


class _Attn:
    def __init__(self, kind, nb, max_dist, gqa, qw, kvw, qcb, kcb, vcb):
        self.kind, self.nb, self.gqa, self.qw, self.kvw = kind, nb, gqa, qw, kvw
        npairs = qw // 128
        self.groups = ([(hk, [2 * hk, 2 * hk + 1]) for hk in range(npairs // 2)] if gqa
                       else [(p, [p]) for p in range(npairs)])
        self.nh = 2 * len(self.groups[0][1])
        self.cols = 128 * self.nh
        self.reach = BLK - max_dist
        self.ext_prev = kind == "band" and nb > QB
        self.q_spec = pl.BlockSpec((QR, qw), lambda g: (g, qcb))
        self.row_spec = pl.BlockSpec((QR, qw), lambda g: (g, 0))
        self.stat_spec = pl.BlockSpec((QR, 128), lambda g: (g, 0))
        if kind == "mem":
            per = SEQ // QR
            self.kv_specs = [pl.BlockSpec((MEM_LEN, kvw), lambda g: (g // per, kcb)),
                             pl.BlockSpec((MEM_LEN, kvw), lambda g: (g // per, vcb))]
        else:
            self.kv_specs = [pl.BlockSpec((QR, kvw), lambda g: (g, kcb)), pl.BlockSpec((QR, kvw), lambda g: (g, vcb))]
            if self.ext_prev:
                self.kv_specs += [pl.BlockSpec((BLK, kvw), lambda g: (jnp.maximum(g * QB - 1, 0), kcb)),
                                  pl.BlockSpec((BLK, kvw), lambda g: (jnp.maximum(g * QB - 1, 0), vcb))]

    def masks(self):
        if self.kind == "mem":
            return None
        kj = lax.broadcasted_iota(jnp.int32, (2 * BLK, self.cols), 0)
        qi = lax.broadcasted_iota(jnp.int32, (2 * BLK, self.cols), 1) & (BLK - 1)
        both = jnp.logical_and(kj >= qi + self.reach, kj <= qi + BLK)
        return kj, qi, self.as_mask(both), self.as_mask(kj[:BLK] <= qi[:BLK])

    def as_mask(self, in_reach):
        return jnp.where(in_reach, 0.0, NEG) if self.gqa else in_reach

    def hide(self, s, mask):
        return s + mask if self.gqa else jnp.where(mask, s, NEG)

    def keys(self, j, gi, kc_ref, vc_ref, kp_ref, vp_ref, lo, kq, g, dup):
        def kv(k_ref, v_ref, r):
            if self.gqa:
                return _dup_head(k_ref[r, :], gi, lo), _dup_head(v_ref[r, :], gi, lo)
            sl = slice(128 * gi, 128 * (gi + 1))
            return k_ref[r, sl], v_ref[r, sl]

        def blocks(b0, b1):
            if not self.gqa:
                return kv(kc_ref, vc_ref, slice(BLK * b0, BLK * b1))
            for b in range(b0, b1):
                if (b, gi) not in dup:
                    dup[b, gi] = kv(kc_ref, vc_ref, slice(BLK * b, BLK * (b + 1)))
            ks, vs = zip(*(dup[b, gi] for b in range(b0, b1)))
            return jnp.concatenate(ks, axis=0), jnp.concatenate(vs, axis=0)

        if self.kind == "mem":
            key0 = pl.multiple_of((g // (SEQ // QR)) * MEM_LEN, MEM_LEN)
            return (*kv(kc_ref, vc_ref, slice(None)), None, [(0, MEM_LEN, key0)])
        kj, qi, both, cur = kq
        row0 = g * QR + BLK * j
        mode = _prev_mode(self.kind, self.nb, j)
        if mode == "no":
            return (*blocks(j, j + 1), cur, [(0, BLK, pl.multiple_of(row0, BLK))])
        if mode == "yes":
            return (*blocks(j - 1, j + 1), both, [(0, 2 * BLK, pl.multiple_of(row0 - BLK, BLK))])
        has_prev = ((g * QB) % self.nb) > 0
        hp = has_prev.astype(jnp.int32)
        mask = self.as_mask(jnp.logical_and(kj >= qi * hp + (self.reach * hp + BLK * (1 - hp)), kj <= qi + BLK))
        kp, vp = kv(kp_ref, vp_ref, slice(None))
        kc, vc = blocks(0, 1)
        return (jnp.concatenate([kp, kc], axis=0), jnp.concatenate([vp, vc], axis=0), mask,
                [(0, BLK, pl.multiple_of(jnp.maximum(row0 - BLK, 0), BLK)), (BLK, BLK, pl.multiple_of(row0, BLK))])


def _attn_fwd(q, qcb, qw, k, kcb, v, vcb, kvw, *, kind, nb=1, max_dist=BLK, gqa=False, sinks=None):
    a = _Attn(kind, nb, max_dist, gqa, qw, kvw, qcb, kcb, vcb)

    def body(*refs):
        it = iter(refs)
        q_ref, kc_ref, vc_ref = next(it), next(it), next(it)
        kp_ref, vp_ref = (next(it), next(it)) if a.ext_prev else (None, None)
        sink_ref = next(it) if sinks is not None else None
        o_ref, lse_ref = next(it), next(it)
        g = pl.program_id(0)
        lo = _lane_lo()
        top = lax.broadcasted_iota(jnp.int32, (128, 1), 0) < 64
        rid = lax.broadcasted_iota(jnp.int32, (8, 128), 0)
        kq, dup = a.masks(), {}
        stats = {}

        def scores(j, gi, pairs):
            rows = slice(BLK * j, BLK * (j + 1))
            qs = _stack_heads([q_ref[rows, 128 * p:128 * (p + 1)] for p in pairs], lo)
            kk, vv, mask, _ = a.keys(j, gi, kc_ref, vc_ref, kp_ref, vp_ref, lo, kq, g, dup)
            pieces = [slice(r0, r0 + BLK) for r0 in range(0, kk.shape[0], BLK)]
            return dict(j=j, gi=gi, pairs=pairs, rows=rows, vv=vv, mask=mask, pieces=pieces,
                        ss=[_dot(kk[r], qs, NT) for r in pieces])

        def softmax(c):
            gi, mask = c["gi"], c["mask"]
            ss = [s if mask is None else a.hide(s, mask[r]) for r, s in zip(c["pieces"], c.pop("ss"))]
            m = jnp.max(ss[0], axis=0, keepdims=True)
            for s in ss[1:]:
                m = jnp.maximum(m, jnp.max(s, axis=0, keepdims=True))
            if sink_ref is not None:
                sk = jnp.concatenate([jnp.full((1, 128), sink_ref[0, a.nh * gi + i], F32) for i in range(a.nh)], axis=1)
                m = jnp.maximum(m, sk)
            ps = [jnp.exp(s - m) for s in ss]
            l = sum(jnp.sum(p, axis=0, keepdims=True) for p in ps)
            if sink_ref is not None:
                l = l + jnp.exp(sk - m)
            c["ps"] = [p.astype(BF16) for p in ps]
            c["l"], c["lse"] = l, m + jnp.log(l)

        def outputs(c):
            j, gi, rows = c["j"], c["gi"], c["rows"]
            ot = sum(_dot(c["vv"][r], p, TN) for r, p in zip(c["pieces"], c["ps"]))
            ot = ot * pl.reciprocal(c["l"], approx=True)
            for i, p in enumerate(c["pairs"]):
                o2t = jnp.where(top, ot[:, 256 * i:256 * i + 128], ot[:, 256 * i + 128:256 * i + 256])
                o_ref[rows, 128 * p:128 * (p + 1)] = o2t.T.astype(BF16)
            stat = stats.get(j, jnp.zeros((8, 128), F32))
            for i in range(a.nh):
                stat = jnp.where(rid == a.nh * gi + i, c["lse"][:, 128 * i:128 * (i + 1)], stat)
            stats[j] = stat
            if gi == a.groups[-1][0]:
                lse_ref[rows, :] = jnp.concatenate([stats.pop(j), jnp.zeros((120, 128), F32)], axis=0).T

        chains = [(j, gi, pairs) for j in range(QB) for gi, pairs in a.groups]
        live = {}
        for t in range(len(chains) + 2):
            if t < len(chains):
                live[t] = scores(*chains[t])
            if 0 <= t - 1 < len(chains):
                softmax(live[t - 1])
            if 0 <= t - 2 < len(chains):
                outputs(live.pop(t - 2))


    args = [q, k, v] + ([k, v] if a.ext_prev else [])
    in_specs = [a.q_spec] + a.kv_specs
    if sinks is not None:
        args.append(sinks)
        in_specs.append(pl.BlockSpec(memory_space=pltpu.SMEM))
    return _Part(body, args, in_specs, [a.row_spec, a.stat_spec], [_sds((T, qw), BF16), _sds((T, 128), F32)])


def _attn_bwd(q, qcb, qw, k, kcb, v, vcb, kvw, do, lse, dl, *, kind, nb=1, max_dist=BLK, gqa=False, sinkv=None,
              mem_in=None):
    a = _Attn(kind, nb, max_dist, gqa, qw, kvw, qcb, kcb, vcb)

    def body(*refs):
        it = iter(refs)
        q_ref, kc_ref, vc_ref = next(it), next(it), next(it)
        kp_ref, vp_ref = (next(it), next(it)) if a.ext_prev else (None, None)
        do_ref, lse_ref, dl_ref = next(it), next(it), next(it)
        sinkv_ref = next(it) if sinkv is not None else None
        mem_ref = next(it) if kind == "mem" else None
        dq_ref = next(it)
        if kind == "mem":
            gmem_ref = next(it)
        else:
            dk_out, dv_out = next(it), next(it)
        dsink_ref = next(it) if sinkv is not None else None
        if kind != "mem":
            dk_ref, dv_ref, stage_k, stage_v, flush_sem = next(it), next(it), next(it), next(it), next(it)
        else:
            dkv_ref = next(it)
        g = pl.program_id(0)
        lo = _lane_lo()
        top = lax.broadcasted_iota(jnp.int32, (128, 1), 0) < 64

        @pl.when(g == 0)
        def _():
            if kind == "mem":
                dkv_ref[...] = jnp.zeros_like(dkv_ref)
            else:
                dk_ref[...] = jnp.zeros_like(dk_ref)
                dv_ref[...] = jnp.zeros_like(dv_ref)
            if dsink_ref is not None:
                dsink_ref[...] = jnp.zeros_like(dsink_ref)

        kq, dup = a.masks(), {}
        stats_t = {}

        def first_matmuls(j, gi, pairs):
            rows = slice(BLK * j, BLK * (j + 1))
            if j not in stats_t:
                stats_t[j] = (lse_ref[rows, :].T, dl_ref[rows, :].T)
            lse_t, dl_t = stats_t[j]
            heads = [a.nh * gi + i for i in range(a.nh)]
            c = dict(rows=rows, gi=gi, pairs=pairs)
            c["qs"] = _stack_heads([q_ref[rows, 128 * p:128 * (p + 1)] for p in pairs], lo)
            c["dos"] = _stack_heads([do_ref[rows, 128 * p:128 * (p + 1)] for p in pairs], lo)
            c["lse_row"] = jnp.concatenate([lse_t[h:h + 1, :] for h in heads], axis=1)
            c["dl_row"] = jnp.concatenate([dl_t[h:h + 1, :] for h in heads], axis=1)
            c["kk"], vv, c["mask"], c["dests"] = a.keys(j, gi, kc_ref, vc_ref, kp_ref, vp_ref, lo, kq, g, dup)
            c["s"] = _dot(c["kk"], c["qs"], NT)
            c["dp"] = _dot(vv, c["dos"], NT)
            return c

        def elementwise(c):
            s = c.pop("s")
            if c["mask"] is not None:
                s = a.hide(s, c["mask"])
            p = jnp.exp(s - c["lse_row"])
            c["ds"] = (p * (c.pop("dp") - c["dl_row"])).astype(BF16)
            c["p"] = p.astype(BF16)

        def last_matmuls(c):
            gi, rows = c["gi"], c["rows"]
            dqt = _dot(c["kk"], c["ds"], TN)
            ck = _dot(c["ds"], c["qs"], NN)
            cv = _dot(c["p"], c["dos"], NN)
            if gqa:
                sel = lo if gi == 0 else jnp.logical_not(lo)
                ck = jnp.where(sel, ck + pltpu.roll(ck, 64, 1), 0.0)
                cv = jnp.where(sel, cv + pltpu.roll(cv, 64, 1), 0.0)
                kcols = slice(0, 128)
            else:
                kcols = slice(128 * gi, 128 * (gi + 1))
            for r0, nr, key0 in c["dests"]:
                krows = pl.ds(key0, nr)
                if kind == "mem":
                    dkv_ref[krows, kcols] += ck[r0:r0 + nr]
                    dkv_ref[krows, slice(kvw + kcols.start, kvw + kcols.stop)] += cv[r0:r0 + nr]
                else:
                    dk_ref[krows, kcols] += ck[r0:r0 + nr]
                    dv_ref[krows, kcols] += cv[r0:r0 + nr]
            for i, p in enumerate(c["pairs"]):
                dq2t = jnp.where(top, dqt[:, 256 * i:256 * i + 128], dqt[:, 256 * i + 128:256 * i + 256])
                dq_ref[rows, 128 * p:128 * (p + 1)] = dq2t.T.astype(BF16)

        chains = [(j, gi, pairs) for j in range(QB) for gi, pairs in a.groups]
        live = {}
        for t in range(len(chains) + 2):
            if t < len(chains):
                live[t] = first_matmuls(*chains[t])
            if 0 <= t - 1 < len(chains):
                elementwise(live[t - 1])
            if 0 <= t - 2 < len(chains):
                last_matmuls(live.pop(t - 2))
        if dsink_ref is not None:
            ps = jnp.exp(sinkv_ref[...] - lse_ref[...]) * dl_ref[...]
            dsink_ref[...] += jnp.sum(ps, axis=0, keepdims=True)
        if kind == "mem":
            @pl.when(g == T // QR - 1)
            def _():
                gmem_ref[...] = _dot(mem_ref[...], dkv_ref[...].astype(BF16), TN)
        else:
            n_steps = T // QR

            def flush(step):
                rows = pl.ds(pl.multiple_of(step * QR, QR), QR)
                out = []
                for acc, stage, dst, i in ((dk_ref, stage_k, dk_out, 0), (dv_ref, stage_v, dv_out, 1)):
                    stage[...] = acc[rows, :].astype(BF16)
                    out.append(pltpu.make_async_copy(stage, dst.at[rows, :], flush_sem.at[i]))
                return out

            def flushed(step):
                rows = pl.ds(pl.multiple_of(step * QR, QR), QR)
                return [pltpu.make_async_copy(stage, dst.at[rows, :], flush_sem.at[i])
                        for stage, dst, i in ((stage_k, dk_out, 0), (stage_v, dv_out, 1))]

            @pl.when(g >= 2)
            def _():
                for cp in flushed(g - 2):
                    cp.wait()

            @pl.when(g >= 1)
            def _():
                for cp in flush(g - 1):
                    cp.start()

            @pl.when(g == n_steps - 1)
            def _():
                for cp in flushed(g - 1):
                    cp.wait()
                for cp in flush(g):
                    cp.start()
                for cp in flushed(g):
                    cp.wait()

    args = [q, k, v] + ([k, v] if a.ext_prev else []) + [do, lse, dl]
    in_specs = [a.q_spec] + a.kv_specs + [a.row_spec, a.stat_spec, a.stat_spec]
    if sinkv is not None:
        args.append(sinkv)
        in_specs.append(_full((1, 128)))
    if kind == "mem":
        args.append(mem_in)
        in_specs.append(pl.BlockSpec(mem_in.shape, lambda g: (0, 0), pipeline_mode=pl.Buffered(1)))
    out_shape = [_sds((T, qw), BF16)]
    out_specs = [a.row_spec]
    scratch = []
    if kind == "mem":
        out_shape.append(_sds((D_MODEL, 2 * kvw), F32))
        out_specs.append(pl.BlockSpec((D_MODEL, 2 * kvw), lambda g: (0, 0), pipeline_mode=pl.Buffered(1)))
        scratch = [pltpu.VMEM((B_LOC * MEM_LEN, 2 * kvw), F32)]
    else:
        out_shape += [_sds((T, kvw), BF16)] * 2
        out_specs += [pl.BlockSpec(memory_space=pl.ANY)] * 2
        scratch = [pltpu.VMEM((T, kvw), F32)] * 2 + [pltpu.VMEM((QR, kvw), BF16)] * 2 + [pltpu.SemaphoreType.DMA((2,))]
    if sinkv is not None:
        out_shape.append(_sds((1, 128), F32))
        out_specs.append(_full((1, 128)))
    return _Part(body, args, in_specs, out_specs, out_shape, scratch)


def _dot2(v, w_ref):
    hi = v.astype(BF16)
    lo = (v - hi.astype(F32)).astype(BF16)
    return _dot(hi, w_ref[...], NN) + _dot(lo, w_ref[...], NN)


def _middle(oa, o1, l1, o4, l4, o16, l16, oc, z, x, tgt, g_br, ln_g, ln_b, wout, spread4, gather4, gather8):
    tm = 512
    spt = SEQ // tm

    def body(oa_ref, o1_ref, l1_ref, o4_ref, l4_ref, o16_ref, l16_ref, oc_ref, z_ref, x_ref, t_ref,
             g_ref, lg_ref, lb_ref, w_ref, sp4_ref, ga4_ref, ga8_ref,
             du_ref, dz_ref, doa_ref, dla_ref,
             dobn_ref, lsen_ref, dlbn_ref, dob4_ref, lse4_ref, dlb4_ref, dob16_ref, lse16_ref, dlb16_ref,
             doc_ref, dlc_ref, acc_ref, gout_ref, scr):
        i = pl.program_id(0)

        @pl.when(i == 0)
        def _():
            acc_ref[...] = jnp.zeros_like(acc_ref)
            gout_ref[...] = jnp.zeros_like(gout_ref)

        q = tm // 4
        for res in range(16):
            rows = pl.ds((res % 4) * q + res // 4, tm // 16, stride=4)
            for j in range(2):
                scr[6 + j, rows, :] = o16_ref[0, res, :, 128 * j:128 * (j + 1)].astype(F32)
            scr[8, rows, :] = l16_ref[0, res]
        for res in range(4):
            rows, blk = pl.ds(res, q, stride=4), slice(res * q, (res + 1) * q)
            for j in range(2):
                scr[j, rows, :] = o4_ref[0, res, :, 128 * j:128 * (j + 1)].astype(F32)
                scr[3 + j, rows, :] = scr[6 + j, blk, :]
            scr[2, rows, :] = l4_ref[0, res]
            scr[5, rows, :] = scr[8, blk, :]
        inv_d = 1.0 / D_MODEL
        gb, lg, lb = g_ref[...], lg_ref[...], lb_ref[...]

        def rms(o):
            r = lax.rsqrt(jnp.sum(o * o, axis=1, keepdims=True) * (1.0 / o.shape[1]) + RMS_EPS)
            return o * r, r

        def rms_bwd(dn_, n_, r):
            return r * (dn_ - n_ * (jnp.sum(dn_ * n_, axis=1, keepdims=True) * (1.0 / n_.shape[1])))

        def forward(rs):
            o4v = jnp.concatenate([scr[0, rs, :], scr[1, rs, :]], axis=1)
            o16v = jnp.concatenate([scr[3, rs, :], scr[4, rs, :]], axis=1)
            l1v, l4v, l16v = l1_ref[rs, :], scr[2, rs, :], scr[5, rs, :]
            mx = jnp.maximum(jnp.maximum(l1v, l4v), l16v)
            e1, e4, e16 = jnp.exp(l1v - mx), jnp.exp(l4v - mx), jnp.exp(l16v - mx)
            ssum = e1 + e4 + e16
            inv = 1.0 / ssum
            c = dict(rs=rs, lse_b=mx + jnp.log(ssum))
            c["ob"] = (_dot2(e1 * inv, sp4_ref) * o1_ref[rs, :].astype(F32) + _dot2(e4 * inv, sp4_ref) * o4v
                       + _dot2(e16 * inv, sp4_ref) * o16v)
            c["oa"], c["oc"] = oa_ref[rs, :].astype(F32), oc_ref[rs, :].astype(F32)
            na, c["ra"] = rms(c["oa"])
            nb_, c["rb"] = rms(c["ob"])
            nc, c["rc"] = rms(c["oc"])
            c["n"] = jnp.concatenate([na, nb_, nc], axis=1)
            c["zf"] = z_ref[rs, :].astype(F32)
            c["sig"] = 1.0 / (1.0 + jnp.exp(-c["zf"]))
            c["sz"] = c["zf"] * c["sig"]
            c["yb"] = (c["n"] * gb * c["sz"]).astype(BF16)
            c["y2"] = _dot(c["yb"], w_ref[...], NN)
            return c

        def norm(c):
            rs = c["rs"]
            u = ALPHA * x_ref[rs, :] + c.pop("y2")
            mu = jnp.sum(u, axis=1, keepdims=True) * inv_d
            uc = u - mu
            rstd = lax.rsqrt(jnp.sum(uc * uc, axis=1, keepdims=True) * inv_d + LN_EPS)
            xh = uc * rstd
            diff = xh * lg + lb - t_ref[rs, :]
            acc_ref[0:1, :] += jnp.sum(diff * diff, axis=0, keepdims=True) * (0.5 * inv_d)
            dout = diff * inv_d
            acc_ref[2:3, :] += jnp.sum(dout * xh, axis=0, keepdims=True)
            acc_ref[3:4, :] += jnp.sum(dout, axis=0, keepdims=True)
            dxh = dout * lg
            du = rstd * (dxh - jnp.sum(dxh, axis=1, keepdims=True) * inv_d
                         - xh * (jnp.sum(dxh * xh, axis=1, keepdims=True) * inv_d))
            dub = du.astype(BF16)
            du_ref[rs, :] = dub
            c["dy"] = _dot(dub, w_ref[...], NT)
            gout_ref[...] += _dot(c.pop("yb"), dub, TN)

        def backward(c):
            rs, n, dy, zf, sig = c["rs"], c["n"], c["dy"], c["zf"], c["sig"]
            t1 = dy * c["sz"]
            acc_ref[1:2, :] += jnp.sum(t1 * n, axis=0, keepdims=True)
            dn = t1 * gb
            dz_ref[rs, :] = (dy * n * gb * (sig * (1.0 + zf * (1.0 - sig)))).astype(BF16)
            doa = rms_bwd(dn[:, :W_A], n[:, :W_A], c["ra"])
            dob = rms_bwd(dn[:, W_A:W_A + W_B], n[:, W_A:W_A + W_B], c["rb"])
            doc = rms_bwd(dn[:, W_A + W_B:], n[:, W_A + W_B:], c["rc"])
            doa_ref[rs, :] = doa.astype(BF16)
            dla_ref[rs, :] = _dot2(doa * c["oa"], ga8_ref)
            doc_ref[rs, :] = doc.astype(BF16)
            dlc_ref[rs, :] = _dot2(doc * c["oc"], ga4_ref)
            dobn_ref[rs, :] = dob.astype(BF16)
            lsen_ref[rs, :] = c["lse_b"]
            dlbn_ref[rs, :] = _dot2(dob * c["ob"], ga4_ref)
            scr[0, rs, :] = dob[:, :128]
            scr[1, rs, :] = dob[:, 128:]

        halves = [slice(h * (tm // 2), (h + 1) * (tm // 2)) for h in range(2)]
        live = {}
        for t in range(len(halves) + 2):
            if t < len(halves):
                live[t] = forward(halves[t])
            if 0 <= t - 1 < len(halves):
                norm(live[t - 1])
            if 0 <= t - 2 < len(halves):
                backward(live.pop(t - 2))
        for j in range(2):
            sl = slice(128 * j, 128 * (j + 1))
            for res in range(4):
                t = scr[j, pl.ds(res, q, stride=4), :]
                dob4_ref[0, res, :, sl] = t.astype(BF16)
                scr[6 + j, res * q:(res + 1) * q, :] = t
            for res in range(16):
                dob16_ref[0, res, :, sl] = scr[6 + j, pl.ds((res % 4) * q + res // 4, tm // 16, stride=4),
                                               :].astype(BF16)
        for res in range(4):
            rows = pl.ds(res, q, stride=4)
            lse4_ref[0, res] = lsen_ref[rows, :]
            dlb4_ref[0, res] = dlbn_ref[rows, :]
        for res in range(16):
            rows = pl.ds(res // 4, tm // 16, stride=4)
            lse16_ref[0, res] = lse4_ref[0, res % 4, rows, :]
            dlb16_ref[0, res] = dlb4_ref[0, res % 4, rows, :]


    tok = lambda w: pl.BlockSpec((tm, w), lambda i: (i, 0))
    p4 = lambda w: pl.BlockSpec((1, 4, tm // 4, w), lambda i: (i // spt, 0, i % spt, 0))
    p16 = lambda w: pl.BlockSpec((1, 16, tm // 16, w), lambda i: (i // spt, 0, i % spt, 0))
    s4 = lambda w, dt: _sds((B_LOC, 4, SEQ // 4, w), dt)
    s16 = lambda w, dt: _sds((B_LOC, 16, SEQ // 16, w), dt)
    row = _full((1, D_MODEL))
    return pl.pallas_call(
        body, name="middle", grid=(T // tm,),
        in_specs=[tok(W_A), tok(W_B), tok(128), p4(W_B), p4(128), p16(W_B), p16(128), tok(W_C), tok(D_MIX),
                  tok(D_MODEL), tok(D_MODEL), row, row, row, _full((D_MIX, D_MODEL)),
                  _full((128, W_B)), _full((W_B, 128)), _full((W_A, 128))],
        out_specs=(tok(D_MODEL), tok(D_MIX), tok(W_A), tok(128),
                   tok(W_B), tok(128), tok(128), p4(W_B), p4(128), p4(128), p16(W_B), p16(128), p16(128),
                   tok(W_C), tok(128), _full((8, D_MODEL)), _full((D_MIX, D_MODEL))),
        out_shape=(_sds((T, D_MODEL), BF16), _sds((T, D_MIX), BF16),
                   _sds((T, W_A), BF16), _sds((T, 128), F32),
                   _sds((T, W_B), BF16), _sds((T, 128), F32), _sds((T, 128), F32),
                   s4(W_B, BF16), s4(128, F32), s4(128, F32), s16(W_B, BF16), s16(128, F32), s16(128, F32),
                   _sds((T, W_C), BF16), _sds((T, 128), F32), _sds((8, D_MODEL), F32),
                   _sds((D_MIX, D_MODEL), F32)),
        scratch_shapes=[pltpu.VMEM((9, tm, 128), F32)],
        compiler_params=_cp(("arbitrary",), vmem_mb=56),
    )(*_pin(oa, o1, l1, o4, l4, o16, l16, oc, z, x, tgt, g_br, ln_g, ln_b, wout, spread4, gather4, gather8))


class _ReduceScatter:
    def __init__(self, shapes):
        self.shapes = shapes

    def scratch_shapes(self):
        out = []
        for n, w in self.shapes:
            h, p = n // 2, n // 4
            out += [pltpu.VMEM((4, h, w), F32), pltpu.VMEM((4, h, w), F32), pltpu.VMEM((6, p, w), BF16),
                    pltpu.VMEM((6, p, w), BF16), pltpu.VMEM((2, p, w), F32), pltpu.VMEM((h, w), F32)]
        na = len(self.shapes)
        dma = pltpu.SemaphoreType.DMA
        return out + [dma((na, 4)), dma((na, 4)), dma((na, 4)), dma((na, 6)), dma((na, 6)), dma((na,)), dma((na,)),
                      dma((na,))]

    def bind(self, g_refs, r_refs, scratch):
        na = len(self.shapes)
        bufs = [scratch[6 * a:6 * a + 6] for a in range(na)]
        mine, sib, stage, land, keep, tot = (tuple(b[i] for b in bufs) for i in range(6))
        loc_sem, s1_send, s1_recv, s2_send, s2_recv, s3_send, s3_recv, st_sem = scratch[6 * na:6 * na + 8]
        x, y, c = lax.axis_index("x"), lax.axis_index("y"), lax.axis_index("c")
        me, sibling = (x, y, c), (x, y, 1 - c)
        xn, yn, dg = (1 - x, y), (x, 1 - y), (1 - x, 1 - y)
        idx = lambda chip: 2 * chip[0] + chip[1]
        my_chip = idx((x, y))
        order = [idx(xn), idx(dg), idx(yn), my_chip]

        def rows(a, k, half):
            n = self.shapes[a][0]
            return pl.ds(pl.multiple_of(k * n + half * (n // 2), 8), n // 2)

        def piece(a, q):
            p = self.shapes[a][0] // 4
            return slice(q * p, (q + 1) * p)

        def load(a, k):
            return pltpu.make_async_copy(g_refs[a].at[rows(a, k, c), :], mine[a].at[k], loc_sem.at[a, k])

        def s1(a, k, half):
            return pltpu.make_async_remote_copy(
                src_ref=g_refs[a].at[rows(a, k, half), :], dst_ref=sib[a].at[k],
                send_sem=s1_send.at[a, k], recv_sem=s1_recv.at[a, k], device_id=sibling, device_id_type=MESH)

        def s2(a, i, to):
            return pltpu.make_async_remote_copy(
                src_ref=stage[a].at[i], dst_ref=land[a].at[i], send_sem=s2_send.at[a, i], recv_sem=s2_recv.at[a, i],
                device_id=to, device_id_type=MESH)

        via = {0: xn, 1: xn, 2: yn, 3: yn, 4: yn, 5: xn}

        def s3(a, half, to):
            return pltpu.make_async_remote_copy(
                src_ref=tot[a], dst_ref=r_refs[a].at[rows(a, 0, half), :], send_sem=s3_send.at[a],
                recv_sem=s3_recv.at[a], device_id=to, device_id_type=MESH)

        def store(a):
            return pltpu.make_async_copy(tot[a], r_refs[a].at[rows(a, 0, c), :], st_sem.at[a])

        def start():
            for k in order:
                for a in range(na):
                    load(a, k).start()
                    s1(a, k, 1 - c).start()

        def chip_sum(a, k):
            load(a, k).wait()
            s1(a, k, c).wait_recv()
            return mine[a][k] + sib[a][k]

        def exchange():
            for a in range(na):
                P, Q = piece(a, 0), piece(a, 1)
                s_xn = chip_sum(a, idx(xn))
                stage[a][0] = s_xn[P].astype(BF16)
                keep[a][1] = s_xn[Q]
                s_dg = chip_sum(a, idx(dg))
                stage[a][1] = s_dg[P].astype(BF16)
                s2(a, 0, (*xn, c)).start()
                s2(a, 1, (*xn, c)).start()
                stage[a][3] = s_dg[Q].astype(BF16)
                s_yn = chip_sum(a, idx(yn))
                stage[a][2] = s_yn[Q].astype(BF16)
                keep[a][0] = s_yn[P]
                s2(a, 2, (*yn, c)).start()
                s2(a, 3, (*yn, c)).start()
                tot[a][...] = chip_sum(a, my_chip)

        def relay():
            for a in range(na):
                P, Q = piece(a, 0), piece(a, 1)
                s2(a, 1, me).wait_recv()
                stage[a][4] = (keep[a][0] + land[a][1].astype(F32)).astype(BF16)
                s2(a, 4, (*yn, c)).start()
                s2(a, 3, me).wait_recv()
                stage[a][5] = (keep[a][1] + land[a][3].astype(F32)).astype(BF16)
                s2(a, 5, (*xn, c)).start()
                s2(a, 0, me).wait_recv()
                tot[a][P, :] += land[a][0].astype(F32)
                s2(a, 2, me).wait_recv()
                tot[a][Q, :] += land[a][2].astype(F32)

        def finish():
            for a in range(na):
                P, Q = piece(a, 0), piece(a, 1)
                s2(a, 4, me).wait_recv()
                tot[a][P, :] += land[a][4].astype(F32)
                s2(a, 5, me).wait_recv()
                tot[a][Q, :] += land[a][5].astype(F32)
                s3(a, c, sibling).start()
                store(a).start()

        def drain():
            for a in range(na):
                s3(a, 1 - c, me).wait_recv()
                store(a).wait()
            for a in range(na):
                for k in order:
                    s1(a, k, 1 - c).wait_send()
                for i in range(6):
                    s2(a, i, (*via[i], c)).wait_send()
                s3(a, c, sibling).wait_send()

        return start, exchange, relay, finish, drain

    def part(self, grads, steps):
        def body(*refs):
            na = len(self.shapes)
            i = pl.program_id(0)
            for step, phase in zip(steps, self.bind(refs[:na], refs[na:2 * na], refs[2 * na:])):
                pl.when(i == step)(phase)

        hbm = pl.BlockSpec(memory_space=pl.ANY)
        return _Part(body, list(grads), [hbm] * len(grads), [hbm] * len(grads),
                     [_sds((n, w), F32) for n, w in self.shapes], self.scratch_shapes())


def _dh_dx(dqa, dka, dva, dqn, dkn, dvn, dq4, dk4, dv4, dq16, dk16, dv16, dqc, dz, du, xb, cos, sa, sb, winT):
    tm = 512
    spt = SEQ // tm

    def body(dqa_ref, dka_ref, dva_ref, dqn_ref, dkn_ref, dvn_ref, dq4_ref, dk4_ref, dv4_ref,
             dq16_ref, dk16_ref, dv16_ref, dqc_ref, dz_ref, du_ref, xb_ref, cos_ref, sa_ref, sb_ref, w_ref,
             gx_ref, db_ref, gin_ref, dh_ref, scr):
        i = pl.program_id(0)

        @pl.when(i == 0)
        def _():
            db_ref[...] = jnp.zeros_like(db_ref)
            gin_ref[...] = jnp.zeros_like(gin_ref)

        cos_t, sa_t, sb_t = cos_ref[...], sa_ref[...], sb_ref[...]

        def rope_t(t):
            return _rope(t, cos_t, sa_t, sb_t, -1)

        def put(r0, val):
            n = val.shape[1]
            dh_ref[:, r0:r0 + n] = val.astype(BF16)
            db_ref[:, r0:r0 + n] += jnp.sum(val, axis=0, keepdims=True)

        put(O_QA, rope_t(dqa_ref[...].astype(F32)) * QK_SCALE)
        put(O_KA, rope_t(dka_ref[...].astype(F32)))
        put(O_VA, dva_ref[...].astype(F32))
        put(O_QC, dqc_ref[...].astype(F32) * QK_SCALE)
        put(O_Z, dz_ref[...].astype(F32))
        for k, (n_ref, r4, r16) in enumerate(((dqn_ref, dq4_ref, dq16_ref), (dkn_ref, dk4_ref, dk16_ref),
                                               (dvn_ref, dv4_ref, dv16_ref))):
            for j in range(2):
                sl = slice(128 * j, 128 * (j + 1))
                a, q = 2 * k + j, tm // 4
                scr[a] = n_ref[:, sl].astype(F32)
                for res in range(16):
                    scr[6 + a, pl.ds((res % 4) * q + res // 4, tm // 16, stride=4), :] = r16[0, res, :, sl].astype(F32)
                for res in range(4):
                    scr[a, pl.ds(res, q, stride=4), :] += (scr[6 + a, res * q:(res + 1) * q, :]
                                                           + r4[0, res, :, sl].astype(F32))
        cat = lambda a: jnp.concatenate([scr[a], scr[a + 1]], axis=1)
        put(O_QB, rope_t(cat(0)) * QK_SCALE)
        put(O_KB, rope_t(cat(2)))
        put(O_VB, cat(4))
        gx_ref[...] = _dot(dh_ref[...], w_ref[...], NN) + ALPHA * du_ref[...].astype(F32)
        gin_ref[...] += _dot(dh_ref[...], xb_ref[...], TN)

    tok = lambda w: pl.BlockSpec((tm, w), lambda i: (i, 0))
    tab = pl.BlockSpec((tm, 128), lambda i: (i % spt, 0))
    p4 = pl.BlockSpec((1, 4, tm // 4, W_B), lambda i: (i // spt, 0, i % spt, 0))
    p16 = pl.BlockSpec((1, 16, tm // 16, W_B), lambda i: (i // spt, 0, i % spt, 0))
    once = lambda shape: pl.BlockSpec(shape, lambda i: (0, 0), pipeline_mode=pl.Buffered(1))
    return pl.pallas_call(
        body, name="dh_dx", grid=(T // tm,),
        in_specs=[tok(W_A), tok(W_KV_A), tok(W_KV_A), tok(W_B), tok(W_B), tok(W_B), p4, p4, p4, p16, p16, p16,
                  tok(W_C), tok(D_MIX), tok(D_MODEL), tok(D_MODEL), tab, tab, tab, once((D_IN, D_MODEL))],
        out_specs=(tok(D_MODEL), _full((1, D_IN)), once((D_IN, D_MODEL))),
        out_shape=(_sds((T, D_MODEL), F32), _sds((1, D_IN), F32), _sds((D_IN, D_MODEL), F32)),
        scratch_shapes=[pltpu.VMEM((tm, D_IN), BF16), pltpu.VMEM((12, tm, 128), F32)],
        compiler_params=_cp(("arbitrary",), vmem_mb=56),
    )(*_pin(dqa, dka, dva, dqn, dkn, dvn, dq4, dk4, dv4, dq16, dk16, dv16, dqc, dz, du, xb, cos, sa, sb, winT))


def _reduce_grads(g_in, acc, dbin, dsink):
    rs = _ReduceScatter([(SH_IN, D_MODEL)])

    def body(g_ref, acc_ref, dbin_ref, dsink_ref, r_ref, sv_ref, sv_mine, sv_all, sv_send, sv_recv, *rs_scratch):
        x, y, c = lax.axis_index("x"), lax.axis_index("y"), lax.axis_index("c")
        chips = [(1 - x, y), (x, 1 - y), (1 - x, 1 - y)]
        start, exchange, relay, finish, drain = rs.bind((g_ref,), (r_ref,), rs_scratch)
        start()

        sv_mine[...] = jnp.zeros_like(sv_mine)
        sv_mine[0:4, :] = acc_ref[0:4, :]
        for k, c0 in enumerate(range(0, D_IN, SV_W)):
            n = min(SV_W, D_IN - c0)
            sv_mine[SV_DB + k:SV_DB + k + 1, 0:n] = dbin_ref[:, c0:c0 + n]
        sv_mine[SV_SINK:SV_SINK + 1, 0:128] = dsink_ref[...]
        my_dev = 4 * x + 2 * y + c
        others = [(x, y, 1 - c)] + [(*chip, cc) for chip in chips for cc in (c, 1 - c)]

        def sv_copy(j, to):
            return pltpu.make_async_remote_copy(
                src_ref=sv_mine, dst_ref=sv_all.at[my_dev], send_sem=sv_send.at[j], recv_sem=sv_recv.at[j],
                device_id=to, device_id_type=MESH)

        sv_sends = [sv_copy(j, to) for j, to in enumerate(others)]
        for cp in sv_sends:
            cp.start()
        exchange()
        relay()
        finish()
        sv_all[my_dev] = sv_mine[...]
        for j in range(7):
            sv_copy(j, (x, y, c)).wait_recv()
        tot = sv_all[0]
        for d in range(1, 8):
            tot = tot + sv_all[d]
        sv_ref[...] = tot
        drain()
        for cp in sv_sends:
            cp.wait_send()

    vm = pl.BlockSpec(memory_space=pltpu.VMEM)
    hbm = pl.BlockSpec(memory_space=pl.ANY)
    return pl.pallas_call(
        body, name="reduce_grads",
        out_shape=(_sds((SH_IN, D_MODEL), F32), _vm_sds((8, SV_W), F32)),
        in_specs=[hbm, vm, vm, vm], out_specs=(hbm, vm),
        scratch_shapes=[pltpu.VMEM((8, SV_W), F32), pltpu.VMEM((8, 8, SV_W), F32),
                        pltpu.SemaphoreType.DMA((7,)), pltpu.SemaphoreType.DMA((7,))] + rs.scratch_shapes(),
        compiler_params=_cp(vmem_mb=40),
    )(pltpu.with_memory_space_constraint(g_in, pltpu.HBM), acc, dbin, dsink)


def _adamw_update(w, g, m, v):
    nm = ADAM_B1 * m + (1.0 - ADAM_B1) * g
    nv = ADAM_B2 * v + (1.0 - ADAM_B2) * (g * g)
    m_hat = nm / (1.0 - ADAM_B1 ** ADAM_STEP)
    v_hat = nv / (1.0 - ADAM_B2 ** ADAM_STEP)
    return -ADAM_LR * (m_hat / (jnp.sqrt(v_hat) + ADAM_EPS) + ADAM_WD * w), nm, nv


SMALL = ((SV_DB, D_IN, 1.0), (SV_SINK, 8, -1.0), (1, D_MIX, 1.0), (2, D_MODEL, 1.0), (3, D_MODEL, 1.0))


def _adamw_all(items, sv, ws, ms, vs, n_steps=4):
    nb, ns = 4 * len(items), len(SMALL)

    def body(*refs):
        ins, sv_ref, small_in = refs[:nb], refs[nb], refs[nb + 1:nb + 1 + 3 * ns]
        outs = refs[nb + 1 + 3 * ns:]
        big_out, loss_ref, small_out = outs[:nb], outs[nb], outs[nb + 1:]
        for p in range(len(items)):
            w_ref, g_ref, m_ref, v_ref = ins[4 * p:4 * p + 4]
            gv = g_ref[...]
            big_out[4 * p][...] = gv
            big_out[4 * p + 1][...], big_out[4 * p + 2][...], big_out[4 * p + 3][...] = _adamw_update(
                w_ref[...], gv, m_ref[...], v_ref[...])

        @pl.when(pl.program_id(0) == 0)
        def _():
            loss_ref[...] = jnp.sum(sv_ref[0:1, 0:D_MODEL], axis=1, keepdims=True)
            for p, (row, width, sign) in enumerate(SMALL):
                gv = sign * jnp.concatenate([sv_ref[row + k:row + k + 1, 0:min(SV_W, width - c0)]
                                             for k, c0 in enumerate(range(0, width, SV_W))], axis=1)
                small_out[4 * p][...] = gv
                small_out[4 * p + 1][...], small_out[4 * p + 2][...], small_out[4 * p + 3][...] = _adamw_update(
                    small_in[p][...], gv, small_in[ns + p][...], small_in[2 * ns + p][...])

    specs, shapes, args = [], [], []
    for w, g, m, v in items:
        rows, width = w.shape
        specs += [pl.BlockSpec((rows // n_steps, width), lambda i: (i, 0))] * 4
        shapes += [_sds((rows, width), F32)] * 4
        args += [w, g, m, v]
    small_args = [*ws, *ms, *vs]
    whole = lambda a: _full(a.shape)
    res = pl.pallas_call(
        body, name="adamw", grid=(n_steps,),
        in_specs=specs + [whole(sv)] + [whole(a) for a in small_args],
        out_specs=tuple(specs + [_full((1, 1))] + [whole(w) for w in ws for _ in range(4)]),
        out_shape=tuple(shapes + [_sds((1, 1), F32)] + [_sds(w.shape, F32) for w in ws for _ in range(4)]),
        compiler_params=_cp(("arbitrary",), vmem_mb=40),
    )(*_pin(*args, sv, *small_args))
    big = [tuple(res[4 * p:4 * p + 4]) for p in range(len(items))]
    return big, res[nb], [tuple(res[nb + 1 + 4 * p:nb + 5 + 4 * p]) for p in range(ns)]


def _rope_tables():
    pos = np.arange(SEQ, dtype=np.float32)
    inv = (np.float32(ROPE_THETA) ** (-np.arange(0, 64, 2, dtype=np.float32) / np.float32(64))).astype(np.float32)
    ang = np.tile(pos[:, None] * inv[None, :], (1, 4))
    cos, sin = np.cos(ang).astype(np.float32), np.sin(ang).astype(np.float32)
    low = (np.arange(128) % 64) < 32
    zero = np.float32(0.0)
    return jnp.asarray(cos), jnp.asarray(np.where(low, -sin, zero)), jnp.asarray(np.where(low, zero, sin))


def _local_step(x2, mem2, tgt2, winT, wout, wmem, b_in, sinks, g_branch, ln_gain, ln_bias):
    cos, sa, sb = _rope_tables()
    sinkv = jnp.pad(sinks, ((0, 0), (0, 120)))
    head_of_lane = np.arange(512)[:, None] // 64
    gather8 = jnp.asarray(head_of_lane == np.arange(128)[None, :], BF16)
    gather4 = jnp.asarray(head_of_lane[:W_B] == np.arange(128)[None, :], BF16)
    spread4 = jnp.asarray((head_of_lane[:W_B] == np.arange(128)[None, :]).T, BF16)

    xb, qa, ka, va, bn, b4, b16, qc, z, wout, wmem = _in_proj(x2, winT, b_in, cos, sa, sb, wout, wmem)
    memb, mkv = _mem_kv(mem2, wmem)
    b4f, b16f = b4.reshape(T, 768), b16.reshape(T, 768)

    swa = dict(kind="band", nb=SEQ // BLK, max_dist=BLK - 1, gqa=True)
    dil = (dict(kind="band", nb=SEQ // BLK), dict(kind="band", nb=SEQ // 4 // BLK), dict(kind="band", nb=1))
    (oa, lse_a), (o1, l1), (o4, l4), (o16, l16), (oc, lse_c) = _run_parts("attn_fwd", [
        _attn_fwd(qa, 0, W_A, ka, 0, va, 0, W_KV_A, sinks=sinks, **swa),
        _attn_fwd(bn, 0, W_B, bn, 1, bn, 2, W_B, **dil[0]),
        _attn_fwd(b4f, 0, W_B, b4f, 1, b4f, 2, W_B, **dil[1]),
        _attn_fwd(b16f, 0, W_B, b16f, 1, b16f, 2, W_B, **dil[2]),
        _attn_fwd(qc, 0, W_C, mkv, 0, mkv, 1, W_C, kind="mem")], "parallel", 48)

    s4 = lambda w: (B_LOC, 4, SEQ // 4, w)
    s16 = lambda w: (B_LOC, 16, SEQ // 16, w)
    (du, dz, doa, dla, dobn, lsen, dlbn, dob4, lse4, dlb4, dob16, lse16, dlb16, doc, dlc, acc, g_out) = _middle(
        oa, o1, l1, o4.reshape(s4(W_B)), l4.reshape(s4(128)), o16.reshape(s16(W_B)), l16.reshape(s16(128)), oc, z,
        x2, tgt2, g_branch, ln_gain, ln_bias, wout, spread4, gather4, gather8)

    flat = lambda a: a.reshape(T, a.shape[-1])
    (dqa, dka, dva, dsink), (dqc, g_mem) = _run_parts("attn_bwd_a", [
        _attn_bwd(qa, 0, W_A, ka, 0, va, 0, W_KV_A, doa, lse_a, dla, sinkv=sinkv, **swa),
        _attn_bwd(qc, 0, W_C, mkv, 0, mkv, 1, W_C, doc, lse_c, dlc, kind="mem", mem_in=memb)], "arbitrary", 48)
    last = T // QR - 1
    (r_out, r_mem), (dqn, dkn, dvn), (dq4, dk4, dv4), (dq16, dk16, dv16) = _run_parts("attn_bwd_b", [
        _ReduceScatter([(SH_OUT, D_MODEL), (SH_MEM, 2 * W_C)]).part((g_out, g_mem), (0, 1, 2, last, last)),
        _attn_bwd(bn, 0, W_B, bn, 1, bn, 2, W_B, dobn, lsen, dlbn, **dil[0]),
        _attn_bwd(b4f, 0, W_B, b4f, 1, b4f, 2, W_B, flat(dob4), flat(lse4), flat(dlb4), **dil[1]),
        _attn_bwd(b16f, 0, W_B, b16f, 1, b16f, 2, W_B, flat(dob16), flat(lse16), flat(dlb16), **dil[2])],
        "arbitrary", 62)

    r4 = lambda a: a.reshape(s4(W_B))
    r16 = lambda a: a.reshape(s16(W_B))
    gx, dbin, g_in = _dh_dx(dqa, dka, dva, dqn, dkn, dvn, r4(dq4), r4(dk4), r4(dv4), r16(dq16), r16(dk16),
                            r16(dv16), dqc, dz, du, xb, cos, sa, sb, winT)
    return gx, g_in, r_out, r_mem, acc, dbin, dsink


def kernel(x, mem, w_in, b_in, w_mem, attn_sinks, g_branch, w_out, ln_gain, ln_bias, loss_target, m_w_in, m_b_in, m_w_mem, m_attn_sinks, m_g_branch, m_w_out, m_ln_gain, m_ln_bias, v_w_in, v_b_in, v_w_mem, v_attn_sinks, v_g_branch, v_w_out, v_ln_gain, v_ln_bias):
    winT, wout, wmem = _gather_weights(w_in[0].T, w_out[0], w_mem[0])
    gx, g_in, r_out, r_mem, acc, dbin, dsink = _local_step(
        x.reshape(T, D_MODEL), mem.reshape(B_LOC * MEM_LEN, D_MODEL), loss_target.reshape(T, D_MODEL),
        winT, wout, wmem, b_in, attn_sinks, g_branch, ln_gain, ln_bias)
    r_in, sv = _reduce_grads(g_in, acc, dbin, dsink)

    small = ["b_in", "attn_sinks", "g_branch", "ln_gain", "ln_bias"]
    big, loss, steps = _adamw_all(
        [(w_in[0].T, r_in, m_w_in[0].T, v_w_in[0].T), (w_out[0], r_out, m_w_out[0], v_w_out[0]),
         (w_mem[0], r_mem, m_w_mem[0], v_w_mem[0])],
        sv, [b_in, attn_sinks, g_branch, ln_gain, ln_bias], [m_b_in, m_attn_sinks, m_g_branch, m_ln_gain, m_ln_bias],
        [v_b_in, v_attn_sinks, v_g_branch, v_ln_gain, v_ln_bias])
    out = dict(zip(small, steps))
    out["w_in"] = tuple(a.T[None] for a in big[0])
    out["w_out"], out["w_mem"] = (tuple(a[None] for a in st) for st in big[1:])
    names = ["w_in", "b_in", "w_mem", "attn_sinks", "g_branch", "w_out", "ln_gain", "ln_bias"]
    return (loss.reshape(()), gx.reshape(B_LOC, SEQ, D_MODEL), *[out[n][k] for k in range(4) for n in names])
```

```python
import jax
import jax.numpy as jnp
import numpy as np
from jax import lax
from jax.experimental import pallas as pl
from jax.experimental.pallas import tpu as pltpu

F32, BF16 = jnp.float32, jnp.bfloat16

D_MODEL = 1024
SEQ = 2048
B_LOC = 2
T = B_LOC * SEQ
BLK = 128
MEM_LEN = 256
W_A, W_KV_A, W_B, W_C, D_MIX = 512, 128, 256, 256, 1024
D_IN = 2816
O_QA, O_KA, O_VA, O_QB, O_KB, O_VB, O_QC, O_Z = 0, 512, 640, 768, 1024, 1280, 1536, 1792
ROPE_THETA = 10000.0
LN_EPS = 1e-5
RMS_EPS = 1e-6
ALPHA = 2.0 ** 0.25
QK_SCALE = 0.125
N_CHIP = 4
SH_IN, SH_OUT, SH_MEM = D_IN // N_CHIP, D_MIX // N_CHIP, D_MODEL // N_CHIP
NEG = -1e30
ADAM_LR, ADAM_B1, ADAM_B2, ADAM_EPS, ADAM_WD, ADAM_STEP = 0.001, 0.9, 0.999, 1e-08, 0.01, 10
SV_W = 1024
SV_DB, SV_SINK = 4, 7
assert D_MODEL == D_MIX == SV_W and D_IN <= (SV_SINK - SV_DB) * SV_W
MESH = pl.DeviceIdType.MESH

NN = ((1,), (0,))
NT = ((1,), (1,))
TN = ((0,), (0,))


def _dot(a, b, dims):
    return lax.dot_general(a, b, (dims, ((), ())), preferred_element_type=F32)


def _cp(sem=None, vmem_mb=None):
    kw = {}
    if sem is not None:
        kw["dimension_semantics"] = sem
    if vmem_mb is not None:
        kw["vmem_limit_bytes"] = vmem_mb * 1024 * 1024
    return pltpu.CompilerParams(**kw)


def _sds(shape, dtype):
    return pltpu.HBM(shape, dtype)


def _vm_sds(shape, dtype):
    return jax.ShapeDtypeStruct(shape, dtype)


def _pin(*args):
    return [pltpu.with_memory_space_constraint(a, pltpu.HBM) for a in args]


def _full(shape):
    n = len(shape)
    return pl.BlockSpec(shape, lambda *_: (0,) * n)


def _shard_rows(ref, n, chip, half):
    start = pl.multiple_of((2 * chip[0] + chip[1]) * n + half * (n // 2), 16)
    return ref.at[pl.ds(start, n // 2), :]


def _gather_weights(win_sh, wout_sh, wmem_sh):
    half, piece = SH_IN // 2, SH_IN // 4
    shards = ((SH_IN, D_MODEL), (SH_OUT, D_MODEL), (SH_MEM, 2 * W_C))

    def body(a_ref, b_ref, c_ref, oa_ref, ob_ref, oc_ref, raw_a, raw_b, raw_c, own_a, own_b, own_c,
             load_sem, store_sem, ici_send, ici_recv, d2d_send, d2d_recv):
        x, y, c = lax.axis_index("x"), lax.axis_index("y"), lax.axis_index("c")
        me, sibling = (x, y, c), (x, y, 1 - c)
        xn, yn, dg = (1 - x, y), (x, 1 - y), (1 - x, 1 - y)
        srcs, raws = (a_ref, b_ref, c_ref), (raw_a, raw_b, raw_c)
        owns, outs = (own_a, own_b, own_c), (oa_ref, ob_ref, oc_ref)
        loads = [pltpu.make_async_copy(srcs[a], raws[a], load_sem.at[a]) for a in range(3)]
        for cp in loads:
            cp.start()

        def rows(chip, hf, q):
            start = pl.multiple_of((2 * chip[0] + chip[1]) * SH_IN + hf * half + q * piece, 16)
            return oa_ref.at[pl.ds(start, piece), :]

        def copy(sems, k, chip, hf, q, to, src=None):
            blk = rows(chip, hf, q)
            return pltpu.make_async_remote_copy(
                src_ref=blk if src is None else src, dst_ref=blk, send_sem=sems[0].at[k], recv_sem=sems[1].at[k],
                device_id=to, device_id_type=MESH)

        def my_piece(q):
            return own_a.at[pl.ds(pl.multiple_of(c * half + q * piece, 16), piece), :]

        ici, d2d = (ici_send, ici_recv), (d2d_send, d2d_recv)
        stores, direct = [], []
        for a, (n, _) in enumerate(shards):
            loads[a].wait()
            owns[a][...] = raws[a][...].astype(BF16)
            mine = pl.ds(pl.multiple_of((2 * x + y) * n, 16), n)
            stores.append(pltpu.make_async_copy(owns[a], outs[a].at[mine, :], store_sem.at[a]))
            stores[-1].start()
            if a == 0:
                direct = [copy(ici, 0, (x, y), c, 0, (*xn, c), my_piece(0)),
                          copy(ici, 1, (x, y), c, 1, (*xn, c), my_piece(1)),
                          copy(ici, 3, (x, y), c, 0, (*yn, c), my_piece(0)),
                          copy(ici, 4, (x, y), c, 1, (*yn, c), my_piece(1))]
                for cp in direct:
                    cp.start()
        arrivals = [(0, xn, 0), (1, xn, 1), (3, yn, 0), (4, yn, 1), (2, dg, 1), (5, dg, 0)]
        passed = []
        for k, chip, q in arrivals:
            copy(ici, k, chip, c, q, me).wait_recv()
            if k == 0:
                passed.append(copy(ici, 5, xn, c, 0, (*yn, c)))
                passed[-1].start()
            if k == 4:
                passed.append(copy(ici, 2, yn, c, 1, (*xn, c)))
                passed[-1].start()
            passed.append(copy(d2d, k, chip, c, q, sibling))
            passed[-1].start()
        for k, chip, q in arrivals:
            copy(d2d, k, chip, 1 - c, q, me).wait_recv()
        for cp in direct + passed:
            cp.wait_send()
        for cp in stores:
            cp.wait()

    hbm = pl.BlockSpec(memory_space=pl.ANY)
    return pl.pallas_call(
        body, name="gather_weights",
        out_shape=(_sds((D_IN, D_MODEL), BF16), _sds((D_MIX, D_MODEL), BF16), _sds((D_MODEL, 2 * W_C), BF16)),
        in_specs=[hbm, hbm, hbm], out_specs=(hbm, hbm, hbm),
        scratch_shapes=([pltpu.VMEM(sh, F32) for sh in shards] + [pltpu.VMEM(sh, BF16) for sh in shards]
                        + [pltpu.SemaphoreType.DMA((3,))] * 2 + [pltpu.SemaphoreType.DMA((6,))] * 4),
        compiler_params=_cp(vmem_mb=40),
    )(*_pin(win_sh, wout_sh, wmem_sh))


def _rope(t, cos, sa, sb, sign):
    w = t.shape[1]
    reps = w // 128
    c, a, b = (jnp.tile(v, (1, reps)) if reps > 1 else v for v in (cos, sa, sb))
    rot = pltpu.roll(t, w - 32, 1) * a + pltpu.roll(t, 32, 1) * b
    return t * c + rot if sign > 0 else t * c - rot


def _in_proj(x, winT, b_in, cos, sa, sb, wout_own, wmem_own):
    tm = 512
    spt = SEQ // tm
    n_steps = T // tm
    forward_step = n_steps // 2

    def body(x_ref, w_ref, b_ref, cos_ref, sa_ref, sb_ref, wo_in, wm_in,
             xb_ref, qa_ref, ka_ref, va_ref, bn_ref, b4_ref, b16_ref, qc_ref, z_ref, wo_ref, wm_ref,
             scr, ici_send, ici_recv, d2d_send, d2d_recv):
        i = pl.program_id(0)
        mx, my, mc = lax.axis_index("x"), lax.axis_index("y"), lax.axis_index("c")
        chips = [(1 - mx, my), (mx, 1 - my), (1 - mx, 1 - my)]
        full = ((wo_ref, SH_OUT), (wm_ref, SH_MEM))

        def copy(sems, a, j, chip_of_block, half, to):
            blk = _shard_rows(full[a][0], full[a][1], chip_of_block, half)
            return pltpu.make_async_remote_copy(
                src_ref=blk, dst_ref=blk, send_sem=sems[0].at[a, j], recv_sem=sems[1].at[a, j],
                device_id=to, device_id_type=MESH)

        ici, d2d = (ici_send, ici_recv), (d2d_send, d2d_recv)
        pairs = [(a, j, chip) for j, chip in enumerate(chips) for a in range(2)]

        @pl.when(i == 0)
        def _():
            for a, j, chip in pairs:
                copy(ici, a, j, (mx, my), mc, (*chip, mc)).start()

        @pl.when(i == forward_step)
        def _():
            for a, j, chip in pairs:
                copy(ici, a, j, chip, mc, (mx, my, mc)).wait_recv()
                copy(d2d, a, j, chip, mc, (mx, my, 1 - mc)).start()

        @pl.when(i == n_steps - 1)
        def _():
            for a, j, chip in pairs:
                copy(d2d, a, j, chip, 1 - mc, (mx, my, mc)).wait_recv()
            for a, j, chip in pairs:
                copy(ici, a, j, (mx, my), mc, (*chip, mc)).wait_send()
                copy(d2d, a, j, chip, mc, (mx, my, 1 - mc)).wait_send()

        xb = x_ref[...].astype(BF16)
        xb_ref[...] = xb
        cos_t, sa_t, sb_t = cos_ref[...], sa_ref[...], sb_ref[...]

        def proj(r0, n):
            return _dot(xb, w_ref[r0:r0 + n, :], NT) + b_ref[:, r0:r0 + n]

        def rope(t):
            return _rope(t, cos_t, sa_t, sb_t, +1)

        parts = (rope(proj(O_QB, W_B)) * QK_SCALE, rope(proj(O_KB, W_B)), proj(O_VB, W_B))
        for k, part in enumerate(parts):
            bn_ref[:, 256 * k:256 * (k + 1)] = part.astype(BF16)
            scr[2 * k] = part[:, :128]
            scr[2 * k + 1] = part[:, 128:]
        for j in range(6):
            lanes = slice(128 * j, 128 * (j + 1))
            for res in range(4):
                t = scr[j, pl.ds(res, tm // 4, stride=4), :]
                b4_ref[0, res, :, lanes] = t.astype(BF16)
                scr[6 + j, res * (tm // 4):(res + 1) * (tm // 4), :] = t
            for res in range(16):
                b16_ref[0, res, :, lanes] = scr[6 + j, pl.ds((res % 4) * (tm // 4) + res // 4, tm // 16, stride=4),
                                                :].astype(BF16)
        qa_ref[...] = (rope(proj(O_QA, W_A)) * QK_SCALE).astype(BF16)
        assert O_VA == O_KA + W_KV_A
        kv = proj(O_KA, 2 * W_KV_A)
        ka_ref[...] = rope(kv[:, :W_KV_A]).astype(BF16)
        va_ref[...] = kv[:, W_KV_A:].astype(BF16)
        qc_ref[...] = (proj(O_QC, W_C) * QK_SCALE).astype(BF16)
        z_ref[...] = proj(O_Z, D_MIX).astype(BF16)

    tok = lambda w: pl.BlockSpec((tm, w), lambda i: (i, 0))
    tab = pl.BlockSpec((tm, 128), lambda i: (i % spt, 0))
    hbm = pl.BlockSpec(memory_space=pl.ANY)
    return pl.pallas_call(
        body, name="in_proj", grid=(n_steps,),
        in_specs=[tok(D_MODEL), _full((D_IN, D_MODEL)), _full((1, D_IN)), tab, tab, tab, hbm, hbm],
        out_specs=(tok(D_MODEL), tok(W_A), tok(W_KV_A), tok(W_KV_A), tok(768),
                   pl.BlockSpec((1, 4, tm // 4, 768), lambda i: (i // spt, 0, i % spt, 0)),
                   pl.BlockSpec((1, 16, tm // 16, 768), lambda i: (i // spt, 0, i % spt, 0)),
                   tok(W_C), tok(D_MIX), hbm, hbm),
        out_shape=(_sds((T, D_MODEL), BF16), _sds((T, W_A), BF16), _sds((T, W_KV_A), BF16), _sds((T, W_KV_A), BF16),
                   _sds((T, 768), BF16), _sds((B_LOC, 4, SEQ // 4, 768), BF16), _sds((B_LOC, 16, SEQ // 16, 768), BF16),
                   _sds((T, W_C), BF16), _sds((T, D_MIX), BF16),
                   _sds((D_MIX, D_MODEL), BF16), _sds((D_MODEL, 2 * W_C), BF16)),
        input_output_aliases={6: 9, 7: 10},
        scratch_shapes=[pltpu.VMEM((12, tm, 128), F32)] + [pltpu.SemaphoreType.DMA((2, 3))] * 4,
        compiler_params=_cp(("arbitrary",), vmem_mb=48),
    )(*_pin(x, winT, b_in, cos, sa, sb, wout_own, wmem_own))


def _mem_kv(mem, wmem):
    def body(m_ref, w_ref, mb_ref, kv_ref):
        mb = m_ref[...].astype(BF16)
        mb_ref[...] = mb
        kv_ref[...] = _dot(mb, w_ref[...], NN).astype(BF16)

    n = B_LOC * MEM_LEN
    return pl.pallas_call(
        body, name="mem_kv",
        out_shape=(_sds((n, D_MODEL), BF16), _sds((n, 2 * W_C), BF16)),
    )(*_pin(mem, wmem))


class _Part:
    def __init__(self, body, args, in_specs, out_specs, out_shape, scratch=()):
        self.body, self.args, self.in_specs, self.out_specs, self.out_shape = body, args, in_specs, out_specs, out_shape
        self.scratch = list(scratch)


def _run_parts(name, parts, semantics, vmem_mb):
    n_in = [len(p.args) for p in parts]
    n_out = [len(p.out_shape) for p in parts]
    n_scr = [len(p.scratch) for p in parts]

    def body(*refs):
        ins, outs, scr = refs[:sum(n_in)], refs[sum(n_in):sum(n_in) + sum(n_out)], refs[sum(n_in) + sum(n_out):]
        i0 = o0 = s0 = 0
        for p, ni, no, ns in zip(parts, n_in, n_out, n_scr):
            p.body(*ins[i0:i0 + ni], *outs[o0:o0 + no], *scr[s0:s0 + ns])
            i0, o0, s0 = i0 + ni, o0 + no, s0 + ns

    res = pl.pallas_call(
        body, name=name, grid=(T // QR,),
        in_specs=[sp for p in parts for sp in p.in_specs], out_specs=tuple(sp for p in parts for sp in p.out_specs),
        out_shape=tuple(sh for p in parts for sh in p.out_shape),
        scratch_shapes=[sc for p in parts for sc in p.scratch],
        compiler_params=_cp((semantics,), vmem_mb=vmem_mb),
    )(*_pin(*[a for p in parts for a in p.args]))
    out, o0 = [], 0
    for no in n_out:
        out.append(tuple(res[o0:o0 + no]))
        o0 += no
    return out


QB = 8
QR = QB * BLK


def _lane_lo():
    return lax.broadcasted_iota(jnp.int32, (1, 128), 1) < 64


def _dup_head(k2, hk, lo):
    kf = k2.astype(F32)
    r = pltpu.roll(kf, 64, 1)
    return (jnp.where(lo, kf, r) if hk == 0 else jnp.where(lo, r, kf)).astype(BF16)


def _stack_heads(pairs, lo):
    parts = []
    for x2 in pairs:
        z = jnp.zeros_like(x2)
        parts += [jnp.where(lo, x2, z), jnp.where(lo, z, x2)]
    return jnp.concatenate(parts, axis=0)


def _prev_mode(kind, nb, j):
    if kind == "mem" or nb == 1:
        return "no"
    if nb <= QB:
        return "yes" if j % nb else "no"
    return "yes" if j else "dyn"


class _Attn:
    def __init__(self, kind, nb, max_dist, gqa, qw, kvw, qcb, kcb, vcb):
        self.kind, self.nb, self.gqa, self.qw, self.kvw = kind, nb, gqa, qw, kvw
        npairs = qw // 128
        self.groups = ([(hk, [2 * hk, 2 * hk + 1]) for hk in range(npairs // 2)] if gqa
                       else [(p, [p]) for p in range(npairs)])
        self.nh = 2 * len(self.groups[0][1])
        self.cols = 128 * self.nh
        self.reach = BLK - max_dist
        self.ext_prev = kind == "band" and nb > QB
        self.q_spec = pl.BlockSpec((QR, qw), lambda g: (g, qcb))
        self.row_spec = pl.BlockSpec((QR, qw), lambda g: (g, 0))
        self.stat_spec = pl.BlockSpec((QR, 128), lambda g: (g, 0))
        if kind == "mem":
            per = SEQ // QR
            self.kv_specs = [pl.BlockSpec((MEM_LEN, kvw), lambda g: (g // per, kcb)),
                             pl.BlockSpec((MEM_LEN, kvw), lambda g: (g // per, vcb))]
        else:
            self.kv_specs = [pl.BlockSpec((QR, kvw), lambda g: (g, kcb)), pl.BlockSpec((QR, kvw), lambda g: (g, vcb))]
            if self.ext_prev:
                self.kv_specs += [pl.BlockSpec((BLK, kvw), lambda g: (jnp.maximum(g * QB - 1, 0), kcb)),
                                  pl.BlockSpec((BLK, kvw), lambda g: (jnp.maximum(g * QB - 1, 0), vcb))]

    def masks(self):
        if self.kind == "mem":
            return None
        kj = lax.broadcasted_iota(jnp.int32, (2 * BLK, self.cols), 0)
        qi = lax.broadcasted_iota(jnp.int32, (2 * BLK, self.cols), 1) & (BLK - 1)
        both = jnp.logical_and(kj >= qi + self.reach, kj <= qi + BLK)
        return kj, qi, self.as_mask(both), self.as_mask(kj[:BLK] <= qi[:BLK])

    def as_mask(self, in_reach):
        return jnp.where(in_reach, 0.0, NEG) if self.gqa else in_reach

    def hide(self, s, mask):
        return s + mask if self.gqa else jnp.where(mask, s, NEG)

    def keys(self, j, gi, kc_ref, vc_ref, kp_ref, vp_ref, lo, kq, g, dup):
        def kv(k_ref, v_ref, r):
            if self.gqa:
                return _dup_head(k_ref[r, :], gi, lo), _dup_head(v_ref[r, :], gi, lo)
            sl = slice(128 * gi, 128 * (gi + 1))
            return k_ref[r, sl], v_ref[r, sl]

        def blocks(b0, b1):
            if not self.gqa:
                return kv(kc_ref, vc_ref, slice(BLK * b0, BLK * b1))
            for b in range(b0, b1):
                if (b, gi) not in dup:
                    dup[b, gi] = kv(kc_ref, vc_ref, slice(BLK * b, BLK * (b + 1)))
            ks, vs = zip(*(dup[b, gi] for b in range(b0, b1)))
            return jnp.concatenate(ks, axis=0), jnp.concatenate(vs, axis=0)

        if self.kind == "mem":
            key0 = pl.multiple_of((g // (SEQ // QR)) * MEM_LEN, MEM_LEN)
            return (*kv(kc_ref, vc_ref, slice(None)), None, [(0, MEM_LEN, key0)])
        kj, qi, both, cur = kq
        row0 = g * QR + BLK * j
        mode = _prev_mode(self.kind, self.nb, j)
        if mode == "no":
            return (*blocks(j, j + 1), cur, [(0, BLK, pl.multiple_of(row0, BLK))])
        if mode == "yes":
            return (*blocks(j - 1, j + 1), both, [(0, 2 * BLK, pl.multiple_of(row0 - BLK, BLK))])
        has_prev = ((g * QB) % self.nb) > 0
        hp = has_prev.astype(jnp.int32)
        mask = self.as_mask(jnp.logical_and(kj >= qi * hp + (self.reach * hp + BLK * (1 - hp)), kj <= qi + BLK))
        kp, vp = kv(kp_ref, vp_ref, slice(None))
        kc, vc = blocks(0, 1)
        return (jnp.concatenate([kp, kc], axis=0), jnp.concatenate([vp, vc], axis=0), mask,
                [(0, BLK, pl.multiple_of(jnp.maximum(row0 - BLK, 0), BLK)), (BLK, BLK, pl.multiple_of(row0, BLK))])


def _attn_fwd(q, qcb, qw, k, kcb, v, vcb, kvw, *, kind, nb=1, max_dist=BLK, gqa=False, sinks=None):
    a = _Attn(kind, nb, max_dist, gqa, qw, kvw, qcb, kcb, vcb)

    def body(*refs):
        it = iter(refs)
        q_ref, kc_ref, vc_ref = next(it), next(it), next(it)
        kp_ref, vp_ref = (next(it), next(it)) if a.ext_prev else (None, None)
        sink_ref = next(it) if sinks is not None else None
        o_ref, lse_ref = next(it), next(it)
        g = pl.program_id(0)
        lo = _lane_lo()
        top = lax.broadcasted_iota(jnp.int32, (128, 1), 0) < 64
        rid = lax.broadcasted_iota(jnp.int32, (8, 128), 0)
        kq, dup = a.masks(), {}
        stats = {}

        def scores(j, gi, pairs):
            rows = slice(BLK * j, BLK * (j + 1))
            qs = _stack_heads([q_ref[rows, 128 * p:128 * (p + 1)] for p in pairs], lo)
            kk, vv, mask, _ = a.keys(j, gi, kc_ref, vc_ref, kp_ref, vp_ref, lo, kq, g, dup)
            pieces = [slice(r0, r0 + BLK) for r0 in range(0, kk.shape[0], BLK)]
            return dict(j=j, gi=gi, pairs=pairs, rows=rows, vv=vv, mask=mask, pieces=pieces,
                        ss=[_dot(kk[r], qs, NT) for r in pieces])

        def softmax(c):
            gi, mask = c["gi"], c["mask"]
            ss = [s if mask is None else a.hide(s, mask[r]) for r, s in zip(c["pieces"], c.pop("ss"))]
            m = jnp.max(ss[0], axis=0, keepdims=True)
            for s in ss[1:]:
                m = jnp.maximum(m, jnp.max(s, axis=0, keepdims=True))
            if sink_ref is not None:
                sk = jnp.concatenate([jnp.full((1, 128), sink_ref[0, a.nh * gi + i], F32) for i in range(a.nh)], axis=1)
                m = jnp.maximum(m, sk)
            ps = [jnp.exp(s - m) for s in ss]
            l = sum(jnp.sum(p, axis=0, keepdims=True) for p in ps)
            if sink_ref is not None:
                l = l + jnp.exp(sk - m)
            c["ps"] = [p.astype(BF16) for p in ps]
            c["l"], c["lse"] = l, m + jnp.log(l)

        def outputs(c):
            j, gi, rows = c["j"], c["gi"], c["rows"]
            ot = sum(_dot(c["vv"][r], p, TN) for r, p in zip(c["pieces"], c["ps"]))
            ot = ot * pl.reciprocal(c["l"], approx=True)
            for i, p in enumerate(c["pairs"]):
                o2t = jnp.where(top, ot[:, 256 * i:256 * i + 128], ot[:, 256 * i + 128:256 * i + 256])
                o_ref[rows, 128 * p:128 * (p + 1)] = o2t.T.astype(BF16)
            stat = stats.get(j, jnp.zeros((8, 128), F32))
            for i in range(a.nh):
                stat = jnp.where(rid == a.nh * gi + i, c["lse"][:, 128 * i:128 * (i + 1)], stat)
            stats[j] = stat
            if gi == a.groups[-1][0]:
                lse_ref[rows, :] = jnp.concatenate([stats.pop(j), jnp.zeros((120, 128), F32)], axis=0).T

        chains = [(j, gi, pairs) for j in range(QB) for gi, pairs in a.groups]
        live = {}
        for t in range(len(chains) + 2):
            if t < len(chains):
                live[t] = scores(*chains[t])
            if 0 <= t - 1 < len(chains):
                softmax(live[t - 1])
            if 0 <= t - 2 < len(chains):
                outputs(live.pop(t - 2))


    args = [q, k, v] + ([k, v] if a.ext_prev else [])
    in_specs = [a.q_spec] + a.kv_specs
    if sinks is not None:
        args.append(sinks)
        in_specs.append(pl.BlockSpec(memory_space=pltpu.SMEM))
    return _Part(body, args, in_specs, [a.row_spec, a.stat_spec], [_sds((T, qw), BF16), _sds((T, 128), F32)])


def _attn_bwd(q, qcb, qw, k, kcb, v, vcb, kvw, do, lse, dl, *, kind, nb=1, max_dist=BLK, gqa=False, sinkv=None,
              mem_in=None):
    a = _Attn(kind, nb, max_dist, gqa, qw, kvw, qcb, kcb, vcb)

    def body(*refs):
        it = iter(refs)
        q_ref, kc_ref, vc_ref = next(it), next(it), next(it)
        kp_ref, vp_ref = (next(it), next(it)) if a.ext_prev else (None, None)
        do_ref, lse_ref, dl_ref = next(it), next(it), next(it)
        sinkv_ref = next(it) if sinkv is not None else None
        mem_ref = next(it) if kind == "mem" else None
        dq_ref = next(it)
        if kind == "mem":
            gmem_ref = next(it)
        else:
            dk_out, dv_out = next(it), next(it)
        dsink_ref = next(it) if sinkv is not None else None
        if kind != "mem":
            dk_ref, dv_ref, stage_k, stage_v, flush_sem = next(it), next(it), next(it), next(it), next(it)
        else:
            dkv_ref = next(it)
        g = pl.program_id(0)
        lo = _lane_lo()
        top = lax.broadcasted_iota(jnp.int32, (128, 1), 0) < 64

        @pl.when(g == 0)
        def _():
            if kind == "mem":
                dkv_ref[...] = jnp.zeros_like(dkv_ref)
            else:
                dk_ref[...] = jnp.zeros_like(dk_ref)
                dv_ref[...] = jnp.zeros_like(dv_ref)
            if dsink_ref is not None:
                dsink_ref[...] = jnp.zeros_like(dsink_ref)

        kq, dup = a.masks(), {}
        stats_t = {}

        def first_matmuls(j, gi, pairs):
            rows = slice(BLK * j, BLK * (j + 1))
            if j not in stats_t:
                stats_t[j] = (lse_ref[rows, :].T, dl_ref[rows, :].T)
            lse_t, dl_t = stats_t[j]
            heads = [a.nh * gi + i for i in range(a.nh)]
            c = dict(rows=rows, gi=gi, pairs=pairs)
            c["qs"] = _stack_heads([q_ref[rows, 128 * p:128 * (p + 1)] for p in pairs], lo)
            c["dos"] = _stack_heads([do_ref[rows, 128 * p:128 * (p + 1)] for p in pairs], lo)
            c["lse_row"] = jnp.concatenate([lse_t[h:h + 1, :] for h in heads], axis=1)
            c["dl_row"] = jnp.concatenate([dl_t[h:h + 1, :] for h in heads], axis=1)
            c["kk"], vv, c["mask"], c["dests"] = a.keys(j, gi, kc_ref, vc_ref, kp_ref, vp_ref, lo, kq, g, dup)
            c["s"] = _dot(c["kk"], c["qs"], NT)
            c["dp"] = _dot(vv, c["dos"], NT)
            return c

        def elementwise(c):
            s = c.pop("s")
            if c["mask"] is not None:
                s = a.hide(s, c["mask"])
            p = jnp.exp(s - c["lse_row"])
            c["ds"] = (p * (c.pop("dp") - c["dl_row"])).astype(BF16)
            c["p"] = p.astype(BF16)

        def last_matmuls(c):
            gi, rows = c["gi"], c["rows"]
            dqt = _dot(c["kk"], c["ds"], TN)
            ck = _dot(c["ds"], c["qs"], NN)
            cv = _dot(c["p"], c["dos"], NN)
            if gqa:
                sel = lo if gi == 0 else jnp.logical_not(lo)
                ck = jnp.where(sel, ck + pltpu.roll(ck, 64, 1), 0.0)
                cv = jnp.where(sel, cv + pltpu.roll(cv, 64, 1), 0.0)
                kcols = slice(0, 128)
            else:
                kcols = slice(128 * gi, 128 * (gi + 1))
            for r0, nr, key0 in c["dests"]:
                krows = pl.ds(key0, nr)
                if kind == "mem":
                    dkv_ref[krows, kcols] += ck[r0:r0 + nr]
                    dkv_ref[krows, slice(kvw + kcols.start, kvw + kcols.stop)] += cv[r0:r0 + nr]
                else:
                    dk_ref[krows, kcols] += ck[r0:r0 + nr]
                    dv_ref[krows, kcols] += cv[r0:r0 + nr]
            for i, p in enumerate(c["pairs"]):
                dq2t = jnp.where(top, dqt[:, 256 * i:256 * i + 128], dqt[:, 256 * i + 128:256 * i + 256])
                dq_ref[rows, 128 * p:128 * (p + 1)] = dq2t.T.astype(BF16)

        chains = [(j, gi, pairs) for j in range(QB) for gi, pairs in a.groups]
        live = {}
        for t in range(len(chains) + 2):
            if t < len(chains):
                live[t] = first_matmuls(*chains[t])
            if 0 <= t - 1 < len(chains):
                elementwise(live[t - 1])
            if 0 <= t - 2 < len(chains):
                last_matmuls(live.pop(t - 2))
        if dsink_ref is not None:
            ps = jnp.exp(sinkv_ref[...] - lse_ref[...]) * dl_ref[...]
            dsink_ref[...] += jnp.sum(ps, axis=0, keepdims=True)
        if kind == "mem":
            @pl.when(g == T // QR - 1)
            def _():
                gmem_ref[...] = _dot(mem_ref[...], dkv_ref[...].astype(BF16), TN)
        else:
            n_steps = T // QR

            def flush(step):
                rows = pl.ds(pl.multiple_of(step * QR, QR), QR)
                out = []
                for acc, stage, dst, i in ((dk_ref, stage_k, dk_out, 0), (dv_ref, stage_v, dv_out, 1)):
                    stage[...] = acc[rows, :].astype(BF16)
                    out.append(pltpu.make_async_copy(stage, dst.at[rows, :], flush_sem.at[i]))
                return out

            def flushed(step):
                rows = pl.ds(pl.multiple_of(step * QR, QR), QR)
                return [pltpu.make_async_copy(stage, dst.at[rows, :], flush_sem.at[i])
                        for stage, dst, i in ((stage_k, dk_out, 0), (stage_v, dv_out, 1))]

            @pl.when(g >= 2)
            def _():
                for cp in flushed(g - 2):
                    cp.wait()

            @pl.when(g >= 1)
            def _():
                for cp in flush(g - 1):
                    cp.start()

            @pl.when(g == n_steps - 1)
            def _():
                for cp in flushed(g - 1):
                    cp.wait()
                for cp in flush(g):
                    cp.start()
                for cp in flushed(g):
                    cp.wait()

    args = [q, k, v] + ([k, v] if a.ext_prev else []) + [do, lse, dl]
    in_specs = [a.q_spec] + a.kv_specs + [a.row_spec, a.stat_spec, a.stat_spec]
    if sinkv is not None:
        args.append(sinkv)
        in_specs.append(_full((1, 128)))
    if kind == "mem":
        args.append(mem_in)
        in_specs.append(pl.BlockSpec(mem_in.shape, lambda g: (0, 0), pipeline_mode=pl.Buffered(1)))
    out_shape = [_sds((T, qw), BF16)]
    out_specs = [a.row_spec]
    scratch = []
    if kind == "mem":
        out_shape.append(_sds((D_MODEL, 2 * kvw), F32))
        out_specs.append(pl.BlockSpec((D_MODEL, 2 * kvw), lambda g: (0, 0), pipeline_mode=pl.Buffered(1)))
        scratch = [pltpu.VMEM((B_LOC * MEM_LEN, 2 * kvw), F32)]
    else:
        out_shape += [_sds((T, kvw), BF16)] * 2
        out_specs += [pl.BlockSpec(memory_space=pl.ANY)] * 2
        scratch = [pltpu.VMEM((T, kvw), F32)] * 2 + [pltpu.VMEM((QR, kvw), BF16)] * 2 + [pltpu.SemaphoreType.DMA((2,))]
    if sinkv is not None:
        out_shape.append(_sds((1, 128), F32))
        out_specs.append(_full((1, 128)))
    return _Part(body, args, in_specs, out_specs, out_shape, scratch)


def _dot2(v, w_ref):
    hi = v.astype(BF16)
    lo = (v - hi.astype(F32)).astype(BF16)
    return _dot(hi, w_ref[...], NN) + _dot(lo, w_ref[...], NN)


def _middle(oa, o1, l1, o4, l4, o16, l16, oc, z, x, tgt, g_br, ln_g, ln_b, wout, spread4, gather4, gather8):
    tm = 512
    spt = SEQ // tm

    def body(oa_ref, o1_ref, l1_ref, o4_ref, l4_ref, o16_ref, l16_ref, oc_ref, z_ref, x_ref, t_ref,
             g_ref, lg_ref, lb_ref, w_ref, sp4_ref, ga4_ref, ga8_ref,
             du_ref, dz_ref, doa_ref, dla_ref,
             dobn_ref, lsen_ref, dlbn_ref, dob4_ref, lse4_ref, dlb4_ref, dob16_ref, lse16_ref, dlb16_ref,
             doc_ref, dlc_ref, acc_ref, gout_ref, scr):
        i = pl.program_id(0)

        @pl.when(i == 0)
        def _():
            acc_ref[...] = jnp.zeros_like(acc_ref)
            gout_ref[...] = jnp.zeros_like(gout_ref)

        q = tm // 4
        for res in range(16):
            rows = pl.ds((res % 4) * q + res // 4, tm // 16, stride=4)
            for j in range(2):
                scr[6 + j, rows, :] = o16_ref[0, res, :, 128 * j:128 * (j + 1)].astype(F32)
            scr[8, rows, :] = l16_ref[0, res]
        for res in range(4):
            rows, blk = pl.ds(res, q, stride=4), slice(res * q, (res + 1) * q)
            for j in range(2):
                scr[j, rows, :] = o4_ref[0, res, :, 128 * j:128 * (j + 1)].astype(F32)
                scr[3 + j, rows, :] = scr[6 + j, blk, :]
            scr[2, rows, :] = l4_ref[0, res]
            scr[5, rows, :] = scr[8, blk, :]
        inv_d = 1.0 / D_MODEL
        gb, lg, lb = g_ref[...], lg_ref[...], lb_ref[...]

        def rms(o):
            r = lax.rsqrt(jnp.sum(o * o, axis=1, keepdims=True) * (1.0 / o.shape[1]) + RMS_EPS)
            return o * r, r

        def rms_bwd(dn_, n_, r):
            return r * (dn_ - n_ * (jnp.sum(dn_ * n_, axis=1, keepdims=True) * (1.0 / n_.shape[1])))

        def forward(rs):
            o4v = jnp.concatenate([scr[0, rs, :], scr[1, rs, :]], axis=1)
            o16v = jnp.concatenate([scr[3, rs, :], scr[4, rs, :]], axis=1)
            l1v, l4v, l16v = l1_ref[rs, :], scr[2, rs, :], scr[5, rs, :]
            mx = jnp.maximum(jnp.maximum(l1v, l4v), l16v)
            e1, e4, e16 = jnp.exp(l1v - mx), jnp.exp(l4v - mx), jnp.exp(l16v - mx)
            ssum = e1 + e4 + e16
            inv = 1.0 / ssum
            c = dict(rs=rs, lse_b=mx + jnp.log(ssum))
            c["ob"] = (_dot2(e1 * inv, sp4_ref) * o1_ref[rs, :].astype(F32) + _dot2(e4 * inv, sp4_ref) * o4v
                       + _dot2(e16 * inv, sp4_ref) * o16v)
            c["oa"], c["oc"] = oa_ref[rs, :].astype(F32), oc_ref[rs, :].astype(F32)
            na, c["ra"] = rms(c["oa"])
            nb_, c["rb"] = rms(c["ob"])
            nc, c["rc"] = rms(c["oc"])
            c["n"] = jnp.concatenate([na, nb_, nc], axis=1)
            c["zf"] = z_ref[rs, :].astype(F32)
            c["sig"] = 1.0 / (1.0 + jnp.exp(-c["zf"]))
            c["sz"] = c["zf"] * c["sig"]
            c["yb"] = (c["n"] * gb * c["sz"]).astype(BF16)
            c["y2"] = _dot(c["yb"], w_ref[...], NN)
            return c

        def norm(c):
            rs = c["rs"]
            u = ALPHA * x_ref[rs, :] + c.pop("y2")
            mu = jnp.sum(u, axis=1, keepdims=True) * inv_d
            uc = u - mu
            rstd = lax.rsqrt(jnp.sum(uc * uc, axis=1, keepdims=True) * inv_d + LN_EPS)
            xh = uc * rstd
            diff = xh * lg + lb - t_ref[rs, :]
            acc_ref[0:1, :] += jnp.sum(diff * diff, axis=0, keepdims=True) * (0.5 * inv_d)
            dout = diff * inv_d
            acc_ref[2:3, :] += jnp.sum(dout * xh, axis=0, keepdims=True)
            acc_ref[3:4, :] += jnp.sum(dout, axis=0, keepdims=True)
            dxh = dout * lg
            du = rstd * (dxh - jnp.sum(dxh, axis=1, keepdims=True) * inv_d
                         - xh * (jnp.sum(dxh * xh, axis=1, keepdims=True) * inv_d))
            dub = du.astype(BF16)
            du_ref[rs, :] = dub
            c["dy"] = _dot(dub, w_ref[...], NT)
            gout_ref[...] += _dot(c.pop("yb"), dub, TN)

        def backward(c):
            rs, n, dy, zf, sig = c["rs"], c["n"], c["dy"], c["zf"], c["sig"]
            t1 = dy * c["sz"]
            acc_ref[1:2, :] += jnp.sum(t1 * n, axis=0, keepdims=True)
            dn = t1 * gb
            dz_ref[rs, :] = (dy * n * gb * (sig * (1.0 + zf * (1.0 - sig)))).astype(BF16)
            doa = rms_bwd(dn[:, :W_A], n[:, :W_A], c["ra"])
            dob = rms_bwd(dn[:, W_A:W_A + W_B], n[:, W_A:W_A + W_B], c["rb"])
            doc = rms_bwd(dn[:, W_A + W_B:], n[:, W_A + W_B:], c["rc"])
            doa_ref[rs, :] = doa.astype(BF16)
            dla_ref[rs, :] = _dot2(doa * c["oa"], ga8_ref)
            doc_ref[rs, :] = doc.astype(BF16)
            dlc_ref[rs, :] = _dot2(doc * c["oc"], ga4_ref)
            dobn_ref[rs, :] = dob.astype(BF16)
            lsen_ref[rs, :] = c["lse_b"]
            dlbn_ref[rs, :] = _dot2(dob * c["ob"], ga4_ref)
            scr[0, rs, :] = dob[:, :128]
            scr[1, rs, :] = dob[:, 128:]

        halves = [slice(h * (tm // 2), (h + 1) * (tm // 2)) for h in range(2)]
        live = {}
        for t in range(len(halves) + 2):
            if t < len(halves):
                live[t] = forward(halves[t])
            if 0 <= t - 1 < len(halves):
                norm(live[t - 1])
            if 0 <= t - 2 < len(halves):
                backward(live.pop(t - 2))
        for j in range(2):
            sl = slice(128 * j, 128 * (j + 1))
            for res in range(4):
                t = scr[j, pl.ds(res, q, stride=4), :]
                dob4_ref[0, res, :, sl] = t.astype(BF16)
                scr[6 + j, res * q:(res + 1) * q, :] = t
            for res in range(16):
                dob16_ref[0, res, :, sl] = scr[6 + j, pl.ds((res % 4) * q + res // 4, tm // 16, stride=4),
                                               :].astype(BF16)
        for res in range(4):
            rows = pl.ds(res, q, stride=4)
            lse4_ref[0, res] = lsen_ref[rows, :]
            dlb4_ref[0, res] = dlbn_ref[rows, :]
        for res in range(16):
            rows = pl.ds(res // 4, tm // 16, stride=4)
            lse16_ref[0, res] = lse4_ref[0, res % 4, rows, :]
            dlb16_ref[0, res] = dlb4_ref[0, res % 4, rows, :]


    tok = lambda w: pl.BlockSpec((tm, w), lambda i: (i, 0))
    p4 = lambda w: pl.BlockSpec((1, 4, tm // 4, w), lambda i: (i // spt, 0, i % spt, 0))
    p16 = lambda w: pl.BlockSpec((1, 16, tm // 16, w), lambda i: (i // spt, 0, i % spt, 0))
    s4 = lambda w, dt: _sds((B_LOC, 4, SEQ // 4, w), dt)
    s16 = lambda w, dt: _sds((B_LOC, 16, SEQ // 16, w), dt)
    row = _full((1, D_MODEL))
    return pl.pallas_call(
        body, name="middle", grid=(T // tm,),
        in_specs=[tok(W_A), tok(W_B), tok(128), p4(W_B), p4(128), p16(W_B), p16(128), tok(W_C), tok(D_MIX),
                  tok(D_MODEL), tok(D_MODEL), row, row, row, _full((D_MIX, D_MODEL)),
                  _full((128, W_B)), _full((W_B, 128)), _full((W_A, 128))],
        out_specs=(tok(D_MODEL), tok(D_MIX), tok(W_A), tok(128),
                   tok(W_B), tok(128), tok(128), p4(W_B), p4(128), p4(128), p16(W_B), p16(128), p16(128),
                   tok(W_C), tok(128), _full((8, D_MODEL)), _full((D_MIX, D_MODEL))),
        out_shape=(_sds((T, D_MODEL), BF16), _sds((T, D_MIX), BF16),
                   _sds((T, W_A), BF16), _sds((T, 128), F32),
                   _sds((T, W_B), BF16), _sds((T, 128), F32), _sds((T, 128), F32),
                   s4(W_B, BF16), s4(128, F32), s4(128, F32), s16(W_B, BF16), s16(128, F32), s16(128, F32),
                   _sds((T, W_C), BF16), _sds((T, 128), F32), _sds((8, D_MODEL), F32),
                   _sds((D_MIX, D_MODEL), F32)),
        scratch_shapes=[pltpu.VMEM((9, tm, 128), F32)],
        compiler_params=_cp(("arbitrary",), vmem_mb=56),
    )(*_pin(oa, o1, l1, o4, l4, o16, l16, oc, z, x, tgt, g_br, ln_g, ln_b, wout, spread4, gather4, gather8))


class _ReduceScatter:
    def __init__(self, shapes):
        self.shapes = shapes

    def scratch_shapes(self):
        out = []
        for n, w in self.shapes:
            h, p = n // 2, n // 4
            out += [pltpu.VMEM((4, h, w), F32), pltpu.VMEM((4, h, w), F32), pltpu.VMEM((6, p, w), BF16),
                    pltpu.VMEM((6, p, w), BF16), pltpu.VMEM((2, p, w), F32), pltpu.VMEM((h, w), F32)]
        na = len(self.shapes)
        dma = pltpu.SemaphoreType.DMA
        return out + [dma((na, 4)), dma((na, 4)), dma((na, 4)), dma((na, 6)), dma((na, 6)), dma((na,)), dma((na,)),
                      dma((na,))]

    def bind(self, g_refs, r_refs, scratch):
        na = len(self.shapes)
        bufs = [scratch[6 * a:6 * a + 6] for a in range(na)]
        mine, sib, stage, land, keep, tot = (tuple(b[i] for b in bufs) for i in range(6))
        loc_sem, s1_send, s1_recv, s2_send, s2_recv, s3_send, s3_recv, st_sem = scratch[6 * na:6 * na + 8]
        x, y, c = lax.axis_index("x"), lax.axis_index("y"), lax.axis_index("c")
        me, sibling = (x, y, c), (x, y, 1 - c)
        xn, yn, dg = (1 - x, y), (x, 1 - y), (1 - x, 1 - y)
        idx = lambda chip: 2 * chip[0] + chip[1]
        my_chip = idx((x, y))
        order = [idx(xn), idx(dg), idx(yn), my_chip]

        def rows(a, k, half):
            n = self.shapes[a][0]
            return pl.ds(pl.multiple_of(k * n + half * (n // 2), 8), n // 2)

        def piece(a, q):
            p = self.shapes[a][0] // 4
            return slice(q * p, (q + 1) * p)

        def load(a, k):
            return pltpu.make_async_copy(g_refs[a].at[rows(a, k, c), :], mine[a].at[k], loc_sem.at[a, k])

        def s1(a, k, half):
            return pltpu.make_async_remote_copy(
                src_ref=g_refs[a].at[rows(a, k, half), :], dst_ref=sib[a].at[k],
                send_sem=s1_send.at[a, k], recv_sem=s1_recv.at[a, k], device_id=sibling, device_id_type=MESH)

        def s2(a, i, to):
            return pltpu.make_async_remote_copy(
                src_ref=stage[a].at[i], dst_ref=land[a].at[i], send_sem=s2_send.at[a, i], recv_sem=s2_recv.at[a, i],
                device_id=to, device_id_type=MESH)

        via = {0: xn, 1: xn, 2: yn, 3: yn, 4: yn, 5: xn}

        def s3(a, half, to):
            return pltpu.make_async_remote_copy(
                src_ref=tot[a], dst_ref=r_refs[a].at[rows(a, 0, half), :], send_sem=s3_send.at[a],
                recv_sem=s3_recv.at[a], device_id=to, device_id_type=MESH)

        def store(a):
            return pltpu.make_async_copy(tot[a], r_refs[a].at[rows(a, 0, c), :], st_sem.at[a])

        def start():
            for k in order:
                for a in range(na):
                    load(a, k).start()
                    s1(a, k, 1 - c).start()

        def chip_sum(a, k):
            load(a, k).wait()
            s1(a, k, c).wait_recv()
            return mine[a][k] + sib[a][k]

        def exchange():
            for a in range(na):
                P, Q = piece(a, 0), piece(a, 1)
                s_xn = chip_sum(a, idx(xn))
                stage[a][0] = s_xn[P].astype(BF16)
                keep[a][1] = s_xn[Q]
                s_dg = chip_sum(a, idx(dg))
                stage[a][1] = s_dg[P].astype(BF16)
                s2(a, 0, (*xn, c)).start()
                s2(a, 1, (*xn, c)).start()
                stage[a][3] = s_dg[Q].astype(BF16)
                s_yn = chip_sum(a, idx(yn))
                stage[a][2] = s_yn[Q].astype(BF16)
                keep[a][0] = s_yn[P]
                s2(a, 2, (*yn, c)).start()
                s2(a, 3, (*yn, c)).start()
                tot[a][...] = chip_sum(a, my_chip)

        def relay():
            for a in range(na):
                P, Q = piece(a, 0), piece(a, 1)
                s2(a, 1, me).wait_recv()
                stage[a][4] = (keep[a][0] + land[a][1].astype(F32)).astype(BF16)
                s2(a, 4, (*yn, c)).start()
                s2(a, 3, me).wait_recv()
                stage[a][5] = (keep[a][1] + land[a][3].astype(F32)).astype(BF16)
                s2(a, 5, (*xn, c)).start()
                s2(a, 0, me).wait_recv()
                tot[a][P, :] += land[a][0].astype(F32)
                s2(a, 2, me).wait_recv()
                tot[a][Q, :] += land[a][2].astype(F32)

        def finish():
            for a in range(na):
                P, Q = piece(a, 0), piece(a, 1)
                s2(a, 4, me).wait_recv()
                tot[a][P, :] += land[a][4].astype(F32)
                s2(a, 5, me).wait_recv()
                tot[a][Q, :] += land[a][5].astype(F32)
                s3(a, c, sibling).start()
                store(a).start()

        def drain():
            for a in range(na):
                s3(a, 1 - c, me).wait_recv()
                store(a).wait()
            for a in range(na):
                for k in order:
                    s1(a, k, 1 - c).wait_send()
                for i in range(6):
                    s2(a, i, (*via[i], c)).wait_send()
                s3(a, c, sibling).wait_send()

        return start, exchange, relay, finish, drain

    def part(self, grads, steps):
        def body(*refs):
            na = len(self.shapes)
            i = pl.program_id(0)
            for step, phase in zip(steps, self.bind(refs[:na], refs[na:2 * na], refs[2 * na:])):
                pl.when(i == step)(phase)

        hbm = pl.BlockSpec(memory_space=pl.ANY)
        return _Part(body, list(grads), [hbm] * len(grads), [hbm] * len(grads),
                     [_sds((n, w), F32) for n, w in self.shapes], self.scratch_shapes())


def _dh_dx(dqa, dka, dva, dqn, dkn, dvn, dq4, dk4, dv4, dq16, dk16, dv16, dqc, dz, du, xb, cos, sa, sb, winT):
    tm = 512
    spt = SEQ // tm

    def body(dqa_ref, dka_ref, dva_ref, dqn_ref, dkn_ref, dvn_ref, dq4_ref, dk4_ref, dv4_ref,
             dq16_ref, dk16_ref, dv16_ref, dqc_ref, dz_ref, du_ref, xb_ref, cos_ref, sa_ref, sb_ref, w_ref,
             gx_ref, db_ref, gin_ref, dh_ref, scr):
        i = pl.program_id(0)

        @pl.when(i == 0)
        def _():
            db_ref[...] = jnp.zeros_like(db_ref)
            gin_ref[...] = jnp.zeros_like(gin_ref)

        cos_t, sa_t, sb_t = cos_ref[...], sa_ref[...], sb_ref[...]

        def rope_t(t):
            return _rope(t, cos_t, sa_t, sb_t, -1)

        def put(r0, val):
            n = val.shape[1]
            dh_ref[:, r0:r0 + n] = val.astype(BF16)
            db_ref[:, r0:r0 + n] += jnp.sum(val, axis=0, keepdims=True)

        put(O_QA, rope_t(dqa_ref[...].astype(F32)) * QK_SCALE)
        put(O_KA, rope_t(dka_ref[...].astype(F32)))
        put(O_VA, dva_ref[...].astype(F32))
        put(O_QC, dqc_ref[...].astype(F32) * QK_SCALE)
        put(O_Z, dz_ref[...].astype(F32))
        for k, (n_ref, r4, r16) in enumerate(((dqn_ref, dq4_ref, dq16_ref), (dkn_ref, dk4_ref, dk16_ref),
                                               (dvn_ref, dv4_ref, dv16_ref))):
            for j in range(2):
                sl = slice(128 * j, 128 * (j + 1))
                a, q = 2 * k + j, tm // 4
                scr[a] = n_ref[:, sl].astype(F32)
                for res in range(16):
                    scr[6 + a, pl.ds((res % 4) * q + res // 4, tm // 16, stride=4), :] = r16[0, res, :, sl].astype(F32)
                for res in range(4):
                    scr[a, pl.ds(res, q, stride=4), :] += (scr[6 + a, res * q:(res + 1) * q, :]
                                                           + r4[0, res, :, sl].astype(F32))
        cat = lambda a: jnp.concatenate([scr[a], scr[a + 1]], axis=1)
        put(O_QB, rope_t(cat(0)) * QK_SCALE)
        put(O_KB, rope_t(cat(2)))
        put(O_VB, cat(4))
        gx_ref[...] = _dot(dh_ref[...], w_ref[...], NN) + ALPHA * du_ref[...].astype(F32)
        gin_ref[...] += _dot(dh_ref[...], xb_ref[...], TN)

    tok = lambda w: pl.BlockSpec((tm, w), lambda i: (i, 0))
    tab = pl.BlockSpec((tm, 128), lambda i: (i % spt, 0))
    p4 = pl.BlockSpec((1, 4, tm // 4, W_B), lambda i: (i // spt, 0, i % spt, 0))
    p16 = pl.BlockSpec((1, 16, tm // 16, W_B), lambda i: (i // spt, 0, i % spt, 0))
    once = lambda shape: pl.BlockSpec(shape, lambda i: (0, 0), pipeline_mode=pl.Buffered(1))
    return pl.pallas_call(
        body, name="dh_dx", grid=(T // tm,),
        in_specs=[tok(W_A), tok(W_KV_A), tok(W_KV_A), tok(W_B), tok(W_B), tok(W_B), p4, p4, p4, p16, p16, p16,
                  tok(W_C), tok(D_MIX), tok(D_MODEL), tok(D_MODEL), tab, tab, tab, once((D_IN, D_MODEL))],
        out_specs=(tok(D_MODEL), _full((1, D_IN)), once((D_IN, D_MODEL))),
        out_shape=(_sds((T, D_MODEL), F32), _sds((1, D_IN), F32), _sds((D_IN, D_MODEL), F32)),
        scratch_shapes=[pltpu.VMEM((tm, D_IN), BF16), pltpu.VMEM((12, tm, 128), F32)],
        compiler_params=_cp(("arbitrary",), vmem_mb=56),
    )(*_pin(dqa, dka, dva, dqn, dkn, dvn, dq4, dk4, dv4, dq16, dk16, dv16, dqc, dz, du, xb, cos, sa, sb, winT))


def _reduce_grads(g_in, acc, dbin, dsink):
    rs = _ReduceScatter([(SH_IN, D_MODEL)])

    def body(g_ref, acc_ref, dbin_ref, dsink_ref, r_ref, sv_ref, sv_mine, sv_all, sv_send, sv_recv, *rs_scratch):
        x, y, c = lax.axis_index("x"), lax.axis_index("y"), lax.axis_index("c")
        chips = [(1 - x, y), (x, 1 - y), (1 - x, 1 - y)]
        start, exchange, relay, finish, drain = rs.bind((g_ref,), (r_ref,), rs_scratch)
        start()

        sv_mine[...] = jnp.zeros_like(sv_mine)
        sv_mine[0:4, :] = acc_ref[0:4, :]
        for k, c0 in enumerate(range(0, D_IN, SV_W)):
            n = min(SV_W, D_IN - c0)
            sv_mine[SV_DB + k:SV_DB + k + 1, 0:n] = dbin_ref[:, c0:c0 + n]
        sv_mine[SV_SINK:SV_SINK + 1, 0:128] = dsink_ref[...]
        my_dev = 4 * x + 2 * y + c
        others = [(x, y, 1 - c)] + [(*chip, cc) for chip in chips for cc in (c, 1 - c)]

        def sv_copy(j, to):
            return pltpu.make_async_remote_copy(
                src_ref=sv_mine, dst_ref=sv_all.at[my_dev], send_sem=sv_send.at[j], recv_sem=sv_recv.at[j],
                device_id=to, device_id_type=MESH)

        sv_sends = [sv_copy(j, to) for j, to in enumerate(others)]
        for cp in sv_sends:
            cp.start()
        exchange()
        relay()
        finish()
        sv_all[my_dev] = sv_mine[...]
        for j in range(7):
            sv_copy(j, (x, y, c)).wait_recv()
        tot = sv_all[0]
        for d in range(1, 8):
            tot = tot + sv_all[d]
        sv_ref[...] = tot
        drain()
        for cp in sv_sends:
            cp.wait_send()

    vm = pl.BlockSpec(memory_space=pltpu.VMEM)
    hbm = pl.BlockSpec(memory_space=pl.ANY)
    return pl.pallas_call(
        body, name="reduce_grads",
        out_shape=(_sds((SH_IN, D_MODEL), F32), _vm_sds((8, SV_W), F32)),
        in_specs=[hbm, vm, vm, vm], out_specs=(hbm, vm),
        scratch_shapes=[pltpu.VMEM((8, SV_W), F32), pltpu.VMEM((8, 8, SV_W), F32),
                        pltpu.SemaphoreType.DMA((7,)), pltpu.SemaphoreType.DMA((7,))] + rs.scratch_shapes(),
        compiler_params=_cp(vmem_mb=40),
    )(pltpu.with_memory_space_constraint(g_in, pltpu.HBM), acc, dbin, dsink)


def _adamw_update(w, g, m, v):
    nm = ADAM_B1 * m + (1.0 - ADAM_B1) * g
    nv = ADAM_B2 * v + (1.0 - ADAM_B2) * (g * g)
    m_hat = nm / (1.0 - ADAM_B1 ** ADAM_STEP)
    v_hat = nv / (1.0 - ADAM_B2 ** ADAM_STEP)
    return -ADAM_LR * (m_hat / (jnp.sqrt(v_hat) + ADAM_EPS) + ADAM_WD * w), nm, nv


SMALL = ((SV_DB, D_IN, 1.0), (SV_SINK, 8, -1.0), (1, D_MIX, 1.0), (2, D_MODEL, 1.0), (3, D_MODEL, 1.0))


def _adamw_all(items, sv, ws, ms, vs, n_chunks=4):
    nb, ns = 4 * len(items), len(SMALL)
    n_out = nb + 1 + 4 * ns

    def body(*refs):
        ins, sv_ref, small_in = refs[:nb], refs[nb], refs[nb + 1:nb + 1 + 3 * ns]
        outs, scratch = refs[nb + 1 + 3 * ns:nb + 1 + 3 * ns + n_out], refs[nb + 1 + 3 * ns + n_out:]
        big_out, loss_ref, small_out = outs[:nb], outs[nb], outs[nb + 1:]
        in_buf, out_buf, load_sem, store_sem = scratch[:nb], scratch[nb:2 * nb], scratch[2 * nb], scratch[2 * nb + 1]

        def rows(p, c):
            n = items[p][0].shape[0] // n_chunks
            return pl.ds(c * n, n)

        def load(a, c):
            r = rows(a // 4, c)
            return pltpu.make_async_copy(ins[a].at[r, :], in_buf[a].at[r, :], load_sem.at[a, c])

        def store(a, c):
            r = rows(a // 4, c)
            src = in_buf[a + 1] if a % 4 == 0 else out_buf[a]
            return pltpu.make_async_copy(src.at[r, :], big_out[a].at[r, :], store_sem.at[a, c])

        order = [(p, c) for c in range(n_chunks) for p in range(len(items))]
        for p, c in order:
            for k in range(4):
                load(4 * p + k, c).start()

        loss_ref[...] = jnp.sum(sv_ref[0:1, 0:D_MODEL], axis=1, keepdims=True)
        for p, (row, width, sign) in enumerate(SMALL):
            gv = sign * jnp.concatenate([sv_ref[row + k:row + k + 1, 0:min(SV_W, width - c0)]
                                         for k, c0 in enumerate(range(0, width, SV_W))], axis=1)
            small_out[4 * p][...] = gv
            small_out[4 * p + 1][...], small_out[4 * p + 2][...], small_out[4 * p + 3][...] = _adamw_update(
                small_in[p][...], gv, small_in[ns + p][...], small_in[2 * ns + p][...])

        for p, c in order:
            for k in range(4):
                load(4 * p + k, c).wait()
            r = rows(p, c)
            w_buf, g_buf, m_buf, v_buf = in_buf[4 * p:4 * p + 4]
            out_buf[4 * p + 1][r, :], out_buf[4 * p + 2][r, :], out_buf[4 * p + 3][r, :] = _adamw_update(
                w_buf[r, :], g_buf[r, :], m_buf[r, :], v_buf[r, :])
            for k in range(4):
                store(4 * p + k, c).start()
        for p, c in order:
            for k in range(4):
                store(4 * p + k, c).wait()

    shapes, args, bufs = [], [], []
    for w, g, m, v in items:
        assert w.shape[0] % (8 * n_chunks) == 0
        shapes += [_sds(w.shape, F32)] * 4
        bufs += [pltpu.VMEM(w.shape, F32)] * 4
        args += [w, g, m, v]
    small_args = [*ws, *ms, *vs]
    whole = lambda a: _full(a.shape)
    hbm = pl.BlockSpec(memory_space=pl.ANY)
    res = pl.pallas_call(
        body, name="adamw", grid=(1,),
        in_specs=[hbm] * nb + [whole(sv)] + [whole(a) for a in small_args],
        out_specs=tuple([hbm] * nb + [_full((1, 1))] + [whole(w) for w in ws for _ in range(4)]),
        out_shape=tuple(shapes + [_sds((1, 1), F32)] + [_sds(w.shape, F32) for w in ws for _ in range(4)]),
        scratch_shapes=bufs + bufs + [pltpu.SemaphoreType.DMA((nb, n_chunks))] * 2,
        compiler_params=_cp(("arbitrary",), vmem_mb=52),
    )(*_pin(*args, sv, *small_args))
    big = [tuple(res[4 * p:4 * p + 4]) for p in range(len(items))]
    return big, res[nb], [tuple(res[nb + 1 + 4 * p:nb + 5 + 4 * p]) for p in range(ns)]


def _rope_tables():
    pos = np.arange(SEQ, dtype=np.float32)
    inv = (np.float32(ROPE_THETA) ** (-np.arange(0, 64, 2, dtype=np.float32) / np.float32(64))).astype(np.float32)
    ang = np.tile(pos[:, None] * inv[None, :], (1, 4))
    cos, sin = np.cos(ang).astype(np.float32), np.sin(ang).astype(np.float32)
    low = (np.arange(128) % 64) < 32
    zero = np.float32(0.0)
    return jnp.asarray(cos), jnp.asarray(np.where(low, -sin, zero)), jnp.asarray(np.where(low, zero, sin))


def _local_step(x2, mem2, tgt2, winT, wout, wmem, b_in, sinks, g_branch, ln_gain, ln_bias):
    cos, sa, sb = _rope_tables()
    sinkv = jnp.pad(sinks, ((0, 0), (0, 120)))
    head_of_lane = np.arange(512)[:, None] // 64
    gather8 = jnp.asarray(head_of_lane == np.arange(128)[None, :], BF16)
    gather4 = jnp.asarray(head_of_lane[:W_B] == np.arange(128)[None, :], BF16)
    spread4 = jnp.asarray((head_of_lane[:W_B] == np.arange(128)[None, :]).T, BF16)

    xb, qa, ka, va, bn, b4, b16, qc, z, wout, wmem = _in_proj(x2, winT, b_in, cos, sa, sb, wout, wmem)
    memb, mkv = _mem_kv(mem2, wmem)
    b4f, b16f = b4.reshape(T, 768), b16.reshape(T, 768)

    swa = dict(kind="band", nb=SEQ // BLK, max_dist=BLK - 1, gqa=True)
    dil = (dict(kind="band", nb=SEQ // BLK), dict(kind="band", nb=SEQ // 4 // BLK), dict(kind="band", nb=1))
    (oa, lse_a), (o1, l1), (o4, l4), (o16, l16), (oc, lse_c) = _run_parts("attn_fwd", [
        _attn_fwd(qa, 0, W_A, ka, 0, va, 0, W_KV_A, sinks=sinks, **swa),
        _attn_fwd(bn, 0, W_B, bn, 1, bn, 2, W_B, **dil[0]),
        _attn_fwd(b4f, 0, W_B, b4f, 1, b4f, 2, W_B, **dil[1]),
        _attn_fwd(b16f, 0, W_B, b16f, 1, b16f, 2, W_B, **dil[2]),
        _attn_fwd(qc, 0, W_C, mkv, 0, mkv, 1, W_C, kind="mem")], "parallel", 48)

    s4 = lambda w: (B_LOC, 4, SEQ // 4, w)
    s16 = lambda w: (B_LOC, 16, SEQ // 16, w)
    (du, dz, doa, dla, dobn, lsen, dlbn, dob4, lse4, dlb4, dob16, lse16, dlb16, doc, dlc, acc, g_out) = _middle(
        oa, o1, l1, o4.reshape(s4(W_B)), l4.reshape(s4(128)), o16.reshape(s16(W_B)), l16.reshape(s16(128)), oc, z,
        x2, tgt2, g_branch, ln_gain, ln_bias, wout, spread4, gather4, gather8)

    flat = lambda a: a.reshape(T, a.shape[-1])
    (dqa, dka, dva, dsink), (dqc, g_mem) = _run_parts("attn_bwd_a", [
        _attn_bwd(qa, 0, W_A, ka, 0, va, 0, W_KV_A, doa, lse_a, dla, sinkv=sinkv, **swa),
        _attn_bwd(qc, 0, W_C, mkv, 0, mkv, 1, W_C, doc, lse_c, dlc, kind="mem", mem_in=memb)], "arbitrary", 48)
    last = T // QR - 1
    (r_out, r_mem), (dqn, dkn, dvn), (dq4, dk4, dv4), (dq16, dk16, dv16) = _run_parts("attn_bwd_b", [
        _ReduceScatter([(SH_OUT, D_MODEL), (SH_MEM, 2 * W_C)]).part((g_out, g_mem), (0, 1, 2, last, last)),
        _attn_bwd(bn, 0, W_B, bn, 1, bn, 2, W_B, dobn, lsen, dlbn, **dil[0]),
        _attn_bwd(b4f, 0, W_B, b4f, 1, b4f, 2, W_B, flat(dob4), flat(lse4), flat(dlb4), **dil[1]),
        _attn_bwd(b16f, 0, W_B, b16f, 1, b16f, 2, W_B, flat(dob16), flat(lse16), flat(dlb16), **dil[2])],
        "arbitrary", 62)

    r4 = lambda a: a.reshape(s4(W_B))
    r16 = lambda a: a.reshape(s16(W_B))
    gx, dbin, g_in = _dh_dx(dqa, dka, dva, dqn, dkn, dvn, r4(dq4), r4(dk4), r4(dv4), r16(dq16), r16(dk16),
                            r16(dv16), dqc, dz, du, xb, cos, sa, sb, winT)
    return gx, g_in, r_out, r_mem, acc, dbin, dsink


def kernel(x, mem, w_in, b_in, w_mem, attn_sinks, g_branch, w_out, ln_gain, ln_bias, loss_target, m_w_in, m_b_in, m_w_mem, m_attn_sinks, m_g_branch, m_w_out, m_ln_gain, m_ln_bias, v_w_in, v_b_in, v_w_mem, v_attn_sinks, v_g_branch, v_w_out, v_ln_gain, v_ln_bias):
    winT, wout, wmem = _gather_weights(w_in[0].T, w_out[0], w_mem[0])
    gx, g_in, r_out, r_mem, acc, dbin, dsink = _local_step(
        x.reshape(T, D_MODEL), mem.reshape(B_LOC * MEM_LEN, D_MODEL), loss_target.reshape(T, D_MODEL),
        winT, wout, wmem, b_in, attn_sinks, g_branch, ln_gain, ln_bias)
    r_in, sv = _reduce_grads(g_in, acc, dbin, dsink)

    small = ["b_in", "attn_sinks", "g_branch", "ln_gain", "ln_bias"]
    big, loss, steps = _adamw_all(
        [(w_in[0].T, r_in, m_w_in[0].T, v_w_in[0].T), (w_out[0], r_out, m_w_out[0], v_w_out[0]),
         (w_mem[0], r_mem, m_w_mem[0], v_w_mem[0])],
        sv, [b_in, attn_sinks, g_branch, ln_gain, ln_bias], [m_b_in, m_attn_sinks, m_g_branch, m_ln_gain, m_ln_bias],
        [v_b_in, v_attn_sinks, v_g_branch, v_ln_gain, v_ln_bias])
    out = dict(zip(small, steps))
    out["w_in"] = tuple(a.T[None] for a in big[0])
    out["w_out"], out["w_mem"] = (tuple(a[None] for a in st) for st in big[1:])
    names = ["w_in", "b_in", "w_mem", "attn_sinks", "g_branch", "w_out", "ln_gain", "ln_bias"]
    return (loss.reshape(()), gx.reshape(B_LOC, SEQ, D_MODEL), *[out[n][k] for k in range(4) for n in names])
```

```python
import jax
import jax.numpy as jnp
import numpy as np
from jax import lax
from jax.experimental import pallas as pl
from jax.experimental.pallas import tpu as pltpu

F32, BF16 = jnp.float32, jnp.bfloat16

D_MODEL = 1024
SEQ = 2048
B_LOC = 2
T = B_LOC * SEQ
BLK = 128
MEM_LEN = 256
W_A, W_KV_A, W_B, W_C, D_MIX = 512, 128, 256, 256, 1024
D_IN = 2816
O_QA, O_KA, O_VA, O_QB, O_KB, O_VB, O_QC, O_Z = 0, 512, 640, 768, 1024, 1280, 1536, 1792
ROPE_THETA = 10000.0
LN_EPS = 1e-5
RMS_EPS = 1e-6
ALPHA = 2.0 ** 0.25
QK_SCALE = 0.125
N_CHIP = 4
SH_IN, SH_OUT, SH_MEM = D_IN // N_CHIP, D_MIX // N_CHIP, D_MODEL // N_CHIP
NEG = -1e30
ADAM_LR, ADAM_B1, ADAM_B2, ADAM_EPS, ADAM_WD, ADAM_STEP = 0.001, 0.9, 0.999, 1e-08, 0.01, 10
SV_W = 1024
SV_DB, SV_SINK = 4, 7
assert D_MODEL == D_MIX == SV_W and D_IN <= (SV_SINK - SV_DB) * SV_W
MESH = pl.DeviceIdType.MESH

NN = ((1,), (0,))
NT = ((1,), (1,))
TN = ((0,), (0,))


def _dot(a, b, dims):
    return lax.dot_general(a, b, (dims, ((), ())), preferred_element_type=F32)


def _cp(sem=None, vmem_mb=None):
    kw = {}
    if sem is not None:
        kw["dimension_semantics"] = sem
    if vmem_mb is not None:
        kw["vmem_limit_bytes"] = vmem_mb * 1024 * 1024
    return pltpu.CompilerParams(**kw)


def _sds(shape, dtype):
    return pltpu.HBM(shape, dtype)


def _vm_sds(shape, dtype):
    return jax.ShapeDtypeStruct(shape, dtype)


def _pin(*args):
    return [pltpu.with_memory_space_constraint(a, pltpu.HBM) for a in args]


def _full(shape):
    n = len(shape)
    return pl.BlockSpec(shape, lambda *_: (0,) * n)


def _shard_rows(ref, n, chip, half):
    start = pl.multiple_of((2 * chip[0] + chip[1]) * n + half * (n // 2), 16)
    return ref.at[pl.ds(start, n // 2), :]


def _gather_weights(win_sh, wout_sh, wmem_sh):
    half, piece = SH_IN // 2, SH_IN // 4
    shards = ((SH_IN, D_MODEL), (SH_OUT, D_MODEL), (SH_MEM, 2 * W_C))

    def body(a_ref, b_ref, c_ref, oa_ref, ob_ref, oc_ref, raw_a, raw_b, raw_c, own_a, own_b, own_c,
             load_sem, store_sem, ici_send, ici_recv, d2d_send, d2d_recv):
        x, y, c = lax.axis_index("x"), lax.axis_index("y"), lax.axis_index("c")
        me, sibling = (x, y, c), (x, y, 1 - c)
        xn, yn, dg = (1 - x, y), (x, 1 - y), (1 - x, 1 - y)
        srcs, raws = (a_ref, b_ref, c_ref), (raw_a, raw_b, raw_c)
        owns, outs = (own_a, own_b, own_c), (oa_ref, ob_ref, oc_ref)
        loads = [pltpu.make_async_copy(srcs[a], raws[a], load_sem.at[a]) for a in range(3)]
        for cp in loads:
            cp.start()

        def rows(chip, hf, q):
            start = pl.multiple_of((2 * chip[0] + chip[1]) * SH_IN + hf * half + q * piece, 16)
            return oa_ref.at[pl.ds(start, piece), :]

        def copy(sems, k, chip, hf, q, to, src=None):
            blk = rows(chip, hf, q)
            return pltpu.make_async_remote_copy(
                src_ref=blk if src is None else src, dst_ref=blk, send_sem=sems[0].at[k], recv_sem=sems[1].at[k],
                device_id=to, device_id_type=MESH)

        def my_piece(q):
            return own_a.at[pl.ds(pl.multiple_of(c * half + q * piece, 16), piece), :]

        ici, d2d = (ici_send, ici_recv), (d2d_send, d2d_recv)
        stores, direct = [], []
        for a, (n, _) in enumerate(shards):
            loads[a].wait()
            owns[a][...] = raws[a][...].astype(BF16)
            mine = pl.ds(pl.multiple_of((2 * x + y) * n, 16), n)
            stores.append(pltpu.make_async_copy(owns[a], outs[a].at[mine, :], store_sem.at[a]))
            stores[-1].start()
            if a == 0:
                direct = [copy(ici, 0, (x, y), c, 0, (*xn, c), my_piece(0)),
                          copy(ici, 1, (x, y), c, 1, (*xn, c), my_piece(1)),
                          copy(ici, 3, (x, y), c, 0, (*yn, c), my_piece(0)),
                          copy(ici, 4, (x, y), c, 1, (*yn, c), my_piece(1))]
                for cp in direct:
                    cp.start()
        arrivals = [(0, xn, 0), (1, xn, 1), (3, yn, 0), (4, yn, 1), (2, dg, 1), (5, dg, 0)]
        passed = []
        for k, chip, q in arrivals:
            copy(ici, k, chip, c, q, me).wait_recv()
            if k == 0:
                passed.append(copy(ici, 5, xn, c, 0, (*yn, c)))
                passed[-1].start()
            if k == 4:
                passed.append(copy(ici, 2, yn, c, 1, (*xn, c)))
                passed[-1].start()
            passed.append(copy(d2d, k, chip, c, q, sibling))
            passed[-1].start()
        for k, chip, q in arrivals:
            copy(d2d, k, chip, 1 - c, q, me).wait_recv()
        for cp in direct + passed:
            cp.wait_send()
        for cp in stores:
            cp.wait()

    hbm = pl.BlockSpec(memory_space=pl.ANY)
    return pl.pallas_call(
        body, name="gather_weights",
        out_shape=(_sds((D_IN, D_MODEL), BF16), _sds((D_MIX, D_MODEL), BF16), _sds((D_MODEL, 2 * W_C), BF16)),
        in_specs=[hbm, hbm, hbm], out_specs=(hbm, hbm, hbm),
        scratch_shapes=([pltpu.VMEM(sh, F32) for sh in shards] + [pltpu.VMEM(sh, BF16) for sh in shards]
                        + [pltpu.SemaphoreType.DMA((3,))] * 2 + [pltpu.SemaphoreType.DMA((6,))] * 4),
        compiler_params=_cp(vmem_mb=40),
    )(*_pin(win_sh, wout_sh, wmem_sh))


def _rope(t, cos, sa, sb, sign):
    w = t.shape[1]
    reps = w // 128
    c, a, b = (jnp.tile(v, (1, reps)) if reps > 1 else v for v in (cos, sa, sb))
    rot = pltpu.roll(t, w - 32, 1) * a + pltpu.roll(t, 32, 1) * b
    return t * c + rot if sign > 0 else t * c - rot


def _in_proj(x, winT, b_in, cos, sa, sb, wout_own, wmem_own):
    tm = 512
    spt = SEQ // tm
    n_steps = T // tm
    forward_step = n_steps // 2

    def body(x_ref, w_ref, b_ref, cos_ref, sa_ref, sb_ref, wo_in, wm_in,
             xb_ref, qa_ref, ka_ref, va_ref, bn_ref, b4_ref, b16_ref, qc_ref, z_ref, wo_ref, wm_ref,
             scr, ici_send, ici_recv, d2d_send, d2d_recv):
        i = pl.program_id(0)
        mx, my, mc = lax.axis_index("x"), lax.axis_index("y"), lax.axis_index("c")
        chips = [(1 - mx, my), (mx, 1 - my), (1 - mx, 1 - my)]
        full = ((wo_ref, SH_OUT), (wm_ref, SH_MEM))

        def copy(sems, a, j, chip_of_block, half, to):
            blk = _shard_rows(full[a][0], full[a][1], chip_of_block, half)
            return pltpu.make_async_remote_copy(
                src_ref=blk, dst_ref=blk, send_sem=sems[0].at[a, j], recv_sem=sems[1].at[a, j],
                device_id=to, device_id_type=MESH)

        ici, d2d = (ici_send, ici_recv), (d2d_send, d2d_recv)
        pairs = [(a, j, chip) for j, chip in enumerate(chips) for a in range(2)]

        @pl.when(i == 0)
        def _():
            for a, j, chip in pairs:
                copy(ici, a, j, (mx, my), mc, (*chip, mc)).start()

        @pl.when(i == forward_step)
        def _():
            for a, j, chip in pairs:
                copy(ici, a, j, chip, mc, (mx, my, mc)).wait_recv()
                copy(d2d, a, j, chip, mc, (mx, my, 1 - mc)).start()

        @pl.when(i == n_steps - 1)
        def _():
            for a, j, chip in pairs:
                copy(d2d, a, j, chip, 1 - mc, (mx, my, mc)).wait_recv()
            for a, j, chip in pairs:
                copy(ici, a, j, (mx, my), mc, (*chip, mc)).wait_send()
                copy(d2d, a, j, chip, mc, (mx, my, 1 - mc)).wait_send()

        xb = x_ref[...].astype(BF16)
        xb_ref[...] = xb
        cos_t, sa_t, sb_t = cos_ref[...], sa_ref[...], sb_ref[...]

        def proj(r0, n):
            return _dot(xb, w_ref[r0:r0 + n, :], NT) + b_ref[:, r0:r0 + n]

        def rope(t):
            return _rope(t, cos_t, sa_t, sb_t, +1)

        parts = (rope(proj(O_QB, W_B)) * QK_SCALE, rope(proj(O_KB, W_B)), proj(O_VB, W_B))
        for k, part in enumerate(parts):
            bn_ref[:, 256 * k:256 * (k + 1)] = part.astype(BF16)
            scr[2 * k] = part[:, :128]
            scr[2 * k + 1] = part[:, 128:]
        for j in range(6):
            lanes = slice(128 * j, 128 * (j + 1))
            for res in range(4):
                t = scr[j, pl.ds(res, tm // 4, stride=4), :]
                b4_ref[0, res, :, lanes] = t.astype(BF16)
                scr[6 + j, res * (tm // 4):(res + 1) * (tm // 4), :] = t
            for res in range(16):
                b16_ref[0, res, :, lanes] = scr[6 + j, pl.ds((res % 4) * (tm // 4) + res // 4, tm // 16, stride=4),
                                                :].astype(BF16)
        qa_ref[...] = (rope(proj(O_QA, W_A)) * QK_SCALE).astype(BF16)
        assert O_VA == O_KA + W_KV_A
        kv = proj(O_KA, 2 * W_KV_A)
        ka_ref[...] = rope(kv[:, :W_KV_A]).astype(BF16)
        va_ref[...] = kv[:, W_KV_A:].astype(BF16)
        qc_ref[...] = (proj(O_QC, W_C) * QK_SCALE).astype(BF16)
        z_ref[...] = proj(O_Z, D_MIX).astype(BF16)

    tok = lambda w: pl.BlockSpec((tm, w), lambda i: (i, 0))
    tab = pl.BlockSpec((tm, 128), lambda i: (i % spt, 0))
    hbm = pl.BlockSpec(memory_space=pl.ANY)
    return pl.pallas_call(
        body, name="in_proj", grid=(n_steps,),
        in_specs=[tok(D_MODEL), _full((D_IN, D_MODEL)), _full((1, D_IN)), tab, tab, tab, hbm, hbm],
        out_specs=(tok(D_MODEL), tok(W_A), tok(W_KV_A), tok(W_KV_A), tok(768),
                   pl.BlockSpec((1, 4, tm // 4, 768), lambda i: (i // spt, 0, i % spt, 0)),
                   pl.BlockSpec((1, 16, tm // 16, 768), lambda i: (i // spt, 0, i % spt, 0)),
                   tok(W_C), tok(D_MIX), hbm, hbm),
        out_shape=(_sds((T, D_MODEL), BF16), _sds((T, W_A), BF16), _sds((T, W_KV_A), BF16), _sds((T, W_KV_A), BF16),
                   _sds((T, 768), BF16), _sds((B_LOC, 4, SEQ // 4, 768), BF16), _sds((B_LOC, 16, SEQ // 16, 768), BF16),
                   _sds((T, W_C), BF16), _sds((T, D_MIX), BF16),
                   _sds((D_MIX, D_MODEL), BF16), _sds((D_MODEL, 2 * W_C), BF16)),
        input_output_aliases={6: 9, 7: 10},
        scratch_shapes=[pltpu.VMEM((12, tm, 128), F32)] + [pltpu.SemaphoreType.DMA((2, 3))] * 4,
        compiler_params=_cp(("arbitrary",), vmem_mb=48),
    )(*_pin(x, winT, b_in, cos, sa, sb, wout_own, wmem_own))


def _mem_kv(mem, wmem):
    def body(m_ref, w_ref, mb_ref, kv_ref):
        mb = m_ref[...].astype(BF16)
        mb_ref[...] = mb
        kv_ref[...] = _dot(mb, w_ref[...], NN).astype(BF16)

    n = B_LOC * MEM_LEN
    return pl.pallas_call(
        body, name="mem_kv",
        out_shape=(_sds((n, D_MODEL), BF16), _sds((n, 2 * W_C), BF16)),
    )(*_pin(mem, wmem))


class _Part:
    def __init__(self, body, args, in_specs, out_specs, out_shape, scratch=()):
        self.body, self.args, self.in_specs, self.out_specs, self.out_shape = body, args, in_specs, out_specs, out_shape
        self.scratch = list(scratch)


def _run_parts(name, parts, semantics, vmem_mb):
    n_in = [len(p.args) for p in parts]
    n_out = [len(p.out_shape) for p in parts]
    n_scr = [len(p.scratch) for p in parts]

    def body(*refs):
        ins, outs, scr = refs[:sum(n_in)], refs[sum(n_in):sum(n_in) + sum(n_out)], refs[sum(n_in) + sum(n_out):]
        i0 = o0 = s0 = 0
        for p, ni, no, ns in zip(parts, n_in, n_out, n_scr):
            p.body(*ins[i0:i0 + ni], *outs[o0:o0 + no], *scr[s0:s0 + ns])
            i0, o0, s0 = i0 + ni, o0 + no, s0 + ns

    res = pl.pallas_call(
        body, name=name, grid=(T // QR,),
        in_specs=[sp for p in parts for sp in p.in_specs], out_specs=tuple(sp for p in parts for sp in p.out_specs),
        out_shape=tuple(sh for p in parts for sh in p.out_shape),
        scratch_shapes=[sc for p in parts for sc in p.scratch],
        compiler_params=_cp((semantics,), vmem_mb=vmem_mb),
    )(*_pin(*[a for p in parts for a in p.args]))
    out, o0 = [], 0
    for no in n_out:
        out.append(tuple(res[o0:o0 + no]))
        o0 += no
    return out


QB = 8
QR = QB * BLK


def _lane_lo():
    return lax.broadcasted_iota(jnp.int32, (1, 128), 1) < 64


def _dup_head(k2, hk, lo):
    kf = k2.astype(F32)
    r = pltpu.roll(kf, 64, 1)
    return (jnp.where(lo, kf, r) if hk == 0 else jnp.where(lo, r, kf)).astype(BF16)


def _stack_heads(pairs, lo):
    parts = []
    for x2 in pairs:
        z = jnp.zeros_like(x2)
        parts += [jnp.where(lo, x2, z), jnp.where(lo, z, x2)]
    return jnp.concatenate(parts, axis=0)


def _prev_mode(kind, nb, j):
    if kind == "mem" or nb == 1:
        return "no"
    if nb <= QB:
        return "yes" if j % nb else "no"
    return "yes" if j else "dyn"


class _Attn:
    def __init__(self, kind, nb, max_dist, gqa, qw, kvw, qcb, kcb, vcb):
        self.kind, self.nb, self.gqa, self.qw, self.kvw = kind, nb, gqa, qw, kvw
        npairs = qw // 128
        self.groups = ([(hk, [2 * hk, 2 * hk + 1]) for hk in range(npairs // 2)] if gqa
                       else [(p, [p]) for p in range(npairs)])
        self.nh = 2 * len(self.groups[0][1])
        self.cols = 128 * self.nh
        self.reach = BLK - max_dist
        self.ext_prev = kind == "band" and nb > QB
        self.q_spec = pl.BlockSpec((QR, qw), lambda g: (g, qcb))
        self.row_spec = pl.BlockSpec((QR, qw), lambda g: (g, 0))
        self.stat_spec = pl.BlockSpec((QR, 128), lambda g: (g, 0))
        if kind == "mem":
            per = SEQ // QR
            self.kv_specs = [pl.BlockSpec((MEM_LEN, kvw), lambda g: (g // per, kcb)),
                             pl.BlockSpec((MEM_LEN, kvw), lambda g: (g // per, vcb))]
        else:
            self.kv_specs = [pl.BlockSpec((QR, kvw), lambda g: (g, kcb)), pl.BlockSpec((QR, kvw), lambda g: (g, vcb))]
            if self.ext_prev:
                self.kv_specs += [pl.BlockSpec((BLK, kvw), lambda g: (jnp.maximum(g * QB - 1, 0), kcb)),
                                  pl.BlockSpec((BLK, kvw), lambda g: (jnp.maximum(g * QB - 1, 0), vcb))]

    def masks(self):
        if self.kind == "mem":
            return None
        kj = lax.broadcasted_iota(jnp.int32, (2 * BLK, self.cols), 0)
        qi = lax.broadcasted_iota(jnp.int32, (2 * BLK, self.cols), 1) & (BLK - 1)
        both = jnp.logical_and(kj >= qi + self.reach, kj <= qi + BLK)
        return kj, qi, self.as_mask(both), self.as_mask(kj[:BLK] <= qi[:BLK])

    def as_mask(self, in_reach):
        return jnp.where(in_reach, 0.0, NEG) if self.gqa else in_reach

    def hide(self, s, mask):
        return s + mask if self.gqa else jnp.where(mask, s, NEG)

    def keys(self, j, gi, kc_ref, vc_ref, kp_ref, vp_ref, lo, kq, g, dup):
        def kv(k_ref, v_ref, r):
            if self.gqa:
                return _dup_head(k_ref[r, :], gi, lo), _dup_head(v_ref[r, :], gi, lo)
            sl = slice(128 * gi, 128 * (gi + 1))
            return k_ref[r, sl], v_ref[r, sl]

        def blocks(b0, b1):
            if not self.gqa:
                return kv(kc_ref, vc_ref, slice(BLK * b0, BLK * b1))
            for b in range(b0, b1):
                if (b, gi) not in dup:
                    dup[b, gi] = kv(kc_ref, vc_ref, slice(BLK * b, BLK * (b + 1)))
            ks, vs = zip(*(dup[b, gi] for b in range(b0, b1)))
            return jnp.concatenate(ks, axis=0), jnp.concatenate(vs, axis=0)

        if self.kind == "mem":
            key0 = pl.multiple_of((g // (SEQ // QR)) * MEM_LEN, MEM_LEN)
            return (*kv(kc_ref, vc_ref, slice(None)), None, [(0, MEM_LEN, key0)])
        kj, qi, both, cur = kq
        row0 = g * QR + BLK * j
        mode = _prev_mode(self.kind, self.nb, j)
        if mode == "no":
            return (*blocks(j, j + 1), cur, [(0, BLK, pl.multiple_of(row0, BLK))])
        if mode == "yes":
            return (*blocks(j - 1, j + 1), both, [(0, 2 * BLK, pl.multiple_of(row0 - BLK, BLK))])
        has_prev = ((g * QB) % self.nb) > 0
        hp = has_prev.astype(jnp.int32)
        mask = self.as_mask(jnp.logical_and(kj >= qi * hp + (self.reach * hp + BLK * (1 - hp)), kj <= qi + BLK))
        kp, vp = kv(kp_ref, vp_ref, slice(None))
        kc, vc = blocks(0, 1)
        return (jnp.concatenate([kp, kc], axis=0), jnp.concatenate([vp, vc], axis=0), mask,
                [(0, BLK, pl.multiple_of(jnp.maximum(row0 - BLK, 0), BLK)), (BLK, BLK, pl.multiple_of(row0, BLK))])


def _attn_fwd(q, qcb, qw, k, kcb, v, vcb, kvw, *, kind, nb=1, max_dist=BLK, gqa=False, sinks=None):
    a = _Attn(kind, nb, max_dist, gqa, qw, kvw, qcb, kcb, vcb)

    def body(*refs):
        it = iter(refs)
        q_ref, kc_ref, vc_ref = next(it), next(it), next(it)
        kp_ref, vp_ref = (next(it), next(it)) if a.ext_prev else (None, None)
        sink_ref = next(it) if sinks is not None else None
        o_ref, lse_ref = next(it), next(it)
        g = pl.program_id(0)
        lo = _lane_lo()
        top = lax.broadcasted_iota(jnp.int32, (128, 1), 0) < 64
        rid = lax.broadcasted_iota(jnp.int32, (8, 128), 0)
        kq, dup = a.masks(), {}
        stats = {}

        def scores(j, gi, pairs):
            rows = slice(BLK * j, BLK * (j + 1))
            qs = _stack_heads([q_ref[rows, 128 * p:128 * (p + 1)] for p in pairs], lo)
            kk, vv, mask, _ = a.keys(j, gi, kc_ref, vc_ref, kp_ref, vp_ref, lo, kq, g, dup)
            pieces = [slice(r0, r0 + BLK) for r0 in range(0, kk.shape[0], BLK)]
            return dict(j=j, gi=gi, pairs=pairs, rows=rows, vv=vv, mask=mask, pieces=pieces,
                        ss=[_dot(kk[r], qs, NT) for r in pieces])

        def softmax(c):
            gi, mask = c["gi"], c["mask"]
            ss = [s if mask is None else a.hide(s, mask[r]) for r, s in zip(c["pieces"], c.pop("ss"))]
            m = jnp.max(ss[0], axis=0, keepdims=True)
            for s in ss[1:]:
                m = jnp.maximum(m, jnp.max(s, axis=0, keepdims=True))
            if sink_ref is not None:
                sk = jnp.concatenate([jnp.full((1, 128), sink_ref[0, a.nh * gi + i], F32) for i in range(a.nh)], axis=1)
                m = jnp.maximum(m, sk)
            ps = [jnp.exp(s - m) for s in ss]
            l = sum(jnp.sum(p, axis=0, keepdims=True) for p in ps)
            if sink_ref is not None:
                l = l + jnp.exp(sk - m)
            c["ps"] = [p.astype(BF16) for p in ps]
            c["l"], c["lse"] = l, m + jnp.log(l)

        def outputs(c):
            j, gi, rows = c["j"], c["gi"], c["rows"]
            ot = sum(_dot(c["vv"][r], p, TN) for r, p in zip(c["pieces"], c["ps"]))
            ot = ot * pl.reciprocal(c["l"], approx=True)
            for i, p in enumerate(c["pairs"]):
                o2t = jnp.where(top, ot[:, 256 * i:256 * i + 128], ot[:, 256 * i + 128:256 * i + 256])
                o_ref[rows, 128 * p:128 * (p + 1)] = o2t.T.astype(BF16)
            stat = stats.get(j, jnp.zeros((8, 128), F32))
            for i in range(a.nh):
                stat = jnp.where(rid == a.nh * gi + i, c["lse"][:, 128 * i:128 * (i + 1)], stat)
            stats[j] = stat
            if gi == a.groups[-1][0]:
                lse_ref[rows, :] = jnp.concatenate([stats.pop(j), jnp.zeros((120, 128), F32)], axis=0).T

        chains = [(j, gi, pairs) for j in range(QB) for gi, pairs in a.groups]
        live = {}
        for t in range(len(chains) + 2):
            if t < len(chains):
                live[t] = scores(*chains[t])
            if 0 <= t - 1 < len(chains):
                softmax(live[t - 1])
            if 0 <= t - 2 < len(chains):
                outputs(live.pop(t - 2))


    args = [q, k, v] + ([k, v] if a.ext_prev else [])
    in_specs = [a.q_spec] + a.kv_specs
    if sinks is not None:
        args.append(sinks)
        in_specs.append(pl.BlockSpec(memory_space=pltpu.SMEM))
    return _Part(body, args, in_specs, [a.row_spec, a.stat_spec], [_sds((T, qw), BF16), _sds((T, 128), F32)])


def _attn_bwd(q, qcb, qw, k, kcb, v, vcb, kvw, do, lse, dl, *, kind, nb=1, max_dist=BLK, gqa=False, sinkv=None,
              mem_in=None):
    a = _Attn(kind, nb, max_dist, gqa, qw, kvw, qcb, kcb, vcb)

    def body(*refs):
        it = iter(refs)
        q_ref, kc_ref, vc_ref = next(it), next(it), next(it)
        kp_ref, vp_ref = (next(it), next(it)) if a.ext_prev else (None, None)
        do_ref, lse_ref, dl_ref = next(it), next(it), next(it)
        sinkv_ref = next(it) if sinkv is not None else None
        mem_ref = next(it) if kind == "mem" else None
        dq_ref = next(it)
        if kind == "mem":
            gmem_ref = next(it)
        else:
            dk_out, dv_out = next(it), next(it)
        dsink_ref = next(it) if sinkv is not None else None
        if kind != "mem":
            dk_ref, dv_ref, stage_k, stage_v, flush_sem = next(it), next(it), next(it), next(it), next(it)
        else:
            dkv_ref = next(it)
        g = pl.program_id(0)
        lo = _lane_lo()
        top = lax.broadcasted_iota(jnp.int32, (128, 1), 0) < 64

        @pl.when(g == 0)
        def _():
            if kind == "mem":
                dkv_ref[...] = jnp.zeros_like(dkv_ref)
            else:
                dk_ref[...] = jnp.zeros_like(dk_ref)
                dv_ref[...] = jnp.zeros_like(dv_ref)
            if dsink_ref is not None:
                dsink_ref[...] = jnp.zeros_like(dsink_ref)

        kq, dup = a.masks(), {}
        stats_t = {}

        def first_matmuls(j, gi, pairs):
            rows = slice(BLK * j, BLK * (j + 1))
            if j not in stats_t:
                stats_t[j] = (lse_ref[rows, :].T, dl_ref[rows, :].T)
            lse_t, dl_t = stats_t[j]
            heads = [a.nh * gi + i for i in range(a.nh)]
            c = dict(rows=rows, gi=gi, pairs=pairs)
            c["qs"] = _stack_heads([q_ref[rows, 128 * p:128 * (p + 1)] for p in pairs], lo)
            c["dos"] = _stack_heads([do_ref[rows, 128 * p:128 * (p + 1)] for p in pairs], lo)
            c["lse_row"] = jnp.concatenate([lse_t[h:h + 1, :] for h in heads], axis=1)
            c["dl_row"] = jnp.concatenate([dl_t[h:h + 1, :] for h in heads], axis=1)
            c["kk"], vv, c["mask"], c["dests"] = a.keys(j, gi, kc_ref, vc_ref, kp_ref, vp_ref, lo, kq, g, dup)
            c["s"] = _dot(c["kk"], c["qs"], NT)
            c["dp"] = _dot(vv, c["dos"], NT)
            return c

        def elementwise(c):
            s = c.pop("s")
            if c["mask"] is not None:
                s = a.hide(s, c["mask"])
            p = jnp.exp(s - c["lse_row"])
            c["ds"] = (p * (c.pop("dp") - c["dl_row"])).astype(BF16)
            c["p"] = p.astype(BF16)

        def last_matmuls(c):
            gi, rows = c["gi"], c["rows"]
            dqt = _dot(c["kk"], c["ds"], TN)
            ck = _dot(c["ds"], c["qs"], NN)
            cv = _dot(c["p"], c["dos"], NN)
            if gqa:
                sel = lo if gi == 0 else jnp.logical_not(lo)
                ck = jnp.where(sel, ck + pltpu.roll(ck, 64, 1), 0.0)
                cv = jnp.where(sel, cv + pltpu.roll(cv, 64, 1), 0.0)
                kcols = slice(0, 128)
            else:
                kcols = slice(128 * gi, 128 * (gi + 1))
            for r0, nr, key0 in c["dests"]:
                krows = pl.ds(key0, nr)
                if kind == "mem":
                    dkv_ref[krows, kcols] += ck[r0:r0 + nr]
                    dkv_ref[krows, slice(kvw + kcols.start, kvw + kcols.stop)] += cv[r0:r0 + nr]
                else:
                    dk_ref[krows, kcols] += ck[r0:r0 + nr]
                    dv_ref[krows, kcols] += cv[r0:r0 + nr]
            for i, p in enumerate(c["pairs"]):
                dq2t = jnp.where(top, dqt[:, 256 * i:256 * i + 128], dqt[:, 256 * i + 128:256 * i + 256])
                dq_ref[rows, 128 * p:128 * (p + 1)] = dq2t.T.astype(BF16)

        chains = [(j, gi, pairs) for j in range(QB) for gi, pairs in a.groups]
        live = {}
        for t in range(len(chains) + 2):
            if t < len(chains):
                live[t] = first_matmuls(*chains[t])
            if 0 <= t - 1 < len(chains):
                elementwise(live[t - 1])
            if 0 <= t - 2 < len(chains):
                last_matmuls(live.pop(t - 2))
        if dsink_ref is not None:
            ps = jnp.exp(sinkv_ref[...] - lse_ref[...]) * dl_ref[...]
            dsink_ref[...] += jnp.sum(ps, axis=0, keepdims=True)
        if kind == "mem":
            @pl.when(g == T // QR - 1)
            def _():
                gmem_ref[...] = _dot(mem_ref[...], dkv_ref[...].astype(BF16), TN)
        else:
            n_steps = T // QR

            def flush(step):
                rows = pl.ds(pl.multiple_of(step * QR, QR), QR)
                out = []
                for acc, stage, dst, i in ((dk_ref, stage_k, dk_out, 0), (dv_ref, stage_v, dv_out, 1)):
                    stage[...] = acc[rows, :].astype(BF16)
                    out.append(pltpu.make_async_copy(stage, dst.at[rows, :], flush_sem.at[i]))
                return out

            def flushed(step):
                rows = pl.ds(pl.multiple_of(step * QR, QR), QR)
                return [pltpu.make_async_copy(stage, dst.at[rows, :], flush_sem.at[i])
                        for stage, dst, i in ((stage_k, dk_out, 0), (stage_v, dv_out, 1))]

            @pl.when(g >= 2)
            def _():
                for cp in flushed(g - 2):
                    cp.wait()

            @pl.when(g >= 1)
            def _():
                for cp in flush(g - 1):
                    cp.start()

            @pl.when(g == n_steps - 1)
            def _():
                for cp in flushed(g - 1):
                    cp.wait()
                for cp in flush(g):
                    cp.start()
                for cp in flushed(g):
                    cp.wait()

    args = [q, k, v] + ([k, v] if a.ext_prev else []) + [do, lse, dl]
    in_specs = [a.q_spec] + a.kv_specs + [a.row_spec, a.stat_spec, a.stat_spec]
    if sinkv is not None:
        args.append(sinkv)
        in_specs.append(_full((1, 128)))
    if kind == "mem":
        args.append(mem_in)
        in_specs.append(pl.BlockSpec(mem_in.shape, lambda g: (0, 0), pipeline_mode=pl.Buffered(1)))
    out_shape = [_sds((T, qw), BF16)]
    out_specs = [a.row_spec]
    scratch = []
    if kind == "mem":
        out_shape.append(_sds((D_MODEL, 2 * kvw), F32))
        out_specs.append(pl.BlockSpec((D_MODEL, 2 * kvw), lambda g: (0, 0), pipeline_mode=pl.Buffered(1)))
        scratch = [pltpu.VMEM((B_LOC * MEM_LEN, 2 * kvw), F32)]
    else:
        out_shape += [_sds((T, kvw), BF16)] * 2
        out_specs += [pl.BlockSpec(memory_space=pl.ANY)] * 2
        scratch = [pltpu.VMEM((T, kvw), F32)] * 2 + [pltpu.VMEM((QR, kvw), BF16)] * 2 + [pltpu.SemaphoreType.DMA((2,))]
    if sinkv is not None:
        out_shape.append(_sds((1, 128), F32))
        out_specs.append(_full((1, 128)))
    return _Part(body, args, in_specs, out_specs, out_shape, scratch)


def _dot2(v, w_ref):
    hi = v.astype(BF16)
    lo = (v - hi.astype(F32)).astype(BF16)
    return _dot(hi, w_ref[...], NN) + _dot(lo, w_ref[...], NN)


def _middle(oa, o1, l1, o4, l4, o16, l16, oc, z, x, tgt, g_br, ln_g, ln_b, wout, spread4, gather4, gather8):
    tm = 512
    spt = SEQ // tm

    def body(oa_ref, o1_ref, l1_ref, o4_ref, l4_ref, o16_ref, l16_ref, oc_ref, z_ref, x_ref, t_ref,
             g_ref, lg_ref, lb_ref, w_ref, sp4_ref, ga4_ref, ga8_ref,
             du_ref, dz_ref, doa_ref, dla_ref,
             dobn_ref, lsen_ref, dlbn_ref, dob4_ref, lse4_ref, dlb4_ref, dob16_ref, lse16_ref, dlb16_ref,
             doc_ref, dlc_ref, acc_ref, gout_ref, scr):
        i = pl.program_id(0)

        @pl.when(i == 0)
        def _():
            acc_ref[...] = jnp.zeros_like(acc_ref)
            gout_ref[...] = jnp.zeros_like(gout_ref)

        q = tm // 4
        for res in range(16):
            rows = pl.ds((res % 4) * q + res // 4, tm // 16, stride=4)
            for j in range(2):
                scr[6 + j, rows, :] = o16_ref[0, res, :, 128 * j:128 * (j + 1)].astype(F32)
            scr[8, rows, :] = l16_ref[0, res]
        for res in range(4):
            rows, blk = pl.ds(res, q, stride=4), slice(res * q, (res + 1) * q)
            for j in range(2):
                scr[j, rows, :] = o4_ref[0, res, :, 128 * j:128 * (j + 1)].astype(F32)
                scr[3 + j, rows, :] = scr[6 + j, blk, :]
            scr[2, rows, :] = l4_ref[0, res]
            scr[5, rows, :] = scr[8, blk, :]
        inv_d = 1.0 / D_MODEL
        gb, lg, lb = g_ref[...], lg_ref[...], lb_ref[...]

        def rms(o):
            r = lax.rsqrt(jnp.sum(o * o, axis=1, keepdims=True) * (1.0 / o.shape[1]) + RMS_EPS)
            return o * r, r

        def rms_bwd(dn_, n_, r):
            return r * (dn_ - n_ * (jnp.sum(dn_ * n_, axis=1, keepdims=True) * (1.0 / n_.shape[1])))

        def forward(rs):
            o4v = jnp.concatenate([scr[0, rs, :], scr[1, rs, :]], axis=1)
            o16v = jnp.concatenate([scr[3, rs, :], scr[4, rs, :]], axis=1)
            l1v, l4v, l16v = l1_ref[rs, :], scr[2, rs, :], scr[5, rs, :]
            mx = jnp.maximum(jnp.maximum(l1v, l4v), l16v)
            e1, e4, e16 = jnp.exp(l1v - mx), jnp.exp(l4v - mx), jnp.exp(l16v - mx)
            ssum = e1 + e4 + e16
            inv = 1.0 / ssum
            c = dict(rs=rs, lse_b=mx + jnp.log(ssum))
            c["ob"] = (_dot2(e1 * inv, sp4_ref) * o1_ref[rs, :].astype(F32) + _dot2(e4 * inv, sp4_ref) * o4v
                       + _dot2(e16 * inv, sp4_ref) * o16v)
            c["oa"], c["oc"] = oa_ref[rs, :].astype(F32), oc_ref[rs, :].astype(F32)
            na, c["ra"] = rms(c["oa"])
            nb_, c["rb"] = rms(c["ob"])
            nc, c["rc"] = rms(c["oc"])
            c["n"] = jnp.concatenate([na, nb_, nc], axis=1)
            c["zf"] = z_ref[rs, :].astype(F32)
            c["sig"] = 1.0 / (1.0 + jnp.exp(-c["zf"]))
            c["sz"] = c["zf"] * c["sig"]
            c["yb"] = (c["n"] * gb * c["sz"]).astype(BF16)
            c["y2"] = _dot(c["yb"], w_ref[...], NN)
            return c

        def norm(c):
            rs = c["rs"]
            u = ALPHA * x_ref[rs, :] + c.pop("y2")
            mu = jnp.sum(u, axis=1, keepdims=True) * inv_d
            uc = u - mu
            rstd = lax.rsqrt(jnp.sum(uc * uc, axis=1, keepdims=True) * inv_d + LN_EPS)
            xh = uc * rstd
            diff = xh * lg + lb - t_ref[rs, :]
            acc_ref[0:1, :] += jnp.sum(diff * diff, axis=0, keepdims=True) * (0.5 * inv_d)
            dout = diff * inv_d
            acc_ref[2:3, :] += jnp.sum(dout * xh, axis=0, keepdims=True)
            acc_ref[3:4, :] += jnp.sum(dout, axis=0, keepdims=True)
            dxh = dout * lg
            du = rstd * (dxh - jnp.sum(dxh, axis=1, keepdims=True) * inv_d
                         - xh * (jnp.sum(dxh * xh, axis=1, keepdims=True) * inv_d))
            dub = du.astype(BF16)
            du_ref[rs, :] = dub
            c["dy"] = _dot(dub, w_ref[...], NT)
            gout_ref[...] += _dot(c.pop("yb"), dub, TN)

        def backward(c):
            rs, n, dy, zf, sig = c["rs"], c["n"], c["dy"], c["zf"], c["sig"]
            t1 = dy * c["sz"]
            acc_ref[1:2, :] += jnp.sum(t1 * n, axis=0, keepdims=True)
            dn = t1 * gb
            dz_ref[rs, :] = (dy * n * gb * (sig * (1.0 + zf * (1.0 - sig)))).astype(BF16)
            doa = rms_bwd(dn[:, :W_A], n[:, :W_A], c["ra"])
            dob = rms_bwd(dn[:, W_A:W_A + W_B], n[:, W_A:W_A + W_B], c["rb"])
            doc = rms_bwd(dn[:, W_A + W_B:], n[:, W_A + W_B:], c["rc"])
            doa_ref[rs, :] = doa.astype(BF16)
            dla_ref[rs, :] = _dot2(doa * c["oa"], ga8_ref)
            doc_ref[rs, :] = doc.astype(BF16)
            dlc_ref[rs, :] = _dot2(doc * c["oc"], ga4_ref)
            dobn_ref[rs, :] = dob.astype(BF16)
            lsen_ref[rs, :] = c["lse_b"]
            dlbn_ref[rs, :] = _dot2(dob * c["ob"], ga4_ref)
            scr[0, rs, :] = dob[:, :128]
            scr[1, rs, :] = dob[:, 128:]

        halves = [slice(h * (tm // 2), (h + 1) * (tm // 2)) for h in range(2)]
        live = {}
        for t in range(len(halves) + 2):
            if t < len(halves):
                live[t] = forward(halves[t])
            if 0 <= t - 1 < len(halves):
                norm(live[t - 1])
            if 0 <= t - 2 < len(halves):
                backward(live.pop(t - 2))
        for j in range(2):
            sl = slice(128 * j, 128 * (j + 1))
            for res in range(4):
                t = scr[j, pl.ds(res, q, stride=4), :]
                dob4_ref[0, res, :, sl] = t.astype(BF16)
                scr[6 + j, res * q:(res + 1) * q, :] = t
            for res in range(16):
                dob16_ref[0, res, :, sl] = scr[6 + j, pl.ds((res % 4) * q + res // 4, tm // 16, stride=4),
                                               :].astype(BF16)
        for res in range(4):
            rows = pl.ds(res, q, stride=4)
            lse4_ref[0, res] = lsen_ref[rows, :]
            dlb4_ref[0, res] = dlbn_ref[rows, :]
        for res in range(16):
            rows = pl.ds(res // 4, tm // 16, stride=4)
            lse16_ref[0, res] = lse4_ref[0, res % 4, rows, :]
            dlb16_ref[0, res] = dlb4_ref[0, res % 4, rows, :]


    tok = lambda w: pl.BlockSpec((tm, w), lambda i: (i, 0))
    p4 = lambda w: pl.BlockSpec((1, 4, tm // 4, w), lambda i: (i // spt, 0, i % spt, 0))
    p16 = lambda w: pl.BlockSpec((1, 16, tm // 16, w), lambda i: (i // spt, 0, i % spt, 0))
    s4 = lambda w, dt: _sds((B_LOC, 4, SEQ // 4, w), dt)
    s16 = lambda w, dt: _sds((B_LOC, 16, SEQ // 16, w), dt)
    row = _full((1, D_MODEL))
    return pl.pallas_call(
        body, name="middle", grid=(T // tm,),
        in_specs=[tok(W_A), tok(W_B), tok(128), p4(W_B), p4(128), p16(W_B), p16(128), tok(W_C), tok(D_MIX),
                  tok(D_MODEL), tok(D_MODEL), row, row, row, _full((D_MIX, D_MODEL)),
                  _full((128, W_B)), _full((W_B, 128)), _full((W_A, 128))],
        out_specs=(tok(D_MODEL), tok(D_MIX), tok(W_A), tok(128),
                   tok(W_B), tok(128), tok(128), p4(W_B), p4(128), p4(128), p16(W_B), p16(128), p16(128),
                   tok(W_C), tok(128), _full((8, D_MODEL)), _full((D_MIX, D_MODEL))),
        out_shape=(_sds((T, D_MODEL), BF16), _sds((T, D_MIX), BF16),
                   _sds((T, W_A), BF16), _sds((T, 128), F32),
                   _sds((T, W_B), BF16), _sds((T, 128), F32), _sds((T, 128), F32),
                   s4(W_B, BF16), s4(128, F32), s4(128, F32), s16(W_B, BF16), s16(128, F32), s16(128, F32),
                   _sds((T, W_C), BF16), _sds((T, 128), F32), _sds((8, D_MODEL), F32),
                   _sds((D_MIX, D_MODEL), F32)),
        scratch_shapes=[pltpu.VMEM((9, tm, 128), F32)],
        compiler_params=_cp(("arbitrary",), vmem_mb=56),
    )(*_pin(oa, o1, l1, o4, l4, o16, l16, oc, z, x, tgt, g_br, ln_g, ln_b, wout, spread4, gather4, gather8))


class _ReduceScatter:
    def __init__(self, shapes):
        self.shapes = shapes

    def scratch_shapes(self):
        out = []
        for n, w in self.shapes:
            h, p = n // 2, n // 4
            out += [pltpu.VMEM((4, h, w), F32), pltpu.VMEM((4, h, w), F32), pltpu.VMEM((6, p, w), BF16),
                    pltpu.VMEM((6, p, w), BF16), pltpu.VMEM((2, p, w), F32), pltpu.VMEM((h, w), F32)]
        na = len(self.shapes)
        dma = pltpu.SemaphoreType.DMA
        return out + [dma((na, 4)), dma((na, 4)), dma((na, 4)), dma((na, 6)), dma((na, 6)), dma((na,)), dma((na,)),
                      dma((na,))]

    def bind(self, g_refs, r_refs, scratch):
        na = len(self.shapes)
        bufs = [scratch[6 * a:6 * a + 6] for a in range(na)]
        mine, sib, stage, land, keep, tot = (tuple(b[i] for b in bufs) for i in range(6))
        loc_sem, s1_send, s1_recv, s2_send, s2_recv, s3_send, s3_recv, st_sem = scratch[6 * na:6 * na + 8]
        x, y, c = lax.axis_index("x"), lax.axis_index("y"), lax.axis_index("c")
        me, sibling = (x, y, c), (x, y, 1 - c)
        xn, yn, dg = (1 - x, y), (x, 1 - y), (1 - x, 1 - y)
        idx = lambda chip: 2 * chip[0] + chip[1]
        my_chip = idx((x, y))
        order = [idx(xn), idx(dg), idx(yn), my_chip]

        def rows(a, k, half):
            n = self.shapes[a][0]
            return pl.ds(pl.multiple_of(k * n + half * (n // 2), 8), n // 2)

        def piece(a, q):
            p = self.shapes[a][0] // 4
            return slice(q * p, (q + 1) * p)

        def load(a, k):
            return pltpu.make_async_copy(g_refs[a].at[rows(a, k, c), :], mine[a].at[k], loc_sem.at[a, k])

        def s1(a, k, half):
            return pltpu.make_async_remote_copy(
                src_ref=g_refs[a].at[rows(a, k, half), :], dst_ref=sib[a].at[k],
                send_sem=s1_send.at[a, k], recv_sem=s1_recv.at[a, k], device_id=sibling, device_id_type=MESH)

        def s2(a, i, to):
            return pltpu.make_async_remote_copy(
                src_ref=stage[a].at[i], dst_ref=land[a].at[i], send_sem=s2_send.at[a, i], recv_sem=s2_recv.at[a, i],
                device_id=to, device_id_type=MESH)

        via = {0: xn, 1: xn, 2: yn, 3: yn, 4: yn, 5: xn}

        def s3(a, half, to):
            return pltpu.make_async_remote_copy(
                src_ref=tot[a], dst_ref=r_refs[a].at[rows(a, 0, half), :], send_sem=s3_send.at[a],
                recv_sem=s3_recv.at[a], device_id=to, device_id_type=MESH)

        def store(a):
            return pltpu.make_async_copy(tot[a], r_refs[a].at[rows(a, 0, c), :], st_sem.at[a])

        def start():
            for k in order:
                for a in range(na):
                    load(a, k).start()
                    s1(a, k, 1 - c).start()

        def chip_sum(a, k):
            load(a, k).wait()
            s1(a, k, c).wait_recv()
            return mine[a][k] + sib[a][k]

        def exchange():
            for a in range(na):
                P, Q = piece(a, 0), piece(a, 1)
                s_xn = chip_sum(a, idx(xn))
                stage[a][0] = s_xn[P].astype(BF16)
                keep[a][1] = s_xn[Q]
                s_dg = chip_sum(a, idx(dg))
                stage[a][1] = s_dg[P].astype(BF16)
                s2(a, 0, (*xn, c)).start()
                s2(a, 1, (*xn, c)).start()
                stage[a][3] = s_dg[Q].astype(BF16)
                s_yn = chip_sum(a, idx(yn))
                stage[a][2] = s_yn[Q].astype(BF16)
                keep[a][0] = s_yn[P]
                s2(a, 2, (*yn, c)).start()
                s2(a, 3, (*yn, c)).start()
                tot[a][...] = chip_sum(a, my_chip)

        def relay():
            for a in range(na):
                P, Q = piece(a, 0), piece(a, 1)
                s2(a, 1, me).wait_recv()
                stage[a][4] = (keep[a][0] + land[a][1].astype(F32)).astype(BF16)
                s2(a, 4, (*yn, c)).start()
                s2(a, 3, me).wait_recv()
                stage[a][5] = (keep[a][1] + land[a][3].astype(F32)).astype(BF16)
                s2(a, 5, (*xn, c)).start()
                s2(a, 0, me).wait_recv()
                tot[a][P, :] += land[a][0].astype(F32)
                s2(a, 2, me).wait_recv()
                tot[a][Q, :] += land[a][2].astype(F32)

        def finish():
            for a in range(na):
                P, Q = piece(a, 0), piece(a, 1)
                s2(a, 4, me).wait_recv()
                tot[a][P, :] += land[a][4].astype(F32)
                s2(a, 5, me).wait_recv()
                tot[a][Q, :] += land[a][5].astype(F32)
                s3(a, c, sibling).start()
                store(a).start()

        def drain():
            for a in range(na):
                s3(a, 1 - c, me).wait_recv()
                store(a).wait()
            for a in range(na):
                for k in order:
                    s1(a, k, 1 - c).wait_send()
                for i in range(6):
                    s2(a, i, (*via[i], c)).wait_send()
                s3(a, c, sibling).wait_send()

        return start, exchange, relay, finish, drain

    def part(self, grads, steps):
        def body(*refs):
            na = len(self.shapes)
            i = pl.program_id(0)
            for step, phase in zip(steps, self.bind(refs[:na], refs[na:2 * na], refs[2 * na:])):
                pl.when(i == step)(phase)

        hbm = pl.BlockSpec(memory_space=pl.ANY)
        return _Part(body, list(grads), [hbm] * len(grads), [hbm] * len(grads),
                     [_sds((n, w), F32) for n, w in self.shapes], self.scratch_shapes())


def _dh_dx(dqa, dka, dva, dqn, dkn, dvn, dq4, dk4, dv4, dq16, dk16, dv16, dqc, dz, du, xb, cos, sa, sb, winT):
    tm = 512
    spt = SEQ // tm

    def body(dqa_ref, dka_ref, dva_ref, dqn_ref, dkn_ref, dvn_ref, dq4_ref, dk4_ref, dv4_ref,
             dq16_ref, dk16_ref, dv16_ref, dqc_ref, dz_ref, du_ref, xb_ref, cos_ref, sa_ref, sb_ref, w_ref,
             gx_ref, db_ref, gin_ref, dh_ref, scr):
        i = pl.program_id(0)

        @pl.when(i == 0)
        def _():
            db_ref[...] = jnp.zeros_like(db_ref)
            gin_ref[...] = jnp.zeros_like(gin_ref)

        cos_t, sa_t, sb_t = cos_ref[...], sa_ref[...], sb_ref[...]

        def rope_t(t):
            return _rope(t, cos_t, sa_t, sb_t, -1)

        def put(r0, val):
            n = val.shape[1]
            dh_ref[:, r0:r0 + n] = val.astype(BF16)
            db_ref[:, r0:r0 + n] += jnp.sum(val, axis=0, keepdims=True)

        put(O_QA, rope_t(dqa_ref[...].astype(F32)) * QK_SCALE)
        put(O_KA, rope_t(dka_ref[...].astype(F32)))
        put(O_VA, dva_ref[...].astype(F32))
        put(O_QC, dqc_ref[...].astype(F32) * QK_SCALE)
        put(O_Z, dz_ref[...].astype(F32))
        for k, (n_ref, r4, r16) in enumerate(((dqn_ref, dq4_ref, dq16_ref), (dkn_ref, dk4_ref, dk16_ref),
                                               (dvn_ref, dv4_ref, dv16_ref))):
            for j in range(2):
                sl = slice(128 * j, 128 * (j + 1))
                a, q = 2 * k + j, tm // 4
                scr[a] = n_ref[:, sl].astype(F32)
                for res in range(16):
                    scr[6 + a, pl.ds((res % 4) * q + res // 4, tm // 16, stride=4), :] = r16[0, res, :, sl].astype(F32)
                for res in range(4):
                    scr[a, pl.ds(res, q, stride=4), :] += (scr[6 + a, res * q:(res + 1) * q, :]
                                                           + r4[0, res, :, sl].astype(F32))
        cat = lambda a: jnp.concatenate([scr[a], scr[a + 1]], axis=1)
        put(O_QB, rope_t(cat(0)) * QK_SCALE)
        put(O_KB, rope_t(cat(2)))
        put(O_VB, cat(4))
        gx_ref[...] = _dot(dh_ref[...], w_ref[...], NN) + ALPHA * du_ref[...].astype(F32)
        gin_ref[...] += _dot(dh_ref[...], xb_ref[...], TN)

    tok = lambda w: pl.BlockSpec((tm, w), lambda i: (i, 0))
    tab = pl.BlockSpec((tm, 128), lambda i: (i % spt, 0))
    p4 = pl.BlockSpec((1, 4, tm // 4, W_B), lambda i: (i // spt, 0, i % spt, 0))
    p16 = pl.BlockSpec((1, 16, tm // 16, W_B), lambda i: (i // spt, 0, i % spt, 0))
    once = lambda shape: pl.BlockSpec(shape, lambda i: (0, 0), pipeline_mode=pl.Buffered(1))
    return pl.pallas_call(
        body, name="dh_dx", grid=(T // tm,),
        in_specs=[tok(W_A), tok(W_KV_A), tok(W_KV_A), tok(W_B), tok(W_B), tok(W_B), p4, p4, p4, p16, p16, p16,
                  tok(W_C), tok(D_MIX), tok(D_MODEL), tok(D_MODEL), tab, tab, tab, once((D_IN, D_MODEL))],
        out_specs=(tok(D_MODEL), _full((1, D_IN)), once((D_IN, D_MODEL))),
        out_shape=(_sds((T, D_MODEL), F32), _sds((1, D_IN), F32), _sds((D_IN, D_MODEL), F32)),
        scratch_shapes=[pltpu.VMEM((tm, D_IN), BF16), pltpu.VMEM((12, tm, 128), F32)],
        compiler_params=_cp(("arbitrary",), vmem_mb=56),
    )(*_pin(dqa, dka, dva, dqn, dkn, dvn, dq4, dk4, dv4, dq16, dk16, dv16, dqc, dz, du, xb, cos, sa, sb, winT))


def _reduce_grads(g_in, acc, dbin, dsink):
    rs = _ReduceScatter([(SH_IN, D_MODEL)])

    def body(g_ref, acc_ref, dbin_ref, dsink_ref, r_ref, sv_ref, sv_mine, sv_all, sv_send, sv_recv, *rs_scratch):
        x, y, c = lax.axis_index("x"), lax.axis_index("y"), lax.axis_index("c")
        chips = [(1 - x, y), (x, 1 - y), (1 - x, 1 - y)]
        start, exchange, relay, finish, drain = rs.bind((g_ref,), (r_ref,), rs_scratch)
        start()

        sv_mine[...] = jnp.zeros_like(sv_mine)
        sv_mine[0:4, :] = acc_ref[0:4, :]
        for k, c0 in enumerate(range(0, D_IN, SV_W)):
            n = min(SV_W, D_IN - c0)
            sv_mine[SV_DB + k:SV_DB + k + 1, 0:n] = dbin_ref[:, c0:c0 + n]
        sv_mine[SV_SINK:SV_SINK + 1, 0:128] = dsink_ref[...]
        my_dev = 4 * x + 2 * y + c
        others = [(x, y, 1 - c)] + [(*chip, cc) for chip in chips for cc in (c, 1 - c)]

        def sv_copy(j, to):
            return pltpu.make_async_remote_copy(
                src_ref=sv_mine, dst_ref=sv_all.at[my_dev], send_sem=sv_send.at[j], recv_sem=sv_recv.at[j],
                device_id=to, device_id_type=MESH)

        sv_sends = [sv_copy(j, to) for j, to in enumerate(others)]
        for cp in sv_sends:
            cp.start()
        exchange()
        relay()
        finish()
        sv_all[my_dev] = sv_mine[...]
        for j in range(7):
            sv_copy(j, (x, y, c)).wait_recv()
        tot = sv_all[0]
        for d in range(1, 8):
            tot = tot + sv_all[d]
        sv_ref[...] = tot
        drain()
        for cp in sv_sends:
            cp.wait_send()

    vm = pl.BlockSpec(memory_space=pltpu.VMEM)
    hbm = pl.BlockSpec(memory_space=pl.ANY)
    return pl.pallas_call(
        body, name="reduce_grads",
        out_shape=(_sds((SH_IN, D_MODEL), F32), _vm_sds((8, SV_W), F32)),
        in_specs=[hbm, vm, vm, vm], out_specs=(hbm, vm),
        scratch_shapes=[pltpu.VMEM((8, SV_W), F32), pltpu.VMEM((8, 8, SV_W), F32),
                        pltpu.SemaphoreType.DMA((7,)), pltpu.SemaphoreType.DMA((7,))] + rs.scratch_shapes(),
        compiler_params=_cp(vmem_mb=40),
    )(pltpu.with_memory_space_constraint(g_in, pltpu.HBM), acc, dbin, dsink)


def _adamw_update(w, g, m, v):
    nm = ADAM_B1 * m + (1.0 - ADAM_B1) * g
    nv = ADAM_B2 * v + (1.0 - ADAM_B2) * (g * g)
    m_hat = nm / (1.0 - ADAM_B1 ** ADAM_STEP)
    v_hat = nv / (1.0 - ADAM_B2 ** ADAM_STEP)
    return -ADAM_LR * (m_hat / (jnp.sqrt(v_hat) + ADAM_EPS) + ADAM_WD * w), nm, nv


SMALL = ((SV_DB, D_IN, 1.0), (SV_SINK, 8, -1.0), (1, D_MIX, 1.0), (2, D_MODEL, 1.0), (3, D_MODEL, 1.0))


def _adamw_all(items, sv, ws, ms, vs, n_chunks=4):
    nb, ns = 4 * len(items), len(SMALL)
    n_out = nb + 1 + 4 * ns

    def body(*refs):
        ins, sv_ref, small_in = refs[:nb], refs[nb], refs[nb + 1:nb + 1 + 3 * ns]
        outs, scratch = refs[nb + 1 + 3 * ns:nb + 1 + 3 * ns + n_out], refs[nb + 1 + 3 * ns + n_out:]
        big_out, loss_ref, small_out = outs[:nb], outs[nb], outs[nb + 1:]
        in_buf, out_buf, load_sem, store_sem = scratch[:nb], scratch[nb:2 * nb], scratch[2 * nb], scratch[2 * nb + 1]
        small_buf, small_sem = scratch[2 * nb + 2:-1], scratch[-1]
        small_loads = [pltpu.make_async_copy(src, buf, small_sem.at[k])
                       for k, (src, buf) in enumerate(zip((sv_ref, *small_in), small_buf))]
        for cp in small_loads:
            cp.start()

        def rows(p, c):
            n = items[p][0].shape[0] // n_chunks
            return pl.ds(c * n, n)

        def load(a, c):
            r = rows(a // 4, c)
            return pltpu.make_async_copy(ins[a].at[r, :], in_buf[a].at[r, :], load_sem.at[a, c])

        def store(a, c):
            r = rows(a // 4, c)
            src = in_buf[a + 1] if a % 4 == 0 else out_buf[a]
            return pltpu.make_async_copy(src.at[r, :], big_out[a].at[r, :], store_sem.at[a, c])

        order = [(p, c) for c in range(n_chunks) for p in range(len(items))]
        for p, c in order:
            for k in range(4):
                load(4 * p + k, c).start()

        for p, c in order:
            for k in range(4):
                load(4 * p + k, c).wait()
            r = rows(p, c)
            w_buf, g_buf, m_buf, v_buf = in_buf[4 * p:4 * p + 4]
            out_buf[4 * p + 1][r, :], out_buf[4 * p + 2][r, :], out_buf[4 * p + 3][r, :] = _adamw_update(
                w_buf[r, :], g_buf[r, :], m_buf[r, :], v_buf[r, :])
            for k in range(4):
                store(4 * p + k, c).start()

        for cp in small_loads:
            cp.wait()
        sv_v, small_v = small_buf[0], small_buf[1:]
        loss_ref[...] = jnp.sum(sv_v[0:1, 0:D_MODEL], axis=1, keepdims=True)
        for p, (row, width, sign) in enumerate(SMALL):
            gv = sign * jnp.concatenate([sv_v[row + k:row + k + 1, 0:min(SV_W, width - c0)]
                                         for k, c0 in enumerate(range(0, width, SV_W))], axis=1)
            small_out[4 * p][...] = gv
            small_out[4 * p + 1][...], small_out[4 * p + 2][...], small_out[4 * p + 3][...] = _adamw_update(
                small_v[p][...], gv, small_v[ns + p][...], small_v[2 * ns + p][...])

        for p, c in order:
            for k in range(4):
                store(4 * p + k, c).wait()

    shapes, args, bufs = [], [], []
    for w, g, m, v in items:
        assert w.shape[0] % (8 * n_chunks) == 0
        shapes += [_sds(w.shape, F32)] * 4
        bufs += [pltpu.VMEM(w.shape, F32)] * 4
        args += [w, g, m, v]
    small_args = [*ws, *ms, *vs]
    whole = lambda a: _full(a.shape)
    hbm = pl.BlockSpec(memory_space=pl.ANY)
    res = pl.pallas_call(
        body, name="adamw", grid=(1,),
        in_specs=[hbm] * (nb + 1 + 3 * ns),
        out_specs=tuple([hbm] * nb + [_full((1, 1))] + [whole(w) for w in ws for _ in range(4)]),
        out_shape=tuple(shapes + [_sds((1, 1), F32)] + [_sds(w.shape, F32) for w in ws for _ in range(4)]),
        scratch_shapes=(bufs + bufs + [pltpu.SemaphoreType.DMA((nb, n_chunks))] * 2
                        + [pltpu.VMEM(a.shape, F32) for a in (sv, *small_args)]
                        + [pltpu.SemaphoreType.DMA((1 + 3 * ns,))]),
        compiler_params=_cp(("arbitrary",), vmem_mb=52),
    )(*_pin(*args, sv, *small_args))
    big = [tuple(res[4 * p:4 * p + 4]) for p in range(len(items))]
    return big, res[nb], [tuple(res[nb + 1 + 4 * p:nb + 5 + 4 * p]) for p in range(ns)]


def _rope_tables():
    pos = np.arange(SEQ, dtype=np.float32)
    inv = (np.float32(ROPE_THETA) ** (-np.arange(0, 64, 2, dtype=np.float32) / np.float32(64))).astype(np.float32)
    ang = np.tile(pos[:, None] * inv[None, :], (1, 4))
    cos, sin = np.cos(ang).astype(np.float32), np.sin(ang).astype(np.float32)
    low = (np.arange(128) % 64) < 32
    zero = np.float32(0.0)
    return jnp.asarray(cos), jnp.asarray(np.where(low, -sin, zero)), jnp.asarray(np.where(low, zero, sin))


def _local_step(x2, mem2, tgt2, winT, wout, wmem, b_in, sinks, g_branch, ln_gain, ln_bias):
    cos, sa, sb = _rope_tables()
    sinkv = jnp.pad(sinks, ((0, 0), (0, 120)))
    head_of_lane = np.arange(512)[:, None] // 64
    gather8 = jnp.asarray(head_of_lane == np.arange(128)[None, :], BF16)
    gather4 = jnp.asarray(head_of_lane[:W_B] == np.arange(128)[None, :], BF16)
    spread4 = jnp.asarray((head_of_lane[:W_B] == np.arange(128)[None, :]).T, BF16)

    xb, qa, ka, va, bn, b4, b16, qc, z, wout, wmem = _in_proj(x2, winT, b_in, cos, sa, sb, wout, wmem)
    memb, mkv = _mem_kv(mem2, wmem)
    b4f, b16f = b4.reshape(T, 768), b16.reshape(T, 768)

    swa = dict(kind="band", nb=SEQ // BLK, max_dist=BLK - 1, gqa=True)
    dil = (dict(kind="band", nb=SEQ // BLK), dict(kind="band", nb=SEQ // 4 // BLK), dict(kind="band", nb=1))
    (oa, lse_a), (o1, l1), (o4, l4), (o16, l16), (oc, lse_c) = _run_parts("attn_fwd", [
        _attn_fwd(qa, 0, W_A, ka, 0, va, 0, W_KV_A, sinks=sinks, **swa),
        _attn_fwd(bn, 0, W_B, bn, 1, bn, 2, W_B, **dil[0]),
        _attn_fwd(b4f, 0, W_B, b4f, 1, b4f, 2, W_B, **dil[1]),
        _attn_fwd(b16f, 0, W_B, b16f, 1, b16f, 2, W_B, **dil[2]),
        _attn_fwd(qc, 0, W_C, mkv, 0, mkv, 1, W_C, kind="mem")], "parallel", 48)

    s4 = lambda w: (B_LOC, 4, SEQ // 4, w)
    s16 = lambda w: (B_LOC, 16, SEQ // 16, w)
    (du, dz, doa, dla, dobn, lsen, dlbn, dob4, lse4, dlb4, dob16, lse16, dlb16, doc, dlc, acc, g_out) = _middle(
        oa, o1, l1, o4.reshape(s4(W_B)), l4.reshape(s4(128)), o16.reshape(s16(W_B)), l16.reshape(s16(128)), oc, z,
        x2, tgt2, g_branch, ln_gain, ln_bias, wout, spread4, gather4, gather8)

    flat = lambda a: a.reshape(T, a.shape[-1])
    (dqa, dka, dva, dsink), (dqc, g_mem) = _run_parts("attn_bwd_a", [
        _attn_bwd(qa, 0, W_A, ka, 0, va, 0, W_KV_A, doa, lse_a, dla, sinkv=sinkv, **swa),
        _attn_bwd(qc, 0, W_C, mkv, 0, mkv, 1, W_C, doc, lse_c, dlc, kind="mem", mem_in=memb)], "arbitrary", 48)
    last = T // QR - 1
    (r_out, r_mem), (dqn, dkn, dvn), (dq4, dk4, dv4), (dq16, dk16, dv16) = _run_parts("attn_bwd_b", [
        _ReduceScatter([(SH_OUT, D_MODEL), (SH_MEM, 2 * W_C)]).part((g_out, g_mem), (0, 1, 2, last, last)),
        _attn_bwd(bn, 0, W_B, bn, 1, bn, 2, W_B, dobn, lsen, dlbn, **dil[0]),
        _attn_bwd(b4f, 0, W_B, b4f, 1, b4f, 2, W_B, flat(dob4), flat(lse4), flat(dlb4), **dil[1]),
        _attn_bwd(b16f, 0, W_B, b16f, 1, b16f, 2, W_B, flat(dob16), flat(lse16), flat(dlb16), **dil[2])],
        "arbitrary", 62)

    r4 = lambda a: a.reshape(s4(W_B))
    r16 = lambda a: a.reshape(s16(W_B))
    gx, dbin, g_in = _dh_dx(dqa, dka, dva, dqn, dkn, dvn, r4(dq4), r4(dk4), r4(dv4), r16(dq16), r16(dk16),
                            r16(dv16), dqc, dz, du, xb, cos, sa, sb, winT)
    return gx, g_in, r_out, r_mem, acc, dbin, dsink


def kernel(x, mem, w_in, b_in, w_mem, attn_sinks, g_branch, w_out, ln_gain, ln_bias, loss_target, m_w_in, m_b_in, m_w_mem, m_attn_sinks, m_g_branch, m_w_out, m_ln_gain, m_ln_bias, v_w_in, v_b_in, v_w_mem, v_attn_sinks, v_g_branch, v_w_out, v_ln_gain, v_ln_bias):
    winT, wout, wmem = _gather_weights(w_in[0].T, w_out[0], w_mem[0])
    gx, g_in, r_out, r_mem, acc, dbin, dsink = _local_step(
        x.reshape(T, D_MODEL), mem.reshape(B_LOC * MEM_LEN, D_MODEL), loss_target.reshape(T, D_MODEL),
        winT, wout, wmem, b_in, attn_sinks, g_branch, ln_gain, ln_bias)
    r_in, sv = _reduce_grads(g_in, acc, dbin, dsink)

    small = ["b_in", "attn_sinks", "g_branch", "ln_gain", "ln_bias"]
    big, loss, steps = _adamw_all(
        [(w_in[0].T, r_in, m_w_in[0].T, v_w_in[0].T), (w_out[0], r_out, m_w_out[0], v_w_out[0]),
         (w_mem[0], r_mem, m_w_mem[0], v_w_mem[0])],
        sv, [b_in, attn_sinks, g_branch, ln_gain, ln_bias], [m_b_in, m_attn_sinks, m_g_branch, m_ln_gain, m_ln_bias],
        [v_b_in, v_attn_sinks, v_g_branch, v_ln_gain, v_ln_bias])
    out = dict(zip(small, steps))
    out["w_in"] = tuple(a.T[None] for a in big[0])
    out["w_out"], out["w_mem"] = (tuple(a[None] for a in st) for st in big[1:])
    names = ["w_in", "b_in", "w_mem", "attn_sinks", "g_branch", "w_out", "ln_gain", "ln_bias"]
    return (loss.reshape(()), gx.reshape(B_LOC, SEQ, D_MODEL), *[out[n][k] for k in range(4) for n in names])
```

```python
import jax
import jax.numpy as jnp
import numpy as np
from jax import lax
from jax.experimental import pallas as pl
from jax.experimental.pallas import tpu as pltpu

F32, BF16 = jnp.float32, jnp.bfloat16

D_MODEL = 1024
SEQ = 2048
B_LOC = 2
T = B_LOC * SEQ
BLK = 128
MEM_LEN = 256
W_A, W_KV_A, W_B, W_C, D_MIX = 512, 128, 256, 256, 1024
D_IN = 2816
O_QA, O_KA, O_VA, O_QB, O_KB, O_VB, O_QC, O_Z = 0, 512, 640, 768, 1024, 1280, 1536, 1792
ROPE_THETA = 10000.0
LN_EPS = 1e-5
RMS_EPS = 1e-6
ALPHA = 2.0 ** 0.25
QK_SCALE = 0.125
N_CHIP = 4
SH_IN, SH_OUT, SH_MEM = D_IN // N_CHIP, D_MIX // N_CHIP, D_MODEL // N_CHIP
NEG = -1e30
ADAM_LR, ADAM_B1, ADAM_B2, ADAM_EPS, ADAM_WD, ADAM_STEP = 0.001, 0.9, 0.999, 1e-08, 0.01, 10
SV_W = 1024
SV_DB, SV_SINK = 4, 7
assert D_MODEL == D_MIX == SV_W and D_IN <= (SV_SINK - SV_DB) * SV_W
MESH = pl.DeviceIdType.MESH

NN = ((1,), (0,))
NT = ((1,), (1,))
TN = ((0,), (0,))


def _dot(a, b, dims):
    return lax.dot_general(a, b, (dims, ((), ())), preferred_element_type=F32)


def _cp(sem=None, vmem_mb=None):
    kw = {}
    if sem is not None:
        kw["dimension_semantics"] = sem
    if vmem_mb is not None:
        kw["vmem_limit_bytes"] = vmem_mb * 1024 * 1024
    return pltpu.CompilerParams(**kw)


def _sds(shape, dtype):
    return pltpu.HBM(shape, dtype)


def _vm_sds(shape, dtype):
    return jax.ShapeDtypeStruct(shape, dtype)


def _pin(*args):
    return [pltpu.with_memory_space_constraint(a, pltpu.HBM) for a in args]


def _full(shape):
    n = len(shape)
    return pl.BlockSpec(shape, lambda *_: (0,) * n)


def _shard_rows(ref, n, chip, half):
    start = pl.multiple_of((2 * chip[0] + chip[1]) * n + half * (n // 2), 16)
    return ref.at[pl.ds(start, n // 2), :]


def _gather_weights(win_sh, wout_sh, wmem_sh):
    half, piece = SH_IN // 2, SH_IN // 4
    shards = ((SH_IN, D_MODEL), (SH_OUT, D_MODEL), (SH_MEM, 2 * W_C))

    def body(a_ref, b_ref, c_ref, oa_ref, ob_ref, oc_ref, raw_a, raw_b, raw_c, own_a, own_b, own_c,
             load_sem, store_sem, ici_send, ici_recv, d2d_send, d2d_recv):
        x, y, c = lax.axis_index("x"), lax.axis_index("y"), lax.axis_index("c")
        me, sibling = (x, y, c), (x, y, 1 - c)
        xn, yn, dg = (1 - x, y), (x, 1 - y), (1 - x, 1 - y)
        srcs, raws = (a_ref, b_ref, c_ref), (raw_a, raw_b, raw_c)
        owns, outs = (own_a, own_b, own_c), (oa_ref, ob_ref, oc_ref)
        loads = [pltpu.make_async_copy(srcs[a], raws[a], load_sem.at[a]) for a in range(3)]
        for cp in loads:
            cp.start()

        def rows(chip, hf, q):
            start = pl.multiple_of((2 * chip[0] + chip[1]) * SH_IN + hf * half + q * piece, 16)
            return oa_ref.at[pl.ds(start, piece), :]

        def copy(sems, k, chip, hf, q, to, src=None):
            blk = rows(chip, hf, q)
            return pltpu.make_async_remote_copy(
                src_ref=blk if src is None else src, dst_ref=blk, send_sem=sems[0].at[k], recv_sem=sems[1].at[k],
                device_id=to, device_id_type=MESH)

        def my_piece(q):
            return own_a.at[pl.ds(pl.multiple_of(c * half + q * piece, 16), piece), :]

        ici, d2d = (ici_send, ici_recv), (d2d_send, d2d_recv)
        stores, direct = [], []
        for a, (n, _) in enumerate(shards):
            loads[a].wait()
            owns[a][...] = raws[a][...].astype(BF16)
            mine = pl.ds(pl.multiple_of((2 * x + y) * n, 16), n)
            stores.append(pltpu.make_async_copy(owns[a], outs[a].at[mine, :], store_sem.at[a]))
            stores[-1].start()
            if a == 0:
                direct = [copy(ici, 0, (x, y), c, 0, (*xn, c), my_piece(0)),
                          copy(ici, 1, (x, y), c, 1, (*xn, c), my_piece(1)),
                          copy(ici, 3, (x, y), c, 0, (*yn, c), my_piece(0)),
                          copy(ici, 4, (x, y), c, 1, (*yn, c), my_piece(1))]
                for cp in direct:
                    cp.start()
        arrivals = [(0, xn, 0), (1, xn, 1), (3, yn, 0), (4, yn, 1), (2, dg, 1), (5, dg, 0)]
        passed = []
        for k, chip, q in arrivals:
            copy(ici, k, chip, c, q, me).wait_recv()
            if k == 0:
                passed.append(copy(ici, 5, xn, c, 0, (*yn, c)))
                passed[-1].start()
            if k == 4:
                passed.append(copy(ici, 2, yn, c, 1, (*xn, c)))
                passed[-1].start()
            passed.append(copy(d2d, k, chip, c, q, sibling))
            passed[-1].start()
        for k, chip, q in arrivals:
            copy(d2d, k, chip, 1 - c, q, me).wait_recv()
        for cp in direct + passed:
            cp.wait_send()
        for cp in stores:
            cp.wait()

    hbm = pl.BlockSpec(memory_space=pl.ANY)
    return pl.pallas_call(
        body, name="gather_weights",
        out_shape=(_sds((D_IN, D_MODEL), BF16), _sds((D_MIX, D_MODEL), BF16), _sds((D_MODEL, 2 * W_C), BF16)),
        in_specs=[hbm, hbm, hbm], out_specs=(hbm, hbm, hbm),
        scratch_shapes=([pltpu.VMEM(sh, F32) for sh in shards] + [pltpu.VMEM(sh, BF16) for sh in shards]
                        + [pltpu.SemaphoreType.DMA((3,))] * 2 + [pltpu.SemaphoreType.DMA((6,))] * 4),
        compiler_params=_cp(vmem_mb=40),
    )(*_pin(win_sh, wout_sh, wmem_sh))


def _rope(t, cos, sa, sb, sign):
    w = t.shape[1]
    reps = w // 128
    c, a, b = (jnp.tile(v, (1, reps)) if reps > 1 else v for v in (cos, sa, sb))
    rot = pltpu.roll(t, w - 32, 1) * a + pltpu.roll(t, 32, 1) * b
    return t * c + rot if sign > 0 else t * c - rot


def _in_proj(x, winT, b_in, cos, sa, sb, wout_own, wmem_own):
    tm = 512
    spt = SEQ // tm
    n_steps = T // tm
    forward_step = n_steps // 2

    def body(x_ref, w_ref, b_ref, cos_ref, sa_ref, sb_ref, wo_in, wm_in,
             xb_ref, qa_ref, ka_ref, va_ref, bn_ref, b4_ref, b16_ref, qc_ref, z_ref, wo_ref, wm_ref,
             scr, ici_send, ici_recv, d2d_send, d2d_recv):
        i = pl.program_id(0)
        mx, my, mc = lax.axis_index("x"), lax.axis_index("y"), lax.axis_index("c")
        chips = [(1 - mx, my), (mx, 1 - my), (1 - mx, 1 - my)]
        full = ((wo_ref, SH_OUT), (wm_ref, SH_MEM))

        def copy(sems, a, j, chip_of_block, half, to):
            blk = _shard_rows(full[a][0], full[a][1], chip_of_block, half)
            return pltpu.make_async_remote_copy(
                src_ref=blk, dst_ref=blk, send_sem=sems[0].at[a, j], recv_sem=sems[1].at[a, j],
                device_id=to, device_id_type=MESH)

        ici, d2d = (ici_send, ici_recv), (d2d_send, d2d_recv)
        pairs = [(a, j, chip) for j, chip in enumerate(chips) for a in range(2)]

        @pl.when(i == 0)
        def _():
            for a, j, chip in pairs:
                copy(ici, a, j, (mx, my), mc, (*chip, mc)).start()

        @pl.when(i == forward_step)
        def _():
            for a, j, chip in pairs:
                copy(ici, a, j, chip, mc, (mx, my, mc)).wait_recv()
                copy(d2d, a, j, chip, mc, (mx, my, 1 - mc)).start()

        @pl.when(i == n_steps - 1)
        def _():
            for a, j, chip in pairs:
                copy(d2d, a, j, chip, 1 - mc, (mx, my, mc)).wait_recv()
            for a, j, chip in pairs:
                copy(ici, a, j, (mx, my), mc, (*chip, mc)).wait_send()
                copy(d2d, a, j, chip, mc, (mx, my, 1 - mc)).wait_send()

        xb = x_ref[...].astype(BF16)
        xb_ref[...] = xb
        cos_t, sa_t, sb_t = cos_ref[...], sa_ref[...], sb_ref[...]

        def proj(r0, n):
            return _dot(xb, w_ref[r0:r0 + n, :], NT) + b_ref[:, r0:r0 + n]

        def rope(t):
            return _rope(t, cos_t, sa_t, sb_t, +1)

        parts = (rope(proj(O_QB, W_B)) * QK_SCALE, rope(proj(O_KB, W_B)), proj(O_VB, W_B))
        for k, part in enumerate(parts):
            bn_ref[:, 256 * k:256 * (k + 1)] = part.astype(BF16)
            scr[2 * k] = part[:, :128]
            scr[2 * k + 1] = part[:, 128:]
        for j in range(6):
            lanes = slice(128 * j, 128 * (j + 1))
            for res in range(4):
                t = scr[j, pl.ds(res, tm // 4, stride=4), :]
                b4_ref[0, res, :, lanes] = t.astype(BF16)
                scr[6 + j, res * (tm // 4):(res + 1) * (tm // 4), :] = t
            for res in range(16):
                b16_ref[0, res, :, lanes] = scr[6 + j, pl.ds((res % 4) * (tm // 4) + res // 4, tm // 16, stride=4),
                                                :].astype(BF16)
        qa_ref[...] = (rope(proj(O_QA, W_A)) * QK_SCALE).astype(BF16)
        assert O_VA == O_KA + W_KV_A
        kv = proj(O_KA, 2 * W_KV_A)
        ka_ref[...] = rope(kv[:, :W_KV_A]).astype(BF16)
        va_ref[...] = kv[:, W_KV_A:].astype(BF16)
        qc_ref[...] = (proj(O_QC, W_C) * QK_SCALE).astype(BF16)
        z_ref[...] = proj(O_Z, D_MIX).astype(BF16)

    tok = lambda w: pl.BlockSpec((tm, w), lambda i: (i, 0))
    tab = pl.BlockSpec((tm, 128), lambda i: (i % spt, 0))
    hbm = pl.BlockSpec(memory_space=pl.ANY)
    return pl.pallas_call(
        body, name="in_proj", grid=(n_steps,),
        in_specs=[tok(D_MODEL), _full((D_IN, D_MODEL)), _full((1, D_IN)), tab, tab, tab, hbm, hbm],
        out_specs=(tok(D_MODEL), tok(W_A), tok(W_KV_A), tok(W_KV_A), tok(768),
                   pl.BlockSpec((1, 4, tm // 4, 768), lambda i: (i // spt, 0, i % spt, 0)),
                   pl.BlockSpec((1, 16, tm // 16, 768), lambda i: (i // spt, 0, i % spt, 0)),
                   tok(W_C), tok(D_MIX), hbm, hbm),
        out_shape=(_sds((T, D_MODEL), BF16), _sds((T, W_A), BF16), _sds((T, W_KV_A), BF16), _sds((T, W_KV_A), BF16),
                   _sds((T, 768), BF16), _sds((B_LOC, 4, SEQ // 4, 768), BF16), _sds((B_LOC, 16, SEQ // 16, 768), BF16),
                   _sds((T, W_C), BF16), _sds((T, D_MIX), BF16),
                   _sds((D_MIX, D_MODEL), BF16), _sds((D_MODEL, 2 * W_C), BF16)),
        input_output_aliases={6: 9, 7: 10},
        scratch_shapes=[pltpu.VMEM((12, tm, 128), F32)] + [pltpu.SemaphoreType.DMA((2, 3))] * 4,
        compiler_params=_cp(("arbitrary",), vmem_mb=48),
    )(*_pin(x, winT, b_in, cos, sa, sb, wout_own, wmem_own))


def _mem_kv(mem, wmem):
    def body(m_ref, w_ref, mb_ref, kv_ref):
        mb = m_ref[...].astype(BF16)
        mb_ref[...] = mb
        kv_ref[...] = _dot(mb, w_ref[...], NN).astype(BF16)

    n = B_LOC * MEM_LEN
    return pl.pallas_call(
        body, name="mem_kv",
        out_shape=(_sds((n, D_MODEL), BF16), _sds((n, 2 * W_C), BF16)),
    )(*_pin(mem, wmem))


class _Part:
    def __init__(self, body, args, in_specs, out_specs, out_shape, scratch=()):
        self.body, self.args, self.in_specs, self.out_specs, self.out_shape = body, args, in_specs, out_specs, out_shape
        self.scratch = list(scratch)


def _run_parts(name, parts, semantics, vmem_mb):
    n_in = [len(p.args) for p in parts]
    n_out = [len(p.out_shape) for p in parts]
    n_scr = [len(p.scratch) for p in parts]

    def body(*refs):
        ins, outs, scr = refs[:sum(n_in)], refs[sum(n_in):sum(n_in) + sum(n_out)], refs[sum(n_in) + sum(n_out):]
        i0 = o0 = s0 = 0
        for p, ni, no, ns in zip(parts, n_in, n_out, n_scr):
            p.body(*ins[i0:i0 + ni], *outs[o0:o0 + no], *scr[s0:s0 + ns])
            i0, o0, s0 = i0 + ni, o0 + no, s0 + ns

    res = pl.pallas_call(
        body, name=name, grid=(T // QR,),
        in_specs=[sp for p in parts for sp in p.in_specs], out_specs=tuple(sp for p in parts for sp in p.out_specs),
        out_shape=tuple(sh for p in parts for sh in p.out_shape),
        scratch_shapes=[sc for p in parts for sc in p.scratch],
        compiler_params=_cp((semantics,), vmem_mb=vmem_mb),
    )(*_pin(*[a for p in parts for a in p.args]))
    out, o0 = [], 0
    for no in n_out:
        out.append(tuple(res[o0:o0 + no]))
        o0 += no
    return out


QB = 8
QR = QB * BLK


def _lane_lo():
    return lax.broadcasted_iota(jnp.int32, (1, 128), 1) < 64


def _dup_head(k2, hk, lo):
    kf = k2.astype(F32)
    r = pltpu.roll(kf, 64, 1)
    return (jnp.where(lo, kf, r) if hk == 0 else jnp.where(lo, r, kf)).astype(BF16)


def _stack_heads(pairs, lo):
    parts = []
    for x2 in pairs:
        z = jnp.zeros_like(x2)
        parts += [jnp.where(lo, x2, z), jnp.where(lo, z, x2)]
    return jnp.concatenate(parts, axis=0)


def _prev_mode(kind, nb, j):
    if kind == "mem" or nb == 1:
        return "no"
    if nb <= QB:
        return "yes" if j % nb else "no"
    return "yes" if j else "dyn"


class _Attn:
    def __init__(self, kind, nb, max_dist, gqa, qw, kvw, qcb, kcb, vcb):
        self.kind, self.nb, self.gqa, self.qw, self.kvw = kind, nb, gqa, qw, kvw
        npairs = qw // 128
        self.groups = ([(hk, [2 * hk, 2 * hk + 1]) for hk in range(npairs // 2)] if gqa
                       else [(p, [p]) for p in range(npairs)])
        self.nh = 2 * len(self.groups[0][1])
        self.cols = 128 * self.nh
        self.reach = BLK - max_dist
        self.ext_prev = kind == "band" and nb > QB
        self.q_spec = pl.BlockSpec((QR, qw), lambda g: (g, qcb))
        self.row_spec = pl.BlockSpec((QR, qw), lambda g: (g, 0))
        self.stat_spec = pl.BlockSpec((QR, 128), lambda g: (g, 0))
        if kind == "mem":
            per = SEQ // QR
            self.kv_specs = [pl.BlockSpec((MEM_LEN, kvw), lambda g: (g // per, kcb)),
                             pl.BlockSpec((MEM_LEN, kvw), lambda g: (g // per, vcb))]
        else:
            self.kv_specs = [pl.BlockSpec((QR, kvw), lambda g: (g, kcb)), pl.BlockSpec((QR, kvw), lambda g: (g, vcb))]
            if self.ext_prev:
                self.kv_specs += [pl.BlockSpec((BLK, kvw), lambda g: (jnp.maximum(g * QB - 1, 0), kcb)),
                                  pl.BlockSpec((BLK, kvw), lambda g: (jnp.maximum(g * QB - 1, 0), vcb))]

    def masks(self):
        if self.kind == "mem":
            return None
        kj = lax.broadcasted_iota(jnp.int32, (2 * BLK, self.cols), 0)
        qi = lax.broadcasted_iota(jnp.int32, (2 * BLK, self.cols), 1) & (BLK - 1)
        both = jnp.logical_and(kj >= qi + self.reach, kj <= qi + BLK)
        return kj, qi, self.as_mask(both), self.as_mask(kj[:BLK] <= qi[:BLK])

    def as_mask(self, in_reach):
        return jnp.where(in_reach, 0.0, NEG) if self.gqa else in_reach

    def hide(self, s, mask):
        return s + mask if self.gqa else jnp.where(mask, s, NEG)

    def keys(self, j, gi, kc_ref, vc_ref, kp_ref, vp_ref, lo, kq, g, dup):
        def kv(k_ref, v_ref, r):
            if self.gqa:
                return _dup_head(k_ref[r, :], gi, lo), _dup_head(v_ref[r, :], gi, lo)
            sl = slice(128 * gi, 128 * (gi + 1))
            return k_ref[r, sl], v_ref[r, sl]

        def blocks(b0, b1):
            if not self.gqa:
                return kv(kc_ref, vc_ref, slice(BLK * b0, BLK * b1))
            for b in range(b0, b1):
                if (b, gi) not in dup:
                    dup[b, gi] = kv(kc_ref, vc_ref, slice(BLK * b, BLK * (b + 1)))
            ks, vs = zip(*(dup[b, gi] for b in range(b0, b1)))
            return jnp.concatenate(ks, axis=0), jnp.concatenate(vs, axis=0)

        if self.kind == "mem":
            key0 = pl.multiple_of((g // (SEQ // QR)) * MEM_LEN, MEM_LEN)
            return (*kv(kc_ref, vc_ref, slice(None)), None, [(0, MEM_LEN, key0)])
        kj, qi, both, cur = kq
        row0 = g * QR + BLK * j
        mode = _prev_mode(self.kind, self.nb, j)
        if mode == "no":
            return (*blocks(j, j + 1), cur, [(0, BLK, pl.multiple_of(row0, BLK))])
        if mode == "yes":
            return (*blocks(j - 1, j + 1), both, [(0, 2 * BLK, pl.multiple_of(row0 - BLK, BLK))])
        has_prev = ((g * QB) % self.nb) > 0
        hp = has_prev.astype(jnp.int32)
        mask = self.as_mask(jnp.logical_and(kj >= qi * hp + (self.reach * hp + BLK * (1 - hp)), kj <= qi + BLK))
        kp, vp = kv(kp_ref, vp_ref, slice(None))
        kc, vc = blocks(0, 1)
        return (jnp.concatenate([kp, kc], axis=0), jnp.concatenate([vp, vc], axis=0), mask,
                [(0, BLK, pl.multiple_of(jnp.maximum(row0 - BLK, 0), BLK)), (BLK, BLK, pl.multiple_of(row0, BLK))])


def _attn_fwd(q, qcb, qw, k, kcb, v, vcb, kvw, *, kind, nb=1, max_dist=BLK, gqa=False, sinks=None):
    a = _Attn(kind, nb, max_dist, gqa, qw, kvw, qcb, kcb, vcb)

    def body(*refs):
        it = iter(refs)
        q_ref, kc_ref, vc_ref = next(it), next(it), next(it)
        kp_ref, vp_ref = (next(it), next(it)) if a.ext_prev else (None, None)
        sink_ref = next(it) if sinks is not None else None
        o_ref, lse_ref = next(it), next(it)
        g = pl.program_id(0)
        lo = _lane_lo()
        top = lax.broadcasted_iota(jnp.int32, (128, 1), 0) < 64
        rid = lax.broadcasted_iota(jnp.int32, (8, 128), 0)
        kq, dup = a.masks(), {}
        stats = {}

        def scores(j, gi, pairs):
            rows = slice(BLK * j, BLK * (j + 1))
            qs = _stack_heads([q_ref[rows, 128 * p:128 * (p + 1)] for p in pairs], lo)
            kk, vv, mask, _ = a.keys(j, gi, kc_ref, vc_ref, kp_ref, vp_ref, lo, kq, g, dup)
            pieces = [slice(r0, r0 + BLK) for r0 in range(0, kk.shape[0], BLK)]
            return dict(j=j, gi=gi, pairs=pairs, rows=rows, vv=vv, mask=mask, pieces=pieces,
                        ss=[_dot(kk[r], qs, NT) for r in pieces])

        def softmax(c):
            gi, mask = c["gi"], c["mask"]
            ss = [s if mask is None else a.hide(s, mask[r]) for r, s in zip(c["pieces"], c.pop("ss"))]
            m = jnp.max(ss[0], axis=0, keepdims=True)
            for s in ss[1:]:
                m = jnp.maximum(m, jnp.max(s, axis=0, keepdims=True))
            if sink_ref is not None:
                sk = jnp.concatenate([jnp.full((1, 128), sink_ref[0, a.nh * gi + i], F32) for i in range(a.nh)], axis=1)
                m = jnp.maximum(m, sk)
            ps = [jnp.exp(s - m) for s in ss]
            l = sum(jnp.sum(p, axis=0, keepdims=True) for p in ps)
            if sink_ref is not None:
                l = l + jnp.exp(sk - m)
            c["ps"] = [p.astype(BF16) for p in ps]
            c["l"], c["lse"] = l, m + jnp.log(l)

        def outputs(c):
            j, gi, rows = c["j"], c["gi"], c["rows"]
            ot = sum(_dot(c["vv"][r], p, TN) for r, p in zip(c["pieces"], c["ps"]))
            ot = ot * pl.reciprocal(c["l"], approx=True)
            for i, p in enumerate(c["pairs"]):
                o2t = jnp.where(top, ot[:, 256 * i:256 * i + 128], ot[:, 256 * i + 128:256 * i + 256])
                o_ref[rows, 128 * p:128 * (p + 1)] = o2t.T.astype(BF16)
            stat = stats.get(j, jnp.zeros((8, 128), F32))
            for i in range(a.nh):
                stat = jnp.where(rid == a.nh * gi + i, c["lse"][:, 128 * i:128 * (i + 1)], stat)
            stats[j] = stat
            if gi == a.groups[-1][0]:
                lse_ref[rows, :] = jnp.concatenate([stats.pop(j), jnp.zeros((120, 128), F32)], axis=0).T

        chains = [(j, gi, pairs) for j in range(QB) for gi, pairs in a.groups]
        live = {}
        for t in range(len(chains) + 2):
            if t < len(chains):
                live[t] = scores(*chains[t])
            if 0 <= t - 1 < len(chains):
                softmax(live[t - 1])
            if 0 <= t - 2 < len(chains):
                outputs(live.pop(t - 2))


    args = [q, k, v] + ([k, v] if a.ext_prev else [])
    in_specs = [a.q_spec] + a.kv_specs
    if sinks is not None:
        args.append(sinks)
        in_specs.append(pl.BlockSpec(memory_space=pltpu.SMEM))
    return _Part(body, args, in_specs, [a.row_spec, a.stat_spec], [_sds((T, qw), BF16), _sds((T, 128), F32)])


def _attn_bwd(q, qcb, qw, k, kcb, v, vcb, kvw, do, lse, dl, *, kind, nb=1, max_dist=BLK, gqa=False, sinkv=None,
              mem_in=None):
    a = _Attn(kind, nb, max_dist, gqa, qw, kvw, qcb, kcb, vcb)

    def body(*refs):
        it = iter(refs)
        q_ref, kc_ref, vc_ref = next(it), next(it), next(it)
        kp_ref, vp_ref = (next(it), next(it)) if a.ext_prev else (None, None)
        do_ref, lse_ref, dl_ref = next(it), next(it), next(it)
        sinkv_ref = next(it) if sinkv is not None else None
        mem_ref = next(it) if kind == "mem" else None
        dq_ref = next(it)
        if kind == "mem":
            gmem_ref = next(it)
        else:
            dk_out, dv_out = next(it), next(it)
        dsink_ref = next(it) if sinkv is not None else None
        if kind != "mem":
            dk_ref, dv_ref, stage_k, stage_v, flush_sem = next(it), next(it), next(it), next(it), next(it)
        else:
            dkv_ref = next(it)
        g = pl.program_id(0)
        lo = _lane_lo()
        top = lax.broadcasted_iota(jnp.int32, (128, 1), 0) < 64

        @pl.when(g == 0)
        def _():
            if kind == "mem":
                dkv_ref[...] = jnp.zeros_like(dkv_ref)
            else:
                dk_ref[...] = jnp.zeros_like(dk_ref)
                dv_ref[...] = jnp.zeros_like(dv_ref)
            if dsink_ref is not None:
                dsink_ref[...] = jnp.zeros_like(dsink_ref)

        kq, dup = a.masks(), {}
        stats_t = {}

        def first_matmuls(j, gi, pairs):
            rows = slice(BLK * j, BLK * (j + 1))
            if j not in stats_t:
                stats_t[j] = (lse_ref[rows, :].T, dl_ref[rows, :].T)
            lse_t, dl_t = stats_t[j]
            heads = [a.nh * gi + i for i in range(a.nh)]
            c = dict(rows=rows, gi=gi, pairs=pairs)
            c["qs"] = _stack_heads([q_ref[rows, 128 * p:128 * (p + 1)] for p in pairs], lo)
            c["dos"] = _stack_heads([do_ref[rows, 128 * p:128 * (p + 1)] for p in pairs], lo)
            c["lse_row"] = jnp.concatenate([lse_t[h:h + 1, :] for h in heads], axis=1)
            c["dl_row"] = jnp.concatenate([dl_t[h:h + 1, :] for h in heads], axis=1)
            c["kk"], vv, c["mask"], c["dests"] = a.keys(j, gi, kc_ref, vc_ref, kp_ref, vp_ref, lo, kq, g, dup)
            c["s"] = _dot(c["kk"], c["qs"], NT)
            c["dp"] = _dot(vv, c["dos"], NT)
            return c

        def elementwise(c):
            s = c.pop("s")
            if c["mask"] is not None:
                s = a.hide(s, c["mask"])
            p = jnp.exp(s - c["lse_row"])
            c["ds"] = (p * (c.pop("dp") - c["dl_row"])).astype(BF16)
            c["p"] = p.astype(BF16)

        def last_matmuls(c):
            gi, rows = c["gi"], c["rows"]
            dqt = _dot(c["kk"], c["ds"], TN)
            ck = _dot(c["ds"], c["qs"], NN)
            cv = _dot(c["p"], c["dos"], NN)
            if gqa:
                sel = lo if gi == 0 else jnp.logical_not(lo)
                ck = jnp.where(sel, ck + pltpu.roll(ck, 64, 1), 0.0)
                cv = jnp.where(sel, cv + pltpu.roll(cv, 64, 1), 0.0)
                kcols = slice(0, 128)
            else:
                kcols = slice(128 * gi, 128 * (gi + 1))
            for r0, nr, key0 in c["dests"]:
                krows = pl.ds(key0, nr)
                if kind == "mem":
                    dkv_ref[krows, kcols] += ck[r0:r0 + nr]
                    dkv_ref[krows, slice(kvw + kcols.start, kvw + kcols.stop)] += cv[r0:r0 + nr]
                else:
                    dk_ref[krows, kcols] += ck[r0:r0 + nr]
                    dv_ref[krows, kcols] += cv[r0:r0 + nr]
            for i, p in enumerate(c["pairs"]):
                dq2t = jnp.where(top, dqt[:, 256 * i:256 * i + 128], dqt[:, 256 * i + 128:256 * i + 256])
                dq_ref[rows, 128 * p:128 * (p + 1)] = dq2t.T.astype(BF16)

        chains = [(j, gi, pairs) for j in range(QB) for gi, pairs in a.groups]
        live = {}
        for t in range(len(chains) + 2):
            if t < len(chains):
                live[t] = first_matmuls(*chains[t])
            if 0 <= t - 1 < len(chains):
                elementwise(live[t - 1])
            if 0 <= t - 2 < len(chains):
                last_matmuls(live.pop(t - 2))
        if dsink_ref is not None:
            ps = jnp.exp(sinkv_ref[...] - lse_ref[...]) * dl_ref[...]
            dsink_ref[...] += jnp.sum(ps, axis=0, keepdims=True)
        if kind == "mem":
            @pl.when(g == T // QR - 1)
            def _():
                gmem_ref[...] = _dot(mem_ref[...], dkv_ref[...].astype(BF16), TN)
        else:
            n_steps = T // QR

            def flush(step):
                rows = pl.ds(pl.multiple_of(step * QR, QR), QR)
                out = []
                for acc, stage, dst, i in ((dk_ref, stage_k, dk_out, 0), (dv_ref, stage_v, dv_out, 1)):
                    stage[...] = acc[rows, :].astype(BF16)
                    out.append(pltpu.make_async_copy(stage, dst.at[rows, :], flush_sem.at[i]))
                return out

            def flushed(step):
                rows = pl.ds(pl.multiple_of(step * QR, QR), QR)
                return [pltpu.make_async_copy(stage, dst.at[rows, :], flush_sem.at[i])
                        for stage, dst, i in ((stage_k, dk_out, 0), (stage_v, dv_out, 1))]

            @pl.when(g >= 2)
            def _():
                for cp in flushed(g - 2):
                    cp.wait()

            @pl.when(g >= 1)
            def _():
                for cp in flush(g - 1):
                    cp.start()

            @pl.when(g == n_steps - 1)
            def _():
                for cp in flushed(g - 1):
                    cp.wait()
                for cp in flush(g):
                    cp.start()
                for cp in flushed(g):
                    cp.wait()

    args = [q, k, v] + ([k, v] if a.ext_prev else []) + [do, lse, dl]
    in_specs = [a.q_spec] + a.kv_specs + [a.row_spec, a.stat_spec, a.stat_spec]
    if sinkv is not None:
        args.append(sinkv)
        in_specs.append(_full((1, 128)))
    if kind == "mem":
        args.append(mem_in)
        in_specs.append(pl.BlockSpec(mem_in.shape, lambda g: (0, 0), pipeline_mode=pl.Buffered(1)))
    out_shape = [_sds((T, qw), BF16)]
    out_specs = [a.row_spec]
    scratch = []
    if kind == "mem":
        out_shape.append(_sds((D_MODEL, 2 * kvw), F32))
        out_specs.append(pl.BlockSpec((D_MODEL, 2 * kvw), lambda g: (0, 0), pipeline_mode=pl.Buffered(1)))
        scratch = [pltpu.VMEM((B_LOC * MEM_LEN, 2 * kvw), F32)]
    else:
        out_shape += [_sds((T, kvw), BF16)] * 2
        out_specs += [pl.BlockSpec(memory_space=pl.ANY)] * 2
        scratch = [pltpu.VMEM((T, kvw), F32)] * 2 + [pltpu.VMEM((QR, kvw), BF16)] * 2 + [pltpu.SemaphoreType.DMA((2,))]
    if sinkv is not None:
        out_shape.append(_sds((1, 128), F32))
        out_specs.append(_full((1, 128)))
    return _Part(body, args, in_specs, out_specs, out_shape, scratch)


def _dot2(v, w_ref):
    hi = v.astype(BF16)
    lo = (v - hi.astype(F32)).astype(BF16)
    return _dot(hi, w_ref[...], NN) + _dot(lo, w_ref[...], NN)


def _middle(oa, o1, l1, o4, l4, o16, l16, oc, z, x, tgt, g_br, ln_g, ln_b, wout, spread4, gather4, gather8):
    tm = 512
    spt = SEQ // tm

    def body(oa_ref, o1_ref, l1_ref, o4_ref, l4_ref, o16_ref, l16_ref, oc_ref, z_ref, x_ref, t_ref,
             g_ref, lg_ref, lb_ref, w_ref, sp4_ref, ga4_ref, ga8_ref,
             du_ref, dz_ref, doa_ref, dla_ref,
             dobn_ref, lsen_ref, dlbn_ref, dob4_ref, lse4_ref, dlb4_ref, dob16_ref, lse16_ref, dlb16_ref,
             doc_ref, dlc_ref, acc_ref, gout_ref, scr):
        i = pl.program_id(0)

        @pl.when(i == 0)
        def _():
            acc_ref[...] = jnp.zeros_like(acc_ref)
            gout_ref[...] = jnp.zeros_like(gout_ref)

        q = tm // 4
        for res in range(16):
            rows = pl.ds((res % 4) * q + res // 4, tm // 16, stride=4)
            for j in range(2):
                scr[6 + j, rows, :] = o16_ref[0, res, :, 128 * j:128 * (j + 1)].astype(F32)
            scr[8, rows, :] = l16_ref[0, res]
        for res in range(4):
            rows, blk = pl.ds(res, q, stride=4), slice(res * q, (res + 1) * q)
            for j in range(2):
                scr[j, rows, :] = o4_ref[0, res, :, 128 * j:128 * (j + 1)].astype(F32)
                scr[3 + j, rows, :] = scr[6 + j, blk, :]
            scr[2, rows, :] = l4_ref[0, res]
            scr[5, rows, :] = scr[8, blk, :]
        inv_d = 1.0 / D_MODEL
        gb, lg, lb = g_ref[...], lg_ref[...], lb_ref[...]

        def rms(o):
            r = lax.rsqrt(jnp.sum(o * o, axis=1, keepdims=True) * (1.0 / o.shape[1]) + RMS_EPS)
            return o * r, r

        def rms_bwd(dn_, n_, r):
            return r * (dn_ - n_ * (jnp.sum(dn_ * n_, axis=1, keepdims=True) * (1.0 / n_.shape[1])))

        def forward(rs):
            o4v = jnp.concatenate([scr[0, rs, :], scr[1, rs, :]], axis=1)
            o16v = jnp.concatenate([scr[3, rs, :], scr[4, rs, :]], axis=1)
            l1v, l4v, l16v = l1_ref[rs, :], scr[2, rs, :], scr[5, rs, :]
            mx = jnp.maximum(jnp.maximum(l1v, l4v), l16v)
            e1, e4, e16 = jnp.exp(l1v - mx), jnp.exp(l4v - mx), jnp.exp(l16v - mx)
            ssum = e1 + e4 + e16
            inv = 1.0 / ssum
            c = dict(rs=rs, lse_b=mx + jnp.log(ssum))
            c["ob"] = (_dot2(e1 * inv, sp4_ref) * o1_ref[rs, :].astype(F32) + _dot2(e4 * inv, sp4_ref) * o4v
                       + _dot2(e16 * inv, sp4_ref) * o16v)
            c["oa"], c["oc"] = oa_ref[rs, :].astype(F32), oc_ref[rs, :].astype(F32)
            na, c["ra"] = rms(c["oa"])
            nb_, c["rb"] = rms(c["ob"])
            nc, c["rc"] = rms(c["oc"])
            c["n"] = jnp.concatenate([na, nb_, nc], axis=1)
            c["zf"] = z_ref[rs, :].astype(F32)
            c["sig"] = 1.0 / (1.0 + jnp.exp(-c["zf"]))
            c["sz"] = c["zf"] * c["sig"]
            c["yb"] = (c["n"] * gb * c["sz"]).astype(BF16)
            c["y2"] = _dot(c["yb"], w_ref[...], NN)
            return c

        def norm(c):
            rs = c["rs"]
            u = ALPHA * x_ref[rs, :] + c.pop("y2")
            mu = jnp.sum(u, axis=1, keepdims=True) * inv_d
            uc = u - mu
            rstd = lax.rsqrt(jnp.sum(uc * uc, axis=1, keepdims=True) * inv_d + LN_EPS)
            xh = uc * rstd
            diff = xh * lg + lb - t_ref[rs, :]
            acc_ref[0:1, :] += jnp.sum(diff * diff, axis=0, keepdims=True) * (0.5 * inv_d)
            dout = diff * inv_d
            acc_ref[2:3, :] += jnp.sum(dout * xh, axis=0, keepdims=True)
            acc_ref[3:4, :] += jnp.sum(dout, axis=0, keepdims=True)
            dxh = dout * lg
            du = rstd * (dxh - jnp.sum(dxh, axis=1, keepdims=True) * inv_d
                         - xh * (jnp.sum(dxh * xh, axis=1, keepdims=True) * inv_d))
            dub = du.astype(BF16)
            du_ref[rs, :] = dub
            c["dy"] = _dot(dub, w_ref[...], NT)
            gout_ref[...] += _dot(c.pop("yb"), dub, TN)

        def backward(c):
            rs, n, dy, zf, sig = c["rs"], c["n"], c["dy"], c["zf"], c["sig"]
            t1 = dy * c["sz"]
            acc_ref[1:2, :] += jnp.sum(t1 * n, axis=0, keepdims=True)
            dn = t1 * gb
            dz_ref[rs, :] = (dy * n * gb * (sig * (1.0 + zf * (1.0 - sig)))).astype(BF16)
            doa = rms_bwd(dn[:, :W_A], n[:, :W_A], c["ra"])
            dob = rms_bwd(dn[:, W_A:W_A + W_B], n[:, W_A:W_A + W_B], c["rb"])
            doc = rms_bwd(dn[:, W_A + W_B:], n[:, W_A + W_B:], c["rc"])
            doa_ref[rs, :] = doa.astype(BF16)
            dla_ref[rs, :] = _dot2(doa * c["oa"], ga8_ref)
            doc_ref[rs, :] = doc.astype(BF16)
            dlc_ref[rs, :] = _dot2(doc * c["oc"], ga4_ref)
            dobn_ref[rs, :] = dob.astype(BF16)
            lsen_ref[rs, :] = c["lse_b"]
            dlbn_ref[rs, :] = _dot2(dob * c["ob"], ga4_ref)
            scr[0, rs, :] = dob[:, :128]
            scr[1, rs, :] = dob[:, 128:]

        halves = [slice(h * (tm // 2), (h + 1) * (tm // 2)) for h in range(2)]
        live = {}
        for t in range(len(halves) + 2):
            if t < len(halves):
                live[t] = forward(halves[t])
            if 0 <= t - 1 < len(halves):
                norm(live[t - 1])
            if 0 <= t - 2 < len(halves):
                backward(live.pop(t - 2))
        for j in range(2):
            sl = slice(128 * j, 128 * (j + 1))
            for res in range(4):
                t = scr[j, pl.ds(res, q, stride=4), :]
                dob4_ref[0, res, :, sl] = t.astype(BF16)
                scr[6 + j, res * q:(res + 1) * q, :] = t
            for res in range(16):
                dob16_ref[0, res, :, sl] = scr[6 + j, pl.ds((res % 4) * q + res // 4, tm // 16, stride=4),
                                               :].astype(BF16)
        for res in range(4):
            rows = pl.ds(res, q, stride=4)
            lse4_ref[0, res] = lsen_ref[rows, :]
            dlb4_ref[0, res] = dlbn_ref[rows, :]
        for res in range(16):
            rows = pl.ds(res // 4, tm // 16, stride=4)
            lse16_ref[0, res] = lse4_ref[0, res % 4, rows, :]
            dlb16_ref[0, res] = dlb4_ref[0, res % 4, rows, :]


    tok = lambda w: pl.BlockSpec((tm, w), lambda i: (i, 0))
    p4 = lambda w: pl.BlockSpec((1, 4, tm // 4, w), lambda i: (i // spt, 0, i % spt, 0))
    p16 = lambda w: pl.BlockSpec((1, 16, tm // 16, w), lambda i: (i // spt, 0, i % spt, 0))
    s4 = lambda w, dt: _sds((B_LOC, 4, SEQ // 4, w), dt)
    s16 = lambda w, dt: _sds((B_LOC, 16, SEQ // 16, w), dt)
    row = _full((1, D_MODEL))
    return pl.pallas_call(
        body, name="middle", grid=(T // tm,),
        in_specs=[tok(W_A), tok(W_B), tok(128), p4(W_B), p4(128), p16(W_B), p16(128), tok(W_C), tok(D_MIX),
                  tok(D_MODEL), tok(D_MODEL), row, row, row, _full((D_MIX, D_MODEL)),
                  _full((128, W_B)), _full((W_B, 128)), _full((W_A, 128))],
        out_specs=(tok(D_MODEL), tok(D_MIX), tok(W_A), tok(128),
                   tok(W_B), tok(128), tok(128), p4(W_B), p4(128), p4(128), p16(W_B), p16(128), p16(128),
                   tok(W_C), tok(128), _full((8, D_MODEL)), _full((D_MIX, D_MODEL))),
        out_shape=(_sds((T, D_MODEL), BF16), _sds((T, D_MIX), BF16),
                   _sds((T, W_A), BF16), _sds((T, 128), F32),
                   _sds((T, W_B), BF16), _sds((T, 128), F32), _sds((T, 128), F32),
                   s4(W_B, BF16), s4(128, F32), s4(128, F32), s16(W_B, BF16), s16(128, F32), s16(128, F32),
                   _sds((T, W_C), BF16), _sds((T, 128), F32), _sds((8, D_MODEL), F32),
                   _sds((D_MIX, D_MODEL), F32)),
        scratch_shapes=[pltpu.VMEM((9, tm, 128), F32)],
        compiler_params=_cp(("arbitrary",), vmem_mb=56),
    )(*_pin(oa, o1, l1, o4, l4, o16, l16, oc, z, x, tgt, g_br, ln_g, ln_b, wout, spread4, gather4, gather8))


class _ReduceScatter:
    def __init__(self, shapes):
        self.shapes = shapes

    def scratch_shapes(self):
        out = []
        for n, w in self.shapes:
            h, p = n // 2, n // 4
            out += [pltpu.VMEM((4, h, w), F32), pltpu.VMEM((4, h, w), F32), pltpu.VMEM((6, p, w), BF16),
                    pltpu.VMEM((6, p, w), BF16), pltpu.VMEM((2, p, w), F32), pltpu.VMEM((h, w), F32)]
        na = len(self.shapes)
        dma = pltpu.SemaphoreType.DMA
        return out + [dma((na, 4)), dma((na, 4)), dma((na, 4)), dma((na, 6)), dma((na, 6)), dma((na,)), dma((na,)),
                      dma((na,))]

    def bind(self, g_refs, r_refs, scratch):
        na = len(self.shapes)
        bufs = [scratch[6 * a:6 * a + 6] for a in range(na)]
        mine, sib, stage, land, keep, tot = (tuple(b[i] for b in bufs) for i in range(6))
        loc_sem, s1_send, s1_recv, s2_send, s2_recv, s3_send, s3_recv, st_sem = scratch[6 * na:6 * na + 8]
        x, y, c = lax.axis_index("x"), lax.axis_index("y"), lax.axis_index("c")
        me, sibling = (x, y, c), (x, y, 1 - c)
        xn, yn, dg = (1 - x, y), (x, 1 - y), (1 - x, 1 - y)
        idx = lambda chip: 2 * chip[0] + chip[1]
        my_chip = idx((x, y))
        order = [idx(xn), idx(dg), idx(yn), my_chip]

        def rows(a, k, half):
            n = self.shapes[a][0]
            return pl.ds(pl.multiple_of(k * n + half * (n // 2), 8), n // 2)

        def piece(a, q):
            p = self.shapes[a][0] // 4
            return slice(q * p, (q + 1) * p)

        def load(a, k):
            return pltpu.make_async_copy(g_refs[a].at[rows(a, k, c), :], mine[a].at[k], loc_sem.at[a, k])

        def s1(a, k, half):
            return pltpu.make_async_remote_copy(
                src_ref=g_refs[a].at[rows(a, k, half), :], dst_ref=sib[a].at[k],
                send_sem=s1_send.at[a, k], recv_sem=s1_recv.at[a, k], device_id=sibling, device_id_type=MESH)

        def s2(a, i, to):
            return pltpu.make_async_remote_copy(
                src_ref=stage[a].at[i], dst_ref=land[a].at[i], send_sem=s2_send.at[a, i], recv_sem=s2_recv.at[a, i],
                device_id=to, device_id_type=MESH)

        via = {0: xn, 1: xn, 2: yn, 3: yn, 4: yn, 5: xn}

        def s3(a, half, to):
            return pltpu.make_async_remote_copy(
                src_ref=tot[a], dst_ref=r_refs[a].at[rows(a, 0, half), :], send_sem=s3_send.at[a],
                recv_sem=s3_recv.at[a], device_id=to, device_id_type=MESH)

        def store(a):
            return pltpu.make_async_copy(tot[a], r_refs[a].at[rows(a, 0, c), :], st_sem.at[a])

        def start():
            for k in order:
                for a in range(na):
                    load(a, k).start()
                    s1(a, k, 1 - c).start()

        def chip_sum(a, k):
            load(a, k).wait()
            s1(a, k, c).wait_recv()
            return mine[a][k] + sib[a][k]

        def exchange():
            for a in range(na):
                P, Q = piece(a, 0), piece(a, 1)
                s_xn = chip_sum(a, idx(xn))
                stage[a][0] = s_xn[P].astype(BF16)
                keep[a][1] = s_xn[Q]
                s_dg = chip_sum(a, idx(dg))
                stage[a][1] = s_dg[P].astype(BF16)
                s2(a, 0, (*xn, c)).start()
                s2(a, 1, (*xn, c)).start()
                stage[a][3] = s_dg[Q].astype(BF16)
                s_yn = chip_sum(a, idx(yn))
                stage[a][2] = s_yn[Q].astype(BF16)
                keep[a][0] = s_yn[P]
                s2(a, 2, (*yn, c)).start()
                s2(a, 3, (*yn, c)).start()
                tot[a][...] = chip_sum(a, my_chip)

        def relay():
            for a in range(na):
                P, Q = piece(a, 0), piece(a, 1)
                s2(a, 1, me).wait_recv()
                stage[a][4] = (keep[a][0] + land[a][1].astype(F32)).astype(BF16)
                s2(a, 4, (*yn, c)).start()
                s2(a, 3, me).wait_recv()
                stage[a][5] = (keep[a][1] + land[a][3].astype(F32)).astype(BF16)
                s2(a, 5, (*xn, c)).start()
                s2(a, 0, me).wait_recv()
                tot[a][P, :] += land[a][0].astype(F32)
                s2(a, 2, me).wait_recv()
                tot[a][Q, :] += land[a][2].astype(F32)

        def finish():
            for a in range(na):
                P, Q = piece(a, 0), piece(a, 1)
                s2(a, 4, me).wait_recv()
                tot[a][P, :] += land[a][4].astype(F32)
                s2(a, 5, me).wait_recv()
                tot[a][Q, :] += land[a][5].astype(F32)
                s3(a, c, sibling).start()
                store(a).start()

        def drain():
            for a in range(na):
                s3(a, 1 - c, me).wait_recv()
                store(a).wait()
            for a in range(na):
                for k in order:
                    s1(a, k, 1 - c).wait_send()
                for i in range(6):
                    s2(a, i, (*via[i], c)).wait_send()
                s3(a, c, sibling).wait_send()

        return start, exchange, relay, finish, drain

    def part(self, grads, steps):
        def body(*refs):
            na = len(self.shapes)
            i = pl.program_id(0)
            for step, phase in zip(steps, self.bind(refs[:na], refs[na:2 * na], refs[2 * na:])):
                pl.when(i == step)(phase)

        hbm = pl.BlockSpec(memory_space=pl.ANY)
        return _Part(body, list(grads), [hbm] * len(grads), [hbm] * len(grads),
                     [_sds((n, w), F32) for n, w in self.shapes], self.scratch_shapes())


def _dh_dx(dqa, dka, dva, dqn, dkn, dvn, dq4, dk4, dv4, dq16, dk16, dv16, dqc, dz, du, xb, cos, sa, sb, winT):
    tm = 512
    spt = SEQ // tm

    def body(dqa_ref, dka_ref, dva_ref, dqn_ref, dkn_ref, dvn_ref, dq4_ref, dk4_ref, dv4_ref,
             dq16_ref, dk16_ref, dv16_ref, dqc_ref, dz_ref, du_ref, xb_ref, cos_ref, sa_ref, sb_ref, w_ref,
             gx_ref, db_ref, gin_ref, dh_ref, scr):
        i = pl.program_id(0)

        @pl.when(i == 0)
        def _():
            db_ref[...] = jnp.zeros_like(db_ref)
            gin_ref[...] = jnp.zeros_like(gin_ref)

        cos_t, sa_t, sb_t = cos_ref[...], sa_ref[...], sb_ref[...]

        def rope_t(t):
            return _rope(t, cos_t, sa_t, sb_t, -1)

        def put(r0, val):
            n = val.shape[1]
            dh_ref[:, r0:r0 + n] = val.astype(BF16)
            db_ref[:, r0:r0 + n] += jnp.sum(val, axis=0, keepdims=True)

        put(O_QA, rope_t(dqa_ref[...].astype(F32)) * QK_SCALE)
        put(O_KA, rope_t(dka_ref[...].astype(F32)))
        put(O_VA, dva_ref[...].astype(F32))
        put(O_QC, dqc_ref[...].astype(F32) * QK_SCALE)
        put(O_Z, dz_ref[...].astype(F32))
        for k, (n_ref, r4, r16) in enumerate(((dqn_ref, dq4_ref, dq16_ref), (dkn_ref, dk4_ref, dk16_ref),
                                               (dvn_ref, dv4_ref, dv16_ref))):
            for j in range(2):
                sl = slice(128 * j, 128 * (j + 1))
                a, q = 2 * k + j, tm // 4
                scr[a] = n_ref[:, sl].astype(F32)
                for res in range(16):
                    scr[6 + a, pl.ds((res % 4) * q + res // 4, tm // 16, stride=4), :] = r16[0, res, :, sl].astype(F32)
                for res in range(4):
                    scr[a, pl.ds(res, q, stride=4), :] += (scr[6 + a, res * q:(res + 1) * q, :]
                                                           + r4[0, res, :, sl].astype(F32))
        cat = lambda a: jnp.concatenate([scr[a], scr[a + 1]], axis=1)
        put(O_QB, rope_t(cat(0)) * QK_SCALE)
        put(O_KB, rope_t(cat(2)))
        put(O_VB, cat(4))
        gx_ref[...] = _dot(dh_ref[...], w_ref[...], NN) + ALPHA * du_ref[...].astype(F32)
        gin_ref[...] += _dot(dh_ref[...], xb_ref[...], TN)

    tok = lambda w: pl.BlockSpec((tm, w), lambda i: (i, 0))
    tab = pl.BlockSpec((tm, 128), lambda i: (i % spt, 0))
    p4 = pl.BlockSpec((1, 4, tm // 4, W_B), lambda i: (i // spt, 0, i % spt, 0))
    p16 = pl.BlockSpec((1, 16, tm // 16, W_B), lambda i: (i // spt, 0, i % spt, 0))
    once = lambda shape: pl.BlockSpec(shape, lambda i: (0, 0), pipeline_mode=pl.Buffered(1))
    return pl.pallas_call(
        body, name="dh_dx", grid=(T // tm,),
        in_specs=[tok(W_A), tok(W_KV_A), tok(W_KV_A), tok(W_B), tok(W_B), tok(W_B), p4, p4, p4, p16, p16, p16,
                  tok(W_C), tok(D_MIX), tok(D_MODEL), tok(D_MODEL), tab, tab, tab, once((D_IN, D_MODEL))],
        out_specs=(tok(D_MODEL), _full((1, D_IN)), once((D_IN, D_MODEL))),
        out_shape=(_sds((T, D_MODEL), F32), _sds((1, D_IN), F32), _sds((D_IN, D_MODEL), F32)),
        scratch_shapes=[pltpu.VMEM((tm, D_IN), BF16), pltpu.VMEM((12, tm, 128), F32)],
        compiler_params=_cp(("arbitrary",), vmem_mb=56),
    )(*_pin(dqa, dka, dva, dqn, dkn, dvn, dq4, dk4, dv4, dq16, dk16, dv16, dqc, dz, du, xb, cos, sa, sb, winT))


def _reduce_grads(g_in, acc, dbin, dsink):
    rs = _ReduceScatter([(SH_IN, D_MODEL)])

    def body(g_ref, acc_ref, dbin_ref, dsink_ref, r_ref, sv_ref, sv_mine, sv_all, sv_send, sv_recv, *rs_scratch):
        x, y, c = lax.axis_index("x"), lax.axis_index("y"), lax.axis_index("c")
        chips = [(1 - x, y), (x, 1 - y), (1 - x, 1 - y)]
        start, exchange, relay, finish, drain = rs.bind((g_ref,), (r_ref,), rs_scratch)
        start()

        sv_mine[...] = jnp.zeros_like(sv_mine)
        sv_mine[0:4, :] = acc_ref[0:4, :]
        for k, c0 in enumerate(range(0, D_IN, SV_W)):
            n = min(SV_W, D_IN - c0)
            sv_mine[SV_DB + k:SV_DB + k + 1, 0:n] = dbin_ref[:, c0:c0 + n]
        sv_mine[SV_SINK:SV_SINK + 1, 0:128] = dsink_ref[...]
        my_dev = 4 * x + 2 * y + c
        others = [(x, y, 1 - c)] + [(*chip, cc) for chip in chips for cc in (c, 1 - c)]

        def sv_copy(j, to):
            return pltpu.make_async_remote_copy(
                src_ref=sv_mine, dst_ref=sv_all.at[my_dev], send_sem=sv_send.at[j], recv_sem=sv_recv.at[j],
                device_id=to, device_id_type=MESH)

        sv_sends = [sv_copy(j, to) for j, to in enumerate(others)]
        for cp in sv_sends:
            cp.start()
        exchange()
        relay()
        finish()
        sv_all[my_dev] = sv_mine[...]
        for j in range(7):
            sv_copy(j, (x, y, c)).wait_recv()
        tot = sv_all[0]
        for d in range(1, 8):
            tot = tot + sv_all[d]
        sv_ref[...] = tot
        drain()
        for cp in sv_sends:
            cp.wait_send()

    vm = pl.BlockSpec(memory_space=pltpu.VMEM)
    hbm = pl.BlockSpec(memory_space=pl.ANY)
    return pl.pallas_call(
        body, name="reduce_grads",
        out_shape=(_sds((SH_IN, D_MODEL), F32), _vm_sds((8, SV_W), F32)),
        in_specs=[hbm, vm, vm, vm], out_specs=(hbm, vm),
        scratch_shapes=[pltpu.VMEM((8, SV_W), F32), pltpu.VMEM((8, 8, SV_W), F32),
                        pltpu.SemaphoreType.DMA((7,)), pltpu.SemaphoreType.DMA((7,))] + rs.scratch_shapes(),
        compiler_params=_cp(vmem_mb=40),
    )(pltpu.with_memory_space_constraint(g_in, pltpu.HBM), acc, dbin, dsink)


def _adamw_update(w, g, m, v):
    nm = ADAM_B1 * m + (1.0 - ADAM_B1) * g
    nv = ADAM_B2 * v + (1.0 - ADAM_B2) * (g * g)
    m_hat = nm / (1.0 - ADAM_B1 ** ADAM_STEP)
    v_hat = nv / (1.0 - ADAM_B2 ** ADAM_STEP)
    return -ADAM_LR * (m_hat / (jnp.sqrt(v_hat) + ADAM_EPS) + ADAM_WD * w), nm, nv


SMALL = ((SV_DB, D_IN, 1.0), (SV_SINK, 8, -1.0), (1, D_MIX, 1.0), (2, D_MODEL, 1.0), (3, D_MODEL, 1.0))


def _adamw_all(items, sv, ws, ms, vs, n_chunks=8):
    nb, ns = 4 * len(items), len(SMALL)
    n_out = nb + 1 + 4 * ns

    def body(*refs):
        ins, sv_ref, small_in = refs[:nb], refs[nb], refs[nb + 1:nb + 1 + 3 * ns]
        outs, scratch = refs[nb + 1 + 3 * ns:nb + 1 + 3 * ns + n_out], refs[nb + 1 + 3 * ns + n_out:]
        big_out, loss_ref, small_out = outs[:nb], outs[nb], outs[nb + 1:]
        in_buf, out_buf, load_sem, store_sem = scratch[:nb], scratch[nb:2 * nb], scratch[2 * nb], scratch[2 * nb + 1]
        small_buf, small_sem = scratch[2 * nb + 2:-1], scratch[-1]
        small_loads = [pltpu.make_async_copy(src, buf, small_sem.at[k])
                       for k, (src, buf) in enumerate(zip((sv_ref, *small_in), small_buf))]
        for cp in small_loads:
            cp.start()

        def rows(p, c):
            n = items[p][0].shape[0] // n_chunks
            return pl.ds(c * n, n)

        def load(a, c):
            r = rows(a // 4, c)
            return pltpu.make_async_copy(ins[a].at[r, :], in_buf[a].at[r, :], load_sem.at[a, c])

        def store(a, c):
            r = rows(a // 4, c)
            src = in_buf[a + 1] if a % 4 == 0 else out_buf[a]
            return pltpu.make_async_copy(src.at[r, :], big_out[a].at[r, :], store_sem.at[a, c])

        order = [(p, c) for c in range(n_chunks) for p in range(len(items))]
        for p, c in order:
            for k in range(4):
                load(4 * p + k, c).start()

        for p, c in order:
            for k in range(4):
                load(4 * p + k, c).wait()
            r = rows(p, c)
            w_buf, g_buf, m_buf, v_buf = in_buf[4 * p:4 * p + 4]
            out_buf[4 * p + 1][r, :], out_buf[4 * p + 2][r, :], out_buf[4 * p + 3][r, :] = _adamw_update(
                w_buf[r, :], g_buf[r, :], m_buf[r, :], v_buf[r, :])
            for k in range(4):
                store(4 * p + k, c).start()

        for cp in small_loads:
            cp.wait()
        sv_v, small_v = small_buf[0], small_buf[1:]
        loss_ref[...] = jnp.sum(sv_v[0:1, 0:D_MODEL], axis=1, keepdims=True)
        for p, (row, width, sign) in enumerate(SMALL):
            gv = sign * jnp.concatenate([sv_v[row + k:row + k + 1, 0:min(SV_W, width - c0)]
                                         for k, c0 in enumerate(range(0, width, SV_W))], axis=1)
            small_out[4 * p][...] = gv
            small_out[4 * p + 1][...], small_out[4 * p + 2][...], small_out[4 * p + 3][...] = _adamw_update(
                small_v[p][...], gv, small_v[ns + p][...], small_v[2 * ns + p][...])

        for p, c in order:
            for k in range(4):
                store(4 * p + k, c).wait()

    shapes, args, bufs = [], [], []
    for w, g, m, v in items:
        assert w.shape[0] % (8 * n_chunks) == 0
        shapes += [_sds(w.shape, F32)] * 4
        bufs += [pltpu.VMEM(w.shape, F32)] * 4
        args += [w, g, m, v]
    small_args = [*ws, *ms, *vs]
    whole = lambda a: _full(a.shape)
    hbm = pl.BlockSpec(memory_space=pl.ANY)
    res = pl.pallas_call(
        body, name="adamw", grid=(1,),
        in_specs=[hbm] * (nb + 1 + 3 * ns),
        out_specs=tuple([hbm] * nb + [_full((1, 1))] + [whole(w) for w in ws for _ in range(4)]),
        out_shape=tuple(shapes + [_sds((1, 1), F32)] + [_sds(w.shape, F32) for w in ws for _ in range(4)]),
        scratch_shapes=(bufs + bufs + [pltpu.SemaphoreType.DMA((nb, n_chunks))] * 2
                        + [pltpu.VMEM(a.shape, F32) for a in (sv, *small_args)]
                        + [pltpu.SemaphoreType.DMA((1 + 3 * ns,))]),
        compiler_params=_cp(("arbitrary",), vmem_mb=52),
    )(*_pin(*args, sv, *small_args))
    big = [tuple(res[4 * p:4 * p + 4]) for p in range(len(items))]
    return big, res[nb], [tuple(res[nb + 1 + 4 * p:nb + 5 + 4 * p]) for p in range(ns)]


def _rope_tables():
    pos = np.arange(SEQ, dtype=np.float32)
    inv = (np.float32(ROPE_THETA) ** (-np.arange(0, 64, 2, dtype=np.float32) / np.float32(64))).astype(np.float32)
    ang = np.tile(pos[:, None] * inv[None, :], (1, 4))
    cos, sin = np.cos(ang).astype(np.float32), np.sin(ang).astype(np.float32)
    low = (np.arange(128) % 64) < 32
    zero = np.float32(0.0)
    return jnp.asarray(cos), jnp.asarray(np.where(low, -sin, zero)), jnp.asarray(np.where(low, zero, sin))


def _local_step(x2, mem2, tgt2, winT, wout, wmem, b_in, sinks, g_branch, ln_gain, ln_bias):
    cos, sa, sb = _rope_tables()
    sinkv = jnp.pad(sinks, ((0, 0), (0, 120)))
    head_of_lane = np.arange(512)[:, None] // 64
    gather8 = jnp.asarray(head_of_lane == np.arange(128)[None, :], BF16)
    gather4 = jnp.asarray(head_of_lane[:W_B] == np.arange(128)[None, :], BF16)
    spread4 = jnp.asarray((head_of_lane[:W_B] == np.arange(128)[None, :]).T, BF16)

    xb, qa, ka, va, bn, b4, b16, qc, z, wout, wmem = _in_proj(x2, winT, b_in, cos, sa, sb, wout, wmem)
    memb, mkv = _mem_kv(mem2, wmem)
    b4f, b16f = b4.reshape(T, 768), b16.reshape(T, 768)

    swa = dict(kind="band", nb=SEQ // BLK, max_dist=BLK - 1, gqa=True)
    dil = (dict(kind="band", nb=SEQ // BLK), dict(kind="band", nb=SEQ // 4 // BLK), dict(kind="band", nb=1))
    (oa, lse_a), (o1, l1), (o4, l4), (o16, l16), (oc, lse_c) = _run_parts("attn_fwd", [
        _attn_fwd(qa, 0, W_A, ka, 0, va, 0, W_KV_A, sinks=sinks, **swa),
        _attn_fwd(bn, 0, W_B, bn, 1, bn, 2, W_B, **dil[0]),
        _attn_fwd(b4f, 0, W_B, b4f, 1, b4f, 2, W_B, **dil[1]),
        _attn_fwd(b16f, 0, W_B, b16f, 1, b16f, 2, W_B, **dil[2]),
        _attn_fwd(qc, 0, W_C, mkv, 0, mkv, 1, W_C, kind="mem")], "parallel", 48)

    s4 = lambda w: (B_LOC, 4, SEQ // 4, w)
    s16 = lambda w: (B_LOC, 16, SEQ // 16, w)
    (du, dz, doa, dla, dobn, lsen, dlbn, dob4, lse4, dlb4, dob16, lse16, dlb16, doc, dlc, acc, g_out) = _middle(
        oa, o1, l1, o4.reshape(s4(W_B)), l4.reshape(s4(128)), o16.reshape(s16(W_B)), l16.reshape(s16(128)), oc, z,
        x2, tgt2, g_branch, ln_gain, ln_bias, wout, spread4, gather4, gather8)

    flat = lambda a: a.reshape(T, a.shape[-1])
    (dqa, dka, dva, dsink), (dqc, g_mem) = _run_parts("attn_bwd_a", [
        _attn_bwd(qa, 0, W_A, ka, 0, va, 0, W_KV_A, doa, lse_a, dla, sinkv=sinkv, **swa),
        _attn_bwd(qc, 0, W_C, mkv, 0, mkv, 1, W_C, doc, lse_c, dlc, kind="mem", mem_in=memb)], "arbitrary", 48)
    last = T // QR - 1
    (r_out, r_mem), (dqn, dkn, dvn), (dq4, dk4, dv4), (dq16, dk16, dv16) = _run_parts("attn_bwd_b", [
        _ReduceScatter([(SH_OUT, D_MODEL), (SH_MEM, 2 * W_C)]).part((g_out, g_mem), (0, 1, 2, last, last)),
        _attn_bwd(bn, 0, W_B, bn, 1, bn, 2, W_B, dobn, lsen, dlbn, **dil[0]),
        _attn_bwd(b4f, 0, W_B, b4f, 1, b4f, 2, W_B, flat(dob4), flat(lse4), flat(dlb4), **dil[1]),
        _attn_bwd(b16f, 0, W_B, b16f, 1, b16f, 2, W_B, flat(dob16), flat(lse16), flat(dlb16), **dil[2])],
        "arbitrary", 62)

    r4 = lambda a: a.reshape(s4(W_B))
    r16 = lambda a: a.reshape(s16(W_B))
    gx, dbin, g_in = _dh_dx(dqa, dka, dva, dqn, dkn, dvn, r4(dq4), r4(dk4), r4(dv4), r16(dq16), r16(dk16),
                            r16(dv16), dqc, dz, du, xb, cos, sa, sb, winT)
    return gx, g_in, r_out, r_mem, acc, dbin, dsink


def kernel(x, mem, w_in, b_in, w_mem, attn_sinks, g_branch, w_out, ln_gain, ln_bias, loss_target, m_w_in, m_b_in, m_w_mem, m_attn_sinks, m_g_branch, m_w_out, m_ln_gain, m_ln_bias, v_w_in, v_b_in, v_w_mem, v_attn_sinks, v_g_branch, v_w_out, v_ln_gain, v_ln_bias):
    winT, wout, wmem = _gather_weights(w_in[0].T, w_out[0], w_mem[0])
    gx, g_in, r_out, r_mem, acc, dbin, dsink = _local_step(
        x.reshape(T, D_MODEL), mem.reshape(B_LOC * MEM_LEN, D_MODEL), loss_target.reshape(T, D_MODEL),
        winT, wout, wmem, b_in, attn_sinks, g_branch, ln_gain, ln_bias)
    r_in, sv = _reduce_grads(g_in, acc, dbin, dsink)

    small = ["b_in", "attn_sinks", "g_branch", "ln_gain", "ln_bias"]
    big, loss, steps = _adamw_all(
        [(w_in[0].T, r_in, m_w_in[0].T, v_w_in[0].T), (w_out[0], r_out, m_w_out[0], v_w_out[0]),
         (w_mem[0], r_mem, m_w_mem[0], v_w_mem[0])],
        sv, [b_in, attn_sinks, g_branch, ln_gain, ln_bias], [m_b_in, m_attn_sinks, m_g_branch, m_ln_gain, m_ln_bias],
        [v_b_in, v_attn_sinks, v_g_branch, v_ln_gain, v_ln_bias])
    out = dict(zip(small, steps))
    out["w_in"] = tuple(a.T[None] for a in big[0])
    out["w_out"], out["w_mem"] = (tuple(a[None] for a in st) for st in big[1:])
    names = ["w_in", "b_in", "w_mem", "attn_sinks", "g_branch", "w_out", "ln_gain", "ln_bias"]
    return (loss.reshape(()), gx.reshape(B_LOC, SEQ, D_MODEL), *[out[n][k] for k in range(4) for n in names])
```

```python
import jax
import jax.numpy as jnp
import numpy as np
from jax import lax
from jax.experimental import pallas as pl
from jax.experimental.pallas import tpu as pltpu

F32, BF16 = jnp.float32, jnp.bfloat16

D_MODEL = 1024
SEQ = 2048
B_LOC = 2
T = B_LOC * SEQ
BLK = 128
MEM_LEN = 256
W_A, W_KV_A, W_B, W_C, D_MIX = 512, 128, 256, 256, 1024
D_IN = 2816
O_QA, O_KA, O_VA, O_QB, O_KB, O_VB, O_QC, O_Z = 0, 512, 640, 768, 1024, 1280, 1536, 1792
ROPE_THETA = 10000.0
LN_EPS = 1e-5
RMS_EPS = 1e-6
ALPHA = 2.0 ** 0.25
QK_SCALE = 0.125
N_CHIP = 4
SH_IN, SH_OUT, SH_MEM = D_IN // N_CHIP, D_MIX // N_CHIP, D_MODEL // N_CHIP
NEG = -1e30
ADAM_LR, ADAM_B1, ADAM_B2, ADAM_EPS, ADAM_WD, ADAM_STEP = 0.001, 0.9, 0.999, 1e-08, 0.01, 10
SV_W = 1024
SV_DB, SV_SINK = 4, 7
assert D_MODEL == D_MIX == SV_W and D_IN <= (SV_SINK - SV_DB) * SV_W
MESH = pl.DeviceIdType.MESH

NN = ((1,), (0,))
NT = ((1,), (1,))
TN = ((0,), (0,))


def _dot(a, b, dims):
    return lax.dot_general(a, b, (dims, ((), ())), preferred_element_type=F32)


def _cp(sem=None, vmem_mb=None):
    kw = {}
    if sem is not None:
        kw["dimension_semantics"] = sem
    if vmem_mb is not None:
        kw["vmem_limit_bytes"] = vmem_mb * 1024 * 1024
    return pltpu.CompilerParams(**kw)


def _sds(shape, dtype):
    return pltpu.HBM(shape, dtype)


def _vm_sds(shape, dtype):
    return jax.ShapeDtypeStruct(shape, dtype)


def _pin(*args):
    return [pltpu.with_memory_space_constraint(a, pltpu.HBM) for a in args]


def _full(shape):
    n = len(shape)
    return pl.BlockSpec(shape, lambda *_: (0,) * n)


def _shard_rows(ref, n, chip, half):
    start = pl.multiple_of((2 * chip[0] + chip[1]) * n + half * (n // 2), 16)
    return ref.at[pl.ds(start, n // 2), :]


def _gather_weights(win_sh, wout_sh, wmem_sh):
    half, piece = SH_IN // 2, SH_IN // 4
    shards = ((SH_IN, D_MODEL), (SH_OUT, D_MODEL), (SH_MEM, 2 * W_C))

    def body(a_ref, b_ref, c_ref, oa_ref, ob_ref, oc_ref, raw_a, raw_b, raw_c, own_a, own_b, own_c,
             load_sem, store_sem, ici_send, ici_recv, d2d_send, d2d_recv):
        x, y, c = lax.axis_index("x"), lax.axis_index("y"), lax.axis_index("c")
        me, sibling = (x, y, c), (x, y, 1 - c)
        xn, yn, dg = (1 - x, y), (x, 1 - y), (1 - x, 1 - y)
        srcs, raws = (a_ref, b_ref, c_ref), (raw_a, raw_b, raw_c)
        owns, outs = (own_a, own_b, own_c), (oa_ref, ob_ref, oc_ref)
        loads = [pltpu.make_async_copy(srcs[a], raws[a], load_sem.at[a]) for a in range(3)]
        for cp in loads:
            cp.start()

        def rows(chip, hf, q):
            start = pl.multiple_of((2 * chip[0] + chip[1]) * SH_IN + hf * half + q * piece, 16)
            return oa_ref.at[pl.ds(start, piece), :]

        def copy(sems, k, chip, hf, q, to, src=None):
            blk = rows(chip, hf, q)
            return pltpu.make_async_remote_copy(
                src_ref=blk if src is None else src, dst_ref=blk, send_sem=sems[0].at[k], recv_sem=sems[1].at[k],
                device_id=to, device_id_type=MESH)

        def my_piece(q):
            return own_a.at[pl.ds(pl.multiple_of(c * half + q * piece, 16), piece), :]

        ici, d2d = (ici_send, ici_recv), (d2d_send, d2d_recv)
        stores, direct = [], []
        for a, (n, _) in enumerate(shards):
            loads[a].wait()
            owns[a][...] = raws[a][...].astype(BF16)
            mine = pl.ds(pl.multiple_of((2 * x + y) * n, 16), n)
            stores.append(pltpu.make_async_copy(owns[a], outs[a].at[mine, :], store_sem.at[a]))
            stores[-1].start()
            if a == 0:
                direct = [copy(ici, 0, (x, y), c, 0, (*xn, c), my_piece(0)),
                          copy(ici, 1, (x, y), c, 1, (*xn, c), my_piece(1)),
                          copy(ici, 3, (x, y), c, 0, (*yn, c), my_piece(0)),
                          copy(ici, 4, (x, y), c, 1, (*yn, c), my_piece(1))]
                for cp in direct:
                    cp.start()
        arrivals = [(0, xn, 0), (1, xn, 1), (3, yn, 0), (4, yn, 1), (2, dg, 1), (5, dg, 0)]
        passed = []
        for k, chip, q in arrivals:
            copy(ici, k, chip, c, q, me).wait_recv()
            if k == 0:
                passed.append(copy(ici, 5, xn, c, 0, (*yn, c)))
                passed[-1].start()
            if k == 4:
                passed.append(copy(ici, 2, yn, c, 1, (*xn, c)))
                passed[-1].start()
            passed.append(copy(d2d, k, chip, c, q, sibling))
            passed[-1].start()
        for k, chip, q in arrivals:
            copy(d2d, k, chip, 1 - c, q, me).wait_recv()
        for cp in direct + passed:
            cp.wait_send()
        for cp in stores:
            cp.wait()

    hbm = pl.BlockSpec(memory_space=pl.ANY)
    return pl.pallas_call(
        body, name="gather_weights",
        out_shape=(_sds((D_IN, D_MODEL), BF16), _sds((D_MIX, D_MODEL), BF16), _sds((D_MODEL, 2 * W_C), BF16)),
        in_specs=[hbm, hbm, hbm], out_specs=(hbm, hbm, hbm),
        scratch_shapes=([pltpu.VMEM(sh, F32) for sh in shards] + [pltpu.VMEM(sh, BF16) for sh in shards]
                        + [pltpu.SemaphoreType.DMA((3,))] * 2 + [pltpu.SemaphoreType.DMA((6,))] * 4),
        compiler_params=_cp(vmem_mb=40),
    )(*_pin(win_sh, wout_sh, wmem_sh))


def _rope(t, cos, sa, sb, sign):
    w = t.shape[1]
    reps = w // 128
    c, a, b = (jnp.tile(v, (1, reps)) if reps > 1 else v for v in (cos, sa, sb))
    rot = pltpu.roll(t, w - 32, 1) * a + pltpu.roll(t, 32, 1) * b
    return t * c + rot if sign > 0 else t * c - rot


def _in_proj(x, winT, b_in, cos, sa, sb, wout_own, wmem_own):
    tm = 512
    spt = SEQ // tm
    n_steps = T // tm
    forward_step = n_steps // 2

    def body(x_ref, w_ref, b_ref, cos_ref, sa_ref, sb_ref, wo_in, wm_in,
             xb_ref, qa_ref, ka_ref, va_ref, bn_ref, b4_ref, b16_ref, qc_ref, z_ref, wo_ref, wm_ref,
             scr, ici_send, ici_recv, d2d_send, d2d_recv):
        i = pl.program_id(0)
        mx, my, mc = lax.axis_index("x"), lax.axis_index("y"), lax.axis_index("c")
        chips = [(1 - mx, my), (mx, 1 - my), (1 - mx, 1 - my)]
        full = ((wo_ref, SH_OUT), (wm_ref, SH_MEM))

        def copy(sems, a, j, chip_of_block, half, to):
            blk = _shard_rows(full[a][0], full[a][1], chip_of_block, half)
            return pltpu.make_async_remote_copy(
                src_ref=blk, dst_ref=blk, send_sem=sems[0].at[a, j], recv_sem=sems[1].at[a, j],
                device_id=to, device_id_type=MESH)

        ici, d2d = (ici_send, ici_recv), (d2d_send, d2d_recv)
        pairs = [(a, j, chip) for j, chip in enumerate(chips) for a in range(2)]

        @pl.when(i == 0)
        def _():
            for a, j, chip in pairs:
                copy(ici, a, j, (mx, my), mc, (*chip, mc)).start()

        @pl.when(i == forward_step)
        def _():
            for a, j, chip in pairs:
                copy(ici, a, j, chip, mc, (mx, my, mc)).wait_recv()
                copy(d2d, a, j, chip, mc, (mx, my, 1 - mc)).start()

        @pl.when(i == n_steps - 1)
        def _():
            for a, j, chip in pairs:
                copy(d2d, a, j, chip, 1 - mc, (mx, my, mc)).wait_recv()
            for a, j, chip in pairs:
                copy(ici, a, j, (mx, my), mc, (*chip, mc)).wait_send()
                copy(d2d, a, j, chip, mc, (mx, my, 1 - mc)).wait_send()

        xb = x_ref[...].astype(BF16)
        xb_ref[...] = xb
        cos_t, sa_t, sb_t = cos_ref[...], sa_ref[...], sb_ref[...]

        def proj(r0, n):
            return _dot(xb, w_ref[r0:r0 + n, :], NT) + b_ref[:, r0:r0 + n]

        def rope(t):
            return _rope(t, cos_t, sa_t, sb_t, +1)

        parts = (rope(proj(O_QB, W_B)) * QK_SCALE, rope(proj(O_KB, W_B)), proj(O_VB, W_B))
        for k, part in enumerate(parts):
            bn_ref[:, 256 * k:256 * (k + 1)] = part.astype(BF16)
            scr[2 * k] = part[:, :128]
            scr[2 * k + 1] = part[:, 128:]
        for j in range(6):
            lanes = slice(128 * j, 128 * (j + 1))
            for res in range(4):
                t = scr[j, pl.ds(res, tm // 4, stride=4), :]
                b4_ref[0, res, :, lanes] = t.astype(BF16)
                scr[6 + j, res * (tm // 4):(res + 1) * (tm // 4), :] = t
            for res in range(16):
                b16_ref[0, res, :, lanes] = scr[6 + j, pl.ds((res % 4) * (tm // 4) + res // 4, tm // 16, stride=4),
                                                :].astype(BF16)
        qa_ref[...] = (rope(proj(O_QA, W_A)) * QK_SCALE).astype(BF16)
        assert O_VA == O_KA + W_KV_A
        kv = proj(O_KA, 2 * W_KV_A)
        ka_ref[...] = rope(kv[:, :W_KV_A]).astype(BF16)
        va_ref[...] = kv[:, W_KV_A:].astype(BF16)
        qc_ref[...] = (proj(O_QC, W_C) * QK_SCALE).astype(BF16)
        z_ref[...] = proj(O_Z, D_MIX).astype(BF16)

    tok = lambda w: pl.BlockSpec((tm, w), lambda i: (i, 0))
    tab = pl.BlockSpec((tm, 128), lambda i: (i % spt, 0))
    hbm = pl.BlockSpec(memory_space=pl.ANY)
    return pl.pallas_call(
        body, name="in_proj", grid=(n_steps,),
        in_specs=[tok(D_MODEL), _full((D_IN, D_MODEL)), _full((1, D_IN)), tab, tab, tab, hbm, hbm],
        out_specs=(tok(D_MODEL), tok(W_A), tok(W_KV_A), tok(W_KV_A), tok(768),
                   pl.BlockSpec((1, 4, tm // 4, 768), lambda i: (i // spt, 0, i % spt, 0)),
                   pl.BlockSpec((1, 16, tm // 16, 768), lambda i: (i // spt, 0, i % spt, 0)),
                   tok(W_C), tok(D_MIX), hbm, hbm),
        out_shape=(_sds((T, D_MODEL), BF16), _sds((T, W_A), BF16), _sds((T, W_KV_A), BF16), _sds((T, W_KV_A), BF16),
                   _sds((T, 768), BF16), _sds((B_LOC, 4, SEQ // 4, 768), BF16), _sds((B_LOC, 16, SEQ // 16, 768), BF16),
                   _sds((T, W_C), BF16), _sds((T, D_MIX), BF16),
                   _sds((D_MIX, D_MODEL), BF16), _sds((D_MODEL, 2 * W_C), BF16)),
        input_output_aliases={6: 9, 7: 10},
        scratch_shapes=[pltpu.VMEM((12, tm, 128), F32)] + [pltpu.SemaphoreType.DMA((2, 3))] * 4,
        compiler_params=_cp(("arbitrary",), vmem_mb=48),
    )(*_pin(x, winT, b_in, cos, sa, sb, wout_own, wmem_own))


def _mem_kv(mem, wmem):
    def body(m_ref, w_ref, mb_ref, kv_ref):
        mb = m_ref[...].astype(BF16)
        mb_ref[...] = mb
        kv_ref[...] = _dot(mb, w_ref[...], NN).astype(BF16)

    n = B_LOC * MEM_LEN
    return pl.pallas_call(
        body, name="mem_kv",
        out_shape=(_sds((n, D_MODEL), BF16), _sds((n, 2 * W_C), BF16)),
    )(*_pin(mem, wmem))


class _Part:
    def __init__(self, body, args, in_specs, out_specs, out_shape, scratch=()):
        self.body, self.args, self.in_specs, self.out_specs, self.out_shape = body, args, in_specs, out_specs, out_shape
        self.scratch = list(scratch)


def _run_parts(name, parts, semantics, vmem_mb):
    n_in = [len(p.args) for p in parts]
    n_out = [len(p.out_shape) for p in parts]
    n_scr = [len(p.scratch) for p in parts]

    def body(*refs):
        ins, outs, scr = refs[:sum(n_in)], refs[sum(n_in):sum(n_in) + sum(n_out)], refs[sum(n_in) + sum(n_out):]
        i0 = o0 = s0 = 0
        for p, ni, no, ns in zip(parts, n_in, n_out, n_scr):
            p.body(*ins[i0:i0 + ni], *outs[o0:o0 + no], *scr[s0:s0 + ns])
            i0, o0, s0 = i0 + ni, o0 + no, s0 + ns

    res = pl.pallas_call(
        body, name=name, grid=(T // QR,),
        in_specs=[sp for p in parts for sp in p.in_specs], out_specs=tuple(sp for p in parts for sp in p.out_specs),
        out_shape=tuple(sh for p in parts for sh in p.out_shape),
        scratch_shapes=[sc for p in parts for sc in p.scratch],
        compiler_params=_cp((semantics,), vmem_mb=vmem_mb),
    )(*_pin(*[a for p in parts for a in p.args]))
    out, o0 = [], 0
    for no in n_out:
        out.append(tuple(res[o0:o0 + no]))
        o0 += no
    return out


QB = 8
QR = QB * BLK


def _lane_lo():
    return lax.broadcasted_iota(jnp.int32, (1, 128), 1) < 64


def _dup_head(k2, hk, lo):
    kf = k2.astype(F32)
    r = pltpu.roll(kf, 64, 1)
    return (jnp.where(lo, kf, r) if hk == 0 else jnp.where(lo, r, kf)).astype(BF16)


def _stack_heads(pairs, lo):
    parts = []
    for x2 in pairs:
        z = jnp.zeros_like(x2)
        parts += [jnp.where(lo, x2, z), jnp.where(lo, z, x2)]
    return jnp.concatenate(parts, axis=0)


def _prev_mode(kind, nb, j):
    if kind == "mem" or nb == 1:
        return "no"
    if nb <= QB:
        return "yes" if j % nb else "no"
    return "yes" if j else "dyn"


class _Attn:
    def __init__(self, kind, nb, max_dist, gqa, qw, kvw, qcb, kcb, vcb):
        self.kind, self.nb, self.gqa, self.qw, self.kvw = kind, nb, gqa, qw, kvw
        npairs = qw // 128
        self.groups = ([(hk, [2 * hk, 2 * hk + 1]) for hk in range(npairs // 2)] if gqa
                       else [(p, [p]) for p in range(npairs)])
        self.nh = 2 * len(self.groups[0][1])
        self.cols = 128 * self.nh
        self.reach = BLK - max_dist
        self.ext_prev = kind == "band" and nb > QB
        self.q_spec = pl.BlockSpec((QR, qw), lambda g: (g, qcb))
        self.row_spec = pl.BlockSpec((QR, qw), lambda g: (g, 0))
        self.stat_spec = pl.BlockSpec((QR, 128), lambda g: (g, 0))
        if kind == "mem":
            per = SEQ // QR
            self.kv_specs = [pl.BlockSpec((MEM_LEN, kvw), lambda g: (g // per, kcb)),
                             pl.BlockSpec((MEM_LEN, kvw), lambda g: (g // per, vcb))]
        else:
            self.kv_specs = [pl.BlockSpec((QR, kvw), lambda g: (g, kcb)), pl.BlockSpec((QR, kvw), lambda g: (g, vcb))]
            if self.ext_prev:
                self.kv_specs += [pl.BlockSpec((BLK, kvw), lambda g: (jnp.maximum(g * QB - 1, 0), kcb)),
                                  pl.BlockSpec((BLK, kvw), lambda g: (jnp.maximum(g * QB - 1, 0), vcb))]

    def masks(self):
        if self.kind == "mem":
            return None
        kj = lax.broadcasted_iota(jnp.int32, (2 * BLK, self.cols), 0)
        qi = lax.broadcasted_iota(jnp.int32, (2 * BLK, self.cols), 1) & (BLK - 1)
        both = jnp.logical_and(kj >= qi + self.reach, kj <= qi + BLK)
        return kj, qi, self.as_mask(both), self.as_mask(kj[:BLK] <= qi[:BLK])

    def as_mask(self, in_reach):
        return jnp.where(in_reach, 0.0, NEG) if self.gqa else in_reach

    def hide(self, s, mask):
        return s + mask if self.gqa else jnp.where(mask, s, NEG)

    def keys(self, j, gi, kc_ref, vc_ref, kp_ref, vp_ref, lo, kq, g, dup):
        def kv(k_ref, v_ref, r):
            if self.gqa:
                return _dup_head(k_ref[r, :], gi, lo), _dup_head(v_ref[r, :], gi, lo)
            sl = slice(128 * gi, 128 * (gi + 1))
            return k_ref[r, sl], v_ref[r, sl]

        def blocks(b0, b1):
            if not self.gqa:
                return kv(kc_ref, vc_ref, slice(BLK * b0, BLK * b1))
            for b in range(b0, b1):
                if (b, gi) not in dup:
                    dup[b, gi] = kv(kc_ref, vc_ref, slice(BLK * b, BLK * (b + 1)))
            ks, vs = zip(*(dup[b, gi] for b in range(b0, b1)))
            return jnp.concatenate(ks, axis=0), jnp.concatenate(vs, axis=0)

        if self.kind == "mem":
            key0 = pl.multiple_of((g // (SEQ // QR)) * MEM_LEN, MEM_LEN)
            return (*kv(kc_ref, vc_ref, slice(None)), None, [(0, MEM_LEN, key0)])
        kj, qi, both, cur = kq
        row0 = g * QR + BLK * j
        mode = _prev_mode(self.kind, self.nb, j)
        if mode == "no":
            return (*blocks(j, j + 1), cur, [(0, BLK, pl.multiple_of(row0, BLK))])
        if mode == "yes":
            return (*blocks(j - 1, j + 1), both, [(0, 2 * BLK, pl.multiple_of(row0 - BLK, BLK))])
        has_prev = ((g * QB) % self.nb) > 0
        hp = has_prev.astype(jnp.int32)
        mask = self.as_mask(jnp.logical_and(kj >= qi * hp + (self.reach * hp + BLK * (1 - hp)), kj <= qi + BLK))
        kp, vp = kv(kp_ref, vp_ref, slice(None))
        kc, vc = blocks(0, 1)
        return (jnp.concatenate([kp, kc], axis=0), jnp.concatenate([vp, vc], axis=0), mask,
                [(0, BLK, pl.multiple_of(jnp.maximum(row0 - BLK, 0), BLK)), (BLK, BLK, pl.multiple_of(row0, BLK))])


def _attn_fwd(q, qcb, qw, k, kcb, v, vcb, kvw, *, kind, nb=1, max_dist=BLK, gqa=False, sinks=None):
    a = _Attn(kind, nb, max_dist, gqa, qw, kvw, qcb, kcb, vcb)

    def body(*refs):
        it = iter(refs)
        q_ref, kc_ref, vc_ref = next(it), next(it), next(it)
        kp_ref, vp_ref = (next(it), next(it)) if a.ext_prev else (None, None)
        sink_ref = next(it) if sinks is not None else None
        o_ref, lse_ref = next(it), next(it)
        g = pl.program_id(0)
        lo = _lane_lo()
        top = lax.broadcasted_iota(jnp.int32, (128, 1), 0) < 64
        rid = lax.broadcasted_iota(jnp.int32, (8, 128), 0)
        kq, dup = a.masks(), {}
        stats = {}

        def scores(j, gi, pairs):
            rows = slice(BLK * j, BLK * (j + 1))
            qs = _stack_heads([q_ref[rows, 128 * p:128 * (p + 1)] for p in pairs], lo)
            kk, vv, mask, _ = a.keys(j, gi, kc_ref, vc_ref, kp_ref, vp_ref, lo, kq, g, dup)
            pieces = [slice(r0, r0 + BLK) for r0 in range(0, kk.shape[0], BLK)]
            return dict(j=j, gi=gi, pairs=pairs, rows=rows, vv=vv, mask=mask, pieces=pieces,
                        ss=[_dot(kk[r], qs, NT) for r in pieces])

        def softmax(c):
            gi, mask = c["gi"], c["mask"]
            ss = [s if mask is None else a.hide(s, mask[r]) for r, s in zip(c["pieces"], c.pop("ss"))]
            m = jnp.max(ss[0], axis=0, keepdims=True)
            for s in ss[1:]:
                m = jnp.maximum(m, jnp.max(s, axis=0, keepdims=True))
            if sink_ref is not None:
                sk = jnp.concatenate([jnp.full((1, 128), sink_ref[0, a.nh * gi + i], F32) for i in range(a.nh)], axis=1)
                m = jnp.maximum(m, sk)
            ps = [jnp.exp(s - m) for s in ss]
            l = sum(jnp.sum(p, axis=0, keepdims=True) for p in ps)
            if sink_ref is not None:
                l = l + jnp.exp(sk - m)
            c["ps"] = [p.astype(BF16) for p in ps]
            c["l"], c["lse"] = l, m + jnp.log(l)

        def outputs(c):
            j, gi, rows = c["j"], c["gi"], c["rows"]
            ot = sum(_dot(c["vv"][r], p, TN) for r, p in zip(c["pieces"], c["ps"]))
            ot = ot * pl.reciprocal(c["l"], approx=True)
            for i, p in enumerate(c["pairs"]):
                o2t = jnp.where(top, ot[:, 256 * i:256 * i + 128], ot[:, 256 * i + 128:256 * i + 256])
                o_ref[rows, 128 * p:128 * (p + 1)] = o2t.T.astype(BF16)
            stat = stats.get(j, jnp.zeros((8, 128), F32))
            for i in range(a.nh):
                stat = jnp.where(rid == a.nh * gi + i, c["lse"][:, 128 * i:128 * (i + 1)], stat)
            stats[j] = stat
            if gi == a.groups[-1][0]:
                lse_ref[rows, :] = jnp.concatenate([stats.pop(j), jnp.zeros((120, 128), F32)], axis=0).T

        chains = [(j, gi, pairs) for j in range(QB) for gi, pairs in a.groups]
        live = {}
        for t in range(len(chains) + 2):
            if t < len(chains):
                live[t] = scores(*chains[t])
            if 0 <= t - 1 < len(chains):
                softmax(live[t - 1])
            if 0 <= t - 2 < len(chains):
                outputs(live.pop(t - 2))


    args = [q, k, v] + ([k, v] if a.ext_prev else [])
    in_specs = [a.q_spec] + a.kv_specs
    if sinks is not None:
        args.append(sinks)
        in_specs.append(pl.BlockSpec(memory_space=pltpu.SMEM))
    return _Part(body, args, in_specs, [a.row_spec, a.stat_spec], [_sds((T, qw), BF16), _sds((T, 128), F32)])


def _attn_bwd(q, qcb, qw, k, kcb, v, vcb, kvw, do, lse, dl, *, kind, nb=1, max_dist=BLK, gqa=False, sinkv=None,
              mem_in=None):
    a = _Attn(kind, nb, max_dist, gqa, qw, kvw, qcb, kcb, vcb)

    def body(*refs):
        it = iter(refs)
        q_ref, kc_ref, vc_ref = next(it), next(it), next(it)
        kp_ref, vp_ref = (next(it), next(it)) if a.ext_prev else (None, None)
        do_ref, lse_ref, dl_ref = next(it), next(it), next(it)
        sinkv_ref = next(it) if sinkv is not None else None
        mem_ref = next(it) if kind == "mem" else None
        dq_ref = next(it)
        if kind == "mem":
            gmem_ref = next(it)
        else:
            dk_out, dv_out = next(it), next(it)
        dsink_ref = next(it) if sinkv is not None else None
        if kind != "mem":
            dk_ref, dv_ref, stage_k, stage_v, flush_sem = next(it), next(it), next(it), next(it), next(it)
        else:
            dkv_ref = next(it)
        g = pl.program_id(0)
        lo = _lane_lo()
        top = lax.broadcasted_iota(jnp.int32, (128, 1), 0) < 64

        @pl.when(g == 0)
        def _():
            if kind == "mem":
                dkv_ref[...] = jnp.zeros_like(dkv_ref)
            else:
                dk_ref[...] = jnp.zeros_like(dk_ref)
                dv_ref[...] = jnp.zeros_like(dv_ref)
            if dsink_ref is not None:
                dsink_ref[...] = jnp.zeros_like(dsink_ref)

        kq, dup = a.masks(), {}
        stats_t = {}

        def first_matmuls(j, gi, pairs):
            rows = slice(BLK * j, BLK * (j + 1))
            if j not in stats_t:
                stats_t[j] = (lse_ref[rows, :].T, dl_ref[rows, :].T)
            lse_t, dl_t = stats_t[j]
            heads = [a.nh * gi + i for i in range(a.nh)]
            c = dict(rows=rows, gi=gi, pairs=pairs)
            c["qs"] = _stack_heads([q_ref[rows, 128 * p:128 * (p + 1)] for p in pairs], lo)
            c["dos"] = _stack_heads([do_ref[rows, 128 * p:128 * (p + 1)] for p in pairs], lo)
            c["lse_row"] = jnp.concatenate([lse_t[h:h + 1, :] for h in heads], axis=1)
            c["dl_row"] = jnp.concatenate([dl_t[h:h + 1, :] for h in heads], axis=1)
            c["kk"], vv, c["mask"], c["dests"] = a.keys(j, gi, kc_ref, vc_ref, kp_ref, vp_ref, lo, kq, g, dup)
            c["s"] = _dot(c["kk"], c["qs"], NT)
            c["dp"] = _dot(vv, c["dos"], NT)
            return c

        def elementwise(c):
            s = c.pop("s")
            if c["mask"] is not None:
                s = a.hide(s, c["mask"])
            p = jnp.exp(s - c["lse_row"])
            c["ds"] = (p * (c.pop("dp") - c["dl_row"])).astype(BF16)
            c["p"] = p.astype(BF16)

        def last_matmuls(c):
            gi, rows = c["gi"], c["rows"]
            dqt = _dot(c["kk"], c["ds"], TN)
            ck = _dot(c["ds"], c["qs"], NN)
            cv = _dot(c["p"], c["dos"], NN)
            if gqa:
                sel = lo if gi == 0 else jnp.logical_not(lo)
                ck = jnp.where(sel, ck + pltpu.roll(ck, 64, 1), 0.0)
                cv = jnp.where(sel, cv + pltpu.roll(cv, 64, 1), 0.0)
                kcols = slice(0, 128)
            else:
                kcols = slice(128 * gi, 128 * (gi + 1))
            for r0, nr, key0 in c["dests"]:
                krows = pl.ds(key0, nr)
                if kind == "mem":
                    dkv_ref[krows, kcols] += ck[r0:r0 + nr]
                    dkv_ref[krows, slice(kvw + kcols.start, kvw + kcols.stop)] += cv[r0:r0 + nr]
                else:
                    dk_ref[krows, kcols] += ck[r0:r0 + nr]
                    dv_ref[krows, kcols] += cv[r0:r0 + nr]
            for i, p in enumerate(c["pairs"]):
                dq2t = jnp.where(top, dqt[:, 256 * i:256 * i + 128], dqt[:, 256 * i + 128:256 * i + 256])
                dq_ref[rows, 128 * p:128 * (p + 1)] = dq2t.T.astype(BF16)

        chains = [(j, gi, pairs) for j in range(QB) for gi, pairs in a.groups]
        live = {}
        for t in range(len(chains) + 2):
            if t < len(chains):
                live[t] = first_matmuls(*chains[t])
            if 0 <= t - 1 < len(chains):
                elementwise(live[t - 1])
            if 0 <= t - 2 < len(chains):
                last_matmuls(live.pop(t - 2))
        if dsink_ref is not None:
            ps = jnp.exp(sinkv_ref[...] - lse_ref[...]) * dl_ref[...]
            dsink_ref[...] += jnp.sum(ps, axis=0, keepdims=True)
        if kind == "mem":
            @pl.when(g == T // QR - 1)
            def _():
                gmem_ref[...] = _dot(mem_ref[...], dkv_ref[...].astype(BF16), TN)
        else:
            n_steps = T // QR

            def flush(step):
                rows = pl.ds(pl.multiple_of(step * QR, QR), QR)
                out = []
                for acc, stage, dst, i in ((dk_ref, stage_k, dk_out, 0), (dv_ref, stage_v, dv_out, 1)):
                    stage[...] = acc[rows, :].astype(BF16)
                    out.append(pltpu.make_async_copy(stage, dst.at[rows, :], flush_sem.at[i]))
                return out

            def flushed(step):
                rows = pl.ds(pl.multiple_of(step * QR, QR), QR)
                return [pltpu.make_async_copy(stage, dst.at[rows, :], flush_sem.at[i])
                        for stage, dst, i in ((stage_k, dk_out, 0), (stage_v, dv_out, 1))]

            @pl.when(g >= 2)
            def _():
                for cp in flushed(g - 2):
                    cp.wait()

            @pl.when(g >= 1)
            def _():
                for cp in flush(g - 1):
                    cp.start()

            @pl.when(g == n_steps - 1)
            def _():
                for cp in flushed(g - 1):
                    cp.wait()
                for cp in flush(g):
                    cp.start()
                for cp in flushed(g):
                    cp.wait()

    args = [q, k, v] + ([k, v] if a.ext_prev else []) + [do, lse, dl]
    in_specs = [a.q_spec] + a.kv_specs + [a.row_spec, a.stat_spec, a.stat_spec]
    if sinkv is not None:
        args.append(sinkv)
        in_specs.append(_full((1, 128)))
    if kind == "mem":
        args.append(mem_in)
        in_specs.append(pl.BlockSpec(mem_in.shape, lambda g: (0, 0), pipeline_mode=pl.Buffered(1)))
    out_shape = [_sds((T, qw), BF16)]
    out_specs = [a.row_spec]
    scratch = []
    if kind == "mem":
        out_shape.append(_sds((D_MODEL, 2 * kvw), F32))
        out_specs.append(pl.BlockSpec((D_MODEL, 2 * kvw), lambda g: (0, 0), pipeline_mode=pl.Buffered(1)))
        scratch = [pltpu.VMEM((B_LOC * MEM_LEN, 2 * kvw), F32)]
    else:
        out_shape += [_sds((T, kvw), BF16)] * 2
        out_specs += [pl.BlockSpec(memory_space=pl.ANY)] * 2
        scratch = [pltpu.VMEM((T, kvw), F32)] * 2 + [pltpu.VMEM((QR, kvw), BF16)] * 2 + [pltpu.SemaphoreType.DMA((2,))]
    if sinkv is not None:
        out_shape.append(_sds((1, 128), F32))
        out_specs.append(_full((1, 128)))
    return _Part(body, args, in_specs, out_specs, out_shape, scratch)


def _dot2(v, w_ref):
    hi = v.astype(BF16)
    lo = (v - hi.astype(F32)).astype(BF16)
    return _dot(hi, w_ref[...], NN) + _dot(lo, w_ref[...], NN)


def _middle(oa, o1, l1, o4, l4, o16, l16, oc, z, x, tgt, g_br, ln_g, ln_b, wout, spread4, gather4, gather8):
    tm = 512
    spt = SEQ // tm

    def body(oa_ref, o1_ref, l1_ref, o4_ref, l4_ref, o16_ref, l16_ref, oc_ref, z_ref, x_ref, t_ref,
             g_ref, lg_ref, lb_ref, w_ref, sp4_ref, ga4_ref, ga8_ref,
             du_ref, dz_ref, doa_ref, dla_ref,
             dobn_ref, lsen_ref, dlbn_ref, dob4_ref, lse4_ref, dlb4_ref, dob16_ref, lse16_ref, dlb16_ref,
             doc_ref, dlc_ref, acc_ref, gout_ref, scr):
        i = pl.program_id(0)

        @pl.when(i == 0)
        def _():
            acc_ref[...] = jnp.zeros_like(acc_ref)
            gout_ref[...] = jnp.zeros_like(gout_ref)

        q = tm // 4
        for res in range(16):
            rows = pl.ds((res % 4) * q + res // 4, tm // 16, stride=4)
            for j in range(2):
                scr[6 + j, rows, :] = o16_ref[0, res, :, 128 * j:128 * (j + 1)].astype(F32)
            scr[8, rows, :] = l16_ref[0, res]
        for res in range(4):
            rows, blk = pl.ds(res, q, stride=4), slice(res * q, (res + 1) * q)
            for j in range(2):
                scr[j, rows, :] = o4_ref[0, res, :, 128 * j:128 * (j + 1)].astype(F32)
                scr[3 + j, rows, :] = scr[6 + j, blk, :]
            scr[2, rows, :] = l4_ref[0, res]
            scr[5, rows, :] = scr[8, blk, :]
        inv_d = 1.0 / D_MODEL
        gb, lg, lb = g_ref[...], lg_ref[...], lb_ref[...]

        def rms(o):
            r = lax.rsqrt(jnp.sum(o * o, axis=1, keepdims=True) * (1.0 / o.shape[1]) + RMS_EPS)
            return o * r, r

        def rms_bwd(dn_, n_, r):
            return r * (dn_ - n_ * (jnp.sum(dn_ * n_, axis=1, keepdims=True) * (1.0 / n_.shape[1])))

        def forward(rs):
            o4v = jnp.concatenate([scr[0, rs, :], scr[1, rs, :]], axis=1)
            o16v = jnp.concatenate([scr[3, rs, :], scr[4, rs, :]], axis=1)
            l1v, l4v, l16v = l1_ref[rs, :], scr[2, rs, :], scr[5, rs, :]
            mx = jnp.maximum(jnp.maximum(l1v, l4v), l16v)
            e1, e4, e16 = jnp.exp(l1v - mx), jnp.exp(l4v - mx), jnp.exp(l16v - mx)
            ssum = e1 + e4 + e16
            inv = 1.0 / ssum
            c = dict(rs=rs, lse_b=mx + jnp.log(ssum))
            c["ob"] = (_dot2(e1 * inv, sp4_ref) * o1_ref[rs, :].astype(F32) + _dot2(e4 * inv, sp4_ref) * o4v
                       + _dot2(e16 * inv, sp4_ref) * o16v)
            c["oa"], c["oc"] = oa_ref[rs, :].astype(F32), oc_ref[rs, :].astype(F32)
            na, c["ra"] = rms(c["oa"])
            nb_, c["rb"] = rms(c["ob"])
            nc, c["rc"] = rms(c["oc"])
            c["n"] = jnp.concatenate([na, nb_, nc], axis=1)
            c["zf"] = z_ref[rs, :].astype(F32)
            c["sig"] = 1.0 / (1.0 + jnp.exp(-c["zf"]))
            c["sz"] = c["zf"] * c["sig"]
            c["yb"] = (c["n"] * gb * c["sz"]).astype(BF16)
            c["y2"] = _dot(c["yb"], w_ref[...], NN)
            return c

        def norm(c):
            rs = c["rs"]
            u = ALPHA * x_ref[rs, :] + c.pop("y2")
            mu = jnp.sum(u, axis=1, keepdims=True) * inv_d
            uc = u - mu
            rstd = lax.rsqrt(jnp.sum(uc * uc, axis=1, keepdims=True) * inv_d + LN_EPS)
            xh = uc * rstd
            diff = xh * lg + lb - t_ref[rs, :]
            acc_ref[0:1, :] += jnp.sum(diff * diff, axis=0, keepdims=True) * (0.5 * inv_d)
            dout = diff * inv_d
            acc_ref[2:3, :] += jnp.sum(dout * xh, axis=0, keepdims=True)
            acc_ref[3:4, :] += jnp.sum(dout, axis=0, keepdims=True)
            dxh = dout * lg
            du = rstd * (dxh - jnp.sum(dxh, axis=1, keepdims=True) * inv_d
                         - xh * (jnp.sum(dxh * xh, axis=1, keepdims=True) * inv_d))
            dub = du.astype(BF16)
            du_ref[rs, :] = dub
            c["dy"] = _dot(dub, w_ref[...], NT)
            gout_ref[...] += _dot(c.pop("yb"), dub, TN)

        def backward(c):
            rs, n, dy, zf, sig = c["rs"], c["n"], c["dy"], c["zf"], c["sig"]
            t1 = dy * c["sz"]
            acc_ref[1:2, :] += jnp.sum(t1 * n, axis=0, keepdims=True)
            dn = t1 * gb
            dz_ref[rs, :] = (dy * n * gb * (sig * (1.0 + zf * (1.0 - sig)))).astype(BF16)
            doa = rms_bwd(dn[:, :W_A], n[:, :W_A], c["ra"])
            dob = rms_bwd(dn[:, W_A:W_A + W_B], n[:, W_A:W_A + W_B], c["rb"])
            doc = rms_bwd(dn[:, W_A + W_B:], n[:, W_A + W_B:], c["rc"])
            doa_ref[rs, :] = doa.astype(BF16)
            dla_ref[rs, :] = _dot2(doa * c["oa"], ga8_ref)
            doc_ref[rs, :] = doc.astype(BF16)
            dlc_ref[rs, :] = _dot2(doc * c["oc"], ga4_ref)
            dobn_ref[rs, :] = dob.astype(BF16)
            lsen_ref[rs, :] = c["lse_b"]
            dlbn_ref[rs, :] = _dot2(dob * c["ob"], ga4_ref)
            scr[0, rs, :] = dob[:, :128]
            scr[1, rs, :] = dob[:, 128:]

        halves = [slice(h * (tm // 2), (h + 1) * (tm // 2)) for h in range(2)]
        live = {}
        for t in range(len(halves) + 2):
            if t < len(halves):
                live[t] = forward(halves[t])
            if 0 <= t - 1 < len(halves):
                norm(live[t - 1])
            if 0 <= t - 2 < len(halves):
                backward(live.pop(t - 2))
        for j in range(2):
            sl = slice(128 * j, 128 * (j + 1))
            for res in range(4):
                t = scr[j, pl.ds(res, q, stride=4), :]
                dob4_ref[0, res, :, sl] = t.astype(BF16)
                scr[6 + j, res * q:(res + 1) * q, :] = t
            for res in range(16):
                dob16_ref[0, res, :, sl] = scr[6 + j, pl.ds((res % 4) * q + res // 4, tm // 16, stride=4),
                                               :].astype(BF16)
        for res in range(4):
            rows = pl.ds(res, q, stride=4)
            lse4_ref[0, res] = lsen_ref[rows, :]
            dlb4_ref[0, res] = dlbn_ref[rows, :]
        for res in range(16):
            rows = pl.ds(res // 4, tm // 16, stride=4)
            lse16_ref[0, res] = lse4_ref[0, res % 4, rows, :]
            dlb16_ref[0, res] = dlb4_ref[0, res % 4, rows, :]


    tok = lambda w: pl.BlockSpec((tm, w), lambda i: (i, 0))
    p4 = lambda w: pl.BlockSpec((1, 4, tm // 4, w), lambda i: (i // spt, 0, i % spt, 0))
    p16 = lambda w: pl.BlockSpec((1, 16, tm // 16, w), lambda i: (i // spt, 0, i % spt, 0))
    s4 = lambda w, dt: _sds((B_LOC, 4, SEQ // 4, w), dt)
    s16 = lambda w, dt: _sds((B_LOC, 16, SEQ // 16, w), dt)
    row = _full((1, D_MODEL))
    return pl.pallas_call(
        body, name="middle", grid=(T // tm,),
        in_specs=[tok(W_A), tok(W_B), tok(128), p4(W_B), p4(128), p16(W_B), p16(128), tok(W_C), tok(D_MIX),
                  tok(D_MODEL), tok(D_MODEL), row, row, row, _full((D_MIX, D_MODEL)),
                  _full((128, W_B)), _full((W_B, 128)), _full((W_A, 128))],
        out_specs=(tok(D_MODEL), tok(D_MIX), tok(W_A), tok(128),
                   tok(W_B), tok(128), tok(128), p4(W_B), p4(128), p4(128), p16(W_B), p16(128), p16(128),
                   tok(W_C), tok(128), _full((8, D_MODEL)), _full((D_MIX, D_MODEL))),
        out_shape=(_sds((T, D_MODEL), BF16), _sds((T, D_MIX), BF16),
                   _sds((T, W_A), BF16), _sds((T, 128), F32),
                   _sds((T, W_B), BF16), _sds((T, 128), F32), _sds((T, 128), F32),
                   s4(W_B, BF16), s4(128, F32), s4(128, F32), s16(W_B, BF16), s16(128, F32), s16(128, F32),
                   _sds((T, W_C), BF16), _sds((T, 128), F32), _sds((8, D_MODEL), F32),
                   _sds((D_MIX, D_MODEL), F32)),
        scratch_shapes=[pltpu.VMEM((9, tm, 128), F32)],
        compiler_params=_cp(("arbitrary",), vmem_mb=56),
    )(*_pin(oa, o1, l1, o4, l4, o16, l16, oc, z, x, tgt, g_br, ln_g, ln_b, wout, spread4, gather4, gather8))


class _ReduceScatter:
    def __init__(self, shapes):
        self.shapes = shapes

    def scratch_shapes(self):
        out = []
        for n, w in self.shapes:
            h, p = n // 2, n // 4
            out += [pltpu.VMEM((4, h, w), F32), pltpu.VMEM((4, h, w), F32), pltpu.VMEM((6, p, w), BF16),
                    pltpu.VMEM((6, p, w), BF16), pltpu.VMEM((2, p, w), F32), pltpu.VMEM((h, w), F32)]
        na = len(self.shapes)
        dma = pltpu.SemaphoreType.DMA
        return out + [dma((na, 4)), dma((na, 4)), dma((na, 4)), dma((na, 6)), dma((na, 6)), dma((na,)), dma((na,)),
                      dma((na,))]

    def bind(self, g_refs, r_refs, scratch):
        na = len(self.shapes)
        bufs = [scratch[6 * a:6 * a + 6] for a in range(na)]
        mine, sib, stage, land, keep, tot = (tuple(b[i] for b in bufs) for i in range(6))
        loc_sem, s1_send, s1_recv, s2_send, s2_recv, s3_send, s3_recv, st_sem = scratch[6 * na:6 * na + 8]
        x, y, c = lax.axis_index("x"), lax.axis_index("y"), lax.axis_index("c")
        me, sibling = (x, y, c), (x, y, 1 - c)
        xn, yn, dg = (1 - x, y), (x, 1 - y), (1 - x, 1 - y)
        idx = lambda chip: 2 * chip[0] + chip[1]
        my_chip = idx((x, y))
        order = [idx(xn), idx(dg), idx(yn), my_chip]

        def rows(a, k, half):
            n = self.shapes[a][0]
            return pl.ds(pl.multiple_of(k * n + half * (n // 2), 8), n // 2)

        def piece(a, q):
            p = self.shapes[a][0] // 4
            return slice(q * p, (q + 1) * p)

        def load(a, k):
            return pltpu.make_async_copy(g_refs[a].at[rows(a, k, c), :], mine[a].at[k], loc_sem.at[a, k])

        def s1(a, k, half):
            return pltpu.make_async_remote_copy(
                src_ref=g_refs[a].at[rows(a, k, half), :], dst_ref=sib[a].at[k],
                send_sem=s1_send.at[a, k], recv_sem=s1_recv.at[a, k], device_id=sibling, device_id_type=MESH)

        def s2(a, i, to):
            return pltpu.make_async_remote_copy(
                src_ref=stage[a].at[i], dst_ref=land[a].at[i], send_sem=s2_send.at[a, i], recv_sem=s2_recv.at[a, i],
                device_id=to, device_id_type=MESH)

        via = {0: xn, 1: xn, 2: yn, 3: yn, 4: yn, 5: xn}

        def s3(a, half, to):
            return pltpu.make_async_remote_copy(
                src_ref=tot[a], dst_ref=r_refs[a].at[rows(a, 0, half), :], send_sem=s3_send.at[a],
                recv_sem=s3_recv.at[a], device_id=to, device_id_type=MESH)

        def store(a):
            return pltpu.make_async_copy(tot[a], r_refs[a].at[rows(a, 0, c), :], st_sem.at[a])

        def start():
            for k in order:
                for a in range(na):
                    load(a, k).start()
                    s1(a, k, 1 - c).start()

        def chip_sum(a, k):
            load(a, k).wait()
            s1(a, k, c).wait_recv()
            return mine[a][k] + sib[a][k]

        def exchange():
            for a in range(na):
                P, Q = piece(a, 0), piece(a, 1)
                s_xn = chip_sum(a, idx(xn))
                stage[a][0] = s_xn[P].astype(BF16)
                keep[a][1] = s_xn[Q]
                s_dg = chip_sum(a, idx(dg))
                stage[a][1] = s_dg[P].astype(BF16)
                s2(a, 0, (*xn, c)).start()
                s2(a, 1, (*xn, c)).start()
                stage[a][3] = s_dg[Q].astype(BF16)
                s_yn = chip_sum(a, idx(yn))
                stage[a][2] = s_yn[Q].astype(BF16)
                keep[a][0] = s_yn[P]
                s2(a, 2, (*yn, c)).start()
                s2(a, 3, (*yn, c)).start()
                tot[a][...] = chip_sum(a, my_chip)

        def relay():
            for a in range(na):
                P, Q = piece(a, 0), piece(a, 1)
                s2(a, 1, me).wait_recv()
                stage[a][4] = (keep[a][0] + land[a][1].astype(F32)).astype(BF16)
                s2(a, 4, (*yn, c)).start()
                s2(a, 3, me).wait_recv()
                stage[a][5] = (keep[a][1] + land[a][3].astype(F32)).astype(BF16)
                s2(a, 5, (*xn, c)).start()
                s2(a, 0, me).wait_recv()
                tot[a][P, :] += land[a][0].astype(F32)
                s2(a, 2, me).wait_recv()
                tot[a][Q, :] += land[a][2].astype(F32)

        def finish():
            for a in range(na):
                P, Q = piece(a, 0), piece(a, 1)
                s2(a, 4, me).wait_recv()
                tot[a][P, :] += land[a][4].astype(F32)
                s2(a, 5, me).wait_recv()
                tot[a][Q, :] += land[a][5].astype(F32)
                s3(a, c, sibling).start()
                store(a).start()

        def drain():
            for a in range(na):
                s3(a, 1 - c, me).wait_recv()
                store(a).wait()
            for a in range(na):
                for k in order:
                    s1(a, k, 1 - c).wait_send()
                for i in range(6):
                    s2(a, i, (*via[i], c)).wait_send()
                s3(a, c, sibling).wait_send()

        return start, exchange, relay, finish, drain

    def part(self, grads, steps):
        def body(*refs):
            na = len(self.shapes)
            i = pl.program_id(0)
            for step, phase in zip(steps, self.bind(refs[:na], refs[na:2 * na], refs[2 * na:])):
                pl.when(i == step)(phase)

        hbm = pl.BlockSpec(memory_space=pl.ANY)
        return _Part(body, list(grads), [hbm] * len(grads), [hbm] * len(grads),
                     [_sds((n, w), F32) for n, w in self.shapes], self.scratch_shapes())


def _dh_dx(dqa, dka, dva, dqn, dkn, dvn, dq4, dk4, dv4, dq16, dk16, dv16, dqc, dz, du, xb, cos, sa, sb, winT):
    tm = 512
    spt = SEQ // tm

    def body(dqa_ref, dka_ref, dva_ref, dqn_ref, dkn_ref, dvn_ref, dq4_ref, dk4_ref, dv4_ref,
             dq16_ref, dk16_ref, dv16_ref, dqc_ref, dz_ref, du_ref, xb_ref, cos_ref, sa_ref, sb_ref, w_ref,
             gx_ref, db_ref, gin_ref, dh_ref, scr):
        i = pl.program_id(0)

        @pl.when(i == 0)
        def _():
            db_ref[...] = jnp.zeros_like(db_ref)
            gin_ref[...] = jnp.zeros_like(gin_ref)

        cos_t, sa_t, sb_t = cos_ref[...], sa_ref[...], sb_ref[...]

        def rope_t(t):
            return _rope(t, cos_t, sa_t, sb_t, -1)

        def put(r0, val):
            n = val.shape[1]
            dh_ref[:, r0:r0 + n] = val.astype(BF16)
            db_ref[:, r0:r0 + n] += jnp.sum(val, axis=0, keepdims=True)

        put(O_QA, rope_t(dqa_ref[...].astype(F32)) * QK_SCALE)
        put(O_KA, rope_t(dka_ref[...].astype(F32)))
        put(O_VA, dva_ref[...].astype(F32))
        put(O_QC, dqc_ref[...].astype(F32) * QK_SCALE)
        put(O_Z, dz_ref[...].astype(F32))
        for k, (n_ref, r4, r16) in enumerate(((dqn_ref, dq4_ref, dq16_ref), (dkn_ref, dk4_ref, dk16_ref),
                                               (dvn_ref, dv4_ref, dv16_ref))):
            for j in range(2):
                sl = slice(128 * j, 128 * (j + 1))
                a, q = 2 * k + j, tm // 4
                scr[a] = n_ref[:, sl].astype(F32)
                for res in range(16):
                    scr[6 + a, pl.ds((res % 4) * q + res // 4, tm // 16, stride=4), :] = r16[0, res, :, sl].astype(F32)
                for res in range(4):
                    scr[a, pl.ds(res, q, stride=4), :] += (scr[6 + a, res * q:(res + 1) * q, :]
                                                           + r4[0, res, :, sl].astype(F32))
        cat = lambda a: jnp.concatenate([scr[a], scr[a + 1]], axis=1)
        put(O_QB, rope_t(cat(0)) * QK_SCALE)
        put(O_KB, rope_t(cat(2)))
        put(O_VB, cat(4))
        gx_ref[...] = _dot(dh_ref[...], w_ref[...], NN) + ALPHA * du_ref[...].astype(F32)
        gin_ref[...] += _dot(dh_ref[...], xb_ref[...], TN)

    tok = lambda w: pl.BlockSpec((tm, w), lambda i: (i, 0))
    tab = pl.BlockSpec((tm, 128), lambda i: (i % spt, 0))
    p4 = pl.BlockSpec((1, 4, tm // 4, W_B), lambda i: (i // spt, 0, i % spt, 0))
    p16 = pl.BlockSpec((1, 16, tm // 16, W_B), lambda i: (i // spt, 0, i % spt, 0))
    once = lambda shape: pl.BlockSpec(shape, lambda i: (0, 0), pipeline_mode=pl.Buffered(1))
    return pl.pallas_call(
        body, name="dh_dx", grid=(T // tm,),
        in_specs=[tok(W_A), tok(W_KV_A), tok(W_KV_A), tok(W_B), tok(W_B), tok(W_B), p4, p4, p4, p16, p16, p16,
                  tok(W_C), tok(D_MIX), tok(D_MODEL), tok(D_MODEL), tab, tab, tab, once((D_IN, D_MODEL))],
        out_specs=(tok(D_MODEL), _full((1, D_IN)), once((D_IN, D_MODEL))),
        out_shape=(_sds((T, D_MODEL), F32), _sds((1, D_IN), F32), _sds((D_IN, D_MODEL), F32)),
        scratch_shapes=[pltpu.VMEM((tm, D_IN), BF16), pltpu.VMEM((12, tm, 128), F32)],
        compiler_params=_cp(("arbitrary",), vmem_mb=56),
    )(*_pin(dqa, dka, dva, dqn, dkn, dvn, dq4, dk4, dv4, dq16, dk16, dv16, dqc, dz, du, xb, cos, sa, sb, winT))


def _reduce_grads(g_in, acc, dbin, dsink):
    rs = _ReduceScatter([(SH_IN, D_MODEL)])

    def body(g_ref, acc_ref, dbin_ref, dsink_ref, r_ref, sv_ref, sv_mine, sv_all, sv_send, sv_recv, *rs_scratch):
        x, y, c = lax.axis_index("x"), lax.axis_index("y"), lax.axis_index("c")
        chips = [(1 - x, y), (x, 1 - y), (1 - x, 1 - y)]
        start, exchange, relay, finish, drain = rs.bind((g_ref,), (r_ref,), rs_scratch)
        start()

        sv_mine[...] = jnp.zeros_like(sv_mine)
        sv_mine[0:4, :] = acc_ref[0:4, :]
        for k, c0 in enumerate(range(0, D_IN, SV_W)):
            n = min(SV_W, D_IN - c0)
            sv_mine[SV_DB + k:SV_DB + k + 1, 0:n] = dbin_ref[:, c0:c0 + n]
        sv_mine[SV_SINK:SV_SINK + 1, 0:128] = dsink_ref[...]
        my_dev = 4 * x + 2 * y + c
        others = [(x, y, 1 - c)] + [(*chip, cc) for chip in chips for cc in (c, 1 - c)]

        def sv_copy(j, to):
            return pltpu.make_async_remote_copy(
                src_ref=sv_mine, dst_ref=sv_all.at[my_dev], send_sem=sv_send.at[j], recv_sem=sv_recv.at[j],
                device_id=to, device_id_type=MESH)

        sv_sends = [sv_copy(j, to) for j, to in enumerate(others)]
        for cp in sv_sends:
            cp.start()
        exchange()
        relay()
        finish()
        sv_all[my_dev] = sv_mine[...]
        for j in range(7):
            sv_copy(j, (x, y, c)).wait_recv()
        tot = sv_all[0]
        for d in range(1, 8):
            tot = tot + sv_all[d]
        sv_ref[...] = tot
        drain()
        for cp in sv_sends:
            cp.wait_send()

    vm = pl.BlockSpec(memory_space=pltpu.VMEM)
    hbm = pl.BlockSpec(memory_space=pl.ANY)
    return pl.pallas_call(
        body, name="reduce_grads",
        out_shape=(_sds((SH_IN, D_MODEL), F32), _vm_sds((8, SV_W), F32)),
        in_specs=[hbm, vm, vm, vm], out_specs=(hbm, vm),
        scratch_shapes=[pltpu.VMEM((8, SV_W), F32), pltpu.VMEM((8, 8, SV_W), F32),
                        pltpu.SemaphoreType.DMA((7,)), pltpu.SemaphoreType.DMA((7,))] + rs.scratch_shapes(),
        compiler_params=_cp(vmem_mb=40),
    )(pltpu.with_memory_space_constraint(g_in, pltpu.HBM), acc, dbin, dsink)


def _adamw_update(w, g, m, v):
    nm = ADAM_B1 * m + (1.0 - ADAM_B1) * g
    nv = ADAM_B2 * v + (1.0 - ADAM_B2) * (g * g)
    m_hat = nm / (1.0 - ADAM_B1 ** ADAM_STEP)
    v_hat = nv / (1.0 - ADAM_B2 ** ADAM_STEP)
    return -ADAM_LR * (m_hat / (jnp.sqrt(v_hat) + ADAM_EPS) + ADAM_WD * w), nm, nv


SMALL = ((SV_DB, D_IN, 1.0), (SV_SINK, 8, -1.0), (1, D_MIX, 1.0), (2, D_MODEL, 1.0), (3, D_MODEL, 1.0))


def _adamw_all(items, sv, ws, ms, vs, n_chunks=2):
    nb, ns = 4 * len(items), len(SMALL)
    n_out = nb + 1 + 4 * ns

    def body(*refs):
        ins, sv_ref, small_in = refs[:nb], refs[nb], refs[nb + 1:nb + 1 + 3 * ns]
        outs, scratch = refs[nb + 1 + 3 * ns:nb + 1 + 3 * ns + n_out], refs[nb + 1 + 3 * ns + n_out:]
        big_out, loss_ref, small_out = outs[:nb], outs[nb], outs[nb + 1:]
        in_buf, out_buf, load_sem, store_sem = scratch[:nb], scratch[nb:2 * nb], scratch[2 * nb], scratch[2 * nb + 1]
        small_buf, small_sem = scratch[2 * nb + 2:-1], scratch[-1]
        small_loads = [pltpu.make_async_copy(src, buf, small_sem.at[k])
                       for k, (src, buf) in enumerate(zip((sv_ref, *small_in), small_buf))]
        for cp in small_loads:
            cp.start()

        def rows(p, c):
            n = items[p][0].shape[0] // n_chunks
            return pl.ds(c * n, n)

        def load(a, c):
            r = rows(a // 4, c)
            return pltpu.make_async_copy(ins[a].at[r, :], in_buf[a].at[r, :], load_sem.at[a, c])

        def store(a, c):
            r = rows(a // 4, c)
            src = in_buf[a + 1] if a % 4 == 0 else out_buf[a]
            return pltpu.make_async_copy(src.at[r, :], big_out[a].at[r, :], store_sem.at[a, c])

        order = [(p, c) for c in range(n_chunks) for p in range(len(items))]
        for p, c in order:
            for k in range(4):
                load(4 * p + k, c).start()

        for p, c in order:
            for k in range(4):
                load(4 * p + k, c).wait()
            r = rows(p, c)
            w_buf, g_buf, m_buf, v_buf = in_buf[4 * p:4 * p + 4]
            out_buf[4 * p + 1][r, :], out_buf[4 * p + 2][r, :], out_buf[4 * p + 3][r, :] = _adamw_update(
                w_buf[r, :], g_buf[r, :], m_buf[r, :], v_buf[r, :])
            for k in range(4):
                store(4 * p + k, c).start()

        for cp in small_loads:
            cp.wait()
        sv_v, small_v = small_buf[0], small_buf[1:]
        loss_ref[...] = jnp.sum(sv_v[0:1, 0:D_MODEL], axis=1, keepdims=True)
        for p, (row, width, sign) in enumerate(SMALL):
            gv = sign * jnp.concatenate([sv_v[row + k:row + k + 1, 0:min(SV_W, width - c0)]
                                         for k, c0 in enumerate(range(0, width, SV_W))], axis=1)
            small_out[4 * p][...] = gv
            small_out[4 * p + 1][...], small_out[4 * p + 2][...], small_out[4 * p + 3][...] = _adamw_update(
                small_v[p][...], gv, small_v[ns + p][...], small_v[2 * ns + p][...])

        for p, c in order:
            for k in range(4):
                store(4 * p + k, c).wait()

    shapes, args, bufs = [], [], []
    for w, g, m, v in items:
        assert w.shape[0] % (8 * n_chunks) == 0
        shapes += [_sds(w.shape, F32)] * 4
        bufs += [pltpu.VMEM(w.shape, F32)] * 4
        args += [w, g, m, v]
    small_args = [*ws, *ms, *vs]
    whole = lambda a: _full(a.shape)
    hbm = pl.BlockSpec(memory_space=pl.ANY)
    res = pl.pallas_call(
        body, name="adamw", grid=(1,),
        in_specs=[hbm] * (nb + 1 + 3 * ns),
        out_specs=tuple([hbm] * nb + [_full((1, 1))] + [whole(w) for w in ws for _ in range(4)]),
        out_shape=tuple(shapes + [_sds((1, 1), F32)] + [_sds(w.shape, F32) for w in ws for _ in range(4)]),
        scratch_shapes=(bufs + bufs + [pltpu.SemaphoreType.DMA((nb, n_chunks))] * 2
                        + [pltpu.VMEM(a.shape, F32) for a in (sv, *small_args)]
                        + [pltpu.SemaphoreType.DMA((1 + 3 * ns,))]),
        compiler_params=_cp(("arbitrary",), vmem_mb=52),
    )(*_pin(*args, sv, *small_args))
    big = [tuple(res[4 * p:4 * p + 4]) for p in range(len(items))]
    return big, res[nb], [tuple(res[nb + 1 + 4 * p:nb + 5 + 4 * p]) for p in range(ns)]


def _rope_tables():
    pos = np.arange(SEQ, dtype=np.float32)
    inv = (np.float32(ROPE_THETA) ** (-np.arange(0, 64, 2, dtype=np.float32) / np.float32(64))).astype(np.float32)
    ang = np.tile(pos[:, None] * inv[None, :], (1, 4))
    cos, sin = np.cos(ang).astype(np.float32), np.sin(ang).astype(np.float32)
    low = (np.arange(128) % 64) < 32
    zero = np.float32(0.0)
    return jnp.asarray(cos), jnp.asarray(np.where(low, -sin, zero)), jnp.asarray(np.where(low, zero, sin))


def _local_step(x2, mem2, tgt2, winT, wout, wmem, b_in, sinks, g_branch, ln_gain, ln_bias):
    cos, sa, sb = _rope_tables()
    sinkv = jnp.pad(sinks, ((0, 0), (0, 120)))
    head_of_lane = np.arange(512)[:, None] // 64
    gather8 = jnp.asarray(head_of_lane == np.arange(128)[None, :], BF16)
    gather4 = jnp.asarray(head_of_lane[:W_B] == np.arange(128)[None, :], BF16)
    spread4 = jnp.asarray((head_of_lane[:W_B] == np.arange(128)[None, :]).T, BF16)

    xb, qa, ka, va, bn, b4, b16, qc, z, wout, wmem = _in_proj(x2, winT, b_in, cos, sa, sb, wout, wmem)
    memb, mkv = _mem_kv(mem2, wmem)
    b4f, b16f = b4.reshape(T, 768), b16.reshape(T, 768)

    swa = dict(kind="band", nb=SEQ // BLK, max_dist=BLK - 1, gqa=True)
    dil = (dict(kind="band", nb=SEQ // BLK), dict(kind="band", nb=SEQ // 4 // BLK), dict(kind="band", nb=1))
    (oa, lse_a), (o1, l1), (o4, l4), (o16, l16), (oc, lse_c) = _run_parts("attn_fwd", [
        _attn_fwd(qa, 0, W_A, ka, 0, va, 0, W_KV_A, sinks=sinks, **swa),
        _attn_fwd(bn, 0, W_B, bn, 1, bn, 2, W_B, **dil[0]),
        _attn_fwd(b4f, 0, W_B, b4f, 1, b4f, 2, W_B, **dil[1]),
        _attn_fwd(b16f, 0, W_B, b16f, 1, b16f, 2, W_B, **dil[2]),
        _attn_fwd(qc, 0, W_C, mkv, 0, mkv, 1, W_C, kind="mem")], "parallel", 48)

    s4 = lambda w: (B_LOC, 4, SEQ // 4, w)
    s16 = lambda w: (B_LOC, 16, SEQ // 16, w)
    (du, dz, doa, dla, dobn, lsen, dlbn, dob4, lse4, dlb4, dob16, lse16, dlb16, doc, dlc, acc, g_out) = _middle(
        oa, o1, l1, o4.reshape(s4(W_B)), l4.reshape(s4(128)), o16.reshape(s16(W_B)), l16.reshape(s16(128)), oc, z,
        x2, tgt2, g_branch, ln_gain, ln_bias, wout, spread4, gather4, gather8)

    flat = lambda a: a.reshape(T, a.shape[-1])
    (dqa, dka, dva, dsink), (dqc, g_mem) = _run_parts("attn_bwd_a", [
        _attn_bwd(qa, 0, W_A, ka, 0, va, 0, W_KV_A, doa, lse_a, dla, sinkv=sinkv, **swa),
        _attn_bwd(qc, 0, W_C, mkv, 0, mkv, 1, W_C, doc, lse_c, dlc, kind="mem", mem_in=memb)], "arbitrary", 48)
    last = T // QR - 1
    (r_out, r_mem), (dqn, dkn, dvn), (dq4, dk4, dv4), (dq16, dk16, dv16) = _run_parts("attn_bwd_b", [
        _ReduceScatter([(SH_OUT, D_MODEL), (SH_MEM, 2 * W_C)]).part((g_out, g_mem), (0, 1, 2, last, last)),
        _attn_bwd(bn, 0, W_B, bn, 1, bn, 2, W_B, dobn, lsen, dlbn, **dil[0]),
        _attn_bwd(b4f, 0, W_B, b4f, 1, b4f, 2, W_B, flat(dob4), flat(lse4), flat(dlb4), **dil[1]),
        _attn_bwd(b16f, 0, W_B, b16f, 1, b16f, 2, W_B, flat(dob16), flat(lse16), flat(dlb16), **dil[2])],
        "arbitrary", 62)

    r4 = lambda a: a.reshape(s4(W_B))
    r16 = lambda a: a.reshape(s16(W_B))
    gx, dbin, g_in = _dh_dx(dqa, dka, dva, dqn, dkn, dvn, r4(dq4), r4(dk4), r4(dv4), r16(dq16), r16(dk16),
                            r16(dv16), dqc, dz, du, xb, cos, sa, sb, winT)
    return gx, g_in, r_out, r_mem, acc, dbin, dsink


def kernel(x, mem, w_in, b_in, w_mem, attn_sinks, g_branch, w_out, ln_gain, ln_bias, loss_target, m_w_in, m_b_in, m_w_mem, m_attn_sinks, m_g_branch, m_w_out, m_ln_gain, m_ln_bias, v_w_in, v_b_in, v_w_mem, v_attn_sinks, v_g_branch, v_w_out, v_ln_gain, v_ln_bias):
    winT, wout, wmem = _gather_weights(w_in[0].T, w_out[0], w_mem[0])
    gx, g_in, r_out, r_mem, acc, dbin, dsink = _local_step(
        x.reshape(T, D_MODEL), mem.reshape(B_LOC * MEM_LEN, D_MODEL), loss_target.reshape(T, D_MODEL),
        winT, wout, wmem, b_in, attn_sinks, g_branch, ln_gain, ln_bias)
    r_in, sv = _reduce_grads(g_in, acc, dbin, dsink)

    small = ["b_in", "attn_sinks", "g_branch", "ln_gain", "ln_bias"]
    big, loss, steps = _adamw_all(
        [(w_in[0].T, r_in, m_w_in[0].T, v_w_in[0].T), (w_out[0], r_out, m_w_out[0], v_w_out[0]),
         (w_mem[0], r_mem, m_w_mem[0], v_w_mem[0])],
        sv, [b_in, attn_sinks, g_branch, ln_gain, ln_bias], [m_b_in, m_attn_sinks, m_g_branch, m_ln_gain, m_ln_bias],
        [v_b_in, v_attn_sinks, v_g_branch, v_ln_gain, v_ln_bias])
    out = dict(zip(small, steps))
    out["w_in"] = tuple(a.T[None] for a in big[0])
    out["w_out"], out["w_mem"] = (tuple(a[None] for a in st) for st in big[1:])
    names = ["w_in", "b_in", "w_mem", "attn_sinks", "g_branch", "w_out", "ln_gain", "ln_bias"]
    return (loss.reshape(()), gx.reshape(B_LOC, SEQ, D_MODEL), *[out[n][k] for k in range(4) for n in names])
```

```python
import jax
import jax.numpy as jnp
import numpy as np
from jax import lax
from jax.experimental import pallas as pl
from jax.experimental.pallas import tpu as pltpu

F32, BF16 = jnp.float32, jnp.bfloat16

D_MODEL = 1024
SEQ = 2048
B_LOC = 2
T = B_LOC * SEQ
BLK = 128
MEM_LEN = 256
W_A, W_KV_A, W_B, W_C, D_MIX = 512, 128, 256, 256, 1024
D_IN = 2816
O_QA, O_KA, O_VA, O_QB, O_KB, O_VB, O_QC, O_Z = 0, 512, 640, 768, 1024, 1280, 1536, 1792
ROPE_THETA = 10000.0
LN_EPS = 1e-5
RMS_EPS = 1e-6
ALPHA = 2.0 ** 0.25
QK_SCALE = 0.125
N_CHIP = 4
SH_IN, SH_OUT, SH_MEM = D_IN // N_CHIP, D_MIX // N_CHIP, D_MODEL // N_CHIP
NEG = -1e30
ADAM_LR, ADAM_B1, ADAM_B2, ADAM_EPS, ADAM_WD, ADAM_STEP = 0.001, 0.9, 0.999, 1e-08, 0.01, 10
SV_W = 1024
SV_DB, SV_SINK = 4, 7
assert D_MODEL == D_MIX == SV_W and D_IN <= (SV_SINK - SV_DB) * SV_W
MESH = pl.DeviceIdType.MESH

NN = ((1,), (0,))
NT = ((1,), (1,))
TN = ((0,), (0,))


def _dot(a, b, dims):
    return lax.dot_general(a, b, (dims, ((), ())), preferred_element_type=F32)


def _cp(sem=None, vmem_mb=None):
    kw = {}
    if sem is not None:
        kw["dimension_semantics"] = sem
    if vmem_mb is not None:
        kw["vmem_limit_bytes"] = vmem_mb * 1024 * 1024
    return pltpu.CompilerParams(**kw)


def _sds(shape, dtype):
    return pltpu.HBM(shape, dtype)


def _vm_sds(shape, dtype):
    return jax.ShapeDtypeStruct(shape, dtype)


def _pin(*args):
    return [pltpu.with_memory_space_constraint(a, pltpu.HBM) for a in args]


def _full(shape):
    n = len(shape)
    return pl.BlockSpec(shape, lambda *_: (0,) * n)


def _shard_rows(ref, n, chip, half):
    start = pl.multiple_of((2 * chip[0] + chip[1]) * n + half * (n // 2), 16)
    return ref.at[pl.ds(start, n // 2), :]


def _gather_weights(win_sh, wout_sh, wmem_sh):
    half, piece = SH_IN // 2, SH_IN // 4
    shards = ((SH_IN, D_MODEL), (SH_OUT, D_MODEL), (SH_MEM, 2 * W_C))

    def body(a_ref, b_ref, c_ref, oa_ref, ob_ref, oc_ref, raw_a, raw_b, raw_c, own_a, own_b, own_c,
             load_sem, store_sem, ici_send, ici_recv, d2d_send, d2d_recv):
        x, y, c = lax.axis_index("x"), lax.axis_index("y"), lax.axis_index("c")
        me, sibling = (x, y, c), (x, y, 1 - c)
        xn, yn, dg = (1 - x, y), (x, 1 - y), (1 - x, 1 - y)
        srcs, raws = (a_ref, b_ref, c_ref), (raw_a, raw_b, raw_c)
        owns, outs = (own_a, own_b, own_c), (oa_ref, ob_ref, oc_ref)
        loads = [pltpu.make_async_copy(srcs[a], raws[a], load_sem.at[a]) for a in range(3)]
        for cp in loads:
            cp.start()

        def rows(chip, hf, q):
            start = pl.multiple_of((2 * chip[0] + chip[1]) * SH_IN + hf * half + q * piece, 16)
            return oa_ref.at[pl.ds(start, piece), :]

        def copy(sems, k, chip, hf, q, to, src=None):
            blk = rows(chip, hf, q)
            return pltpu.make_async_remote_copy(
                src_ref=blk if src is None else src, dst_ref=blk, send_sem=sems[0].at[k], recv_sem=sems[1].at[k],
                device_id=to, device_id_type=MESH)

        def my_piece(q):
            return own_a.at[pl.ds(pl.multiple_of(c * half + q * piece, 16), piece), :]

        ici, d2d = (ici_send, ici_recv), (d2d_send, d2d_recv)
        stores, direct = [], []
        for a, (n, _) in enumerate(shards):
            loads[a].wait()
            owns[a][...] = raws[a][...].astype(BF16)
            mine = pl.ds(pl.multiple_of((2 * x + y) * n, 16), n)
            stores.append(pltpu.make_async_copy(owns[a], outs[a].at[mine, :], store_sem.at[a]))
            stores[-1].start()
            if a == 0:
                direct = [copy(ici, 0, (x, y), c, 0, (*xn, c), my_piece(0)),
                          copy(ici, 1, (x, y), c, 1, (*xn, c), my_piece(1)),
                          copy(ici, 3, (x, y), c, 0, (*yn, c), my_piece(0)),
                          copy(ici, 4, (x, y), c, 1, (*yn, c), my_piece(1))]
                for cp in direct:
                    cp.start()
        arrivals = [(0, xn, 0), (1, xn, 1), (3, yn, 0), (4, yn, 1), (2, dg, 1), (5, dg, 0)]
        passed = []
        for k, chip, q in arrivals:
            copy(ici, k, chip, c, q, me).wait_recv()
            if k == 0:
                passed.append(copy(ici, 5, xn, c, 0, (*yn, c)))
                passed[-1].start()
            if k == 4:
                passed.append(copy(ici, 2, yn, c, 1, (*xn, c)))
                passed[-1].start()
            passed.append(copy(d2d, k, chip, c, q, sibling))
            passed[-1].start()
        for k, chip, q in arrivals:
            copy(d2d, k, chip, 1 - c, q, me).wait_recv()
        for cp in direct + passed:
            cp.wait_send()
        for cp in stores:
            cp.wait()

    hbm = pl.BlockSpec(memory_space=pl.ANY)
    return pl.pallas_call(
        body, name="gather_weights",
        out_shape=(_sds((D_IN, D_MODEL), BF16), _sds((D_MIX, D_MODEL), BF16), _sds((D_MODEL, 2 * W_C), BF16)),
        in_specs=[hbm, hbm, hbm], out_specs=(hbm, hbm, hbm),
        scratch_shapes=([pltpu.VMEM(sh, F32) for sh in shards] + [pltpu.VMEM(sh, BF16) for sh in shards]
                        + [pltpu.SemaphoreType.DMA((3,))] * 2 + [pltpu.SemaphoreType.DMA((6,))] * 4),
        compiler_params=_cp(vmem_mb=40),
    )(*_pin(win_sh, wout_sh, wmem_sh))


def _rope(t, cos, sa, sb, sign):
    w = t.shape[1]
    reps = w // 128
    c, a, b = (jnp.tile(v, (1, reps)) if reps > 1 else v for v in (cos, sa, sb))
    rot = pltpu.roll(t, w - 32, 1) * a + pltpu.roll(t, 32, 1) * b
    return t * c + rot if sign > 0 else t * c - rot


def _in_proj(x, winT, b_in, cos, sa, sb, wout_own, wmem_own):
    tm = 512
    spt = SEQ // tm
    n_steps = T // tm
    forward_step = n_steps // 2

    def body(x_ref, w_ref, b_ref, cos_ref, sa_ref, sb_ref, wo_in, wm_in,
             xb_ref, qa_ref, ka_ref, va_ref, bn_ref, b4_ref, b16_ref, qc_ref, z_ref, wo_ref, wm_ref,
             scr, ici_send, ici_recv, d2d_send, d2d_recv):
        i = pl.program_id(0)
        mx, my, mc = lax.axis_index("x"), lax.axis_index("y"), lax.axis_index("c")
        chips = [(1 - mx, my), (mx, 1 - my), (1 - mx, 1 - my)]
        full = ((wo_ref, SH_OUT), (wm_ref, SH_MEM))

        def copy(sems, a, j, chip_of_block, half, to):
            blk = _shard_rows(full[a][0], full[a][1], chip_of_block, half)
            return pltpu.make_async_remote_copy(
                src_ref=blk, dst_ref=blk, send_sem=sems[0].at[a, j], recv_sem=sems[1].at[a, j],
                device_id=to, device_id_type=MESH)

        ici, d2d = (ici_send, ici_recv), (d2d_send, d2d_recv)
        pairs = [(a, j, chip) for j, chip in enumerate(chips) for a in range(2)]

        @pl.when(i == 0)
        def _():
            for a, j, chip in pairs:
                copy(ici, a, j, (mx, my), mc, (*chip, mc)).start()

        @pl.when(i == forward_step)
        def _():
            for a, j, chip in pairs:
                copy(ici, a, j, chip, mc, (mx, my, mc)).wait_recv()
                copy(d2d, a, j, chip, mc, (mx, my, 1 - mc)).start()

        @pl.when(i == n_steps - 1)
        def _():
            for a, j, chip in pairs:
                copy(d2d, a, j, chip, 1 - mc, (mx, my, mc)).wait_recv()
            for a, j, chip in pairs:
                copy(ici, a, j, (mx, my), mc, (*chip, mc)).wait_send()
                copy(d2d, a, j, chip, mc, (mx, my, 1 - mc)).wait_send()

        xb = x_ref[...].astype(BF16)
        xb_ref[...] = xb
        cos_t, sa_t, sb_t = cos_ref[...], sa_ref[...], sb_ref[...]

        def proj(r0, n):
            return _dot(xb, w_ref[r0:r0 + n, :], NT) + b_ref[:, r0:r0 + n]

        def rope(t):
            return _rope(t, cos_t, sa_t, sb_t, +1)

        parts = (rope(proj(O_QB, W_B)) * QK_SCALE, rope(proj(O_KB, W_B)), proj(O_VB, W_B))
        for k, part in enumerate(parts):
            bn_ref[:, 256 * k:256 * (k + 1)] = part.astype(BF16)
            scr[2 * k] = part[:, :128]
            scr[2 * k + 1] = part[:, 128:]
        for j in range(6):
            lanes = slice(128 * j, 128 * (j + 1))
            for res in range(4):
                t = scr[j, pl.ds(res, tm // 4, stride=4), :]
                b4_ref[0, res, :, lanes] = t.astype(BF16)
                scr[6 + j, res * (tm // 4):(res + 1) * (tm // 4), :] = t
            for res in range(16):
                b16_ref[0, res, :, lanes] = scr[6 + j, pl.ds((res % 4) * (tm // 4) + res // 4, tm // 16, stride=4),
                                                :].astype(BF16)
        qa_ref[...] = (rope(proj(O_QA, W_A)) * QK_SCALE).astype(BF16)
        assert O_VA == O_KA + W_KV_A
        kv = proj(O_KA, 2 * W_KV_A)
        ka_ref[...] = rope(kv[:, :W_KV_A]).astype(BF16)
        va_ref[...] = kv[:, W_KV_A:].astype(BF16)
        qc_ref[...] = (proj(O_QC, W_C) * QK_SCALE).astype(BF16)
        z_ref[...] = proj(O_Z, D_MIX).astype(BF16)

    tok = lambda w: pl.BlockSpec((tm, w), lambda i: (i, 0))
    tab = pl.BlockSpec((tm, 128), lambda i: (i % spt, 0))
    hbm = pl.BlockSpec(memory_space=pl.ANY)
    return pl.pallas_call(
        body, name="in_proj", grid=(n_steps,),
        in_specs=[tok(D_MODEL), _full((D_IN, D_MODEL)), _full((1, D_IN)), tab, tab, tab, hbm, hbm],
        out_specs=(tok(D_MODEL), tok(W_A), tok(W_KV_A), tok(W_KV_A), tok(768),
                   pl.BlockSpec((1, 4, tm // 4, 768), lambda i: (i // spt, 0, i % spt, 0)),
                   pl.BlockSpec((1, 16, tm // 16, 768), lambda i: (i // spt, 0, i % spt, 0)),
                   tok(W_C), tok(D_MIX), hbm, hbm),
        out_shape=(_sds((T, D_MODEL), BF16), _sds((T, W_A), BF16), _sds((T, W_KV_A), BF16), _sds((T, W_KV_A), BF16),
                   _sds((T, 768), BF16), _sds((B_LOC, 4, SEQ // 4, 768), BF16), _sds((B_LOC, 16, SEQ // 16, 768), BF16),
                   _sds((T, W_C), BF16), _sds((T, D_MIX), BF16),
                   _sds((D_MIX, D_MODEL), BF16), _sds((D_MODEL, 2 * W_C), BF16)),
        input_output_aliases={6: 9, 7: 10},
        scratch_shapes=[pltpu.VMEM((12, tm, 128), F32)] + [pltpu.SemaphoreType.DMA((2, 3))] * 4,
        compiler_params=_cp(("arbitrary",), vmem_mb=48),
    )(*_pin(x, winT, b_in, cos, sa, sb, wout_own, wmem_own))


def _mem_kv(mem, wmem):
    def body(m_ref, w_ref, mb_ref, kv_ref):
        mb = m_ref[...].astype(BF16)
        mb_ref[...] = mb
        kv_ref[...] = _dot(mb, w_ref[...], NN).astype(BF16)

    n = B_LOC * MEM_LEN
    return pl.pallas_call(
        body, name="mem_kv",
        out_shape=(_sds((n, D_MODEL), BF16), _sds((n, 2 * W_C), BF16)),
    )(*_pin(mem, wmem))


class _Part:
    def __init__(self, body, args, in_specs, out_specs, out_shape, scratch=()):
        self.body, self.args, self.in_specs, self.out_specs, self.out_shape = body, args, in_specs, out_specs, out_shape
        self.scratch = list(scratch)


def _run_parts(name, parts, semantics, vmem_mb):
    n_in = [len(p.args) for p in parts]
    n_out = [len(p.out_shape) for p in parts]
    n_scr = [len(p.scratch) for p in parts]

    def body(*refs):
        ins, outs, scr = refs[:sum(n_in)], refs[sum(n_in):sum(n_in) + sum(n_out)], refs[sum(n_in) + sum(n_out):]
        i0 = o0 = s0 = 0
        for p, ni, no, ns in zip(parts, n_in, n_out, n_scr):
            p.body(*ins[i0:i0 + ni], *outs[o0:o0 + no], *scr[s0:s0 + ns])
            i0, o0, s0 = i0 + ni, o0 + no, s0 + ns

    res = pl.pallas_call(
        body, name=name, grid=(T // QR,),
        in_specs=[sp for p in parts for sp in p.in_specs], out_specs=tuple(sp for p in parts for sp in p.out_specs),
        out_shape=tuple(sh for p in parts for sh in p.out_shape),
        scratch_shapes=[sc for p in parts for sc in p.scratch],
        compiler_params=_cp((semantics,), vmem_mb=vmem_mb),
    )(*_pin(*[a for p in parts for a in p.args]))
    out, o0 = [], 0
    for no in n_out:
        out.append(tuple(res[o0:o0 + no]))
        o0 += no
    return out


QB = 8
QR = QB * BLK


def _lane_lo():
    return lax.broadcasted_iota(jnp.int32, (1, 128), 1) < 64


def _dup_head(k2, hk, lo):
    kf = k2.astype(F32)
    r = pltpu.roll(kf, 64, 1)
    return (jnp.where(lo, kf, r) if hk == 0 else jnp.where(lo, r, kf)).astype(BF16)


def _stack_heads(pairs, lo):
    parts = []
    for x2 in pairs:
        z = jnp.zeros_like(x2)
        parts += [jnp.where(lo, x2, z), jnp.where(lo, z, x2)]
    return jnp.concatenate(parts, axis=0)


def _prev_mode(kind, nb, j):
    if kind == "mem" or nb == 1:
        return "no"
    if nb <= QB:
        return "yes" if j % nb else "no"
    return "yes" if j else "dyn"


class _Attn:
    def __init__(self, kind, nb, max_dist, gqa, qw, kvw, qcb, kcb, vcb):
        self.kind, self.nb, self.gqa, self.qw, self.kvw = kind, nb, gqa, qw, kvw
        npairs = qw // 128
        self.groups = ([(hk, [2 * hk, 2 * hk + 1]) for hk in range(npairs // 2)] if gqa
                       else [(p, [p]) for p in range(npairs)])
        self.nh = 2 * len(self.groups[0][1])
        self.cols = 128 * self.nh
        self.reach = BLK - max_dist
        self.ext_prev = kind == "band" and nb > QB
        self.q_spec = pl.BlockSpec((QR, qw), lambda g: (g, qcb))
        self.row_spec = pl.BlockSpec((QR, qw), lambda g: (g, 0))
        self.stat_spec = pl.BlockSpec((QR, 128), lambda g: (g, 0))
        if kind == "mem":
            per = SEQ // QR
            self.kv_specs = [pl.BlockSpec((MEM_LEN, kvw), lambda g: (g // per, kcb)),
                             pl.BlockSpec((MEM_LEN, kvw), lambda g: (g // per, vcb))]
        else:
            self.kv_specs = [pl.BlockSpec((QR, kvw), lambda g: (g, kcb)), pl.BlockSpec((QR, kvw), lambda g: (g, vcb))]
            if self.ext_prev:
                self.kv_specs += [pl.BlockSpec((BLK, kvw), lambda g: (jnp.maximum(g * QB - 1, 0), kcb)),
                                  pl.BlockSpec((BLK, kvw), lambda g: (jnp.maximum(g * QB - 1, 0), vcb))]

    def masks(self):
        if self.kind == "mem":
            return None
        kj = lax.broadcasted_iota(jnp.int32, (2 * BLK, self.cols), 0)
        qi = lax.broadcasted_iota(jnp.int32, (2 * BLK, self.cols), 1) & (BLK - 1)
        both = jnp.logical_and(kj >= qi + self.reach, kj <= qi + BLK)
        return kj, qi, self.as_mask(both), self.as_mask(kj[:BLK] <= qi[:BLK])

    def as_mask(self, in_reach):
        return jnp.where(in_reach, 0.0, NEG) if self.gqa else in_reach

    def hide(self, s, mask):
        return s + mask if self.gqa else jnp.where(mask, s, NEG)

    def keys(self, j, gi, kc_ref, vc_ref, kp_ref, vp_ref, lo, kq, g, dup):
        def kv(k_ref, v_ref, r):
            if self.gqa:
                return _dup_head(k_ref[r, :], gi, lo), _dup_head(v_ref[r, :], gi, lo)
            sl = slice(128 * gi, 128 * (gi + 1))
            return k_ref[r, sl], v_ref[r, sl]

        def blocks(b0, b1):
            if not self.gqa:
                return kv(kc_ref, vc_ref, slice(BLK * b0, BLK * b1))
            for b in range(b0, b1):
                if (b, gi) not in dup:
                    dup[b, gi] = kv(kc_ref, vc_ref, slice(BLK * b, BLK * (b + 1)))
            ks, vs = zip(*(dup[b, gi] for b in range(b0, b1)))
            return jnp.concatenate(ks, axis=0), jnp.concatenate(vs, axis=0)

        if self.kind == "mem":
            key0 = pl.multiple_of((g // (SEQ // QR)) * MEM_LEN, MEM_LEN)
            return (*kv(kc_ref, vc_ref, slice(None)), None, [(0, MEM_LEN, key0)])
        kj, qi, both, cur = kq
        row0 = g * QR + BLK * j
        mode = _prev_mode(self.kind, self.nb, j)
        if mode == "no":
            return (*blocks(j, j + 1), cur, [(0, BLK, pl.multiple_of(row0, BLK))])
        if mode == "yes":
            return (*blocks(j - 1, j + 1), both, [(0, 2 * BLK, pl.multiple_of(row0 - BLK, BLK))])
        has_prev = ((g * QB) % self.nb) > 0
        hp = has_prev.astype(jnp.int32)
        mask = self.as_mask(jnp.logical_and(kj >= qi * hp + (self.reach * hp + BLK * (1 - hp)), kj <= qi + BLK))
        kp, vp = kv(kp_ref, vp_ref, slice(None))
        kc, vc = blocks(0, 1)
        return (jnp.concatenate([kp, kc], axis=0), jnp.concatenate([vp, vc], axis=0), mask,
                [(0, BLK, pl.multiple_of(jnp.maximum(row0 - BLK, 0), BLK)), (BLK, BLK, pl.multiple_of(row0, BLK))])


def _attn_fwd(q, qcb, qw, k, kcb, v, vcb, kvw, *, kind, nb=1, max_dist=BLK, gqa=False, sinks=None):
    a = _Attn(kind, nb, max_dist, gqa, qw, kvw, qcb, kcb, vcb)

    def body(*refs):
        it = iter(refs)
        q_ref, kc_ref, vc_ref = next(it), next(it), next(it)
        kp_ref, vp_ref = (next(it), next(it)) if a.ext_prev else (None, None)
        sink_ref = next(it) if sinks is not None else None
        o_ref, lse_ref = next(it), next(it)
        g = pl.program_id(0)
        lo = _lane_lo()
        top = lax.broadcasted_iota(jnp.int32, (128, 1), 0) < 64
        rid = lax.broadcasted_iota(jnp.int32, (8, 128), 0)
        kq, dup = a.masks(), {}
        stats = {}

        def scores(j, gi, pairs):
            rows = slice(BLK * j, BLK * (j + 1))
            qs = _stack_heads([q_ref[rows, 128 * p:128 * (p + 1)] for p in pairs], lo)
            kk, vv, mask, _ = a.keys(j, gi, kc_ref, vc_ref, kp_ref, vp_ref, lo, kq, g, dup)
            pieces = [slice(r0, r0 + BLK) for r0 in range(0, kk.shape[0], BLK)]
            return dict(j=j, gi=gi, pairs=pairs, rows=rows, vv=vv, mask=mask, pieces=pieces,
                        ss=[_dot(kk[r], qs, NT) for r in pieces])

        def softmax(c):
            gi, mask = c["gi"], c["mask"]
            ss = [s if mask is None else a.hide(s, mask[r]) for r, s in zip(c["pieces"], c.pop("ss"))]
            m = jnp.max(ss[0], axis=0, keepdims=True)
            for s in ss[1:]:
                m = jnp.maximum(m, jnp.max(s, axis=0, keepdims=True))
            if sink_ref is not None:
                sk = jnp.concatenate([jnp.full((1, 128), sink_ref[0, a.nh * gi + i], F32) for i in range(a.nh)], axis=1)
                m = jnp.maximum(m, sk)
            ps = [jnp.exp(s - m) for s in ss]
            l = sum(jnp.sum(p, axis=0, keepdims=True) for p in ps)
            if sink_ref is not None:
                l = l + jnp.exp(sk - m)
            c["ps"] = [p.astype(BF16) for p in ps]
            c["l"], c["lse"] = l, m + jnp.log(l)

        def outputs(c):
            j, gi, rows = c["j"], c["gi"], c["rows"]
            ot = sum(_dot(c["vv"][r], p, TN) for r, p in zip(c["pieces"], c["ps"]))
            ot = ot * pl.reciprocal(c["l"], approx=True)
            for i, p in enumerate(c["pairs"]):
                o2t = jnp.where(top, ot[:, 256 * i:256 * i + 128], ot[:, 256 * i + 128:256 * i + 256])
                o_ref[rows, 128 * p:128 * (p + 1)] = o2t.T.astype(BF16)
            stat = stats.get(j, jnp.zeros((8, 128), F32))
            for i in range(a.nh):
                stat = jnp.where(rid == a.nh * gi + i, c["lse"][:, 128 * i:128 * (i + 1)], stat)
            stats[j] = stat
            if gi == a.groups[-1][0]:
                lse_ref[rows, :] = jnp.concatenate([stats.pop(j), jnp.zeros((120, 128), F32)], axis=0).T

        chains = [(j, gi, pairs) for j in range(QB) for gi, pairs in a.groups]
        live = {}
        for t in range(len(chains) + 2):
            if t < len(chains):
                live[t] = scores(*chains[t])
            if 0 <= t - 1 < len(chains):
                softmax(live[t - 1])
            if 0 <= t - 2 < len(chains):
                outputs(live.pop(t - 2))


    args = [q, k, v] + ([k, v] if a.ext_prev else [])
    in_specs = [a.q_spec] + a.kv_specs
    if sinks is not None:
        args.append(sinks)
        in_specs.append(pl.BlockSpec(memory_space=pltpu.SMEM))
    return _Part(body, args, in_specs, [a.row_spec, a.stat_spec], [_sds((T, qw), BF16), _sds((T, 128), F32)])


def _attn_bwd(q, qcb, qw, k, kcb, v, vcb, kvw, do, lse, dl, *, kind, nb=1, max_dist=BLK, gqa=False, sinkv=None,
              mem_in=None):
    a = _Attn(kind, nb, max_dist, gqa, qw, kvw, qcb, kcb, vcb)

    def body(*refs):
        it = iter(refs)
        q_ref, kc_ref, vc_ref = next(it), next(it), next(it)
        kp_ref, vp_ref = (next(it), next(it)) if a.ext_prev else (None, None)
        do_ref, lse_ref, dl_ref = next(it), next(it), next(it)
        sinkv_ref = next(it) if sinkv is not None else None
        mem_ref = next(it) if kind == "mem" else None
        dq_ref = next(it)
        if kind == "mem":
            gmem_ref = next(it)
        else:
            dk_out, dv_out = next(it), next(it)
        dsink_ref = next(it) if sinkv is not None else None
        if kind != "mem":
            dk_ref, dv_ref, stage_k, stage_v, flush_sem = next(it), next(it), next(it), next(it), next(it)
        else:
            dkv_ref = next(it)
        g = pl.program_id(0)
        lo = _lane_lo()
        top = lax.broadcasted_iota(jnp.int32, (128, 1), 0) < 64

        @pl.when(g == 0)
        def _():
            if kind == "mem":
                dkv_ref[...] = jnp.zeros_like(dkv_ref)
            else:
                dk_ref[...] = jnp.zeros_like(dk_ref)
                dv_ref[...] = jnp.zeros_like(dv_ref)
            if dsink_ref is not None:
                dsink_ref[...] = jnp.zeros_like(dsink_ref)

        kq, dup = a.masks(), {}
        stats_t = {}

        def first_matmuls(j, gi, pairs):
            rows = slice(BLK * j, BLK * (j + 1))
            if j not in stats_t:
                stats_t[j] = (lse_ref[rows, :].T, dl_ref[rows, :].T)
            lse_t, dl_t = stats_t[j]
            heads = [a.nh * gi + i for i in range(a.nh)]
            c = dict(rows=rows, gi=gi, pairs=pairs)
            c["qs"] = _stack_heads([q_ref[rows, 128 * p:128 * (p + 1)] for p in pairs], lo)
            c["dos"] = _stack_heads([do_ref[rows, 128 * p:128 * (p + 1)] for p in pairs], lo)
            c["lse_row"] = jnp.concatenate([lse_t[h:h + 1, :] for h in heads], axis=1)
            c["dl_row"] = jnp.concatenate([dl_t[h:h + 1, :] for h in heads], axis=1)
            c["kk"], vv, c["mask"], c["dests"] = a.keys(j, gi, kc_ref, vc_ref, kp_ref, vp_ref, lo, kq, g, dup)
            c["s"] = _dot(c["kk"], c["qs"], NT)
            c["dp"] = _dot(vv, c["dos"], NT)
            return c

        def elementwise(c):
            s = c.pop("s")
            if c["mask"] is not None:
                s = a.hide(s, c["mask"])
            p = jnp.exp(s - c["lse_row"])
            c["ds"] = (p * (c.pop("dp") - c["dl_row"])).astype(BF16)
            c["p"] = p.astype(BF16)

        def last_matmuls(c):
            gi, rows = c["gi"], c["rows"]
            dqt = _dot(c["kk"], c["ds"], TN)
            ck = _dot(c["ds"], c["qs"], NN)
            cv = _dot(c["p"], c["dos"], NN)
            if gqa:
                sel = lo if gi == 0 else jnp.logical_not(lo)
                ck = jnp.where(sel, ck + pltpu.roll(ck, 64, 1), 0.0)
                cv = jnp.where(sel, cv + pltpu.roll(cv, 64, 1), 0.0)
                kcols = slice(0, 128)
            else:
                kcols = slice(128 * gi, 128 * (gi + 1))
            for r0, nr, key0 in c["dests"]:
                krows = pl.ds(key0, nr)
                if kind == "mem":
                    dkv_ref[krows, kcols] += ck[r0:r0 + nr]
                    dkv_ref[krows, slice(kvw + kcols.start, kvw + kcols.stop)] += cv[r0:r0 + nr]
                else:
                    dk_ref[krows, kcols] += ck[r0:r0 + nr]
                    dv_ref[krows, kcols] += cv[r0:r0 + nr]
            for i, p in enumerate(c["pairs"]):
                dq2t = jnp.where(top, dqt[:, 256 * i:256 * i + 128], dqt[:, 256 * i + 128:256 * i + 256])
                dq_ref[rows, 128 * p:128 * (p + 1)] = dq2t.T.astype(BF16)

        chains = [(j, gi, pairs) for j in range(QB) for gi, pairs in a.groups]
        live = {}
        for t in range(len(chains) + 2):
            if t < len(chains):
                live[t] = first_matmuls(*chains[t])
            if 0 <= t - 1 < len(chains):
                elementwise(live[t - 1])
            if 0 <= t - 2 < len(chains):
                last_matmuls(live.pop(t - 2))
        if dsink_ref is not None:
            ps = jnp.exp(sinkv_ref[...] - lse_ref[...]) * dl_ref[...]
            dsink_ref[...] += jnp.sum(ps, axis=0, keepdims=True)
        if kind == "mem":
            @pl.when(g == T // QR - 1)
            def _():
                gmem_ref[...] = _dot(mem_ref[...], dkv_ref[...].astype(BF16), TN)
        else:
            n_steps = T // QR

            def flush(step):
                rows = pl.ds(pl.multiple_of(step * QR, QR), QR)
                out = []
                for acc, stage, dst, i in ((dk_ref, stage_k, dk_out, 0), (dv_ref, stage_v, dv_out, 1)):
                    stage[...] = acc[rows, :].astype(BF16)
                    out.append(pltpu.make_async_copy(stage, dst.at[rows, :], flush_sem.at[i]))
                return out

            def flushed(step):
                rows = pl.ds(pl.multiple_of(step * QR, QR), QR)
                return [pltpu.make_async_copy(stage, dst.at[rows, :], flush_sem.at[i])
                        for stage, dst, i in ((stage_k, dk_out, 0), (stage_v, dv_out, 1))]

            @pl.when(g >= 2)
            def _():
                for cp in flushed(g - 2):
                    cp.wait()

            @pl.when(g >= 1)
            def _():
                for cp in flush(g - 1):
                    cp.start()

            @pl.when(g == n_steps - 1)
            def _():
                for cp in flushed(g - 1):
                    cp.wait()
                for cp in flush(g):
                    cp.start()
                for cp in flushed(g):
                    cp.wait()

    args = [q, k, v] + ([k, v] if a.ext_prev else []) + [do, lse, dl]
    in_specs = [a.q_spec] + a.kv_specs + [a.row_spec, a.stat_spec, a.stat_spec]
    if sinkv is not None:
        args.append(sinkv)
        in_specs.append(_full((1, 128)))
    if kind == "mem":
        args.append(mem_in)
        in_specs.append(pl.BlockSpec(mem_in.shape, lambda g: (0, 0), pipeline_mode=pl.Buffered(1)))
    out_shape = [_sds((T, qw), BF16)]
    out_specs = [a.row_spec]
    scratch = []
    if kind == "mem":
        out_shape.append(_sds((D_MODEL, 2 * kvw), F32))
        out_specs.append(pl.BlockSpec((D_MODEL, 2 * kvw), lambda g: (0, 0), pipeline_mode=pl.Buffered(1)))
        scratch = [pltpu.VMEM((B_LOC * MEM_LEN, 2 * kvw), F32)]
    else:
        out_shape += [_sds((T, kvw), BF16)] * 2
        out_specs += [pl.BlockSpec(memory_space=pl.ANY)] * 2
        scratch = [pltpu.VMEM((T, kvw), F32)] * 2 + [pltpu.VMEM((QR, kvw), BF16)] * 2 + [pltpu.SemaphoreType.DMA((2,))]
    if sinkv is not None:
        out_shape.append(_sds((1, 128), F32))
        out_specs.append(_full((1, 128)))
    return _Part(body, args, in_specs, out_specs, out_shape, scratch)


def _dot2(v, w_ref):
    hi = v.astype(BF16)
    lo = (v - hi.astype(F32)).astype(BF16)
    return _dot(hi, w_ref[...], NN) + _dot(lo, w_ref[...], NN)


def _middle(oa, o1, l1, o4, l4, o16, l16, oc, z, x, tgt, g_br, ln_g, ln_b, wout, spread4, gather4, gather8):
    tm = 512
    spt = SEQ // tm

    def body(oa_ref, o1_ref, l1_ref, o4_ref, l4_ref, o16_ref, l16_ref, oc_ref, z_ref, x_ref, t_ref,
             g_ref, lg_ref, lb_ref, w_ref, sp4_ref, ga4_ref, ga8_ref,
             du_ref, dz_ref, doa_ref, dla_ref,
             dobn_ref, lsen_ref, dlbn_ref, dob4_ref, lse4_ref, dlb4_ref, dob16_ref, lse16_ref, dlb16_ref,
             doc_ref, dlc_ref, acc_ref, gout_ref, scr):
        i = pl.program_id(0)

        @pl.when(i == 0)
        def _():
            acc_ref[...] = jnp.zeros_like(acc_ref)
            gout_ref[...] = jnp.zeros_like(gout_ref)

        q = tm // 4
        for res in range(16):
            rows = pl.ds((res % 4) * q + res // 4, tm // 16, stride=4)
            for j in range(2):
                scr[6 + j, rows, :] = o16_ref[0, res, :, 128 * j:128 * (j + 1)].astype(F32)
            scr[8, rows, :] = l16_ref[0, res]
        for res in range(4):
            rows, blk = pl.ds(res, q, stride=4), slice(res * q, (res + 1) * q)
            for j in range(2):
                scr[j, rows, :] = o4_ref[0, res, :, 128 * j:128 * (j + 1)].astype(F32)
                scr[3 + j, rows, :] = scr[6 + j, blk, :]
            scr[2, rows, :] = l4_ref[0, res]
            scr[5, rows, :] = scr[8, blk, :]
        inv_d = 1.0 / D_MODEL
        gb, lg, lb = g_ref[...], lg_ref[...], lb_ref[...]

        def rms(o):
            r = lax.rsqrt(jnp.sum(o * o, axis=1, keepdims=True) * (1.0 / o.shape[1]) + RMS_EPS)
            return o * r, r

        def rms_bwd(dn_, n_, r):
            return r * (dn_ - n_ * (jnp.sum(dn_ * n_, axis=1, keepdims=True) * (1.0 / n_.shape[1])))

        def forward(rs):
            o4v = jnp.concatenate([scr[0, rs, :], scr[1, rs, :]], axis=1)
            o16v = jnp.concatenate([scr[3, rs, :], scr[4, rs, :]], axis=1)
            l1v, l4v, l16v = l1_ref[rs, :], scr[2, rs, :], scr[5, rs, :]
            mx = jnp.maximum(jnp.maximum(l1v, l4v), l16v)
            e1, e4, e16 = jnp.exp(l1v - mx), jnp.exp(l4v - mx), jnp.exp(l16v - mx)
            ssum = e1 + e4 + e16
            inv = 1.0 / ssum
            c = dict(rs=rs, lse_b=mx + jnp.log(ssum))
            c["ob"] = (_dot2(e1 * inv, sp4_ref) * o1_ref[rs, :].astype(F32) + _dot2(e4 * inv, sp4_ref) * o4v
                       + _dot2(e16 * inv, sp4_ref) * o16v)
            c["oa"], c["oc"] = oa_ref[rs, :].astype(F32), oc_ref[rs, :].astype(F32)
            na, c["ra"] = rms(c["oa"])
            nb_, c["rb"] = rms(c["ob"])
            nc, c["rc"] = rms(c["oc"])
            c["n"] = jnp.concatenate([na, nb_, nc], axis=1)
            c["zf"] = z_ref[rs, :].astype(F32)
            c["sig"] = 1.0 / (1.0 + jnp.exp(-c["zf"]))
            c["sz"] = c["zf"] * c["sig"]
            c["yb"] = (c["n"] * gb * c["sz"]).astype(BF16)
            c["y2"] = _dot(c["yb"], w_ref[...], NN)
            return c

        def norm(c):
            rs = c["rs"]
            u = ALPHA * x_ref[rs, :] + c.pop("y2")
            mu = jnp.sum(u, axis=1, keepdims=True) * inv_d
            uc = u - mu
            rstd = lax.rsqrt(jnp.sum(uc * uc, axis=1, keepdims=True) * inv_d + LN_EPS)
            xh = uc * rstd
            diff = xh * lg + lb - t_ref[rs, :]
            acc_ref[0:1, :] += jnp.sum(diff * diff, axis=0, keepdims=True) * (0.5 * inv_d)
            dout = diff * inv_d
            acc_ref[2:3, :] += jnp.sum(dout * xh, axis=0, keepdims=True)
            acc_ref[3:4, :] += jnp.sum(dout, axis=0, keepdims=True)
            dxh = dout * lg
            du = rstd * (dxh - jnp.sum(dxh, axis=1, keepdims=True) * inv_d
                         - xh * (jnp.sum(dxh * xh, axis=1, keepdims=True) * inv_d))
            dub = du.astype(BF16)
            du_ref[rs, :] = dub
            c["dy"] = _dot(dub, w_ref[...], NT)
            gout_ref[...] += _dot(c.pop("yb"), dub, TN)

        def backward(c):
            rs, n, dy, zf, sig = c["rs"], c["n"], c["dy"], c["zf"], c["sig"]
            t1 = dy * c["sz"]
            acc_ref[1:2, :] += jnp.sum(t1 * n, axis=0, keepdims=True)
            dn = t1 * gb
            dz_ref[rs, :] = (dy * n * gb * (sig * (1.0 + zf * (1.0 - sig)))).astype(BF16)
            doa = rms_bwd(dn[:, :W_A], n[:, :W_A], c["ra"])
            dob = rms_bwd(dn[:, W_A:W_A + W_B], n[:, W_A:W_A + W_B], c["rb"])
            doc = rms_bwd(dn[:, W_A + W_B:], n[:, W_A + W_B:], c["rc"])
            doa_ref[rs, :] = doa.astype(BF16)
            dla_ref[rs, :] = _dot2(doa * c["oa"], ga8_ref)
            doc_ref[rs, :] = doc.astype(BF16)
            dlc_ref[rs, :] = _dot2(doc * c["oc"], ga4_ref)
            dobn_ref[rs, :] = dob.astype(BF16)
            lsen_ref[rs, :] = c["lse_b"]
            dlbn_ref[rs, :] = _dot2(dob * c["ob"], ga4_ref)
            scr[0, rs, :] = dob[:, :128]
            scr[1, rs, :] = dob[:, 128:]

        halves = [slice(h * (tm // 2), (h + 1) * (tm // 2)) for h in range(2)]
        live = {}
        for t in range(len(halves) + 2):
            if t < len(halves):
                live[t] = forward(halves[t])
            if 0 <= t - 1 < len(halves):
                norm(live[t - 1])
            if 0 <= t - 2 < len(halves):
                backward(live.pop(t - 2))
        for j in range(2):
            sl = slice(128 * j, 128 * (j + 1))
            for res in range(4):
                t = scr[j, pl.ds(res, q, stride=4), :]
                dob4_ref[0, res, :, sl] = t.astype(BF16)
                scr[6 + j, res * q:(res + 1) * q, :] = t
            for res in range(16):
                dob16_ref[0, res, :, sl] = scr[6 + j, pl.ds((res % 4) * q + res // 4, tm // 16, stride=4),
                                               :].astype(BF16)
        for res in range(4):
            rows = pl.ds(res, q, stride=4)
            lse4_ref[0, res] = lsen_ref[rows, :]
            dlb4_ref[0, res] = dlbn_ref[rows, :]
        for res in range(16):
            rows = pl.ds(res // 4, tm // 16, stride=4)
            lse16_ref[0, res] = lse4_ref[0, res % 4, rows, :]
            dlb16_ref[0, res] = dlb4_ref[0, res % 4, rows, :]


    tok = lambda w: pl.BlockSpec((tm, w), lambda i: (i, 0))
    p4 = lambda w: pl.BlockSpec((1, 4, tm // 4, w), lambda i: (i // spt, 0, i % spt, 0))
    p16 = lambda w: pl.BlockSpec((1, 16, tm // 16, w), lambda i: (i // spt, 0, i % spt, 0))
    s4 = lambda w, dt: _sds((B_LOC, 4, SEQ // 4, w), dt)
    s16 = lambda w, dt: _sds((B_LOC, 16, SEQ // 16, w), dt)
    row = _full((1, D_MODEL))
    return pl.pallas_call(
        body, name="middle", grid=(T // tm,),
        in_specs=[tok(W_A), tok(W_B), tok(128), p4(W_B), p4(128), p16(W_B), p16(128), tok(W_C), tok(D_MIX),
                  tok(D_MODEL), tok(D_MODEL), row, row, row, _full((D_MIX, D_MODEL)),
                  _full((128, W_B)), _full((W_B, 128)), _full((W_A, 128))],
        out_specs=(tok(D_MODEL), tok(D_MIX), tok(W_A), tok(128),
                   tok(W_B), tok(128), tok(128), p4(W_B), p4(128), p4(128), p16(W_B), p16(128), p16(128),
                   tok(W_C), tok(128), _full((8, D_MODEL)), _full((D_MIX, D_MODEL))),
        out_shape=(_sds((T, D_MODEL), BF16), _sds((T, D_MIX), BF16),
                   _sds((T, W_A), BF16), _sds((T, 128), F32),
                   _sds((T, W_B), BF16), _sds((T, 128), F32), _sds((T, 128), F32),
                   s4(W_B, BF16), s4(128, F32), s4(128, F32), s16(W_B, BF16), s16(128, F32), s16(128, F32),
                   _sds((T, W_C), BF16), _sds((T, 128), F32), _sds((8, D_MODEL), F32),
                   _sds((D_MIX, D_MODEL), F32)),
        scratch_shapes=[pltpu.VMEM((9, tm, 128), F32)],
        compiler_params=_cp(("arbitrary",), vmem_mb=56),
    )(*_pin(oa, o1, l1, o4, l4, o16, l16, oc, z, x, tgt, g_br, ln_g, ln_b, wout, spread4, gather4, gather8))


class _ReduceScatter:
    def __init__(self, shapes):
        self.shapes = shapes

    def scratch_shapes(self):
        out = []
        for n, w in self.shapes:
            h, p = n // 2, n // 4
            out += [pltpu.VMEM((4, h, w), F32), pltpu.VMEM((4, h, w), F32), pltpu.VMEM((6, p, w), BF16),
                    pltpu.VMEM((6, p, w), BF16), pltpu.VMEM((2, p, w), F32), pltpu.VMEM((h, w), F32)]
        na = len(self.shapes)
        dma = pltpu.SemaphoreType.DMA
        return out + [dma((na, 4)), dma((na, 4)), dma((na, 4)), dma((na, 6)), dma((na, 6)), dma((na,)), dma((na,)),
                      dma((na,))]

    def bind(self, g_refs, r_refs, scratch):
        na = len(self.shapes)
        bufs = [scratch[6 * a:6 * a + 6] for a in range(na)]
        mine, sib, stage, land, keep, tot = (tuple(b[i] for b in bufs) for i in range(6))
        loc_sem, s1_send, s1_recv, s2_send, s2_recv, s3_send, s3_recv, st_sem = scratch[6 * na:6 * na + 8]
        x, y, c = lax.axis_index("x"), lax.axis_index("y"), lax.axis_index("c")
        me, sibling = (x, y, c), (x, y, 1 - c)
        xn, yn, dg = (1 - x, y), (x, 1 - y), (1 - x, 1 - y)
        idx = lambda chip: 2 * chip[0] + chip[1]
        my_chip = idx((x, y))
        order = [idx(xn), idx(dg), idx(yn), my_chip]

        def rows(a, k, half):
            n = self.shapes[a][0]
            return pl.ds(pl.multiple_of(k * n + half * (n // 2), 8), n // 2)

        def piece(a, q):
            p = self.shapes[a][0] // 4
            return slice(q * p, (q + 1) * p)

        def load(a, k):
            return pltpu.make_async_copy(g_refs[a].at[rows(a, k, c), :], mine[a].at[k], loc_sem.at[a, k])

        def s1(a, k, half):
            return pltpu.make_async_remote_copy(
                src_ref=g_refs[a].at[rows(a, k, half), :], dst_ref=sib[a].at[k],
                send_sem=s1_send.at[a, k], recv_sem=s1_recv.at[a, k], device_id=sibling, device_id_type=MESH)

        def s2(a, i, to):
            return pltpu.make_async_remote_copy(
                src_ref=stage[a].at[i], dst_ref=land[a].at[i], send_sem=s2_send.at[a, i], recv_sem=s2_recv.at[a, i],
                device_id=to, device_id_type=MESH)

        via = {0: xn, 1: xn, 2: yn, 3: yn, 4: yn, 5: xn}

        def s3(a, half, to):
            return pltpu.make_async_remote_copy(
                src_ref=tot[a], dst_ref=r_refs[a].at[rows(a, 0, half), :], send_sem=s3_send.at[a],
                recv_sem=s3_recv.at[a], device_id=to, device_id_type=MESH)

        def store(a):
            return pltpu.make_async_copy(tot[a], r_refs[a].at[rows(a, 0, c), :], st_sem.at[a])

        def start():
            for k in order:
                for a in range(na):
                    load(a, k).start()
                    s1(a, k, 1 - c).start()

        def chip_sum(a, k):
            load(a, k).wait()
            s1(a, k, c).wait_recv()
            return mine[a][k] + sib[a][k]

        def exchange():
            for a in range(na):
                P, Q = piece(a, 0), piece(a, 1)
                s_xn = chip_sum(a, idx(xn))
                stage[a][0] = s_xn[P].astype(BF16)
                keep[a][1] = s_xn[Q]
                s_dg = chip_sum(a, idx(dg))
                stage[a][1] = s_dg[P].astype(BF16)
                s2(a, 0, (*xn, c)).start()
                s2(a, 1, (*xn, c)).start()
                stage[a][3] = s_dg[Q].astype(BF16)
                s_yn = chip_sum(a, idx(yn))
                stage[a][2] = s_yn[Q].astype(BF16)
                keep[a][0] = s_yn[P]
                s2(a, 2, (*yn, c)).start()
                s2(a, 3, (*yn, c)).start()
                tot[a][...] = chip_sum(a, my_chip)

        def relay():
            for a in range(na):
                P, Q = piece(a, 0), piece(a, 1)
                s2(a, 1, me).wait_recv()
                stage[a][4] = (keep[a][0] + land[a][1].astype(F32)).astype(BF16)
                s2(a, 4, (*yn, c)).start()
                s2(a, 3, me).wait_recv()
                stage[a][5] = (keep[a][1] + land[a][3].astype(F32)).astype(BF16)
                s2(a, 5, (*xn, c)).start()
                s2(a, 0, me).wait_recv()
                tot[a][P, :] += land[a][0].astype(F32)
                s2(a, 2, me).wait_recv()
                tot[a][Q, :] += land[a][2].astype(F32)

        def finish():
            for a in range(na):
                P, Q = piece(a, 0), piece(a, 1)
                s2(a, 4, me).wait_recv()
                tot[a][P, :] += land[a][4].astype(F32)
                s2(a, 5, me).wait_recv()
                tot[a][Q, :] += land[a][5].astype(F32)
                s3(a, c, sibling).start()
                store(a).start()

        def drain():
            for a in range(na):
                s3(a, 1 - c, me).wait_recv()
                store(a).wait()
            for a in range(na):
                for k in order:
                    s1(a, k, 1 - c).wait_send()
                for i in range(6):
                    s2(a, i, (*via[i], c)).wait_send()
                s3(a, c, sibling).wait_send()

        return start, exchange, relay, finish, drain

    def part(self, grads, steps):
        def body(*refs):
            na = len(self.shapes)
            i = pl.program_id(0)
            for step, phase in zip(steps, self.bind(refs[:na], refs[na:2 * na], refs[2 * na:])):
                pl.when(i == step)(phase)

        hbm = pl.BlockSpec(memory_space=pl.ANY)
        return _Part(body, list(grads), [hbm] * len(grads), [hbm] * len(grads),
                     [_sds((n, w), F32) for n, w in self.shapes], self.scratch_shapes())


def _dh_dx(dqa, dka, dva, dqn, dkn, dvn, dq4, dk4, dv4, dq16, dk16, dv16, dqc, dz, du, xb, cos, sa, sb, winT):
    tm = 512
    spt = SEQ // tm

    def body(dqa_ref, dka_ref, dva_ref, dqn_ref, dkn_ref, dvn_ref, dq4_ref, dk4_ref, dv4_ref,
             dq16_ref, dk16_ref, dv16_ref, dqc_ref, dz_ref, du_ref, xb_ref, cos_ref, sa_ref, sb_ref, w_ref,
             gx_ref, db_ref, gin_ref, dh_ref, scr):
        i = pl.program_id(0)

        @pl.when(i == 0)
        def _():
            db_ref[...] = jnp.zeros_like(db_ref)
            gin_ref[...] = jnp.zeros_like(gin_ref)

        cos_t, sa_t, sb_t = cos_ref[...], sa_ref[...], sb_ref[...]

        def rope_t(t):
            return _rope(t, cos_t, sa_t, sb_t, -1)

        def put(r0, val):
            n = val.shape[1]
            dh_ref[:, r0:r0 + n] = val.astype(BF16)
            db_ref[:, r0:r0 + n] += jnp.sum(val, axis=0, keepdims=True)

        put(O_QA, rope_t(dqa_ref[...].astype(F32)) * QK_SCALE)
        put(O_KA, rope_t(dka_ref[...].astype(F32)))
        put(O_VA, dva_ref[...].astype(F32))
        put(O_QC, dqc_ref[...].astype(F32) * QK_SCALE)
        put(O_Z, dz_ref[...].astype(F32))
        for k, (n_ref, r4, r16) in enumerate(((dqn_ref, dq4_ref, dq16_ref), (dkn_ref, dk4_ref, dk16_ref),
                                               (dvn_ref, dv4_ref, dv16_ref))):
            for j in range(2):
                sl = slice(128 * j, 128 * (j + 1))
                a, q = 2 * k + j, tm // 4
                scr[a] = n_ref[:, sl].astype(F32)
                for res in range(16):
                    scr[6 + a, pl.ds((res % 4) * q + res // 4, tm // 16, stride=4), :] = r16[0, res, :, sl].astype(F32)
                for res in range(4):
                    scr[a, pl.ds(res, q, stride=4), :] += (scr[6 + a, res * q:(res + 1) * q, :]
                                                           + r4[0, res, :, sl].astype(F32))
        cat = lambda a: jnp.concatenate([scr[a], scr[a + 1]], axis=1)
        put(O_QB, rope_t(cat(0)) * QK_SCALE)
        put(O_KB, rope_t(cat(2)))
        put(O_VB, cat(4))
        gx_ref[...] = _dot(dh_ref[...], w_ref[...], NN) + ALPHA * du_ref[...].astype(F32)
        gin_ref[...] += _dot(dh_ref[...], xb_ref[...], TN)

    tok = lambda w: pl.BlockSpec((tm, w), lambda i: (i, 0))
    tab = pl.BlockSpec((tm, 128), lambda i: (i % spt, 0))
    p4 = pl.BlockSpec((1, 4, tm // 4, W_B), lambda i: (i // spt, 0, i % spt, 0))
    p16 = pl.BlockSpec((1, 16, tm // 16, W_B), lambda i: (i // spt, 0, i % spt, 0))
    once = lambda shape: pl.BlockSpec(shape, lambda i: (0, 0), pipeline_mode=pl.Buffered(1))
    return pl.pallas_call(
        body, name="dh_dx", grid=(T // tm,),
        in_specs=[tok(W_A), tok(W_KV_A), tok(W_KV_A), tok(W_B), tok(W_B), tok(W_B), p4, p4, p4, p16, p16, p16,
                  tok(W_C), tok(D_MIX), tok(D_MODEL), tok(D_MODEL), tab, tab, tab, once((D_IN, D_MODEL))],
        out_specs=(tok(D_MODEL), _full((1, D_IN)), once((D_IN, D_MODEL))),
        out_shape=(_sds((T, D_MODEL), F32), _sds((1, D_IN), F32), _sds((D_IN, D_MODEL), F32)),
        scratch_shapes=[pltpu.VMEM((tm, D_IN), BF16), pltpu.VMEM((12, tm, 128), F32)],
        compiler_params=_cp(("arbitrary",), vmem_mb=56),
    )(*_pin(dqa, dka, dva, dqn, dkn, dvn, dq4, dk4, dv4, dq16, dk16, dv16, dqc, dz, du, xb, cos, sa, sb, winT))


def _reduce_grads(g_in, acc, dbin, dsink):
    rs = _ReduceScatter([(SH_IN, D_MODEL)])

    def body(g_ref, acc_ref, dbin_ref, dsink_ref, r_ref, sv_ref, sv_mine, sv_all, sv_send, sv_recv, *rs_scratch):
        x, y, c = lax.axis_index("x"), lax.axis_index("y"), lax.axis_index("c")
        chips = [(1 - x, y), (x, 1 - y), (1 - x, 1 - y)]
        start, exchange, relay, finish, drain = rs.bind((g_ref,), (r_ref,), rs_scratch)
        start()

        sv_mine[...] = jnp.zeros_like(sv_mine)
        sv_mine[0:4, :] = acc_ref[0:4, :]
        for k, c0 in enumerate(range(0, D_IN, SV_W)):
            n = min(SV_W, D_IN - c0)
            sv_mine[SV_DB + k:SV_DB + k + 1, 0:n] = dbin_ref[:, c0:c0 + n]
        sv_mine[SV_SINK:SV_SINK + 1, 0:128] = dsink_ref[...]
        my_dev = 4 * x + 2 * y + c
        others = [(x, y, 1 - c)] + [(*chip, cc) for chip in chips for cc in (c, 1 - c)]

        def sv_copy(j, to):
            return pltpu.make_async_remote_copy(
                src_ref=sv_mine, dst_ref=sv_all.at[my_dev], send_sem=sv_send.at[j], recv_sem=sv_recv.at[j],
                device_id=to, device_id_type=MESH)

        sv_sends = [sv_copy(j, to) for j, to in enumerate(others)]
        for cp in sv_sends:
            cp.start()
        exchange()
        relay()
        finish()
        sv_all[my_dev] = sv_mine[...]
        for j in range(7):
            sv_copy(j, (x, y, c)).wait_recv()
        tot = sv_all[0]
        for d in range(1, 8):
            tot = tot + sv_all[d]
        sv_ref[...] = tot
        drain()
        for cp in sv_sends:
            cp.wait_send()

    vm = pl.BlockSpec(memory_space=pltpu.VMEM)
    hbm = pl.BlockSpec(memory_space=pl.ANY)
    return pl.pallas_call(
        body, name="reduce_grads",
        out_shape=(_sds((SH_IN, D_MODEL), F32), _vm_sds((8, SV_W), F32)),
        in_specs=[hbm, vm, vm, vm], out_specs=(hbm, vm),
        scratch_shapes=[pltpu.VMEM((8, SV_W), F32), pltpu.VMEM((8, 8, SV_W), F32),
                        pltpu.SemaphoreType.DMA((7,)), pltpu.SemaphoreType.DMA((7,))] + rs.scratch_shapes(),
        compiler_params=_cp(vmem_mb=40),
    )(pltpu.with_memory_space_constraint(g_in, pltpu.HBM), acc, dbin, dsink)


def _adamw_update(w, g, m, v):
    nm = ADAM_B1 * m + (1.0 - ADAM_B1) * g
    nv = ADAM_B2 * v + (1.0 - ADAM_B2) * (g * g)
    m_hat = nm / (1.0 - ADAM_B1 ** ADAM_STEP)
    v_hat = nv / (1.0 - ADAM_B2 ** ADAM_STEP)
    return -ADAM_LR * (m_hat / (jnp.sqrt(v_hat) + ADAM_EPS) + ADAM_WD * w), nm, nv


SMALL = ((SV_DB, D_IN, 1.0), (SV_SINK, 8, -1.0), (1, D_MIX, 1.0), (2, D_MODEL, 1.0), (3, D_MODEL, 1.0))


def _adamw_all(items, sv, ws, ms, vs, n_chunks=2):
    nb, ns = 4 * len(items), len(SMALL)
    n_out = nb + 1 + 4 * ns

    def body(*refs):
        ins, sv_ref, small_in = refs[:nb], refs[nb], refs[nb + 1:nb + 1 + 3 * ns]
        outs, scratch = refs[nb + 1 + 3 * ns:nb + 1 + 3 * ns + n_out], refs[nb + 1 + 3 * ns + n_out:]
        big_out, loss_ref, small_out = outs[:nb], outs[nb], outs[nb + 1:]
        in_buf, out_buf, load_sem, store_sem = scratch[:nb], scratch[nb:2 * nb], scratch[2 * nb], scratch[2 * nb + 1]
        small_buf, small_sem = scratch[2 * nb + 2:-1], scratch[-1]
        small_loads = [pltpu.make_async_copy(src, buf, small_sem.at[k])
                       for k, (src, buf) in enumerate(zip((sv_ref, *small_in), small_buf))]
        for cp in small_loads:
            cp.start()

        def rows(p, c):
            n = items[p][0].shape[0] // n_chunks
            return pl.ds(c * n, n)

        def load(a, c):
            r = rows(a // 4, c)
            return pltpu.make_async_copy(ins[a].at[r, :], in_buf[a].at[r, :], load_sem.at[a, c])

        def store(a, c):
            r = rows(a // 4, c)
            src = in_buf[a + 1] if a % 4 == 0 else out_buf[a]
            return pltpu.make_async_copy(src.at[r, :], big_out[a].at[r, :], store_sem.at[a, c])

        order = [(p, c) for c in range(n_chunks) for p in range(len(items))]
        for p, c in order:
            for k in range(4):
                load(4 * p + k, c).start()

        for p, c in order:
            for k in range(4):
                load(4 * p + k, c).wait()
            r = rows(p, c)
            w_buf, g_buf, m_buf, v_buf = in_buf[4 * p:4 * p + 4]
            out_buf[4 * p + 1][r, :], out_buf[4 * p + 2][r, :], out_buf[4 * p + 3][r, :] = _adamw_update(
                w_buf[r, :], g_buf[r, :], m_buf[r, :], v_buf[r, :])
            for k in range(4):
                store(4 * p + k, c).start(priority=1)

        for cp in small_loads:
            cp.wait()
        sv_v, small_v = small_buf[0], small_buf[1:]
        loss_ref[...] = jnp.sum(sv_v[0:1, 0:D_MODEL], axis=1, keepdims=True)
        for p, (row, width, sign) in enumerate(SMALL):
            gv = sign * jnp.concatenate([sv_v[row + k:row + k + 1, 0:min(SV_W, width - c0)]
                                         for k, c0 in enumerate(range(0, width, SV_W))], axis=1)
            small_out[4 * p][...] = gv
            small_out[4 * p + 1][...], small_out[4 * p + 2][...], small_out[4 * p + 3][...] = _adamw_update(
                small_v[p][...], gv, small_v[ns + p][...], small_v[2 * ns + p][...])

        for p, c in order:
            for k in range(4):
                store(4 * p + k, c).wait()

    shapes, args, bufs = [], [], []
    for w, g, m, v in items:
        assert w.shape[0] % (8 * n_chunks) == 0
        shapes += [_sds(w.shape, F32)] * 4
        bufs += [pltpu.VMEM(w.shape, F32)] * 4
        args += [w, g, m, v]
    small_args = [*ws, *ms, *vs]
    whole = lambda a: _full(a.shape)
    hbm = pl.BlockSpec(memory_space=pl.ANY)
    res = pl.pallas_call(
        body, name="adamw", grid=(1,),
        in_specs=[hbm] * (nb + 1 + 3 * ns),
        out_specs=tuple([hbm] * nb + [_full((1, 1))] + [whole(w) for w in ws for _ in range(4)]),
        out_shape=tuple(shapes + [_sds((1, 1), F32)] + [_sds(w.shape, F32) for w in ws for _ in range(4)]),
        scratch_shapes=(bufs + bufs + [pltpu.SemaphoreType.DMA((nb, n_chunks))] * 2
                        + [pltpu.VMEM(a.shape, F32) for a in (sv, *small_args)]
                        + [pltpu.SemaphoreType.DMA((1 + 3 * ns,))]),
        compiler_params=_cp(("arbitrary",), vmem_mb=52),
    )(*_pin(*args, sv, *small_args))
    big = [tuple(res[4 * p:4 * p + 4]) for p in range(len(items))]
    return big, res[nb], [tuple(res[nb + 1 + 4 * p:nb + 5 + 4 * p]) for p in range(ns)]


def _rope_tables():
    pos = np.arange(SEQ, dtype=np.float32)
    inv = (np.float32(ROPE_THETA) ** (-np.arange(0, 64, 2, dtype=np.float32) / np.float32(64))).astype(np.float32)
    ang = np.tile(pos[:, None] * inv[None, :], (1, 4))
    cos, sin = np.cos(ang).astype(np.float32), np.sin(ang).astype(np.float32)
    low = (np.arange(128) % 64) < 32
    zero = np.float32(0.0)
    return jnp.asarray(cos), jnp.asarray(np.where(low, -sin, zero)), jnp.asarray(np.where(low, zero, sin))


def _local_step(x2, mem2, tgt2, winT, wout, wmem, b_in, sinks, g_branch, ln_gain, ln_bias):
    cos, sa, sb = _rope_tables()
    sinkv = jnp.pad(sinks, ((0, 0), (0, 120)))
    head_of_lane = np.arange(512)[:, None] // 64
    gather8 = jnp.asarray(head_of_lane == np.arange(128)[None, :], BF16)
    gather4 = jnp.asarray(head_of_lane[:W_B] == np.arange(128)[None, :], BF16)
    spread4 = jnp.asarray((head_of_lane[:W_B] == np.arange(128)[None, :]).T, BF16)

    xb, qa, ka, va, bn, b4, b16, qc, z, wout, wmem = _in_proj(x2, winT, b_in, cos, sa, sb, wout, wmem)
    memb, mkv = _mem_kv(mem2, wmem)
    b4f, b16f = b4.reshape(T, 768), b16.reshape(T, 768)

    swa = dict(kind="band", nb=SEQ // BLK, max_dist=BLK - 1, gqa=True)
    dil = (dict(kind="band", nb=SEQ // BLK), dict(kind="band", nb=SEQ // 4 // BLK), dict(kind="band", nb=1))
    (oa, lse_a), (o1, l1), (o4, l4), (o16, l16), (oc, lse_c) = _run_parts("attn_fwd", [
        _attn_fwd(qa, 0, W_A, ka, 0, va, 0, W_KV_A, sinks=sinks, **swa),
        _attn_fwd(bn, 0, W_B, bn, 1, bn, 2, W_B, **dil[0]),
        _attn_fwd(b4f, 0, W_B, b4f, 1, b4f, 2, W_B, **dil[1]),
        _attn_fwd(b16f, 0, W_B, b16f, 1, b16f, 2, W_B, **dil[2]),
        _attn_fwd(qc, 0, W_C, mkv, 0, mkv, 1, W_C, kind="mem")], "parallel", 48)

    s4 = lambda w: (B_LOC, 4, SEQ // 4, w)
    s16 = lambda w: (B_LOC, 16, SEQ // 16, w)
    (du, dz, doa, dla, dobn, lsen, dlbn, dob4, lse4, dlb4, dob16, lse16, dlb16, doc, dlc, acc, g_out) = _middle(
        oa, o1, l1, o4.reshape(s4(W_B)), l4.reshape(s4(128)), o16.reshape(s16(W_B)), l16.reshape(s16(128)), oc, z,
        x2, tgt2, g_branch, ln_gain, ln_bias, wout, spread4, gather4, gather8)

    flat = lambda a: a.reshape(T, a.shape[-1])
    (dqa, dka, dva, dsink), (dqc, g_mem) = _run_parts("attn_bwd_a", [
        _attn_bwd(qa, 0, W_A, ka, 0, va, 0, W_KV_A, doa, lse_a, dla, sinkv=sinkv, **swa),
        _attn_bwd(qc, 0, W_C, mkv, 0, mkv, 1, W_C, doc, lse_c, dlc, kind="mem", mem_in=memb)], "arbitrary", 48)
    last = T // QR - 1
    (r_out, r_mem), (dqn, dkn, dvn), (dq4, dk4, dv4), (dq16, dk16, dv16) = _run_parts("attn_bwd_b", [
        _ReduceScatter([(SH_OUT, D_MODEL), (SH_MEM, 2 * W_C)]).part((g_out, g_mem), (0, 1, 2, last, last)),
        _attn_bwd(bn, 0, W_B, bn, 1, bn, 2, W_B, dobn, lsen, dlbn, **dil[0]),
        _attn_bwd(b4f, 0, W_B, b4f, 1, b4f, 2, W_B, flat(dob4), flat(lse4), flat(dlb4), **dil[1]),
        _attn_bwd(b16f, 0, W_B, b16f, 1, b16f, 2, W_B, flat(dob16), flat(lse16), flat(dlb16), **dil[2])],
        "arbitrary", 62)

    r4 = lambda a: a.reshape(s4(W_B))
    r16 = lambda a: a.reshape(s16(W_B))
    gx, dbin, g_in = _dh_dx(dqa, dka, dva, dqn, dkn, dvn, r4(dq4), r4(dk4), r4(dv4), r16(dq16), r16(dk16),
                            r16(dv16), dqc, dz, du, xb, cos, sa, sb, winT)
    return gx, g_in, r_out, r_mem, acc, dbin, dsink


def kernel(x, mem, w_in, b_in, w_mem, attn_sinks, g_branch, w_out, ln_gain, ln_bias, loss_target, m_w_in, m_b_in, m_w_mem, m_attn_sinks, m_g_branch, m_w_out, m_ln_gain, m_ln_bias, v_w_in, v_b_in, v_w_mem, v_attn_sinks, v_g_branch, v_w_out, v_ln_gain, v_ln_bias):
    winT, wout, wmem = _gather_weights(w_in[0].T, w_out[0], w_mem[0])
    gx, g_in, r_out, r_mem, acc, dbin, dsink = _local_step(
        x.reshape(T, D_MODEL), mem.reshape(B_LOC * MEM_LEN, D_MODEL), loss_target.reshape(T, D_MODEL),
        winT, wout, wmem, b_in, attn_sinks, g_branch, ln_gain, ln_bias)
    r_in, sv = _reduce_grads(g_in, acc, dbin, dsink)

    small = ["b_in", "attn_sinks", "g_branch", "ln_gain", "ln_bias"]
    big, loss, steps = _adamw_all(
        [(w_in[0].T, r_in, m_w_in[0].T, v_w_in[0].T), (w_out[0], r_out, m_w_out[0], v_w_out[0]),
         (w_mem[0], r_mem, m_w_mem[0], v_w_mem[0])],
        sv, [b_in, attn_sinks, g_branch, ln_gain, ln_bias], [m_b_in, m_attn_sinks, m_g_branch, m_ln_gain, m_ln_bias],
        [v_b_in, v_attn_sinks, v_g_branch, v_ln_gain, v_ln_bias])
    out = dict(zip(small, steps))
    out["w_in"] = tuple(a.T[None] for a in big[0])
    out["w_out"], out["w_mem"] = (tuple(a[None] for a in st) for st in big[1:])
    names = ["w_in", "b_in", "w_mem", "attn_sinks", "g_branch", "w_out", "ln_gain", "ln_bias"]
    return (loss.reshape(()), gx.reshape(B_LOC, SEQ, D_MODEL), *[out[n][k] for k in range(4) for n in names])
```
